```python
import jax, jax.numpy as jnp
from jax import lax
import numpy as np

D_MODEL = 1024
BATCH = 8
SEQ = 4096
DEPTH = 2

N_META = 16
D_FF = 2816
EPS = 1e-6
POOL_WINDOWS = (2, 4, 8, 16)
POOL_GROUP = D_MODEL // 16
D_POOL = POOL_GROUP * len(POOL_WINDOWS)
HG_HEAD_K = 128
HG_HEAD_V = 128
D_HGRN = D_MODEL - D_POOL
HG_HEADS = D_HGRN // HG_HEAD_K
HG_CHUNK = 64
D_IN_EVEN = D_POOL + 4 * D_HGRN
D_CONV = D_MODEL // 2
CONV_WIDTH = 31
CONV_GROUPS = 4
D_LRU = D_MODEL // 2
LRU_HEADS = 8
LRU_HEAD = D_LRU // LRU_HEADS
LRU_CONV = 4
LRU_C = 8.0
D_IN_ODD = 2 * D_CONV + 2 * D_LRU
N_EVEN = (DEPTH + 1) // 2
N_ODD = DEPTH // 2

kernel_name = "hybrid_pool_hgrn2_conv_rglru_macaron"


def rms_norm(x, g):
    xf = x.astype(jnp.float32)
    y = xf * lax.rsqrt(jnp.mean(xf * xf, axis=-1, keepdims=True) + EPS)
    return (y * g.astype(jnp.float32)).astype(x.dtype)


def swiglu_ffn(x, wg, wu, wd):
    return (jax.nn.silu(x @ wg) * (x @ wu)) @ wd


def causal_depthwise_conv(x, w, b):
    width = w.shape[0]
    xp = jnp.pad(x, ((0, 0), (width - 1, 0), (0, 0)))
    y = lax.conv_general_dilated(xp, w.astype(x.dtype)[:, None, :], window_strides=(1,), padding='VALID',
                                 dimension_numbers=('NWC', 'WIO', 'NWC'), feature_group_count=x.shape[-1])
    return y + b.astype(x.dtype)


def multiscale_pool(u, w_grp, scale):
    bn, L, _ = u.shape
    ug = u.astype(jnp.float32).reshape(bn, L, len(POOL_WINDOWS), POOL_GROUP)
    c = jnp.cumsum(ug, axis=1)
    t = jnp.arange(1, L + 1, dtype=jnp.float32)[None, :, None]
    pooled = []
    for gi, w in enumerate(POOL_WINDOWS):
        cg = c[:, :, gi]
        lagged = jnp.pad(cg, ((0, 0), (w, 0), (0, 0)))[:, :L]
        pooled.append((cg - lagged) / jnp.minimum(t, float(w)))
    mixed = jnp.stack(pooled, axis=2) - ug
    y = jnp.einsum('blgc,gcd->blgd', mixed, w_grp.astype(jnp.float32))
    return (y.reshape(bn, L, D_POOL) * scale.astype(jnp.float32)).astype(u.dtype)


def hgrn2_mixer(q_raw, f_raw, i_raw, g_raw, lb, gnorm):
    f32 = jnp.float32
    bn, L, _ = q_raw.shape
    q = jax.nn.silu(q_raw.astype(f32))
    z = f_raw.astype(f32)
    lbf = lb.astype(f32)
    log_f = jnp.logaddexp(jnp.log(lbf), jnp.log1p(-lbf) + jax.nn.log_sigmoid(z))
    k = (1.0 - lbf) * jax.nn.sigmoid(-z)
    v = i_raw.astype(f32)
    pad = (-N_META) % HG_CHUNK
    n_chunks = (L + pad) // HG_CHUNK

    def to_chunks(a):
        a = jnp.pad(a, ((0, 0), (pad, 0), (0, 0)))
        return a.reshape(bn, n_chunks, HG_CHUNK, HG_HEADS, -1).transpose(1, 0, 3, 2, 4)

    causal = jnp.tril(jnp.ones((HG_CHUNK, HG_CHUNK), dtype=bool))[:, :, None]

    def step(S, inp):
        qc, kc, lfc, vc = inp
        b = jnp.cumsum(lfc, axis=2)
        o_inter = jnp.einsum('bhtk,bhkv->bhtv', qc * jnp.exp(b), S)
        diff = b[:, :, :, None, :] - b[:, :, None, :, :]
        decay = jnp.exp(jnp.where(causal, diff, -jnp.inf))
        A = jnp.einsum('bhtsk,bhsk->bhts', qc[:, :, :, None, :] * decay, kc)
        o = o_inter + jnp.einsum('bhts,bhsv->bhtv', A, vc)
        b_last = b[:, :, -1:, :]
        S_new = jnp.exp(b_last[:, :, 0, :])[..., None] * S + jnp.einsum('bhsk,bhsv->bhkv', kc * jnp.exp(b_last - b), vc)
        return S_new, o

    S0 = jnp.zeros((bn, HG_HEADS, HG_HEAD_K, HG_HEAD_V), f32)
    _, o = lax.scan(step, S0, (to_chunks(q), to_chunks(k), to_chunks(log_f), to_chunks(v)))
    o = o.transpose(1, 0, 3, 2, 4).reshape(bn, n_chunks * HG_CHUNK, HG_HEADS, HG_HEAD_V)[:, pad:]
    o = o * lax.rsqrt(jnp.mean(o * o, axis=-1, keepdims=True) + EPS) * gnorm.astype(f32)
    o = o * jax.nn.silu(g_raw.astype(f32)).reshape(bn, L, HG_HEADS, HG_HEAD_V)
    return o.reshape(bn, L, D_HGRN).astype(q_raw.dtype)


def conformer_conv_module(a, b, w, bias, ln_g, ln_b):
    f32 = jnp.float32
    bn, L, _ = a.shape
    u = a * jax.nn.sigmoid(b)
    u = causal_depthwise_conv(u, w, bias).astype(f32).reshape(bn, L, CONV_GROUPS, D_CONV // CONV_GROUPS)
    mu = jnp.mean(u, axis=-1, keepdims=True)
    var = jnp.mean(jnp.square(u - mu), axis=-1, keepdims=True)
    un = ((u - mu) * lax.rsqrt(var + EPS)).reshape(bn, L, D_CONV) * ln_g.astype(f32) + ln_b.astype(f32)
    return jax.nn.silu(un).astype(a.dtype)


def rglru_block(xb, gate, conv_w, conv_b, wa, ba, wx, bx, lam):
    f32 = jnp.float32
    u = causal_depthwise_conv(xb, conv_w, conv_b).astype(f32)
    bn, L, _ = u.shape
    uh = u.reshape(bn, L, LRU_HEADS, LRU_HEAD)
    r = jax.nn.sigmoid(jnp.einsum('blhi,hij->blhj', uh, wa.astype(f32)).reshape(bn, L, D_LRU) + ba.astype(f32))
    i = jax.nn.sigmoid(jnp.einsum('blhi,hij->blhj', uh, wx.astype(f32)).reshape(bn, L, D_LRU) + bx.astype(f32))
    log_a = -LRU_C * r * jax.nn.softplus(-lam.astype(f32))
    a = jnp.exp(log_a)
    mult = jnp.sqrt(-jnp.expm1(2.0 * log_a))
    reset = (jnp.arange(L) == 0)[None, :, None]
    bterm = jnp.where(reset, 1.0, mult) * (i * u)

    def combine(c1, c2):
        a1, b1 = c1
        a2, b2 = c2
        return a1 * a2, a2 * b1 + b2

    _, h = lax.associative_scan(combine, (a, bterm), axis=1)
    return (jax.nn.gelu(gate.astype(f32)) * h).astype(xb.dtype)


def _fwd_setup_inputs(seed: int = 0) -> dict:
    key = jax.random.key(seed)
    ks = iter(jax.random.split(key, 48))
    f32 = jnp.float32

    def nrm(shape, scale):
        return jax.random.normal(next(ks), shape, f32) * scale

    def gain(shape):
        return 1.0 + 0.05 * jax.random.normal(next(ks), shape, f32)

    a0 = jax.random.uniform(next(ks), (N_ODD, D_LRU), f32, 0.9, 0.999)
    s = a0 ** (1.0 / LRU_C)
    lam = jnp.log(s) - jnp.log1p(-s)
    return {
        "x": nrm((BATCH, SEQ, D_MODEL), 1.0),
        "meta_tokens": nrm((N_META, D_MODEL), 1.0),
        "ffn1_norm": gain((DEPTH, D_MODEL)),
        "ffn1_wg": nrm((DEPTH, D_MODEL, D_FF), D_MODEL ** -0.5),
        "ffn1_wu": nrm((DEPTH, D_MODEL, D_FF), D_MODEL ** -0.5),
        "ffn1_wd": nrm((DEPTH, D_FF, D_MODEL), D_FF ** -0.5),
        "mix_norm": gain((DEPTH, D_MODEL)),
        "ffn2_norm": gain((DEPTH, D_MODEL)),
        "ffn2_wg": nrm((DEPTH, D_MODEL, D_FF), D_MODEL ** -0.5),
        "ffn2_wu": nrm((DEPTH, D_MODEL, D_FF), D_MODEL ** -0.5),
        "ffn2_wd": nrm((DEPTH, D_FF, D_MODEL), D_FF ** -0.5),
        "w_in_even": nrm((N_EVEN, D_MODEL, D_IN_EVEN), D_MODEL ** -0.5),
        "pool_w": nrm((N_EVEN, len(POOL_WINDOWS), POOL_GROUP, POOL_GROUP), POOL_GROUP ** -0.5),
        "pool_scale": gain((N_EVEN, D_POOL)),
        "hgrn_lb_logits": nrm((N_EVEN + 1, D_HGRN), 0.5),
        "hgrn_gnorm": gain((N_EVEN, HG_HEAD_V)),
        "w_out_even": nrm((N_EVEN, D_POOL + D_HGRN, D_MODEL), (D_POOL + D_HGRN) ** -0.5),
        "w_in_odd": nrm((N_ODD, D_MODEL, D_IN_ODD), D_MODEL ** -0.5),
        "conv_w": nrm((N_ODD, CONV_WIDTH, D_CONV), CONV_WIDTH ** -0.5),
        "conv_b": nrm((N_ODD, D_CONV), 0.02),
        "conv_ln_g": gain((N_ODD, D_CONV)),
        "conv_ln_b": nrm((N_ODD, D_CONV), 0.02),
        "lru_conv_w": nrm((N_ODD, LRU_CONV, D_LRU), LRU_CONV ** -0.5),
        "lru_conv_b": nrm((N_ODD, D_LRU), 0.02),
        "lru_wa": nrm((N_ODD, LRU_HEADS, LRU_HEAD, LRU_HEAD), LRU_HEAD ** -0.5),
        "lru_ba": nrm((N_ODD, D_LRU), 0.02),
        "lru_wx": nrm((N_ODD, LRU_HEADS, LRU_HEAD, LRU_HEAD), LRU_HEAD ** -0.5),
        "lru_bx": nrm((N_ODD, D_LRU), 0.02),
        "lru_lambda": lam,
        "w_out_odd": nrm((N_ODD, D_CONV + D_LRU, D_MODEL), (D_CONV + D_LRU) ** -0.5),
        "final_norm": gain((D_MODEL,)),
    }


def _fwd_reference(x, meta_tokens, ffn1_norm, ffn1_wg, ffn1_wu, ffn1_wd, mix_norm, ffn2_norm, ffn2_wg, ffn2_wu,
              ffn2_wd, w_in_even, pool_w, pool_scale, hgrn_lb_logits, hgrn_gnorm, w_out_even, w_in_odd,
              conv_w, conv_b, conv_ln_g, conv_ln_b, lru_conv_w, lru_conv_b, lru_wa, lru_ba, lru_wx, lru_bx,
              lru_lambda, w_out_odd, final_norm):
    bn = x.shape[0]
    meta = jnp.broadcast_to(meta_tokens.astype(x.dtype)[None], (bn, N_META, D_MODEL))
    h = jnp.concatenate([meta, x], axis=1)
    lbs = jnp.cumsum(jax.nn.softmax(hgrn_lb_logits.astype(jnp.float32), axis=0), axis=0)
    for l in range(DEPTH):
        j = l // 2
        h = h + 0.5 * swiglu_ffn(rms_norm(h, ffn1_norm[l]), ffn1_wg[l], ffn1_wu[l], ffn1_wd[l])
        u = rms_norm(h, mix_norm[l])
        if l % 2 == 0:
            p = u @ w_in_even[j]
            p_pool, q_r, f_r, i_r, g_r = jnp.split(
                p, [D_POOL, D_POOL + D_HGRN, D_POOL + 2 * D_HGRN, D_POOL + 3 * D_HGRN], axis=-1)
            ya = multiscale_pool(p_pool, pool_w[j], pool_scale[j])
            yb = hgrn2_mixer(q_r, f_r, i_r, g_r, lbs[j], hgrn_gnorm[j])
            y = jnp.concatenate([ya, yb], axis=-1) @ w_out_even[j]
        else:
            p = u @ w_in_odd[j]
            c_a, c_b, d_x, d_g = jnp.split(p, [D_CONV, 2 * D_CONV, 2 * D_CONV + D_LRU], axis=-1)
            yc = conformer_conv_module(c_a, c_b, conv_w[j], conv_b[j], conv_ln_g[j], conv_ln_b[j])
            yd = rglru_block(d_x, d_g, lru_conv_w[j], lru_conv_b[j], lru_wa[j], lru_ba[j], lru_wx[j],
                             lru_bx[j], lru_lambda[j])
            y = jnp.concatenate([yc, yd], axis=-1) @ w_out_odd[j]
        h = h + y
        h = h + 0.5 * swiglu_ffn(rms_norm(h, ffn2_norm[l]), ffn2_wg[l], ffn2_wu[l], ffn2_wd[l])
    h = rms_norm(h, final_norm)
    return h[:, N_META:]


import jax as _jax
import jax.numpy as _jnp

TWIN_FORMAT = 'train_step'
FWD_PARAMS = ['x', 'meta_tokens', 'ffn1_norm', 'ffn1_wg', 'ffn1_wu', 'ffn1_wd', 'mix_norm', 'ffn2_norm', 'ffn2_wg', 'ffn2_wu', 'ffn2_wd', 'w_in_even', 'pool_w', 'pool_scale', 'hgrn_lb_logits', 'hgrn_gnorm', 'w_out_even', 'w_in_odd', 'conv_w', 'conv_b', 'conv_ln_g', 'conv_ln_b', 'lru_conv_w', 'lru_conv_b', 'lru_wa', 'lru_ba', 'lru_wx', 'lru_bx', 'lru_lambda', 'w_out_odd', 'final_norm']
TWIN_WEIGHTS = ['meta_tokens', 'ffn1_norm', 'ffn1_wg', 'ffn1_wu', 'ffn1_wd', 'mix_norm', 'ffn2_norm', 'ffn2_wg', 'ffn2_wu', 'ffn2_wd', 'w_in_even', 'pool_w', 'pool_scale', 'hgrn_lb_logits', 'hgrn_gnorm', 'w_out_even', 'w_in_odd', 'conv_w', 'conv_b', 'conv_ln_g', 'conv_ln_b', 'lru_conv_w', 'lru_conv_b', 'lru_wa', 'lru_ba', 'lru_wx', 'lru_bx', 'lru_lambda', 'w_out_odd', 'final_norm']
TWIN_DIFF_INPUT = 'x'
TWIN_INPUTS = ['x', 'meta_tokens', 'ffn1_norm', 'ffn1_wg', 'ffn1_wu', 'ffn1_wd', 'mix_norm', 'ffn2_norm', 'ffn2_wg', 'ffn2_wu', 'ffn2_wd', 'w_in_even', 'pool_w', 'pool_scale', 'hgrn_lb_logits', 'hgrn_gnorm', 'w_out_even', 'w_in_odd', 'conv_w', 'conv_b', 'conv_ln_g', 'conv_ln_b', 'lru_conv_w', 'lru_conv_b', 'lru_wa', 'lru_ba', 'lru_wx', 'lru_bx', 'lru_lambda', 'w_out_odd', 'final_norm', 'loss_target', 'm_meta_tokens', 'm_ffn1_norm', 'm_ffn1_wg', 'm_ffn1_wu', 'm_ffn1_wd', 'm_mix_norm', 'm_ffn2_norm', 'm_ffn2_wg', 'm_ffn2_wu', 'm_ffn2_wd', 'm_w_in_even', 'm_pool_w', 'm_pool_scale', 'm_hgrn_lb_logits', 'm_hgrn_gnorm', 'm_w_out_even', 'm_w_in_odd', 'm_conv_w', 'm_conv_b', 'm_conv_ln_g', 'm_conv_ln_b', 'm_lru_conv_w', 'm_lru_conv_b', 'm_lru_wa', 'm_lru_ba', 'm_lru_wx', 'm_lru_bx', 'm_lru_lambda', 'm_w_out_odd', 'm_final_norm', 'v_meta_tokens', 'v_ffn1_norm', 'v_ffn1_wg', 'v_ffn1_wu', 'v_ffn1_wd', 'v_mix_norm', 'v_ffn2_norm', 'v_ffn2_wg', 'v_ffn2_wu', 'v_ffn2_wd', 'v_w_in_even', 'v_pool_w', 'v_pool_scale', 'v_hgrn_lb_logits', 'v_hgrn_gnorm', 'v_w_out_even', 'v_w_in_odd', 'v_conv_w', 'v_conv_b', 'v_conv_ln_g', 'v_conv_ln_b', 'v_lru_conv_w', 'v_lru_conv_b', 'v_lru_wa', 'v_lru_ba', 'v_lru_wx', 'v_lru_bx', 'v_lru_lambda', 'v_w_out_odd', 'v_final_norm']
TWIN_OUTPUTS = ['loss', 'grad_x', 'grad_meta_tokens', 'grad_ffn1_norm', 'grad_ffn1_wg', 'grad_ffn1_wu', 'grad_ffn1_wd', 'grad_mix_norm', 'grad_ffn2_norm', 'grad_ffn2_wg', 'grad_ffn2_wu', 'grad_ffn2_wd', 'grad_w_in_even', 'grad_pool_w', 'grad_pool_scale', 'grad_hgrn_lb_logits', 'grad_hgrn_gnorm', 'grad_w_out_even', 'grad_w_in_odd', 'grad_conv_w', 'grad_conv_b', 'grad_conv_ln_g', 'grad_conv_ln_b', 'grad_lru_conv_w', 'grad_lru_conv_b', 'grad_lru_wa', 'grad_lru_ba', 'grad_lru_wx', 'grad_lru_bx', 'grad_lru_lambda', 'grad_w_out_odd', 'grad_final_norm', 'delta_meta_tokens', 'delta_ffn1_norm', 'delta_ffn1_wg', 'delta_ffn1_wu', 'delta_ffn1_wd', 'delta_mix_norm', 'delta_ffn2_norm', 'delta_ffn2_wg', 'delta_ffn2_wu', 'delta_ffn2_wd', 'delta_w_in_even', 'delta_pool_w', 'delta_pool_scale', 'delta_hgrn_lb_logits', 'delta_hgrn_gnorm', 'delta_w_out_even', 'delta_w_in_odd', 'delta_conv_w', 'delta_conv_b', 'delta_conv_ln_g', 'delta_conv_ln_b', 'delta_lru_conv_w', 'delta_lru_conv_b', 'delta_lru_wa', 'delta_lru_ba', 'delta_lru_wx', 'delta_lru_bx', 'delta_lru_lambda', 'delta_w_out_odd', 'delta_final_norm', 'new_m_meta_tokens', 'new_m_ffn1_norm', 'new_m_ffn1_wg', 'new_m_ffn1_wu', 'new_m_ffn1_wd', 'new_m_mix_norm', 'new_m_ffn2_norm', 'new_m_ffn2_wg', 'new_m_ffn2_wu', 'new_m_ffn2_wd', 'new_m_w_in_even', 'new_m_pool_w', 'new_m_pool_scale', 'new_m_hgrn_lb_logits', 'new_m_hgrn_gnorm', 'new_m_w_out_even', 'new_m_w_in_odd', 'new_m_conv_w', 'new_m_conv_b', 'new_m_conv_ln_g', 'new_m_conv_ln_b', 'new_m_lru_conv_w', 'new_m_lru_conv_b', 'new_m_lru_wa', 'new_m_lru_ba', 'new_m_lru_wx', 'new_m_lru_bx', 'new_m_lru_lambda', 'new_m_w_out_odd', 'new_m_final_norm', 'new_v_meta_tokens', 'new_v_ffn1_norm', 'new_v_ffn1_wg', 'new_v_ffn1_wu', 'new_v_ffn1_wd', 'new_v_mix_norm', 'new_v_ffn2_norm', 'new_v_ffn2_wg', 'new_v_ffn2_wu', 'new_v_ffn2_wd', 'new_v_w_in_even', 'new_v_pool_w', 'new_v_pool_scale', 'new_v_hgrn_lb_logits', 'new_v_hgrn_gnorm', 'new_v_w_out_even', 'new_v_w_in_odd', 'new_v_conv_w', 'new_v_conv_b', 'new_v_conv_ln_g', 'new_v_conv_ln_b', 'new_v_lru_conv_w', 'new_v_lru_conv_b', 'new_v_lru_wa', 'new_v_lru_ba', 'new_v_lru_wx', 'new_v_lru_bx', 'new_v_lru_lambda', 'new_v_w_out_odd', 'new_v_final_norm']
TWIN_LEAF_KINDS = {'loss': 'loss', 'grad_x': 'grad_x', 'grad_meta_tokens': 'grad_w', 'grad_ffn1_norm': 'grad_w', 'grad_ffn1_wg': 'grad_w', 'grad_ffn1_wu': 'grad_w', 'grad_ffn1_wd': 'grad_w', 'grad_mix_norm': 'grad_w', 'grad_ffn2_norm': 'grad_w', 'grad_ffn2_wg': 'grad_w', 'grad_ffn2_wu': 'grad_w', 'grad_ffn2_wd': 'grad_w', 'grad_w_in_even': 'grad_w', 'grad_pool_w': 'grad_w', 'grad_pool_scale': 'grad_w', 'grad_hgrn_lb_logits': 'grad_w', 'grad_hgrn_gnorm': 'grad_w', 'grad_w_out_even': 'grad_w', 'grad_w_in_odd': 'grad_w', 'grad_conv_w': 'grad_w', 'grad_conv_b': 'grad_w', 'grad_conv_ln_g': 'grad_w', 'grad_conv_ln_b': 'grad_w', 'grad_lru_conv_w': 'grad_w', 'grad_lru_conv_b': 'grad_w', 'grad_lru_wa': 'grad_w', 'grad_lru_ba': 'grad_w', 'grad_lru_wx': 'grad_w', 'grad_lru_bx': 'grad_w', 'grad_lru_lambda': 'grad_w', 'grad_w_out_odd': 'grad_w', 'grad_final_norm': 'grad_w', 'delta_meta_tokens': 'delta_w', 'delta_ffn1_norm': 'delta_w', 'delta_ffn1_wg': 'delta_w', 'delta_ffn1_wu': 'delta_w', 'delta_ffn1_wd': 'delta_w', 'delta_mix_norm': 'delta_w', 'delta_ffn2_norm': 'delta_w', 'delta_ffn2_wg': 'delta_w', 'delta_ffn2_wu': 'delta_w', 'delta_ffn2_wd': 'delta_w', 'delta_w_in_even': 'delta_w', 'delta_pool_w': 'delta_w', 'delta_pool_scale': 'delta_w', 'delta_hgrn_lb_logits': 'delta_w', 'delta_hgrn_gnorm': 'delta_w', 'delta_w_out_even': 'delta_w', 'delta_w_in_odd': 'delta_w', 'delta_conv_w': 'delta_w', 'delta_conv_b': 'delta_w', 'delta_conv_ln_g': 'delta_w', 'delta_conv_ln_b': 'delta_w', 'delta_lru_conv_w': 'delta_w', 'delta_lru_conv_b': 'delta_w', 'delta_lru_wa': 'delta_w', 'delta_lru_ba': 'delta_w', 'delta_lru_wx': 'delta_w', 'delta_lru_bx': 'delta_w', 'delta_lru_lambda': 'delta_w', 'delta_w_out_odd': 'delta_w', 'delta_final_norm': 'delta_w', 'new_m_meta_tokens': 'new_m', 'new_m_ffn1_norm': 'new_m', 'new_m_ffn1_wg': 'new_m', 'new_m_ffn1_wu': 'new_m', 'new_m_ffn1_wd': 'new_m', 'new_m_mix_norm': 'new_m', 'new_m_ffn2_norm': 'new_m', 'new_m_ffn2_wg': 'new_m', 'new_m_ffn2_wu': 'new_m', 'new_m_ffn2_wd': 'new_m', 'new_m_w_in_even': 'new_m', 'new_m_pool_w': 'new_m', 'new_m_pool_scale': 'new_m', 'new_m_hgrn_lb_logits': 'new_m', 'new_m_hgrn_gnorm': 'new_m', 'new_m_w_out_even': 'new_m', 'new_m_w_in_odd': 'new_m', 'new_m_conv_w': 'new_m', 'new_m_conv_b': 'new_m', 'new_m_conv_ln_g': 'new_m', 'new_m_conv_ln_b': 'new_m', 'new_m_lru_conv_w': 'new_m', 'new_m_lru_conv_b': 'new_m', 'new_m_lru_wa': 'new_m', 'new_m_lru_ba': 'new_m', 'new_m_lru_wx': 'new_m', 'new_m_lru_bx': 'new_m', 'new_m_lru_lambda': 'new_m', 'new_m_w_out_odd': 'new_m', 'new_m_final_norm': 'new_m', 'new_v_meta_tokens': 'new_v', 'new_v_ffn1_norm': 'new_v', 'new_v_ffn1_wg': 'new_v', 'new_v_ffn1_wu': 'new_v', 'new_v_ffn1_wd': 'new_v', 'new_v_mix_norm': 'new_v', 'new_v_ffn2_norm': 'new_v', 'new_v_ffn2_wg': 'new_v', 'new_v_ffn2_wu': 'new_v', 'new_v_ffn2_wd': 'new_v', 'new_v_w_in_even': 'new_v', 'new_v_pool_w': 'new_v', 'new_v_pool_scale': 'new_v', 'new_v_hgrn_lb_logits': 'new_v', 'new_v_hgrn_gnorm': 'new_v', 'new_v_w_out_even': 'new_v', 'new_v_w_in_odd': 'new_v', 'new_v_conv_w': 'new_v', 'new_v_conv_b': 'new_v', 'new_v_conv_ln_g': 'new_v', 'new_v_conv_ln_b': 'new_v', 'new_v_lru_conv_w': 'new_v', 'new_v_lru_conv_b': 'new_v', 'new_v_lru_wa': 'new_v', 'new_v_lru_ba': 'new_v', 'new_v_lru_wx': 'new_v', 'new_v_lru_bx': 'new_v', 'new_v_lru_lambda': 'new_v', 'new_v_w_out_odd': 'new_v', 'new_v_final_norm': 'new_v'}


def _forward(args):
    return _fwd_reference(*[args[k] for k in FWD_PARAMS])


def _output_shape():
    out = _jax.eval_shape(lambda: _forward(_fwd_setup_inputs(0)))
    return out.shape, out.dtype

N_MICROBATCH = 1
ADAM_LR = 0.001
ADAM_B1 = 0.9
ADAM_B2 = 0.999
ADAM_EPS = 1e-08
ADAM_WD = 0.01
ADAM_STEP = 10
PER_EXAMPLE_BATCH_AXIS = {'x': 0, 'loss_target': 0}
SHARED_INPUTS = []
_WEIGHT_DTYPES = {'meta_tokens': _jnp.float32, 'ffn1_norm': _jnp.float32, 'ffn1_wg': _jnp.float32, 'ffn1_wu': _jnp.float32, 'ffn1_wd': _jnp.float32, 'mix_norm': _jnp.float32, 'ffn2_norm': _jnp.float32, 'ffn2_wg': _jnp.float32, 'ffn2_wu': _jnp.float32, 'ffn2_wd': _jnp.float32, 'w_in_even': _jnp.float32, 'pool_w': _jnp.float32, 'pool_scale': _jnp.float32, 'hgrn_lb_logits': _jnp.float32, 'hgrn_gnorm': _jnp.float32, 'w_out_even': _jnp.float32, 'w_in_odd': _jnp.float32, 'conv_w': _jnp.float32, 'conv_b': _jnp.float32, 'conv_ln_g': _jnp.float32, 'conv_ln_b': _jnp.float32, 'lru_conv_w': _jnp.float32, 'lru_conv_b': _jnp.float32, 'lru_wa': _jnp.float32, 'lru_ba': _jnp.float32, 'lru_wx': _jnp.float32, 'lru_bx': _jnp.float32, 'lru_lambda': _jnp.float32, 'w_out_odd': _jnp.float32, 'final_norm': _jnp.float32}
MOMENT_SCALE = {'meta_tokens': 5.236740e-03, 'ffn1_norm': 7.715786e-02, 'ffn1_wg': 3.314748e-02, 'ffn1_wu': 3.211318e-02, 'ffn1_wd': 5.329837e-02, 'mix_norm': 1.142310e-01, 'ffn2_norm': 6.090410e-02, 'ffn2_wg': 2.567843e-02, 'ffn2_wu': 2.491596e-02, 'ffn2_wd': 4.135360e-02, 'w_in_even': 7.702891e-02, 'pool_w': 1.420009e-01, 'pool_scale': 1.514798e-01, 'hgrn_lb_logits': 8.787967e-03, 'hgrn_gnorm': 2.574172e-01, 'w_out_even': 1.094627e-01, 'w_in_odd': 5.857810e-02, 'conv_w': 8.131707e-02, 'conv_b': 1.839107e-01, 'conv_ln_g': 1.000127e-01, 'conv_ln_b': 9.850500e-02, 'lru_conv_w': 6.099614e-02, 'lru_conv_b': 6.627910e-01, 'lru_wa': 1.965653e-02, 'lru_ba': 1.540587e-02, 'lru_wx': 3.500110e-02, 'lru_bx': 2.274995e-02, 'lru_lambda': 3.194337e-02, 'w_out_odd': 7.227282e-02, 'final_norm': 3.197356e+01}


def _to_microbatches(a, axis):
    t = _jnp.moveaxis(a, axis, 0)
    t = t.reshape((N_MICROBATCH, t.shape[0] // N_MICROBATCH) + t.shape[1:])
    return _jnp.moveaxis(t, 1, axis + 1)


def setup_inputs(seed: int = 0) -> dict:
    inp = _fwd_setup_inputs(seed)
    key = _jax.random.fold_in(_jax.random.key(seed), 7919)
    shape, _ = _output_shape()
    out = dict(inp)
    out["loss_target"] = _jax.random.normal(_jax.random.fold_in(key, 0), shape, _jnp.float32)
    for i, name in enumerate(TWIN_WEIGHTS):
        w = inp[name].astype(_jnp.float32)
        if MOMENT_SCALE is None:
            s = _jnp.sqrt(_jnp.mean(_jnp.square(w)) + 1e-30)
        else:
            s = MOMENT_SCALE[name]
        km, kv = _jax.random.split(_jax.random.fold_in(key, i + 1))
        out[name] = w
        out["m_" + name] = s * _jax.random.normal(km, w.shape, _jnp.float32)
        out["v_" + name] = (s * s) * _jax.random.uniform(kv, w.shape, _jnp.float32, 0.5, 1.5)
    if N_MICROBATCH > 1:
        for name, axis in PER_EXAMPLE_BATCH_AXIS.items():
            out[name] = _to_microbatches(out[name], axis)
    return {'x': out['x'], 'meta_tokens': out['meta_tokens'], 'ffn1_norm': out['ffn1_norm'], 'ffn1_wg': out['ffn1_wg'], 'ffn1_wu': out['ffn1_wu'], 'ffn1_wd': out['ffn1_wd'], 'mix_norm': out['mix_norm'], 'ffn2_norm': out['ffn2_norm'], 'ffn2_wg': out['ffn2_wg'], 'ffn2_wu': out['ffn2_wu'], 'ffn2_wd': out['ffn2_wd'], 'w_in_even': out['w_in_even'], 'pool_w': out['pool_w'], 'pool_scale': out['pool_scale'], 'hgrn_lb_logits': out['hgrn_lb_logits'], 'hgrn_gnorm': out['hgrn_gnorm'], 'w_out_even': out['w_out_even'], 'w_in_odd': out['w_in_odd'], 'conv_w': out['conv_w'], 'conv_b': out['conv_b'], 'conv_ln_g': out['conv_ln_g'], 'conv_ln_b': out['conv_ln_b'], 'lru_conv_w': out['lru_conv_w'], 'lru_conv_b': out['lru_conv_b'], 'lru_wa': out['lru_wa'], 'lru_ba': out['lru_ba'], 'lru_wx': out['lru_wx'], 'lru_bx': out['lru_bx'], 'lru_lambda': out['lru_lambda'], 'w_out_odd': out['w_out_odd'], 'final_norm': out['final_norm'], 'loss_target': out['loss_target'], 'm_meta_tokens': out['m_meta_tokens'], 'm_ffn1_norm': out['m_ffn1_norm'], 'm_ffn1_wg': out['m_ffn1_wg'], 'm_ffn1_wu': out['m_ffn1_wu'], 'm_ffn1_wd': out['m_ffn1_wd'], 'm_mix_norm': out['m_mix_norm'], 'm_ffn2_norm': out['m_ffn2_norm'], 'm_ffn2_wg': out['m_ffn2_wg'], 'm_ffn2_wu': out['m_ffn2_wu'], 'm_ffn2_wd': out['m_ffn2_wd'], 'm_w_in_even': out['m_w_in_even'], 'm_pool_w': out['m_pool_w'], 'm_pool_scale': out['m_pool_scale'], 'm_hgrn_lb_logits': out['m_hgrn_lb_logits'], 'm_hgrn_gnorm': out['m_hgrn_gnorm'], 'm_w_out_even': out['m_w_out_even'], 'm_w_in_odd': out['m_w_in_odd'], 'm_conv_w': out['m_conv_w'], 'm_conv_b': out['m_conv_b'], 'm_conv_ln_g': out['m_conv_ln_g'], 'm_conv_ln_b': out['m_conv_ln_b'], 'm_lru_conv_w': out['m_lru_conv_w'], 'm_lru_conv_b': out['m_lru_conv_b'], 'm_lru_wa': out['m_lru_wa'], 'm_lru_ba': out['m_lru_ba'], 'm_lru_wx': out['m_lru_wx'], 'm_lru_bx': out['m_lru_bx'], 'm_lru_lambda': out['m_lru_lambda'], 'm_w_out_odd': out['m_w_out_odd'], 'm_final_norm': out['m_final_norm'], 'v_meta_tokens': out['v_meta_tokens'], 'v_ffn1_norm': out['v_ffn1_norm'], 'v_ffn1_wg': out['v_ffn1_wg'], 'v_ffn1_wu': out['v_ffn1_wu'], 'v_ffn1_wd': out['v_ffn1_wd'], 'v_mix_norm': out['v_mix_norm'], 'v_ffn2_norm': out['v_ffn2_norm'], 'v_ffn2_wg': out['v_ffn2_wg'], 'v_ffn2_wu': out['v_ffn2_wu'], 'v_ffn2_wd': out['v_ffn2_wd'], 'v_w_in_even': out['v_w_in_even'], 'v_pool_w': out['v_pool_w'], 'v_pool_scale': out['v_pool_scale'], 'v_hgrn_lb_logits': out['v_hgrn_lb_logits'], 'v_hgrn_gnorm': out['v_hgrn_gnorm'], 'v_w_out_even': out['v_w_out_even'], 'v_w_in_odd': out['v_w_in_odd'], 'v_conv_w': out['v_conv_w'], 'v_conv_b': out['v_conv_b'], 'v_conv_ln_g': out['v_conv_ln_g'], 'v_conv_ln_b': out['v_conv_ln_b'], 'v_lru_conv_w': out['v_lru_conv_w'], 'v_lru_conv_b': out['v_lru_conv_b'], 'v_lru_wa': out['v_lru_wa'], 'v_lru_ba': out['v_lru_ba'], 'v_lru_wx': out['v_lru_wx'], 'v_lru_bx': out['v_lru_bx'], 'v_lru_lambda': out['v_lru_lambda'], 'v_w_out_odd': out['v_w_out_odd'], 'v_final_norm': out['v_final_norm']}


def _loss(weights, diff, rest, loss_target):
    with _jax.named_scope("forward"):
        args = {**rest, TWIN_DIFF_INPUT: diff, **{k: w.astype(_WEIGHT_DTYPES[k]) for k, w in weights.items()}}
        y = _forward(args)
    with _jax.named_scope("loss_head"):
        err = _jnp.square(y.astype(_jnp.float32) - loss_target)
        return 0.5 * _jnp.sum(_jnp.mean(err, axis=-1)) if err.ndim else 0.5 * err


def _adamw(w, g, m, v):
    m = ADAM_B1 * m + (1.0 - ADAM_B1) * g
    v = ADAM_B2 * v + (1.0 - ADAM_B2) * _jnp.square(g)
    m_hat = m / (1.0 - ADAM_B1 ** ADAM_STEP)
    v_hat = v / (1.0 - ADAM_B2 ** ADAM_STEP)
    delta = -ADAM_LR * (m_hat / (_jnp.sqrt(v_hat) + ADAM_EPS) + ADAM_WD * w)
    return delta, m, v


def reference(x, meta_tokens, ffn1_norm, ffn1_wg, ffn1_wu, ffn1_wd, mix_norm, ffn2_norm, ffn2_wg, ffn2_wu, ffn2_wd, w_in_even, pool_w, pool_scale, hgrn_lb_logits, hgrn_gnorm, w_out_even, w_in_odd, conv_w, conv_b, conv_ln_g, conv_ln_b, lru_conv_w, lru_conv_b, lru_wa, lru_ba, lru_wx, lru_bx, lru_lambda, w_out_odd, final_norm, loss_target, m_meta_tokens, m_ffn1_norm, m_ffn1_wg, m_ffn1_wu, m_ffn1_wd, m_mix_norm, m_ffn2_norm, m_ffn2_wg, m_ffn2_wu, m_ffn2_wd, m_w_in_even, m_pool_w, m_pool_scale, m_hgrn_lb_logits, m_hgrn_gnorm, m_w_out_even, m_w_in_odd, m_conv_w, m_conv_b, m_conv_ln_g, m_conv_ln_b, m_lru_conv_w, m_lru_conv_b, m_lru_wa, m_lru_ba, m_lru_wx, m_lru_bx, m_lru_lambda, m_w_out_odd, m_final_norm, v_meta_tokens, v_ffn1_norm, v_ffn1_wg, v_ffn1_wu, v_ffn1_wd, v_mix_norm, v_ffn2_norm, v_ffn2_wg, v_ffn2_wu, v_ffn2_wd, v_w_in_even, v_pool_w, v_pool_scale, v_hgrn_lb_logits, v_hgrn_gnorm, v_w_out_even, v_w_in_odd, v_conv_w, v_conv_b, v_conv_ln_g, v_conv_ln_b, v_lru_conv_w, v_lru_conv_b, v_lru_wa, v_lru_ba, v_lru_wx, v_lru_bx, v_lru_lambda, v_w_out_odd, v_final_norm):
    given = dict(x=x, meta_tokens=meta_tokens, ffn1_norm=ffn1_norm, ffn1_wg=ffn1_wg, ffn1_wu=ffn1_wu, ffn1_wd=ffn1_wd, mix_norm=mix_norm, ffn2_norm=ffn2_norm, ffn2_wg=ffn2_wg, ffn2_wu=ffn2_wu, ffn2_wd=ffn2_wd, w_in_even=w_in_even, pool_w=pool_w, pool_scale=pool_scale, hgrn_lb_logits=hgrn_lb_logits, hgrn_gnorm=hgrn_gnorm, w_out_even=w_out_even, w_in_odd=w_in_odd, conv_w=conv_w, conv_b=conv_b, conv_ln_g=conv_ln_g, conv_ln_b=conv_ln_b, lru_conv_w=lru_conv_w, lru_conv_b=lru_conv_b, lru_wa=lru_wa, lru_ba=lru_ba, lru_wx=lru_wx, lru_bx=lru_bx, lru_lambda=lru_lambda, w_out_odd=w_out_odd, final_norm=final_norm, loss_target=loss_target, m_meta_tokens=m_meta_tokens, m_ffn1_norm=m_ffn1_norm, m_ffn1_wg=m_ffn1_wg, m_ffn1_wu=m_ffn1_wu, m_ffn1_wd=m_ffn1_wd, m_mix_norm=m_mix_norm, m_ffn2_norm=m_ffn2_norm, m_ffn2_wg=m_ffn2_wg, m_ffn2_wu=m_ffn2_wu, m_ffn2_wd=m_ffn2_wd, m_w_in_even=m_w_in_even, m_pool_w=m_pool_w, m_pool_scale=m_pool_scale, m_hgrn_lb_logits=m_hgrn_lb_logits, m_hgrn_gnorm=m_hgrn_gnorm, m_w_out_even=m_w_out_even, m_w_in_odd=m_w_in_odd, m_conv_w=m_conv_w, m_conv_b=m_conv_b, m_conv_ln_g=m_conv_ln_g, m_conv_ln_b=m_conv_ln_b, m_lru_conv_w=m_lru_conv_w, m_lru_conv_b=m_lru_conv_b, m_lru_wa=m_lru_wa, m_lru_ba=m_lru_ba, m_lru_wx=m_lru_wx, m_lru_bx=m_lru_bx, m_lru_lambda=m_lru_lambda, m_w_out_odd=m_w_out_odd, m_final_norm=m_final_norm, v_meta_tokens=v_meta_tokens, v_ffn1_norm=v_ffn1_norm, v_ffn1_wg=v_ffn1_wg, v_ffn1_wu=v_ffn1_wu, v_ffn1_wd=v_ffn1_wd, v_mix_norm=v_mix_norm, v_ffn2_norm=v_ffn2_norm, v_ffn2_wg=v_ffn2_wg, v_ffn2_wu=v_ffn2_wu, v_ffn2_wd=v_ffn2_wd, v_w_in_even=v_w_in_even, v_pool_w=v_pool_w, v_pool_scale=v_pool_scale, v_hgrn_lb_logits=v_hgrn_lb_logits, v_hgrn_gnorm=v_hgrn_gnorm, v_w_out_even=v_w_out_even, v_w_in_odd=v_w_in_odd, v_conv_w=v_conv_w, v_conv_b=v_conv_b, v_conv_ln_g=v_conv_ln_g, v_conv_ln_b=v_conv_ln_b, v_lru_conv_w=v_lru_conv_w, v_lru_conv_b=v_lru_conv_b, v_lru_wa=v_lru_wa, v_lru_ba=v_lru_ba, v_lru_wx=v_lru_wx, v_lru_bx=v_lru_bx, v_lru_lambda=v_lru_lambda, v_w_out_odd=v_w_out_odd, v_final_norm=v_final_norm)
    weights = {n: given[n] for n in TWIN_WEIGHTS}
    shared = {n: given[n] for n in SHARED_INPUTS}
    per_example = {n: given[n] for n in ['x']}
    grad_fn = _jax.value_and_grad(_loss, argnums=(0, 1))

    def one_microbatch(ex, loss_target):
        ex = dict(ex)
        diff = ex.pop(TWIN_DIFF_INPUT)
        return grad_fn(weights, diff, {**shared, **ex}, loss_target)

    if N_MICROBATCH == 1:
        loss, (grad_w, grad_x) = one_microbatch(per_example, given["loss_target"])
    else:
        def body(carry, xs):
            loss_sum, grad_sum = carry
            l_k, (gw_k, gx_k) = one_microbatch(xs[0], xs[1])
            with _jax.named_scope("update"):
                return (loss_sum + l_k, _jax.tree.map(_jnp.add, grad_sum, gw_k)), gx_k

        init = (_jnp.zeros((), _jnp.float32), _jax.tree.map(_jnp.zeros_like, weights))
        (loss, grad_w), grad_x = _jax.lax.scan(body, init, (per_example, given["loss_target"]))
    with _jax.named_scope("update"):
        delta_w, new_m, new_v = {}, {}, {}
        for n in TWIN_WEIGHTS:
            delta_w[n], new_m[n], new_v[n] = _adamw(weights[n], grad_w[n], given["m_" + n], given["v_" + n])
    return (loss, grad_x, *[grad_w[n] for n in TWIN_WEIGHTS], *[delta_w[n] for n in TWIN_WEIGHTS],
            *[new_m[n] for n in TWIN_WEIGHTS], *[new_v[n] for n in TWIN_WEIGHTS])
```

```python
import functools

import numpy as np
import jax
import jax.numpy as jnp
from jax import lax
from jax.experimental import pallas as pl
from jax.experimental.pallas import tpu as pltpu

F32 = jnp.float32
BF16 = jnp.bfloat16
MESH = pl.DeviceIdType.MESH

EPS = 1e-6
N_META = 16
D_MODEL = 1024
D_FF = 2816
N_SHARD = 4
FF_SHARD = D_FF // N_SHARD
D_POOL = 256
POOL_GROUP = 64
POOL_WINDOWS = (2, 4, 8, 16)
D_HGRN = 768
HG_HEADS = 6
HEAD = 128
HG_CHUNK = 64
D_IN_EVEN = D_POOL + 4 * D_HGRN
D_CONV = 512
CONV_WIDTH = 31
CONV_HALO = 32
D_LRU = 512
LRU_CONV = 4
LRU_HALO = 8
LRU_C = 8.0
D_IN_ODD = 2 * D_CONV + 2 * D_LRU
SUBLANE = 8
ROW_ALIGN = 64

ADAM_LR = 0.001
ADAM_B1 = 0.9
ADAM_B2 = 0.999
ADAM_EPS = 1e-08
ADAM_WD = 0.01
ADAM_STEP = 10

VMEM_LIMIT_MB = 56


def _cparams(n_grid_axes=0, vmem_mb=VMEM_LIMIT_MB):
    sem = ("arbitrary",) * n_grid_axes if n_grid_axes else None
    return pltpu.CompilerParams(dimension_semantics=sem, vmem_limit_bytes=vmem_mb * 1024 * 1024)


def _tile(n, target, mult):
    best = None
    for t in range(mult, min(n, target) + 1, mult):
        if n % t == 0:
            best = t
    assert best is not None, (n, target, mult)
    return best


def _dot(a, b):
    return jnp.dot(a, b, preferred_element_type=F32)


def _dot_nt(a, b):
    return lax.dot_general(a, b, (((1,), (1,)), ((), ())), preferred_element_type=F32)


def _dot_tn(a, b):
    return lax.dot_general(a, b, (((0,), (0,)), ((), ())), preferred_element_type=F32)


def _sigmoid(x):
    return 1.0 / (1.0 + jnp.exp(-x))


def _colsum(x):
    return jnp.sum(x, axis=0, keepdims=True)


def _rms_stats(h):
    rstd = lax.rsqrt(jnp.mean(h * h, axis=-1, keepdims=True) + EPS)
    return rstd, h * rstd


def _rms_bwd(dn, g, rstd, xhat):
    dng = dn * g
    dh = rstd * (dng - xhat * jnp.mean(dng * xhat, axis=-1, keepdims=True))
    return dh, _colsum(dn * xhat)


def _ffn_fwd(h, norm, wg4, wu4, wd4, layer, tm):
    tp, d = h.shape
    nt = tp // tm

    def body(h_ref, g_ref, wg_ref, wu_ref, wd_ref, ho_ref, a_ref, b_ref, n_ref, n_sc, acc):
        s = pl.program_id(1)

        @pl.when(s == 0)
        def _():
            hh = h_ref[...]
            rstd, xhat = _rms_stats(hh)
            n = (xhat * g_ref[...]).astype(BF16)
            n_sc[...] = n
            n_ref[...] = n
            acc[...] = jnp.zeros_like(acc)

        n = n_sc[...]
        a = _dot(n, wg_ref[0])
        b = _dot(n, wu_ref[0])
        a_ref[0] = a.astype(BF16)
        b_ref[0] = b.astype(BF16)
        sg = (a * _sigmoid(a) * b).astype(BF16)
        acc[...] += _dot(sg, wd_ref[0])

        @pl.when(s == N_SHARD - 1)
        def _():
            ho_ref[...] = h_ref[...] + 0.5 * acc[...]

    return pl.pallas_call(
        body, name="ffn_fwd",
        grid=(nt, N_SHARD),
        in_specs=[
            pl.BlockSpec((tm, d), lambda i, s: (i, 0)),
            pl.BlockSpec((1, d), lambda i, s: (0, 0)),
            pl.BlockSpec((1, d, FF_SHARD), lambda i, s: (s, layer, 0)),
            pl.BlockSpec((1, d, FF_SHARD), lambda i, s: (s, layer, 0)),
            pl.BlockSpec((1, FF_SHARD, d), lambda i, s: (s, layer, 0)),
        ],
        out_specs=[
            pl.BlockSpec((tm, d), lambda i, s: (i, 0)),
            pl.BlockSpec((1, tm, FF_SHARD), lambda i, s: (s, i, 0)),
            pl.BlockSpec((1, tm, FF_SHARD), lambda i, s: (s, i, 0)),
            pl.BlockSpec((tm, d), lambda i, s: (i, 0)),
        ],
        out_shape=[
            jax.ShapeDtypeStruct((tp, d), F32),
            jax.ShapeDtypeStruct((N_SHARD, tp, FF_SHARD), BF16),
            jax.ShapeDtypeStruct((N_SHARD, tp, FF_SHARD), BF16),
            jax.ShapeDtypeStruct((tp, d), BF16),
        ],
        scratch_shapes=[pltpu.VMEM((tm, d), BF16), pltpu.VMEM((tm, d), F32)],
        compiler_params=_cparams(2),
    )(h, norm, wg4, wu4, wd4)


def _ffn_bwd_act(dho, h, norm, a4, b4, wg4, wu4, wd4, layer, tm):
    tp, d = h.shape
    nt = tp // tm

    def body(dho_ref, h_ref, g_ref, a_ref, b_ref, wg_ref, wu_ref, wd_ref,
             dh_ref, da_ref, db_ref, dg_ref, dy_sc, dn_sc):
        i = pl.program_id(0)
        s = pl.program_id(1)

        @pl.when(s == 0)
        def _():
            dy_sc[...] = (0.5 * dho_ref[...]).astype(BF16)
            dn_sc[...] = jnp.zeros_like(dn_sc)

        @pl.when((s == 0) & (i == 0))
        def _():
            dg_ref[...] = jnp.zeros_like(dg_ref)

        ds = _dot_nt(dy_sc[...], wd_ref[0])
        a = a_ref[0].astype(F32)
        b = b_ref[0].astype(F32)
        sig = _sigmoid(a)
        da = (ds * b * (sig * (1.0 + a * (1.0 - sig)))).astype(BF16)
        db = (ds * (a * sig)).astype(BF16)
        da_ref[0] = da
        db_ref[0] = db
        dn_sc[...] += _dot_nt(da, wg_ref[0]) + _dot_nt(db, wu_ref[0])

        @pl.when(s == N_SHARD - 1)
        def _():
            rstd, xhat = _rms_stats(h_ref[...])
            dh, dg = _rms_bwd(dn_sc[...], g_ref[...], rstd, xhat)
            dh_ref[...] = dho_ref[...] + dh
            dg_ref[...] += dg

    return pl.pallas_call(
        body, name="ffn_bwd_act",
        grid=(nt, N_SHARD),
        in_specs=[
            pl.BlockSpec((tm, d), lambda i, s: (i, 0)),
            pl.BlockSpec((tm, d), lambda i, s: (i, 0)),
            pl.BlockSpec((1, d), lambda i, s: (0, 0)),
            pl.BlockSpec((1, tm, FF_SHARD), lambda i, s: (s, i, 0)),
            pl.BlockSpec((1, tm, FF_SHARD), lambda i, s: (s, i, 0)),
            pl.BlockSpec((1, d, FF_SHARD), lambda i, s: (s, layer, 0)),
            pl.BlockSpec((1, d, FF_SHARD), lambda i, s: (s, layer, 0)),
            pl.BlockSpec((1, FF_SHARD, d), lambda i, s: (s, layer, 0)),
        ],
        out_specs=[
            pl.BlockSpec((tm, d), lambda i, s: (i, 0)),
            pl.BlockSpec((1, tm, FF_SHARD), lambda i, s: (s, i, 0)),
            pl.BlockSpec((1, tm, FF_SHARD), lambda i, s: (s, i, 0)),
            pl.BlockSpec((1, d), lambda i, s: (0, 0)),
        ],
        out_shape=[
            jax.ShapeDtypeStruct((tp, d), F32),
            jax.ShapeDtypeStruct((N_SHARD, tp, FF_SHARD), BF16),
            jax.ShapeDtypeStruct((N_SHARD, tp, FF_SHARD), BF16),
            jax.ShapeDtypeStruct((1, d), F32),
        ],
        scratch_shapes=[pltpu.VMEM((tm, d), BF16), pltpu.VMEM((tm, d), F32)],
        compiler_params=_cparams(2),
    )(dho, h, norm, a4, b4, wg4, wu4, wd4)


def _ffn_bwd_w(dho, n, a4, b4, da4, db4, gwg, gwu, gwd, layer, tm):
    tp, d = n.shape
    nt = tp // tm
    first = gwg is None

    def body(*refs):
        if first:
            dho_ref, n_ref, a_ref, b_ref, da_ref, db_ref = refs[:6]
            og_ref, ou_ref, od_ref, accg, accu, accd = refs[6:]
        else:
            dho_ref, n_ref, a_ref, b_ref, da_ref, db_ref = refs[:6]
            og_ref, ou_ref, od_ref, accg, accu, accd = refs[9:]
        i = pl.program_id(1)

        @pl.when(i == 0)
        def _():
            accg[...] = jnp.zeros_like(accg)
            accu[...] = jnp.zeros_like(accu)
            accd[...] = jnp.zeros_like(accd)

        nn = n_ref[...]
        accg[...] += _dot_tn(nn, da_ref[0])
        accu[...] += _dot_tn(nn, db_ref[0])
        a = a_ref[0].astype(F32)
        b = b_ref[0].astype(F32)
        sact = (a * _sigmoid(a) * b).astype(BF16)
        dy = (0.5 * dho_ref[...]).astype(BF16)
        accd[...] += _dot_tn(sact, dy)

        @pl.when(i == nt - 1)
        def _():
            og_ref[0] = accg[...].astype(BF16)
            ou_ref[0] = accu[...].astype(BF16)
            od_ref[0] = accd[...].astype(BF16)

    in_specs = [
        pl.BlockSpec((tm, d), lambda s, i: (i, 0)),
        pl.BlockSpec((tm, d), lambda s, i: (i, 0)),
        pl.BlockSpec((1, tm, FF_SHARD), lambda s, i: (s, i, 0)),
        pl.BlockSpec((1, tm, FF_SHARD), lambda s, i: (s, i, 0)),
        pl.BlockSpec((1, tm, FF_SHARD), lambda s, i: (s, i, 0)),
        pl.BlockSpec((1, tm, FF_SHARD), lambda s, i: (s, i, 0)),
    ]
    args = [dho, n, a4, b4, da4, db4]
    aliases = {}
    if not first:
        in_specs += [pl.BlockSpec(memory_space=pl.ANY)] * 3
        args += [gwg, gwu, gwd]
        aliases = {6: 0, 7: 1, 8: 2}
    return pl.pallas_call(
        body, name="ffn_bwd_w",
        grid=(N_SHARD, nt),
        in_specs=in_specs,
        out_specs=[
            pl.BlockSpec((1, d, FF_SHARD), lambda s, i: (s, layer, 0)),
            pl.BlockSpec((1, d, FF_SHARD), lambda s, i: (s, layer, 0)),
            pl.BlockSpec((1, FF_SHARD, d), lambda s, i: (s, layer, 0)),
        ],
        out_shape=[
            jax.ShapeDtypeStruct((N_SHARD, 2 * d, FF_SHARD), BF16),
            jax.ShapeDtypeStruct((N_SHARD, 2 * d, FF_SHARD), BF16),
            jax.ShapeDtypeStruct((N_SHARD, 2 * FF_SHARD, d), BF16),
        ],
        scratch_shapes=[pltpu.VMEM((d, FF_SHARD), F32), pltpu.VMEM((d, FF_SHARD), F32),
                        pltpu.VMEM((FF_SHARD, d), F32)],
        input_output_aliases=aliases,
        compiler_params=_cparams(2),
    )(*args)


def _proj_fwd(h, norm, layer, w_pieces, tm):
    tp, d = h.shape
    widths = [bs[-1] for _, bs, _ in w_pieces]
    ntot = sum(widths)
    npc = len(w_pieces)

    def body(*refs):
        h_ref, g_ref = refs[:2]
        w_refs = refs[2:2 + npc]
        p_ref, n_ref = refs[2 + npc:]
        rstd, xhat = _rms_stats(h_ref[...])
        n = (xhat * g_ref[...]).astype(BF16)
        n_ref[...] = n
        off = 0
        for k in range(npc):
            w = w_refs[k][...]
            w = w.reshape(w.shape[-2], w.shape[-1])
            p_ref[:, off:off + widths[k]] = _dot(n, w)
            off += widths[k]

    in_specs = [pl.BlockSpec((tm, d), lambda i: (i, 0)), pl.BlockSpec((1, d), lambda i: (0, 0))]
    for _, bs, idx in w_pieces:
        in_specs.append(pl.BlockSpec(bs, functools.partial(lambda i, idx: idx, idx=idx)))
    return pl.pallas_call(
        body, name="proj_fwd",
        grid=(tp // tm,),
        in_specs=in_specs,
        out_specs=[pl.BlockSpec((tm, ntot), lambda i: (i, 0)), pl.BlockSpec((tm, d), lambda i: (i, 0))],
        out_shape=[jax.ShapeDtypeStruct((tp, ntot), F32), jax.ShapeDtypeStruct((tp, d), BF16)],
        compiler_params=_cparams(1),
    )(h, norm, *[w for w, _, _ in w_pieces])


def _proj_bwd_act(dres, h, norm, layer, dp_pieces, w_pieces, tm):
    tp, d = h.shape
    npc = len(w_pieces)

    def body(*refs):
        dres_ref, h_ref, g_ref = refs[:3]
        dp_refs = refs[3:3 + npc]
        w_refs = refs[3 + npc:3 + 2 * npc]
        dh_ref, dg_ref = refs[3 + 2 * npc:]
        i = pl.program_id(0)

        @pl.when(i == 0)
        def _():
            dg_ref[...] = jnp.zeros_like(dg_ref)

        dn = None
        for k in range(npc):
            w = w_refs[k][...]
            w = w.reshape(w.shape[-2], w.shape[-1])
            t = _dot_nt(dp_refs[k][...], w)
            dn = t if dn is None else dn + t
        rstd, xhat = _rms_stats(h_ref[...])
        dh, dg = _rms_bwd(dn, g_ref[...], rstd, xhat)
        dh_ref[...] = dres_ref[...] + dh
        dg_ref[...] += dg

    in_specs = [pl.BlockSpec((tm, d), lambda i: (i, 0)), pl.BlockSpec((tm, d), lambda i: (i, 0)),
                pl.BlockSpec((1, d), lambda i: (0, 0))]
    for dp in dp_pieces:
        in_specs.append(pl.BlockSpec((tm, dp.shape[1]), lambda i: (i, 0)))
    for _, bs, idx in w_pieces:
        in_specs.append(pl.BlockSpec(bs, functools.partial(lambda i, idx: idx, idx=idx)))
    return pl.pallas_call(
        body, name="proj_bwd_act",
        grid=(tp // tm,),
        in_specs=in_specs,
        out_specs=[pl.BlockSpec((tm, d), lambda i: (i, 0)), pl.BlockSpec((1, d), lambda i: (0, 0))],
        out_shape=[jax.ShapeDtypeStruct((tp, d), F32), jax.ShapeDtypeStruct((1, d), F32)],
        compiler_params=_cparams(1),
    )(dres, h, norm, *dp_pieces, *[w for w, _, _ in w_pieces])


def _proj_bwd_w(n, dp_pieces, tm):
    tp, d = n.shape
    npc = len(dp_pieces)
    widths = [dp.shape[1] for dp in dp_pieces]

    def body(*refs):
        n_ref = refs[0]
        dp_refs = refs[1:1 + npc]
        o_refs = refs[1 + npc:1 + 2 * npc]
        accs = refs[1 + 2 * npc:]
        i = pl.program_id(0)

        @pl.when(i == 0)
        def _():
            for acc in accs:
                acc[...] = jnp.zeros_like(acc)

        nn = n_ref[...]
        for k in range(npc):
            accs[k][...] += _dot_tn(nn, dp_refs[k][...])

        @pl.when(i == pl.num_programs(0) - 1)
        def _():
            for k in range(npc):
                o_refs[k][...] = accs[k][...].astype(BF16)

    return pl.pallas_call(
        body, name="proj_bwd_w",
        grid=(tp // tm,),
        in_specs=[pl.BlockSpec((tm, d), lambda i: (i, 0))]
        + [pl.BlockSpec((tm, w), lambda i: (i, 0)) for w in widths],
        out_specs=[pl.BlockSpec((d, w), lambda i: (0, 0)) for w in widths],
        out_shape=[jax.ShapeDtypeStruct((d, w), BF16) for w in widths],
        scratch_shapes=[pltpu.VMEM((d, w), F32) for w in widths],
        compiler_params=_cparams(1),
    )(n, *dp_pieces)


def _out_fwd(h, ya, yb, w, tm):
    tp, d = h.shape
    na, nb = ya.shape[1], yb.shape[1]

    def body(h_ref, ya_ref, yb_ref, w_ref, o_ref):
        y = _dot(ya_ref[...].astype(BF16), w_ref[0:na, :]) + _dot(yb_ref[...].astype(BF16), w_ref[na:, :])
        o_ref[...] = h_ref[...] + y

    return pl.pallas_call(
        body, name="out_fwd",
        grid=(tp // tm,),
        in_specs=[pl.BlockSpec((tm, d), lambda i: (i, 0)), pl.BlockSpec((tm, na), lambda i: (i, 0)),
                  pl.BlockSpec((tm, nb), lambda i: (i, 0)), pl.BlockSpec((d, d), lambda i: (0, 0))],
        out_specs=pl.BlockSpec((tm, d), lambda i: (i, 0)),
        out_shape=jax.ShapeDtypeStruct((tp, d), F32),
        compiler_params=_cparams(1),
    )(h, ya, yb, w)


def _out_bwd(dy, ya, yb, w, tm):
    tp, d = dy.shape
    na, nb = ya.shape[1], yb.shape[1]

    def body(dy_ref, ya_ref, yb_ref, w_ref, da_ref, db_ref, dw_ref, acc):
        i = pl.program_id(0)

        @pl.when(i == 0)
        def _():
            acc[...] = jnp.zeros_like(acc)

        dyb16 = dy_ref[...].astype(BF16)
        da_ref[...] = _dot_nt(dyb16, w_ref[0:na, :])
        db_ref[...] = _dot_nt(dyb16, w_ref[na:, :])
        acc[0:na, :] += _dot_tn(ya_ref[...].astype(BF16), dyb16)
        acc[na:, :] += _dot_tn(yb_ref[...].astype(BF16), dyb16)

        @pl.when(i == pl.num_programs(0) - 1)
        def _():
            dw_ref[...] = acc[...].astype(BF16)

    return pl.pallas_call(
        body, name="out_bwd",
        grid=(tp // tm,),
        in_specs=[pl.BlockSpec((tm, d), lambda i: (i, 0)), pl.BlockSpec((tm, na), lambda i: (i, 0)),
                  pl.BlockSpec((tm, nb), lambda i: (i, 0)), pl.BlockSpec((d, d), lambda i: (0, 0))],
        out_specs=[pl.BlockSpec((tm, na), lambda i: (i, 0)), pl.BlockSpec((tm, nb), lambda i: (i, 0)),
                   pl.BlockSpec((d, d), lambda i: (0, 0))],
        out_shape=[jax.ShapeDtypeStruct((tp, na), F32), jax.ShapeDtypeStruct((tp, nb), F32),
                   jax.ShapeDtypeStruct((d, d), BF16)],
        scratch_shapes=[pltpu.VMEM((d, d), F32)],
        compiler_params=_cparams(1),
    )(dy, ya, yb, w)


def _loss_bwd(h, gfin, tgt, t_real, tm):
    tp, d = h.shape

    def body(h_ref, g_ref, t_ref, loss_ref, dh_ref, dg_ref):
        i = pl.program_id(0)

        @pl.when(i == 0)
        def _():
            loss_ref[...] = jnp.zeros_like(loss_ref)
            dg_ref[...] = jnp.zeros_like(dg_ref)

        rows = i * tm + lax.broadcasted_iota(jnp.int32, (tm, 1), 0)
        valid = (rows >= N_META) & (rows < t_real)
        rstd, xhat = _rms_stats(h_ref[...])
        g = g_ref[...]
        err = jnp.where(valid, xhat * g - t_ref[...], 0.0)
        e2 = jnp.sum(err * err, axis=1, keepdims=True)
        loss_ref[...] += (0.5 / d) * jnp.sum(e2, axis=0, keepdims=True)
        dy = err * (1.0 / d)
        dh, dg = _rms_bwd(dy, g, rstd, xhat)
        dh_ref[...] = dh
        dg_ref[...] += dg

    return pl.pallas_call(
        body, name="loss_bwd",
        grid=(tp // tm,),
        in_specs=[pl.BlockSpec((tm, d), lambda i: (i, 0)), pl.BlockSpec((1, d), lambda i: (0, 0)),
                  pl.BlockSpec((tm, d), lambda i: (i, 0))],
        out_specs=[pl.BlockSpec((1, 1), lambda i: (0, 0)), pl.BlockSpec((tm, d), lambda i: (i, 0)),
                   pl.BlockSpec((1, d), lambda i: (0, 0))],
        out_shape=[jax.ShapeDtypeStruct((1, 1), F32), jax.ShapeDtypeStruct((tp, d), F32),
                   jax.ShapeDtypeStruct((1, d), F32)],
        compiler_params=_cparams(1),
    )(h, gfin, tgt)


POOL_HALO = 16


def _pool_lane_consts(n_rows):
    lane = lax.broadcasted_iota(jnp.int32, (n_rows, D_POOL), 1)
    grp = lane // POOL_GROUP
    win = jnp.where(grp == 0, 2.0, jnp.where(grp == 1, 4.0, jnp.where(grp == 2, 8.0, 16.0)))
    return grp, win


def _pool_select(grp, s2, s4, s8, s16):
    return jnp.where(grp == 0, s2, jnp.where(grp == 1, s4, jnp.where(grp == 2, s8, s16)))


def _pool_mixed(x, row0, tr):
    n = tr + POOL_HALO
    s2 = x + pltpu.roll(x, 1, 0)
    s4 = s2 + pltpu.roll(s2, 2, 0)
    s8 = s4 + pltpu.roll(s4, 4, 0)
    s16 = s8 + pltpu.roll(s8, 8, 0)
    grp, win = _pool_lane_consts(n)
    rows = row0 - POOL_HALO + lax.broadcasted_iota(jnp.int32, (n, D_POOL), 0)
    cnt = jnp.minimum((rows + 1).astype(F32), win)
    pooled = _pool_select(grp, s2, s4, s8, s16) / jnp.maximum(cnt, 1.0)
    return (pooled - x)[POOL_HALO:, :]


def _pool_fwd(p, wbd, scale, tr):
    tp = p.shape[0]
    nt = tp // tr

    def body(p_ref, w_ref, s_ref, y_ref, usc):
        usc[0:POOL_HALO, :] = jnp.zeros((POOL_HALO, D_POOL), F32)
        usc[POOL_HALO:, :] = p_ref[...]

        def tile(r, carry):
            r0 = pl.multiple_of(r * tr, SUBLANE)
            x = usc[pl.ds(r0, tr + POOL_HALO), :]
            mixed = _pool_mixed(x, r0, tr)
            y_ref[pl.ds(r0, tr), :] = _dot(mixed.astype(BF16), w_ref[...]) * s_ref[...]
            return carry

        lax.fori_loop(0, nt, tile, 0)

    return pl.pallas_call(
        body, name="pool_fwd",
        grid=(1,),
        in_specs=[pl.BlockSpec((tp, D_POOL), lambda i: (0, 0)), pl.BlockSpec((D_POOL, D_POOL), lambda i: (0, 0)),
                  pl.BlockSpec((1, D_POOL), lambda i: (0, 0))],
        out_specs=pl.BlockSpec((tp, D_POOL), lambda i: (0, 0)),
        out_shape=jax.ShapeDtypeStruct((tp, D_POOL), F32),
        scratch_shapes=[pltpu.VMEM((tp + POOL_HALO, D_POOL), F32)],
        compiler_params=_cparams(1),
    )(p, wbd, scale)


def _pool_bwd(p, dya, wbd, scale, tr):
    tp = p.shape[0]
    nt = tp // tr

    def body(p_ref, dy_ref, w_ref, s_ref, du_ref, dw_ref, ds_ref, usc, gsc):
        usc[0:POOL_HALO, :] = jnp.zeros((POOL_HALO, D_POOL), F32)
        usc[POOL_HALO:, :] = p_ref[...]
        gsc[tp:, :] = jnp.zeros((POOL_HALO, D_POOL), F32)
        dw_ref[...] = jnp.zeros_like(dw_ref)
        ds_ref[...] = jnp.zeros_like(ds_ref)
        grp, win = _pool_lane_consts(tr)

        def tile1(r, carry):
            r0 = pl.multiple_of(r * tr, SUBLANE)
            x = usc[pl.ds(r0, tr + POOL_HALO), :]
            mixed = _pool_mixed(x, r0, tr).astype(BF16)
            dy = dy_ref[pl.ds(r0, tr), :]
            dys = (dy * s_ref[...]).astype(BF16)
            ypre = _dot(mixed, w_ref[...])
            ds_ref[...] += _colsum(dy * ypre)
            dw_ref[...] += _dot_tn(mixed, dys)
            dmx = _dot_nt(dys, w_ref[...])
            rows = r0 + lax.broadcasted_iota(jnp.int32, (tr, D_POOL), 0)
            cnt = jnp.minimum((rows + 1).astype(F32), win)
            gsc[pl.ds(r0, tr), :] = dmx / cnt
            return carry

        lax.fori_loop(0, nt, tile1, 0)
        n = tr + POOL_HALO
        grp2, win2 = _pool_lane_consts(n)

        def tile2(r, carry):
            r0 = pl.multiple_of(r * tr, SUBLANE)
            g = gsc[pl.ds(r0, n), :]
            s2 = g + pltpu.roll(g, n - 1, 0)
            s4 = s2 + pltpu.roll(s2, n - 2, 0)
            s8 = s4 + pltpu.roll(s4, n - 4, 0)
            s16 = s8 + pltpu.roll(s8, n - 8, 0)
            pooled_t = _pool_select(grp2, s2, s4, s8, s16)
            rows = r0 + lax.broadcasted_iota(jnp.int32, (n, D_POOL), 0)
            cnt = jnp.minimum((rows + 1).astype(F32), win2)
            du = pooled_t - g * cnt
            du_ref[pl.ds(r0, tr), :] = du[0:tr, :].astype(BF16)
            return carry

        lax.fori_loop(0, nt, tile2, 0)

    return pl.pallas_call(
        body, name="pool_bwd",
        grid=(1,),
        in_specs=[pl.BlockSpec((tp, D_POOL), lambda i: (0, 0)), pl.BlockSpec((tp, D_POOL), lambda i: (0, 0)),
                  pl.BlockSpec((D_POOL, D_POOL), lambda i: (0, 0)), pl.BlockSpec((1, D_POOL), lambda i: (0, 0))],
        out_specs=[pl.BlockSpec((tp, D_POOL), lambda i: (0, 0)), pl.BlockSpec((D_POOL, D_POOL), lambda i: (0, 0)),
                   pl.BlockSpec((1, D_POOL), lambda i: (0, 0))],
        out_shape=[jax.ShapeDtypeStruct((tp, D_POOL), BF16), jax.ShapeDtypeStruct((D_POOL, D_POOL), F32),
                   jax.ShapeDtypeStruct((1, D_POOL), F32)],
        scratch_shapes=[pltpu.VMEM((tp + POOL_HALO, D_POOL), F32), pltpu.VMEM((tp + POOL_HALO, D_POOL), F32)],
        compiler_params=_cparams(1),
    )(p, dya, wbd, scale)


def _hgrn_consts(ch):
    t = np.arange(ch)
    levels = []
    w = ch // 2
    while w >= 1:
        levels.append(w)
        w //= 2
    tril = t[None, :] <= t[:, None]
    to_end = t[None, :] > t[:, None]
    mats = [tril, to_end]
    masks = []
    for w in levels:
        pos = t % (2 * w)
        blk = t // (2 * w)
        upper = pos >= w
        mid = blk * 2 * w + w - 1
        mats.append(upper[:, None] & (t[None, :] > mid[:, None]) & (t[None, :] <= t[:, None]))
        mats.append((~upper)[:, None] & (t[None, :] > t[:, None]) & (t[None, :] <= mid[:, None]))
        masks.append(upper[:, None] & (~upper)[None, :] & (blk[:, None] == blk[None, :]))
    masks.append(tril)
    mst = np.concatenate(mats, axis=0).astype(np.float32)
    msk = np.stack(masks).astype(np.float32)
    return jnp.asarray(mst, BF16), jnp.asarray(msk, F32), len(levels)


def _split3(x):
    hi = x.astype(BF16)
    r1 = x - hi.astype(F32)
    mid = r1.astype(BF16)
    lo = (r1 - mid.astype(F32)).astype(BF16)
    return hi, mid, lo


def _hgrn_exponents(mst, logf):
    hi, mid, lo = _split3(logf)
    x = _dot(mst, jnp.concatenate([hi, mid, lo], axis=1))
    return x[:, 0:HEAD] + x[:, HEAD:2 * HEAD] + x[:, 2 * HEAD:3 * HEAD]


def _hgrn_gates(q_raw, z, lb):
    sz = _sigmoid(z)
    f = lb + (1.0 - lb) * sz
    q = q_raw * _sigmoid(q_raw)
    k = (1.0 - lb) * (1.0 - sz)
    return q, k, f, sz


def _hgrn_intra(q, k, x, msk_ref, n_lev, ch):
    a = msk_ref[n_lev] * 0.0
    eye = (lax.broadcasted_iota(jnp.int32, (ch, ch), 0) == lax.broadcasted_iota(jnp.int32, (ch, ch), 1))
    a = jnp.where(eye, jnp.sum(q * k, axis=1, keepdims=True), 0.0)
    ops = []
    for lv in range(n_lev):
        eq = jnp.exp(x[(2 + 2 * lv) * ch:(3 + 2 * lv) * ch, :])
        ek = jnp.exp(x[(3 + 2 * lv) * ch:(4 + 2 * lv) * ch, :])
        qd = q * eq
        kd = k * ek
        a = a + msk_ref[lv] * _dot_nt(qd.astype(BF16), kd.astype(BF16))
        ops.append((eq, ek, qd, kd))
    return a, ops


def _hgrn_fwd(p, lb_logits, gnorm, mst, msk, n_lev, tm):
    tp = p.shape[0]
    ch = HG_CHUNK
    nct = tm // ch
    nt = tp // tm
    nr = mst.shape[0]
    base = D_POOL // HEAD

    def body(q_ref, z_ref, v_ref, g_ref, lg_ref, gn_ref, mst_ref, msk_ref, y_ref, ss_ref, st_sc):
        @pl.when(pl.program_id(1) == 0)
        def _():
            st_sc[...] = jnp.zeros_like(st_sc)

        lb = _sigmoid(lg_ref[0:1, :] - lg_ref[1:2, :])

        def chunk(c, carry):
            r0 = pl.multiple_of(c * ch, ch)
            q, k, f, _ = _hgrn_gates(q_ref[pl.ds(r0, ch), :], z_ref[pl.ds(r0, ch), :], lb)
            v = v_ref[pl.ds(r0, ch), :]
            x = _hgrn_exponents(mst_ref[...], jnp.log(f))
            st = st_sc[...]
            ss_ref[0, c] = st
            qe = q * jnp.exp(x[0:ch, :])
            a, _ = _hgrn_intra(q, k, x, msk_ref, n_lev, ch)
            v16 = v.astype(BF16)
            o = _dot_nt(qe.astype(BF16), st.astype(BF16)) + _dot(a.astype(BF16), v16)
            kl = k * jnp.exp(x[ch:2 * ch, :])
            st_sc[...] = st * jnp.exp(x[ch - 1:ch, :]) + _dot_tn(v16, kl.astype(BF16))
            rstd = lax.rsqrt(jnp.mean(o * o, axis=-1, keepdims=True) + EPS)
            g_raw = g_ref[pl.ds(r0, ch), :]
            y_ref[pl.ds(r0, ch), :] = o * rstd * gn_ref[...] * (g_raw * _sigmoid(g_raw))
            return carry

        lax.fori_loop(0, nct, chunk, 0)

    def pspec(seg):
        return pl.BlockSpec((tm, HEAD), lambda h, i: (i, base + seg * HG_HEADS + h))

    return pl.pallas_call(
        body, name="hgrn_fwd",
        grid=(HG_HEADS, nt),
        in_specs=[pspec(0), pspec(1), pspec(2), pspec(3),
                  pl.BlockSpec((2, HEAD), lambda h, i: (0, h)),
                  pl.BlockSpec((1, HEAD), lambda h, i: (0, 0)),
                  pl.BlockSpec((nr, ch), lambda h, i: (0, 0)),
                  pl.BlockSpec((n_lev + 1, ch, ch), lambda h, i: (0, 0, 0))],
        out_specs=[pl.BlockSpec((tm, HEAD), lambda h, i: (i, h)),
                   pl.BlockSpec((1, nct, HEAD, HEAD), lambda h, i: (h, i, 0, 0))],
        out_shape=[jax.ShapeDtypeStruct((tp, D_HGRN), F32),
                   jax.ShapeDtypeStruct((HG_HEADS, tp // ch, HEAD, HEAD), F32)],
        scratch_shapes=[pltpu.VMEM((HEAD, HEAD), F32)],
        compiler_params=_cparams(2),
    )(p, p, p, p, lb_logits, gnorm, mst, msk)


def _hgrn_bwd(p, dyb, states, lb_logits, gnorm, mst, msk, n_lev, tm):
    tp = p.shape[0]
    ch = HG_CHUNK
    nct = tm // ch
    nt = tp // tm
    nr = mst.shape[0]
    base = D_POOL // HEAD

    def body(q_ref, z_ref, v_ref, g_ref, dy_ref, ss_ref, lg_ref, gn_ref, mst_ref, msk_ref,
             dq_ref, dz_ref, dv_ref, dg_ref, dlg_ref, dgn_ref, dst_sc, dlb_sc, stk):
        ti = pl.program_id(1)

        @pl.when(ti == 0)
        def _():
            dst_sc[...] = jnp.zeros_like(dst_sc)
            dlb_sc[...] = jnp.zeros_like(dlb_sc)
            dgn_ref[...] = jnp.zeros_like(dgn_ref)

        lb = _sigmoid(lg_ref[0:1, :] - lg_ref[1:2, :])
        gn = gn_ref[...]

        def chunk(cc, carry):
            c = nct - 1 - cc
            r0 = pl.multiple_of(c * ch, ch)
            q_raw = q_ref[pl.ds(r0, ch), :]
            z = z_ref[pl.ds(r0, ch), :]
            q, k, f, sz = _hgrn_gates(q_raw, z, lb)
            v = v_ref[pl.ds(r0, ch), :]
            x = _hgrn_exponents(mst_ref[...], jnp.log(f))
            st = ss_ref[0, c]
            eb = jnp.exp(x[0:ch, :])
            ef = jnp.exp(x[ch:2 * ch, :])
            elast = jnp.exp(x[ch - 1:ch, :])
            qe = q * eb
            kl = k * ef
            a, ops = _hgrn_intra(q, k, x, msk_ref, n_lev, ch)
            v16 = v.astype(BF16)
            st16 = st.astype(BF16)
            qe16 = qe.astype(BF16)
            kl16 = kl.astype(BF16)
            a16 = a.astype(BF16)
            o = _dot_nt(qe16, st16) + _dot(a16, v16)
            g_raw = g_ref[pl.ds(r0, ch), :]
            sg = _sigmoid(g_raw)
            rstd = lax.rsqrt(jnp.mean(o * o, axis=-1, keepdims=True) + EPS)
            oh = o * rstd
            dy = dy_ref[pl.ds(r0, ch), :]
            dg_ref[pl.ds(r0, ch), :] = (dy * oh * gn * (sg * (1.0 + g_raw * (1.0 - sg)))).astype(BF16)
            don = dy * (g_raw * sg)
            dgn_ref[0] += _colsum(don * oh)
            doh = don * gn
            do = rstd * (doh - oh * jnp.mean(doh * oh, axis=-1, keepdims=True))
            do16 = do.astype(BF16)
            dst = dst_sc[...]
            dst16 = dst.astype(BF16)
            dv = _dot_tn(a16, do16) + _dot_nt(kl16, dst16)
            da = msk_ref[n_lev] * _dot_nt(do16, v16)
            dqe = _dot(do16, st16)
            dkl = _dot(v16, dst16)
            dst_sc[...] = dst * elast + _dot_tn(do16, qe16)
            db_last = _colsum(dst * st) * elast
            dad = jnp.sum(do * v, axis=1, keepdims=True)
            dq = dad * k + dqe * eb
            dk = dad * q + dkl * ef
            stk[0:ch, :] = dqe * qe
            stk[ch:2 * ch, :] = dkl * kl
            for lv in range(n_lev):
                eq, ek, qd, kd = ops[lv]
                gl = (msk_ref[lv] * da).astype(BF16)
                dqd = _dot(gl, kd.astype(BF16))
                dkd = _dot_tn(gl, qd.astype(BF16))
                dq = dq + dqd * eq
                dk = dk + dkd * ek
                stk[(2 + 2 * lv) * ch:(3 + 2 * lv) * ch, :] = dqd * qd
                stk[(3 + 2 * lv) * ch:(4 + 2 * lv) * ch, :] = dkd * kd
            sk = stk[...]
            hi = sk.astype(BF16)
            lo = (sk - hi.astype(F32)).astype(BF16)
            dl2 = _dot_tn(mst_ref[...], jnp.concatenate([hi, lo], axis=1))
            dlogf = dl2[:, 0:HEAD] + dl2[:, HEAD:2 * HEAD] + db_last
            sq = _sigmoid(q_raw)
            dq_ref[pl.ds(r0, ch), :] = (dq * (sq * (1.0 + q_raw * (1.0 - sq)))).astype(BF16)
            dfk = dlogf / f - dk
            dz_ref[pl.ds(r0, ch), :] = (dfk * (1.0 - lb) * sz * (1.0 - sz)).astype(BF16)
            dlb_sc[...] += _colsum(dfk * (1.0 - sz))
            dv_ref[pl.ds(r0, ch), :] = dv.astype(BF16)
            return carry

        lax.fori_loop(0, nct, chunk, 0)

        @pl.when(ti == nt - 1)
        def _():
            dl0 = dlb_sc[...] * lb * (1.0 - lb)
            dlg_ref[0:1, :] = dl0
            dlg_ref[1:2, :] = -dl0

    def pspec(seg):
        return pl.BlockSpec((tm, HEAD), lambda h, i: (nt - 1 - i, base + seg * HG_HEADS + h))

    ospec = pl.BlockSpec((tm, HEAD), lambda h, i: (nt - 1 - i, h))
    return pl.pallas_call(
        body, name="hgrn_bwd",
        grid=(HG_HEADS, nt),
        in_specs=[pspec(0), pspec(1), pspec(2), pspec(3), ospec,
                  pl.BlockSpec((1, nct, HEAD, HEAD), lambda h, i: (h, nt - 1 - i, 0, 0)),
                  pl.BlockSpec((2, HEAD), lambda h, i: (0, h)),
                  pl.BlockSpec((1, HEAD), lambda h, i: (0, 0)),
                  pl.BlockSpec((nr, ch), lambda h, i: (0, 0)),
                  pl.BlockSpec((n_lev + 1, ch, ch), lambda h, i: (0, 0, 0))],
        out_specs=[ospec, ospec, ospec, ospec,
                   pl.BlockSpec((2, HEAD), lambda h, i: (0, h)),
                   pl.BlockSpec((1, 1, HEAD), lambda h, i: (h, 0, 0))],
        out_shape=[jax.ShapeDtypeStruct((tp, D_HGRN), BF16)] * 4
        + [jax.ShapeDtypeStruct((2, D_HGRN), F32), jax.ShapeDtypeStruct((HG_HEADS, 1, HEAD), F32)],
        scratch_shapes=[pltpu.VMEM((HEAD, HEAD), F32), pltpu.VMEM((1, HEAD), F32), pltpu.VMEM((nr, HEAD), F32)],
        compiler_params=_cparams(2),
    )(p, p, p, p, dyb, states, lb_logits, gnorm, mst, msk)


def _conv_taps(x, w_ref, tr, halo, width):
    acc = None
    for j in range(width):
        sh = width - 1 - j
        xs = x if sh == 0 else pltpu.roll(x, sh, 0)
        term = xs[halo:, :] * w_ref[j:j + 1, :]
        acc = term if acc is None else acc + term
    return acc


def _conv_taps_t(y, w_ref, tr, halo, width):
    n = tr + halo
    acc = None
    for j in range(width):
        sh = width - 1 - j
        ys = y if sh == 0 else pltpu.roll(y, n - sh, 0)
        term = ys[0:tr, :] * w_ref[j:j + 1, :]
        acc = term if acc is None else acc + term
    return acc


def _ln_stats(cv):
    mu = jnp.mean(cv, axis=-1, keepdims=True)
    xc = cv - mu
    rstd = lax.rsqrt(jnp.mean(xc * xc, axis=-1, keepdims=True) + EPS)
    return rstd, xc * rstd


def _convmod_fwd(p, w, bias, ln_g, ln_b, tr):
    tp = p.shape[0]
    nt = tp // tr
    nb = D_CONV // HEAD

    def body(a_ref, b_ref, w_ref, bi_ref, g_ref, be_ref, y_ref, usc):
        usc[0:CONV_HALO, :] = jnp.zeros((CONV_HALO, HEAD), F32)
        usc[CONV_HALO:, :] = a_ref[...] * _sigmoid(b_ref[...])

        def tile(r, carry):
            r0 = pl.multiple_of(r * tr, SUBLANE)
            x = usc[pl.ds(r0, tr + CONV_HALO), :]
            cv = _conv_taps(x, w_ref, tr, CONV_HALO, CONV_WIDTH) + bi_ref[...]
            _, xh = _ln_stats(cv)
            un = xh * g_ref[...] + be_ref[...]
            y_ref[pl.ds(r0, tr), :] = un * _sigmoid(un)
            return carry

        lax.fori_loop(0, nt, tile, 0)

    vec = lambda: pl.BlockSpec((1, HEAD), lambda j: (0, j))
    return pl.pallas_call(
        body, name="convmod_fwd",
        grid=(nb,),
        in_specs=[pl.BlockSpec((tp, HEAD), lambda j: (0, j)), pl.BlockSpec((tp, HEAD), lambda j: (0, nb + j)),
                  pl.BlockSpec((CONV_HALO, HEAD), lambda j: (0, j)), vec(), vec(), vec()],
        out_specs=pl.BlockSpec((tp, HEAD), lambda j: (0, j)),
        out_shape=jax.ShapeDtypeStruct((tp, D_CONV), F32),
        scratch_shapes=[pltpu.VMEM((tp + CONV_HALO, HEAD), F32)],
        compiler_params=_cparams(1),
    )(p, p, w, bias, ln_g, ln_b)


def _convmod_bwd(p, dyc, w, bias, ln_g, ln_b, tr):
    tp = p.shape[0]
    nt = tp // tr
    nb = D_CONV // HEAD

    def body(a_ref, b_ref, dy_ref, w_ref, bi_ref, g_ref, be_ref, da_ref, db_ref, dw_ref, dv_ref, usc, dsc):
        usc[0:CONV_HALO, :] = jnp.zeros((CONV_HALO, HEAD), F32)
        usc[CONV_HALO:, :] = a_ref[...] * _sigmoid(b_ref[...])
        dsc[tp:, :] = jnp.zeros((CONV_HALO, HEAD), F32)
        dw_ref[...] = jnp.zeros_like(dw_ref)
        dv_ref[...] = jnp.zeros_like(dv_ref)

        def tile1(r, carry):
            r0 = pl.multiple_of(r * tr, SUBLANE)
            x = usc[pl.ds(r0, tr + CONV_HALO), :]
            cv = _conv_taps(x, w_ref, tr, CONV_HALO, CONV_WIDTH) + bi_ref[...]
            rstd, xh = _ln_stats(cv)
            un = xh * g_ref[...] + be_ref[...]
            sg = _sigmoid(un)
            dun = dy_ref[pl.ds(r0, tr), :] * (sg * (1.0 + un * (1.0 - sg)))
            dv_ref[0, 1:2, :] += _colsum(dun * xh)
            dv_ref[0, 2:3, :] += _colsum(dun)
            dxh = dun * g_ref[...]
            dcv = rstd * (dxh - jnp.mean(dxh, axis=-1, keepdims=True)
                          - xh * jnp.mean(dxh * xh, axis=-1, keepdims=True))
            dv_ref[0, 0:1, :] += _colsum(dcv)
            for j in range(CONV_WIDTH):
                sh = CONV_WIDTH - 1 - j
                xs = x if sh == 0 else pltpu.roll(x, sh, 0)
                dw_ref[0, j:j + 1, :] += _colsum(dcv * xs[CONV_HALO:, :])
            dsc[pl.ds(r0, tr), :] = dcv
            return carry

        lax.fori_loop(0, nt, tile1, 0)

        def tile2(r, carry):
            r0 = pl.multiple_of(r * tr, SUBLANE)
            y = dsc[pl.ds(r0, tr + CONV_HALO), :]
            du = _conv_taps_t(y, w_ref, tr, CONV_HALO, CONV_WIDTH)
            a = a_ref[pl.ds(r0, tr), :]
            sb = _sigmoid(b_ref[pl.ds(r0, tr), :])
            da_ref[pl.ds(r0, tr), :] = (du * sb).astype(BF16)
            db_ref[pl.ds(r0, tr), :] = (du * a * sb * (1.0 - sb)).astype(BF16)
            return carry

        lax.fori_loop(0, nt, tile2, 0)

    vec = lambda: pl.BlockSpec((1, HEAD), lambda j: (0, j))
    col = lambda: pl.BlockSpec((tp, HEAD), lambda j: (0, j))
    return pl.pallas_call(
        body, name="convmod_bwd",
        grid=(nb,),
        in_specs=[col(), pl.BlockSpec((tp, HEAD), lambda j: (0, nb + j)), col(),
                  pl.BlockSpec((CONV_HALO, HEAD), lambda j: (0, j)), vec(), vec(), vec()],
        out_specs=[col(), col(), pl.BlockSpec((1, CONV_HALO, HEAD), lambda j: (j, 0, 0)),
                   pl.BlockSpec((1, SUBLANE, HEAD), lambda j: (j, 0, 0))],
        out_shape=[jax.ShapeDtypeStruct((tp, D_CONV), BF16), jax.ShapeDtypeStruct((tp, D_CONV), BF16),
                   jax.ShapeDtypeStruct((nb, CONV_HALO, HEAD), F32), jax.ShapeDtypeStruct((nb, SUBLANE, HEAD), F32)],
        scratch_shapes=[pltpu.VMEM((tp + CONV_HALO, HEAD), F32), pltpu.VMEM((tp + CONV_HALO, HEAD), F32)],
        compiler_params=_cparams(1),
    )(p, p, dyc, w, bias, ln_g, ln_b)


def _log1p_small(y):
    return jnp.where(y < 1e-4, y * (1.0 - 0.5 * y), jnp.log(1.0 + y))


def _softplus(x):
    return jnp.maximum(x, 0.0) + _log1p_small(jnp.exp(-jnp.abs(x)))


def _expm1(x):
    return jnp.where(jnp.abs(x) < 1e-2, x * (1.0 + 0.5 * x * (1.0 + x * (1.0 / 3.0))), jnp.exp(x) - 1.0)


def _gelu_parts(x):
    c = 0.7978845608028654
    inner = c * (x + 0.044715 * x * x * x)
    th = jnp.tanh(inner)
    gelu = 0.5 * x * (1.0 + th)
    dgelu = 0.5 * (1.0 + th) + 0.5 * x * (1.0 - th * th) * c * (1.0 + 3.0 * 0.044715 * x * x)
    return gelu, dgelu


def _lru_gates(x_all, tp, cw_ref, cb_ref, wa_ref, ba_ref, wx_ref, bx_ref, lam_ref):
    u = _conv_taps(x_all, cw_ref, tp, LRU_HALO, LRU_CONV) + cb_ref[...]
    u16 = u.astype(BF16)
    r = _sigmoid(_dot(u16, wa_ref[0]) + ba_ref[...])
    i = _sigmoid(_dot(u16, wx_ref[0]) + bx_ref[...])
    sp = _softplus(-lam_ref[...])
    la = -LRU_C * r * sp
    a = jnp.exp(la)
    mult = jnp.sqrt(-_expm1(2.0 * la))
    return u, r, i, a, mult, sp


def _lru_specs(tp, nb):
    col = lambda k: pl.BlockSpec((tp, HEAD), functools.partial(lambda j, k: (0, k * nb + j), k=k))
    vec = lambda: pl.BlockSpec((1, HEAD), lambda j: (0, j))
    mat = lambda: pl.BlockSpec((1, HEAD, HEAD), lambda j: (j, 0, 0))
    return col, vec, mat


def _lru_fwd(p, cw, cb, wa, ba, wx, bx, lam):
    tp = p.shape[0]
    nb = D_LRU // HEAD
    ng = tp // SUBLANE

    def body(x_ref, gt_ref, cw_ref, cb_ref, wa_ref, ba_ref, wx_ref, bx_ref, lam_ref, y_ref, hs_ref,
             xsc, asc, bsc):
        xsc[0:LRU_HALO, :] = jnp.zeros((LRU_HALO, HEAD), F32)
        xsc[LRU_HALO:, :] = x_ref[...]
        u, r, i, a, mult, _ = _lru_gates(xsc[...], tp, cw_ref, cb_ref, wa_ref, ba_ref, wx_ref, bx_ref, lam_ref)
        rows = lax.broadcasted_iota(jnp.int32, (tp, HEAD), 0)
        b = jnp.where(rows == 0, 1.0, mult) * (i * u)
        sub = rows % SUBLANE
        for k in (1, 2, 4):
            m = sub >= k
            b = jnp.where(m, a * pltpu.roll(b, k, 0) + b, b)
            a = jnp.where(m, a * pltpu.roll(a, k, 0), a)
        asc[...] = a
        bsc[...] = b

        def grp(g, carry):
            r0 = pl.multiple_of(g * SUBLANE, SUBLANE)
            h = bsc[pl.ds(r0, SUBLANE), :] + asc[pl.ds(r0, SUBLANE), :] * carry
            hs_ref[pl.ds(r0, SUBLANE), :] = h
            return jnp.broadcast_to(h[SUBLANE - 1:SUBLANE, :], (SUBLANE, HEAD))

        lax.fori_loop(0, ng, grp, jnp.zeros((SUBLANE, HEAD), F32))
        gelu, _ = _gelu_parts(gt_ref[...])
        y_ref[...] = gelu * hs_ref[...]

    col, vec, mat = _lru_specs(tp, nb)
    return pl.pallas_call(
        body, name="lru_fwd",
        grid=(nb,),
        in_specs=[col(2), col(3), pl.BlockSpec((LRU_CONV, HEAD), lambda j: (0, j)), vec(), mat(), vec(), mat(),
                  vec(), vec()],
        out_specs=[pl.BlockSpec((tp, HEAD), lambda j: (0, j)), pl.BlockSpec((tp, HEAD), lambda j: (0, j))],
        out_shape=[jax.ShapeDtypeStruct((tp, D_LRU), F32), jax.ShapeDtypeStruct((tp, D_LRU), F32)],
        scratch_shapes=[pltpu.VMEM((tp + LRU_HALO, HEAD), F32), pltpu.VMEM((tp, HEAD), F32),
                        pltpu.VMEM((tp, HEAD), F32)],
        compiler_params=_cparams(1),
    )(p, p, cw, cb, wa, ba, wx, bx, lam)


def _lru_bwd(p, hs, dyd, cw, cb, wa, ba, wx, bx, lam):
    tp = p.shape[0]
    nb = D_LRU // HEAD
    ng = tp // SUBLANE

    def body(x_ref, gt_ref, hs_ref, dy_ref, cw_ref, cb_ref, wa_ref, ba_ref, wx_ref, bx_ref, lam_ref,
             dx_ref, dgt_ref, dwa_ref, dwx_ref, dv_ref, xsc, asc, bsc, gsc, dusc):
        xsc[0:LRU_HALO, :] = jnp.zeros((LRU_HALO, HEAD), F32)
        xsc[LRU_HALO:, :] = x_ref[...]
        x_all = xsc[...]
        u, r, i, a, mult, sp = _lru_gates(x_all, tp, cw_ref, cb_ref, wa_ref, ba_ref, wx_ref, bx_ref, lam_ref)
        rows = lax.broadcasted_iota(jnp.int32, (tp, HEAD), 0)
        hs = hs_ref[...]
        dy = dy_ref[...]
        gelu, dgelu = _gelu_parts(gt_ref[...])
        dgt_ref[...] = (dy * hs * dgelu).astype(BF16)
        bb = dy * gelu
        aa = jnp.where(rows == tp - 1, 0.0, pltpu.roll(a, tp - 1, 0))
        sub = rows % SUBLANE
        for k in (1, 2, 4):
            m = sub < SUBLANE - k
            bb = jnp.where(m, aa * pltpu.roll(bb, tp - k, 0) + bb, bb)
            aa = jnp.where(m, aa * pltpu.roll(aa, tp - k, 0), aa)
        asc[...] = aa
        bsc[...] = bb

        def grp(gi, carry):
            g = ng - 1 - gi
            r0 = pl.multiple_of(g * SUBLANE, SUBLANE)
            gg = bsc[pl.ds(r0, SUBLANE), :] + asc[pl.ds(r0, SUBLANE), :] * carry
            gsc[pl.ds(r0, SUBLANE), :] = gg
            return jnp.broadcast_to(gg[0:1, :], (SUBLANE, HEAD))

        lax.fori_loop(0, ng, grp, jnp.zeros((SUBLANE, HEAD), F32))
        g = gsc[...]
        first = rows == 0
        hprev = jnp.where(first, 0.0, pltpu.roll(hs, 1, 0))
        iu = i * u
        d_iu = g * jnp.where(first, 1.0, mult)
        dmult_term = jnp.where(first, 0.0, g * iu * (-(a * a) / mult))
        dla = g * hprev * a + dmult_term
        dr = dla * (-LRU_C) * sp
        dv_ref[0, 7:8, :] = _colsum(dla * (LRU_C * r) * _sigmoid(-lam_ref[...]))
        dpr = dr * r * (1.0 - r)
        dpi = d_iu * u * i * (1.0 - i)
        dv_ref[0, 5:6, :] = _colsum(dpr)
        dv_ref[0, 6:7, :] = _colsum(dpi)
        u16 = u.astype(BF16)
        dpr16 = dpr.astype(BF16)
        dpi16 = dpi.astype(BF16)
        dwa_ref[0] = _dot_tn(u16, dpr16)
        dwx_ref[0] = _dot_tn(u16, dpi16)
        du = d_iu * i + _dot_nt(dpr16, wa_ref[0]) + _dot_nt(dpi16, wx_ref[0])
        dv_ref[0, 4:5, :] = _colsum(du)
        for j in range(LRU_CONV):
            sh = LRU_CONV - 1 - j
            xs = x_all if sh == 0 else pltpu.roll(x_all, sh, 0)
            dv_ref[0, j:j + 1, :] = _colsum(du * xs[LRU_HALO:, :])
        dusc[0:tp, :] = du
        dusc[tp:, :] = jnp.zeros((LRU_HALO, HEAD), F32)
        dx_ref[...] = _conv_taps_t(dusc[...], cw_ref, tp, LRU_HALO, LRU_CONV).astype(BF16)

    col, vec, mat = _lru_specs(tp, nb)
    ocol = lambda: pl.BlockSpec((tp, HEAD), lambda j: (0, j))
    return pl.pallas_call(
        body, name="lru_bwd",
        grid=(nb,),
        in_specs=[col(2), col(3), ocol(), ocol(), pl.BlockSpec((LRU_CONV, HEAD), lambda j: (0, j)), vec(), mat(),
                  vec(), mat(), vec(), vec()],
        out_specs=[ocol(), ocol(), mat(), mat(), pl.BlockSpec((1, SUBLANE, HEAD), lambda j: (j, 0, 0))],
        out_shape=[jax.ShapeDtypeStruct((tp, D_LRU), BF16), jax.ShapeDtypeStruct((tp, D_LRU), BF16),
                   jax.ShapeDtypeStruct((nb, HEAD, HEAD), F32), jax.ShapeDtypeStruct((nb, HEAD, HEAD), F32),
                   jax.ShapeDtypeStruct((nb, SUBLANE, HEAD), F32)],
        scratch_shapes=[pltpu.VMEM((tp + LRU_HALO, HEAD), F32), pltpu.VMEM((tp, HEAD), F32),
                        pltpu.VMEM((tp, HEAD), F32), pltpu.VMEM((tp, HEAD), F32),
                        pltpu.VMEM((tp + LRU_HALO, HEAD), F32)],
        compiler_params=_cparams(1),
    )(p, p, hs, dyd, cw, cb, wa, ba, wx, bx, lam)


def _mesh_pos():
    return lax.axis_index("x"), lax.axis_index("y"), lax.axis_index("c")


def _other_chips(x, y):
    return [(1 - x, y), (x, 1 - y), (1 - x, 1 - y)]


ANY = pl.BlockSpec(memory_space=pl.ANY)


def _allgather_shards(arrs, split):
    n = len(arrs)

    def body(*refs):
        ins, outs = refs[:n], refs[n:2 * n]
        send1, recv1, send2, recv2, lsem = refs[2 * n:]
        x, y, c = _mesh_pos()
        chip = 2 * x + y
        sibling = (x, y, 1 - c)
        others = _other_chips(x, y)

        def rows(k, cc):
            half = arrs[k].shape[0] // 2
            return pl.ds(cc * half, half)

        def remote(src, dst, ssem, rsem, dev):
            return pltpu.make_async_remote_copy(src_ref=src, dst_ref=dst, send_sem=ssem, recv_sem=rsem,
                                                device_id=dev, device_id_type=MESH)

        local = [pltpu.make_async_copy(ins[k], outs[k].at[chip], lsem.at[k]) for k in range(n)]
        for cp in local:
            cp.start()
        started = []
        for k in range(n):
            for j, (ox, oy) in enumerate(others):
                if split[k]:
                    src, dst = ins[k].at[rows(k, c)], outs[k].at[chip, rows(k, c)]
                else:
                    src, dst = ins[k], outs[k].at[chip]
                cp = remote(src, dst, send1.at[3 * k + j], recv1.at[3 * k + j], (ox, oy, c))
                cp.start()
                started.append(cp)
        for j, (ox, oy) in enumerate(others):
            ochip = 2 * ox + oy
            for k in range(n):
                if split[k]:
                    blk = outs[k].at[ochip, rows(k, c)]
                    remote(blk, blk, send1.at[3 * k + j], recv1.at[3 * k + j], sibling).wait_recv()
                    cp = remote(blk, blk, send2.at[3 * k + j], recv2.at[3 * k + j], sibling)
                    cp.start()
                    started.append(cp)
                else:
                    blk = outs[k].at[ochip]
                    remote(blk, blk, send1.at[3 * k + j], recv1.at[3 * k + j], sibling).wait_recv()
        for j, (ox, oy) in enumerate(others):
            ochip = 2 * ox + oy
            for k in range(n):
                if split[k]:
                    blk = outs[k].at[ochip, rows(k, 1 - c)]
                    remote(blk, blk, send2.at[3 * k + j], recv2.at[3 * k + j], sibling).wait_recv()
        for cp in started:
            cp.wait_send()
        for cp in local:
            cp.wait()

    return pl.pallas_call(
        body, name="allgather_shards",
        in_specs=[ANY] * n, out_specs=[ANY] * n,
        out_shape=[jax.ShapeDtypeStruct((N_SHARD,) + a.shape, a.dtype) for a in arrs],
        scratch_shapes=[pltpu.SemaphoreType.DMA((3 * n,)), pltpu.SemaphoreType.DMA((3 * n,)),
                        pltpu.SemaphoreType.DMA((3 * n,)), pltpu.SemaphoreType.DMA((3 * n,)),
                        pltpu.SemaphoreType.DMA((n,))],
    )(*arrs)


def _pair_exchange_halves(arrs):
    n = len(arrs)

    def body(*refs):
        ins, outs = refs[:n], refs[n:2 * n]
        ssem, rsem = refs[2 * n:]
        x, y, c = _mesh_pos()
        cps = []
        for k in range(n):
            half = arrs[k].shape[1] // 2
            cp = pltpu.make_async_remote_copy(
                src_ref=ins[k].at[:, pl.ds((1 - c) * half, half)], dst_ref=outs[k],
                send_sem=ssem.at[k], recv_sem=rsem.at[k], device_id=(x, y, 1 - c), device_id_type=MESH)
            cp.start()
            cps.append(cp)
        for cp in cps:
            cp.wait()

    return pl.pallas_call(
        body, name="pair_exchange_halves",
        in_specs=[ANY] * n, out_specs=[ANY] * n,
        out_shape=[jax.ShapeDtypeStruct((a.shape[0], a.shape[1] // 2, a.shape[2]), a.dtype) for a in arrs],
        scratch_shapes=[pltpu.SemaphoreType.DMA((n,)), pltpu.SemaphoreType.DMA((n,))],
    )(*arrs)


GRAD_ROW_BLOCKS = 4


def _pair_add(arrs, recvd, core):
    n = len(arrs)
    nb = GRAD_ROW_BLOCKS

    def body(c_ref, *refs):
        for k in range(n):
            refs[2 * n + k][...] = (refs[k][...].astype(F32) + refs[n + k][...].astype(F32)).astype(BF16)

    def blk(a):
        return (1, a.shape[1] // 2 // nb, a.shape[2])

    grid_spec = pltpu.PrefetchScalarGridSpec(
        num_scalar_prefetch=1, grid=(N_SHARD, nb),
        in_specs=[pl.BlockSpec(blk(a), lambda s, i, c: (s, c[0] * nb + i, 0)) for a in arrs]
        + [pl.BlockSpec(blk(a), lambda s, i, c: (s, i, 0)) for a in arrs],
        out_specs=[pl.BlockSpec(blk(a), lambda s, i, c: (s, i, 0)) for a in arrs])
    return pl.pallas_call(
        body, name="pair_add", grid_spec=grid_spec,
        out_shape=[jax.ShapeDtypeStruct(r.shape, BF16) for r in recvd],
        compiler_params=_cparams(2),
    )(core, *arrs, *recvd)


def _scatter_to_owners(arrs):
    n = len(arrs)

    def body(*refs):
        ins, outs = refs[:n], refs[n:2 * n]
        ssem, rsem, lsem = refs[2 * n:]
        x, y, c = _mesh_pos()
        chip = 2 * x + y
        others = _other_chips(x, y)
        local = [pltpu.make_async_copy(ins[k].at[chip], outs[k].at[chip], lsem.at[k]) for k in range(n)]
        for cp in local:
            cp.start()
        cps = []
        for k in range(n):
            for j, (ox, oy) in enumerate(others):
                cp = pltpu.make_async_remote_copy(
                    src_ref=ins[k].at[2 * ox + oy], dst_ref=outs[k].at[chip],
                    send_sem=ssem.at[3 * k + j], recv_sem=rsem.at[3 * k + j],
                    device_id=(ox, oy, c), device_id_type=MESH)
                cp.start()
                cps.append(cp)
        for k in range(n):
            for j, (ox, oy) in enumerate(others):
                blk = outs[k].at[2 * ox + oy]
                pltpu.make_async_remote_copy(
                    src_ref=blk, dst_ref=blk, send_sem=ssem.at[3 * k + j], recv_sem=rsem.at[3 * k + j],
                    device_id=(ox, oy, c), device_id_type=MESH).wait_recv()
        for cp in cps:
            cp.wait_send()
        for cp in local:
            cp.wait()

    return pl.pallas_call(
        body, name="scatter_to_owners",
        in_specs=[ANY] * n, out_specs=[ANY] * n,
        out_shape=[jax.ShapeDtypeStruct(a.shape, a.dtype) for a in arrs],
        scratch_shapes=[pltpu.SemaphoreType.DMA((3 * n,)), pltpu.SemaphoreType.DMA((3 * n,)),
                        pltpu.SemaphoreType.DMA((n,))],
    )(*arrs)


def _sum_chips(arrs):
    n = len(arrs)
    nb = GRAD_ROW_BLOCKS

    def body(*refs):
        for k in range(n):
            r = refs[k]
            refs[n + k][...] = ((r[0].astype(F32) + r[1].astype(F32)) + r[2].astype(F32)) + r[3].astype(F32)

    return pl.pallas_call(
        body, name="sum_chips", grid=(nb,),
        in_specs=[pl.BlockSpec((N_SHARD, a.shape[1] // nb, a.shape[2]), lambda i: (0, i, 0)) for a in arrs],
        out_specs=[pl.BlockSpec((a.shape[1] // nb, a.shape[2]), lambda i: (i, 0)) for a in arrs],
        out_shape=[jax.ShapeDtypeStruct(a.shape[1:], F32) for a in arrs],
        compiler_params=_cparams(1),
    )(*arrs)


def _pair_allgather_halves(arrs):
    n = len(arrs)

    def body(*refs):
        ins, outs = refs[:n], refs[n:2 * n]
        ssem, rsem, lsem = refs[2 * n:]
        x, y, c = _mesh_pos()
        cps = []
        for k in range(n):
            h = arrs[k].shape[0]
            mine = outs[k].at[pl.ds(c * h, h)]
            lc = pltpu.make_async_copy(ins[k], mine, lsem.at[k])
            lc.start()
            cp = pltpu.make_async_remote_copy(src_ref=ins[k], dst_ref=mine, send_sem=ssem.at[k],
                                              recv_sem=rsem.at[k], device_id=(x, y, 1 - c), device_id_type=MESH)
            cp.start()
            cps.append((lc, cp))
        for k, (lc, cp) in enumerate(cps):
            h = arrs[k].shape[0]
            theirs = outs[k].at[pl.ds((1 - c) * h, h)]
            pltpu.make_async_remote_copy(src_ref=theirs, dst_ref=theirs, send_sem=ssem.at[k], recv_sem=rsem.at[k],
                                         device_id=(x, y, 1 - c), device_id_type=MESH).wait_recv()
            cp.wait_send()
            lc.wait()

    return pl.pallas_call(
        body, name="pair_allgather_halves",
        in_specs=[ANY] * n, out_specs=[ANY] * n,
        out_shape=[jax.ShapeDtypeStruct((2 * a.shape[0], a.shape[1]), a.dtype) for a in arrs],
        scratch_shapes=[pltpu.SemaphoreType.DMA((n,)), pltpu.SemaphoreType.DMA((n,)),
                        pltpu.SemaphoreType.DMA((n,))],
    )(*arrs)


N_DEV = 8


def _allgather_all(v):
    m_per, n = v.shape

    def body(x_ref, out_ref, send_sems, recv_sems, local_sem):
        x, y, c = _mesh_pos()
        me, sibling = (x, y, c), (x, y, 1 - c)
        chips = _other_chips(x, y)

        def rows(px, py, pc):
            return out_ref.at[pl.ds((4 * px + 2 * py + pc) * m_per, m_per), :]

        def copy(k, block, to, src=None):
            return pltpu.make_async_remote_copy(
                src_ref=rows(*block) if src is None else src, dst_ref=rows(*block),
                send_sem=send_sems.at[k], recv_sem=recv_sems.at[k], device_id=to, device_id_type=MESH)

        mine = pltpu.make_async_copy(x_ref, rows(*me), local_sem)
        mine.start()
        first = [copy(0, me, sibling, src=x_ref)]
        first += [copy(1 + j, me, (*chip, c), src=x_ref) for j, chip in enumerate(chips)]
        for cp in first:
            cp.start()
        passed = [copy(4 + j, (*chip, c), sibling) for j, chip in enumerate(chips)]
        for j, chip in enumerate(chips):
            copy(1 + j, (*chip, c), me).wait_recv()
            passed[j].start()
        copy(0, sibling, me).wait_recv()
        for j, chip in enumerate(chips):
            copy(4 + j, (*chip, 1 - c), me).wait_recv()
        for cp in first + passed:
            cp.wait_send()
        mine.wait()

    return pl.pallas_call(
        body, name="allgather_all",
        out_shape=jax.ShapeDtypeStruct((N_DEV * m_per, n), v.dtype),
        in_specs=[pl.BlockSpec(memory_space=pltpu.VMEM)],
        out_specs=pl.BlockSpec(memory_space=pltpu.VMEM),
        scratch_shapes=[pltpu.SemaphoreType.DMA((7,)), pltpu.SemaphoreType.DMA((7,)), pltpu.SemaphoreType.DMA],
        compiler_params=pltpu.CompilerParams(vmem_limit_bytes=VMEM_LIMIT_MB * 1024 * 1024),
    )(v)


def _adamw_math(w, g, m, v):
    m2 = ADAM_B1 * m + (1.0 - ADAM_B1) * g
    v2 = ADAM_B2 * v + (1.0 - ADAM_B2) * (g * g)
    m_hat = m2 / (1.0 - ADAM_B1 ** ADAM_STEP)
    v_hat = v2 / (1.0 - ADAM_B2 ** ADAM_STEP)
    delta = -ADAM_LR * (m_hat / (jnp.sqrt(v_hat) + ADAM_EPS) + ADAM_WD * w)
    return delta, m2, v2


def _adamw(w, m, v, g, row_off, nblk):
    r, n = w.shape
    br = r // nblk
    assert row_off % br == 0
    ob = row_off // br

    def body(w_ref, m_ref, v_ref, g_ref, go_ref, d_ref, mo_ref, vo_ref):
        g = g_ref[...]
        delta, m2, v2 = _adamw_math(w_ref[...], g, m_ref[...], v_ref[...])
        go_ref[...] = g
        d_ref[...] = delta
        mo_ref[...] = m2
        vo_ref[...] = v2

    spec = pl.BlockSpec((br, n), lambda i: (i, 0))
    return pl.pallas_call(
        body, name="adamw", grid=(nblk,),
        in_specs=[spec, spec, spec, pl.BlockSpec((br, n), lambda i: (ob + i, 0))],
        out_specs=[spec] * 4,
        out_shape=[jax.ShapeDtypeStruct((r, n), F32)] * 4,
        compiler_params=_cparams(1),
    )(w, m, v, g)


def _small_reduce_adamw(parts, w, m, v, rep_rows, sh_rows):
    mrows = rep_rows + N_SHARD * sh_rows

    def body(p_ref, w_ref, m_ref, v_ref, go_ref, d_ref, mo_ref, vo_ref):
        x, y, _ = _mesh_pos()
        mine = rep_rows + (2 * x + y) * sh_rows
        g_rep = p_ref[0:rep_rows, :]
        g_sh = p_ref[pl.ds(pl.multiple_of(mine, SUBLANE), sh_rows), :]
        for k in range(1, N_DEV):
            g_rep = g_rep + p_ref[k * mrows:k * mrows + rep_rows, :]
            g_sh = g_sh + p_ref[pl.ds(pl.multiple_of(k * mrows + mine, SUBLANE), sh_rows), :]
        g = jnp.concatenate([g_rep, g_sh], axis=0)
        delta, m2, v2 = _adamw_math(w_ref[...], g, m_ref[...], v_ref[...])
        go_ref[...] = g
        d_ref[...] = delta
        mo_ref[...] = m2
        vo_ref[...] = v2

    return pl.pallas_call(
        body, name="small_reduce_adamw",
        out_shape=[jax.ShapeDtypeStruct((rep_rows + sh_rows, 128), F32)] * 4,
        compiler_params=pltpu.CompilerParams(vmem_limit_bytes=VMEM_LIMIT_MB * 1024 * 1024),
    )(parts, w, m, v)


LANE = 128
REP_SPEC = (("ffn1_norm", 16), ("mix_norm", 16), ("ffn2_norm", 16), ("final_norm", 8), ("pool_w", 128),
            ("pool_scale", 8), ("hgrn_lb_logits", 16), ("hgrn_gnorm", 8), ("lru_wa", 256), ("lru_wx", 256))
SH_SPEC = (("meta_tokens", 32), ("conv_w", 32), ("lru_conv_w", 8), ("conv_b", 8), ("conv_ln_g", 8),
           ("conv_ln_b", 8), ("lru_conv_b", 8), ("lru_ba", 8), ("lru_bx", 8), ("lru_lambda", 8))
REP_ROWS = sum(r for _, r in REP_SPEC)
SH_ROWS = sum(r for _, r in SH_SPEC)


def _pack_rows(vals, spec):
    parts = []
    for name, rows in spec:
        flat = vals[name].astype(F32).reshape(-1, LANE)
        if flat.shape[0] < rows:
            flat = jnp.concatenate([flat, jnp.zeros((rows - flat.shape[0], LANE), F32)], axis=0)
        parts.append(flat)
    return jnp.concatenate(parts, axis=0)


def _unpack_rows(packed, spec, shapes):
    out = {}
    off = 0
    for name, rows in spec:
        shp = shapes[name]
        n = int(np.prod(shp)) // LANE
        out[name] = packed[off:off + n].reshape(shp)
        off += rows
    return out


def _block_diag(blocks):
    n, b, _ = blocks.shape
    return sum(jnp.pad(blocks[g], ((g * b, (n - 1 - g) * b), (g * b, (n - 1 - g) * b))) for g in range(n))


def _diag_blocks(mat, n):
    b = mat.shape[0] // n
    return jnp.stack([mat[g * b:(g + 1) * b, g * b:(g + 1) * b] for g in range(n)])


BIG = ("ffn1_wg", "ffn1_wu", "ffn2_wg", "ffn2_wu", "ffn1_wd", "ffn2_wd", "w_in_even", "w_out_even",
       "w_in_odd", "w_out_odd")
ADAM_BLOCKS = {"ffn1_wg": 8, "ffn1_wu": 8, "ffn2_wg": 8, "ffn2_wu": 8, "ffn1_wd": 4, "ffn2_wd": 4,
               "w_in_even": 4, "w_out_even": 2, "w_in_odd": 4, "w_out_odd": 2}
WEIGHT_NAMES = ('meta_tokens', 'ffn1_norm', 'ffn1_wg', 'ffn1_wu', 'ffn1_wd', 'mix_norm', 'ffn2_norm', 'ffn2_wg',
                'ffn2_wu', 'ffn2_wd', 'w_in_even', 'pool_w', 'pool_scale', 'hgrn_lb_logits', 'hgrn_gnorm',
                'w_out_even', 'w_in_odd', 'conv_w', 'conv_b', 'conv_ln_g', 'conv_ln_b', 'lru_conv_w',
                'lru_conv_b', 'lru_wa', 'lru_ba', 'lru_wx', 'lru_bx', 'lru_lambda', 'w_out_odd', 'final_norm')


def _rows2d(a):
    return a.reshape(-1, a.shape[-1])


def _local_step(x, tgt, w, gathered, small_full):
    s_len, d = x.shape
    t_real = s_len + N_META
    tp = -(-t_real // ROW_ALIGN) * ROW_ALIGN
    tm = _tile(tp, 832, ROW_ALIGN)
    tm_small = _tile(tp, 416, 16)
    tr = _tile(tp, 416, SUBLANE)
    f1 = ("ffn1_norm", "ffn1_wg", "ffn1_wu", "ffn1_wd")
    f2 = ("ffn2_norm", "ffn2_wg", "ffn2_wu", "ffn2_wd")

    meta_full = small_full["meta_tokens"]
    h0 = jnp.concatenate([meta_full, x, jnp.zeros((tp - t_real, d), F32)], axis=0)
    tgt_pad = jnp.concatenate([jnp.zeros((N_META, d), F32), tgt, jnp.zeros((tp - t_real, d), F32)], axis=0)

    w_in_even = jnp.transpose(gathered["w_in_even"], (1, 0, 2)).reshape(d, D_IN_EVEN)
    w_out_even = gathered["w_out_even"].reshape(d, d)
    w_out_odd = gathered["w_out_odd"].reshape(d, d)
    even_piece = [(w_in_even, (d, D_IN_EVEN), (0, 0))]
    odd_pieces = [(gathered["w_in_odd"], (1, d, D_IN_ODD // N_SHARD), (k, 0, 0)) for k in range(N_SHARD)]
    pool_wbd = _block_diag(w["pool_w"][0]).astype(BF16)
    pool_scale = w["pool_scale"]
    wa_bd = _block_diag2(w["lru_wa"][0]).astype(BF16)
    wx_bd = _block_diag2(w["lru_wx"][0]).astype(BF16)
    mst, msk, n_lev = _hgrn_consts(HG_CHUNK)
    conv_w = small_full["conv_w"]
    sf = small_full

    def gain(name, layer):
        return w[name][layer:layer + 1]

    def ffn(h, names, layer):
        return _ffn_fwd(h, gain(names[0], layer), gathered[names[1]], gathered[names[2]], gathered[names[3]],
                        layer, tm)

    h1, a1, b1, n1 = ffn(h0, f1, 0)
    p0, nm0 = _proj_fwd(h1, gain("mix_norm", 0), 0, even_piece, tm_small)
    ya = _pool_fwd(p0, pool_wbd, pool_scale, tr)
    yb, states = _hgrn_fwd(p0, w["hgrn_lb_logits"], w["hgrn_gnorm"], mst, msk, n_lev, tm)
    h2 = _out_fwd(h1, ya, yb, w_out_even, tm)
    h3, a2, b2, n2 = ffn(h2, f2, 0)
    h4, a3, b3, n3 = ffn(h3, f1, 1)
    p1, nm1 = _proj_fwd(h4, gain("mix_norm", 1), 1, odd_pieces, tm_small)
    yc = _convmod_fwd(p1, conv_w, sf["conv_b"], sf["conv_ln_g"], sf["conv_ln_b"], tr)
    lru_args = (sf["lru_conv_w"], sf["lru_conv_b"], wa_bd, sf["lru_ba"], wx_bd, sf["lru_bx"], sf["lru_lambda"])
    yd, hs = _lru_fwd(p1, *lru_args)
    h5 = _out_fwd(h4, yc, yd, w_out_odd, tm)
    h6, a4, b4, n4 = ffn(h5, f2, 1)
    loss, dh6, dg_final = _loss_bwd(h6, w["final_norm"].reshape(1, d), tgt_pad, t_real, tm)

    def ffn_bwd(dho, h, n, a, b, names, layer, acc):
        dh, da, db, dg = _ffn_bwd_act(dho, h, gain(names[0], layer), a, b, gathered[names[1]], gathered[names[2]],
                                      gathered[names[3]], layer, tm_small)
        acc = _ffn_bwd_w(dho, n, a, b, da, db, acc[0], acc[1], acc[2], layer, tm)
        return dh, dg, acc

    none3 = (None, None, None)
    dh5, dg_f2_l1, g_f2 = ffn_bwd(dh6, h5, n4, a4, b4, f2, 1, none3)
    dyc, dyd, dw_out_odd = _out_bwd(dh5, yc, yd, w_out_odd, tm)
    dca, dcb, dconv_w, dconv_vec = _convmod_bwd(p1, dyc, conv_w, sf["conv_b"], sf["conv_ln_g"], sf["conv_ln_b"], tr)
    dlx, dlg, dwa_bd, dwx_bd, dlru_vec = _lru_bwd(p1, hs, dyd, *lru_args)
    dp1 = [dca, dcb, dlx, dlg]
    dh4, dg_mix_l1 = _proj_bwd_act(dh5, h4, gain("mix_norm", 1), 1, dp1, odd_pieces, tm_small)
    dw_in_odd = jnp.stack(_proj_bwd_w(nm1, dp1, tm))
    dh3, dg_f1_l1, g_f1 = ffn_bwd(dh4, h3, n3, a3, b3, f1, 1, none3)
    dh2, dg_f2_l0, g_f2 = ffn_bwd(dh3, h2, n2, a2, b2, f2, 0, g_f2)
    dya, dyb, dw_out_even = _out_bwd(dh2, ya, yb, w_out_even, tm)
    dpool, dpool_wbd, dpool_scale = _pool_bwd(p0, dya, pool_wbd, pool_scale, tr)
    dq, dz, dv, dgate, dlb_logits, dgn_heads = _hgrn_bwd(p0, dyb, states, w["hgrn_lb_logits"], w["hgrn_gnorm"],
                                                         mst, msk, n_lev, tm)
    dp0 = [jnp.concatenate([dpool, dq, dz, dv, dgate], axis=1)]
    dh1, dg_mix_l0 = _proj_bwd_act(dh2, h1, gain("mix_norm", 0), 0, dp0, even_piece, tm_small)
    (dw_in_even,) = _proj_bwd_w(nm0, dp0, tm_small)
    dh0, dg_f1_l0, g_f1 = ffn_bwd(dh1, h0, n1, a1, b1, f1, 0, g_f1)

    grad_x = dh0[N_META:t_real]
    big = {
        "ffn1_wg": g_f1[0], "ffn1_wu": g_f1[1], "ffn1_wd": g_f1[2],
        "ffn2_wg": g_f2[0], "ffn2_wu": g_f2[1], "ffn2_wd": g_f2[2],
        "w_in_even": jnp.transpose(dw_in_even.reshape(d, N_SHARD, D_IN_EVEN // N_SHARD), (1, 0, 2)),
        "w_out_even": dw_out_even.reshape(N_SHARD, d // N_SHARD, d),
        "w_in_odd": dw_in_odd,
        "w_out_odd": dw_out_odd.reshape(N_SHARD, d // N_SHARD, d),
    }
    rep = {
        "ffn1_norm": jnp.concatenate([dg_f1_l0, dg_f1_l1], axis=0),
        "mix_norm": jnp.concatenate([dg_mix_l0, dg_mix_l1], axis=0),
        "ffn2_norm": jnp.concatenate([dg_f2_l0, dg_f2_l1], axis=0),
        "final_norm": dg_final,
        "pool_w": _diag_blocks(dpool_wbd, len(POOL_WINDOWS)),
        "pool_scale": dpool_scale,
        "hgrn_lb_logits": dlb_logits,
        "hgrn_gnorm": jnp.sum(dgn_heads, axis=0),
        "lru_wa": _diag_blocks2(dwa_bd),
        "lru_wx": _diag_blocks2(dwx_bd),
    }
    dmeta = jnp.transpose(dh0[:N_META].reshape(N_META, N_SHARD, 2, LANE), (1, 0, 2, 3)).reshape(N_SHARD, 32, LANE)
    packs = [_pack_rows(rep, REP_SPEC)]
    for s in range(N_SHARD):
        sh = {
            "meta_tokens": dmeta[s], "conv_w": dconv_w[s], "lru_conv_w": dlru_vec[s, 0:4],
            "conv_b": dconv_vec[s, 0:1], "conv_ln_g": dconv_vec[s, 1:2], "conv_ln_b": dconv_vec[s, 2:3],
            "lru_conv_b": dlru_vec[s, 4:5], "lru_ba": dlru_vec[s, 5:6], "lru_bx": dlru_vec[s, 6:7],
            "lru_lambda": dlru_vec[s, 7:8],
        }
        packs.append(_pack_rows(sh, SH_SPEC))
    return loss, grad_x, big, jnp.concatenate(packs, axis=0)


def _block_diag2(heads):
    nb = heads.shape[0] // 2
    return jnp.stack([_block_diag(heads[2 * j:2 * j + 2]) for j in range(nb)])


def _diag_blocks2(mats):
    return jnp.concatenate([_diag_blocks(mats[j], 2) for j in range(mats.shape[0])], axis=0)


def kernel(x, meta_tokens, ffn1_norm, ffn1_wg, ffn1_wu, ffn1_wd, mix_norm, ffn2_norm, ffn2_wg, ffn2_wu, ffn2_wd, w_in_even, pool_w, pool_scale, hgrn_lb_logits, hgrn_gnorm, w_out_even, w_in_odd, conv_w, conv_b, conv_ln_g, conv_ln_b, lru_conv_w, lru_conv_b, lru_wa, lru_ba, lru_wx, lru_bx, lru_lambda, w_out_odd, final_norm, loss_target, m_meta_tokens, m_ffn1_norm, m_ffn1_wg, m_ffn1_wu, m_ffn1_wd, m_mix_norm, m_ffn2_norm, m_ffn2_wg, m_ffn2_wu, m_ffn2_wd, m_w_in_even, m_pool_w, m_pool_scale, m_hgrn_lb_logits, m_hgrn_gnorm, m_w_out_even, m_w_in_odd, m_conv_w, m_conv_b, m_conv_ln_g, m_conv_ln_b, m_lru_conv_w, m_lru_conv_b, m_lru_wa, m_lru_ba, m_lru_wx, m_lru_bx, m_lru_lambda, m_w_out_odd, m_final_norm, v_meta_tokens, v_ffn1_norm, v_ffn1_wg, v_ffn1_wu, v_ffn1_wd, v_mix_norm, v_ffn2_norm, v_ffn2_wg, v_ffn2_wu, v_ffn2_wd, v_w_in_even, v_pool_w, v_pool_scale, v_hgrn_lb_logits, v_hgrn_gnorm, v_w_out_even, v_w_in_odd, v_conv_w, v_conv_b, v_conv_ln_g, v_conv_ln_b, v_lru_conv_w, v_lru_conv_b, v_lru_wa, v_lru_ba, v_lru_wx, v_lru_bx, v_lru_lambda, v_w_out_odd, v_final_norm):
    args = locals()
    w = {n: args[n] for n in WEIGHT_NAMES}
    m = {n: args["m_" + n] for n in WEIGHT_NAMES}
    v = {n: args["v_" + n] for n in WEIGHT_NAMES}
    shapes = {n: w[n].shape for n in WEIGHT_NAMES}

    big_in = [_rows2d(w[n]).astype(BF16) for n in BIG]
    small_sh = _pack_rows(w, SH_SPEC)
    gath = _allgather_shards(big_in + [small_sh], [True] * len(BIG) + [False])
    gathered = dict(zip(BIG, gath[:len(BIG)]))
    sm = gath[len(BIG)]
    sh_shapes = {n: (N_SHARD,) + tuple(shapes[n]) for n, _ in SH_SPEC}
    per_shard = [_unpack_rows(sm[s], SH_SPEC, shapes) for s in range(N_SHARD)]
    small_full = {}
    for n, _ in SH_SPEC:
        stacked = [per_shard[s][n] for s in range(N_SHARD)]
        small_full[n] = jnp.concatenate([p.reshape(-1, p.shape[-1]) for p in stacked], axis=-1)
    small_full["conv_w"] = jnp.concatenate(
        [small_full["conv_w"], jnp.zeros((CONV_HALO - CONV_WIDTH, D_CONV), F32)], axis=0)

    loss, grad_x, big, small_part = _local_step(x[0], loss_target[0], w, gathered, small_full)
    loss = lax.psum(loss[0, 0], ("x", "y", "c"))

    core = lax.axis_index("c").astype(jnp.int32).reshape(1)
    parts = [big[n] for n in BIG]
    recvd = _pair_exchange_halves(parts)
    pair = _pair_add(parts, recvd, core)
    slots = _scatter_to_owners(pair)
    halves = _sum_chips(slots)
    full = _pair_allgather_halves(halves)
    out_g, out_d, out_m, out_v = {}, {}, {}, {}
    for n, g in zip(BIG, full):
        res = _adamw(_rows2d(w[n]), _rows2d(m[n]), _rows2d(v[n]), g, 0, ADAM_BLOCKS[n])
        out_g[n], out_d[n], out_m[n], out_v[n] = [r.reshape(shapes[n]) for r in res]

    gathered_small = _allgather_all(small_part)

    def pack_small(src):
        return jnp.concatenate([_pack_rows(src, REP_SPEC), _pack_rows(src, SH_SPEC)], axis=0)

    res = _small_reduce_adamw(gathered_small, pack_small(w), pack_small(m), pack_small(v), REP_ROWS, SH_ROWS)
    for dst, packed in zip((out_g, out_d, out_m, out_v), res):
        dst.update(_unpack_rows(packed[:REP_ROWS], REP_SPEC, shapes))
        dst.update(_unpack_rows(packed[REP_ROWS:], SH_SPEC, shapes))

    return (loss, grad_x[None], *[out_g[n] for n in WEIGHT_NAMES], *[out_d[n] for n in WEIGHT_NAMES],
            *[out_m[n] for n in WEIGHT_NAMES], *[out_v[n] for n in WEIGHT_NAMES])
```

```python
import functools

import numpy as np
import jax
import jax.numpy as jnp
from jax import lax
from jax.experimental import pallas as pl
from jax.experimental.pallas import tpu as pltpu

F32 = jnp.float32
BF16 = jnp.bfloat16
MESH = pl.DeviceIdType.MESH

EPS = 1e-6
N_META = 16
D_MODEL = 1024
D_FF = 2816
N_SHARD = 4
FF_SHARD = D_FF // N_SHARD
D_POOL = 256
POOL_GROUP = 64
POOL_WINDOWS = (2, 4, 8, 16)
D_HGRN = 768
HG_HEADS = 6
HEAD = 128
HG_CHUNK = 64
D_IN_EVEN = D_POOL + 4 * D_HGRN
D_CONV = 512
CONV_WIDTH = 31
CONV_HALO = 32
D_LRU = 512
LRU_CONV = 4
LRU_HALO = 8
LRU_C = 8.0
D_IN_ODD = 2 * D_CONV + 2 * D_LRU
SUBLANE = 8
ROW_ALIGN = 64

ADAM_LR = 0.001
ADAM_B1 = 0.9
ADAM_B2 = 0.999
ADAM_EPS = 1e-08
ADAM_WD = 0.01
ADAM_STEP = 10

VMEM_LIMIT_MB = 56


def _cparams(n_grid_axes=0, vmem_mb=VMEM_LIMIT_MB):
    sem = ("arbitrary",) * n_grid_axes if n_grid_axes else None
    return pltpu.CompilerParams(dimension_semantics=sem, vmem_limit_bytes=vmem_mb * 1024 * 1024)


def _tile(n, target, mult):
    best = None
    for t in range(mult, min(n, target) + 1, mult):
        if n % t == 0:
            best = t
    assert best is not None, (n, target, mult)
    return best


def _dot(a, b):
    return jnp.dot(a, b, preferred_element_type=F32)


def _dot_nt(a, b):
    return lax.dot_general(a, b, (((1,), (1,)), ((), ())), preferred_element_type=F32)


def _dot_tn(a, b):
    return lax.dot_general(a, b, (((0,), (0,)), ((), ())), preferred_element_type=F32)


def _sigmoid(x):
    return 1.0 / (1.0 + jnp.exp(-x))


def _colsum(x):
    return jnp.sum(x, axis=0, keepdims=True)


def _rms_stats(h):
    rstd = lax.rsqrt(jnp.mean(h * h, axis=-1, keepdims=True) + EPS)
    return rstd, h * rstd


def _rms_bwd(dn, g, rstd, xhat):
    dng = dn * g
    dh = rstd * (dng - xhat * jnp.mean(dng * xhat, axis=-1, keepdims=True))
    return dh, _colsum(dn * xhat)


def _ffn_fwd(h, norm, wg4, wu4, wd4, layer, tm):
    tp, d = h.shape
    nt = tp // tm

    def body(h_ref, g_ref, wg_ref, wu_ref, wd_ref, ho_ref, a_ref, b_ref, n_ref, n_sc, acc):
        s = pl.program_id(1)

        @pl.when(s == 0)
        def _():
            hh = h_ref[...]
            rstd, xhat = _rms_stats(hh)
            n = (xhat * g_ref[...]).astype(BF16)
            n_sc[...] = n
            n_ref[...] = n
            acc[...] = jnp.zeros_like(acc)

        n = n_sc[...]
        a = _dot(n, wg_ref[0])
        b = _dot(n, wu_ref[0])
        a_ref[0] = a.astype(BF16)
        b_ref[0] = b.astype(BF16)
        sg = (a * _sigmoid(a) * b).astype(BF16)
        acc[...] += _dot(sg, wd_ref[0])

        @pl.when(s == N_SHARD - 1)
        def _():
            ho_ref[...] = h_ref[...] + 0.5 * acc[...]

    return pl.pallas_call(
        body, name="ffn_fwd",
        grid=(nt, N_SHARD),
        in_specs=[
            pl.BlockSpec((tm, d), lambda i, s: (i, 0)),
            pl.BlockSpec((1, d), lambda i, s: (0, 0)),
            pl.BlockSpec((1, d, FF_SHARD), lambda i, s: (s, layer, 0)),
            pl.BlockSpec((1, d, FF_SHARD), lambda i, s: (s, layer, 0)),
            pl.BlockSpec((1, FF_SHARD, d), lambda i, s: (s, layer, 0)),
        ],
        out_specs=[
            pl.BlockSpec((tm, d), lambda i, s: (i, 0)),
            pl.BlockSpec((1, tm, FF_SHARD), lambda i, s: (s, i, 0)),
            pl.BlockSpec((1, tm, FF_SHARD), lambda i, s: (s, i, 0)),
            pl.BlockSpec((tm, d), lambda i, s: (i, 0)),
        ],
        out_shape=[
            jax.ShapeDtypeStruct((tp, d), F32),
            jax.ShapeDtypeStruct((N_SHARD, tp, FF_SHARD), BF16),
            jax.ShapeDtypeStruct((N_SHARD, tp, FF_SHARD), BF16),
            jax.ShapeDtypeStruct((tp, d), BF16),
        ],
        scratch_shapes=[pltpu.VMEM((tm, d), BF16), pltpu.VMEM((tm, d), F32)],
        compiler_params=_cparams(2),
    )(h, norm, wg4, wu4, wd4)


def _ffn_bwd_act(dho, h, norm, a4, b4, wg4, wu4, wd4, layer, tm):
    tp, d = h.shape
    nt = tp // tm

    def body(dho_ref, h_ref, g_ref, a_ref, b_ref, wg_ref, wu_ref, wd_ref,
             dh_ref, da_ref, db_ref, dg_ref, dy_sc, dn_sc):
        i = pl.program_id(0)
        s = pl.program_id(1)

        @pl.when(s == 0)
        def _():
            dy_sc[...] = (0.5 * dho_ref[...]).astype(BF16)
            dn_sc[...] = jnp.zeros_like(dn_sc)

        @pl.when((s == 0) & (i == 0))
        def _():
            dg_ref[...] = jnp.zeros_like(dg_ref)

        ds = _dot_nt(dy_sc[...], wd_ref[0])
        a = a_ref[0].astype(F32)
        b = b_ref[0].astype(F32)
        sig = _sigmoid(a)
        da = (ds * b * (sig * (1.0 + a * (1.0 - sig)))).astype(BF16)
        db = (ds * (a * sig)).astype(BF16)
        da_ref[0] = da
        db_ref[0] = db
        dn_sc[...] += _dot_nt(da, wg_ref[0]) + _dot_nt(db, wu_ref[0])

        @pl.when(s == N_SHARD - 1)
        def _():
            rstd, xhat = _rms_stats(h_ref[...])
            dh, dg = _rms_bwd(dn_sc[...], g_ref[...], rstd, xhat)
            dh_ref[...] = dho_ref[...] + dh
            dg_ref[...] += dg

    return pl.pallas_call(
        body, name="ffn_bwd_act",
        grid=(nt, N_SHARD),
        in_specs=[
            pl.BlockSpec((tm, d), lambda i, s: (i, 0)),
            pl.BlockSpec((tm, d), lambda i, s: (i, 0)),
            pl.BlockSpec((1, d), lambda i, s: (0, 0)),
            pl.BlockSpec((1, tm, FF_SHARD), lambda i, s: (s, i, 0)),
            pl.BlockSpec((1, tm, FF_SHARD), lambda i, s: (s, i, 0)),
            pl.BlockSpec((1, d, FF_SHARD), lambda i, s: (s, layer, 0)),
            pl.BlockSpec((1, d, FF_SHARD), lambda i, s: (s, layer, 0)),
            pl.BlockSpec((1, FF_SHARD, d), lambda i, s: (s, layer, 0)),
        ],
        out_specs=[
            pl.BlockSpec((tm, d), lambda i, s: (i, 0)),
            pl.BlockSpec((1, tm, FF_SHARD), lambda i, s: (s, i, 0)),
            pl.BlockSpec((1, tm, FF_SHARD), lambda i, s: (s, i, 0)),
            pl.BlockSpec((1, d), lambda i, s: (0, 0)),
        ],
        out_shape=[
            jax.ShapeDtypeStruct((tp, d), F32),
            jax.ShapeDtypeStruct((N_SHARD, tp, FF_SHARD), BF16),
            jax.ShapeDtypeStruct((N_SHARD, tp, FF_SHARD), BF16),
            jax.ShapeDtypeStruct((1, d), F32),
        ],
        scratch_shapes=[pltpu.VMEM((tm, d), BF16), pltpu.VMEM((tm, d), F32)],
        compiler_params=_cparams(2),
    )(dho, h, norm, a4, b4, wg4, wu4, wd4)


def _ffn_bwd_w(dho, n, a4, b4, da4, db4, gwg, gwu, gwd, layer, tm):
    tp, d = n.shape
    nt = tp // tm
    first = gwg is None

    def body(*refs):
        if first:
            dho_ref, n_ref, a_ref, b_ref, da_ref, db_ref = refs[:6]
            og_ref, ou_ref, od_ref, accg, accu, accd = refs[6:]
        else:
            dho_ref, n_ref, a_ref, b_ref, da_ref, db_ref = refs[:6]
            og_ref, ou_ref, od_ref, accg, accu, accd = refs[9:]
        i = pl.program_id(1)

        @pl.when(i == 0)
        def _():
            accg[...] = jnp.zeros_like(accg)
            accu[...] = jnp.zeros_like(accu)
            accd[...] = jnp.zeros_like(accd)

        nn = n_ref[...]
        accg[...] += _dot_tn(nn, da_ref[0])
        accu[...] += _dot_tn(nn, db_ref[0])
        a = a_ref[0].astype(F32)
        b = b_ref[0].astype(F32)
        sact = (a * _sigmoid(a) * b).astype(BF16)
        dy = (0.5 * dho_ref[...]).astype(BF16)
        accd[...] += _dot_tn(sact, dy)

        @pl.when(i == nt - 1)
        def _():
            og_ref[0] = accg[...].astype(BF16)
            ou_ref[0] = accu[...].astype(BF16)
            od_ref[0] = accd[...].astype(BF16)

    in_specs = [
        pl.BlockSpec((tm, d), lambda s, i: (i, 0)),
        pl.BlockSpec((tm, d), lambda s, i: (i, 0)),
        pl.BlockSpec((1, tm, FF_SHARD), lambda s, i: (s, i, 0)),
        pl.BlockSpec((1, tm, FF_SHARD), lambda s, i: (s, i, 0)),
        pl.BlockSpec((1, tm, FF_SHARD), lambda s, i: (s, i, 0)),
        pl.BlockSpec((1, tm, FF_SHARD), lambda s, i: (s, i, 0)),
    ]
    args = [dho, n, a4, b4, da4, db4]
    aliases = {}
    if not first:
        in_specs += [pl.BlockSpec(memory_space=pl.ANY)] * 3
        args += [gwg, gwu, gwd]
        aliases = {6: 0, 7: 1, 8: 2}
    return pl.pallas_call(
        body, name="ffn_bwd_w",
        grid=(N_SHARD, nt),
        in_specs=in_specs,
        out_specs=[
            pl.BlockSpec((1, d, FF_SHARD), lambda s, i: (s, layer, 0)),
            pl.BlockSpec((1, d, FF_SHARD), lambda s, i: (s, layer, 0)),
            pl.BlockSpec((1, FF_SHARD, d), lambda s, i: (s, layer, 0)),
        ],
        out_shape=[
            jax.ShapeDtypeStruct((N_SHARD, 2 * d, FF_SHARD), BF16),
            jax.ShapeDtypeStruct((N_SHARD, 2 * d, FF_SHARD), BF16),
            jax.ShapeDtypeStruct((N_SHARD, 2 * FF_SHARD, d), BF16),
        ],
        scratch_shapes=[pltpu.VMEM((d, FF_SHARD), F32), pltpu.VMEM((d, FF_SHARD), F32),
                        pltpu.VMEM((FF_SHARD, d), F32)],
        input_output_aliases=aliases,
        compiler_params=_cparams(2),
    )(*args)


def _proj_fwd(h, norm, layer, w_pieces, tm):
    tp, d = h.shape
    widths = [bs[-1] for _, bs, _ in w_pieces]
    ntot = sum(widths)
    npc = len(w_pieces)

    def body(*refs):
        h_ref, g_ref = refs[:2]
        w_refs = refs[2:2 + npc]
        p_ref, n_ref = refs[2 + npc:]
        rstd, xhat = _rms_stats(h_ref[...])
        n = (xhat * g_ref[...]).astype(BF16)
        n_ref[...] = n
        off = 0
        for k in range(npc):
            w = w_refs[k][...]
            w = w.reshape(w.shape[-2], w.shape[-1])
            p_ref[:, off:off + widths[k]] = _dot(n, w)
            off += widths[k]

    in_specs = [pl.BlockSpec((tm, d), lambda i: (i, 0)), pl.BlockSpec((1, d), lambda i: (0, 0))]
    for _, bs, idx in w_pieces:
        in_specs.append(pl.BlockSpec(bs, functools.partial(lambda i, idx: idx, idx=idx)))
    return pl.pallas_call(
        body, name="proj_fwd",
        grid=(tp // tm,),
        in_specs=in_specs,
        out_specs=[pl.BlockSpec((tm, ntot), lambda i: (i, 0)), pl.BlockSpec((tm, d), lambda i: (i, 0))],
        out_shape=[jax.ShapeDtypeStruct((tp, ntot), F32), jax.ShapeDtypeStruct((tp, d), BF16)],
        compiler_params=_cparams(1),
    )(h, norm, *[w for w, _, _ in w_pieces])


def _proj_bwd_act(dres, h, norm, layer, dp_pieces, w_pieces, tm):
    tp, d = h.shape
    npc = len(w_pieces)

    def body(*refs):
        dres_ref, h_ref, g_ref = refs[:3]
        dp_refs = refs[3:3 + npc]
        w_refs = refs[3 + npc:3 + 2 * npc]
        dh_ref, dg_ref = refs[3 + 2 * npc:]
        i = pl.program_id(0)

        @pl.when(i == 0)
        def _():
            dg_ref[...] = jnp.zeros_like(dg_ref)

        dn = None
        for k in range(npc):
            w = w_refs[k][...]
            w = w.reshape(w.shape[-2], w.shape[-1])
            t = _dot_nt(dp_refs[k][...], w)
            dn = t if dn is None else dn + t
        rstd, xhat = _rms_stats(h_ref[...])
        dh, dg = _rms_bwd(dn, g_ref[...], rstd, xhat)
        dh_ref[...] = dres_ref[...] + dh
        dg_ref[...] += dg

    in_specs = [pl.BlockSpec((tm, d), lambda i: (i, 0)), pl.BlockSpec((tm, d), lambda i: (i, 0)),
                pl.BlockSpec((1, d), lambda i: (0, 0))]
    for dp in dp_pieces:
        in_specs.append(pl.BlockSpec((tm, dp.shape[1]), lambda i: (i, 0)))
    for _, bs, idx in w_pieces:
        in_specs.append(pl.BlockSpec(bs, functools.partial(lambda i, idx: idx, idx=idx)))
    return pl.pallas_call(
        body, name="proj_bwd_act",
        grid=(tp // tm,),
        in_specs=in_specs,
        out_specs=[pl.BlockSpec((tm, d), lambda i: (i, 0)), pl.BlockSpec((1, d), lambda i: (0, 0))],
        out_shape=[jax.ShapeDtypeStruct((tp, d), F32), jax.ShapeDtypeStruct((1, d), F32)],
        compiler_params=_cparams(1),
    )(dres, h, norm, *dp_pieces, *[w for w, _, _ in w_pieces])


def _proj_bwd_w(n, dp_pieces, tm):
    tp, d = n.shape
    npc = len(dp_pieces)
    widths = [dp.shape[1] for dp in dp_pieces]

    def body(*refs):
        n_ref = refs[0]
        dp_refs = refs[1:1 + npc]
        o_refs = refs[1 + npc:1 + 2 * npc]
        accs = refs[1 + 2 * npc:]
        i = pl.program_id(0)

        @pl.when(i == 0)
        def _():
            for acc in accs:
                acc[...] = jnp.zeros_like(acc)

        nn = n_ref[...]
        for k in range(npc):
            accs[k][...] += _dot_tn(nn, dp_refs[k][...])

        @pl.when(i == pl.num_programs(0) - 1)
        def _():
            for k in range(npc):
                o_refs[k][...] = accs[k][...].astype(BF16)

    return pl.pallas_call(
        body, name="proj_bwd_w",
        grid=(tp // tm,),
        in_specs=[pl.BlockSpec((tm, d), lambda i: (i, 0))]
        + [pl.BlockSpec((tm, w), lambda i: (i, 0)) for w in widths],
        out_specs=[pl.BlockSpec((d, w), lambda i: (0, 0)) for w in widths],
        out_shape=[jax.ShapeDtypeStruct((d, w), BF16) for w in widths],
        scratch_shapes=[pltpu.VMEM((d, w), F32) for w in widths],
        compiler_params=_cparams(1),
    )(n, *dp_pieces)


def _out_fwd(h, ya, yb, w, tm):
    tp, d = h.shape
    na, nb = ya.shape[1], yb.shape[1]

    def body(h_ref, ya_ref, yb_ref, w_ref, o_ref):
        y = _dot(ya_ref[...].astype(BF16), w_ref[0:na, :]) + _dot(yb_ref[...].astype(BF16), w_ref[na:, :])
        o_ref[...] = h_ref[...] + y

    return pl.pallas_call(
        body, name="out_fwd",
        grid=(tp // tm,),
        in_specs=[pl.BlockSpec((tm, d), lambda i: (i, 0)), pl.BlockSpec((tm, na), lambda i: (i, 0)),
                  pl.BlockSpec((tm, nb), lambda i: (i, 0)), pl.BlockSpec((d, d), lambda i: (0, 0))],
        out_specs=pl.BlockSpec((tm, d), lambda i: (i, 0)),
        out_shape=jax.ShapeDtypeStruct((tp, d), F32),
        compiler_params=_cparams(1),
    )(h, ya, yb, w)


def _out_bwd(dy, ya, yb, w, tm):
    tp, d = dy.shape
    na, nb = ya.shape[1], yb.shape[1]

    def body(dy_ref, ya_ref, yb_ref, w_ref, da_ref, db_ref, dw_ref, acc):
        i = pl.program_id(0)

        @pl.when(i == 0)
        def _():
            acc[...] = jnp.zeros_like(acc)

        dyb16 = dy_ref[...].astype(BF16)
        da_ref[...] = _dot_nt(dyb16, w_ref[0:na, :])
        db_ref[...] = _dot_nt(dyb16, w_ref[na:, :])
        acc[0:na, :] += _dot_tn(ya_ref[...].astype(BF16), dyb16)
        acc[na:, :] += _dot_tn(yb_ref[...].astype(BF16), dyb16)

        @pl.when(i == pl.num_programs(0) - 1)
        def _():
            dw_ref[...] = acc[...].astype(BF16)

    return pl.pallas_call(
        body, name="out_bwd",
        grid=(tp // tm,),
        in_specs=[pl.BlockSpec((tm, d), lambda i: (i, 0)), pl.BlockSpec((tm, na), lambda i: (i, 0)),
                  pl.BlockSpec((tm, nb), lambda i: (i, 0)), pl.BlockSpec((d, d), lambda i: (0, 0))],
        out_specs=[pl.BlockSpec((tm, na), lambda i: (i, 0)), pl.BlockSpec((tm, nb), lambda i: (i, 0)),
                   pl.BlockSpec((d, d), lambda i: (0, 0))],
        out_shape=[jax.ShapeDtypeStruct((tp, na), F32), jax.ShapeDtypeStruct((tp, nb), F32),
                   jax.ShapeDtypeStruct((d, d), BF16)],
        scratch_shapes=[pltpu.VMEM((d, d), F32)],
        compiler_params=_cparams(1),
    )(dy, ya, yb, w)


def _loss_bwd(h, gfin, tgt, t_real, tm):
    tp, d = h.shape

    def body(h_ref, g_ref, t_ref, loss_ref, dh_ref, dg_ref):
        i = pl.program_id(0)

        @pl.when(i == 0)
        def _():
            loss_ref[...] = jnp.zeros_like(loss_ref)
            dg_ref[...] = jnp.zeros_like(dg_ref)

        rows = i * tm + lax.broadcasted_iota(jnp.int32, (tm, 1), 0)
        valid = (rows >= N_META) & (rows < t_real)
        rstd, xhat = _rms_stats(h_ref[...])
        g = g_ref[...]
        err = jnp.where(valid, xhat * g - t_ref[...], 0.0)
        e2 = jnp.sum(err * err, axis=1, keepdims=True)
        loss_ref[...] += (0.5 / d) * jnp.sum(e2, axis=0, keepdims=True)
        dy = err * (1.0 / d)
        dh, dg = _rms_bwd(dy, g, rstd, xhat)
        dh_ref[...] = dh
        dg_ref[...] += dg

    return pl.pallas_call(
        body, name="loss_bwd",
        grid=(tp // tm,),
        in_specs=[pl.BlockSpec((tm, d), lambda i: (i, 0)), pl.BlockSpec((1, d), lambda i: (0, 0)),
                  pl.BlockSpec((tm, d), lambda i: (i, 0))],
        out_specs=[pl.BlockSpec((1, 1), lambda i: (0, 0)), pl.BlockSpec((tm, d), lambda i: (i, 0)),
                   pl.BlockSpec((1, d), lambda i: (0, 0))],
        out_shape=[jax.ShapeDtypeStruct((1, 1), F32), jax.ShapeDtypeStruct((tp, d), F32),
                   jax.ShapeDtypeStruct((1, d), F32)],
        compiler_params=_cparams(1),
    )(h, gfin, tgt)


POOL_HALO = 16


def _pool_lane_consts(n_rows):
    lane = lax.broadcasted_iota(jnp.int32, (n_rows, D_POOL), 1)
    grp = lane // POOL_GROUP
    win = jnp.where(grp == 0, 2.0, jnp.where(grp == 1, 4.0, jnp.where(grp == 2, 8.0, 16.0)))
    return grp, win


def _pool_select(grp, s2, s4, s8, s16):
    return jnp.where(grp == 0, s2, jnp.where(grp == 1, s4, jnp.where(grp == 2, s8, s16)))


def _pool_mixed(x, row0, tr):
    n = tr + POOL_HALO
    s2 = x + pltpu.roll(x, 1, 0)
    s4 = s2 + pltpu.roll(s2, 2, 0)
    s8 = s4 + pltpu.roll(s4, 4, 0)
    s16 = s8 + pltpu.roll(s8, 8, 0)
    grp, win = _pool_lane_consts(n)
    rows = row0 - POOL_HALO + lax.broadcasted_iota(jnp.int32, (n, D_POOL), 0)
    cnt = jnp.minimum((rows + 1).astype(F32), win)
    pooled = _pool_select(grp, s2, s4, s8, s16) / jnp.maximum(cnt, 1.0)
    return (pooled - x)[POOL_HALO:, :]


def _pool_fwd(p, wbd, scale, tr):
    tp = p.shape[0]
    nt = tp // tr

    def body(p_ref, w_ref, s_ref, y_ref, usc):
        usc[0:POOL_HALO, :] = jnp.zeros((POOL_HALO, D_POOL), F32)
        usc[POOL_HALO:, :] = p_ref[...]

        def tile(r, carry):
            r0 = pl.multiple_of(r * tr, SUBLANE)
            x = usc[pl.ds(r0, tr + POOL_HALO), :]
            mixed = _pool_mixed(x, r0, tr)
            y_ref[pl.ds(r0, tr), :] = _dot(mixed.astype(BF16), w_ref[...]) * s_ref[...]
            return carry

        lax.fori_loop(0, nt, tile, 0)

    return pl.pallas_call(
        body, name="pool_fwd",
        grid=(1,),
        in_specs=[pl.BlockSpec((tp, D_POOL), lambda i: (0, 0)), pl.BlockSpec((D_POOL, D_POOL), lambda i: (0, 0)),
                  pl.BlockSpec((1, D_POOL), lambda i: (0, 0))],
        out_specs=pl.BlockSpec((tp, D_POOL), lambda i: (0, 0)),
        out_shape=jax.ShapeDtypeStruct((tp, D_POOL), F32),
        scratch_shapes=[pltpu.VMEM((tp + POOL_HALO, D_POOL), F32)],
        compiler_params=_cparams(1),
    )(p, wbd, scale)


def _pool_bwd(p, dya, wbd, scale, tr):
    tp = p.shape[0]
    nt = tp // tr

    def body(p_ref, dy_ref, w_ref, s_ref, du_ref, dw_ref, ds_ref, usc, gsc):
        usc[0:POOL_HALO, :] = jnp.zeros((POOL_HALO, D_POOL), F32)
        usc[POOL_HALO:, :] = p_ref[...]
        gsc[tp:, :] = jnp.zeros((POOL_HALO, D_POOL), F32)
        dw_ref[...] = jnp.zeros_like(dw_ref)
        ds_ref[...] = jnp.zeros_like(ds_ref)
        grp, win = _pool_lane_consts(tr)

        def tile1(r, carry):
            r0 = pl.multiple_of(r * tr, SUBLANE)
            x = usc[pl.ds(r0, tr + POOL_HALO), :]
            mixed = _pool_mixed(x, r0, tr).astype(BF16)
            dy = dy_ref[pl.ds(r0, tr), :]
            dys = (dy * s_ref[...]).astype(BF16)
            ypre = _dot(mixed, w_ref[...])
            ds_ref[...] += _colsum(dy * ypre)
            dw_ref[...] += _dot_tn(mixed, dys)
            dmx = _dot_nt(dys, w_ref[...])
            rows = r0 + lax.broadcasted_iota(jnp.int32, (tr, D_POOL), 0)
            cnt = jnp.minimum((rows + 1).astype(F32), win)
            gsc[pl.ds(r0, tr), :] = dmx / cnt
            return carry

        lax.fori_loop(0, nt, tile1, 0)
        n = tr + POOL_HALO
        grp2, win2 = _pool_lane_consts(n)

        def tile2(r, carry):
            r0 = pl.multiple_of(r * tr, SUBLANE)
            g = gsc[pl.ds(r0, n), :]
            s2 = g + pltpu.roll(g, n - 1, 0)
            s4 = s2 + pltpu.roll(s2, n - 2, 0)
            s8 = s4 + pltpu.roll(s4, n - 4, 0)
            s16 = s8 + pltpu.roll(s8, n - 8, 0)
            pooled_t = _pool_select(grp2, s2, s4, s8, s16)
            rows = r0 + lax.broadcasted_iota(jnp.int32, (n, D_POOL), 0)
            cnt = jnp.minimum((rows + 1).astype(F32), win2)
            du = pooled_t - g * cnt
            du_ref[pl.ds(r0, tr), :] = du[0:tr, :].astype(BF16)
            return carry

        lax.fori_loop(0, nt, tile2, 0)

    return pl.pallas_call(
        body, name="pool_bwd",
        grid=(1,),
        in_specs=[pl.BlockSpec((tp, D_POOL), lambda i: (0, 0)), pl.BlockSpec((tp, D_POOL), lambda i: (0, 0)),
                  pl.BlockSpec((D_POOL, D_POOL), lambda i: (0, 0)), pl.BlockSpec((1, D_POOL), lambda i: (0, 0))],
        out_specs=[pl.BlockSpec((tp, D_POOL), lambda i: (0, 0)), pl.BlockSpec((D_POOL, D_POOL), lambda i: (0, 0)),
                   pl.BlockSpec((1, D_POOL), lambda i: (0, 0))],
        out_shape=[jax.ShapeDtypeStruct((tp, D_POOL), BF16), jax.ShapeDtypeStruct((D_POOL, D_POOL), F32),
                   jax.ShapeDtypeStruct((1, D_POOL), F32)],
        scratch_shapes=[pltpu.VMEM((tp + POOL_HALO, D_POOL), F32), pltpu.VMEM((tp + POOL_HALO, D_POOL), F32)],
        compiler_params=_cparams(1),
    )(p, dya, wbd, scale)


def _hgrn_consts(ch):
    t = np.arange(ch)
    levels = []
    w = ch // 2
    while w >= 1:
        levels.append(w)
        w //= 2
    tril = t[None, :] <= t[:, None]
    to_end = t[None, :] > t[:, None]
    mats = [tril, to_end]
    masks = []
    for w in levels:
        pos = t % (2 * w)
        blk = t // (2 * w)
        upper = pos >= w
        mid = blk * 2 * w + w - 1
        mats.append(upper[:, None] & (t[None, :] > mid[:, None]) & (t[None, :] <= t[:, None]))
        mats.append((~upper)[:, None] & (t[None, :] > t[:, None]) & (t[None, :] <= mid[:, None]))
        masks.append(upper[:, None] & (~upper)[None, :] & (blk[:, None] == blk[None, :]))
    masks.append(tril)
    mst = np.concatenate(mats, axis=0).astype(np.float32)
    msk = np.stack(masks).astype(np.float32)
    return jnp.asarray(mst, BF16), jnp.asarray(msk, F32), len(levels)


def _split3(x):
    hi = x.astype(BF16)
    r1 = x - hi.astype(F32)
    mid = r1.astype(BF16)
    lo = (r1 - mid.astype(F32)).astype(BF16)
    return hi, mid, lo


def _hgrn_exponents(mst, logf):
    hi, mid, lo = _split3(logf)
    x = _dot(mst, jnp.concatenate([hi, mid, lo], axis=1))
    return x[:, 0:HEAD] + x[:, HEAD:2 * HEAD] + x[:, 2 * HEAD:3 * HEAD]


def _hgrn_gates(q_raw, z, lb):
    sz = _sigmoid(z)
    f = lb + (1.0 - lb) * sz
    q = q_raw * _sigmoid(q_raw)
    k = (1.0 - lb) * (1.0 - sz)
    return q, k, f, sz


def _hgrn_intra(q, k, x, msk_ref, n_lev, ch):
    a = msk_ref[n_lev] * 0.0
    eye = (lax.broadcasted_iota(jnp.int32, (ch, ch), 0) == lax.broadcasted_iota(jnp.int32, (ch, ch), 1))
    a = jnp.where(eye, jnp.sum(q * k, axis=1, keepdims=True), 0.0)
    ops = []
    for lv in range(n_lev):
        eq = jnp.exp(x[(2 + 2 * lv) * ch:(3 + 2 * lv) * ch, :])
        ek = jnp.exp(x[(3 + 2 * lv) * ch:(4 + 2 * lv) * ch, :])
        qd = q * eq
        kd = k * ek
        a = a + msk_ref[lv] * _dot_nt(qd.astype(BF16), kd.astype(BF16))
        ops.append((eq, ek, qd, kd))
    return a, ops


def _hgrn_fwd(p, lb_logits, gnorm, mst, msk, n_lev, tm):
    tp = p.shape[0]
    ch = HG_CHUNK
    nct = tm // ch
    nt = tp // tm
    nr = mst.shape[0]
    base = D_POOL // HEAD

    def body(q_ref, z_ref, v_ref, g_ref, lg_ref, gn_ref, mst_ref, msk_ref, y_ref, ss_ref, st_sc):
        @pl.when(pl.program_id(1) == 0)
        def _():
            st_sc[...] = jnp.zeros_like(st_sc)

        lb = _sigmoid(lg_ref[0:1, :] - lg_ref[1:2, :])

        def chunk(c, carry):
            r0 = pl.multiple_of(c * ch, ch)
            q, k, f, _ = _hgrn_gates(q_ref[pl.ds(r0, ch), :], z_ref[pl.ds(r0, ch), :], lb)
            v = v_ref[pl.ds(r0, ch), :]
            x = _hgrn_exponents(mst_ref[...], jnp.log(f))
            st = st_sc[...]
            ss_ref[0, c] = st
            qe = q * jnp.exp(x[0:ch, :])
            a, _ = _hgrn_intra(q, k, x, msk_ref, n_lev, ch)
            v16 = v.astype(BF16)
            o = _dot_nt(qe.astype(BF16), st.astype(BF16)) + _dot(a.astype(BF16), v16)
            kl = k * jnp.exp(x[ch:2 * ch, :])
            st_sc[...] = st * jnp.exp(x[ch - 1:ch, :]) + _dot_tn(v16, kl.astype(BF16))
            rstd = lax.rsqrt(jnp.mean(o * o, axis=-1, keepdims=True) + EPS)
            g_raw = g_ref[pl.ds(r0, ch), :]
            y_ref[pl.ds(r0, ch), :] = o * rstd * gn_ref[...] * (g_raw * _sigmoid(g_raw))
            return carry

        lax.fori_loop(0, nct, chunk, 0)

    def pspec(seg):
        return pl.BlockSpec((tm, HEAD), lambda h, i: (i, base + seg * HG_HEADS + h))

    return pl.pallas_call(
        body, name="hgrn_fwd",
        grid=(HG_HEADS, nt),
        in_specs=[pspec(0), pspec(1), pspec(2), pspec(3),
                  pl.BlockSpec((2, HEAD), lambda h, i: (0, h)),
                  pl.BlockSpec((1, HEAD), lambda h, i: (0, 0)),
                  pl.BlockSpec((nr, ch), lambda h, i: (0, 0)),
                  pl.BlockSpec((n_lev + 1, ch, ch), lambda h, i: (0, 0, 0))],
        out_specs=[pl.BlockSpec((tm, HEAD), lambda h, i: (i, h)),
                   pl.BlockSpec((1, nct, HEAD, HEAD), lambda h, i: (h, i, 0, 0))],
        out_shape=[jax.ShapeDtypeStruct((tp, D_HGRN), F32),
                   jax.ShapeDtypeStruct((HG_HEADS, tp // ch, HEAD, HEAD), F32)],
        scratch_shapes=[pltpu.VMEM((HEAD, HEAD), F32)],
        compiler_params=_cparams(2),
    )(p, p, p, p, lb_logits, gnorm, mst, msk)


def _hgrn_bwd(p, dyb, states, lb_logits, gnorm, mst, msk, n_lev, tm):
    tp = p.shape[0]
    ch = HG_CHUNK
    nct = tm // ch
    nt = tp // tm
    nr = mst.shape[0]
    base = D_POOL // HEAD

    def body(q_ref, z_ref, v_ref, g_ref, dy_ref, ss_ref, lg_ref, gn_ref, mst_ref, msk_ref,
             dq_ref, dz_ref, dv_ref, dg_ref, dlg_ref, dgn_ref, dst_sc, dlb_sc, stk):
        ti = pl.program_id(1)

        @pl.when(ti == 0)
        def _():
            dst_sc[...] = jnp.zeros_like(dst_sc)
            dlb_sc[...] = jnp.zeros_like(dlb_sc)
            dgn_ref[...] = jnp.zeros_like(dgn_ref)

        lb = _sigmoid(lg_ref[0:1, :] - lg_ref[1:2, :])
        gn = gn_ref[...]

        def chunk(cc, carry):
            c = nct - 1 - cc
            r0 = pl.multiple_of(c * ch, ch)
            q_raw = q_ref[pl.ds(r0, ch), :]
            z = z_ref[pl.ds(r0, ch), :]
            q, k, f, sz = _hgrn_gates(q_raw, z, lb)
            v = v_ref[pl.ds(r0, ch), :]
            x = _hgrn_exponents(mst_ref[...], jnp.log(f))
            st = ss_ref[0, c]
            eb = jnp.exp(x[0:ch, :])
            ef = jnp.exp(x[ch:2 * ch, :])
            elast = jnp.exp(x[ch - 1:ch, :])
            qe = q * eb
            kl = k * ef
            a, ops = _hgrn_intra(q, k, x, msk_ref, n_lev, ch)
            v16 = v.astype(BF16)
            st16 = st.astype(BF16)
            qe16 = qe.astype(BF16)
            kl16 = kl.astype(BF16)
            a16 = a.astype(BF16)
            o = _dot_nt(qe16, st16) + _dot(a16, v16)
            g_raw = g_ref[pl.ds(r0, ch), :]
            sg = _sigmoid(g_raw)
            rstd = lax.rsqrt(jnp.mean(o * o, axis=-1, keepdims=True) + EPS)
            oh = o * rstd
            dy = dy_ref[pl.ds(r0, ch), :]
            dg_ref[pl.ds(r0, ch), :] = (dy * oh * gn * (sg * (1.0 + g_raw * (1.0 - sg)))).astype(BF16)
            don = dy * (g_raw * sg)
            dgn_ref[0] += _colsum(don * oh)
            doh = don * gn
            do = rstd * (doh - oh * jnp.mean(doh * oh, axis=-1, keepdims=True))
            do16 = do.astype(BF16)
            dst = dst_sc[...]
            dst16 = dst.astype(BF16)
            dv = _dot_tn(a16, do16) + _dot_nt(kl16, dst16)
            da = msk_ref[n_lev] * _dot_nt(do16, v16)
            dqe = _dot(do16, st16)
            dkl = _dot(v16, dst16)
            dst_sc[...] = dst * elast + _dot_tn(do16, qe16)
            db_last = _colsum(dst * st) * elast
            dad = jnp.sum(do * v, axis=1, keepdims=True)
            dq = dad * k + dqe * eb
            dk = dad * q + dkl * ef
            stk[0:ch, :] = dqe * qe
            stk[ch:2 * ch, :] = dkl * kl
            for lv in range(n_lev):
                eq, ek, qd, kd = ops[lv]
                gl = (msk_ref[lv] * da).astype(BF16)
                dqd = _dot(gl, kd.astype(BF16))
                dkd = _dot_tn(gl, qd.astype(BF16))
                dq = dq + dqd * eq
                dk = dk + dkd * ek
                stk[(2 + 2 * lv) * ch:(3 + 2 * lv) * ch, :] = dqd * qd
                stk[(3 + 2 * lv) * ch:(4 + 2 * lv) * ch, :] = dkd * kd
            sk = stk[...]
            hi = sk.astype(BF16)
            lo = (sk - hi.astype(F32)).astype(BF16)
            dl2 = _dot_tn(mst_ref[...], jnp.concatenate([hi, lo], axis=1))
            dlogf = dl2[:, 0:HEAD] + dl2[:, HEAD:2 * HEAD] + db_last
            sq = _sigmoid(q_raw)
            dq_ref[pl.ds(r0, ch), :] = (dq * (sq * (1.0 + q_raw * (1.0 - sq)))).astype(BF16)
            dfk = dlogf / f - dk
            dz_ref[pl.ds(r0, ch), :] = (dfk * (1.0 - lb) * sz * (1.0 - sz)).astype(BF16)
            dlb_sc[...] += _colsum(dfk * (1.0 - sz))
            dv_ref[pl.ds(r0, ch), :] = dv.astype(BF16)
            return carry

        lax.fori_loop(0, nct, chunk, 0)

        @pl.when(ti == nt - 1)
        def _():
            dl0 = dlb_sc[...] * lb * (1.0 - lb)
            dlg_ref[0:1, :] = dl0
            dlg_ref[1:2, :] = -dl0

    def pspec(seg):
        return pl.BlockSpec((tm, HEAD), lambda h, i: (nt - 1 - i, base + seg * HG_HEADS + h))

    ospec = pl.BlockSpec((tm, HEAD), lambda h, i: (nt - 1 - i, h))
    return pl.pallas_call(
        body, name="hgrn_bwd",
        grid=(HG_HEADS, nt),
        in_specs=[pspec(0), pspec(1), pspec(2), pspec(3), ospec,
                  pl.BlockSpec((1, nct, HEAD, HEAD), lambda h, i: (h, nt - 1 - i, 0, 0)),
                  pl.BlockSpec((2, HEAD), lambda h, i: (0, h)),
                  pl.BlockSpec((1, HEAD), lambda h, i: (0, 0)),
                  pl.BlockSpec((nr, ch), lambda h, i: (0, 0)),
                  pl.BlockSpec((n_lev + 1, ch, ch), lambda h, i: (0, 0, 0))],
        out_specs=[ospec, ospec, ospec, ospec,
                   pl.BlockSpec((2, HEAD), lambda h, i: (0, h)),
                   pl.BlockSpec((1, 1, HEAD), lambda h, i: (h, 0, 0))],
        out_shape=[jax.ShapeDtypeStruct((tp, D_HGRN), BF16)] * 4
        + [jax.ShapeDtypeStruct((2, D_HGRN), F32), jax.ShapeDtypeStruct((HG_HEADS, 1, HEAD), F32)],
        scratch_shapes=[pltpu.VMEM((HEAD, HEAD), F32), pltpu.VMEM((1, HEAD), F32), pltpu.VMEM((nr, HEAD), F32)],
        compiler_params=_cparams(2),
    )(p, p, p, p, dyb, states, lb_logits, gnorm, mst, msk)


def _conv_taps(x, w_ref, tr, halo, width):
    acc = None
    for j in range(width):
        sh = width - 1 - j
        xs = x if sh == 0 else pltpu.roll(x, sh, 0)
        term = xs[halo:, :] * w_ref[j:j + 1, :]
        acc = term if acc is None else acc + term
    return acc


def _conv_taps_t(y, w_ref, tr, halo, width):
    n = tr + halo
    acc = None
    for j in range(width):
        sh = width - 1 - j
        ys = y if sh == 0 else pltpu.roll(y, n - sh, 0)
        term = ys[0:tr, :] * w_ref[j:j + 1, :]
        acc = term if acc is None else acc + term
    return acc


def _ln_stats(cv):
    mu = jnp.mean(cv, axis=-1, keepdims=True)
    xc = cv - mu
    rstd = lax.rsqrt(jnp.mean(xc * xc, axis=-1, keepdims=True) + EPS)
    return rstd, xc * rstd


def _convmod_fwd(p, w, bias, ln_g, ln_b, tr):
    tp = p.shape[0]
    nt = tp // tr
    nb = D_CONV // HEAD

    def body(a_ref, b_ref, w_ref, bi_ref, g_ref, be_ref, y_ref, usc):
        usc[0:CONV_HALO, :] = jnp.zeros((CONV_HALO, HEAD), F32)
        usc[CONV_HALO:, :] = a_ref[...] * _sigmoid(b_ref[...])

        def tile(r, carry):
            r0 = pl.multiple_of(r * tr, SUBLANE)
            x = usc[pl.ds(r0, tr + CONV_HALO), :]
            cv = _conv_taps(x, w_ref, tr, CONV_HALO, CONV_WIDTH) + bi_ref[...]
            _, xh = _ln_stats(cv)
            un = xh * g_ref[...] + be_ref[...]
            y_ref[pl.ds(r0, tr), :] = un * _sigmoid(un)
            return carry

        lax.fori_loop(0, nt, tile, 0)

    vec = lambda: pl.BlockSpec((1, HEAD), lambda j: (0, j))
    return pl.pallas_call(
        body, name="convmod_fwd",
        grid=(nb,),
        in_specs=[pl.BlockSpec((tp, HEAD), lambda j: (0, j)), pl.BlockSpec((tp, HEAD), lambda j: (0, nb + j)),
                  pl.BlockSpec((CONV_HALO, HEAD), lambda j: (0, j)), vec(), vec(), vec()],
        out_specs=pl.BlockSpec((tp, HEAD), lambda j: (0, j)),
        out_shape=jax.ShapeDtypeStruct((tp, D_CONV), F32),
        scratch_shapes=[pltpu.VMEM((tp + CONV_HALO, HEAD), F32)],
        compiler_params=_cparams(1),
    )(p, p, w, bias, ln_g, ln_b)


def _convmod_bwd(p, dyc, w, bias, ln_g, ln_b, tr):
    tp = p.shape[0]
    nt = tp // tr
    nb = D_CONV // HEAD

    def body(a_ref, b_ref, dy_ref, w_ref, bi_ref, g_ref, be_ref, da_ref, db_ref, dw_ref, dv_ref, usc, dsc):
        usc[0:CONV_HALO, :] = jnp.zeros((CONV_HALO, HEAD), F32)
        usc[CONV_HALO:, :] = a_ref[...] * _sigmoid(b_ref[...])
        dsc[tp:, :] = jnp.zeros((CONV_HALO, HEAD), F32)
        dw_ref[...] = jnp.zeros_like(dw_ref)
        dv_ref[...] = jnp.zeros_like(dv_ref)

        def tile1(r, carry):
            r0 = pl.multiple_of(r * tr, SUBLANE)
            x = usc[pl.ds(r0, tr + CONV_HALO), :]
            cv = _conv_taps(x, w_ref, tr, CONV_HALO, CONV_WIDTH) + bi_ref[...]
            rstd, xh = _ln_stats(cv)
            un = xh * g_ref[...] + be_ref[...]
            sg = _sigmoid(un)
            dun = dy_ref[pl.ds(r0, tr), :] * (sg * (1.0 + un * (1.0 - sg)))
            dv_ref[0, 1:2, :] += _colsum(dun * xh)
            dv_ref[0, 2:3, :] += _colsum(dun)
            dxh = dun * g_ref[...]
            dcv = rstd * (dxh - jnp.mean(dxh, axis=-1, keepdims=True)
                          - xh * jnp.mean(dxh * xh, axis=-1, keepdims=True))
            dv_ref[0, 0:1, :] += _colsum(dcv)
            for j in range(CONV_WIDTH):
                sh = CONV_WIDTH - 1 - j
                xs = x if sh == 0 else pltpu.roll(x, sh, 0)
                dw_ref[0, j:j + 1, :] += _colsum(dcv * xs[CONV_HALO:, :])
            dsc[pl.ds(r0, tr), :] = dcv
            return carry

        lax.fori_loop(0, nt, tile1, 0)

        def tile2(r, carry):
            r0 = pl.multiple_of(r * tr, SUBLANE)
            y = dsc[pl.ds(r0, tr + CONV_HALO), :]
            du = _conv_taps_t(y, w_ref, tr, CONV_HALO, CONV_WIDTH)
            a = a_ref[pl.ds(r0, tr), :]
            sb = _sigmoid(b_ref[pl.ds(r0, tr), :])
            da_ref[pl.ds(r0, tr), :] = (du * sb).astype(BF16)
            db_ref[pl.ds(r0, tr), :] = (du * a * sb * (1.0 - sb)).astype(BF16)
            return carry

        lax.fori_loop(0, nt, tile2, 0)

    vec = lambda: pl.BlockSpec((1, HEAD), lambda j: (0, j))
    col = lambda: pl.BlockSpec((tp, HEAD), lambda j: (0, j))
    return pl.pallas_call(
        body, name="convmod_bwd",
        grid=(nb,),
        in_specs=[col(), pl.BlockSpec((tp, HEAD), lambda j: (0, nb + j)), col(),
                  pl.BlockSpec((CONV_HALO, HEAD), lambda j: (0, j)), vec(), vec(), vec()],
        out_specs=[col(), col(), pl.BlockSpec((1, CONV_HALO, HEAD), lambda j: (j, 0, 0)),
                   pl.BlockSpec((1, SUBLANE, HEAD), lambda j: (j, 0, 0))],
        out_shape=[jax.ShapeDtypeStruct((tp, D_CONV), BF16), jax.ShapeDtypeStruct((tp, D_CONV), BF16),
                   jax.ShapeDtypeStruct((nb, CONV_HALO, HEAD), F32), jax.ShapeDtypeStruct((nb, SUBLANE, HEAD), F32)],
        scratch_shapes=[pltpu.VMEM((tp + CONV_HALO, HEAD), F32), pltpu.VMEM((tp + CONV_HALO, HEAD), F32)],
        compiler_params=_cparams(1),
    )(p, p, dyc, w, bias, ln_g, ln_b)


def _log1p_small(y):
    return jnp.where(y < 1e-4, y * (1.0 - 0.5 * y), jnp.log(1.0 + y))


def _softplus(x):
    return jnp.maximum(x, 0.0) + _log1p_small(jnp.exp(-jnp.abs(x)))


def _expm1(x):
    return jnp.where(jnp.abs(x) < 1e-2, x * (1.0 + 0.5 * x * (1.0 + x * (1.0 / 3.0))), jnp.exp(x) - 1.0)


def _gelu_parts(x):
    c = 0.7978845608028654
    inner = c * (x + 0.044715 * x * x * x)
    th = jnp.tanh(inner)
    gelu = 0.5 * x * (1.0 + th)
    dgelu = 0.5 * (1.0 + th) + 0.5 * x * (1.0 - th * th) * c * (1.0 + 3.0 * 0.044715 * x * x)
    return gelu, dgelu


def _lru_gates(x_all, tp, cw_ref, cb_ref, wa_ref, ba_ref, wx_ref, bx_ref, lam_ref):
    u = _conv_taps(x_all, cw_ref, tp, LRU_HALO, LRU_CONV) + cb_ref[...]
    u16 = u.astype(BF16)
    r = _sigmoid(_dot(u16, wa_ref[0]) + ba_ref[...])
    i = _sigmoid(_dot(u16, wx_ref[0]) + bx_ref[...])
    sp = _softplus(-lam_ref[...])
    la = -LRU_C * r * sp
    a = jnp.exp(la)
    mult = jnp.sqrt(-_expm1(2.0 * la))
    return u, r, i, a, mult, sp


def _lru_specs(tp, nb):
    col = lambda k: pl.BlockSpec((tp, HEAD), functools.partial(lambda j, k: (0, k * nb + j), k=k))
    vec = lambda: pl.BlockSpec((1, HEAD), lambda j: (0, j))
    mat = lambda: pl.BlockSpec((1, HEAD, HEAD), lambda j: (j, 0, 0))
    return col, vec, mat


def _lru_fwd(p, cw, cb, wa, ba, wx, bx, lam):
    tp = p.shape[0]
    nb = D_LRU // HEAD
    ng = tp // SUBLANE

    def body(x_ref, gt_ref, cw_ref, cb_ref, wa_ref, ba_ref, wx_ref, bx_ref, lam_ref, y_ref, hs_ref,
             xsc, asc, bsc):
        xsc[0:LRU_HALO, :] = jnp.zeros((LRU_HALO, HEAD), F32)
        xsc[LRU_HALO:, :] = x_ref[...]
        u, r, i, a, mult, _ = _lru_gates(xsc[...], tp, cw_ref, cb_ref, wa_ref, ba_ref, wx_ref, bx_ref, lam_ref)
        rows = lax.broadcasted_iota(jnp.int32, (tp, HEAD), 0)
        b = jnp.where(rows == 0, 1.0, mult) * (i * u)
        sub = rows % SUBLANE
        for k in (1, 2, 4):
            m = sub >= k
            b = jnp.where(m, a * pltpu.roll(b, k, 0) + b, b)
            a = jnp.where(m, a * pltpu.roll(a, k, 0), a)
        asc[...] = a
        bsc[...] = b

        def grp(g, carry):
            r0 = pl.multiple_of(g * SUBLANE, SUBLANE)
            h = bsc[pl.ds(r0, SUBLANE), :] + asc[pl.ds(r0, SUBLANE), :] * carry
            hs_ref[pl.ds(r0, SUBLANE), :] = h
            return jnp.broadcast_to(h[SUBLANE - 1:SUBLANE, :], (SUBLANE, HEAD))

        lax.fori_loop(0, ng, grp, jnp.zeros((SUBLANE, HEAD), F32))
        gelu, _ = _gelu_parts(gt_ref[...])
        y_ref[...] = gelu * hs_ref[...]

    col, vec, mat = _lru_specs(tp, nb)
    return pl.pallas_call(
        body, name="lru_fwd",
        grid=(nb,),
        in_specs=[col(2), col(3), pl.BlockSpec((LRU_CONV, HEAD), lambda j: (0, j)), vec(), mat(), vec(), mat(),
                  vec(), vec()],
        out_specs=[pl.BlockSpec((tp, HEAD), lambda j: (0, j)), pl.BlockSpec((tp, HEAD), lambda j: (0, j))],
        out_shape=[jax.ShapeDtypeStruct((tp, D_LRU), F32), jax.ShapeDtypeStruct((tp, D_LRU), F32)],
        scratch_shapes=[pltpu.VMEM((tp + LRU_HALO, HEAD), F32), pltpu.VMEM((tp, HEAD), F32),
                        pltpu.VMEM((tp, HEAD), F32)],
        compiler_params=_cparams(1),
    )(p, p, cw, cb, wa, ba, wx, bx, lam)


def _lru_bwd(p, hs, dyd, cw, cb, wa, ba, wx, bx, lam):
    tp = p.shape[0]
    nb = D_LRU // HEAD
    ng = tp // SUBLANE

    def body(x_ref, gt_ref, hs_ref, dy_ref, cw_ref, cb_ref, wa_ref, ba_ref, wx_ref, bx_ref, lam_ref,
             dx_ref, dgt_ref, dwa_ref, dwx_ref, dv_ref, xsc, asc, bsc, gsc, dusc):
        xsc[0:LRU_HALO, :] = jnp.zeros((LRU_HALO, HEAD), F32)
        xsc[LRU_HALO:, :] = x_ref[...]
        x_all = xsc[...]
        u, r, i, a, mult, sp = _lru_gates(x_all, tp, cw_ref, cb_ref, wa_ref, ba_ref, wx_ref, bx_ref, lam_ref)
        rows = lax.broadcasted_iota(jnp.int32, (tp, HEAD), 0)
        hs = hs_ref[...]
        dy = dy_ref[...]
        gelu, dgelu = _gelu_parts(gt_ref[...])
        dgt_ref[...] = (dy * hs * dgelu).astype(BF16)
        bb = dy * gelu
        aa = jnp.where(rows == tp - 1, 0.0, pltpu.roll(a, tp - 1, 0))
        sub = rows % SUBLANE
        for k in (1, 2, 4):
            m = sub < SUBLANE - k
            bb = jnp.where(m, aa * pltpu.roll(bb, tp - k, 0) + bb, bb)
            aa = jnp.where(m, aa * pltpu.roll(aa, tp - k, 0), aa)
        asc[...] = aa
        bsc[...] = bb

        def grp(gi, carry):
            g = ng - 1 - gi
            r0 = pl.multiple_of(g * SUBLANE, SUBLANE)
            gg = bsc[pl.ds(r0, SUBLANE), :] + asc[pl.ds(r0, SUBLANE), :] * carry
            gsc[pl.ds(r0, SUBLANE), :] = gg
            return jnp.broadcast_to(gg[0:1, :], (SUBLANE, HEAD))

        lax.fori_loop(0, ng, grp, jnp.zeros((SUBLANE, HEAD), F32))
        g = gsc[...]
        first = rows == 0
        hprev = jnp.where(first, 0.0, pltpu.roll(hs, 1, 0))
        iu = i * u
        d_iu = g * jnp.where(first, 1.0, mult)
        dmult_term = jnp.where(first, 0.0, g * iu * (-(a * a) / mult))
        dla = g * hprev * a + dmult_term
        dr = dla * (-LRU_C) * sp
        dv_ref[0, 7:8, :] = _colsum(dla * (LRU_C * r) * _sigmoid(-lam_ref[...]))
        dpr = dr * r * (1.0 - r)
        dpi = d_iu * u * i * (1.0 - i)
        dv_ref[0, 5:6, :] = _colsum(dpr)
        dv_ref[0, 6:7, :] = _colsum(dpi)
        u16 = u.astype(BF16)
        dpr16 = dpr.astype(BF16)
        dpi16 = dpi.astype(BF16)
        dwa_ref[0] = _dot_tn(u16, dpr16)
        dwx_ref[0] = _dot_tn(u16, dpi16)
        du = d_iu * i + _dot_nt(dpr16, wa_ref[0]) + _dot_nt(dpi16, wx_ref[0])
        dv_ref[0, 4:5, :] = _colsum(du)
        for j in range(LRU_CONV):
            sh = LRU_CONV - 1 - j
            xs = x_all if sh == 0 else pltpu.roll(x_all, sh, 0)
            dv_ref[0, j:j + 1, :] = _colsum(du * xs[LRU_HALO:, :])
        dusc[0:tp, :] = du
        dusc[tp:, :] = jnp.zeros((LRU_HALO, HEAD), F32)
        dx_ref[...] = _conv_taps_t(dusc[...], cw_ref, tp, LRU_HALO, LRU_CONV).astype(BF16)

    col, vec, mat = _lru_specs(tp, nb)
    ocol = lambda: pl.BlockSpec((tp, HEAD), lambda j: (0, j))
    return pl.pallas_call(
        body, name="lru_bwd",
        grid=(nb,),
        in_specs=[col(2), col(3), ocol(), ocol(), pl.BlockSpec((LRU_CONV, HEAD), lambda j: (0, j)), vec(), mat(),
                  vec(), mat(), vec(), vec()],
        out_specs=[ocol(), ocol(), mat(), mat(), pl.BlockSpec((1, SUBLANE, HEAD), lambda j: (j, 0, 0))],
        out_shape=[jax.ShapeDtypeStruct((tp, D_LRU), BF16), jax.ShapeDtypeStruct((tp, D_LRU), BF16),
                   jax.ShapeDtypeStruct((nb, HEAD, HEAD), F32), jax.ShapeDtypeStruct((nb, HEAD, HEAD), F32),
                   jax.ShapeDtypeStruct((nb, SUBLANE, HEAD), F32)],
        scratch_shapes=[pltpu.VMEM((tp + LRU_HALO, HEAD), F32), pltpu.VMEM((tp, HEAD), F32),
                        pltpu.VMEM((tp, HEAD), F32), pltpu.VMEM((tp, HEAD), F32),
                        pltpu.VMEM((tp + LRU_HALO, HEAD), F32)],
        compiler_params=_cparams(1),
    )(p, p, hs, dyd, cw, cb, wa, ba, wx, bx, lam)


def _mesh_pos():
    return lax.axis_index("x"), lax.axis_index("y"), lax.axis_index("c")


def _other_chips(x, y):
    return [(1 - x, y), (x, 1 - y), (1 - x, 1 - y)]


ANY = pl.BlockSpec(memory_space=pl.ANY)


def _allgather_shards(arrs, split):
    n = len(arrs)

    def body(*refs):
        ins, outs = refs[:n], refs[n:2 * n]
        send1, recv1, send2, recv2, send3, recv3 = refs[2 * n:]
        x, y, c = _mesh_pos()
        chip = 2 * x + y
        sibling = (x, y, 1 - c)
        others = _other_chips(x, y)

        def rows(k, cc):
            half = arrs[k].shape[0] // 2
            return pl.ds(cc * half, half)

        def remote(src, dst, ssem, rsem, dev):
            return pltpu.make_async_remote_copy(src_ref=src, dst_ref=dst, send_sem=ssem, recv_sem=rsem,
                                                device_id=dev, device_id_type=MESH)

        started = [remote(ins[k], outs[k].at[chip], send3.at[k], recv3.at[k], sibling) for k in range(n)]
        for cp in started:
            cp.start()
        for k in range(n):
            for j, (ox, oy) in enumerate(others):
                if split[k]:
                    src, dst = ins[k].at[rows(k, c)], outs[k].at[chip, rows(k, c)]
                else:
                    src, dst = ins[k], outs[k].at[chip]
                cp = remote(src, dst, send1.at[3 * k + j], recv1.at[3 * k + j], (ox, oy, c))
                cp.start()
                started.append(cp)
        for j, (ox, oy) in enumerate(others):
            ochip = 2 * ox + oy
            for k in range(n):
                if split[k]:
                    blk = outs[k].at[ochip, rows(k, c)]
                    remote(blk, blk, send1.at[3 * k + j], recv1.at[3 * k + j], sibling).wait_recv()
                    cp = remote(blk, blk, send2.at[3 * k + j], recv2.at[3 * k + j], sibling)
                    cp.start()
                    started.append(cp)
                else:
                    blk = outs[k].at[ochip]
                    remote(blk, blk, send1.at[3 * k + j], recv1.at[3 * k + j], sibling).wait_recv()
        for j, (ox, oy) in enumerate(others):
            ochip = 2 * ox + oy
            for k in range(n):
                if split[k]:
                    blk = outs[k].at[ochip, rows(k, 1 - c)]
                    remote(blk, blk, send2.at[3 * k + j], recv2.at[3 * k + j], sibling).wait_recv()
        for k in range(n):
            blk = outs[k].at[chip]
            remote(blk, blk, send3.at[k], recv3.at[k], sibling).wait_recv()
        for cp in started:
            cp.wait_send()

    return pl.pallas_call(
        body, name="allgather_shards",
        in_specs=[ANY] * n, out_specs=[ANY] * n,
        out_shape=[jax.ShapeDtypeStruct((N_SHARD,) + a.shape, a.dtype) for a in arrs],
        scratch_shapes=[pltpu.SemaphoreType.DMA((3 * n,)), pltpu.SemaphoreType.DMA((3 * n,)),
                        pltpu.SemaphoreType.DMA((3 * n,)), pltpu.SemaphoreType.DMA((3 * n,)),
                        pltpu.SemaphoreType.DMA((n,)), pltpu.SemaphoreType.DMA((n,))],
    )(*arrs)


def _pair_exchange_halves(arrs):
    n = len(arrs)

    def body(*refs):
        ins, outs = refs[:n], refs[n:2 * n]
        ssem, rsem = refs[2 * n:]
        x, y, c = _mesh_pos()
        cps = []
        for k in range(n):
            half = arrs[k].shape[1] // 2
            cp = pltpu.make_async_remote_copy(
                src_ref=ins[k].at[:, pl.ds((1 - c) * half, half)], dst_ref=outs[k],
                send_sem=ssem.at[k], recv_sem=rsem.at[k], device_id=(x, y, 1 - c), device_id_type=MESH)
            cp.start()
            cps.append(cp)
        for cp in cps:
            cp.wait()

    return pl.pallas_call(
        body, name="pair_exchange_halves",
        in_specs=[ANY] * n, out_specs=[ANY] * n,
        out_shape=[jax.ShapeDtypeStruct((a.shape[0], a.shape[1] // 2, a.shape[2]), a.dtype) for a in arrs],
        scratch_shapes=[pltpu.SemaphoreType.DMA((n,)), pltpu.SemaphoreType.DMA((n,))],
    )(*arrs)


GRAD_ROW_BLOCKS = 4


def _pair_add(arrs, recvd, core):
    n = len(arrs)
    nb = GRAD_ROW_BLOCKS

    def body(c_ref, *refs):
        for k in range(n):
            refs[2 * n + k][...] = (refs[k][...].astype(F32) + refs[n + k][...].astype(F32)).astype(BF16)

    def blk(a):
        return (1, a.shape[1] // 2 // nb, a.shape[2])

    grid_spec = pltpu.PrefetchScalarGridSpec(
        num_scalar_prefetch=1, grid=(N_SHARD, nb),
        in_specs=[pl.BlockSpec(blk(a), lambda s, i, c: (s, c[0] * nb + i, 0)) for a in arrs]
        + [pl.BlockSpec(blk(a), lambda s, i, c: (s, i, 0)) for a in arrs],
        out_specs=[pl.BlockSpec(blk(a), lambda s, i, c: (s, i, 0)) for a in arrs])
    return pl.pallas_call(
        body, name="pair_add", grid_spec=grid_spec,
        out_shape=[jax.ShapeDtypeStruct(r.shape, BF16) for r in recvd],
        compiler_params=_cparams(2),
    )(core, *arrs, *recvd)


def _own_slot(arrs, chip):
    n = len(arrs)
    nb = GRAD_ROW_BLOCKS

    def body(c_ref, *refs):
        for k in range(n):
            refs[n + k][...] = refs[k][...]

    def blk(a):
        return (1, a.shape[1] // nb, a.shape[2])

    grid_spec = pltpu.PrefetchScalarGridSpec(
        num_scalar_prefetch=1, grid=(nb,),
        in_specs=[pl.BlockSpec(blk(a), lambda i, c: (c[0], i, 0)) for a in arrs],
        out_specs=[pl.BlockSpec(blk(a), lambda i, c: (c[0], i, 0)) for a in arrs])
    return pl.pallas_call(
        body, name="own_slot", grid_spec=grid_spec,
        out_shape=[jax.ShapeDtypeStruct(a.shape, a.dtype) for a in arrs],
        compiler_params=_cparams(1),
    )(chip, *arrs)


def _scatter_to_owners(arrs, slots):
    n = len(arrs)

    def body(*refs):
        ins, outs = refs[:n], refs[2 * n:3 * n]
        ssem, rsem = refs[3 * n:]
        x, y, c = _mesh_pos()
        chip = 2 * x + y
        others = _other_chips(x, y)
        cps = []
        for k in range(n):
            for j, (ox, oy) in enumerate(others):
                cp = pltpu.make_async_remote_copy(
                    src_ref=ins[k].at[2 * ox + oy], dst_ref=outs[k].at[chip],
                    send_sem=ssem.at[3 * k + j], recv_sem=rsem.at[3 * k + j],
                    device_id=(ox, oy, c), device_id_type=MESH)
                cp.start()
                cps.append(cp)
        for k in range(n):
            for j, (ox, oy) in enumerate(others):
                blk = outs[k].at[2 * ox + oy]
                pltpu.make_async_remote_copy(
                    src_ref=blk, dst_ref=blk, send_sem=ssem.at[3 * k + j], recv_sem=rsem.at[3 * k + j],
                    device_id=(ox, oy, c), device_id_type=MESH).wait_recv()
        for cp in cps:
            cp.wait_send()

    return pl.pallas_call(
        body, name="scatter_to_owners",
        in_specs=[ANY] * (2 * n), out_specs=[ANY] * n,
        out_shape=[jax.ShapeDtypeStruct(a.shape, a.dtype) for a in arrs],
        scratch_shapes=[pltpu.SemaphoreType.DMA((3 * n,)), pltpu.SemaphoreType.DMA((3 * n,))],
        input_output_aliases={n + k: k for k in range(n)},
    )(*arrs, *slots)


def _sum_chips(arrs, core):
    n = len(arrs)
    nb = GRAD_ROW_BLOCKS

    def body(c_ref, *refs):
        for k in range(n):
            r = refs[k]
            refs[n + k][...] = ((r[0].astype(F32) + r[1].astype(F32)) + r[2].astype(F32)) + r[3].astype(F32)

    grid_spec = pltpu.PrefetchScalarGridSpec(
        num_scalar_prefetch=1, grid=(nb,),
        in_specs=[pl.BlockSpec((N_SHARD, a.shape[1] // nb, a.shape[2]), lambda i, c: (0, i, 0)) for a in arrs],
        out_specs=[pl.BlockSpec((a.shape[1] // nb, a.shape[2]), lambda i, c: (c[0] * nb + i, 0)) for a in arrs])
    return pl.pallas_call(
        body, name="sum_chips", grid_spec=grid_spec,
        out_shape=[jax.ShapeDtypeStruct((2 * a.shape[1], a.shape[2]), F32) for a in arrs],
        compiler_params=_cparams(1),
    )(core, *arrs)


def _pair_allgather_halves(arrs):
    n = len(arrs)

    def body(*refs):
        outs = refs[n:2 * n]
        ssem, rsem = refs[2 * n:]
        x, y, c = _mesh_pos()
        cps = []
        for k in range(n):
            h = arrs[k].shape[0] // 2
            mine = outs[k].at[pl.ds(c * h, h)]
            cp = pltpu.make_async_remote_copy(src_ref=mine, dst_ref=mine, send_sem=ssem.at[k],
                                              recv_sem=rsem.at[k], device_id=(x, y, 1 - c), device_id_type=MESH)
            cp.start()
            cps.append(cp)
        for k, cp in enumerate(cps):
            h = arrs[k].shape[0] // 2
            theirs = outs[k].at[pl.ds((1 - c) * h, h)]
            pltpu.make_async_remote_copy(src_ref=theirs, dst_ref=theirs, send_sem=ssem.at[k], recv_sem=rsem.at[k],
                                         device_id=(x, y, 1 - c), device_id_type=MESH).wait_recv()
            cp.wait_send()

    return pl.pallas_call(
        body, name="pair_allgather_halves",
        in_specs=[ANY] * n, out_specs=[ANY] * n,
        out_shape=[jax.ShapeDtypeStruct(a.shape, a.dtype) for a in arrs],
        scratch_shapes=[pltpu.SemaphoreType.DMA((n,)), pltpu.SemaphoreType.DMA((n,))],
        input_output_aliases={k: k for k in range(n)},
    )(*arrs)


N_DEV = 8


def _allgather_all(v):
    m_per, n = v.shape

    def body(x_ref, out_ref, send_sems, recv_sems, local_sem):
        x, y, c = _mesh_pos()
        me, sibling = (x, y, c), (x, y, 1 - c)
        chips = _other_chips(x, y)

        def rows(px, py, pc):
            return out_ref.at[pl.ds((4 * px + 2 * py + pc) * m_per, m_per), :]

        def copy(k, block, to, src=None):
            return pltpu.make_async_remote_copy(
                src_ref=rows(*block) if src is None else src, dst_ref=rows(*block),
                send_sem=send_sems.at[k], recv_sem=recv_sems.at[k], device_id=to, device_id_type=MESH)

        mine = pltpu.make_async_copy(x_ref, rows(*me), local_sem)
        mine.start()
        first = [copy(0, me, sibling, src=x_ref)]
        first += [copy(1 + j, me, (*chip, c), src=x_ref) for j, chip in enumerate(chips)]
        for cp in first:
            cp.start()
        passed = [copy(4 + j, (*chip, c), sibling) for j, chip in enumerate(chips)]
        for j, chip in enumerate(chips):
            copy(1 + j, (*chip, c), me).wait_recv()
            passed[j].start()
        copy(0, sibling, me).wait_recv()
        for j, chip in enumerate(chips):
            copy(4 + j, (*chip, 1 - c), me).wait_recv()
        for cp in first + passed:
            cp.wait_send()
        mine.wait()

    return pl.pallas_call(
        body, name="allgather_all",
        out_shape=jax.ShapeDtypeStruct((N_DEV * m_per, n), v.dtype),
        in_specs=[pl.BlockSpec(memory_space=pltpu.VMEM)],
        out_specs=pl.BlockSpec(memory_space=pltpu.VMEM),
        scratch_shapes=[pltpu.SemaphoreType.DMA((7,)), pltpu.SemaphoreType.DMA((7,)), pltpu.SemaphoreType.DMA],
        compiler_params=pltpu.CompilerParams(vmem_limit_bytes=VMEM_LIMIT_MB * 1024 * 1024),
    )(v)


def _adamw_math(w, g, m, v):
    m2 = ADAM_B1 * m + (1.0 - ADAM_B1) * g
    v2 = ADAM_B2 * v + (1.0 - ADAM_B2) * (g * g)
    m_hat = m2 / (1.0 - ADAM_B1 ** ADAM_STEP)
    v_hat = v2 / (1.0 - ADAM_B2 ** ADAM_STEP)
    delta = -ADAM_LR * (m_hat / (jnp.sqrt(v_hat) + ADAM_EPS) + ADAM_WD * w)
    return delta, m2, v2


def _adamw(w, m, v, g, row_off, nblk):
    r, n = w.shape
    br = r // nblk
    assert row_off % br == 0
    ob = row_off // br

    def body(w_ref, m_ref, v_ref, g_ref, go_ref, d_ref, mo_ref, vo_ref):
        g = g_ref[...]
        delta, m2, v2 = _adamw_math(w_ref[...], g, m_ref[...], v_ref[...])
        go_ref[...] = g
        d_ref[...] = delta
        mo_ref[...] = m2
        vo_ref[...] = v2

    spec = pl.BlockSpec((br, n), lambda i: (i, 0))
    return pl.pallas_call(
        body, name="adamw", grid=(nblk,),
        in_specs=[spec, spec, spec, pl.BlockSpec((br, n), lambda i: (ob + i, 0))],
        out_specs=[spec] * 4,
        out_shape=[jax.ShapeDtypeStruct((r, n), F32)] * 4,
        compiler_params=_cparams(1),
    )(w, m, v, g)


def _small_reduce_adamw(parts, w, m, v, rep_rows, sh_rows):
    mrows = rep_rows + N_SHARD * sh_rows

    def body(p_ref, w_ref, m_ref, v_ref, go_ref, d_ref, mo_ref, vo_ref):
        x, y, _ = _mesh_pos()
        mine = rep_rows + (2 * x + y) * sh_rows
        g_rep = p_ref[0:rep_rows, :]
        g_sh = p_ref[pl.ds(pl.multiple_of(mine, SUBLANE), sh_rows), :]
        for k in range(1, N_DEV):
            g_rep = g_rep + p_ref[k * mrows:k * mrows + rep_rows, :]
            g_sh = g_sh + p_ref[pl.ds(pl.multiple_of(k * mrows + mine, SUBLANE), sh_rows), :]
        g = jnp.concatenate([g_rep, g_sh], axis=0)
        delta, m2, v2 = _adamw_math(w_ref[...], g, m_ref[...], v_ref[...])
        go_ref[...] = g
        d_ref[...] = delta
        mo_ref[...] = m2
        vo_ref[...] = v2

    return pl.pallas_call(
        body, name="small_reduce_adamw",
        out_shape=[jax.ShapeDtypeStruct((rep_rows + sh_rows, 128), F32)] * 4,
        compiler_params=pltpu.CompilerParams(vmem_limit_bytes=VMEM_LIMIT_MB * 1024 * 1024),
    )(parts, w, m, v)


LANE = 128
REP_SPEC = (("ffn1_norm", 16), ("mix_norm", 16), ("ffn2_norm", 16), ("final_norm", 8), ("pool_w", 128),
            ("pool_scale", 8), ("hgrn_lb_logits", 16), ("hgrn_gnorm", 8), ("lru_wa", 256), ("lru_wx", 256))
SH_SPEC = (("meta_tokens", 32), ("conv_w", 32), ("lru_conv_w", 8), ("conv_b", 8), ("conv_ln_g", 8),
           ("conv_ln_b", 8), ("lru_conv_b", 8), ("lru_ba", 8), ("lru_bx", 8), ("lru_lambda", 8))
REP_ROWS = sum(r for _, r in REP_SPEC)
SH_ROWS = sum(r for _, r in SH_SPEC)


def _pack_rows(vals, spec):
    parts = []
    for name, rows in spec:
        flat = vals[name].astype(F32).reshape(-1, LANE)
        if flat.shape[0] < rows:
            flat = jnp.concatenate([flat, jnp.zeros((rows - flat.shape[0], LANE), F32)], axis=0)
        parts.append(flat)
    return jnp.concatenate(parts, axis=0)


def _unpack_rows(packed, spec, shapes):
    out = {}
    off = 0
    for name, rows in spec:
        shp = shapes[name]
        n = int(np.prod(shp)) // LANE
        out[name] = packed[off:off + n].reshape(shp)
        off += rows
    return out


def _block_diag(blocks):
    n, b, _ = blocks.shape
    return sum(jnp.pad(blocks[g], ((g * b, (n - 1 - g) * b), (g * b, (n - 1 - g) * b))) for g in range(n))


def _diag_blocks(mat, n):
    b = mat.shape[0] // n
    return jnp.stack([mat[g * b:(g + 1) * b, g * b:(g + 1) * b] for g in range(n)])


BIG = ("ffn1_wg", "ffn1_wu", "ffn2_wg", "ffn2_wu", "ffn1_wd", "ffn2_wd", "w_in_even", "w_out_even",
       "w_in_odd", "w_out_odd")
ADAM_BLOCKS = {"ffn1_wg": 8, "ffn1_wu": 8, "ffn2_wg": 8, "ffn2_wu": 8, "ffn1_wd": 4, "ffn2_wd": 4,
               "w_in_even": 4, "w_out_even": 2, "w_in_odd": 4, "w_out_odd": 2}
WEIGHT_NAMES = ('meta_tokens', 'ffn1_norm', 'ffn1_wg', 'ffn1_wu', 'ffn1_wd', 'mix_norm', 'ffn2_norm', 'ffn2_wg',
                'ffn2_wu', 'ffn2_wd', 'w_in_even', 'pool_w', 'pool_scale', 'hgrn_lb_logits', 'hgrn_gnorm',
                'w_out_even', 'w_in_odd', 'conv_w', 'conv_b', 'conv_ln_g', 'conv_ln_b', 'lru_conv_w',
                'lru_conv_b', 'lru_wa', 'lru_ba', 'lru_wx', 'lru_bx', 'lru_lambda', 'w_out_odd', 'final_norm')


def _rows2d(a):
    return a.reshape(-1, a.shape[-1])


def _local_step(x, tgt, w, gathered, small_full):
    s_len, d = x.shape
    t_real = s_len + N_META
    tp = -(-t_real // ROW_ALIGN) * ROW_ALIGN
    tm = _tile(tp, 832, ROW_ALIGN)
    tm_small = _tile(tp, 416, 16)
    tr = _tile(tp, 416, SUBLANE)
    f1 = ("ffn1_norm", "ffn1_wg", "ffn1_wu", "ffn1_wd")
    f2 = ("ffn2_norm", "ffn2_wg", "ffn2_wu", "ffn2_wd")

    meta_full = small_full["meta_tokens"]
    h0 = jnp.concatenate([meta_full, x, jnp.zeros((tp - t_real, d), F32)], axis=0)
    tgt_pad = jnp.concatenate([jnp.zeros((N_META, d), F32), tgt, jnp.zeros((tp - t_real, d), F32)], axis=0)

    w_in_even = jnp.transpose(gathered["w_in_even"], (1, 0, 2)).reshape(d, D_IN_EVEN)
    w_out_even = gathered["w_out_even"].reshape(d, d)
    w_out_odd = gathered["w_out_odd"].reshape(d, d)
    even_piece = [(w_in_even, (d, D_IN_EVEN), (0, 0))]
    odd_pieces = [(gathered["w_in_odd"], (1, d, D_IN_ODD // N_SHARD), (k, 0, 0)) for k in range(N_SHARD)]
    pool_wbd = _block_diag(w["pool_w"][0]).astype(BF16)
    pool_scale = w["pool_scale"]
    wa_bd = _block_diag2(w["lru_wa"][0]).astype(BF16)
    wx_bd = _block_diag2(w["lru_wx"][0]).astype(BF16)
    mst, msk, n_lev = _hgrn_consts(HG_CHUNK)
    conv_w = small_full["conv_w"]
    sf = small_full

    def gain(name, layer):
        return w[name][layer:layer + 1]

    def ffn(h, names, layer):
        return _ffn_fwd(h, gain(names[0], layer), gathered[names[1]], gathered[names[2]], gathered[names[3]],
                        layer, tm)

    h1, a1, b1, n1 = ffn(h0, f1, 0)
    p0, nm0 = _proj_fwd(h1, gain("mix_norm", 0), 0, even_piece, tm_small)
    ya = _pool_fwd(p0, pool_wbd, pool_scale, tr)
    yb, states = _hgrn_fwd(p0, w["hgrn_lb_logits"], w["hgrn_gnorm"], mst, msk, n_lev, tm)
    h2 = _out_fwd(h1, ya, yb, w_out_even, tm)
    h3, a2, b2, n2 = ffn(h2, f2, 0)
    h4, a3, b3, n3 = ffn(h3, f1, 1)
    p1, nm1 = _proj_fwd(h4, gain("mix_norm", 1), 1, odd_pieces, tm_small)
    yc = _convmod_fwd(p1, conv_w, sf["conv_b"], sf["conv_ln_g"], sf["conv_ln_b"], tr)
    lru_args = (sf["lru_conv_w"], sf["lru_conv_b"], wa_bd, sf["lru_ba"], wx_bd, sf["lru_bx"], sf["lru_lambda"])
    yd, hs = _lru_fwd(p1, *lru_args)
    h5 = _out_fwd(h4, yc, yd, w_out_odd, tm)
    h6, a4, b4, n4 = ffn(h5, f2, 1)
    loss, dh6, dg_final = _loss_bwd(h6, w["final_norm"].reshape(1, d), tgt_pad, t_real, tm)

    def ffn_bwd(dho, h, n, a, b, names, layer, acc):
        dh, da, db, dg = _ffn_bwd_act(dho, h, gain(names[0], layer), a, b, gathered[names[1]], gathered[names[2]],
                                      gathered[names[3]], layer, tm_small)
        acc = _ffn_bwd_w(dho, n, a, b, da, db, acc[0], acc[1], acc[2], layer, tm)
        return dh, dg, acc

    none3 = (None, None, None)
    dh5, dg_f2_l1, g_f2 = ffn_bwd(dh6, h5, n4, a4, b4, f2, 1, none3)
    dyc, dyd, dw_out_odd = _out_bwd(dh5, yc, yd, w_out_odd, tm)
    dca, dcb, dconv_w, dconv_vec = _convmod_bwd(p1, dyc, conv_w, sf["conv_b"], sf["conv_ln_g"], sf["conv_ln_b"], tr)
    dlx, dlg, dwa_bd, dwx_bd, dlru_vec = _lru_bwd(p1, hs, dyd, *lru_args)
    dp1 = [dca, dcb, dlx, dlg]
    dh4, dg_mix_l1 = _proj_bwd_act(dh5, h4, gain("mix_norm", 1), 1, dp1, odd_pieces, tm_small)
    dw_in_odd = jnp.stack(_proj_bwd_w(nm1, dp1, tm))
    dh3, dg_f1_l1, g_f1 = ffn_bwd(dh4, h3, n3, a3, b3, f1, 1, none3)
    dh2, dg_f2_l0, g_f2 = ffn_bwd(dh3, h2, n2, a2, b2, f2, 0, g_f2)
    dya, dyb, dw_out_even = _out_bwd(dh2, ya, yb, w_out_even, tm)
    dpool, dpool_wbd, dpool_scale = _pool_bwd(p0, dya, pool_wbd, pool_scale, tr)
    dq, dz, dv, dgate, dlb_logits, dgn_heads = _hgrn_bwd(p0, dyb, states, w["hgrn_lb_logits"], w["hgrn_gnorm"],
                                                         mst, msk, n_lev, tm)
    dp0 = [jnp.concatenate([dpool, dq, dz, dv, dgate], axis=1)]
    dh1, dg_mix_l0 = _proj_bwd_act(dh2, h1, gain("mix_norm", 0), 0, dp0, even_piece, tm_small)
    (dw_in_even,) = _proj_bwd_w(nm0, dp0, tm_small)
    dh0, dg_f1_l0, g_f1 = ffn_bwd(dh1, h0, n1, a1, b1, f1, 0, g_f1)

    grad_x = dh0[N_META:t_real]
    big = {
        "ffn1_wg": g_f1[0], "ffn1_wu": g_f1[1], "ffn1_wd": g_f1[2],
        "ffn2_wg": g_f2[0], "ffn2_wu": g_f2[1], "ffn2_wd": g_f2[2],
        "w_in_even": jnp.transpose(dw_in_even.reshape(d, N_SHARD, D_IN_EVEN // N_SHARD), (1, 0, 2)),
        "w_out_even": dw_out_even.reshape(N_SHARD, d // N_SHARD, d),
        "w_in_odd": dw_in_odd,
        "w_out_odd": dw_out_odd.reshape(N_SHARD, d // N_SHARD, d),
    }
    rep = {
        "ffn1_norm": jnp.concatenate([dg_f1_l0, dg_f1_l1], axis=0),
        "mix_norm": jnp.concatenate([dg_mix_l0, dg_mix_l1], axis=0),
        "ffn2_norm": jnp.concatenate([dg_f2_l0, dg_f2_l1], axis=0),
        "final_norm": dg_final,
        "pool_w": _diag_blocks(dpool_wbd, len(POOL_WINDOWS)),
        "pool_scale": dpool_scale,
        "hgrn_lb_logits": dlb_logits,
        "hgrn_gnorm": jnp.sum(dgn_heads, axis=0),
        "lru_wa": _diag_blocks2(dwa_bd),
        "lru_wx": _diag_blocks2(dwx_bd),
    }
    dmeta = jnp.transpose(dh0[:N_META].reshape(N_META, N_SHARD, 2, LANE), (1, 0, 2, 3)).reshape(N_SHARD, 32, LANE)
    packs = [_pack_rows(rep, REP_SPEC)]
    for s in range(N_SHARD):
        sh = {
            "meta_tokens": dmeta[s], "conv_w": dconv_w[s], "lru_conv_w": dlru_vec[s, 0:4],
            "conv_b": dconv_vec[s, 0:1], "conv_ln_g": dconv_vec[s, 1:2], "conv_ln_b": dconv_vec[s, 2:3],
            "lru_conv_b": dlru_vec[s, 4:5], "lru_ba": dlru_vec[s, 5:6], "lru_bx": dlru_vec[s, 6:7],
            "lru_lambda": dlru_vec[s, 7:8],
        }
        packs.append(_pack_rows(sh, SH_SPEC))
    return loss, grad_x, big, jnp.concatenate(packs, axis=0)


def _block_diag2(heads):
    nb = heads.shape[0] // 2
    return jnp.stack([_block_diag(heads[2 * j:2 * j + 2]) for j in range(nb)])


def _diag_blocks2(mats):
    return jnp.concatenate([_diag_blocks(mats[j], 2) for j in range(mats.shape[0])], axis=0)


def kernel(x, meta_tokens, ffn1_norm, ffn1_wg, ffn1_wu, ffn1_wd, mix_norm, ffn2_norm, ffn2_wg, ffn2_wu, ffn2_wd, w_in_even, pool_w, pool_scale, hgrn_lb_logits, hgrn_gnorm, w_out_even, w_in_odd, conv_w, conv_b, conv_ln_g, conv_ln_b, lru_conv_w, lru_conv_b, lru_wa, lru_ba, lru_wx, lru_bx, lru_lambda, w_out_odd, final_norm, loss_target, m_meta_tokens, m_ffn1_norm, m_ffn1_wg, m_ffn1_wu, m_ffn1_wd, m_mix_norm, m_ffn2_norm, m_ffn2_wg, m_ffn2_wu, m_ffn2_wd, m_w_in_even, m_pool_w, m_pool_scale, m_hgrn_lb_logits, m_hgrn_gnorm, m_w_out_even, m_w_in_odd, m_conv_w, m_conv_b, m_conv_ln_g, m_conv_ln_b, m_lru_conv_w, m_lru_conv_b, m_lru_wa, m_lru_ba, m_lru_wx, m_lru_bx, m_lru_lambda, m_w_out_odd, m_final_norm, v_meta_tokens, v_ffn1_norm, v_ffn1_wg, v_ffn1_wu, v_ffn1_wd, v_mix_norm, v_ffn2_norm, v_ffn2_wg, v_ffn2_wu, v_ffn2_wd, v_w_in_even, v_pool_w, v_pool_scale, v_hgrn_lb_logits, v_hgrn_gnorm, v_w_out_even, v_w_in_odd, v_conv_w, v_conv_b, v_conv_ln_g, v_conv_ln_b, v_lru_conv_w, v_lru_conv_b, v_lru_wa, v_lru_ba, v_lru_wx, v_lru_bx, v_lru_lambda, v_w_out_odd, v_final_norm):
    args = locals()
    w = {n: args[n] for n in WEIGHT_NAMES}
    m = {n: args["m_" + n] for n in WEIGHT_NAMES}
    v = {n: args["v_" + n] for n in WEIGHT_NAMES}
    shapes = {n: w[n].shape for n in WEIGHT_NAMES}

    big_in = [_rows2d(w[n]).astype(BF16) for n in BIG]
    small_sh = _pack_rows(w, SH_SPEC)
    gath = _allgather_shards(big_in + [small_sh], [True] * len(BIG) + [False])
    gathered = dict(zip(BIG, gath[:len(BIG)]))
    sm = gath[len(BIG)]
    sh_shapes = {n: (N_SHARD,) + tuple(shapes[n]) for n, _ in SH_SPEC}
    per_shard = [_unpack_rows(sm[s], SH_SPEC, shapes) for s in range(N_SHARD)]
    small_full = {}
    for n, _ in SH_SPEC:
        stacked = [per_shard[s][n] for s in range(N_SHARD)]
        small_full[n] = jnp.concatenate([p.reshape(-1, p.shape[-1]) for p in stacked], axis=-1)
    small_full["conv_w"] = jnp.concatenate(
        [small_full["conv_w"], jnp.zeros((CONV_HALO - CONV_WIDTH, D_CONV), F32)], axis=0)

    loss, grad_x, big, small_part = _local_step(x[0], loss_target[0], w, gathered, small_full)
    loss = lax.psum(loss[0, 0], ("x", "y", "c"))

    core = lax.axis_index("c").astype(jnp.int32).reshape(1)
    parts = [big[n] for n in BIG]
    recvd = _pair_exchange_halves(parts)
    pair = _pair_add(parts, recvd, core)
    chip = (2 * lax.axis_index("x") + lax.axis_index("y")).astype(jnp.int32).reshape(1)
    slots = _scatter_to_owners(pair, _own_slot(pair, chip))
    halves = _sum_chips(slots, core)
    full = _pair_allgather_halves(halves)
    out_g, out_d, out_m, out_v = {}, {}, {}, {}
    for n, g in zip(BIG, full):
        res = _adamw(_rows2d(w[n]), _rows2d(m[n]), _rows2d(v[n]), g, 0, ADAM_BLOCKS[n])
        out_g[n], out_d[n], out_m[n], out_v[n] = [r.reshape(shapes[n]) for r in res]

    gathered_small = _allgather_all(small_part)

    def pack_small(src):
        return jnp.concatenate([_pack_rows(src, REP_SPEC), _pack_rows(src, SH_SPEC)], axis=0)

    res = _small_reduce_adamw(gathered_small, pack_small(w), pack_small(m), pack_small(v), REP_ROWS, SH_ROWS)
    for dst, packed in zip((out_g, out_d, out_m, out_v), res):
        dst.update(_unpack_rows(packed[:REP_ROWS], REP_SPEC, shapes))
        dst.update(_unpack_rows(packed[REP_ROWS:], SH_SPEC, shapes))

    return (loss, grad_x[None], *[out_g[n] for n in WEIGHT_NAMES], *[out_d[n] for n in WEIGHT_NAMES],
            *[out_m[n] for n in WEIGHT_NAMES], *[out_v[n] for n in WEIGHT_NAMES])
```

```python
import functools

import numpy as np
import jax
import jax.numpy as jnp
from jax import lax
from jax.experimental import pallas as pl
from jax.experimental.pallas import tpu as pltpu

F32 = jnp.float32
BF16 = jnp.bfloat16
MESH = pl.DeviceIdType.MESH

EPS = 1e-6
N_META = 16
D_MODEL = 1024
D_FF = 2816
N_SHARD = 4
FF_SHARD = D_FF // N_SHARD
D_POOL = 256
POOL_GROUP = 64
POOL_WINDOWS = (2, 4, 8, 16)
D_HGRN = 768
HG_HEADS = 6
HEAD = 128
HG_CHUNK = 64
D_IN_EVEN = D_POOL + 4 * D_HGRN
D_CONV = 512
CONV_WIDTH = 31
CONV_HALO = 32
D_LRU = 512
LRU_CONV = 4
LRU_HALO = 8
LRU_C = 8.0
D_IN_ODD = 2 * D_CONV + 2 * D_LRU
SUBLANE = 8
ROW_ALIGN = 64

ADAM_LR = 0.001
ADAM_B1 = 0.9
ADAM_B2 = 0.999
ADAM_EPS = 1e-08
ADAM_WD = 0.01
ADAM_STEP = 10

VMEM_LIMIT_MB = 56


def _cparams(n_grid_axes=0, vmem_mb=VMEM_LIMIT_MB):
    sem = ("arbitrary",) * n_grid_axes if n_grid_axes else None
    return pltpu.CompilerParams(dimension_semantics=sem, vmem_limit_bytes=vmem_mb * 1024 * 1024)


def _tile(n, target, mult):
    best = None
    for t in range(mult, min(n, target) + 1, mult):
        if n % t == 0:
            best = t
    assert best is not None, (n, target, mult)
    return best


def _dot(a, b):
    return jnp.dot(a, b, preferred_element_type=F32)


def _dot_nt(a, b):
    return lax.dot_general(a, b, (((1,), (1,)), ((), ())), preferred_element_type=F32)


def _dot_tn(a, b):
    return lax.dot_general(a, b, (((0,), (0,)), ((), ())), preferred_element_type=F32)


def _sigmoid(x):
    return 1.0 / (1.0 + jnp.exp(-x))


def _colsum(x):
    return jnp.sum(x, axis=0, keepdims=True)


def _rms_stats(h):
    rstd = lax.rsqrt(jnp.mean(h * h, axis=-1, keepdims=True) + EPS)
    return rstd, h * rstd


def _rms_bwd(dn, g, rstd, xhat):
    dng = dn * g
    dh = rstd * (dng - xhat * jnp.mean(dng * xhat, axis=-1, keepdims=True))
    return dh, _colsum(dn * xhat)


def _ffn_fwd(h, norm, wg4, wu4, wd4, layer, tm):
    tp, d = h.shape
    nt = tp // tm

    def body(h_ref, g_ref, wg_ref, wu_ref, wd_ref, ho_ref, a_ref, b_ref, n_ref, n_sc, acc):
        s = pl.program_id(1)

        @pl.when(s == 0)
        def _():
            hh = h_ref[...]
            rstd, xhat = _rms_stats(hh)
            n = (xhat * g_ref[...]).astype(BF16)
            n_sc[...] = n
            n_ref[...] = n
            acc[...] = jnp.zeros_like(acc)

        n = n_sc[...]
        a = _dot(n, wg_ref[0])
        b = _dot(n, wu_ref[0])
        a_ref[0] = a.astype(BF16)
        b_ref[0] = b.astype(BF16)
        sg = (a * _sigmoid(a) * b).astype(BF16)
        acc[...] += _dot(sg, wd_ref[0])

        @pl.when(s == N_SHARD - 1)
        def _():
            ho_ref[...] = h_ref[...] + 0.5 * acc[...]

    return pl.pallas_call(
        body, name="ffn_fwd",
        grid=(nt, N_SHARD),
        in_specs=[
            pl.BlockSpec((tm, d), lambda i, s: (i, 0)),
            pl.BlockSpec((1, d), lambda i, s: (0, 0)),
            pl.BlockSpec((1, d, FF_SHARD), lambda i, s: (s, layer, 0)),
            pl.BlockSpec((1, d, FF_SHARD), lambda i, s: (s, layer, 0)),
            pl.BlockSpec((1, FF_SHARD, d), lambda i, s: (s, layer, 0)),
        ],
        out_specs=[
            pl.BlockSpec((tm, d), lambda i, s: (i, 0)),
            pl.BlockSpec((1, tm, FF_SHARD), lambda i, s: (s, i, 0)),
            pl.BlockSpec((1, tm, FF_SHARD), lambda i, s: (s, i, 0)),
            pl.BlockSpec((tm, d), lambda i, s: (i, 0)),
        ],
        out_shape=[
            jax.ShapeDtypeStruct((tp, d), F32),
            jax.ShapeDtypeStruct((N_SHARD, tp, FF_SHARD), BF16),
            jax.ShapeDtypeStruct((N_SHARD, tp, FF_SHARD), BF16),
            jax.ShapeDtypeStruct((tp, d), BF16),
        ],
        scratch_shapes=[pltpu.VMEM((tm, d), BF16), pltpu.VMEM((tm, d), F32)],
        compiler_params=_cparams(2),
    )(h, norm, wg4, wu4, wd4)


def _ffn_bwd_act(dho, h, norm, a4, b4, wg4, wu4, wd4, layer, tm):
    tp, d = h.shape
    nt = tp // tm

    def body(dho_ref, h_ref, g_ref, a_ref, b_ref, wg_ref, wu_ref, wd_ref,
             dh_ref, da_ref, db_ref, dg_ref, dy_sc, dn_sc):
        i = pl.program_id(0)
        s = pl.program_id(1)

        @pl.when(s == 0)
        def _():
            dy_sc[...] = (0.5 * dho_ref[...]).astype(BF16)
            dn_sc[...] = jnp.zeros_like(dn_sc)

        @pl.when((s == 0) & (i == 0))
        def _():
            dg_ref[...] = jnp.zeros_like(dg_ref)

        ds = _dot_nt(dy_sc[...], wd_ref[0])
        a = a_ref[0].astype(F32)
        b = b_ref[0].astype(F32)
        sig = _sigmoid(a)
        da = (ds * b * (sig * (1.0 + a * (1.0 - sig)))).astype(BF16)
        db = (ds * (a * sig)).astype(BF16)
        da_ref[0] = da
        db_ref[0] = db
        dn_sc[...] += _dot_nt(da, wg_ref[0]) + _dot_nt(db, wu_ref[0])

        @pl.when(s == N_SHARD - 1)
        def _():
            rstd, xhat = _rms_stats(h_ref[...])
            dh, dg = _rms_bwd(dn_sc[...], g_ref[...], rstd, xhat)
            dh_ref[...] = dho_ref[...] + dh
            dg_ref[...] += dg

    return pl.pallas_call(
        body, name="ffn_bwd_act",
        grid=(nt, N_SHARD),
        in_specs=[
            pl.BlockSpec((tm, d), lambda i, s: (i, 0)),
            pl.BlockSpec((tm, d), lambda i, s: (i, 0)),
            pl.BlockSpec((1, d), lambda i, s: (0, 0)),
            pl.BlockSpec((1, tm, FF_SHARD), lambda i, s: (s, i, 0)),
            pl.BlockSpec((1, tm, FF_SHARD), lambda i, s: (s, i, 0)),
            pl.BlockSpec((1, d, FF_SHARD), lambda i, s: (s, layer, 0)),
            pl.BlockSpec((1, d, FF_SHARD), lambda i, s: (s, layer, 0)),
            pl.BlockSpec((1, FF_SHARD, d), lambda i, s: (s, layer, 0)),
        ],
        out_specs=[
            pl.BlockSpec((tm, d), lambda i, s: (i, 0)),
            pl.BlockSpec((1, tm, FF_SHARD), lambda i, s: (s, i, 0)),
            pl.BlockSpec((1, tm, FF_SHARD), lambda i, s: (s, i, 0)),
            pl.BlockSpec((1, d), lambda i, s: (0, 0)),
        ],
        out_shape=[
            jax.ShapeDtypeStruct((tp, d), F32),
            jax.ShapeDtypeStruct((N_SHARD, tp, FF_SHARD), BF16),
            jax.ShapeDtypeStruct((N_SHARD, tp, FF_SHARD), BF16),
            jax.ShapeDtypeStruct((1, d), F32),
        ],
        scratch_shapes=[pltpu.VMEM((tm, d), BF16), pltpu.VMEM((tm, d), F32)],
        compiler_params=_cparams(2),
    )(dho, h, norm, a4, b4, wg4, wu4, wd4)


def _ffn_bwd_w(dho, n, a4, b4, da4, db4, tm):
    tp, d = n.shape
    nt = tp // tm

    def body(dho_ref, n_ref, a_ref, b_ref, da_ref, db_ref, og_ref, ou_ref, od_ref, accg, accu, accd):
        i = pl.program_id(1)

        @pl.when(i == 0)
        def _():
            accg[...] = jnp.zeros_like(accg)
            accu[...] = jnp.zeros_like(accu)
            accd[...] = jnp.zeros_like(accd)

        nn = n_ref[...]
        accg[...] += _dot_tn(nn, da_ref[0])
        accu[...] += _dot_tn(nn, db_ref[0])
        a = a_ref[0].astype(F32)
        b = b_ref[0].astype(F32)
        sact = (a * _sigmoid(a) * b).astype(BF16)
        dy = (0.5 * dho_ref[...]).astype(BF16)
        accd[...] += _dot_tn(sact, dy)

        @pl.when(i == nt - 1)
        def _():
            og_ref[0] = accg[...].astype(BF16)
            ou_ref[0] = accu[...].astype(BF16)
            od_ref[0] = accd[...].astype(BF16)

    in_specs = [
        pl.BlockSpec((tm, d), lambda s, i: (i, 0)),
        pl.BlockSpec((tm, d), lambda s, i: (i, 0)),
        pl.BlockSpec((1, tm, FF_SHARD), lambda s, i: (s, i, 0)),
        pl.BlockSpec((1, tm, FF_SHARD), lambda s, i: (s, i, 0)),
        pl.BlockSpec((1, tm, FF_SHARD), lambda s, i: (s, i, 0)),
        pl.BlockSpec((1, tm, FF_SHARD), lambda s, i: (s, i, 0)),
    ]
    return pl.pallas_call(
        body, name="ffn_bwd_w",
        grid=(N_SHARD, nt),
        in_specs=in_specs,
        out_specs=[
            pl.BlockSpec((1, d, FF_SHARD), lambda s, i: (s, 0, 0)),
            pl.BlockSpec((1, d, FF_SHARD), lambda s, i: (s, 0, 0)),
            pl.BlockSpec((1, FF_SHARD, d), lambda s, i: (s, 0, 0)),
        ],
        out_shape=[
            jax.ShapeDtypeStruct((N_SHARD, d, FF_SHARD), BF16),
            jax.ShapeDtypeStruct((N_SHARD, d, FF_SHARD), BF16),
            jax.ShapeDtypeStruct((N_SHARD, FF_SHARD, d), BF16),
        ],
        scratch_shapes=[pltpu.VMEM((d, FF_SHARD), F32), pltpu.VMEM((d, FF_SHARD), F32),
                        pltpu.VMEM((FF_SHARD, d), F32)],
        compiler_params=_cparams(2),
    )(dho, n, a4, b4, da4, db4)


def _proj_fwd(h, norm, layer, w_pieces, tm):
    tp, d = h.shape
    widths = [bs[-1] for _, bs, _ in w_pieces]
    ntot = sum(widths)
    npc = len(w_pieces)

    def body(*refs):
        h_ref, g_ref = refs[:2]
        w_refs = refs[2:2 + npc]
        p_ref, n_ref = refs[2 + npc:]
        rstd, xhat = _rms_stats(h_ref[...])
        n = (xhat * g_ref[...]).astype(BF16)
        n_ref[...] = n
        off = 0
        for k in range(npc):
            w = w_refs[k][...]
            w = w.reshape(w.shape[-2], w.shape[-1])
            p_ref[:, off:off + widths[k]] = _dot(n, w)
            off += widths[k]

    in_specs = [pl.BlockSpec((tm, d), lambda i: (i, 0)), pl.BlockSpec((1, d), lambda i: (0, 0))]
    for _, bs, idx in w_pieces:
        in_specs.append(pl.BlockSpec(bs, functools.partial(lambda i, idx: idx, idx=idx)))
    return pl.pallas_call(
        body, name="proj_fwd",
        grid=(tp // tm,),
        in_specs=in_specs,
        out_specs=[pl.BlockSpec((tm, ntot), lambda i: (i, 0)), pl.BlockSpec((tm, d), lambda i: (i, 0))],
        out_shape=[jax.ShapeDtypeStruct((tp, ntot), F32), jax.ShapeDtypeStruct((tp, d), BF16)],
        compiler_params=_cparams(1),
    )(h, norm, *[w for w, _, _ in w_pieces])


def _proj_bwd_act(dres, h, norm, layer, dp_pieces, w_pieces, tm):
    tp, d = h.shape
    npc = len(w_pieces)

    def body(*refs):
        dres_ref, h_ref, g_ref = refs[:3]
        dp_refs = refs[3:3 + npc]
        w_refs = refs[3 + npc:3 + 2 * npc]
        dh_ref, dg_ref = refs[3 + 2 * npc:]
        i = pl.program_id(0)

        @pl.when(i == 0)
        def _():
            dg_ref[...] = jnp.zeros_like(dg_ref)

        dn = None
        for k in range(npc):
            w = w_refs[k][...]
            w = w.reshape(w.shape[-2], w.shape[-1])
            t = _dot_nt(dp_refs[k][...], w)
            dn = t if dn is None else dn + t
        rstd, xhat = _rms_stats(h_ref[...])
        dh, dg = _rms_bwd(dn, g_ref[...], rstd, xhat)
        dh_ref[...] = dres_ref[...] + dh
        dg_ref[...] += dg

    in_specs = [pl.BlockSpec((tm, d), lambda i: (i, 0)), pl.BlockSpec((tm, d), lambda i: (i, 0)),
                pl.BlockSpec((1, d), lambda i: (0, 0))]
    for dp in dp_pieces:
        in_specs.append(pl.BlockSpec((tm, dp.shape[1]), lambda i: (i, 0)))
    for _, bs, idx in w_pieces:
        in_specs.append(pl.BlockSpec(bs, functools.partial(lambda i, idx: idx, idx=idx)))
    return pl.pallas_call(
        body, name="proj_bwd_act",
        grid=(tp // tm,),
        in_specs=in_specs,
        out_specs=[pl.BlockSpec((tm, d), lambda i: (i, 0)), pl.BlockSpec((1, d), lambda i: (0, 0))],
        out_shape=[jax.ShapeDtypeStruct((tp, d), F32), jax.ShapeDtypeStruct((1, d), F32)],
        compiler_params=_cparams(1),
    )(dres, h, norm, *dp_pieces, *[w for w, _, _ in w_pieces])


def _proj_bwd_w(n, dp_pieces, tm):
    tp, d = n.shape
    npc = len(dp_pieces)
    widths = [dp.shape[1] for dp in dp_pieces]

    def body(*refs):
        n_ref = refs[0]
        dp_refs = refs[1:1 + npc]
        o_refs = refs[1 + npc:1 + 2 * npc]
        accs = refs[1 + 2 * npc:]
        i = pl.program_id(0)

        @pl.when(i == 0)
        def _():
            for acc in accs:
                acc[...] = jnp.zeros_like(acc)

        nn = n_ref[...]
        for k in range(npc):
            accs[k][...] += _dot_tn(nn, dp_refs[k][...])

        @pl.when(i == pl.num_programs(0) - 1)
        def _():
            for k in range(npc):
                o_refs[k][...] = accs[k][...].astype(BF16)

    return pl.pallas_call(
        body, name="proj_bwd_w",
        grid=(tp // tm,),
        in_specs=[pl.BlockSpec((tm, d), lambda i: (i, 0))]
        + [pl.BlockSpec((tm, w), lambda i: (i, 0)) for w in widths],
        out_specs=[pl.BlockSpec((d, w), lambda i: (0, 0)) for w in widths],
        out_shape=[jax.ShapeDtypeStruct((d, w), BF16) for w in widths],
        scratch_shapes=[pltpu.VMEM((d, w), F32) for w in widths],
        compiler_params=_cparams(1),
    )(n, *dp_pieces)


def _out_fwd(h, ya, yb, w, tm):
    tp, d = h.shape
    na, nb = ya.shape[1], yb.shape[1]

    def body(h_ref, ya_ref, yb_ref, w_ref, o_ref):
        y = _dot(ya_ref[...].astype(BF16), w_ref[0:na, :]) + _dot(yb_ref[...].astype(BF16), w_ref[na:, :])
        o_ref[...] = h_ref[...] + y

    return pl.pallas_call(
        body, name="out_fwd",
        grid=(tp // tm,),
        in_specs=[pl.BlockSpec((tm, d), lambda i: (i, 0)), pl.BlockSpec((tm, na), lambda i: (i, 0)),
                  pl.BlockSpec((tm, nb), lambda i: (i, 0)), pl.BlockSpec((d, d), lambda i: (0, 0))],
        out_specs=pl.BlockSpec((tm, d), lambda i: (i, 0)),
        out_shape=jax.ShapeDtypeStruct((tp, d), F32),
        compiler_params=_cparams(1),
    )(h, ya, yb, w)


def _out_bwd(dy, ya, yb, w, tm):
    tp, d = dy.shape
    na, nb = ya.shape[1], yb.shape[1]

    def body(dy_ref, ya_ref, yb_ref, w_ref, da_ref, db_ref, dw_ref, acc):
        i = pl.program_id(0)

        @pl.when(i == 0)
        def _():
            acc[...] = jnp.zeros_like(acc)

        dyb16 = dy_ref[...].astype(BF16)
        da_ref[...] = _dot_nt(dyb16, w_ref[0:na, :])
        db_ref[...] = _dot_nt(dyb16, w_ref[na:, :])
        acc[0:na, :] += _dot_tn(ya_ref[...].astype(BF16), dyb16)
        acc[na:, :] += _dot_tn(yb_ref[...].astype(BF16), dyb16)

        @pl.when(i == pl.num_programs(0) - 1)
        def _():
            dw_ref[...] = acc[...].astype(BF16)

    return pl.pallas_call(
        body, name="out_bwd",
        grid=(tp // tm,),
        in_specs=[pl.BlockSpec((tm, d), lambda i: (i, 0)), pl.BlockSpec((tm, na), lambda i: (i, 0)),
                  pl.BlockSpec((tm, nb), lambda i: (i, 0)), pl.BlockSpec((d, d), lambda i: (0, 0))],
        out_specs=[pl.BlockSpec((tm, na), lambda i: (i, 0)), pl.BlockSpec((tm, nb), lambda i: (i, 0)),
                   pl.BlockSpec((d, d), lambda i: (0, 0))],
        out_shape=[jax.ShapeDtypeStruct((tp, na), F32), jax.ShapeDtypeStruct((tp, nb), F32),
                   jax.ShapeDtypeStruct((d, d), BF16)],
        scratch_shapes=[pltpu.VMEM((d, d), F32)],
        compiler_params=_cparams(1),
    )(dy, ya, yb, w)


def _loss_bwd(h, gfin, tgt, t_real, tm):
    tp, d = h.shape

    def body(h_ref, g_ref, t_ref, loss_ref, dh_ref, dg_ref):
        i = pl.program_id(0)

        @pl.when(i == 0)
        def _():
            loss_ref[...] = jnp.zeros_like(loss_ref)
            dg_ref[...] = jnp.zeros_like(dg_ref)

        rows = i * tm + lax.broadcasted_iota(jnp.int32, (tm, 1), 0)
        valid = (rows >= N_META) & (rows < t_real)
        rstd, xhat = _rms_stats(h_ref[...])
        g = g_ref[...]
        err = jnp.where(valid, xhat * g - t_ref[...], 0.0)
        e2 = jnp.sum(err * err, axis=1, keepdims=True)
        loss_ref[...] += (0.5 / d) * jnp.sum(e2, axis=0, keepdims=True)
        dy = err * (1.0 / d)
        dh, dg = _rms_bwd(dy, g, rstd, xhat)
        dh_ref[...] = dh
        dg_ref[...] += dg

    return pl.pallas_call(
        body, name="loss_bwd",
        grid=(tp // tm,),
        in_specs=[pl.BlockSpec((tm, d), lambda i: (i, 0)), pl.BlockSpec((1, d), lambda i: (0, 0)),
                  pl.BlockSpec((tm, d), lambda i: (i, 0))],
        out_specs=[pl.BlockSpec((1, 1), lambda i: (0, 0)), pl.BlockSpec((tm, d), lambda i: (i, 0)),
                   pl.BlockSpec((1, d), lambda i: (0, 0))],
        out_shape=[jax.ShapeDtypeStruct((1, 1), F32), jax.ShapeDtypeStruct((tp, d), F32),
                   jax.ShapeDtypeStruct((1, d), F32)],
        compiler_params=_cparams(1),
    )(h, gfin, tgt)


POOL_HALO = 16


def _pool_lane_consts(n_rows):
    lane = lax.broadcasted_iota(jnp.int32, (n_rows, D_POOL), 1)
    grp = lane // POOL_GROUP
    win = jnp.where(grp == 0, 2.0, jnp.where(grp == 1, 4.0, jnp.where(grp == 2, 8.0, 16.0)))
    return grp, win


def _pool_select(grp, s2, s4, s8, s16):
    return jnp.where(grp == 0, s2, jnp.where(grp == 1, s4, jnp.where(grp == 2, s8, s16)))


def _pool_mixed(x, row0, tr):
    n = tr + POOL_HALO
    s2 = x + pltpu.roll(x, 1, 0)
    s4 = s2 + pltpu.roll(s2, 2, 0)
    s8 = s4 + pltpu.roll(s4, 4, 0)
    s16 = s8 + pltpu.roll(s8, 8, 0)
    grp, win = _pool_lane_consts(n)
    rows = row0 - POOL_HALO + lax.broadcasted_iota(jnp.int32, (n, D_POOL), 0)
    cnt = jnp.minimum((rows + 1).astype(F32), win)
    pooled = _pool_select(grp, s2, s4, s8, s16) / jnp.maximum(cnt, 1.0)
    return (pooled - x)[POOL_HALO:, :]


def _pool_fwd(p, wbd, scale, tr):
    tp = p.shape[0]
    nt = tp // tr

    def body(p_ref, w_ref, s_ref, y_ref, usc):
        usc[0:POOL_HALO, :] = jnp.zeros((POOL_HALO, D_POOL), F32)
        usc[POOL_HALO:, :] = p_ref[...]

        def tile(r, carry):
            r0 = pl.multiple_of(r * tr, SUBLANE)
            x = usc[pl.ds(r0, tr + POOL_HALO), :]
            mixed = _pool_mixed(x, r0, tr)
            y_ref[pl.ds(r0, tr), :] = _dot(mixed.astype(BF16), w_ref[...]) * s_ref[...]
            return carry

        lax.fori_loop(0, nt, tile, 0)

    return pl.pallas_call(
        body, name="pool_fwd",
        grid=(1,),
        in_specs=[pl.BlockSpec((tp, D_POOL), lambda i: (0, 0)), pl.BlockSpec((D_POOL, D_POOL), lambda i: (0, 0)),
                  pl.BlockSpec((1, D_POOL), lambda i: (0, 0))],
        out_specs=pl.BlockSpec((tp, D_POOL), lambda i: (0, 0)),
        out_shape=jax.ShapeDtypeStruct((tp, D_POOL), F32),
        scratch_shapes=[pltpu.VMEM((tp + POOL_HALO, D_POOL), F32)],
        compiler_params=_cparams(1),
    )(p, wbd, scale)


def _pool_bwd(p, dya, wbd, scale, tr):
    tp = p.shape[0]
    nt = tp // tr

    def body(p_ref, dy_ref, w_ref, s_ref, du_ref, dw_ref, ds_ref, usc, gsc):
        usc[0:POOL_HALO, :] = jnp.zeros((POOL_HALO, D_POOL), F32)
        usc[POOL_HALO:, :] = p_ref[...]
        gsc[tp:, :] = jnp.zeros((POOL_HALO, D_POOL), F32)
        dw_ref[...] = jnp.zeros_like(dw_ref)
        ds_ref[...] = jnp.zeros_like(ds_ref)
        grp, win = _pool_lane_consts(tr)

        def tile1(r, carry):
            r0 = pl.multiple_of(r * tr, SUBLANE)
            x = usc[pl.ds(r0, tr + POOL_HALO), :]
            mixed = _pool_mixed(x, r0, tr).astype(BF16)
            dy = dy_ref[pl.ds(r0, tr), :]
            dys = (dy * s_ref[...]).astype(BF16)
            ypre = _dot(mixed, w_ref[...])
            ds_ref[...] += _colsum(dy * ypre)
            dw_ref[...] += _dot_tn(mixed, dys)
            dmx = _dot_nt(dys, w_ref[...])
            rows = r0 + lax.broadcasted_iota(jnp.int32, (tr, D_POOL), 0)
            cnt = jnp.minimum((rows + 1).astype(F32), win)
            gsc[pl.ds(r0, tr), :] = dmx / cnt
            return carry

        lax.fori_loop(0, nt, tile1, 0)
        n = tr + POOL_HALO
        grp2, win2 = _pool_lane_consts(n)

        def tile2(r, carry):
            r0 = pl.multiple_of(r * tr, SUBLANE)
            g = gsc[pl.ds(r0, n), :]
            s2 = g + pltpu.roll(g, n - 1, 0)
            s4 = s2 + pltpu.roll(s2, n - 2, 0)
            s8 = s4 + pltpu.roll(s4, n - 4, 0)
            s16 = s8 + pltpu.roll(s8, n - 8, 0)
            pooled_t = _pool_select(grp2, s2, s4, s8, s16)
            rows = r0 + lax.broadcasted_iota(jnp.int32, (n, D_POOL), 0)
            cnt = jnp.minimum((rows + 1).astype(F32), win2)
            du = pooled_t - g * cnt
            du_ref[pl.ds(r0, tr), :] = du[0:tr, :].astype(BF16)
            return carry

        lax.fori_loop(0, nt, tile2, 0)

    return pl.pallas_call(
        body, name="pool_bwd",
        grid=(1,),
        in_specs=[pl.BlockSpec((tp, D_POOL), lambda i: (0, 0)), pl.BlockSpec((tp, D_POOL), lambda i: (0, 0)),
                  pl.BlockSpec((D_POOL, D_POOL), lambda i: (0, 0)), pl.BlockSpec((1, D_POOL), lambda i: (0, 0))],
        out_specs=[pl.BlockSpec((tp, D_POOL), lambda i: (0, 0)), pl.BlockSpec((D_POOL, D_POOL), lambda i: (0, 0)),
                   pl.BlockSpec((1, D_POOL), lambda i: (0, 0))],
        out_shape=[jax.ShapeDtypeStruct((tp, D_POOL), BF16), jax.ShapeDtypeStruct((D_POOL, D_POOL), F32),
                   jax.ShapeDtypeStruct((1, D_POOL), F32)],
        scratch_shapes=[pltpu.VMEM((tp + POOL_HALO, D_POOL), F32), pltpu.VMEM((tp + POOL_HALO, D_POOL), F32)],
        compiler_params=_cparams(1),
    )(p, dya, wbd, scale)


def _hgrn_consts(ch):
    t = np.arange(ch)
    levels = []
    w = ch // 2
    while w >= 1:
        levels.append(w)
        w //= 2
    tril = t[None, :] <= t[:, None]
    to_end = t[None, :] > t[:, None]
    mats = [tril, to_end]
    masks = []
    for w in levels:
        pos = t % (2 * w)
        blk = t // (2 * w)
        upper = pos >= w
        mid = blk * 2 * w + w - 1
        mats.append(upper[:, None] & (t[None, :] > mid[:, None]) & (t[None, :] <= t[:, None]))
        mats.append((~upper)[:, None] & (t[None, :] > t[:, None]) & (t[None, :] <= mid[:, None]))
        masks.append(upper[:, None] & (~upper)[None, :] & (blk[:, None] == blk[None, :]))
    masks.append(tril)
    mst = np.concatenate(mats, axis=0).astype(np.float32)
    msk = np.stack(masks).astype(np.float32)
    return jnp.asarray(mst, BF16), jnp.asarray(msk, F32), len(levels)


def _split3(x):
    hi = x.astype(BF16)
    r1 = x - hi.astype(F32)
    mid = r1.astype(BF16)
    lo = (r1 - mid.astype(F32)).astype(BF16)
    return hi, mid, lo


def _hgrn_exponents(mst, logf):
    hi, mid, lo = _split3(logf)
    x = _dot(mst, jnp.concatenate([hi, mid, lo], axis=1))
    return x[:, 0:HEAD] + x[:, HEAD:2 * HEAD] + x[:, 2 * HEAD:3 * HEAD]


def _hgrn_gates(q_raw, z, lb):
    sz = _sigmoid(z)
    f = lb + (1.0 - lb) * sz
    q = q_raw * _sigmoid(q_raw)
    k = (1.0 - lb) * (1.0 - sz)
    return q, k, f, sz


def _hgrn_intra(q, k, x, msk_ref, n_lev, ch):
    a = msk_ref[n_lev] * 0.0
    eye = (lax.broadcasted_iota(jnp.int32, (ch, ch), 0) == lax.broadcasted_iota(jnp.int32, (ch, ch), 1))
    a = jnp.where(eye, jnp.sum(q * k, axis=1, keepdims=True), 0.0)
    ops = []
    for lv in range(n_lev):
        eq = jnp.exp(x[(2 + 2 * lv) * ch:(3 + 2 * lv) * ch, :])
        ek = jnp.exp(x[(3 + 2 * lv) * ch:(4 + 2 * lv) * ch, :])
        qd = q * eq
        kd = k * ek
        a = a + msk_ref[lv] * _dot_nt(qd.astype(BF16), kd.astype(BF16))
        ops.append((eq, ek, qd, kd))
    return a, ops


def _hgrn_fwd(p, lb_logits, gnorm, mst, msk, n_lev, tm):
    tp = p.shape[0]
    ch = HG_CHUNK
    nct = tm // ch
    nt = tp // tm
    nr = mst.shape[0]
    base = D_POOL // HEAD

    def body(q_ref, z_ref, v_ref, g_ref, lg_ref, gn_ref, mst_ref, msk_ref, y_ref, ss_ref, st_sc):
        @pl.when(pl.program_id(1) == 0)
        def _():
            st_sc[...] = jnp.zeros_like(st_sc)

        lb = _sigmoid(lg_ref[0:1, :] - lg_ref[1:2, :])

        def chunk(c, carry):
            r0 = pl.multiple_of(c * ch, ch)
            q, k, f, _ = _hgrn_gates(q_ref[pl.ds(r0, ch), :], z_ref[pl.ds(r0, ch), :], lb)
            v = v_ref[pl.ds(r0, ch), :]
            x = _hgrn_exponents(mst_ref[...], jnp.log(f))
            st = st_sc[...]
            ss_ref[0, c] = st
            qe = q * jnp.exp(x[0:ch, :])
            a, _ = _hgrn_intra(q, k, x, msk_ref, n_lev, ch)
            v16 = v.astype(BF16)
            o = _dot_nt(qe.astype(BF16), st.astype(BF16)) + _dot(a.astype(BF16), v16)
            kl = k * jnp.exp(x[ch:2 * ch, :])
            st_sc[...] = st * jnp.exp(x[ch - 1:ch, :]) + _dot_tn(v16, kl.astype(BF16))
            rstd = lax.rsqrt(jnp.mean(o * o, axis=-1, keepdims=True) + EPS)
            g_raw = g_ref[pl.ds(r0, ch), :]
            y_ref[pl.ds(r0, ch), :] = o * rstd * gn_ref[...] * (g_raw * _sigmoid(g_raw))
            return carry

        lax.fori_loop(0, nct, chunk, 0)

    def pspec(seg):
        return pl.BlockSpec((tm, HEAD), lambda h, i: (i, base + seg * HG_HEADS + h))

    return pl.pallas_call(
        body, name="hgrn_fwd",
        grid=(HG_HEADS, nt),
        in_specs=[pspec(0), pspec(1), pspec(2), pspec(3),
                  pl.BlockSpec((2, HEAD), lambda h, i: (0, h)),
                  pl.BlockSpec((1, HEAD), lambda h, i: (0, 0)),
                  pl.BlockSpec((nr, ch), lambda h, i: (0, 0)),
                  pl.BlockSpec((n_lev + 1, ch, ch), lambda h, i: (0, 0, 0))],
        out_specs=[pl.BlockSpec((tm, HEAD), lambda h, i: (i, h)),
                   pl.BlockSpec((1, nct, HEAD, HEAD), lambda h, i: (h, i, 0, 0))],
        out_shape=[jax.ShapeDtypeStruct((tp, D_HGRN), F32),
                   jax.ShapeDtypeStruct((HG_HEADS, tp // ch, HEAD, HEAD), F32)],
        scratch_shapes=[pltpu.VMEM((HEAD, HEAD), F32)],
        compiler_params=_cparams(2),
    )(p, p, p, p, lb_logits, gnorm, mst, msk)


def _hgrn_bwd(p, dyb, states, lb_logits, gnorm, mst, msk, n_lev, tm):
    tp = p.shape[0]
    ch = HG_CHUNK
    nct = tm // ch
    nt = tp // tm
    nr = mst.shape[0]
    base = D_POOL // HEAD

    def body(q_ref, z_ref, v_ref, g_ref, dy_ref, ss_ref, lg_ref, gn_ref, mst_ref, msk_ref,
             dq_ref, dz_ref, dv_ref, dg_ref, dlg_ref, dgn_ref, dst_sc, dlb_sc, stk):
        ti = pl.program_id(1)

        @pl.when(ti == 0)
        def _():
            dst_sc[...] = jnp.zeros_like(dst_sc)
            dlb_sc[...] = jnp.zeros_like(dlb_sc)
            dgn_ref[...] = jnp.zeros_like(dgn_ref)

        lb = _sigmoid(lg_ref[0:1, :] - lg_ref[1:2, :])
        gn = gn_ref[...]

        def chunk(cc, carry):
            c = nct - 1 - cc
            r0 = pl.multiple_of(c * ch, ch)
            q_raw = q_ref[pl.ds(r0, ch), :]
            z = z_ref[pl.ds(r0, ch), :]
            q, k, f, sz = _hgrn_gates(q_raw, z, lb)
            v = v_ref[pl.ds(r0, ch), :]
            x = _hgrn_exponents(mst_ref[...], jnp.log(f))
            st = ss_ref[0, c]
            eb = jnp.exp(x[0:ch, :])
            ef = jnp.exp(x[ch:2 * ch, :])
            elast = jnp.exp(x[ch - 1:ch, :])
            qe = q * eb
            kl = k * ef
            a, ops = _hgrn_intra(q, k, x, msk_ref, n_lev, ch)
            v16 = v.astype(BF16)
            st16 = st.astype(BF16)
            qe16 = qe.astype(BF16)
            kl16 = kl.astype(BF16)
            a16 = a.astype(BF16)
            o = _dot_nt(qe16, st16) + _dot(a16, v16)
            g_raw = g_ref[pl.ds(r0, ch), :]
            sg = _sigmoid(g_raw)
            rstd = lax.rsqrt(jnp.mean(o * o, axis=-1, keepdims=True) + EPS)
            oh = o * rstd
            dy = dy_ref[pl.ds(r0, ch), :]
            dg_ref[pl.ds(r0, ch), :] = (dy * oh * gn * (sg * (1.0 + g_raw * (1.0 - sg)))).astype(BF16)
            don = dy * (g_raw * sg)
            dgn_ref[0] += _colsum(don * oh)
            doh = don * gn
            do = rstd * (doh - oh * jnp.mean(doh * oh, axis=-1, keepdims=True))
            do16 = do.astype(BF16)
            dst = dst_sc[...]
            dst16 = dst.astype(BF16)
            dv = _dot_tn(a16, do16) + _dot_nt(kl16, dst16)
            da = msk_ref[n_lev] * _dot_nt(do16, v16)
            dqe = _dot(do16, st16)
            dkl = _dot(v16, dst16)
            dst_sc[...] = dst * elast + _dot_tn(do16, qe16)
            db_last = _colsum(dst * st) * elast
            dad = jnp.sum(do * v, axis=1, keepdims=True)
            dq = dad * k + dqe * eb
            dk = dad * q + dkl * ef
            stk[0:ch, :] = dqe * qe
            stk[ch:2 * ch, :] = dkl * kl
            for lv in range(n_lev):
                eq, ek, qd, kd = ops[lv]
                gl = (msk_ref[lv] * da).astype(BF16)
                dqd = _dot(gl, kd.astype(BF16))
                dkd = _dot_tn(gl, qd.astype(BF16))
                dq = dq + dqd * eq
                dk = dk + dkd * ek
                stk[(2 + 2 * lv) * ch:(3 + 2 * lv) * ch, :] = dqd * qd
                stk[(3 + 2 * lv) * ch:(4 + 2 * lv) * ch, :] = dkd * kd
            sk = stk[...]
            hi = sk.astype(BF16)
            lo = (sk - hi.astype(F32)).astype(BF16)
            dl2 = _dot_tn(mst_ref[...], jnp.concatenate([hi, lo], axis=1))
            dlogf = dl2[:, 0:HEAD] + dl2[:, HEAD:2 * HEAD] + db_last
            sq = _sigmoid(q_raw)
            dq_ref[pl.ds(r0, ch), :] = (dq * (sq * (1.0 + q_raw * (1.0 - sq)))).astype(BF16)
            dfk = dlogf / f - dk
            dz_ref[pl.ds(r0, ch), :] = (dfk * (1.0 - lb) * sz * (1.0 - sz)).astype(BF16)
            dlb_sc[...] += _colsum(dfk * (1.0 - sz))
            dv_ref[pl.ds(r0, ch), :] = dv.astype(BF16)
            return carry

        lax.fori_loop(0, nct, chunk, 0)

        @pl.when(ti == nt - 1)
        def _():
            dl0 = dlb_sc[...] * lb * (1.0 - lb)
            dlg_ref[0:1, :] = dl0
            dlg_ref[1:2, :] = -dl0

    def pspec(seg):
        return pl.BlockSpec((tm, HEAD), lambda h, i: (nt - 1 - i, base + seg * HG_HEADS + h))

    ospec = pl.BlockSpec((tm, HEAD), lambda h, i: (nt - 1 - i, h))
    return pl.pallas_call(
        body, name="hgrn_bwd",
        grid=(HG_HEADS, nt),
        in_specs=[pspec(0), pspec(1), pspec(2), pspec(3), ospec,
                  pl.BlockSpec((1, nct, HEAD, HEAD), lambda h, i: (h, nt - 1 - i, 0, 0)),
                  pl.BlockSpec((2, HEAD), lambda h, i: (0, h)),
                  pl.BlockSpec((1, HEAD), lambda h, i: (0, 0)),
                  pl.BlockSpec((nr, ch), lambda h, i: (0, 0)),
                  pl.BlockSpec((n_lev + 1, ch, ch), lambda h, i: (0, 0, 0))],
        out_specs=[ospec, ospec, ospec, ospec,
                   pl.BlockSpec((2, HEAD), lambda h, i: (0, h)),
                   pl.BlockSpec((1, 1, HEAD), lambda h, i: (h, 0, 0))],
        out_shape=[jax.ShapeDtypeStruct((tp, D_HGRN), BF16)] * 4
        + [jax.ShapeDtypeStruct((2, D_HGRN), F32), jax.ShapeDtypeStruct((HG_HEADS, 1, HEAD), F32)],
        scratch_shapes=[pltpu.VMEM((HEAD, HEAD), F32), pltpu.VMEM((1, HEAD), F32), pltpu.VMEM((nr, HEAD), F32)],
        compiler_params=_cparams(2),
    )(p, p, p, p, dyb, states, lb_logits, gnorm, mst, msk)


def _conv_taps(x, w_ref, tr, halo, width):
    acc = None
    for j in range(width):
        sh = width - 1 - j
        xs = x if sh == 0 else pltpu.roll(x, sh, 0)
        term = xs[halo:, :] * w_ref[j:j + 1, :]
        acc = term if acc is None else acc + term
    return acc


def _conv_taps_t(y, w_ref, tr, halo, width):
    n = tr + halo
    acc = None
    for j in range(width):
        sh = width - 1 - j
        ys = y if sh == 0 else pltpu.roll(y, n - sh, 0)
        term = ys[0:tr, :] * w_ref[j:j + 1, :]
        acc = term if acc is None else acc + term
    return acc


def _ln_stats(cv):
    mu = jnp.mean(cv, axis=-1, keepdims=True)
    xc = cv - mu
    rstd = lax.rsqrt(jnp.mean(xc * xc, axis=-1, keepdims=True) + EPS)
    return rstd, xc * rstd


def _convmod_fwd(p, w, bias, ln_g, ln_b, tr):
    tp = p.shape[0]
    nt = tp // tr
    nb = D_CONV // HEAD

    def body(a_ref, b_ref, w_ref, bi_ref, g_ref, be_ref, y_ref, usc):
        usc[0:CONV_HALO, :] = jnp.zeros((CONV_HALO, HEAD), F32)
        usc[CONV_HALO:, :] = a_ref[...] * _sigmoid(b_ref[...])

        def tile(r, carry):
            r0 = pl.multiple_of(r * tr, SUBLANE)
            x = usc[pl.ds(r0, tr + CONV_HALO), :]
            cv = _conv_taps(x, w_ref, tr, CONV_HALO, CONV_WIDTH) + bi_ref[...]
            _, xh = _ln_stats(cv)
            un = xh * g_ref[...] + be_ref[...]
            y_ref[pl.ds(r0, tr), :] = un * _sigmoid(un)
            return carry

        lax.fori_loop(0, nt, tile, 0)

    vec = lambda: pl.BlockSpec((1, HEAD), lambda j: (0, j))
    return pl.pallas_call(
        body, name="convmod_fwd",
        grid=(nb,),
        in_specs=[pl.BlockSpec((tp, HEAD), lambda j: (0, j)), pl.BlockSpec((tp, HEAD), lambda j: (0, nb + j)),
                  pl.BlockSpec((CONV_HALO, HEAD), lambda j: (0, j)), vec(), vec(), vec()],
        out_specs=pl.BlockSpec((tp, HEAD), lambda j: (0, j)),
        out_shape=jax.ShapeDtypeStruct((tp, D_CONV), F32),
        scratch_shapes=[pltpu.VMEM((tp + CONV_HALO, HEAD), F32)],
        compiler_params=_cparams(1),
    )(p, p, w, bias, ln_g, ln_b)


def _convmod_bwd(p, dyc, w, bias, ln_g, ln_b, tr):
    tp = p.shape[0]
    nt = tp // tr
    nb = D_CONV // HEAD

    def body(a_ref, b_ref, dy_ref, w_ref, bi_ref, g_ref, be_ref, da_ref, db_ref, dw_ref, dv_ref, usc, dsc):
        usc[0:CONV_HALO, :] = jnp.zeros((CONV_HALO, HEAD), F32)
        usc[CONV_HALO:, :] = a_ref[...] * _sigmoid(b_ref[...])
        dsc[tp:, :] = jnp.zeros((CONV_HALO, HEAD), F32)
        dw_ref[...] = jnp.zeros_like(dw_ref)
        dv_ref[...] = jnp.zeros_like(dv_ref)

        def tile1(r, carry):
            r0 = pl.multiple_of(r * tr, SUBLANE)
            x = usc[pl.ds(r0, tr + CONV_HALO), :]
            cv = _conv_taps(x, w_ref, tr, CONV_HALO, CONV_WIDTH) + bi_ref[...]
            rstd, xh = _ln_stats(cv)
            un = xh * g_ref[...] + be_ref[...]
            sg = _sigmoid(un)
            dun = dy_ref[pl.ds(r0, tr), :] * (sg * (1.0 + un * (1.0 - sg)))
            dv_ref[0, 1:2, :] += _colsum(dun * xh)
            dv_ref[0, 2:3, :] += _colsum(dun)
            dxh = dun * g_ref[...]
            dcv = rstd * (dxh - jnp.mean(dxh, axis=-1, keepdims=True)
                          - xh * jnp.mean(dxh * xh, axis=-1, keepdims=True))
            dv_ref[0, 0:1, :] += _colsum(dcv)
            for j in range(CONV_WIDTH):
                sh = CONV_WIDTH - 1 - j
                xs = x if sh == 0 else pltpu.roll(x, sh, 0)
                dw_ref[0, j:j + 1, :] += _colsum(dcv * xs[CONV_HALO:, :])
            dsc[pl.ds(r0, tr), :] = dcv
            return carry

        lax.fori_loop(0, nt, tile1, 0)

        def tile2(r, carry):
            r0 = pl.multiple_of(r * tr, SUBLANE)
            y = dsc[pl.ds(r0, tr + CONV_HALO), :]
            du = _conv_taps_t(y, w_ref, tr, CONV_HALO, CONV_WIDTH)
            a = a_ref[pl.ds(r0, tr), :]
            sb = _sigmoid(b_ref[pl.ds(r0, tr), :])
            da_ref[pl.ds(r0, tr), :] = (du * sb).astype(BF16)
            db_ref[pl.ds(r0, tr), :] = (du * a * sb * (1.0 - sb)).astype(BF16)
            return carry

        lax.fori_loop(0, nt, tile2, 0)

    vec = lambda: pl.BlockSpec((1, HEAD), lambda j: (0, j))
    col = lambda: pl.BlockSpec((tp, HEAD), lambda j: (0, j))
    return pl.pallas_call(
        body, name="convmod_bwd",
        grid=(nb,),
        in_specs=[col(), pl.BlockSpec((tp, HEAD), lambda j: (0, nb + j)), col(),
                  pl.BlockSpec((CONV_HALO, HEAD), lambda j: (0, j)), vec(), vec(), vec()],
        out_specs=[col(), col(), pl.BlockSpec((1, CONV_HALO, HEAD), lambda j: (j, 0, 0)),
                   pl.BlockSpec((1, SUBLANE, HEAD), lambda j: (j, 0, 0))],
        out_shape=[jax.ShapeDtypeStruct((tp, D_CONV), BF16), jax.ShapeDtypeStruct((tp, D_CONV), BF16),
                   jax.ShapeDtypeStruct((nb, CONV_HALO, HEAD), F32), jax.ShapeDtypeStruct((nb, SUBLANE, HEAD), F32)],
        scratch_shapes=[pltpu.VMEM((tp + CONV_HALO, HEAD), F32), pltpu.VMEM((tp + CONV_HALO, HEAD), F32)],
        compiler_params=_cparams(1),
    )(p, p, dyc, w, bias, ln_g, ln_b)


def _log1p_small(y):
    return jnp.where(y < 1e-4, y * (1.0 - 0.5 * y), jnp.log(1.0 + y))


def _softplus(x):
    return jnp.maximum(x, 0.0) + _log1p_small(jnp.exp(-jnp.abs(x)))


def _expm1(x):
    return jnp.where(jnp.abs(x) < 1e-2, x * (1.0 + 0.5 * x * (1.0 + x * (1.0 / 3.0))), jnp.exp(x) - 1.0)


def _gelu_parts(x):
    c = 0.7978845608028654
    inner = c * (x + 0.044715 * x * x * x)
    th = jnp.tanh(inner)
    gelu = 0.5 * x * (1.0 + th)
    dgelu = 0.5 * (1.0 + th) + 0.5 * x * (1.0 - th * th) * c * (1.0 + 3.0 * 0.044715 * x * x)
    return gelu, dgelu


def _lru_gates(x_all, tp, cw_ref, cb_ref, wa_ref, ba_ref, wx_ref, bx_ref, lam_ref):
    u = _conv_taps(x_all, cw_ref, tp, LRU_HALO, LRU_CONV) + cb_ref[...]
    u16 = u.astype(BF16)
    r = _sigmoid(_dot(u16, wa_ref[0]) + ba_ref[...])
    i = _sigmoid(_dot(u16, wx_ref[0]) + bx_ref[...])
    sp = _softplus(-lam_ref[...])
    la = -LRU_C * r * sp
    a = jnp.exp(la)
    mult = jnp.sqrt(-_expm1(2.0 * la))
    return u, r, i, a, mult, sp


def _lru_specs(tp, nb):
    col = lambda k: pl.BlockSpec((tp, HEAD), functools.partial(lambda j, k: (0, k * nb + j), k=k))
    vec = lambda: pl.BlockSpec((1, HEAD), lambda j: (0, j))
    mat = lambda: pl.BlockSpec((1, HEAD, HEAD), lambda j: (j, 0, 0))
    return col, vec, mat


def _lru_fwd(p, cw, cb, wa, ba, wx, bx, lam):
    tp = p.shape[0]
    nb = D_LRU // HEAD
    ng = tp // SUBLANE

    def body(x_ref, gt_ref, cw_ref, cb_ref, wa_ref, ba_ref, wx_ref, bx_ref, lam_ref, y_ref, hs_ref,
             xsc, asc, bsc):
        xsc[0:LRU_HALO, :] = jnp.zeros((LRU_HALO, HEAD), F32)
        xsc[LRU_HALO:, :] = x_ref[...]
        u, r, i, a, mult, _ = _lru_gates(xsc[...], tp, cw_ref, cb_ref, wa_ref, ba_ref, wx_ref, bx_ref, lam_ref)
        rows = lax.broadcasted_iota(jnp.int32, (tp, HEAD), 0)
        b = jnp.where(rows == 0, 1.0, mult) * (i * u)
        sub = rows % SUBLANE
        for k in (1, 2, 4):
            m = sub >= k
            b = jnp.where(m, a * pltpu.roll(b, k, 0) + b, b)
            a = jnp.where(m, a * pltpu.roll(a, k, 0), a)
        asc[...] = a
        bsc[...] = b

        def grp(g, carry):
            r0 = pl.multiple_of(g * SUBLANE, SUBLANE)
            h = bsc[pl.ds(r0, SUBLANE), :] + asc[pl.ds(r0, SUBLANE), :] * carry
            hs_ref[pl.ds(r0, SUBLANE), :] = h
            return jnp.broadcast_to(h[SUBLANE - 1:SUBLANE, :], (SUBLANE, HEAD))

        lax.fori_loop(0, ng, grp, jnp.zeros((SUBLANE, HEAD), F32))
        gelu, _ = _gelu_parts(gt_ref[...])
        y_ref[...] = gelu * hs_ref[...]

    col, vec, mat = _lru_specs(tp, nb)
    return pl.pallas_call(
        body, name="lru_fwd",
        grid=(nb,),
        in_specs=[col(2), col(3), pl.BlockSpec((LRU_CONV, HEAD), lambda j: (0, j)), vec(), mat(), vec(), mat(),
                  vec(), vec()],
        out_specs=[pl.BlockSpec((tp, HEAD), lambda j: (0, j)), pl.BlockSpec((tp, HEAD), lambda j: (0, j))],
        out_shape=[jax.ShapeDtypeStruct((tp, D_LRU), F32), jax.ShapeDtypeStruct((tp, D_LRU), F32)],
        scratch_shapes=[pltpu.VMEM((tp + LRU_HALO, HEAD), F32), pltpu.VMEM((tp, HEAD), F32),
                        pltpu.VMEM((tp, HEAD), F32)],
        compiler_params=_cparams(1),
    )(p, p, cw, cb, wa, ba, wx, bx, lam)


def _lru_bwd(p, hs, dyd, cw, cb, wa, ba, wx, bx, lam):
    tp = p.shape[0]
    nb = D_LRU // HEAD
    ng = tp // SUBLANE

    def body(x_ref, gt_ref, hs_ref, dy_ref, cw_ref, cb_ref, wa_ref, ba_ref, wx_ref, bx_ref, lam_ref,
             dx_ref, dgt_ref, dwa_ref, dwx_ref, dv_ref, xsc, asc, bsc, gsc, dusc):
        xsc[0:LRU_HALO, :] = jnp.zeros((LRU_HALO, HEAD), F32)
        xsc[LRU_HALO:, :] = x_ref[...]
        x_all = xsc[...]
        u, r, i, a, mult, sp = _lru_gates(x_all, tp, cw_ref, cb_ref, wa_ref, ba_ref, wx_ref, bx_ref, lam_ref)
        rows = lax.broadcasted_iota(jnp.int32, (tp, HEAD), 0)
        hs = hs_ref[...]
        dy = dy_ref[...]
        gelu, dgelu = _gelu_parts(gt_ref[...])
        dgt_ref[...] = (dy * hs * dgelu).astype(BF16)
        bb = dy * gelu
        aa = jnp.where(rows == tp - 1, 0.0, pltpu.roll(a, tp - 1, 0))
        sub = rows % SUBLANE
        for k in (1, 2, 4):
            m = sub < SUBLANE - k
            bb = jnp.where(m, aa * pltpu.roll(bb, tp - k, 0) + bb, bb)
            aa = jnp.where(m, aa * pltpu.roll(aa, tp - k, 0), aa)
        asc[...] = aa
        bsc[...] = bb

        def grp(gi, carry):
            g = ng - 1 - gi
            r0 = pl.multiple_of(g * SUBLANE, SUBLANE)
            gg = bsc[pl.ds(r0, SUBLANE), :] + asc[pl.ds(r0, SUBLANE), :] * carry
            gsc[pl.ds(r0, SUBLANE), :] = gg
            return jnp.broadcast_to(gg[0:1, :], (SUBLANE, HEAD))

        lax.fori_loop(0, ng, grp, jnp.zeros((SUBLANE, HEAD), F32))
        g = gsc[...]
        first = rows == 0
        hprev = jnp.where(first, 0.0, pltpu.roll(hs, 1, 0))
        iu = i * u
        d_iu = g * jnp.where(first, 1.0, mult)
        dmult_term = jnp.where(first, 0.0, g * iu * (-(a * a) / mult))
        dla = g * hprev * a + dmult_term
        dr = dla * (-LRU_C) * sp
        dv_ref[0, 7:8, :] = _colsum(dla * (LRU_C * r) * _sigmoid(-lam_ref[...]))
        dpr = dr * r * (1.0 - r)
        dpi = d_iu * u * i * (1.0 - i)
        dv_ref[0, 5:6, :] = _colsum(dpr)
        dv_ref[0, 6:7, :] = _colsum(dpi)
        u16 = u.astype(BF16)
        dpr16 = dpr.astype(BF16)
        dpi16 = dpi.astype(BF16)
        dwa_ref[0] = _dot_tn(u16, dpr16)
        dwx_ref[0] = _dot_tn(u16, dpi16)
        du = d_iu * i + _dot_nt(dpr16, wa_ref[0]) + _dot_nt(dpi16, wx_ref[0])
        dv_ref[0, 4:5, :] = _colsum(du)
        for j in range(LRU_CONV):
            sh = LRU_CONV - 1 - j
            xs = x_all if sh == 0 else pltpu.roll(x_all, sh, 0)
            dv_ref[0, j:j + 1, :] = _colsum(du * xs[LRU_HALO:, :])
        dusc[0:tp, :] = du
        dusc[tp:, :] = jnp.zeros((LRU_HALO, HEAD), F32)
        dx_ref[...] = _conv_taps_t(dusc[...], cw_ref, tp, LRU_HALO, LRU_CONV).astype(BF16)

    col, vec, mat = _lru_specs(tp, nb)
    ocol = lambda: pl.BlockSpec((tp, HEAD), lambda j: (0, j))
    return pl.pallas_call(
        body, name="lru_bwd",
        grid=(nb,),
        in_specs=[col(2), col(3), ocol(), ocol(), pl.BlockSpec((LRU_CONV, HEAD), lambda j: (0, j)), vec(), mat(),
                  vec(), mat(), vec(), vec()],
        out_specs=[ocol(), ocol(), mat(), mat(), pl.BlockSpec((1, SUBLANE, HEAD), lambda j: (j, 0, 0))],
        out_shape=[jax.ShapeDtypeStruct((tp, D_LRU), BF16), jax.ShapeDtypeStruct((tp, D_LRU), BF16),
                   jax.ShapeDtypeStruct((nb, HEAD, HEAD), F32), jax.ShapeDtypeStruct((nb, HEAD, HEAD), F32),
                   jax.ShapeDtypeStruct((nb, SUBLANE, HEAD), F32)],
        scratch_shapes=[pltpu.VMEM((tp + LRU_HALO, HEAD), F32), pltpu.VMEM((tp, HEAD), F32),
                        pltpu.VMEM((tp, HEAD), F32), pltpu.VMEM((tp, HEAD), F32),
                        pltpu.VMEM((tp + LRU_HALO, HEAD), F32)],
        compiler_params=_cparams(1),
    )(p, p, hs, dyd, cw, cb, wa, ba, wx, bx, lam)


def _mesh_pos():
    return lax.axis_index("x"), lax.axis_index("y"), lax.axis_index("c")


def _other_chips(x, y):
    return [(1 - x, y), (x, 1 - y), (1 - x, 1 - y)]


ANY = pl.BlockSpec(memory_space=pl.ANY)


def _allgather_shards(arrs, split):
    n = len(arrs)

    def body(*refs):
        ins, outs = refs[:n], refs[n:2 * n]
        send1, recv1, send2, recv2, send3, recv3 = refs[2 * n:]
        x, y, c = _mesh_pos()
        chip = 2 * x + y
        sibling = (x, y, 1 - c)
        others = _other_chips(x, y)

        def rows(k, cc):
            half = arrs[k].shape[0] // 2
            return pl.ds(cc * half, half)

        def remote(src, dst, ssem, rsem, dev):
            return pltpu.make_async_remote_copy(src_ref=src, dst_ref=dst, send_sem=ssem, recv_sem=rsem,
                                                device_id=dev, device_id_type=MESH)

        started = [remote(ins[k], outs[k].at[chip], send3.at[k], recv3.at[k], sibling) for k in range(n)]
        for cp in started:
            cp.start()
        for k in range(n):
            for j, (ox, oy) in enumerate(others):
                if split[k]:
                    src, dst = ins[k].at[rows(k, c)], outs[k].at[chip, rows(k, c)]
                else:
                    src, dst = ins[k], outs[k].at[chip]
                cp = remote(src, dst, send1.at[3 * k + j], recv1.at[3 * k + j], (ox, oy, c))
                cp.start()
                started.append(cp)
        for j, (ox, oy) in enumerate(others):
            ochip = 2 * ox + oy
            for k in range(n):
                if split[k]:
                    blk = outs[k].at[ochip, rows(k, c)]
                    remote(blk, blk, send1.at[3 * k + j], recv1.at[3 * k + j], sibling).wait_recv()
                    cp = remote(blk, blk, send2.at[3 * k + j], recv2.at[3 * k + j], sibling)
                    cp.start()
                    started.append(cp)
                else:
                    blk = outs[k].at[ochip]
                    remote(blk, blk, send1.at[3 * k + j], recv1.at[3 * k + j], sibling).wait_recv()
        for j, (ox, oy) in enumerate(others):
            ochip = 2 * ox + oy
            for k in range(n):
                if split[k]:
                    blk = outs[k].at[ochip, rows(k, 1 - c)]
                    remote(blk, blk, send2.at[3 * k + j], recv2.at[3 * k + j], sibling).wait_recv()
        for k in range(n):
            blk = outs[k].at[chip]
            remote(blk, blk, send3.at[k], recv3.at[k], sibling).wait_recv()
        for cp in started:
            cp.wait_send()

    return pl.pallas_call(
        body, name="allgather_shards",
        in_specs=[ANY] * n, out_specs=[ANY] * n,
        out_shape=[jax.ShapeDtypeStruct((N_SHARD,) + a.shape, a.dtype) for a in arrs],
        scratch_shapes=[pltpu.SemaphoreType.DMA((3 * n,)), pltpu.SemaphoreType.DMA((3 * n,)),
                        pltpu.SemaphoreType.DMA((3 * n,)), pltpu.SemaphoreType.DMA((3 * n,)),
                        pltpu.SemaphoreType.DMA((n,)), pltpu.SemaphoreType.DMA((n,))],
    )(*arrs)


HBM = pl.BlockSpec(memory_space=pltpu.HBM)
SEM = pl.BlockSpec(memory_space=pltpu.SEMAPHORE)
DATAFLOW = pltpu.SideEffectType.DATAFLOW_SIDE_EFFECTING
N_PEERS = 4


def _in_hbm(a):
    return pltpu.with_memory_space_constraint(a, pltpu.HBM)


def _gather_peers(x, y, c):
    return [((ox, oy, c), 2 * ox + oy) for ox, oy in _other_chips(x, y)] + [((x, y, 1 - c), 2 * x + y)]


def _gather_start(arrs):
    n = len(arrs)

    def body(*refs):
        ins, lands = refs[:n], refs[n:2 * n]
        ssem, rsem = refs[2 * n:2 * n + 2]
        token = refs[-1]
        x, y, c = _mesh_pos()
        chip = 2 * x + y
        for k in range(n):
            for j, (dev, _) in enumerate(_gather_peers(x, y, c)):
                pltpu.make_async_remote_copy(
                    src_ref=ins[k], dst_ref=lands[k].at[chip], send_sem=ssem.at[N_PEERS * k + j],
                    recv_sem=rsem.at[N_PEERS * k + j], device_id=dev, device_id_type=MESH).start()
        token[...] = jnp.zeros_like(token)

    lands = [_in_hbm(lax.empty((N_SHARD,) + a.shape, a.dtype)) for a in arrs]
    out = pl.pallas_call(
        body, name="gather_start",
        in_specs=[HBM] * (2 * n),
        out_specs=[SEM, SEM] + [HBM] * (2 * n) + [pl.BlockSpec(memory_space=pltpu.VMEM)],
        out_shape=[pltpu.SemaphoreType.DMA((N_PEERS * n,)), pltpu.SemaphoreType.DMA((N_PEERS * n,))]
        + [pltpu.HBM(a.shape, a.dtype) for a in arrs]
        + [pltpu.HBM((N_SHARD,) + a.shape, a.dtype) for a in arrs]
        + [jax.ShapeDtypeStruct((SUBLANE, LANE), F32)],
        input_output_aliases={k: 2 + k for k in range(2 * n)},
        compiler_params=pltpu.CompilerParams(has_side_effects=DATAFLOW),
    )(*[_in_hbm(a) for a in arrs], *lands)
    return out[0], out[1], list(out[2:2 + n]), list(out[2 + n:2 + 2 * n]), out[-1]


def _gather_wait(ssem, rsem, srcs, lands, ks, after):
    n = len(ks)

    def body(*refs):
        ins, lnd = refs[:n], refs[n:2 * n]
        ssem_ref, rsem_ref = refs[2 * n:2 * n + 2]
        x, y, c = _mesh_pos()
        for i, k in enumerate(ks):
            for j, (dev, pchip) in enumerate(_gather_peers(x, y, c)):
                cp = pltpu.make_async_remote_copy(
                    src_ref=ins[i], dst_ref=lnd[i].at[pchip], send_sem=ssem_ref.at[N_PEERS * k + j],
                    recv_sem=rsem_ref.at[N_PEERS * k + j], device_id=dev, device_id_type=MESH)
                cp.wait_send()
                cp.wait_recv()

    out = pl.pallas_call(
        body, name="gather_wait",
        in_specs=[HBM] * (2 * n) + [SEM, SEM, ANY],
        out_specs=[HBM] * (2 * n),
        out_shape=[pltpu.HBM(a.shape, a.dtype) for a in srcs] + [pltpu.HBM(a.shape, a.dtype) for a in lands],
        input_output_aliases={k: k for k in range(2 * n)},
        compiler_params=pltpu.CompilerParams(has_side_effects=DATAFLOW),
    )(*srcs, *lands, ssem, rsem, after)
    return list(out[n:])


def _scatter_start(arrs, slots):
    n = len(arrs)

    def body(*refs):
        ins, lands = refs[:n], refs[n:2 * n]
        ssem, rsem = refs[2 * n:2 * n + 2]
        token = refs[-1]
        x, y, c = _mesh_pos()
        chip = 2 * x + y
        for k in range(n):
            for j, (ox, oy) in enumerate(_other_chips(x, y)):
                pltpu.make_async_remote_copy(
                    src_ref=ins[k].at[2 * ox + oy], dst_ref=lands[k].at[chip], send_sem=ssem.at[3 * k + j],
                    recv_sem=rsem.at[3 * k + j], device_id=(ox, oy, c), device_id_type=MESH).start()
        token[...] = jnp.zeros_like(token)

    out = pl.pallas_call(
        body, name="scatter_start",
        in_specs=[HBM] * (2 * n),
        out_specs=[SEM, SEM] + [HBM] * (2 * n) + [pl.BlockSpec(memory_space=pltpu.VMEM)],
        out_shape=[pltpu.SemaphoreType.DMA((3 * n,)), pltpu.SemaphoreType.DMA((3 * n,))]
        + [pltpu.HBM(a.shape, a.dtype) for a in arrs] + [pltpu.HBM(a.shape, a.dtype) for a in slots]
        + [jax.ShapeDtypeStruct((SUBLANE, LANE), F32)],
        input_output_aliases={k: 2 + k for k in range(2 * n)},
        compiler_params=pltpu.CompilerParams(has_side_effects=DATAFLOW),
    )(*[_in_hbm(a) for a in arrs], *[_in_hbm(a) for a in slots])
    return out[0], out[1], list(out[2:2 + n]), list(out[2 + n:2 + 2 * n])


def _scatter_wait(ssem, rsem, arrs, slots, after):
    n = len(arrs)

    def body(*refs):
        ins, lnd = refs[:n], refs[n:2 * n]
        ssem_ref, rsem_ref = refs[2 * n:2 * n + 2]
        x, y, c = _mesh_pos()
        for k in range(n):
            for j, (ox, oy) in enumerate(_other_chips(x, y)):
                ochip = 2 * ox + oy
                cp = pltpu.make_async_remote_copy(
                    src_ref=ins[k].at[ochip], dst_ref=lnd[k].at[ochip], send_sem=ssem_ref.at[3 * k + j],
                    recv_sem=rsem_ref.at[3 * k + j], device_id=(ox, oy, c), device_id_type=MESH)
                cp.wait_send()
                cp.wait_recv()

    out = pl.pallas_call(
        body, name="scatter_wait",
        in_specs=[HBM] * (2 * n) + [SEM, SEM, ANY],
        out_specs=[HBM] * (2 * n),
        out_shape=[pltpu.HBM(a.shape, a.dtype) for a in arrs] + [pltpu.HBM(a.shape, a.dtype) for a in slots],
        input_output_aliases={k: k for k in range(2 * n)},
        compiler_params=pltpu.CompilerParams(has_side_effects=DATAFLOW),
    )(*arrs, *slots, ssem, rsem, after)
    return list(out[n:])


def _pair_exchange_halves(arrs):
    n = len(arrs)

    def body(*refs):
        ins, outs = refs[:n], refs[n:2 * n]
        ssem, rsem = refs[2 * n:]
        x, y, c = _mesh_pos()
        cps = []
        for k in range(n):
            half = arrs[k].shape[1] // 2
            cp = pltpu.make_async_remote_copy(
                src_ref=ins[k].at[:, pl.ds((1 - c) * half, half)], dst_ref=outs[k],
                send_sem=ssem.at[k], recv_sem=rsem.at[k], device_id=(x, y, 1 - c), device_id_type=MESH)
            cp.start()
            cps.append(cp)
        for cp in cps:
            cp.wait()

    return pl.pallas_call(
        body, name="pair_exchange_halves",
        in_specs=[ANY] * n, out_specs=[ANY] * n,
        out_shape=[jax.ShapeDtypeStruct((a.shape[0], a.shape[1] // 2, a.shape[2]), a.dtype) for a in arrs],
        scratch_shapes=[pltpu.SemaphoreType.DMA((n,)), pltpu.SemaphoreType.DMA((n,))],
    )(*arrs)


GRAD_ROW_BLOCKS = 2


def _pair_add(arrs, recvd, core):
    n = len(arrs)
    nb = GRAD_ROW_BLOCKS

    def body(c_ref, *refs):
        for k in range(n):
            refs[2 * n + k][...] = (refs[k][...].astype(F32) + refs[n + k][...].astype(F32)).astype(BF16)

    def blk(a):
        return (1, a.shape[1] // 2 // nb, a.shape[2])

    grid_spec = pltpu.PrefetchScalarGridSpec(
        num_scalar_prefetch=1, grid=(N_SHARD, nb),
        in_specs=[pl.BlockSpec(blk(a), lambda s, i, c: (s, c[0] * nb + i, 0)) for a in arrs]
        + [pl.BlockSpec(blk(a), lambda s, i, c: (s, i, 0)) for a in arrs],
        out_specs=[pl.BlockSpec(blk(a), lambda s, i, c: (s, i, 0)) for a in arrs])
    return pl.pallas_call(
        body, name="pair_add", grid_spec=grid_spec,
        out_shape=[jax.ShapeDtypeStruct(r.shape, BF16) for r in recvd],
        compiler_params=_cparams(2),
    )(core, *arrs, *recvd)


def _own_slot(arrs, chip):
    n = len(arrs)
    nb = GRAD_ROW_BLOCKS

    def body(c_ref, *refs):
        for k in range(n):
            refs[n + k][...] = refs[k][...]

    def blk(a):
        return (1, a.shape[1] // nb, a.shape[2])

    grid_spec = pltpu.PrefetchScalarGridSpec(
        num_scalar_prefetch=1, grid=(nb,),
        in_specs=[pl.BlockSpec(blk(a), lambda i, c: (c[0], i, 0)) for a in arrs],
        out_specs=[pl.BlockSpec(blk(a), lambda i, c: (c[0], i, 0)) for a in arrs])
    return pl.pallas_call(
        body, name="own_slot", grid_spec=grid_spec,
        out_shape=[jax.ShapeDtypeStruct(a.shape, a.dtype) for a in arrs],
        compiler_params=_cparams(1),
    )(chip, *arrs)


def _scatter_to_owners(arrs, slots):
    n = len(arrs)

    def body(*refs):
        ins, outs = refs[:n], refs[2 * n:3 * n]
        ssem, rsem = refs[3 * n:]
        x, y, c = _mesh_pos()
        chip = 2 * x + y
        others = _other_chips(x, y)
        cps = []
        for k in range(n):
            for j, (ox, oy) in enumerate(others):
                cp = pltpu.make_async_remote_copy(
                    src_ref=ins[k].at[2 * ox + oy], dst_ref=outs[k].at[chip],
                    send_sem=ssem.at[3 * k + j], recv_sem=rsem.at[3 * k + j],
                    device_id=(ox, oy, c), device_id_type=MESH)
                cp.start()
                cps.append(cp)
        for k in range(n):
            for j, (ox, oy) in enumerate(others):
                blk = outs[k].at[2 * ox + oy]
                pltpu.make_async_remote_copy(
                    src_ref=blk, dst_ref=blk, send_sem=ssem.at[3 * k + j], recv_sem=rsem.at[3 * k + j],
                    device_id=(ox, oy, c), device_id_type=MESH).wait_recv()
        for cp in cps:
            cp.wait_send()

    return pl.pallas_call(
        body, name="scatter_to_owners",
        in_specs=[ANY] * (2 * n), out_specs=[ANY] * n,
        out_shape=[jax.ShapeDtypeStruct(a.shape, a.dtype) for a in arrs],
        scratch_shapes=[pltpu.SemaphoreType.DMA((3 * n,)), pltpu.SemaphoreType.DMA((3 * n,))],
        input_output_aliases={n + k: k for k in range(n)},
    )(*arrs, *slots)


def _sum_chips(arrs, core):
    n = len(arrs)
    nb = GRAD_ROW_BLOCKS

    def body(c_ref, *refs):
        for k in range(n):
            r = refs[k]
            refs[n + k][...] = ((r[0].astype(F32) + r[1].astype(F32)) + r[2].astype(F32)) + r[3].astype(F32)

    grid_spec = pltpu.PrefetchScalarGridSpec(
        num_scalar_prefetch=1, grid=(nb,),
        in_specs=[pl.BlockSpec((N_SHARD, a.shape[1] // nb, a.shape[2]), lambda i, c: (0, i, 0)) for a in arrs],
        out_specs=[pl.BlockSpec((a.shape[1] // nb, a.shape[2]), lambda i, c: (c[0] * nb + i, 0)) for a in arrs])
    return pl.pallas_call(
        body, name="sum_chips", grid_spec=grid_spec,
        out_shape=[jax.ShapeDtypeStruct((2 * a.shape[1], a.shape[2]), F32) for a in arrs],
        compiler_params=_cparams(1),
    )(core, *arrs)


def _pair_allgather_halves(arrs):
    n = len(arrs)

    def body(*refs):
        outs = refs[n:2 * n]
        ssem, rsem = refs[2 * n:]
        x, y, c = _mesh_pos()
        cps = []
        for k in range(n):
            h = arrs[k].shape[0] // 2
            mine = outs[k].at[pl.ds(c * h, h)]
            cp = pltpu.make_async_remote_copy(src_ref=mine, dst_ref=mine, send_sem=ssem.at[k],
                                              recv_sem=rsem.at[k], device_id=(x, y, 1 - c), device_id_type=MESH)
            cp.start()
            cps.append(cp)
        for k, cp in enumerate(cps):
            h = arrs[k].shape[0] // 2
            theirs = outs[k].at[pl.ds((1 - c) * h, h)]
            pltpu.make_async_remote_copy(src_ref=theirs, dst_ref=theirs, send_sem=ssem.at[k], recv_sem=rsem.at[k],
                                         device_id=(x, y, 1 - c), device_id_type=MESH).wait_recv()
            cp.wait_send()

    return pl.pallas_call(
        body, name="pair_allgather_halves",
        in_specs=[ANY] * n, out_specs=[ANY] * n,
        out_shape=[jax.ShapeDtypeStruct(a.shape, a.dtype) for a in arrs],
        scratch_shapes=[pltpu.SemaphoreType.DMA((n,)), pltpu.SemaphoreType.DMA((n,))],
        input_output_aliases={k: k for k in range(n)},
    )(*arrs)


N_DEV = 8


def _allgather_all(v):
    m_per, n = v.shape

    def body(x_ref, out_ref, send_sems, recv_sems, local_sem):
        x, y, c = _mesh_pos()
        me, sibling = (x, y, c), (x, y, 1 - c)
        chips = _other_chips(x, y)

        def rows(px, py, pc):
            return out_ref.at[pl.ds((4 * px + 2 * py + pc) * m_per, m_per), :]

        def copy(k, block, to, src=None):
            return pltpu.make_async_remote_copy(
                src_ref=rows(*block) if src is None else src, dst_ref=rows(*block),
                send_sem=send_sems.at[k], recv_sem=recv_sems.at[k], device_id=to, device_id_type=MESH)

        mine = pltpu.make_async_copy(x_ref, rows(*me), local_sem)
        mine.start()
        first = [copy(0, me, sibling, src=x_ref)]
        first += [copy(1 + j, me, (*chip, c), src=x_ref) for j, chip in enumerate(chips)]
        for cp in first:
            cp.start()
        passed = [copy(4 + j, (*chip, c), sibling) for j, chip in enumerate(chips)]
        for j, chip in enumerate(chips):
            copy(1 + j, (*chip, c), me).wait_recv()
            passed[j].start()
        copy(0, sibling, me).wait_recv()
        for j, chip in enumerate(chips):
            copy(4 + j, (*chip, 1 - c), me).wait_recv()
        for cp in first + passed:
            cp.wait_send()
        mine.wait()

    return pl.pallas_call(
        body, name="allgather_all",
        out_shape=jax.ShapeDtypeStruct((N_DEV * m_per, n), v.dtype),
        in_specs=[pl.BlockSpec(memory_space=pltpu.VMEM)],
        out_specs=pl.BlockSpec(memory_space=pltpu.VMEM),
        scratch_shapes=[pltpu.SemaphoreType.DMA((7,)), pltpu.SemaphoreType.DMA((7,)), pltpu.SemaphoreType.DMA],
        compiler_params=pltpu.CompilerParams(vmem_limit_bytes=VMEM_LIMIT_MB * 1024 * 1024),
    )(v)


def _adamw_math(w, g, m, v):
    m2 = ADAM_B1 * m + (1.0 - ADAM_B1) * g
    v2 = ADAM_B2 * v + (1.0 - ADAM_B2) * (g * g)
    m_hat = m2 / (1.0 - ADAM_B1 ** ADAM_STEP)
    v_hat = v2 / (1.0 - ADAM_B2 ** ADAM_STEP)
    delta = -ADAM_LR * (m_hat / (jnp.sqrt(v_hat) + ADAM_EPS) + ADAM_WD * w)
    return delta, m2, v2


def _adamw(w, m, v, gs, nblk):
    nl, r, n = w.shape
    assert nl == len(gs) and nl in (1, 2)
    br = r // nblk

    def body(w_ref, m_ref, v_ref, *rest):
        g_refs, (go_ref, d_ref, mo_ref, vo_ref) = rest[:nl], rest[nl:]
        g = g_refs[0][...]
        if nl == 2:
            g = jnp.where(pl.program_id(0) == 0, g, g_refs[1][...])
        delta, m2, v2 = _adamw_math(w_ref[0], g, m_ref[0], v_ref[0])
        go_ref[0] = g
        d_ref[0] = delta
        mo_ref[0] = m2
        vo_ref[0] = v2

    spec = pl.BlockSpec((1, br, n), lambda l, i: (l, i, 0))
    g_specs = [pl.BlockSpec((br, n), lambda l, i: (i, 0))] if nl == 1 else [
        pl.BlockSpec((br, n), lambda l, i: (jnp.where(l == 0, i, nblk - 1), 0)),
        pl.BlockSpec((br, n), lambda l, i: (jnp.where(l == 1, i, 0), 0))]
    return pl.pallas_call(
        body, name="adamw", grid=(nl, nblk),
        in_specs=[spec, spec, spec] + g_specs,
        out_specs=[spec] * 4,
        out_shape=[jax.ShapeDtypeStruct((nl, r, n), F32)] * 4,
        compiler_params=_cparams(2),
    )(w, m, v, *gs)


def _small_reduce_adamw(parts, w, m, v, rep_rows, sh_rows):
    mrows = rep_rows + N_SHARD * sh_rows

    def body(p_ref, w_ref, m_ref, v_ref, go_ref, d_ref, mo_ref, vo_ref):
        x, y, _ = _mesh_pos()
        mine = rep_rows + (2 * x + y) * sh_rows
        g_rep = p_ref[0:rep_rows, :]
        g_sh = p_ref[pl.ds(pl.multiple_of(mine, SUBLANE), sh_rows), :]
        for k in range(1, N_DEV):
            g_rep = g_rep + p_ref[k * mrows:k * mrows + rep_rows, :]
            g_sh = g_sh + p_ref[pl.ds(pl.multiple_of(k * mrows + mine, SUBLANE), sh_rows), :]
        g = jnp.concatenate([g_rep, g_sh], axis=0)
        delta, m2, v2 = _adamw_math(w_ref[...], g, m_ref[...], v_ref[...])
        go_ref[...] = g
        d_ref[...] = delta
        mo_ref[...] = m2
        vo_ref[...] = v2

    return pl.pallas_call(
        body, name="small_reduce_adamw",
        out_shape=[jax.ShapeDtypeStruct((rep_rows + sh_rows, 128), F32)] * 4,
        compiler_params=pltpu.CompilerParams(vmem_limit_bytes=VMEM_LIMIT_MB * 1024 * 1024),
    )(parts, w, m, v)


LANE = 128
REP_SPEC = (("ffn1_norm", 16), ("mix_norm", 16), ("ffn2_norm", 16), ("final_norm", 8), ("pool_w", 128),
            ("pool_scale", 8), ("hgrn_lb_logits", 16), ("hgrn_gnorm", 8), ("lru_wa", 256), ("lru_wx", 256))
SH_SPEC = (("meta_tokens", 32), ("conv_w", 32), ("lru_conv_w", 8), ("conv_b", 8), ("conv_ln_g", 8),
           ("conv_ln_b", 8), ("lru_conv_b", 8), ("lru_ba", 8), ("lru_bx", 8), ("lru_lambda", 8))
REP_ROWS = sum(r for _, r in REP_SPEC)
SH_ROWS = sum(r for _, r in SH_SPEC)


def _pack_rows(vals, spec):
    parts = []
    for name, rows in spec:
        flat = vals[name].astype(F32).reshape(-1, LANE)
        if flat.shape[0] < rows:
            flat = jnp.concatenate([flat, jnp.zeros((rows - flat.shape[0], LANE), F32)], axis=0)
        parts.append(flat)
    return jnp.concatenate(parts, axis=0)


def _unpack_rows(packed, spec, shapes):
    out = {}
    off = 0
    for name, rows in spec:
        shp = shapes[name]
        n = int(np.prod(shp)) // LANE
        out[name] = packed[off:off + n].reshape(shp)
        off += rows
    return out


def _block_diag(blocks):
    n, b, _ = blocks.shape
    return sum(jnp.pad(blocks[g], ((g * b, (n - 1 - g) * b), (g * b, (n - 1 - g) * b))) for g in range(n))


def _diag_blocks(mat, n):
    b = mat.shape[0] // n
    return jnp.stack([mat[g * b:(g + 1) * b, g * b:(g + 1) * b] for g in range(n)])


BIG = ("ffn1_wg", "ffn1_wu", "ffn2_wg", "ffn2_wu", "ffn1_wd", "ffn2_wd", "w_in_even", "w_out_even",
       "w_in_odd", "w_out_odd")
ADAM_BLOCKS = {"ffn1_wg": 8, "ffn1_wu": 8, "ffn2_wg": 8, "ffn2_wu": 8, "ffn1_wd": 4, "ffn2_wd": 4,
               "w_in_even": 4, "w_out_even": 2, "w_in_odd": 4, "w_out_odd": 2}
WEIGHT_NAMES = ('meta_tokens', 'ffn1_norm', 'ffn1_wg', 'ffn1_wu', 'ffn1_wd', 'mix_norm', 'ffn2_norm', 'ffn2_wg',
                'ffn2_wu', 'ffn2_wd', 'w_in_even', 'pool_w', 'pool_scale', 'hgrn_lb_logits', 'hgrn_gnorm',
                'w_out_even', 'w_in_odd', 'conv_w', 'conv_b', 'conv_ln_g', 'conv_ln_b', 'lru_conv_w',
                'lru_conv_b', 'lru_wa', 'lru_ba', 'lru_wx', 'lru_bx', 'lru_lambda', 'w_out_odd', 'final_norm')


def _rows2d(a):
    return a.reshape(-1, a.shape[-1])


def _local_step_v2(x, tgt, w, gathered, small_full):
    s_len, d = x.shape
    t_real = s_len + N_META
    tp = -(-t_real // ROW_ALIGN) * ROW_ALIGN
    tm = _tile(tp, 832, ROW_ALIGN)
    tm_small = _tile(tp, 416, 16)
    tr = _tile(tp, 416, SUBLANE)
    f1 = ("ffn1_norm", "ffn1_wg", "ffn1_wu", "ffn1_wd")
    f2 = ("ffn2_norm", "ffn2_wg", "ffn2_wu", "ffn2_wd")

    meta_full = small_full["meta_tokens"]
    h0 = jnp.concatenate([meta_full, x, jnp.zeros((tp - t_real, d), F32)], axis=0)
    tgt_pad = jnp.concatenate([jnp.zeros((N_META, d), F32), tgt, jnp.zeros((tp - t_real, d), F32)], axis=0)

    w_in_even = jnp.transpose(gathered["w_in_even"], (1, 0, 2)).reshape(d, D_IN_EVEN)
    w_out_even = gathered["w_out_even"].reshape(d, d)
    w_out_odd = gathered["w_out_odd"].reshape(d, d)
    even_piece = [(w_in_even, (d, D_IN_EVEN), (0, 0))]
    odd_pieces = [(gathered["w_in_odd"], (1, d, D_IN_ODD // N_SHARD), (k, 0, 0)) for k in range(N_SHARD)]
    pool_wbd = _block_diag(w["pool_w"][0]).astype(BF16)
    pool_scale = w["pool_scale"]
    wa_bd = _block_diag2(w["lru_wa"][0]).astype(BF16)
    wx_bd = _block_diag2(w["lru_wx"][0]).astype(BF16)
    mst, msk, n_lev = _hgrn_consts(HG_CHUNK)
    conv_w = small_full["conv_w"]
    sf = small_full

    def gain(name, layer):
        return w[name][layer:layer + 1]

    def ffn(h, names, layer):
        return _ffn_fwd(h, gain(names[0], layer), gathered[names[1]], gathered[names[2]], gathered[names[3]],
                        layer, tm)

    h1, a1, b1, n1 = ffn(h0, f1, 0)
    p0, nm0 = _proj_fwd(h1, gain("mix_norm", 0), 0, even_piece, tm_small)
    ya = _pool_fwd(p0, pool_wbd, pool_scale, tr)
    yb, states = _hgrn_fwd(p0, w["hgrn_lb_logits"], w["hgrn_gnorm"], mst, msk, n_lev, tm)
    h2 = _out_fwd(h1, ya, yb, w_out_even, tm)
    h3, a2, b2, n2 = ffn(h2, f2, 0)
    h4, a3, b3, n3 = ffn(h3, f1, 1)
    p1, nm1 = _proj_fwd(h4, gain("mix_norm", 1), 1, odd_pieces, tm_small)
    yc = _convmod_fwd(p1, conv_w, sf["conv_b"], sf["conv_ln_g"], sf["conv_ln_b"], tr)
    lru_args = (sf["lru_conv_w"], sf["lru_conv_b"], wa_bd, sf["lru_ba"], wx_bd, sf["lru_bx"], sf["lru_lambda"])
    yd, hs = _lru_fwd(p1, *lru_args)
    h5 = _out_fwd(h4, yc, yd, w_out_odd, tm)
    h6, a4, b4, n4 = ffn(h5, f2, 1)
    loss, dh6, dg_final = _loss_bwd(h6, w["final_norm"].reshape(1, d), tgt_pad, t_real, tm)

    def ffn_bwd(dho, h, n, a, b, names, layer, acc):
        dh, da, db, dg = _ffn_bwd_act(dho, h, gain(names[0], layer), a, b, gathered[names[1]], gathered[names[2]],
                                      gathered[names[3]], layer, tm_small)
        acc = _ffn_bwd_w(dho, n, a, b, da, db, acc[0], acc[1], acc[2], layer, tm)
        return dh, dg, acc

    none3 = (None, None, None)
    dh5, dg_f2_l1, g_f2 = ffn_bwd(dh6, h5, n4, a4, b4, f2, 1, none3)
    dyc, dyd, dw_out_odd = _out_bwd(dh5, yc, yd, w_out_odd, tm)
    dca, dcb, dconv_w, dconv_vec = _convmod_bwd(p1, dyc, conv_w, sf["conv_b"], sf["conv_ln_g"], sf["conv_ln_b"], tr)
    dlx, dlg, dwa_bd, dwx_bd, dlru_vec = _lru_bwd(p1, hs, dyd, *lru_args)
    dp1 = [dca, dcb, dlx, dlg]
    dh4, dg_mix_l1 = _proj_bwd_act(dh5, h4, gain("mix_norm", 1), 1, dp1, odd_pieces, tm_small)
    dw_in_odd = jnp.stack(_proj_bwd_w(nm1, dp1, tm))
    dh3, dg_f1_l1, g_f1 = ffn_bwd(dh4, h3, n3, a3, b3, f1, 1, none3)
    dh2, dg_f2_l0, g_f2 = ffn_bwd(dh3, h2, n2, a2, b2, f2, 0, g_f2)
    dya, dyb, dw_out_even = _out_bwd(dh2, ya, yb, w_out_even, tm)
    dpool, dpool_wbd, dpool_scale = _pool_bwd(p0, dya, pool_wbd, pool_scale, tr)
    dq, dz, dv, dgate, dlb_logits, dgn_heads = _hgrn_bwd(p0, dyb, states, w["hgrn_lb_logits"], w["hgrn_gnorm"],
                                                         mst, msk, n_lev, tm)
    dp0 = [jnp.concatenate([dpool, dq, dz, dv, dgate], axis=1)]
    dh1, dg_mix_l0 = _proj_bwd_act(dh2, h1, gain("mix_norm", 0), 0, dp0, even_piece, tm_small)
    (dw_in_even,) = _proj_bwd_w(nm0, dp0, tm_small)
    dh0, dg_f1_l0, g_f1 = ffn_bwd(dh1, h0, n1, a1, b1, f1, 0, g_f1)

    grad_x = dh0[N_META:t_real]
    big = {
        "ffn1_wg": g_f1[0], "ffn1_wu": g_f1[1], "ffn1_wd": g_f1[2],
        "ffn2_wg": g_f2[0], "ffn2_wu": g_f2[1], "ffn2_wd": g_f2[2],
        "w_in_even": jnp.transpose(dw_in_even.reshape(d, N_SHARD, D_IN_EVEN // N_SHARD), (1, 0, 2)),
        "w_out_even": dw_out_even.reshape(N_SHARD, d // N_SHARD, d),
        "w_in_odd": dw_in_odd,
        "w_out_odd": dw_out_odd.reshape(N_SHARD, d // N_SHARD, d),
    }
    rep = {
        "ffn1_norm": jnp.concatenate([dg_f1_l0, dg_f1_l1], axis=0),
        "mix_norm": jnp.concatenate([dg_mix_l0, dg_mix_l1], axis=0),
        "ffn2_norm": jnp.concatenate([dg_f2_l0, dg_f2_l1], axis=0),
        "final_norm": dg_final,
        "pool_w": _diag_blocks(dpool_wbd, len(POOL_WINDOWS)),
        "pool_scale": dpool_scale,
        "hgrn_lb_logits": dlb_logits,
        "hgrn_gnorm": jnp.sum(dgn_heads, axis=0),
        "lru_wa": _diag_blocks2(dwa_bd),
        "lru_wx": _diag_blocks2(dwx_bd),
    }
    dmeta = jnp.transpose(dh0[:N_META].reshape(N_META, N_SHARD, 2, LANE), (1, 0, 2, 3)).reshape(N_SHARD, 32, LANE)
    packs = [_pack_rows(rep, REP_SPEC)]
    for s in range(N_SHARD):
        sh = {
            "meta_tokens": dmeta[s], "conv_w": dconv_w[s], "lru_conv_w": dlru_vec[s, 0:4],
            "conv_b": dconv_vec[s, 0:1], "conv_ln_g": dconv_vec[s, 1:2], "conv_ln_b": dconv_vec[s, 2:3],
            "lru_conv_b": dlru_vec[s, 4:5], "lru_ba": dlru_vec[s, 5:6], "lru_bx": dlru_vec[s, 6:7],
            "lru_lambda": dlru_vec[s, 7:8],
        }
        packs.append(_pack_rows(sh, SH_SPEC))
    return loss, grad_x, big, jnp.concatenate(packs, axis=0)


def _block_diag2(heads):
    nb = heads.shape[0] // 2
    return jnp.stack([_block_diag(heads[2 * j:2 * j + 2]) for j in range(nb)])


def _diag_blocks2(mats):
    return jnp.concatenate([_diag_blocks(mats[j], 2) for j in range(mats.shape[0])], axis=0)


def _kernel_v2(x, meta_tokens, ffn1_norm, ffn1_wg, ffn1_wu, ffn1_wd, mix_norm, ffn2_norm, ffn2_wg, ffn2_wu, ffn2_wd, w_in_even, pool_w, pool_scale, hgrn_lb_logits, hgrn_gnorm, w_out_even, w_in_odd, conv_w, conv_b, conv_ln_g, conv_ln_b, lru_conv_w, lru_conv_b, lru_wa, lru_ba, lru_wx, lru_bx, lru_lambda, w_out_odd, final_norm, loss_target, m_meta_tokens, m_ffn1_norm, m_ffn1_wg, m_ffn1_wu, m_ffn1_wd, m_mix_norm, m_ffn2_norm, m_ffn2_wg, m_ffn2_wu, m_ffn2_wd, m_w_in_even, m_pool_w, m_pool_scale, m_hgrn_lb_logits, m_hgrn_gnorm, m_w_out_even, m_w_in_odd, m_conv_w, m_conv_b, m_conv_ln_g, m_conv_ln_b, m_lru_conv_w, m_lru_conv_b, m_lru_wa, m_lru_ba, m_lru_wx, m_lru_bx, m_lru_lambda, m_w_out_odd, m_final_norm, v_meta_tokens, v_ffn1_norm, v_ffn1_wg, v_ffn1_wu, v_ffn1_wd, v_mix_norm, v_ffn2_norm, v_ffn2_wg, v_ffn2_wu, v_ffn2_wd, v_w_in_even, v_pool_w, v_pool_scale, v_hgrn_lb_logits, v_hgrn_gnorm, v_w_out_even, v_w_in_odd, v_conv_w, v_conv_b, v_conv_ln_g, v_conv_ln_b, v_lru_conv_w, v_lru_conv_b, v_lru_wa, v_lru_ba, v_lru_wx, v_lru_bx, v_lru_lambda, v_w_out_odd, v_final_norm):
    args = locals()
    w = {n: args[n] for n in WEIGHT_NAMES}
    m = {n: args["m_" + n] for n in WEIGHT_NAMES}
    v = {n: args["v_" + n] for n in WEIGHT_NAMES}
    shapes = {n: w[n].shape for n in WEIGHT_NAMES}

    big_in = [_rows2d(w[n]).astype(BF16) for n in BIG]
    small_sh = _pack_rows(w, SH_SPEC)
    gath = _allgather_shards(big_in + [small_sh], [True] * len(BIG) + [False])
    gathered = dict(zip(BIG, gath[:len(BIG)]))
    sm = gath[len(BIG)]
    sh_shapes = {n: (N_SHARD,) + tuple(shapes[n]) for n, _ in SH_SPEC}
    per_shard = [_unpack_rows(sm[s], SH_SPEC, shapes) for s in range(N_SHARD)]
    small_full = {}
    for n, _ in SH_SPEC:
        stacked = [per_shard[s][n] for s in range(N_SHARD)]
        small_full[n] = jnp.concatenate([p.reshape(-1, p.shape[-1]) for p in stacked], axis=-1)
    small_full["conv_w"] = jnp.concatenate(
        [small_full["conv_w"], jnp.zeros((CONV_HALO - CONV_WIDTH, D_CONV), F32)], axis=0)

    loss, grad_x, big, small_part = _local_step(x[0], loss_target[0], w, gathered, small_full)
    loss = lax.psum(loss[0, 0], ("x", "y", "c"))

    core = lax.axis_index("c").astype(jnp.int32).reshape(1)
    parts = [big[n] for n in BIG]
    recvd = _pair_exchange_halves(parts)
    pair = _pair_add(parts, recvd, core)
    chip = (2 * lax.axis_index("x") + lax.axis_index("y")).astype(jnp.int32).reshape(1)
    slots = _scatter_to_owners(pair, _own_slot(pair, chip))
    halves = _sum_chips(slots, core)
    full = _pair_allgather_halves(halves)
    out_g, out_d, out_m, out_v = {}, {}, {}, {}
    for n, g in zip(BIG, full):
        res = _adamw(_rows2d(w[n]), _rows2d(m[n]), _rows2d(v[n]), g, 0, ADAM_BLOCKS[n])
        out_g[n], out_d[n], out_m[n], out_v[n] = [r.reshape(shapes[n]) for r in res]

    gathered_small = _allgather_all(small_part)

    def pack_small(src):
        return jnp.concatenate([_pack_rows(src, REP_SPEC), _pack_rows(src, SH_SPEC)], axis=0)

    res = _small_reduce_adamw(gathered_small, pack_small(w), pack_small(m), pack_small(v), REP_ROWS, SH_ROWS)
    for dst, packed in zip((out_g, out_d, out_m, out_v), res):
        dst.update(_unpack_rows(packed[:REP_ROWS], REP_SPEC, shapes))
        dst.update(_unpack_rows(packed[REP_ROWS:], SH_SPEC, shapes))

    return (loss, grad_x[None], *[out_g[n] for n in WEIGHT_NAMES], *[out_d[n] for n in WEIGHT_NAMES],
            *[out_m[n] for n in WEIGHT_NAMES], *[out_v[n] for n in WEIGHT_NAMES])


GATHER_GROUPS = (
    (("small", 0), ("ffn1_wg", 0), ("ffn1_wu", 0), ("ffn1_wd", 0)),
    (("w_in_even", 0), ("w_out_even", 0)),
    (("ffn2_wg", 0), ("ffn2_wu", 0), ("ffn2_wd", 0)),
    (("ffn1_wg", 1), ("ffn1_wu", 1), ("ffn1_wd", 1)),
    (("w_in_odd", 0), ("w_out_odd", 0)),
    (("ffn2_wg", 1), ("ffn2_wu", 1), ("ffn2_wd", 1)),
)
ADAM_ROW_BLOCKS = {"ffn1_wg": 4, "ffn1_wu": 4, "ffn2_wg": 4, "ffn2_wu": 4, "ffn1_wd": 2, "ffn2_wd": 2,
                   "w_in_even": 4, "w_out_even": 2, "w_in_odd": 4, "w_out_odd": 2}


def _unpack_small(sm, shapes):
    per_shard = [_unpack_rows(sm[s], SH_SPEC, shapes) for s in range(N_SHARD)]
    full = {}
    for n, _ in SH_SPEC:
        full[n] = jnp.concatenate([per_shard[s][n].reshape(-1, shapes[n][-1]) for s in range(N_SHARD)], axis=-1)
    full["conv_w"] = jnp.concatenate([full["conv_w"], jnp.zeros((CONV_HALO - CONV_WIDTH, D_CONV), F32)], axis=0)
    return full


def _local_step(x, tgt, w, shapes, fetch, emit):
    s_len, d = x.shape
    t_real = s_len + N_META
    tp = -(-t_real // ROW_ALIGN) * ROW_ALIGN
    tm = _tile(tp, 832, ROW_ALIGN)
    tm_small = _tile(tp, 416, 16)
    tr = _tile(tp, 416, SUBLANE)

    def gain(name, layer):
        return w[name][layer:layer + 1]

    pool_wbd = _block_diag(w["pool_w"][0]).astype(BF16)
    pool_scale = w["pool_scale"]
    wa_bd = _block_diag2(w["lru_wa"][0]).astype(BF16)
    wx_bd = _block_diag2(w["lru_wx"][0]).astype(BF16)
    mst, msk, n_lev = _hgrn_consts(HG_CHUNK)

    sm, *f1l0 = fetch(0, None)
    sf = _unpack_small(sm, shapes)
    h0 = jnp.concatenate([sf["meta_tokens"], x, jnp.zeros((tp - t_real, d), F32)], axis=0)
    tgt_pad = jnp.concatenate([jnp.zeros((N_META, d), F32), tgt, jnp.zeros((tp - t_real, d), F32)], axis=0)
    h1, a1, b1, n1 = _ffn_fwd(h0, gain("ffn1_norm", 0), *f1l0, 0, tm)
    w_in_even4, w_out_even4 = fetch(1, h1)
    w_in_even = jnp.transpose(w_in_even4, (1, 0, 2)).reshape(d, D_IN_EVEN)
    w_out_even = w_out_even4.reshape(d, d)
    even_piece = [(w_in_even, (d, D_IN_EVEN), (0, 0))]
    p0, nm0 = _proj_fwd(h1, gain("mix_norm", 0), 0, even_piece, tm_small)
    ya = _pool_fwd(p0, pool_wbd, pool_scale, tr)
    yb, states = _hgrn_fwd(p0, w["hgrn_lb_logits"], w["hgrn_gnorm"], mst, msk, n_lev, tm)
    h2 = _out_fwd(h1, ya, yb, w_out_even, tm)
    f2l0 = fetch(2, h2)
    h3, a2, b2, n2 = _ffn_fwd(h2, gain("ffn2_norm", 0), *f2l0, 0, tm)
    f1l1 = fetch(3, h3)
    h4, a3, b3, n3 = _ffn_fwd(h3, gain("ffn1_norm", 1), *f1l1, 0, tm)
    w_in_odd4, w_out_odd4 = fetch(4, h4)
    w_out_odd = w_out_odd4.reshape(d, d)
    odd_pieces = [(w_in_odd4, (1, d, D_IN_ODD // N_SHARD), (k, 0, 0)) for k in range(N_SHARD)]
    p1, nm1 = _proj_fwd(h4, gain("mix_norm", 1), 1, odd_pieces, tm_small)
    yc = _convmod_fwd(p1, sf["conv_w"], sf["conv_b"], sf["conv_ln_g"], sf["conv_ln_b"], tr)
    lru_args = (sf["lru_conv_w"], sf["lru_conv_b"], wa_bd, sf["lru_ba"], wx_bd, sf["lru_bx"], sf["lru_lambda"])
    yd, hs = _lru_fwd(p1, *lru_args)
    h5 = _out_fwd(h4, yc, yd, w_out_odd, tm)
    f2l1 = fetch(5, h5)
    h6, a4, b4, n4 = _ffn_fwd(h5, gain("ffn2_norm", 1), *f2l1, 0, tm)
    loss, dh6, dg_final = _loss_bwd(h6, w["final_norm"].reshape(1, d), tgt_pad, t_real, tm)

    def ffn_bwd(dho, h, n, a, b, norm, wts):
        dh, da, db, dg = _ffn_bwd_act(dho, h, norm, a, b, *wts, 0, tm_small)
        return dh, dg, _ffn_bwd_w(dho, n, a, b, da, db, tm)

    dh5, dg_f2_l1, g = ffn_bwd(dh6, h5, n4, a4, b4, gain("ffn2_norm", 1), f2l1)
    emit((("ffn2_wg", 1), ("ffn2_wu", 1), ("ffn2_wd", 1)), g)
    dyc, dyd, dw_out_odd = _out_bwd(dh5, yc, yd, w_out_odd, tm)
    dca, dcb, dconv_w, dconv_vec = _convmod_bwd(p1, dyc, sf["conv_w"], sf["conv_b"], sf["conv_ln_g"],
                                                sf["conv_ln_b"], tr)
    dlx, dlg, dwa_bd, dwx_bd, dlru_vec = _lru_bwd(p1, hs, dyd, *lru_args)
    dp1 = [dca, dcb, dlx, dlg]
    dh4, dg_mix_l1 = _proj_bwd_act(dh5, h4, gain("mix_norm", 1), 1, dp1, odd_pieces, tm_small)
    dw_in_odd = jnp.stack(_proj_bwd_w(nm1, dp1, tm))
    dh3, dg_f1_l1, g = ffn_bwd(dh4, h3, n3, a3, b3, gain("ffn1_norm", 1), f1l1)
    emit((("w_out_odd", 0), ("w_in_odd", 0), ("ffn1_wg", 1), ("ffn1_wu", 1), ("ffn1_wd", 1)),
         [dw_out_odd.reshape(N_SHARD, d // N_SHARD, d), dw_in_odd] + list(g))
    dh2, dg_f2_l0, g_f2l0 = ffn_bwd(dh3, h2, n2, a2, b2, gain("ffn2_norm", 0), f2l0)
    dya, dyb, dw_out_even = _out_bwd(dh2, ya, yb, w_out_even, tm)
    dpool, dpool_wbd, dpool_scale = _pool_bwd(p0, dya, pool_wbd, pool_scale, tr)
    dq, dz, dv, dgate, dlb_logits, dgn_heads = _hgrn_bwd(p0, dyb, states, w["hgrn_lb_logits"], w["hgrn_gnorm"],
                                                         mst, msk, n_lev, tm)
    dp0 = [jnp.concatenate([dpool, dq, dz, dv, dgate], axis=1)]
    dh1, dg_mix_l0 = _proj_bwd_act(dh2, h1, gain("mix_norm", 0), 0, dp0, even_piece, tm_small)
    (dw_in_even,) = _proj_bwd_w(nm0, dp0, tm_small)
    emit((("ffn2_wg", 0), ("ffn2_wu", 0), ("ffn2_wd", 0), ("w_out_even", 0), ("w_in_even", 0)),
         list(g_f2l0) + [dw_out_even.reshape(N_SHARD, d // N_SHARD, d),
                         jnp.transpose(dw_in_even.reshape(d, N_SHARD, D_IN_EVEN // N_SHARD), (1, 0, 2))])
    dh0, dg_f1_l0, g = ffn_bwd(dh1, h0, n1, a1, b1, gain("ffn1_norm", 0), f1l0)
    emit((("ffn1_wg", 0), ("ffn1_wu", 0), ("ffn1_wd", 0)), g)

    grad_x = dh0[N_META:t_real]
    rep = {
        "ffn1_norm": jnp.concatenate([dg_f1_l0, dg_f1_l1], axis=0),
        "mix_norm": jnp.concatenate([dg_mix_l0, dg_mix_l1], axis=0),
        "ffn2_norm": jnp.concatenate([dg_f2_l0, dg_f2_l1], axis=0),
        "final_norm": dg_final,
        "pool_w": _diag_blocks(dpool_wbd, len(POOL_WINDOWS)),
        "pool_scale": dpool_scale,
        "hgrn_lb_logits": dlb_logits,
        "hgrn_gnorm": jnp.sum(dgn_heads, axis=0),
        "lru_wa": _diag_blocks2(dwa_bd),
        "lru_wx": _diag_blocks2(dwx_bd),
    }
    dmeta = jnp.transpose(dh0[:N_META].reshape(N_META, N_SHARD, 2, LANE), (1, 0, 2, 3)).reshape(N_SHARD, 32, LANE)
    packs = [_pack_rows(rep, REP_SPEC)]
    for s in range(N_SHARD):
        sh = {
            "meta_tokens": dmeta[s], "conv_w": dconv_w[s], "lru_conv_w": dlru_vec[s, 0:4],
            "conv_b": dconv_vec[s, 0:1], "conv_ln_g": dconv_vec[s, 1:2], "conv_ln_b": dconv_vec[s, 2:3],
            "lru_conv_b": dlru_vec[s, 4:5], "lru_ba": dlru_vec[s, 5:6], "lru_bx": dlru_vec[s, 6:7],
            "lru_lambda": dlru_vec[s, 7:8],
        }
        packs.append(_pack_rows(sh, SH_SPEC))
    return loss, grad_x, jnp.concatenate(packs, axis=0)


def kernel(x, meta_tokens, ffn1_norm, ffn1_wg, ffn1_wu, ffn1_wd, mix_norm, ffn2_norm, ffn2_wg, ffn2_wu, ffn2_wd, w_in_even, pool_w, pool_scale, hgrn_lb_logits, hgrn_gnorm, w_out_even, w_in_odd, conv_w, conv_b, conv_ln_g, conv_ln_b, lru_conv_w, lru_conv_b, lru_wa, lru_ba, lru_wx, lru_bx, lru_lambda, w_out_odd, final_norm, loss_target, m_meta_tokens, m_ffn1_norm, m_ffn1_wg, m_ffn1_wu, m_ffn1_wd, m_mix_norm, m_ffn2_norm, m_ffn2_wg, m_ffn2_wu, m_ffn2_wd, m_w_in_even, m_pool_w, m_pool_scale, m_hgrn_lb_logits, m_hgrn_gnorm, m_w_out_even, m_w_in_odd, m_conv_w, m_conv_b, m_conv_ln_g, m_conv_ln_b, m_lru_conv_w, m_lru_conv_b, m_lru_wa, m_lru_ba, m_lru_wx, m_lru_bx, m_lru_lambda, m_w_out_odd, m_final_norm, v_meta_tokens, v_ffn1_norm, v_ffn1_wg, v_ffn1_wu, v_ffn1_wd, v_mix_norm, v_ffn2_norm, v_ffn2_wg, v_ffn2_wu, v_ffn2_wd, v_w_in_even, v_pool_w, v_pool_scale, v_hgrn_lb_logits, v_hgrn_gnorm, v_w_out_even, v_w_in_odd, v_conv_w, v_conv_b, v_conv_ln_g, v_conv_ln_b, v_lru_conv_w, v_lru_conv_b, v_lru_wa, v_lru_ba, v_lru_wx, v_lru_bx, v_lru_lambda, v_w_out_odd, v_final_norm):
    args = locals()
    w = {n: args[n] for n in WEIGHT_NAMES}
    m = {n: args["m_" + n] for n in WEIGHT_NAMES}
    v = {n: args["v_" + n] for n in WEIGHT_NAMES}
    shapes = {n: w[n].shape for n in WEIGHT_NAMES}
    core = lax.axis_index("c").astype(jnp.int32).reshape(1)
    chip = (2 * lax.axis_index("x") + lax.axis_index("y")).astype(jnp.int32).reshape(1)

    keys = [key for grp in GATHER_GROUPS for key in grp]
    index = {key: k for k, key in enumerate(keys)}
    shards = [_pack_rows(w, SH_SPEC) if n == "small" else w[n][l].astype(BF16) for n, l in keys]
    ssem, rsem, srcs, lands, token = _gather_start(shards)

    def fetch(group, after):
        ks = [index[key] for key in GATHER_GROUPS[group]]
        return _gather_wait(ssem, rsem, [srcs[k] for k in ks], [lands[k] for k in ks], ks,
                            token if after is None else after)

    in_flight = []

    def emit(gkeys, grads):
        grads = list(grads)
        pair = _pair_add(grads, _pair_exchange_halves(grads), core)
        in_flight.append((gkeys,) + tuple(_scatter_start(pair, _own_slot(pair, chip))))

    loss, grad_x, small_part = _local_step(x[0], loss_target[0], w, shapes, fetch, emit)
    loss = lax.psum(loss[0, 0], ("x", "y", "c"))

    halves, hkeys = [], []
    for gkeys, gs_sem, gr_sem, pair_thru, slots_thru in in_flight:
        slots = _scatter_wait(gs_sem, gr_sem, pair_thru, slots_thru, grad_x)
        halves += _sum_chips(slots, core)
        hkeys += list(gkeys)
    full = dict(zip(hkeys, _pair_allgather_halves(halves)))
    out_g, out_d, out_m, out_v = {}, {}, {}, {}
    for n in BIG:
        gs = [full[(n, l)] for l in range(shapes[n][0])]
        res = _adamw(w[n], m[n], v[n], gs, ADAM_ROW_BLOCKS[n])
        out_g[n], out_d[n], out_m[n], out_v[n] = res

    gathered_small = _allgather_all(small_part)

    def pack_small(src):
        return jnp.concatenate([_pack_rows(src, REP_SPEC), _pack_rows(src, SH_SPEC)], axis=0)

    res = _small_reduce_adamw(gathered_small, pack_small(w), pack_small(m), pack_small(v), REP_ROWS, SH_ROWS)
    for dst, packed in zip((out_g, out_d, out_m, out_v), res):
        dst.update(_unpack_rows(packed[:REP_ROWS], REP_SPEC, shapes))
        dst.update(_unpack_rows(packed[REP_ROWS:], SH_SPEC, shapes))

    return (loss, grad_x[None], *[out_g[n] for n in WEIGHT_NAMES], *[out_d[n] for n in WEIGHT_NAMES],
            *[out_m[n] for n in WEIGHT_NAMES], *[out_v[n] for n in WEIGHT_NAMES])
```

```python
import functools

import numpy as np
import jax
import jax.numpy as jnp
from jax import lax
from jax.experimental import pallas as pl
from jax.experimental.pallas import tpu as pltpu

F32 = jnp.float32
BF16 = jnp.bfloat16
MESH = pl.DeviceIdType.MESH

EPS = 1e-6
N_META = 16
D_MODEL = 1024
D_FF = 2816
N_SHARD = 4
FF_SHARD = D_FF // N_SHARD
D_POOL = 256
POOL_GROUP = 64
POOL_WINDOWS = (2, 4, 8, 16)
D_HGRN = 768
HG_HEADS = 6
HEAD = 128
HG_CHUNK = 64
D_IN_EVEN = D_POOL + 4 * D_HGRN
D_CONV = 512
CONV_WIDTH = 31
CONV_HALO = 32
D_LRU = 512
LRU_CONV = 4
LRU_HALO = 8
LRU_C = 8.0
D_IN_ODD = 2 * D_CONV + 2 * D_LRU
SUBLANE = 8
ROW_ALIGN = 64

ADAM_LR = 0.001
ADAM_B1 = 0.9
ADAM_B2 = 0.999
ADAM_EPS = 1e-08
ADAM_WD = 0.01
ADAM_STEP = 10

VMEM_LIMIT_MB = 56


def _cparams(n_grid_axes=0, vmem_mb=VMEM_LIMIT_MB):
    sem = ("arbitrary",) * n_grid_axes if n_grid_axes else None
    return pltpu.CompilerParams(dimension_semantics=sem, vmem_limit_bytes=vmem_mb * 1024 * 1024)


def _tile(n, target, mult):
    best = None
    for t in range(mult, min(n, target) + 1, mult):
        if n % t == 0:
            best = t
    assert best is not None, (n, target, mult)
    return best


def _dot(a, b):
    return jnp.dot(a, b, preferred_element_type=F32)


def _dot_nt(a, b):
    return lax.dot_general(a, b, (((1,), (1,)), ((), ())), preferred_element_type=F32)


def _dot_tn(a, b):
    return lax.dot_general(a, b, (((0,), (0,)), ((), ())), preferred_element_type=F32)


def _sigmoid(x):
    return 1.0 / (1.0 + jnp.exp(-x))


def _colsum(x):
    return jnp.sum(x, axis=0, keepdims=True)


def _rms_stats(h):
    rstd = lax.rsqrt(jnp.mean(h * h, axis=-1, keepdims=True) + EPS)
    return rstd, h * rstd


def _rms_bwd(dn, g, rstd, xhat):
    dng = dn * g
    dh = rstd * (dng - xhat * jnp.mean(dng * xhat, axis=-1, keepdims=True))
    return dh, _colsum(dn * xhat)


def _ffn_fwd(h, norm, wg4, wu4, wd4, layer, tm):
    tp, d = h.shape
    nt = tp // tm

    def body(h_ref, g_ref, wg_ref, wu_ref, wd_ref, ho_ref, a_ref, b_ref, n_ref, n_sc, acc):
        s = pl.program_id(1)

        @pl.when(s == 0)
        def _():
            hh = h_ref[...]
            rstd, xhat = _rms_stats(hh)
            n = (xhat * g_ref[...]).astype(BF16)
            n_sc[...] = n
            n_ref[...] = n
            acc[...] = jnp.zeros_like(acc)

        n = n_sc[...]
        a = _dot_nt(n, wg_ref[0])
        b = _dot_nt(n, wu_ref[0])
        a_ref[0] = a.astype(BF16)
        b_ref[0] = b.astype(BF16)
        sg = (a * _sigmoid(a) * b).astype(BF16)
        acc[...] += _dot(sg, wd_ref[0])

        @pl.when(s == N_SHARD - 1)
        def _():
            ho_ref[...] = h_ref[...] + 0.5 * acc[...]

    return pl.pallas_call(
        body, name="ffn_fwd",
        grid=(nt, N_SHARD),
        in_specs=[
            pl.BlockSpec((tm, d), lambda i, s: (i, 0)),
            pl.BlockSpec((1, d), lambda i, s: (0, 0)),
            pl.BlockSpec((1, FF_SHARD, d), lambda i, s: (s, layer, 0)),
            pl.BlockSpec((1, FF_SHARD, d), lambda i, s: (s, layer, 0)),
            pl.BlockSpec((1, FF_SHARD, d), lambda i, s: (s, layer, 0)),
        ],
        out_specs=[
            pl.BlockSpec((tm, d), lambda i, s: (i, 0)),
            pl.BlockSpec((1, tm, FF_SHARD), lambda i, s: (s, i, 0)),
            pl.BlockSpec((1, tm, FF_SHARD), lambda i, s: (s, i, 0)),
            pl.BlockSpec((tm, d), lambda i, s: (i, 0)),
        ],
        out_shape=[
            jax.ShapeDtypeStruct((tp, d), F32),
            jax.ShapeDtypeStruct((N_SHARD, tp, FF_SHARD), BF16),
            jax.ShapeDtypeStruct((N_SHARD, tp, FF_SHARD), BF16),
            jax.ShapeDtypeStruct((tp, d), BF16),
        ],
        scratch_shapes=[pltpu.VMEM((tm, d), BF16), pltpu.VMEM((tm, d), F32)],
        compiler_params=_cparams(2),
    )(h, norm, wg4, wu4, wd4)


def _ffn_bwd_act(dho, h, norm, a4, b4, wg4, wu4, wd4, layer, tm, after=()):
    tp, d = h.shape
    nt = tp // tm

    def body(dho_ref, h_ref, g_ref, a_ref, b_ref, wg_ref, wu_ref, wd_ref, *rest):
        dh_ref, da_ref, db_ref, dg_ref, dy_sc, dn_sc = rest[len(after):]
        i = pl.program_id(0)
        s = pl.program_id(1)

        @pl.when(s == 0)
        def _():
            dy_sc[...] = (0.5 * dho_ref[...]).astype(BF16)
            dn_sc[...] = jnp.zeros_like(dn_sc)

        @pl.when((s == 0) & (i == 0))
        def _():
            dg_ref[...] = jnp.zeros_like(dg_ref)

        ds = _dot_nt(dy_sc[...], wd_ref[0])
        a = a_ref[0].astype(F32)
        b = b_ref[0].astype(F32)
        sig = _sigmoid(a)
        da = (ds * b * (sig * (1.0 + a * (1.0 - sig)))).astype(BF16)
        db = (ds * (a * sig)).astype(BF16)
        da_ref[0] = da
        db_ref[0] = db
        dn_sc[...] += _dot(da, wg_ref[0]) + _dot(db, wu_ref[0])

        @pl.when(s == N_SHARD - 1)
        def _():
            rstd, xhat = _rms_stats(h_ref[...])
            dh, dg = _rms_bwd(dn_sc[...], g_ref[...], rstd, xhat)
            dh_ref[...] = dho_ref[...] + dh
            dg_ref[...] += dg

    return pl.pallas_call(
        body, name="ffn_bwd_act",
        grid=(nt, N_SHARD),
        in_specs=[
            pl.BlockSpec((tm, d), lambda i, s: (i, 0)),
            pl.BlockSpec((tm, d), lambda i, s: (i, 0)),
            pl.BlockSpec((1, d), lambda i, s: (0, 0)),
            pl.BlockSpec((1, tm, FF_SHARD), lambda i, s: (s, i, 0)),
            pl.BlockSpec((1, tm, FF_SHARD), lambda i, s: (s, i, 0)),
            pl.BlockSpec((1, FF_SHARD, d), lambda i, s: (s, layer, 0)),
            pl.BlockSpec((1, FF_SHARD, d), lambda i, s: (s, layer, 0)),
            pl.BlockSpec((1, FF_SHARD, d), lambda i, s: (s, layer, 0)),
        ] + [pl.BlockSpec(memory_space=pl.ANY)] * len(after),
        out_specs=[
            pl.BlockSpec((tm, d), lambda i, s: (i, 0)),
            pl.BlockSpec((1, tm, FF_SHARD), lambda i, s: (s, i, 0)),
            pl.BlockSpec((1, tm, FF_SHARD), lambda i, s: (s, i, 0)),
            pl.BlockSpec((1, d), lambda i, s: (0, 0)),
        ],
        out_shape=[
            jax.ShapeDtypeStruct((tp, d), F32),
            jax.ShapeDtypeStruct((N_SHARD, tp, FF_SHARD), BF16),
            jax.ShapeDtypeStruct((N_SHARD, tp, FF_SHARD), BF16),
            jax.ShapeDtypeStruct((1, d), F32),
        ],
        scratch_shapes=[pltpu.VMEM((tm, d), BF16), pltpu.VMEM((tm, d), F32)],
        compiler_params=_cparams(2),
    )(dho, h, norm, a4, b4, wg4, wu4, wd4, *after)


def _ffn_bwd_w(dho, n, a4, b4, da4, db4, tm):
    tp, d = n.shape
    nt = tp // tm

    def body(dho_ref, n_ref, a_ref, b_ref, da_ref, db_ref, og_ref, ou_ref, od_ref, accg, accu, accd):
        i = pl.program_id(1)

        @pl.when(i == 0)
        def _():
            accg[...] = jnp.zeros_like(accg)
            accu[...] = jnp.zeros_like(accu)
            accd[...] = jnp.zeros_like(accd)

        nn = n_ref[...]
        accg[...] += _dot_tn(da_ref[0], nn)
        accu[...] += _dot_tn(db_ref[0], nn)
        a = a_ref[0].astype(F32)
        b = b_ref[0].astype(F32)
        sact = (a * _sigmoid(a) * b).astype(BF16)
        dy = (0.5 * dho_ref[...]).astype(BF16)
        accd[...] += _dot_tn(sact, dy)

        @pl.when(i == nt - 1)
        def _():
            og_ref[0] = accg[...].astype(BF16)
            ou_ref[0] = accu[...].astype(BF16)
            od_ref[0] = accd[...].astype(BF16)

    in_specs = [
        pl.BlockSpec((tm, d), lambda s, i: (i, 0)),
        pl.BlockSpec((tm, d), lambda s, i: (i, 0)),
        pl.BlockSpec((1, tm, FF_SHARD), lambda s, i: (s, i, 0)),
        pl.BlockSpec((1, tm, FF_SHARD), lambda s, i: (s, i, 0)),
        pl.BlockSpec((1, tm, FF_SHARD), lambda s, i: (s, i, 0)),
        pl.BlockSpec((1, tm, FF_SHARD), lambda s, i: (s, i, 0)),
    ]
    return pl.pallas_call(
        body, name="ffn_bwd_w",
        grid=(N_SHARD, nt),
        in_specs=in_specs,
        out_specs=[pl.BlockSpec((1, FF_SHARD, d), lambda s, i: (s, 0, 0))] * 3,
        out_shape=[jax.ShapeDtypeStruct((N_SHARD, FF_SHARD, d), BF16)] * 3,
        scratch_shapes=[pltpu.VMEM((FF_SHARD, d), F32)] * 3,
        compiler_params=_cparams(2),
    )(dho, n, a4, b4, da4, db4)


def _proj_fwd(h, norm, layer, w_pieces, tm, wt=False):
    tp, d = h.shape
    widths = [bs[-2] if wt else bs[-1] for _, bs, _ in w_pieces]
    ntot = sum(widths)
    npc = len(w_pieces)

    def body(*refs):
        h_ref, g_ref = refs[:2]
        w_refs = refs[2:2 + npc]
        p_ref, n_ref = refs[2 + npc:]
        rstd, xhat = _rms_stats(h_ref[...])
        n = (xhat * g_ref[...]).astype(BF16)
        n_ref[...] = n
        off = 0
        for k in range(npc):
            w = w_refs[k][...]
            w = w.reshape(w.shape[-2], w.shape[-1])
            p_ref[:, off:off + widths[k]] = _dot_nt(n, w) if wt else _dot(n, w)
            off += widths[k]

    in_specs = [pl.BlockSpec((tm, d), lambda i: (i, 0)), pl.BlockSpec((1, d), lambda i: (0, 0))]
    for _, bs, idx in w_pieces:
        in_specs.append(pl.BlockSpec(bs, functools.partial(lambda i, idx: idx, idx=idx)))
    return pl.pallas_call(
        body, name="proj_fwd",
        grid=(tp // tm,),
        in_specs=in_specs,
        out_specs=[pl.BlockSpec((tm, ntot), lambda i: (i, 0)), pl.BlockSpec((tm, d), lambda i: (i, 0))],
        out_shape=[jax.ShapeDtypeStruct((tp, ntot), F32), jax.ShapeDtypeStruct((tp, d), BF16)],
        compiler_params=_cparams(1),
    )(h, norm, *[w for w, _, _ in w_pieces])


def _proj_bwd_act(dres, h, norm, layer, dp_pieces, w_pieces, tm, wt=False):
    tp, d = h.shape
    npc = len(w_pieces)

    def body(*refs):
        dres_ref, h_ref, g_ref = refs[:3]
        dp_refs = refs[3:3 + npc]
        w_refs = refs[3 + npc:3 + 2 * npc]
        dh_ref, dg_ref = refs[3 + 2 * npc:]
        i = pl.program_id(0)

        @pl.when(i == 0)
        def _():
            dg_ref[...] = jnp.zeros_like(dg_ref)

        dn = None
        for k in range(npc):
            w = w_refs[k][...]
            w = w.reshape(w.shape[-2], w.shape[-1])
            t = _dot(dp_refs[k][...], w) if wt else _dot_nt(dp_refs[k][...], w)
            dn = t if dn is None else dn + t
        rstd, xhat = _rms_stats(h_ref[...])
        dh, dg = _rms_bwd(dn, g_ref[...], rstd, xhat)
        dh_ref[...] = dres_ref[...] + dh
        dg_ref[...] += dg

    in_specs = [pl.BlockSpec((tm, d), lambda i: (i, 0)), pl.BlockSpec((tm, d), lambda i: (i, 0)),
                pl.BlockSpec((1, d), lambda i: (0, 0))]
    for dp in dp_pieces:
        in_specs.append(pl.BlockSpec((tm, dp.shape[1]), lambda i: (i, 0)))
    for _, bs, idx in w_pieces:
        in_specs.append(pl.BlockSpec(bs, functools.partial(lambda i, idx: idx, idx=idx)))
    return pl.pallas_call(
        body, name="proj_bwd_act",
        grid=(tp // tm,),
        in_specs=in_specs,
        out_specs=[pl.BlockSpec((tm, d), lambda i: (i, 0)), pl.BlockSpec((1, d), lambda i: (0, 0))],
        out_shape=[jax.ShapeDtypeStruct((tp, d), F32), jax.ShapeDtypeStruct((1, d), F32)],
        compiler_params=_cparams(1),
    )(dres, h, norm, *dp_pieces, *[w for w, _, _ in w_pieces])


def _proj_bwd_w(n, dp_pieces, tm, wt=False):
    tp, d = n.shape
    npc = len(dp_pieces)
    widths = [dp.shape[1] for dp in dp_pieces]
    oshape = (lambda w: (w, d)) if wt else (lambda w: (d, w))

    def body(*refs):
        n_ref = refs[0]
        dp_refs = refs[1:1 + npc]
        o_refs = refs[1 + npc:1 + 2 * npc]
        accs = refs[1 + 2 * npc:]
        i = pl.program_id(0)

        @pl.when(i == 0)
        def _():
            for acc in accs:
                acc[...] = jnp.zeros_like(acc)

        nn = n_ref[...]
        for k in range(npc):
            accs[k][...] += _dot_tn(dp_refs[k][...], nn) if wt else _dot_tn(nn, dp_refs[k][...])

        @pl.when(i == pl.num_programs(0) - 1)
        def _():
            for k in range(npc):
                o_refs[k][...] = accs[k][...].astype(BF16)

    return pl.pallas_call(
        body, name="proj_bwd_w",
        grid=(tp // tm,),
        in_specs=[pl.BlockSpec((tm, d), lambda i: (i, 0))]
        + [pl.BlockSpec((tm, w), lambda i: (i, 0)) for w in widths],
        out_specs=[pl.BlockSpec(oshape(w), lambda i: (0, 0)) for w in widths],
        out_shape=[jax.ShapeDtypeStruct(oshape(w), BF16) for w in widths],
        scratch_shapes=[pltpu.VMEM(oshape(w), F32) for w in widths],
        compiler_params=_cparams(1),
    )(n, *dp_pieces)


def _out_fwd(h, ya, yb, w, tm):
    tp, d = h.shape
    na, nb = ya.shape[1], yb.shape[1]

    def body(h_ref, ya_ref, yb_ref, w_ref, o_ref):
        y = _dot(ya_ref[...].astype(BF16), w_ref[0:na, :]) + _dot(yb_ref[...].astype(BF16), w_ref[na:, :])
        o_ref[...] = h_ref[...] + y

    return pl.pallas_call(
        body, name="out_fwd",
        grid=(tp // tm,),
        in_specs=[pl.BlockSpec((tm, d), lambda i: (i, 0)), pl.BlockSpec((tm, na), lambda i: (i, 0)),
                  pl.BlockSpec((tm, nb), lambda i: (i, 0)), pl.BlockSpec((d, d), lambda i: (0, 0))],
        out_specs=pl.BlockSpec((tm, d), lambda i: (i, 0)),
        out_shape=jax.ShapeDtypeStruct((tp, d), F32),
        compiler_params=_cparams(1),
    )(h, ya, yb, w)


def _out_bwd(dy, ya, yb, w, tm, after=()):
    tp, d = dy.shape
    na, nb = ya.shape[1], yb.shape[1]

    def body(dy_ref, ya_ref, yb_ref, w_ref, *rest):
        da_ref, db_ref, dw_ref, acc = rest[len(after):]
        i = pl.program_id(0)

        @pl.when(i == 0)
        def _():
            acc[...] = jnp.zeros_like(acc)

        dyb16 = dy_ref[...].astype(BF16)
        da_ref[...] = _dot_nt(dyb16, w_ref[0:na, :])
        db_ref[...] = _dot_nt(dyb16, w_ref[na:, :])
        acc[0:na, :] += _dot_tn(ya_ref[...].astype(BF16), dyb16)
        acc[na:, :] += _dot_tn(yb_ref[...].astype(BF16), dyb16)

        @pl.when(i == pl.num_programs(0) - 1)
        def _():
            dw_ref[...] = acc[...].astype(BF16)

    return pl.pallas_call(
        body, name="out_bwd",
        grid=(tp // tm,),
        in_specs=[pl.BlockSpec((tm, d), lambda i: (i, 0)), pl.BlockSpec((tm, na), lambda i: (i, 0)),
                  pl.BlockSpec((tm, nb), lambda i: (i, 0)), pl.BlockSpec((d, d), lambda i: (0, 0))]
        + [pl.BlockSpec(memory_space=pl.ANY)] * len(after),
        out_specs=[pl.BlockSpec((tm, na), lambda i: (i, 0)), pl.BlockSpec((tm, nb), lambda i: (i, 0)),
                   pl.BlockSpec((d, d), lambda i: (0, 0))],
        out_shape=[jax.ShapeDtypeStruct((tp, na), F32), jax.ShapeDtypeStruct((tp, nb), F32),
                   jax.ShapeDtypeStruct((d, d), BF16)],
        scratch_shapes=[pltpu.VMEM((d, d), F32)],
        compiler_params=_cparams(1),
    )(dy, ya, yb, w, *after)


def _loss_bwd(h, gfin, tgt, t_real, tm):
    tp, d = h.shape

    def body(h_ref, g_ref, t_ref, loss_ref, dh_ref, dg_ref):
        i = pl.program_id(0)

        @pl.when(i == 0)
        def _():
            loss_ref[...] = jnp.zeros_like(loss_ref)
            dg_ref[...] = jnp.zeros_like(dg_ref)

        rows = i * tm + lax.broadcasted_iota(jnp.int32, (tm, 1), 0)
        valid = (rows >= N_META) & (rows < t_real)
        rstd, xhat = _rms_stats(h_ref[...])
        g = g_ref[...]
        err = jnp.where(valid, xhat * g - t_ref[...], 0.0)
        e2 = jnp.sum(err * err, axis=1, keepdims=True)
        loss_ref[...] += (0.5 / d) * jnp.sum(e2, axis=0, keepdims=True)
        dy = err * (1.0 / d)
        dh, dg = _rms_bwd(dy, g, rstd, xhat)
        dh_ref[...] = dh
        dg_ref[...] += dg

    return pl.pallas_call(
        body, name="loss_bwd",
        grid=(tp // tm,),
        in_specs=[pl.BlockSpec((tm, d), lambda i: (i, 0)), pl.BlockSpec((1, d), lambda i: (0, 0)),
                  pl.BlockSpec((tm, d), lambda i: (i, 0))],
        out_specs=[pl.BlockSpec((1, 1), lambda i: (0, 0)), pl.BlockSpec((tm, d), lambda i: (i, 0)),
                   pl.BlockSpec((1, d), lambda i: (0, 0))],
        out_shape=[jax.ShapeDtypeStruct((1, 1), F32), jax.ShapeDtypeStruct((tp, d), F32),
                   jax.ShapeDtypeStruct((1, d), F32)],
        compiler_params=_cparams(1),
    )(h, gfin, tgt)


POOL_HALO = 16


def _pool_lane_consts(n_rows):
    lane = lax.broadcasted_iota(jnp.int32, (n_rows, D_POOL), 1)
    grp = lane // POOL_GROUP
    win = jnp.where(grp == 0, 2.0, jnp.where(grp == 1, 4.0, jnp.where(grp == 2, 8.0, 16.0)))
    return grp, win


def _pool_select(grp, s2, s4, s8, s16):
    return jnp.where(grp == 0, s2, jnp.where(grp == 1, s4, jnp.where(grp == 2, s8, s16)))


def _pool_mixed(x, row0, tr):
    n = tr + POOL_HALO
    s2 = x + pltpu.roll(x, 1, 0)
    s4 = s2 + pltpu.roll(s2, 2, 0)
    s8 = s4 + pltpu.roll(s4, 4, 0)
    s16 = s8 + pltpu.roll(s8, 8, 0)
    grp, win = _pool_lane_consts(n)
    rows = row0 - POOL_HALO + lax.broadcasted_iota(jnp.int32, (n, D_POOL), 0)
    cnt = jnp.minimum((rows + 1).astype(F32), win)
    pooled = _pool_select(grp, s2, s4, s8, s16) / jnp.maximum(cnt, 1.0)
    return (pooled - x)[POOL_HALO:, :]


def _pool_fwd(p, wbd, scale, tr):
    tp = p.shape[0]
    nt = tp // tr

    def body(p_ref, w_ref, s_ref, y_ref, usc):
        usc[0:POOL_HALO, :] = jnp.zeros((POOL_HALO, D_POOL), F32)
        usc[POOL_HALO:, :] = p_ref[...]

        def tile(r, carry):
            r0 = pl.multiple_of(r * tr, SUBLANE)
            x = usc[pl.ds(r0, tr + POOL_HALO), :]
            mixed = _pool_mixed(x, r0, tr)
            y_ref[pl.ds(r0, tr), :] = _dot(mixed.astype(BF16), w_ref[...]) * s_ref[...]
            return carry

        lax.fori_loop(0, nt, tile, 0)

    return pl.pallas_call(
        body, name="pool_fwd",
        grid=(1,),
        in_specs=[pl.BlockSpec((tp, D_POOL), lambda i: (0, 0)), pl.BlockSpec((D_POOL, D_POOL), lambda i: (0, 0)),
                  pl.BlockSpec((1, D_POOL), lambda i: (0, 0))],
        out_specs=pl.BlockSpec((tp, D_POOL), lambda i: (0, 0)),
        out_shape=jax.ShapeDtypeStruct((tp, D_POOL), F32),
        scratch_shapes=[pltpu.VMEM((tp + POOL_HALO, D_POOL), F32)],
        compiler_params=_cparams(1),
    )(p, wbd, scale)


def _pool_bwd(p, dya, wbd, scale, tr):
    tp = p.shape[0]
    nt = tp // tr

    def body(p_ref, dy_ref, w_ref, s_ref, du_ref, dw_ref, ds_ref, usc, gsc):
        usc[0:POOL_HALO, :] = jnp.zeros((POOL_HALO, D_POOL), F32)
        usc[POOL_HALO:, :] = p_ref[...]
        gsc[tp:, :] = jnp.zeros((POOL_HALO, D_POOL), F32)
        dw_ref[...] = jnp.zeros_like(dw_ref)
        ds_ref[...] = jnp.zeros_like(ds_ref)
        grp, win = _pool_lane_consts(tr)

        def tile1(r, carry):
            r0 = pl.multiple_of(r * tr, SUBLANE)
            x = usc[pl.ds(r0, tr + POOL_HALO), :]
            mixed = _pool_mixed(x, r0, tr).astype(BF16)
            dy = dy_ref[pl.ds(r0, tr), :]
            dys = (dy * s_ref[...]).astype(BF16)
            ypre = _dot(mixed, w_ref[...])
            ds_ref[...] += _colsum(dy * ypre)
            dw_ref[...] += _dot_tn(mixed, dys)
            dmx = _dot_nt(dys, w_ref[...])
            rows = r0 + lax.broadcasted_iota(jnp.int32, (tr, D_POOL), 0)
            cnt = jnp.minimum((rows + 1).astype(F32), win)
            gsc[pl.ds(r0, tr), :] = dmx / cnt
            return carry

        lax.fori_loop(0, nt, tile1, 0)
        n = tr + POOL_HALO
        grp2, win2 = _pool_lane_consts(n)

        def tile2(r, carry):
            r0 = pl.multiple_of(r * tr, SUBLANE)
            g = gsc[pl.ds(r0, n), :]
            s2 = g + pltpu.roll(g, n - 1, 0)
            s4 = s2 + pltpu.roll(s2, n - 2, 0)
            s8 = s4 + pltpu.roll(s4, n - 4, 0)
            s16 = s8 + pltpu.roll(s8, n - 8, 0)
            pooled_t = _pool_select(grp2, s2, s4, s8, s16)
            rows = r0 + lax.broadcasted_iota(jnp.int32, (n, D_POOL), 0)
            cnt = jnp.minimum((rows + 1).astype(F32), win2)
            du = pooled_t - g * cnt
            du_ref[pl.ds(r0, tr), :] = du[0:tr, :].astype(BF16)
            return carry

        lax.fori_loop(0, nt, tile2, 0)

    return pl.pallas_call(
        body, name="pool_bwd",
        grid=(1,),
        in_specs=[pl.BlockSpec((tp, D_POOL), lambda i: (0, 0)), pl.BlockSpec((tp, D_POOL), lambda i: (0, 0)),
                  pl.BlockSpec((D_POOL, D_POOL), lambda i: (0, 0)), pl.BlockSpec((1, D_POOL), lambda i: (0, 0))],
        out_specs=[pl.BlockSpec((tp, D_POOL), lambda i: (0, 0)), pl.BlockSpec((D_POOL, D_POOL), lambda i: (0, 0)),
                   pl.BlockSpec((1, D_POOL), lambda i: (0, 0))],
        out_shape=[jax.ShapeDtypeStruct((tp, D_POOL), BF16), jax.ShapeDtypeStruct((D_POOL, D_POOL), F32),
                   jax.ShapeDtypeStruct((1, D_POOL), F32)],
        scratch_shapes=[pltpu.VMEM((tp + POOL_HALO, D_POOL), F32), pltpu.VMEM((tp + POOL_HALO, D_POOL), F32)],
        compiler_params=_cparams(1),
    )(p, dya, wbd, scale)


def _hgrn_consts(ch):
    t = np.arange(ch)
    levels = []
    w = ch // 2
    while w >= 1:
        levels.append(w)
        w //= 2
    tril = t[None, :] <= t[:, None]
    to_end = t[None, :] > t[:, None]
    mats = [tril, to_end]
    masks = []
    for w in levels:
        pos = t % (2 * w)
        blk = t // (2 * w)
        upper = pos >= w
        mid = blk * 2 * w + w - 1
        mats.append(upper[:, None] & (t[None, :] > mid[:, None]) & (t[None, :] <= t[:, None]))
        mats.append((~upper)[:, None] & (t[None, :] > t[:, None]) & (t[None, :] <= mid[:, None]))
        masks.append(upper[:, None] & (~upper)[None, :] & (blk[:, None] == blk[None, :]))
    masks.append(tril)
    mst = np.concatenate(mats, axis=0).astype(np.float32)
    msk = np.stack(masks).astype(np.float32)
    return jnp.asarray(mst, BF16), jnp.asarray(msk, F32), len(levels)


def _split3(x):
    hi = x.astype(BF16)
    r1 = x - hi.astype(F32)
    mid = r1.astype(BF16)
    lo = (r1 - mid.astype(F32)).astype(BF16)
    return hi, mid, lo


def _hgrn_exponents(mst, logf):
    hi, mid, lo = _split3(logf)
    x = _dot(mst, jnp.concatenate([hi, mid, lo], axis=1))
    return x[:, 0:HEAD] + x[:, HEAD:2 * HEAD] + x[:, 2 * HEAD:3 * HEAD]


def _hgrn_gates(q_raw, z, lb):
    sz = _sigmoid(z)
    f = lb + (1.0 - lb) * sz
    q = q_raw * _sigmoid(q_raw)
    k = (1.0 - lb) * (1.0 - sz)
    return q, k, f, sz


def _hgrn_intra(q, k, x, msk_ref, n_lev, ch):
    a = msk_ref[n_lev] * 0.0
    eye = (lax.broadcasted_iota(jnp.int32, (ch, ch), 0) == lax.broadcasted_iota(jnp.int32, (ch, ch), 1))
    a = jnp.where(eye, jnp.sum(q * k, axis=1, keepdims=True), 0.0)
    ops = []
    for lv in range(n_lev):
        eq = jnp.exp(x[(2 + 2 * lv) * ch:(3 + 2 * lv) * ch, :])
        ek = jnp.exp(x[(3 + 2 * lv) * ch:(4 + 2 * lv) * ch, :])
        qd = q * eq
        kd = k * ek
        a = a + msk_ref[lv] * _dot_nt(qd.astype(BF16), kd.astype(BF16))
        ops.append((eq, ek, qd, kd))
    return a, ops


def _hgrn_fwd(p, lb_logits, gnorm, mst, msk, n_lev, tm):
    tp = p.shape[0]
    ch = HG_CHUNK
    nct = tm // ch
    nt = tp // tm
    nr = mst.shape[0]
    base = D_POOL // HEAD

    def body(q_ref, z_ref, v_ref, g_ref, lg_ref, gn_ref, mst_ref, msk_ref, y_ref, ss_ref, st_sc):
        @pl.when(pl.program_id(1) == 0)
        def _():
            st_sc[...] = jnp.zeros_like(st_sc)

        lb = _sigmoid(lg_ref[0:1, :] - lg_ref[1:2, :])

        def chunk(c, carry):
            r0 = pl.multiple_of(c * ch, ch)
            q, k, f, _ = _hgrn_gates(q_ref[pl.ds(r0, ch), :], z_ref[pl.ds(r0, ch), :], lb)
            v = v_ref[pl.ds(r0, ch), :]
            x = _hgrn_exponents(mst_ref[...], jnp.log(f))
            st = st_sc[...]
            ss_ref[0, c] = st
            qe = q * jnp.exp(x[0:ch, :])
            a, _ = _hgrn_intra(q, k, x, msk_ref, n_lev, ch)
            v16 = v.astype(BF16)
            o = _dot_nt(qe.astype(BF16), st.astype(BF16)) + _dot(a.astype(BF16), v16)
            kl = k * jnp.exp(x[ch:2 * ch, :])
            st_sc[...] = st * jnp.exp(x[ch - 1:ch, :]) + _dot_tn(v16, kl.astype(BF16))
            rstd = lax.rsqrt(jnp.mean(o * o, axis=-1, keepdims=True) + EPS)
            g_raw = g_ref[pl.ds(r0, ch), :]
            y_ref[pl.ds(r0, ch), :] = o * rstd * gn_ref[...] * (g_raw * _sigmoid(g_raw))
            return carry

        lax.fori_loop(0, nct, chunk, 0)

    def pspec(seg):
        return pl.BlockSpec((tm, HEAD), lambda h, i: (i, base + seg * HG_HEADS + h))

    return pl.pallas_call(
        body, name="hgrn_fwd",
        grid=(HG_HEADS, nt),
        in_specs=[pspec(0), pspec(1), pspec(2), pspec(3),
                  pl.BlockSpec((2, HEAD), lambda h, i: (0, h)),
                  pl.BlockSpec((1, HEAD), lambda h, i: (0, 0)),
                  pl.BlockSpec((nr, ch), lambda h, i: (0, 0)),
                  pl.BlockSpec((n_lev + 1, ch, ch), lambda h, i: (0, 0, 0))],
        out_specs=[pl.BlockSpec((tm, HEAD), lambda h, i: (i, h)),
                   pl.BlockSpec((1, nct, HEAD, HEAD), lambda h, i: (h, i, 0, 0))],
        out_shape=[jax.ShapeDtypeStruct((tp, D_HGRN), F32),
                   jax.ShapeDtypeStruct((HG_HEADS, tp // ch, HEAD, HEAD), F32)],
        scratch_shapes=[pltpu.VMEM((HEAD, HEAD), F32)],
        compiler_params=_cparams(2),
    )(p, p, p, p, lb_logits, gnorm, mst, msk)


def _hgrn_bwd(p, dyb, states, lb_logits, gnorm, mst, msk, n_lev, tm):
    tp = p.shape[0]
    ch = HG_CHUNK
    nct = tm // ch
    nt = tp // tm
    nr = mst.shape[0]
    base = D_POOL // HEAD

    def body(q_ref, z_ref, v_ref, g_ref, dy_ref, ss_ref, lg_ref, gn_ref, mst_ref, msk_ref,
             dq_ref, dz_ref, dv_ref, dg_ref, dlg_ref, dgn_ref, dst_sc, dlb_sc, stk):
        ti = pl.program_id(1)

        @pl.when(ti == 0)
        def _():
            dst_sc[...] = jnp.zeros_like(dst_sc)
            dlb_sc[...] = jnp.zeros_like(dlb_sc)
            dgn_ref[...] = jnp.zeros_like(dgn_ref)

        lb = _sigmoid(lg_ref[0:1, :] - lg_ref[1:2, :])
        gn = gn_ref[...]

        def chunk(cc, carry):
            c = nct - 1 - cc
            r0 = pl.multiple_of(c * ch, ch)
            q_raw = q_ref[pl.ds(r0, ch), :]
            z = z_ref[pl.ds(r0, ch), :]
            q, k, f, sz = _hgrn_gates(q_raw, z, lb)
            v = v_ref[pl.ds(r0, ch), :]
            x = _hgrn_exponents(mst_ref[...], jnp.log(f))
            st = ss_ref[0, c]
            eb = jnp.exp(x[0:ch, :])
            ef = jnp.exp(x[ch:2 * ch, :])
            elast = jnp.exp(x[ch - 1:ch, :])
            qe = q * eb
            kl = k * ef
            a, ops = _hgrn_intra(q, k, x, msk_ref, n_lev, ch)
            v16 = v.astype(BF16)
            st16 = st.astype(BF16)
            qe16 = qe.astype(BF16)
            kl16 = kl.astype(BF16)
            a16 = a.astype(BF16)
            o = _dot_nt(qe16, st16) + _dot(a16, v16)
            g_raw = g_ref[pl.ds(r0, ch), :]
            sg = _sigmoid(g_raw)
            rstd = lax.rsqrt(jnp.mean(o * o, axis=-1, keepdims=True) + EPS)
            oh = o * rstd
            dy = dy_ref[pl.ds(r0, ch), :]
            dg_ref[pl.ds(r0, ch), :] = (dy * oh * gn * (sg * (1.0 + g_raw * (1.0 - sg)))).astype(BF16)
            don = dy * (g_raw * sg)
            dgn_ref[0] += _colsum(don * oh)
            doh = don * gn
            do = rstd * (doh - oh * jnp.mean(doh * oh, axis=-1, keepdims=True))
            do16 = do.astype(BF16)
            dst = dst_sc[...]
            dst16 = dst.astype(BF16)
            dv = _dot_tn(a16, do16) + _dot_nt(kl16, dst16)
            da = msk_ref[n_lev] * _dot_nt(do16, v16)
            dqe = _dot(do16, st16)
            dkl = _dot(v16, dst16)
            dst_sc[...] = dst * elast + _dot_tn(do16, qe16)
            db_last = _colsum(dst * st) * elast
            dad = jnp.sum(do * v, axis=1, keepdims=True)
            dq = dad * k + dqe * eb
            dk = dad * q + dkl * ef
            stk[0:ch, :] = dqe * qe
            stk[ch:2 * ch, :] = dkl * kl
            for lv in range(n_lev):
                eq, ek, qd, kd = ops[lv]
                gl = (msk_ref[lv] * da).astype(BF16)
                dqd = _dot(gl, kd.astype(BF16))
                dkd = _dot_tn(gl, qd.astype(BF16))
                dq = dq + dqd * eq
                dk = dk + dkd * ek
                stk[(2 + 2 * lv) * ch:(3 + 2 * lv) * ch, :] = dqd * qd
                stk[(3 + 2 * lv) * ch:(4 + 2 * lv) * ch, :] = dkd * kd
            sk = stk[...]
            hi = sk.astype(BF16)
            lo = (sk - hi.astype(F32)).astype(BF16)
            dl2 = _dot_tn(mst_ref[...], jnp.concatenate([hi, lo], axis=1))
            dlogf = dl2[:, 0:HEAD] + dl2[:, HEAD:2 * HEAD] + db_last
            sq = _sigmoid(q_raw)
            dq_ref[pl.ds(r0, ch), :] = (dq * (sq * (1.0 + q_raw * (1.0 - sq)))).astype(BF16)
            dfk = dlogf / f - dk
            dz_ref[pl.ds(r0, ch), :] = (dfk * (1.0 - lb) * sz * (1.0 - sz)).astype(BF16)
            dlb_sc[...] += _colsum(dfk * (1.0 - sz))
            dv_ref[pl.ds(r0, ch), :] = dv.astype(BF16)
            return carry

        lax.fori_loop(0, nct, chunk, 0)

        @pl.when(ti == nt - 1)
        def _():
            dl0 = dlb_sc[...] * lb * (1.0 - lb)
            dlg_ref[0:1, :] = dl0
            dlg_ref[1:2, :] = -dl0

    def pspec(seg):
        return pl.BlockSpec((tm, HEAD), lambda h, i: (nt - 1 - i, base + seg * HG_HEADS + h))

    ospec = pl.BlockSpec((tm, HEAD), lambda h, i: (nt - 1 - i, h))
    return pl.pallas_call(
        body, name="hgrn_bwd",
        grid=(HG_HEADS, nt),
        in_specs=[pspec(0), pspec(1), pspec(2), pspec(3), ospec,
                  pl.BlockSpec((1, nct, HEAD, HEAD), lambda h, i: (h, nt - 1 - i, 0, 0)),
                  pl.BlockSpec((2, HEAD), lambda h, i: (0, h)),
                  pl.BlockSpec((1, HEAD), lambda h, i: (0, 0)),
                  pl.BlockSpec((nr, ch), lambda h, i: (0, 0)),
                  pl.BlockSpec((n_lev + 1, ch, ch), lambda h, i: (0, 0, 0))],
        out_specs=[ospec, ospec, ospec, ospec,
                   pl.BlockSpec((2, HEAD), lambda h, i: (0, h)),
                   pl.BlockSpec((1, 1, HEAD), lambda h, i: (h, 0, 0))],
        out_shape=[jax.ShapeDtypeStruct((tp, D_HGRN), BF16)] * 4
        + [jax.ShapeDtypeStruct((2, D_HGRN), F32), jax.ShapeDtypeStruct((HG_HEADS, 1, HEAD), F32)],
        scratch_shapes=[pltpu.VMEM((HEAD, HEAD), F32), pltpu.VMEM((1, HEAD), F32), pltpu.VMEM((nr, HEAD), F32)],
        compiler_params=_cparams(2),
    )(p, p, p, p, dyb, states, lb_logits, gnorm, mst, msk)


def _conv_taps(x, w_ref, tr, halo, width):
    acc = None
    for j in range(width):
        sh = width - 1 - j
        xs = x if sh == 0 else pltpu.roll(x, sh, 0)
        term = xs[halo:, :] * w_ref[j:j + 1, :]
        acc = term if acc is None else acc + term
    return acc


def _conv_taps_t(y, w_ref, tr, halo, width):
    n = tr + halo
    acc = None
    for j in range(width):
        sh = width - 1 - j
        ys = y if sh == 0 else pltpu.roll(y, n - sh, 0)
        term = ys[0:tr, :] * w_ref[j:j + 1, :]
        acc = term if acc is None else acc + term
    return acc


def _ln_stats(cv):
    mu = jnp.mean(cv, axis=-1, keepdims=True)
    xc = cv - mu
    rstd = lax.rsqrt(jnp.mean(xc * xc, axis=-1, keepdims=True) + EPS)
    return rstd, xc * rstd


def _convmod_fwd(p, w, bias, ln_g, ln_b, tr):
    tp = p.shape[0]
    nt = tp // tr
    nb = D_CONV // HEAD

    def body(a_ref, b_ref, w_ref, bi_ref, g_ref, be_ref, y_ref, usc):
        usc[0:CONV_HALO, :] = jnp.zeros((CONV_HALO, HEAD), F32)
        usc[CONV_HALO:, :] = a_ref[...] * _sigmoid(b_ref[...])

        def tile(r, carry):
            r0 = pl.multiple_of(r * tr, SUBLANE)
            x = usc[pl.ds(r0, tr + CONV_HALO), :]
            cv = _conv_taps(x, w_ref, tr, CONV_HALO, CONV_WIDTH) + bi_ref[...]
            _, xh = _ln_stats(cv)
            un = xh * g_ref[...] + be_ref[...]
            y_ref[pl.ds(r0, tr), :] = un * _sigmoid(un)
            return carry

        lax.fori_loop(0, nt, tile, 0)

    vec = lambda: pl.BlockSpec((1, HEAD), lambda j: (0, j))
    return pl.pallas_call(
        body, name="convmod_fwd",
        grid=(nb,),
        in_specs=[pl.BlockSpec((tp, HEAD), lambda j: (0, j)), pl.BlockSpec((tp, HEAD), lambda j: (0, nb + j)),
                  pl.BlockSpec((CONV_HALO, HEAD), lambda j: (0, j)), vec(), vec(), vec()],
        out_specs=pl.BlockSpec((tp, HEAD), lambda j: (0, j)),
        out_shape=jax.ShapeDtypeStruct((tp, D_CONV), F32),
        scratch_shapes=[pltpu.VMEM((tp + CONV_HALO, HEAD), F32)],
        compiler_params=_cparams(1),
    )(p, p, w, bias, ln_g, ln_b)


def _convmod_bwd(p, dyc, w, bias, ln_g, ln_b, tr):
    tp = p.shape[0]
    nt = tp // tr
    nb = D_CONV // HEAD

    def body(a_ref, b_ref, dy_ref, w_ref, bi_ref, g_ref, be_ref, da_ref, db_ref, dw_ref, dv_ref, usc, dsc):
        usc[0:CONV_HALO, :] = jnp.zeros((CONV_HALO, HEAD), F32)
        usc[CONV_HALO:, :] = a_ref[...] * _sigmoid(b_ref[...])
        dsc[tp:, :] = jnp.zeros((CONV_HALO, HEAD), F32)
        dw_ref[...] = jnp.zeros_like(dw_ref)
        dv_ref[...] = jnp.zeros_like(dv_ref)

        def tile1(r, carry):
            r0 = pl.multiple_of(r * tr, SUBLANE)
            x = usc[pl.ds(r0, tr + CONV_HALO), :]
            cv = _conv_taps(x, w_ref, tr, CONV_HALO, CONV_WIDTH) + bi_ref[...]
            rstd, xh = _ln_stats(cv)
            un = xh * g_ref[...] + be_ref[...]
            sg = _sigmoid(un)
            dun = dy_ref[pl.ds(r0, tr), :] * (sg * (1.0 + un * (1.0 - sg)))
            dv_ref[0, 1:2, :] += _colsum(dun * xh)
            dv_ref[0, 2:3, :] += _colsum(dun)
            dxh = dun * g_ref[...]
            dcv = rstd * (dxh - jnp.mean(dxh, axis=-1, keepdims=True)
                          - xh * jnp.mean(dxh * xh, axis=-1, keepdims=True))
            dv_ref[0, 0:1, :] += _colsum(dcv)
            for j in range(CONV_WIDTH):
                sh = CONV_WIDTH - 1 - j
                xs = x if sh == 0 else pltpu.roll(x, sh, 0)
                dw_ref[0, j:j + 1, :] += _colsum(dcv * xs[CONV_HALO:, :])
            dsc[pl.ds(r0, tr), :] = dcv
            return carry

        lax.fori_loop(0, nt, tile1, 0)

        def tile2(r, carry):
            r0 = pl.multiple_of(r * tr, SUBLANE)
            y = dsc[pl.ds(r0, tr + CONV_HALO), :]
            du = _conv_taps_t(y, w_ref, tr, CONV_HALO, CONV_WIDTH)
            a = a_ref[pl.ds(r0, tr), :]
            sb = _sigmoid(b_ref[pl.ds(r0, tr), :])
            da_ref[pl.ds(r0, tr), :] = (du * sb).astype(BF16)
            db_ref[pl.ds(r0, tr), :] = (du * a * sb * (1.0 - sb)).astype(BF16)
            return carry

        lax.fori_loop(0, nt, tile2, 0)

    vec = lambda: pl.BlockSpec((1, HEAD), lambda j: (0, j))
    col = lambda: pl.BlockSpec((tp, HEAD), lambda j: (0, j))
    return pl.pallas_call(
        body, name="convmod_bwd",
        grid=(nb,),
        in_specs=[col(), pl.BlockSpec((tp, HEAD), lambda j: (0, nb + j)), col(),
                  pl.BlockSpec((CONV_HALO, HEAD), lambda j: (0, j)), vec(), vec(), vec()],
        out_specs=[col(), col(), pl.BlockSpec((1, CONV_HALO, HEAD), lambda j: (j, 0, 0)),
                   pl.BlockSpec((1, SUBLANE, HEAD), lambda j: (j, 0, 0))],
        out_shape=[jax.ShapeDtypeStruct((tp, D_CONV), BF16), jax.ShapeDtypeStruct((tp, D_CONV), BF16),
                   jax.ShapeDtypeStruct((nb, CONV_HALO, HEAD), F32), jax.ShapeDtypeStruct((nb, SUBLANE, HEAD), F32)],
        scratch_shapes=[pltpu.VMEM((tp + CONV_HALO, HEAD), F32), pltpu.VMEM((tp + CONV_HALO, HEAD), F32)],
        compiler_params=_cparams(1),
    )(p, p, dyc, w, bias, ln_g, ln_b)


def _log1p_small(y):
    return jnp.where(y < 1e-4, y * (1.0 - 0.5 * y), jnp.log(1.0 + y))


def _softplus(x):
    return jnp.maximum(x, 0.0) + _log1p_small(jnp.exp(-jnp.abs(x)))


def _expm1(x):
    return jnp.where(jnp.abs(x) < 1e-2, x * (1.0 + 0.5 * x * (1.0 + x * (1.0 / 3.0))), jnp.exp(x) - 1.0)


def _gelu_parts(x):
    c = 0.7978845608028654
    inner = c * (x + 0.044715 * x * x * x)
    th = jnp.tanh(inner)
    gelu = 0.5 * x * (1.0 + th)
    dgelu = 0.5 * (1.0 + th) + 0.5 * x * (1.0 - th * th) * c * (1.0 + 3.0 * 0.044715 * x * x)
    return gelu, dgelu


def _lru_gates(x_all, tp, cw_ref, cb_ref, wa_ref, ba_ref, wx_ref, bx_ref, lam_ref):
    u = _conv_taps(x_all, cw_ref, tp, LRU_HALO, LRU_CONV) + cb_ref[...]
    u16 = u.astype(BF16)
    r = _sigmoid(_dot(u16, wa_ref[0]) + ba_ref[...])
    i = _sigmoid(_dot(u16, wx_ref[0]) + bx_ref[...])
    sp = _softplus(-lam_ref[...])
    la = -LRU_C * r * sp
    a = jnp.exp(la)
    mult = jnp.sqrt(-_expm1(2.0 * la))
    return u, r, i, a, mult, sp


def _lru_specs(tp, nb):
    col = lambda k: pl.BlockSpec((tp, HEAD), functools.partial(lambda j, k: (0, k * nb + j), k=k))
    vec = lambda: pl.BlockSpec((1, HEAD), lambda j: (0, j))
    mat = lambda: pl.BlockSpec((1, HEAD, HEAD), lambda j: (j, 0, 0))
    return col, vec, mat


def _lru_fwd(p, cw, cb, wa, ba, wx, bx, lam):
    tp = p.shape[0]
    nb = D_LRU // HEAD
    ng = tp // SUBLANE

    def body(x_ref, gt_ref, cw_ref, cb_ref, wa_ref, ba_ref, wx_ref, bx_ref, lam_ref, y_ref, hs_ref,
             xsc, asc, bsc):
        xsc[0:LRU_HALO, :] = jnp.zeros((LRU_HALO, HEAD), F32)
        xsc[LRU_HALO:, :] = x_ref[...]
        u, r, i, a, mult, _ = _lru_gates(xsc[...], tp, cw_ref, cb_ref, wa_ref, ba_ref, wx_ref, bx_ref, lam_ref)
        rows = lax.broadcasted_iota(jnp.int32, (tp, HEAD), 0)
        b = jnp.where(rows == 0, 1.0, mult) * (i * u)
        sub = rows % SUBLANE
        for k in (1, 2, 4):
            m = sub >= k
            b = jnp.where(m, a * pltpu.roll(b, k, 0) + b, b)
            a = jnp.where(m, a * pltpu.roll(a, k, 0), a)
        asc[...] = a
        bsc[...] = b

        def grp(g, carry):
            r0 = pl.multiple_of(g * SUBLANE, SUBLANE)
            h = bsc[pl.ds(r0, SUBLANE), :] + asc[pl.ds(r0, SUBLANE), :] * carry
            hs_ref[pl.ds(r0, SUBLANE), :] = h
            return jnp.broadcast_to(h[SUBLANE - 1:SUBLANE, :], (SUBLANE, HEAD))

        lax.fori_loop(0, ng, grp, jnp.zeros((SUBLANE, HEAD), F32))
        gelu, _ = _gelu_parts(gt_ref[...])
        y_ref[...] = gelu * hs_ref[...]

    col, vec, mat = _lru_specs(tp, nb)
    return pl.pallas_call(
        body, name="lru_fwd",
        grid=(nb,),
        in_specs=[col(2), col(3), pl.BlockSpec((LRU_CONV, HEAD), lambda j: (0, j)), vec(), mat(), vec(), mat(),
                  vec(), vec()],
        out_specs=[pl.BlockSpec((tp, HEAD), lambda j: (0, j)), pl.BlockSpec((tp, HEAD), lambda j: (0, j))],
        out_shape=[jax.ShapeDtypeStruct((tp, D_LRU), F32), jax.ShapeDtypeStruct((tp, D_LRU), F32)],
        scratch_shapes=[pltpu.VMEM((tp + LRU_HALO, HEAD), F32), pltpu.VMEM((tp, HEAD), F32),
                        pltpu.VMEM((tp, HEAD), F32)],
        compiler_params=_cparams(1),
    )(p, p, cw, cb, wa, ba, wx, bx, lam)


def _lru_bwd(p, hs, dyd, cw, cb, wa, ba, wx, bx, lam):
    tp = p.shape[0]
    nb = D_LRU // HEAD
    ng = tp // SUBLANE

    def body(x_ref, gt_ref, hs_ref, dy_ref, cw_ref, cb_ref, wa_ref, ba_ref, wx_ref, bx_ref, lam_ref,
             dx_ref, dgt_ref, dwa_ref, dwx_ref, dv_ref, xsc, asc, bsc, gsc, dusc):
        xsc[0:LRU_HALO, :] = jnp.zeros((LRU_HALO, HEAD), F32)
        xsc[LRU_HALO:, :] = x_ref[...]
        x_all = xsc[...]
        u, r, i, a, mult, sp = _lru_gates(x_all, tp, cw_ref, cb_ref, wa_ref, ba_ref, wx_ref, bx_ref, lam_ref)
        rows = lax.broadcasted_iota(jnp.int32, (tp, HEAD), 0)
        hs = hs_ref[...]
        dy = dy_ref[...]
        gelu, dgelu = _gelu_parts(gt_ref[...])
        dgt_ref[...] = (dy * hs * dgelu).astype(BF16)
        bb = dy * gelu
        aa = jnp.where(rows == tp - 1, 0.0, pltpu.roll(a, tp - 1, 0))
        sub = rows % SUBLANE
        for k in (1, 2, 4):
            m = sub < SUBLANE - k
            bb = jnp.where(m, aa * pltpu.roll(bb, tp - k, 0) + bb, bb)
            aa = jnp.where(m, aa * pltpu.roll(aa, tp - k, 0), aa)
        asc[...] = aa
        bsc[...] = bb

        def grp(gi, carry):
            g = ng - 1 - gi
            r0 = pl.multiple_of(g * SUBLANE, SUBLANE)
            gg = bsc[pl.ds(r0, SUBLANE), :] + asc[pl.ds(r0, SUBLANE), :] * carry
            gsc[pl.ds(r0, SUBLANE), :] = gg
            return jnp.broadcast_to(gg[0:1, :], (SUBLANE, HEAD))

        lax.fori_loop(0, ng, grp, jnp.zeros((SUBLANE, HEAD), F32))
        g = gsc[...]
        first = rows == 0
        hprev = jnp.where(first, 0.0, pltpu.roll(hs, 1, 0))
        iu = i * u
        d_iu = g * jnp.where(first, 1.0, mult)
        dmult_term = jnp.where(first, 0.0, g * iu * (-(a * a) / mult))
        dla = g * hprev * a + dmult_term
        dr = dla * (-LRU_C) * sp
        dv_ref[0, 7:8, :] = _colsum(dla * (LRU_C * r) * _sigmoid(-lam_ref[...]))
        dpr = dr * r * (1.0 - r)
        dpi = d_iu * u * i * (1.0 - i)
        dv_ref[0, 5:6, :] = _colsum(dpr)
        dv_ref[0, 6:7, :] = _colsum(dpi)
        u16 = u.astype(BF16)
        dpr16 = dpr.astype(BF16)
        dpi16 = dpi.astype(BF16)
        dwa_ref[0] = _dot_tn(u16, dpr16)
        dwx_ref[0] = _dot_tn(u16, dpi16)
        du = d_iu * i + _dot_nt(dpr16, wa_ref[0]) + _dot_nt(dpi16, wx_ref[0])
        dv_ref[0, 4:5, :] = _colsum(du)
        for j in range(LRU_CONV):
            sh = LRU_CONV - 1 - j
            xs = x_all if sh == 0 else pltpu.roll(x_all, sh, 0)
            dv_ref[0, j:j + 1, :] = _colsum(du * xs[LRU_HALO:, :])
        dusc[0:tp, :] = du
        dusc[tp:, :] = jnp.zeros((LRU_HALO, HEAD), F32)
        dx_ref[...] = _conv_taps_t(dusc[...], cw_ref, tp, LRU_HALO, LRU_CONV).astype(BF16)

    col, vec, mat = _lru_specs(tp, nb)
    ocol = lambda: pl.BlockSpec((tp, HEAD), lambda j: (0, j))
    return pl.pallas_call(
        body, name="lru_bwd",
        grid=(nb,),
        in_specs=[col(2), col(3), ocol(), ocol(), pl.BlockSpec((LRU_CONV, HEAD), lambda j: (0, j)), vec(), mat(),
                  vec(), mat(), vec(), vec()],
        out_specs=[ocol(), ocol(), mat(), mat(), pl.BlockSpec((1, SUBLANE, HEAD), lambda j: (j, 0, 0))],
        out_shape=[jax.ShapeDtypeStruct((tp, D_LRU), BF16), jax.ShapeDtypeStruct((tp, D_LRU), BF16),
                   jax.ShapeDtypeStruct((nb, HEAD, HEAD), F32), jax.ShapeDtypeStruct((nb, HEAD, HEAD), F32),
                   jax.ShapeDtypeStruct((nb, SUBLANE, HEAD), F32)],
        scratch_shapes=[pltpu.VMEM((tp + LRU_HALO, HEAD), F32), pltpu.VMEM((tp, HEAD), F32),
                        pltpu.VMEM((tp, HEAD), F32), pltpu.VMEM((tp, HEAD), F32),
                        pltpu.VMEM((tp + LRU_HALO, HEAD), F32)],
        compiler_params=_cparams(1),
    )(p, p, hs, dyd, cw, cb, wa, ba, wx, bx, lam)


def _mesh_pos():
    return lax.axis_index("x"), lax.axis_index("y"), lax.axis_index("c")


def _other_chips(x, y):
    return [(1 - x, y), (x, 1 - y), (1 - x, 1 - y)]


ANY = pl.BlockSpec(memory_space=pl.ANY)


def _allgather_shards(arrs, split):
    n = len(arrs)

    def body(*refs):
        ins, outs = refs[:n], refs[n:2 * n]
        send1, recv1, send2, recv2, send3, recv3 = refs[2 * n:]
        x, y, c = _mesh_pos()
        chip = 2 * x + y
        sibling = (x, y, 1 - c)
        others = _other_chips(x, y)

        def rows(k, cc):
            half = arrs[k].shape[0] // 2
            return pl.ds(cc * half, half)

        def remote(src, dst, ssem, rsem, dev):
            return pltpu.make_async_remote_copy(src_ref=src, dst_ref=dst, send_sem=ssem, recv_sem=rsem,
                                                device_id=dev, device_id_type=MESH)

        started = [remote(ins[k], outs[k].at[chip], send3.at[k], recv3.at[k], sibling) for k in range(n)]
        for cp in started:
            cp.start()
        for k in range(n):
            for j, (ox, oy) in enumerate(others):
                if split[k]:
                    src, dst = ins[k].at[rows(k, c)], outs[k].at[chip, rows(k, c)]
                else:
                    src, dst = ins[k], outs[k].at[chip]
                cp = remote(src, dst, send1.at[3 * k + j], recv1.at[3 * k + j], (ox, oy, c))
                cp.start()
                started.append(cp)
        for j, (ox, oy) in enumerate(others):
            ochip = 2 * ox + oy
            for k in range(n):
                if split[k]:
                    blk = outs[k].at[ochip, rows(k, c)]
                    remote(blk, blk, send1.at[3 * k + j], recv1.at[3 * k + j], sibling).wait_recv()
                    cp = remote(blk, blk, send2.at[3 * k + j], recv2.at[3 * k + j], sibling)
                    cp.start()
                    started.append(cp)
                else:
                    blk = outs[k].at[ochip]
                    remote(blk, blk, send1.at[3 * k + j], recv1.at[3 * k + j], sibling).wait_recv()
        for j, (ox, oy) in enumerate(others):
            ochip = 2 * ox + oy
            for k in range(n):
                if split[k]:
                    blk = outs[k].at[ochip, rows(k, 1 - c)]
                    remote(blk, blk, send2.at[3 * k + j], recv2.at[3 * k + j], sibling).wait_recv()
        for k in range(n):
            blk = outs[k].at[chip]
            remote(blk, blk, send3.at[k], recv3.at[k], sibling).wait_recv()
        for cp in started:
            cp.wait_send()

    return pl.pallas_call(
        body, name="allgather_shards",
        in_specs=[ANY] * n, out_specs=[ANY] * n,
        out_shape=[jax.ShapeDtypeStruct((N_SHARD,) + a.shape, a.dtype) for a in arrs],
        scratch_shapes=[pltpu.SemaphoreType.DMA((3 * n,)), pltpu.SemaphoreType.DMA((3 * n,)),
                        pltpu.SemaphoreType.DMA((3 * n,)), pltpu.SemaphoreType.DMA((3 * n,)),
                        pltpu.SemaphoreType.DMA((n,)), pltpu.SemaphoreType.DMA((n,))],
    )(*arrs)


HBM = pl.BlockSpec(memory_space=pltpu.HBM)
SEM = pl.BlockSpec(memory_space=pltpu.SEMAPHORE)
DATAFLOW = pltpu.SideEffectType.DATAFLOW_SIDE_EFFECTING
N_PEERS = 4


def _in_hbm(a):
    return pltpu.with_memory_space_constraint(a, pltpu.HBM)


def _gather_peers(x, y, c):
    return [((ox, oy, c), 2 * ox + oy) for ox, oy in _other_chips(x, y)] + [((x, y, 1 - c), 2 * x + y)]


def _gather_start(arrs):
    n = len(arrs)

    def body(*refs):
        ins, lands = refs[:n], refs[n:2 * n]
        ssem, rsem = refs[2 * n:2 * n + 2]
        token = refs[-1]
        x, y, c = _mesh_pos()
        chip = 2 * x + y
        for k in range(n):
            for j, (dev, _) in enumerate(_gather_peers(x, y, c)):
                pltpu.make_async_remote_copy(
                    src_ref=ins[k], dst_ref=lands[k].at[chip], send_sem=ssem.at[N_PEERS * k + j],
                    recv_sem=rsem.at[N_PEERS * k + j], device_id=dev, device_id_type=MESH).start()
        token[...] = jnp.zeros_like(token)

    lands = [_in_hbm(lax.empty((N_SHARD,) + a.shape, a.dtype)) for a in arrs]
    out = pl.pallas_call(
        body, name="gather_start",
        in_specs=[HBM] * (2 * n),
        out_specs=[SEM, SEM] + [HBM] * (2 * n) + [pl.BlockSpec(memory_space=pltpu.VMEM)],
        out_shape=[pltpu.SemaphoreType.DMA((N_PEERS * n,)), pltpu.SemaphoreType.DMA((N_PEERS * n,))]
        + [pltpu.HBM(a.shape, a.dtype) for a in arrs]
        + [pltpu.HBM((N_SHARD,) + a.shape, a.dtype) for a in arrs]
        + [jax.ShapeDtypeStruct((SUBLANE, LANE), F32)],
        input_output_aliases={k: 2 + k for k in range(2 * n)},
        compiler_params=pltpu.CompilerParams(has_side_effects=DATAFLOW),
    )(*[_in_hbm(a) for a in arrs], *lands)
    return out[0], out[1], list(out[2:2 + n]), list(out[2 + n:2 + 2 * n]), out[-1]


def _gather_wait(ssem, rsem, srcs, lands, ks, after):
    n = len(ks)

    def body(*refs):
        ins, lnd = refs[:n], refs[n:2 * n]
        ssem_ref, rsem_ref = refs[2 * n:2 * n + 2]
        x, y, c = _mesh_pos()
        for i, k in enumerate(ks):
            for j, (dev, pchip) in enumerate(_gather_peers(x, y, c)):
                cp = pltpu.make_async_remote_copy(
                    src_ref=ins[i], dst_ref=lnd[i].at[pchip], send_sem=ssem_ref.at[N_PEERS * k + j],
                    recv_sem=rsem_ref.at[N_PEERS * k + j], device_id=dev, device_id_type=MESH)
                cp.wait_send()
                cp.wait_recv()

    out = pl.pallas_call(
        body, name="gather_wait",
        in_specs=[HBM] * (2 * n) + [SEM, SEM, ANY],
        out_specs=[HBM] * (2 * n),
        out_shape=[pltpu.HBM(a.shape, a.dtype) for a in srcs] + [pltpu.HBM(a.shape, a.dtype) for a in lands],
        input_output_aliases={k: k for k in range(2 * n)},
        compiler_params=pltpu.CompilerParams(has_side_effects=DATAFLOW),
    )(*srcs, *lands, ssem, rsem, after)
    return list(out[n:])


def _scatter_start(arrs, slots):
    n = len(arrs)

    def body(*refs):
        ins, lands = refs[:n], refs[n:2 * n]
        ssem, rsem = refs[2 * n:2 * n + 2]
        token = refs[-1]
        x, y, c = _mesh_pos()
        chip = 2 * x + y
        for k in range(n):
            for j, (ox, oy) in enumerate(_other_chips(x, y)):
                pltpu.make_async_remote_copy(
                    src_ref=ins[k].at[2 * ox + oy], dst_ref=lands[k].at[chip], send_sem=ssem.at[3 * k + j],
                    recv_sem=rsem.at[3 * k + j], device_id=(ox, oy, c), device_id_type=MESH).start()
        token[...] = jnp.zeros_like(token)

    out = pl.pallas_call(
        body, name="scatter_start",
        in_specs=[HBM] * (2 * n),
        out_specs=[SEM, SEM] + [HBM] * (2 * n) + [pl.BlockSpec(memory_space=pltpu.VMEM)],
        out_shape=[pltpu.SemaphoreType.DMA((3 * n,)), pltpu.SemaphoreType.DMA((3 * n,))]
        + [pltpu.HBM(a.shape, a.dtype) for a in arrs] + [pltpu.HBM(a.shape, a.dtype) for a in slots]
        + [jax.ShapeDtypeStruct((SUBLANE, LANE), F32)],
        input_output_aliases={k: 2 + k for k in range(2 * n)},
        compiler_params=pltpu.CompilerParams(has_side_effects=DATAFLOW),
    )(*[_in_hbm(a) for a in arrs], *[_in_hbm(a) for a in slots])
    return out[0], out[1], list(out[2:2 + n]), list(out[2 + n:2 + 2 * n]), out[-1]


def _scatter_wait(ssem, rsem, arrs, slots, after):
    n = len(arrs)

    def body(*refs):
        ins, lnd = refs[:n], refs[n:2 * n]
        ssem_ref, rsem_ref = refs[2 * n:2 * n + 2]
        x, y, c = _mesh_pos()
        for k in range(n):
            for j, (ox, oy) in enumerate(_other_chips(x, y)):
                ochip = 2 * ox + oy
                cp = pltpu.make_async_remote_copy(
                    src_ref=ins[k].at[ochip], dst_ref=lnd[k].at[ochip], send_sem=ssem_ref.at[3 * k + j],
                    recv_sem=rsem_ref.at[3 * k + j], device_id=(ox, oy, c), device_id_type=MESH)
                cp.wait_send()
                cp.wait_recv()

    out = pl.pallas_call(
        body, name="scatter_wait",
        in_specs=[HBM] * (2 * n) + [SEM, SEM, ANY],
        out_specs=[HBM] * (2 * n),
        out_shape=[pltpu.HBM(a.shape, a.dtype) for a in arrs] + [pltpu.HBM(a.shape, a.dtype) for a in slots],
        input_output_aliases={k: k for k in range(2 * n)},
        compiler_params=pltpu.CompilerParams(has_side_effects=DATAFLOW),
    )(*arrs, *slots, ssem, rsem, after)
    return list(out[n:])


def _pair_exchange_halves(arrs):
    n = len(arrs)

    def body(*refs):
        ins, outs = refs[:n], refs[n:2 * n]
        ssem, rsem = refs[2 * n:]
        x, y, c = _mesh_pos()
        cps = []
        for k in range(n):
            half = arrs[k].shape[1] // 2
            cp = pltpu.make_async_remote_copy(
                src_ref=ins[k].at[:, pl.ds((1 - c) * half, half)], dst_ref=outs[k],
                send_sem=ssem.at[k], recv_sem=rsem.at[k], device_id=(x, y, 1 - c), device_id_type=MESH)
            cp.start()
            cps.append(cp)
        for cp in cps:
            cp.wait()

    return pl.pallas_call(
        body, name="pair_exchange_halves",
        in_specs=[ANY] * n, out_specs=[ANY] * n,
        out_shape=[jax.ShapeDtypeStruct((a.shape[0], a.shape[1] // 2, a.shape[2]), a.dtype) for a in arrs],
        scratch_shapes=[pltpu.SemaphoreType.DMA((n,)), pltpu.SemaphoreType.DMA((n,))],
    )(*arrs)


GRAD_ROW_BLOCKS = 2


def _pair_add(arrs, recvd, core):
    n = len(arrs)
    nb = GRAD_ROW_BLOCKS

    def body(c_ref, *refs):
        for k in range(n):
            refs[2 * n + k][...] = (refs[k][...].astype(F32) + refs[n + k][...].astype(F32)).astype(BF16)

    def blk(a):
        return (1, a.shape[1] // 2 // nb, a.shape[2])

    grid_spec = pltpu.PrefetchScalarGridSpec(
        num_scalar_prefetch=1, grid=(N_SHARD, nb),
        in_specs=[pl.BlockSpec(blk(a), lambda s, i, c: (s, c[0] * nb + i, 0)) for a in arrs]
        + [pl.BlockSpec(blk(a), lambda s, i, c: (s, i, 0)) for a in arrs],
        out_specs=[pl.BlockSpec(blk(a), lambda s, i, c: (s, i, 0)) for a in arrs])
    return pl.pallas_call(
        body, name="pair_add", grid_spec=grid_spec,
        out_shape=[jax.ShapeDtypeStruct(r.shape, BF16) for r in recvd],
        compiler_params=_cparams(2),
    )(core, *arrs, *recvd)


def _own_slot(arrs, chip):
    n = len(arrs)
    nb = GRAD_ROW_BLOCKS

    def body(c_ref, *refs):
        for k in range(n):
            refs[n + k][...] = refs[k][...]

    def blk(a):
        return (1, a.shape[1] // nb, a.shape[2])

    grid_spec = pltpu.PrefetchScalarGridSpec(
        num_scalar_prefetch=1, grid=(nb,),
        in_specs=[pl.BlockSpec(blk(a), lambda i, c: (c[0], i, 0)) for a in arrs],
        out_specs=[pl.BlockSpec(blk(a), lambda i, c: (c[0], i, 0)) for a in arrs])
    return pl.pallas_call(
        body, name="own_slot", grid_spec=grid_spec,
        out_shape=[jax.ShapeDtypeStruct(a.shape, a.dtype) for a in arrs],
        compiler_params=_cparams(1),
    )(chip, *arrs)


def _scatter_to_owners(arrs, slots):
    n = len(arrs)

    def body(*refs):
        ins, outs = refs[:n], refs[2 * n:3 * n]
        ssem, rsem = refs[3 * n:]
        x, y, c = _mesh_pos()
        chip = 2 * x + y
        others = _other_chips(x, y)
        cps = []
        for k in range(n):
            for j, (ox, oy) in enumerate(others):
                cp = pltpu.make_async_remote_copy(
                    src_ref=ins[k].at[2 * ox + oy], dst_ref=outs[k].at[chip],
                    send_sem=ssem.at[3 * k + j], recv_sem=rsem.at[3 * k + j],
                    device_id=(ox, oy, c), device_id_type=MESH)
                cp.start()
                cps.append(cp)
        for k in range(n):
            for j, (ox, oy) in enumerate(others):
                blk = outs[k].at[2 * ox + oy]
                pltpu.make_async_remote_copy(
                    src_ref=blk, dst_ref=blk, send_sem=ssem.at[3 * k + j], recv_sem=rsem.at[3 * k + j],
                    device_id=(ox, oy, c), device_id_type=MESH).wait_recv()
        for cp in cps:
            cp.wait_send()

    return pl.pallas_call(
        body, name="scatter_to_owners",
        in_specs=[ANY] * (2 * n), out_specs=[ANY] * n,
        out_shape=[jax.ShapeDtypeStruct(a.shape, a.dtype) for a in arrs],
        scratch_shapes=[pltpu.SemaphoreType.DMA((3 * n,)), pltpu.SemaphoreType.DMA((3 * n,))],
        input_output_aliases={n + k: k for k in range(n)},
    )(*arrs, *slots)


def _sum_chips(arrs, core):
    n = len(arrs)
    nb = GRAD_ROW_BLOCKS

    def body(c_ref, *refs):
        for k in range(n):
            r = refs[k]
            refs[n + k][...] = ((r[0].astype(F32) + r[1].astype(F32)) + r[2].astype(F32)) + r[3].astype(F32)

    grid_spec = pltpu.PrefetchScalarGridSpec(
        num_scalar_prefetch=1, grid=(nb,),
        in_specs=[pl.BlockSpec((N_SHARD, a.shape[1] // nb, a.shape[2]), lambda i, c: (0, i, 0)) for a in arrs],
        out_specs=[pl.BlockSpec((a.shape[1] // nb, a.shape[2]), lambda i, c: (c[0] * nb + i, 0)) for a in arrs])
    return pl.pallas_call(
        body, name="sum_chips", grid_spec=grid_spec,
        out_shape=[jax.ShapeDtypeStruct((2 * a.shape[1], a.shape[2]), F32) for a in arrs],
        compiler_params=_cparams(1),
    )(core, *arrs)


def _pair_allgather_halves(arrs):
    n = len(arrs)

    def body(*refs):
        outs = refs[n:2 * n]
        ssem, rsem = refs[2 * n:]
        x, y, c = _mesh_pos()
        cps = []
        for k in range(n):
            h = arrs[k].shape[0] // 2
            mine = outs[k].at[pl.ds(c * h, h)]
            cp = pltpu.make_async_remote_copy(src_ref=mine, dst_ref=mine, send_sem=ssem.at[k],
                                              recv_sem=rsem.at[k], device_id=(x, y, 1 - c), device_id_type=MESH)
            cp.start()
            cps.append(cp)
        for k, cp in enumerate(cps):
            h = arrs[k].shape[0] // 2
            theirs = outs[k].at[pl.ds((1 - c) * h, h)]
            pltpu.make_async_remote_copy(src_ref=theirs, dst_ref=theirs, send_sem=ssem.at[k], recv_sem=rsem.at[k],
                                         device_id=(x, y, 1 - c), device_id_type=MESH).wait_recv()
            cp.wait_send()

    return pl.pallas_call(
        body, name="pair_allgather_halves",
        in_specs=[ANY] * n, out_specs=[ANY] * n,
        out_shape=[jax.ShapeDtypeStruct(a.shape, a.dtype) for a in arrs],
        scratch_shapes=[pltpu.SemaphoreType.DMA((n,)), pltpu.SemaphoreType.DMA((n,))],
        input_output_aliases={k: k for k in range(n)},
    )(*arrs)


N_DEV = 8


def _allgather_all(v):
    m_per, n = v.shape

    def body(x_ref, out_ref, send_sems, recv_sems, local_sem):
        x, y, c = _mesh_pos()
        me, sibling = (x, y, c), (x, y, 1 - c)
        chips = _other_chips(x, y)

        def rows(px, py, pc):
            return out_ref.at[pl.ds((4 * px + 2 * py + pc) * m_per, m_per), :]

        def copy(k, block, to, src=None):
            return pltpu.make_async_remote_copy(
                src_ref=rows(*block) if src is None else src, dst_ref=rows(*block),
                send_sem=send_sems.at[k], recv_sem=recv_sems.at[k], device_id=to, device_id_type=MESH)

        mine = pltpu.make_async_copy(x_ref, rows(*me), local_sem)
        mine.start()
        first = [copy(0, me, sibling, src=x_ref)]
        first += [copy(1 + j, me, (*chip, c), src=x_ref) for j, chip in enumerate(chips)]
        for cp in first:
            cp.start()
        passed = [copy(4 + j, (*chip, c), sibling) for j, chip in enumerate(chips)]
        for j, chip in enumerate(chips):
            copy(1 + j, (*chip, c), me).wait_recv()
            passed[j].start()
        copy(0, sibling, me).wait_recv()
        for j, chip in enumerate(chips):
            copy(4 + j, (*chip, 1 - c), me).wait_recv()
        for cp in first + passed:
            cp.wait_send()
        mine.wait()

    return pl.pallas_call(
        body, name="allgather_all",
        out_shape=jax.ShapeDtypeStruct((N_DEV * m_per, n), v.dtype),
        in_specs=[pl.BlockSpec(memory_space=pltpu.VMEM)],
        out_specs=pl.BlockSpec(memory_space=pltpu.VMEM),
        scratch_shapes=[pltpu.SemaphoreType.DMA((7,)), pltpu.SemaphoreType.DMA((7,)), pltpu.SemaphoreType.DMA],
        compiler_params=pltpu.CompilerParams(vmem_limit_bytes=VMEM_LIMIT_MB * 1024 * 1024),
    )(v)


def _adamw_math(w, g, m, v):
    m2 = ADAM_B1 * m + (1.0 - ADAM_B1) * g
    v2 = ADAM_B2 * v + (1.0 - ADAM_B2) * (g * g)
    m_hat = m2 / (1.0 - ADAM_B1 ** ADAM_STEP)
    v_hat = v2 / (1.0 - ADAM_B2 ** ADAM_STEP)
    delta = -ADAM_LR * (m_hat / (jnp.sqrt(v_hat) + ADAM_EPS) + ADAM_WD * w)
    return delta, m2, v2


def _adamw(w, m, v, gs, nblk):
    nl, r, n = w.shape
    assert nl == len(gs) and nl in (1, 2)
    br = r // nblk

    def body(w_ref, m_ref, v_ref, *rest):
        g_refs, (go_ref, d_ref, mo_ref, vo_ref) = rest[:nl], rest[nl:]
        g = g_refs[0][...]
        if nl == 2:
            g = jnp.where(pl.program_id(0) == 0, g, g_refs[1][...])
        delta, m2, v2 = _adamw_math(w_ref[0], g, m_ref[0], v_ref[0])
        go_ref[0] = g
        d_ref[0] = delta
        mo_ref[0] = m2
        vo_ref[0] = v2

    spec = pl.BlockSpec((1, br, n), lambda l, i: (l, i, 0))
    g_specs = [pl.BlockSpec((br, n), lambda l, i: (i, 0))] if nl == 1 else [
        pl.BlockSpec((br, n), lambda l, i: (jnp.where(l == 0, i, nblk - 1), 0)),
        pl.BlockSpec((br, n), lambda l, i: (jnp.where(l == 1, i, 0), 0))]
    return pl.pallas_call(
        body, name="adamw", grid=(nl, nblk),
        in_specs=[spec, spec, spec] + g_specs,
        out_specs=[spec] * 4,
        out_shape=[jax.ShapeDtypeStruct((nl, r, n), F32)] * 4,
        compiler_params=_cparams(2),
    )(w, m, v, *gs)


def _small_reduce_adamw(parts, w, m, v, rep_rows, sh_rows):
    mrows = rep_rows + N_SHARD * sh_rows

    def body(p_ref, w_ref, m_ref, v_ref, go_ref, d_ref, mo_ref, vo_ref):
        x, y, _ = _mesh_pos()
        mine = rep_rows + (2 * x + y) * sh_rows
        g_rep = p_ref[0:rep_rows, :]
        g_sh = p_ref[pl.ds(pl.multiple_of(mine, SUBLANE), sh_rows), :]
        for k in range(1, N_DEV):
            g_rep = g_rep + p_ref[k * mrows:k * mrows + rep_rows, :]
            g_sh = g_sh + p_ref[pl.ds(pl.multiple_of(k * mrows + mine, SUBLANE), sh_rows), :]
        g = jnp.concatenate([g_rep, g_sh], axis=0)
        delta, m2, v2 = _adamw_math(w_ref[...], g, m_ref[...], v_ref[...])
        go_ref[...] = g
        d_ref[...] = delta
        mo_ref[...] = m2
        vo_ref[...] = v2

    return pl.pallas_call(
        body, name="small_reduce_adamw",
        out_shape=[jax.ShapeDtypeStruct((rep_rows + sh_rows, 128), F32)] * 4,
        compiler_params=pltpu.CompilerParams(vmem_limit_bytes=VMEM_LIMIT_MB * 1024 * 1024),
    )(parts, w, m, v)


LANE = 128
REP_SPEC = (("ffn1_norm", 16), ("mix_norm", 16), ("ffn2_norm", 16), ("final_norm", 8), ("pool_w", 128),
            ("pool_scale", 8), ("hgrn_lb_logits", 16), ("hgrn_gnorm", 8), ("lru_wa", 256), ("lru_wx", 256))
SH_SPEC = (("meta_tokens", 32), ("conv_w", 32), ("lru_conv_w", 8), ("conv_b", 8), ("conv_ln_g", 8),
           ("conv_ln_b", 8), ("lru_conv_b", 8), ("lru_ba", 8), ("lru_bx", 8), ("lru_lambda", 8))
REP_ROWS = sum(r for _, r in REP_SPEC)
SH_ROWS = sum(r for _, r in SH_SPEC)


def _pack_rows(vals, spec):
    parts = []
    for name, rows in spec:
        flat = vals[name].astype(F32).reshape(-1, LANE)
        if flat.shape[0] < rows:
            flat = jnp.concatenate([flat, jnp.zeros((rows - flat.shape[0], LANE), F32)], axis=0)
        parts.append(flat)
    return jnp.concatenate(parts, axis=0)


def _unpack_rows(packed, spec, shapes):
    out = {}
    off = 0
    for name, rows in spec:
        shp = shapes[name]
        n = int(np.prod(shp)) // LANE
        out[name] = packed[off:off + n].reshape(shp)
        off += rows
    return out


def _block_diag(blocks):
    n, b, _ = blocks.shape
    return sum(jnp.pad(blocks[g], ((g * b, (n - 1 - g) * b), (g * b, (n - 1 - g) * b))) for g in range(n))


def _diag_blocks(mat, n):
    b = mat.shape[0] // n
    return jnp.stack([mat[g * b:(g + 1) * b, g * b:(g + 1) * b] for g in range(n)])


BIG = ("ffn1_wg", "ffn1_wu", "ffn2_wg", "ffn2_wu", "ffn1_wd", "ffn2_wd", "w_in_even", "w_out_even",
       "w_in_odd", "w_out_odd")
ADAM_BLOCKS = {"ffn1_wg": 8, "ffn1_wu": 8, "ffn2_wg": 8, "ffn2_wu": 8, "ffn1_wd": 4, "ffn2_wd": 4,
               "w_in_even": 4, "w_out_even": 2, "w_in_odd": 4, "w_out_odd": 2}
WEIGHT_NAMES = ('meta_tokens', 'ffn1_norm', 'ffn1_wg', 'ffn1_wu', 'ffn1_wd', 'mix_norm', 'ffn2_norm', 'ffn2_wg',
                'ffn2_wu', 'ffn2_wd', 'w_in_even', 'pool_w', 'pool_scale', 'hgrn_lb_logits', 'hgrn_gnorm',
                'w_out_even', 'w_in_odd', 'conv_w', 'conv_b', 'conv_ln_g', 'conv_ln_b', 'lru_conv_w',
                'lru_conv_b', 'lru_wa', 'lru_ba', 'lru_wx', 'lru_bx', 'lru_lambda', 'w_out_odd', 'final_norm')


def _rows2d(a):
    return a.reshape(-1, a.shape[-1])


def _local_step_v2(x, tgt, w, gathered, small_full):
    s_len, d = x.shape
    t_real = s_len + N_META
    tp = -(-t_real // ROW_ALIGN) * ROW_ALIGN
    tm = _tile(tp, 832, ROW_ALIGN)
    tm_small = _tile(tp, 416, 16)
    tr = _tile(tp, 416, SUBLANE)
    f1 = ("ffn1_norm", "ffn1_wg", "ffn1_wu", "ffn1_wd")
    f2 = ("ffn2_norm", "ffn2_wg", "ffn2_wu", "ffn2_wd")

    meta_full = small_full["meta_tokens"]
    h0 = jnp.concatenate([meta_full, x, jnp.zeros((tp - t_real, d), F32)], axis=0)
    tgt_pad = jnp.concatenate([jnp.zeros((N_META, d), F32), tgt, jnp.zeros((tp - t_real, d), F32)], axis=0)

    w_in_even = jnp.transpose(gathered["w_in_even"], (1, 0, 2)).reshape(d, D_IN_EVEN)
    w_out_even = gathered["w_out_even"].reshape(d, d)
    w_out_odd = gathered["w_out_odd"].reshape(d, d)
    even_piece = [(w_in_even, (d, D_IN_EVEN), (0, 0))]
    odd_pieces = [(gathered["w_in_odd"], (1, d, D_IN_ODD // N_SHARD), (k, 0, 0)) for k in range(N_SHARD)]
    pool_wbd = _block_diag(w["pool_w"][0]).astype(BF16)
    pool_scale = w["pool_scale"]
    wa_bd = _block_diag2(w["lru_wa"][0]).astype(BF16)
    wx_bd = _block_diag2(w["lru_wx"][0]).astype(BF16)
    mst, msk, n_lev = _hgrn_consts(HG_CHUNK)
    conv_w = small_full["conv_w"]
    sf = small_full

    def gain(name, layer):
        return w[name][layer:layer + 1]

    def ffn(h, names, layer):
        return _ffn_fwd(h, gain(names[0], layer), gathered[names[1]], gathered[names[2]], gathered[names[3]],
                        layer, tm)

    h1, a1, b1, n1 = ffn(h0, f1, 0)
    p0, nm0 = _proj_fwd(h1, gain("mix_norm", 0), 0, even_piece, tm_small)
    ya = _pool_fwd(p0, pool_wbd, pool_scale, tr)
    yb, states = _hgrn_fwd(p0, w["hgrn_lb_logits"], w["hgrn_gnorm"], mst, msk, n_lev, tm)
    h2 = _out_fwd(h1, ya, yb, w_out_even, tm)
    h3, a2, b2, n2 = ffn(h2, f2, 0)
    h4, a3, b3, n3 = ffn(h3, f1, 1)
    p1, nm1 = _proj_fwd(h4, gain("mix_norm", 1), 1, odd_pieces, tm_small)
    yc = _convmod_fwd(p1, conv_w, sf["conv_b"], sf["conv_ln_g"], sf["conv_ln_b"], tr)
    lru_args = (sf["lru_conv_w"], sf["lru_conv_b"], wa_bd, sf["lru_ba"], wx_bd, sf["lru_bx"], sf["lru_lambda"])
    yd, hs = _lru_fwd(p1, *lru_args)
    h5 = _out_fwd(h4, yc, yd, w_out_odd, tm)
    h6, a4, b4, n4 = ffn(h5, f2, 1)
    loss, dh6, dg_final = _loss_bwd(h6, w["final_norm"].reshape(1, d), tgt_pad, t_real, tm)

    def ffn_bwd(dho, h, n, a, b, names, layer, acc):
        dh, da, db, dg = _ffn_bwd_act(dho, h, gain(names[0], layer), a, b, gathered[names[1]], gathered[names[2]],
                                      gathered[names[3]], layer, tm_small)
        acc = _ffn_bwd_w(dho, n, a, b, da, db, acc[0], acc[1], acc[2], layer, tm)
        return dh, dg, acc

    none3 = (None, None, None)
    dh5, dg_f2_l1, g_f2 = ffn_bwd(dh6, h5, n4, a4, b4, f2, 1, none3)
    dyc, dyd, dw_out_odd = _out_bwd(dh5, yc, yd, w_out_odd, tm)
    dca, dcb, dconv_w, dconv_vec = _convmod_bwd(p1, dyc, conv_w, sf["conv_b"], sf["conv_ln_g"], sf["conv_ln_b"], tr)
    dlx, dlg, dwa_bd, dwx_bd, dlru_vec = _lru_bwd(p1, hs, dyd, *lru_args)
    dp1 = [dca, dcb, dlx, dlg]
    dh4, dg_mix_l1 = _proj_bwd_act(dh5, h4, gain("mix_norm", 1), 1, dp1, odd_pieces, tm_small)
    dw_in_odd = jnp.stack(_proj_bwd_w(nm1, dp1, tm))
    dh3, dg_f1_l1, g_f1 = ffn_bwd(dh4, h3, n3, a3, b3, f1, 1, none3)
    dh2, dg_f2_l0, g_f2 = ffn_bwd(dh3, h2, n2, a2, b2, f2, 0, g_f2)
    dya, dyb, dw_out_even = _out_bwd(dh2, ya, yb, w_out_even, tm)
    dpool, dpool_wbd, dpool_scale = _pool_bwd(p0, dya, pool_wbd, pool_scale, tr)
    dq, dz, dv, dgate, dlb_logits, dgn_heads = _hgrn_bwd(p0, dyb, states, w["hgrn_lb_logits"], w["hgrn_gnorm"],
                                                         mst, msk, n_lev, tm)
    dp0 = [jnp.concatenate([dpool, dq, dz, dv, dgate], axis=1)]
    dh1, dg_mix_l0 = _proj_bwd_act(dh2, h1, gain("mix_norm", 0), 0, dp0, even_piece, tm_small)
    (dw_in_even,) = _proj_bwd_w(nm0, dp0, tm_small)
    dh0, dg_f1_l0, g_f1 = ffn_bwd(dh1, h0, n1, a1, b1, f1, 0, g_f1)

    grad_x = dh0[N_META:t_real]
    big = {
        "ffn1_wg": g_f1[0], "ffn1_wu": g_f1[1], "ffn1_wd": g_f1[2],
        "ffn2_wg": g_f2[0], "ffn2_wu": g_f2[1], "ffn2_wd": g_f2[2],
        "w_in_even": jnp.transpose(dw_in_even.reshape(d, N_SHARD, D_IN_EVEN // N_SHARD), (1, 0, 2)),
        "w_out_even": dw_out_even.reshape(N_SHARD, d // N_SHARD, d),
        "w_in_odd": dw_in_odd,
        "w_out_odd": dw_out_odd.reshape(N_SHARD, d // N_SHARD, d),
    }
    rep = {
        "ffn1_norm": jnp.concatenate([dg_f1_l0, dg_f1_l1], axis=0),
        "mix_norm": jnp.concatenate([dg_mix_l0, dg_mix_l1], axis=0),
        "ffn2_norm": jnp.concatenate([dg_f2_l0, dg_f2_l1], axis=0),
        "final_norm": dg_final,
        "pool_w": _diag_blocks(dpool_wbd, len(POOL_WINDOWS)),
        "pool_scale": dpool_scale,
        "hgrn_lb_logits": dlb_logits,
        "hgrn_gnorm": jnp.sum(dgn_heads, axis=0),
        "lru_wa": _diag_blocks2(dwa_bd),
        "lru_wx": _diag_blocks2(dwx_bd),
    }
    dmeta = jnp.transpose(dh0[:N_META].reshape(N_META, N_SHARD, 2, LANE), (1, 0, 2, 3)).reshape(N_SHARD, 32, LANE)
    packs = [_pack_rows(rep, REP_SPEC)]
    for s in range(N_SHARD):
        sh = {
            "meta_tokens": dmeta[s], "conv_w": dconv_w[s], "lru_conv_w": dlru_vec[s, 0:4],
            "conv_b": dconv_vec[s, 0:1], "conv_ln_g": dconv_vec[s, 1:2], "conv_ln_b": dconv_vec[s, 2:3],
            "lru_conv_b": dlru_vec[s, 4:5], "lru_ba": dlru_vec[s, 5:6], "lru_bx": dlru_vec[s, 6:7],
            "lru_lambda": dlru_vec[s, 7:8],
        }
        packs.append(_pack_rows(sh, SH_SPEC))
    return loss, grad_x, big, jnp.concatenate(packs, axis=0)


def _block_diag2(heads):
    nb = heads.shape[0] // 2
    return jnp.stack([_block_diag(heads[2 * j:2 * j + 2]) for j in range(nb)])


def _diag_blocks2(mats):
    return jnp.concatenate([_diag_blocks(mats[j], 2) for j in range(mats.shape[0])], axis=0)


def _kernel_v2(x, meta_tokens, ffn1_norm, ffn1_wg, ffn1_wu, ffn1_wd, mix_norm, ffn2_norm, ffn2_wg, ffn2_wu, ffn2_wd, w_in_even, pool_w, pool_scale, hgrn_lb_logits, hgrn_gnorm, w_out_even, w_in_odd, conv_w, conv_b, conv_ln_g, conv_ln_b, lru_conv_w, lru_conv_b, lru_wa, lru_ba, lru_wx, lru_bx, lru_lambda, w_out_odd, final_norm, loss_target, m_meta_tokens, m_ffn1_norm, m_ffn1_wg, m_ffn1_wu, m_ffn1_wd, m_mix_norm, m_ffn2_norm, m_ffn2_wg, m_ffn2_wu, m_ffn2_wd, m_w_in_even, m_pool_w, m_pool_scale, m_hgrn_lb_logits, m_hgrn_gnorm, m_w_out_even, m_w_in_odd, m_conv_w, m_conv_b, m_conv_ln_g, m_conv_ln_b, m_lru_conv_w, m_lru_conv_b, m_lru_wa, m_lru_ba, m_lru_wx, m_lru_bx, m_lru_lambda, m_w_out_odd, m_final_norm, v_meta_tokens, v_ffn1_norm, v_ffn1_wg, v_ffn1_wu, v_ffn1_wd, v_mix_norm, v_ffn2_norm, v_ffn2_wg, v_ffn2_wu, v_ffn2_wd, v_w_in_even, v_pool_w, v_pool_scale, v_hgrn_lb_logits, v_hgrn_gnorm, v_w_out_even, v_w_in_odd, v_conv_w, v_conv_b, v_conv_ln_g, v_conv_ln_b, v_lru_conv_w, v_lru_conv_b, v_lru_wa, v_lru_ba, v_lru_wx, v_lru_bx, v_lru_lambda, v_w_out_odd, v_final_norm):
    args = locals()
    w = {n: args[n] for n in WEIGHT_NAMES}
    m = {n: args["m_" + n] for n in WEIGHT_NAMES}
    v = {n: args["v_" + n] for n in WEIGHT_NAMES}
    shapes = {n: w[n].shape for n in WEIGHT_NAMES}

    big_in = [_rows2d(w[n]).astype(BF16) for n in BIG]
    small_sh = _pack_rows(w, SH_SPEC)
    gath = _allgather_shards(big_in + [small_sh], [True] * len(BIG) + [False])
    gathered = dict(zip(BIG, gath[:len(BIG)]))
    sm = gath[len(BIG)]
    sh_shapes = {n: (N_SHARD,) + tuple(shapes[n]) for n, _ in SH_SPEC}
    per_shard = [_unpack_rows(sm[s], SH_SPEC, shapes) for s in range(N_SHARD)]
    small_full = {}
    for n, _ in SH_SPEC:
        stacked = [per_shard[s][n] for s in range(N_SHARD)]
        small_full[n] = jnp.concatenate([p.reshape(-1, p.shape[-1]) for p in stacked], axis=-1)
    small_full["conv_w"] = jnp.concatenate(
        [small_full["conv_w"], jnp.zeros((CONV_HALO - CONV_WIDTH, D_CONV), F32)], axis=0)

    loss, grad_x, big, small_part = _local_step(x[0], loss_target[0], w, gathered, small_full)
    loss = lax.psum(loss[0, 0], ("x", "y", "c"))

    core = lax.axis_index("c").astype(jnp.int32).reshape(1)
    parts = [big[n] for n in BIG]
    recvd = _pair_exchange_halves(parts)
    pair = _pair_add(parts, recvd, core)
    chip = (2 * lax.axis_index("x") + lax.axis_index("y")).astype(jnp.int32).reshape(1)
    slots = _scatter_to_owners(pair, _own_slot(pair, chip))
    halves = _sum_chips(slots, core)
    full = _pair_allgather_halves(halves)
    out_g, out_d, out_m, out_v = {}, {}, {}, {}
    for n, g in zip(BIG, full):
        res = _adamw(_rows2d(w[n]), _rows2d(m[n]), _rows2d(v[n]), g, 0, ADAM_BLOCKS[n])
        out_g[n], out_d[n], out_m[n], out_v[n] = [r.reshape(shapes[n]) for r in res]

    gathered_small = _allgather_all(small_part)

    def pack_small(src):
        return jnp.concatenate([_pack_rows(src, REP_SPEC), _pack_rows(src, SH_SPEC)], axis=0)

    res = _small_reduce_adamw(gathered_small, pack_small(w), pack_small(m), pack_small(v), REP_ROWS, SH_ROWS)
    for dst, packed in zip((out_g, out_d, out_m, out_v), res):
        dst.update(_unpack_rows(packed[:REP_ROWS], REP_SPEC, shapes))
        dst.update(_unpack_rows(packed[REP_ROWS:], SH_SPEC, shapes))

    return (loss, grad_x[None], *[out_g[n] for n in WEIGHT_NAMES], *[out_d[n] for n in WEIGHT_NAMES],
            *[out_m[n] for n in WEIGHT_NAMES], *[out_v[n] for n in WEIGHT_NAMES])


GATHER_GROUPS = (
    (("small", 0),),
    (("ffn1_wg", 0), ("ffn1_wu", 0), ("ffn1_wd", 0)),
    (("w_in_even", 0), ("w_out_even", 0)),
    (("ffn2_wg", 0), ("ffn2_wu", 0), ("ffn2_wd", 0)),
    (("ffn1_wg", 1), ("ffn1_wu", 1), ("ffn1_wd", 1)),
    (("w_in_odd", 0), ("w_out_odd", 0)),
    (("ffn2_wg", 1), ("ffn2_wu", 1), ("ffn2_wd", 1)),
)
ADAM_ROW_BLOCKS = {"ffn1_wg": 2, "ffn1_wu": 2, "ffn2_wg": 2, "ffn2_wu": 2, "ffn1_wd": 2, "ffn2_wd": 2,
                   "w_in_even": 4, "w_out_even": 2, "w_in_odd": 4, "w_out_odd": 2}
TRANSPOSED = ("ffn1_wg", "ffn1_wu", "ffn2_wg", "ffn2_wu", "w_in_even")
SCATTER_DEPTH = 2


def _unpack_small(sm, shapes):
    per_shard = [_unpack_rows(sm[s], SH_SPEC, shapes) for s in range(N_SHARD)]
    full = {}
    for n, _ in SH_SPEC:
        full[n] = jnp.concatenate([per_shard[s][n].reshape(-1, shapes[n][-1]) for s in range(N_SHARD)], axis=-1)
    full["conv_w"] = jnp.concatenate([full["conv_w"], jnp.zeros((CONV_HALO - CONV_WIDTH, D_CONV), F32)], axis=0)
    return full


def _local_step(x, tgt, w, shapes, fetch, emit):
    s_len, d = x.shape
    t_real = s_len + N_META
    tp = -(-t_real // ROW_ALIGN) * ROW_ALIGN
    tm = _tile(tp, 832, ROW_ALIGN)
    tm_small = _tile(tp, 416, 16)
    tr = _tile(tp, 416, SUBLANE)

    def gain(name, layer):
        return w[name][layer:layer + 1]

    pool_wbd = _block_diag(w["pool_w"][0]).astype(BF16)
    pool_scale = w["pool_scale"]
    wa_bd = _block_diag2(w["lru_wa"][0]).astype(BF16)
    wx_bd = _block_diag2(w["lru_wx"][0]).astype(BF16)
    mst, msk, n_lev = _hgrn_consts(HG_CHUNK)

    (sm,) = fetch(0, None)
    sf = _unpack_small(sm, shapes)
    h0 = jnp.concatenate([sf["meta_tokens"], x, jnp.zeros((tp - t_real, d), F32)], axis=0)
    tgt_pad = jnp.concatenate([jnp.zeros((N_META, d), F32), tgt, jnp.zeros((tp - t_real, d), F32)], axis=0)
    f1l0 = fetch(1, h0)
    h1, a1, b1, n1 = _ffn_fwd(h0, gain("ffn1_norm", 0), *f1l0, 0, tm)
    w_in_even4, w_out_even4 = fetch(2, h1)
    w_out_even = w_out_even4.reshape(d, d)
    even_piece = [(w_in_even4.reshape(D_IN_EVEN, d), (D_IN_EVEN, d), (0, 0))]
    p0, nm0 = _proj_fwd(h1, gain("mix_norm", 0), 0, even_piece, tm_small, wt=True)
    ya = _pool_fwd(p0, pool_wbd, pool_scale, tr)
    yb, states = _hgrn_fwd(p0, w["hgrn_lb_logits"], w["hgrn_gnorm"], mst, msk, n_lev, tm)
    h2 = _out_fwd(h1, ya, yb, w_out_even, tm)
    f2l0 = fetch(3, h2)
    h3, a2, b2, n2 = _ffn_fwd(h2, gain("ffn2_norm", 0), *f2l0, 0, tm)
    f1l1 = fetch(4, h3)
    h4, a3, b3, n3 = _ffn_fwd(h3, gain("ffn1_norm", 1), *f1l1, 0, tm)
    w_in_odd4, w_out_odd4 = fetch(5, h4)
    w_out_odd = w_out_odd4.reshape(d, d)
    odd_pieces = [(w_in_odd4, (1, d, D_IN_ODD // N_SHARD), (k, 0, 0)) for k in range(N_SHARD)]
    p1, nm1 = _proj_fwd(h4, gain("mix_norm", 1), 1, odd_pieces, tm_small)
    yc = _convmod_fwd(p1, sf["conv_w"], sf["conv_b"], sf["conv_ln_g"], sf["conv_ln_b"], tr)
    lru_args = (sf["lru_conv_w"], sf["lru_conv_b"], wa_bd, sf["lru_ba"], wx_bd, sf["lru_bx"], sf["lru_lambda"])
    yd, hs = _lru_fwd(p1, *lru_args)
    h5 = _out_fwd(h4, yc, yd, w_out_odd, tm)
    f2l1 = fetch(6, h5)
    h6, a4, b4, n4 = _ffn_fwd(h5, gain("ffn2_norm", 1), *f2l1, 0, tm)
    loss, dh6, dg_final = _loss_bwd(h6, w["final_norm"].reshape(1, d), tgt_pad, t_real, tm)

    def ffn_bwd(dho, h, n, a, b, norm, wts, after=()):
        dh, da, db, dg = _ffn_bwd_act(dho, h, norm, a, b, *wts, 0, tm_small, after)
        return dh, dg, _ffn_bwd_w(dho, n, a, b, da, db, tm)

    dh5, dg_f2_l1, g = ffn_bwd(dh6, h5, n4, a4, b4, gain("ffn2_norm", 1), f2l1)
    sent = emit((("ffn2_wg", 1), ("ffn2_wu", 1), ("ffn2_wd", 1)), g)
    dyc, dyd, dw_out_odd = _out_bwd(dh5, yc, yd, w_out_odd, tm, tuple(sent))
    dca, dcb, dconv_w, dconv_vec = _convmod_bwd(p1, dyc, sf["conv_w"], sf["conv_b"], sf["conv_ln_g"],
                                                sf["conv_ln_b"], tr)
    dlx, dlg, dwa_bd, dwx_bd, dlru_vec = _lru_bwd(p1, hs, dyd, *lru_args)
    dp1 = [dca, dcb, dlx, dlg]
    dh4, dg_mix_l1 = _proj_bwd_act(dh5, h4, gain("mix_norm", 1), 1, dp1, odd_pieces, tm_small)
    dw_in_odd = jnp.stack(_proj_bwd_w(nm1, dp1, tm))
    dh3, dg_f1_l1, g = ffn_bwd(dh4, h3, n3, a3, b3, gain("ffn1_norm", 1), f1l1)
    sent = emit((("w_out_odd", 0), ("w_in_odd", 0), ("ffn1_wg", 1), ("ffn1_wu", 1), ("ffn1_wd", 1)),
                [dw_out_odd.reshape(N_SHARD, d // N_SHARD, d), dw_in_odd] + list(g))
    dh2, dg_f2_l0, g_f2l0 = ffn_bwd(dh3, h2, n2, a2, b2, gain("ffn2_norm", 0), f2l0, tuple(sent))
    dya, dyb, dw_out_even = _out_bwd(dh2, ya, yb, w_out_even, tm)
    dpool, dpool_wbd, dpool_scale = _pool_bwd(p0, dya, pool_wbd, pool_scale, tr)
    dq, dz, dv, dgate, dlb_logits, dgn_heads = _hgrn_bwd(p0, dyb, states, w["hgrn_lb_logits"], w["hgrn_gnorm"],
                                                         mst, msk, n_lev, tm)
    dp0 = [jnp.concatenate([dpool, dq, dz, dv, dgate], axis=1)]
    dh1, dg_mix_l0 = _proj_bwd_act(dh2, h1, gain("mix_norm", 0), 0, dp0, even_piece, tm_small, wt=True)
    (dw_in_even_t,) = _proj_bwd_w(nm0, dp0, tm_small, wt=True)
    sent = emit((("ffn2_wg", 0), ("ffn2_wu", 0), ("ffn2_wd", 0), ("w_out_even", 0), ("w_in_even", 0)),
                list(g_f2l0) + [dw_out_even.reshape(N_SHARD, d // N_SHARD, d),
                                dw_in_even_t.reshape(N_SHARD, D_IN_EVEN // N_SHARD, d)])
    dh0, dg_f1_l0, g = ffn_bwd(dh1, h0, n1, a1, b1, gain("ffn1_norm", 0), f1l0, tuple(sent))
    emit((("ffn1_wg", 0), ("ffn1_wu", 0), ("ffn1_wd", 0)), g)

    grad_x = dh0[N_META:t_real]
    rep = {
        "ffn1_norm": jnp.concatenate([dg_f1_l0, dg_f1_l1], axis=0),
        "mix_norm": jnp.concatenate([dg_mix_l0, dg_mix_l1], axis=0),
        "ffn2_norm": jnp.concatenate([dg_f2_l0, dg_f2_l1], axis=0),
        "final_norm": dg_final,
        "pool_w": _diag_blocks(dpool_wbd, len(POOL_WINDOWS)),
        "pool_scale": dpool_scale,
        "hgrn_lb_logits": dlb_logits,
        "hgrn_gnorm": jnp.sum(dgn_heads, axis=0),
        "lru_wa": _diag_blocks2(dwa_bd),
        "lru_wx": _diag_blocks2(dwx_bd),
    }
    dmeta = jnp.transpose(dh0[:N_META].reshape(N_META, N_SHARD, 2, LANE), (1, 0, 2, 3)).reshape(N_SHARD, 32, LANE)
    packs = [_pack_rows(rep, REP_SPEC)]
    for s in range(N_SHARD):
        sh = {
            "meta_tokens": dmeta[s], "conv_w": dconv_w[s], "lru_conv_w": dlru_vec[s, 0:4],
            "conv_b": dconv_vec[s, 0:1], "conv_ln_g": dconv_vec[s, 1:2], "conv_ln_b": dconv_vec[s, 2:3],
            "lru_conv_b": dlru_vec[s, 4:5], "lru_ba": dlru_vec[s, 5:6], "lru_bx": dlru_vec[s, 6:7],
            "lru_lambda": dlru_vec[s, 7:8],
        }
        packs.append(_pack_rows(sh, SH_SPEC))
    return loss, grad_x, jnp.concatenate(packs, axis=0)


def kernel(x, meta_tokens, ffn1_norm, ffn1_wg, ffn1_wu, ffn1_wd, mix_norm, ffn2_norm, ffn2_wg, ffn2_wu, ffn2_wd, w_in_even, pool_w, pool_scale, hgrn_lb_logits, hgrn_gnorm, w_out_even, w_in_odd, conv_w, conv_b, conv_ln_g, conv_ln_b, lru_conv_w, lru_conv_b, lru_wa, lru_ba, lru_wx, lru_bx, lru_lambda, w_out_odd, final_norm, loss_target, m_meta_tokens, m_ffn1_norm, m_ffn1_wg, m_ffn1_wu, m_ffn1_wd, m_mix_norm, m_ffn2_norm, m_ffn2_wg, m_ffn2_wu, m_ffn2_wd, m_w_in_even, m_pool_w, m_pool_scale, m_hgrn_lb_logits, m_hgrn_gnorm, m_w_out_even, m_w_in_odd, m_conv_w, m_conv_b, m_conv_ln_g, m_conv_ln_b, m_lru_conv_w, m_lru_conv_b, m_lru_wa, m_lru_ba, m_lru_wx, m_lru_bx, m_lru_lambda, m_w_out_odd, m_final_norm, v_meta_tokens, v_ffn1_norm, v_ffn1_wg, v_ffn1_wu, v_ffn1_wd, v_mix_norm, v_ffn2_norm, v_ffn2_wg, v_ffn2_wu, v_ffn2_wd, v_w_in_even, v_pool_w, v_pool_scale, v_hgrn_lb_logits, v_hgrn_gnorm, v_w_out_even, v_w_in_odd, v_conv_w, v_conv_b, v_conv_ln_g, v_conv_ln_b, v_lru_conv_w, v_lru_conv_b, v_lru_wa, v_lru_ba, v_lru_wx, v_lru_bx, v_lru_lambda, v_w_out_odd, v_final_norm):
    args = locals()
    w = {n: args[n] for n in WEIGHT_NAMES}
    m = {n: args["m_" + n] for n in WEIGHT_NAMES}
    v = {n: args["v_" + n] for n in WEIGHT_NAMES}
    shapes = {n: w[n].shape for n in WEIGHT_NAMES}
    core = lax.axis_index("c").astype(jnp.int32).reshape(1)
    chip = (2 * lax.axis_index("x") + lax.axis_index("y")).astype(jnp.int32).reshape(1)

    def view(a, n):
        return jnp.swapaxes(a, 1, 2) if n in TRANSPOSED else a

    wv, mv, vv = [{n: view(src[n], n) for n in BIG} for src in (w, m, v)]

    def shard(key):
        n, l = key
        return _pack_rows(w, SH_SPEC) if n == "small" else wv[n][l].astype(BF16)

    started = {}
    for groups in (GATHER_GROUPS[:2], GATHER_GROUPS[2:]):
        gkeys = [key for grp in groups for key in grp]
        ssem, rsem, srcs, lands, token = _gather_start([shard(key) for key in gkeys])
        for k, key in enumerate(gkeys):
            started[key] = (ssem, rsem, srcs[k], lands[k], k, token)

    def fetch(group, after):
        st = [started[key] for key in GATHER_GROUPS[group]]
        return _gather_wait(st[0][0], st[0][1], [s[2] for s in st], [s[3] for s in st], [s[4] for s in st],
                            st[0][5] if after is None else after)

    in_flight, reduced = [], {}

    def collect(entry, after):
        gkeys, gs_sem, gr_sem, pair_thru, slots_thru, _ = entry
        slots = _scatter_wait(gs_sem, gr_sem, pair_thru, slots_thru, after)
        full = _pair_allgather_halves(_sum_chips(slots, core))
        reduced.update(zip(gkeys, full))
        return full[0]

    def emit(gkeys, grads):
        grads = list(grads)
        pair = _pair_add(grads, _pair_exchange_halves(grads), core)
        in_flight.append((gkeys,) + tuple(_scatter_start(pair, _own_slot(pair, chip))))
        token = in_flight[-1][-1]
        if len(in_flight) > SCATTER_DEPTH:
            return token, collect(in_flight[-1 - SCATTER_DEPTH], token)
        return (token,)

    loss, grad_x, small_part = _local_step(x[0], loss_target[0], w, shapes, fetch, emit)
    loss = lax.psum(loss[0, 0], ("x", "y", "c"))
    for entry in in_flight[-SCATTER_DEPTH:]:
        collect(entry, grad_x)

    out_g, out_d, out_m, out_v = {}, {}, {}, {}
    for n in BIG:
        gs = [reduced[(n, l)] for l in range(shapes[n][0])]
        res = _adamw(wv[n], mv[n], vv[n], gs, ADAM_ROW_BLOCKS[n])
        out_g[n], out_d[n], out_m[n], out_v[n] = [view(r, n) for r in res]

    gathered_small = _allgather_all(small_part)

    def pack_small(src):
        return jnp.concatenate([_pack_rows(src, REP_SPEC), _pack_rows(src, SH_SPEC)], axis=0)

    res = _small_reduce_adamw(gathered_small, pack_small(w), pack_small(m), pack_small(v), REP_ROWS, SH_ROWS)
    for dst, packed in zip((out_g, out_d, out_m, out_v), res):
        dst.update(_unpack_rows(packed[:REP_ROWS], REP_SPEC, shapes))
        dst.update(_unpack_rows(packed[REP_ROWS:], SH_SPEC, shapes))

    return (loss, grad_x[None], *[out_g[n] for n in WEIGHT_NAMES], *[out_d[n] for n in WEIGHT_NAMES],
            *[out_m[n] for n in WEIGHT_NAMES], *[out_v[n] for n in WEIGHT_NAMES])
```

```python
import functools

import numpy as np
import jax
import jax.numpy as jnp
from jax import lax
from jax.experimental import pallas as pl
from jax.experimental.pallas import tpu as pltpu

F32 = jnp.float32
BF16 = jnp.bfloat16
MESH = pl.DeviceIdType.MESH

EPS = 1e-6
N_META = 16
D_MODEL = 1024
D_FF = 2816
N_SHARD = 4
FF_SHARD = D_FF // N_SHARD
D_POOL = 256
POOL_GROUP = 64
POOL_WINDOWS = (2, 4, 8, 16)
D_HGRN = 768
HG_HEADS = 6
HEAD = 128
HG_CHUNK = 64
HG_HEADS_PER_STEP = 2
D_IN_EVEN = D_POOL + 4 * D_HGRN
D_CONV = 512
CONV_WIDTH = 31
CONV_HALO = 32
D_LRU = 512
LRU_CONV = 4
LRU_HALO = 8
LRU_C = 8.0
D_IN_ODD = 2 * D_CONV + 2 * D_LRU
SUBLANE = 8
ROW_ALIGN = 64

ADAM_LR = 0.001
ADAM_B1 = 0.9
ADAM_B2 = 0.999
ADAM_EPS = 1e-08
ADAM_WD = 0.01
ADAM_STEP = 10

VMEM_LIMIT_MB = 56


def _cparams(n_grid_axes=0, vmem_mb=VMEM_LIMIT_MB):
    sem = ("arbitrary",) * n_grid_axes if n_grid_axes else None
    return pltpu.CompilerParams(dimension_semantics=sem, vmem_limit_bytes=vmem_mb * 1024 * 1024)


def _tile(n, target, mult):
    best = None
    for t in range(mult, min(n, target) + 1, mult):
        if n % t == 0:
            best = t
    assert best is not None, (n, target, mult)
    return best


def _dot(a, b):
    return jnp.dot(a, b, preferred_element_type=F32)


def _dot_nt(a, b):
    return lax.dot_general(a, b, (((1,), (1,)), ((), ())), preferred_element_type=F32)


def _dot_tn(a, b):
    return lax.dot_general(a, b, (((0,), (0,)), ((), ())), preferred_element_type=F32)


def _sigmoid(x):
    return 1.0 / (1.0 + jnp.exp(-x))


def _colsum(x):
    return jnp.sum(x, axis=0, keepdims=True)


def _rms_stats(h):
    rstd = lax.rsqrt(jnp.mean(h * h, axis=-1, keepdims=True) + EPS)
    return rstd, h * rstd


def _rms_bwd(dn, g, rstd, xhat):
    dng = dn * g
    dh = rstd * (dng - xhat * jnp.mean(dng * xhat, axis=-1, keepdims=True))
    return dh, _colsum(dn * xhat)


def _ffn_fwd(h, norm, wg4, wu4, wd4, layer, tm):
    tp, d = h.shape
    nt = tp // tm

    def body(h_ref, g_ref, wg_ref, wu_ref, wd_ref, ho_ref, a_ref, b_ref, n_ref, n_sc, acc):
        s = pl.program_id(1)

        @pl.when(s == 0)
        def _():
            hh = h_ref[...]
            rstd, xhat = _rms_stats(hh)
            n = (xhat * g_ref[...]).astype(BF16)
            n_sc[...] = n
            n_ref[...] = n
            acc[...] = jnp.zeros_like(acc)

        n = n_sc[...]
        a = _dot_nt(n, wg_ref[0])
        b = _dot_nt(n, wu_ref[0])
        a_ref[0] = a.astype(BF16)
        b_ref[0] = b.astype(BF16)
        sg = (a * _sigmoid(a) * b).astype(BF16)
        acc[...] += _dot(sg, wd_ref[0])

        @pl.when(s == N_SHARD - 1)
        def _():
            ho_ref[...] = h_ref[...] + 0.5 * acc[...]

    return pl.pallas_call(
        body, name="ffn_fwd",
        grid=(nt, N_SHARD),
        in_specs=[
            pl.BlockSpec((tm, d), lambda i, s: (i, 0)),
            pl.BlockSpec((1, d), lambda i, s: (0, 0)),
            pl.BlockSpec((1, FF_SHARD, d), lambda i, s: (s, layer, 0)),
            pl.BlockSpec((1, FF_SHARD, d), lambda i, s: (s, layer, 0)),
            pl.BlockSpec((1, FF_SHARD, d), lambda i, s: (s, layer, 0)),
        ],
        out_specs=[
            pl.BlockSpec((tm, d), lambda i, s: (i, 0)),
            pl.BlockSpec((1, tm, FF_SHARD), lambda i, s: (s, i, 0)),
            pl.BlockSpec((1, tm, FF_SHARD), lambda i, s: (s, i, 0)),
            pl.BlockSpec((tm, d), lambda i, s: (i, 0)),
        ],
        out_shape=[
            jax.ShapeDtypeStruct((tp, d), F32),
            jax.ShapeDtypeStruct((N_SHARD, tp, FF_SHARD), BF16),
            jax.ShapeDtypeStruct((N_SHARD, tp, FF_SHARD), BF16),
            jax.ShapeDtypeStruct((tp, d), BF16),
        ],
        scratch_shapes=[pltpu.VMEM((tm, d), BF16), pltpu.VMEM((tm, d), F32)],
        compiler_params=_cparams(2),
    )(h, norm, wg4, wu4, wd4)


def _ffn_bwd_act(dho, h, norm, a4, b4, wg4, wu4, wd4, layer, tm, after=()):
    tp, d = h.shape
    nt = tp // tm

    def body(dho_ref, h_ref, g_ref, a_ref, b_ref, wg_ref, wu_ref, wd_ref, *rest):
        dh_ref, da_ref, db_ref, dg_ref, dy_sc, dn_sc = rest[len(after):]
        i = pl.program_id(0)
        s = pl.program_id(1)

        @pl.when(s == 0)
        def _():
            dy_sc[...] = (0.5 * dho_ref[...]).astype(BF16)
            dn_sc[...] = jnp.zeros_like(dn_sc)

        @pl.when((s == 0) & (i == 0))
        def _():
            dg_ref[...] = jnp.zeros_like(dg_ref)

        ds = _dot_nt(dy_sc[...], wd_ref[0])
        a = a_ref[0].astype(F32)
        b = b_ref[0].astype(F32)
        sig = _sigmoid(a)
        da = (ds * b * (sig * (1.0 + a * (1.0 - sig)))).astype(BF16)
        db = (ds * (a * sig)).astype(BF16)
        da_ref[0] = da
        db_ref[0] = db
        dn_sc[...] += _dot(da, wg_ref[0]) + _dot(db, wu_ref[0])

        @pl.when(s == N_SHARD - 1)
        def _():
            rstd, xhat = _rms_stats(h_ref[...])
            dh, dg = _rms_bwd(dn_sc[...], g_ref[...], rstd, xhat)
            dh_ref[...] = dho_ref[...] + dh
            dg_ref[...] += dg

    return pl.pallas_call(
        body, name="ffn_bwd_act",
        grid=(nt, N_SHARD),
        in_specs=[
            pl.BlockSpec((tm, d), lambda i, s: (i, 0)),
            pl.BlockSpec((tm, d), lambda i, s: (i, 0)),
            pl.BlockSpec((1, d), lambda i, s: (0, 0)),
            pl.BlockSpec((1, tm, FF_SHARD), lambda i, s: (s, i, 0)),
            pl.BlockSpec((1, tm, FF_SHARD), lambda i, s: (s, i, 0)),
            pl.BlockSpec((1, FF_SHARD, d), lambda i, s: (s, layer, 0)),
            pl.BlockSpec((1, FF_SHARD, d), lambda i, s: (s, layer, 0)),
            pl.BlockSpec((1, FF_SHARD, d), lambda i, s: (s, layer, 0)),
        ] + [pl.BlockSpec(memory_space=pl.ANY)] * len(after),
        out_specs=[
            pl.BlockSpec((tm, d), lambda i, s: (i, 0)),
            pl.BlockSpec((1, tm, FF_SHARD), lambda i, s: (s, i, 0)),
            pl.BlockSpec((1, tm, FF_SHARD), lambda i, s: (s, i, 0)),
            pl.BlockSpec((1, d), lambda i, s: (0, 0)),
        ],
        out_shape=[
            jax.ShapeDtypeStruct((tp, d), F32),
            jax.ShapeDtypeStruct((N_SHARD, tp, FF_SHARD), BF16),
            jax.ShapeDtypeStruct((N_SHARD, tp, FF_SHARD), BF16),
            jax.ShapeDtypeStruct((1, d), F32),
        ],
        scratch_shapes=[pltpu.VMEM((tm, d), BF16), pltpu.VMEM((tm, d), F32)],
        compiler_params=_cparams(2),
    )(dho, h, norm, a4, b4, wg4, wu4, wd4, *after)


def _ffn_bwd_w(dho, n, a4, b4, da4, db4, tm):
    tp, d = n.shape
    nt = tp // tm

    def body(dho_ref, n_ref, a_ref, b_ref, da_ref, db_ref, og_ref, ou_ref, od_ref, accg, accu, accd):
        i = pl.program_id(1)

        @pl.when(i == 0)
        def _():
            accg[...] = jnp.zeros_like(accg)
            accu[...] = jnp.zeros_like(accu)
            accd[...] = jnp.zeros_like(accd)

        nn = n_ref[...]
        accg[...] += _dot_tn(da_ref[0], nn)
        accu[...] += _dot_tn(db_ref[0], nn)
        a = a_ref[0].astype(F32)
        b = b_ref[0].astype(F32)
        sact = (a * _sigmoid(a) * b).astype(BF16)
        dy = (0.5 * dho_ref[...]).astype(BF16)
        accd[...] += _dot_tn(sact, dy)

        @pl.when(i == nt - 1)
        def _():
            og_ref[0] = accg[...].astype(BF16)
            ou_ref[0] = accu[...].astype(BF16)
            od_ref[0] = accd[...].astype(BF16)

    in_specs = [
        pl.BlockSpec((tm, d), lambda s, i: (i, 0)),
        pl.BlockSpec((tm, d), lambda s, i: (i, 0)),
        pl.BlockSpec((1, tm, FF_SHARD), lambda s, i: (s, i, 0)),
        pl.BlockSpec((1, tm, FF_SHARD), lambda s, i: (s, i, 0)),
        pl.BlockSpec((1, tm, FF_SHARD), lambda s, i: (s, i, 0)),
        pl.BlockSpec((1, tm, FF_SHARD), lambda s, i: (s, i, 0)),
    ]
    return pl.pallas_call(
        body, name="ffn_bwd_w",
        grid=(N_SHARD, nt),
        in_specs=in_specs,
        out_specs=[pl.BlockSpec((1, FF_SHARD, d), lambda s, i: (s, 0, 0))] * 3,
        out_shape=[jax.ShapeDtypeStruct((N_SHARD, FF_SHARD, d), BF16)] * 3,
        scratch_shapes=[pltpu.VMEM((FF_SHARD, d), F32)] * 3,
        compiler_params=_cparams(2),
    )(dho, n, a4, b4, da4, db4)


def _proj_fwd(h, norm, layer, w_pieces, tm, wt=False):
    tp, d = h.shape
    widths = [bs[-2] if wt else bs[-1] for _, bs, _ in w_pieces]
    ntot = sum(widths)
    npc = len(w_pieces)

    def body(*refs):
        h_ref, g_ref = refs[:2]
        w_refs = refs[2:2 + npc]
        p_ref, n_ref = refs[2 + npc:]
        rstd, xhat = _rms_stats(h_ref[...])
        n = (xhat * g_ref[...]).astype(BF16)
        n_ref[...] = n
        off = 0
        for k in range(npc):
            w = w_refs[k][...]
            w = w.reshape(w.shape[-2], w.shape[-1])
            p_ref[:, off:off + widths[k]] = _dot_nt(n, w) if wt else _dot(n, w)
            off += widths[k]

    in_specs = [pl.BlockSpec((tm, d), lambda i: (i, 0)), pl.BlockSpec((1, d), lambda i: (0, 0))]
    for _, bs, idx in w_pieces:
        in_specs.append(pl.BlockSpec(bs, functools.partial(lambda i, idx: idx, idx=idx)))
    return pl.pallas_call(
        body, name="proj_fwd",
        grid=(tp // tm,),
        in_specs=in_specs,
        out_specs=[pl.BlockSpec((tm, ntot), lambda i: (i, 0)), pl.BlockSpec((tm, d), lambda i: (i, 0))],
        out_shape=[jax.ShapeDtypeStruct((tp, ntot), F32), jax.ShapeDtypeStruct((tp, d), BF16)],
        compiler_params=_cparams(1),
    )(h, norm, *[w for w, _, _ in w_pieces])


def _proj_bwd_act(dres, h, norm, layer, dp_pieces, w_pieces, tm, wt=False):
    tp, d = h.shape
    npc = len(w_pieces)

    def body(*refs):
        dres_ref, h_ref, g_ref = refs[:3]
        dp_refs = refs[3:3 + npc]
        w_refs = refs[3 + npc:3 + 2 * npc]
        dh_ref, dg_ref = refs[3 + 2 * npc:]
        i = pl.program_id(0)

        @pl.when(i == 0)
        def _():
            dg_ref[...] = jnp.zeros_like(dg_ref)

        dn = None
        for k in range(npc):
            w = w_refs[k][...]
            w = w.reshape(w.shape[-2], w.shape[-1])
            t = _dot(dp_refs[k][...], w) if wt else _dot_nt(dp_refs[k][...], w)
            dn = t if dn is None else dn + t
        rstd, xhat = _rms_stats(h_ref[...])
        dh, dg = _rms_bwd(dn, g_ref[...], rstd, xhat)
        dh_ref[...] = dres_ref[...] + dh
        dg_ref[...] += dg

    in_specs = [pl.BlockSpec((tm, d), lambda i: (i, 0)), pl.BlockSpec((tm, d), lambda i: (i, 0)),
                pl.BlockSpec((1, d), lambda i: (0, 0))]
    for dp in dp_pieces:
        in_specs.append(pl.BlockSpec((tm, dp.shape[1]), lambda i: (i, 0)))
    for _, bs, idx in w_pieces:
        in_specs.append(pl.BlockSpec(bs, functools.partial(lambda i, idx: idx, idx=idx)))
    return pl.pallas_call(
        body, name="proj_bwd_act",
        grid=(tp // tm,),
        in_specs=in_specs,
        out_specs=[pl.BlockSpec((tm, d), lambda i: (i, 0)), pl.BlockSpec((1, d), lambda i: (0, 0))],
        out_shape=[jax.ShapeDtypeStruct((tp, d), F32), jax.ShapeDtypeStruct((1, d), F32)],
        compiler_params=_cparams(1),
    )(dres, h, norm, *dp_pieces, *[w for w, _, _ in w_pieces])


def _proj_bwd_w(n, dp_pieces, tm, wt=False):
    tp, d = n.shape
    npc = len(dp_pieces)
    widths = [dp.shape[1] for dp in dp_pieces]
    oshape = (lambda w: (w, d)) if wt else (lambda w: (d, w))

    def body(*refs):
        n_ref = refs[0]
        dp_refs = refs[1:1 + npc]
        o_refs = refs[1 + npc:1 + 2 * npc]
        accs = refs[1 + 2 * npc:]
        i = pl.program_id(0)

        @pl.when(i == 0)
        def _():
            for acc in accs:
                acc[...] = jnp.zeros_like(acc)

        nn = n_ref[...]
        for k in range(npc):
            accs[k][...] += _dot_tn(dp_refs[k][...], nn) if wt else _dot_tn(nn, dp_refs[k][...])

        @pl.when(i == pl.num_programs(0) - 1)
        def _():
            for k in range(npc):
                o_refs[k][...] = accs[k][...].astype(BF16)

    return pl.pallas_call(
        body, name="proj_bwd_w",
        grid=(tp // tm,),
        in_specs=[pl.BlockSpec((tm, d), lambda i: (i, 0))]
        + [pl.BlockSpec((tm, w), lambda i: (i, 0)) for w in widths],
        out_specs=[pl.BlockSpec(oshape(w), lambda i: (0, 0)) for w in widths],
        out_shape=[jax.ShapeDtypeStruct(oshape(w), BF16) for w in widths],
        scratch_shapes=[pltpu.VMEM(oshape(w), F32) for w in widths],
        compiler_params=_cparams(1),
    )(n, *dp_pieces)


def _out_fwd(h, ya, yb, w, tm):
    tp, d = h.shape
    na, nb = ya.shape[1], yb.shape[1]

    def body(h_ref, ya_ref, yb_ref, w_ref, o_ref):
        y = _dot(ya_ref[...].astype(BF16), w_ref[0:na, :]) + _dot(yb_ref[...].astype(BF16), w_ref[na:, :])
        o_ref[...] = h_ref[...] + y

    return pl.pallas_call(
        body, name="out_fwd",
        grid=(tp // tm,),
        in_specs=[pl.BlockSpec((tm, d), lambda i: (i, 0)), pl.BlockSpec((tm, na), lambda i: (i, 0)),
                  pl.BlockSpec((tm, nb), lambda i: (i, 0)), pl.BlockSpec((d, d), lambda i: (0, 0))],
        out_specs=pl.BlockSpec((tm, d), lambda i: (i, 0)),
        out_shape=jax.ShapeDtypeStruct((tp, d), F32),
        compiler_params=_cparams(1),
    )(h, ya, yb, w)


def _out_bwd(dy, ya, yb, w, tm, after=()):
    tp, d = dy.shape
    na, nb = ya.shape[1], yb.shape[1]

    def body(dy_ref, ya_ref, yb_ref, w_ref, *rest):
        da_ref, db_ref, dw_ref, acc = rest[len(after):]
        i = pl.program_id(0)

        @pl.when(i == 0)
        def _():
            acc[...] = jnp.zeros_like(acc)

        dyb16 = dy_ref[...].astype(BF16)
        da_ref[...] = _dot_nt(dyb16, w_ref[0:na, :])
        db_ref[...] = _dot_nt(dyb16, w_ref[na:, :])
        acc[0:na, :] += _dot_tn(ya_ref[...].astype(BF16), dyb16)
        acc[na:, :] += _dot_tn(yb_ref[...].astype(BF16), dyb16)

        @pl.when(i == pl.num_programs(0) - 1)
        def _():
            dw_ref[...] = acc[...].astype(BF16)

    return pl.pallas_call(
        body, name="out_bwd",
        grid=(tp // tm,),
        in_specs=[pl.BlockSpec((tm, d), lambda i: (i, 0)), pl.BlockSpec((tm, na), lambda i: (i, 0)),
                  pl.BlockSpec((tm, nb), lambda i: (i, 0)), pl.BlockSpec((d, d), lambda i: (0, 0))]
        + [pl.BlockSpec(memory_space=pl.ANY)] * len(after),
        out_specs=[pl.BlockSpec((tm, na), lambda i: (i, 0)), pl.BlockSpec((tm, nb), lambda i: (i, 0)),
                   pl.BlockSpec((d, d), lambda i: (0, 0))],
        out_shape=[jax.ShapeDtypeStruct((tp, na), F32), jax.ShapeDtypeStruct((tp, nb), F32),
                   jax.ShapeDtypeStruct((d, d), BF16)],
        scratch_shapes=[pltpu.VMEM((d, d), F32)],
        compiler_params=_cparams(1),
    )(dy, ya, yb, w, *after)


def _loss_bwd(h, gfin, tgt, t_real, tm):
    tp, d = h.shape

    def body(h_ref, g_ref, t_ref, loss_ref, dh_ref, dg_ref):
        i = pl.program_id(0)

        @pl.when(i == 0)
        def _():
            loss_ref[...] = jnp.zeros_like(loss_ref)
            dg_ref[...] = jnp.zeros_like(dg_ref)

        rows = i * tm + lax.broadcasted_iota(jnp.int32, (tm, 1), 0)
        valid = (rows >= N_META) & (rows < t_real)
        rstd, xhat = _rms_stats(h_ref[...])
        g = g_ref[...]
        err = jnp.where(valid, xhat * g - t_ref[...], 0.0)
        e2 = jnp.sum(err * err, axis=1, keepdims=True)
        loss_ref[...] += (0.5 / d) * jnp.sum(e2, axis=0, keepdims=True)
        dy = err * (1.0 / d)
        dh, dg = _rms_bwd(dy, g, rstd, xhat)
        dh_ref[...] = dh
        dg_ref[...] += dg

    return pl.pallas_call(
        body, name="loss_bwd",
        grid=(tp // tm,),
        in_specs=[pl.BlockSpec((tm, d), lambda i: (i, 0)), pl.BlockSpec((1, d), lambda i: (0, 0)),
                  pl.BlockSpec((tm, d), lambda i: (i, 0))],
        out_specs=[pl.BlockSpec((1, 1), lambda i: (0, 0)), pl.BlockSpec((tm, d), lambda i: (i, 0)),
                   pl.BlockSpec((1, d), lambda i: (0, 0))],
        out_shape=[jax.ShapeDtypeStruct((1, 1), F32), jax.ShapeDtypeStruct((tp, d), F32),
                   jax.ShapeDtypeStruct((1, d), F32)],
        compiler_params=_cparams(1),
    )(h, gfin, tgt)


POOL_HALO = 16


def _pool_lane_consts(n_rows):
    lane = lax.broadcasted_iota(jnp.int32, (n_rows, D_POOL), 1)
    grp = lane // POOL_GROUP
    win = jnp.where(grp == 0, 2.0, jnp.where(grp == 1, 4.0, jnp.where(grp == 2, 8.0, 16.0)))
    return grp, win


def _pool_select(grp, s2, s4, s8, s16):
    return jnp.where(grp == 0, s2, jnp.where(grp == 1, s4, jnp.where(grp == 2, s8, s16)))


def _pool_mixed(x, row0, tr):
    n = tr + POOL_HALO
    s2 = x + pltpu.roll(x, 1, 0)
    s4 = s2 + pltpu.roll(s2, 2, 0)
    s8 = s4 + pltpu.roll(s4, 4, 0)
    s16 = s8 + pltpu.roll(s8, 8, 0)
    grp, win = _pool_lane_consts(n)
    rows = row0 - POOL_HALO + lax.broadcasted_iota(jnp.int32, (n, D_POOL), 0)
    cnt = jnp.minimum((rows + 1).astype(F32), win)
    pooled = _pool_select(grp, s2, s4, s8, s16) / jnp.maximum(cnt, 1.0)
    return (pooled - x)[POOL_HALO:, :]


def _pool_fwd(p, wbd, scale, tr):
    tp = p.shape[0]
    nt = tp // tr

    def body(p_ref, w_ref, s_ref, y_ref, usc):
        usc[0:POOL_HALO, :] = jnp.zeros((POOL_HALO, D_POOL), F32)
        usc[POOL_HALO:, :] = p_ref[...]

        def tile(r, carry):
            r0 = pl.multiple_of(r * tr, SUBLANE)
            x = usc[pl.ds(r0, tr + POOL_HALO), :]
            mixed = _pool_mixed(x, r0, tr)
            y_ref[pl.ds(r0, tr), :] = _dot(mixed.astype(BF16), w_ref[...]) * s_ref[...]
            return carry

        lax.fori_loop(0, nt, tile, 0)

    return pl.pallas_call(
        body, name="pool_fwd",
        grid=(1,),
        in_specs=[pl.BlockSpec((tp, D_POOL), lambda i: (0, 0)), pl.BlockSpec((D_POOL, D_POOL), lambda i: (0, 0)),
                  pl.BlockSpec((1, D_POOL), lambda i: (0, 0))],
        out_specs=pl.BlockSpec((tp, D_POOL), lambda i: (0, 0)),
        out_shape=jax.ShapeDtypeStruct((tp, D_POOL), F32),
        scratch_shapes=[pltpu.VMEM((tp + POOL_HALO, D_POOL), F32)],
        compiler_params=_cparams(1),
    )(p, wbd, scale)


def _pool_bwd(p, dya, wbd, scale, tr):
    tp = p.shape[0]
    nt = tp // tr

    def body(p_ref, dy_ref, w_ref, s_ref, du_ref, dw_ref, ds_ref, usc, gsc):
        usc[0:POOL_HALO, :] = jnp.zeros((POOL_HALO, D_POOL), F32)
        usc[POOL_HALO:, :] = p_ref[...]
        gsc[tp:, :] = jnp.zeros((POOL_HALO, D_POOL), F32)
        dw_ref[...] = jnp.zeros_like(dw_ref)
        ds_ref[...] = jnp.zeros_like(ds_ref)
        grp, win = _pool_lane_consts(tr)

        def tile1(r, carry):
            r0 = pl.multiple_of(r * tr, SUBLANE)
            x = usc[pl.ds(r0, tr + POOL_HALO), :]
            mixed = _pool_mixed(x, r0, tr).astype(BF16)
            dy = dy_ref[pl.ds(r0, tr), :]
            dys = (dy * s_ref[...]).astype(BF16)
            ypre = _dot(mixed, w_ref[...])
            ds_ref[...] += _colsum(dy * ypre)
            dw_ref[...] += _dot_tn(mixed, dys)
            dmx = _dot_nt(dys, w_ref[...])
            rows = r0 + lax.broadcasted_iota(jnp.int32, (tr, D_POOL), 0)
            cnt = jnp.minimum((rows + 1).astype(F32), win)
            gsc[pl.ds(r0, tr), :] = dmx / cnt
            return carry

        lax.fori_loop(0, nt, tile1, 0)
        n = tr + POOL_HALO
        grp2, win2 = _pool_lane_consts(n)

        def tile2(r, carry):
            r0 = pl.multiple_of(r * tr, SUBLANE)
            g = gsc[pl.ds(r0, n), :]
            s2 = g + pltpu.roll(g, n - 1, 0)
            s4 = s2 + pltpu.roll(s2, n - 2, 0)
            s8 = s4 + pltpu.roll(s4, n - 4, 0)
            s16 = s8 + pltpu.roll(s8, n - 8, 0)
            pooled_t = _pool_select(grp2, s2, s4, s8, s16)
            rows = r0 + lax.broadcasted_iota(jnp.int32, (n, D_POOL), 0)
            cnt = jnp.minimum((rows + 1).astype(F32), win2)
            du = pooled_t - g * cnt
            du_ref[pl.ds(r0, tr), :] = du[0:tr, :].astype(BF16)
            return carry

        lax.fori_loop(0, nt, tile2, 0)

    return pl.pallas_call(
        body, name="pool_bwd",
        grid=(1,),
        in_specs=[pl.BlockSpec((tp, D_POOL), lambda i: (0, 0)), pl.BlockSpec((tp, D_POOL), lambda i: (0, 0)),
                  pl.BlockSpec((D_POOL, D_POOL), lambda i: (0, 0)), pl.BlockSpec((1, D_POOL), lambda i: (0, 0))],
        out_specs=[pl.BlockSpec((tp, D_POOL), lambda i: (0, 0)), pl.BlockSpec((D_POOL, D_POOL), lambda i: (0, 0)),
                   pl.BlockSpec((1, D_POOL), lambda i: (0, 0))],
        out_shape=[jax.ShapeDtypeStruct((tp, D_POOL), BF16), jax.ShapeDtypeStruct((D_POOL, D_POOL), F32),
                   jax.ShapeDtypeStruct((1, D_POOL), F32)],
        scratch_shapes=[pltpu.VMEM((tp + POOL_HALO, D_POOL), F32), pltpu.VMEM((tp + POOL_HALO, D_POOL), F32)],
        compiler_params=_cparams(1),
    )(p, dya, wbd, scale)


def _hgrn_consts(ch):
    t = np.arange(ch)
    levels = []
    w = ch // 2
    while w >= 1:
        levels.append(w)
        w //= 2
    tril = t[None, :] <= t[:, None]
    to_end = t[None, :] > t[:, None]
    mats = [tril, to_end]
    masks = []
    for w in levels:
        pos = t % (2 * w)
        blk = t // (2 * w)
        upper = pos >= w
        mid = blk * 2 * w + w - 1
        mats.append(upper[:, None] & (t[None, :] > mid[:, None]) & (t[None, :] <= t[:, None]))
        mats.append((~upper)[:, None] & (t[None, :] > t[:, None]) & (t[None, :] <= mid[:, None]))
        masks.append(upper[:, None] & (~upper)[None, :] & (blk[:, None] == blk[None, :]))
    masks.append(tril)
    mst = np.concatenate(mats, axis=0).astype(np.float32)
    msk = np.stack(masks).astype(np.float32)
    return jnp.asarray(mst, BF16), jnp.asarray(msk, F32), len(levels)


def _split3(x):
    hi = x.astype(BF16)
    r1 = x - hi.astype(F32)
    mid = r1.astype(BF16)
    lo = (r1 - mid.astype(F32)).astype(BF16)
    return hi, mid, lo


def _hgrn_exponents(mst, logf):
    hi, mid, lo = _split3(logf)
    x = _dot(mst, jnp.concatenate([hi, mid, lo], axis=1))
    return x[:, 0:HEAD] + x[:, HEAD:2 * HEAD] + x[:, 2 * HEAD:3 * HEAD]


def _hgrn_gates(q_raw, z, lb):
    sz = _sigmoid(z)
    f = lb + (1.0 - lb) * sz
    q = q_raw * _sigmoid(q_raw)
    k = (1.0 - lb) * (1.0 - sz)
    return q, k, f, sz


def _hgrn_intra(q, k, x, msk_ref, n_lev, ch):
    a = msk_ref[n_lev] * 0.0
    eye = (lax.broadcasted_iota(jnp.int32, (ch, ch), 0) == lax.broadcasted_iota(jnp.int32, (ch, ch), 1))
    a = jnp.where(eye, jnp.sum(q * k, axis=1, keepdims=True), 0.0)
    ops = []
    for lv in range(n_lev):
        eq = jnp.exp(x[(2 + 2 * lv) * ch:(3 + 2 * lv) * ch, :])
        ek = jnp.exp(x[(3 + 2 * lv) * ch:(4 + 2 * lv) * ch, :])
        qd = q * eq
        kd = k * ek
        a = a + msk_ref[lv] * _dot_nt(qd.astype(BF16), kd.astype(BF16))
        ops.append((eq, ek, qd, kd))
    return a, ops


def _hgrn_fwd(p, lb_logits, gnorm, mst, msk, n_lev, tm):
    tp = p.shape[0]
    ch = HG_CHUNK
    nct = tm // ch
    nt = tp // tm
    nr = mst.shape[0]
    base = D_POOL // HEAD

    def body(q_ref, z_ref, v_ref, g_ref, lg_ref, gn_ref, mst_ref, msk_ref, y_ref, ss_ref, st_sc):
        @pl.when(pl.program_id(1) == 0)
        def _():
            st_sc[...] = jnp.zeros_like(st_sc)

        lb_all = _sigmoid(lg_ref[0:1, :] - lg_ref[1:2, :])

        def one_head(hh, c, r0):
            ls = slice(hh * HEAD, (hh + 1) * HEAD)
            lb = lb_all[:, ls]
            q, k, f, _ = _hgrn_gates(q_ref[pl.ds(r0, ch), ls], z_ref[pl.ds(r0, ch), ls], lb)
            v = v_ref[pl.ds(r0, ch), ls]
            x = _hgrn_exponents(mst_ref[...], jnp.log(f))
            st = st_sc[hh]
            ss_ref[hh, c] = st
            qe = q * jnp.exp(x[0:ch, :])
            a, _ = _hgrn_intra(q, k, x, msk_ref, n_lev, ch)
            v16 = v.astype(BF16)
            o = _dot_nt(qe.astype(BF16), st.astype(BF16)) + _dot(a.astype(BF16), v16)
            kl = k * jnp.exp(x[ch:2 * ch, :])
            st_sc[hh] = st * jnp.exp(x[ch - 1:ch, :]) + _dot_tn(v16, kl.astype(BF16))
            rstd = lax.rsqrt(jnp.mean(o * o, axis=-1, keepdims=True) + EPS)
            g_raw = g_ref[pl.ds(r0, ch), ls]
            y_ref[pl.ds(r0, ch), ls] = o * rstd * gn_ref[...] * (g_raw * _sigmoid(g_raw))

        def chunk(c, carry):
            r0 = pl.multiple_of(c * ch, ch)
            for hh in range(HG_HEADS_PER_STEP):
                one_head(hh, c, r0)
            return carry

        lax.fori_loop(0, nct, chunk, 0)

    hp = HG_HEADS_PER_STEP
    wide = hp * HEAD

    def pspec(seg):
        return pl.BlockSpec((tm, wide), lambda h, i: (i, (base + seg * HG_HEADS) // hp + h))

    return pl.pallas_call(
        body, name="hgrn_fwd",
        grid=(HG_HEADS // hp, nt),
        in_specs=[pspec(0), pspec(1), pspec(2), pspec(3),
                  pl.BlockSpec((2, wide), lambda h, i: (0, h)),
                  pl.BlockSpec((1, HEAD), lambda h, i: (0, 0)),
                  pl.BlockSpec((nr, ch), lambda h, i: (0, 0)),
                  pl.BlockSpec((n_lev + 1, ch, ch), lambda h, i: (0, 0, 0))],
        out_specs=[pl.BlockSpec((tm, wide), lambda h, i: (i, h)),
                   pl.BlockSpec((hp, nct, HEAD, HEAD), lambda h, i: (h, i, 0, 0))],
        out_shape=[jax.ShapeDtypeStruct((tp, D_HGRN), F32),
                   jax.ShapeDtypeStruct((HG_HEADS, tp // ch, HEAD, HEAD), F32)],
        scratch_shapes=[pltpu.VMEM((hp, HEAD, HEAD), F32)],
        compiler_params=_cparams(2),
    )(p, p, p, p, lb_logits, gnorm, mst, msk)


def _hgrn_bwd(p, dyb, states, lb_logits, gnorm, mst, msk, n_lev, tm):
    tp = p.shape[0]
    ch = HG_CHUNK
    nct = tm // ch
    nt = tp // tm
    nr = mst.shape[0]
    base = D_POOL // HEAD

    def body(q_ref, z_ref, v_ref, g_ref, dy_ref, ss_ref, lg_ref, gn_ref, mst_ref, msk_ref,
             dq_ref, dz_ref, dv_ref, dg_ref, dlg_ref, dgn_ref, dst_sc, dlb_sc, stk):
        ti = pl.program_id(1)

        @pl.when(ti == 0)
        def _():
            dst_sc[...] = jnp.zeros_like(dst_sc)
            dlb_sc[...] = jnp.zeros_like(dlb_sc)
            dgn_ref[...] = jnp.zeros_like(dgn_ref)

        lb_all = _sigmoid(lg_ref[0:1, :] - lg_ref[1:2, :])
        gn = gn_ref[...]

        def one_head(hh, c, r0):
            ls = slice(hh * HEAD, (hh + 1) * HEAD)
            lb = lb_all[:, ls]
            q_raw = q_ref[pl.ds(r0, ch), ls]
            z = z_ref[pl.ds(r0, ch), ls]
            q, k, f, sz = _hgrn_gates(q_raw, z, lb)
            v = v_ref[pl.ds(r0, ch), ls]
            x = _hgrn_exponents(mst_ref[...], jnp.log(f))
            st = ss_ref[hh, c]
            eb = jnp.exp(x[0:ch, :])
            ef = jnp.exp(x[ch:2 * ch, :])
            elast = jnp.exp(x[ch - 1:ch, :])
            qe = q * eb
            kl = k * ef
            a, ops = _hgrn_intra(q, k, x, msk_ref, n_lev, ch)
            v16 = v.astype(BF16)
            st16 = st.astype(BF16)
            qe16 = qe.astype(BF16)
            kl16 = kl.astype(BF16)
            a16 = a.astype(BF16)
            o = _dot_nt(qe16, st16) + _dot(a16, v16)
            g_raw = g_ref[pl.ds(r0, ch), ls]
            sg = _sigmoid(g_raw)
            rstd = lax.rsqrt(jnp.mean(o * o, axis=-1, keepdims=True) + EPS)
            oh = o * rstd
            dy = dy_ref[pl.ds(r0, ch), ls]
            dg_ref[pl.ds(r0, ch), ls] = (dy * oh * gn * (sg * (1.0 + g_raw * (1.0 - sg)))).astype(BF16)
            don = dy * (g_raw * sg)
            dgn_ref[hh] += _colsum(don * oh)
            doh = don * gn
            do = rstd * (doh - oh * jnp.mean(doh * oh, axis=-1, keepdims=True))
            do16 = do.astype(BF16)
            dst = dst_sc[hh]
            dst16 = dst.astype(BF16)
            dv = _dot_tn(a16, do16) + _dot_nt(kl16, dst16)
            da = msk_ref[n_lev] * _dot_nt(do16, v16)
            dqe = _dot(do16, st16)
            dkl = _dot(v16, dst16)
            dst_sc[hh] = dst * elast + _dot_tn(do16, qe16)
            db_last = _colsum(dst * st) * elast
            dad = jnp.sum(do * v, axis=1, keepdims=True)
            dq = dad * k + dqe * eb
            dk = dad * q + dkl * ef
            stk[hh, 0:ch, :] = dqe * qe
            stk[hh, ch:2 * ch, :] = dkl * kl
            for lv in range(n_lev):
                eq, ek, qd, kd = ops[lv]
                gl = (msk_ref[lv] * da).astype(BF16)
                dqd = _dot(gl, kd.astype(BF16))
                dkd = _dot_tn(gl, qd.astype(BF16))
                dq = dq + dqd * eq
                dk = dk + dkd * ek
                stk[hh, (2 + 2 * lv) * ch:(3 + 2 * lv) * ch, :] = dqd * qd
                stk[hh, (3 + 2 * lv) * ch:(4 + 2 * lv) * ch, :] = dkd * kd
            sk = stk[hh]
            hi = sk.astype(BF16)
            lo = (sk - hi.astype(F32)).astype(BF16)
            dl2 = _dot_tn(mst_ref[...], jnp.concatenate([hi, lo], axis=1))
            dlogf = dl2[:, 0:HEAD] + dl2[:, HEAD:2 * HEAD] + db_last
            sq = _sigmoid(q_raw)
            dq_ref[pl.ds(r0, ch), ls] = (dq * (sq * (1.0 + q_raw * (1.0 - sq)))).astype(BF16)
            dfk = dlogf / f - dk
            dz_ref[pl.ds(r0, ch), ls] = (dfk * (1.0 - lb) * sz * (1.0 - sz)).astype(BF16)
            dlb_sc[:, ls] += _colsum(dfk * (1.0 - sz))
            dv_ref[pl.ds(r0, ch), ls] = dv.astype(BF16)

        def chunk(cc, carry):
            c = nct - 1 - cc
            r0 = pl.multiple_of(c * ch, ch)
            for hh in range(HG_HEADS_PER_STEP):
                one_head(hh, c, r0)
            return carry

        lax.fori_loop(0, nct, chunk, 0)

        @pl.when(ti == nt - 1)
        def _():
            dl0 = dlb_sc[...] * lb_all * (1.0 - lb_all)
            dlg_ref[0:1, :] = dl0
            dlg_ref[1:2, :] = -dl0

    hp = HG_HEADS_PER_STEP
    wide = hp * HEAD

    def pspec(seg):
        return pl.BlockSpec((tm, wide), lambda h, i: (nt - 1 - i, (base + seg * HG_HEADS) // hp + h))

    ospec = pl.BlockSpec((tm, wide), lambda h, i: (nt - 1 - i, h))
    return pl.pallas_call(
        body, name="hgrn_bwd",
        grid=(HG_HEADS // hp, nt),
        in_specs=[pspec(0), pspec(1), pspec(2), pspec(3), ospec,
                  pl.BlockSpec((hp, nct, HEAD, HEAD), lambda h, i: (h, nt - 1 - i, 0, 0)),
                  pl.BlockSpec((2, wide), lambda h, i: (0, h)),
                  pl.BlockSpec((1, HEAD), lambda h, i: (0, 0)),
                  pl.BlockSpec((nr, ch), lambda h, i: (0, 0)),
                  pl.BlockSpec((n_lev + 1, ch, ch), lambda h, i: (0, 0, 0))],
        out_specs=[ospec, ospec, ospec, ospec,
                   pl.BlockSpec((2, wide), lambda h, i: (0, h)),
                   pl.BlockSpec((hp, 1, HEAD), lambda h, i: (h, 0, 0))],
        out_shape=[jax.ShapeDtypeStruct((tp, D_HGRN), BF16)] * 4
        + [jax.ShapeDtypeStruct((2, D_HGRN), F32), jax.ShapeDtypeStruct((HG_HEADS, 1, HEAD), F32)],
        scratch_shapes=[pltpu.VMEM((hp, HEAD, HEAD), F32), pltpu.VMEM((1, wide), F32),
                        pltpu.VMEM((hp, nr, HEAD), F32)],
        compiler_params=_cparams(2),
    )(p, p, p, p, dyb, states, lb_logits, gnorm, mst, msk)


def _conv_taps(x, w_ref, tr, halo, width):
    acc = None
    for j in range(width):
        sh = width - 1 - j
        xs = x if sh == 0 else pltpu.roll(x, sh, 0)
        term = xs[halo:, :] * w_ref[j:j + 1, :]
        acc = term if acc is None else acc + term
    return acc


def _conv_taps_t(y, w_ref, tr, halo, width):
    n = tr + halo
    acc = None
    for j in range(width):
        sh = width - 1 - j
        ys = y if sh == 0 else pltpu.roll(y, n - sh, 0)
        term = ys[0:tr, :] * w_ref[j:j + 1, :]
        acc = term if acc is None else acc + term
    return acc


def _ln_stats(cv):
    mu = jnp.mean(cv, axis=-1, keepdims=True)
    xc = cv - mu
    rstd = lax.rsqrt(jnp.mean(xc * xc, axis=-1, keepdims=True) + EPS)
    return rstd, xc * rstd


def _convmod_fwd(p, w, bias, ln_g, ln_b, tr):
    tp = p.shape[0]
    nt = tp // tr
    nb = D_CONV // HEAD

    def body(a_ref, b_ref, w_ref, bi_ref, g_ref, be_ref, y_ref, usc):
        usc[0:CONV_HALO, :] = jnp.zeros((CONV_HALO, HEAD), F32)
        usc[CONV_HALO:, :] = a_ref[...] * _sigmoid(b_ref[...])

        def tile(r, carry):
            r0 = pl.multiple_of(r * tr, SUBLANE)
            x = usc[pl.ds(r0, tr + CONV_HALO), :]
            cv = _conv_taps(x, w_ref, tr, CONV_HALO, CONV_WIDTH) + bi_ref[...]
            _, xh = _ln_stats(cv)
            un = xh * g_ref[...] + be_ref[...]
            y_ref[pl.ds(r0, tr), :] = un * _sigmoid(un)
            return carry

        lax.fori_loop(0, nt, tile, 0)

    vec = lambda: pl.BlockSpec((1, HEAD), lambda j: (0, j))
    return pl.pallas_call(
        body, name="convmod_fwd",
        grid=(nb,),
        in_specs=[pl.BlockSpec((tp, HEAD), lambda j: (0, j)), pl.BlockSpec((tp, HEAD), lambda j: (0, nb + j)),
                  pl.BlockSpec((CONV_HALO, HEAD), lambda j: (0, j)), vec(), vec(), vec()],
        out_specs=pl.BlockSpec((tp, HEAD), lambda j: (0, j)),
        out_shape=jax.ShapeDtypeStruct((tp, D_CONV), F32),
        scratch_shapes=[pltpu.VMEM((tp + CONV_HALO, HEAD), F32)],
        compiler_params=_cparams(1),
    )(p, p, w, bias, ln_g, ln_b)


def _convmod_bwd(p, dyc, w, bias, ln_g, ln_b, tr):
    tp = p.shape[0]
    nt = tp // tr
    nb = D_CONV // HEAD

    def body(a_ref, b_ref, dy_ref, w_ref, bi_ref, g_ref, be_ref, da_ref, db_ref, dw_ref, dv_ref, usc, dsc):
        usc[0:CONV_HALO, :] = jnp.zeros((CONV_HALO, HEAD), F32)
        usc[CONV_HALO:, :] = a_ref[...] * _sigmoid(b_ref[...])
        dsc[tp:, :] = jnp.zeros((CONV_HALO, HEAD), F32)
        dw_ref[...] = jnp.zeros_like(dw_ref)
        dv_ref[...] = jnp.zeros_like(dv_ref)

        def tile1(r, carry):
            r0 = pl.multiple_of(r * tr, SUBLANE)
            x = usc[pl.ds(r0, tr + CONV_HALO), :]
            cv = _conv_taps(x, w_ref, tr, CONV_HALO, CONV_WIDTH) + bi_ref[...]
            rstd, xh = _ln_stats(cv)
            un = xh * g_ref[...] + be_ref[...]
            sg = _sigmoid(un)
            dun = dy_ref[pl.ds(r0, tr), :] * (sg * (1.0 + un * (1.0 - sg)))
            dv_ref[0, 1:2, :] += _colsum(dun * xh)
            dv_ref[0, 2:3, :] += _colsum(dun)
            dxh = dun * g_ref[...]
            dcv = rstd * (dxh - jnp.mean(dxh, axis=-1, keepdims=True)
                          - xh * jnp.mean(dxh * xh, axis=-1, keepdims=True))
            dv_ref[0, 0:1, :] += _colsum(dcv)
            for j in range(CONV_WIDTH):
                sh = CONV_WIDTH - 1 - j
                xs = x if sh == 0 else pltpu.roll(x, sh, 0)
                dw_ref[0, j:j + 1, :] += _colsum(dcv * xs[CONV_HALO:, :])
            dsc[pl.ds(r0, tr), :] = dcv
            return carry

        lax.fori_loop(0, nt, tile1, 0)

        def tile2(r, carry):
            r0 = pl.multiple_of(r * tr, SUBLANE)
            y = dsc[pl.ds(r0, tr + CONV_HALO), :]
            du = _conv_taps_t(y, w_ref, tr, CONV_HALO, CONV_WIDTH)
            a = a_ref[pl.ds(r0, tr), :]
            sb = _sigmoid(b_ref[pl.ds(r0, tr), :])
            da_ref[pl.ds(r0, tr), :] = (du * sb).astype(BF16)
            db_ref[pl.ds(r0, tr), :] = (du * a * sb * (1.0 - sb)).astype(BF16)
            return carry

        lax.fori_loop(0, nt, tile2, 0)

    vec = lambda: pl.BlockSpec((1, HEAD), lambda j: (0, j))
    col = lambda: pl.BlockSpec((tp, HEAD), lambda j: (0, j))
    return pl.pallas_call(
        body, name="convmod_bwd",
        grid=(nb,),
        in_specs=[col(), pl.BlockSpec((tp, HEAD), lambda j: (0, nb + j)), col(),
                  pl.BlockSpec((CONV_HALO, HEAD), lambda j: (0, j)), vec(), vec(), vec()],
        out_specs=[col(), col(), pl.BlockSpec((1, CONV_HALO, HEAD), lambda j: (j, 0, 0)),
                   pl.BlockSpec((1, SUBLANE, HEAD), lambda j: (j, 0, 0))],
        out_shape=[jax.ShapeDtypeStruct((tp, D_CONV), BF16), jax.ShapeDtypeStruct((tp, D_CONV), BF16),
                   jax.ShapeDtypeStruct((nb, CONV_HALO, HEAD), F32), jax.ShapeDtypeStruct((nb, SUBLANE, HEAD), F32)],
        scratch_shapes=[pltpu.VMEM((tp + CONV_HALO, HEAD), F32), pltpu.VMEM((tp + CONV_HALO, HEAD), F32)],
        compiler_params=_cparams(1),
    )(p, p, dyc, w, bias, ln_g, ln_b)


def _log1p_small(y):
    return jnp.where(y < 1e-4, y * (1.0 - 0.5 * y), jnp.log(1.0 + y))


def _softplus(x):
    return jnp.maximum(x, 0.0) + _log1p_small(jnp.exp(-jnp.abs(x)))


def _expm1(x):
    return jnp.where(jnp.abs(x) < 1e-2, x * (1.0 + 0.5 * x * (1.0 + x * (1.0 / 3.0))), jnp.exp(x) - 1.0)


def _gelu_parts(x):
    c = 0.7978845608028654
    inner = c * (x + 0.044715 * x * x * x)
    th = jnp.tanh(inner)
    gelu = 0.5 * x * (1.0 + th)
    dgelu = 0.5 * (1.0 + th) + 0.5 * x * (1.0 - th * th) * c * (1.0 + 3.0 * 0.044715 * x * x)
    return gelu, dgelu


def _lru_gates(x_all, tp, cw_ref, cb_ref, wa_ref, ba_ref, wx_ref, bx_ref, lam_ref):
    u = _conv_taps(x_all, cw_ref, tp, LRU_HALO, LRU_CONV) + cb_ref[...]
    u16 = u.astype(BF16)
    r = _sigmoid(_dot(u16, wa_ref[0]) + ba_ref[...])
    i = _sigmoid(_dot(u16, wx_ref[0]) + bx_ref[...])
    sp = _softplus(-lam_ref[...])
    la = -LRU_C * r * sp
    a = jnp.exp(la)
    mult = jnp.sqrt(-_expm1(2.0 * la))
    return u, r, i, a, mult, sp


def _lru_specs(tp, nb):
    col = lambda k: pl.BlockSpec((tp, HEAD), functools.partial(lambda j, k: (0, k * nb + j), k=k))
    vec = lambda: pl.BlockSpec((1, HEAD), lambda j: (0, j))
    mat = lambda: pl.BlockSpec((1, HEAD, HEAD), lambda j: (j, 0, 0))
    return col, vec, mat


def _lru_fwd(p, cw, cb, wa, ba, wx, bx, lam):
    tp = p.shape[0]
    nb = D_LRU // HEAD
    ng = tp // SUBLANE

    def body(x_ref, gt_ref, cw_ref, cb_ref, wa_ref, ba_ref, wx_ref, bx_ref, lam_ref, y_ref, hs_ref,
             xsc, asc, bsc):
        xsc[0:LRU_HALO, :] = jnp.zeros((LRU_HALO, HEAD), F32)
        xsc[LRU_HALO:, :] = x_ref[...]
        u, r, i, a, mult, _ = _lru_gates(xsc[...], tp, cw_ref, cb_ref, wa_ref, ba_ref, wx_ref, bx_ref, lam_ref)
        rows = lax.broadcasted_iota(jnp.int32, (tp, HEAD), 0)
        b = jnp.where(rows == 0, 1.0, mult) * (i * u)
        sub = rows % SUBLANE
        for k in (1, 2, 4):
            m = sub >= k
            b = jnp.where(m, a * pltpu.roll(b, k, 0) + b, b)
            a = jnp.where(m, a * pltpu.roll(a, k, 0), a)
        asc[...] = a
        bsc[...] = b

        def grp(g, carry):
            r0 = pl.multiple_of(g * SUBLANE, SUBLANE)
            h = bsc[pl.ds(r0, SUBLANE), :] + asc[pl.ds(r0, SUBLANE), :] * carry
            hs_ref[pl.ds(r0, SUBLANE), :] = h
            return jnp.broadcast_to(h[SUBLANE - 1:SUBLANE, :], (SUBLANE, HEAD))

        lax.fori_loop(0, ng, grp, jnp.zeros((SUBLANE, HEAD), F32))
        gelu, _ = _gelu_parts(gt_ref[...])
        y_ref[...] = gelu * hs_ref[...]

    col, vec, mat = _lru_specs(tp, nb)
    return pl.pallas_call(
        body, name="lru_fwd",
        grid=(nb,),
        in_specs=[col(2), col(3), pl.BlockSpec((LRU_CONV, HEAD), lambda j: (0, j)), vec(), mat(), vec(), mat(),
                  vec(), vec()],
        out_specs=[pl.BlockSpec((tp, HEAD), lambda j: (0, j)), pl.BlockSpec((tp, HEAD), lambda j: (0, j))],
        out_shape=[jax.ShapeDtypeStruct((tp, D_LRU), F32), jax.ShapeDtypeStruct((tp, D_LRU), F32)],
        scratch_shapes=[pltpu.VMEM((tp + LRU_HALO, HEAD), F32), pltpu.VMEM((tp, HEAD), F32),
                        pltpu.VMEM((tp, HEAD), F32)],
        compiler_params=_cparams(1),
    )(p, p, cw, cb, wa, ba, wx, bx, lam)


def _lru_bwd(p, hs, dyd, cw, cb, wa, ba, wx, bx, lam):
    tp = p.shape[0]
    nb = D_LRU // HEAD
    ng = tp // SUBLANE

    def body(x_ref, gt_ref, hs_ref, dy_ref, cw_ref, cb_ref, wa_ref, ba_ref, wx_ref, bx_ref, lam_ref,
             dx_ref, dgt_ref, dwa_ref, dwx_ref, dv_ref, xsc, asc, bsc, gsc, dusc):
        xsc[0:LRU_HALO, :] = jnp.zeros((LRU_HALO, HEAD), F32)
        xsc[LRU_HALO:, :] = x_ref[...]
        x_all = xsc[...]
        u, r, i, a, mult, sp = _lru_gates(x_all, tp, cw_ref, cb_ref, wa_ref, ba_ref, wx_ref, bx_ref, lam_ref)
        rows = lax.broadcasted_iota(jnp.int32, (tp, HEAD), 0)
        hs = hs_ref[...]
        dy = dy_ref[...]
        gelu, dgelu = _gelu_parts(gt_ref[...])
        dgt_ref[...] = (dy * hs * dgelu).astype(BF16)
        bb = dy * gelu
        aa = jnp.where(rows == tp - 1, 0.0, pltpu.roll(a, tp - 1, 0))
        sub = rows % SUBLANE
        for k in (1, 2, 4):
            m = sub < SUBLANE - k
            bb = jnp.where(m, aa * pltpu.roll(bb, tp - k, 0) + bb, bb)
            aa = jnp.where(m, aa * pltpu.roll(aa, tp - k, 0), aa)
        asc[...] = aa
        bsc[...] = bb

        def grp(gi, carry):
            g = ng - 1 - gi
            r0 = pl.multiple_of(g * SUBLANE, SUBLANE)
            gg = bsc[pl.ds(r0, SUBLANE), :] + asc[pl.ds(r0, SUBLANE), :] * carry
            gsc[pl.ds(r0, SUBLANE), :] = gg
            return jnp.broadcast_to(gg[0:1, :], (SUBLANE, HEAD))

        lax.fori_loop(0, ng, grp, jnp.zeros((SUBLANE, HEAD), F32))
        g = gsc[...]
        first = rows == 0
        hprev = jnp.where(first, 0.0, pltpu.roll(hs, 1, 0))
        iu = i * u
        d_iu = g * jnp.where(first, 1.0, mult)
        dmult_term = jnp.where(first, 0.0, g * iu * (-(a * a) / mult))
        dla = g * hprev * a + dmult_term
        dr = dla * (-LRU_C) * sp
        dv_ref[0, 7:8, :] = _colsum(dla * (LRU_C * r) * _sigmoid(-lam_ref[...]))
        dpr = dr * r * (1.0 - r)
        dpi = d_iu * u * i * (1.0 - i)
        dv_ref[0, 5:6, :] = _colsum(dpr)
        dv_ref[0, 6:7, :] = _colsum(dpi)
        u16 = u.astype(BF16)
        dpr16 = dpr.astype(BF16)
        dpi16 = dpi.astype(BF16)
        dwa_ref[0] = _dot_tn(u16, dpr16)
        dwx_ref[0] = _dot_tn(u16, dpi16)
        du = d_iu * i + _dot_nt(dpr16, wa_ref[0]) + _dot_nt(dpi16, wx_ref[0])
        dv_ref[0, 4:5, :] = _colsum(du)
        for j in range(LRU_CONV):
            sh = LRU_CONV - 1 - j
            xs = x_all if sh == 0 else pltpu.roll(x_all, sh, 0)
            dv_ref[0, j:j + 1, :] = _colsum(du * xs[LRU_HALO:, :])
        dusc[0:tp, :] = du
        dusc[tp:, :] = jnp.zeros((LRU_HALO, HEAD), F32)
        dx_ref[...] = _conv_taps_t(dusc[...], cw_ref, tp, LRU_HALO, LRU_CONV).astype(BF16)

    col, vec, mat = _lru_specs(tp, nb)
    ocol = lambda: pl.BlockSpec((tp, HEAD), lambda j: (0, j))
    return pl.pallas_call(
        body, name="lru_bwd",
        grid=(nb,),
        in_specs=[col(2), col(3), ocol(), ocol(), pl.BlockSpec((LRU_CONV, HEAD), lambda j: (0, j)), vec(), mat(),
                  vec(), mat(), vec(), vec()],
        out_specs=[ocol(), ocol(), mat(), mat(), pl.BlockSpec((1, SUBLANE, HEAD), lambda j: (j, 0, 0))],
        out_shape=[jax.ShapeDtypeStruct((tp, D_LRU), BF16), jax.ShapeDtypeStruct((tp, D_LRU), BF16),
                   jax.ShapeDtypeStruct((nb, HEAD, HEAD), F32), jax.ShapeDtypeStruct((nb, HEAD, HEAD), F32),
                   jax.ShapeDtypeStruct((nb, SUBLANE, HEAD), F32)],
        scratch_shapes=[pltpu.VMEM((tp + LRU_HALO, HEAD), F32), pltpu.VMEM((tp, HEAD), F32),
                        pltpu.VMEM((tp, HEAD), F32), pltpu.VMEM((tp, HEAD), F32),
                        pltpu.VMEM((tp + LRU_HALO, HEAD), F32)],
        compiler_params=_cparams(1),
    )(p, p, hs, dyd, cw, cb, wa, ba, wx, bx, lam)


def _mesh_pos():
    return lax.axis_index("x"), lax.axis_index("y"), lax.axis_index("c")


def _other_chips(x, y):
    return [(1 - x, y), (x, 1 - y), (1 - x, 1 - y)]


ANY = pl.BlockSpec(memory_space=pl.ANY)


def _allgather_shards(arrs, split):
    n = len(arrs)

    def body(*refs):
        ins, outs = refs[:n], refs[n:2 * n]
        send1, recv1, send2, recv2, send3, recv3 = refs[2 * n:]
        x, y, c = _mesh_pos()
        chip = 2 * x + y
        sibling = (x, y, 1 - c)
        others = _other_chips(x, y)

        def rows(k, cc):
            half = arrs[k].shape[0] // 2
            return pl.ds(cc * half, half)

        def remote(src, dst, ssem, rsem, dev):
            return pltpu.make_async_remote_copy(src_ref=src, dst_ref=dst, send_sem=ssem, recv_sem=rsem,
                                                device_id=dev, device_id_type=MESH)

        started = [remote(ins[k], outs[k].at[chip], send3.at[k], recv3.at[k], sibling) for k in range(n)]
        for cp in started:
            cp.start()
        for k in range(n):
            for j, (ox, oy) in enumerate(others):
                if split[k]:
                    src, dst = ins[k].at[rows(k, c)], outs[k].at[chip, rows(k, c)]
                else:
                    src, dst = ins[k], outs[k].at[chip]
                cp = remote(src, dst, send1.at[3 * k + j], recv1.at[3 * k + j], (ox, oy, c))
                cp.start()
                started.append(cp)
        for j, (ox, oy) in enumerate(others):
            ochip = 2 * ox + oy
            for k in range(n):
                if split[k]:
                    blk = outs[k].at[ochip, rows(k, c)]
                    remote(blk, blk, send1.at[3 * k + j], recv1.at[3 * k + j], sibling).wait_recv()
                    cp = remote(blk, blk, send2.at[3 * k + j], recv2.at[3 * k + j], sibling)
                    cp.start()
                    started.append(cp)
                else:
                    blk = outs[k].at[ochip]
                    remote(blk, blk, send1.at[3 * k + j], recv1.at[3 * k + j], sibling).wait_recv()
        for j, (ox, oy) in enumerate(others):
            ochip = 2 * ox + oy
            for k in range(n):
                if split[k]:
                    blk = outs[k].at[ochip, rows(k, 1 - c)]
                    remote(blk, blk, send2.at[3 * k + j], recv2.at[3 * k + j], sibling).wait_recv()
        for k in range(n):
            blk = outs[k].at[chip]
            remote(blk, blk, send3.at[k], recv3.at[k], sibling).wait_recv()
        for cp in started:
            cp.wait_send()

    return pl.pallas_call(
        body, name="allgather_shards",
        in_specs=[ANY] * n, out_specs=[ANY] * n,
        out_shape=[jax.ShapeDtypeStruct((N_SHARD,) + a.shape, a.dtype) for a in arrs],
        scratch_shapes=[pltpu.SemaphoreType.DMA((3 * n,)), pltpu.SemaphoreType.DMA((3 * n,)),
                        pltpu.SemaphoreType.DMA((3 * n,)), pltpu.SemaphoreType.DMA((3 * n,)),
                        pltpu.SemaphoreType.DMA((n,)), pltpu.SemaphoreType.DMA((n,))],
    )(*arrs)


HBM = pl.BlockSpec(memory_space=pltpu.HBM)
SEM = pl.BlockSpec(memory_space=pltpu.SEMAPHORE)
DATAFLOW = pltpu.SideEffectType.DATAFLOW_SIDE_EFFECTING
N_PEERS = 4


def _in_hbm(a):
    return pltpu.with_memory_space_constraint(a, pltpu.HBM)


def _gather_peers(x, y, c):
    return [((ox, oy, c), 2 * ox + oy) for ox, oy in _other_chips(x, y)] + [((x, y, 1 - c), 2 * x + y)]


def _gather_start(arrs):
    n = len(arrs)

    def body(*refs):
        ins, lands = refs[:n], refs[n:2 * n]
        ssem, rsem = refs[2 * n:2 * n + 2]
        token = refs[-1]
        x, y, c = _mesh_pos()
        chip = 2 * x + y
        for k in range(n):
            for j, (dev, _) in enumerate(_gather_peers(x, y, c)):
                pltpu.make_async_remote_copy(
                    src_ref=ins[k], dst_ref=lands[k].at[chip], send_sem=ssem.at[N_PEERS * k + j],
                    recv_sem=rsem.at[N_PEERS * k + j], device_id=dev, device_id_type=MESH).start()
        token[...] = jnp.zeros_like(token)

    lands = [_in_hbm(lax.empty((N_SHARD,) + a.shape, a.dtype)) for a in arrs]
    out = pl.pallas_call(
        body, name="gather_start",
        in_specs=[HBM] * (2 * n),
        out_specs=[SEM, SEM] + [HBM] * (2 * n) + [pl.BlockSpec(memory_space=pltpu.VMEM)],
        out_shape=[pltpu.SemaphoreType.DMA((N_PEERS * n,)), pltpu.SemaphoreType.DMA((N_PEERS * n,))]
        + [pltpu.HBM(a.shape, a.dtype) for a in arrs]
        + [pltpu.HBM((N_SHARD,) + a.shape, a.dtype) for a in arrs]
        + [jax.ShapeDtypeStruct((SUBLANE, LANE), F32)],
        input_output_aliases={k: 2 + k for k in range(2 * n)},
        compiler_params=pltpu.CompilerParams(has_side_effects=DATAFLOW),
    )(*[_in_hbm(a) for a in arrs], *lands)
    return out[0], out[1], list(out[2:2 + n]), list(out[2 + n:2 + 2 * n]), out[-1]


def _gather_wait(ssem, rsem, srcs, lands, ks, after):
    n = len(ks)

    def body(*refs):
        ins, lnd = refs[:n], refs[n:2 * n]
        ssem_ref, rsem_ref = refs[2 * n:2 * n + 2]
        x, y, c = _mesh_pos()
        for i, k in enumerate(ks):
            for j, (dev, pchip) in enumerate(_gather_peers(x, y, c)):
                cp = pltpu.make_async_remote_copy(
                    src_ref=ins[i], dst_ref=lnd[i].at[pchip], send_sem=ssem_ref.at[N_PEERS * k + j],
                    recv_sem=rsem_ref.at[N_PEERS * k + j], device_id=dev, device_id_type=MESH)
                cp.wait_send()
                cp.wait_recv()

    out = pl.pallas_call(
        body, name="gather_wait",
        in_specs=[HBM] * (2 * n) + [SEM, SEM] + [ANY] * len(after),
        out_specs=[HBM] * (2 * n),
        out_shape=[pltpu.HBM(a.shape, a.dtype) for a in srcs] + [pltpu.HBM(a.shape, a.dtype) for a in lands],
        input_output_aliases={k: k for k in range(2 * n)},
        compiler_params=pltpu.CompilerParams(has_side_effects=DATAFLOW),
    )(*srcs, *lands, ssem, rsem, *after)
    return list(out[n:])


def _scatter_start(arrs, slots):
    n = len(arrs)

    def body(*refs):
        ins, lands = refs[:n], refs[n:2 * n]
        ssem, rsem = refs[2 * n:2 * n + 2]
        token = refs[-1]
        x, y, c = _mesh_pos()
        chip = 2 * x + y
        for k in range(n):
            for j, (ox, oy) in enumerate(_other_chips(x, y)):
                pltpu.make_async_remote_copy(
                    src_ref=ins[k].at[2 * ox + oy], dst_ref=lands[k].at[chip], send_sem=ssem.at[3 * k + j],
                    recv_sem=rsem.at[3 * k + j], device_id=(ox, oy, c), device_id_type=MESH).start()
        token[...] = jnp.zeros_like(token)

    out = pl.pallas_call(
        body, name="scatter_start",
        in_specs=[HBM] * (2 * n),
        out_specs=[SEM, SEM] + [HBM] * (2 * n) + [pl.BlockSpec(memory_space=pltpu.VMEM)],
        out_shape=[pltpu.SemaphoreType.DMA((3 * n,)), pltpu.SemaphoreType.DMA((3 * n,))]
        + [pltpu.HBM(a.shape, a.dtype) for a in arrs] + [pltpu.HBM(a.shape, a.dtype) for a in slots]
        + [jax.ShapeDtypeStruct((SUBLANE, LANE), F32)],
        input_output_aliases={k: 2 + k for k in range(2 * n)},
        compiler_params=pltpu.CompilerParams(has_side_effects=DATAFLOW),
    )(*[_in_hbm(a) for a in arrs], *[_in_hbm(a) for a in slots])
    return out[0], out[1], list(out[2:2 + n]), list(out[2 + n:2 + 2 * n]), out[-1]


def _scatter_wait(ssem, rsem, arrs, slots, after):
    n = len(arrs)

    def body(*refs):
        ins, lnd = refs[:n], refs[n:2 * n]
        ssem_ref, rsem_ref = refs[2 * n:2 * n + 2]
        x, y, c = _mesh_pos()
        for k in range(n):
            for j, (ox, oy) in enumerate(_other_chips(x, y)):
                ochip = 2 * ox + oy
                cp = pltpu.make_async_remote_copy(
                    src_ref=ins[k].at[ochip], dst_ref=lnd[k].at[ochip], send_sem=ssem_ref.at[3 * k + j],
                    recv_sem=rsem_ref.at[3 * k + j], device_id=(ox, oy, c), device_id_type=MESH)
                cp.wait_send()
                cp.wait_recv()

    out = pl.pallas_call(
        body, name="scatter_wait",
        in_specs=[HBM] * (2 * n) + [SEM, SEM] + [ANY] * len(after),
        out_specs=[HBM] * (2 * n),
        out_shape=[pltpu.HBM(a.shape, a.dtype) for a in arrs] + [pltpu.HBM(a.shape, a.dtype) for a in slots],
        input_output_aliases={k: k for k in range(2 * n)},
        compiler_params=pltpu.CompilerParams(has_side_effects=DATAFLOW),
    )(*arrs, *slots, ssem, rsem, *after)
    return list(out[n:])


def _pair_exchange_halves(arrs):
    n = len(arrs)

    def body(*refs):
        ins, outs = refs[:n], refs[n:2 * n]
        ssem, rsem = refs[2 * n:]
        x, y, c = _mesh_pos()
        cps = []
        for k in range(n):
            half = arrs[k].shape[1] // 2
            cp = pltpu.make_async_remote_copy(
                src_ref=ins[k].at[:, pl.ds((1 - c) * half, half)], dst_ref=outs[k],
                send_sem=ssem.at[k], recv_sem=rsem.at[k], device_id=(x, y, 1 - c), device_id_type=MESH)
            cp.start()
            cps.append(cp)
        for cp in cps:
            cp.wait()

    return pl.pallas_call(
        body, name="pair_exchange_halves",
        in_specs=[ANY] * n, out_specs=[ANY] * n,
        out_shape=[jax.ShapeDtypeStruct((a.shape[0], a.shape[1] // 2, a.shape[2]), a.dtype) for a in arrs],
        scratch_shapes=[pltpu.SemaphoreType.DMA((n,)), pltpu.SemaphoreType.DMA((n,))],
    )(*arrs)


GRAD_ROW_BLOCKS = 2


def _pair_add(arrs, recvd, core):
    n = len(arrs)
    nb = GRAD_ROW_BLOCKS

    def body(c_ref, *refs):
        for k in range(n):
            refs[2 * n + k][...] = (refs[k][...].astype(F32) + refs[n + k][...].astype(F32)).astype(BF16)

    def blk(a):
        return (1, a.shape[1] // 2 // nb, a.shape[2])

    grid_spec = pltpu.PrefetchScalarGridSpec(
        num_scalar_prefetch=1, grid=(N_SHARD, nb),
        in_specs=[pl.BlockSpec(blk(a), lambda s, i, c: (s, c[0] * nb + i, 0)) for a in arrs]
        + [pl.BlockSpec(blk(a), lambda s, i, c: (s, i, 0)) for a in arrs],
        out_specs=[pl.BlockSpec(blk(a), lambda s, i, c: (s, i, 0)) for a in arrs])
    return pl.pallas_call(
        body, name="pair_add", grid_spec=grid_spec,
        out_shape=[jax.ShapeDtypeStruct(r.shape, BF16) for r in recvd],
        compiler_params=_cparams(2),
    )(core, *arrs, *recvd)


def _own_slot(arrs, chip):
    n = len(arrs)
    nb = GRAD_ROW_BLOCKS

    def body(c_ref, *refs):
        for k in range(n):
            refs[n + k][...] = refs[k][...]

    def blk(a):
        return (1, a.shape[1] // nb, a.shape[2])

    grid_spec = pltpu.PrefetchScalarGridSpec(
        num_scalar_prefetch=1, grid=(nb,),
        in_specs=[pl.BlockSpec(blk(a), lambda i, c: (c[0], i, 0)) for a in arrs],
        out_specs=[pl.BlockSpec(blk(a), lambda i, c: (c[0], i, 0)) for a in arrs])
    return pl.pallas_call(
        body, name="own_slot", grid_spec=grid_spec,
        out_shape=[jax.ShapeDtypeStruct(a.shape, a.dtype) for a in arrs],
        compiler_params=_cparams(1),
    )(chip, *arrs)


def _scatter_to_owners(arrs, slots):
    n = len(arrs)

    def body(*refs):
        ins, outs = refs[:n], refs[2 * n:3 * n]
        ssem, rsem = refs[3 * n:]
        x, y, c = _mesh_pos()
        chip = 2 * x + y
        others = _other_chips(x, y)
        cps = []
        for k in range(n):
            for j, (ox, oy) in enumerate(others):
                cp = pltpu.make_async_remote_copy(
                    src_ref=ins[k].at[2 * ox + oy], dst_ref=outs[k].at[chip],
                    send_sem=ssem.at[3 * k + j], recv_sem=rsem.at[3 * k + j],
                    device_id=(ox, oy, c), device_id_type=MESH)
                cp.start()
                cps.append(cp)
        for k in range(n):
            for j, (ox, oy) in enumerate(others):
                blk = outs[k].at[2 * ox + oy]
                pltpu.make_async_remote_copy(
                    src_ref=blk, dst_ref=blk, send_sem=ssem.at[3 * k + j], recv_sem=rsem.at[3 * k + j],
                    device_id=(ox, oy, c), device_id_type=MESH).wait_recv()
        for cp in cps:
            cp.wait_send()

    return pl.pallas_call(
        body, name="scatter_to_owners",
        in_specs=[ANY] * (2 * n), out_specs=[ANY] * n,
        out_shape=[jax.ShapeDtypeStruct(a.shape, a.dtype) for a in arrs],
        scratch_shapes=[pltpu.SemaphoreType.DMA((3 * n,)), pltpu.SemaphoreType.DMA((3 * n,))],
        input_output_aliases={n + k: k for k in range(n)},
    )(*arrs, *slots)


def _sum_chips(arrs, core):
    n = len(arrs)
    nb = GRAD_ROW_BLOCKS

    def body(c_ref, *refs):
        for k in range(n):
            r = refs[k]
            refs[n + k][...] = ((r[0].astype(F32) + r[1].astype(F32)) + r[2].astype(F32)) + r[3].astype(F32)

    grid_spec = pltpu.PrefetchScalarGridSpec(
        num_scalar_prefetch=1, grid=(nb,),
        in_specs=[pl.BlockSpec((N_SHARD, a.shape[1] // nb, a.shape[2]), lambda i, c: (0, i, 0)) for a in arrs],
        out_specs=[pl.BlockSpec((a.shape[1] // nb, a.shape[2]), lambda i, c: (c[0] * nb + i, 0)) for a in arrs])
    return pl.pallas_call(
        body, name="sum_chips", grid_spec=grid_spec,
        out_shape=[jax.ShapeDtypeStruct((2 * a.shape[1], a.shape[2]), F32) for a in arrs],
        compiler_params=_cparams(1),
    )(core, *arrs)


def _pair_allgather_halves(arrs):
    n = len(arrs)

    def body(*refs):
        outs = refs[n:2 * n]
        ssem, rsem = refs[2 * n:]
        x, y, c = _mesh_pos()
        cps = []
        for k in range(n):
            h = arrs[k].shape[0] // 2
            mine = outs[k].at[pl.ds(c * h, h)]
            cp = pltpu.make_async_remote_copy(src_ref=mine, dst_ref=mine, send_sem=ssem.at[k],
                                              recv_sem=rsem.at[k], device_id=(x, y, 1 - c), device_id_type=MESH)
            cp.start()
            cps.append(cp)
        for k, cp in enumerate(cps):
            h = arrs[k].shape[0] // 2
            theirs = outs[k].at[pl.ds((1 - c) * h, h)]
            pltpu.make_async_remote_copy(src_ref=theirs, dst_ref=theirs, send_sem=ssem.at[k], recv_sem=rsem.at[k],
                                         device_id=(x, y, 1 - c), device_id_type=MESH).wait_recv()
            cp.wait_send()

    return pl.pallas_call(
        body, name="pair_allgather_halves",
        in_specs=[ANY] * n, out_specs=[ANY] * n,
        out_shape=[jax.ShapeDtypeStruct(a.shape, a.dtype) for a in arrs],
        scratch_shapes=[pltpu.SemaphoreType.DMA((n,)), pltpu.SemaphoreType.DMA((n,))],
        input_output_aliases={k: k for k in range(n)},
    )(*arrs)


N_DEV = 8


def _allgather_all(v):
    m_per, n = v.shape

    def body(x_ref, out_ref, send_sems, recv_sems, local_sem):
        x, y, c = _mesh_pos()
        me, sibling = (x, y, c), (x, y, 1 - c)
        chips = _other_chips(x, y)

        def rows(px, py, pc):
            return out_ref.at[pl.ds((4 * px + 2 * py + pc) * m_per, m_per), :]

        def copy(k, block, to, src=None):
            return pltpu.make_async_remote_copy(
                src_ref=rows(*block) if src is None else src, dst_ref=rows(*block),
                send_sem=send_sems.at[k], recv_sem=recv_sems.at[k], device_id=to, device_id_type=MESH)

        mine = pltpu.make_async_copy(x_ref, rows(*me), local_sem)
        mine.start()
        first = [copy(0, me, sibling, src=x_ref)]
        first += [copy(1 + j, me, (*chip, c), src=x_ref) for j, chip in enumerate(chips)]
        for cp in first:
            cp.start()
        passed = [copy(4 + j, (*chip, c), sibling) for j, chip in enumerate(chips)]
        for j, chip in enumerate(chips):
            copy(1 + j, (*chip, c), me).wait_recv()
            passed[j].start()
        copy(0, sibling, me).wait_recv()
        for j, chip in enumerate(chips):
            copy(4 + j, (*chip, 1 - c), me).wait_recv()
        for cp in first + passed:
            cp.wait_send()
        mine.wait()

    return pl.pallas_call(
        body, name="allgather_all",
        out_shape=jax.ShapeDtypeStruct((N_DEV * m_per, n), v.dtype),
        in_specs=[pl.BlockSpec(memory_space=pltpu.VMEM)],
        out_specs=pl.BlockSpec(memory_space=pltpu.VMEM),
        scratch_shapes=[pltpu.SemaphoreType.DMA((7,)), pltpu.SemaphoreType.DMA((7,)), pltpu.SemaphoreType.DMA],
        compiler_params=pltpu.CompilerParams(vmem_limit_bytes=VMEM_LIMIT_MB * 1024 * 1024),
    )(v)


def _adamw_math(w, g, m, v):
    m2 = ADAM_B1 * m + (1.0 - ADAM_B1) * g
    v2 = ADAM_B2 * v + (1.0 - ADAM_B2) * (g * g)
    m_hat = m2 / (1.0 - ADAM_B1 ** ADAM_STEP)
    v_hat = v2 / (1.0 - ADAM_B2 ** ADAM_STEP)
    delta = -ADAM_LR * (m_hat / (jnp.sqrt(v_hat) + ADAM_EPS) + ADAM_WD * w)
    return delta, m2, v2


def _adamw(w, m, v, gs, nblk):
    nl, r, n = w.shape
    assert nl == len(gs) and nl in (1, 2)
    br = r // nblk

    def body(w_ref, m_ref, v_ref, *rest):
        g_refs, (go_ref, d_ref, mo_ref, vo_ref) = rest[:nl], rest[nl:]
        g = g_refs[0][...]
        if nl == 2:
            g = jnp.where(pl.program_id(0) == 0, g, g_refs[1][...])
        delta, m2, v2 = _adamw_math(w_ref[0], g, m_ref[0], v_ref[0])
        go_ref[0] = g
        d_ref[0] = delta
        mo_ref[0] = m2
        vo_ref[0] = v2

    spec = pl.BlockSpec((1, br, n), lambda l, i: (l, i, 0))
    g_specs = [pl.BlockSpec((br, n), lambda l, i: (i, 0))] if nl == 1 else [
        pl.BlockSpec((br, n), lambda l, i: (jnp.where(l == 0, i, nblk - 1), 0)),
        pl.BlockSpec((br, n), lambda l, i: (jnp.where(l == 1, i, 0), 0))]
    return pl.pallas_call(
        body, name="adamw", grid=(nl, nblk),
        in_specs=[spec, spec, spec] + g_specs,
        out_specs=[spec] * 4,
        out_shape=[jax.ShapeDtypeStruct((nl, r, n), F32)] * 4,
        compiler_params=_cparams(2),
    )(w, m, v, *gs)


def _small_reduce_adamw(parts, w, m, v, rep_rows, sh_rows):
    mrows = rep_rows + N_SHARD * sh_rows

    def body(p_ref, w_ref, m_ref, v_ref, go_ref, d_ref, mo_ref, vo_ref):
        x, y, _ = _mesh_pos()
        mine = rep_rows + (2 * x + y) * sh_rows
        g_rep = p_ref[0:rep_rows, :]
        g_sh = p_ref[pl.ds(pl.multiple_of(mine, SUBLANE), sh_rows), :]
        for k in range(1, N_DEV):
            g_rep = g_rep + p_ref[k * mrows:k * mrows + rep_rows, :]
            g_sh = g_sh + p_ref[pl.ds(pl.multiple_of(k * mrows + mine, SUBLANE), sh_rows), :]
        g = jnp.concatenate([g_rep, g_sh], axis=0)
        delta, m2, v2 = _adamw_math(w_ref[...], g, m_ref[...], v_ref[...])
        go_ref[...] = g
        d_ref[...] = delta
        mo_ref[...] = m2
        vo_ref[...] = v2

    return pl.pallas_call(
        body, name="small_reduce_adamw",
        out_shape=[jax.ShapeDtypeStruct((rep_rows + sh_rows, 128), F32)] * 4,
        compiler_params=pltpu.CompilerParams(vmem_limit_bytes=VMEM_LIMIT_MB * 1024 * 1024),
    )(parts, w, m, v)


LANE = 128
REP_SPEC = (("ffn1_norm", 16), ("mix_norm", 16), ("ffn2_norm", 16), ("final_norm", 8), ("pool_w", 128),
            ("pool_scale", 8), ("hgrn_lb_logits", 16), ("hgrn_gnorm", 8), ("lru_wa", 256), ("lru_wx", 256))
SH_SPEC = (("meta_tokens", 32), ("conv_w", 32), ("lru_conv_w", 8), ("conv_b", 8), ("conv_ln_g", 8),
           ("conv_ln_b", 8), ("lru_conv_b", 8), ("lru_ba", 8), ("lru_bx", 8), ("lru_lambda", 8))
REP_ROWS = sum(r for _, r in REP_SPEC)
SH_ROWS = sum(r for _, r in SH_SPEC)


def _pack_rows(vals, spec):
    parts = []
    for name, rows in spec:
        flat = vals[name].astype(F32).reshape(-1, LANE)
        if flat.shape[0] < rows:
            flat = jnp.concatenate([flat, jnp.zeros((rows - flat.shape[0], LANE), F32)], axis=0)
        parts.append(flat)
    return jnp.concatenate(parts, axis=0)


def _unpack_rows(packed, spec, shapes):
    out = {}
    off = 0
    for name, rows in spec:
        shp = shapes[name]
        n = int(np.prod(shp)) // LANE
        out[name] = packed[off:off + n].reshape(shp)
        off += rows
    return out


def _block_diag(blocks):
    n, b, _ = blocks.shape
    return sum(jnp.pad(blocks[g], ((g * b, (n - 1 - g) * b), (g * b, (n - 1 - g) * b))) for g in range(n))


def _diag_blocks(mat, n):
    b = mat.shape[0] // n
    return jnp.stack([mat[g * b:(g + 1) * b, g * b:(g + 1) * b] for g in range(n)])


BIG = ("ffn1_wg", "ffn1_wu", "ffn2_wg", "ffn2_wu", "ffn1_wd", "ffn2_wd", "w_in_even", "w_out_even",
       "w_in_odd", "w_out_odd")
ADAM_BLOCKS = {"ffn1_wg": 8, "ffn1_wu": 8, "ffn2_wg": 8, "ffn2_wu": 8, "ffn1_wd": 4, "ffn2_wd": 4,
               "w_in_even": 4, "w_out_even": 2, "w_in_odd": 4, "w_out_odd": 2}
WEIGHT_NAMES = ('meta_tokens', 'ffn1_norm', 'ffn1_wg', 'ffn1_wu', 'ffn1_wd', 'mix_norm', 'ffn2_norm', 'ffn2_wg',
                'ffn2_wu', 'ffn2_wd', 'w_in_even', 'pool_w', 'pool_scale', 'hgrn_lb_logits', 'hgrn_gnorm',
                'w_out_even', 'w_in_odd', 'conv_w', 'conv_b', 'conv_ln_g', 'conv_ln_b', 'lru_conv_w',
                'lru_conv_b', 'lru_wa', 'lru_ba', 'lru_wx', 'lru_bx', 'lru_lambda', 'w_out_odd', 'final_norm')


def _rows2d(a):
    return a.reshape(-1, a.shape[-1])


def _local_step_v2(x, tgt, w, gathered, small_full):
    s_len, d = x.shape
    t_real = s_len + N_META
    tp = -(-t_real // ROW_ALIGN) * ROW_ALIGN
    tm = _tile(tp, 832, ROW_ALIGN)
    tm_small = _tile(tp, 416, 16)
    tr = _tile(tp, 416, SUBLANE)
    f1 = ("ffn1_norm", "ffn1_wg", "ffn1_wu", "ffn1_wd")
    f2 = ("ffn2_norm", "ffn2_wg", "ffn2_wu", "ffn2_wd")

    meta_full = small_full["meta_tokens"]
    h0 = jnp.concatenate([meta_full, x, jnp.zeros((tp - t_real, d), F32)], axis=0)
    tgt_pad = jnp.concatenate([jnp.zeros((N_META, d), F32), tgt, jnp.zeros((tp - t_real, d), F32)], axis=0)

    w_in_even = jnp.transpose(gathered["w_in_even"], (1, 0, 2)).reshape(d, D_IN_EVEN)
    w_out_even = gathered["w_out_even"].reshape(d, d)
    w_out_odd = gathered["w_out_odd"].reshape(d, d)
    even_piece = [(w_in_even, (d, D_IN_EVEN), (0, 0))]
    odd_pieces = [(gathered["w_in_odd"], (1, d, D_IN_ODD // N_SHARD), (k, 0, 0)) for k in range(N_SHARD)]
    pool_wbd = _block_diag(w["pool_w"][0]).astype(BF16)
    pool_scale = w["pool_scale"]
    wa_bd = _block_diag2(w["lru_wa"][0]).astype(BF16)
    wx_bd = _block_diag2(w["lru_wx"][0]).astype(BF16)
    mst, msk, n_lev = _hgrn_consts(HG_CHUNK)
    conv_w = small_full["conv_w"]
    sf = small_full

    def gain(name, layer):
        return w[name][layer:layer + 1]

    def ffn(h, names, layer):
        return _ffn_fwd(h, gain(names[0], layer), gathered[names[1]], gathered[names[2]], gathered[names[3]],
                        layer, tm)

    h1, a1, b1, n1 = ffn(h0, f1, 0)
    p0, nm0 = _proj_fwd(h1, gain("mix_norm", 0), 0, even_piece, tm_small)
    ya = _pool_fwd(p0, pool_wbd, pool_scale, tr)
    yb, states = _hgrn_fwd(p0, w["hgrn_lb_logits"], w["hgrn_gnorm"], mst, msk, n_lev, tm)
    h2 = _out_fwd(h1, ya, yb, w_out_even, tm)
    h3, a2, b2, n2 = ffn(h2, f2, 0)
    h4, a3, b3, n3 = ffn(h3, f1, 1)
    p1, nm1 = _proj_fwd(h4, gain("mix_norm", 1), 1, odd_pieces, tm_small)
    yc = _convmod_fwd(p1, conv_w, sf["conv_b"], sf["conv_ln_g"], sf["conv_ln_b"], tr)
    lru_args = (sf["lru_conv_w"], sf["lru_conv_b"], wa_bd, sf["lru_ba"], wx_bd, sf["lru_bx"], sf["lru_lambda"])
    yd, hs = _lru_fwd(p1, *lru_args)
    h5 = _out_fwd(h4, yc, yd, w_out_odd, tm)
    h6, a4, b4, n4 = ffn(h5, f2, 1)
    loss, dh6, dg_final = _loss_bwd(h6, w["final_norm"].reshape(1, d), tgt_pad, t_real, tm)

    def ffn_bwd(dho, h, n, a, b, names, layer, acc):
        dh, da, db, dg = _ffn_bwd_act(dho, h, gain(names[0], layer), a, b, gathered[names[1]], gathered[names[2]],
                                      gathered[names[3]], layer, tm_small)
        acc = _ffn_bwd_w(dho, n, a, b, da, db, acc[0], acc[1], acc[2], layer, tm)
        return dh, dg, acc

    none3 = (None, None, None)
    dh5, dg_f2_l1, g_f2 = ffn_bwd(dh6, h5, n4, a4, b4, f2, 1, none3)
    dyc, dyd, dw_out_odd = _out_bwd(dh5, yc, yd, w_out_odd, tm)
    dca, dcb, dconv_w, dconv_vec = _convmod_bwd(p1, dyc, conv_w, sf["conv_b"], sf["conv_ln_g"], sf["conv_ln_b"], tr)
    dlx, dlg, dwa_bd, dwx_bd, dlru_vec = _lru_bwd(p1, hs, dyd, *lru_args)
    dp1 = [dca, dcb, dlx, dlg]
    dh4, dg_mix_l1 = _proj_bwd_act(dh5, h4, gain("mix_norm", 1), 1, dp1, odd_pieces, tm_small)
    dw_in_odd = jnp.stack(_proj_bwd_w(nm1, dp1, tm))
    dh3, dg_f1_l1, g_f1 = ffn_bwd(dh4, h3, n3, a3, b3, f1, 1, none3)
    dh2, dg_f2_l0, g_f2 = ffn_bwd(dh3, h2, n2, a2, b2, f2, 0, g_f2)
    dya, dyb, dw_out_even = _out_bwd(dh2, ya, yb, w_out_even, tm)
    dpool, dpool_wbd, dpool_scale = _pool_bwd(p0, dya, pool_wbd, pool_scale, tr)
    dq, dz, dv, dgate, dlb_logits, dgn_heads = _hgrn_bwd(p0, dyb, states, w["hgrn_lb_logits"], w["hgrn_gnorm"],
                                                         mst, msk, n_lev, tm)
    dp0 = [jnp.concatenate([dpool, dq, dz, dv, dgate], axis=1)]
    dh1, dg_mix_l0 = _proj_bwd_act(dh2, h1, gain("mix_norm", 0), 0, dp0, even_piece, tm_small)
    (dw_in_even,) = _proj_bwd_w(nm0, dp0, tm_small)
    dh0, dg_f1_l0, g_f1 = ffn_bwd(dh1, h0, n1, a1, b1, f1, 0, g_f1)

    grad_x = dh0[N_META:t_real]
    big = {
        "ffn1_wg": g_f1[0], "ffn1_wu": g_f1[1], "ffn1_wd": g_f1[2],
        "ffn2_wg": g_f2[0], "ffn2_wu": g_f2[1], "ffn2_wd": g_f2[2],
        "w_in_even": jnp.transpose(dw_in_even.reshape(d, N_SHARD, D_IN_EVEN // N_SHARD), (1, 0, 2)),
        "w_out_even": dw_out_even.reshape(N_SHARD, d // N_SHARD, d),
        "w_in_odd": dw_in_odd,
        "w_out_odd": dw_out_odd.reshape(N_SHARD, d // N_SHARD, d),
    }
    rep = {
        "ffn1_norm": jnp.concatenate([dg_f1_l0, dg_f1_l1], axis=0),
        "mix_norm": jnp.concatenate([dg_mix_l0, dg_mix_l1], axis=0),
        "ffn2_norm": jnp.concatenate([dg_f2_l0, dg_f2_l1], axis=0),
        "final_norm": dg_final,
        "pool_w": _diag_blocks(dpool_wbd, len(POOL_WINDOWS)),
        "pool_scale": dpool_scale,
        "hgrn_lb_logits": dlb_logits,
        "hgrn_gnorm": jnp.sum(dgn_heads, axis=0),
        "lru_wa": _diag_blocks2(dwa_bd),
        "lru_wx": _diag_blocks2(dwx_bd),
    }
    dmeta = jnp.transpose(dh0[:N_META].reshape(N_META, N_SHARD, 2, LANE), (1, 0, 2, 3)).reshape(N_SHARD, 32, LANE)
    packs = [_pack_rows(rep, REP_SPEC)]
    for s in range(N_SHARD):
        sh = {
            "meta_tokens": dmeta[s], "conv_w": dconv_w[s], "lru_conv_w": dlru_vec[s, 0:4],
            "conv_b": dconv_vec[s, 0:1], "conv_ln_g": dconv_vec[s, 1:2], "conv_ln_b": dconv_vec[s, 2:3],
            "lru_conv_b": dlru_vec[s, 4:5], "lru_ba": dlru_vec[s, 5:6], "lru_bx": dlru_vec[s, 6:7],
            "lru_lambda": dlru_vec[s, 7:8],
        }
        packs.append(_pack_rows(sh, SH_SPEC))
    return loss, grad_x, big, jnp.concatenate(packs, axis=0)


def _block_diag2(heads):
    nb = heads.shape[0] // 2
    return jnp.stack([_block_diag(heads[2 * j:2 * j + 2]) for j in range(nb)])


def _diag_blocks2(mats):
    return jnp.concatenate([_diag_blocks(mats[j], 2) for j in range(mats.shape[0])], axis=0)


def _kernel_v2(x, meta_tokens, ffn1_norm, ffn1_wg, ffn1_wu, ffn1_wd, mix_norm, ffn2_norm, ffn2_wg, ffn2_wu, ffn2_wd, w_in_even, pool_w, pool_scale, hgrn_lb_logits, hgrn_gnorm, w_out_even, w_in_odd, conv_w, conv_b, conv_ln_g, conv_ln_b, lru_conv_w, lru_conv_b, lru_wa, lru_ba, lru_wx, lru_bx, lru_lambda, w_out_odd, final_norm, loss_target, m_meta_tokens, m_ffn1_norm, m_ffn1_wg, m_ffn1_wu, m_ffn1_wd, m_mix_norm, m_ffn2_norm, m_ffn2_wg, m_ffn2_wu, m_ffn2_wd, m_w_in_even, m_pool_w, m_pool_scale, m_hgrn_lb_logits, m_hgrn_gnorm, m_w_out_even, m_w_in_odd, m_conv_w, m_conv_b, m_conv_ln_g, m_conv_ln_b, m_lru_conv_w, m_lru_conv_b, m_lru_wa, m_lru_ba, m_lru_wx, m_lru_bx, m_lru_lambda, m_w_out_odd, m_final_norm, v_meta_tokens, v_ffn1_norm, v_ffn1_wg, v_ffn1_wu, v_ffn1_wd, v_mix_norm, v_ffn2_norm, v_ffn2_wg, v_ffn2_wu, v_ffn2_wd, v_w_in_even, v_pool_w, v_pool_scale, v_hgrn_lb_logits, v_hgrn_gnorm, v_w_out_even, v_w_in_odd, v_conv_w, v_conv_b, v_conv_ln_g, v_conv_ln_b, v_lru_conv_w, v_lru_conv_b, v_lru_wa, v_lru_ba, v_lru_wx, v_lru_bx, v_lru_lambda, v_w_out_odd, v_final_norm):
    args = locals()
    w = {n: args[n] for n in WEIGHT_NAMES}
    m = {n: args["m_" + n] for n in WEIGHT_NAMES}
    v = {n: args["v_" + n] for n in WEIGHT_NAMES}
    shapes = {n: w[n].shape for n in WEIGHT_NAMES}

    big_in = [_rows2d(w[n]).astype(BF16) for n in BIG]
    small_sh = _pack_rows(w, SH_SPEC)
    gath = _allgather_shards(big_in + [small_sh], [True] * len(BIG) + [False])
    gathered = dict(zip(BIG, gath[:len(BIG)]))
    sm = gath[len(BIG)]
    sh_shapes = {n: (N_SHARD,) + tuple(shapes[n]) for n, _ in SH_SPEC}
    per_shard = [_unpack_rows(sm[s], SH_SPEC, shapes) for s in range(N_SHARD)]
    small_full = {}
    for n, _ in SH_SPEC:
        stacked = [per_shard[s][n] for s in range(N_SHARD)]
        small_full[n] = jnp.concatenate([p.reshape(-1, p.shape[-1]) for p in stacked], axis=-1)
    small_full["conv_w"] = jnp.concatenate(
        [small_full["conv_w"], jnp.zeros((CONV_HALO - CONV_WIDTH, D_CONV), F32)], axis=0)

    loss, grad_x, big, small_part = _local_step(x[0], loss_target[0], w, gathered, small_full)
    loss = lax.psum(loss[0, 0], ("x", "y", "c"))

    core = lax.axis_index("c").astype(jnp.int32).reshape(1)
    parts = [big[n] for n in BIG]
    recvd = _pair_exchange_halves(parts)
    pair = _pair_add(parts, recvd, core)
    chip = (2 * lax.axis_index("x") + lax.axis_index("y")).astype(jnp.int32).reshape(1)
    slots = _scatter_to_owners(pair, _own_slot(pair, chip))
    halves = _sum_chips(slots, core)
    full = _pair_allgather_halves(halves)
    out_g, out_d, out_m, out_v = {}, {}, {}, {}
    for n, g in zip(BIG, full):
        res = _adamw(_rows2d(w[n]), _rows2d(m[n]), _rows2d(v[n]), g, 0, ADAM_BLOCKS[n])
        out_g[n], out_d[n], out_m[n], out_v[n] = [r.reshape(shapes[n]) for r in res]

    gathered_small = _allgather_all(small_part)

    def pack_small(src):
        return jnp.concatenate([_pack_rows(src, REP_SPEC), _pack_rows(src, SH_SPEC)], axis=0)

    res = _small_reduce_adamw(gathered_small, pack_small(w), pack_small(m), pack_small(v), REP_ROWS, SH_ROWS)
    for dst, packed in zip((out_g, out_d, out_m, out_v), res):
        dst.update(_unpack_rows(packed[:REP_ROWS], REP_SPEC, shapes))
        dst.update(_unpack_rows(packed[REP_ROWS:], SH_SPEC, shapes))

    return (loss, grad_x[None], *[out_g[n] for n in WEIGHT_NAMES], *[out_d[n] for n in WEIGHT_NAMES],
            *[out_m[n] for n in WEIGHT_NAMES], *[out_v[n] for n in WEIGHT_NAMES])


GATHER_GROUPS = (
    (("small", 0),),
    (("ffn1_wg", 0), ("ffn1_wu", 0), ("ffn1_wd", 0)),
    (("w_in_even", 0), ("w_out_even", 0)),
    (("ffn2_wg", 0), ("ffn2_wu", 0), ("ffn2_wd", 0)),
    (("ffn1_wg", 1), ("ffn1_wu", 1), ("ffn1_wd", 1)),
    (("w_in_odd", 0), ("w_out_odd", 0)),
    (("ffn2_wg", 1), ("ffn2_wu", 1), ("ffn2_wd", 1)),
)
ADAM_ROW_BLOCKS = {"ffn1_wg": 2, "ffn1_wu": 2, "ffn2_wg": 2, "ffn2_wu": 2, "ffn1_wd": 2, "ffn2_wd": 2,
                   "w_in_even": 4, "w_out_even": 2, "w_in_odd": 4, "w_out_odd": 2}
TRANSPOSED = ("ffn1_wg", "ffn1_wu", "ffn2_wg", "ffn2_wu", "w_in_even")
SCATTER_DEPTH = 2


def _unpack_small(sm, shapes):
    per_shard = [_unpack_rows(sm[s], SH_SPEC, shapes) for s in range(N_SHARD)]
    full = {}
    for n, _ in SH_SPEC:
        full[n] = jnp.concatenate([per_shard[s][n].reshape(-1, shapes[n][-1]) for s in range(N_SHARD)], axis=-1)
    full["conv_w"] = jnp.concatenate([full["conv_w"], jnp.zeros((CONV_HALO - CONV_WIDTH, D_CONV), F32)], axis=0)
    return full


def _local_step(x, tgt, w, shapes, fetch, emit):
    s_len, d = x.shape
    t_real = s_len + N_META
    tp = -(-t_real // ROW_ALIGN) * ROW_ALIGN
    tm = _tile(tp, 832, ROW_ALIGN)
    tm_small = _tile(tp, 416, 16)
    tr = _tile(tp, 416, SUBLANE)

    def gain(name, layer):
        return w[name][layer:layer + 1]

    pool_wbd = _block_diag(w["pool_w"][0]).astype(BF16)
    pool_scale = w["pool_scale"]
    wa_bd = _block_diag2(w["lru_wa"][0]).astype(BF16)
    wx_bd = _block_diag2(w["lru_wx"][0]).astype(BF16)
    mst, msk, n_lev = _hgrn_consts(HG_CHUNK)

    (sm,) = fetch(0, None)
    sf = _unpack_small(sm, shapes)
    h0 = jnp.concatenate([sf["meta_tokens"], x, jnp.zeros((tp - t_real, d), F32)], axis=0)
    tgt_pad = jnp.concatenate([jnp.zeros((N_META, d), F32), tgt, jnp.zeros((tp - t_real, d), F32)], axis=0)
    f1l0 = fetch(1, h0)
    h1, a1, b1, n1 = _ffn_fwd(h0, gain("ffn1_norm", 0), *f1l0, 0, tm)
    w_in_even4, w_out_even4 = fetch(2, h1)
    w_out_even = w_out_even4.reshape(d, d)
    even_piece = [(w_in_even4.reshape(D_IN_EVEN, d), (D_IN_EVEN, d), (0, 0))]
    p0, nm0 = _proj_fwd(h1, gain("mix_norm", 0), 0, even_piece, tm_small, wt=True)
    ya = _pool_fwd(p0, pool_wbd, pool_scale, tr)
    yb, states = _hgrn_fwd(p0, w["hgrn_lb_logits"], w["hgrn_gnorm"], mst, msk, n_lev, tm)
    h2 = _out_fwd(h1, ya, yb, w_out_even, tm)
    f2l0 = fetch(3, h2)
    h3, a2, b2, n2 = _ffn_fwd(h2, gain("ffn2_norm", 0), *f2l0, 0, tm)
    f1l1 = fetch(4, h3)
    h4, a3, b3, n3 = _ffn_fwd(h3, gain("ffn1_norm", 1), *f1l1, 0, tm)
    w_in_odd4, w_out_odd4 = fetch(5, h4)
    w_out_odd = w_out_odd4.reshape(d, d)
    odd_pieces = [(w_in_odd4, (1, d, D_IN_ODD // N_SHARD), (k, 0, 0)) for k in range(N_SHARD)]
    p1, nm1 = _proj_fwd(h4, gain("mix_norm", 1), 1, odd_pieces, tm_small)
    yc = _convmod_fwd(p1, sf["conv_w"], sf["conv_b"], sf["conv_ln_g"], sf["conv_ln_b"], tr)
    lru_args = (sf["lru_conv_w"], sf["lru_conv_b"], wa_bd, sf["lru_ba"], wx_bd, sf["lru_bx"], sf["lru_lambda"])
    yd, hs = _lru_fwd(p1, *lru_args)
    h5 = _out_fwd(h4, yc, yd, w_out_odd, tm)
    f2l1 = fetch(6, h5)
    h6, a4, b4, n4 = _ffn_fwd(h5, gain("ffn2_norm", 1), *f2l1, 0, tm)
    loss, dh6, dg_final = _loss_bwd(h6, w["final_norm"].reshape(1, d), tgt_pad, t_real, tm)

    def ffn_bwd(dho, h, n, a, b, norm, wts, after=()):
        dh, da, db, dg = _ffn_bwd_act(dho, h, norm, a, b, *wts, 0, tm_small, after)
        return dh, dg, _ffn_bwd_w(dho, n, a, b, da, db, tm)

    dh5, dg_f2_l1, g = ffn_bwd(dh6, h5, n4, a4, b4, gain("ffn2_norm", 1), f2l1)
    sent = emit((("ffn2_wg", 1), ("ffn2_wu", 1), ("ffn2_wd", 1)), g)
    dyc, dyd, dw_out_odd = _out_bwd(dh5, yc, yd, w_out_odd, tm, tuple(sent))
    dca, dcb, dconv_w, dconv_vec = _convmod_bwd(p1, dyc, sf["conv_w"], sf["conv_b"], sf["conv_ln_g"],
                                                sf["conv_ln_b"], tr)
    dlx, dlg, dwa_bd, dwx_bd, dlru_vec = _lru_bwd(p1, hs, dyd, *lru_args)
    dp1 = [dca, dcb, dlx, dlg]
    dh4, dg_mix_l1 = _proj_bwd_act(dh5, h4, gain("mix_norm", 1), 1, dp1, odd_pieces, tm_small)
    dw_in_odd = jnp.stack(_proj_bwd_w(nm1, dp1, tm))
    dh3, dg_f1_l1, g = ffn_bwd(dh4, h3, n3, a3, b3, gain("ffn1_norm", 1), f1l1)
    sent = emit((("w_out_odd", 0), ("w_in_odd", 0), ("ffn1_wg", 1), ("ffn1_wu", 1), ("ffn1_wd", 1)),
                [dw_out_odd.reshape(N_SHARD, d // N_SHARD, d), dw_in_odd] + list(g))
    dh2, dg_f2_l0, g_f2l0 = ffn_bwd(dh3, h2, n2, a2, b2, gain("ffn2_norm", 0), f2l0, tuple(sent))
    dya, dyb, dw_out_even = _out_bwd(dh2, ya, yb, w_out_even, tm)
    dpool, dpool_wbd, dpool_scale = _pool_bwd(p0, dya, pool_wbd, pool_scale, tr)
    dq, dz, dv, dgate, dlb_logits, dgn_heads = _hgrn_bwd(p0, dyb, states, w["hgrn_lb_logits"], w["hgrn_gnorm"],
                                                         mst, msk, n_lev, tm)
    dp0 = [jnp.concatenate([dpool, dq, dz, dv, dgate], axis=1)]
    dh1, dg_mix_l0 = _proj_bwd_act(dh2, h1, gain("mix_norm", 0), 0, dp0, even_piece, tm_small, wt=True)
    (dw_in_even_t,) = _proj_bwd_w(nm0, dp0, tm_small, wt=True)
    sent = emit((("ffn2_wg", 0), ("ffn2_wu", 0), ("ffn2_wd", 0), ("w_out_even", 0), ("w_in_even", 0)),
                list(g_f2l0) + [dw_out_even.reshape(N_SHARD, d // N_SHARD, d),
                                dw_in_even_t.reshape(N_SHARD, D_IN_EVEN // N_SHARD, d)])
    dh0, dg_f1_l0, g = ffn_bwd(dh1, h0, n1, a1, b1, gain("ffn1_norm", 0), f1l0, tuple(sent))
    emit((("ffn1_wg", 0), ("ffn1_wu", 0), ("ffn1_wd", 0)), g)

    grad_x = dh0[N_META:t_real]
    rep = {
        "ffn1_norm": jnp.concatenate([dg_f1_l0, dg_f1_l1], axis=0),
        "mix_norm": jnp.concatenate([dg_mix_l0, dg_mix_l1], axis=0),
        "ffn2_norm": jnp.concatenate([dg_f2_l0, dg_f2_l1], axis=0),
        "final_norm": dg_final,
        "pool_w": _diag_blocks(dpool_wbd, len(POOL_WINDOWS)),
        "pool_scale": dpool_scale,
        "hgrn_lb_logits": dlb_logits,
        "hgrn_gnorm": jnp.sum(dgn_heads, axis=0),
        "lru_wa": _diag_blocks2(dwa_bd),
        "lru_wx": _diag_blocks2(dwx_bd),
    }
    dmeta = jnp.transpose(dh0[:N_META].reshape(N_META, N_SHARD, 2, LANE), (1, 0, 2, 3)).reshape(N_SHARD, 32, LANE)
    packs = [_pack_rows(rep, REP_SPEC)]
    for s in range(N_SHARD):
        sh = {
            "meta_tokens": dmeta[s], "conv_w": dconv_w[s], "lru_conv_w": dlru_vec[s, 0:4],
            "conv_b": dconv_vec[s, 0:1], "conv_ln_g": dconv_vec[s, 1:2], "conv_ln_b": dconv_vec[s, 2:3],
            "lru_conv_b": dlru_vec[s, 4:5], "lru_ba": dlru_vec[s, 5:6], "lru_bx": dlru_vec[s, 6:7],
            "lru_lambda": dlru_vec[s, 7:8],
        }
        packs.append(_pack_rows(sh, SH_SPEC))
    return loss, grad_x, jnp.concatenate(packs, axis=0)


def kernel(x, meta_tokens, ffn1_norm, ffn1_wg, ffn1_wu, ffn1_wd, mix_norm, ffn2_norm, ffn2_wg, ffn2_wu, ffn2_wd, w_in_even, pool_w, pool_scale, hgrn_lb_logits, hgrn_gnorm, w_out_even, w_in_odd, conv_w, conv_b, conv_ln_g, conv_ln_b, lru_conv_w, lru_conv_b, lru_wa, lru_ba, lru_wx, lru_bx, lru_lambda, w_out_odd, final_norm, loss_target, m_meta_tokens, m_ffn1_norm, m_ffn1_wg, m_ffn1_wu, m_ffn1_wd, m_mix_norm, m_ffn2_norm, m_ffn2_wg, m_ffn2_wu, m_ffn2_wd, m_w_in_even, m_pool_w, m_pool_scale, m_hgrn_lb_logits, m_hgrn_gnorm, m_w_out_even, m_w_in_odd, m_conv_w, m_conv_b, m_conv_ln_g, m_conv_ln_b, m_lru_conv_w, m_lru_conv_b, m_lru_wa, m_lru_ba, m_lru_wx, m_lru_bx, m_lru_lambda, m_w_out_odd, m_final_norm, v_meta_tokens, v_ffn1_norm, v_ffn1_wg, v_ffn1_wu, v_ffn1_wd, v_mix_norm, v_ffn2_norm, v_ffn2_wg, v_ffn2_wu, v_ffn2_wd, v_w_in_even, v_pool_w, v_pool_scale, v_hgrn_lb_logits, v_hgrn_gnorm, v_w_out_even, v_w_in_odd, v_conv_w, v_conv_b, v_conv_ln_g, v_conv_ln_b, v_lru_conv_w, v_lru_conv_b, v_lru_wa, v_lru_ba, v_lru_wx, v_lru_bx, v_lru_lambda, v_w_out_odd, v_final_norm):
    args = locals()
    w = {n: args[n] for n in WEIGHT_NAMES}
    m = {n: args["m_" + n] for n in WEIGHT_NAMES}
    v = {n: args["v_" + n] for n in WEIGHT_NAMES}
    shapes = {n: w[n].shape for n in WEIGHT_NAMES}
    core = lax.axis_index("c").astype(jnp.int32).reshape(1)
    chip = (2 * lax.axis_index("x") + lax.axis_index("y")).astype(jnp.int32).reshape(1)

    def view(a, n):
        return jnp.swapaxes(a, 1, 2) if n in TRANSPOSED else a

    wv, mv, vv = [{n: view(src[n], n) for n in BIG} for src in (w, m, v)]

    def shard(key):
        n, l = key
        return _pack_rows(w, SH_SPEC) if n == "small" else wv[n][l].astype(BF16)

    started = {}
    for groups in (GATHER_GROUPS[:2], GATHER_GROUPS[2:]):
        gkeys = [key for grp in groups for key in grp]
        ssem, rsem, srcs, lands, token = _gather_start([shard(key) for key in gkeys])
        for k, key in enumerate(gkeys):
            started[key] = (ssem, rsem, srcs[k], lands[k], k, token)

    def pack_small(src):
        return jnp.concatenate([_pack_rows(src, REP_SPEC), _pack_rows(src, SH_SPEC)], axis=0)

    small_packs = [pack_small(src) for src in (w, m, v)]

    def fetch(group, after):
        st = [started[key] for key in GATHER_GROUPS[group]]
        deps = (st[0][5],) if after is None else (after,)
        if group == 1:
            deps += (started[GATHER_GROUPS[2][0]][5],) + tuple(small_packs)
        return _gather_wait(st[0][0], st[0][1], [s[2] for s in st], [s[3] for s in st], [s[4] for s in st], deps)

    in_flight, reduced = [], {}

    def collect(entry, after):
        gkeys, gs_sem, gr_sem, pair_thru, slots_thru, _ = entry
        slots = _scatter_wait(gs_sem, gr_sem, pair_thru, slots_thru, after)
        full = _pair_allgather_halves(_sum_chips(slots, core))
        reduced.update(zip(gkeys, full))
        return full[0]

    def emit(gkeys, grads):
        grads = list(grads)
        pair = _pair_add(grads, _pair_exchange_halves(grads), core)
        in_flight.append((gkeys,) + tuple(_scatter_start(pair, _own_slot(pair, chip))))
        token = in_flight[-1][-1]
        if len(in_flight) > SCATTER_DEPTH:
            return token, collect(in_flight[-1 - SCATTER_DEPTH], (token,))
        return (token,)

    loss, grad_x, small_part = _local_step(x[0], loss_target[0], w, shapes, fetch, emit)
    loss = lax.psum(loss[0, 0], ("x", "y", "c"))

    small_res = _small_reduce_adamw(_allgather_all(small_part), *small_packs, REP_ROWS, SH_ROWS)
    for entry in in_flight[-SCATTER_DEPTH:]:
        collect(entry, (small_res[0],))

    out_g, out_d, out_m, out_v = {}, {}, {}, {}
    for n in BIG:
        gs = [reduced[(n, l)] for l in range(shapes[n][0])]
        res = _adamw(wv[n], mv[n], vv[n], gs, ADAM_ROW_BLOCKS[n])
        out_g[n], out_d[n], out_m[n], out_v[n] = [view(r, n) for r in res]

    for dst, packed in zip((out_g, out_d, out_m, out_v), small_res):
        dst.update(_unpack_rows(packed[:REP_ROWS], REP_SPEC, shapes))
        dst.update(_unpack_rows(packed[REP_ROWS:], SH_SPEC, shapes))

    return (loss, grad_x[None], *[out_g[n] for n in WEIGHT_NAMES], *[out_d[n] for n in WEIGHT_NAMES],
            *[out_m[n] for n in WEIGHT_NAMES], *[out_v[n] for n in WEIGHT_NAMES])
```

```python
import functools

import numpy as np
import jax
import jax.numpy as jnp
from jax import lax
from jax.experimental import pallas as pl
from jax.experimental.pallas import tpu as pltpu

F32 = jnp.float32
BF16 = jnp.bfloat16
MESH = pl.DeviceIdType.MESH

EPS = 1e-6
N_META = 16
D_MODEL = 1024
D_FF = 2816
N_SHARD = 4
FF_SHARD = D_FF // N_SHARD
D_POOL = 256
POOL_GROUP = 64
POOL_WINDOWS = (2, 4, 8, 16)
D_HGRN = 768
HG_HEADS = 6
HEAD = 128
HG_CHUNK = 64
HG_HEADS_PER_STEP = 2
D_IN_EVEN = D_POOL + 4 * D_HGRN
D_CONV = 512
CONV_WIDTH = 31
CONV_HALO = 32
D_LRU = 512
LRU_CONV = 4
LRU_HALO = 8
LRU_C = 8.0
D_IN_ODD = 2 * D_CONV + 2 * D_LRU
SUBLANE = 8
ROW_ALIGN = 64

ADAM_LR = 0.001
ADAM_B1 = 0.9
ADAM_B2 = 0.999
ADAM_EPS = 1e-08
ADAM_WD = 0.01
ADAM_STEP = 10

VMEM_LIMIT_MB = 56


def _cparams(n_grid_axes=0, vmem_mb=VMEM_LIMIT_MB):
    sem = ("arbitrary",) * n_grid_axes if n_grid_axes else None
    return pltpu.CompilerParams(dimension_semantics=sem, vmem_limit_bytes=vmem_mb * 1024 * 1024)


def _tile(n, target, mult):
    best = None
    for t in range(mult, min(n, target) + 1, mult):
        if n % t == 0:
            best = t
    assert best is not None, (n, target, mult)
    return best


def _dot(a, b):
    return jnp.dot(a, b, preferred_element_type=F32)


def _dot_nt(a, b):
    return lax.dot_general(a, b, (((1,), (1,)), ((), ())), preferred_element_type=F32)


def _dot_tn(a, b):
    return lax.dot_general(a, b, (((0,), (0,)), ((), ())), preferred_element_type=F32)


def _sigmoid(x):
    return 1.0 / (1.0 + jnp.exp(-x))


def _colsum(x):
    return jnp.sum(x, axis=0, keepdims=True)


def _rms_stats(h):
    rstd = lax.rsqrt(jnp.mean(h * h, axis=-1, keepdims=True) + EPS)
    return rstd, h * rstd


def _rms_bwd(dn, g, rstd, xhat):
    dng = dn * g
    dh = rstd * (dng - xhat * jnp.mean(dng * xhat, axis=-1, keepdims=True))
    return dh, _colsum(dn * xhat)


def _ffn_fwd(h, norm, wg4, wu4, wd4, layer, tm):
    tp, d = h.shape
    nt = tp // tm

    def body(h_ref, g_ref, wg_ref, wu_ref, wd_ref, ho_ref, ga_ref, gb_ref, sa_ref, n_ref, n_sc, acc):
        s = pl.program_id(1)

        @pl.when(s == 0)
        def _():
            hh = h_ref[...]
            rstd, xhat = _rms_stats(hh)
            n = (xhat * g_ref[...]).astype(BF16)
            n_sc[...] = n
            n_ref[...] = n
            acc[...] = jnp.zeros_like(acc)

        n = n_sc[...]
        a = _dot_nt(n, wg_ref[0])
        b = _dot_nt(n, wu_ref[0])
        sig = _sigmoid(a)
        sil = a * sig
        ga_ref[0] = (sig * (1.0 + a * (1.0 - sig)) * b).astype(BF16)
        gb_ref[0] = sil.astype(BF16)
        sg = (sil * b).astype(BF16)
        sa_ref[0] = sg
        acc[...] += _dot(sg, wd_ref[0])

        @pl.when(s == N_SHARD - 1)
        def _():
            ho_ref[...] = h_ref[...] + 0.5 * acc[...]

    return pl.pallas_call(
        body, name="ffn_fwd",
        grid=(nt, N_SHARD),
        in_specs=[
            pl.BlockSpec((tm, d), lambda i, s: (i, 0)),
            pl.BlockSpec((1, d), lambda i, s: (0, 0)),
            pl.BlockSpec((1, FF_SHARD, d), lambda i, s: (s, layer, 0)),
            pl.BlockSpec((1, FF_SHARD, d), lambda i, s: (s, layer, 0)),
            pl.BlockSpec((1, FF_SHARD, d), lambda i, s: (s, layer, 0)),
        ],
        out_specs=[
            pl.BlockSpec((tm, d), lambda i, s: (i, 0)),
            pl.BlockSpec((1, tm, FF_SHARD), lambda i, s: (s, i, 0)),
            pl.BlockSpec((1, tm, FF_SHARD), lambda i, s: (s, i, 0)),
            pl.BlockSpec((1, tm, FF_SHARD), lambda i, s: (s, i, 0)),
            pl.BlockSpec((tm, d), lambda i, s: (i, 0)),
        ],
        out_shape=[
            jax.ShapeDtypeStruct((tp, d), F32),
            jax.ShapeDtypeStruct((N_SHARD, tp, FF_SHARD), BF16),
            jax.ShapeDtypeStruct((N_SHARD, tp, FF_SHARD), BF16),
            jax.ShapeDtypeStruct((N_SHARD, tp, FF_SHARD), BF16),
            jax.ShapeDtypeStruct((tp, d), BF16),
        ],
        scratch_shapes=[pltpu.VMEM((tm, d), BF16), pltpu.VMEM((tm, d), F32)],
        compiler_params=_cparams(2),
    )(h, norm, wg4, wu4, wd4)


def _ffn_bwd_act(dho, h, norm, ga4, gb4, wg4, wu4, wd4, layer, tm, after=()):
    tp, d = h.shape
    nt = tp // tm

    def body(dho_ref, h_ref, g_ref, ga_ref, gb_ref, wg_ref, wu_ref, wd_ref, *rest):
        dh_ref, da_ref, db_ref, dg_ref, dy_ref, dn_sc = rest[len(after):]
        i = pl.program_id(0)
        s = pl.program_id(1)

        @pl.when(s == 0)
        def _():
            dy_ref[...] = (0.5 * dho_ref[...]).astype(BF16)
            dn_sc[...] = jnp.zeros_like(dn_sc)

        @pl.when((s == 0) & (i == 0))
        def _():
            dg_ref[...] = jnp.zeros_like(dg_ref)

        ds = _dot_nt(dy_ref[...], wd_ref[0])
        da = (ds * ga_ref[0].astype(F32)).astype(BF16)
        db = (ds * gb_ref[0].astype(F32)).astype(BF16)
        da_ref[0] = da
        db_ref[0] = db
        dn_sc[...] += _dot(da, wg_ref[0]) + _dot(db, wu_ref[0])

        @pl.when(s == N_SHARD - 1)
        def _():
            rstd, xhat = _rms_stats(h_ref[...])
            dh, dg = _rms_bwd(dn_sc[...], g_ref[...], rstd, xhat)
            dh_ref[...] = dho_ref[...] + dh
            dg_ref[...] += dg

    return pl.pallas_call(
        body, name="ffn_bwd_act",
        grid=(nt, N_SHARD),
        in_specs=[
            pl.BlockSpec((tm, d), lambda i, s: (i, 0)),
            pl.BlockSpec((tm, d), lambda i, s: (i, 0)),
            pl.BlockSpec((1, d), lambda i, s: (0, 0)),
            pl.BlockSpec((1, tm, FF_SHARD), lambda i, s: (s, i, 0)),
            pl.BlockSpec((1, tm, FF_SHARD), lambda i, s: (s, i, 0)),
            pl.BlockSpec((1, FF_SHARD, d), lambda i, s: (s, layer, 0)),
            pl.BlockSpec((1, FF_SHARD, d), lambda i, s: (s, layer, 0)),
            pl.BlockSpec((1, FF_SHARD, d), lambda i, s: (s, layer, 0)),
        ] + [pl.BlockSpec(memory_space=pl.ANY)] * len(after),
        out_specs=[
            pl.BlockSpec((tm, d), lambda i, s: (i, 0)),
            pl.BlockSpec((1, tm, FF_SHARD), lambda i, s: (s, i, 0)),
            pl.BlockSpec((1, tm, FF_SHARD), lambda i, s: (s, i, 0)),
            pl.BlockSpec((1, d), lambda i, s: (0, 0)),
            pl.BlockSpec((tm, d), lambda i, s: (i, 0)),
        ],
        out_shape=[
            jax.ShapeDtypeStruct((tp, d), F32),
            jax.ShapeDtypeStruct((N_SHARD, tp, FF_SHARD), BF16),
            jax.ShapeDtypeStruct((N_SHARD, tp, FF_SHARD), BF16),
            jax.ShapeDtypeStruct((1, d), F32),
            jax.ShapeDtypeStruct((tp, d), BF16),
        ],
        scratch_shapes=[pltpu.VMEM((tm, d), F32)],
        compiler_params=_cparams(2),
    )(dho, h, norm, ga4, gb4, wg4, wu4, wd4, *after)


def _ffn_bwd_w(dy, n, sa4, da4, db4, tm):
    tp, d = n.shape
    nt = tp // tm

    def body(dy_ref, n_ref, sa_ref, da_ref, db_ref, og_ref, ou_ref, od_ref, accg, accu, accd):
        i = pl.program_id(1)

        @pl.when(i == 0)
        def _():
            accg[...] = jnp.zeros_like(accg)
            accu[...] = jnp.zeros_like(accu)
            accd[...] = jnp.zeros_like(accd)

        nn = n_ref[...]
        accg[...] += _dot_tn(da_ref[0], nn)
        accu[...] += _dot_tn(db_ref[0], nn)
        accd[...] += _dot_tn(sa_ref[0], dy_ref[...])

        @pl.when(i == nt - 1)
        def _():
            og_ref[0] = accg[...].astype(BF16)
            ou_ref[0] = accu[...].astype(BF16)
            od_ref[0] = accd[...].astype(BF16)

    in_specs = [
        pl.BlockSpec((tm, d), lambda s, i: (i, 0)),
        pl.BlockSpec((tm, d), lambda s, i: (i, 0)),
        pl.BlockSpec((1, tm, FF_SHARD), lambda s, i: (s, i, 0)),
        pl.BlockSpec((1, tm, FF_SHARD), lambda s, i: (s, i, 0)),
        pl.BlockSpec((1, tm, FF_SHARD), lambda s, i: (s, i, 0)),
    ]
    return pl.pallas_call(
        body, name="ffn_bwd_w",
        grid=(N_SHARD, nt),
        in_specs=in_specs,
        out_specs=[pl.BlockSpec((1, FF_SHARD, d), lambda s, i: (s, 0, 0))] * 3,
        out_shape=[jax.ShapeDtypeStruct((N_SHARD, FF_SHARD, d), BF16)] * 3,
        scratch_shapes=[pltpu.VMEM((FF_SHARD, d), F32)] * 3,
        compiler_params=_cparams(2),
    )(dy, n, sa4, da4, db4)


def _proj_fwd(h, norm, layer, w_pieces, tm, wt=False):
    tp, d = h.shape
    widths = [bs[-2] if wt else bs[-1] for _, bs, _ in w_pieces]
    ntot = sum(widths)
    npc = len(w_pieces)

    def body(*refs):
        h_ref, g_ref = refs[:2]
        w_refs = refs[2:2 + npc]
        p_ref, n_ref = refs[2 + npc:]
        rstd, xhat = _rms_stats(h_ref[...])
        n = (xhat * g_ref[...]).astype(BF16)
        n_ref[...] = n
        off = 0
        for k in range(npc):
            w = w_refs[k][...]
            w = w.reshape(w.shape[-2], w.shape[-1])
            p_ref[:, off:off + widths[k]] = _dot_nt(n, w) if wt else _dot(n, w)
            off += widths[k]

    in_specs = [pl.BlockSpec((tm, d), lambda i: (i, 0)), pl.BlockSpec((1, d), lambda i: (0, 0))]
    for _, bs, idx in w_pieces:
        in_specs.append(pl.BlockSpec(bs, functools.partial(lambda i, idx: idx, idx=idx)))
    return pl.pallas_call(
        body, name="proj_fwd",
        grid=(tp // tm,),
        in_specs=in_specs,
        out_specs=[pl.BlockSpec((tm, ntot), lambda i: (i, 0)), pl.BlockSpec((tm, d), lambda i: (i, 0))],
        out_shape=[jax.ShapeDtypeStruct((tp, ntot), F32), jax.ShapeDtypeStruct((tp, d), BF16)],
        compiler_params=_cparams(1),
    )(h, norm, *[w for w, _, _ in w_pieces])


def _proj_bwd_act(dres, h, norm, layer, dp_pieces, w_pieces, tm, wt=False):
    tp, d = h.shape
    npc = len(w_pieces)

    def body(*refs):
        dres_ref, h_ref, g_ref = refs[:3]
        dp_refs = refs[3:3 + npc]
        w_refs = refs[3 + npc:3 + 2 * npc]
        dh_ref, dg_ref = refs[3 + 2 * npc:]
        i = pl.program_id(0)

        @pl.when(i == 0)
        def _():
            dg_ref[...] = jnp.zeros_like(dg_ref)

        dn = None
        for k in range(npc):
            w = w_refs[k][...]
            w = w.reshape(w.shape[-2], w.shape[-1])
            t = _dot(dp_refs[k][...], w) if wt else _dot_nt(dp_refs[k][...], w)
            dn = t if dn is None else dn + t
        rstd, xhat = _rms_stats(h_ref[...])
        dh, dg = _rms_bwd(dn, g_ref[...], rstd, xhat)
        dh_ref[...] = dres_ref[...] + dh
        dg_ref[...] += dg

    in_specs = [pl.BlockSpec((tm, d), lambda i: (i, 0)), pl.BlockSpec((tm, d), lambda i: (i, 0)),
                pl.BlockSpec((1, d), lambda i: (0, 0))]
    for dp in dp_pieces:
        in_specs.append(pl.BlockSpec((tm, dp.shape[1]), lambda i: (i, 0)))
    for _, bs, idx in w_pieces:
        in_specs.append(pl.BlockSpec(bs, functools.partial(lambda i, idx: idx, idx=idx)))
    return pl.pallas_call(
        body, name="proj_bwd_act",
        grid=(tp // tm,),
        in_specs=in_specs,
        out_specs=[pl.BlockSpec((tm, d), lambda i: (i, 0)), pl.BlockSpec((1, d), lambda i: (0, 0))],
        out_shape=[jax.ShapeDtypeStruct((tp, d), F32), jax.ShapeDtypeStruct((1, d), F32)],
        compiler_params=_cparams(1),
    )(dres, h, norm, *dp_pieces, *[w for w, _, _ in w_pieces])


def _proj_bwd_w(n, dp_pieces, tm, wt=False):
    tp, d = n.shape
    npc = len(dp_pieces)
    widths = [dp.shape[1] for dp in dp_pieces]
    oshape = (lambda w: (w, d)) if wt else (lambda w: (d, w))

    def body(*refs):
        n_ref = refs[0]
        dp_refs = refs[1:1 + npc]
        o_refs = refs[1 + npc:1 + 2 * npc]
        accs = refs[1 + 2 * npc:]
        i = pl.program_id(0)

        @pl.when(i == 0)
        def _():
            for acc in accs:
                acc[...] = jnp.zeros_like(acc)

        nn = n_ref[...]
        for k in range(npc):
            accs[k][...] += _dot_tn(dp_refs[k][...], nn) if wt else _dot_tn(nn, dp_refs[k][...])

        @pl.when(i == pl.num_programs(0) - 1)
        def _():
            for k in range(npc):
                o_refs[k][...] = accs[k][...].astype(BF16)

    return pl.pallas_call(
        body, name="proj_bwd_w",
        grid=(tp // tm,),
        in_specs=[pl.BlockSpec((tm, d), lambda i: (i, 0))]
        + [pl.BlockSpec((tm, w), lambda i: (i, 0)) for w in widths],
        out_specs=[pl.BlockSpec(oshape(w), lambda i: (0, 0)) for w in widths],
        out_shape=[jax.ShapeDtypeStruct(oshape(w), BF16) for w in widths],
        scratch_shapes=[pltpu.VMEM(oshape(w), F32) for w in widths],
        compiler_params=_cparams(1),
    )(n, *dp_pieces)


def _out_fwd(h, ya, yb, w, tm):
    tp, d = h.shape
    na, nb = ya.shape[1], yb.shape[1]

    def body(h_ref, ya_ref, yb_ref, w_ref, o_ref):
        y = _dot(ya_ref[...].astype(BF16), w_ref[0:na, :]) + _dot(yb_ref[...].astype(BF16), w_ref[na:, :])
        o_ref[...] = h_ref[...] + y

    return pl.pallas_call(
        body, name="out_fwd",
        grid=(tp // tm,),
        in_specs=[pl.BlockSpec((tm, d), lambda i: (i, 0)), pl.BlockSpec((tm, na), lambda i: (i, 0)),
                  pl.BlockSpec((tm, nb), lambda i: (i, 0)), pl.BlockSpec((d, d), lambda i: (0, 0))],
        out_specs=pl.BlockSpec((tm, d), lambda i: (i, 0)),
        out_shape=jax.ShapeDtypeStruct((tp, d), F32),
        compiler_params=_cparams(1),
    )(h, ya, yb, w)


def _out_bwd(dy, ya, yb, w, tm, after=()):
    tp, d = dy.shape
    na, nb = ya.shape[1], yb.shape[1]

    def body(dy_ref, ya_ref, yb_ref, w_ref, *rest):
        da_ref, db_ref, dw_ref, acc = rest[len(after):]
        i = pl.program_id(0)

        @pl.when(i == 0)
        def _():
            acc[...] = jnp.zeros_like(acc)

        dyb16 = dy_ref[...].astype(BF16)
        da_ref[...] = _dot_nt(dyb16, w_ref[0:na, :])
        db_ref[...] = _dot_nt(dyb16, w_ref[na:, :])
        acc[0:na, :] += _dot_tn(ya_ref[...].astype(BF16), dyb16)
        acc[na:, :] += _dot_tn(yb_ref[...].astype(BF16), dyb16)

        @pl.when(i == pl.num_programs(0) - 1)
        def _():
            dw_ref[...] = acc[...].astype(BF16)

    return pl.pallas_call(
        body, name="out_bwd",
        grid=(tp // tm,),
        in_specs=[pl.BlockSpec((tm, d), lambda i: (i, 0)), pl.BlockSpec((tm, na), lambda i: (i, 0)),
                  pl.BlockSpec((tm, nb), lambda i: (i, 0)), pl.BlockSpec((d, d), lambda i: (0, 0))]
        + [pl.BlockSpec(memory_space=pl.ANY)] * len(after),
        out_specs=[pl.BlockSpec((tm, na), lambda i: (i, 0)), pl.BlockSpec((tm, nb), lambda i: (i, 0)),
                   pl.BlockSpec((d, d), lambda i: (0, 0))],
        out_shape=[jax.ShapeDtypeStruct((tp, na), F32), jax.ShapeDtypeStruct((tp, nb), F32),
                   jax.ShapeDtypeStruct((d, d), BF16)],
        scratch_shapes=[pltpu.VMEM((d, d), F32)],
        compiler_params=_cparams(1),
    )(dy, ya, yb, w, *after)


def _loss_bwd(h, gfin, tgt, t_real, tm):
    tp, d = h.shape

    def body(h_ref, g_ref, t_ref, loss_ref, dh_ref, dg_ref):
        i = pl.program_id(0)

        @pl.when(i == 0)
        def _():
            loss_ref[...] = jnp.zeros_like(loss_ref)
            dg_ref[...] = jnp.zeros_like(dg_ref)

        rows = i * tm + lax.broadcasted_iota(jnp.int32, (tm, 1), 0)
        valid = (rows >= N_META) & (rows < t_real)
        rstd, xhat = _rms_stats(h_ref[...])
        g = g_ref[...]
        err = jnp.where(valid, xhat * g - t_ref[...], 0.0)
        e2 = jnp.sum(err * err, axis=1, keepdims=True)
        loss_ref[...] += (0.5 / d) * jnp.sum(e2, axis=0, keepdims=True)
        dy = err * (1.0 / d)
        dh, dg = _rms_bwd(dy, g, rstd, xhat)
        dh_ref[...] = dh
        dg_ref[...] += dg

    return pl.pallas_call(
        body, name="loss_bwd",
        grid=(tp // tm,),
        in_specs=[pl.BlockSpec((tm, d), lambda i: (i, 0)), pl.BlockSpec((1, d), lambda i: (0, 0)),
                  pl.BlockSpec((tm, d), lambda i: (i, 0))],
        out_specs=[pl.BlockSpec((1, 1), lambda i: (0, 0)), pl.BlockSpec((tm, d), lambda i: (i, 0)),
                   pl.BlockSpec((1, d), lambda i: (0, 0))],
        out_shape=[jax.ShapeDtypeStruct((1, 1), F32), jax.ShapeDtypeStruct((tp, d), F32),
                   jax.ShapeDtypeStruct((1, d), F32)],
        compiler_params=_cparams(1),
    )(h, gfin, tgt)


POOL_HALO = 16


def _pool_lane_consts(n_rows):
    lane = lax.broadcasted_iota(jnp.int32, (n_rows, D_POOL), 1)
    grp = lane // POOL_GROUP
    win = jnp.where(grp == 0, 2.0, jnp.where(grp == 1, 4.0, jnp.where(grp == 2, 8.0, 16.0)))
    return grp, win


def _pool_select(grp, s2, s4, s8, s16):
    return jnp.where(grp == 0, s2, jnp.where(grp == 1, s4, jnp.where(grp == 2, s8, s16)))


def _pool_mixed(x, row0, tr):
    n = tr + POOL_HALO
    s2 = x + pltpu.roll(x, 1, 0)
    s4 = s2 + pltpu.roll(s2, 2, 0)
    s8 = s4 + pltpu.roll(s4, 4, 0)
    s16 = s8 + pltpu.roll(s8, 8, 0)
    grp, win = _pool_lane_consts(n)
    rows = row0 - POOL_HALO + lax.broadcasted_iota(jnp.int32, (n, D_POOL), 0)
    cnt = jnp.minimum((rows + 1).astype(F32), win)
    pooled = _pool_select(grp, s2, s4, s8, s16) / jnp.maximum(cnt, 1.0)
    return (pooled - x)[POOL_HALO:, :]


def _pool_fwd(p, wbd, scale, tr):
    tp = p.shape[0]
    nt = tp // tr

    def body(p_ref, w_ref, s_ref, y_ref, usc):
        usc[0:POOL_HALO, :] = jnp.zeros((POOL_HALO, D_POOL), F32)
        usc[POOL_HALO:, :] = p_ref[...]

        def tile(r, carry):
            r0 = pl.multiple_of(r * tr, SUBLANE)
            x = usc[pl.ds(r0, tr + POOL_HALO), :]
            mixed = _pool_mixed(x, r0, tr)
            y_ref[pl.ds(r0, tr), :] = _dot(mixed.astype(BF16), w_ref[...]) * s_ref[...]
            return carry

        lax.fori_loop(0, nt, tile, 0)

    return pl.pallas_call(
        body, name="pool_fwd",
        grid=(1,),
        in_specs=[pl.BlockSpec((tp, D_POOL), lambda i: (0, 0)), pl.BlockSpec((D_POOL, D_POOL), lambda i: (0, 0)),
                  pl.BlockSpec((1, D_POOL), lambda i: (0, 0))],
        out_specs=pl.BlockSpec((tp, D_POOL), lambda i: (0, 0)),
        out_shape=jax.ShapeDtypeStruct((tp, D_POOL), F32),
        scratch_shapes=[pltpu.VMEM((tp + POOL_HALO, D_POOL), F32)],
        compiler_params=_cparams(1),
    )(p, wbd, scale)


def _pool_bwd(p, dya, wbd, scale, tr):
    tp = p.shape[0]
    nt = tp // tr

    def body(p_ref, dy_ref, w_ref, s_ref, du_ref, dw_ref, ds_ref, usc, gsc):
        usc[0:POOL_HALO, :] = jnp.zeros((POOL_HALO, D_POOL), F32)
        usc[POOL_HALO:, :] = p_ref[...]
        gsc[tp:, :] = jnp.zeros((POOL_HALO, D_POOL), F32)
        dw_ref[...] = jnp.zeros_like(dw_ref)
        ds_ref[...] = jnp.zeros_like(ds_ref)
        grp, win = _pool_lane_consts(tr)

        def tile1(r, carry):
            r0 = pl.multiple_of(r * tr, SUBLANE)
            x = usc[pl.ds(r0, tr + POOL_HALO), :]
            mixed = _pool_mixed(x, r0, tr).astype(BF16)
            dy = dy_ref[pl.ds(r0, tr), :]
            dys = (dy * s_ref[...]).astype(BF16)
            ypre = _dot(mixed, w_ref[...])
            ds_ref[...] += _colsum(dy * ypre)
            dw_ref[...] += _dot_tn(mixed, dys)
            dmx = _dot_nt(dys, w_ref[...])
            rows = r0 + lax.broadcasted_iota(jnp.int32, (tr, D_POOL), 0)
            cnt = jnp.minimum((rows + 1).astype(F32), win)
            gsc[pl.ds(r0, tr), :] = dmx / cnt
            return carry

        lax.fori_loop(0, nt, tile1, 0)
        n = tr + POOL_HALO
        grp2, win2 = _pool_lane_consts(n)

        def tile2(r, carry):
            r0 = pl.multiple_of(r * tr, SUBLANE)
            g = gsc[pl.ds(r0, n), :]
            s2 = g + pltpu.roll(g, n - 1, 0)
            s4 = s2 + pltpu.roll(s2, n - 2, 0)
            s8 = s4 + pltpu.roll(s4, n - 4, 0)
            s16 = s8 + pltpu.roll(s8, n - 8, 0)
            pooled_t = _pool_select(grp2, s2, s4, s8, s16)
            rows = r0 + lax.broadcasted_iota(jnp.int32, (n, D_POOL), 0)
            cnt = jnp.minimum((rows + 1).astype(F32), win2)
            du = pooled_t - g * cnt
            du_ref[pl.ds(r0, tr), :] = du[0:tr, :].astype(BF16)
            return carry

        lax.fori_loop(0, nt, tile2, 0)

    return pl.pallas_call(
        body, name="pool_bwd",
        grid=(1,),
        in_specs=[pl.BlockSpec((tp, D_POOL), lambda i: (0, 0)), pl.BlockSpec((tp, D_POOL), lambda i: (0, 0)),
                  pl.BlockSpec((D_POOL, D_POOL), lambda i: (0, 0)), pl.BlockSpec((1, D_POOL), lambda i: (0, 0))],
        out_specs=[pl.BlockSpec((tp, D_POOL), lambda i: (0, 0)), pl.BlockSpec((D_POOL, D_POOL), lambda i: (0, 0)),
                   pl.BlockSpec((1, D_POOL), lambda i: (0, 0))],
        out_shape=[jax.ShapeDtypeStruct((tp, D_POOL), BF16), jax.ShapeDtypeStruct((D_POOL, D_POOL), F32),
                   jax.ShapeDtypeStruct((1, D_POOL), F32)],
        scratch_shapes=[pltpu.VMEM((tp + POOL_HALO, D_POOL), F32), pltpu.VMEM((tp + POOL_HALO, D_POOL), F32)],
        compiler_params=_cparams(1),
    )(p, dya, wbd, scale)


def _hgrn_consts(ch):
    t = np.arange(ch)
    levels = []
    w = ch // 2
    while w >= 1:
        levels.append(w)
        w //= 2
    tril = t[None, :] <= t[:, None]
    to_end = t[None, :] > t[:, None]
    mats = [tril, to_end]
    masks = []
    for w in levels:
        pos = t % (2 * w)
        blk = t // (2 * w)
        upper = pos >= w
        mid = blk * 2 * w + w - 1
        mats.append(upper[:, None] & (t[None, :] > mid[:, None]) & (t[None, :] <= t[:, None]))
        mats.append((~upper)[:, None] & (t[None, :] > t[:, None]) & (t[None, :] <= mid[:, None]))
        masks.append(upper[:, None] & (~upper)[None, :] & (blk[:, None] == blk[None, :]))
    masks.append(tril)
    mst = np.concatenate(mats, axis=0).astype(np.float32)
    msk = np.stack(masks).astype(np.float32)
    return jnp.asarray(mst, BF16), jnp.asarray(msk, F32), len(levels)


def _split3(x):
    hi = x.astype(BF16)
    r1 = x - hi.astype(F32)
    mid = r1.astype(BF16)
    lo = (r1 - mid.astype(F32)).astype(BF16)
    return hi, mid, lo


def _hgrn_exponents(mst, logf):
    hi, mid, lo = _split3(logf)
    x = _dot(mst, jnp.concatenate([hi, mid, lo], axis=1))
    return x[:, 0:HEAD] + x[:, HEAD:2 * HEAD] + x[:, 2 * HEAD:3 * HEAD]


def _hgrn_gates(q_raw, z, lb):
    sz = _sigmoid(z)
    f = lb + (1.0 - lb) * sz
    q = q_raw * _sigmoid(q_raw)
    k = (1.0 - lb) * (1.0 - sz)
    return q, k, f, sz


def _hgrn_intra(q, k, x, msk_ref, n_lev, ch):
    a = msk_ref[n_lev] * 0.0
    eye = (lax.broadcasted_iota(jnp.int32, (ch, ch), 0) == lax.broadcasted_iota(jnp.int32, (ch, ch), 1))
    a = jnp.where(eye, jnp.sum(q * k, axis=1, keepdims=True), 0.0)
    ops = []
    for lv in range(n_lev):
        eq = jnp.exp(x[(2 + 2 * lv) * ch:(3 + 2 * lv) * ch, :])
        ek = jnp.exp(x[(3 + 2 * lv) * ch:(4 + 2 * lv) * ch, :])
        qd = q * eq
        kd = k * ek
        a = a + msk_ref[lv] * _dot_nt(qd.astype(BF16), kd.astype(BF16))
        ops.append((eq, ek, qd, kd))
    return a, ops


def _hgrn_fwd(p, lb_logits, gnorm, mst, msk, n_lev, tm):
    tp = p.shape[0]
    ch = HG_CHUNK
    nct = tm // ch
    nt = tp // tm
    nr = mst.shape[0]
    base = D_POOL // HEAD

    def body(q_ref, z_ref, v_ref, g_ref, lg_ref, gn_ref, mst_ref, msk_ref, y_ref, ss_ref, st_sc):
        @pl.when(pl.program_id(1) == 0)
        def _():
            st_sc[...] = jnp.zeros_like(st_sc)

        lb_all = _sigmoid(lg_ref[0:1, :] - lg_ref[1:2, :])

        def one_head(hh, c, r0):
            ls = slice(hh * HEAD, (hh + 1) * HEAD)
            lb = lb_all[:, ls]
            q, k, f, _ = _hgrn_gates(q_ref[pl.ds(r0, ch), ls], z_ref[pl.ds(r0, ch), ls], lb)
            v = v_ref[pl.ds(r0, ch), ls]
            x = _hgrn_exponents(mst_ref[...], jnp.log(f))
            st = st_sc[hh]
            ss_ref[hh, c] = st
            qe = q * jnp.exp(x[0:ch, :])
            a, _ = _hgrn_intra(q, k, x, msk_ref, n_lev, ch)
            v16 = v.astype(BF16)
            o = _dot_nt(qe.astype(BF16), st.astype(BF16)) + _dot(a.astype(BF16), v16)
            kl = k * jnp.exp(x[ch:2 * ch, :])
            st_sc[hh] = st * jnp.exp(x[ch - 1:ch, :]) + _dot_tn(v16, kl.astype(BF16))
            rstd = lax.rsqrt(jnp.mean(o * o, axis=-1, keepdims=True) + EPS)
            g_raw = g_ref[pl.ds(r0, ch), ls]
            y_ref[pl.ds(r0, ch), ls] = o * rstd * gn_ref[...] * (g_raw * _sigmoid(g_raw))

        def chunk(c, carry):
            r0 = pl.multiple_of(c * ch, ch)
            for hh in range(HG_HEADS_PER_STEP):
                one_head(hh, c, r0)
            return carry

        lax.fori_loop(0, nct, chunk, 0)

    hp = HG_HEADS_PER_STEP
    wide = hp * HEAD

    def pspec(seg):
        return pl.BlockSpec((tm, wide), lambda h, i: (i, (base + seg * HG_HEADS) // hp + h))

    return pl.pallas_call(
        body, name="hgrn_fwd",
        grid=(HG_HEADS // hp, nt),
        in_specs=[pspec(0), pspec(1), pspec(2), pspec(3),
                  pl.BlockSpec((2, wide), lambda h, i: (0, h)),
                  pl.BlockSpec((1, HEAD), lambda h, i: (0, 0)),
                  pl.BlockSpec((nr, ch), lambda h, i: (0, 0)),
                  pl.BlockSpec((n_lev + 1, ch, ch), lambda h, i: (0, 0, 0))],
        out_specs=[pl.BlockSpec((tm, wide), lambda h, i: (i, h)),
                   pl.BlockSpec((hp, nct, HEAD, HEAD), lambda h, i: (h, i, 0, 0))],
        out_shape=[jax.ShapeDtypeStruct((tp, D_HGRN), F32),
                   jax.ShapeDtypeStruct((HG_HEADS, tp // ch, HEAD, HEAD), F32)],
        scratch_shapes=[pltpu.VMEM((hp, HEAD, HEAD), F32)],
        compiler_params=_cparams(2),
    )(p, p, p, p, lb_logits, gnorm, mst, msk)


def _hgrn_bwd(p, dyb, states, lb_logits, gnorm, mst, msk, n_lev, tm):
    tp = p.shape[0]
    ch = HG_CHUNK
    nct = tm // ch
    nt = tp // tm
    nr = mst.shape[0]
    base = D_POOL // HEAD

    def body(q_ref, z_ref, v_ref, g_ref, dy_ref, ss_ref, lg_ref, gn_ref, mst_ref, msk_ref,
             dq_ref, dz_ref, dv_ref, dg_ref, dlg_ref, dgn_ref, dst_sc, dlb_sc, stk):
        ti = pl.program_id(1)

        @pl.when(ti == 0)
        def _():
            dst_sc[...] = jnp.zeros_like(dst_sc)
            dlb_sc[...] = jnp.zeros_like(dlb_sc)
            dgn_ref[...] = jnp.zeros_like(dgn_ref)

        lb_all = _sigmoid(lg_ref[0:1, :] - lg_ref[1:2, :])
        gn = gn_ref[...]

        def one_head(hh, c, r0):
            ls = slice(hh * HEAD, (hh + 1) * HEAD)
            lb = lb_all[:, ls]
            q_raw = q_ref[pl.ds(r0, ch), ls]
            z = z_ref[pl.ds(r0, ch), ls]
            q, k, f, sz = _hgrn_gates(q_raw, z, lb)
            v = v_ref[pl.ds(r0, ch), ls]
            x = _hgrn_exponents(mst_ref[...], jnp.log(f))
            st = ss_ref[hh, c]
            eb = jnp.exp(x[0:ch, :])
            ef = jnp.exp(x[ch:2 * ch, :])
            elast = jnp.exp(x[ch - 1:ch, :])
            qe = q * eb
            kl = k * ef
            a, ops = _hgrn_intra(q, k, x, msk_ref, n_lev, ch)
            v16 = v.astype(BF16)
            st16 = st.astype(BF16)
            qe16 = qe.astype(BF16)
            kl16 = kl.astype(BF16)
            a16 = a.astype(BF16)
            o = _dot_nt(qe16, st16) + _dot(a16, v16)
            g_raw = g_ref[pl.ds(r0, ch), ls]
            sg = _sigmoid(g_raw)
            rstd = lax.rsqrt(jnp.mean(o * o, axis=-1, keepdims=True) + EPS)
            oh = o * rstd
            dy = dy_ref[pl.ds(r0, ch), ls]
            dg_ref[pl.ds(r0, ch), ls] = (dy * oh * gn * (sg * (1.0 + g_raw * (1.0 - sg)))).astype(BF16)
            don = dy * (g_raw * sg)
            dgn_ref[hh] += _colsum(don * oh)
            doh = don * gn
            do = rstd * (doh - oh * jnp.mean(doh * oh, axis=-1, keepdims=True))
            do16 = do.astype(BF16)
            dst = dst_sc[hh]
            dst16 = dst.astype(BF16)
            dv = _dot_tn(a16, do16) + _dot_nt(kl16, dst16)
            da = msk_ref[n_lev] * _dot_nt(do16, v16)
            dqe = _dot(do16, st16)
            dkl = _dot(v16, dst16)
            dst_sc[hh] = dst * elast + _dot_tn(do16, qe16)
            db_last = _colsum(dst * st) * elast
            dad = jnp.sum(do * v, axis=1, keepdims=True)
            dq = dad * k + dqe * eb
            dk = dad * q + dkl * ef
            stk[hh, 0:ch, :] = dqe * qe
            stk[hh, ch:2 * ch, :] = dkl * kl
            for lv in range(n_lev):
                eq, ek, qd, kd = ops[lv]
                gl = (msk_ref[lv] * da).astype(BF16)
                dqd = _dot(gl, kd.astype(BF16))
                dkd = _dot_tn(gl, qd.astype(BF16))
                dq = dq + dqd * eq
                dk = dk + dkd * ek
                stk[hh, (2 + 2 * lv) * ch:(3 + 2 * lv) * ch, :] = dqd * qd
                stk[hh, (3 + 2 * lv) * ch:(4 + 2 * lv) * ch, :] = dkd * kd
            sk = stk[hh]
            hi = sk.astype(BF16)
            lo = (sk - hi.astype(F32)).astype(BF16)
            dl2 = _dot_tn(mst_ref[...], jnp.concatenate([hi, lo], axis=1))
            dlogf = dl2[:, 0:HEAD] + dl2[:, HEAD:2 * HEAD] + db_last
            sq = _sigmoid(q_raw)
            dq_ref[pl.ds(r0, ch), ls] = (dq * (sq * (1.0 + q_raw * (1.0 - sq)))).astype(BF16)
            dfk = dlogf / f - dk
            dz_ref[pl.ds(r0, ch), ls] = (dfk * (1.0 - lb) * sz * (1.0 - sz)).astype(BF16)
            dlb_sc[:, ls] += _colsum(dfk * (1.0 - sz))
            dv_ref[pl.ds(r0, ch), ls] = dv.astype(BF16)

        def chunk(cc, carry):
            c = nct - 1 - cc
            r0 = pl.multiple_of(c * ch, ch)
            for hh in range(HG_HEADS_PER_STEP):
                one_head(hh, c, r0)
            return carry

        lax.fori_loop(0, nct, chunk, 0)

        @pl.when(ti == nt - 1)
        def _():
            dl0 = dlb_sc[...] * lb_all * (1.0 - lb_all)
            dlg_ref[0:1, :] = dl0
            dlg_ref[1:2, :] = -dl0

    hp = HG_HEADS_PER_STEP
    wide = hp * HEAD

    def pspec(seg):
        return pl.BlockSpec((tm, wide), lambda h, i: (nt - 1 - i, (base + seg * HG_HEADS) // hp + h))

    ospec = pl.BlockSpec((tm, wide), lambda h, i: (nt - 1 - i, h))
    return pl.pallas_call(
        body, name="hgrn_bwd",
        grid=(HG_HEADS // hp, nt),
        in_specs=[pspec(0), pspec(1), pspec(2), pspec(3), ospec,
                  pl.BlockSpec((hp, nct, HEAD, HEAD), lambda h, i: (h, nt - 1 - i, 0, 0)),
                  pl.BlockSpec((2, wide), lambda h, i: (0, h)),
                  pl.BlockSpec((1, HEAD), lambda h, i: (0, 0)),
                  pl.BlockSpec((nr, ch), lambda h, i: (0, 0)),
                  pl.BlockSpec((n_lev + 1, ch, ch), lambda h, i: (0, 0, 0))],
        out_specs=[ospec, ospec, ospec, ospec,
                   pl.BlockSpec((2, wide), lambda h, i: (0, h)),
                   pl.BlockSpec((hp, 1, HEAD), lambda h, i: (h, 0, 0))],
        out_shape=[jax.ShapeDtypeStruct((tp, D_HGRN), BF16)] * 4
        + [jax.ShapeDtypeStruct((2, D_HGRN), F32), jax.ShapeDtypeStruct((HG_HEADS, 1, HEAD), F32)],
        scratch_shapes=[pltpu.VMEM((hp, HEAD, HEAD), F32), pltpu.VMEM((1, wide), F32),
                        pltpu.VMEM((hp, nr, HEAD), F32)],
        compiler_params=_cparams(2),
    )(p, p, p, p, dyb, states, lb_logits, gnorm, mst, msk)


def _conv_taps(x, w_ref, tr, halo, width):
    acc = None
    for j in range(width):
        sh = width - 1 - j
        xs = x if sh == 0 else pltpu.roll(x, sh, 0)
        term = xs[halo:, :] * w_ref[j:j + 1, :]
        acc = term if acc is None else acc + term
    return acc


def _conv_taps_t(y, w_ref, tr, halo, width):
    n = tr + halo
    acc = None
    for j in range(width):
        sh = width - 1 - j
        ys = y if sh == 0 else pltpu.roll(y, n - sh, 0)
        term = ys[0:tr, :] * w_ref[j:j + 1, :]
        acc = term if acc is None else acc + term
    return acc


def _ln_stats(cv):
    mu = jnp.mean(cv, axis=-1, keepdims=True)
    xc = cv - mu
    rstd = lax.rsqrt(jnp.mean(xc * xc, axis=-1, keepdims=True) + EPS)
    return rstd, xc * rstd


def _convmod_fwd(p, w, bias, ln_g, ln_b, tr):
    tp = p.shape[0]
    nt = tp // tr
    nb = D_CONV // HEAD

    def body(a_ref, b_ref, w_ref, bi_ref, g_ref, be_ref, y_ref, usc):
        usc[0:CONV_HALO, :] = jnp.zeros((CONV_HALO, HEAD), F32)
        usc[CONV_HALO:, :] = a_ref[...] * _sigmoid(b_ref[...])

        def tile(r, carry):
            r0 = pl.multiple_of(r * tr, SUBLANE)
            x = usc[pl.ds(r0, tr + CONV_HALO), :]
            cv = _conv_taps(x, w_ref, tr, CONV_HALO, CONV_WIDTH) + bi_ref[...]
            _, xh = _ln_stats(cv)
            un = xh * g_ref[...] + be_ref[...]
            y_ref[pl.ds(r0, tr), :] = un * _sigmoid(un)
            return carry

        lax.fori_loop(0, nt, tile, 0)

    vec = lambda: pl.BlockSpec((1, HEAD), lambda j: (0, j))
    return pl.pallas_call(
        body, name="convmod_fwd",
        grid=(nb,),
        in_specs=[pl.BlockSpec((tp, HEAD), lambda j: (0, j)), pl.BlockSpec((tp, HEAD), lambda j: (0, nb + j)),
                  pl.BlockSpec((CONV_HALO, HEAD), lambda j: (0, j)), vec(), vec(), vec()],
        out_specs=pl.BlockSpec((tp, HEAD), lambda j: (0, j)),
        out_shape=jax.ShapeDtypeStruct((tp, D_CONV), F32),
        scratch_shapes=[pltpu.VMEM((tp + CONV_HALO, HEAD), F32)],
        compiler_params=_cparams(1),
    )(p, p, w, bias, ln_g, ln_b)


def _convmod_bwd(p, dyc, w, bias, ln_g, ln_b, tr):
    tp = p.shape[0]
    nt = tp // tr
    nb = D_CONV // HEAD

    def body(a_ref, b_ref, dy_ref, w_ref, bi_ref, g_ref, be_ref, da_ref, db_ref, dw_ref, dv_ref, usc, dsc):
        usc[0:CONV_HALO, :] = jnp.zeros((CONV_HALO, HEAD), F32)
        usc[CONV_HALO:, :] = a_ref[...] * _sigmoid(b_ref[...])
        dsc[tp:, :] = jnp.zeros((CONV_HALO, HEAD), F32)
        dw_ref[...] = jnp.zeros_like(dw_ref)
        dv_ref[...] = jnp.zeros_like(dv_ref)

        def tile1(r, carry):
            r0 = pl.multiple_of(r * tr, SUBLANE)
            x = usc[pl.ds(r0, tr + CONV_HALO), :]
            cv = _conv_taps(x, w_ref, tr, CONV_HALO, CONV_WIDTH) + bi_ref[...]
            rstd, xh = _ln_stats(cv)
            un = xh * g_ref[...] + be_ref[...]
            sg = _sigmoid(un)
            dun = dy_ref[pl.ds(r0, tr), :] * (sg * (1.0 + un * (1.0 - sg)))
            dv_ref[0, 1:2, :] += _colsum(dun * xh)
            dv_ref[0, 2:3, :] += _colsum(dun)
            dxh = dun * g_ref[...]
            dcv = rstd * (dxh - jnp.mean(dxh, axis=-1, keepdims=True)
                          - xh * jnp.mean(dxh * xh, axis=-1, keepdims=True))
            dv_ref[0, 0:1, :] += _colsum(dcv)
            for j in range(CONV_WIDTH):
                sh = CONV_WIDTH - 1 - j
                xs = x if sh == 0 else pltpu.roll(x, sh, 0)
                dw_ref[0, j:j + 1, :] += _colsum(dcv * xs[CONV_HALO:, :])
            dsc[pl.ds(r0, tr), :] = dcv
            return carry

        lax.fori_loop(0, nt, tile1, 0)

        def tile2(r, carry):
            r0 = pl.multiple_of(r * tr, SUBLANE)
            y = dsc[pl.ds(r0, tr + CONV_HALO), :]
            du = _conv_taps_t(y, w_ref, tr, CONV_HALO, CONV_WIDTH)
            a = a_ref[pl.ds(r0, tr), :]
            sb = _sigmoid(b_ref[pl.ds(r0, tr), :])
            da_ref[pl.ds(r0, tr), :] = (du * sb).astype(BF16)
            db_ref[pl.ds(r0, tr), :] = (du * a * sb * (1.0 - sb)).astype(BF16)
            return carry

        lax.fori_loop(0, nt, tile2, 0)

    vec = lambda: pl.BlockSpec((1, HEAD), lambda j: (0, j))
    col = lambda: pl.BlockSpec((tp, HEAD), lambda j: (0, j))
    return pl.pallas_call(
        body, name="convmod_bwd",
        grid=(nb,),
        in_specs=[col(), pl.BlockSpec((tp, HEAD), lambda j: (0, nb + j)), col(),
                  pl.BlockSpec((CONV_HALO, HEAD), lambda j: (0, j)), vec(), vec(), vec()],
        out_specs=[col(), col(), pl.BlockSpec((1, CONV_HALO, HEAD), lambda j: (j, 0, 0)),
                   pl.BlockSpec((1, SUBLANE, HEAD), lambda j: (j, 0, 0))],
        out_shape=[jax.ShapeDtypeStruct((tp, D_CONV), BF16), jax.ShapeDtypeStruct((tp, D_CONV), BF16),
                   jax.ShapeDtypeStruct((nb, CONV_HALO, HEAD), F32), jax.ShapeDtypeStruct((nb, SUBLANE, HEAD), F32)],
        scratch_shapes=[pltpu.VMEM((tp + CONV_HALO, HEAD), F32), pltpu.VMEM((tp + CONV_HALO, HEAD), F32)],
        compiler_params=_cparams(1),
    )(p, p, dyc, w, bias, ln_g, ln_b)


def _log1p_small(y):
    return jnp.where(y < 1e-4, y * (1.0 - 0.5 * y), jnp.log(1.0 + y))


def _softplus(x):
    return jnp.maximum(x, 0.0) + _log1p_small(jnp.exp(-jnp.abs(x)))


def _expm1(x):
    return jnp.where(jnp.abs(x) < 1e-2, x * (1.0 + 0.5 * x * (1.0 + x * (1.0 / 3.0))), jnp.exp(x) - 1.0)


def _gelu_parts(x):
    c = 0.7978845608028654
    inner = c * (x + 0.044715 * x * x * x)
    th = jnp.tanh(inner)
    gelu = 0.5 * x * (1.0 + th)
    dgelu = 0.5 * (1.0 + th) + 0.5 * x * (1.0 - th * th) * c * (1.0 + 3.0 * 0.044715 * x * x)
    return gelu, dgelu


def _lru_gates(x_all, tp, cw_ref, cb_ref, wa_ref, ba_ref, wx_ref, bx_ref, lam_ref):
    u = _conv_taps(x_all, cw_ref, tp, LRU_HALO, LRU_CONV) + cb_ref[...]
    u16 = u.astype(BF16)
    r = _sigmoid(_dot(u16, wa_ref[0]) + ba_ref[...])
    i = _sigmoid(_dot(u16, wx_ref[0]) + bx_ref[...])
    sp = _softplus(-lam_ref[...])
    la = -LRU_C * r * sp
    a = jnp.exp(la)
    mult = jnp.sqrt(-_expm1(2.0 * la))
    return u, r, i, a, mult, sp


def _lru_specs(tp, nb):
    col = lambda k: pl.BlockSpec((tp, HEAD), functools.partial(lambda j, k: (0, k * nb + j), k=k))
    vec = lambda: pl.BlockSpec((1, HEAD), lambda j: (0, j))
    mat = lambda: pl.BlockSpec((1, HEAD, HEAD), lambda j: (j, 0, 0))
    return col, vec, mat


def _lru_fwd(p, cw, cb, wa, ba, wx, bx, lam):
    tp = p.shape[0]
    nb = D_LRU // HEAD
    ng = tp // SUBLANE

    def body(x_ref, gt_ref, cw_ref, cb_ref, wa_ref, ba_ref, wx_ref, bx_ref, lam_ref, y_ref, hs_ref,
             xsc, asc, bsc):
        xsc[0:LRU_HALO, :] = jnp.zeros((LRU_HALO, HEAD), F32)
        xsc[LRU_HALO:, :] = x_ref[...]
        u, r, i, a, mult, _ = _lru_gates(xsc[...], tp, cw_ref, cb_ref, wa_ref, ba_ref, wx_ref, bx_ref, lam_ref)
        rows = lax.broadcasted_iota(jnp.int32, (tp, HEAD), 0)
        b = jnp.where(rows == 0, 1.0, mult) * (i * u)
        sub = rows % SUBLANE
        for k in (1, 2, 4):
            m = sub >= k
            b = jnp.where(m, a * pltpu.roll(b, k, 0) + b, b)
            a = jnp.where(m, a * pltpu.roll(a, k, 0), a)
        asc[...] = a
        bsc[...] = b

        def grp(g, carry):
            r0 = pl.multiple_of(g * SUBLANE, SUBLANE)
            h = bsc[pl.ds(r0, SUBLANE), :] + asc[pl.ds(r0, SUBLANE), :] * carry
            hs_ref[pl.ds(r0, SUBLANE), :] = h
            return jnp.broadcast_to(h[SUBLANE - 1:SUBLANE, :], (SUBLANE, HEAD))

        lax.fori_loop(0, ng, grp, jnp.zeros((SUBLANE, HEAD), F32))
        gelu, _ = _gelu_parts(gt_ref[...])
        y_ref[...] = gelu * hs_ref[...]

    col, vec, mat = _lru_specs(tp, nb)
    return pl.pallas_call(
        body, name="lru_fwd",
        grid=(nb,),
        in_specs=[col(2), col(3), pl.BlockSpec((LRU_CONV, HEAD), lambda j: (0, j)), vec(), mat(), vec(), mat(),
                  vec(), vec()],
        out_specs=[pl.BlockSpec((tp, HEAD), lambda j: (0, j)), pl.BlockSpec((tp, HEAD), lambda j: (0, j))],
        out_shape=[jax.ShapeDtypeStruct((tp, D_LRU), F32), jax.ShapeDtypeStruct((tp, D_LRU), F32)],
        scratch_shapes=[pltpu.VMEM((tp + LRU_HALO, HEAD), F32), pltpu.VMEM((tp, HEAD), F32),
                        pltpu.VMEM((tp, HEAD), F32)],
        compiler_params=_cparams(1),
    )(p, p, cw, cb, wa, ba, wx, bx, lam)


def _lru_bwd(p, hs, dyd, cw, cb, wa, ba, wx, bx, lam):
    tp = p.shape[0]
    nb = D_LRU // HEAD
    ng = tp // SUBLANE

    def body(x_ref, gt_ref, hs_ref, dy_ref, cw_ref, cb_ref, wa_ref, ba_ref, wx_ref, bx_ref, lam_ref,
             dx_ref, dgt_ref, dwa_ref, dwx_ref, dv_ref, xsc, asc, bsc, gsc, dusc):
        xsc[0:LRU_HALO, :] = jnp.zeros((LRU_HALO, HEAD), F32)
        xsc[LRU_HALO:, :] = x_ref[...]
        x_all = xsc[...]
        u, r, i, a, mult, sp = _lru_gates(x_all, tp, cw_ref, cb_ref, wa_ref, ba_ref, wx_ref, bx_ref, lam_ref)
        rows = lax.broadcasted_iota(jnp.int32, (tp, HEAD), 0)
        hs = hs_ref[...]
        dy = dy_ref[...]
        gelu, dgelu = _gelu_parts(gt_ref[...])
        dgt_ref[...] = (dy * hs * dgelu).astype(BF16)
        bb = dy * gelu
        aa = jnp.where(rows == tp - 1, 0.0, pltpu.roll(a, tp - 1, 0))
        sub = rows % SUBLANE
        for k in (1, 2, 4):
            m = sub < SUBLANE - k
            bb = jnp.where(m, aa * pltpu.roll(bb, tp - k, 0) + bb, bb)
            aa = jnp.where(m, aa * pltpu.roll(aa, tp - k, 0), aa)
        asc[...] = aa
        bsc[...] = bb

        def grp(gi, carry):
            g = ng - 1 - gi
            r0 = pl.multiple_of(g * SUBLANE, SUBLANE)
            gg = bsc[pl.ds(r0, SUBLANE), :] + asc[pl.ds(r0, SUBLANE), :] * carry
            gsc[pl.ds(r0, SUBLANE), :] = gg
            return jnp.broadcast_to(gg[0:1, :], (SUBLANE, HEAD))

        lax.fori_loop(0, ng, grp, jnp.zeros((SUBLANE, HEAD), F32))
        g = gsc[...]
        first = rows == 0
        hprev = jnp.where(first, 0.0, pltpu.roll(hs, 1, 0))
        iu = i * u
        d_iu = g * jnp.where(first, 1.0, mult)
        dmult_term = jnp.where(first, 0.0, g * iu * (-(a * a) / mult))
        dla = g * hprev * a + dmult_term
        dr = dla * (-LRU_C) * sp
        dv_ref[0, 7:8, :] = _colsum(dla * (LRU_C * r) * _sigmoid(-lam_ref[...]))
        dpr = dr * r * (1.0 - r)
        dpi = d_iu * u * i * (1.0 - i)
        dv_ref[0, 5:6, :] = _colsum(dpr)
        dv_ref[0, 6:7, :] = _colsum(dpi)
        u16 = u.astype(BF16)
        dpr16 = dpr.astype(BF16)
        dpi16 = dpi.astype(BF16)
        dwa_ref[0] = _dot_tn(u16, dpr16)
        dwx_ref[0] = _dot_tn(u16, dpi16)
        du = d_iu * i + _dot_nt(dpr16, wa_ref[0]) + _dot_nt(dpi16, wx_ref[0])
        dv_ref[0, 4:5, :] = _colsum(du)
        for j in range(LRU_CONV):
            sh = LRU_CONV - 1 - j
            xs = x_all if sh == 0 else pltpu.roll(x_all, sh, 0)
            dv_ref[0, j:j + 1, :] = _colsum(du * xs[LRU_HALO:, :])
        dusc[0:tp, :] = du
        dusc[tp:, :] = jnp.zeros((LRU_HALO, HEAD), F32)
        dx_ref[...] = _conv_taps_t(dusc[...], cw_ref, tp, LRU_HALO, LRU_CONV).astype(BF16)

    col, vec, mat = _lru_specs(tp, nb)
    ocol = lambda: pl.BlockSpec((tp, HEAD), lambda j: (0, j))
    return pl.pallas_call(
        body, name="lru_bwd",
        grid=(nb,),
        in_specs=[col(2), col(3), ocol(), ocol(), pl.BlockSpec((LRU_CONV, HEAD), lambda j: (0, j)), vec(), mat(),
                  vec(), mat(), vec(), vec()],
        out_specs=[ocol(), ocol(), mat(), mat(), pl.BlockSpec((1, SUBLANE, HEAD), lambda j: (j, 0, 0))],
        out_shape=[jax.ShapeDtypeStruct((tp, D_LRU), BF16), jax.ShapeDtypeStruct((tp, D_LRU), BF16),
                   jax.ShapeDtypeStruct((nb, HEAD, HEAD), F32), jax.ShapeDtypeStruct((nb, HEAD, HEAD), F32),
                   jax.ShapeDtypeStruct((nb, SUBLANE, HEAD), F32)],
        scratch_shapes=[pltpu.VMEM((tp + LRU_HALO, HEAD), F32), pltpu.VMEM((tp, HEAD), F32),
                        pltpu.VMEM((tp, HEAD), F32), pltpu.VMEM((tp, HEAD), F32),
                        pltpu.VMEM((tp + LRU_HALO, HEAD), F32)],
        compiler_params=_cparams(1),
    )(p, p, hs, dyd, cw, cb, wa, ba, wx, bx, lam)


def _mesh_pos():
    return lax.axis_index("x"), lax.axis_index("y"), lax.axis_index("c")


def _other_chips(x, y):
    return [(1 - x, y), (x, 1 - y), (1 - x, 1 - y)]


ANY = pl.BlockSpec(memory_space=pl.ANY)


def _allgather_shards(arrs, split):
    n = len(arrs)

    def body(*refs):
        ins, outs = refs[:n], refs[n:2 * n]
        send1, recv1, send2, recv2, send3, recv3 = refs[2 * n:]
        x, y, c = _mesh_pos()
        chip = 2 * x + y
        sibling = (x, y, 1 - c)
        others = _other_chips(x, y)

        def rows(k, cc):
            half = arrs[k].shape[0] // 2
            return pl.ds(cc * half, half)

        def remote(src, dst, ssem, rsem, dev):
            return pltpu.make_async_remote_copy(src_ref=src, dst_ref=dst, send_sem=ssem, recv_sem=rsem,
                                                device_id=dev, device_id_type=MESH)

        started = [remote(ins[k], outs[k].at[chip], send3.at[k], recv3.at[k], sibling) for k in range(n)]
        for cp in started:
            cp.start()
        for k in range(n):
            for j, (ox, oy) in enumerate(others):
                if split[k]:
                    src, dst = ins[k].at[rows(k, c)], outs[k].at[chip, rows(k, c)]
                else:
                    src, dst = ins[k], outs[k].at[chip]
                cp = remote(src, dst, send1.at[3 * k + j], recv1.at[3 * k + j], (ox, oy, c))
                cp.start()
                started.append(cp)
        for j, (ox, oy) in enumerate(others):
            ochip = 2 * ox + oy
            for k in range(n):
                if split[k]:
                    blk = outs[k].at[ochip, rows(k, c)]
                    remote(blk, blk, send1.at[3 * k + j], recv1.at[3 * k + j], sibling).wait_recv()
                    cp = remote(blk, blk, send2.at[3 * k + j], recv2.at[3 * k + j], sibling)
                    cp.start()
                    started.append(cp)
                else:
                    blk = outs[k].at[ochip]
                    remote(blk, blk, send1.at[3 * k + j], recv1.at[3 * k + j], sibling).wait_recv()
        for j, (ox, oy) in enumerate(others):
            ochip = 2 * ox + oy
            for k in range(n):
                if split[k]:
                    blk = outs[k].at[ochip, rows(k, 1 - c)]
                    remote(blk, blk, send2.at[3 * k + j], recv2.at[3 * k + j], sibling).wait_recv()
        for k in range(n):
            blk = outs[k].at[chip]
            remote(blk, blk, send3.at[k], recv3.at[k], sibling).wait_recv()
        for cp in started:
            cp.wait_send()

    return pl.pallas_call(
        body, name="allgather_shards",
        in_specs=[ANY] * n, out_specs=[ANY] * n,
        out_shape=[jax.ShapeDtypeStruct((N_SHARD,) + a.shape, a.dtype) for a in arrs],
        scratch_shapes=[pltpu.SemaphoreType.DMA((3 * n,)), pltpu.SemaphoreType.DMA((3 * n,)),
                        pltpu.SemaphoreType.DMA((3 * n,)), pltpu.SemaphoreType.DMA((3 * n,)),
                        pltpu.SemaphoreType.DMA((n,)), pltpu.SemaphoreType.DMA((n,))],
    )(*arrs)


HBM = pl.BlockSpec(memory_space=pltpu.HBM)
SEM = pl.BlockSpec(memory_space=pltpu.SEMAPHORE)
DATAFLOW = pltpu.SideEffectType.DATAFLOW_SIDE_EFFECTING
N_PEERS = 4


def _in_hbm(a):
    return pltpu.with_memory_space_constraint(a, pltpu.HBM)


def _gather_peers(x, y, c):
    return [((ox, oy, c), 2 * ox + oy) for ox, oy in _other_chips(x, y)] + [((x, y, 1 - c), 2 * x + y)]


def _gather_start(arrs):
    n = len(arrs)

    def body(*refs):
        ins, lands = refs[:n], refs[n:2 * n]
        ssem, rsem = refs[2 * n:2 * n + 2]
        token = refs[-1]
        x, y, c = _mesh_pos()
        chip = 2 * x + y
        for k in range(n):
            for j, (dev, _) in enumerate(_gather_peers(x, y, c)):
                pltpu.make_async_remote_copy(
                    src_ref=ins[k], dst_ref=lands[k].at[chip], send_sem=ssem.at[N_PEERS * k + j],
                    recv_sem=rsem.at[N_PEERS * k + j], device_id=dev, device_id_type=MESH).start()
        token[...] = jnp.zeros_like(token)

    lands = [_in_hbm(lax.empty((N_SHARD,) + a.shape, a.dtype)) for a in arrs]
    out = pl.pallas_call(
        body, name="gather_start",
        in_specs=[HBM] * (2 * n),
        out_specs=[SEM, SEM] + [HBM] * (2 * n) + [pl.BlockSpec(memory_space=pltpu.VMEM)],
        out_shape=[pltpu.SemaphoreType.DMA((N_PEERS * n,)), pltpu.SemaphoreType.DMA((N_PEERS * n,))]
        + [pltpu.HBM(a.shape, a.dtype) for a in arrs]
        + [pltpu.HBM((N_SHARD,) + a.shape, a.dtype) for a in arrs]
        + [jax.ShapeDtypeStruct((SUBLANE, LANE), F32)],
        input_output_aliases={k: 2 + k for k in range(2 * n)},
        compiler_params=pltpu.CompilerParams(has_side_effects=DATAFLOW),
    )(*[_in_hbm(a) for a in arrs], *lands)
    return out[0], out[1], list(out[2:2 + n]), list(out[2 + n:2 + 2 * n]), out[-1]


def _gather_wait(ssem, rsem, srcs, lands, ks, after):
    n = len(ks)

    def body(*refs):
        ins, lnd = refs[:n], refs[n:2 * n]
        ssem_ref, rsem_ref = refs[2 * n:2 * n + 2]
        x, y, c = _mesh_pos()
        for i, k in enumerate(ks):
            for j, (dev, pchip) in enumerate(_gather_peers(x, y, c)):
                cp = pltpu.make_async_remote_copy(
                    src_ref=ins[i], dst_ref=lnd[i].at[pchip], send_sem=ssem_ref.at[N_PEERS * k + j],
                    recv_sem=rsem_ref.at[N_PEERS * k + j], device_id=dev, device_id_type=MESH)
                cp.wait_send()
                cp.wait_recv()

    out = pl.pallas_call(
        body, name="gather_wait",
        in_specs=[HBM] * (2 * n) + [SEM, SEM] + [ANY] * len(after),
        out_specs=[HBM] * (2 * n),
        out_shape=[pltpu.HBM(a.shape, a.dtype) for a in srcs] + [pltpu.HBM(a.shape, a.dtype) for a in lands],
        input_output_aliases={k: k for k in range(2 * n)},
        compiler_params=pltpu.CompilerParams(has_side_effects=DATAFLOW),
    )(*srcs, *lands, ssem, rsem, *after)
    return list(out[n:])


def _scatter_start(arrs, slots):
    n = len(arrs)

    def body(*refs):
        ins, lands = refs[:n], refs[n:2 * n]
        ssem, rsem = refs[2 * n:2 * n + 2]
        token = refs[-1]
        x, y, c = _mesh_pos()
        chip = 2 * x + y
        for k in range(n):
            for j, (ox, oy) in enumerate(_other_chips(x, y)):
                pltpu.make_async_remote_copy(
                    src_ref=ins[k].at[2 * ox + oy], dst_ref=lands[k].at[chip], send_sem=ssem.at[3 * k + j],
                    recv_sem=rsem.at[3 * k + j], device_id=(ox, oy, c), device_id_type=MESH).start()
        token[...] = jnp.zeros_like(token)

    out = pl.pallas_call(
        body, name="scatter_start",
        in_specs=[HBM] * (2 * n),
        out_specs=[SEM, SEM] + [HBM] * (2 * n) + [pl.BlockSpec(memory_space=pltpu.VMEM)],
        out_shape=[pltpu.SemaphoreType.DMA((3 * n,)), pltpu.SemaphoreType.DMA((3 * n,))]
        + [pltpu.HBM(a.shape, a.dtype) for a in arrs] + [pltpu.HBM(a.shape, a.dtype) for a in slots]
        + [jax.ShapeDtypeStruct((SUBLANE, LANE), F32)],
        input_output_aliases={k: 2 + k for k in range(2 * n)},
        compiler_params=pltpu.CompilerParams(has_side_effects=DATAFLOW),
    )(*[_in_hbm(a) for a in arrs], *[_in_hbm(a) for a in slots])
    return out[0], out[1], list(out[2:2 + n]), list(out[2 + n:2 + 2 * n]), out[-1]


def _scatter_wait(ssem, rsem, arrs, slots, after):
    n = len(arrs)

    def body(*refs):
        ins, lnd = refs[:n], refs[n:2 * n]
        ssem_ref, rsem_ref = refs[2 * n:2 * n + 2]
        x, y, c = _mesh_pos()
        for k in range(n):
            for j, (ox, oy) in enumerate(_other_chips(x, y)):
                ochip = 2 * ox + oy
                cp = pltpu.make_async_remote_copy(
                    src_ref=ins[k].at[ochip], dst_ref=lnd[k].at[ochip], send_sem=ssem_ref.at[3 * k + j],
                    recv_sem=rsem_ref.at[3 * k + j], device_id=(ox, oy, c), device_id_type=MESH)
                cp.wait_send()
                cp.wait_recv()

    out = pl.pallas_call(
        body, name="scatter_wait",
        in_specs=[HBM] * (2 * n) + [SEM, SEM] + [ANY] * len(after),
        out_specs=[HBM] * (2 * n),
        out_shape=[pltpu.HBM(a.shape, a.dtype) for a in arrs] + [pltpu.HBM(a.shape, a.dtype) for a in slots],
        input_output_aliases={k: k for k in range(2 * n)},
        compiler_params=pltpu.CompilerParams(has_side_effects=DATAFLOW),
    )(*arrs, *slots, ssem, rsem, *after)
    return list(out[n:])


def _pair_exchange_halves(arrs):
    n = len(arrs)

    def body(*refs):
        ins, outs = refs[:n], refs[n:2 * n]
        ssem, rsem = refs[2 * n:]
        x, y, c = _mesh_pos()
        cps = []
        for k in range(n):
            half = arrs[k].shape[1] // 2
            cp = pltpu.make_async_remote_copy(
                src_ref=ins[k].at[:, pl.ds((1 - c) * half, half)], dst_ref=outs[k],
                send_sem=ssem.at[k], recv_sem=rsem.at[k], device_id=(x, y, 1 - c), device_id_type=MESH)
            cp.start()
            cps.append(cp)
        for cp in cps:
            cp.wait()

    return pl.pallas_call(
        body, name="pair_exchange_halves",
        in_specs=[ANY] * n, out_specs=[ANY] * n,
        out_shape=[jax.ShapeDtypeStruct((a.shape[0], a.shape[1] // 2, a.shape[2]), a.dtype) for a in arrs],
        scratch_shapes=[pltpu.SemaphoreType.DMA((n,)), pltpu.SemaphoreType.DMA((n,))],
    )(*arrs)


GRAD_ROW_BLOCKS = 2


def _pair_add(arrs, recvd, core):
    n = len(arrs)
    nb = GRAD_ROW_BLOCKS

    def body(c_ref, *refs):
        for k in range(n):
            refs[2 * n + k][...] = (refs[k][...].astype(F32) + refs[n + k][...].astype(F32)).astype(BF16)

    def blk(a):
        return (1, a.shape[1] // 2 // nb, a.shape[2])

    grid_spec = pltpu.PrefetchScalarGridSpec(
        num_scalar_prefetch=1, grid=(N_SHARD, nb),
        in_specs=[pl.BlockSpec(blk(a), lambda s, i, c: (s, c[0] * nb + i, 0)) for a in arrs]
        + [pl.BlockSpec(blk(a), lambda s, i, c: (s, i, 0)) for a in arrs],
        out_specs=[pl.BlockSpec(blk(a), lambda s, i, c: (s, i, 0)) for a in arrs])
    return pl.pallas_call(
        body, name="pair_add", grid_spec=grid_spec,
        out_shape=[jax.ShapeDtypeStruct(r.shape, BF16) for r in recvd],
        compiler_params=_cparams(2),
    )(core, *arrs, *recvd)


def _own_slot(arrs, chip):
    n = len(arrs)
    nb = GRAD_ROW_BLOCKS

    def body(c_ref, *refs):
        for k in range(n):
            refs[n + k][...] = refs[k][...]

    def blk(a):
        return (1, a.shape[1] // nb, a.shape[2])

    grid_spec = pltpu.PrefetchScalarGridSpec(
        num_scalar_prefetch=1, grid=(nb,),
        in_specs=[pl.BlockSpec(blk(a), lambda i, c: (c[0], i, 0)) for a in arrs],
        out_specs=[pl.BlockSpec(blk(a), lambda i, c: (c[0], i, 0)) for a in arrs])
    return pl.pallas_call(
        body, name="own_slot", grid_spec=grid_spec,
        out_shape=[jax.ShapeDtypeStruct(a.shape, a.dtype) for a in arrs],
        compiler_params=_cparams(1),
    )(chip, *arrs)


def _scatter_to_owners(arrs, slots):
    n = len(arrs)

    def body(*refs):
        ins, outs = refs[:n], refs[2 * n:3 * n]
        ssem, rsem = refs[3 * n:]
        x, y, c = _mesh_pos()
        chip = 2 * x + y
        others = _other_chips(x, y)
        cps = []
        for k in range(n):
            for j, (ox, oy) in enumerate(others):
                cp = pltpu.make_async_remote_copy(
                    src_ref=ins[k].at[2 * ox + oy], dst_ref=outs[k].at[chip],
                    send_sem=ssem.at[3 * k + j], recv_sem=rsem.at[3 * k + j],
                    device_id=(ox, oy, c), device_id_type=MESH)
                cp.start()
                cps.append(cp)
        for k in range(n):
            for j, (ox, oy) in enumerate(others):
                blk = outs[k].at[2 * ox + oy]
                pltpu.make_async_remote_copy(
                    src_ref=blk, dst_ref=blk, send_sem=ssem.at[3 * k + j], recv_sem=rsem.at[3 * k + j],
                    device_id=(ox, oy, c), device_id_type=MESH).wait_recv()
        for cp in cps:
            cp.wait_send()

    return pl.pallas_call(
        body, name="scatter_to_owners",
        in_specs=[ANY] * (2 * n), out_specs=[ANY] * n,
        out_shape=[jax.ShapeDtypeStruct(a.shape, a.dtype) for a in arrs],
        scratch_shapes=[pltpu.SemaphoreType.DMA((3 * n,)), pltpu.SemaphoreType.DMA((3 * n,))],
        input_output_aliases={n + k: k for k in range(n)},
    )(*arrs, *slots)


def _sum_chips(arrs, core):
    n = len(arrs)
    nb = GRAD_ROW_BLOCKS

    def body(c_ref, *refs):
        for k in range(n):
            r = refs[k]
            refs[n + k][...] = ((r[0].astype(F32) + r[1].astype(F32)) + r[2].astype(F32)) + r[3].astype(F32)

    grid_spec = pltpu.PrefetchScalarGridSpec(
        num_scalar_prefetch=1, grid=(nb,),
        in_specs=[pl.BlockSpec((N_SHARD, a.shape[1] // nb, a.shape[2]), lambda i, c: (0, i, 0)) for a in arrs],
        out_specs=[pl.BlockSpec((a.shape[1] // nb, a.shape[2]), lambda i, c: (c[0] * nb + i, 0)) for a in arrs])
    return pl.pallas_call(
        body, name="sum_chips", grid_spec=grid_spec,
        out_shape=[jax.ShapeDtypeStruct((2 * a.shape[1], a.shape[2]), F32) for a in arrs],
        compiler_params=_cparams(1),
    )(core, *arrs)


def _pair_allgather_halves(arrs):
    n = len(arrs)

    def body(*refs):
        outs = refs[n:2 * n]
        ssem, rsem = refs[2 * n:]
        x, y, c = _mesh_pos()
        cps = []
        for k in range(n):
            h = arrs[k].shape[0] // 2
            mine = outs[k].at[pl.ds(c * h, h)]
            cp = pltpu.make_async_remote_copy(src_ref=mine, dst_ref=mine, send_sem=ssem.at[k],
                                              recv_sem=rsem.at[k], device_id=(x, y, 1 - c), device_id_type=MESH)
            cp.start()
            cps.append(cp)
        for k, cp in enumerate(cps):
            h = arrs[k].shape[0] // 2
            theirs = outs[k].at[pl.ds((1 - c) * h, h)]
            pltpu.make_async_remote_copy(src_ref=theirs, dst_ref=theirs, send_sem=ssem.at[k], recv_sem=rsem.at[k],
                                         device_id=(x, y, 1 - c), device_id_type=MESH).wait_recv()
            cp.wait_send()

    return pl.pallas_call(
        body, name="pair_allgather_halves",
        in_specs=[ANY] * n, out_specs=[ANY] * n,
        out_shape=[jax.ShapeDtypeStruct(a.shape, a.dtype) for a in arrs],
        scratch_shapes=[pltpu.SemaphoreType.DMA((n,)), pltpu.SemaphoreType.DMA((n,))],
        input_output_aliases={k: k for k in range(n)},
    )(*arrs)


N_DEV = 8


def _allgather_all(v):
    m_per, n = v.shape

    def body(x_ref, out_ref, send_sems, recv_sems, local_sem):
        x, y, c = _mesh_pos()
        me, sibling = (x, y, c), (x, y, 1 - c)
        chips = _other_chips(x, y)

        def rows(px, py, pc):
            return out_ref.at[pl.ds((4 * px + 2 * py + pc) * m_per, m_per), :]

        def copy(k, block, to, src=None):
            return pltpu.make_async_remote_copy(
                src_ref=rows(*block) if src is None else src, dst_ref=rows(*block),
                send_sem=send_sems.at[k], recv_sem=recv_sems.at[k], device_id=to, device_id_type=MESH)

        mine = pltpu.make_async_copy(x_ref, rows(*me), local_sem)
        mine.start()
        first = [copy(0, me, sibling, src=x_ref)]
        first += [copy(1 + j, me, (*chip, c), src=x_ref) for j, chip in enumerate(chips)]
        for cp in first:
            cp.start()
        passed = [copy(4 + j, (*chip, c), sibling) for j, chip in enumerate(chips)]
        for j, chip in enumerate(chips):
            copy(1 + j, (*chip, c), me).wait_recv()
            passed[j].start()
        copy(0, sibling, me).wait_recv()
        for j, chip in enumerate(chips):
            copy(4 + j, (*chip, 1 - c), me).wait_recv()
        for cp in first + passed:
            cp.wait_send()
        mine.wait()

    return pl.pallas_call(
        body, name="allgather_all",
        out_shape=jax.ShapeDtypeStruct((N_DEV * m_per, n), v.dtype),
        in_specs=[pl.BlockSpec(memory_space=pltpu.VMEM)],
        out_specs=pl.BlockSpec(memory_space=pltpu.VMEM),
        scratch_shapes=[pltpu.SemaphoreType.DMA((7,)), pltpu.SemaphoreType.DMA((7,)), pltpu.SemaphoreType.DMA],
        compiler_params=pltpu.CompilerParams(vmem_limit_bytes=VMEM_LIMIT_MB * 1024 * 1024),
    )(v)


def _adamw_math(w, g, m, v):
    m2 = ADAM_B1 * m + (1.0 - ADAM_B1) * g
    v2 = ADAM_B2 * v + (1.0 - ADAM_B2) * (g * g)
    m_hat = m2 / (1.0 - ADAM_B1 ** ADAM_STEP)
    v_hat = v2 / (1.0 - ADAM_B2 ** ADAM_STEP)
    delta = -ADAM_LR * (m_hat / (jnp.sqrt(v_hat) + ADAM_EPS) + ADAM_WD * w)
    return delta, m2, v2


def _adamw(w, m, v, gs, nblk):
    nl, r, n = w.shape
    assert nl == len(gs) and nl in (1, 2)
    br = r // nblk

    def body(w_ref, m_ref, v_ref, *rest):
        g_refs, (go_ref, d_ref, mo_ref, vo_ref) = rest[:nl], rest[nl:]
        g = g_refs[0][...]
        if nl == 2:
            g = jnp.where(pl.program_id(0) == 0, g, g_refs[1][...])
        delta, m2, v2 = _adamw_math(w_ref[0], g, m_ref[0], v_ref[0])
        go_ref[0] = g
        d_ref[0] = delta
        mo_ref[0] = m2
        vo_ref[0] = v2

    spec = pl.BlockSpec((1, br, n), lambda l, i: (l, i, 0))
    g_specs = [pl.BlockSpec((br, n), lambda l, i: (i, 0))] if nl == 1 else [
        pl.BlockSpec((br, n), lambda l, i: (jnp.where(l == 0, i, nblk - 1), 0)),
        pl.BlockSpec((br, n), lambda l, i: (jnp.where(l == 1, i, 0), 0))]
    return pl.pallas_call(
        body, name="adamw", grid=(nl, nblk),
        in_specs=[spec, spec, spec] + g_specs,
        out_specs=[spec] * 4,
        out_shape=[jax.ShapeDtypeStruct((nl, r, n), F32)] * 4,
        compiler_params=_cparams(2),
    )(w, m, v, *gs)


def _small_reduce_adamw(parts, w, m, v, rep_rows, sh_rows):
    mrows = rep_rows + N_SHARD * sh_rows

    def body(p_ref, w_ref, m_ref, v_ref, go_ref, d_ref, mo_ref, vo_ref):
        x, y, _ = _mesh_pos()
        mine = rep_rows + (2 * x + y) * sh_rows
        g_rep = p_ref[0:rep_rows, :]
        g_sh = p_ref[pl.ds(pl.multiple_of(mine, SUBLANE), sh_rows), :]
        for k in range(1, N_DEV):
            g_rep = g_rep + p_ref[k * mrows:k * mrows + rep_rows, :]
            g_sh = g_sh + p_ref[pl.ds(pl.multiple_of(k * mrows + mine, SUBLANE), sh_rows), :]
        g = jnp.concatenate([g_rep, g_sh], axis=0)
        delta, m2, v2 = _adamw_math(w_ref[...], g, m_ref[...], v_ref[...])
        go_ref[...] = g
        d_ref[...] = delta
        mo_ref[...] = m2
        vo_ref[...] = v2

    return pl.pallas_call(
        body, name="small_reduce_adamw",
        out_shape=[jax.ShapeDtypeStruct((rep_rows + sh_rows, 128), F32)] * 4,
        compiler_params=pltpu.CompilerParams(vmem_limit_bytes=VMEM_LIMIT_MB * 1024 * 1024),
    )(parts, w, m, v)


LANE = 128
REP_SPEC = (("ffn1_norm", 16), ("mix_norm", 16), ("ffn2_norm", 16), ("final_norm", 8), ("pool_w", 128),
            ("pool_scale", 8), ("hgrn_lb_logits", 16), ("hgrn_gnorm", 8), ("lru_wa", 256), ("lru_wx", 256))
SH_SPEC = (("meta_tokens", 32), ("conv_w", 32), ("lru_conv_w", 8), ("conv_b", 8), ("conv_ln_g", 8),
           ("conv_ln_b", 8), ("lru_conv_b", 8), ("lru_ba", 8), ("lru_bx", 8), ("lru_lambda", 8))
REP_ROWS = sum(r for _, r in REP_SPEC)
SH_ROWS = sum(r for _, r in SH_SPEC)


def _pack_rows(vals, spec):
    parts = []
    for name, rows in spec:
        flat = vals[name].astype(F32).reshape(-1, LANE)
        if flat.shape[0] < rows:
            flat = jnp.concatenate([flat, jnp.zeros((rows - flat.shape[0], LANE), F32)], axis=0)
        parts.append(flat)
    return jnp.concatenate(parts, axis=0)


def _unpack_rows(packed, spec, shapes):
    out = {}
    off = 0
    for name, rows in spec:
        shp = shapes[name]
        n = int(np.prod(shp)) // LANE
        out[name] = packed[off:off + n].reshape(shp)
        off += rows
    return out


def _block_diag(blocks):
    n, b, _ = blocks.shape
    return sum(jnp.pad(blocks[g], ((g * b, (n - 1 - g) * b), (g * b, (n - 1 - g) * b))) for g in range(n))


def _diag_blocks(mat, n):
    b = mat.shape[0] // n
    return jnp.stack([mat[g * b:(g + 1) * b, g * b:(g + 1) * b] for g in range(n)])


BIG = ("ffn1_wg", "ffn1_wu", "ffn2_wg", "ffn2_wu", "ffn1_wd", "ffn2_wd", "w_in_even", "w_out_even",
       "w_in_odd", "w_out_odd")
ADAM_BLOCKS = {"ffn1_wg": 8, "ffn1_wu": 8, "ffn2_wg": 8, "ffn2_wu": 8, "ffn1_wd": 4, "ffn2_wd": 4,
               "w_in_even": 4, "w_out_even": 2, "w_in_odd": 4, "w_out_odd": 2}
WEIGHT_NAMES = ('meta_tokens', 'ffn1_norm', 'ffn1_wg', 'ffn1_wu', 'ffn1_wd', 'mix_norm', 'ffn2_norm', 'ffn2_wg',
                'ffn2_wu', 'ffn2_wd', 'w_in_even', 'pool_w', 'pool_scale', 'hgrn_lb_logits', 'hgrn_gnorm',
                'w_out_even', 'w_in_odd', 'conv_w', 'conv_b', 'conv_ln_g', 'conv_ln_b', 'lru_conv_w',
                'lru_conv_b', 'lru_wa', 'lru_ba', 'lru_wx', 'lru_bx', 'lru_lambda', 'w_out_odd', 'final_norm')


def _rows2d(a):
    return a.reshape(-1, a.shape[-1])


def _local_step_v2(x, tgt, w, gathered, small_full):
    s_len, d = x.shape
    t_real = s_len + N_META
    tp = -(-t_real // ROW_ALIGN) * ROW_ALIGN
    tm = _tile(tp, 832, ROW_ALIGN)
    tm_small = _tile(tp, 416, 16)
    tr = _tile(tp, 416, SUBLANE)
    f1 = ("ffn1_norm", "ffn1_wg", "ffn1_wu", "ffn1_wd")
    f2 = ("ffn2_norm", "ffn2_wg", "ffn2_wu", "ffn2_wd")

    meta_full = small_full["meta_tokens"]
    h0 = jnp.concatenate([meta_full, x, jnp.zeros((tp - t_real, d), F32)], axis=0)
    tgt_pad = jnp.concatenate([jnp.zeros((N_META, d), F32), tgt, jnp.zeros((tp - t_real, d), F32)], axis=0)

    w_in_even = jnp.transpose(gathered["w_in_even"], (1, 0, 2)).reshape(d, D_IN_EVEN)
    w_out_even = gathered["w_out_even"].reshape(d, d)
    w_out_odd = gathered["w_out_odd"].reshape(d, d)
    even_piece = [(w_in_even, (d, D_IN_EVEN), (0, 0))]
    odd_pieces = [(gathered["w_in_odd"], (1, d, D_IN_ODD // N_SHARD), (k, 0, 0)) for k in range(N_SHARD)]
    pool_wbd = _block_diag(w["pool_w"][0]).astype(BF16)
    pool_scale = w["pool_scale"]
    wa_bd = _block_diag2(w["lru_wa"][0]).astype(BF16)
    wx_bd = _block_diag2(w["lru_wx"][0]).astype(BF16)
    mst, msk, n_lev = _hgrn_consts(HG_CHUNK)
    conv_w = small_full["conv_w"]
    sf = small_full

    def gain(name, layer):
        return w[name][layer:layer + 1]

    def ffn(h, names, layer):
        return _ffn_fwd(h, gain(names[0], layer), gathered[names[1]], gathered[names[2]], gathered[names[3]],
                        layer, tm)

    h1, a1, b1, n1 = ffn(h0, f1, 0)
    p0, nm0 = _proj_fwd(h1, gain("mix_norm", 0), 0, even_piece, tm_small)
    ya = _pool_fwd(p0, pool_wbd, pool_scale, tr)
    yb, states = _hgrn_fwd(p0, w["hgrn_lb_logits"], w["hgrn_gnorm"], mst, msk, n_lev, tm)
    h2 = _out_fwd(h1, ya, yb, w_out_even, tm)
    h3, a2, b2, n2 = ffn(h2, f2, 0)
    h4, a3, b3, n3 = ffn(h3, f1, 1)
    p1, nm1 = _proj_fwd(h4, gain("mix_norm", 1), 1, odd_pieces, tm_small)
    yc = _convmod_fwd(p1, conv_w, sf["conv_b"], sf["conv_ln_g"], sf["conv_ln_b"], tr)
    lru_args = (sf["lru_conv_w"], sf["lru_conv_b"], wa_bd, sf["lru_ba"], wx_bd, sf["lru_bx"], sf["lru_lambda"])
    yd, hs = _lru_fwd(p1, *lru_args)
    h5 = _out_fwd(h4, yc, yd, w_out_odd, tm)
    h6, a4, b4, n4 = ffn(h5, f2, 1)
    loss, dh6, dg_final = _loss_bwd(h6, w["final_norm"].reshape(1, d), tgt_pad, t_real, tm)

    def ffn_bwd(dho, h, n, a, b, names, layer, acc):
        dh, da, db, dg = _ffn_bwd_act(dho, h, gain(names[0], layer), a, b, gathered[names[1]], gathered[names[2]],
                                      gathered[names[3]], layer, tm_small)
        acc = _ffn_bwd_w(dho, n, a, b, da, db, acc[0], acc[1], acc[2], layer, tm)
        return dh, dg, acc

    none3 = (None, None, None)
    dh5, dg_f2_l1, g_f2 = ffn_bwd(dh6, h5, n4, a4, b4, f2, 1, none3)
    dyc, dyd, dw_out_odd = _out_bwd(dh5, yc, yd, w_out_odd, tm)
    dca, dcb, dconv_w, dconv_vec = _convmod_bwd(p1, dyc, conv_w, sf["conv_b"], sf["conv_ln_g"], sf["conv_ln_b"], tr)
    dlx, dlg, dwa_bd, dwx_bd, dlru_vec = _lru_bwd(p1, hs, dyd, *lru_args)
    dp1 = [dca, dcb, dlx, dlg]
    dh4, dg_mix_l1 = _proj_bwd_act(dh5, h4, gain("mix_norm", 1), 1, dp1, odd_pieces, tm_small)
    dw_in_odd = jnp.stack(_proj_bwd_w(nm1, dp1, tm))
    dh3, dg_f1_l1, g_f1 = ffn_bwd(dh4, h3, n3, a3, b3, f1, 1, none3)
    dh2, dg_f2_l0, g_f2 = ffn_bwd(dh3, h2, n2, a2, b2, f2, 0, g_f2)
    dya, dyb, dw_out_even = _out_bwd(dh2, ya, yb, w_out_even, tm)
    dpool, dpool_wbd, dpool_scale = _pool_bwd(p0, dya, pool_wbd, pool_scale, tr)
    dq, dz, dv, dgate, dlb_logits, dgn_heads = _hgrn_bwd(p0, dyb, states, w["hgrn_lb_logits"], w["hgrn_gnorm"],
                                                         mst, msk, n_lev, tm)
    dp0 = [jnp.concatenate([dpool, dq, dz, dv, dgate], axis=1)]
    dh1, dg_mix_l0 = _proj_bwd_act(dh2, h1, gain("mix_norm", 0), 0, dp0, even_piece, tm_small)
    (dw_in_even,) = _proj_bwd_w(nm0, dp0, tm_small)
    dh0, dg_f1_l0, g_f1 = ffn_bwd(dh1, h0, n1, a1, b1, f1, 0, g_f1)

    grad_x = dh0[N_META:t_real]
    big = {
        "ffn1_wg": g_f1[0], "ffn1_wu": g_f1[1], "ffn1_wd": g_f1[2],
        "ffn2_wg": g_f2[0], "ffn2_wu": g_f2[1], "ffn2_wd": g_f2[2],
        "w_in_even": jnp.transpose(dw_in_even.reshape(d, N_SHARD, D_IN_EVEN // N_SHARD), (1, 0, 2)),
        "w_out_even": dw_out_even.reshape(N_SHARD, d // N_SHARD, d),
        "w_in_odd": dw_in_odd,
        "w_out_odd": dw_out_odd.reshape(N_SHARD, d // N_SHARD, d),
    }
    rep = {
        "ffn1_norm": jnp.concatenate([dg_f1_l0, dg_f1_l1], axis=0),
        "mix_norm": jnp.concatenate([dg_mix_l0, dg_mix_l1], axis=0),
        "ffn2_norm": jnp.concatenate([dg_f2_l0, dg_f2_l1], axis=0),
        "final_norm": dg_final,
        "pool_w": _diag_blocks(dpool_wbd, len(POOL_WINDOWS)),
        "pool_scale": dpool_scale,
        "hgrn_lb_logits": dlb_logits,
        "hgrn_gnorm": jnp.sum(dgn_heads, axis=0),
        "lru_wa": _diag_blocks2(dwa_bd),
        "lru_wx": _diag_blocks2(dwx_bd),
    }
    dmeta = jnp.transpose(dh0[:N_META].reshape(N_META, N_SHARD, 2, LANE), (1, 0, 2, 3)).reshape(N_SHARD, 32, LANE)
    packs = [_pack_rows(rep, REP_SPEC)]
    for s in range(N_SHARD):
        sh = {
            "meta_tokens": dmeta[s], "conv_w": dconv_w[s], "lru_conv_w": dlru_vec[s, 0:4],
            "conv_b": dconv_vec[s, 0:1], "conv_ln_g": dconv_vec[s, 1:2], "conv_ln_b": dconv_vec[s, 2:3],
            "lru_conv_b": dlru_vec[s, 4:5], "lru_ba": dlru_vec[s, 5:6], "lru_bx": dlru_vec[s, 6:7],
            "lru_lambda": dlru_vec[s, 7:8],
        }
        packs.append(_pack_rows(sh, SH_SPEC))
    return loss, grad_x, big, jnp.concatenate(packs, axis=0)


def _block_diag2(heads):
    nb = heads.shape[0] // 2
    return jnp.stack([_block_diag(heads[2 * j:2 * j + 2]) for j in range(nb)])


def _diag_blocks2(mats):
    return jnp.concatenate([_diag_blocks(mats[j], 2) for j in range(mats.shape[0])], axis=0)


def _kernel_v2(x, meta_tokens, ffn1_norm, ffn1_wg, ffn1_wu, ffn1_wd, mix_norm, ffn2_norm, ffn2_wg, ffn2_wu, ffn2_wd, w_in_even, pool_w, pool_scale, hgrn_lb_logits, hgrn_gnorm, w_out_even, w_in_odd, conv_w, conv_b, conv_ln_g, conv_ln_b, lru_conv_w, lru_conv_b, lru_wa, lru_ba, lru_wx, lru_bx, lru_lambda, w_out_odd, final_norm, loss_target, m_meta_tokens, m_ffn1_norm, m_ffn1_wg, m_ffn1_wu, m_ffn1_wd, m_mix_norm, m_ffn2_norm, m_ffn2_wg, m_ffn2_wu, m_ffn2_wd, m_w_in_even, m_pool_w, m_pool_scale, m_hgrn_lb_logits, m_hgrn_gnorm, m_w_out_even, m_w_in_odd, m_conv_w, m_conv_b, m_conv_ln_g, m_conv_ln_b, m_lru_conv_w, m_lru_conv_b, m_lru_wa, m_lru_ba, m_lru_wx, m_lru_bx, m_lru_lambda, m_w_out_odd, m_final_norm, v_meta_tokens, v_ffn1_norm, v_ffn1_wg, v_ffn1_wu, v_ffn1_wd, v_mix_norm, v_ffn2_norm, v_ffn2_wg, v_ffn2_wu, v_ffn2_wd, v_w_in_even, v_pool_w, v_pool_scale, v_hgrn_lb_logits, v_hgrn_gnorm, v_w_out_even, v_w_in_odd, v_conv_w, v_conv_b, v_conv_ln_g, v_conv_ln_b, v_lru_conv_w, v_lru_conv_b, v_lru_wa, v_lru_ba, v_lru_wx, v_lru_bx, v_lru_lambda, v_w_out_odd, v_final_norm):
    args = locals()
    w = {n: args[n] for n in WEIGHT_NAMES}
    m = {n: args["m_" + n] for n in WEIGHT_NAMES}
    v = {n: args["v_" + n] for n in WEIGHT_NAMES}
    shapes = {n: w[n].shape for n in WEIGHT_NAMES}

    big_in = [_rows2d(w[n]).astype(BF16) for n in BIG]
    small_sh = _pack_rows(w, SH_SPEC)
    gath = _allgather_shards(big_in + [small_sh], [True] * len(BIG) + [False])
    gathered = dict(zip(BIG, gath[:len(BIG)]))
    sm = gath[len(BIG)]
    sh_shapes = {n: (N_SHARD,) + tuple(shapes[n]) for n, _ in SH_SPEC}
    per_shard = [_unpack_rows(sm[s], SH_SPEC, shapes) for s in range(N_SHARD)]
    small_full = {}
    for n, _ in SH_SPEC:
        stacked = [per_shard[s][n] for s in range(N_SHARD)]
        small_full[n] = jnp.concatenate([p.reshape(-1, p.shape[-1]) for p in stacked], axis=-1)
    small_full["conv_w"] = jnp.concatenate(
        [small_full["conv_w"], jnp.zeros((CONV_HALO - CONV_WIDTH, D_CONV), F32)], axis=0)

    loss, grad_x, big, small_part = _local_step(x[0], loss_target[0], w, gathered, small_full)
    loss = lax.psum(loss[0, 0], ("x", "y", "c"))

    core = lax.axis_index("c").astype(jnp.int32).reshape(1)
    parts = [big[n] for n in BIG]
    recvd = _pair_exchange_halves(parts)
    pair = _pair_add(parts, recvd, core)
    chip = (2 * lax.axis_index("x") + lax.axis_index("y")).astype(jnp.int32).reshape(1)
    slots = _scatter_to_owners(pair, _own_slot(pair, chip))
    halves = _sum_chips(slots, core)
    full = _pair_allgather_halves(halves)
    out_g, out_d, out_m, out_v = {}, {}, {}, {}
    for n, g in zip(BIG, full):
        res = _adamw(_rows2d(w[n]), _rows2d(m[n]), _rows2d(v[n]), g, 0, ADAM_BLOCKS[n])
        out_g[n], out_d[n], out_m[n], out_v[n] = [r.reshape(shapes[n]) for r in res]

    gathered_small = _allgather_all(small_part)

    def pack_small(src):
        return jnp.concatenate([_pack_rows(src, REP_SPEC), _pack_rows(src, SH_SPEC)], axis=0)

    res = _small_reduce_adamw(gathered_small, pack_small(w), pack_small(m), pack_small(v), REP_ROWS, SH_ROWS)
    for dst, packed in zip((out_g, out_d, out_m, out_v), res):
        dst.update(_unpack_rows(packed[:REP_ROWS], REP_SPEC, shapes))
        dst.update(_unpack_rows(packed[REP_ROWS:], SH_SPEC, shapes))

    return (loss, grad_x[None], *[out_g[n] for n in WEIGHT_NAMES], *[out_d[n] for n in WEIGHT_NAMES],
            *[out_m[n] for n in WEIGHT_NAMES], *[out_v[n] for n in WEIGHT_NAMES])


GATHER_GROUPS = (
    (("small", 0),),
    (("ffn1_wg", 0), ("ffn1_wu", 0), ("ffn1_wd", 0)),
    (("w_in_even", 0), ("w_out_even", 0)),
    (("ffn2_wg", 0), ("ffn2_wu", 0), ("ffn2_wd", 0)),
    (("ffn1_wg", 1), ("ffn1_wu", 1), ("ffn1_wd", 1)),
    (("w_in_odd", 0), ("w_out_odd", 0)),
    (("ffn2_wg", 1), ("ffn2_wu", 1), ("ffn2_wd", 1)),
)
ADAM_ROW_BLOCKS = {"ffn1_wg": 2, "ffn1_wu": 2, "ffn2_wg": 2, "ffn2_wu": 2, "ffn1_wd": 2, "ffn2_wd": 2,
                   "w_in_even": 4, "w_out_even": 2, "w_in_odd": 4, "w_out_odd": 2}
TRANSPOSED = ("ffn1_wg", "ffn1_wu", "ffn2_wg", "ffn2_wu", "w_in_even")
SCATTER_DEPTH = 2


def _unpack_small(sm, shapes):
    per_shard = [_unpack_rows(sm[s], SH_SPEC, shapes) for s in range(N_SHARD)]
    full = {}
    for n, _ in SH_SPEC:
        full[n] = jnp.concatenate([per_shard[s][n].reshape(-1, shapes[n][-1]) for s in range(N_SHARD)], axis=-1)
    full["conv_w"] = jnp.concatenate([full["conv_w"], jnp.zeros((CONV_HALO - CONV_WIDTH, D_CONV), F32)], axis=0)
    return full


def _local_step(x, tgt, w, shapes, fetch, emit):
    s_len, d = x.shape
    t_real = s_len + N_META
    tp = -(-t_real // ROW_ALIGN) * ROW_ALIGN
    tm = _tile(tp, 832, ROW_ALIGN)
    tm_small = _tile(tp, 416, 16)
    tr = _tile(tp, 416, SUBLANE)

    def gain(name, layer):
        return w[name][layer:layer + 1]

    pool_wbd = _block_diag(w["pool_w"][0]).astype(BF16)
    pool_scale = w["pool_scale"]
    wa_bd = _block_diag2(w["lru_wa"][0]).astype(BF16)
    wx_bd = _block_diag2(w["lru_wx"][0]).astype(BF16)
    mst, msk, n_lev = _hgrn_consts(HG_CHUNK)

    (sm,) = fetch(0, None)
    sf = _unpack_small(sm, shapes)
    h0 = jnp.concatenate([sf["meta_tokens"], x, jnp.zeros((tp - t_real, d), F32)], axis=0)
    tgt_pad = jnp.concatenate([jnp.zeros((N_META, d), F32), tgt, jnp.zeros((tp - t_real, d), F32)], axis=0)
    f1l0 = fetch(1, h0)
    h1, *s1 = _ffn_fwd(h0, gain("ffn1_norm", 0), *f1l0, 0, tm)
    w_in_even4, w_out_even4 = fetch(2, h1)
    w_out_even = w_out_even4.reshape(d, d)
    even_piece = [(w_in_even4.reshape(D_IN_EVEN, d), (D_IN_EVEN, d), (0, 0))]
    p0, nm0 = _proj_fwd(h1, gain("mix_norm", 0), 0, even_piece, tm_small, wt=True)
    ya = _pool_fwd(p0, pool_wbd, pool_scale, tr)
    yb, states = _hgrn_fwd(p0, w["hgrn_lb_logits"], w["hgrn_gnorm"], mst, msk, n_lev, tm)
    h2 = _out_fwd(h1, ya, yb, w_out_even, tm)
    f2l0 = fetch(3, h2)
    h3, *s2 = _ffn_fwd(h2, gain("ffn2_norm", 0), *f2l0, 0, tm)
    f1l1 = fetch(4, h3)
    h4, *s3 = _ffn_fwd(h3, gain("ffn1_norm", 1), *f1l1, 0, tm)
    w_in_odd4, w_out_odd4 = fetch(5, h4)
    w_out_odd = w_out_odd4.reshape(d, d)
    odd_pieces = [(w_in_odd4, (1, d, D_IN_ODD // N_SHARD), (k, 0, 0)) for k in range(N_SHARD)]
    p1, nm1 = _proj_fwd(h4, gain("mix_norm", 1), 1, odd_pieces, tm_small)
    yc = _convmod_fwd(p1, sf["conv_w"], sf["conv_b"], sf["conv_ln_g"], sf["conv_ln_b"], tr)
    lru_args = (sf["lru_conv_w"], sf["lru_conv_b"], wa_bd, sf["lru_ba"], wx_bd, sf["lru_bx"], sf["lru_lambda"])
    yd, hs = _lru_fwd(p1, *lru_args)
    h5 = _out_fwd(h4, yc, yd, w_out_odd, tm)
    f2l1 = fetch(6, h5)
    h6, *s4 = _ffn_fwd(h5, gain("ffn2_norm", 1), *f2l1, 0, tm)
    loss, dh6, dg_final = _loss_bwd(h6, w["final_norm"].reshape(1, d), tgt_pad, t_real, tm)

    def ffn_bwd(dho, h, saved, norm, wts, after=()):
        ga, gb, sa, n = saved
        dh, da, db, dg, dy = _ffn_bwd_act(dho, h, norm, ga, gb, *wts, 0, tm, after)
        return dh, dg, _ffn_bwd_w(dy, n, sa, da, db, tm)

    dh5, dg_f2_l1, g = ffn_bwd(dh6, h5, s4, gain("ffn2_norm", 1), f2l1)
    sent = emit((("ffn2_wg", 1), ("ffn2_wu", 1), ("ffn2_wd", 1)), g)
    dyc, dyd, dw_out_odd = _out_bwd(dh5, yc, yd, w_out_odd, tm, tuple(sent))
    dca, dcb, dconv_w, dconv_vec = _convmod_bwd(p1, dyc, sf["conv_w"], sf["conv_b"], sf["conv_ln_g"],
                                                sf["conv_ln_b"], tr)
    dlx, dlg, dwa_bd, dwx_bd, dlru_vec = _lru_bwd(p1, hs, dyd, *lru_args)
    dp1 = [dca, dcb, dlx, dlg]
    dh4, dg_mix_l1 = _proj_bwd_act(dh5, h4, gain("mix_norm", 1), 1, dp1, odd_pieces, tm_small)
    dw_in_odd = jnp.stack(_proj_bwd_w(nm1, dp1, tm))
    dh3, dg_f1_l1, g = ffn_bwd(dh4, h3, s3, gain("ffn1_norm", 1), f1l1)
    sent = emit((("w_out_odd", 0), ("w_in_odd", 0), ("ffn1_wg", 1), ("ffn1_wu", 1), ("ffn1_wd", 1)),
                [dw_out_odd.reshape(N_SHARD, d // N_SHARD, d), dw_in_odd] + list(g))
    dh2, dg_f2_l0, g_f2l0 = ffn_bwd(dh3, h2, s2, gain("ffn2_norm", 0), f2l0, tuple(sent))
    dya, dyb, dw_out_even = _out_bwd(dh2, ya, yb, w_out_even, tm)
    dpool, dpool_wbd, dpool_scale = _pool_bwd(p0, dya, pool_wbd, pool_scale, tr)
    dq, dz, dv, dgate, dlb_logits, dgn_heads = _hgrn_bwd(p0, dyb, states, w["hgrn_lb_logits"], w["hgrn_gnorm"],
                                                         mst, msk, n_lev, tm)
    dp0 = [jnp.concatenate([dpool, dq, dz, dv, dgate], axis=1)]
    dh1, dg_mix_l0 = _proj_bwd_act(dh2, h1, gain("mix_norm", 0), 0, dp0, even_piece, tm_small, wt=True)
    (dw_in_even_t,) = _proj_bwd_w(nm0, dp0, tm_small, wt=True)
    sent = emit((("ffn2_wg", 0), ("ffn2_wu", 0), ("ffn2_wd", 0), ("w_out_even", 0), ("w_in_even", 0)),
                list(g_f2l0) + [dw_out_even.reshape(N_SHARD, d // N_SHARD, d),
                                dw_in_even_t.reshape(N_SHARD, D_IN_EVEN // N_SHARD, d)])
    dh0, dg_f1_l0, g = ffn_bwd(dh1, h0, s1, gain("ffn1_norm", 0), f1l0, tuple(sent))
    emit((("ffn1_wg", 0), ("ffn1_wu", 0), ("ffn1_wd", 0)), g)

    grad_x = dh0[N_META:t_real]
    rep = {
        "ffn1_norm": jnp.concatenate([dg_f1_l0, dg_f1_l1], axis=0),
        "mix_norm": jnp.concatenate([dg_mix_l0, dg_mix_l1], axis=0),
        "ffn2_norm": jnp.concatenate([dg_f2_l0, dg_f2_l1], axis=0),
        "final_norm": dg_final,
        "pool_w": _diag_blocks(dpool_wbd, len(POOL_WINDOWS)),
        "pool_scale": dpool_scale,
        "hgrn_lb_logits": dlb_logits,
        "hgrn_gnorm": jnp.sum(dgn_heads, axis=0),
        "lru_wa": _diag_blocks2(dwa_bd),
        "lru_wx": _diag_blocks2(dwx_bd),
    }
    dmeta = jnp.transpose(dh0[:N_META].reshape(N_META, N_SHARD, 2, LANE), (1, 0, 2, 3)).reshape(N_SHARD, 32, LANE)
    packs = [_pack_rows(rep, REP_SPEC)]
    for s in range(N_SHARD):
        sh = {
            "meta_tokens": dmeta[s], "conv_w": dconv_w[s], "lru_conv_w": dlru_vec[s, 0:4],
            "conv_b": dconv_vec[s, 0:1], "conv_ln_g": dconv_vec[s, 1:2], "conv_ln_b": dconv_vec[s, 2:3],
            "lru_conv_b": dlru_vec[s, 4:5], "lru_ba": dlru_vec[s, 5:6], "lru_bx": dlru_vec[s, 6:7],
            "lru_lambda": dlru_vec[s, 7:8],
        }
        packs.append(_pack_rows(sh, SH_SPEC))
    return loss, grad_x, jnp.concatenate(packs, axis=0)


def kernel(x, meta_tokens, ffn1_norm, ffn1_wg, ffn1_wu, ffn1_wd, mix_norm, ffn2_norm, ffn2_wg, ffn2_wu, ffn2_wd, w_in_even, pool_w, pool_scale, hgrn_lb_logits, hgrn_gnorm, w_out_even, w_in_odd, conv_w, conv_b, conv_ln_g, conv_ln_b, lru_conv_w, lru_conv_b, lru_wa, lru_ba, lru_wx, lru_bx, lru_lambda, w_out_odd, final_norm, loss_target, m_meta_tokens, m_ffn1_norm, m_ffn1_wg, m_ffn1_wu, m_ffn1_wd, m_mix_norm, m_ffn2_norm, m_ffn2_wg, m_ffn2_wu, m_ffn2_wd, m_w_in_even, m_pool_w, m_pool_scale, m_hgrn_lb_logits, m_hgrn_gnorm, m_w_out_even, m_w_in_odd, m_conv_w, m_conv_b, m_conv_ln_g, m_conv_ln_b, m_lru_conv_w, m_lru_conv_b, m_lru_wa, m_lru_ba, m_lru_wx, m_lru_bx, m_lru_lambda, m_w_out_odd, m_final_norm, v_meta_tokens, v_ffn1_norm, v_ffn1_wg, v_ffn1_wu, v_ffn1_wd, v_mix_norm, v_ffn2_norm, v_ffn2_wg, v_ffn2_wu, v_ffn2_wd, v_w_in_even, v_pool_w, v_pool_scale, v_hgrn_lb_logits, v_hgrn_gnorm, v_w_out_even, v_w_in_odd, v_conv_w, v_conv_b, v_conv_ln_g, v_conv_ln_b, v_lru_conv_w, v_lru_conv_b, v_lru_wa, v_lru_ba, v_lru_wx, v_lru_bx, v_lru_lambda, v_w_out_odd, v_final_norm):
    args = locals()
    w = {n: args[n] for n in WEIGHT_NAMES}
    m = {n: args["m_" + n] for n in WEIGHT_NAMES}
    v = {n: args["v_" + n] for n in WEIGHT_NAMES}
    shapes = {n: w[n].shape for n in WEIGHT_NAMES}
    core = lax.axis_index("c").astype(jnp.int32).reshape(1)
    chip = (2 * lax.axis_index("x") + lax.axis_index("y")).astype(jnp.int32).reshape(1)

    def view(a, n):
        return jnp.swapaxes(a, 1, 2) if n in TRANSPOSED else a

    wv, mv, vv = [{n: view(src[n], n) for n in BIG} for src in (w, m, v)]

    def shard(key):
        n, l = key
        return _pack_rows(w, SH_SPEC) if n == "small" else wv[n][l].astype(BF16)

    started = {}
    for groups in (GATHER_GROUPS[:2], GATHER_GROUPS[2:]):
        gkeys = [key for grp in groups for key in grp]
        ssem, rsem, srcs, lands, token = _gather_start([shard(key) for key in gkeys])
        for k, key in enumerate(gkeys):
            started[key] = (ssem, rsem, srcs[k], lands[k], k, token)

    def pack_small(src):
        return jnp.concatenate([_pack_rows(src, REP_SPEC), _pack_rows(src, SH_SPEC)], axis=0)

    small_packs = [pack_small(src) for src in (w, m, v)]

    def fetch(group, after):
        st = [started[key] for key in GATHER_GROUPS[group]]
        deps = (st[0][5],) if after is None else (after,)
        if group == 1:
            deps += (started[GATHER_GROUPS[2][0]][5],) + tuple(small_packs)
        return _gather_wait(st[0][0], st[0][1], [s[2] for s in st], [s[3] for s in st], [s[4] for s in st], deps)

    in_flight, reduced = [], {}

    def collect(entry, after):
        gkeys, gs_sem, gr_sem, pair_thru, slots_thru, _ = entry
        slots = _scatter_wait(gs_sem, gr_sem, pair_thru, slots_thru, after)
        full = _pair_allgather_halves(_sum_chips(slots, core))
        reduced.update(zip(gkeys, full))
        return full[0]

    def emit(gkeys, grads):
        grads = list(grads)
        pair = _pair_add(grads, _pair_exchange_halves(grads), core)
        in_flight.append((gkeys,) + tuple(_scatter_start(pair, _own_slot(pair, chip))))
        token = in_flight[-1][-1]
        if len(in_flight) > SCATTER_DEPTH:
            return token, collect(in_flight[-1 - SCATTER_DEPTH], (token,))
        return (token,)

    loss, grad_x, small_part = _local_step(x[0], loss_target[0], w, shapes, fetch, emit)
    loss = lax.psum(loss[0, 0], ("x", "y", "c"))

    small_res = _small_reduce_adamw(_allgather_all(small_part), *small_packs, REP_ROWS, SH_ROWS)
    out_g, out_d, out_m, out_v = {}, {}, {}, {}
    deps = (small_res[0],)
    for entry in in_flight[-SCATTER_DEPTH:]:
        collect(entry, deps)
        for n in BIG:
            layers = range(shapes[n][0])
            if n not in out_g and all((n, l) in reduced for l in layers):
                res = _adamw(wv[n], mv[n], vv[n], [reduced[(n, l)] for l in layers], ADAM_ROW_BLOCKS[n])
                out_g[n], out_d[n], out_m[n], out_v[n] = [view(r, n) for r in res]
                deps += (res[1],)

    for dst, packed in zip((out_g, out_d, out_m, out_v), small_res):
        dst.update(_unpack_rows(packed[:REP_ROWS], REP_SPEC, shapes))
        dst.update(_unpack_rows(packed[REP_ROWS:], SH_SPEC, shapes))

    return (loss, grad_x[None], *[out_g[n] for n in WEIGHT_NAMES], *[out_d[n] for n in WEIGHT_NAMES],
            *[out_m[n] for n in WEIGHT_NAMES], *[out_v[n] for n in WEIGHT_NAMES])
```

```python
import functools

import numpy as np
import jax
import jax.numpy as jnp
from jax import lax
from jax.experimental import pallas as pl
from jax.experimental.pallas import tpu as pltpu

F32 = jnp.float32
BF16 = jnp.bfloat16
MESH = pl.DeviceIdType.MESH

EPS = 1e-6
N_META = 16
D_MODEL = 1024
D_FF = 2816
N_SHARD = 4
FF_SHARD = D_FF // N_SHARD
D_POOL = 256
POOL_GROUP = 64
POOL_WINDOWS = (2, 4, 8, 16)
D_HGRN = 768
HG_HEADS = 6
HEAD = 128
HG_CHUNK = 64
HG_HEADS_PER_STEP = 2
HG_UNROLL = 2
D_IN_EVEN = D_POOL + 4 * D_HGRN
D_CONV = 512
CONV_WIDTH = 31
CONV_HALO = 32
D_LRU = 512
LRU_CONV = 4
LRU_HALO = 8
LRU_C = 8.0
D_IN_ODD = 2 * D_CONV + 2 * D_LRU
SUBLANE = 8
ROW_ALIGN = 64

ADAM_LR = 0.001
ADAM_B1 = 0.9
ADAM_B2 = 0.999
ADAM_EPS = 1e-08
ADAM_WD = 0.01
ADAM_STEP = 10

VMEM_LIMIT_MB = 56


def _cparams(n_grid_axes=0, vmem_mb=VMEM_LIMIT_MB):
    sem = ("arbitrary",) * n_grid_axes if n_grid_axes else None
    return pltpu.CompilerParams(dimension_semantics=sem, vmem_limit_bytes=vmem_mb * 1024 * 1024)


def _tile(n, target, mult):
    best = None
    for t in range(mult, min(n, target) + 1, mult):
        if n % t == 0:
            best = t
    assert best is not None, (n, target, mult)
    return best


def _dot(a, b):
    return jnp.dot(a, b, preferred_element_type=F32)


def _dot_nt(a, b):
    return lax.dot_general(a, b, (((1,), (1,)), ((), ())), preferred_element_type=F32)


def _dot_tn(a, b):
    return lax.dot_general(a, b, (((0,), (0,)), ((), ())), preferred_element_type=F32)


def _sigmoid(x):
    return 1.0 / (1.0 + jnp.exp(-x))


def _colsum(x):
    return jnp.sum(x, axis=0, keepdims=True)


def _rms_stats(h):
    rstd = lax.rsqrt(jnp.mean(h * h, axis=-1, keepdims=True) + EPS)
    return rstd, h * rstd


def _rms_bwd(dn, g, rstd, xhat):
    dng = dn * g
    dh = rstd * (dng - xhat * jnp.mean(dng * xhat, axis=-1, keepdims=True))
    return dh, _colsum(dn * xhat)


def _ffn_fwd(h, norm, wg4, wu4, wd4, layer, tm):
    tp, d = h.shape
    nt = tp // tm

    def body(h_ref, g_ref, wg_ref, wu_ref, wd_ref, ho_ref, ga_ref, gb_ref, sa_ref, n_ref, n_sc, acc):
        s = pl.program_id(1)

        @pl.when(s == 0)
        def _():
            hh = h_ref[...]
            rstd, xhat = _rms_stats(hh)
            n = (xhat * g_ref[...]).astype(BF16)
            n_sc[...] = n
            n_ref[...] = n
            acc[...] = jnp.zeros_like(acc)

        n = n_sc[...]
        a = _dot_nt(n, wg_ref[0])
        b = _dot_nt(n, wu_ref[0])
        sig = _sigmoid(a)
        sil = a * sig
        ga_ref[0] = (sig * (1.0 + a * (1.0 - sig)) * b).astype(BF16)
        gb_ref[0] = sil.astype(BF16)
        sg = (sil * b).astype(BF16)
        sa_ref[0] = sg
        acc[...] += _dot(sg, wd_ref[0])

        @pl.when(s == N_SHARD - 1)
        def _():
            ho_ref[...] = h_ref[...] + 0.5 * acc[...]

    return pl.pallas_call(
        body, name="ffn_fwd",
        grid=(nt, N_SHARD),
        in_specs=[
            pl.BlockSpec((tm, d), lambda i, s: (i, 0)),
            pl.BlockSpec((1, d), lambda i, s: (0, 0)),
            pl.BlockSpec((1, FF_SHARD, d), lambda i, s: (s, layer, 0)),
            pl.BlockSpec((1, FF_SHARD, d), lambda i, s: (s, layer, 0)),
            pl.BlockSpec((1, FF_SHARD, d), lambda i, s: (s, layer, 0)),
        ],
        out_specs=[
            pl.BlockSpec((tm, d), lambda i, s: (i, 0)),
            pl.BlockSpec((1, tm, FF_SHARD), lambda i, s: (s, i, 0)),
            pl.BlockSpec((1, tm, FF_SHARD), lambda i, s: (s, i, 0)),
            pl.BlockSpec((1, tm, FF_SHARD), lambda i, s: (s, i, 0)),
            pl.BlockSpec((tm, d), lambda i, s: (i, 0)),
        ],
        out_shape=[
            jax.ShapeDtypeStruct((tp, d), F32),
            jax.ShapeDtypeStruct((N_SHARD, tp, FF_SHARD), BF16),
            jax.ShapeDtypeStruct((N_SHARD, tp, FF_SHARD), BF16),
            jax.ShapeDtypeStruct((N_SHARD, tp, FF_SHARD), BF16),
            jax.ShapeDtypeStruct((tp, d), BF16),
        ],
        scratch_shapes=[pltpu.VMEM((tm, d), BF16), pltpu.VMEM((tm, d), F32)],
        compiler_params=_cparams(2),
    )(h, norm, wg4, wu4, wd4)


def _ffn_bwd_act(dho, h, norm, ga4, gb4, wg4, wu4, wd4, layer, tm, after=()):
    tp, d = h.shape
    nt = tp // tm

    def body(dho_ref, h_ref, g_ref, ga_ref, gb_ref, wg_ref, wu_ref, wd_ref, *rest):
        dh_ref, da_ref, db_ref, dg_ref, dy_ref, dn_sc = rest[len(after):]
        i = pl.program_id(0)
        s = pl.program_id(1)

        @pl.when(s == 0)
        def _():
            dy_ref[...] = (0.5 * dho_ref[...]).astype(BF16)
            dn_sc[...] = jnp.zeros_like(dn_sc)

        @pl.when((s == 0) & (i == 0))
        def _():
            dg_ref[...] = jnp.zeros_like(dg_ref)

        ds = _dot_nt(dy_ref[...], wd_ref[0])
        da = (ds * ga_ref[0].astype(F32)).astype(BF16)
        db = (ds * gb_ref[0].astype(F32)).astype(BF16)
        da_ref[0] = da
        db_ref[0] = db
        dn_sc[...] += _dot(da, wg_ref[0]) + _dot(db, wu_ref[0])

        @pl.when(s == N_SHARD - 1)
        def _():
            rstd, xhat = _rms_stats(h_ref[...])
            dh, dg = _rms_bwd(dn_sc[...], g_ref[...], rstd, xhat)
            dh_ref[...] = dho_ref[...] + dh
            dg_ref[...] += dg

    return pl.pallas_call(
        body, name="ffn_bwd_act",
        grid=(nt, N_SHARD),
        in_specs=[
            pl.BlockSpec((tm, d), lambda i, s: (i, 0)),
            pl.BlockSpec((tm, d), lambda i, s: (i, 0)),
            pl.BlockSpec((1, d), lambda i, s: (0, 0)),
            pl.BlockSpec((1, tm, FF_SHARD), lambda i, s: (s, i, 0)),
            pl.BlockSpec((1, tm, FF_SHARD), lambda i, s: (s, i, 0)),
            pl.BlockSpec((1, FF_SHARD, d), lambda i, s: (s, layer, 0)),
            pl.BlockSpec((1, FF_SHARD, d), lambda i, s: (s, layer, 0)),
            pl.BlockSpec((1, FF_SHARD, d), lambda i, s: (s, layer, 0)),
        ] + [pl.BlockSpec(memory_space=pl.ANY)] * len(after),
        out_specs=[
            pl.BlockSpec((tm, d), lambda i, s: (i, 0)),
            pl.BlockSpec((1, tm, FF_SHARD), lambda i, s: (s, i, 0)),
            pl.BlockSpec((1, tm, FF_SHARD), lambda i, s: (s, i, 0)),
            pl.BlockSpec((1, d), lambda i, s: (0, 0)),
            pl.BlockSpec((tm, d), lambda i, s: (i, 0)),
        ],
        out_shape=[
            jax.ShapeDtypeStruct((tp, d), F32),
            jax.ShapeDtypeStruct((N_SHARD, tp, FF_SHARD), BF16),
            jax.ShapeDtypeStruct((N_SHARD, tp, FF_SHARD), BF16),
            jax.ShapeDtypeStruct((1, d), F32),
            jax.ShapeDtypeStruct((tp, d), BF16),
        ],
        scratch_shapes=[pltpu.VMEM((tm, d), F32)],
        compiler_params=_cparams(2),
    )(dho, h, norm, ga4, gb4, wg4, wu4, wd4, *after)


def _ffn_bwd_w(dy, n, sa4, da4, db4, tm):
    tp, d = n.shape
    nt = tp // tm

    def body(dy_ref, n_ref, sa_ref, da_ref, db_ref, og_ref, ou_ref, od_ref, accg, accu, accd):
        i = pl.program_id(1)

        @pl.when(i == 0)
        def _():
            accg[...] = jnp.zeros_like(accg)
            accu[...] = jnp.zeros_like(accu)
            accd[...] = jnp.zeros_like(accd)

        nn = n_ref[...]
        accg[...] += _dot_tn(da_ref[0], nn)
        accu[...] += _dot_tn(db_ref[0], nn)
        accd[...] += _dot_tn(sa_ref[0], dy_ref[...])

        @pl.when(i == nt - 1)
        def _():
            og_ref[0] = accg[...].astype(BF16)
            ou_ref[0] = accu[...].astype(BF16)
            od_ref[0] = accd[...].astype(BF16)

    in_specs = [
        pl.BlockSpec((tm, d), lambda s, i: (i, 0)),
        pl.BlockSpec((tm, d), lambda s, i: (i, 0)),
        pl.BlockSpec((1, tm, FF_SHARD), lambda s, i: (s, i, 0)),
        pl.BlockSpec((1, tm, FF_SHARD), lambda s, i: (s, i, 0)),
        pl.BlockSpec((1, tm, FF_SHARD), lambda s, i: (s, i, 0)),
    ]
    return pl.pallas_call(
        body, name="ffn_bwd_w",
        grid=(N_SHARD, nt),
        in_specs=in_specs,
        out_specs=[pl.BlockSpec((1, FF_SHARD, d), lambda s, i: (s, 0, 0))] * 3,
        out_shape=[jax.ShapeDtypeStruct((N_SHARD, FF_SHARD, d), BF16)] * 3,
        scratch_shapes=[pltpu.VMEM((FF_SHARD, d), F32)] * 3,
        compiler_params=_cparams(2),
    )(dy, n, sa4, da4, db4)


def _proj_fwd(h, norm, layer, w_pieces, tm, wt=False):
    tp, d = h.shape
    widths = [bs[-2] if wt else bs[-1] for _, bs, _ in w_pieces]
    ntot = sum(widths)
    npc = len(w_pieces)

    def body(*refs):
        h_ref, g_ref = refs[:2]
        w_refs = refs[2:2 + npc]
        p_ref, n_ref = refs[2 + npc:]
        rstd, xhat = _rms_stats(h_ref[...])
        n = (xhat * g_ref[...]).astype(BF16)
        n_ref[...] = n
        off = 0
        for k in range(npc):
            w = w_refs[k][...]
            w = w.reshape(w.shape[-2], w.shape[-1])
            p_ref[:, off:off + widths[k]] = _dot_nt(n, w) if wt else _dot(n, w)
            off += widths[k]

    in_specs = [pl.BlockSpec((tm, d), lambda i: (i, 0)), pl.BlockSpec((1, d), lambda i: (0, 0))]
    for _, bs, idx in w_pieces:
        in_specs.append(pl.BlockSpec(bs, functools.partial(lambda i, idx: idx, idx=idx)))
    return pl.pallas_call(
        body, name="proj_fwd",
        grid=(tp // tm,),
        in_specs=in_specs,
        out_specs=[pl.BlockSpec((tm, ntot), lambda i: (i, 0)), pl.BlockSpec((tm, d), lambda i: (i, 0))],
        out_shape=[jax.ShapeDtypeStruct((tp, ntot), F32), jax.ShapeDtypeStruct((tp, d), BF16)],
        compiler_params=_cparams(1),
    )(h, norm, *[w for w, _, _ in w_pieces])


def _proj_bwd_act(dres, h, norm, layer, dp_pieces, w_pieces, tm, wt=False):
    tp, d = h.shape
    npc = len(w_pieces)

    def body(*refs):
        dres_ref, h_ref, g_ref = refs[:3]
        dp_refs = refs[3:3 + npc]
        w_refs = refs[3 + npc:3 + 2 * npc]
        dh_ref, dg_ref = refs[3 + 2 * npc:]
        i = pl.program_id(0)

        @pl.when(i == 0)
        def _():
            dg_ref[...] = jnp.zeros_like(dg_ref)

        dn = None
        for k in range(npc):
            w = w_refs[k][...]
            w = w.reshape(w.shape[-2], w.shape[-1])
            t = _dot(dp_refs[k][...], w) if wt else _dot_nt(dp_refs[k][...], w)
            dn = t if dn is None else dn + t
        rstd, xhat = _rms_stats(h_ref[...])
        dh, dg = _rms_bwd(dn, g_ref[...], rstd, xhat)
        dh_ref[...] = dres_ref[...] + dh
        dg_ref[...] += dg

    in_specs = [pl.BlockSpec((tm, d), lambda i: (i, 0)), pl.BlockSpec((tm, d), lambda i: (i, 0)),
                pl.BlockSpec((1, d), lambda i: (0, 0))]
    for dp in dp_pieces:
        in_specs.append(pl.BlockSpec((tm, dp.shape[1]), lambda i: (i, 0)))
    for _, bs, idx in w_pieces:
        in_specs.append(pl.BlockSpec(bs, functools.partial(lambda i, idx: idx, idx=idx)))
    return pl.pallas_call(
        body, name="proj_bwd_act",
        grid=(tp // tm,),
        in_specs=in_specs,
        out_specs=[pl.BlockSpec((tm, d), lambda i: (i, 0)), pl.BlockSpec((1, d), lambda i: (0, 0))],
        out_shape=[jax.ShapeDtypeStruct((tp, d), F32), jax.ShapeDtypeStruct((1, d), F32)],
        compiler_params=_cparams(1),
    )(dres, h, norm, *dp_pieces, *[w for w, _, _ in w_pieces])


def _proj_bwd_w(n, dp_pieces, tm, wt=False):
    tp, d = n.shape
    npc = len(dp_pieces)
    widths = [dp.shape[1] for dp in dp_pieces]
    oshape = (lambda w: (w, d)) if wt else (lambda w: (d, w))

    def body(*refs):
        n_ref = refs[0]
        dp_refs = refs[1:1 + npc]
        o_refs = refs[1 + npc:1 + 2 * npc]
        accs = refs[1 + 2 * npc:]
        i = pl.program_id(0)

        @pl.when(i == 0)
        def _():
            for acc in accs:
                acc[...] = jnp.zeros_like(acc)

        nn = n_ref[...]
        for k in range(npc):
            accs[k][...] += _dot_tn(dp_refs[k][...], nn) if wt else _dot_tn(nn, dp_refs[k][...])

        @pl.when(i == pl.num_programs(0) - 1)
        def _():
            for k in range(npc):
                o_refs[k][...] = accs[k][...].astype(BF16)

    return pl.pallas_call(
        body, name="proj_bwd_w",
        grid=(tp // tm,),
        in_specs=[pl.BlockSpec((tm, d), lambda i: (i, 0))]
        + [pl.BlockSpec((tm, w), lambda i: (i, 0)) for w in widths],
        out_specs=[pl.BlockSpec(oshape(w), lambda i: (0, 0)) for w in widths],
        out_shape=[jax.ShapeDtypeStruct(oshape(w), BF16) for w in widths],
        scratch_shapes=[pltpu.VMEM(oshape(w), F32) for w in widths],
        compiler_params=_cparams(1),
    )(n, *dp_pieces)


def _out_fwd(h, ya, yb, w, tm):
    tp, d = h.shape
    na, nb = ya.shape[1], yb.shape[1]

    def body(h_ref, ya_ref, yb_ref, w_ref, o_ref):
        y = _dot(ya_ref[...].astype(BF16), w_ref[0:na, :]) + _dot(yb_ref[...].astype(BF16), w_ref[na:, :])
        o_ref[...] = h_ref[...] + y

    return pl.pallas_call(
        body, name="out_fwd",
        grid=(tp // tm,),
        in_specs=[pl.BlockSpec((tm, d), lambda i: (i, 0)), pl.BlockSpec((tm, na), lambda i: (i, 0)),
                  pl.BlockSpec((tm, nb), lambda i: (i, 0)), pl.BlockSpec((d, d), lambda i: (0, 0))],
        out_specs=pl.BlockSpec((tm, d), lambda i: (i, 0)),
        out_shape=jax.ShapeDtypeStruct((tp, d), F32),
        compiler_params=_cparams(1),
    )(h, ya, yb, w)


def _out_bwd(dy, ya, yb, w, tm, after=()):
    tp, d = dy.shape
    na, nb = ya.shape[1], yb.shape[1]

    def body(dy_ref, ya_ref, yb_ref, w_ref, *rest):
        da_ref, db_ref, dw_ref, acc = rest[len(after):]
        i = pl.program_id(0)

        @pl.when(i == 0)
        def _():
            acc[...] = jnp.zeros_like(acc)

        dyb16 = dy_ref[...].astype(BF16)
        da_ref[...] = _dot_nt(dyb16, w_ref[0:na, :])
        db_ref[...] = _dot_nt(dyb16, w_ref[na:, :])
        acc[0:na, :] += _dot_tn(ya_ref[...].astype(BF16), dyb16)
        acc[na:, :] += _dot_tn(yb_ref[...].astype(BF16), dyb16)

        @pl.when(i == pl.num_programs(0) - 1)
        def _():
            dw_ref[...] = acc[...].astype(BF16)

    return pl.pallas_call(
        body, name="out_bwd",
        grid=(tp // tm,),
        in_specs=[pl.BlockSpec((tm, d), lambda i: (i, 0)), pl.BlockSpec((tm, na), lambda i: (i, 0)),
                  pl.BlockSpec((tm, nb), lambda i: (i, 0)), pl.BlockSpec((d, d), lambda i: (0, 0))]
        + [pl.BlockSpec(memory_space=pl.ANY)] * len(after),
        out_specs=[pl.BlockSpec((tm, na), lambda i: (i, 0)), pl.BlockSpec((tm, nb), lambda i: (i, 0)),
                   pl.BlockSpec((d, d), lambda i: (0, 0))],
        out_shape=[jax.ShapeDtypeStruct((tp, na), F32), jax.ShapeDtypeStruct((tp, nb), F32),
                   jax.ShapeDtypeStruct((d, d), BF16)],
        scratch_shapes=[pltpu.VMEM((d, d), F32)],
        compiler_params=_cparams(1),
    )(dy, ya, yb, w, *after)


def _loss_bwd(h, gfin, tgt, t_real, tm):
    tp, d = h.shape

    def body(h_ref, g_ref, t_ref, loss_ref, dh_ref, dg_ref):
        i = pl.program_id(0)

        @pl.when(i == 0)
        def _():
            loss_ref[...] = jnp.zeros_like(loss_ref)
            dg_ref[...] = jnp.zeros_like(dg_ref)

        rows = i * tm + lax.broadcasted_iota(jnp.int32, (tm, 1), 0)
        valid = (rows >= N_META) & (rows < t_real)
        rstd, xhat = _rms_stats(h_ref[...])
        g = g_ref[...]
        err = jnp.where(valid, xhat * g - t_ref[...], 0.0)
        e2 = jnp.sum(err * err, axis=1, keepdims=True)
        loss_ref[...] += (0.5 / d) * jnp.sum(e2, axis=0, keepdims=True)
        dy = err * (1.0 / d)
        dh, dg = _rms_bwd(dy, g, rstd, xhat)
        dh_ref[...] = dh
        dg_ref[...] += dg

    return pl.pallas_call(
        body, name="loss_bwd",
        grid=(tp // tm,),
        in_specs=[pl.BlockSpec((tm, d), lambda i: (i, 0)), pl.BlockSpec((1, d), lambda i: (0, 0)),
                  pl.BlockSpec((tm, d), lambda i: (i, 0))],
        out_specs=[pl.BlockSpec((1, 1), lambda i: (0, 0)), pl.BlockSpec((tm, d), lambda i: (i, 0)),
                   pl.BlockSpec((1, d), lambda i: (0, 0))],
        out_shape=[jax.ShapeDtypeStruct((1, 1), F32), jax.ShapeDtypeStruct((tp, d), F32),
                   jax.ShapeDtypeStruct((1, d), F32)],
        compiler_params=_cparams(1),
    )(h, gfin, tgt)


POOL_HALO = 16


def _pool_lane_consts(n_rows):
    lane = lax.broadcasted_iota(jnp.int32, (n_rows, D_POOL), 1)
    grp = lane // POOL_GROUP
    win = jnp.where(grp == 0, 2.0, jnp.where(grp == 1, 4.0, jnp.where(grp == 2, 8.0, 16.0)))
    return grp, win


def _pool_select(grp, s2, s4, s8, s16):
    return jnp.where(grp == 0, s2, jnp.where(grp == 1, s4, jnp.where(grp == 2, s8, s16)))


def _pool_mixed(x, row0, tr):
    n = tr + POOL_HALO
    s2 = x + pltpu.roll(x, 1, 0)
    s4 = s2 + pltpu.roll(s2, 2, 0)
    s8 = s4 + pltpu.roll(s4, 4, 0)
    s16 = s8 + pltpu.roll(s8, 8, 0)
    grp, win = _pool_lane_consts(n)
    rows = row0 - POOL_HALO + lax.broadcasted_iota(jnp.int32, (n, D_POOL), 0)
    cnt = jnp.minimum((rows + 1).astype(F32), win)
    pooled = _pool_select(grp, s2, s4, s8, s16) / jnp.maximum(cnt, 1.0)
    return (pooled - x)[POOL_HALO:, :]


def _pool_fwd(p, wbd, scale, tr):
    tp = p.shape[0]
    nt = tp // tr

    def body(p_ref, w_ref, s_ref, y_ref, usc):
        usc[0:POOL_HALO, :] = jnp.zeros((POOL_HALO, D_POOL), F32)
        usc[POOL_HALO:, :] = p_ref[...]

        def tile(r, carry):
            r0 = pl.multiple_of(r * tr, SUBLANE)
            x = usc[pl.ds(r0, tr + POOL_HALO), :]
            mixed = _pool_mixed(x, r0, tr)
            y_ref[pl.ds(r0, tr), :] = _dot(mixed.astype(BF16), w_ref[...]) * s_ref[...]
            return carry

        lax.fori_loop(0, nt, tile, 0)

    return pl.pallas_call(
        body, name="pool_fwd",
        grid=(1,),
        in_specs=[pl.BlockSpec((tp, D_POOL), lambda i: (0, 0)), pl.BlockSpec((D_POOL, D_POOL), lambda i: (0, 0)),
                  pl.BlockSpec((1, D_POOL), lambda i: (0, 0))],
        out_specs=pl.BlockSpec((tp, D_POOL), lambda i: (0, 0)),
        out_shape=jax.ShapeDtypeStruct((tp, D_POOL), F32),
        scratch_shapes=[pltpu.VMEM((tp + POOL_HALO, D_POOL), F32)],
        compiler_params=_cparams(1),
    )(p, wbd, scale)


def _pool_bwd(p, dya, wbd, scale, tr):
    tp = p.shape[0]
    nt = tp // tr

    def body(p_ref, dy_ref, w_ref, s_ref, du_ref, dw_ref, ds_ref, usc, gsc):
        usc[0:POOL_HALO, :] = jnp.zeros((POOL_HALO, D_POOL), F32)
        usc[POOL_HALO:, :] = p_ref[...]
        gsc[tp:, :] = jnp.zeros((POOL_HALO, D_POOL), F32)
        dw_ref[...] = jnp.zeros_like(dw_ref)
        ds_ref[...] = jnp.zeros_like(ds_ref)
        grp, win = _pool_lane_consts(tr)

        def tile1(r, carry):
            r0 = pl.multiple_of(r * tr, SUBLANE)
            x = usc[pl.ds(r0, tr + POOL_HALO), :]
            mixed = _pool_mixed(x, r0, tr).astype(BF16)
            dy = dy_ref[pl.ds(r0, tr), :]
            dys = (dy * s_ref[...]).astype(BF16)
            ypre = _dot(mixed, w_ref[...])
            ds_ref[...] += _colsum(dy * ypre)
            dw_ref[...] += _dot_tn(mixed, dys)
            dmx = _dot_nt(dys, w_ref[...])
            rows = r0 + lax.broadcasted_iota(jnp.int32, (tr, D_POOL), 0)
            cnt = jnp.minimum((rows + 1).astype(F32), win)
            gsc[pl.ds(r0, tr), :] = dmx / cnt
            return carry

        lax.fori_loop(0, nt, tile1, 0)
        n = tr + POOL_HALO
        grp2, win2 = _pool_lane_consts(n)

        def tile2(r, carry):
            r0 = pl.multiple_of(r * tr, SUBLANE)
            g = gsc[pl.ds(r0, n), :]
            s2 = g + pltpu.roll(g, n - 1, 0)
            s4 = s2 + pltpu.roll(s2, n - 2, 0)
            s8 = s4 + pltpu.roll(s4, n - 4, 0)
            s16 = s8 + pltpu.roll(s8, n - 8, 0)
            pooled_t = _pool_select(grp2, s2, s4, s8, s16)
            rows = r0 + lax.broadcasted_iota(jnp.int32, (n, D_POOL), 0)
            cnt = jnp.minimum((rows + 1).astype(F32), win2)
            du = pooled_t - g * cnt
            du_ref[pl.ds(r0, tr), :] = du[0:tr, :].astype(BF16)
            return carry

        lax.fori_loop(0, nt, tile2, 0)

    return pl.pallas_call(
        body, name="pool_bwd",
        grid=(1,),
        in_specs=[pl.BlockSpec((tp, D_POOL), lambda i: (0, 0)), pl.BlockSpec((tp, D_POOL), lambda i: (0, 0)),
                  pl.BlockSpec((D_POOL, D_POOL), lambda i: (0, 0)), pl.BlockSpec((1, D_POOL), lambda i: (0, 0))],
        out_specs=[pl.BlockSpec((tp, D_POOL), lambda i: (0, 0)), pl.BlockSpec((D_POOL, D_POOL), lambda i: (0, 0)),
                   pl.BlockSpec((1, D_POOL), lambda i: (0, 0))],
        out_shape=[jax.ShapeDtypeStruct((tp, D_POOL), BF16), jax.ShapeDtypeStruct((D_POOL, D_POOL), F32),
                   jax.ShapeDtypeStruct((1, D_POOL), F32)],
        scratch_shapes=[pltpu.VMEM((tp + POOL_HALO, D_POOL), F32), pltpu.VMEM((tp + POOL_HALO, D_POOL), F32)],
        compiler_params=_cparams(1),
    )(p, dya, wbd, scale)


def _hgrn_levels(ch):
    levels = []
    w = ch // 2
    while w >= 1:
        levels.append(w)
        w //= 2
    return levels


def _hgrn_consts(ch):
    t = np.arange(ch)
    tril = t[None, :] <= t[:, None]
    masks = []
    for w in _hgrn_levels(ch):
        blk = t // (2 * w)
        upper = t % (2 * w) >= w
        masks.append(upper[:, None] & (~upper)[None, :] & (blk[:, None] == blk[None, :]))
    masks.append(tril)
    msk = np.stack(masks).astype(np.float32)
    return jnp.asarray(tril.astype(np.float32), BF16), jnp.asarray(msk, F32), len(masks) - 1


def _split3(x):
    hi = x.astype(BF16)
    r1 = x - hi.astype(F32)
    mid = r1.astype(BF16)
    lo = (r1 - mid.astype(F32)).astype(BF16)
    return hi, mid, lo


def _hgrn_exponents(tril, logf):
    ch = logf.shape[0]
    hi, mid, lo = _split3(logf)
    x = _dot(tril, jnp.concatenate([hi, mid, lo], axis=1))
    b = x[:, 0:HEAD] + x[:, HEAD:2 * HEAD] + x[:, 2 * HEAD:3 * HEAD]
    rows = lax.broadcasted_iota(jnp.int32, (ch, HEAD), 0)
    fx = jnp.broadcast_to(b[ch - 1:ch, :], (ch, HEAD)) - b
    lev = []
    for w in _hgrn_levels(ch):
        pos = rows % (2 * w)
        upper = pos >= w
        if w >= SUBLANE:
            parts = [jnp.broadcast_to(b[k * 2 * w + w - 1:k * 2 * w + w, :], (2 * w, HEAD))
                     for k in range(ch // (2 * w))]
            bmid = parts[0] if len(parts) == 1 else jnp.concatenate(parts, axis=0)
            dx = jnp.where(upper, b - bmid, 0.0)
            ex = jnp.where(upper, 0.0, bmid - b)
        else:
            dx = logf
            ex = jnp.zeros_like(logf)
            for i in range(1, w):
                dx = dx + jnp.where(pos >= w + i, pltpu.roll(logf, i, 0), 0.0)
                ex = ex + jnp.where(pos <= w - 1 - i, pltpu.roll(logf, ch - i, 0), 0.0)
            dx = jnp.where(upper, dx, 0.0)
        lev.append((dx, ex))
    return b, fx, lev


def _hgrn_exponents_bwd(tril, d_b, d_fx, d_blast, lev_grads):
    ch = d_b.shape[0]
    rows = lax.broadcasted_iota(jnp.int32, (ch, HEAD), 0)
    db = d_b - d_fx
    dlf = jnp.zeros_like(d_b)
    for w, (ddx, dex) in zip(_hgrn_levels(ch), lev_grads):
        pos = rows % (2 * w)
        upper = pos >= w
        gu = jnp.where(upper, ddx, 0.0)
        if w >= SUBLANE:
            gl = jnp.where(upper, 0.0, dex)
            db = db + gu - gl
            diff = gl - gu
            for k in range(ch // (2 * w)):
                s = _colsum(diff[k * 2 * w:(k + 1) * 2 * w, :])
                db = db + jnp.where(rows == k * 2 * w + w - 1, s, 0.0)
        else:
            dlf = dlf + gu
            for i in range(1, w):
                dlf = dlf + pltpu.roll(jnp.where(pos >= w + i, gu, 0.0), ch - i, 0)
                dlf = dlf + pltpu.roll(jnp.where(pos <= w - 1 - i, dex, 0.0), i, 0)
    db = db + jnp.where(rows == ch - 1, _colsum(d_fx) + d_blast, 0.0)
    hi = db.astype(BF16)
    lo = (db - hi.astype(F32)).astype(BF16)
    d2 = _dot_tn(tril, jnp.concatenate([hi, lo], axis=1))
    return d2[:, 0:HEAD] + d2[:, HEAD:2 * HEAD] + dlf


def _hgrn_gates(q_raw, z, lb):
    sz = _sigmoid(z)
    f = lb + (1.0 - lb) * sz
    q = q_raw * _sigmoid(q_raw)
    k = (1.0 - lb) * (1.0 - sz)
    return q, k, f, sz


def _hgrn_intra(q, k, lev, msk_ref, n_lev, ch):
    eye = (lax.broadcasted_iota(jnp.int32, (ch, ch), 0) == lax.broadcasted_iota(jnp.int32, (ch, ch), 1))
    a = jnp.where(eye, jnp.sum(q * k, axis=1, keepdims=True), 0.0)
    ops = []
    for lv in range(n_lev):
        eq = jnp.exp(lev[lv][0])
        ek = jnp.exp(lev[lv][1])
        qd = q * eq
        kd = k * ek
        a = a + msk_ref[lv] * _dot_nt(qd.astype(BF16), kd.astype(BF16))
        ops.append((eq, ek, qd, kd))
    return a, ops


def _hgrn_fwd(p, lb_logits, gnorm, mst, msk, n_lev, tm):
    tp = p.shape[0]
    ch = HG_CHUNK
    nct = tm // ch
    nt = tp // tm
    nr = mst.shape[0]
    base = D_POOL // HEAD

    def body(q_ref, z_ref, v_ref, g_ref, lg_ref, gn_ref, mst_ref, msk_ref, y_ref, ss_ref, st_sc):
        @pl.when(pl.program_id(1) == 0)
        def _():
            st_sc[...] = jnp.zeros_like(st_sc)

        lb_all = _sigmoid(lg_ref[0:1, :] - lg_ref[1:2, :])

        def one_head(hh, c, r0):
            ls = slice(hh * HEAD, (hh + 1) * HEAD)
            lb = lb_all[:, ls]
            q, k, f, _ = _hgrn_gates(q_ref[pl.ds(r0, ch), ls], z_ref[pl.ds(r0, ch), ls], lb)
            v = v_ref[pl.ds(r0, ch), ls]
            b, fx, lev = _hgrn_exponents(mst_ref[...], jnp.log(f))
            st = st_sc[hh]
            ss_ref[hh, c] = st
            qe = q * jnp.exp(b)
            a, _ = _hgrn_intra(q, k, lev, msk_ref, n_lev, ch)
            v16 = v.astype(BF16)
            o = _dot_nt(qe.astype(BF16), st.astype(BF16)) + _dot(a.astype(BF16), v16)
            kl = k * jnp.exp(fx)
            st_sc[hh] = st * jnp.exp(b[ch - 1:ch, :]) + _dot_tn(v16, kl.astype(BF16))
            rstd = lax.rsqrt(jnp.mean(o * o, axis=-1, keepdims=True) + EPS)
            g_raw = g_ref[pl.ds(r0, ch), ls]
            y_ref[pl.ds(r0, ch), ls] = o * rstd * gn_ref[...] * (g_raw * _sigmoid(g_raw))

        def chunk(c, carry):
            r0 = pl.multiple_of(c * ch, ch)
            for hh in range(HG_HEADS_PER_STEP):
                one_head(hh, c, r0)
            return carry

        lax.fori_loop(0, nct, chunk, 0, unroll=HG_UNROLL)

    hp = HG_HEADS_PER_STEP
    wide = hp * HEAD

    def pspec(seg):
        return pl.BlockSpec((tm, wide), lambda h, i: (i, (base + seg * HG_HEADS) // hp + h))

    return pl.pallas_call(
        body, name="hgrn_fwd",
        grid=(HG_HEADS // hp, nt),
        in_specs=[pspec(0), pspec(1), pspec(2), pspec(3),
                  pl.BlockSpec((2, wide), lambda h, i: (0, h)),
                  pl.BlockSpec((1, HEAD), lambda h, i: (0, 0)),
                  pl.BlockSpec((nr, ch), lambda h, i: (0, 0)),
                  pl.BlockSpec((n_lev + 1, ch, ch), lambda h, i: (0, 0, 0))],
        out_specs=[pl.BlockSpec((tm, wide), lambda h, i: (i, h)),
                   pl.BlockSpec((hp, nct, HEAD, HEAD), lambda h, i: (h, i, 0, 0))],
        out_shape=[jax.ShapeDtypeStruct((tp, D_HGRN), F32),
                   jax.ShapeDtypeStruct((HG_HEADS, tp // ch, HEAD, HEAD), F32)],
        scratch_shapes=[pltpu.VMEM((hp, HEAD, HEAD), F32)],
        compiler_params=_cparams(2),
    )(p, p, p, p, lb_logits, gnorm, mst, msk)


def _hgrn_bwd(p, dyb, states, lb_logits, gnorm, mst, msk, n_lev, tm):
    tp = p.shape[0]
    ch = HG_CHUNK
    nct = tm // ch
    nt = tp // tm
    nr = mst.shape[0]
    base = D_POOL // HEAD

    def body(q_ref, z_ref, v_ref, g_ref, dy_ref, ss_ref, lg_ref, gn_ref, mst_ref, msk_ref,
             dq_ref, dz_ref, dv_ref, dg_ref, dlg_ref, dgn_ref, dst_sc, dlb_sc):
        ti = pl.program_id(1)

        @pl.when(ti == 0)
        def _():
            dst_sc[...] = jnp.zeros_like(dst_sc)
            dlb_sc[...] = jnp.zeros_like(dlb_sc)
            dgn_ref[...] = jnp.zeros_like(dgn_ref)

        lb_all = _sigmoid(lg_ref[0:1, :] - lg_ref[1:2, :])
        gn = gn_ref[...]

        def one_head(hh, c, r0):
            ls = slice(hh * HEAD, (hh + 1) * HEAD)
            lb = lb_all[:, ls]
            q_raw = q_ref[pl.ds(r0, ch), ls]
            z = z_ref[pl.ds(r0, ch), ls]
            q, k, f, sz = _hgrn_gates(q_raw, z, lb)
            v = v_ref[pl.ds(r0, ch), ls]
            b, fx, lev = _hgrn_exponents(mst_ref[...], jnp.log(f))
            st = ss_ref[hh, c]
            eb = jnp.exp(b)
            ef = jnp.exp(fx)
            elast = jnp.exp(b[ch - 1:ch, :])
            qe = q * eb
            kl = k * ef
            a, ops = _hgrn_intra(q, k, lev, msk_ref, n_lev, ch)
            v16 = v.astype(BF16)
            st16 = st.astype(BF16)
            qe16 = qe.astype(BF16)
            kl16 = kl.astype(BF16)
            a16 = a.astype(BF16)
            o = _dot_nt(qe16, st16) + _dot(a16, v16)
            g_raw = g_ref[pl.ds(r0, ch), ls]
            sg = _sigmoid(g_raw)
            rstd = lax.rsqrt(jnp.mean(o * o, axis=-1, keepdims=True) + EPS)
            oh = o * rstd
            dy = dy_ref[pl.ds(r0, ch), ls]
            dg_ref[pl.ds(r0, ch), ls] = (dy * oh * gn * (sg * (1.0 + g_raw * (1.0 - sg)))).astype(BF16)
            don = dy * (g_raw * sg)
            dgn_ref[hh] += _colsum(don * oh)
            doh = don * gn
            do = rstd * (doh - oh * jnp.mean(doh * oh, axis=-1, keepdims=True))
            do16 = do.astype(BF16)
            dst = dst_sc[hh]
            dst16 = dst.astype(BF16)
            dv = _dot_tn(a16, do16) + _dot_nt(kl16, dst16)
            da = msk_ref[n_lev] * _dot_nt(do16, v16)
            dqe = _dot(do16, st16)
            dkl = _dot(v16, dst16)
            dst_sc[hh] = dst * elast + _dot_tn(do16, qe16)
            db_last = _colsum(dst * st) * elast
            dad = jnp.sum(do * v, axis=1, keepdims=True)
            dq = dad * k + dqe * eb
            dk = dad * q + dkl * ef
            lev_grads = []
            for lv in range(n_lev):
                eq, ek, qd, kd = ops[lv]
                gl = (msk_ref[lv] * da).astype(BF16)
                dqd = _dot(gl, kd.astype(BF16))
                dkd = _dot_tn(gl, qd.astype(BF16))
                dq = dq + dqd * eq
                dk = dk + dkd * ek
                lev_grads.append((dqd * qd, dkd * kd))
            dlogf = _hgrn_exponents_bwd(mst_ref[...], dqe * qe, dkl * kl, db_last, lev_grads)
            sq = _sigmoid(q_raw)
            dq_ref[pl.ds(r0, ch), ls] = (dq * (sq * (1.0 + q_raw * (1.0 - sq)))).astype(BF16)
            dfk = dlogf / f - dk
            dz_ref[pl.ds(r0, ch), ls] = (dfk * (1.0 - lb) * sz * (1.0 - sz)).astype(BF16)
            dlb_sc[:, ls] += _colsum(dfk * (1.0 - sz))
            dv_ref[pl.ds(r0, ch), ls] = dv.astype(BF16)

        def chunk(cc, carry):
            c = nct - 1 - cc
            r0 = pl.multiple_of(c * ch, ch)
            for hh in range(HG_HEADS_PER_STEP):
                one_head(hh, c, r0)
            return carry

        lax.fori_loop(0, nct, chunk, 0, unroll=HG_UNROLL)

        @pl.when(ti == nt - 1)
        def _():
            dl0 = dlb_sc[...] * lb_all * (1.0 - lb_all)
            dlg_ref[0:1, :] = dl0
            dlg_ref[1:2, :] = -dl0

    hp = HG_HEADS_PER_STEP
    wide = hp * HEAD

    def pspec(seg):
        return pl.BlockSpec((tm, wide), lambda h, i: (nt - 1 - i, (base + seg * HG_HEADS) // hp + h))

    ospec = pl.BlockSpec((tm, wide), lambda h, i: (nt - 1 - i, h))
    return pl.pallas_call(
        body, name="hgrn_bwd",
        grid=(HG_HEADS // hp, nt),
        in_specs=[pspec(0), pspec(1), pspec(2), pspec(3), ospec,
                  pl.BlockSpec((hp, nct, HEAD, HEAD), lambda h, i: (h, nt - 1 - i, 0, 0)),
                  pl.BlockSpec((2, wide), lambda h, i: (0, h)),
                  pl.BlockSpec((1, HEAD), lambda h, i: (0, 0)),
                  pl.BlockSpec((nr, ch), lambda h, i: (0, 0)),
                  pl.BlockSpec((n_lev + 1, ch, ch), lambda h, i: (0, 0, 0))],
        out_specs=[ospec, ospec, ospec, ospec,
                   pl.BlockSpec((2, wide), lambda h, i: (0, h)),
                   pl.BlockSpec((hp, 1, HEAD), lambda h, i: (h, 0, 0))],
        out_shape=[jax.ShapeDtypeStruct((tp, D_HGRN), BF16)] * 4
        + [jax.ShapeDtypeStruct((2, D_HGRN), F32), jax.ShapeDtypeStruct((HG_HEADS, 1, HEAD), F32)],
        scratch_shapes=[pltpu.VMEM((hp, HEAD, HEAD), F32), pltpu.VMEM((1, wide), F32)],
        compiler_params=_cparams(2),
    )(p, p, p, p, dyb, states, lb_logits, gnorm, mst, msk)


def _conv_taps(x, w_ref, tr, halo, width):
    acc = None
    for j in range(width):
        sh = width - 1 - j
        xs = x if sh == 0 else pltpu.roll(x, sh, 0)
        term = xs[halo:, :] * w_ref[j:j + 1, :]
        acc = term if acc is None else acc + term
    return acc


def _conv_taps_t(y, w_ref, tr, halo, width):
    n = tr + halo
    acc = None
    for j in range(width):
        sh = width - 1 - j
        ys = y if sh == 0 else pltpu.roll(y, n - sh, 0)
        term = ys[0:tr, :] * w_ref[j:j + 1, :]
        acc = term if acc is None else acc + term
    return acc


def _ln_stats(cv):
    mu = jnp.mean(cv, axis=-1, keepdims=True)
    xc = cv - mu
    rstd = lax.rsqrt(jnp.mean(xc * xc, axis=-1, keepdims=True) + EPS)
    return rstd, xc * rstd


def _convmod_fwd(p, w, bias, ln_g, ln_b, tr):
    tp = p.shape[0]
    nt = tp // tr
    nb = D_CONV // HEAD

    def body(a_ref, b_ref, w_ref, bi_ref, g_ref, be_ref, y_ref, usc):
        usc[0:CONV_HALO, :] = jnp.zeros((CONV_HALO, HEAD), F32)
        usc[CONV_HALO:, :] = a_ref[...] * _sigmoid(b_ref[...])

        def tile(r, carry):
            r0 = pl.multiple_of(r * tr, SUBLANE)
            x = usc[pl.ds(r0, tr + CONV_HALO), :]
            cv = _conv_taps(x, w_ref, tr, CONV_HALO, CONV_WIDTH) + bi_ref[...]
            _, xh = _ln_stats(cv)
            un = xh * g_ref[...] + be_ref[...]
            y_ref[pl.ds(r0, tr), :] = un * _sigmoid(un)
            return carry

        lax.fori_loop(0, nt, tile, 0)

    vec = lambda: pl.BlockSpec((1, HEAD), lambda j: (0, j))
    return pl.pallas_call(
        body, name="convmod_fwd",
        grid=(nb,),
        in_specs=[pl.BlockSpec((tp, HEAD), lambda j: (0, j)), pl.BlockSpec((tp, HEAD), lambda j: (0, nb + j)),
                  pl.BlockSpec((CONV_HALO, HEAD), lambda j: (0, j)), vec(), vec(), vec()],
        out_specs=pl.BlockSpec((tp, HEAD), lambda j: (0, j)),
        out_shape=jax.ShapeDtypeStruct((tp, D_CONV), F32),
        scratch_shapes=[pltpu.VMEM((tp + CONV_HALO, HEAD), F32)],
        compiler_params=_cparams(1),
    )(p, p, w, bias, ln_g, ln_b)


def _convmod_bwd(p, dyc, w, bias, ln_g, ln_b, tr):
    tp = p.shape[0]
    nt = tp // tr
    nb = D_CONV // HEAD

    def body(a_ref, b_ref, dy_ref, w_ref, bi_ref, g_ref, be_ref, da_ref, db_ref, dw_ref, dv_ref, usc, dsc):
        usc[0:CONV_HALO, :] = jnp.zeros((CONV_HALO, HEAD), F32)
        usc[CONV_HALO:, :] = a_ref[...] * _sigmoid(b_ref[...])
        dsc[tp:, :] = jnp.zeros((CONV_HALO, HEAD), F32)
        dw_ref[...] = jnp.zeros_like(dw_ref)
        dv_ref[...] = jnp.zeros_like(dv_ref)

        def tile1(r, carry):
            r0 = pl.multiple_of(r * tr, SUBLANE)
            x = usc[pl.ds(r0, tr + CONV_HALO), :]
            cv = _conv_taps(x, w_ref, tr, CONV_HALO, CONV_WIDTH) + bi_ref[...]
            rstd, xh = _ln_stats(cv)
            un = xh * g_ref[...] + be_ref[...]
            sg = _sigmoid(un)
            dun = dy_ref[pl.ds(r0, tr), :] * (sg * (1.0 + un * (1.0 - sg)))
            dv_ref[0, 1:2, :] += _colsum(dun * xh)
            dv_ref[0, 2:3, :] += _colsum(dun)
            dxh = dun * g_ref[...]
            dcv = rstd * (dxh - jnp.mean(dxh, axis=-1, keepdims=True)
                          - xh * jnp.mean(dxh * xh, axis=-1, keepdims=True))
            dv_ref[0, 0:1, :] += _colsum(dcv)
            for j in range(CONV_WIDTH):
                sh = CONV_WIDTH - 1 - j
                xs = x if sh == 0 else pltpu.roll(x, sh, 0)
                dw_ref[0, j:j + 1, :] += _colsum(dcv * xs[CONV_HALO:, :])
            dsc[pl.ds(r0, tr), :] = dcv
            return carry

        lax.fori_loop(0, nt, tile1, 0)

        def tile2(r, carry):
            r0 = pl.multiple_of(r * tr, SUBLANE)
            y = dsc[pl.ds(r0, tr + CONV_HALO), :]
            du = _conv_taps_t(y, w_ref, tr, CONV_HALO, CONV_WIDTH)
            a = a_ref[pl.ds(r0, tr), :]
            sb = _sigmoid(b_ref[pl.ds(r0, tr), :])
            da_ref[pl.ds(r0, tr), :] = (du * sb).astype(BF16)
            db_ref[pl.ds(r0, tr), :] = (du * a * sb * (1.0 - sb)).astype(BF16)
            return carry

        lax.fori_loop(0, nt, tile2, 0)

    vec = lambda: pl.BlockSpec((1, HEAD), lambda j: (0, j))
    col = lambda: pl.BlockSpec((tp, HEAD), lambda j: (0, j))
    return pl.pallas_call(
        body, name="convmod_bwd",
        grid=(nb,),
        in_specs=[col(), pl.BlockSpec((tp, HEAD), lambda j: (0, nb + j)), col(),
                  pl.BlockSpec((CONV_HALO, HEAD), lambda j: (0, j)), vec(), vec(), vec()],
        out_specs=[col(), col(), pl.BlockSpec((1, CONV_HALO, HEAD), lambda j: (j, 0, 0)),
                   pl.BlockSpec((1, SUBLANE, HEAD), lambda j: (j, 0, 0))],
        out_shape=[jax.ShapeDtypeStruct((tp, D_CONV), BF16), jax.ShapeDtypeStruct((tp, D_CONV), BF16),
                   jax.ShapeDtypeStruct((nb, CONV_HALO, HEAD), F32), jax.ShapeDtypeStruct((nb, SUBLANE, HEAD), F32)],
        scratch_shapes=[pltpu.VMEM((tp + CONV_HALO, HEAD), F32), pltpu.VMEM((tp + CONV_HALO, HEAD), F32)],
        compiler_params=_cparams(1),
    )(p, p, dyc, w, bias, ln_g, ln_b)


def _log1p_small(y):
    return jnp.where(y < 1e-4, y * (1.0 - 0.5 * y), jnp.log(1.0 + y))


def _softplus(x):
    return jnp.maximum(x, 0.0) + _log1p_small(jnp.exp(-jnp.abs(x)))


def _expm1(x):
    return jnp.where(jnp.abs(x) < 1e-2, x * (1.0 + 0.5 * x * (1.0 + x * (1.0 / 3.0))), jnp.exp(x) - 1.0)


def _gelu_parts(x):
    c = 0.7978845608028654
    inner = c * (x + 0.044715 * x * x * x)
    th = jnp.tanh(inner)
    gelu = 0.5 * x * (1.0 + th)
    dgelu = 0.5 * (1.0 + th) + 0.5 * x * (1.0 - th * th) * c * (1.0 + 3.0 * 0.044715 * x * x)
    return gelu, dgelu


def _lru_gates(x_all, tp, cw_ref, cb_ref, wa_ref, ba_ref, wx_ref, bx_ref, lam_ref):
    u = _conv_taps(x_all, cw_ref, tp, LRU_HALO, LRU_CONV) + cb_ref[...]
    u16 = u.astype(BF16)
    r = _sigmoid(_dot(u16, wa_ref[0]) + ba_ref[...])
    i = _sigmoid(_dot(u16, wx_ref[0]) + bx_ref[...])
    sp = _softplus(-lam_ref[...])
    la = -LRU_C * r * sp
    a = jnp.exp(la)
    mult = jnp.sqrt(-_expm1(2.0 * la))
    return u, r, i, a, mult, sp


def _lru_specs(tp, nb):
    col = lambda k: pl.BlockSpec((tp, HEAD), functools.partial(lambda j, k: (0, k * nb + j), k=k))
    vec = lambda: pl.BlockSpec((1, HEAD), lambda j: (0, j))
    mat = lambda: pl.BlockSpec((1, HEAD, HEAD), lambda j: (j, 0, 0))
    return col, vec, mat


def _lru_fwd(p, cw, cb, wa, ba, wx, bx, lam):
    tp = p.shape[0]
    nb = D_LRU // HEAD
    ng = tp // SUBLANE

    def body(x_ref, gt_ref, cw_ref, cb_ref, wa_ref, ba_ref, wx_ref, bx_ref, lam_ref, y_ref, hs_ref,
             xsc, asc, bsc):
        xsc[0:LRU_HALO, :] = jnp.zeros((LRU_HALO, HEAD), F32)
        xsc[LRU_HALO:, :] = x_ref[...]
        u, r, i, a, mult, _ = _lru_gates(xsc[...], tp, cw_ref, cb_ref, wa_ref, ba_ref, wx_ref, bx_ref, lam_ref)
        rows = lax.broadcasted_iota(jnp.int32, (tp, HEAD), 0)
        b = jnp.where(rows == 0, 1.0, mult) * (i * u)
        sub = rows % SUBLANE
        for k in (1, 2, 4):
            m = sub >= k
            b = jnp.where(m, a * pltpu.roll(b, k, 0) + b, b)
            a = jnp.where(m, a * pltpu.roll(a, k, 0), a)
        asc[...] = a
        bsc[...] = b

        def grp(g, carry):
            r0 = pl.multiple_of(g * SUBLANE, SUBLANE)
            h = bsc[pl.ds(r0, SUBLANE), :] + asc[pl.ds(r0, SUBLANE), :] * carry
            hs_ref[pl.ds(r0, SUBLANE), :] = h
            return jnp.broadcast_to(h[SUBLANE - 1:SUBLANE, :], (SUBLANE, HEAD))

        lax.fori_loop(0, ng, grp, jnp.zeros((SUBLANE, HEAD), F32))
        gelu, _ = _gelu_parts(gt_ref[...])
        y_ref[...] = gelu * hs_ref[...]

    col, vec, mat = _lru_specs(tp, nb)
    return pl.pallas_call(
        body, name="lru_fwd",
        grid=(nb,),
        in_specs=[col(2), col(3), pl.BlockSpec((LRU_CONV, HEAD), lambda j: (0, j)), vec(), mat(), vec(), mat(),
                  vec(), vec()],
        out_specs=[pl.BlockSpec((tp, HEAD), lambda j: (0, j)), pl.BlockSpec((tp, HEAD), lambda j: (0, j))],
        out_shape=[jax.ShapeDtypeStruct((tp, D_LRU), F32), jax.ShapeDtypeStruct((tp, D_LRU), F32)],
        scratch_shapes=[pltpu.VMEM((tp + LRU_HALO, HEAD), F32), pltpu.VMEM((tp, HEAD), F32),
                        pltpu.VMEM((tp, HEAD), F32)],
        compiler_params=_cparams(1),
    )(p, p, cw, cb, wa, ba, wx, bx, lam)


def _lru_bwd(p, hs, dyd, cw, cb, wa, ba, wx, bx, lam):
    tp = p.shape[0]
    nb = D_LRU // HEAD
    ng = tp // SUBLANE

    def body(x_ref, gt_ref, hs_ref, dy_ref, cw_ref, cb_ref, wa_ref, ba_ref, wx_ref, bx_ref, lam_ref,
             dx_ref, dgt_ref, dwa_ref, dwx_ref, dv_ref, xsc, asc, bsc, gsc, dusc):
        xsc[0:LRU_HALO, :] = jnp.zeros((LRU_HALO, HEAD), F32)
        xsc[LRU_HALO:, :] = x_ref[...]
        x_all = xsc[...]
        u, r, i, a, mult, sp = _lru_gates(x_all, tp, cw_ref, cb_ref, wa_ref, ba_ref, wx_ref, bx_ref, lam_ref)
        rows = lax.broadcasted_iota(jnp.int32, (tp, HEAD), 0)
        hs = hs_ref[...]
        dy = dy_ref[...]
        gelu, dgelu = _gelu_parts(gt_ref[...])
        dgt_ref[...] = (dy * hs * dgelu).astype(BF16)
        bb = dy * gelu
        aa = jnp.where(rows == tp - 1, 0.0, pltpu.roll(a, tp - 1, 0))
        sub = rows % SUBLANE
        for k in (1, 2, 4):
            m = sub < SUBLANE - k
            bb = jnp.where(m, aa * pltpu.roll(bb, tp - k, 0) + bb, bb)
            aa = jnp.where(m, aa * pltpu.roll(aa, tp - k, 0), aa)
        asc[...] = aa
        bsc[...] = bb

        def grp(gi, carry):
            g = ng - 1 - gi
            r0 = pl.multiple_of(g * SUBLANE, SUBLANE)
            gg = bsc[pl.ds(r0, SUBLANE), :] + asc[pl.ds(r0, SUBLANE), :] * carry
            gsc[pl.ds(r0, SUBLANE), :] = gg
            return jnp.broadcast_to(gg[0:1, :], (SUBLANE, HEAD))

        lax.fori_loop(0, ng, grp, jnp.zeros((SUBLANE, HEAD), F32))
        g = gsc[...]
        first = rows == 0
        hprev = jnp.where(first, 0.0, pltpu.roll(hs, 1, 0))
        iu = i * u
        d_iu = g * jnp.where(first, 1.0, mult)
        dmult_term = jnp.where(first, 0.0, g * iu * (-(a * a) / mult))
        dla = g * hprev * a + dmult_term
        dr = dla * (-LRU_C) * sp
        dv_ref[0, 7:8, :] = _colsum(dla * (LRU_C * r) * _sigmoid(-lam_ref[...]))
        dpr = dr * r * (1.0 - r)
        dpi = d_iu * u * i * (1.0 - i)
        dv_ref[0, 5:6, :] = _colsum(dpr)
        dv_ref[0, 6:7, :] = _colsum(dpi)
        u16 = u.astype(BF16)
        dpr16 = dpr.astype(BF16)
        dpi16 = dpi.astype(BF16)
        dwa_ref[0] = _dot_tn(u16, dpr16)
        dwx_ref[0] = _dot_tn(u16, dpi16)
        du = d_iu * i + _dot_nt(dpr16, wa_ref[0]) + _dot_nt(dpi16, wx_ref[0])
        dv_ref[0, 4:5, :] = _colsum(du)
        for j in range(LRU_CONV):
            sh = LRU_CONV - 1 - j
            xs = x_all if sh == 0 else pltpu.roll(x_all, sh, 0)
            dv_ref[0, j:j + 1, :] = _colsum(du * xs[LRU_HALO:, :])
        dusc[0:tp, :] = du
        dusc[tp:, :] = jnp.zeros((LRU_HALO, HEAD), F32)
        dx_ref[...] = _conv_taps_t(dusc[...], cw_ref, tp, LRU_HALO, LRU_CONV).astype(BF16)

    col, vec, mat = _lru_specs(tp, nb)
    ocol = lambda: pl.BlockSpec((tp, HEAD), lambda j: (0, j))
    return pl.pallas_call(
        body, name="lru_bwd",
        grid=(nb,),
        in_specs=[col(2), col(3), ocol(), ocol(), pl.BlockSpec((LRU_CONV, HEAD), lambda j: (0, j)), vec(), mat(),
                  vec(), mat(), vec(), vec()],
        out_specs=[ocol(), ocol(), mat(), mat(), pl.BlockSpec((1, SUBLANE, HEAD), lambda j: (j, 0, 0))],
        out_shape=[jax.ShapeDtypeStruct((tp, D_LRU), BF16), jax.ShapeDtypeStruct((tp, D_LRU), BF16),
                   jax.ShapeDtypeStruct((nb, HEAD, HEAD), F32), jax.ShapeDtypeStruct((nb, HEAD, HEAD), F32),
                   jax.ShapeDtypeStruct((nb, SUBLANE, HEAD), F32)],
        scratch_shapes=[pltpu.VMEM((tp + LRU_HALO, HEAD), F32), pltpu.VMEM((tp, HEAD), F32),
                        pltpu.VMEM((tp, HEAD), F32), pltpu.VMEM((tp, HEAD), F32),
                        pltpu.VMEM((tp + LRU_HALO, HEAD), F32)],
        compiler_params=_cparams(1),
    )(p, p, hs, dyd, cw, cb, wa, ba, wx, bx, lam)


def _mesh_pos():
    return lax.axis_index("x"), lax.axis_index("y"), lax.axis_index("c")


def _other_chips(x, y):
    return [(1 - x, y), (x, 1 - y), (1 - x, 1 - y)]


ANY = pl.BlockSpec(memory_space=pl.ANY)


def _allgather_shards(arrs, split):
    n = len(arrs)

    def body(*refs):
        ins, outs = refs[:n], refs[n:2 * n]
        send1, recv1, send2, recv2, send3, recv3 = refs[2 * n:]
        x, y, c = _mesh_pos()
        chip = 2 * x + y
        sibling = (x, y, 1 - c)
        others = _other_chips(x, y)

        def rows(k, cc):
            half = arrs[k].shape[0] // 2
            return pl.ds(cc * half, half)

        def remote(src, dst, ssem, rsem, dev):
            return pltpu.make_async_remote_copy(src_ref=src, dst_ref=dst, send_sem=ssem, recv_sem=rsem,
                                                device_id=dev, device_id_type=MESH)

        started = [remote(ins[k], outs[k].at[chip], send3.at[k], recv3.at[k], sibling) for k in range(n)]
        for cp in started:
            cp.start()
        for k in range(n):
            for j, (ox, oy) in enumerate(others):
                if split[k]:
                    src, dst = ins[k].at[rows(k, c)], outs[k].at[chip, rows(k, c)]
                else:
                    src, dst = ins[k], outs[k].at[chip]
                cp = remote(src, dst, send1.at[3 * k + j], recv1.at[3 * k + j], (ox, oy, c))
                cp.start()
                started.append(cp)
        for j, (ox, oy) in enumerate(others):
            ochip = 2 * ox + oy
            for k in range(n):
                if split[k]:
                    blk = outs[k].at[ochip, rows(k, c)]
                    remote(blk, blk, send1.at[3 * k + j], recv1.at[3 * k + j], sibling).wait_recv()
                    cp = remote(blk, blk, send2.at[3 * k + j], recv2.at[3 * k + j], sibling)
                    cp.start()
                    started.append(cp)
                else:
                    blk = outs[k].at[ochip]
                    remote(blk, blk, send1.at[3 * k + j], recv1.at[3 * k + j], sibling).wait_recv()
        for j, (ox, oy) in enumerate(others):
            ochip = 2 * ox + oy
            for k in range(n):
                if split[k]:
                    blk = outs[k].at[ochip, rows(k, 1 - c)]
                    remote(blk, blk, send2.at[3 * k + j], recv2.at[3 * k + j], sibling).wait_recv()
        for k in range(n):
            blk = outs[k].at[chip]
            remote(blk, blk, send3.at[k], recv3.at[k], sibling).wait_recv()
        for cp in started:
            cp.wait_send()

    return pl.pallas_call(
        body, name="allgather_shards",
        in_specs=[ANY] * n, out_specs=[ANY] * n,
        out_shape=[jax.ShapeDtypeStruct((N_SHARD,) + a.shape, a.dtype) for a in arrs],
        scratch_shapes=[pltpu.SemaphoreType.DMA((3 * n,)), pltpu.SemaphoreType.DMA((3 * n,)),
                        pltpu.SemaphoreType.DMA((3 * n,)), pltpu.SemaphoreType.DMA((3 * n,)),
                        pltpu.SemaphoreType.DMA((n,)), pltpu.SemaphoreType.DMA((n,))],
    )(*arrs)


HBM = pl.BlockSpec(memory_space=pltpu.HBM)
SEM = pl.BlockSpec(memory_space=pltpu.SEMAPHORE)
DATAFLOW = pltpu.SideEffectType.DATAFLOW_SIDE_EFFECTING
N_PEERS = 4


def _in_hbm(a):
    return pltpu.with_memory_space_constraint(a, pltpu.HBM)


def _gather_peers(x, y, c):
    return [((ox, oy, c), 2 * ox + oy) for ox, oy in _other_chips(x, y)] + [((x, y, 1 - c), 2 * x + y)]


def _gather_start(arrs):
    n = len(arrs)

    def body(*refs):
        ins, lands = refs[:n], refs[n:2 * n]
        ssem, rsem = refs[2 * n:2 * n + 2]
        token = refs[-1]
        x, y, c = _mesh_pos()
        chip = 2 * x + y
        for k in range(n):
            for j, (dev, _) in enumerate(_gather_peers(x, y, c)):
                pltpu.make_async_remote_copy(
                    src_ref=ins[k], dst_ref=lands[k].at[chip], send_sem=ssem.at[N_PEERS * k + j],
                    recv_sem=rsem.at[N_PEERS * k + j], device_id=dev, device_id_type=MESH).start()
        token[...] = jnp.zeros_like(token)

    lands = [_in_hbm(lax.empty((N_SHARD,) + a.shape, a.dtype)) for a in arrs]
    out = pl.pallas_call(
        body, name="gather_start",
        in_specs=[HBM] * (2 * n),
        out_specs=[SEM, SEM] + [HBM] * (2 * n) + [pl.BlockSpec(memory_space=pltpu.VMEM)],
        out_shape=[pltpu.SemaphoreType.DMA((N_PEERS * n,)), pltpu.SemaphoreType.DMA((N_PEERS * n,))]
        + [pltpu.HBM(a.shape, a.dtype) for a in arrs]
        + [pltpu.HBM((N_SHARD,) + a.shape, a.dtype) for a in arrs]
        + [jax.ShapeDtypeStruct((SUBLANE, LANE), F32)],
        input_output_aliases={k: 2 + k for k in range(2 * n)},
        compiler_params=pltpu.CompilerParams(has_side_effects=DATAFLOW),
    )(*[_in_hbm(a) for a in arrs], *lands)
    return out[0], out[1], list(out[2:2 + n]), list(out[2 + n:2 + 2 * n]), out[-1]


def _gather_wait(ssem, rsem, srcs, lands, ks, after):
    n = len(ks)

    def body(*refs):
        ins, lnd = refs[:n], refs[n:2 * n]
        ssem_ref, rsem_ref = refs[2 * n:2 * n + 2]
        x, y, c = _mesh_pos()
        for i, k in enumerate(ks):
            for j, (dev, pchip) in enumerate(_gather_peers(x, y, c)):
                cp = pltpu.make_async_remote_copy(
                    src_ref=ins[i], dst_ref=lnd[i].at[pchip], send_sem=ssem_ref.at[N_PEERS * k + j],
                    recv_sem=rsem_ref.at[N_PEERS * k + j], device_id=dev, device_id_type=MESH)
                cp.wait_send()
                cp.wait_recv()

    out = pl.pallas_call(
        body, name="gather_wait",
        in_specs=[HBM] * (2 * n) + [SEM, SEM] + [ANY] * len(after),
        out_specs=[HBM] * (2 * n),
        out_shape=[pltpu.HBM(a.shape, a.dtype) for a in srcs] + [pltpu.HBM(a.shape, a.dtype) for a in lands],
        input_output_aliases={k: k for k in range(2 * n)},
        compiler_params=pltpu.CompilerParams(has_side_effects=DATAFLOW),
    )(*srcs, *lands, ssem, rsem, *after)
    return list(out[n:])


def _scatter_start(arrs, slots):
    n = len(arrs)

    def body(*refs):
        ins, lands = refs[:n], refs[n:2 * n]
        ssem, rsem = refs[2 * n:2 * n + 2]
        token = refs[-1]
        x, y, c = _mesh_pos()
        chip = 2 * x + y
        for k in range(n):
            for j, (ox, oy) in enumerate(_other_chips(x, y)):
                pltpu.make_async_remote_copy(
                    src_ref=ins[k].at[2 * ox + oy], dst_ref=lands[k].at[chip], send_sem=ssem.at[3 * k + j],
                    recv_sem=rsem.at[3 * k + j], device_id=(ox, oy, c), device_id_type=MESH).start()
        token[...] = jnp.zeros_like(token)

    out = pl.pallas_call(
        body, name="scatter_start",
        in_specs=[HBM] * (2 * n),
        out_specs=[SEM, SEM] + [HBM] * (2 * n) + [pl.BlockSpec(memory_space=pltpu.VMEM)],
        out_shape=[pltpu.SemaphoreType.DMA((3 * n,)), pltpu.SemaphoreType.DMA((3 * n,))]
        + [pltpu.HBM(a.shape, a.dtype) for a in arrs] + [pltpu.HBM(a.shape, a.dtype) for a in slots]
        + [jax.ShapeDtypeStruct((SUBLANE, LANE), F32)],
        input_output_aliases={k: 2 + k for k in range(2 * n)},
        compiler_params=pltpu.CompilerParams(has_side_effects=DATAFLOW),
    )(*[_in_hbm(a) for a in arrs], *[_in_hbm(a) for a in slots])
    return out[0], out[1], list(out[2:2 + n]), list(out[2 + n:2 + 2 * n]), out[-1]


def _scatter_wait(ssem, rsem, arrs, slots, after):
    n = len(arrs)

    def body(*refs):
        ins, lnd = refs[:n], refs[n:2 * n]
        ssem_ref, rsem_ref = refs[2 * n:2 * n + 2]
        x, y, c = _mesh_pos()
        for k in range(n):
            for j, (ox, oy) in enumerate(_other_chips(x, y)):
                ochip = 2 * ox + oy
                cp = pltpu.make_async_remote_copy(
                    src_ref=ins[k].at[ochip], dst_ref=lnd[k].at[ochip], send_sem=ssem_ref.at[3 * k + j],
                    recv_sem=rsem_ref.at[3 * k + j], device_id=(ox, oy, c), device_id_type=MESH)
                cp.wait_send()
                cp.wait_recv()

    out = pl.pallas_call(
        body, name="scatter_wait",
        in_specs=[HBM] * (2 * n) + [SEM, SEM] + [ANY] * len(after),
        out_specs=[HBM] * (2 * n),
        out_shape=[pltpu.HBM(a.shape, a.dtype) for a in arrs] + [pltpu.HBM(a.shape, a.dtype) for a in slots],
        input_output_aliases={k: k for k in range(2 * n)},
        compiler_params=pltpu.CompilerParams(has_side_effects=DATAFLOW),
    )(*arrs, *slots, ssem, rsem, *after)
    return list(out[n:])


def _pair_exchange_halves(arrs):
    n = len(arrs)

    def body(*refs):
        ins, outs = refs[:n], refs[n:2 * n]
        ssem, rsem = refs[2 * n:]
        x, y, c = _mesh_pos()
        cps = []
        for k in range(n):
            half = arrs[k].shape[1] // 2
            cp = pltpu.make_async_remote_copy(
                src_ref=ins[k].at[:, pl.ds((1 - c) * half, half)], dst_ref=outs[k],
                send_sem=ssem.at[k], recv_sem=rsem.at[k], device_id=(x, y, 1 - c), device_id_type=MESH)
            cp.start()
            cps.append(cp)
        for cp in cps:
            cp.wait()

    return pl.pallas_call(
        body, name="pair_exchange_halves",
        in_specs=[ANY] * n, out_specs=[ANY] * n,
        out_shape=[jax.ShapeDtypeStruct((a.shape[0], a.shape[1] // 2, a.shape[2]), a.dtype) for a in arrs],
        scratch_shapes=[pltpu.SemaphoreType.DMA((n,)), pltpu.SemaphoreType.DMA((n,))],
    )(*arrs)


GRAD_ROW_BLOCKS = 2


def _pair_add(arrs, recvd, core):
    n = len(arrs)
    nb = GRAD_ROW_BLOCKS

    def body(c_ref, *refs):
        for k in range(n):
            refs[2 * n + k][...] = (refs[k][...].astype(F32) + refs[n + k][...].astype(F32)).astype(BF16)

    def blk(a):
        return (1, a.shape[1] // 2 // nb, a.shape[2])

    grid_spec = pltpu.PrefetchScalarGridSpec(
        num_scalar_prefetch=1, grid=(N_SHARD, nb),
        in_specs=[pl.BlockSpec(blk(a), lambda s, i, c: (s, c[0] * nb + i, 0)) for a in arrs]
        + [pl.BlockSpec(blk(a), lambda s, i, c: (s, i, 0)) for a in arrs],
        out_specs=[pl.BlockSpec(blk(a), lambda s, i, c: (s, i, 0)) for a in arrs])
    return pl.pallas_call(
        body, name="pair_add", grid_spec=grid_spec,
        out_shape=[jax.ShapeDtypeStruct(r.shape, BF16) for r in recvd],
        compiler_params=_cparams(2),
    )(core, *arrs, *recvd)


def _own_slot(arrs, chip):
    n = len(arrs)
    nb = GRAD_ROW_BLOCKS

    def body(c_ref, *refs):
        for k in range(n):
            refs[n + k][...] = refs[k][...]

    def blk(a):
        return (1, a.shape[1] // nb, a.shape[2])

    grid_spec = pltpu.PrefetchScalarGridSpec(
        num_scalar_prefetch=1, grid=(nb,),
        in_specs=[pl.BlockSpec(blk(a), lambda i, c: (c[0], i, 0)) for a in arrs],
        out_specs=[pl.BlockSpec(blk(a), lambda i, c: (c[0], i, 0)) for a in arrs])
    return pl.pallas_call(
        body, name="own_slot", grid_spec=grid_spec,
        out_shape=[jax.ShapeDtypeStruct(a.shape, a.dtype) for a in arrs],
        compiler_params=_cparams(1),
    )(chip, *arrs)


def _scatter_to_owners(arrs, slots):
    n = len(arrs)

    def body(*refs):
        ins, outs = refs[:n], refs[2 * n:3 * n]
        ssem, rsem = refs[3 * n:]
        x, y, c = _mesh_pos()
        chip = 2 * x + y
        others = _other_chips(x, y)
        cps = []
        for k in range(n):
            for j, (ox, oy) in enumerate(others):
                cp = pltpu.make_async_remote_copy(
                    src_ref=ins[k].at[2 * ox + oy], dst_ref=outs[k].at[chip],
                    send_sem=ssem.at[3 * k + j], recv_sem=rsem.at[3 * k + j],
                    device_id=(ox, oy, c), device_id_type=MESH)
                cp.start()
                cps.append(cp)
        for k in range(n):
            for j, (ox, oy) in enumerate(others):
                blk = outs[k].at[2 * ox + oy]
                pltpu.make_async_remote_copy(
                    src_ref=blk, dst_ref=blk, send_sem=ssem.at[3 * k + j], recv_sem=rsem.at[3 * k + j],
                    device_id=(ox, oy, c), device_id_type=MESH).wait_recv()
        for cp in cps:
            cp.wait_send()

    return pl.pallas_call(
        body, name="scatter_to_owners",
        in_specs=[ANY] * (2 * n), out_specs=[ANY] * n,
        out_shape=[jax.ShapeDtypeStruct(a.shape, a.dtype) for a in arrs],
        scratch_shapes=[pltpu.SemaphoreType.DMA((3 * n,)), pltpu.SemaphoreType.DMA((3 * n,))],
        input_output_aliases={n + k: k for k in range(n)},
    )(*arrs, *slots)


def _sum_chips(arrs, core):
    n = len(arrs)
    nb = GRAD_ROW_BLOCKS

    def body(c_ref, *refs):
        for k in range(n):
            r = refs[k]
            refs[n + k][...] = ((r[0].astype(F32) + r[1].astype(F32)) + r[2].astype(F32)) + r[3].astype(F32)

    grid_spec = pltpu.PrefetchScalarGridSpec(
        num_scalar_prefetch=1, grid=(nb,),
        in_specs=[pl.BlockSpec((N_SHARD, a.shape[1] // nb, a.shape[2]), lambda i, c: (0, i, 0)) for a in arrs],
        out_specs=[pl.BlockSpec((a.shape[1] // nb, a.shape[2]), lambda i, c: (c[0] * nb + i, 0)) for a in arrs])
    return pl.pallas_call(
        body, name="sum_chips", grid_spec=grid_spec,
        out_shape=[jax.ShapeDtypeStruct((2 * a.shape[1], a.shape[2]), F32) for a in arrs],
        compiler_params=_cparams(1),
    )(core, *arrs)


def _pair_allgather_halves(arrs):
    n = len(arrs)

    def body(*refs):
        outs = refs[n:2 * n]
        ssem, rsem = refs[2 * n:]
        x, y, c = _mesh_pos()
        cps = []
        for k in range(n):
            h = arrs[k].shape[0] // 2
            mine = outs[k].at[pl.ds(c * h, h)]
            cp = pltpu.make_async_remote_copy(src_ref=mine, dst_ref=mine, send_sem=ssem.at[k],
                                              recv_sem=rsem.at[k], device_id=(x, y, 1 - c), device_id_type=MESH)
            cp.start()
            cps.append(cp)
        for k, cp in enumerate(cps):
            h = arrs[k].shape[0] // 2
            theirs = outs[k].at[pl.ds((1 - c) * h, h)]
            pltpu.make_async_remote_copy(src_ref=theirs, dst_ref=theirs, send_sem=ssem.at[k], recv_sem=rsem.at[k],
                                         device_id=(x, y, 1 - c), device_id_type=MESH).wait_recv()
            cp.wait_send()

    return pl.pallas_call(
        body, name="pair_allgather_halves",
        in_specs=[ANY] * n, out_specs=[ANY] * n,
        out_shape=[jax.ShapeDtypeStruct(a.shape, a.dtype) for a in arrs],
        scratch_shapes=[pltpu.SemaphoreType.DMA((n,)), pltpu.SemaphoreType.DMA((n,))],
        input_output_aliases={k: k for k in range(n)},
    )(*arrs)


N_DEV = 8


def _allgather_all(v):
    m_per, n = v.shape

    def body(x_ref, out_ref, send_sems, recv_sems, local_sem):
        x, y, c = _mesh_pos()
        me, sibling = (x, y, c), (x, y, 1 - c)
        chips = _other_chips(x, y)

        def rows(px, py, pc):
            return out_ref.at[pl.ds((4 * px + 2 * py + pc) * m_per, m_per), :]

        def copy(k, block, to, src=None):
            return pltpu.make_async_remote_copy(
                src_ref=rows(*block) if src is None else src, dst_ref=rows(*block),
                send_sem=send_sems.at[k], recv_sem=recv_sems.at[k], device_id=to, device_id_type=MESH)

        mine = pltpu.make_async_copy(x_ref, rows(*me), local_sem)
        mine.start()
        first = [copy(0, me, sibling, src=x_ref)]
        first += [copy(1 + j, me, (*chip, c), src=x_ref) for j, chip in enumerate(chips)]
        for cp in first:
            cp.start()
        passed = [copy(4 + j, (*chip, c), sibling) for j, chip in enumerate(chips)]
        for j, chip in enumerate(chips):
            copy(1 + j, (*chip, c), me).wait_recv()
            passed[j].start()
        copy(0, sibling, me).wait_recv()
        for j, chip in enumerate(chips):
            copy(4 + j, (*chip, 1 - c), me).wait_recv()
        for cp in first + passed:
            cp.wait_send()
        mine.wait()

    return pl.pallas_call(
        body, name="allgather_all",
        out_shape=jax.ShapeDtypeStruct((N_DEV * m_per, n), v.dtype),
        in_specs=[pl.BlockSpec(memory_space=pltpu.VMEM)],
        out_specs=pl.BlockSpec(memory_space=pltpu.VMEM),
        scratch_shapes=[pltpu.SemaphoreType.DMA((7,)), pltpu.SemaphoreType.DMA((7,)), pltpu.SemaphoreType.DMA],
        compiler_params=pltpu.CompilerParams(vmem_limit_bytes=VMEM_LIMIT_MB * 1024 * 1024),
    )(v)


def _adamw_math(w, g, m, v):
    m2 = ADAM_B1 * m + (1.0 - ADAM_B1) * g
    v2 = ADAM_B2 * v + (1.0 - ADAM_B2) * (g * g)
    m_hat = m2 / (1.0 - ADAM_B1 ** ADAM_STEP)
    v_hat = v2 / (1.0 - ADAM_B2 ** ADAM_STEP)
    delta = -ADAM_LR * (m_hat / (jnp.sqrt(v_hat) + ADAM_EPS) + ADAM_WD * w)
    return delta, m2, v2


def _adamw(w, m, v, gs, nblk):
    nl, r, n = w.shape
    assert nl == len(gs) and nl in (1, 2)
    br = r // nblk

    def body(w_ref, m_ref, v_ref, *rest):
        g_refs, (go_ref, d_ref, mo_ref, vo_ref) = rest[:nl], rest[nl:]
        g = g_refs[0][...]
        if nl == 2:
            g = jnp.where(pl.program_id(0) == 0, g, g_refs[1][...])
        delta, m2, v2 = _adamw_math(w_ref[0], g, m_ref[0], v_ref[0])
        go_ref[0] = g
        d_ref[0] = delta
        mo_ref[0] = m2
        vo_ref[0] = v2

    spec = pl.BlockSpec((1, br, n), lambda l, i: (l, i, 0))
    g_specs = [pl.BlockSpec((br, n), lambda l, i: (i, 0))] if nl == 1 else [
        pl.BlockSpec((br, n), lambda l, i: (jnp.where(l == 0, i, nblk - 1), 0)),
        pl.BlockSpec((br, n), lambda l, i: (jnp.where(l == 1, i, 0), 0))]
    return pl.pallas_call(
        body, name="adamw", grid=(nl, nblk),
        in_specs=[spec, spec, spec] + g_specs,
        out_specs=[spec] * 4,
        out_shape=[jax.ShapeDtypeStruct((nl, r, n), F32)] * 4,
        compiler_params=_cparams(2),
    )(w, m, v, *gs)


def _small_reduce_adamw(parts, w, m, v, rep_rows, sh_rows):
    mrows = rep_rows + N_SHARD * sh_rows

    def body(p_ref, w_ref, m_ref, v_ref, go_ref, d_ref, mo_ref, vo_ref):
        x, y, _ = _mesh_pos()
        mine = rep_rows + (2 * x + y) * sh_rows
        g_rep = p_ref[0:rep_rows, :]
        g_sh = p_ref[pl.ds(pl.multiple_of(mine, SUBLANE), sh_rows), :]
        for k in range(1, N_DEV):
            g_rep = g_rep + p_ref[k * mrows:k * mrows + rep_rows, :]
            g_sh = g_sh + p_ref[pl.ds(pl.multiple_of(k * mrows + mine, SUBLANE), sh_rows), :]
        g = jnp.concatenate([g_rep, g_sh], axis=0)
        delta, m2, v2 = _adamw_math(w_ref[...], g, m_ref[...], v_ref[...])
        go_ref[...] = g
        d_ref[...] = delta
        mo_ref[...] = m2
        vo_ref[...] = v2

    return pl.pallas_call(
        body, name="small_reduce_adamw",
        out_shape=[jax.ShapeDtypeStruct((rep_rows + sh_rows, 128), F32)] * 4,
        compiler_params=pltpu.CompilerParams(vmem_limit_bytes=VMEM_LIMIT_MB * 1024 * 1024),
    )(parts, w, m, v)


LANE = 128
REP_SPEC = (("ffn1_norm", 16), ("mix_norm", 16), ("ffn2_norm", 16), ("final_norm", 8), ("pool_w", 128),
            ("pool_scale", 8), ("hgrn_lb_logits", 16), ("hgrn_gnorm", 8), ("lru_wa", 256), ("lru_wx", 256))
SH_SPEC = (("meta_tokens", 32), ("conv_w", 32), ("lru_conv_w", 8), ("conv_b", 8), ("conv_ln_g", 8),
           ("conv_ln_b", 8), ("lru_conv_b", 8), ("lru_ba", 8), ("lru_bx", 8), ("lru_lambda", 8))
REP_ROWS = sum(r for _, r in REP_SPEC)
SH_ROWS = sum(r for _, r in SH_SPEC)


def _pack_rows(vals, spec):
    parts = []
    for name, rows in spec:
        flat = vals[name].astype(F32).reshape(-1, LANE)
        if flat.shape[0] < rows:
            flat = jnp.concatenate([flat, jnp.zeros((rows - flat.shape[0], LANE), F32)], axis=0)
        parts.append(flat)
    return jnp.concatenate(parts, axis=0)


def _unpack_rows(packed, spec, shapes):
    out = {}
    off = 0
    for name, rows in spec:
        shp = shapes[name]
        n = int(np.prod(shp)) // LANE
        out[name] = packed[off:off + n].reshape(shp)
        off += rows
    return out


def _block_diag(blocks):
    n, b, _ = blocks.shape
    return sum(jnp.pad(blocks[g], ((g * b, (n - 1 - g) * b), (g * b, (n - 1 - g) * b))) for g in range(n))


def _diag_blocks(mat, n):
    b = mat.shape[0] // n
    return jnp.stack([mat[g * b:(g + 1) * b, g * b:(g + 1) * b] for g in range(n)])


BIG = ("ffn1_wg", "ffn1_wu", "ffn2_wg", "ffn2_wu", "ffn1_wd", "ffn2_wd", "w_in_even", "w_out_even",
       "w_in_odd", "w_out_odd")
ADAM_BLOCKS = {"ffn1_wg": 8, "ffn1_wu": 8, "ffn2_wg": 8, "ffn2_wu": 8, "ffn1_wd": 4, "ffn2_wd": 4,
               "w_in_even": 4, "w_out_even": 2, "w_in_odd": 4, "w_out_odd": 2}
WEIGHT_NAMES = ('meta_tokens', 'ffn1_norm', 'ffn1_wg', 'ffn1_wu', 'ffn1_wd', 'mix_norm', 'ffn2_norm', 'ffn2_wg',
                'ffn2_wu', 'ffn2_wd', 'w_in_even', 'pool_w', 'pool_scale', 'hgrn_lb_logits', 'hgrn_gnorm',
                'w_out_even', 'w_in_odd', 'conv_w', 'conv_b', 'conv_ln_g', 'conv_ln_b', 'lru_conv_w',
                'lru_conv_b', 'lru_wa', 'lru_ba', 'lru_wx', 'lru_bx', 'lru_lambda', 'w_out_odd', 'final_norm')


def _rows2d(a):
    return a.reshape(-1, a.shape[-1])


def _local_step_v2(x, tgt, w, gathered, small_full):
    s_len, d = x.shape
    t_real = s_len + N_META
    tp = -(-t_real // ROW_ALIGN) * ROW_ALIGN
    tm = _tile(tp, 832, ROW_ALIGN)
    tm_small = _tile(tp, 416, 16)
    tr = _tile(tp, 416, SUBLANE)
    f1 = ("ffn1_norm", "ffn1_wg", "ffn1_wu", "ffn1_wd")
    f2 = ("ffn2_norm", "ffn2_wg", "ffn2_wu", "ffn2_wd")

    meta_full = small_full["meta_tokens"]
    h0 = jnp.concatenate([meta_full, x, jnp.zeros((tp - t_real, d), F32)], axis=0)
    tgt_pad = jnp.concatenate([jnp.zeros((N_META, d), F32), tgt, jnp.zeros((tp - t_real, d), F32)], axis=0)

    w_in_even = jnp.transpose(gathered["w_in_even"], (1, 0, 2)).reshape(d, D_IN_EVEN)
    w_out_even = gathered["w_out_even"].reshape(d, d)
    w_out_odd = gathered["w_out_odd"].reshape(d, d)
    even_piece = [(w_in_even, (d, D_IN_EVEN), (0, 0))]
    odd_pieces = [(gathered["w_in_odd"], (1, d, D_IN_ODD // N_SHARD), (k, 0, 0)) for k in range(N_SHARD)]
    pool_wbd = _block_diag(w["pool_w"][0]).astype(BF16)
    pool_scale = w["pool_scale"]
    wa_bd = _block_diag2(w["lru_wa"][0]).astype(BF16)
    wx_bd = _block_diag2(w["lru_wx"][0]).astype(BF16)
    mst, msk, n_lev = _hgrn_consts(HG_CHUNK)
    conv_w = small_full["conv_w"]
    sf = small_full

    def gain(name, layer):
        return w[name][layer:layer + 1]

    def ffn(h, names, layer):
        return _ffn_fwd(h, gain(names[0], layer), gathered[names[1]], gathered[names[2]], gathered[names[3]],
                        layer, tm)

    h1, a1, b1, n1 = ffn(h0, f1, 0)
    p0, nm0 = _proj_fwd(h1, gain("mix_norm", 0), 0, even_piece, tm_small)
    ya = _pool_fwd(p0, pool_wbd, pool_scale, tr)
    yb, states = _hgrn_fwd(p0, w["hgrn_lb_logits"], w["hgrn_gnorm"], mst, msk, n_lev, tm)
    h2 = _out_fwd(h1, ya, yb, w_out_even, tm)
    h3, a2, b2, n2 = ffn(h2, f2, 0)
    h4, a3, b3, n3 = ffn(h3, f1, 1)
    p1, nm1 = _proj_fwd(h4, gain("mix_norm", 1), 1, odd_pieces, tm_small)
    yc = _convmod_fwd(p1, conv_w, sf["conv_b"], sf["conv_ln_g"], sf["conv_ln_b"], tr)
    lru_args = (sf["lru_conv_w"], sf["lru_conv_b"], wa_bd, sf["lru_ba"], wx_bd, sf["lru_bx"], sf["lru_lambda"])
    yd, hs = _lru_fwd(p1, *lru_args)
    h5 = _out_fwd(h4, yc, yd, w_out_odd, tm)
    h6, a4, b4, n4 = ffn(h5, f2, 1)
    loss, dh6, dg_final = _loss_bwd(h6, w["final_norm"].reshape(1, d), tgt_pad, t_real, tm)

    def ffn_bwd(dho, h, n, a, b, names, layer, acc):
        dh, da, db, dg = _ffn_bwd_act(dho, h, gain(names[0], layer), a, b, gathered[names[1]], gathered[names[2]],
                                      gathered[names[3]], layer, tm_small)
        acc = _ffn_bwd_w(dho, n, a, b, da, db, acc[0], acc[1], acc[2], layer, tm)
        return dh, dg, acc

    none3 = (None, None, None)
    dh5, dg_f2_l1, g_f2 = ffn_bwd(dh6, h5, n4, a4, b4, f2, 1, none3)
    dyc, dyd, dw_out_odd = _out_bwd(dh5, yc, yd, w_out_odd, tm)
    dca, dcb, dconv_w, dconv_vec = _convmod_bwd(p1, dyc, conv_w, sf["conv_b"], sf["conv_ln_g"], sf["conv_ln_b"], tr)
    dlx, dlg, dwa_bd, dwx_bd, dlru_vec = _lru_bwd(p1, hs, dyd, *lru_args)
    dp1 = [dca, dcb, dlx, dlg]
    dh4, dg_mix_l1 = _proj_bwd_act(dh5, h4, gain("mix_norm", 1), 1, dp1, odd_pieces, tm_small)
    dw_in_odd = jnp.stack(_proj_bwd_w(nm1, dp1, tm))
    dh3, dg_f1_l1, g_f1 = ffn_bwd(dh4, h3, n3, a3, b3, f1, 1, none3)
    dh2, dg_f2_l0, g_f2 = ffn_bwd(dh3, h2, n2, a2, b2, f2, 0, g_f2)
    dya, dyb, dw_out_even = _out_bwd(dh2, ya, yb, w_out_even, tm)
    dpool, dpool_wbd, dpool_scale = _pool_bwd(p0, dya, pool_wbd, pool_scale, tr)
    dq, dz, dv, dgate, dlb_logits, dgn_heads = _hgrn_bwd(p0, dyb, states, w["hgrn_lb_logits"], w["hgrn_gnorm"],
                                                         mst, msk, n_lev, tm)
    dp0 = [jnp.concatenate([dpool, dq, dz, dv, dgate], axis=1)]
    dh1, dg_mix_l0 = _proj_bwd_act(dh2, h1, gain("mix_norm", 0), 0, dp0, even_piece, tm_small)
    (dw_in_even,) = _proj_bwd_w(nm0, dp0, tm_small)
    dh0, dg_f1_l0, g_f1 = ffn_bwd(dh1, h0, n1, a1, b1, f1, 0, g_f1)

    grad_x = dh0[N_META:t_real]
    big = {
        "ffn1_wg": g_f1[0], "ffn1_wu": g_f1[1], "ffn1_wd": g_f1[2],
        "ffn2_wg": g_f2[0], "ffn2_wu": g_f2[1], "ffn2_wd": g_f2[2],
        "w_in_even": jnp.transpose(dw_in_even.reshape(d, N_SHARD, D_IN_EVEN // N_SHARD), (1, 0, 2)),
        "w_out_even": dw_out_even.reshape(N_SHARD, d // N_SHARD, d),
        "w_in_odd": dw_in_odd,
        "w_out_odd": dw_out_odd.reshape(N_SHARD, d // N_SHARD, d),
    }
    rep = {
        "ffn1_norm": jnp.concatenate([dg_f1_l0, dg_f1_l1], axis=0),
        "mix_norm": jnp.concatenate([dg_mix_l0, dg_mix_l1], axis=0),
        "ffn2_norm": jnp.concatenate([dg_f2_l0, dg_f2_l1], axis=0),
        "final_norm": dg_final,
        "pool_w": _diag_blocks(dpool_wbd, len(POOL_WINDOWS)),
        "pool_scale": dpool_scale,
        "hgrn_lb_logits": dlb_logits,
        "hgrn_gnorm": jnp.sum(dgn_heads, axis=0),
        "lru_wa": _diag_blocks2(dwa_bd),
        "lru_wx": _diag_blocks2(dwx_bd),
    }
    dmeta = jnp.transpose(dh0[:N_META].reshape(N_META, N_SHARD, 2, LANE), (1, 0, 2, 3)).reshape(N_SHARD, 32, LANE)
    packs = [_pack_rows(rep, REP_SPEC)]
    for s in range(N_SHARD):
        sh = {
            "meta_tokens": dmeta[s], "conv_w": dconv_w[s], "lru_conv_w": dlru_vec[s, 0:4],
            "conv_b": dconv_vec[s, 0:1], "conv_ln_g": dconv_vec[s, 1:2], "conv_ln_b": dconv_vec[s, 2:3],
            "lru_conv_b": dlru_vec[s, 4:5], "lru_ba": dlru_vec[s, 5:6], "lru_bx": dlru_vec[s, 6:7],
            "lru_lambda": dlru_vec[s, 7:8],
        }
        packs.append(_pack_rows(sh, SH_SPEC))
    return loss, grad_x, big, jnp.concatenate(packs, axis=0)


def _block_diag2(heads):
    nb = heads.shape[0] // 2
    return jnp.stack([_block_diag(heads[2 * j:2 * j + 2]) for j in range(nb)])


def _diag_blocks2(mats):
    return jnp.concatenate([_diag_blocks(mats[j], 2) for j in range(mats.shape[0])], axis=0)


def _kernel_v2(x, meta_tokens, ffn1_norm, ffn1_wg, ffn1_wu, ffn1_wd, mix_norm, ffn2_norm, ffn2_wg, ffn2_wu, ffn2_wd, w_in_even, pool_w, pool_scale, hgrn_lb_logits, hgrn_gnorm, w_out_even, w_in_odd, conv_w, conv_b, conv_ln_g, conv_ln_b, lru_conv_w, lru_conv_b, lru_wa, lru_ba, lru_wx, lru_bx, lru_lambda, w_out_odd, final_norm, loss_target, m_meta_tokens, m_ffn1_norm, m_ffn1_wg, m_ffn1_wu, m_ffn1_wd, m_mix_norm, m_ffn2_norm, m_ffn2_wg, m_ffn2_wu, m_ffn2_wd, m_w_in_even, m_pool_w, m_pool_scale, m_hgrn_lb_logits, m_hgrn_gnorm, m_w_out_even, m_w_in_odd, m_conv_w, m_conv_b, m_conv_ln_g, m_conv_ln_b, m_lru_conv_w, m_lru_conv_b, m_lru_wa, m_lru_ba, m_lru_wx, m_lru_bx, m_lru_lambda, m_w_out_odd, m_final_norm, v_meta_tokens, v_ffn1_norm, v_ffn1_wg, v_ffn1_wu, v_ffn1_wd, v_mix_norm, v_ffn2_norm, v_ffn2_wg, v_ffn2_wu, v_ffn2_wd, v_w_in_even, v_pool_w, v_pool_scale, v_hgrn_lb_logits, v_hgrn_gnorm, v_w_out_even, v_w_in_odd, v_conv_w, v_conv_b, v_conv_ln_g, v_conv_ln_b, v_lru_conv_w, v_lru_conv_b, v_lru_wa, v_lru_ba, v_lru_wx, v_lru_bx, v_lru_lambda, v_w_out_odd, v_final_norm):
    args = locals()
    w = {n: args[n] for n in WEIGHT_NAMES}
    m = {n: args["m_" + n] for n in WEIGHT_NAMES}
    v = {n: args["v_" + n] for n in WEIGHT_NAMES}
    shapes = {n: w[n].shape for n in WEIGHT_NAMES}

    big_in = [_rows2d(w[n]).astype(BF16) for n in BIG]
    small_sh = _pack_rows(w, SH_SPEC)
    gath = _allgather_shards(big_in + [small_sh], [True] * len(BIG) + [False])
    gathered = dict(zip(BIG, gath[:len(BIG)]))
    sm = gath[len(BIG)]
    sh_shapes = {n: (N_SHARD,) + tuple(shapes[n]) for n, _ in SH_SPEC}
    per_shard = [_unpack_rows(sm[s], SH_SPEC, shapes) for s in range(N_SHARD)]
    small_full = {}
    for n, _ in SH_SPEC:
        stacked = [per_shard[s][n] for s in range(N_SHARD)]
        small_full[n] = jnp.concatenate([p.reshape(-1, p.shape[-1]) for p in stacked], axis=-1)
    small_full["conv_w"] = jnp.concatenate(
        [small_full["conv_w"], jnp.zeros((CONV_HALO - CONV_WIDTH, D_CONV), F32)], axis=0)

    loss, grad_x, big, small_part = _local_step(x[0], loss_target[0], w, gathered, small_full)
    loss = lax.psum(loss[0, 0], ("x", "y", "c"))

    core = lax.axis_index("c").astype(jnp.int32).reshape(1)
    parts = [big[n] for n in BIG]
    recvd = _pair_exchange_halves(parts)
    pair = _pair_add(parts, recvd, core)
    chip = (2 * lax.axis_index("x") + lax.axis_index("y")).astype(jnp.int32).reshape(1)
    slots = _scatter_to_owners(pair, _own_slot(pair, chip))
    halves = _sum_chips(slots, core)
    full = _pair_allgather_halves(halves)
    out_g, out_d, out_m, out_v = {}, {}, {}, {}
    for n, g in zip(BIG, full):
        res = _adamw(_rows2d(w[n]), _rows2d(m[n]), _rows2d(v[n]), g, 0, ADAM_BLOCKS[n])
        out_g[n], out_d[n], out_m[n], out_v[n] = [r.reshape(shapes[n]) for r in res]

    gathered_small = _allgather_all(small_part)

    def pack_small(src):
        return jnp.concatenate([_pack_rows(src, REP_SPEC), _pack_rows(src, SH_SPEC)], axis=0)

    res = _small_reduce_adamw(gathered_small, pack_small(w), pack_small(m), pack_small(v), REP_ROWS, SH_ROWS)
    for dst, packed in zip((out_g, out_d, out_m, out_v), res):
        dst.update(_unpack_rows(packed[:REP_ROWS], REP_SPEC, shapes))
        dst.update(_unpack_rows(packed[REP_ROWS:], SH_SPEC, shapes))

    return (loss, grad_x[None], *[out_g[n] for n in WEIGHT_NAMES], *[out_d[n] for n in WEIGHT_NAMES],
            *[out_m[n] for n in WEIGHT_NAMES], *[out_v[n] for n in WEIGHT_NAMES])


GATHER_GROUPS = (
    (("small", 0),),
    (("ffn1_wg", 0), ("ffn1_wu", 0), ("ffn1_wd", 0)),
    (("w_in_even", 0), ("w_out_even", 0)),
    (("ffn2_wg", 0), ("ffn2_wu", 0), ("ffn2_wd", 0)),
    (("ffn1_wg", 1), ("ffn1_wu", 1), ("ffn1_wd", 1)),
    (("w_in_odd", 0), ("w_out_odd", 0)),
    (("ffn2_wg", 1), ("ffn2_wu", 1), ("ffn2_wd", 1)),
)
ADAM_ROW_BLOCKS = {"ffn1_wg": 2, "ffn1_wu": 2, "ffn2_wg": 2, "ffn2_wu": 2, "ffn1_wd": 2, "ffn2_wd": 2,
                   "w_in_even": 4, "w_out_even": 2, "w_in_odd": 4, "w_out_odd": 2}
TRANSPOSED = ("ffn1_wg", "ffn1_wu", "ffn2_wg", "ffn2_wu", "w_in_even")
SCATTER_DEPTH = 2


def _unpack_small(sm, shapes):
    per_shard = [_unpack_rows(sm[s], SH_SPEC, shapes) for s in range(N_SHARD)]
    full = {}
    for n, _ in SH_SPEC:
        full[n] = jnp.concatenate([per_shard[s][n].reshape(-1, shapes[n][-1]) for s in range(N_SHARD)], axis=-1)
    full["conv_w"] = jnp.concatenate([full["conv_w"], jnp.zeros((CONV_HALO - CONV_WIDTH, D_CONV), F32)], axis=0)
    return full


def _local_step(x, tgt, w, shapes, fetch, emit):
    s_len, d = x.shape
    t_real = s_len + N_META
    tp = -(-t_real // ROW_ALIGN) * ROW_ALIGN
    tm = _tile(tp, 832, ROW_ALIGN)
    tm_small = _tile(tp, 416, 16)
    tr = _tile(tp, 416, SUBLANE)

    def gain(name, layer):
        return w[name][layer:layer + 1]

    pool_wbd = _block_diag(w["pool_w"][0]).astype(BF16)
    pool_scale = w["pool_scale"]
    wa_bd = _block_diag2(w["lru_wa"][0]).astype(BF16)
    wx_bd = _block_diag2(w["lru_wx"][0]).astype(BF16)
    mst, msk, n_lev = _hgrn_consts(HG_CHUNK)

    (sm,) = fetch(0, None)
    sf = _unpack_small(sm, shapes)
    h0 = jnp.concatenate([sf["meta_tokens"], x, jnp.zeros((tp - t_real, d), F32)], axis=0)
    tgt_pad = jnp.concatenate([jnp.zeros((N_META, d), F32), tgt, jnp.zeros((tp - t_real, d), F32)], axis=0)
    f1l0 = fetch(1, h0)
    h1, *s1 = _ffn_fwd(h0, gain("ffn1_norm", 0), *f1l0, 0, tm)
    w_in_even4, w_out_even4 = fetch(2, h1)
    w_out_even = w_out_even4.reshape(d, d)
    even_piece = [(w_in_even4.reshape(D_IN_EVEN, d), (D_IN_EVEN, d), (0, 0))]
    p0, nm0 = _proj_fwd(h1, gain("mix_norm", 0), 0, even_piece, tm_small, wt=True)
    ya = _pool_fwd(p0, pool_wbd, pool_scale, tr)
    yb, states = _hgrn_fwd(p0, w["hgrn_lb_logits"], w["hgrn_gnorm"], mst, msk, n_lev, tm)
    h2 = _out_fwd(h1, ya, yb, w_out_even, tm)
    f2l0 = fetch(3, h2)
    h3, *s2 = _ffn_fwd(h2, gain("ffn2_norm", 0), *f2l0, 0, tm)
    f1l1 = fetch(4, h3)
    h4, *s3 = _ffn_fwd(h3, gain("ffn1_norm", 1), *f1l1, 0, tm)
    w_in_odd4, w_out_odd4 = fetch(5, h4)
    w_out_odd = w_out_odd4.reshape(d, d)
    odd_pieces = [(w_in_odd4, (1, d, D_IN_ODD // N_SHARD), (k, 0, 0)) for k in range(N_SHARD)]
    p1, nm1 = _proj_fwd(h4, gain("mix_norm", 1), 1, odd_pieces, tm_small)
    yc = _convmod_fwd(p1, sf["conv_w"], sf["conv_b"], sf["conv_ln_g"], sf["conv_ln_b"], tr)
    lru_args = (sf["lru_conv_w"], sf["lru_conv_b"], wa_bd, sf["lru_ba"], wx_bd, sf["lru_bx"], sf["lru_lambda"])
    yd, hs = _lru_fwd(p1, *lru_args)
    h5 = _out_fwd(h4, yc, yd, w_out_odd, tm)
    f2l1 = fetch(6, h5)
    h6, *s4 = _ffn_fwd(h5, gain("ffn2_norm", 1), *f2l1, 0, tm)
    loss, dh6, dg_final = _loss_bwd(h6, w["final_norm"].reshape(1, d), tgt_pad, t_real, tm)

    def ffn_bwd(dho, h, saved, norm, wts, after=()):
        ga, gb, sa, n = saved
        dh, da, db, dg, dy = _ffn_bwd_act(dho, h, norm, ga, gb, *wts, 0, tm, after)
        return dh, dg, _ffn_bwd_w(dy, n, sa, da, db, tm)

    dh5, dg_f2_l1, g = ffn_bwd(dh6, h5, s4, gain("ffn2_norm", 1), f2l1)
    sent = emit((("ffn2_wg", 1), ("ffn2_wu", 1), ("ffn2_wd", 1)), g)
    dyc, dyd, dw_out_odd = _out_bwd(dh5, yc, yd, w_out_odd, tm, tuple(sent))
    dca, dcb, dconv_w, dconv_vec = _convmod_bwd(p1, dyc, sf["conv_w"], sf["conv_b"], sf["conv_ln_g"],
                                                sf["conv_ln_b"], tr)
    dlx, dlg, dwa_bd, dwx_bd, dlru_vec = _lru_bwd(p1, hs, dyd, *lru_args)
    dp1 = [dca, dcb, dlx, dlg]
    dh4, dg_mix_l1 = _proj_bwd_act(dh5, h4, gain("mix_norm", 1), 1, dp1, odd_pieces, tm_small)
    dw_in_odd = jnp.stack(_proj_bwd_w(nm1, dp1, tm))
    dh3, dg_f1_l1, g = ffn_bwd(dh4, h3, s3, gain("ffn1_norm", 1), f1l1)
    sent = emit((("w_out_odd", 0), ("w_in_odd", 0), ("ffn1_wg", 1), ("ffn1_wu", 1), ("ffn1_wd", 1)),
                [dw_out_odd.reshape(N_SHARD, d // N_SHARD, d), dw_in_odd] + list(g))
    dh2, dg_f2_l0, g_f2l0 = ffn_bwd(dh3, h2, s2, gain("ffn2_norm", 0), f2l0, tuple(sent))
    dya, dyb, dw_out_even = _out_bwd(dh2, ya, yb, w_out_even, tm)
    dpool, dpool_wbd, dpool_scale = _pool_bwd(p0, dya, pool_wbd, pool_scale, tr)
    dq, dz, dv, dgate, dlb_logits, dgn_heads = _hgrn_bwd(p0, dyb, states, w["hgrn_lb_logits"], w["hgrn_gnorm"],
                                                         mst, msk, n_lev, tm)
    dp0 = [jnp.concatenate([dpool, dq, dz, dv, dgate], axis=1)]
    dh1, dg_mix_l0 = _proj_bwd_act(dh2, h1, gain("mix_norm", 0), 0, dp0, even_piece, tm_small, wt=True)
    (dw_in_even_t,) = _proj_bwd_w(nm0, dp0, tm_small, wt=True)
    sent = emit((("ffn2_wg", 0), ("ffn2_wu", 0), ("ffn2_wd", 0), ("w_out_even", 0), ("w_in_even", 0)),
                list(g_f2l0) + [dw_out_even.reshape(N_SHARD, d // N_SHARD, d),
                                dw_in_even_t.reshape(N_SHARD, D_IN_EVEN // N_SHARD, d)])
    dh0, dg_f1_l0, g = ffn_bwd(dh1, h0, s1, gain("ffn1_norm", 0), f1l0, tuple(sent))
    emit((("ffn1_wg", 0), ("ffn1_wu", 0), ("ffn1_wd", 0)), g)

    grad_x = dh0[N_META:t_real]
    rep = {
        "ffn1_norm": jnp.concatenate([dg_f1_l0, dg_f1_l1], axis=0),
        "mix_norm": jnp.concatenate([dg_mix_l0, dg_mix_l1], axis=0),
        "ffn2_norm": jnp.concatenate([dg_f2_l0, dg_f2_l1], axis=0),
        "final_norm": dg_final,
        "pool_w": _diag_blocks(dpool_wbd, len(POOL_WINDOWS)),
        "pool_scale": dpool_scale,
        "hgrn_lb_logits": dlb_logits,
        "hgrn_gnorm": jnp.sum(dgn_heads, axis=0),
        "lru_wa": _diag_blocks2(dwa_bd),
        "lru_wx": _diag_blocks2(dwx_bd),
    }
    dmeta = jnp.transpose(dh0[:N_META].reshape(N_META, N_SHARD, 2, LANE), (1, 0, 2, 3)).reshape(N_SHARD, 32, LANE)
    packs = [_pack_rows(rep, REP_SPEC)]
    for s in range(N_SHARD):
        sh = {
            "meta_tokens": dmeta[s], "conv_w": dconv_w[s], "lru_conv_w": dlru_vec[s, 0:4],
            "conv_b": dconv_vec[s, 0:1], "conv_ln_g": dconv_vec[s, 1:2], "conv_ln_b": dconv_vec[s, 2:3],
            "lru_conv_b": dlru_vec[s, 4:5], "lru_ba": dlru_vec[s, 5:6], "lru_bx": dlru_vec[s, 6:7],
            "lru_lambda": dlru_vec[s, 7:8],
        }
        packs.append(_pack_rows(sh, SH_SPEC))
    return loss, grad_x, jnp.concatenate(packs, axis=0)


def kernel(x, meta_tokens, ffn1_norm, ffn1_wg, ffn1_wu, ffn1_wd, mix_norm, ffn2_norm, ffn2_wg, ffn2_wu, ffn2_wd, w_in_even, pool_w, pool_scale, hgrn_lb_logits, hgrn_gnorm, w_out_even, w_in_odd, conv_w, conv_b, conv_ln_g, conv_ln_b, lru_conv_w, lru_conv_b, lru_wa, lru_ba, lru_wx, lru_bx, lru_lambda, w_out_odd, final_norm, loss_target, m_meta_tokens, m_ffn1_norm, m_ffn1_wg, m_ffn1_wu, m_ffn1_wd, m_mix_norm, m_ffn2_norm, m_ffn2_wg, m_ffn2_wu, m_ffn2_wd, m_w_in_even, m_pool_w, m_pool_scale, m_hgrn_lb_logits, m_hgrn_gnorm, m_w_out_even, m_w_in_odd, m_conv_w, m_conv_b, m_conv_ln_g, m_conv_ln_b, m_lru_conv_w, m_lru_conv_b, m_lru_wa, m_lru_ba, m_lru_wx, m_lru_bx, m_lru_lambda, m_w_out_odd, m_final_norm, v_meta_tokens, v_ffn1_norm, v_ffn1_wg, v_ffn1_wu, v_ffn1_wd, v_mix_norm, v_ffn2_norm, v_ffn2_wg, v_ffn2_wu, v_ffn2_wd, v_w_in_even, v_pool_w, v_pool_scale, v_hgrn_lb_logits, v_hgrn_gnorm, v_w_out_even, v_w_in_odd, v_conv_w, v_conv_b, v_conv_ln_g, v_conv_ln_b, v_lru_conv_w, v_lru_conv_b, v_lru_wa, v_lru_ba, v_lru_wx, v_lru_bx, v_lru_lambda, v_w_out_odd, v_final_norm):
    args = locals()
    w = {n: args[n] for n in WEIGHT_NAMES}
    m = {n: args["m_" + n] for n in WEIGHT_NAMES}
    v = {n: args["v_" + n] for n in WEIGHT_NAMES}
    shapes = {n: w[n].shape for n in WEIGHT_NAMES}
    core = lax.axis_index("c").astype(jnp.int32).reshape(1)
    chip = (2 * lax.axis_index("x") + lax.axis_index("y")).astype(jnp.int32).reshape(1)

    def view(a, n):
        return jnp.swapaxes(a, 1, 2) if n in TRANSPOSED else a

    wv, mv, vv = [{n: view(src[n], n) for n in BIG} for src in (w, m, v)]

    def shard(key):
        n, l = key
        return _pack_rows(w, SH_SPEC) if n == "small" else wv[n][l].astype(BF16)

    started = {}
    for groups in (GATHER_GROUPS[:2], GATHER_GROUPS[2:]):
        gkeys = [key for grp in groups for key in grp]
        ssem, rsem, srcs, lands, token = _gather_start([shard(key) for key in gkeys])
        for k, key in enumerate(gkeys):
            started[key] = (ssem, rsem, srcs[k], lands[k], k, token)

    def pack_small(src):
        return jnp.concatenate([_pack_rows(src, REP_SPEC), _pack_rows(src, SH_SPEC)], axis=0)

    small_packs = [pack_small(src) for src in (w, m, v)]

    def fetch(group, after):
        st = [started[key] for key in GATHER_GROUPS[group]]
        deps = (st[0][5],) if after is None else (after,)
        if group == 1:
            deps += (started[GATHER_GROUPS[2][0]][5],) + tuple(small_packs)
        return _gather_wait(st[0][0], st[0][1], [s[2] for s in st], [s[3] for s in st], [s[4] for s in st], deps)

    in_flight, reduced = [], {}

    def collect(entry, after):
        gkeys, gs_sem, gr_sem, pair_thru, slots_thru, _ = entry
        slots = _scatter_wait(gs_sem, gr_sem, pair_thru, slots_thru, after)
        full = _pair_allgather_halves(_sum_chips(slots, core))
        reduced.update(zip(gkeys, full))
        return full[0]

    def emit(gkeys, grads):
        grads = list(grads)
        pair = _pair_add(grads, _pair_exchange_halves(grads), core)
        in_flight.append((gkeys,) + tuple(_scatter_start(pair, _own_slot(pair, chip))))
        token = in_flight[-1][-1]
        if len(in_flight) > SCATTER_DEPTH:
            return token, collect(in_flight[-1 - SCATTER_DEPTH], (token,))
        return (token,)

    loss, grad_x, small_part = _local_step(x[0], loss_target[0], w, shapes, fetch, emit)
    loss = lax.psum(loss[0, 0], ("x", "y", "c"))

    small_res = _small_reduce_adamw(_allgather_all(small_part), *small_packs, REP_ROWS, SH_ROWS)
    out_g, out_d, out_m, out_v = {}, {}, {}, {}
    deps = (small_res[0],)
    for entry in in_flight[-SCATTER_DEPTH:]:
        collect(entry, deps)
        for n in BIG:
            layers = range(shapes[n][0])
            if n not in out_g and all((n, l) in reduced for l in layers):
                res = _adamw(wv[n], mv[n], vv[n], [reduced[(n, l)] for l in layers], ADAM_ROW_BLOCKS[n])
                out_g[n], out_d[n], out_m[n], out_v[n] = [view(r, n) for r in res]
                deps += (res[1],)

    for dst, packed in zip((out_g, out_d, out_m, out_v), small_res):
        dst.update(_unpack_rows(packed[:REP_ROWS], REP_SPEC, shapes))
        dst.update(_unpack_rows(packed[REP_ROWS:], SH_SPEC, shapes))

    return (loss, grad_x[None], *[out_g[n] for n in WEIGHT_NAMES], *[out_d[n] for n in WEIGHT_NAMES],
            *[out_m[n] for n in WEIGHT_NAMES], *[out_v[n] for n in WEIGHT_NAMES])
```

```python
import functools

import numpy as np
import jax
import jax.numpy as jnp
from jax import lax
from jax.experimental import pallas as pl
from jax.experimental.pallas import tpu as pltpu

F32 = jnp.float32
BF16 = jnp.bfloat16
MESH = pl.DeviceIdType.MESH

EPS = 1e-6
N_META = 16
D_MODEL = 1024
D_FF = 2816
N_SHARD = 4
FF_SHARD = D_FF // N_SHARD
D_POOL = 256
POOL_GROUP = 64
POOL_WINDOWS = (2, 4, 8, 16)
D_HGRN = 768
HG_HEADS = 6
HEAD = 128
HG_CHUNK = 64
HG_HEADS_PER_STEP = 2
HG_UNROLL = 2
D_IN_EVEN = D_POOL + 4 * D_HGRN
D_CONV = 512
CONV_WIDTH = 31
CONV_HALO = 32
D_LRU = 512
LRU_CONV = 4
LRU_HALO = 8
LRU_C = 8.0
D_IN_ODD = 2 * D_CONV + 2 * D_LRU
SUBLANE = 8
ROW_ALIGN = 64

ADAM_LR = 0.001
ADAM_B1 = 0.9
ADAM_B2 = 0.999
ADAM_EPS = 1e-08
ADAM_WD = 0.01
ADAM_STEP = 10

VMEM_LIMIT_MB = 56


def _cparams(n_grid_axes=0, vmem_mb=VMEM_LIMIT_MB):
    sem = ("arbitrary",) * n_grid_axes if n_grid_axes else None
    return pltpu.CompilerParams(dimension_semantics=sem, vmem_limit_bytes=vmem_mb * 1024 * 1024)


def _tile(n, target, mult):
    best = None
    for t in range(mult, min(n, target) + 1, mult):
        if n % t == 0:
            best = t
    assert best is not None, (n, target, mult)
    return best


def _dot(a, b):
    return jnp.dot(a, b, preferred_element_type=F32)


def _dot_nt(a, b):
    return lax.dot_general(a, b, (((1,), (1,)), ((), ())), preferred_element_type=F32)


def _dot_tn(a, b):
    return lax.dot_general(a, b, (((0,), (0,)), ((), ())), preferred_element_type=F32)


def _sigmoid(x):
    return 1.0 / (1.0 + jnp.exp(-x))


def _colsum(x):
    return jnp.sum(x, axis=0, keepdims=True)


def _rms_stats(h):
    rstd = lax.rsqrt(jnp.mean(h * h, axis=-1, keepdims=True) + EPS)
    return rstd, h * rstd


def _rms_bwd(dn, g, rstd, xhat):
    dng = dn * g
    dh = rstd * (dng - xhat * jnp.mean(dng * xhat, axis=-1, keepdims=True))
    return dh, _colsum(dn * xhat)


def _ffn_fwd(h, norm, wg4, wu4, wd4, layer, tm):
    tp, d = h.shape
    nt = tp // tm

    def body(h_ref, g_ref, wg_ref, wu_ref, wd_ref, ho_ref, ga_ref, gb_ref, sa_ref, n_ref, n_sc, acc):
        s = pl.program_id(1)

        @pl.when(s == 0)
        def _():
            hh = h_ref[...]
            rstd, xhat = _rms_stats(hh)
            n = (xhat * g_ref[...]).astype(BF16)
            n_sc[...] = n
            n_ref[...] = n
            acc[...] = jnp.zeros_like(acc)

        n = n_sc[...]
        a = _dot_nt(n, wg_ref[0])
        b = _dot_nt(n, wu_ref[0])
        sig = _sigmoid(a)
        sil = a * sig
        ga_ref[0] = (sig * (1.0 + a * (1.0 - sig)) * b).astype(BF16)
        gb_ref[0] = sil.astype(BF16)
        sg = (sil * b).astype(BF16)
        sa_ref[0] = sg
        acc[...] += _dot(sg, wd_ref[0])

        @pl.when(s == N_SHARD - 1)
        def _():
            ho_ref[...] = h_ref[...] + 0.5 * acc[...]

    return pl.pallas_call(
        body, name="ffn_fwd",
        grid=(nt, N_SHARD),
        in_specs=[
            pl.BlockSpec((tm, d), lambda i, s: (i, 0)),
            pl.BlockSpec((1, d), lambda i, s: (0, 0)),
            pl.BlockSpec((1, FF_SHARD, d), lambda i, s: (s, layer, 0)),
            pl.BlockSpec((1, FF_SHARD, d), lambda i, s: (s, layer, 0)),
            pl.BlockSpec((1, FF_SHARD, d), lambda i, s: (s, layer, 0)),
        ],
        out_specs=[
            pl.BlockSpec((tm, d), lambda i, s: (i, 0)),
            pl.BlockSpec((1, tm, FF_SHARD), lambda i, s: (s, i, 0)),
            pl.BlockSpec((1, tm, FF_SHARD), lambda i, s: (s, i, 0)),
            pl.BlockSpec((1, tm, FF_SHARD), lambda i, s: (s, i, 0)),
            pl.BlockSpec((tm, d), lambda i, s: (i, 0)),
        ],
        out_shape=[
            jax.ShapeDtypeStruct((tp, d), F32),
            jax.ShapeDtypeStruct((N_SHARD, tp, FF_SHARD), BF16),
            jax.ShapeDtypeStruct((N_SHARD, tp, FF_SHARD), BF16),
            jax.ShapeDtypeStruct((N_SHARD, tp, FF_SHARD), BF16),
            jax.ShapeDtypeStruct((tp, d), BF16),
        ],
        scratch_shapes=[pltpu.VMEM((tm, d), BF16), pltpu.VMEM((tm, d), F32)],
        compiler_params=_cparams(2),
    )(h, norm, wg4, wu4, wd4)


def _ffn_bwd_act(dho, h, norm, ga4, gb4, wg4, wu4, wd4, layer, tm, after=()):
    tp, d = h.shape
    nt = tp // tm

    def body(dho_ref, h_ref, g_ref, ga_ref, gb_ref, wg_ref, wu_ref, wd_ref, *rest):
        dh_ref, da_ref, db_ref, dg_ref, dy_ref, dn_sc = rest[len(after):]
        i = pl.program_id(0)
        s = pl.program_id(1)

        @pl.when(s == 0)
        def _():
            dy_ref[...] = (0.5 * dho_ref[...]).astype(BF16)
            dn_sc[...] = jnp.zeros_like(dn_sc)

        @pl.when((s == 0) & (i == 0))
        def _():
            dg_ref[...] = jnp.zeros_like(dg_ref)

        ds = _dot_nt(dy_ref[...], wd_ref[0])
        da = (ds * ga_ref[0].astype(F32)).astype(BF16)
        db = (ds * gb_ref[0].astype(F32)).astype(BF16)
        da_ref[0] = da
        db_ref[0] = db
        dn_sc[...] += _dot(da, wg_ref[0]) + _dot(db, wu_ref[0])

        @pl.when(s == N_SHARD - 1)
        def _():
            rstd, xhat = _rms_stats(h_ref[...])
            dh, dg = _rms_bwd(dn_sc[...], g_ref[...], rstd, xhat)
            dh_ref[...] = dho_ref[...] + dh
            dg_ref[...] += dg

    return pl.pallas_call(
        body, name="ffn_bwd_act",
        grid=(nt, N_SHARD),
        in_specs=[
            pl.BlockSpec((tm, d), lambda i, s: (i, 0)),
            pl.BlockSpec((tm, d), lambda i, s: (i, 0)),
            pl.BlockSpec((1, d), lambda i, s: (0, 0)),
            pl.BlockSpec((1, tm, FF_SHARD), lambda i, s: (s, i, 0)),
            pl.BlockSpec((1, tm, FF_SHARD), lambda i, s: (s, i, 0)),
            pl.BlockSpec((1, FF_SHARD, d), lambda i, s: (s, layer, 0)),
            pl.BlockSpec((1, FF_SHARD, d), lambda i, s: (s, layer, 0)),
            pl.BlockSpec((1, FF_SHARD, d), lambda i, s: (s, layer, 0)),
        ] + [pl.BlockSpec(memory_space=pl.ANY)] * len(after),
        out_specs=[
            pl.BlockSpec((tm, d), lambda i, s: (i, 0)),
            pl.BlockSpec((1, tm, FF_SHARD), lambda i, s: (s, i, 0)),
            pl.BlockSpec((1, tm, FF_SHARD), lambda i, s: (s, i, 0)),
            pl.BlockSpec((1, d), lambda i, s: (0, 0)),
            pl.BlockSpec((tm, d), lambda i, s: (i, 0)),
        ],
        out_shape=[
            jax.ShapeDtypeStruct((tp, d), F32),
            jax.ShapeDtypeStruct((N_SHARD, tp, FF_SHARD), BF16),
            jax.ShapeDtypeStruct((N_SHARD, tp, FF_SHARD), BF16),
            jax.ShapeDtypeStruct((1, d), F32),
            jax.ShapeDtypeStruct((tp, d), BF16),
        ],
        scratch_shapes=[pltpu.VMEM((tm, d), F32)],
        compiler_params=_cparams(2),
    )(dho, h, norm, ga4, gb4, wg4, wu4, wd4, *after)


def _ffn_bwd_w(dy, n, sa4, da4, db4, tm):
    tp, d = n.shape
    nt = tp // tm

    def body(dy_ref, n_ref, sa_ref, da_ref, db_ref, og_ref, ou_ref, od_ref, accg, accu, accd):
        i = pl.program_id(1)

        @pl.when(i == 0)
        def _():
            accg[...] = jnp.zeros_like(accg)
            accu[...] = jnp.zeros_like(accu)
            accd[...] = jnp.zeros_like(accd)

        nn = n_ref[...]
        accg[...] += _dot_tn(da_ref[0], nn)
        accu[...] += _dot_tn(db_ref[0], nn)
        accd[...] += _dot_tn(sa_ref[0], dy_ref[...])

        @pl.when(i == nt - 1)
        def _():
            og_ref[0] = accg[...].astype(BF16)
            ou_ref[0] = accu[...].astype(BF16)
            od_ref[0] = accd[...].astype(BF16)

    in_specs = [
        pl.BlockSpec((tm, d), lambda s, i: (i, 0)),
        pl.BlockSpec((tm, d), lambda s, i: (i, 0)),
        pl.BlockSpec((1, tm, FF_SHARD), lambda s, i: (s, i, 0)),
        pl.BlockSpec((1, tm, FF_SHARD), lambda s, i: (s, i, 0)),
        pl.BlockSpec((1, tm, FF_SHARD), lambda s, i: (s, i, 0)),
    ]
    return pl.pallas_call(
        body, name="ffn_bwd_w",
        grid=(N_SHARD, nt),
        in_specs=in_specs,
        out_specs=[pl.BlockSpec((1, FF_SHARD, d), lambda s, i: (s, 0, 0))] * 3,
        out_shape=[jax.ShapeDtypeStruct((N_SHARD, FF_SHARD, d), BF16)] * 3,
        scratch_shapes=[pltpu.VMEM((FF_SHARD, d), F32)] * 3,
        compiler_params=_cparams(2),
    )(dy, n, sa4, da4, db4)


def _proj_fwd(h, norm, layer, w_pieces, tm, wt=False):
    tp, d = h.shape
    widths = [bs[-2] if wt else bs[-1] for _, bs, _ in w_pieces]
    ntot = sum(widths)
    npc = len(w_pieces)

    def body(*refs):
        h_ref, g_ref = refs[:2]
        w_refs = refs[2:2 + npc]
        p_ref, n_ref = refs[2 + npc:]
        rstd, xhat = _rms_stats(h_ref[...])
        n = (xhat * g_ref[...]).astype(BF16)
        n_ref[...] = n
        off = 0
        for k in range(npc):
            w = w_refs[k][...]
            w = w.reshape(w.shape[-2], w.shape[-1])
            p_ref[:, off:off + widths[k]] = _dot_nt(n, w) if wt else _dot(n, w)
            off += widths[k]

    in_specs = [pl.BlockSpec((tm, d), lambda i: (i, 0)), pl.BlockSpec((1, d), lambda i: (0, 0))]
    for _, bs, idx in w_pieces:
        in_specs.append(pl.BlockSpec(bs, functools.partial(lambda i, idx: idx, idx=idx)))
    return pl.pallas_call(
        body, name="proj_fwd",
        grid=(tp // tm,),
        in_specs=in_specs,
        out_specs=[pl.BlockSpec((tm, ntot), lambda i: (i, 0)), pl.BlockSpec((tm, d), lambda i: (i, 0))],
        out_shape=[jax.ShapeDtypeStruct((tp, ntot), F32), jax.ShapeDtypeStruct((tp, d), BF16)],
        compiler_params=_cparams(1),
    )(h, norm, *[w for w, _, _ in w_pieces])


def _proj_bwd_act(dres, h, norm, layer, dp_pieces, w_pieces, tm, wt=False):
    tp, d = h.shape
    npc = len(w_pieces)

    def body(*refs):
        dres_ref, h_ref, g_ref = refs[:3]
        dp_refs = refs[3:3 + npc]
        w_refs = refs[3 + npc:3 + 2 * npc]
        dh_ref, dg_ref = refs[3 + 2 * npc:]
        i = pl.program_id(0)

        @pl.when(i == 0)
        def _():
            dg_ref[...] = jnp.zeros_like(dg_ref)

        dn = None
        for k in range(npc):
            w = w_refs[k][...]
            w = w.reshape(w.shape[-2], w.shape[-1])
            t = _dot(dp_refs[k][...], w) if wt else _dot_nt(dp_refs[k][...], w)
            dn = t if dn is None else dn + t
        rstd, xhat = _rms_stats(h_ref[...])
        dh, dg = _rms_bwd(dn, g_ref[...], rstd, xhat)
        dh_ref[...] = dres_ref[...] + dh
        dg_ref[...] += dg

    in_specs = [pl.BlockSpec((tm, d), lambda i: (i, 0)), pl.BlockSpec((tm, d), lambda i: (i, 0)),
                pl.BlockSpec((1, d), lambda i: (0, 0))]
    for dp in dp_pieces:
        in_specs.append(pl.BlockSpec((tm, dp.shape[1]), lambda i: (i, 0)))
    for _, bs, idx in w_pieces:
        in_specs.append(pl.BlockSpec(bs, functools.partial(lambda i, idx: idx, idx=idx)))
    return pl.pallas_call(
        body, name="proj_bwd_act",
        grid=(tp // tm,),
        in_specs=in_specs,
        out_specs=[pl.BlockSpec((tm, d), lambda i: (i, 0)), pl.BlockSpec((1, d), lambda i: (0, 0))],
        out_shape=[jax.ShapeDtypeStruct((tp, d), F32), jax.ShapeDtypeStruct((1, d), F32)],
        compiler_params=_cparams(1),
    )(dres, h, norm, *dp_pieces, *[w for w, _, _ in w_pieces])


def _proj_bwd_w(n, dp_pieces, tm, wt=False):
    tp, d = n.shape
    npc = len(dp_pieces)
    widths = [dp.shape[1] for dp in dp_pieces]
    oshape = (lambda w: (w, d)) if wt else (lambda w: (d, w))

    def body(*refs):
        n_ref = refs[0]
        dp_refs = refs[1:1 + npc]
        o_refs = refs[1 + npc:1 + 2 * npc]
        accs = refs[1 + 2 * npc:]
        i = pl.program_id(0)

        @pl.when(i == 0)
        def _():
            for acc in accs:
                acc[...] = jnp.zeros_like(acc)

        nn = n_ref[...]
        for k in range(npc):
            accs[k][...] += _dot_tn(dp_refs[k][...], nn) if wt else _dot_tn(nn, dp_refs[k][...])

        @pl.when(i == pl.num_programs(0) - 1)
        def _():
            for k in range(npc):
                o_refs[k][...] = accs[k][...].astype(BF16)

    return pl.pallas_call(
        body, name="proj_bwd_w",
        grid=(tp // tm,),
        in_specs=[pl.BlockSpec((tm, d), lambda i: (i, 0))]
        + [pl.BlockSpec((tm, w), lambda i: (i, 0)) for w in widths],
        out_specs=[pl.BlockSpec(oshape(w), lambda i: (0, 0)) for w in widths],
        out_shape=[jax.ShapeDtypeStruct(oshape(w), BF16) for w in widths],
        scratch_shapes=[pltpu.VMEM(oshape(w), F32) for w in widths],
        compiler_params=_cparams(1),
    )(n, *dp_pieces)


def _out_fwd(h, ya, yb, w, tm):
    tp, d = h.shape
    na, nb = ya.shape[1], yb.shape[1]

    def body(h_ref, ya_ref, yb_ref, w_ref, o_ref):
        y = _dot(ya_ref[...].astype(BF16), w_ref[0:na, :]) + _dot(yb_ref[...].astype(BF16), w_ref[na:, :])
        o_ref[...] = h_ref[...] + y

    return pl.pallas_call(
        body, name="out_fwd",
        grid=(tp // tm,),
        in_specs=[pl.BlockSpec((tm, d), lambda i: (i, 0)), pl.BlockSpec((tm, na), lambda i: (i, 0)),
                  pl.BlockSpec((tm, nb), lambda i: (i, 0)), pl.BlockSpec((d, d), lambda i: (0, 0))],
        out_specs=pl.BlockSpec((tm, d), lambda i: (i, 0)),
        out_shape=jax.ShapeDtypeStruct((tp, d), F32),
        compiler_params=_cparams(1),
    )(h, ya, yb, w)


def _out_bwd(dy, ya, yb, w, tm, after=()):
    tp, d = dy.shape
    na, nb = ya.shape[1], yb.shape[1]

    def body(dy_ref, ya_ref, yb_ref, w_ref, *rest):
        da_ref, db_ref, dw_ref, acc = rest[len(after):]
        i = pl.program_id(0)

        @pl.when(i == 0)
        def _():
            acc[...] = jnp.zeros_like(acc)

        dyb16 = dy_ref[...].astype(BF16)
        da_ref[...] = _dot_nt(dyb16, w_ref[0:na, :])
        db_ref[...] = _dot_nt(dyb16, w_ref[na:, :])
        acc[0:na, :] += _dot_tn(ya_ref[...].astype(BF16), dyb16)
        acc[na:, :] += _dot_tn(yb_ref[...].astype(BF16), dyb16)

        @pl.when(i == pl.num_programs(0) - 1)
        def _():
            dw_ref[...] = acc[...].astype(BF16)

    return pl.pallas_call(
        body, name="out_bwd",
        grid=(tp // tm,),
        in_specs=[pl.BlockSpec((tm, d), lambda i: (i, 0)), pl.BlockSpec((tm, na), lambda i: (i, 0)),
                  pl.BlockSpec((tm, nb), lambda i: (i, 0)), pl.BlockSpec((d, d), lambda i: (0, 0))]
        + [pl.BlockSpec(memory_space=pl.ANY)] * len(after),
        out_specs=[pl.BlockSpec((tm, na), lambda i: (i, 0)), pl.BlockSpec((tm, nb), lambda i: (i, 0)),
                   pl.BlockSpec((d, d), lambda i: (0, 0))],
        out_shape=[jax.ShapeDtypeStruct((tp, na), F32), jax.ShapeDtypeStruct((tp, nb), F32),
                   jax.ShapeDtypeStruct((d, d), BF16)],
        scratch_shapes=[pltpu.VMEM((d, d), F32)],
        compiler_params=_cparams(1),
    )(dy, ya, yb, w, *after)


def _loss_bwd(h, gfin, tgt, t_real, tm):
    tp, d = h.shape

    def body(h_ref, g_ref, t_ref, loss_ref, dh_ref, dg_ref):
        i = pl.program_id(0)

        @pl.when(i == 0)
        def _():
            loss_ref[...] = jnp.zeros_like(loss_ref)
            dg_ref[...] = jnp.zeros_like(dg_ref)

        rows = i * tm + lax.broadcasted_iota(jnp.int32, (tm, 1), 0)
        valid = (rows >= N_META) & (rows < t_real)
        rstd, xhat = _rms_stats(h_ref[...])
        g = g_ref[...]
        err = jnp.where(valid, xhat * g - t_ref[...], 0.0)
        e2 = jnp.sum(err * err, axis=1, keepdims=True)
        loss_ref[...] += (0.5 / d) * jnp.sum(e2, axis=0, keepdims=True)
        dy = err * (1.0 / d)
        dh, dg = _rms_bwd(dy, g, rstd, xhat)
        dh_ref[...] = dh
        dg_ref[...] += dg

    return pl.pallas_call(
        body, name="loss_bwd",
        grid=(tp // tm,),
        in_specs=[pl.BlockSpec((tm, d), lambda i: (i, 0)), pl.BlockSpec((1, d), lambda i: (0, 0)),
                  pl.BlockSpec((tm, d), lambda i: (i, 0))],
        out_specs=[pl.BlockSpec((1, 1), lambda i: (0, 0)), pl.BlockSpec((tm, d), lambda i: (i, 0)),
                   pl.BlockSpec((1, d), lambda i: (0, 0))],
        out_shape=[jax.ShapeDtypeStruct((1, 1), F32), jax.ShapeDtypeStruct((tp, d), F32),
                   jax.ShapeDtypeStruct((1, d), F32)],
        compiler_params=_cparams(1),
    )(h, gfin, tgt)


POOL_HALO = 16


def _pool_lane_consts(n_rows):
    lane = lax.broadcasted_iota(jnp.int32, (n_rows, D_POOL), 1)
    grp = lane // POOL_GROUP
    win = jnp.where(grp == 0, 2.0, jnp.where(grp == 1, 4.0, jnp.where(grp == 2, 8.0, 16.0)))
    return grp, win


def _pool_select(grp, s2, s4, s8, s16):
    return jnp.where(grp == 0, s2, jnp.where(grp == 1, s4, jnp.where(grp == 2, s8, s16)))


def _pool_mixed(x, row0, tr):
    n = tr + POOL_HALO
    s2 = x + pltpu.roll(x, 1, 0)
    s4 = s2 + pltpu.roll(s2, 2, 0)
    s8 = s4 + pltpu.roll(s4, 4, 0)
    s16 = s8 + pltpu.roll(s8, 8, 0)
    grp, win = _pool_lane_consts(n)
    rows = row0 - POOL_HALO + lax.broadcasted_iota(jnp.int32, (n, D_POOL), 0)
    cnt = jnp.minimum((rows + 1).astype(F32), win)
    pooled = _pool_select(grp, s2, s4, s8, s16) / jnp.maximum(cnt, 1.0)
    return (pooled - x)[POOL_HALO:, :]


def _pool_fwd(p, wbd, scale, tr):
    tp = p.shape[0]
    nt = tp // tr

    def body(p_ref, w_ref, s_ref, y_ref, usc):
        usc[0:POOL_HALO, :] = jnp.zeros((POOL_HALO, D_POOL), F32)
        usc[POOL_HALO:, :] = p_ref[...]

        def tile(r, carry):
            r0 = pl.multiple_of(r * tr, SUBLANE)
            x = usc[pl.ds(r0, tr + POOL_HALO), :]
            mixed = _pool_mixed(x, r0, tr)
            y_ref[pl.ds(r0, tr), :] = _dot(mixed.astype(BF16), w_ref[...]) * s_ref[...]
            return carry

        lax.fori_loop(0, nt, tile, 0)

    return pl.pallas_call(
        body, name="pool_fwd",
        grid=(1,),
        in_specs=[pl.BlockSpec((tp, D_POOL), lambda i: (0, 0)), pl.BlockSpec((D_POOL, D_POOL), lambda i: (0, 0)),
                  pl.BlockSpec((1, D_POOL), lambda i: (0, 0))],
        out_specs=pl.BlockSpec((tp, D_POOL), lambda i: (0, 0)),
        out_shape=jax.ShapeDtypeStruct((tp, D_POOL), F32),
        scratch_shapes=[pltpu.VMEM((tp + POOL_HALO, D_POOL), F32)],
        compiler_params=_cparams(1),
    )(p, wbd, scale)


def _pool_bwd(p, dya, wbd, scale, tr):
    tp = p.shape[0]
    nt = tp // tr

    def body(p_ref, dy_ref, w_ref, s_ref, du_ref, dw_ref, ds_ref, usc, gsc):
        usc[0:POOL_HALO, :] = jnp.zeros((POOL_HALO, D_POOL), F32)
        usc[POOL_HALO:, :] = p_ref[...]
        gsc[tp:, :] = jnp.zeros((POOL_HALO, D_POOL), F32)
        dw_ref[...] = jnp.zeros_like(dw_ref)
        ds_ref[...] = jnp.zeros_like(ds_ref)
        grp, win = _pool_lane_consts(tr)

        def tile1(r, carry):
            r0 = pl.multiple_of(r * tr, SUBLANE)
            x = usc[pl.ds(r0, tr + POOL_HALO), :]
            mixed = _pool_mixed(x, r0, tr).astype(BF16)
            dy = dy_ref[pl.ds(r0, tr), :]
            dys = (dy * s_ref[...]).astype(BF16)
            ypre = _dot(mixed, w_ref[...])
            ds_ref[...] += _colsum(dy * ypre)
            dw_ref[...] += _dot_tn(mixed, dys)
            dmx = _dot_nt(dys, w_ref[...])
            rows = r0 + lax.broadcasted_iota(jnp.int32, (tr, D_POOL), 0)
            cnt = jnp.minimum((rows + 1).astype(F32), win)
            gsc[pl.ds(r0, tr), :] = dmx / cnt
            return carry

        lax.fori_loop(0, nt, tile1, 0)
        n = tr + POOL_HALO
        grp2, win2 = _pool_lane_consts(n)

        def tile2(r, carry):
            r0 = pl.multiple_of(r * tr, SUBLANE)
            g = gsc[pl.ds(r0, n), :]
            s2 = g + pltpu.roll(g, n - 1, 0)
            s4 = s2 + pltpu.roll(s2, n - 2, 0)
            s8 = s4 + pltpu.roll(s4, n - 4, 0)
            s16 = s8 + pltpu.roll(s8, n - 8, 0)
            pooled_t = _pool_select(grp2, s2, s4, s8, s16)
            rows = r0 + lax.broadcasted_iota(jnp.int32, (n, D_POOL), 0)
            cnt = jnp.minimum((rows + 1).astype(F32), win2)
            du = pooled_t - g * cnt
            du_ref[pl.ds(r0, tr), :] = du[0:tr, :].astype(BF16)
            return carry

        lax.fori_loop(0, nt, tile2, 0)

    return pl.pallas_call(
        body, name="pool_bwd",
        grid=(1,),
        in_specs=[pl.BlockSpec((tp, D_POOL), lambda i: (0, 0)), pl.BlockSpec((tp, D_POOL), lambda i: (0, 0)),
                  pl.BlockSpec((D_POOL, D_POOL), lambda i: (0, 0)), pl.BlockSpec((1, D_POOL), lambda i: (0, 0))],
        out_specs=[pl.BlockSpec((tp, D_POOL), lambda i: (0, 0)), pl.BlockSpec((D_POOL, D_POOL), lambda i: (0, 0)),
                   pl.BlockSpec((1, D_POOL), lambda i: (0, 0))],
        out_shape=[jax.ShapeDtypeStruct((tp, D_POOL), BF16), jax.ShapeDtypeStruct((D_POOL, D_POOL), F32),
                   jax.ShapeDtypeStruct((1, D_POOL), F32)],
        scratch_shapes=[pltpu.VMEM((tp + POOL_HALO, D_POOL), F32), pltpu.VMEM((tp + POOL_HALO, D_POOL), F32)],
        compiler_params=_cparams(1),
    )(p, dya, wbd, scale)


def _hgrn_levels(ch):
    levels = []
    w = ch // 2
    while w >= 1:
        levels.append(w)
        w //= 2
    return levels


def _hgrn_consts(ch):
    t = np.arange(ch)
    tril = t[None, :] <= t[:, None]
    masks = []
    for w in _hgrn_levels(ch):
        blk = t // (2 * w)
        upper = t % (2 * w) >= w
        masks.append(upper[:, None] & (~upper)[None, :] & (blk[:, None] == blk[None, :]))
    masks.append(tril)
    msk = np.stack(masks).astype(np.float32)
    return jnp.asarray(tril.astype(np.float32), BF16), jnp.asarray(msk, F32), len(masks) - 1


def _split3(x):
    hi = x.astype(BF16)
    r1 = x - hi.astype(F32)
    mid = r1.astype(BF16)
    lo = (r1 - mid.astype(F32)).astype(BF16)
    return hi, mid, lo


def _hgrn_exponents(tril, logf):
    ch = logf.shape[0]
    hi, mid, lo = _split3(logf)
    x = _dot(tril, jnp.concatenate([hi, mid, lo], axis=1))
    b = x[:, 0:HEAD] + x[:, HEAD:2 * HEAD] + x[:, 2 * HEAD:3 * HEAD]
    rows = lax.broadcasted_iota(jnp.int32, (ch, HEAD), 0)
    fx = jnp.broadcast_to(b[ch - 1:ch, :], (ch, HEAD)) - b
    lev = []
    for w in _hgrn_levels(ch):
        pos = rows % (2 * w)
        upper = pos >= w
        if w >= SUBLANE:
            parts = [jnp.broadcast_to(b[k * 2 * w + w - 1:k * 2 * w + w, :], (2 * w, HEAD))
                     for k in range(ch // (2 * w))]
            bmid = parts[0] if len(parts) == 1 else jnp.concatenate(parts, axis=0)
            dx = jnp.where(upper, b - bmid, 0.0)
            ex = jnp.where(upper, 0.0, bmid - b)
        else:
            dx = logf
            ex = jnp.zeros_like(logf)
            for i in range(1, w):
                dx = dx + jnp.where(pos >= w + i, pltpu.roll(logf, i, 0), 0.0)
                ex = ex + jnp.where(pos <= w - 1 - i, pltpu.roll(logf, ch - i, 0), 0.0)
            dx = jnp.where(upper, dx, 0.0)
        lev.append((dx, ex))
    return b, fx, lev


def _hgrn_exponents_bwd(tril, d_b, d_fx, d_blast, lev_grads):
    ch = d_b.shape[0]
    rows = lax.broadcasted_iota(jnp.int32, (ch, HEAD), 0)
    db = d_b - d_fx
    dlf = jnp.zeros_like(d_b)
    for w, (ddx, dex) in zip(_hgrn_levels(ch), lev_grads):
        pos = rows % (2 * w)
        upper = pos >= w
        gu = jnp.where(upper, ddx, 0.0)
        if w >= SUBLANE:
            gl = jnp.where(upper, 0.0, dex)
            db = db + gu - gl
            diff = gl - gu
            for k in range(ch // (2 * w)):
                s = _colsum(diff[k * 2 * w:(k + 1) * 2 * w, :])
                db = db + jnp.where(rows == k * 2 * w + w - 1, s, 0.0)
        else:
            dlf = dlf + gu
            for i in range(1, w):
                dlf = dlf + pltpu.roll(jnp.where(pos >= w + i, gu, 0.0), ch - i, 0)
                dlf = dlf + pltpu.roll(jnp.where(pos <= w - 1 - i, dex, 0.0), i, 0)
    db = db + jnp.where(rows == ch - 1, _colsum(d_fx) + d_blast, 0.0)
    hi = db.astype(BF16)
    lo = (db - hi.astype(F32)).astype(BF16)
    d2 = _dot_tn(tril, jnp.concatenate([hi, lo], axis=1))
    return d2[:, 0:HEAD] + d2[:, HEAD:2 * HEAD] + dlf


def _hgrn_gates(q_raw, z, lb):
    sz = _sigmoid(z)
    f = lb + (1.0 - lb) * sz
    q = q_raw * _sigmoid(q_raw)
    k = (1.0 - lb) * (1.0 - sz)
    return q, k, f, sz


def _hgrn_intra(q, k, lev, msk_ref, n_lev, ch):
    eye = (lax.broadcasted_iota(jnp.int32, (ch, ch), 0) == lax.broadcasted_iota(jnp.int32, (ch, ch), 1))
    a = jnp.where(eye, jnp.sum(q * k, axis=1, keepdims=True), 0.0)
    ops = []
    for lv in range(n_lev):
        eq = jnp.exp(lev[lv][0])
        ek = jnp.exp(lev[lv][1])
        qd = q * eq
        kd = k * ek
        a = a + msk_ref[lv] * _dot_nt(qd.astype(BF16), kd.astype(BF16))
        ops.append((eq, ek, qd, kd))
    return a, ops


def _hgrn_fwd(p, lb_logits, gnorm, mst, msk, n_lev, tm):
    tp = p.shape[0]
    ch = HG_CHUNK
    nct = tm // ch
    nt = tp // tm
    nr = mst.shape[0]
    base = D_POOL // HEAD

    def body(q_ref, z_ref, v_ref, g_ref, lg_ref, gn_ref, mst_ref, msk_ref, y_ref, ss_ref, st_sc):
        @pl.when(pl.program_id(1) == 0)
        def _():
            st_sc[...] = jnp.zeros_like(st_sc)

        lb_all = _sigmoid(lg_ref[0:1, :] - lg_ref[1:2, :])

        def one_head(hh, c, r0):
            ls = slice(hh * HEAD, (hh + 1) * HEAD)
            lb = lb_all[:, ls]
            q, k, f, _ = _hgrn_gates(q_ref[pl.ds(r0, ch), ls], z_ref[pl.ds(r0, ch), ls], lb)
            v = v_ref[pl.ds(r0, ch), ls]
            b, fx, lev = _hgrn_exponents(mst_ref[...], jnp.log(f))
            st = st_sc[hh]
            ss_ref[hh, c] = st
            qe = q * jnp.exp(b)
            a, _ = _hgrn_intra(q, k, lev, msk_ref, n_lev, ch)
            v16 = v.astype(BF16)
            o = _dot_nt(qe.astype(BF16), st.astype(BF16)) + _dot(a.astype(BF16), v16)
            kl = k * jnp.exp(fx)
            st_sc[hh] = st * jnp.exp(b[ch - 1:ch, :]) + _dot_tn(v16, kl.astype(BF16))
            rstd = lax.rsqrt(jnp.mean(o * o, axis=-1, keepdims=True) + EPS)
            g_raw = g_ref[pl.ds(r0, ch), ls]
            y_ref[pl.ds(r0, ch), ls] = o * rstd * gn_ref[...] * (g_raw * _sigmoid(g_raw))

        def chunk(c, carry):
            r0 = pl.multiple_of(c * ch, ch)
            for hh in range(HG_HEADS_PER_STEP):
                one_head(hh, c, r0)
            return carry

        lax.fori_loop(0, nct, chunk, 0, unroll=HG_UNROLL)

    hp = HG_HEADS_PER_STEP
    wide = hp * HEAD

    def pspec(seg):
        return pl.BlockSpec((tm, wide), lambda h, i: (i, (base + seg * HG_HEADS) // hp + h))

    return pl.pallas_call(
        body, name="hgrn_fwd",
        grid=(HG_HEADS // hp, nt),
        in_specs=[pspec(0), pspec(1), pspec(2), pspec(3),
                  pl.BlockSpec((2, wide), lambda h, i: (0, h)),
                  pl.BlockSpec((1, HEAD), lambda h, i: (0, 0)),
                  pl.BlockSpec((nr, ch), lambda h, i: (0, 0)),
                  pl.BlockSpec((n_lev + 1, ch, ch), lambda h, i: (0, 0, 0))],
        out_specs=[pl.BlockSpec((tm, wide), lambda h, i: (i, h)),
                   pl.BlockSpec((hp, nct, HEAD, HEAD), lambda h, i: (h, i, 0, 0))],
        out_shape=[jax.ShapeDtypeStruct((tp, D_HGRN), F32),
                   jax.ShapeDtypeStruct((HG_HEADS, tp // ch, HEAD, HEAD), F32)],
        scratch_shapes=[pltpu.VMEM((hp, HEAD, HEAD), F32)],
        compiler_params=_cparams(2),
    )(p, p, p, p, lb_logits, gnorm, mst, msk)


def _hgrn_bwd(p, dyb, states, lb_logits, gnorm, mst, msk, n_lev, tm):
    tp = p.shape[0]
    ch = HG_CHUNK
    nct = tm // ch
    nt = tp // tm
    nr = mst.shape[0]
    base = D_POOL // HEAD

    def body(q_ref, z_ref, v_ref, g_ref, dy_ref, ss_ref, lg_ref, gn_ref, mst_ref, msk_ref,
             dq_ref, dz_ref, dv_ref, dg_ref, dlg_ref, dgn_ref, dst_sc, dlb_sc):
        ti = pl.program_id(1)

        @pl.when(ti == 0)
        def _():
            dst_sc[...] = jnp.zeros_like(dst_sc)
            dlb_sc[...] = jnp.zeros_like(dlb_sc)
            dgn_ref[...] = jnp.zeros_like(dgn_ref)

        lb_all = _sigmoid(lg_ref[0:1, :] - lg_ref[1:2, :])
        gn = gn_ref[...]

        def load_head(hh, c, r0):
            ls = slice(hh * HEAD, (hh + 1) * HEAD)
            return (q_ref[pl.ds(r0, ch), ls], z_ref[pl.ds(r0, ch), ls], v_ref[pl.ds(r0, ch), ls],
                    g_ref[pl.ds(r0, ch), ls], dy_ref[pl.ds(r0, ch), ls], ss_ref[hh, c], dst_sc[hh])

        def store_head(hh, r0, res):
            ls = slice(hh * HEAD, (hh + 1) * HEAD)
            dq_raw, dz, dv, dg_raw, dgn, dst_new, dlb = res
            dq_ref[pl.ds(r0, ch), ls] = dq_raw
            dz_ref[pl.ds(r0, ch), ls] = dz
            dv_ref[pl.ds(r0, ch), ls] = dv
            dg_ref[pl.ds(r0, ch), ls] = dg_raw
            dgn_ref[hh] += dgn
            dst_sc[hh] = dst_new
            dlb_sc[:, ls] += dlb

        def one_head(hh, loaded):
            ls = slice(hh * HEAD, (hh + 1) * HEAD)
            lb = lb_all[:, ls]
            q_raw, z, v, g_raw, dy, st, dst = loaded
            q, k, f, sz = _hgrn_gates(q_raw, z, lb)
            b, fx, lev = _hgrn_exponents(mst_ref[...], jnp.log(f))
            eb = jnp.exp(b)
            ef = jnp.exp(fx)
            elast = jnp.exp(b[ch - 1:ch, :])
            qe = q * eb
            kl = k * ef
            a, ops = _hgrn_intra(q, k, lev, msk_ref, n_lev, ch)
            v16 = v.astype(BF16)
            st16 = st.astype(BF16)
            qe16 = qe.astype(BF16)
            kl16 = kl.astype(BF16)
            a16 = a.astype(BF16)
            o = _dot_nt(qe16, st16) + _dot(a16, v16)
            sg = _sigmoid(g_raw)
            rstd = lax.rsqrt(jnp.mean(o * o, axis=-1, keepdims=True) + EPS)
            oh = o * rstd
            dg_out = (dy * oh * gn * (sg * (1.0 + g_raw * (1.0 - sg)))).astype(BF16)
            don = dy * (g_raw * sg)
            dgn = _colsum(don * oh)
            doh = don * gn
            do = rstd * (doh - oh * jnp.mean(doh * oh, axis=-1, keepdims=True))
            do16 = do.astype(BF16)
            dst16 = dst.astype(BF16)
            dv = _dot_tn(a16, do16) + _dot_nt(kl16, dst16)
            da = msk_ref[n_lev] * _dot_nt(do16, v16)
            dqe = _dot(do16, st16)
            dkl = _dot(v16, dst16)
            dst_new = dst * elast + _dot_tn(do16, qe16)
            db_last = _colsum(dst * st) * elast
            dad = jnp.sum(do * v, axis=1, keepdims=True)
            dq = dad * k + dqe * eb
            dk = dad * q + dkl * ef
            lev_grads = []
            for lv in range(n_lev):
                eq, ek, qd, kd = ops[lv]
                gl = (msk_ref[lv] * da).astype(BF16)
                dqd = _dot(gl, kd.astype(BF16))
                dkd = _dot_tn(gl, qd.astype(BF16))
                dq = dq + dqd * eq
                dk = dk + dkd * ek
                lev_grads.append((dqd * qd, dkd * kd))
            dlogf = _hgrn_exponents_bwd(mst_ref[...], dqe * qe, dkl * kl, db_last, lev_grads)
            sq = _sigmoid(q_raw)
            dq_out = (dq * (sq * (1.0 + q_raw * (1.0 - sq)))).astype(BF16)
            dfk = dlogf / f - dk
            dz_out = (dfk * (1.0 - lb) * sz * (1.0 - sz)).astype(BF16)
            return dq_out, dz_out, dv.astype(BF16), dg_out, dgn, dst_new, _colsum(dfk * (1.0 - sz))

        def chunk(cc, carry):
            c = nct - 1 - cc
            r0 = pl.multiple_of(c * ch, ch)
            loaded = [load_head(hh, c, r0) for hh in range(HG_HEADS_PER_STEP)]
            results = [one_head(hh, loaded[hh]) for hh in range(HG_HEADS_PER_STEP)]
            for hh in range(HG_HEADS_PER_STEP):
                store_head(hh, r0, results[hh])
            return carry

        lax.fori_loop(0, nct, chunk, 0, unroll=1)

        @pl.when(ti == nt - 1)
        def _():
            dl0 = dlb_sc[...] * lb_all * (1.0 - lb_all)
            dlg_ref[0:1, :] = dl0
            dlg_ref[1:2, :] = -dl0

    hp = HG_HEADS_PER_STEP
    wide = hp * HEAD

    def pspec(seg):
        return pl.BlockSpec((tm, wide), lambda h, i: (nt - 1 - i, (base + seg * HG_HEADS) // hp + h))

    ospec = pl.BlockSpec((tm, wide), lambda h, i: (nt - 1 - i, h))
    return pl.pallas_call(
        body, name="hgrn_bwd",
        grid=(HG_HEADS // hp, nt),
        in_specs=[pspec(0), pspec(1), pspec(2), pspec(3), ospec,
                  pl.BlockSpec((hp, nct, HEAD, HEAD), lambda h, i: (h, nt - 1 - i, 0, 0)),
                  pl.BlockSpec((2, wide), lambda h, i: (0, h)),
                  pl.BlockSpec((1, HEAD), lambda h, i: (0, 0)),
                  pl.BlockSpec((nr, ch), lambda h, i: (0, 0)),
                  pl.BlockSpec((n_lev + 1, ch, ch), lambda h, i: (0, 0, 0))],
        out_specs=[ospec, ospec, ospec, ospec,
                   pl.BlockSpec((2, wide), lambda h, i: (0, h)),
                   pl.BlockSpec((hp, 1, HEAD), lambda h, i: (h, 0, 0))],
        out_shape=[jax.ShapeDtypeStruct((tp, D_HGRN), BF16)] * 4
        + [jax.ShapeDtypeStruct((2, D_HGRN), F32), jax.ShapeDtypeStruct((HG_HEADS, 1, HEAD), F32)],
        scratch_shapes=[pltpu.VMEM((hp, HEAD, HEAD), F32), pltpu.VMEM((1, wide), F32)],
        compiler_params=_cparams(2),
    )(p, p, p, p, dyb, states, lb_logits, gnorm, mst, msk)


def _conv_taps(x, w_ref, tr, halo, width):
    acc = None
    for j in range(width):
        sh = width - 1 - j
        xs = x if sh == 0 else pltpu.roll(x, sh, 0)
        term = xs[halo:, :] * w_ref[j:j + 1, :]
        acc = term if acc is None else acc + term
    return acc


def _conv_taps_t(y, w_ref, tr, halo, width):
    n = tr + halo
    acc = None
    for j in range(width):
        sh = width - 1 - j
        ys = y if sh == 0 else pltpu.roll(y, n - sh, 0)
        term = ys[0:tr, :] * w_ref[j:j + 1, :]
        acc = term if acc is None else acc + term
    return acc


def _ln_stats(cv):
    mu = jnp.mean(cv, axis=-1, keepdims=True)
    xc = cv - mu
    rstd = lax.rsqrt(jnp.mean(xc * xc, axis=-1, keepdims=True) + EPS)
    return rstd, xc * rstd


def _convmod_fwd(p, w, bias, ln_g, ln_b, tr):
    tp = p.shape[0]
    nt = tp // tr
    nb = D_CONV // HEAD

    def body(a_ref, b_ref, w_ref, bi_ref, g_ref, be_ref, y_ref, usc):
        usc[0:CONV_HALO, :] = jnp.zeros((CONV_HALO, HEAD), F32)
        usc[CONV_HALO:, :] = a_ref[...] * _sigmoid(b_ref[...])

        def tile(r, carry):
            r0 = pl.multiple_of(r * tr, SUBLANE)
            x = usc[pl.ds(r0, tr + CONV_HALO), :]
            cv = _conv_taps(x, w_ref, tr, CONV_HALO, CONV_WIDTH) + bi_ref[...]
            _, xh = _ln_stats(cv)
            un = xh * g_ref[...] + be_ref[...]
            y_ref[pl.ds(r0, tr), :] = un * _sigmoid(un)
            return carry

        lax.fori_loop(0, nt, tile, 0)

    vec = lambda: pl.BlockSpec((1, HEAD), lambda j: (0, j))
    return pl.pallas_call(
        body, name="convmod_fwd",
        grid=(nb,),
        in_specs=[pl.BlockSpec((tp, HEAD), lambda j: (0, j)), pl.BlockSpec((tp, HEAD), lambda j: (0, nb + j)),
                  pl.BlockSpec((CONV_HALO, HEAD), lambda j: (0, j)), vec(), vec(), vec()],
        out_specs=pl.BlockSpec((tp, HEAD), lambda j: (0, j)),
        out_shape=jax.ShapeDtypeStruct((tp, D_CONV), F32),
        scratch_shapes=[pltpu.VMEM((tp + CONV_HALO, HEAD), F32)],
        compiler_params=_cparams(1),
    )(p, p, w, bias, ln_g, ln_b)


def _convmod_bwd(p, dyc, w, bias, ln_g, ln_b, tr):
    tp = p.shape[0]
    nt = tp // tr
    nb = D_CONV // HEAD

    def body(a_ref, b_ref, dy_ref, w_ref, bi_ref, g_ref, be_ref, da_ref, db_ref, dw_ref, dv_ref, usc, dsc):
        usc[0:CONV_HALO, :] = jnp.zeros((CONV_HALO, HEAD), F32)
        usc[CONV_HALO:, :] = a_ref[...] * _sigmoid(b_ref[...])
        dsc[tp:, :] = jnp.zeros((CONV_HALO, HEAD), F32)
        dw_ref[...] = jnp.zeros_like(dw_ref)
        dv_ref[...] = jnp.zeros_like(dv_ref)

        def tile1(r, carry):
            r0 = pl.multiple_of(r * tr, SUBLANE)
            x = usc[pl.ds(r0, tr + CONV_HALO), :]
            cv = _conv_taps(x, w_ref, tr, CONV_HALO, CONV_WIDTH) + bi_ref[...]
            rstd, xh = _ln_stats(cv)
            un = xh * g_ref[...] + be_ref[...]
            sg = _sigmoid(un)
            dun = dy_ref[pl.ds(r0, tr), :] * (sg * (1.0 + un * (1.0 - sg)))
            dv_ref[0, 1:2, :] += _colsum(dun * xh)
            dv_ref[0, 2:3, :] += _colsum(dun)
            dxh = dun * g_ref[...]
            dcv = rstd * (dxh - jnp.mean(dxh, axis=-1, keepdims=True)
                          - xh * jnp.mean(dxh * xh, axis=-1, keepdims=True))
            dv_ref[0, 0:1, :] += _colsum(dcv)
            for j in range(CONV_WIDTH):
                sh = CONV_WIDTH - 1 - j
                xs = x if sh == 0 else pltpu.roll(x, sh, 0)
                dw_ref[0, j:j + 1, :] += _colsum(dcv * xs[CONV_HALO:, :])
            dsc[pl.ds(r0, tr), :] = dcv
            return carry

        lax.fori_loop(0, nt, tile1, 0)

        def tile2(r, carry):
            r0 = pl.multiple_of(r * tr, SUBLANE)
            y = dsc[pl.ds(r0, tr + CONV_HALO), :]
            du = _conv_taps_t(y, w_ref, tr, CONV_HALO, CONV_WIDTH)
            a = a_ref[pl.ds(r0, tr), :]
            sb = _sigmoid(b_ref[pl.ds(r0, tr), :])
            da_ref[pl.ds(r0, tr), :] = (du * sb).astype(BF16)
            db_ref[pl.ds(r0, tr), :] = (du * a * sb * (1.0 - sb)).astype(BF16)
            return carry

        lax.fori_loop(0, nt, tile2, 0)

    vec = lambda: pl.BlockSpec((1, HEAD), lambda j: (0, j))
    col = lambda: pl.BlockSpec((tp, HEAD), lambda j: (0, j))
    return pl.pallas_call(
        body, name="convmod_bwd",
        grid=(nb,),
        in_specs=[col(), pl.BlockSpec((tp, HEAD), lambda j: (0, nb + j)), col(),
                  pl.BlockSpec((CONV_HALO, HEAD), lambda j: (0, j)), vec(), vec(), vec()],
        out_specs=[col(), col(), pl.BlockSpec((1, CONV_HALO, HEAD), lambda j: (j, 0, 0)),
                   pl.BlockSpec((1, SUBLANE, HEAD), lambda j: (j, 0, 0))],
        out_shape=[jax.ShapeDtypeStruct((tp, D_CONV), BF16), jax.ShapeDtypeStruct((tp, D_CONV), BF16),
                   jax.ShapeDtypeStruct((nb, CONV_HALO, HEAD), F32), jax.ShapeDtypeStruct((nb, SUBLANE, HEAD), F32)],
        scratch_shapes=[pltpu.VMEM((tp + CONV_HALO, HEAD), F32), pltpu.VMEM((tp + CONV_HALO, HEAD), F32)],
        compiler_params=_cparams(1),
    )(p, p, dyc, w, bias, ln_g, ln_b)


def _log1p_small(y):
    return jnp.where(y < 1e-4, y * (1.0 - 0.5 * y), jnp.log(1.0 + y))


def _softplus(x):
    return jnp.maximum(x, 0.0) + _log1p_small(jnp.exp(-jnp.abs(x)))


def _expm1(x):
    return jnp.where(jnp.abs(x) < 1e-2, x * (1.0 + 0.5 * x * (1.0 + x * (1.0 / 3.0))), jnp.exp(x) - 1.0)


def _gelu_parts(x):
    c = 0.7978845608028654
    inner = c * (x + 0.044715 * x * x * x)
    th = jnp.tanh(inner)
    gelu = 0.5 * x * (1.0 + th)
    dgelu = 0.5 * (1.0 + th) + 0.5 * x * (1.0 - th * th) * c * (1.0 + 3.0 * 0.044715 * x * x)
    return gelu, dgelu


def _lru_gates(x_all, tp, cw_ref, cb_ref, wa_ref, ba_ref, wx_ref, bx_ref, lam_ref):
    u = _conv_taps(x_all, cw_ref, tp, LRU_HALO, LRU_CONV) + cb_ref[...]
    u16 = u.astype(BF16)
    r = _sigmoid(_dot(u16, wa_ref[0]) + ba_ref[...])
    i = _sigmoid(_dot(u16, wx_ref[0]) + bx_ref[...])
    sp = _softplus(-lam_ref[...])
    la = -LRU_C * r * sp
    a = jnp.exp(la)
    mult = jnp.sqrt(-_expm1(2.0 * la))
    return u, r, i, a, mult, sp


def _lru_specs(tp, nb):
    col = lambda k: pl.BlockSpec((tp, HEAD), functools.partial(lambda j, k: (0, k * nb + j), k=k))
    vec = lambda: pl.BlockSpec((1, HEAD), lambda j: (0, j))
    mat = lambda: pl.BlockSpec((1, HEAD, HEAD), lambda j: (j, 0, 0))
    return col, vec, mat


def _lru_fwd(p, cw, cb, wa, ba, wx, bx, lam):
    tp = p.shape[0]
    nb = D_LRU // HEAD
    ng = tp // SUBLANE

    def body(x_ref, gt_ref, cw_ref, cb_ref, wa_ref, ba_ref, wx_ref, bx_ref, lam_ref, y_ref, hs_ref,
             xsc, asc, bsc):
        xsc[0:LRU_HALO, :] = jnp.zeros((LRU_HALO, HEAD), F32)
        xsc[LRU_HALO:, :] = x_ref[...]
        u, r, i, a, mult, _ = _lru_gates(xsc[...], tp, cw_ref, cb_ref, wa_ref, ba_ref, wx_ref, bx_ref, lam_ref)
        rows = lax.broadcasted_iota(jnp.int32, (tp, HEAD), 0)
        b = jnp.where(rows == 0, 1.0, mult) * (i * u)
        sub = rows % SUBLANE
        for k in (1, 2, 4):
            m = sub >= k
            b = jnp.where(m, a * pltpu.roll(b, k, 0) + b, b)
            a = jnp.where(m, a * pltpu.roll(a, k, 0), a)
        asc[...] = a
        bsc[...] = b

        def grp(g, carry):
            r0 = pl.multiple_of(g * SUBLANE, SUBLANE)
            h = bsc[pl.ds(r0, SUBLANE), :] + asc[pl.ds(r0, SUBLANE), :] * carry
            hs_ref[pl.ds(r0, SUBLANE), :] = h
            return jnp.broadcast_to(h[SUBLANE - 1:SUBLANE, :], (SUBLANE, HEAD))

        lax.fori_loop(0, ng, grp, jnp.zeros((SUBLANE, HEAD), F32))
        gelu, _ = _gelu_parts(gt_ref[...])
        y_ref[...] = gelu * hs_ref[...]

    col, vec, mat = _lru_specs(tp, nb)
    return pl.pallas_call(
        body, name="lru_fwd",
        grid=(nb,),
        in_specs=[col(2), col(3), pl.BlockSpec((LRU_CONV, HEAD), lambda j: (0, j)), vec(), mat(), vec(), mat(),
                  vec(), vec()],
        out_specs=[pl.BlockSpec((tp, HEAD), lambda j: (0, j)), pl.BlockSpec((tp, HEAD), lambda j: (0, j))],
        out_shape=[jax.ShapeDtypeStruct((tp, D_LRU), F32), jax.ShapeDtypeStruct((tp, D_LRU), F32)],
        scratch_shapes=[pltpu.VMEM((tp + LRU_HALO, HEAD), F32), pltpu.VMEM((tp, HEAD), F32),
                        pltpu.VMEM((tp, HEAD), F32)],
        compiler_params=_cparams(1),
    )(p, p, cw, cb, wa, ba, wx, bx, lam)


def _lru_bwd(p, hs, dyd, cw, cb, wa, ba, wx, bx, lam):
    tp = p.shape[0]
    nb = D_LRU // HEAD
    ng = tp // SUBLANE

    def body(x_ref, gt_ref, hs_ref, dy_ref, cw_ref, cb_ref, wa_ref, ba_ref, wx_ref, bx_ref, lam_ref,
             dx_ref, dgt_ref, dwa_ref, dwx_ref, dv_ref, xsc, asc, bsc, gsc, dusc):
        xsc[0:LRU_HALO, :] = jnp.zeros((LRU_HALO, HEAD), F32)
        xsc[LRU_HALO:, :] = x_ref[...]
        x_all = xsc[...]
        u, r, i, a, mult, sp = _lru_gates(x_all, tp, cw_ref, cb_ref, wa_ref, ba_ref, wx_ref, bx_ref, lam_ref)
        rows = lax.broadcasted_iota(jnp.int32, (tp, HEAD), 0)
        hs = hs_ref[...]
        dy = dy_ref[...]
        gelu, dgelu = _gelu_parts(gt_ref[...])
        dgt_ref[...] = (dy * hs * dgelu).astype(BF16)
        bb = dy * gelu
        aa = jnp.where(rows == tp - 1, 0.0, pltpu.roll(a, tp - 1, 0))
        sub = rows % SUBLANE
        for k in (1, 2, 4):
            m = sub < SUBLANE - k
            bb = jnp.where(m, aa * pltpu.roll(bb, tp - k, 0) + bb, bb)
            aa = jnp.where(m, aa * pltpu.roll(aa, tp - k, 0), aa)
        asc[...] = aa
        bsc[...] = bb

        def grp(gi, carry):
            g = ng - 1 - gi
            r0 = pl.multiple_of(g * SUBLANE, SUBLANE)
            gg = bsc[pl.ds(r0, SUBLANE), :] + asc[pl.ds(r0, SUBLANE), :] * carry
            gsc[pl.ds(r0, SUBLANE), :] = gg
            return jnp.broadcast_to(gg[0:1, :], (SUBLANE, HEAD))

        lax.fori_loop(0, ng, grp, jnp.zeros((SUBLANE, HEAD), F32))
        g = gsc[...]
        first = rows == 0
        hprev = jnp.where(first, 0.0, pltpu.roll(hs, 1, 0))
        iu = i * u
        d_iu = g * jnp.where(first, 1.0, mult)
        dmult_term = jnp.where(first, 0.0, g * iu * (-(a * a) / mult))
        dla = g * hprev * a + dmult_term
        dr = dla * (-LRU_C) * sp
        dv_ref[0, 7:8, :] = _colsum(dla * (LRU_C * r) * _sigmoid(-lam_ref[...]))
        dpr = dr * r * (1.0 - r)
        dpi = d_iu * u * i * (1.0 - i)
        dv_ref[0, 5:6, :] = _colsum(dpr)
        dv_ref[0, 6:7, :] = _colsum(dpi)
        u16 = u.astype(BF16)
        dpr16 = dpr.astype(BF16)
        dpi16 = dpi.astype(BF16)
        dwa_ref[0] = _dot_tn(u16, dpr16)
        dwx_ref[0] = _dot_tn(u16, dpi16)
        du = d_iu * i + _dot_nt(dpr16, wa_ref[0]) + _dot_nt(dpi16, wx_ref[0])
        dv_ref[0, 4:5, :] = _colsum(du)
        for j in range(LRU_CONV):
            sh = LRU_CONV - 1 - j
            xs = x_all if sh == 0 else pltpu.roll(x_all, sh, 0)
            dv_ref[0, j:j + 1, :] = _colsum(du * xs[LRU_HALO:, :])
        dusc[0:tp, :] = du
        dusc[tp:, :] = jnp.zeros((LRU_HALO, HEAD), F32)
        dx_ref[...] = _conv_taps_t(dusc[...], cw_ref, tp, LRU_HALO, LRU_CONV).astype(BF16)

    col, vec, mat = _lru_specs(tp, nb)
    ocol = lambda: pl.BlockSpec((tp, HEAD), lambda j: (0, j))
    return pl.pallas_call(
        body, name="lru_bwd",
        grid=(nb,),
        in_specs=[col(2), col(3), ocol(), ocol(), pl.BlockSpec((LRU_CONV, HEAD), lambda j: (0, j)), vec(), mat(),
                  vec(), mat(), vec(), vec()],
        out_specs=[ocol(), ocol(), mat(), mat(), pl.BlockSpec((1, SUBLANE, HEAD), lambda j: (j, 0, 0))],
        out_shape=[jax.ShapeDtypeStruct((tp, D_LRU), BF16), jax.ShapeDtypeStruct((tp, D_LRU), BF16),
                   jax.ShapeDtypeStruct((nb, HEAD, HEAD), F32), jax.ShapeDtypeStruct((nb, HEAD, HEAD), F32),
                   jax.ShapeDtypeStruct((nb, SUBLANE, HEAD), F32)],
        scratch_shapes=[pltpu.VMEM((tp + LRU_HALO, HEAD), F32), pltpu.VMEM((tp, HEAD), F32),
                        pltpu.VMEM((tp, HEAD), F32), pltpu.VMEM((tp, HEAD), F32),
                        pltpu.VMEM((tp + LRU_HALO, HEAD), F32)],
        compiler_params=_cparams(1),
    )(p, p, hs, dyd, cw, cb, wa, ba, wx, bx, lam)


def _mesh_pos():
    return lax.axis_index("x"), lax.axis_index("y"), lax.axis_index("c")


def _other_chips(x, y):
    return [(1 - x, y), (x, 1 - y), (1 - x, 1 - y)]


ANY = pl.BlockSpec(memory_space=pl.ANY)


def _allgather_shards(arrs, split):
    n = len(arrs)

    def body(*refs):
        ins, outs = refs[:n], refs[n:2 * n]
        send1, recv1, send2, recv2, send3, recv3 = refs[2 * n:]
        x, y, c = _mesh_pos()
        chip = 2 * x + y
        sibling = (x, y, 1 - c)
        others = _other_chips(x, y)

        def rows(k, cc):
            half = arrs[k].shape[0] // 2
            return pl.ds(cc * half, half)

        def remote(src, dst, ssem, rsem, dev):
            return pltpu.make_async_remote_copy(src_ref=src, dst_ref=dst, send_sem=ssem, recv_sem=rsem,
                                                device_id=dev, device_id_type=MESH)

        started = [remote(ins[k], outs[k].at[chip], send3.at[k], recv3.at[k], sibling) for k in range(n)]
        for cp in started:
            cp.start()
        for k in range(n):
            for j, (ox, oy) in enumerate(others):
                if split[k]:
                    src, dst = ins[k].at[rows(k, c)], outs[k].at[chip, rows(k, c)]
                else:
                    src, dst = ins[k], outs[k].at[chip]
                cp = remote(src, dst, send1.at[3 * k + j], recv1.at[3 * k + j], (ox, oy, c))
                cp.start()
                started.append(cp)
        for j, (ox, oy) in enumerate(others):
            ochip = 2 * ox + oy
            for k in range(n):
                if split[k]:
                    blk = outs[k].at[ochip, rows(k, c)]
                    remote(blk, blk, send1.at[3 * k + j], recv1.at[3 * k + j], sibling).wait_recv()
                    cp = remote(blk, blk, send2.at[3 * k + j], recv2.at[3 * k + j], sibling)
                    cp.start()
                    started.append(cp)
                else:
                    blk = outs[k].at[ochip]
                    remote(blk, blk, send1.at[3 * k + j], recv1.at[3 * k + j], sibling).wait_recv()
        for j, (ox, oy) in enumerate(others):
            ochip = 2 * ox + oy
            for k in range(n):
                if split[k]:
                    blk = outs[k].at[ochip, rows(k, 1 - c)]
                    remote(blk, blk, send2.at[3 * k + j], recv2.at[3 * k + j], sibling).wait_recv()
        for k in range(n):
            blk = outs[k].at[chip]
            remote(blk, blk, send3.at[k], recv3.at[k], sibling).wait_recv()
        for cp in started:
            cp.wait_send()

    return pl.pallas_call(
        body, name="allgather_shards",
        in_specs=[ANY] * n, out_specs=[ANY] * n,
        out_shape=[jax.ShapeDtypeStruct((N_SHARD,) + a.shape, a.dtype) for a in arrs],
        scratch_shapes=[pltpu.SemaphoreType.DMA((3 * n,)), pltpu.SemaphoreType.DMA((3 * n,)),
                        pltpu.SemaphoreType.DMA((3 * n,)), pltpu.SemaphoreType.DMA((3 * n,)),
                        pltpu.SemaphoreType.DMA((n,)), pltpu.SemaphoreType.DMA((n,))],
    )(*arrs)


HBM = pl.BlockSpec(memory_space=pltpu.HBM)
SEM = pl.BlockSpec(memory_space=pltpu.SEMAPHORE)
DATAFLOW = pltpu.SideEffectType.DATAFLOW_SIDE_EFFECTING
N_PEERS = 4


def _in_hbm(a):
    return pltpu.with_memory_space_constraint(a, pltpu.HBM)


def _gather_peers(x, y, c):
    return [((ox, oy, c), 2 * ox + oy) for ox, oy in _other_chips(x, y)] + [((x, y, 1 - c), 2 * x + y)]


def _gather_refs(src, land, slot, c, split):
    if not split:
        return src, land.at[slot]
    half = src.shape[0] // 2
    return src.at[pl.ds(c * half, half)], land.at[slot, pl.ds(c * half, half)]


def _gather_start(arrs, split):
    n = len(arrs)

    def body(*refs):
        ins, lands = refs[:n], refs[n:2 * n]
        ssem, rsem = refs[2 * n:2 * n + 2]
        token = refs[-1]
        x, y, c = _mesh_pos()
        chip = 2 * x + y
        for k in range(n):
            for j, (dev, _) in enumerate(_gather_peers(x, y, c)):
                src, dst = _gather_refs(ins[k], lands[k], chip, c, split[k] and j < N_PEERS - 1)
                pltpu.make_async_remote_copy(
                    src_ref=src, dst_ref=dst, send_sem=ssem.at[N_PEERS * k + j],
                    recv_sem=rsem.at[N_PEERS * k + j], device_id=dev, device_id_type=MESH).start()
        token[...] = jnp.zeros_like(token)

    lands = [_in_hbm(lax.empty((N_SHARD,) + a.shape, a.dtype)) for a in arrs]
    out = pl.pallas_call(
        body, name="gather_start",
        in_specs=[HBM] * (2 * n),
        out_specs=[SEM, SEM] + [HBM] * (2 * n) + [pl.BlockSpec(memory_space=pltpu.VMEM)],
        out_shape=[pltpu.SemaphoreType.DMA((N_PEERS * n,)), pltpu.SemaphoreType.DMA((N_PEERS * n,))]
        + [pltpu.HBM(a.shape, a.dtype) for a in arrs]
        + [pltpu.HBM((N_SHARD,) + a.shape, a.dtype) for a in arrs]
        + [jax.ShapeDtypeStruct((SUBLANE, LANE), F32)],
        input_output_aliases={k: 2 + k for k in range(2 * n)},
        compiler_params=pltpu.CompilerParams(has_side_effects=DATAFLOW),
    )(*[_in_hbm(a) for a in arrs], *lands)
    return out[0], out[1], list(out[2:2 + n]), list(out[2 + n:2 + 2 * n]), out[-1]


def _gather_wait(ssem, rsem, srcs, lands, ks, after, split=False):
    n = len(ks)

    def body(*refs):
        ins, lnd = refs[:n], refs[n:2 * n]
        ssem_ref, rsem_ref = refs[2 * n:2 * n + 2]
        x, y, c = _mesh_pos()
        for i, k in enumerate(ks):
            for j, (dev, pchip) in enumerate(_gather_peers(x, y, c)):
                src, dst = _gather_refs(ins[i], lnd[i], pchip, c, split and j < N_PEERS - 1)
                cp = pltpu.make_async_remote_copy(
                    src_ref=src, dst_ref=dst, send_sem=ssem_ref.at[N_PEERS * k + j],
                    recv_sem=rsem_ref.at[N_PEERS * k + j], device_id=dev, device_id_type=MESH)
                cp.wait_send()
                cp.wait_recv()

    out = pl.pallas_call(
        body, name="gather_wait",
        in_specs=[HBM] * (2 * n) + [SEM, SEM] + [ANY] * len(after),
        out_specs=[HBM] * (2 * n),
        out_shape=[pltpu.HBM(a.shape, a.dtype) for a in srcs] + [pltpu.HBM(a.shape, a.dtype) for a in lands],
        input_output_aliases={k: k for k in range(2 * n)},
        compiler_params=pltpu.CompilerParams(has_side_effects=DATAFLOW),
    )(*srcs, *lands, ssem, rsem, *after)
    return list(out[n:])


def _pair_forward(lands):
    n = len(lands)

    def body(*refs):
        outs = refs[n:2 * n]
        ssem, rsem = refs[2 * n:]
        x, y, c = _mesh_pos()
        sibling = (x, y, 1 - c)
        cps = []
        for k in range(n):
            half = lands[k].shape[1] // 2
            for j, (ox, oy) in enumerate(_other_chips(x, y)):
                mine = outs[k].at[2 * ox + oy, pl.ds(c * half, half)]
                cp = pltpu.make_async_remote_copy(src_ref=mine, dst_ref=mine, send_sem=ssem.at[3 * k + j],
                                                  recv_sem=rsem.at[3 * k + j], device_id=sibling, device_id_type=MESH)
                cp.start()
                cps.append(cp)
        for k in range(n):
            half = lands[k].shape[1] // 2
            for j, (ox, oy) in enumerate(_other_chips(x, y)):
                theirs = outs[k].at[2 * ox + oy, pl.ds((1 - c) * half, half)]
                pltpu.make_async_remote_copy(src_ref=theirs, dst_ref=theirs, send_sem=ssem.at[3 * k + j],
                                             recv_sem=rsem.at[3 * k + j], device_id=sibling,
                                             device_id_type=MESH).wait_recv()
        for cp in cps:
            cp.wait_send()

    return pl.pallas_call(
        body, name="pair_forward",
        in_specs=[ANY] * n, out_specs=[ANY] * n,
        out_shape=[jax.ShapeDtypeStruct(a.shape, a.dtype) for a in lands],
        scratch_shapes=[pltpu.SemaphoreType.DMA((3 * n,)), pltpu.SemaphoreType.DMA((3 * n,))],
        input_output_aliases={k: k for k in range(n)},
    )(*lands)


N_SOURCES = 7


def _reduce_peers(x, y, c):
    peers = []
    for ox, oy in _other_chips(x, y):
        for rel in range(2):
            peers.append(((ox, oy, c + rel - 2 * c * rel), 2 * ox + oy))
    peers.append(((x, y, 1 - c), 2 * x + y))
    return peers


def _reduce_start(arrs, slots):
    n = len(arrs)

    def body(*refs):
        ins, lands = refs[:n], refs[n:2 * n]
        ssem, rsem = refs[2 * n:2 * n + 2]
        token = refs[-1]
        x, y, c = _mesh_pos()
        me = 2 * (2 * x + y) + c
        for k in range(n):
            half = arrs[k].shape[1] // 2
            for p, (dev, ochip) in enumerate(_reduce_peers(x, y, c)):
                pltpu.make_async_remote_copy(
                    src_ref=ins[k].at[ochip, pl.ds(dev[2] * half, half)], dst_ref=lands[k].at[me],
                    send_sem=ssem.at[N_SOURCES * k + p], recv_sem=rsem.at[N_SOURCES * k + p],
                    device_id=dev, device_id_type=MESH).start()
        token[...] = jnp.zeros_like(token)

    out = pl.pallas_call(
        body, name="reduce_start",
        in_specs=[HBM] * (2 * n),
        out_specs=[SEM, SEM] + [HBM] * (2 * n) + [pl.BlockSpec(memory_space=pltpu.VMEM)],
        out_shape=[pltpu.SemaphoreType.DMA((N_SOURCES * n,)), pltpu.SemaphoreType.DMA((N_SOURCES * n,))]
        + [pltpu.HBM(a.shape, a.dtype) for a in arrs] + [pltpu.HBM(a.shape, a.dtype) for a in slots]
        + [jax.ShapeDtypeStruct((SUBLANE, LANE), F32)],
        input_output_aliases={k: 2 + k for k in range(2 * n)},
        compiler_params=pltpu.CompilerParams(has_side_effects=DATAFLOW),
    )(*[_in_hbm(a) for a in arrs], *[_in_hbm(a) for a in slots])
    return out[0], out[1], list(out[2:2 + n]), list(out[2 + n:2 + 2 * n]), out[-1]


def _reduce_wait(ssem, rsem, arrs, slots, after):
    n = len(arrs)

    def body(*refs):
        ins, lnd = refs[:n], refs[n:2 * n]
        ssem_ref, rsem_ref = refs[2 * n:2 * n + 2]
        x, y, c = _mesh_pos()
        for k in range(n):
            half = arrs[k].shape[1] // 2
            for p, (dev, ochip) in enumerate(_reduce_peers(x, y, c)):
                cp = pltpu.make_async_remote_copy(
                    src_ref=ins[k].at[ochip, pl.ds(dev[2] * half, half)], dst_ref=lnd[k].at[2 * ochip + dev[2]],
                    send_sem=ssem_ref.at[N_SOURCES * k + p], recv_sem=rsem_ref.at[N_SOURCES * k + p],
                    device_id=dev, device_id_type=MESH)
                cp.wait_send()
                cp.wait_recv()

    out = pl.pallas_call(
        body, name="reduce_wait",
        in_specs=[HBM] * (2 * n) + [SEM, SEM] + [ANY] * len(after),
        out_specs=[HBM] * (2 * n),
        out_shape=[pltpu.HBM(a.shape, a.dtype) for a in arrs] + [pltpu.HBM(a.shape, a.dtype) for a in slots],
        input_output_aliases={k: k for k in range(2 * n)},
        compiler_params=pltpu.CompilerParams(has_side_effects=DATAFLOW),
    )(*arrs, *slots, ssem, rsem, *after)
    return list(out[n:])


def _own_part(arrs, chip, core, me):
    n = len(arrs)
    nb = GRAD_ROW_BLOCKS

    def body(chip_ref, core_ref, me_ref, *refs):
        for k in range(n):
            refs[n + k][...] = refs[k][...]

    def blk(a):
        return (1, a.shape[1] // 2 // nb, a.shape[2])

    grid_spec = pltpu.PrefetchScalarGridSpec(
        num_scalar_prefetch=3, grid=(nb,),
        in_specs=[pl.BlockSpec(blk(a), lambda i, ch, co, me: (ch[0], co[0] * nb + i, 0)) for a in arrs],
        out_specs=[pl.BlockSpec(blk(a), lambda i, ch, co, me: (me[0], i, 0)) for a in arrs])
    return pl.pallas_call(
        body, name="own_part", grid_spec=grid_spec,
        out_shape=[jax.ShapeDtypeStruct((N_DEV, a.shape[1] // 2, a.shape[2]), a.dtype) for a in arrs],
        compiler_params=_cparams(1),
    )(chip, core, me, *arrs)


def _sum_devices(arrs, core):
    n = len(arrs)
    nb = GRAD_ROW_BLOCKS

    def body(c_ref, *refs):
        for k in range(n):
            r = refs[k]
            acc = r[0].astype(F32)
            for dev in range(1, N_DEV):
                acc = acc + r[dev].astype(F32)
            refs[n + k][...] = acc

    grid_spec = pltpu.PrefetchScalarGridSpec(
        num_scalar_prefetch=1, grid=(nb,),
        in_specs=[pl.BlockSpec((N_DEV, a.shape[1] // nb, a.shape[2]), lambda i, c: (0, i, 0)) for a in arrs],
        out_specs=[pl.BlockSpec((a.shape[1] // nb, a.shape[2]), lambda i, c: (c[0] * nb + i, 0)) for a in arrs])
    return pl.pallas_call(
        body, name="sum_devices", grid_spec=grid_spec,
        out_shape=[jax.ShapeDtypeStruct((2 * a.shape[1], a.shape[2]), F32) for a in arrs],
        compiler_params=_cparams(1),
    )(core, *arrs)


def _scatter_start(arrs, slots):
    n = len(arrs)

    def body(*refs):
        ins, lands = refs[:n], refs[n:2 * n]
        ssem, rsem = refs[2 * n:2 * n + 2]
        token = refs[-1]
        x, y, c = _mesh_pos()
        chip = 2 * x + y
        for k in range(n):
            for j, (ox, oy) in enumerate(_other_chips(x, y)):
                pltpu.make_async_remote_copy(
                    src_ref=ins[k].at[2 * ox + oy], dst_ref=lands[k].at[chip], send_sem=ssem.at[3 * k + j],
                    recv_sem=rsem.at[3 * k + j], device_id=(ox, oy, c), device_id_type=MESH).start()
        token[...] = jnp.zeros_like(token)

    out = pl.pallas_call(
        body, name="scatter_start",
        in_specs=[HBM] * (2 * n),
        out_specs=[SEM, SEM] + [HBM] * (2 * n) + [pl.BlockSpec(memory_space=pltpu.VMEM)],
        out_shape=[pltpu.SemaphoreType.DMA((3 * n,)), pltpu.SemaphoreType.DMA((3 * n,))]
        + [pltpu.HBM(a.shape, a.dtype) for a in arrs] + [pltpu.HBM(a.shape, a.dtype) for a in slots]
        + [jax.ShapeDtypeStruct((SUBLANE, LANE), F32)],
        input_output_aliases={k: 2 + k for k in range(2 * n)},
        compiler_params=pltpu.CompilerParams(has_side_effects=DATAFLOW),
    )(*[_in_hbm(a) for a in arrs], *[_in_hbm(a) for a in slots])
    return out[0], out[1], list(out[2:2 + n]), list(out[2 + n:2 + 2 * n]), out[-1]


def _scatter_wait(ssem, rsem, arrs, slots, after):
    n = len(arrs)

    def body(*refs):
        ins, lnd = refs[:n], refs[n:2 * n]
        ssem_ref, rsem_ref = refs[2 * n:2 * n + 2]
        x, y, c = _mesh_pos()
        for k in range(n):
            for j, (ox, oy) in enumerate(_other_chips(x, y)):
                ochip = 2 * ox + oy
                cp = pltpu.make_async_remote_copy(
                    src_ref=ins[k].at[ochip], dst_ref=lnd[k].at[ochip], send_sem=ssem_ref.at[3 * k + j],
                    recv_sem=rsem_ref.at[3 * k + j], device_id=(ox, oy, c), device_id_type=MESH)
                cp.wait_send()
                cp.wait_recv()

    out = pl.pallas_call(
        body, name="scatter_wait",
        in_specs=[HBM] * (2 * n) + [SEM, SEM] + [ANY] * len(after),
        out_specs=[HBM] * (2 * n),
        out_shape=[pltpu.HBM(a.shape, a.dtype) for a in arrs] + [pltpu.HBM(a.shape, a.dtype) for a in slots],
        input_output_aliases={k: k for k in range(2 * n)},
        compiler_params=pltpu.CompilerParams(has_side_effects=DATAFLOW),
    )(*arrs, *slots, ssem, rsem, *after)
    return list(out[n:])


def _pair_exchange_halves(arrs):
    n = len(arrs)

    def body(*refs):
        ins, outs = refs[:n], refs[n:2 * n]
        ssem, rsem = refs[2 * n:]
        x, y, c = _mesh_pos()
        cps = []
        for k in range(n):
            half = arrs[k].shape[1] // 2
            cp = pltpu.make_async_remote_copy(
                src_ref=ins[k].at[:, pl.ds((1 - c) * half, half)], dst_ref=outs[k],
                send_sem=ssem.at[k], recv_sem=rsem.at[k], device_id=(x, y, 1 - c), device_id_type=MESH)
            cp.start()
            cps.append(cp)
        for cp in cps:
            cp.wait()

    return pl.pallas_call(
        body, name="pair_exchange_halves",
        in_specs=[ANY] * n, out_specs=[ANY] * n,
        out_shape=[jax.ShapeDtypeStruct((a.shape[0], a.shape[1] // 2, a.shape[2]), a.dtype) for a in arrs],
        scratch_shapes=[pltpu.SemaphoreType.DMA((n,)), pltpu.SemaphoreType.DMA((n,))],
    )(*arrs)


GRAD_ROW_BLOCKS = 2


def _pair_add(arrs, recvd, core):
    n = len(arrs)
    nb = GRAD_ROW_BLOCKS

    def body(c_ref, *refs):
        for k in range(n):
            refs[2 * n + k][...] = (refs[k][...].astype(F32) + refs[n + k][...].astype(F32)).astype(BF16)

    def blk(a):
        return (1, a.shape[1] // 2 // nb, a.shape[2])

    grid_spec = pltpu.PrefetchScalarGridSpec(
        num_scalar_prefetch=1, grid=(N_SHARD, nb),
        in_specs=[pl.BlockSpec(blk(a), lambda s, i, c: (s, c[0] * nb + i, 0)) for a in arrs]
        + [pl.BlockSpec(blk(a), lambda s, i, c: (s, i, 0)) for a in arrs],
        out_specs=[pl.BlockSpec(blk(a), lambda s, i, c: (s, i, 0)) for a in arrs])
    return pl.pallas_call(
        body, name="pair_add", grid_spec=grid_spec,
        out_shape=[jax.ShapeDtypeStruct(r.shape, BF16) for r in recvd],
        compiler_params=_cparams(2),
    )(core, *arrs, *recvd)


def _own_slot(arrs, chip):
    n = len(arrs)
    nb = GRAD_ROW_BLOCKS

    def body(c_ref, *refs):
        for k in range(n):
            refs[n + k][...] = refs[k][...]

    def blk(a):
        return (1, a.shape[1] // nb, a.shape[2])

    grid_spec = pltpu.PrefetchScalarGridSpec(
        num_scalar_prefetch=1, grid=(nb,),
        in_specs=[pl.BlockSpec(blk(a), lambda i, c: (c[0], i, 0)) for a in arrs],
        out_specs=[pl.BlockSpec(blk(a), lambda i, c: (c[0], i, 0)) for a in arrs])
    return pl.pallas_call(
        body, name="own_slot", grid_spec=grid_spec,
        out_shape=[jax.ShapeDtypeStruct(a.shape, a.dtype) for a in arrs],
        compiler_params=_cparams(1),
    )(chip, *arrs)


def _scatter_to_owners(arrs, slots):
    n = len(arrs)

    def body(*refs):
        ins, outs = refs[:n], refs[2 * n:3 * n]
        ssem, rsem = refs[3 * n:]
        x, y, c = _mesh_pos()
        chip = 2 * x + y
        others = _other_chips(x, y)
        cps = []
        for k in range(n):
            for j, (ox, oy) in enumerate(others):
                cp = pltpu.make_async_remote_copy(
                    src_ref=ins[k].at[2 * ox + oy], dst_ref=outs[k].at[chip],
                    send_sem=ssem.at[3 * k + j], recv_sem=rsem.at[3 * k + j],
                    device_id=(ox, oy, c), device_id_type=MESH)
                cp.start()
                cps.append(cp)
        for k in range(n):
            for j, (ox, oy) in enumerate(others):
                blk = outs[k].at[2 * ox + oy]
                pltpu.make_async_remote_copy(
                    src_ref=blk, dst_ref=blk, send_sem=ssem.at[3 * k + j], recv_sem=rsem.at[3 * k + j],
                    device_id=(ox, oy, c), device_id_type=MESH).wait_recv()
        for cp in cps:
            cp.wait_send()

    return pl.pallas_call(
        body, name="scatter_to_owners",
        in_specs=[ANY] * (2 * n), out_specs=[ANY] * n,
        out_shape=[jax.ShapeDtypeStruct(a.shape, a.dtype) for a in arrs],
        scratch_shapes=[pltpu.SemaphoreType.DMA((3 * n,)), pltpu.SemaphoreType.DMA((3 * n,))],
        input_output_aliases={n + k: k for k in range(n)},
    )(*arrs, *slots)


def _sum_chips(arrs, core):
    n = len(arrs)
    nb = GRAD_ROW_BLOCKS

    def body(c_ref, *refs):
        for k in range(n):
            r = refs[k]
            refs[n + k][...] = ((r[0].astype(F32) + r[1].astype(F32)) + r[2].astype(F32)) + r[3].astype(F32)

    grid_spec = pltpu.PrefetchScalarGridSpec(
        num_scalar_prefetch=1, grid=(nb,),
        in_specs=[pl.BlockSpec((N_SHARD, a.shape[1] // nb, a.shape[2]), lambda i, c: (0, i, 0)) for a in arrs],
        out_specs=[pl.BlockSpec((a.shape[1] // nb, a.shape[2]), lambda i, c: (c[0] * nb + i, 0)) for a in arrs])
    return pl.pallas_call(
        body, name="sum_chips", grid_spec=grid_spec,
        out_shape=[jax.ShapeDtypeStruct((2 * a.shape[1], a.shape[2]), F32) for a in arrs],
        compiler_params=_cparams(1),
    )(core, *arrs)


def _pair_allgather_halves(arrs):
    n = len(arrs)

    def body(*refs):
        outs = refs[n:2 * n]
        ssem, rsem = refs[2 * n:]
        x, y, c = _mesh_pos()
        cps = []
        for k in range(n):
            h = arrs[k].shape[0] // 2
            mine = outs[k].at[pl.ds(c * h, h)]
            cp = pltpu.make_async_remote_copy(src_ref=mine, dst_ref=mine, send_sem=ssem.at[k],
                                              recv_sem=rsem.at[k], device_id=(x, y, 1 - c), device_id_type=MESH)
            cp.start()
            cps.append(cp)
        for k, cp in enumerate(cps):
            h = arrs[k].shape[0] // 2
            theirs = outs[k].at[pl.ds((1 - c) * h, h)]
            pltpu.make_async_remote_copy(src_ref=theirs, dst_ref=theirs, send_sem=ssem.at[k], recv_sem=rsem.at[k],
                                         device_id=(x, y, 1 - c), device_id_type=MESH).wait_recv()
            cp.wait_send()

    return pl.pallas_call(
        body, name="pair_allgather_halves",
        in_specs=[ANY] * n, out_specs=[ANY] * n,
        out_shape=[jax.ShapeDtypeStruct(a.shape, a.dtype) for a in arrs],
        scratch_shapes=[pltpu.SemaphoreType.DMA((n,)), pltpu.SemaphoreType.DMA((n,))],
        input_output_aliases={k: k for k in range(n)},
    )(*arrs)


N_DEV = 8


def _allgather_all(v):
    m_per, n = v.shape

    def body(x_ref, out_ref, send_sems, recv_sems, local_sem):
        x, y, c = _mesh_pos()
        me, sibling = (x, y, c), (x, y, 1 - c)
        chips = _other_chips(x, y)

        def rows(px, py, pc):
            return out_ref.at[pl.ds((4 * px + 2 * py + pc) * m_per, m_per), :]

        def copy(k, block, to, src=None):
            return pltpu.make_async_remote_copy(
                src_ref=rows(*block) if src is None else src, dst_ref=rows(*block),
                send_sem=send_sems.at[k], recv_sem=recv_sems.at[k], device_id=to, device_id_type=MESH)

        mine = pltpu.make_async_copy(x_ref, rows(*me), local_sem)
        mine.start()
        first = [copy(0, me, sibling, src=x_ref)]
        first += [copy(1 + j, me, (*chip, c), src=x_ref) for j, chip in enumerate(chips)]
        for cp in first:
            cp.start()
        passed = [copy(4 + j, (*chip, c), sibling) for j, chip in enumerate(chips)]
        for j, chip in enumerate(chips):
            copy(1 + j, (*chip, c), me).wait_recv()
            passed[j].start()
        copy(0, sibling, me).wait_recv()
        for j, chip in enumerate(chips):
            copy(4 + j, (*chip, 1 - c), me).wait_recv()
        for cp in first + passed:
            cp.wait_send()
        mine.wait()

    return pl.pallas_call(
        body, name="allgather_all",
        out_shape=jax.ShapeDtypeStruct((N_DEV * m_per, n), v.dtype),
        in_specs=[pl.BlockSpec(memory_space=pltpu.VMEM)],
        out_specs=pl.BlockSpec(memory_space=pltpu.VMEM),
        scratch_shapes=[pltpu.SemaphoreType.DMA((7,)), pltpu.SemaphoreType.DMA((7,)), pltpu.SemaphoreType.DMA],
        compiler_params=pltpu.CompilerParams(vmem_limit_bytes=VMEM_LIMIT_MB * 1024 * 1024),
    )(v)


def _adamw_math(w, g, m, v):
    m2 = ADAM_B1 * m + (1.0 - ADAM_B1) * g
    v2 = ADAM_B2 * v + (1.0 - ADAM_B2) * (g * g)
    m_hat = m2 / (1.0 - ADAM_B1 ** ADAM_STEP)
    v_hat = v2 / (1.0 - ADAM_B2 ** ADAM_STEP)
    delta = -ADAM_LR * (m_hat / (jnp.sqrt(v_hat) + ADAM_EPS) + ADAM_WD * w)
    return delta, m2, v2


def _adamw(w, m, v, gs, nblk):
    nl, r, n = w.shape
    assert nl == len(gs) and nl in (1, 2)
    br = r // nblk

    def body(w_ref, m_ref, v_ref, *rest):
        g_refs, (go_ref, d_ref, mo_ref, vo_ref) = rest[:nl], rest[nl:]
        g = g_refs[0][...]
        if nl == 2:
            g = jnp.where(pl.program_id(0) == 0, g, g_refs[1][...])
        delta, m2, v2 = _adamw_math(w_ref[0], g, m_ref[0], v_ref[0])
        go_ref[0] = g
        d_ref[0] = delta
        mo_ref[0] = m2
        vo_ref[0] = v2

    spec = pl.BlockSpec((1, br, n), lambda l, i: (l, i, 0))
    g_specs = [pl.BlockSpec((br, n), lambda l, i: (i, 0))] if nl == 1 else [
        pl.BlockSpec((br, n), lambda l, i: (jnp.where(l == 0, i, nblk - 1), 0)),
        pl.BlockSpec((br, n), lambda l, i: (jnp.where(l == 1, i, 0), 0))]
    return pl.pallas_call(
        body, name="adamw", grid=(nl, nblk),
        in_specs=[spec, spec, spec] + g_specs,
        out_specs=[spec] * 4,
        out_shape=[jax.ShapeDtypeStruct((nl, r, n), F32)] * 4,
        compiler_params=_cparams(2),
    )(w, m, v, *gs)


def _small_reduce_adamw(parts, w, m, v, rep_rows, sh_rows):
    mrows = rep_rows + N_SHARD * sh_rows

    def body(p_ref, w_ref, m_ref, v_ref, go_ref, d_ref, mo_ref, vo_ref):
        x, y, _ = _mesh_pos()
        mine = rep_rows + (2 * x + y) * sh_rows
        g_rep = p_ref[0:rep_rows, :]
        g_sh = p_ref[pl.ds(pl.multiple_of(mine, SUBLANE), sh_rows), :]
        for k in range(1, N_DEV):
            g_rep = g_rep + p_ref[k * mrows:k * mrows + rep_rows, :]
            g_sh = g_sh + p_ref[pl.ds(pl.multiple_of(k * mrows + mine, SUBLANE), sh_rows), :]
        g = jnp.concatenate([g_rep, g_sh], axis=0)
        delta, m2, v2 = _adamw_math(w_ref[...], g, m_ref[...], v_ref[...])
        go_ref[...] = g
        d_ref[...] = delta
        mo_ref[...] = m2
        vo_ref[...] = v2

    return pl.pallas_call(
        body, name="small_reduce_adamw",
        out_shape=[jax.ShapeDtypeStruct((rep_rows + sh_rows, 128), F32)] * 4,
        compiler_params=pltpu.CompilerParams(vmem_limit_bytes=VMEM_LIMIT_MB * 1024 * 1024),
    )(parts, w, m, v)


LANE = 128
REP_SPEC = (("ffn1_norm", 16), ("mix_norm", 16), ("ffn2_norm", 16), ("final_norm", 8), ("pool_w", 128),
            ("pool_scale", 8), ("hgrn_lb_logits", 16), ("hgrn_gnorm", 8), ("lru_wa", 256), ("lru_wx", 256))
SH_SPEC = (("meta_tokens", 32), ("conv_w", 32), ("lru_conv_w", 8), ("conv_b", 8), ("conv_ln_g", 8),
           ("conv_ln_b", 8), ("lru_conv_b", 8), ("lru_ba", 8), ("lru_bx", 8), ("lru_lambda", 8))
REP_ROWS = sum(r for _, r in REP_SPEC)
SH_ROWS = sum(r for _, r in SH_SPEC)


def _pack_rows(vals, spec):
    parts = []
    for name, rows in spec:
        flat = vals[name].astype(F32).reshape(-1, LANE)
        if flat.shape[0] < rows:
            flat = jnp.concatenate([flat, jnp.zeros((rows - flat.shape[0], LANE), F32)], axis=0)
        parts.append(flat)
    return jnp.concatenate(parts, axis=0)


def _unpack_rows(packed, spec, shapes):
    out = {}
    off = 0
    for name, rows in spec:
        shp = shapes[name]
        n = int(np.prod(shp)) // LANE
        out[name] = packed[off:off + n].reshape(shp)
        off += rows
    return out


def _block_diag(blocks):
    n, b, _ = blocks.shape
    return sum(jnp.pad(blocks[g], ((g * b, (n - 1 - g) * b), (g * b, (n - 1 - g) * b))) for g in range(n))


def _diag_blocks(mat, n):
    b = mat.shape[0] // n
    return jnp.stack([mat[g * b:(g + 1) * b, g * b:(g + 1) * b] for g in range(n)])


BIG = ("ffn1_wg", "ffn1_wu", "ffn2_wg", "ffn2_wu", "ffn1_wd", "ffn2_wd", "w_in_even", "w_out_even",
       "w_in_odd", "w_out_odd")
ADAM_BLOCKS = {"ffn1_wg": 8, "ffn1_wu": 8, "ffn2_wg": 8, "ffn2_wu": 8, "ffn1_wd": 4, "ffn2_wd": 4,
               "w_in_even": 4, "w_out_even": 2, "w_in_odd": 4, "w_out_odd": 2}
WEIGHT_NAMES = ('meta_tokens', 'ffn1_norm', 'ffn1_wg', 'ffn1_wu', 'ffn1_wd', 'mix_norm', 'ffn2_norm', 'ffn2_wg',
                'ffn2_wu', 'ffn2_wd', 'w_in_even', 'pool_w', 'pool_scale', 'hgrn_lb_logits', 'hgrn_gnorm',
                'w_out_even', 'w_in_odd', 'conv_w', 'conv_b', 'conv_ln_g', 'conv_ln_b', 'lru_conv_w',
                'lru_conv_b', 'lru_wa', 'lru_ba', 'lru_wx', 'lru_bx', 'lru_lambda', 'w_out_odd', 'final_norm')


def _rows2d(a):
    return a.reshape(-1, a.shape[-1])


def _local_step_v2(x, tgt, w, gathered, small_full):
    s_len, d = x.shape
    t_real = s_len + N_META
    tp = -(-t_real // ROW_ALIGN) * ROW_ALIGN
    tm = _tile(tp, 832, ROW_ALIGN)
    tm_small = _tile(tp, 416, 16)
    tr = _tile(tp, 416, SUBLANE)
    f1 = ("ffn1_norm", "ffn1_wg", "ffn1_wu", "ffn1_wd")
    f2 = ("ffn2_norm", "ffn2_wg", "ffn2_wu", "ffn2_wd")

    meta_full = small_full["meta_tokens"]
    h0 = jnp.concatenate([meta_full, x, jnp.zeros((tp - t_real, d), F32)], axis=0)
    tgt_pad = jnp.concatenate([jnp.zeros((N_META, d), F32), tgt, jnp.zeros((tp - t_real, d), F32)], axis=0)

    w_in_even = jnp.transpose(gathered["w_in_even"], (1, 0, 2)).reshape(d, D_IN_EVEN)
    w_out_even = gathered["w_out_even"].reshape(d, d)
    w_out_odd = gathered["w_out_odd"].reshape(d, d)
    even_piece = [(w_in_even, (d, D_IN_EVEN), (0, 0))]
    odd_pieces = [(gathered["w_in_odd"], (1, d, D_IN_ODD // N_SHARD), (k, 0, 0)) for k in range(N_SHARD)]
    pool_wbd = _block_diag(w["pool_w"][0]).astype(BF16)
    pool_scale = w["pool_scale"]
    wa_bd = _block_diag2(w["lru_wa"][0]).astype(BF16)
    wx_bd = _block_diag2(w["lru_wx"][0]).astype(BF16)
    mst, msk, n_lev = _hgrn_consts(HG_CHUNK)
    conv_w = small_full["conv_w"]
    sf = small_full

    def gain(name, layer):
        return w[name][layer:layer + 1]

    def ffn(h, names, layer):
        return _ffn_fwd(h, gain(names[0], layer), gathered[names[1]], gathered[names[2]], gathered[names[3]],
                        layer, tm)

    h1, a1, b1, n1 = ffn(h0, f1, 0)
    p0, nm0 = _proj_fwd(h1, gain("mix_norm", 0), 0, even_piece, tm_small)
    ya = _pool_fwd(p0, pool_wbd, pool_scale, tr)
    yb, states = _hgrn_fwd(p0, w["hgrn_lb_logits"], w["hgrn_gnorm"], mst, msk, n_lev, tm)
    h2 = _out_fwd(h1, ya, yb, w_out_even, tm)
    h3, a2, b2, n2 = ffn(h2, f2, 0)
    h4, a3, b3, n3 = ffn(h3, f1, 1)
    p1, nm1 = _proj_fwd(h4, gain("mix_norm", 1), 1, odd_pieces, tm_small)
    yc = _convmod_fwd(p1, conv_w, sf["conv_b"], sf["conv_ln_g"], sf["conv_ln_b"], tr)
    lru_args = (sf["lru_conv_w"], sf["lru_conv_b"], wa_bd, sf["lru_ba"], wx_bd, sf["lru_bx"], sf["lru_lambda"])
    yd, hs = _lru_fwd(p1, *lru_args)
    h5 = _out_fwd(h4, yc, yd, w_out_odd, tm)
    h6, a4, b4, n4 = ffn(h5, f2, 1)
    loss, dh6, dg_final = _loss_bwd(h6, w["final_norm"].reshape(1, d), tgt_pad, t_real, tm)

    def ffn_bwd(dho, h, n, a, b, names, layer, acc):
        dh, da, db, dg = _ffn_bwd_act(dho, h, gain(names[0], layer), a, b, gathered[names[1]], gathered[names[2]],
                                      gathered[names[3]], layer, tm_small)
        acc = _ffn_bwd_w(dho, n, a, b, da, db, acc[0], acc[1], acc[2], layer, tm)
        return dh, dg, acc

    none3 = (None, None, None)
    dh5, dg_f2_l1, g_f2 = ffn_bwd(dh6, h5, n4, a4, b4, f2, 1, none3)
    dyc, dyd, dw_out_odd = _out_bwd(dh5, yc, yd, w_out_odd, tm)
    dca, dcb, dconv_w, dconv_vec = _convmod_bwd(p1, dyc, conv_w, sf["conv_b"], sf["conv_ln_g"], sf["conv_ln_b"], tr)
    dlx, dlg, dwa_bd, dwx_bd, dlru_vec = _lru_bwd(p1, hs, dyd, *lru_args)
    dp1 = [dca, dcb, dlx, dlg]
    dh4, dg_mix_l1 = _proj_bwd_act(dh5, h4, gain("mix_norm", 1), 1, dp1, odd_pieces, tm_small)
    dw_in_odd = jnp.stack(_proj_bwd_w(nm1, dp1, tm))
    dh3, dg_f1_l1, g_f1 = ffn_bwd(dh4, h3, n3, a3, b3, f1, 1, none3)
    dh2, dg_f2_l0, g_f2 = ffn_bwd(dh3, h2, n2, a2, b2, f2, 0, g_f2)
    dya, dyb, dw_out_even = _out_bwd(dh2, ya, yb, w_out_even, tm)
    dpool, dpool_wbd, dpool_scale = _pool_bwd(p0, dya, pool_wbd, pool_scale, tr)
    dq, dz, dv, dgate, dlb_logits, dgn_heads = _hgrn_bwd(p0, dyb, states, w["hgrn_lb_logits"], w["hgrn_gnorm"],
                                                         mst, msk, n_lev, tm)
    dp0 = [jnp.concatenate([dpool, dq, dz, dv, dgate], axis=1)]
    dh1, dg_mix_l0 = _proj_bwd_act(dh2, h1, gain("mix_norm", 0), 0, dp0, even_piece, tm_small)
    (dw_in_even,) = _proj_bwd_w(nm0, dp0, tm_small)
    dh0, dg_f1_l0, g_f1 = ffn_bwd(dh1, h0, n1, a1, b1, f1, 0, g_f1)

    grad_x = dh0[N_META:t_real]
    big = {
        "ffn1_wg": g_f1[0], "ffn1_wu": g_f1[1], "ffn1_wd": g_f1[2],
        "ffn2_wg": g_f2[0], "ffn2_wu": g_f2[1], "ffn2_wd": g_f2[2],
        "w_in_even": jnp.transpose(dw_in_even.reshape(d, N_SHARD, D_IN_EVEN // N_SHARD), (1, 0, 2)),
        "w_out_even": dw_out_even.reshape(N_SHARD, d // N_SHARD, d),
        "w_in_odd": dw_in_odd,
        "w_out_odd": dw_out_odd.reshape(N_SHARD, d // N_SHARD, d),
    }
    rep = {
        "ffn1_norm": jnp.concatenate([dg_f1_l0, dg_f1_l1], axis=0),
        "mix_norm": jnp.concatenate([dg_mix_l0, dg_mix_l1], axis=0),
        "ffn2_norm": jnp.concatenate([dg_f2_l0, dg_f2_l1], axis=0),
        "final_norm": dg_final,
        "pool_w": _diag_blocks(dpool_wbd, len(POOL_WINDOWS)),
        "pool_scale": dpool_scale,
        "hgrn_lb_logits": dlb_logits,
        "hgrn_gnorm": jnp.sum(dgn_heads, axis=0),
        "lru_wa": _diag_blocks2(dwa_bd),
        "lru_wx": _diag_blocks2(dwx_bd),
    }
    dmeta = jnp.transpose(dh0[:N_META].reshape(N_META, N_SHARD, 2, LANE), (1, 0, 2, 3)).reshape(N_SHARD, 32, LANE)
    packs = [_pack_rows(rep, REP_SPEC)]
    for s in range(N_SHARD):
        sh = {
            "meta_tokens": dmeta[s], "conv_w": dconv_w[s], "lru_conv_w": dlru_vec[s, 0:4],
            "conv_b": dconv_vec[s, 0:1], "conv_ln_g": dconv_vec[s, 1:2], "conv_ln_b": dconv_vec[s, 2:3],
            "lru_conv_b": dlru_vec[s, 4:5], "lru_ba": dlru_vec[s, 5:6], "lru_bx": dlru_vec[s, 6:7],
            "lru_lambda": dlru_vec[s, 7:8],
        }
        packs.append(_pack_rows(sh, SH_SPEC))
    return loss, grad_x, big, jnp.concatenate(packs, axis=0)


def _block_diag2(heads):
    nb = heads.shape[0] // 2
    return jnp.stack([_block_diag(heads[2 * j:2 * j + 2]) for j in range(nb)])


def _diag_blocks2(mats):
    return jnp.concatenate([_diag_blocks(mats[j], 2) for j in range(mats.shape[0])], axis=0)


def _kernel_v2(x, meta_tokens, ffn1_norm, ffn1_wg, ffn1_wu, ffn1_wd, mix_norm, ffn2_norm, ffn2_wg, ffn2_wu, ffn2_wd, w_in_even, pool_w, pool_scale, hgrn_lb_logits, hgrn_gnorm, w_out_even, w_in_odd, conv_w, conv_b, conv_ln_g, conv_ln_b, lru_conv_w, lru_conv_b, lru_wa, lru_ba, lru_wx, lru_bx, lru_lambda, w_out_odd, final_norm, loss_target, m_meta_tokens, m_ffn1_norm, m_ffn1_wg, m_ffn1_wu, m_ffn1_wd, m_mix_norm, m_ffn2_norm, m_ffn2_wg, m_ffn2_wu, m_ffn2_wd, m_w_in_even, m_pool_w, m_pool_scale, m_hgrn_lb_logits, m_hgrn_gnorm, m_w_out_even, m_w_in_odd, m_conv_w, m_conv_b, m_conv_ln_g, m_conv_ln_b, m_lru_conv_w, m_lru_conv_b, m_lru_wa, m_lru_ba, m_lru_wx, m_lru_bx, m_lru_lambda, m_w_out_odd, m_final_norm, v_meta_tokens, v_ffn1_norm, v_ffn1_wg, v_ffn1_wu, v_ffn1_wd, v_mix_norm, v_ffn2_norm, v_ffn2_wg, v_ffn2_wu, v_ffn2_wd, v_w_in_even, v_pool_w, v_pool_scale, v_hgrn_lb_logits, v_hgrn_gnorm, v_w_out_even, v_w_in_odd, v_conv_w, v_conv_b, v_conv_ln_g, v_conv_ln_b, v_lru_conv_w, v_lru_conv_b, v_lru_wa, v_lru_ba, v_lru_wx, v_lru_bx, v_lru_lambda, v_w_out_odd, v_final_norm):
    args = locals()
    w = {n: args[n] for n in WEIGHT_NAMES}
    m = {n: args["m_" + n] for n in WEIGHT_NAMES}
    v = {n: args["v_" + n] for n in WEIGHT_NAMES}
    shapes = {n: w[n].shape for n in WEIGHT_NAMES}

    big_in = [_rows2d(w[n]).astype(BF16) for n in BIG]
    small_sh = _pack_rows(w, SH_SPEC)
    gath = _allgather_shards(big_in + [small_sh], [True] * len(BIG) + [False])
    gathered = dict(zip(BIG, gath[:len(BIG)]))
    sm = gath[len(BIG)]
    sh_shapes = {n: (N_SHARD,) + tuple(shapes[n]) for n, _ in SH_SPEC}
    per_shard = [_unpack_rows(sm[s], SH_SPEC, shapes) for s in range(N_SHARD)]
    small_full = {}
    for n, _ in SH_SPEC:
        stacked = [per_shard[s][n] for s in range(N_SHARD)]
        small_full[n] = jnp.concatenate([p.reshape(-1, p.shape[-1]) for p in stacked], axis=-1)
    small_full["conv_w"] = jnp.concatenate(
        [small_full["conv_w"], jnp.zeros((CONV_HALO - CONV_WIDTH, D_CONV), F32)], axis=0)

    loss, grad_x, big, small_part = _local_step(x[0], loss_target[0], w, gathered, small_full)
    loss = lax.psum(loss[0, 0], ("x", "y", "c"))

    core = lax.axis_index("c").astype(jnp.int32).reshape(1)
    parts = [big[n] for n in BIG]
    recvd = _pair_exchange_halves(parts)
    pair = _pair_add(parts, recvd, core)
    chip = (2 * lax.axis_index("x") + lax.axis_index("y")).astype(jnp.int32).reshape(1)
    slots = _scatter_to_owners(pair, _own_slot(pair, chip))
    halves = _sum_chips(slots, core)
    full = _pair_allgather_halves(halves)
    out_g, out_d, out_m, out_v = {}, {}, {}, {}
    for n, g in zip(BIG, full):
        res = _adamw(_rows2d(w[n]), _rows2d(m[n]), _rows2d(v[n]), g, 0, ADAM_BLOCKS[n])
        out_g[n], out_d[n], out_m[n], out_v[n] = [r.reshape(shapes[n]) for r in res]

    gathered_small = _allgather_all(small_part)

    def pack_small(src):
        return jnp.concatenate([_pack_rows(src, REP_SPEC), _pack_rows(src, SH_SPEC)], axis=0)

    res = _small_reduce_adamw(gathered_small, pack_small(w), pack_small(m), pack_small(v), REP_ROWS, SH_ROWS)
    for dst, packed in zip((out_g, out_d, out_m, out_v), res):
        dst.update(_unpack_rows(packed[:REP_ROWS], REP_SPEC, shapes))
        dst.update(_unpack_rows(packed[REP_ROWS:], SH_SPEC, shapes))

    return (loss, grad_x[None], *[out_g[n] for n in WEIGHT_NAMES], *[out_d[n] for n in WEIGHT_NAMES],
            *[out_m[n] for n in WEIGHT_NAMES], *[out_v[n] for n in WEIGHT_NAMES])


GATHER_GROUPS = (
    (("small", 0),),
    (("ffn1_wg", 0), ("ffn1_wu", 0), ("ffn1_wd", 0)),
    (("w_in_even", 0), ("w_out_even", 0)),
    (("ffn2_wg", 0), ("ffn2_wu", 0), ("ffn2_wd", 0)),
    (("ffn1_wg", 1), ("ffn1_wu", 1), ("ffn1_wd", 1)),
    (("w_in_odd", 0), ("w_out_odd", 0)),
    (("ffn2_wg", 1), ("ffn2_wu", 1), ("ffn2_wd", 1)),
)
ADAM_ROW_BLOCKS = {"ffn1_wg": 2, "ffn1_wu": 2, "ffn2_wg": 2, "ffn2_wu": 2, "ffn1_wd": 2, "ffn2_wd": 2,
                   "w_in_even": 4, "w_out_even": 2, "w_in_odd": 4, "w_out_odd": 2}
TRANSPOSED = ("ffn1_wg", "ffn1_wu", "ffn2_wg", "ffn2_wu", "w_in_even")
SCATTER_DEPTH = 2


def _unpack_small(sm, shapes):
    per_shard = [_unpack_rows(sm[s], SH_SPEC, shapes) for s in range(N_SHARD)]
    full = {}
    for n, _ in SH_SPEC:
        full[n] = jnp.concatenate([per_shard[s][n].reshape(-1, shapes[n][-1]) for s in range(N_SHARD)], axis=-1)
    full["conv_w"] = jnp.concatenate([full["conv_w"], jnp.zeros((CONV_HALO - CONV_WIDTH, D_CONV), F32)], axis=0)
    return full


def _local_step(x, tgt, w, shapes, fetch, emit):
    s_len, d = x.shape
    t_real = s_len + N_META
    tp = -(-t_real // ROW_ALIGN) * ROW_ALIGN
    tm = _tile(tp, 832, ROW_ALIGN)
    tm_small = _tile(tp, 416, 16)
    tr = _tile(tp, 416, SUBLANE)

    def gain(name, layer):
        return w[name][layer:layer + 1]

    pool_wbd = _block_diag(w["pool_w"][0]).astype(BF16)
    pool_scale = w["pool_scale"]
    wa_bd = _block_diag2(w["lru_wa"][0]).astype(BF16)
    wx_bd = _block_diag2(w["lru_wx"][0]).astype(BF16)
    mst, msk, n_lev = _hgrn_consts(HG_CHUNK)

    (sm,) = fetch(0, None)
    sf = _unpack_small(sm, shapes)
    h0 = jnp.concatenate([sf["meta_tokens"], x, jnp.zeros((tp - t_real, d), F32)], axis=0)
    tgt_pad = jnp.concatenate([jnp.zeros((N_META, d), F32), tgt, jnp.zeros((tp - t_real, d), F32)], axis=0)
    f1l0 = fetch(1, h0)
    h1, *s1 = _ffn_fwd(h0, gain("ffn1_norm", 0), *f1l0, 0, tm)
    w_in_even4, w_out_even4 = fetch(2, h1)
    w_out_even = w_out_even4.reshape(d, d)
    even_piece = [(w_in_even4.reshape(D_IN_EVEN, d), (D_IN_EVEN, d), (0, 0))]
    p0, nm0 = _proj_fwd(h1, gain("mix_norm", 0), 0, even_piece, tm_small, wt=True)
    ya = _pool_fwd(p0, pool_wbd, pool_scale, tr)
    yb, states = _hgrn_fwd(p0, w["hgrn_lb_logits"], w["hgrn_gnorm"], mst, msk, n_lev, tm)
    h2 = _out_fwd(h1, ya, yb, w_out_even, tm)
    f2l0 = fetch(3, h2)
    h3, *s2 = _ffn_fwd(h2, gain("ffn2_norm", 0), *f2l0, 0, tm)
    f1l1 = fetch(4, h3)
    h4, *s3 = _ffn_fwd(h3, gain("ffn1_norm", 1), *f1l1, 0, tm)
    w_in_odd4, w_out_odd4 = fetch(5, h4)
    w_out_odd = w_out_odd4.reshape(d, d)
    odd_pieces = [(w_in_odd4, (1, d, D_IN_ODD // N_SHARD), (k, 0, 0)) for k in range(N_SHARD)]
    p1, nm1 = _proj_fwd(h4, gain("mix_norm", 1), 1, odd_pieces, tm_small)
    yc = _convmod_fwd(p1, sf["conv_w"], sf["conv_b"], sf["conv_ln_g"], sf["conv_ln_b"], tr)
    lru_args = (sf["lru_conv_w"], sf["lru_conv_b"], wa_bd, sf["lru_ba"], wx_bd, sf["lru_bx"], sf["lru_lambda"])
    yd, hs = _lru_fwd(p1, *lru_args)
    h5 = _out_fwd(h4, yc, yd, w_out_odd, tm)
    f2l1 = fetch(6, h5)
    h6, *s4 = _ffn_fwd(h5, gain("ffn2_norm", 1), *f2l1, 0, tm)
    loss, dh6, dg_final = _loss_bwd(h6, w["final_norm"].reshape(1, d), tgt_pad, t_real, tm)

    def ffn_bwd(dho, h, saved, norm, wts, after=()):
        ga, gb, sa, n = saved
        dh, da, db, dg, dy = _ffn_bwd_act(dho, h, norm, ga, gb, *wts, 0, tm, after)
        return dh, dg, _ffn_bwd_w(dy, n, sa, da, db, tm)

    dh5, dg_f2_l1, g = ffn_bwd(dh6, h5, s4, gain("ffn2_norm", 1), f2l1)
    sent = emit((("ffn2_wg", 1), ("ffn2_wu", 1), ("ffn2_wd", 1)), g)
    dyc, dyd, dw_out_odd = _out_bwd(dh5, yc, yd, w_out_odd, tm, tuple(sent))
    dca, dcb, dconv_w, dconv_vec = _convmod_bwd(p1, dyc, sf["conv_w"], sf["conv_b"], sf["conv_ln_g"],
                                                sf["conv_ln_b"], tr)
    dlx, dlg, dwa_bd, dwx_bd, dlru_vec = _lru_bwd(p1, hs, dyd, *lru_args)
    dp1 = [dca, dcb, dlx, dlg]
    dh4, dg_mix_l1 = _proj_bwd_act(dh5, h4, gain("mix_norm", 1), 1, dp1, odd_pieces, tm_small)
    dw_in_odd = jnp.stack(_proj_bwd_w(nm1, dp1, tm))
    dh3, dg_f1_l1, g = ffn_bwd(dh4, h3, s3, gain("ffn1_norm", 1), f1l1)
    sent = emit((("w_out_odd", 0), ("w_in_odd", 0), ("ffn1_wg", 1), ("ffn1_wu", 1), ("ffn1_wd", 1)),
                [dw_out_odd.reshape(N_SHARD, d // N_SHARD, d), dw_in_odd] + list(g))
    dh2, dg_f2_l0, g_f2l0 = ffn_bwd(dh3, h2, s2, gain("ffn2_norm", 0), f2l0, tuple(sent))
    dya, dyb, dw_out_even = _out_bwd(dh2, ya, yb, w_out_even, tm)
    dpool, dpool_wbd, dpool_scale = _pool_bwd(p0, dya, pool_wbd, pool_scale, tr)
    dq, dz, dv, dgate, dlb_logits, dgn_heads = _hgrn_bwd(p0, dyb, states, w["hgrn_lb_logits"], w["hgrn_gnorm"],
                                                         mst, msk, n_lev, tm)
    dp0 = [jnp.concatenate([dpool, dq, dz, dv, dgate], axis=1)]
    dh1, dg_mix_l0 = _proj_bwd_act(dh2, h1, gain("mix_norm", 0), 0, dp0, even_piece, tm_small, wt=True)
    (dw_in_even_t,) = _proj_bwd_w(nm0, dp0, tm_small, wt=True)
    sent = emit((("ffn2_wg", 0), ("ffn2_wu", 0), ("ffn2_wd", 0), ("w_out_even", 0), ("w_in_even", 0)),
                list(g_f2l0) + [dw_out_even.reshape(N_SHARD, d // N_SHARD, d),
                                dw_in_even_t.reshape(N_SHARD, D_IN_EVEN // N_SHARD, d)])
    dh0, dg_f1_l0, g = ffn_bwd(dh1, h0, s1, gain("ffn1_norm", 0), f1l0, tuple(sent))
    emit((("ffn1_wg", 0), ("ffn1_wu", 0), ("ffn1_wd", 0)), g)

    grad_x = dh0[N_META:t_real]
    rep = {
        "ffn1_norm": jnp.concatenate([dg_f1_l0, dg_f1_l1], axis=0),
        "mix_norm": jnp.concatenate([dg_mix_l0, dg_mix_l1], axis=0),
        "ffn2_norm": jnp.concatenate([dg_f2_l0, dg_f2_l1], axis=0),
        "final_norm": dg_final,
        "pool_w": _diag_blocks(dpool_wbd, len(POOL_WINDOWS)),
        "pool_scale": dpool_scale,
        "hgrn_lb_logits": dlb_logits,
        "hgrn_gnorm": jnp.sum(dgn_heads, axis=0),
        "lru_wa": _diag_blocks2(dwa_bd),
        "lru_wx": _diag_blocks2(dwx_bd),
    }
    dmeta = jnp.transpose(dh0[:N_META].reshape(N_META, N_SHARD, 2, LANE), (1, 0, 2, 3)).reshape(N_SHARD, 32, LANE)
    packs = [_pack_rows(rep, REP_SPEC)]
    for s in range(N_SHARD):
        sh = {
            "meta_tokens": dmeta[s], "conv_w": dconv_w[s], "lru_conv_w": dlru_vec[s, 0:4],
            "conv_b": dconv_vec[s, 0:1], "conv_ln_g": dconv_vec[s, 1:2], "conv_ln_b": dconv_vec[s, 2:3],
            "lru_conv_b": dlru_vec[s, 4:5], "lru_ba": dlru_vec[s, 5:6], "lru_bx": dlru_vec[s, 6:7],
            "lru_lambda": dlru_vec[s, 7:8],
        }
        packs.append(_pack_rows(sh, SH_SPEC))
    return loss, grad_x, jnp.concatenate(packs, axis=0)


def kernel(x, meta_tokens, ffn1_norm, ffn1_wg, ffn1_wu, ffn1_wd, mix_norm, ffn2_norm, ffn2_wg, ffn2_wu, ffn2_wd, w_in_even, pool_w, pool_scale, hgrn_lb_logits, hgrn_gnorm, w_out_even, w_in_odd, conv_w, conv_b, conv_ln_g, conv_ln_b, lru_conv_w, lru_conv_b, lru_wa, lru_ba, lru_wx, lru_bx, lru_lambda, w_out_odd, final_norm, loss_target, m_meta_tokens, m_ffn1_norm, m_ffn1_wg, m_ffn1_wu, m_ffn1_wd, m_mix_norm, m_ffn2_norm, m_ffn2_wg, m_ffn2_wu, m_ffn2_wd, m_w_in_even, m_pool_w, m_pool_scale, m_hgrn_lb_logits, m_hgrn_gnorm, m_w_out_even, m_w_in_odd, m_conv_w, m_conv_b, m_conv_ln_g, m_conv_ln_b, m_lru_conv_w, m_lru_conv_b, m_lru_wa, m_lru_ba, m_lru_wx, m_lru_bx, m_lru_lambda, m_w_out_odd, m_final_norm, v_meta_tokens, v_ffn1_norm, v_ffn1_wg, v_ffn1_wu, v_ffn1_wd, v_mix_norm, v_ffn2_norm, v_ffn2_wg, v_ffn2_wu, v_ffn2_wd, v_w_in_even, v_pool_w, v_pool_scale, v_hgrn_lb_logits, v_hgrn_gnorm, v_w_out_even, v_w_in_odd, v_conv_w, v_conv_b, v_conv_ln_g, v_conv_ln_b, v_lru_conv_w, v_lru_conv_b, v_lru_wa, v_lru_ba, v_lru_wx, v_lru_bx, v_lru_lambda, v_w_out_odd, v_final_norm):
    args = locals()
    w = {n: args[n] for n in WEIGHT_NAMES}
    m = {n: args["m_" + n] for n in WEIGHT_NAMES}
    v = {n: args["v_" + n] for n in WEIGHT_NAMES}
    shapes = {n: w[n].shape for n in WEIGHT_NAMES}
    core = lax.axis_index("c").astype(jnp.int32).reshape(1)
    chip = (2 * lax.axis_index("x") + lax.axis_index("y")).astype(jnp.int32).reshape(1)
    me = 2 * chip + core

    def view(a, n):
        return jnp.swapaxes(a, 1, 2) if n in TRANSPOSED else a

    wv, mv, vv = [{n: view(src[n], n) for n in BIG} for src in (w, m, v)]

    def shard(key):
        n, l = key
        return _pack_rows(w, SH_SPEC) if n == "small" else wv[n][l].astype(BF16)

    started = {}
    for groups in (GATHER_GROUPS[:2], GATHER_GROUPS[2:]):
        gkeys = [key for grp in groups for key in grp]
        ssem, rsem, srcs, lands, token = _gather_start([shard(key) for key in gkeys],
                                                       [key in GATHER_GROUPS[1] for key in gkeys])
        for k, key in enumerate(gkeys):
            started[key] = (ssem, rsem, srcs[k], lands[k], k, token)

    def pack_small(src):
        return jnp.concatenate([_pack_rows(src, REP_SPEC), _pack_rows(src, SH_SPEC)], axis=0)

    small_packs = [pack_small(src) for src in (w, m, v)]

    def fetch(group, after):
        st = [started[key] for key in GATHER_GROUPS[group]]
        deps = (st[0][5],) if after is None else (after,)
        if group == 1:
            deps += (started[GATHER_GROUPS[2][0]][5],) + tuple(small_packs)
        split = group == 1
        got = _gather_wait(st[0][0], st[0][1], [s[2] for s in st], [s[3] for s in st], [s[4] for s in st], deps,
                           split)
        return _pair_forward(got) if split else got

    in_flight, reduced = [], {}

    def collect(entry, after):
        gkeys, gs_sem, gr_sem, grads_thru, slots_thru, _ = entry
        slots = _reduce_wait(gs_sem, gr_sem, grads_thru, slots_thru, after)
        full = _pair_allgather_halves(_sum_devices(slots, core))
        reduced.update(zip(gkeys, full))
        return full[0]

    def emit(gkeys, grads):
        grads = list(grads)
        in_flight.append((gkeys,) + tuple(_reduce_start(grads, _own_part(grads, chip, core, me))))
        token = in_flight[-1][-1]
        if len(in_flight) > SCATTER_DEPTH:
            return token, collect(in_flight[-1 - SCATTER_DEPTH], (token,))
        return (token,)

    loss, grad_x, small_part = _local_step(x[0], loss_target[0], w, shapes, fetch, emit)
    loss = lax.psum(loss[0, 0], ("x", "y", "c"))

    small_res = _small_reduce_adamw(_allgather_all(small_part), *small_packs, REP_ROWS, SH_ROWS)
    out_g, out_d, out_m, out_v = {}, {}, {}, {}
    deps = (small_res[0],)
    for entry in in_flight[-SCATTER_DEPTH:]:
        collect(entry, deps)
        for n in BIG:
            layers = range(shapes[n][0])
            if n not in out_g and all((n, l) in reduced for l in layers):
                res = _adamw(wv[n], mv[n], vv[n], [reduced[(n, l)] for l in layers], ADAM_ROW_BLOCKS[n])
                out_g[n], out_d[n], out_m[n], out_v[n] = [view(r, n) for r in res]
                deps += (res[1],)

    for dst, packed in zip((out_g, out_d, out_m, out_v), small_res):
        dst.update(_unpack_rows(packed[:REP_ROWS], REP_SPEC, shapes))
        dst.update(_unpack_rows(packed[REP_ROWS:], SH_SPEC, shapes))

    return (loss, grad_x[None], *[out_g[n] for n in WEIGHT_NAMES], *[out_d[n] for n in WEIGHT_NAMES],
            *[out_m[n] for n in WEIGHT_NAMES], *[out_v[n] for n in WEIGHT_NAMES])
```

```python
import functools

import numpy as np
import jax
import jax.numpy as jnp
from jax import lax
from jax.experimental import pallas as pl
from jax.experimental.pallas import tpu as pltpu

F32 = jnp.float32
BF16 = jnp.bfloat16
MESH = pl.DeviceIdType.MESH

EPS = 1e-6
N_META = 16
D_MODEL = 1024
D_FF = 2816
N_SHARD = 4
FF_SHARD = D_FF // N_SHARD
D_POOL = 256
POOL_GROUP = 64
POOL_WINDOWS = (2, 4, 8, 16)
D_HGRN = 768
HG_HEADS = 6
HEAD = 128
HG_CHUNK = 64
HG_HEADS_PER_STEP = 2
HG_UNROLL = 2
D_IN_EVEN = D_POOL + 4 * D_HGRN
D_CONV = 512
CONV_WIDTH = 31
CONV_HALO = 32
D_LRU = 512
LRU_CONV = 4
LRU_HALO = 8
LRU_C = 8.0
D_IN_ODD = 2 * D_CONV + 2 * D_LRU
SUBLANE = 8
ROW_ALIGN = 64

ADAM_LR = 0.001
ADAM_B1 = 0.9
ADAM_B2 = 0.999
ADAM_EPS = 1e-08
ADAM_WD = 0.01
ADAM_STEP = 10

VMEM_LIMIT_MB = 56


def _cparams(n_grid_axes=0, vmem_mb=VMEM_LIMIT_MB):
    sem = ("arbitrary",) * n_grid_axes if n_grid_axes else None
    return pltpu.CompilerParams(dimension_semantics=sem, vmem_limit_bytes=vmem_mb * 1024 * 1024)


def _tile(n, target, mult):
    best = None
    for t in range(mult, min(n, target) + 1, mult):
        if n % t == 0:
            best = t
    assert best is not None, (n, target, mult)
    return best


def _dot(a, b):
    return jnp.dot(a, b, preferred_element_type=F32)


def _dot_nt(a, b):
    return lax.dot_general(a, b, (((1,), (1,)), ((), ())), preferred_element_type=F32)


def _dot_tn(a, b):
    return lax.dot_general(a, b, (((0,), (0,)), ((), ())), preferred_element_type=F32)


def _sigmoid(x):
    return 1.0 / (1.0 + jnp.exp(-x))


def _colsum(x):
    return jnp.sum(x, axis=0, keepdims=True)


def _rms_stats(h):
    rstd = lax.rsqrt(jnp.mean(h * h, axis=-1, keepdims=True) + EPS)
    return rstd, h * rstd


def _rms_bwd(dn, g, rstd, xhat):
    dng = dn * g
    dh = rstd * (dng - xhat * jnp.mean(dng * xhat, axis=-1, keepdims=True))
    return dh, _colsum(dn * xhat)


def _ffn_fwd(h, norm, wg4, wu4, wd4, layer, tm):
    tp, d = h.shape
    nt = tp // tm

    def body(h_ref, g_ref, wg_ref, wu_ref, wd_ref, ho_ref, ga_ref, gb_ref, sa_ref, n_ref, n_sc, acc):
        s = pl.program_id(1)

        @pl.when(s == 0)
        def _():
            hh = h_ref[...]
            rstd, xhat = _rms_stats(hh)
            n = (xhat * g_ref[...]).astype(BF16)
            n_sc[...] = n
            n_ref[...] = n
            acc[...] = jnp.zeros_like(acc)

        n = n_sc[...]
        a = _dot_nt(n, wg_ref[0])
        b = _dot_nt(n, wu_ref[0])
        sig = _sigmoid(a)
        sil = a * sig
        ga_ref[0] = (sig * (1.0 + a * (1.0 - sig)) * b).astype(BF16)
        gb_ref[0] = sil.astype(BF16)
        sg = (sil * b).astype(BF16)
        sa_ref[0] = sg
        acc[...] += _dot(sg, wd_ref[0])

        @pl.when(s == N_SHARD - 1)
        def _():
            ho_ref[...] = h_ref[...] + 0.5 * acc[...]

    return pl.pallas_call(
        body, name="ffn_fwd",
        grid=(nt, N_SHARD),
        in_specs=[
            pl.BlockSpec((tm, d), lambda i, s: (i, 0)),
            pl.BlockSpec((1, d), lambda i, s: (0, 0)),
            pl.BlockSpec((1, FF_SHARD, d), lambda i, s: (s, layer, 0)),
            pl.BlockSpec((1, FF_SHARD, d), lambda i, s: (s, layer, 0)),
            pl.BlockSpec((1, FF_SHARD, d), lambda i, s: (s, layer, 0)),
        ],
        out_specs=[
            pl.BlockSpec((tm, d), lambda i, s: (i, 0)),
            pl.BlockSpec((1, tm, FF_SHARD), lambda i, s: (s, i, 0)),
            pl.BlockSpec((1, tm, FF_SHARD), lambda i, s: (s, i, 0)),
            pl.BlockSpec((1, tm, FF_SHARD), lambda i, s: (s, i, 0)),
            pl.BlockSpec((tm, d), lambda i, s: (i, 0)),
        ],
        out_shape=[
            jax.ShapeDtypeStruct((tp, d), F32),
            jax.ShapeDtypeStruct((N_SHARD, tp, FF_SHARD), BF16),
            jax.ShapeDtypeStruct((N_SHARD, tp, FF_SHARD), BF16),
            jax.ShapeDtypeStruct((N_SHARD, tp, FF_SHARD), BF16),
            jax.ShapeDtypeStruct((tp, d), BF16),
        ],
        scratch_shapes=[pltpu.VMEM((tm, d), BF16), pltpu.VMEM((tm, d), F32)],
        compiler_params=_cparams(2),
    )(h, norm, wg4, wu4, wd4)


def _ffn_bwd_act(dho, h, norm, ga4, gb4, wg4, wu4, wd4, layer, tm, after=()):
    tp, d = h.shape
    nt = tp // tm

    def body(dho_ref, h_ref, g_ref, ga_ref, gb_ref, wg_ref, wu_ref, wd_ref, *rest):
        dh_ref, da_ref, db_ref, dg_ref, dy_ref, dn_sc = rest[len(after):]
        i = pl.program_id(0)
        s = pl.program_id(1)

        @pl.when(s == 0)
        def _():
            dy_ref[...] = (0.5 * dho_ref[...]).astype(BF16)
            dn_sc[...] = jnp.zeros_like(dn_sc)

        @pl.when((s == 0) & (i == 0))
        def _():
            dg_ref[...] = jnp.zeros_like(dg_ref)

        ds = _dot_nt(dy_ref[...], wd_ref[0])
        da = (ds * ga_ref[0].astype(F32)).astype(BF16)
        db = (ds * gb_ref[0].astype(F32)).astype(BF16)
        da_ref[0] = da
        db_ref[0] = db
        dn_sc[...] += _dot(da, wg_ref[0]) + _dot(db, wu_ref[0])

        @pl.when(s == N_SHARD - 1)
        def _():
            rstd, xhat = _rms_stats(h_ref[...])
            dh, dg = _rms_bwd(dn_sc[...], g_ref[...], rstd, xhat)
            dh_ref[...] = dho_ref[...] + dh
            dg_ref[...] += dg

    return pl.pallas_call(
        body, name="ffn_bwd_act",
        grid=(nt, N_SHARD),
        in_specs=[
            pl.BlockSpec((tm, d), lambda i, s: (i, 0)),
            pl.BlockSpec((tm, d), lambda i, s: (i, 0)),
            pl.BlockSpec((1, d), lambda i, s: (0, 0)),
            pl.BlockSpec((1, tm, FF_SHARD), lambda i, s: (s, i, 0)),
            pl.BlockSpec((1, tm, FF_SHARD), lambda i, s: (s, i, 0)),
            pl.BlockSpec((1, FF_SHARD, d), lambda i, s: (s, layer, 0)),
            pl.BlockSpec((1, FF_SHARD, d), lambda i, s: (s, layer, 0)),
            pl.BlockSpec((1, FF_SHARD, d), lambda i, s: (s, layer, 0)),
        ] + [pl.BlockSpec(memory_space=pl.ANY)] * len(after),
        out_specs=[
            pl.BlockSpec((tm, d), lambda i, s: (i, 0)),
            pl.BlockSpec((1, tm, FF_SHARD), lambda i, s: (s, i, 0)),
            pl.BlockSpec((1, tm, FF_SHARD), lambda i, s: (s, i, 0)),
            pl.BlockSpec((1, d), lambda i, s: (0, 0)),
            pl.BlockSpec((tm, d), lambda i, s: (i, 0)),
        ],
        out_shape=[
            jax.ShapeDtypeStruct((tp, d), F32),
            jax.ShapeDtypeStruct((N_SHARD, tp, FF_SHARD), BF16),
            jax.ShapeDtypeStruct((N_SHARD, tp, FF_SHARD), BF16),
            jax.ShapeDtypeStruct((1, d), F32),
            jax.ShapeDtypeStruct((tp, d), BF16),
        ],
        scratch_shapes=[pltpu.VMEM((tm, d), F32)],
        compiler_params=_cparams(2),
    )(dho, h, norm, ga4, gb4, wg4, wu4, wd4, *after)


def _ffn_bwd_w(dy, n, sa4, da4, db4, tm, after=()):
    tp, d = n.shape
    nt = tp // tm

    def body(dy_ref, n_ref, sa_ref, da_ref, db_ref, *rest):
        og_ref, ou_ref, od_ref, accg, accu, accd = rest[len(after):]
        i = pl.program_id(1)

        @pl.when(i == 0)
        def _():
            accg[...] = jnp.zeros_like(accg)
            accu[...] = jnp.zeros_like(accu)
            accd[...] = jnp.zeros_like(accd)

        nn = n_ref[...]
        accg[...] += _dot_tn(da_ref[0], nn)
        accu[...] += _dot_tn(db_ref[0], nn)
        accd[...] += _dot_tn(sa_ref[0], dy_ref[...])

        @pl.when(i == nt - 1)
        def _():
            og_ref[0] = accg[...].astype(BF16)
            ou_ref[0] = accu[...].astype(BF16)
            od_ref[0] = accd[...].astype(BF16)

    in_specs = [
        pl.BlockSpec((tm, d), lambda s, i: (i, 0)),
        pl.BlockSpec((tm, d), lambda s, i: (i, 0)),
        pl.BlockSpec((1, tm, FF_SHARD), lambda s, i: (s, i, 0)),
        pl.BlockSpec((1, tm, FF_SHARD), lambda s, i: (s, i, 0)),
        pl.BlockSpec((1, tm, FF_SHARD), lambda s, i: (s, i, 0)),
    ]
    return pl.pallas_call(
        body, name="ffn_bwd_w",
        grid=(N_SHARD, nt),
        in_specs=in_specs + [pl.BlockSpec(memory_space=pl.ANY)] * len(after),
        out_specs=[pl.BlockSpec((1, FF_SHARD, d), lambda s, i: (s, 0, 0))] * 3,
        out_shape=[jax.ShapeDtypeStruct((N_SHARD, FF_SHARD, d), BF16)] * 3,
        scratch_shapes=[pltpu.VMEM((FF_SHARD, d), F32)] * 3,
        compiler_params=_cparams(2),
    )(dy, n, sa4, da4, db4, *after)


def _proj_fwd(h, norm, layer, w_pieces, tm, wt=False):
    tp, d = h.shape
    widths = [bs[-2] if wt else bs[-1] for _, bs, _ in w_pieces]
    ntot = sum(widths)
    npc = len(w_pieces)

    def body(*refs):
        h_ref, g_ref = refs[:2]
        w_refs = refs[2:2 + npc]
        p_ref, n_ref = refs[2 + npc:]
        rstd, xhat = _rms_stats(h_ref[...])
        n = (xhat * g_ref[...]).astype(BF16)
        n_ref[...] = n
        off = 0
        for k in range(npc):
            w = w_refs[k][...]
            w = w.reshape(w.shape[-2], w.shape[-1])
            p_ref[:, off:off + widths[k]] = _dot_nt(n, w) if wt else _dot(n, w)
            off += widths[k]

    in_specs = [pl.BlockSpec((tm, d), lambda i: (i, 0)), pl.BlockSpec((1, d), lambda i: (0, 0))]
    for _, bs, idx in w_pieces:
        in_specs.append(pl.BlockSpec(bs, functools.partial(lambda i, idx: idx, idx=idx)))
    return pl.pallas_call(
        body, name="proj_fwd",
        grid=(tp // tm,),
        in_specs=in_specs,
        out_specs=[pl.BlockSpec((tm, ntot), lambda i: (i, 0)), pl.BlockSpec((tm, d), lambda i: (i, 0))],
        out_shape=[jax.ShapeDtypeStruct((tp, ntot), F32), jax.ShapeDtypeStruct((tp, d), BF16)],
        compiler_params=_cparams(1),
    )(h, norm, *[w for w, _, _ in w_pieces])


def _proj_bwd_act(dres, h, norm, layer, dp_pieces, w_pieces, tm, wt=False):
    tp, d = h.shape
    npc = len(w_pieces)

    def body(*refs):
        dres_ref, h_ref, g_ref = refs[:3]
        dp_refs = refs[3:3 + npc]
        w_refs = refs[3 + npc:3 + 2 * npc]
        dh_ref, dg_ref = refs[3 + 2 * npc:]
        i = pl.program_id(0)

        @pl.when(i == 0)
        def _():
            dg_ref[...] = jnp.zeros_like(dg_ref)

        dn = None
        for k in range(npc):
            w = w_refs[k][...]
            w = w.reshape(w.shape[-2], w.shape[-1])
            t = _dot(dp_refs[k][...], w) if wt else _dot_nt(dp_refs[k][...], w)
            dn = t if dn is None else dn + t
        rstd, xhat = _rms_stats(h_ref[...])
        dh, dg = _rms_bwd(dn, g_ref[...], rstd, xhat)
        dh_ref[...] = dres_ref[...] + dh
        dg_ref[...] += dg

    in_specs = [pl.BlockSpec((tm, d), lambda i: (i, 0)), pl.BlockSpec((tm, d), lambda i: (i, 0)),
                pl.BlockSpec((1, d), lambda i: (0, 0))]
    for dp in dp_pieces:
        in_specs.append(pl.BlockSpec((tm, dp.shape[1]), lambda i: (i, 0)))
    for _, bs, idx in w_pieces:
        in_specs.append(pl.BlockSpec(bs, functools.partial(lambda i, idx: idx, idx=idx)))
    return pl.pallas_call(
        body, name="proj_bwd_act",
        grid=(tp // tm,),
        in_specs=in_specs,
        out_specs=[pl.BlockSpec((tm, d), lambda i: (i, 0)), pl.BlockSpec((1, d), lambda i: (0, 0))],
        out_shape=[jax.ShapeDtypeStruct((tp, d), F32), jax.ShapeDtypeStruct((1, d), F32)],
        compiler_params=_cparams(1),
    )(dres, h, norm, *dp_pieces, *[w for w, _, _ in w_pieces])


def _proj_bwd_w(n, dp_pieces, tm, wt=False):
    tp, d = n.shape
    npc = len(dp_pieces)
    widths = [dp.shape[1] for dp in dp_pieces]
    oshape = (lambda w: (w, d)) if wt else (lambda w: (d, w))

    def body(*refs):
        n_ref = refs[0]
        dp_refs = refs[1:1 + npc]
        o_refs = refs[1 + npc:1 + 2 * npc]
        accs = refs[1 + 2 * npc:]
        i = pl.program_id(0)

        @pl.when(i == 0)
        def _():
            for acc in accs:
                acc[...] = jnp.zeros_like(acc)

        nn = n_ref[...]
        for k in range(npc):
            accs[k][...] += _dot_tn(dp_refs[k][...], nn) if wt else _dot_tn(nn, dp_refs[k][...])

        @pl.when(i == pl.num_programs(0) - 1)
        def _():
            for k in range(npc):
                o_refs[k][...] = accs[k][...].astype(BF16)

    return pl.pallas_call(
        body, name="proj_bwd_w",
        grid=(tp // tm,),
        in_specs=[pl.BlockSpec((tm, d), lambda i: (i, 0))]
        + [pl.BlockSpec((tm, w), lambda i: (i, 0)) for w in widths],
        out_specs=[pl.BlockSpec(oshape(w), lambda i: (0, 0)) for w in widths],
        out_shape=[jax.ShapeDtypeStruct(oshape(w), BF16) for w in widths],
        scratch_shapes=[pltpu.VMEM(oshape(w), F32) for w in widths],
        compiler_params=_cparams(1),
    )(n, *dp_pieces)


def _out_fwd(h, ya, yb, w, tm):
    tp, d = h.shape
    na, nb = ya.shape[1], yb.shape[1]

    def body(h_ref, ya_ref, yb_ref, w_ref, o_ref):
        y = _dot(ya_ref[...].astype(BF16), w_ref[0:na, :]) + _dot(yb_ref[...].astype(BF16), w_ref[na:, :])
        o_ref[...] = h_ref[...] + y

    return pl.pallas_call(
        body, name="out_fwd",
        grid=(tp // tm,),
        in_specs=[pl.BlockSpec((tm, d), lambda i: (i, 0)), pl.BlockSpec((tm, na), lambda i: (i, 0)),
                  pl.BlockSpec((tm, nb), lambda i: (i, 0)), pl.BlockSpec((d, d), lambda i: (0, 0))],
        out_specs=pl.BlockSpec((tm, d), lambda i: (i, 0)),
        out_shape=jax.ShapeDtypeStruct((tp, d), F32),
        compiler_params=_cparams(1),
    )(h, ya, yb, w)


def _out_bwd(dy, ya, yb, w, tm, after=()):
    tp, d = dy.shape
    na, nb = ya.shape[1], yb.shape[1]

    def body(dy_ref, ya_ref, yb_ref, w_ref, *rest):
        da_ref, db_ref, dw_ref, acc = rest[len(after):]
        i = pl.program_id(0)

        @pl.when(i == 0)
        def _():
            acc[...] = jnp.zeros_like(acc)

        dyb16 = dy_ref[...].astype(BF16)
        da_ref[...] = _dot_nt(dyb16, w_ref[0:na, :])
        db_ref[...] = _dot_nt(dyb16, w_ref[na:, :])
        acc[0:na, :] += _dot_tn(ya_ref[...].astype(BF16), dyb16)
        acc[na:, :] += _dot_tn(yb_ref[...].astype(BF16), dyb16)

        @pl.when(i == pl.num_programs(0) - 1)
        def _():
            dw_ref[...] = acc[...].astype(BF16)

    return pl.pallas_call(
        body, name="out_bwd",
        grid=(tp // tm,),
        in_specs=[pl.BlockSpec((tm, d), lambda i: (i, 0)), pl.BlockSpec((tm, na), lambda i: (i, 0)),
                  pl.BlockSpec((tm, nb), lambda i: (i, 0)), pl.BlockSpec((d, d), lambda i: (0, 0))]
        + [pl.BlockSpec(memory_space=pl.ANY)] * len(after),
        out_specs=[pl.BlockSpec((tm, na), lambda i: (i, 0)), pl.BlockSpec((tm, nb), lambda i: (i, 0)),
                   pl.BlockSpec((d, d), lambda i: (0, 0))],
        out_shape=[jax.ShapeDtypeStruct((tp, na), F32), jax.ShapeDtypeStruct((tp, nb), F32),
                   jax.ShapeDtypeStruct((d, d), BF16)],
        scratch_shapes=[pltpu.VMEM((d, d), F32)],
        compiler_params=_cparams(1),
    )(dy, ya, yb, w, *after)


def _loss_bwd(h, gfin, tgt, t_real, tm):
    tp, d = h.shape

    def body(h_ref, g_ref, t_ref, loss_ref, dh_ref, dg_ref):
        i = pl.program_id(0)

        @pl.when(i == 0)
        def _():
            loss_ref[...] = jnp.zeros_like(loss_ref)
            dg_ref[...] = jnp.zeros_like(dg_ref)

        rows = i * tm + lax.broadcasted_iota(jnp.int32, (tm, 1), 0)
        valid = (rows >= N_META) & (rows < t_real)
        rstd, xhat = _rms_stats(h_ref[...])
        g = g_ref[...]
        err = jnp.where(valid, xhat * g - t_ref[...], 0.0)
        e2 = jnp.sum(err * err, axis=1, keepdims=True)
        loss_ref[...] += (0.5 / d) * jnp.sum(e2, axis=0, keepdims=True)
        dy = err * (1.0 / d)
        dh, dg = _rms_bwd(dy, g, rstd, xhat)
        dh_ref[...] = dh
        dg_ref[...] += dg

    return pl.pallas_call(
        body, name="loss_bwd",
        grid=(tp // tm,),
        in_specs=[pl.BlockSpec((tm, d), lambda i: (i, 0)), pl.BlockSpec((1, d), lambda i: (0, 0)),
                  pl.BlockSpec((tm, d), lambda i: (i, 0))],
        out_specs=[pl.BlockSpec((1, 1), lambda i: (0, 0)), pl.BlockSpec((tm, d), lambda i: (i, 0)),
                   pl.BlockSpec((1, d), lambda i: (0, 0))],
        out_shape=[jax.ShapeDtypeStruct((1, 1), F32), jax.ShapeDtypeStruct((tp, d), F32),
                   jax.ShapeDtypeStruct((1, d), F32)],
        compiler_params=_cparams(1),
    )(h, gfin, tgt)


POOL_HALO = 16


def _pool_lane_consts(n_rows):
    lane = lax.broadcasted_iota(jnp.int32, (n_rows, D_POOL), 1)
    grp = lane // POOL_GROUP
    win = jnp.where(grp == 0, 2.0, jnp.where(grp == 1, 4.0, jnp.where(grp == 2, 8.0, 16.0)))
    return grp, win


def _pool_select(grp, s2, s4, s8, s16):
    return jnp.where(grp == 0, s2, jnp.where(grp == 1, s4, jnp.where(grp == 2, s8, s16)))


def _pool_mixed(x, row0, tr):
    n = tr + POOL_HALO
    s2 = x + pltpu.roll(x, 1, 0)
    s4 = s2 + pltpu.roll(s2, 2, 0)
    s8 = s4 + pltpu.roll(s4, 4, 0)
    s16 = s8 + pltpu.roll(s8, 8, 0)
    grp, win = _pool_lane_consts(n)
    rows = row0 - POOL_HALO + lax.broadcasted_iota(jnp.int32, (n, D_POOL), 0)
    cnt = jnp.minimum((rows + 1).astype(F32), win)
    pooled = _pool_select(grp, s2, s4, s8, s16) / jnp.maximum(cnt, 1.0)
    return (pooled - x)[POOL_HALO:, :]


def _pool_fwd(p, wbd, scale, tr):
    tp = p.shape[0]
    nt = tp // tr

    def body(p_ref, w_ref, s_ref, y_ref, usc):
        usc[0:POOL_HALO, :] = jnp.zeros((POOL_HALO, D_POOL), F32)
        usc[POOL_HALO:, :] = p_ref[...]

        def tile(r, carry):
            r0 = pl.multiple_of(r * tr, SUBLANE)
            x = usc[pl.ds(r0, tr + POOL_HALO), :]
            mixed = _pool_mixed(x, r0, tr)
            y_ref[pl.ds(r0, tr), :] = _dot(mixed.astype(BF16), w_ref[...]) * s_ref[...]
            return carry

        lax.fori_loop(0, nt, tile, 0)

    return pl.pallas_call(
        body, name="pool_fwd",
        grid=(1,),
        in_specs=[pl.BlockSpec((tp, D_POOL), lambda i: (0, 0)), pl.BlockSpec((D_POOL, D_POOL), lambda i: (0, 0)),
                  pl.BlockSpec((1, D_POOL), lambda i: (0, 0))],
        out_specs=pl.BlockSpec((tp, D_POOL), lambda i: (0, 0)),
        out_shape=jax.ShapeDtypeStruct((tp, D_POOL), F32),
        scratch_shapes=[pltpu.VMEM((tp + POOL_HALO, D_POOL), F32)],
        compiler_params=_cparams(1),
    )(p, wbd, scale)


def _pool_bwd(p, dya, wbd, scale, tr):
    tp = p.shape[0]
    nt = tp // tr

    def body(p_ref, dy_ref, w_ref, s_ref, du_ref, dw_ref, ds_ref, usc, gsc):
        usc[0:POOL_HALO, :] = jnp.zeros((POOL_HALO, D_POOL), F32)
        usc[POOL_HALO:, :] = p_ref[...]
        gsc[tp:, :] = jnp.zeros((POOL_HALO, D_POOL), F32)
        dw_ref[...] = jnp.zeros_like(dw_ref)
        ds_ref[...] = jnp.zeros_like(ds_ref)
        grp, win = _pool_lane_consts(tr)

        def tile1(r, carry):
            r0 = pl.multiple_of(r * tr, SUBLANE)
            x = usc[pl.ds(r0, tr + POOL_HALO), :]
            mixed = _pool_mixed(x, r0, tr).astype(BF16)
            dy = dy_ref[pl.ds(r0, tr), :]
            dys = (dy * s_ref[...]).astype(BF16)
            ypre = _dot(mixed, w_ref[...])
            ds_ref[...] += _colsum(dy * ypre)
            dw_ref[...] += _dot_tn(mixed, dys)
            dmx = _dot_nt(dys, w_ref[...])
            rows = r0 + lax.broadcasted_iota(jnp.int32, (tr, D_POOL), 0)
            cnt = jnp.minimum((rows + 1).astype(F32), win)
            gsc[pl.ds(r0, tr), :] = dmx / cnt
            return carry

        lax.fori_loop(0, nt, tile1, 0)
        n = tr + POOL_HALO
        grp2, win2 = _pool_lane_consts(n)

        def tile2(r, carry):
            r0 = pl.multiple_of(r * tr, SUBLANE)
            g = gsc[pl.ds(r0, n), :]
            s2 = g + pltpu.roll(g, n - 1, 0)
            s4 = s2 + pltpu.roll(s2, n - 2, 0)
            s8 = s4 + pltpu.roll(s4, n - 4, 0)
            s16 = s8 + pltpu.roll(s8, n - 8, 0)
            pooled_t = _pool_select(grp2, s2, s4, s8, s16)
            rows = r0 + lax.broadcasted_iota(jnp.int32, (n, D_POOL), 0)
            cnt = jnp.minimum((rows + 1).astype(F32), win2)
            du = pooled_t - g * cnt
            du_ref[pl.ds(r0, tr), :] = du[0:tr, :].astype(BF16)
            return carry

        lax.fori_loop(0, nt, tile2, 0)

    return pl.pallas_call(
        body, name="pool_bwd",
        grid=(1,),
        in_specs=[pl.BlockSpec((tp, D_POOL), lambda i: (0, 0)), pl.BlockSpec((tp, D_POOL), lambda i: (0, 0)),
                  pl.BlockSpec((D_POOL, D_POOL), lambda i: (0, 0)), pl.BlockSpec((1, D_POOL), lambda i: (0, 0))],
        out_specs=[pl.BlockSpec((tp, D_POOL), lambda i: (0, 0)), pl.BlockSpec((D_POOL, D_POOL), lambda i: (0, 0)),
                   pl.BlockSpec((1, D_POOL), lambda i: (0, 0))],
        out_shape=[jax.ShapeDtypeStruct((tp, D_POOL), BF16), jax.ShapeDtypeStruct((D_POOL, D_POOL), F32),
                   jax.ShapeDtypeStruct((1, D_POOL), F32)],
        scratch_shapes=[pltpu.VMEM((tp + POOL_HALO, D_POOL), F32), pltpu.VMEM((tp + POOL_HALO, D_POOL), F32)],
        compiler_params=_cparams(1),
    )(p, dya, wbd, scale)


def _hgrn_levels(ch):
    levels = []
    w = ch // 2
    while w >= 1:
        levels.append(w)
        w //= 2
    return levels


def _hgrn_consts(ch):
    t = np.arange(ch)
    tril = t[None, :] <= t[:, None]
    masks = []
    for w in _hgrn_levels(ch):
        blk = t // (2 * w)
        upper = t % (2 * w) >= w
        masks.append(upper[:, None] & (~upper)[None, :] & (blk[:, None] == blk[None, :]))
    masks.append(tril)
    msk = np.stack(masks).astype(np.float32)
    return jnp.asarray(tril.astype(np.float32), BF16), jnp.asarray(msk, F32), len(masks) - 1


def _split3(x):
    hi = x.astype(BF16)
    r1 = x - hi.astype(F32)
    mid = r1.astype(BF16)
    lo = (r1 - mid.astype(F32)).astype(BF16)
    return hi, mid, lo


def _hgrn_exponents(tril, logf):
    ch = logf.shape[0]
    hi, mid, lo = _split3(logf)
    x = _dot(tril, jnp.concatenate([hi, mid, lo], axis=1))
    b = x[:, 0:HEAD] + x[:, HEAD:2 * HEAD] + x[:, 2 * HEAD:3 * HEAD]
    rows = lax.broadcasted_iota(jnp.int32, (ch, HEAD), 0)
    fx = jnp.broadcast_to(b[ch - 1:ch, :], (ch, HEAD)) - b
    lev = []
    for w in _hgrn_levels(ch):
        pos = rows % (2 * w)
        upper = pos >= w
        if w >= SUBLANE:
            parts = [jnp.broadcast_to(b[k * 2 * w + w - 1:k * 2 * w + w, :], (2 * w, HEAD))
                     for k in range(ch // (2 * w))]
            bmid = parts[0] if len(parts) == 1 else jnp.concatenate(parts, axis=0)
            dx = jnp.where(upper, b - bmid, 0.0)
            ex = jnp.where(upper, 0.0, bmid - b)
        else:
            dx = logf
            ex = jnp.zeros_like(logf)
            for i in range(1, w):
                dx = dx + jnp.where(pos >= w + i, pltpu.roll(logf, i, 0), 0.0)
                ex = ex + jnp.where(pos <= w - 1 - i, pltpu.roll(logf, ch - i, 0), 0.0)
            dx = jnp.where(upper, dx, 0.0)
        lev.append((dx, ex))
    return b, fx, lev


def _hgrn_exponents_bwd(tril, d_b, d_fx, d_blast, lev_grads):
    ch = d_b.shape[0]
    rows = lax.broadcasted_iota(jnp.int32, (ch, HEAD), 0)
    db = d_b - d_fx
    dlf = jnp.zeros_like(d_b)
    for w, (ddx, dex) in zip(_hgrn_levels(ch), lev_grads):
        pos = rows % (2 * w)
        upper = pos >= w
        gu = jnp.where(upper, ddx, 0.0)
        if w >= SUBLANE:
            gl = jnp.where(upper, 0.0, dex)
            db = db + gu - gl
            diff = gl - gu
            for k in range(ch // (2 * w)):
                s = _colsum(diff[k * 2 * w:(k + 1) * 2 * w, :])
                db = db + jnp.where(rows == k * 2 * w + w - 1, s, 0.0)
        else:
            dlf = dlf + gu
            for i in range(1, w):
                dlf = dlf + pltpu.roll(jnp.where(pos >= w + i, gu, 0.0), ch - i, 0)
                dlf = dlf + pltpu.roll(jnp.where(pos <= w - 1 - i, dex, 0.0), i, 0)
    db = db + jnp.where(rows == ch - 1, _colsum(d_fx) + d_blast, 0.0)
    hi = db.astype(BF16)
    lo = (db - hi.astype(F32)).astype(BF16)
    d2 = _dot_tn(tril, jnp.concatenate([hi, lo], axis=1))
    return d2[:, 0:HEAD] + d2[:, HEAD:2 * HEAD] + dlf


def _hgrn_gates(q_raw, z, lb):
    sz = _sigmoid(z)
    f = lb + (1.0 - lb) * sz
    q = q_raw * _sigmoid(q_raw)
    k = (1.0 - lb) * (1.0 - sz)
    return q, k, f, sz


def _hgrn_intra(q, k, lev, msk_ref, n_lev, ch):
    eye = (lax.broadcasted_iota(jnp.int32, (ch, ch), 0) == lax.broadcasted_iota(jnp.int32, (ch, ch), 1))
    a = jnp.where(eye, jnp.sum(q * k, axis=1, keepdims=True), 0.0)
    ops = []
    for lv in range(n_lev):
        eq = jnp.exp(lev[lv][0])
        ek = jnp.exp(lev[lv][1])
        qd = q * eq
        kd = k * ek
        a = a + msk_ref[lv] * _dot_nt(qd.astype(BF16), kd.astype(BF16))
        ops.append((eq, ek, qd, kd))
    return a, ops


def _hgrn_fwd(p, lb_logits, gnorm, mst, msk, n_lev, tm):
    tp = p.shape[0]
    ch = HG_CHUNK
    nct = tm // ch
    nt = tp // tm
    nr = mst.shape[0]
    base = D_POOL // HEAD

    def body(q_ref, z_ref, v_ref, g_ref, lg_ref, gn_ref, mst_ref, msk_ref, y_ref, ss_ref, st_sc):
        @pl.when(pl.program_id(1) == 0)
        def _():
            st_sc[...] = jnp.zeros_like(st_sc)

        lb_all = _sigmoid(lg_ref[0:1, :] - lg_ref[1:2, :])

        def one_head(hh, c, r0):
            ls = slice(hh * HEAD, (hh + 1) * HEAD)
            lb = lb_all[:, ls]
            q, k, f, _ = _hgrn_gates(q_ref[pl.ds(r0, ch), ls], z_ref[pl.ds(r0, ch), ls], lb)
            v = v_ref[pl.ds(r0, ch), ls]
            b, fx, lev = _hgrn_exponents(mst_ref[...], jnp.log(f))
            st = st_sc[hh]
            ss_ref[hh, c] = st
            qe = q * jnp.exp(b)
            a, _ = _hgrn_intra(q, k, lev, msk_ref, n_lev, ch)
            v16 = v.astype(BF16)
            o = _dot_nt(qe.astype(BF16), st.astype(BF16)) + _dot(a.astype(BF16), v16)
            kl = k * jnp.exp(fx)
            st_sc[hh] = st * jnp.exp(b[ch - 1:ch, :]) + _dot_tn(v16, kl.astype(BF16))
            rstd = lax.rsqrt(jnp.mean(o * o, axis=-1, keepdims=True) + EPS)
            g_raw = g_ref[pl.ds(r0, ch), ls]
            y_ref[pl.ds(r0, ch), ls] = o * rstd * gn_ref[...] * (g_raw * _sigmoid(g_raw))

        def chunk(c, carry):
            r0 = pl.multiple_of(c * ch, ch)
            for hh in range(HG_HEADS_PER_STEP):
                one_head(hh, c, r0)
            return carry

        lax.fori_loop(0, nct, chunk, 0, unroll=HG_UNROLL)

    hp = HG_HEADS_PER_STEP
    wide = hp * HEAD

    def pspec(seg):
        return pl.BlockSpec((tm, wide), lambda h, i: (i, (base + seg * HG_HEADS) // hp + h))

    return pl.pallas_call(
        body, name="hgrn_fwd",
        grid=(HG_HEADS // hp, nt),
        in_specs=[pspec(0), pspec(1), pspec(2), pspec(3),
                  pl.BlockSpec((2, wide), lambda h, i: (0, h)),
                  pl.BlockSpec((1, HEAD), lambda h, i: (0, 0)),
                  pl.BlockSpec((nr, ch), lambda h, i: (0, 0)),
                  pl.BlockSpec((n_lev + 1, ch, ch), lambda h, i: (0, 0, 0))],
        out_specs=[pl.BlockSpec((tm, wide), lambda h, i: (i, h)),
                   pl.BlockSpec((hp, nct, HEAD, HEAD), lambda h, i: (h, i, 0, 0))],
        out_shape=[jax.ShapeDtypeStruct((tp, D_HGRN), F32),
                   jax.ShapeDtypeStruct((HG_HEADS, tp // ch, HEAD, HEAD), F32)],
        scratch_shapes=[pltpu.VMEM((hp, HEAD, HEAD), F32)],
        compiler_params=_cparams(2),
    )(p, p, p, p, lb_logits, gnorm, mst, msk)


def _hgrn_bwd(p, dyb, states, lb_logits, gnorm, mst, msk, n_lev, tm):
    tp = p.shape[0]
    ch = HG_CHUNK
    nct = tm // ch
    nt = tp // tm
    nr = mst.shape[0]
    base = D_POOL // HEAD

    def body(q_ref, z_ref, v_ref, g_ref, dy_ref, ss_ref, lg_ref, gn_ref, mst_ref, msk_ref,
             dq_ref, dz_ref, dv_ref, dg_ref, dlg_ref, dgn_ref, dst_sc, dlb_sc):
        ti = pl.program_id(1)

        @pl.when(ti == 0)
        def _():
            dst_sc[...] = jnp.zeros_like(dst_sc)
            dlb_sc[...] = jnp.zeros_like(dlb_sc)
            dgn_ref[...] = jnp.zeros_like(dgn_ref)

        lb_all = _sigmoid(lg_ref[0:1, :] - lg_ref[1:2, :])
        gn = gn_ref[...]

        def load_head(hh, c, r0):
            ls = slice(hh * HEAD, (hh + 1) * HEAD)
            return (q_ref[pl.ds(r0, ch), ls], z_ref[pl.ds(r0, ch), ls], v_ref[pl.ds(r0, ch), ls],
                    g_ref[pl.ds(r0, ch), ls], dy_ref[pl.ds(r0, ch), ls], ss_ref[hh, c], dst_sc[hh])

        def store_head(hh, r0, res):
            ls = slice(hh * HEAD, (hh + 1) * HEAD)
            dq_raw, dz, dv, dg_raw, dgn, dst_new, dlb = res
            dq_ref[pl.ds(r0, ch), ls] = dq_raw
            dz_ref[pl.ds(r0, ch), ls] = dz
            dv_ref[pl.ds(r0, ch), ls] = dv
            dg_ref[pl.ds(r0, ch), ls] = dg_raw
            dgn_ref[hh] += dgn
            dst_sc[hh] = dst_new
            dlb_sc[:, ls] += dlb

        def one_head(hh, loaded):
            ls = slice(hh * HEAD, (hh + 1) * HEAD)
            lb = lb_all[:, ls]
            q_raw, z, v, g_raw, dy, st, dst = loaded
            q, k, f, sz = _hgrn_gates(q_raw, z, lb)
            b, fx, lev = _hgrn_exponents(mst_ref[...], jnp.log(f))
            eb = jnp.exp(b)
            ef = jnp.exp(fx)
            elast = jnp.exp(b[ch - 1:ch, :])
            qe = q * eb
            kl = k * ef
            a, ops = _hgrn_intra(q, k, lev, msk_ref, n_lev, ch)
            v16 = v.astype(BF16)
            st16 = st.astype(BF16)
            qe16 = qe.astype(BF16)
            kl16 = kl.astype(BF16)
            a16 = a.astype(BF16)
            o = _dot_nt(qe16, st16) + _dot(a16, v16)
            sg = _sigmoid(g_raw)
            rstd = lax.rsqrt(jnp.mean(o * o, axis=-1, keepdims=True) + EPS)
            oh = o * rstd
            dg_out = (dy * oh * gn * (sg * (1.0 + g_raw * (1.0 - sg)))).astype(BF16)
            don = dy * (g_raw * sg)
            dgn = _colsum(don * oh)
            doh = don * gn
            do = rstd * (doh - oh * jnp.mean(doh * oh, axis=-1, keepdims=True))
            do16 = do.astype(BF16)
            dst16 = dst.astype(BF16)
            dv = _dot_tn(a16, do16) + _dot_nt(kl16, dst16)
            da = msk_ref[n_lev] * _dot_nt(do16, v16)
            dqe = _dot(do16, st16)
            dkl = _dot(v16, dst16)
            dst_new = dst * elast + _dot_tn(do16, qe16)
            db_last = _colsum(dst * st) * elast
            dad = jnp.sum(do * v, axis=1, keepdims=True)
            dq = dad * k + dqe * eb
            dk = dad * q + dkl * ef
            lev_grads = []
            for lv in range(n_lev):
                eq, ek, qd, kd = ops[lv]
                gl = (msk_ref[lv] * da).astype(BF16)
                dqd = _dot(gl, kd.astype(BF16))
                dkd = _dot_tn(gl, qd.astype(BF16))
                dq = dq + dqd * eq
                dk = dk + dkd * ek
                lev_grads.append((dqd * qd, dkd * kd))
            dlogf = _hgrn_exponents_bwd(mst_ref[...], dqe * qe, dkl * kl, db_last, lev_grads)
            sq = _sigmoid(q_raw)
            dq_out = (dq * (sq * (1.0 + q_raw * (1.0 - sq)))).astype(BF16)
            dfk = dlogf / f - dk
            dz_out = (dfk * (1.0 - lb) * sz * (1.0 - sz)).astype(BF16)
            return dq_out, dz_out, dv.astype(BF16), dg_out, dgn, dst_new, _colsum(dfk * (1.0 - sz))

        def chunk(cc, carry):
            c = nct - 1 - cc
            r0 = pl.multiple_of(c * ch, ch)
            loaded = [load_head(hh, c, r0) for hh in range(HG_HEADS_PER_STEP)]
            results = [one_head(hh, loaded[hh]) for hh in range(HG_HEADS_PER_STEP)]
            for hh in range(HG_HEADS_PER_STEP):
                store_head(hh, r0, results[hh])
            return carry

        lax.fori_loop(0, nct, chunk, 0, unroll=1)

        @pl.when(ti == nt - 1)
        def _():
            dl0 = dlb_sc[...] * lb_all * (1.0 - lb_all)
            dlg_ref[0:1, :] = dl0
            dlg_ref[1:2, :] = -dl0

    hp = HG_HEADS_PER_STEP
    wide = hp * HEAD

    def pspec(seg):
        return pl.BlockSpec((tm, wide), lambda h, i: (nt - 1 - i, (base + seg * HG_HEADS) // hp + h))

    ospec = pl.BlockSpec((tm, wide), lambda h, i: (nt - 1 - i, h))
    return pl.pallas_call(
        body, name="hgrn_bwd",
        grid=(HG_HEADS // hp, nt),
        in_specs=[pspec(0), pspec(1), pspec(2), pspec(3), ospec,
                  pl.BlockSpec((hp, nct, HEAD, HEAD), lambda h, i: (h, nt - 1 - i, 0, 0)),
                  pl.BlockSpec((2, wide), lambda h, i: (0, h)),
                  pl.BlockSpec((1, HEAD), lambda h, i: (0, 0)),
                  pl.BlockSpec((nr, ch), lambda h, i: (0, 0)),
                  pl.BlockSpec((n_lev + 1, ch, ch), lambda h, i: (0, 0, 0))],
        out_specs=[ospec, ospec, ospec, ospec,
                   pl.BlockSpec((2, wide), lambda h, i: (0, h)),
                   pl.BlockSpec((hp, 1, HEAD), lambda h, i: (h, 0, 0))],
        out_shape=[jax.ShapeDtypeStruct((tp, D_HGRN), BF16)] * 4
        + [jax.ShapeDtypeStruct((2, D_HGRN), F32), jax.ShapeDtypeStruct((HG_HEADS, 1, HEAD), F32)],
        scratch_shapes=[pltpu.VMEM((hp, HEAD, HEAD), F32), pltpu.VMEM((1, wide), F32)],
        compiler_params=_cparams(2),
    )(p, p, p, p, dyb, states, lb_logits, gnorm, mst, msk)


def _conv_taps(x, w_ref, tr, halo, width):
    acc = None
    for j in range(width):
        sh = width - 1 - j
        xs = x if sh == 0 else pltpu.roll(x, sh, 0)
        term = xs[halo:, :] * w_ref[j:j + 1, :]
        acc = term if acc is None else acc + term
    return acc


def _conv_taps_t(y, w_ref, tr, halo, width):
    n = tr + halo
    acc = None
    for j in range(width):
        sh = width - 1 - j
        ys = y if sh == 0 else pltpu.roll(y, n - sh, 0)
        term = ys[0:tr, :] * w_ref[j:j + 1, :]
        acc = term if acc is None else acc + term
    return acc


def _ln_stats(cv):
    mu = jnp.mean(cv, axis=-1, keepdims=True)
    xc = cv - mu
    rstd = lax.rsqrt(jnp.mean(xc * xc, axis=-1, keepdims=True) + EPS)
    return rstd, xc * rstd


def _convmod_fwd(p, w, bias, ln_g, ln_b, tr):
    tp = p.shape[0]
    nt = tp // tr
    nb = D_CONV // HEAD

    def body(a_ref, b_ref, w_ref, bi_ref, g_ref, be_ref, y_ref, usc):
        usc[0:CONV_HALO, :] = jnp.zeros((CONV_HALO, HEAD), F32)
        usc[CONV_HALO:, :] = a_ref[...] * _sigmoid(b_ref[...])

        def tile(r, carry):
            r0 = pl.multiple_of(r * tr, SUBLANE)
            x = usc[pl.ds(r0, tr + CONV_HALO), :]
            cv = _conv_taps(x, w_ref, tr, CONV_HALO, CONV_WIDTH) + bi_ref[...]
            _, xh = _ln_stats(cv)
            un = xh * g_ref[...] + be_ref[...]
            y_ref[pl.ds(r0, tr), :] = un * _sigmoid(un)
            return carry

        lax.fori_loop(0, nt, tile, 0)

    vec = lambda: pl.BlockSpec((1, HEAD), lambda j: (0, j))
    return pl.pallas_call(
        body, name="convmod_fwd",
        grid=(nb,),
        in_specs=[pl.BlockSpec((tp, HEAD), lambda j: (0, j)), pl.BlockSpec((tp, HEAD), lambda j: (0, nb + j)),
                  pl.BlockSpec((CONV_HALO, HEAD), lambda j: (0, j)), vec(), vec(), vec()],
        out_specs=pl.BlockSpec((tp, HEAD), lambda j: (0, j)),
        out_shape=jax.ShapeDtypeStruct((tp, D_CONV), F32),
        scratch_shapes=[pltpu.VMEM((tp + CONV_HALO, HEAD), F32)],
        compiler_params=_cparams(1),
    )(p, p, w, bias, ln_g, ln_b)


def _convmod_bwd(p, dyc, w, bias, ln_g, ln_b, tr):
    tp = p.shape[0]
    nt = tp // tr
    nb = D_CONV // HEAD

    def body(a_ref, b_ref, dy_ref, w_ref, bi_ref, g_ref, be_ref, da_ref, db_ref, dw_ref, dv_ref, usc, dsc):
        usc[0:CONV_HALO, :] = jnp.zeros((CONV_HALO, HEAD), F32)
        usc[CONV_HALO:, :] = a_ref[...] * _sigmoid(b_ref[...])
        dsc[tp:, :] = jnp.zeros((CONV_HALO, HEAD), F32)
        dw_ref[...] = jnp.zeros_like(dw_ref)
        dv_ref[...] = jnp.zeros_like(dv_ref)

        def tile1(r, carry):
            r0 = pl.multiple_of(r * tr, SUBLANE)
            x = usc[pl.ds(r0, tr + CONV_HALO), :]
            cv = _conv_taps(x, w_ref, tr, CONV_HALO, CONV_WIDTH) + bi_ref[...]
            rstd, xh = _ln_stats(cv)
            un = xh * g_ref[...] + be_ref[...]
            sg = _sigmoid(un)
            dun = dy_ref[pl.ds(r0, tr), :] * (sg * (1.0 + un * (1.0 - sg)))
            dv_ref[0, 1:2, :] += _colsum(dun * xh)
            dv_ref[0, 2:3, :] += _colsum(dun)
            dxh = dun * g_ref[...]
            dcv = rstd * (dxh - jnp.mean(dxh, axis=-1, keepdims=True)
                          - xh * jnp.mean(dxh * xh, axis=-1, keepdims=True))
            dv_ref[0, 0:1, :] += _colsum(dcv)
            for j in range(CONV_WIDTH):
                sh = CONV_WIDTH - 1 - j
                xs = x if sh == 0 else pltpu.roll(x, sh, 0)
                dw_ref[0, j:j + 1, :] += _colsum(dcv * xs[CONV_HALO:, :])
            dsc[pl.ds(r0, tr), :] = dcv
            return carry

        lax.fori_loop(0, nt, tile1, 0)

        def tile2(r, carry):
            r0 = pl.multiple_of(r * tr, SUBLANE)
            y = dsc[pl.ds(r0, tr + CONV_HALO), :]
            du = _conv_taps_t(y, w_ref, tr, CONV_HALO, CONV_WIDTH)
            a = a_ref[pl.ds(r0, tr), :]
            sb = _sigmoid(b_ref[pl.ds(r0, tr), :])
            da_ref[pl.ds(r0, tr), :] = (du * sb).astype(BF16)
            db_ref[pl.ds(r0, tr), :] = (du * a * sb * (1.0 - sb)).astype(BF16)
            return carry

        lax.fori_loop(0, nt, tile2, 0)

    vec = lambda: pl.BlockSpec((1, HEAD), lambda j: (0, j))
    col = lambda: pl.BlockSpec((tp, HEAD), lambda j: (0, j))
    return pl.pallas_call(
        body, name="convmod_bwd",
        grid=(nb,),
        in_specs=[col(), pl.BlockSpec((tp, HEAD), lambda j: (0, nb + j)), col(),
                  pl.BlockSpec((CONV_HALO, HEAD), lambda j: (0, j)), vec(), vec(), vec()],
        out_specs=[col(), col(), pl.BlockSpec((1, CONV_HALO, HEAD), lambda j: (j, 0, 0)),
                   pl.BlockSpec((1, SUBLANE, HEAD), lambda j: (j, 0, 0))],
        out_shape=[jax.ShapeDtypeStruct((tp, D_CONV), BF16), jax.ShapeDtypeStruct((tp, D_CONV), BF16),
                   jax.ShapeDtypeStruct((nb, CONV_HALO, HEAD), F32), jax.ShapeDtypeStruct((nb, SUBLANE, HEAD), F32)],
        scratch_shapes=[pltpu.VMEM((tp + CONV_HALO, HEAD), F32), pltpu.VMEM((tp + CONV_HALO, HEAD), F32)],
        compiler_params=_cparams(1),
    )(p, p, dyc, w, bias, ln_g, ln_b)


def _log1p_small(y):
    return jnp.where(y < 1e-4, y * (1.0 - 0.5 * y), jnp.log(1.0 + y))


def _softplus(x):
    return jnp.maximum(x, 0.0) + _log1p_small(jnp.exp(-jnp.abs(x)))


def _expm1(x):
    return jnp.where(jnp.abs(x) < 1e-2, x * (1.0 + 0.5 * x * (1.0 + x * (1.0 / 3.0))), jnp.exp(x) - 1.0)


def _gelu_parts(x):
    c = 0.7978845608028654
    inner = c * (x + 0.044715 * x * x * x)
    th = jnp.tanh(inner)
    gelu = 0.5 * x * (1.0 + th)
    dgelu = 0.5 * (1.0 + th) + 0.5 * x * (1.0 - th * th) * c * (1.0 + 3.0 * 0.044715 * x * x)
    return gelu, dgelu


def _lru_gates(x_all, tp, cw_ref, cb_ref, wa_ref, ba_ref, wx_ref, bx_ref, lam_ref):
    u = _conv_taps(x_all, cw_ref, tp, LRU_HALO, LRU_CONV) + cb_ref[...]
    u16 = u.astype(BF16)
    r = _sigmoid(_dot(u16, wa_ref[0]) + ba_ref[...])
    i = _sigmoid(_dot(u16, wx_ref[0]) + bx_ref[...])
    sp = _softplus(-lam_ref[...])
    la = -LRU_C * r * sp
    a = jnp.exp(la)
    mult = jnp.sqrt(-_expm1(2.0 * la))
    return u, r, i, a, mult, sp


def _lru_specs(tp, nb):
    col = lambda k: pl.BlockSpec((tp, HEAD), functools.partial(lambda j, k: (0, k * nb + j), k=k))
    vec = lambda: pl.BlockSpec((1, HEAD), lambda j: (0, j))
    mat = lambda: pl.BlockSpec((1, HEAD, HEAD), lambda j: (j, 0, 0))
    return col, vec, mat


def _lru_fwd(p, cw, cb, wa, ba, wx, bx, lam):
    tp = p.shape[0]
    nb = D_LRU // HEAD
    ng = tp // SUBLANE

    def body(x_ref, gt_ref, cw_ref, cb_ref, wa_ref, ba_ref, wx_ref, bx_ref, lam_ref, y_ref, hs_ref,
             xsc, asc, bsc):
        xsc[0:LRU_HALO, :] = jnp.zeros((LRU_HALO, HEAD), F32)
        xsc[LRU_HALO:, :] = x_ref[...]
        u, r, i, a, mult, _ = _lru_gates(xsc[...], tp, cw_ref, cb_ref, wa_ref, ba_ref, wx_ref, bx_ref, lam_ref)
        rows = lax.broadcasted_iota(jnp.int32, (tp, HEAD), 0)
        b = jnp.where(rows == 0, 1.0, mult) * (i * u)
        sub = rows % SUBLANE
        for k in (1, 2, 4):
            m = sub >= k
            b = jnp.where(m, a * pltpu.roll(b, k, 0) + b, b)
            a = jnp.where(m, a * pltpu.roll(a, k, 0), a)
        asc[...] = a
        bsc[...] = b

        def grp(g, carry):
            r0 = pl.multiple_of(g * SUBLANE, SUBLANE)
            h = bsc[pl.ds(r0, SUBLANE), :] + asc[pl.ds(r0, SUBLANE), :] * carry
            hs_ref[pl.ds(r0, SUBLANE), :] = h
            return jnp.broadcast_to(h[SUBLANE - 1:SUBLANE, :], (SUBLANE, HEAD))

        lax.fori_loop(0, ng, grp, jnp.zeros((SUBLANE, HEAD), F32))
        gelu, _ = _gelu_parts(gt_ref[...])
        y_ref[...] = gelu * hs_ref[...]

    col, vec, mat = _lru_specs(tp, nb)
    return pl.pallas_call(
        body, name="lru_fwd",
        grid=(nb,),
        in_specs=[col(2), col(3), pl.BlockSpec((LRU_CONV, HEAD), lambda j: (0, j)), vec(), mat(), vec(), mat(),
                  vec(), vec()],
        out_specs=[pl.BlockSpec((tp, HEAD), lambda j: (0, j)), pl.BlockSpec((tp, HEAD), lambda j: (0, j))],
        out_shape=[jax.ShapeDtypeStruct((tp, D_LRU), F32), jax.ShapeDtypeStruct((tp, D_LRU), F32)],
        scratch_shapes=[pltpu.VMEM((tp + LRU_HALO, HEAD), F32), pltpu.VMEM((tp, HEAD), F32),
                        pltpu.VMEM((tp, HEAD), F32)],
        compiler_params=_cparams(1),
    )(p, p, cw, cb, wa, ba, wx, bx, lam)


def _lru_bwd(p, hs, dyd, cw, cb, wa, ba, wx, bx, lam):
    tp = p.shape[0]
    nb = D_LRU // HEAD
    ng = tp // SUBLANE

    def body(x_ref, gt_ref, hs_ref, dy_ref, cw_ref, cb_ref, wa_ref, ba_ref, wx_ref, bx_ref, lam_ref,
             dx_ref, dgt_ref, dwa_ref, dwx_ref, dv_ref, xsc, asc, bsc, gsc, dusc):
        xsc[0:LRU_HALO, :] = jnp.zeros((LRU_HALO, HEAD), F32)
        xsc[LRU_HALO:, :] = x_ref[...]
        x_all = xsc[...]
        u, r, i, a, mult, sp = _lru_gates(x_all, tp, cw_ref, cb_ref, wa_ref, ba_ref, wx_ref, bx_ref, lam_ref)
        rows = lax.broadcasted_iota(jnp.int32, (tp, HEAD), 0)
        hs = hs_ref[...]
        dy = dy_ref[...]
        gelu, dgelu = _gelu_parts(gt_ref[...])
        dgt_ref[...] = (dy * hs * dgelu).astype(BF16)
        bb = dy * gelu
        aa = jnp.where(rows == tp - 1, 0.0, pltpu.roll(a, tp - 1, 0))
        sub = rows % SUBLANE
        for k in (1, 2, 4):
            m = sub < SUBLANE - k
            bb = jnp.where(m, aa * pltpu.roll(bb, tp - k, 0) + bb, bb)
            aa = jnp.where(m, aa * pltpu.roll(aa, tp - k, 0), aa)
        asc[...] = aa
        bsc[...] = bb

        def grp(gi, carry):
            g = ng - 1 - gi
            r0 = pl.multiple_of(g * SUBLANE, SUBLANE)
            gg = bsc[pl.ds(r0, SUBLANE), :] + asc[pl.ds(r0, SUBLANE), :] * carry
            gsc[pl.ds(r0, SUBLANE), :] = gg
            return jnp.broadcast_to(gg[0:1, :], (SUBLANE, HEAD))

        lax.fori_loop(0, ng, grp, jnp.zeros((SUBLANE, HEAD), F32))
        g = gsc[...]
        first = rows == 0
        hprev = jnp.where(first, 0.0, pltpu.roll(hs, 1, 0))
        iu = i * u
        d_iu = g * jnp.where(first, 1.0, mult)
        dmult_term = jnp.where(first, 0.0, g * iu * (-(a * a) / mult))
        dla = g * hprev * a + dmult_term
        dr = dla * (-LRU_C) * sp
        dv_ref[0, 7:8, :] = _colsum(dla * (LRU_C * r) * _sigmoid(-lam_ref[...]))
        dpr = dr * r * (1.0 - r)
        dpi = d_iu * u * i * (1.0 - i)
        dv_ref[0, 5:6, :] = _colsum(dpr)
        dv_ref[0, 6:7, :] = _colsum(dpi)
        u16 = u.astype(BF16)
        dpr16 = dpr.astype(BF16)
        dpi16 = dpi.astype(BF16)
        dwa_ref[0] = _dot_tn(u16, dpr16)
        dwx_ref[0] = _dot_tn(u16, dpi16)
        du = d_iu * i + _dot_nt(dpr16, wa_ref[0]) + _dot_nt(dpi16, wx_ref[0])
        dv_ref[0, 4:5, :] = _colsum(du)
        for j in range(LRU_CONV):
            sh = LRU_CONV - 1 - j
            xs = x_all if sh == 0 else pltpu.roll(x_all, sh, 0)
            dv_ref[0, j:j + 1, :] = _colsum(du * xs[LRU_HALO:, :])
        dusc[0:tp, :] = du
        dusc[tp:, :] = jnp.zeros((LRU_HALO, HEAD), F32)
        dx_ref[...] = _conv_taps_t(dusc[...], cw_ref, tp, LRU_HALO, LRU_CONV).astype(BF16)

    col, vec, mat = _lru_specs(tp, nb)
    ocol = lambda: pl.BlockSpec((tp, HEAD), lambda j: (0, j))
    return pl.pallas_call(
        body, name="lru_bwd",
        grid=(nb,),
        in_specs=[col(2), col(3), ocol(), ocol(), pl.BlockSpec((LRU_CONV, HEAD), lambda j: (0, j)), vec(), mat(),
                  vec(), mat(), vec(), vec()],
        out_specs=[ocol(), ocol(), mat(), mat(), pl.BlockSpec((1, SUBLANE, HEAD), lambda j: (j, 0, 0))],
        out_shape=[jax.ShapeDtypeStruct((tp, D_LRU), BF16), jax.ShapeDtypeStruct((tp, D_LRU), BF16),
                   jax.ShapeDtypeStruct((nb, HEAD, HEAD), F32), jax.ShapeDtypeStruct((nb, HEAD, HEAD), F32),
                   jax.ShapeDtypeStruct((nb, SUBLANE, HEAD), F32)],
        scratch_shapes=[pltpu.VMEM((tp + LRU_HALO, HEAD), F32), pltpu.VMEM((tp, HEAD), F32),
                        pltpu.VMEM((tp, HEAD), F32), pltpu.VMEM((tp, HEAD), F32),
                        pltpu.VMEM((tp + LRU_HALO, HEAD), F32)],
        compiler_params=_cparams(1),
    )(p, p, hs, dyd, cw, cb, wa, ba, wx, bx, lam)


def _mesh_pos():
    return lax.axis_index("x"), lax.axis_index("y"), lax.axis_index("c")


def _other_chips(x, y):
    return [(1 - x, y), (x, 1 - y), (1 - x, 1 - y)]


ANY = pl.BlockSpec(memory_space=pl.ANY)


def _allgather_shards(arrs, split):
    n = len(arrs)

    def body(*refs):
        ins, outs = refs[:n], refs[n:2 * n]
        send1, recv1, send2, recv2, send3, recv3 = refs[2 * n:]
        x, y, c = _mesh_pos()
        chip = 2 * x + y
        sibling = (x, y, 1 - c)
        others = _other_chips(x, y)

        def rows(k, cc):
            half = arrs[k].shape[0] // 2
            return pl.ds(cc * half, half)

        def remote(src, dst, ssem, rsem, dev):
            return pltpu.make_async_remote_copy(src_ref=src, dst_ref=dst, send_sem=ssem, recv_sem=rsem,
                                                device_id=dev, device_id_type=MESH)

        started = [remote(ins[k], outs[k].at[chip], send3.at[k], recv3.at[k], sibling) for k in range(n)]
        for cp in started:
            cp.start()
        for k in range(n):
            for j, (ox, oy) in enumerate(others):
                if split[k]:
                    src, dst = ins[k].at[rows(k, c)], outs[k].at[chip, rows(k, c)]
                else:
                    src, dst = ins[k], outs[k].at[chip]
                cp = remote(src, dst, send1.at[3 * k + j], recv1.at[3 * k + j], (ox, oy, c))
                cp.start()
                started.append(cp)
        for j, (ox, oy) in enumerate(others):
            ochip = 2 * ox + oy
            for k in range(n):
                if split[k]:
                    blk = outs[k].at[ochip, rows(k, c)]
                    remote(blk, blk, send1.at[3 * k + j], recv1.at[3 * k + j], sibling).wait_recv()
                    cp = remote(blk, blk, send2.at[3 * k + j], recv2.at[3 * k + j], sibling)
                    cp.start()
                    started.append(cp)
                else:
                    blk = outs[k].at[ochip]
                    remote(blk, blk, send1.at[3 * k + j], recv1.at[3 * k + j], sibling).wait_recv()
        for j, (ox, oy) in enumerate(others):
            ochip = 2 * ox + oy
            for k in range(n):
                if split[k]:
                    blk = outs[k].at[ochip, rows(k, 1 - c)]
                    remote(blk, blk, send2.at[3 * k + j], recv2.at[3 * k + j], sibling).wait_recv()
        for k in range(n):
            blk = outs[k].at[chip]
            remote(blk, blk, send3.at[k], recv3.at[k], sibling).wait_recv()
        for cp in started:
            cp.wait_send()

    return pl.pallas_call(
        body, name="allgather_shards",
        in_specs=[ANY] * n, out_specs=[ANY] * n,
        out_shape=[jax.ShapeDtypeStruct((N_SHARD,) + a.shape, a.dtype) for a in arrs],
        scratch_shapes=[pltpu.SemaphoreType.DMA((3 * n,)), pltpu.SemaphoreType.DMA((3 * n,)),
                        pltpu.SemaphoreType.DMA((3 * n,)), pltpu.SemaphoreType.DMA((3 * n,)),
                        pltpu.SemaphoreType.DMA((n,)), pltpu.SemaphoreType.DMA((n,))],
    )(*arrs)


HBM = pl.BlockSpec(memory_space=pltpu.HBM)
SEM = pl.BlockSpec(memory_space=pltpu.SEMAPHORE)
DATAFLOW = pltpu.SideEffectType.DATAFLOW_SIDE_EFFECTING
N_PEERS = 4


def _in_hbm(a):
    return pltpu.with_memory_space_constraint(a, pltpu.HBM)


def _gather_peers(x, y, c):
    return [((ox, oy, c), 2 * ox + oy) for ox, oy in _other_chips(x, y)] + [((x, y, 1 - c), 2 * x + y)]


def _gather_refs(src, land, slot, c, split):
    if not split:
        return src, land.at[slot]
    half = src.shape[0] // 2
    return src.at[pl.ds(c * half, half)], land.at[slot, pl.ds(c * half, half)]


def _gather_start(arrs, split):
    n = len(arrs)

    def body(*refs):
        ins, lands = refs[:n], refs[n:2 * n]
        ssem, rsem = refs[2 * n:2 * n + 2]
        token = refs[-1]
        x, y, c = _mesh_pos()
        chip = 2 * x + y
        for k in range(n):
            for j, (dev, _) in enumerate(_gather_peers(x, y, c)):
                src, dst = _gather_refs(ins[k], lands[k], chip, c, split[k] and j < N_PEERS - 1)
                pltpu.make_async_remote_copy(
                    src_ref=src, dst_ref=dst, send_sem=ssem.at[N_PEERS * k + j],
                    recv_sem=rsem.at[N_PEERS * k + j], device_id=dev, device_id_type=MESH).start()
        token[...] = jnp.zeros_like(token)

    lands = [_in_hbm(lax.empty((N_SHARD,) + a.shape, a.dtype)) for a in arrs]
    out = pl.pallas_call(
        body, name="gather_start",
        in_specs=[HBM] * (2 * n),
        out_specs=[SEM, SEM] + [HBM] * (2 * n) + [pl.BlockSpec(memory_space=pltpu.VMEM)],
        out_shape=[pltpu.SemaphoreType.DMA((N_PEERS * n,)), pltpu.SemaphoreType.DMA((N_PEERS * n,))]
        + [pltpu.HBM(a.shape, a.dtype) for a in arrs]
        + [pltpu.HBM((N_SHARD,) + a.shape, a.dtype) for a in arrs]
        + [jax.ShapeDtypeStruct((SUBLANE, LANE), F32)],
        input_output_aliases={k: 2 + k for k in range(2 * n)},
        compiler_params=pltpu.CompilerParams(has_side_effects=DATAFLOW),
    )(*[_in_hbm(a) for a in arrs], *lands)
    return out[0], out[1], list(out[2:2 + n]), list(out[2 + n:2 + 2 * n]), out[-1]


def _gather_wait(ssem, rsem, srcs, lands, ks, after, split=False):
    n = len(ks)

    def body(*refs):
        ins, lnd = refs[:n], refs[n:2 * n]
        ssem_ref, rsem_ref = refs[2 * n:2 * n + 2]
        x, y, c = _mesh_pos()
        for i, k in enumerate(ks):
            for j, (dev, pchip) in enumerate(_gather_peers(x, y, c)):
                src, dst = _gather_refs(ins[i], lnd[i], pchip, c, split and j < N_PEERS - 1)
                cp = pltpu.make_async_remote_copy(
                    src_ref=src, dst_ref=dst, send_sem=ssem_ref.at[N_PEERS * k + j],
                    recv_sem=rsem_ref.at[N_PEERS * k + j], device_id=dev, device_id_type=MESH)
                cp.wait_send()
                cp.wait_recv()

    out = pl.pallas_call(
        body, name="gather_wait",
        in_specs=[HBM] * (2 * n) + [SEM, SEM] + [ANY] * len(after),
        out_specs=[HBM] * (2 * n),
        out_shape=[pltpu.HBM(a.shape, a.dtype) for a in srcs] + [pltpu.HBM(a.shape, a.dtype) for a in lands],
        input_output_aliases={k: k for k in range(2 * n)},
        compiler_params=pltpu.CompilerParams(has_side_effects=DATAFLOW),
    )(*srcs, *lands, ssem, rsem, *after)
    return list(out[n:])


def _pair_forward(lands):
    n = len(lands)

    def body(*refs):
        outs = refs[n:2 * n]
        ssem, rsem = refs[2 * n:]
        x, y, c = _mesh_pos()
        sibling = (x, y, 1 - c)
        cps = []
        for k in range(n):
            half = lands[k].shape[1] // 2
            for j, (ox, oy) in enumerate(_other_chips(x, y)):
                mine = outs[k].at[2 * ox + oy, pl.ds(c * half, half)]
                cp = pltpu.make_async_remote_copy(src_ref=mine, dst_ref=mine, send_sem=ssem.at[3 * k + j],
                                                  recv_sem=rsem.at[3 * k + j], device_id=sibling, device_id_type=MESH)
                cp.start()
                cps.append(cp)
        for k in range(n):
            half = lands[k].shape[1] // 2
            for j, (ox, oy) in enumerate(_other_chips(x, y)):
                theirs = outs[k].at[2 * ox + oy, pl.ds((1 - c) * half, half)]
                pltpu.make_async_remote_copy(src_ref=theirs, dst_ref=theirs, send_sem=ssem.at[3 * k + j],
                                             recv_sem=rsem.at[3 * k + j], device_id=sibling,
                                             device_id_type=MESH).wait_recv()
        for cp in cps:
            cp.wait_send()

    return pl.pallas_call(
        body, name="pair_forward",
        in_specs=[ANY] * n, out_specs=[ANY] * n,
        out_shape=[jax.ShapeDtypeStruct(a.shape, a.dtype) for a in lands],
        scratch_shapes=[pltpu.SemaphoreType.DMA((3 * n,)), pltpu.SemaphoreType.DMA((3 * n,))],
        input_output_aliases={k: k for k in range(n)},
    )(*lands)


N_SOURCES = 7


def _reduce_peers(x, y, c):
    peers = []
    for ox, oy in _other_chips(x, y):
        for rel in range(2):
            peers.append(((ox, oy, c + rel - 2 * c * rel), 2 * ox + oy))
    peers.append(((x, y, 1 - c), 2 * x + y))
    return peers


def _reduce_start(arrs, slots):
    n = len(arrs)

    def body(*refs):
        ins, lands = refs[:n], refs[n:2 * n]
        ssem, rsem = refs[2 * n:2 * n + 2]
        token = refs[-1]
        x, y, c = _mesh_pos()
        me = 2 * (2 * x + y) + c
        for k in range(n):
            half = arrs[k].shape[1] // 2
            for p, (dev, ochip) in enumerate(_reduce_peers(x, y, c)):
                pltpu.make_async_remote_copy(
                    src_ref=ins[k].at[ochip, pl.ds(dev[2] * half, half)], dst_ref=lands[k].at[me],
                    send_sem=ssem.at[N_SOURCES * k + p], recv_sem=rsem.at[N_SOURCES * k + p],
                    device_id=dev, device_id_type=MESH).start()
        token[...] = jnp.zeros_like(token)

    out = pl.pallas_call(
        body, name="reduce_start",
        in_specs=[HBM] * (2 * n),
        out_specs=[SEM, SEM] + [HBM] * (2 * n) + [pl.BlockSpec(memory_space=pltpu.VMEM)],
        out_shape=[pltpu.SemaphoreType.DMA((N_SOURCES * n,)), pltpu.SemaphoreType.DMA((N_SOURCES * n,))]
        + [pltpu.HBM(a.shape, a.dtype) for a in arrs] + [pltpu.HBM(a.shape, a.dtype) for a in slots]
        + [jax.ShapeDtypeStruct((SUBLANE, LANE), F32)],
        input_output_aliases={k: 2 + k for k in range(2 * n)},
        compiler_params=pltpu.CompilerParams(has_side_effects=DATAFLOW),
    )(*[_in_hbm(a) for a in arrs], *[_in_hbm(a) for a in slots])
    return out[0], out[1], list(out[2:2 + n]), list(out[2 + n:2 + 2 * n]), out[-1]


def _reduce_wait(ssem, rsem, arrs, slots, after):
    n = len(arrs)

    def body(*refs):
        ins, lnd = refs[:n], refs[n:2 * n]
        ssem_ref, rsem_ref = refs[2 * n:2 * n + 2]
        x, y, c = _mesh_pos()
        for k in range(n):
            half = arrs[k].shape[1] // 2
            for p, (dev, ochip) in enumerate(_reduce_peers(x, y, c)):
                cp = pltpu.make_async_remote_copy(
                    src_ref=ins[k].at[ochip, pl.ds(dev[2] * half, half)], dst_ref=lnd[k].at[2 * ochip + dev[2]],
                    send_sem=ssem_ref.at[N_SOURCES * k + p], recv_sem=rsem_ref.at[N_SOURCES * k + p],
                    device_id=dev, device_id_type=MESH)
                cp.wait_send()
                cp.wait_recv()

    out = pl.pallas_call(
        body, name="reduce_wait",
        in_specs=[HBM] * (2 * n) + [SEM, SEM] + [ANY] * len(after),
        out_specs=[HBM] * (2 * n),
        out_shape=[pltpu.HBM(a.shape, a.dtype) for a in arrs] + [pltpu.HBM(a.shape, a.dtype) for a in slots],
        input_output_aliases={k: k for k in range(2 * n)},
        compiler_params=pltpu.CompilerParams(has_side_effects=DATAFLOW),
    )(*arrs, *slots, ssem, rsem, *after)
    return list(out[n:])


def _own_part(arrs, chip, core, me):
    n = len(arrs)
    nb = GRAD_ROW_BLOCKS

    def body(chip_ref, core_ref, me_ref, *refs):
        for k in range(n):
            refs[n + k][...] = refs[k][...]

    def blk(a):
        return (1, a.shape[1] // 2 // nb, a.shape[2])

    grid_spec = pltpu.PrefetchScalarGridSpec(
        num_scalar_prefetch=3, grid=(nb,),
        in_specs=[pl.BlockSpec(blk(a), lambda i, ch, co, me: (ch[0], co[0] * nb + i, 0)) for a in arrs],
        out_specs=[pl.BlockSpec(blk(a), lambda i, ch, co, me: (me[0], i, 0)) for a in arrs])
    return pl.pallas_call(
        body, name="own_part", grid_spec=grid_spec,
        out_shape=[jax.ShapeDtypeStruct((N_DEV, a.shape[1] // 2, a.shape[2]), a.dtype) for a in arrs],
        compiler_params=_cparams(1),
    )(chip, core, me, *arrs)


def _sum_devices(arrs, core):
    n = len(arrs)
    nb = GRAD_ROW_BLOCKS

    def body(c_ref, *refs):
        for k in range(n):
            r = refs[k]
            acc = r[0].astype(F32)
            for dev in range(1, N_DEV):
                acc = acc + r[dev].astype(F32)
            refs[n + k][...] = acc

    grid_spec = pltpu.PrefetchScalarGridSpec(
        num_scalar_prefetch=1, grid=(nb,),
        in_specs=[pl.BlockSpec((N_DEV, a.shape[1] // nb, a.shape[2]), lambda i, c: (0, i, 0)) for a in arrs],
        out_specs=[pl.BlockSpec((a.shape[1] // nb, a.shape[2]), lambda i, c: (c[0] * nb + i, 0)) for a in arrs])
    return pl.pallas_call(
        body, name="sum_devices", grid_spec=grid_spec,
        out_shape=[jax.ShapeDtypeStruct((2 * a.shape[1], a.shape[2]), F32) for a in arrs],
        compiler_params=_cparams(1),
    )(core, *arrs)


def _small_own(v, me):
    m = v.shape[0]

    def body(me_ref, v_ref, o_ref):
        o_ref[0] = v_ref[...]

    grid_spec = pltpu.PrefetchScalarGridSpec(
        num_scalar_prefetch=1, grid=(1,),
        in_specs=[pl.BlockSpec((m, LANE), lambda i, me: (0, 0))],
        out_specs=pl.BlockSpec((1, m, LANE), lambda i, me: (me[0], 0, 0)))
    return pl.pallas_call(
        body, name="small_own", grid_spec=grid_spec,
        out_shape=jax.ShapeDtypeStruct((N_DEV, m, LANE), v.dtype),
        compiler_params=_cparams(1),
    )(me, v)


def _small_start(v, slots):
    def body(v_ref, land, ssem, rsem, v_thru, land_thru, token):
        del v_thru, land_thru
        x, y, c = _mesh_pos()
        me = 2 * (2 * x + y) + c
        for p, (dev, _) in enumerate(_reduce_peers(x, y, c)):
            pltpu.make_async_remote_copy(src_ref=v_ref, dst_ref=land.at[me], send_sem=ssem.at[p],
                                         recv_sem=rsem.at[p], device_id=dev, device_id_type=MESH).start()
        token[...] = jnp.zeros_like(token)

    out = pl.pallas_call(
        body, name="small_start",
        in_specs=[HBM, HBM],
        out_specs=[SEM, SEM, HBM, HBM, pl.BlockSpec(memory_space=pltpu.VMEM)],
        out_shape=[pltpu.SemaphoreType.DMA((N_SOURCES,)), pltpu.SemaphoreType.DMA((N_SOURCES,)),
                   pltpu.HBM(v.shape, v.dtype), pltpu.HBM(slots.shape, slots.dtype),
                   jax.ShapeDtypeStruct((SUBLANE, LANE), F32)],
        input_output_aliases={0: 2, 1: 3},
        compiler_params=pltpu.CompilerParams(has_side_effects=DATAFLOW),
    )(_in_hbm(v), _in_hbm(slots))
    return out


def _small_wait(ssem, rsem, v, slots, after):
    def body(*refs):
        v_ref, land, ssem_ref, rsem_ref = refs[:4]
        x, y, c = _mesh_pos()
        for p, (dev, ochip) in enumerate(_reduce_peers(x, y, c)):
            cp = pltpu.make_async_remote_copy(src_ref=v_ref, dst_ref=land.at[2 * ochip + dev[2]],
                                              send_sem=ssem_ref.at[p], recv_sem=rsem_ref.at[p],
                                              device_id=dev, device_id_type=MESH)
            cp.wait_send()
            cp.wait_recv()

    out = pl.pallas_call(
        body, name="small_wait",
        in_specs=[HBM, HBM, SEM, SEM] + [ANY] * len(after),
        out_specs=[HBM, HBM],
        out_shape=[pltpu.HBM(v.shape, v.dtype), pltpu.HBM(slots.shape, slots.dtype)],
        input_output_aliases={0: 0, 1: 1},
        compiler_params=pltpu.CompilerParams(has_side_effects=DATAFLOW),
    )(v, slots, ssem, rsem, *after)
    return out[1]


def _scatter_start(arrs, slots):
    n = len(arrs)

    def body(*refs):
        ins, lands = refs[:n], refs[n:2 * n]
        ssem, rsem = refs[2 * n:2 * n + 2]
        token = refs[-1]
        x, y, c = _mesh_pos()
        chip = 2 * x + y
        for k in range(n):
            for j, (ox, oy) in enumerate(_other_chips(x, y)):
                pltpu.make_async_remote_copy(
                    src_ref=ins[k].at[2 * ox + oy], dst_ref=lands[k].at[chip], send_sem=ssem.at[3 * k + j],
                    recv_sem=rsem.at[3 * k + j], device_id=(ox, oy, c), device_id_type=MESH).start()
        token[...] = jnp.zeros_like(token)

    out = pl.pallas_call(
        body, name="scatter_start",
        in_specs=[HBM] * (2 * n),
        out_specs=[SEM, SEM] + [HBM] * (2 * n) + [pl.BlockSpec(memory_space=pltpu.VMEM)],
        out_shape=[pltpu.SemaphoreType.DMA((3 * n,)), pltpu.SemaphoreType.DMA((3 * n,))]
        + [pltpu.HBM(a.shape, a.dtype) for a in arrs] + [pltpu.HBM(a.shape, a.dtype) for a in slots]
        + [jax.ShapeDtypeStruct((SUBLANE, LANE), F32)],
        input_output_aliases={k: 2 + k for k in range(2 * n)},
        compiler_params=pltpu.CompilerParams(has_side_effects=DATAFLOW),
    )(*[_in_hbm(a) for a in arrs], *[_in_hbm(a) for a in slots])
    return out[0], out[1], list(out[2:2 + n]), list(out[2 + n:2 + 2 * n]), out[-1]


def _scatter_wait(ssem, rsem, arrs, slots, after):
    n = len(arrs)

    def body(*refs):
        ins, lnd = refs[:n], refs[n:2 * n]
        ssem_ref, rsem_ref = refs[2 * n:2 * n + 2]
        x, y, c = _mesh_pos()
        for k in range(n):
            for j, (ox, oy) in enumerate(_other_chips(x, y)):
                ochip = 2 * ox + oy
                cp = pltpu.make_async_remote_copy(
                    src_ref=ins[k].at[ochip], dst_ref=lnd[k].at[ochip], send_sem=ssem_ref.at[3 * k + j],
                    recv_sem=rsem_ref.at[3 * k + j], device_id=(ox, oy, c), device_id_type=MESH)
                cp.wait_send()
                cp.wait_recv()

    out = pl.pallas_call(
        body, name="scatter_wait",
        in_specs=[HBM] * (2 * n) + [SEM, SEM] + [ANY] * len(after),
        out_specs=[HBM] * (2 * n),
        out_shape=[pltpu.HBM(a.shape, a.dtype) for a in arrs] + [pltpu.HBM(a.shape, a.dtype) for a in slots],
        input_output_aliases={k: k for k in range(2 * n)},
        compiler_params=pltpu.CompilerParams(has_side_effects=DATAFLOW),
    )(*arrs, *slots, ssem, rsem, *after)
    return list(out[n:])


def _pair_exchange_halves(arrs):
    n = len(arrs)

    def body(*refs):
        ins, outs = refs[:n], refs[n:2 * n]
        ssem, rsem = refs[2 * n:]
        x, y, c = _mesh_pos()
        cps = []
        for k in range(n):
            half = arrs[k].shape[1] // 2
            cp = pltpu.make_async_remote_copy(
                src_ref=ins[k].at[:, pl.ds((1 - c) * half, half)], dst_ref=outs[k],
                send_sem=ssem.at[k], recv_sem=rsem.at[k], device_id=(x, y, 1 - c), device_id_type=MESH)
            cp.start()
            cps.append(cp)
        for cp in cps:
            cp.wait()

    return pl.pallas_call(
        body, name="pair_exchange_halves",
        in_specs=[ANY] * n, out_specs=[ANY] * n,
        out_shape=[jax.ShapeDtypeStruct((a.shape[0], a.shape[1] // 2, a.shape[2]), a.dtype) for a in arrs],
        scratch_shapes=[pltpu.SemaphoreType.DMA((n,)), pltpu.SemaphoreType.DMA((n,))],
    )(*arrs)


GRAD_ROW_BLOCKS = 2


def _pair_add(arrs, recvd, core):
    n = len(arrs)
    nb = GRAD_ROW_BLOCKS

    def body(c_ref, *refs):
        for k in range(n):
            refs[2 * n + k][...] = (refs[k][...].astype(F32) + refs[n + k][...].astype(F32)).astype(BF16)

    def blk(a):
        return (1, a.shape[1] // 2 // nb, a.shape[2])

    grid_spec = pltpu.PrefetchScalarGridSpec(
        num_scalar_prefetch=1, grid=(N_SHARD, nb),
        in_specs=[pl.BlockSpec(blk(a), lambda s, i, c: (s, c[0] * nb + i, 0)) for a in arrs]
        + [pl.BlockSpec(blk(a), lambda s, i, c: (s, i, 0)) for a in arrs],
        out_specs=[pl.BlockSpec(blk(a), lambda s, i, c: (s, i, 0)) for a in arrs])
    return pl.pallas_call(
        body, name="pair_add", grid_spec=grid_spec,
        out_shape=[jax.ShapeDtypeStruct(r.shape, BF16) for r in recvd],
        compiler_params=_cparams(2),
    )(core, *arrs, *recvd)


def _own_slot(arrs, chip):
    n = len(arrs)
    nb = GRAD_ROW_BLOCKS

    def body(c_ref, *refs):
        for k in range(n):
            refs[n + k][...] = refs[k][...]

    def blk(a):
        return (1, a.shape[1] // nb, a.shape[2])

    grid_spec = pltpu.PrefetchScalarGridSpec(
        num_scalar_prefetch=1, grid=(nb,),
        in_specs=[pl.BlockSpec(blk(a), lambda i, c: (c[0], i, 0)) for a in arrs],
        out_specs=[pl.BlockSpec(blk(a), lambda i, c: (c[0], i, 0)) for a in arrs])
    return pl.pallas_call(
        body, name="own_slot", grid_spec=grid_spec,
        out_shape=[jax.ShapeDtypeStruct(a.shape, a.dtype) for a in arrs],
        compiler_params=_cparams(1),
    )(chip, *arrs)


def _scatter_to_owners(arrs, slots):
    n = len(arrs)

    def body(*refs):
        ins, outs = refs[:n], refs[2 * n:3 * n]
        ssem, rsem = refs[3 * n:]
        x, y, c = _mesh_pos()
        chip = 2 * x + y
        others = _other_chips(x, y)
        cps = []
        for k in range(n):
            for j, (ox, oy) in enumerate(others):
                cp = pltpu.make_async_remote_copy(
                    src_ref=ins[k].at[2 * ox + oy], dst_ref=outs[k].at[chip],
                    send_sem=ssem.at[3 * k + j], recv_sem=rsem.at[3 * k + j],
                    device_id=(ox, oy, c), device_id_type=MESH)
                cp.start()
                cps.append(cp)
        for k in range(n):
            for j, (ox, oy) in enumerate(others):
                blk = outs[k].at[2 * ox + oy]
                pltpu.make_async_remote_copy(
                    src_ref=blk, dst_ref=blk, send_sem=ssem.at[3 * k + j], recv_sem=rsem.at[3 * k + j],
                    device_id=(ox, oy, c), device_id_type=MESH).wait_recv()
        for cp in cps:
            cp.wait_send()

    return pl.pallas_call(
        body, name="scatter_to_owners",
        in_specs=[ANY] * (2 * n), out_specs=[ANY] * n,
        out_shape=[jax.ShapeDtypeStruct(a.shape, a.dtype) for a in arrs],
        scratch_shapes=[pltpu.SemaphoreType.DMA((3 * n,)), pltpu.SemaphoreType.DMA((3 * n,))],
        input_output_aliases={n + k: k for k in range(n)},
    )(*arrs, *slots)


def _sum_chips(arrs, core):
    n = len(arrs)
    nb = GRAD_ROW_BLOCKS

    def body(c_ref, *refs):
        for k in range(n):
            r = refs[k]
            refs[n + k][...] = ((r[0].astype(F32) + r[1].astype(F32)) + r[2].astype(F32)) + r[3].astype(F32)

    grid_spec = pltpu.PrefetchScalarGridSpec(
        num_scalar_prefetch=1, grid=(nb,),
        in_specs=[pl.BlockSpec((N_SHARD, a.shape[1] // nb, a.shape[2]), lambda i, c: (0, i, 0)) for a in arrs],
        out_specs=[pl.BlockSpec((a.shape[1] // nb, a.shape[2]), lambda i, c: (c[0] * nb + i, 0)) for a in arrs])
    return pl.pallas_call(
        body, name="sum_chips", grid_spec=grid_spec,
        out_shape=[jax.ShapeDtypeStruct((2 * a.shape[1], a.shape[2]), F32) for a in arrs],
        compiler_params=_cparams(1),
    )(core, *arrs)


def _pair_allgather_halves(arrs):
    n = len(arrs)

    def body(*refs):
        outs = refs[n:2 * n]
        ssem, rsem = refs[2 * n:]
        x, y, c = _mesh_pos()
        cps = []
        for k in range(n):
            h = arrs[k].shape[0] // 2
            mine = outs[k].at[pl.ds(c * h, h)]
            cp = pltpu.make_async_remote_copy(src_ref=mine, dst_ref=mine, send_sem=ssem.at[k],
                                              recv_sem=rsem.at[k], device_id=(x, y, 1 - c), device_id_type=MESH)
            cp.start()
            cps.append(cp)
        for k, cp in enumerate(cps):
            h = arrs[k].shape[0] // 2
            theirs = outs[k].at[pl.ds((1 - c) * h, h)]
            pltpu.make_async_remote_copy(src_ref=theirs, dst_ref=theirs, send_sem=ssem.at[k], recv_sem=rsem.at[k],
                                         device_id=(x, y, 1 - c), device_id_type=MESH).wait_recv()
            cp.wait_send()

    return pl.pallas_call(
        body, name="pair_allgather_halves",
        in_specs=[ANY] * n, out_specs=[ANY] * n,
        out_shape=[jax.ShapeDtypeStruct(a.shape, a.dtype) for a in arrs],
        scratch_shapes=[pltpu.SemaphoreType.DMA((n,)), pltpu.SemaphoreType.DMA((n,))],
        input_output_aliases={k: k for k in range(n)},
    )(*arrs)


N_DEV = 8


def _allgather_all(v):
    m_per, n = v.shape

    def body(x_ref, out_ref, send_sems, recv_sems, local_sem):
        x, y, c = _mesh_pos()
        me, sibling = (x, y, c), (x, y, 1 - c)
        chips = _other_chips(x, y)

        def rows(px, py, pc):
            return out_ref.at[pl.ds((4 * px + 2 * py + pc) * m_per, m_per), :]

        def copy(k, block, to, src=None):
            return pltpu.make_async_remote_copy(
                src_ref=rows(*block) if src is None else src, dst_ref=rows(*block),
                send_sem=send_sems.at[k], recv_sem=recv_sems.at[k], device_id=to, device_id_type=MESH)

        mine = pltpu.make_async_copy(x_ref, rows(*me), local_sem)
        mine.start()
        first = [copy(0, me, sibling, src=x_ref)]
        first += [copy(1 + j, me, (*chip, c), src=x_ref) for j, chip in enumerate(chips)]
        for cp in first:
            cp.start()
        passed = [copy(4 + j, (*chip, c), sibling) for j, chip in enumerate(chips)]
        for j, chip in enumerate(chips):
            copy(1 + j, (*chip, c), me).wait_recv()
            passed[j].start()
        copy(0, sibling, me).wait_recv()
        for j, chip in enumerate(chips):
            copy(4 + j, (*chip, 1 - c), me).wait_recv()
        for cp in first + passed:
            cp.wait_send()
        mine.wait()

    return pl.pallas_call(
        body, name="allgather_all",
        out_shape=jax.ShapeDtypeStruct((N_DEV * m_per, n), v.dtype),
        in_specs=[pl.BlockSpec(memory_space=pltpu.VMEM)],
        out_specs=pl.BlockSpec(memory_space=pltpu.VMEM),
        scratch_shapes=[pltpu.SemaphoreType.DMA((7,)), pltpu.SemaphoreType.DMA((7,)), pltpu.SemaphoreType.DMA],
        compiler_params=pltpu.CompilerParams(vmem_limit_bytes=VMEM_LIMIT_MB * 1024 * 1024),
    )(v)


def _adamw_math(w, g, m, v):
    m2 = ADAM_B1 * m + (1.0 - ADAM_B1) * g
    v2 = ADAM_B2 * v + (1.0 - ADAM_B2) * (g * g)
    m_hat = m2 / (1.0 - ADAM_B1 ** ADAM_STEP)
    v_hat = v2 / (1.0 - ADAM_B2 ** ADAM_STEP)
    delta = -ADAM_LR * (m_hat / (jnp.sqrt(v_hat) + ADAM_EPS) + ADAM_WD * w)
    return delta, m2, v2


def _adamw(w, m, v, gs, nblk):
    nl, r, n = w.shape
    assert nl == len(gs) and nl in (1, 2)
    br = r // nblk

    def body(w_ref, m_ref, v_ref, *rest):
        g_refs, (go_ref, d_ref, mo_ref, vo_ref) = rest[:nl], rest[nl:]
        g = g_refs[0][...]
        if nl == 2:
            g = jnp.where(pl.program_id(0) == 0, g, g_refs[1][...])
        delta, m2, v2 = _adamw_math(w_ref[0], g, m_ref[0], v_ref[0])
        go_ref[0] = g
        d_ref[0] = delta
        mo_ref[0] = m2
        vo_ref[0] = v2

    spec = pl.BlockSpec((1, br, n), lambda l, i: (l, i, 0))
    g_specs = [pl.BlockSpec((br, n), lambda l, i: (i, 0))] if nl == 1 else [
        pl.BlockSpec((br, n), lambda l, i: (jnp.where(l == 0, i, nblk - 1), 0)),
        pl.BlockSpec((br, n), lambda l, i: (jnp.where(l == 1, i, 0), 0))]
    return pl.pallas_call(
        body, name="adamw", grid=(nl, nblk),
        in_specs=[spec, spec, spec] + g_specs,
        out_specs=[spec] * 4,
        out_shape=[jax.ShapeDtypeStruct((nl, r, n), F32)] * 4,
        compiler_params=_cparams(2),
    )(w, m, v, *gs)


def _small_reduce_adamw(parts, w, m, v, rep_rows, sh_rows):
    mrows = rep_rows + N_SHARD * sh_rows

    def body(p_ref, w_ref, m_ref, v_ref, go_ref, d_ref, mo_ref, vo_ref):
        x, y, _ = _mesh_pos()
        mine = rep_rows + (2 * x + y) * sh_rows
        g_rep = p_ref[0:rep_rows, :]
        g_sh = p_ref[pl.ds(pl.multiple_of(mine, SUBLANE), sh_rows), :]
        for k in range(1, N_DEV):
            g_rep = g_rep + p_ref[k * mrows:k * mrows + rep_rows, :]
            g_sh = g_sh + p_ref[pl.ds(pl.multiple_of(k * mrows + mine, SUBLANE), sh_rows), :]
        g = jnp.concatenate([g_rep, g_sh], axis=0)
        delta, m2, v2 = _adamw_math(w_ref[...], g, m_ref[...], v_ref[...])
        go_ref[...] = g
        d_ref[...] = delta
        mo_ref[...] = m2
        vo_ref[...] = v2

    return pl.pallas_call(
        body, name="small_reduce_adamw",
        out_shape=[jax.ShapeDtypeStruct((rep_rows + sh_rows, 128), F32)] * 4,
        compiler_params=pltpu.CompilerParams(vmem_limit_bytes=VMEM_LIMIT_MB * 1024 * 1024),
    )(parts, w, m, v)


LANE = 128
REP_SPEC = (("ffn1_norm", 16), ("mix_norm", 16), ("ffn2_norm", 16), ("final_norm", 8), ("pool_w", 128),
            ("pool_scale", 8), ("hgrn_lb_logits", 16), ("hgrn_gnorm", 8), ("lru_wa", 256), ("lru_wx", 256))
SH_SPEC = (("meta_tokens", 32), ("conv_w", 32), ("lru_conv_w", 8), ("conv_b", 8), ("conv_ln_g", 8),
           ("conv_ln_b", 8), ("lru_conv_b", 8), ("lru_ba", 8), ("lru_bx", 8), ("lru_lambda", 8))
REP_ROWS = sum(r for _, r in REP_SPEC)
SH_ROWS = sum(r for _, r in SH_SPEC)


def _pack_rows(vals, spec):
    parts = []
    for name, rows in spec:
        flat = vals[name].astype(F32).reshape(-1, LANE)
        if flat.shape[0] < rows:
            flat = jnp.concatenate([flat, jnp.zeros((rows - flat.shape[0], LANE), F32)], axis=0)
        parts.append(flat)
    return jnp.concatenate(parts, axis=0)


def _unpack_rows(packed, spec, shapes):
    out = {}
    off = 0
    for name, rows in spec:
        shp = shapes[name]
        n = int(np.prod(shp)) // LANE
        out[name] = packed[off:off + n].reshape(shp)
        off += rows
    return out


def _block_diag(blocks):
    n, b, _ = blocks.shape
    return sum(jnp.pad(blocks[g], ((g * b, (n - 1 - g) * b), (g * b, (n - 1 - g) * b))) for g in range(n))


def _diag_blocks(mat, n):
    b = mat.shape[0] // n
    return jnp.stack([mat[g * b:(g + 1) * b, g * b:(g + 1) * b] for g in range(n)])


BIG = ("ffn1_wg", "ffn1_wu", "ffn2_wg", "ffn2_wu", "ffn1_wd", "ffn2_wd", "w_in_even", "w_out_even",
       "w_in_odd", "w_out_odd")
ADAM_BLOCKS = {"ffn1_wg": 8, "ffn1_wu": 8, "ffn2_wg": 8, "ffn2_wu": 8, "ffn1_wd": 4, "ffn2_wd": 4,
               "w_in_even": 4, "w_out_even": 2, "w_in_odd": 4, "w_out_odd": 2}
WEIGHT_NAMES = ('meta_tokens', 'ffn1_norm', 'ffn1_wg', 'ffn1_wu', 'ffn1_wd', 'mix_norm', 'ffn2_norm', 'ffn2_wg',
                'ffn2_wu', 'ffn2_wd', 'w_in_even', 'pool_w', 'pool_scale', 'hgrn_lb_logits', 'hgrn_gnorm',
                'w_out_even', 'w_in_odd', 'conv_w', 'conv_b', 'conv_ln_g', 'conv_ln_b', 'lru_conv_w',
                'lru_conv_b', 'lru_wa', 'lru_ba', 'lru_wx', 'lru_bx', 'lru_lambda', 'w_out_odd', 'final_norm')


def _rows2d(a):
    return a.reshape(-1, a.shape[-1])


def _local_step_v2(x, tgt, w, gathered, small_full):
    s_len, d = x.shape
    t_real = s_len + N_META
    tp = -(-t_real // ROW_ALIGN) * ROW_ALIGN
    tm = _tile(tp, 832, ROW_ALIGN)
    tm_small = _tile(tp, 416, 16)
    tr = _tile(tp, 416, SUBLANE)
    f1 = ("ffn1_norm", "ffn1_wg", "ffn1_wu", "ffn1_wd")
    f2 = ("ffn2_norm", "ffn2_wg", "ffn2_wu", "ffn2_wd")

    meta_full = small_full["meta_tokens"]
    h0 = jnp.concatenate([meta_full, x, jnp.zeros((tp - t_real, d), F32)], axis=0)
    tgt_pad = jnp.concatenate([jnp.zeros((N_META, d), F32), tgt, jnp.zeros((tp - t_real, d), F32)], axis=0)

    w_in_even = jnp.transpose(gathered["w_in_even"], (1, 0, 2)).reshape(d, D_IN_EVEN)
    w_out_even = gathered["w_out_even"].reshape(d, d)
    w_out_odd = gathered["w_out_odd"].reshape(d, d)
    even_piece = [(w_in_even, (d, D_IN_EVEN), (0, 0))]
    odd_pieces = [(gathered["w_in_odd"], (1, d, D_IN_ODD // N_SHARD), (k, 0, 0)) for k in range(N_SHARD)]
    pool_wbd = _block_diag(w["pool_w"][0]).astype(BF16)
    pool_scale = w["pool_scale"]
    wa_bd = _block_diag2(w["lru_wa"][0]).astype(BF16)
    wx_bd = _block_diag2(w["lru_wx"][0]).astype(BF16)
    mst, msk, n_lev = _hgrn_consts(HG_CHUNK)
    conv_w = small_full["conv_w"]
    sf = small_full

    def gain(name, layer):
        return w[name][layer:layer + 1]

    def ffn(h, names, layer):
        return _ffn_fwd(h, gain(names[0], layer), gathered[names[1]], gathered[names[2]], gathered[names[3]],
                        layer, tm)

    h1, a1, b1, n1 = ffn(h0, f1, 0)
    p0, nm0 = _proj_fwd(h1, gain("mix_norm", 0), 0, even_piece, tm_small)
    ya = _pool_fwd(p0, pool_wbd, pool_scale, tr)
    yb, states = _hgrn_fwd(p0, w["hgrn_lb_logits"], w["hgrn_gnorm"], mst, msk, n_lev, tm)
    h2 = _out_fwd(h1, ya, yb, w_out_even, tm)
    h3, a2, b2, n2 = ffn(h2, f2, 0)
    h4, a3, b3, n3 = ffn(h3, f1, 1)
    p1, nm1 = _proj_fwd(h4, gain("mix_norm", 1), 1, odd_pieces, tm_small)
    yc = _convmod_fwd(p1, conv_w, sf["conv_b"], sf["conv_ln_g"], sf["conv_ln_b"], tr)
    lru_args = (sf["lru_conv_w"], sf["lru_conv_b"], wa_bd, sf["lru_ba"], wx_bd, sf["lru_bx"], sf["lru_lambda"])
    yd, hs = _lru_fwd(p1, *lru_args)
    h5 = _out_fwd(h4, yc, yd, w_out_odd, tm)
    h6, a4, b4, n4 = ffn(h5, f2, 1)
    loss, dh6, dg_final = _loss_bwd(h6, w["final_norm"].reshape(1, d), tgt_pad, t_real, tm)

    def ffn_bwd(dho, h, n, a, b, names, layer, acc):
        dh, da, db, dg = _ffn_bwd_act(dho, h, gain(names[0], layer), a, b, gathered[names[1]], gathered[names[2]],
                                      gathered[names[3]], layer, tm_small)
        acc = _ffn_bwd_w(dho, n, a, b, da, db, acc[0], acc[1], acc[2], layer, tm)
        return dh, dg, acc

    none3 = (None, None, None)
    dh5, dg_f2_l1, g_f2 = ffn_bwd(dh6, h5, n4, a4, b4, f2, 1, none3)
    dyc, dyd, dw_out_odd = _out_bwd(dh5, yc, yd, w_out_odd, tm)
    dca, dcb, dconv_w, dconv_vec = _convmod_bwd(p1, dyc, conv_w, sf["conv_b"], sf["conv_ln_g"], sf["conv_ln_b"], tr)
    dlx, dlg, dwa_bd, dwx_bd, dlru_vec = _lru_bwd(p1, hs, dyd, *lru_args)
    dp1 = [dca, dcb, dlx, dlg]
    dh4, dg_mix_l1 = _proj_bwd_act(dh5, h4, gain("mix_norm", 1), 1, dp1, odd_pieces, tm_small)
    dw_in_odd = jnp.stack(_proj_bwd_w(nm1, dp1, tm))
    dh3, dg_f1_l1, g_f1 = ffn_bwd(dh4, h3, n3, a3, b3, f1, 1, none3)
    dh2, dg_f2_l0, g_f2 = ffn_bwd(dh3, h2, n2, a2, b2, f2, 0, g_f2)
    dya, dyb, dw_out_even = _out_bwd(dh2, ya, yb, w_out_even, tm)
    dpool, dpool_wbd, dpool_scale = _pool_bwd(p0, dya, pool_wbd, pool_scale, tr)
    dq, dz, dv, dgate, dlb_logits, dgn_heads = _hgrn_bwd(p0, dyb, states, w["hgrn_lb_logits"], w["hgrn_gnorm"],
                                                         mst, msk, n_lev, tm)
    dp0 = [jnp.concatenate([dpool, dq, dz, dv, dgate], axis=1)]
    dh1, dg_mix_l0 = _proj_bwd_act(dh2, h1, gain("mix_norm", 0), 0, dp0, even_piece, tm_small)
    (dw_in_even,) = _proj_bwd_w(nm0, dp0, tm_small)
    dh0, dg_f1_l0, g_f1 = ffn_bwd(dh1, h0, n1, a1, b1, f1, 0, g_f1)

    grad_x = dh0[N_META:t_real]
    big = {
        "ffn1_wg": g_f1[0], "ffn1_wu": g_f1[1], "ffn1_wd": g_f1[2],
        "ffn2_wg": g_f2[0], "ffn2_wu": g_f2[1], "ffn2_wd": g_f2[2],
        "w_in_even": jnp.transpose(dw_in_even.reshape(d, N_SHARD, D_IN_EVEN // N_SHARD), (1, 0, 2)),
        "w_out_even": dw_out_even.reshape(N_SHARD, d // N_SHARD, d),
        "w_in_odd": dw_in_odd,
        "w_out_odd": dw_out_odd.reshape(N_SHARD, d // N_SHARD, d),
    }
    rep = {
        "ffn1_norm": jnp.concatenate([dg_f1_l0, dg_f1_l1], axis=0),
        "mix_norm": jnp.concatenate([dg_mix_l0, dg_mix_l1], axis=0),
        "ffn2_norm": jnp.concatenate([dg_f2_l0, dg_f2_l1], axis=0),
        "final_norm": dg_final,
        "pool_w": _diag_blocks(dpool_wbd, len(POOL_WINDOWS)),
        "pool_scale": dpool_scale,
        "hgrn_lb_logits": dlb_logits,
        "hgrn_gnorm": jnp.sum(dgn_heads, axis=0),
        "lru_wa": _diag_blocks2(dwa_bd),
        "lru_wx": _diag_blocks2(dwx_bd),
    }
    dmeta = jnp.transpose(dh0[:N_META].reshape(N_META, N_SHARD, 2, LANE), (1, 0, 2, 3)).reshape(N_SHARD, 32, LANE)
    packs = [_pack_rows(rep, REP_SPEC)]
    for s in range(N_SHARD):
        sh = {
            "meta_tokens": dmeta[s], "conv_w": dconv_w[s], "lru_conv_w": dlru_vec[s, 0:4],
            "conv_b": dconv_vec[s, 0:1], "conv_ln_g": dconv_vec[s, 1:2], "conv_ln_b": dconv_vec[s, 2:3],
            "lru_conv_b": dlru_vec[s, 4:5], "lru_ba": dlru_vec[s, 5:6], "lru_bx": dlru_vec[s, 6:7],
            "lru_lambda": dlru_vec[s, 7:8],
        }
        packs.append(_pack_rows(sh, SH_SPEC))
    return loss, grad_x, big, jnp.concatenate(packs, axis=0)


def _block_diag2(heads):
    nb = heads.shape[0] // 2
    return jnp.stack([_block_diag(heads[2 * j:2 * j + 2]) for j in range(nb)])


def _diag_blocks2(mats):
    return jnp.concatenate([_diag_blocks(mats[j], 2) for j in range(mats.shape[0])], axis=0)


def _kernel_v2(x, meta_tokens, ffn1_norm, ffn1_wg, ffn1_wu, ffn1_wd, mix_norm, ffn2_norm, ffn2_wg, ffn2_wu, ffn2_wd, w_in_even, pool_w, pool_scale, hgrn_lb_logits, hgrn_gnorm, w_out_even, w_in_odd, conv_w, conv_b, conv_ln_g, conv_ln_b, lru_conv_w, lru_conv_b, lru_wa, lru_ba, lru_wx, lru_bx, lru_lambda, w_out_odd, final_norm, loss_target, m_meta_tokens, m_ffn1_norm, m_ffn1_wg, m_ffn1_wu, m_ffn1_wd, m_mix_norm, m_ffn2_norm, m_ffn2_wg, m_ffn2_wu, m_ffn2_wd, m_w_in_even, m_pool_w, m_pool_scale, m_hgrn_lb_logits, m_hgrn_gnorm, m_w_out_even, m_w_in_odd, m_conv_w, m_conv_b, m_conv_ln_g, m_conv_ln_b, m_lru_conv_w, m_lru_conv_b, m_lru_wa, m_lru_ba, m_lru_wx, m_lru_bx, m_lru_lambda, m_w_out_odd, m_final_norm, v_meta_tokens, v_ffn1_norm, v_ffn1_wg, v_ffn1_wu, v_ffn1_wd, v_mix_norm, v_ffn2_norm, v_ffn2_wg, v_ffn2_wu, v_ffn2_wd, v_w_in_even, v_pool_w, v_pool_scale, v_hgrn_lb_logits, v_hgrn_gnorm, v_w_out_even, v_w_in_odd, v_conv_w, v_conv_b, v_conv_ln_g, v_conv_ln_b, v_lru_conv_w, v_lru_conv_b, v_lru_wa, v_lru_ba, v_lru_wx, v_lru_bx, v_lru_lambda, v_w_out_odd, v_final_norm):
    args = locals()
    w = {n: args[n] for n in WEIGHT_NAMES}
    m = {n: args["m_" + n] for n in WEIGHT_NAMES}
    v = {n: args["v_" + n] for n in WEIGHT_NAMES}
    shapes = {n: w[n].shape for n in WEIGHT_NAMES}

    big_in = [_rows2d(w[n]).astype(BF16) for n in BIG]
    small_sh = _pack_rows(w, SH_SPEC)
    gath = _allgather_shards(big_in + [small_sh], [True] * len(BIG) + [False])
    gathered = dict(zip(BIG, gath[:len(BIG)]))
    sm = gath[len(BIG)]
    sh_shapes = {n: (N_SHARD,) + tuple(shapes[n]) for n, _ in SH_SPEC}
    per_shard = [_unpack_rows(sm[s], SH_SPEC, shapes) for s in range(N_SHARD)]
    small_full = {}
    for n, _ in SH_SPEC:
        stacked = [per_shard[s][n] for s in range(N_SHARD)]
        small_full[n] = jnp.concatenate([p.reshape(-1, p.shape[-1]) for p in stacked], axis=-1)
    small_full["conv_w"] = jnp.concatenate(
        [small_full["conv_w"], jnp.zeros((CONV_HALO - CONV_WIDTH, D_CONV), F32)], axis=0)

    loss, grad_x, big, small_part = _local_step(x[0], loss_target[0], w, gathered, small_full)
    loss = lax.psum(loss[0, 0], ("x", "y", "c"))

    core = lax.axis_index("c").astype(jnp.int32).reshape(1)
    parts = [big[n] for n in BIG]
    recvd = _pair_exchange_halves(parts)
    pair = _pair_add(parts, recvd, core)
    chip = (2 * lax.axis_index("x") + lax.axis_index("y")).astype(jnp.int32).reshape(1)
    slots = _scatter_to_owners(pair, _own_slot(pair, chip))
    halves = _sum_chips(slots, core)
    full = _pair_allgather_halves(halves)
    out_g, out_d, out_m, out_v = {}, {}, {}, {}
    for n, g in zip(BIG, full):
        res = _adamw(_rows2d(w[n]), _rows2d(m[n]), _rows2d(v[n]), g, 0, ADAM_BLOCKS[n])
        out_g[n], out_d[n], out_m[n], out_v[n] = [r.reshape(shapes[n]) for r in res]

    gathered_small = _allgather_all(small_part)

    def pack_small(src):
        return jnp.concatenate([_pack_rows(src, REP_SPEC), _pack_rows(src, SH_SPEC)], axis=0)

    res = _small_reduce_adamw(gathered_small, pack_small(w), pack_small(m), pack_small(v), REP_ROWS, SH_ROWS)
    for dst, packed in zip((out_g, out_d, out_m, out_v), res):
        dst.update(_unpack_rows(packed[:REP_ROWS], REP_SPEC, shapes))
        dst.update(_unpack_rows(packed[REP_ROWS:], SH_SPEC, shapes))

    return (loss, grad_x[None], *[out_g[n] for n in WEIGHT_NAMES], *[out_d[n] for n in WEIGHT_NAMES],
            *[out_m[n] for n in WEIGHT_NAMES], *[out_v[n] for n in WEIGHT_NAMES])


GATHER_GROUPS = (
    (("small", 0),),
    (("ffn1_wg", 0), ("ffn1_wu", 0), ("ffn1_wd", 0)),
    (("w_in_even", 0), ("w_out_even", 0)),
    (("ffn2_wg", 0), ("ffn2_wu", 0), ("ffn2_wd", 0)),
    (("ffn1_wg", 1), ("ffn1_wu", 1), ("ffn1_wd", 1)),
    (("w_in_odd", 0), ("w_out_odd", 0)),
    (("ffn2_wg", 1), ("ffn2_wu", 1), ("ffn2_wd", 1)),
)
ADAM_ROW_BLOCKS = {"ffn1_wg": 2, "ffn1_wu": 2, "ffn2_wg": 2, "ffn2_wu": 2, "ffn1_wd": 2, "ffn2_wd": 2,
                   "w_in_even": 4, "w_out_even": 2, "w_in_odd": 4, "w_out_odd": 2}
TRANSPOSED = ("ffn1_wg", "ffn1_wu", "ffn2_wg", "ffn2_wu", "w_in_even")
SCATTER_DEPTH = 2


def _unpack_small(sm, shapes):
    per_shard = [_unpack_rows(sm[s], SH_SPEC, shapes) for s in range(N_SHARD)]
    full = {}
    for n, _ in SH_SPEC:
        full[n] = jnp.concatenate([per_shard[s][n].reshape(-1, shapes[n][-1]) for s in range(N_SHARD)], axis=-1)
    full["conv_w"] = jnp.concatenate([full["conv_w"], jnp.zeros((CONV_HALO - CONV_WIDTH, D_CONV), F32)], axis=0)
    return full


def _local_step(x, tgt, w, shapes, fetch, emit, emit_small):
    s_len, d = x.shape
    t_real = s_len + N_META
    tp = -(-t_real // ROW_ALIGN) * ROW_ALIGN
    tm = _tile(tp, 832, ROW_ALIGN)
    tm_small = _tile(tp, 416, 16)
    tr = _tile(tp, 416, SUBLANE)

    def gain(name, layer):
        return w[name][layer:layer + 1]

    pool_wbd = _block_diag(w["pool_w"][0]).astype(BF16)
    pool_scale = w["pool_scale"]
    wa_bd = _block_diag2(w["lru_wa"][0]).astype(BF16)
    wx_bd = _block_diag2(w["lru_wx"][0]).astype(BF16)
    mst, msk, n_lev = _hgrn_consts(HG_CHUNK)

    (sm,) = fetch(0, None)
    sf = _unpack_small(sm, shapes)
    h0 = jnp.concatenate([sf["meta_tokens"], x, jnp.zeros((tp - t_real, d), F32)], axis=0)
    tgt_pad = jnp.concatenate([jnp.zeros((N_META, d), F32), tgt, jnp.zeros((tp - t_real, d), F32)], axis=0)
    f1l0 = fetch(1, h0)
    h1, *s1 = _ffn_fwd(h0, gain("ffn1_norm", 0), *f1l0, 0, tm)
    w_in_even4, w_out_even4 = fetch(2, h1)
    w_out_even = w_out_even4.reshape(d, d)
    even_piece = [(w_in_even4.reshape(D_IN_EVEN, d), (D_IN_EVEN, d), (0, 0))]
    p0, nm0 = _proj_fwd(h1, gain("mix_norm", 0), 0, even_piece, tm_small, wt=True)
    ya = _pool_fwd(p0, pool_wbd, pool_scale, tr)
    yb, states = _hgrn_fwd(p0, w["hgrn_lb_logits"], w["hgrn_gnorm"], mst, msk, n_lev, tm)
    h2 = _out_fwd(h1, ya, yb, w_out_even, tm)
    f2l0 = fetch(3, h2)
    h3, *s2 = _ffn_fwd(h2, gain("ffn2_norm", 0), *f2l0, 0, tm)
    f1l1 = fetch(4, h3)
    h4, *s3 = _ffn_fwd(h3, gain("ffn1_norm", 1), *f1l1, 0, tm)
    w_in_odd4, w_out_odd4 = fetch(5, h4)
    w_out_odd = w_out_odd4.reshape(d, d)
    odd_pieces = [(w_in_odd4, (1, d, D_IN_ODD // N_SHARD), (k, 0, 0)) for k in range(N_SHARD)]
    p1, nm1 = _proj_fwd(h4, gain("mix_norm", 1), 1, odd_pieces, tm_small)
    yc = _convmod_fwd(p1, sf["conv_w"], sf["conv_b"], sf["conv_ln_g"], sf["conv_ln_b"], tr)
    lru_args = (sf["lru_conv_w"], sf["lru_conv_b"], wa_bd, sf["lru_ba"], wx_bd, sf["lru_bx"], sf["lru_lambda"])
    yd, hs = _lru_fwd(p1, *lru_args)
    h5 = _out_fwd(h4, yc, yd, w_out_odd, tm)
    f2l1 = fetch(6, h5)
    h6, *s4 = _ffn_fwd(h5, gain("ffn2_norm", 1), *f2l1, 0, tm)
    loss, dh6, dg_final = _loss_bwd(h6, w["final_norm"].reshape(1, d), tgt_pad, t_real, tm)

    def ffn_bwd(dho, h, saved, norm, wts, after=()):
        ga, gb, sa, n = saved
        dh, da, db, dg, dy = _ffn_bwd_act(dho, h, norm, ga, gb, *wts, 0, tm, after)
        return dh, dg, _ffn_bwd_w(dy, n, sa, da, db, tm)

    dh5, dg_f2_l1, g = ffn_bwd(dh6, h5, s4, gain("ffn2_norm", 1), f2l1)
    sent = emit((("ffn2_wg", 1), ("ffn2_wu", 1), ("ffn2_wd", 1)), g)
    dyc, dyd, dw_out_odd = _out_bwd(dh5, yc, yd, w_out_odd, tm, tuple(sent))
    dca, dcb, dconv_w, dconv_vec = _convmod_bwd(p1, dyc, sf["conv_w"], sf["conv_b"], sf["conv_ln_g"],
                                                sf["conv_ln_b"], tr)
    dlx, dlg, dwa_bd, dwx_bd, dlru_vec = _lru_bwd(p1, hs, dyd, *lru_args)
    dp1 = [dca, dcb, dlx, dlg]
    dh4, dg_mix_l1 = _proj_bwd_act(dh5, h4, gain("mix_norm", 1), 1, dp1, odd_pieces, tm_small)
    dw_in_odd = jnp.stack(_proj_bwd_w(nm1, dp1, tm))
    dh3, dg_f1_l1, g = ffn_bwd(dh4, h3, s3, gain("ffn1_norm", 1), f1l1)
    sent = emit((("w_out_odd", 0), ("w_in_odd", 0), ("ffn1_wg", 1), ("ffn1_wu", 1), ("ffn1_wd", 1)),
                [dw_out_odd.reshape(N_SHARD, d // N_SHARD, d), dw_in_odd] + list(g))
    dh2, dg_f2_l0, g_f2l0 = ffn_bwd(dh3, h2, s2, gain("ffn2_norm", 0), f2l0, tuple(sent))
    dya, dyb, dw_out_even = _out_bwd(dh2, ya, yb, w_out_even, tm)
    dpool, dpool_wbd, dpool_scale = _pool_bwd(p0, dya, pool_wbd, pool_scale, tr)
    dq, dz, dv, dgate, dlb_logits, dgn_heads = _hgrn_bwd(p0, dyb, states, w["hgrn_lb_logits"], w["hgrn_gnorm"],
                                                         mst, msk, n_lev, tm)
    dp0 = [jnp.concatenate([dpool, dq, dz, dv, dgate], axis=1)]
    dh1, dg_mix_l0 = _proj_bwd_act(dh2, h1, gain("mix_norm", 0), 0, dp0, even_piece, tm_small, wt=True)
    (dw_in_even_t,) = _proj_bwd_w(nm0, dp0, tm_small, wt=True)
    sent = emit((("ffn2_wg", 0), ("ffn2_wu", 0), ("ffn2_wd", 0), ("w_out_even", 0), ("w_in_even", 0)),
                list(g_f2l0) + [dw_out_even.reshape(N_SHARD, d // N_SHARD, d),
                                dw_in_even_t.reshape(N_SHARD, D_IN_EVEN // N_SHARD, d)])
    ga, gb, sa, n1 = s1
    dh0, da, db, dg_f1_l0, dy = _ffn_bwd_act(dh1, h0, gain("ffn1_norm", 0), ga, gb, *f1l0, 0, tm, tuple(sent))

    grad_x = dh0[N_META:t_real]
    rep = {
        "ffn1_norm": jnp.concatenate([dg_f1_l0, dg_f1_l1], axis=0),
        "mix_norm": jnp.concatenate([dg_mix_l0, dg_mix_l1], axis=0),
        "ffn2_norm": jnp.concatenate([dg_f2_l0, dg_f2_l1], axis=0),
        "final_norm": dg_final,
        "pool_w": _diag_blocks(dpool_wbd, len(POOL_WINDOWS)),
        "pool_scale": dpool_scale,
        "hgrn_lb_logits": dlb_logits,
        "hgrn_gnorm": jnp.sum(dgn_heads, axis=0),
        "lru_wa": _diag_blocks2(dwa_bd),
        "lru_wx": _diag_blocks2(dwx_bd),
    }
    dmeta = jnp.transpose(dh0[:N_META].reshape(N_META, N_SHARD, 2, LANE), (1, 0, 2, 3)).reshape(N_SHARD, 32, LANE)
    packs = [_pack_rows(rep, REP_SPEC)]
    for s in range(N_SHARD):
        sh = {
            "meta_tokens": dmeta[s], "conv_w": dconv_w[s], "lru_conv_w": dlru_vec[s, 0:4],
            "conv_b": dconv_vec[s, 0:1], "conv_ln_g": dconv_vec[s, 1:2], "conv_ln_b": dconv_vec[s, 2:3],
            "lru_conv_b": dlru_vec[s, 4:5], "lru_ba": dlru_vec[s, 5:6], "lru_bx": dlru_vec[s, 6:7],
            "lru_lambda": dlru_vec[s, 7:8],
        }
        packs.append(_pack_rows(sh, SH_SPEC))
    sent = emit_small(jnp.concatenate(packs, axis=0))
    emit((("ffn1_wg", 0), ("ffn1_wu", 0), ("ffn1_wd", 0)), _ffn_bwd_w(dy, n1, sa, da, db, tm, tuple(sent)))
    return loss, grad_x


def kernel(x, meta_tokens, ffn1_norm, ffn1_wg, ffn1_wu, ffn1_wd, mix_norm, ffn2_norm, ffn2_wg, ffn2_wu, ffn2_wd, w_in_even, pool_w, pool_scale, hgrn_lb_logits, hgrn_gnorm, w_out_even, w_in_odd, conv_w, conv_b, conv_ln_g, conv_ln_b, lru_conv_w, lru_conv_b, lru_wa, lru_ba, lru_wx, lru_bx, lru_lambda, w_out_odd, final_norm, loss_target, m_meta_tokens, m_ffn1_norm, m_ffn1_wg, m_ffn1_wu, m_ffn1_wd, m_mix_norm, m_ffn2_norm, m_ffn2_wg, m_ffn2_wu, m_ffn2_wd, m_w_in_even, m_pool_w, m_pool_scale, m_hgrn_lb_logits, m_hgrn_gnorm, m_w_out_even, m_w_in_odd, m_conv_w, m_conv_b, m_conv_ln_g, m_conv_ln_b, m_lru_conv_w, m_lru_conv_b, m_lru_wa, m_lru_ba, m_lru_wx, m_lru_bx, m_lru_lambda, m_w_out_odd, m_final_norm, v_meta_tokens, v_ffn1_norm, v_ffn1_wg, v_ffn1_wu, v_ffn1_wd, v_mix_norm, v_ffn2_norm, v_ffn2_wg, v_ffn2_wu, v_ffn2_wd, v_w_in_even, v_pool_w, v_pool_scale, v_hgrn_lb_logits, v_hgrn_gnorm, v_w_out_even, v_w_in_odd, v_conv_w, v_conv_b, v_conv_ln_g, v_conv_ln_b, v_lru_conv_w, v_lru_conv_b, v_lru_wa, v_lru_ba, v_lru_wx, v_lru_bx, v_lru_lambda, v_w_out_odd, v_final_norm):
    args = locals()
    w = {n: args[n] for n in WEIGHT_NAMES}
    m = {n: args["m_" + n] for n in WEIGHT_NAMES}
    v = {n: args["v_" + n] for n in WEIGHT_NAMES}
    shapes = {n: w[n].shape for n in WEIGHT_NAMES}
    core = lax.axis_index("c").astype(jnp.int32).reshape(1)
    chip = (2 * lax.axis_index("x") + lax.axis_index("y")).astype(jnp.int32).reshape(1)
    me = 2 * chip + core

    def view(a, n):
        return jnp.swapaxes(a, 1, 2) if n in TRANSPOSED else a

    wv, mv, vv = [{n: view(src[n], n) for n in BIG} for src in (w, m, v)]

    def shard(key):
        n, l = key
        return _pack_rows(w, SH_SPEC) if n == "small" else wv[n][l].astype(BF16)

    started = {}
    for groups in (GATHER_GROUPS[:2], GATHER_GROUPS[2:]):
        gkeys = [key for grp in groups for key in grp]
        ssem, rsem, srcs, lands, token = _gather_start([shard(key) for key in gkeys],
                                                       [key in GATHER_GROUPS[1] for key in gkeys])
        for k, key in enumerate(gkeys):
            started[key] = (ssem, rsem, srcs[k], lands[k], k, token)

    def pack_small(src):
        return jnp.concatenate([_pack_rows(src, REP_SPEC), _pack_rows(src, SH_SPEC)], axis=0)

    small_packs = [pack_small(src) for src in (w, m, v)]

    def fetch(group, after):
        st = [started[key] for key in GATHER_GROUPS[group]]
        deps = (st[0][5],) if after is None else (after,)
        if group == 1:
            deps += (started[GATHER_GROUPS[2][0]][5],) + tuple(small_packs)
        split = group == 1
        got = _gather_wait(st[0][0], st[0][1], [s[2] for s in st], [s[3] for s in st], [s[4] for s in st], deps,
                           split)
        return _pair_forward(got) if split else got

    in_flight, reduced = [], {}

    def collect(entry, after):
        gkeys, gs_sem, gr_sem, grads_thru, slots_thru, _ = entry
        slots = _reduce_wait(gs_sem, gr_sem, grads_thru, slots_thru, after)
        full = _pair_allgather_halves(_sum_devices(slots, core))
        reduced.update(zip(gkeys, full))
        return full[0]

    def emit(gkeys, grads):
        grads = list(grads)
        in_flight.append((gkeys,) + tuple(_reduce_start(grads, _own_part(grads, chip, core, me))))
        token = in_flight[-1][-1]
        if len(in_flight) > SCATTER_DEPTH:
            return token, collect(in_flight[-1 - SCATTER_DEPTH], (token,))
        return (token,)

    small_flight = []

    def emit_small(part):
        small_flight.append(_small_start(part, _small_own(part, me)))
        return (small_flight[0][4],)

    loss, grad_x = _local_step(x[0], loss_target[0], w, shapes, fetch, emit, emit_small)
    loss = lax.psum(loss[0, 0], ("x", "y", "c"))

    s_ssem, s_rsem, s_part, s_slots, _ = small_flight[0]
    small_all = _small_wait(s_ssem, s_rsem, s_part, s_slots, (in_flight[-1][-1],))
    small_res = _small_reduce_adamw(small_all.reshape(-1, LANE), *small_packs, REP_ROWS, SH_ROWS)
    out_g, out_d, out_m, out_v = {}, {}, {}, {}
    deps = (small_res[0],)
    for entry in in_flight[-SCATTER_DEPTH:]:
        collect(entry, deps)
        for n in BIG:
            layers = range(shapes[n][0])
            if n not in out_g and all((n, l) in reduced for l in layers):
                res = _adamw(wv[n], mv[n], vv[n], [reduced[(n, l)] for l in layers], ADAM_ROW_BLOCKS[n])
                out_g[n], out_d[n], out_m[n], out_v[n] = [view(r, n) for r in res]
                deps += (res[1],)

    for dst, packed in zip((out_g, out_d, out_m, out_v), small_res):
        dst.update(_unpack_rows(packed[:REP_ROWS], REP_SPEC, shapes))
        dst.update(_unpack_rows(packed[REP_ROWS:], SH_SPEC, shapes))

    return (loss, grad_x[None], *[out_g[n] for n in WEIGHT_NAMES], *[out_d[n] for n in WEIGHT_NAMES],
            *[out_m[n] for n in WEIGHT_NAMES], *[out_v[n] for n in WEIGHT_NAMES])
```

```python
import functools

import numpy as np
import jax
import jax.numpy as jnp
from jax import lax
from jax.experimental import pallas as pl
from jax.experimental.pallas import tpu as pltpu

F32 = jnp.float32
BF16 = jnp.bfloat16
MESH = pl.DeviceIdType.MESH

EPS = 1e-6
N_META = 16
D_MODEL = 1024
D_FF = 2816
N_SHARD = 4
FF_SHARD = D_FF // N_SHARD
D_POOL = 256
POOL_GROUP = 64
POOL_WINDOWS = (2, 4, 8, 16)
D_HGRN = 768
HG_HEADS = 6
HEAD = 128
HG_CHUNK = 64
HG_HEADS_PER_STEP = 6
HG_PBLOCK = 256
D_IN_EVEN = D_POOL + 4 * D_HGRN
D_CONV = 512
CONV_WIDTH = 31
CONV_HALO = 32
D_LRU = 512
LRU_CONV = 4
LRU_HALO = 8
LRU_C = 8.0
D_IN_ODD = 2 * D_CONV + 2 * D_LRU
SUBLANE = 8
ROW_ALIGN = 64

ADAM_LR = 0.001
ADAM_B1 = 0.9
ADAM_B2 = 0.999
ADAM_EPS = 1e-08
ADAM_WD = 0.01
ADAM_STEP = 10

VMEM_LIMIT_MB = 56


def _cparams(n_grid_axes=0, vmem_mb=VMEM_LIMIT_MB):
    sem = ("arbitrary",) * n_grid_axes if n_grid_axes else None
    return pltpu.CompilerParams(dimension_semantics=sem, vmem_limit_bytes=vmem_mb * 1024 * 1024)


def _tile(n, target, mult):
    best = None
    for t in range(mult, min(n, target) + 1, mult):
        if n % t == 0:
            best = t
    assert best is not None, (n, target, mult)
    return best


def _dot(a, b):
    return jnp.dot(a, b, preferred_element_type=F32)


def _dot_nt(a, b):
    return lax.dot_general(a, b, (((1,), (1,)), ((), ())), preferred_element_type=F32)


def _dot_tn(a, b):
    return lax.dot_general(a, b, (((0,), (0,)), ((), ())), preferred_element_type=F32)


def _sigmoid(x):
    return 1.0 / (1.0 + jnp.exp(-x))


def _colsum(x):
    return jnp.sum(x, axis=0, keepdims=True)


def _rms_stats(h):
    rstd = lax.rsqrt(jnp.mean(h * h, axis=-1, keepdims=True) + EPS)
    return rstd, h * rstd


def _rms_bwd(dn, g, rstd, xhat):
    dng = dn * g
    dh = rstd * (dng - xhat * jnp.mean(dng * xhat, axis=-1, keepdims=True))
    return dh, _colsum(dn * xhat)


def _ffn_fwd(h, norm, wg4, wu4, wd4, layer, tm):
    tp, d = h.shape
    nt = tp // tm

    def body(h_ref, g_ref, wg_ref, wu_ref, wd_ref, ho_ref, ga_ref, gb_ref, sa_ref, n_ref, n_sc, acc):
        s = pl.program_id(1)

        @pl.when(s == 0)
        def _():
            hh = h_ref[...]
            rstd, xhat = _rms_stats(hh)
            n = (xhat * g_ref[...]).astype(BF16)
            n_sc[...] = n
            n_ref[...] = n
            acc[...] = jnp.zeros_like(acc)

        n = n_sc[...]
        a = _dot_nt(n, wg_ref[0])
        b = _dot_nt(n, wu_ref[0])
        sig = _sigmoid(a)
        sil = a * sig
        ga_ref[0] = (sig * (1.0 + a * (1.0 - sig)) * b).astype(BF16)
        gb_ref[0] = sil.astype(BF16)
        sg = (sil * b).astype(BF16)
        sa_ref[0] = sg
        acc[...] += _dot(sg, wd_ref[0])

        @pl.when(s == N_SHARD - 1)
        def _():
            ho_ref[...] = h_ref[...] + 0.5 * acc[...]

    return pl.pallas_call(
        body, name="ffn_fwd",
        grid=(nt, N_SHARD),
        in_specs=[
            pl.BlockSpec((tm, d), lambda i, s: (i, 0)),
            pl.BlockSpec((1, d), lambda i, s: (0, 0)),
            pl.BlockSpec((1, FF_SHARD, d), lambda i, s: (s, layer, 0)),
            pl.BlockSpec((1, FF_SHARD, d), lambda i, s: (s, layer, 0)),
            pl.BlockSpec((1, FF_SHARD, d), lambda i, s: (s, layer, 0)),
        ],
        out_specs=[
            pl.BlockSpec((tm, d), lambda i, s: (i, 0)),
            pl.BlockSpec((1, tm, FF_SHARD), lambda i, s: (s, i, 0)),
            pl.BlockSpec((1, tm, FF_SHARD), lambda i, s: (s, i, 0)),
            pl.BlockSpec((1, tm, FF_SHARD), lambda i, s: (s, i, 0)),
            pl.BlockSpec((tm, d), lambda i, s: (i, 0)),
        ],
        out_shape=[
            jax.ShapeDtypeStruct((tp, d), F32),
            jax.ShapeDtypeStruct((N_SHARD, tp, FF_SHARD), BF16),
            jax.ShapeDtypeStruct((N_SHARD, tp, FF_SHARD), BF16),
            jax.ShapeDtypeStruct((N_SHARD, tp, FF_SHARD), BF16),
            jax.ShapeDtypeStruct((tp, d), BF16),
        ],
        scratch_shapes=[pltpu.VMEM((tm, d), BF16), pltpu.VMEM((tm, d), F32)],
        compiler_params=_cparams(2),
    )(h, norm, wg4, wu4, wd4)


def _ffn_bwd_act(dho, h, norm, ga4, gb4, wg4, wu4, wd4, layer, tm, after=()):
    tp, d = h.shape
    nt = tp // tm

    def body(dho_ref, h_ref, g_ref, ga_ref, gb_ref, wg_ref, wu_ref, wd_ref, *rest):
        dh_ref, da_ref, db_ref, dg_ref, dy_ref, dn_sc = rest[len(after):]
        i = pl.program_id(0)
        s = pl.program_id(1)

        @pl.when(s == 0)
        def _():
            dy_ref[...] = (0.5 * dho_ref[...]).astype(BF16)
            dn_sc[...] = jnp.zeros_like(dn_sc)

        @pl.when((s == 0) & (i == 0))
        def _():
            dg_ref[...] = jnp.zeros_like(dg_ref)

        ds = _dot_nt(dy_ref[...], wd_ref[0])
        da = (ds * ga_ref[0].astype(F32)).astype(BF16)
        db = (ds * gb_ref[0].astype(F32)).astype(BF16)
        da_ref[0] = da
        db_ref[0] = db
        dn_sc[...] += _dot(da, wg_ref[0]) + _dot(db, wu_ref[0])

        @pl.when(s == N_SHARD - 1)
        def _():
            rstd, xhat = _rms_stats(h_ref[...])
            dh, dg = _rms_bwd(dn_sc[...], g_ref[...], rstd, xhat)
            dh_ref[...] = dho_ref[...] + dh
            dg_ref[...] += dg

    return pl.pallas_call(
        body, name="ffn_bwd_act",
        grid=(nt, N_SHARD),
        in_specs=[
            pl.BlockSpec((tm, d), lambda i, s: (i, 0)),
            pl.BlockSpec((tm, d), lambda i, s: (i, 0)),
            pl.BlockSpec((1, d), lambda i, s: (0, 0)),
            pl.BlockSpec((1, tm, FF_SHARD), lambda i, s: (s, i, 0)),
            pl.BlockSpec((1, tm, FF_SHARD), lambda i, s: (s, i, 0)),
            pl.BlockSpec((1, FF_SHARD, d), lambda i, s: (s, layer, 0)),
            pl.BlockSpec((1, FF_SHARD, d), lambda i, s: (s, layer, 0)),
            pl.BlockSpec((1, FF_SHARD, d), lambda i, s: (s, layer, 0)),
        ] + [pl.BlockSpec(memory_space=pl.ANY)] * len(after),
        out_specs=[
            pl.BlockSpec((tm, d), lambda i, s: (i, 0)),
            pl.BlockSpec((1, tm, FF_SHARD), lambda i, s: (s, i, 0)),
            pl.BlockSpec((1, tm, FF_SHARD), lambda i, s: (s, i, 0)),
            pl.BlockSpec((1, d), lambda i, s: (0, 0)),
            pl.BlockSpec((tm, d), lambda i, s: (i, 0)),
        ],
        out_shape=[
            jax.ShapeDtypeStruct((tp, d), F32),
            jax.ShapeDtypeStruct((N_SHARD, tp, FF_SHARD), BF16),
            jax.ShapeDtypeStruct((N_SHARD, tp, FF_SHARD), BF16),
            jax.ShapeDtypeStruct((1, d), F32),
            jax.ShapeDtypeStruct((tp, d), BF16),
        ],
        scratch_shapes=[pltpu.VMEM((tm, d), F32)],
        compiler_params=_cparams(2),
    )(dho, h, norm, ga4, gb4, wg4, wu4, wd4, *after)


def _ffn_bwd_w(dy, n, sa4, da4, db4, tm, after=()):
    tp, d = n.shape
    nt = tp // tm

    def body(dy_ref, n_ref, sa_ref, da_ref, db_ref, *rest):
        og_ref, ou_ref, od_ref, accg, accu, accd = rest[len(after):]
        i = pl.program_id(1)

        @pl.when(i == 0)
        def _():
            accg[...] = jnp.zeros_like(accg)
            accu[...] = jnp.zeros_like(accu)
            accd[...] = jnp.zeros_like(accd)

        nn = n_ref[...]
        accg[...] += _dot_tn(da_ref[0], nn)
        accu[...] += _dot_tn(db_ref[0], nn)
        accd[...] += _dot_tn(sa_ref[0], dy_ref[...])

        @pl.when(i == nt - 1)
        def _():
            og_ref[0] = accg[...].astype(BF16)
            ou_ref[0] = accu[...].astype(BF16)
            od_ref[0] = accd[...].astype(BF16)

    in_specs = [
        pl.BlockSpec((tm, d), lambda s, i: (i, 0)),
        pl.BlockSpec((tm, d), lambda s, i: (i, 0)),
        pl.BlockSpec((1, tm, FF_SHARD), lambda s, i: (s, i, 0)),
        pl.BlockSpec((1, tm, FF_SHARD), lambda s, i: (s, i, 0)),
        pl.BlockSpec((1, tm, FF_SHARD), lambda s, i: (s, i, 0)),
    ]
    return pl.pallas_call(
        body, name="ffn_bwd_w",
        grid=(N_SHARD, nt),
        in_specs=in_specs + [pl.BlockSpec(memory_space=pl.ANY)] * len(after),
        out_specs=[pl.BlockSpec((1, FF_SHARD, d), lambda s, i: (s, 0, 0))] * 3,
        out_shape=[jax.ShapeDtypeStruct((N_SHARD, FF_SHARD, d), BF16)] * 3,
        scratch_shapes=[pltpu.VMEM((FF_SHARD, d), F32)] * 3,
        compiler_params=_cparams(2),
    )(dy, n, sa4, da4, db4, *after)


def _proj_fwd(h, norm, layer, w_pieces, tm, wt=False):
    tp, d = h.shape
    widths = [bs[-2] if wt else bs[-1] for _, bs, _ in w_pieces]
    ntot = sum(widths)
    npc = len(w_pieces)

    def body(*refs):
        h_ref, g_ref = refs[:2]
        w_refs = refs[2:2 + npc]
        p_ref, n_ref = refs[2 + npc:]
        rstd, xhat = _rms_stats(h_ref[...])
        n = (xhat * g_ref[...]).astype(BF16)
        n_ref[...] = n
        off = 0
        for k in range(npc):
            w = w_refs[k][...]
            w = w.reshape(w.shape[-2], w.shape[-1])
            p_ref[:, off:off + widths[k]] = _dot_nt(n, w) if wt else _dot(n, w)
            off += widths[k]

    in_specs = [pl.BlockSpec((tm, d), lambda i: (i, 0)), pl.BlockSpec((1, d), lambda i: (0, 0))]
    for _, bs, idx in w_pieces:
        in_specs.append(pl.BlockSpec(bs, functools.partial(lambda i, idx: idx, idx=idx)))
    return pl.pallas_call(
        body, name="proj_fwd",
        grid=(tp // tm,),
        in_specs=in_specs,
        out_specs=[pl.BlockSpec((tm, ntot), lambda i: (i, 0)), pl.BlockSpec((tm, d), lambda i: (i, 0))],
        out_shape=[jax.ShapeDtypeStruct((tp, ntot), F32), jax.ShapeDtypeStruct((tp, d), BF16)],
        compiler_params=_cparams(1),
    )(h, norm, *[w for w, _, _ in w_pieces])


def _proj_bwd_act(dres, h, norm, layer, dp_pieces, w_pieces, tm, wt=False):
    tp, d = h.shape
    npc = len(w_pieces)

    def body(*refs):
        dres_ref, h_ref, g_ref = refs[:3]
        dp_refs = refs[3:3 + npc]
        w_refs = refs[3 + npc:3 + 2 * npc]
        dh_ref, dg_ref = refs[3 + 2 * npc:]
        i = pl.program_id(0)

        @pl.when(i == 0)
        def _():
            dg_ref[...] = jnp.zeros_like(dg_ref)

        dn = None
        for k in range(npc):
            w = w_refs[k][...]
            w = w.reshape(w.shape[-2], w.shape[-1])
            t = _dot(dp_refs[k][...], w) if wt else _dot_nt(dp_refs[k][...], w)
            dn = t if dn is None else dn + t
        rstd, xhat = _rms_stats(h_ref[...])
        dh, dg = _rms_bwd(dn, g_ref[...], rstd, xhat)
        dh_ref[...] = dres_ref[...] + dh
        dg_ref[...] += dg

    in_specs = [pl.BlockSpec((tm, d), lambda i: (i, 0)), pl.BlockSpec((tm, d), lambda i: (i, 0)),
                pl.BlockSpec((1, d), lambda i: (0, 0))]
    for dp in dp_pieces:
        in_specs.append(pl.BlockSpec((tm, dp.shape[1]), lambda i: (i, 0)))
    for _, bs, idx in w_pieces:
        in_specs.append(pl.BlockSpec(bs, functools.partial(lambda i, idx: idx, idx=idx)))
    return pl.pallas_call(
        body, name="proj_bwd_act",
        grid=(tp // tm,),
        in_specs=in_specs,
        out_specs=[pl.BlockSpec((tm, d), lambda i: (i, 0)), pl.BlockSpec((1, d), lambda i: (0, 0))],
        out_shape=[jax.ShapeDtypeStruct((tp, d), F32), jax.ShapeDtypeStruct((1, d), F32)],
        compiler_params=_cparams(1),
    )(dres, h, norm, *dp_pieces, *[w for w, _, _ in w_pieces])


def _proj_bwd_w(n, dp_pieces, tm, wt=False):
    tp, d = n.shape
    npc = len(dp_pieces)
    widths = [dp.shape[1] for dp in dp_pieces]
    oshape = (lambda w: (w, d)) if wt else (lambda w: (d, w))

    def body(*refs):
        n_ref = refs[0]
        dp_refs = refs[1:1 + npc]
        o_refs = refs[1 + npc:1 + 2 * npc]
        accs = refs[1 + 2 * npc:]
        i = pl.program_id(0)

        @pl.when(i == 0)
        def _():
            for acc in accs:
                acc[...] = jnp.zeros_like(acc)

        nn = n_ref[...]
        for k in range(npc):
            accs[k][...] += _dot_tn(dp_refs[k][...], nn) if wt else _dot_tn(nn, dp_refs[k][...])

        @pl.when(i == pl.num_programs(0) - 1)
        def _():
            for k in range(npc):
                o_refs[k][...] = accs[k][...].astype(BF16)

    return pl.pallas_call(
        body, name="proj_bwd_w",
        grid=(tp // tm,),
        in_specs=[pl.BlockSpec((tm, d), lambda i: (i, 0))]
        + [pl.BlockSpec((tm, w), lambda i: (i, 0)) for w in widths],
        out_specs=[pl.BlockSpec(oshape(w), lambda i: (0, 0)) for w in widths],
        out_shape=[jax.ShapeDtypeStruct(oshape(w), BF16) for w in widths],
        scratch_shapes=[pltpu.VMEM(oshape(w), F32) for w in widths],
        compiler_params=_cparams(1),
    )(n, *dp_pieces)


def _out_fwd(h, ya, yb, w, tm):
    tp, d = h.shape
    na, nb = ya.shape[1], yb.shape[1]

    def body(h_ref, ya_ref, yb_ref, w_ref, o_ref):
        y = _dot(ya_ref[...].astype(BF16), w_ref[0:na, :]) + _dot(yb_ref[...].astype(BF16), w_ref[na:, :])
        o_ref[...] = h_ref[...] + y

    return pl.pallas_call(
        body, name="out_fwd",
        grid=(tp // tm,),
        in_specs=[pl.BlockSpec((tm, d), lambda i: (i, 0)), pl.BlockSpec((tm, na), lambda i: (i, 0)),
                  pl.BlockSpec((tm, nb), lambda i: (i, 0)), pl.BlockSpec((d, d), lambda i: (0, 0))],
        out_specs=pl.BlockSpec((tm, d), lambda i: (i, 0)),
        out_shape=jax.ShapeDtypeStruct((tp, d), F32),
        compiler_params=_cparams(1),
    )(h, ya, yb, w)


def _out_bwd(dy, ya, yb, w, tm, after=()):
    tp, d = dy.shape
    na, nb = ya.shape[1], yb.shape[1]

    def body(dy_ref, ya_ref, yb_ref, w_ref, *rest):
        da_ref, db_ref, dw_ref, acc = rest[len(after):]
        i = pl.program_id(0)

        @pl.when(i == 0)
        def _():
            acc[...] = jnp.zeros_like(acc)

        dyb16 = dy_ref[...].astype(BF16)
        da_ref[...] = _dot_nt(dyb16, w_ref[0:na, :])
        db_ref[...] = _dot_nt(dyb16, w_ref[na:, :])
        acc[0:na, :] += _dot_tn(ya_ref[...].astype(BF16), dyb16)
        acc[na:, :] += _dot_tn(yb_ref[...].astype(BF16), dyb16)

        @pl.when(i == pl.num_programs(0) - 1)
        def _():
            dw_ref[...] = acc[...].astype(BF16)

    return pl.pallas_call(
        body, name="out_bwd",
        grid=(tp // tm,),
        in_specs=[pl.BlockSpec((tm, d), lambda i: (i, 0)), pl.BlockSpec((tm, na), lambda i: (i, 0)),
                  pl.BlockSpec((tm, nb), lambda i: (i, 0)), pl.BlockSpec((d, d), lambda i: (0, 0))]
        + [pl.BlockSpec(memory_space=pl.ANY)] * len(after),
        out_specs=[pl.BlockSpec((tm, na), lambda i: (i, 0)), pl.BlockSpec((tm, nb), lambda i: (i, 0)),
                   pl.BlockSpec((d, d), lambda i: (0, 0))],
        out_shape=[jax.ShapeDtypeStruct((tp, na), F32), jax.ShapeDtypeStruct((tp, nb), F32),
                   jax.ShapeDtypeStruct((d, d), BF16)],
        scratch_shapes=[pltpu.VMEM((d, d), F32)],
        compiler_params=_cparams(1),
    )(dy, ya, yb, w, *after)


def _loss_bwd(h, gfin, tgt, t_real, tm):
    tp, d = h.shape

    def body(h_ref, g_ref, t_ref, loss_ref, dh_ref, dg_ref):
        i = pl.program_id(0)

        @pl.when(i == 0)
        def _():
            loss_ref[...] = jnp.zeros_like(loss_ref)
            dg_ref[...] = jnp.zeros_like(dg_ref)

        rows = i * tm + lax.broadcasted_iota(jnp.int32, (tm, 1), 0)
        valid = (rows >= N_META) & (rows < t_real)
        rstd, xhat = _rms_stats(h_ref[...])
        g = g_ref[...]
        err = jnp.where(valid, xhat * g - t_ref[...], 0.0)
        e2 = jnp.sum(err * err, axis=1, keepdims=True)
        loss_ref[...] += (0.5 / d) * jnp.sum(e2, axis=0, keepdims=True)
        dy = err * (1.0 / d)
        dh, dg = _rms_bwd(dy, g, rstd, xhat)
        dh_ref[...] = dh
        dg_ref[...] += dg

    return pl.pallas_call(
        body, name="loss_bwd",
        grid=(tp // tm,),
        in_specs=[pl.BlockSpec((tm, d), lambda i: (i, 0)), pl.BlockSpec((1, d), lambda i: (0, 0)),
                  pl.BlockSpec((tm, d), lambda i: (i, 0))],
        out_specs=[pl.BlockSpec((1, 1), lambda i: (0, 0)), pl.BlockSpec((tm, d), lambda i: (i, 0)),
                   pl.BlockSpec((1, d), lambda i: (0, 0))],
        out_shape=[jax.ShapeDtypeStruct((1, 1), F32), jax.ShapeDtypeStruct((tp, d), F32),
                   jax.ShapeDtypeStruct((1, d), F32)],
        compiler_params=_cparams(1),
    )(h, gfin, tgt)


POOL_HALO = 16


def _pool_lane_consts(n_rows):
    lane = lax.broadcasted_iota(jnp.int32, (n_rows, D_POOL), 1)
    grp = lane // POOL_GROUP
    win = jnp.where(grp == 0, 2.0, jnp.where(grp == 1, 4.0, jnp.where(grp == 2, 8.0, 16.0)))
    return grp, win


def _pool_select(grp, s2, s4, s8, s16):
    return jnp.where(grp == 0, s2, jnp.where(grp == 1, s4, jnp.where(grp == 2, s8, s16)))


def _pool_mixed(x, row0, tr):
    n = tr + POOL_HALO
    s2 = x + pltpu.roll(x, 1, 0)
    s4 = s2 + pltpu.roll(s2, 2, 0)
    s8 = s4 + pltpu.roll(s4, 4, 0)
    s16 = s8 + pltpu.roll(s8, 8, 0)
    grp, win = _pool_lane_consts(n)
    rows = row0 - POOL_HALO + lax.broadcasted_iota(jnp.int32, (n, D_POOL), 0)
    cnt = jnp.minimum((rows + 1).astype(F32), win)
    pooled = _pool_select(grp, s2, s4, s8, s16) / jnp.maximum(cnt, 1.0)
    return (pooled - x)[POOL_HALO:, :]


def _pool_fwd(p, wbd, scale, tr):
    tp = p.shape[0]
    nt = tp // tr

    def body(p_ref, w_ref, s_ref, y_ref, usc):
        usc[0:POOL_HALO, :] = jnp.zeros((POOL_HALO, D_POOL), F32)
        usc[POOL_HALO:, :] = p_ref[...]

        def tile(r, carry):
            r0 = pl.multiple_of(r * tr, SUBLANE)
            x = usc[pl.ds(r0, tr + POOL_HALO), :]
            mixed = _pool_mixed(x, r0, tr)
            y_ref[pl.ds(r0, tr), :] = _dot(mixed.astype(BF16), w_ref[...]) * s_ref[...]
            return carry

        lax.fori_loop(0, nt, tile, 0)

    return pl.pallas_call(
        body, name="pool_fwd",
        grid=(1,),
        in_specs=[pl.BlockSpec((tp, D_POOL), lambda i: (0, 0)), pl.BlockSpec((D_POOL, D_POOL), lambda i: (0, 0)),
                  pl.BlockSpec((1, D_POOL), lambda i: (0, 0))],
        out_specs=pl.BlockSpec((tp, D_POOL), lambda i: (0, 0)),
        out_shape=jax.ShapeDtypeStruct((tp, D_POOL), F32),
        scratch_shapes=[pltpu.VMEM((tp + POOL_HALO, D_POOL), F32)],
        compiler_params=_cparams(1),
    )(p, wbd, scale)


def _pool_bwd(p, dya, wbd, scale, tr):
    tp = p.shape[0]
    nt = tp // tr

    def body(p_ref, dy_ref, w_ref, s_ref, du_ref, dw_ref, ds_ref, usc, gsc):
        usc[0:POOL_HALO, :] = jnp.zeros((POOL_HALO, D_POOL), F32)
        usc[POOL_HALO:, :] = p_ref[...]
        gsc[tp:, :] = jnp.zeros((POOL_HALO, D_POOL), F32)
        dw_ref[...] = jnp.zeros_like(dw_ref)
        ds_ref[...] = jnp.zeros_like(ds_ref)
        grp, win = _pool_lane_consts(tr)

        def tile1(r, carry):
            r0 = pl.multiple_of(r * tr, SUBLANE)
            x = usc[pl.ds(r0, tr + POOL_HALO), :]
            mixed = _pool_mixed(x, r0, tr).astype(BF16)
            dy = dy_ref[pl.ds(r0, tr), :]
            dys = (dy * s_ref[...]).astype(BF16)
            ypre = _dot(mixed, w_ref[...])
            ds_ref[...] += _colsum(dy * ypre)
            dw_ref[...] += _dot_tn(mixed, dys)
            dmx = _dot_nt(dys, w_ref[...])
            rows = r0 + lax.broadcasted_iota(jnp.int32, (tr, D_POOL), 0)
            cnt = jnp.minimum((rows + 1).astype(F32), win)
            gsc[pl.ds(r0, tr), :] = dmx / cnt
            return carry

        lax.fori_loop(0, nt, tile1, 0)
        n = tr + POOL_HALO
        grp2, win2 = _pool_lane_consts(n)

        def tile2(r, carry):
            r0 = pl.multiple_of(r * tr, SUBLANE)
            g = gsc[pl.ds(r0, n), :]
            s2 = g + pltpu.roll(g, n - 1, 0)
            s4 = s2 + pltpu.roll(s2, n - 2, 0)
            s8 = s4 + pltpu.roll(s4, n - 4, 0)
            s16 = s8 + pltpu.roll(s8, n - 8, 0)
            pooled_t = _pool_select(grp2, s2, s4, s8, s16)
            rows = r0 + lax.broadcasted_iota(jnp.int32, (n, D_POOL), 0)
            cnt = jnp.minimum((rows + 1).astype(F32), win2)
            du = pooled_t - g * cnt
            du_ref[pl.ds(r0, tr), :] = du[0:tr, :].astype(BF16)
            return carry

        lax.fori_loop(0, nt, tile2, 0)

    return pl.pallas_call(
        body, name="pool_bwd",
        grid=(1,),
        in_specs=[pl.BlockSpec((tp, D_POOL), lambda i: (0, 0)), pl.BlockSpec((tp, D_POOL), lambda i: (0, 0)),
                  pl.BlockSpec((D_POOL, D_POOL), lambda i: (0, 0)), pl.BlockSpec((1, D_POOL), lambda i: (0, 0))],
        out_specs=[pl.BlockSpec((tp, D_POOL), lambda i: (0, 0)), pl.BlockSpec((D_POOL, D_POOL), lambda i: (0, 0)),
                   pl.BlockSpec((1, D_POOL), lambda i: (0, 0))],
        out_shape=[jax.ShapeDtypeStruct((tp, D_POOL), BF16), jax.ShapeDtypeStruct((D_POOL, D_POOL), F32),
                   jax.ShapeDtypeStruct((1, D_POOL), F32)],
        scratch_shapes=[pltpu.VMEM((tp + POOL_HALO, D_POOL), F32), pltpu.VMEM((tp + POOL_HALO, D_POOL), F32)],
        compiler_params=_cparams(1),
    )(p, dya, wbd, scale)


def _hgrn_levels(ch):
    levels = []
    w = ch // 2
    while w >= 1:
        levels.append(w)
        w //= 2
    return levels


def _hgrn_consts(ch):
    t = np.arange(ch)
    tril = t[None, :] <= t[:, None]
    masks = []
    for w in _hgrn_levels(ch):
        blk = t // (2 * w)
        upper = t % (2 * w) >= w
        masks.append(upper[:, None] & (~upper)[None, :] & (blk[:, None] == blk[None, :]))
    masks.append(tril)
    msk = np.stack(masks).astype(np.float32)
    return jnp.asarray(tril.astype(np.float32), BF16), jnp.asarray(msk, F32), len(masks) - 1


def _split3(x):
    hi = x.astype(BF16)
    r1 = x - hi.astype(F32)
    mid = r1.astype(BF16)
    lo = (r1 - mid.astype(F32)).astype(BF16)
    return hi, mid, lo


def _hgrn_exponents(tril, logf):
    ch = logf.shape[0]
    hi, mid, lo = _split3(logf)
    x = _dot(tril, jnp.concatenate([hi, mid, lo], axis=1))
    b = x[:, 0:HEAD] + x[:, HEAD:2 * HEAD] + x[:, 2 * HEAD:3 * HEAD]
    rows = lax.broadcasted_iota(jnp.int32, (ch, HEAD), 0)
    fx = jnp.broadcast_to(b[ch - 1:ch, :], (ch, HEAD)) - b
    lev = []
    for w in _hgrn_levels(ch):
        pos = rows % (2 * w)
        upper = pos >= w
        if w >= SUBLANE:
            parts = [jnp.broadcast_to(b[k * 2 * w + w - 1:k * 2 * w + w, :], (2 * w, HEAD))
                     for k in range(ch // (2 * w))]
            bmid = parts[0] if len(parts) == 1 else jnp.concatenate(parts, axis=0)
            dx = jnp.where(upper, b - bmid, 0.0)
            ex = jnp.where(upper, 0.0, bmid - b)
        else:
            dx = logf
            ex = jnp.zeros_like(logf)
            for i in range(1, w):
                dx = dx + jnp.where(pos >= w + i, pltpu.roll(logf, i, 0), 0.0)
                ex = ex + jnp.where(pos <= w - 1 - i, pltpu.roll(logf, ch - i, 0), 0.0)
            dx = jnp.where(upper, dx, 0.0)
        lev.append((dx, ex))
    return b, fx, lev


def _hgrn_exponents_bwd(tril, d_b, d_fx, d_blast, lev_grads):
    ch = d_b.shape[0]
    rows = lax.broadcasted_iota(jnp.int32, (ch, HEAD), 0)
    db = d_b - d_fx
    dlf = jnp.zeros_like(d_b)
    for w, (ddx, dex) in zip(_hgrn_levels(ch), lev_grads):
        pos = rows % (2 * w)
        upper = pos >= w
        gu = jnp.where(upper, ddx, 0.0)
        if w >= SUBLANE:
            gl = jnp.where(upper, 0.0, dex)
            db = db + gu - gl
            diff = gl - gu
            for k in range(ch // (2 * w)):
                s = _colsum(diff[k * 2 * w:(k + 1) * 2 * w, :])
                db = db + jnp.where(rows == k * 2 * w + w - 1, s, 0.0)
        else:
            dlf = dlf + gu
            for i in range(1, w):
                dlf = dlf + pltpu.roll(jnp.where(pos >= w + i, gu, 0.0), ch - i, 0)
                dlf = dlf + pltpu.roll(jnp.where(pos <= w - 1 - i, dex, 0.0), i, 0)
    db = db + jnp.where(rows == ch - 1, _colsum(d_fx) + d_blast, 0.0)
    hi = db.astype(BF16)
    lo = (db - hi.astype(F32)).astype(BF16)
    d2 = _dot_tn(tril, jnp.concatenate([hi, lo], axis=1))
    return d2[:, 0:HEAD] + d2[:, HEAD:2 * HEAD] + dlf


def _lockstep(gens):
    results = [None] * len(gens)
    live = list(range(len(gens)))
    while live:
        for i in list(live):
            try:
                next(gens[i])
            except StopIteration as stop:
                results[i] = stop.value
                live.remove(i)
    return results


def _hgrn_gates(q_raw, z, lb):
    sz = _sigmoid(z)
    f = lb + (1.0 - lb) * sz
    q = q_raw * _sigmoid(q_raw)
    k = (1.0 - lb) * (1.0 - sz)
    return q, k, f, sz


def _hgrn_intra(q, k, lev, msk_ref, n_lev, ch):
    eye = (lax.broadcasted_iota(jnp.int32, (ch, ch), 0) == lax.broadcasted_iota(jnp.int32, (ch, ch), 1))
    a = jnp.where(eye, jnp.sum(q * k, axis=1, keepdims=True), 0.0)
    ops = []
    for lv in range(n_lev):
        eq = jnp.exp(lev[lv][0])
        ek = jnp.exp(lev[lv][1])
        qd = q * eq
        kd = k * ek
        a = a + msk_ref[lv] * _dot_nt(qd.astype(BF16), kd.astype(BF16))
        ops.append((eq, ek, qd, kd))
        yield
    return a, ops


def _hgrn_fwd(p, lb_logits, gnorm, mst, msk, n_lev, tm):
    tp = p.shape[0]
    ch = HG_CHUNK
    nct = tm // ch
    nt = tp // tm
    nr = mst.shape[0]
    base = D_POOL // HEAD

    hp = HG_HEADS_PER_STEP
    wide = hp * HEAD
    npr = wide // HG_PBLOCK

    def body(*refs):
        p_refs = refs[:4 * npr]
        lg_ref, gn_ref, mst_ref, msk_ref, y_ref, ss_ref, st_sc = refs[4 * npr:]

        @pl.when(pl.program_id(1) == 0)
        def _():
            st_sc[...] = jnp.zeros_like(st_sc)

        lb_all = _sigmoid(lg_ref[0:1, :] - lg_ref[1:2, :])

        def raw(seg, hh, r0):
            per = HG_PBLOCK // HEAD
            return p_refs[seg * npr + hh // per][pl.ds(r0, ch), (hh % per) * HEAD:(hh % per + 1) * HEAD]

        def one_head(hh, c, r0):
            ls = slice(hh * HEAD, (hh + 1) * HEAD)
            q_raw, z, v, g_raw, st = raw(0, hh, r0), raw(1, hh, r0), raw(2, hh, r0), raw(3, hh, r0), st_sc[hh]
            q, k, f, _ = _hgrn_gates(q_raw, z, lb_all[:, ls])
            yield
            b, fx, lev = _hgrn_exponents(mst_ref[...], jnp.log(f))
            yield
            qe = q * jnp.exp(b)
            a, _ = yield from _hgrn_intra(q, k, lev, msk_ref, n_lev, ch)
            v16 = v.astype(BF16)
            o = _dot_nt(qe.astype(BF16), st.astype(BF16)) + _dot(a.astype(BF16), v16)
            kl = k * jnp.exp(fx)
            st_new = st * jnp.exp(b[ch - 1:ch, :]) + _dot_tn(v16, kl.astype(BF16))
            yield
            rstd = lax.rsqrt(jnp.mean(o * o, axis=-1, keepdims=True) + EPS)
            return st, st_new, o * rstd * gn_ref[...] * (g_raw * _sigmoid(g_raw))

        def chunk(c, carry):
            r0 = pl.multiple_of(c * ch, ch)
            results = _lockstep([one_head(hh, c, r0) for hh in range(hp)])
            for hh, (st, st_new, y) in enumerate(results):
                ss_ref[hh, c] = st
                st_sc[hh] = st_new
                y_ref[pl.ds(r0, ch), hh * HEAD:(hh + 1) * HEAD] = y
            return carry

        lax.fori_loop(0, nct, chunk, 0)

    def pspec(seg, part):
        return pl.BlockSpec((tm, HG_PBLOCK),
                            lambda h, i: (i, (base + seg * HG_HEADS) * HEAD // HG_PBLOCK + h * npr + part))

    return pl.pallas_call(
        body, name="hgrn_fwd",
        grid=(HG_HEADS // hp, nt),
        in_specs=[pspec(seg, part) for seg in range(4) for part in range(npr)]
        + [pl.BlockSpec((2, wide), lambda h, i: (0, h)),
           pl.BlockSpec((1, HEAD), lambda h, i: (0, 0)),
           pl.BlockSpec((nr, ch), lambda h, i: (0, 0)),
           pl.BlockSpec((n_lev + 1, ch, ch), lambda h, i: (0, 0, 0))],
        out_specs=[pl.BlockSpec((tm, wide), lambda h, i: (i, h)),
                   pl.BlockSpec((hp, nct, HEAD, HEAD), lambda h, i: (h, i, 0, 0))],
        out_shape=[jax.ShapeDtypeStruct((tp, D_HGRN), F32),
                   jax.ShapeDtypeStruct((HG_HEADS, tp // ch, HEAD, HEAD), F32)],
        scratch_shapes=[pltpu.VMEM((hp, HEAD, HEAD), F32)],
        compiler_params=_cparams(2),
    )(*([p] * (4 * npr)), lb_logits, gnorm, mst, msk)


def _hgrn_bwd(p, dyb, states, lb_logits, gnorm, mst, msk, n_lev, tm):
    tp = p.shape[0]
    ch = HG_CHUNK
    nct = tm // ch
    nt = tp // tm
    nr = mst.shape[0]
    base = D_POOL // HEAD

    hp = HG_HEADS_PER_STEP
    wide = hp * HEAD
    npr = wide // HG_PBLOCK

    def body(*refs):
        p_refs = refs[:4 * npr]
        (dy_ref, ss_ref, lg_ref, gn_ref, mst_ref, msk_ref,
         dq_ref, dz_ref, dv_ref, dg_ref, dlg_ref, dgn_ref, dst_sc, dlb_sc) = refs[4 * npr:]
        ti = pl.program_id(1)

        def raw(seg, hh, r0):
            per = HG_PBLOCK // HEAD
            return p_refs[seg * npr + hh // per][pl.ds(r0, ch), (hh % per) * HEAD:(hh % per + 1) * HEAD]

        @pl.when(ti == 0)
        def _():
            dst_sc[...] = jnp.zeros_like(dst_sc)
            dlb_sc[...] = jnp.zeros_like(dlb_sc)
            dgn_ref[...] = jnp.zeros_like(dgn_ref)

        lb_all = _sigmoid(lg_ref[0:1, :] - lg_ref[1:2, :])
        gn = gn_ref[...]

        def load_head(hh, c, r0):
            ls = slice(hh * HEAD, (hh + 1) * HEAD)
            return (raw(0, hh, r0), raw(1, hh, r0), raw(2, hh, r0), raw(3, hh, r0),
                    dy_ref[pl.ds(r0, ch), ls], ss_ref[hh, c], dst_sc[hh])

        def store_head(hh, r0, res):
            ls = slice(hh * HEAD, (hh + 1) * HEAD)
            dq_raw, dz, dv, dg_raw, dgn, dst_new, dlb = res
            dq_ref[pl.ds(r0, ch), ls] = dq_raw
            dz_ref[pl.ds(r0, ch), ls] = dz
            dv_ref[pl.ds(r0, ch), ls] = dv
            dg_ref[pl.ds(r0, ch), ls] = dg_raw
            dgn_ref[hh] += dgn
            dst_sc[hh] = dst_new
            dlb_sc[:, ls] += dlb

        def one_head(hh, loaded):
            ls = slice(hh * HEAD, (hh + 1) * HEAD)
            lb = lb_all[:, ls]
            q_raw, z, v, g_raw, dy, st, dst = loaded
            q, k, f, sz = _hgrn_gates(q_raw, z, lb)
            yield
            b, fx, lev = _hgrn_exponents(mst_ref[...], jnp.log(f))
            yield
            eb = jnp.exp(b)
            ef = jnp.exp(fx)
            elast = jnp.exp(b[ch - 1:ch, :])
            qe = q * eb
            kl = k * ef
            a, ops = yield from _hgrn_intra(q, k, lev, msk_ref, n_lev, ch)
            v16 = v.astype(BF16)
            st16 = st.astype(BF16)
            qe16 = qe.astype(BF16)
            kl16 = kl.astype(BF16)
            a16 = a.astype(BF16)
            o = _dot_nt(qe16, st16) + _dot(a16, v16)
            yield
            sg = _sigmoid(g_raw)
            rstd = lax.rsqrt(jnp.mean(o * o, axis=-1, keepdims=True) + EPS)
            oh = o * rstd
            dg_out = (dy * oh * gn * (sg * (1.0 + g_raw * (1.0 - sg)))).astype(BF16)
            don = dy * (g_raw * sg)
            dgn = _colsum(don * oh)
            doh = don * gn
            do = rstd * (doh - oh * jnp.mean(doh * oh, axis=-1, keepdims=True))
            do16 = do.astype(BF16)
            dst16 = dst.astype(BF16)
            yield
            dv = _dot_tn(a16, do16) + _dot_nt(kl16, dst16)
            da = msk_ref[n_lev] * _dot_nt(do16, v16)
            dqe = _dot(do16, st16)
            dkl = _dot(v16, dst16)
            dst_new = dst * elast + _dot_tn(do16, qe16)
            yield
            db_last = _colsum(dst * st) * elast
            dad = jnp.sum(do * v, axis=1, keepdims=True)
            dq = dad * k + dqe * eb
            dk = dad * q + dkl * ef
            lev_grads = []
            for lv in range(n_lev):
                eq, ek, qd, kd = ops[lv]
                gl = (msk_ref[lv] * da).astype(BF16)
                dqd = _dot(gl, kd.astype(BF16))
                dkd = _dot_tn(gl, qd.astype(BF16))
                dq = dq + dqd * eq
                dk = dk + dkd * ek
                lev_grads.append((dqd * qd, dkd * kd))
                yield
            dlogf = _hgrn_exponents_bwd(mst_ref[...], dqe * qe, dkl * kl, db_last, lev_grads)
            yield
            sq = _sigmoid(q_raw)
            dq_out = (dq * (sq * (1.0 + q_raw * (1.0 - sq)))).astype(BF16)
            dfk = dlogf / f - dk
            dz_out = (dfk * (1.0 - lb) * sz * (1.0 - sz)).astype(BF16)
            return dq_out, dz_out, dv.astype(BF16), dg_out, dgn, dst_new, _colsum(dfk * (1.0 - sz))

        def chunk(cc, carry):
            c = nct - 1 - cc
            r0 = pl.multiple_of(c * ch, ch)
            loaded = [load_head(hh, c, r0) for hh in range(HG_HEADS_PER_STEP)]
            results = _lockstep([one_head(hh, loaded[hh]) for hh in range(HG_HEADS_PER_STEP)])
            for hh in range(HG_HEADS_PER_STEP):
                store_head(hh, r0, results[hh])
            return carry

        lax.fori_loop(0, nct, chunk, 0, unroll=1)

        @pl.when(ti == nt - 1)
        def _():
            dl0 = dlb_sc[...] * lb_all * (1.0 - lb_all)
            dlg_ref[0:1, :] = dl0
            dlg_ref[1:2, :] = -dl0

    def pspec(seg, part):
        return pl.BlockSpec((tm, HG_PBLOCK), lambda h, i: (
            nt - 1 - i, (base + seg * HG_HEADS) * HEAD // HG_PBLOCK + h * npr + part))

    ospec = pl.BlockSpec((tm, wide), lambda h, i: (nt - 1 - i, h))
    return pl.pallas_call(
        body, name="hgrn_bwd",
        grid=(HG_HEADS // hp, nt),
        in_specs=[pspec(seg, part) for seg in range(4) for part in range(npr)]
        + [ospec, pl.BlockSpec((hp, nct, HEAD, HEAD), lambda h, i: (h, nt - 1 - i, 0, 0)),
           pl.BlockSpec((2, wide), lambda h, i: (0, h)),
           pl.BlockSpec((1, HEAD), lambda h, i: (0, 0)),
           pl.BlockSpec((nr, ch), lambda h, i: (0, 0)),
           pl.BlockSpec((n_lev + 1, ch, ch), lambda h, i: (0, 0, 0))],
        out_specs=[ospec, ospec, ospec, ospec,
                   pl.BlockSpec((2, wide), lambda h, i: (0, h)),
                   pl.BlockSpec((hp, 1, HEAD), lambda h, i: (h, 0, 0))],
        out_shape=[jax.ShapeDtypeStruct((tp, D_HGRN), BF16)] * 4
        + [jax.ShapeDtypeStruct((2, D_HGRN), F32), jax.ShapeDtypeStruct((HG_HEADS, 1, HEAD), F32)],
        scratch_shapes=[pltpu.VMEM((hp, HEAD, HEAD), F32), pltpu.VMEM((1, wide), F32)],
        compiler_params=_cparams(2),
    )(*([p] * (4 * npr)), dyb, states, lb_logits, gnorm, mst, msk)


def _conv_taps(x, w_ref, tr, halo, width):
    acc = None
    for j in range(width):
        sh = width - 1 - j
        xs = x if sh == 0 else pltpu.roll(x, sh, 0)
        term = xs[halo:, :] * w_ref[j:j + 1, :]
        acc = term if acc is None else acc + term
    return acc


def _conv_taps_t(y, w_ref, tr, halo, width):
    n = tr + halo
    acc = None
    for j in range(width):
        sh = width - 1 - j
        ys = y if sh == 0 else pltpu.roll(y, n - sh, 0)
        term = ys[0:tr, :] * w_ref[j:j + 1, :]
        acc = term if acc is None else acc + term
    return acc


def _ln_stats(cv):
    mu = jnp.mean(cv, axis=-1, keepdims=True)
    xc = cv - mu
    rstd = lax.rsqrt(jnp.mean(xc * xc, axis=-1, keepdims=True) + EPS)
    return rstd, xc * rstd


def _convmod_fwd(p, w, bias, ln_g, ln_b, tr):
    tp = p.shape[0]
    nt = tp // tr
    nb = D_CONV // HEAD

    def body(a_ref, b_ref, w_ref, bi_ref, g_ref, be_ref, y_ref, usc):
        usc[0:CONV_HALO, :] = jnp.zeros((CONV_HALO, HEAD), F32)
        usc[CONV_HALO:, :] = a_ref[...] * _sigmoid(b_ref[...])

        def tile(r, carry):
            r0 = pl.multiple_of(r * tr, SUBLANE)
            x = usc[pl.ds(r0, tr + CONV_HALO), :]
            cv = _conv_taps(x, w_ref, tr, CONV_HALO, CONV_WIDTH) + bi_ref[...]
            _, xh = _ln_stats(cv)
            un = xh * g_ref[...] + be_ref[...]
            y_ref[pl.ds(r0, tr), :] = un * _sigmoid(un)
            return carry

        lax.fori_loop(0, nt, tile, 0)

    vec = lambda: pl.BlockSpec((1, HEAD), lambda j: (0, j))
    return pl.pallas_call(
        body, name="convmod_fwd",
        grid=(nb,),
        in_specs=[pl.BlockSpec((tp, HEAD), lambda j: (0, j)), pl.BlockSpec((tp, HEAD), lambda j: (0, nb + j)),
                  pl.BlockSpec((CONV_HALO, HEAD), lambda j: (0, j)), vec(), vec(), vec()],
        out_specs=pl.BlockSpec((tp, HEAD), lambda j: (0, j)),
        out_shape=jax.ShapeDtypeStruct((tp, D_CONV), F32),
        scratch_shapes=[pltpu.VMEM((tp + CONV_HALO, HEAD), F32)],
        compiler_params=_cparams(1),
    )(p, p, w, bias, ln_g, ln_b)


def _convmod_bwd(p, dyc, w, bias, ln_g, ln_b, tr):
    tp = p.shape[0]
    nt = tp // tr
    nb = D_CONV // HEAD

    def body(a_ref, b_ref, dy_ref, w_ref, bi_ref, g_ref, be_ref, da_ref, db_ref, dw_ref, dv_ref, usc, dsc):
        usc[0:CONV_HALO, :] = jnp.zeros((CONV_HALO, HEAD), F32)
        usc[CONV_HALO:, :] = a_ref[...] * _sigmoid(b_ref[...])
        dsc[tp:, :] = jnp.zeros((CONV_HALO, HEAD), F32)
        dw_ref[...] = jnp.zeros_like(dw_ref)
        dv_ref[...] = jnp.zeros_like(dv_ref)

        def tile1(r, carry):
            r0 = pl.multiple_of(r * tr, SUBLANE)
            x = usc[pl.ds(r0, tr + CONV_HALO), :]
            cv = _conv_taps(x, w_ref, tr, CONV_HALO, CONV_WIDTH) + bi_ref[...]
            rstd, xh = _ln_stats(cv)
            un = xh * g_ref[...] + be_ref[...]
            sg = _sigmoid(un)
            dun = dy_ref[pl.ds(r0, tr), :] * (sg * (1.0 + un * (1.0 - sg)))
            dv_ref[0, 1:2, :] += _colsum(dun * xh)
            dv_ref[0, 2:3, :] += _colsum(dun)
            dxh = dun * g_ref[...]
            dcv = rstd * (dxh - jnp.mean(dxh, axis=-1, keepdims=True)
                          - xh * jnp.mean(dxh * xh, axis=-1, keepdims=True))
            dv_ref[0, 0:1, :] += _colsum(dcv)
            for j in range(CONV_WIDTH):
                sh = CONV_WIDTH - 1 - j
                xs = x if sh == 0 else pltpu.roll(x, sh, 0)
                dw_ref[0, j:j + 1, :] += _colsum(dcv * xs[CONV_HALO:, :])
            dsc[pl.ds(r0, tr), :] = dcv
            return carry

        lax.fori_loop(0, nt, tile1, 0)

        def tile2(r, carry):
            r0 = pl.multiple_of(r * tr, SUBLANE)
            y = dsc[pl.ds(r0, tr + CONV_HALO), :]
            du = _conv_taps_t(y, w_ref, tr, CONV_HALO, CONV_WIDTH)
            a = a_ref[pl.ds(r0, tr), :]
            sb = _sigmoid(b_ref[pl.ds(r0, tr), :])
            da_ref[pl.ds(r0, tr), :] = (du * sb).astype(BF16)
            db_ref[pl.ds(r0, tr), :] = (du * a * sb * (1.0 - sb)).astype(BF16)
            return carry

        lax.fori_loop(0, nt, tile2, 0)

    vec = lambda: pl.BlockSpec((1, HEAD), lambda j: (0, j))
    col = lambda: pl.BlockSpec((tp, HEAD), lambda j: (0, j))
    return pl.pallas_call(
        body, name="convmod_bwd",
        grid=(nb,),
        in_specs=[col(), pl.BlockSpec((tp, HEAD), lambda j: (0, nb + j)), col(),
                  pl.BlockSpec((CONV_HALO, HEAD), lambda j: (0, j)), vec(), vec(), vec()],
        out_specs=[col(), col(), pl.BlockSpec((1, CONV_HALO, HEAD), lambda j: (j, 0, 0)),
                   pl.BlockSpec((1, SUBLANE, HEAD), lambda j: (j, 0, 0))],
        out_shape=[jax.ShapeDtypeStruct((tp, D_CONV), BF16), jax.ShapeDtypeStruct((tp, D_CONV), BF16),
                   jax.ShapeDtypeStruct((nb, CONV_HALO, HEAD), F32), jax.ShapeDtypeStruct((nb, SUBLANE, HEAD), F32)],
        scratch_shapes=[pltpu.VMEM((tp + CONV_HALO, HEAD), F32), pltpu.VMEM((tp + CONV_HALO, HEAD), F32)],
        compiler_params=_cparams(1),
    )(p, p, dyc, w, bias, ln_g, ln_b)


def _log1p_small(y):
    return jnp.where(y < 1e-4, y * (1.0 - 0.5 * y), jnp.log(1.0 + y))


def _softplus(x):
    return jnp.maximum(x, 0.0) + _log1p_small(jnp.exp(-jnp.abs(x)))


def _expm1(x):
    return jnp.where(jnp.abs(x) < 1e-2, x * (1.0 + 0.5 * x * (1.0 + x * (1.0 / 3.0))), jnp.exp(x) - 1.0)


def _gelu_parts(x):
    c = 0.7978845608028654
    inner = c * (x + 0.044715 * x * x * x)
    th = jnp.tanh(inner)
    gelu = 0.5 * x * (1.0 + th)
    dgelu = 0.5 * (1.0 + th) + 0.5 * x * (1.0 - th * th) * c * (1.0 + 3.0 * 0.044715 * x * x)
    return gelu, dgelu


def _lru_gates(x_all, tp, cw_ref, cb_ref, wa_ref, ba_ref, wx_ref, bx_ref, lam_ref):
    u = _conv_taps(x_all, cw_ref, tp, LRU_HALO, LRU_CONV) + cb_ref[...]
    u16 = u.astype(BF16)
    r = _sigmoid(_dot(u16, wa_ref[0]) + ba_ref[...])
    i = _sigmoid(_dot(u16, wx_ref[0]) + bx_ref[...])
    sp = _softplus(-lam_ref[...])
    la = -LRU_C * r * sp
    a = jnp.exp(la)
    mult = jnp.sqrt(-_expm1(2.0 * la))
    return u, r, i, a, mult, sp


def _lru_specs(tp, nb):
    col = lambda k: pl.BlockSpec((tp, HEAD), functools.partial(lambda j, k: (0, k * nb + j), k=k))
    vec = lambda: pl.BlockSpec((1, HEAD), lambda j: (0, j))
    mat = lambda: pl.BlockSpec((1, HEAD, HEAD), lambda j: (j, 0, 0))
    return col, vec, mat


def _lru_fwd(p, cw, cb, wa, ba, wx, bx, lam):
    tp = p.shape[0]
    nb = D_LRU // HEAD
    ng = tp // SUBLANE

    def body(x_ref, gt_ref, cw_ref, cb_ref, wa_ref, ba_ref, wx_ref, bx_ref, lam_ref, y_ref, hs_ref,
             xsc, asc, bsc):
        xsc[0:LRU_HALO, :] = jnp.zeros((LRU_HALO, HEAD), F32)
        xsc[LRU_HALO:, :] = x_ref[...]
        u, r, i, a, mult, _ = _lru_gates(xsc[...], tp, cw_ref, cb_ref, wa_ref, ba_ref, wx_ref, bx_ref, lam_ref)
        rows = lax.broadcasted_iota(jnp.int32, (tp, HEAD), 0)
        b = jnp.where(rows == 0, 1.0, mult) * (i * u)
        sub = rows % SUBLANE
        for k in (1, 2, 4):
            m = sub >= k
            b = jnp.where(m, a * pltpu.roll(b, k, 0) + b, b)
            a = jnp.where(m, a * pltpu.roll(a, k, 0), a)
        asc[...] = a
        bsc[...] = b

        def grp(g, carry):
            r0 = pl.multiple_of(g * SUBLANE, SUBLANE)
            h = bsc[pl.ds(r0, SUBLANE), :] + asc[pl.ds(r0, SUBLANE), :] * carry
            hs_ref[pl.ds(r0, SUBLANE), :] = h
            return jnp.broadcast_to(h[SUBLANE - 1:SUBLANE, :], (SUBLANE, HEAD))

        lax.fori_loop(0, ng, grp, jnp.zeros((SUBLANE, HEAD), F32))
        gelu, _ = _gelu_parts(gt_ref[...])
        y_ref[...] = gelu * hs_ref[...]

    col, vec, mat = _lru_specs(tp, nb)
    return pl.pallas_call(
        body, name="lru_fwd",
        grid=(nb,),
        in_specs=[col(2), col(3), pl.BlockSpec((LRU_CONV, HEAD), lambda j: (0, j)), vec(), mat(), vec(), mat(),
                  vec(), vec()],
        out_specs=[pl.BlockSpec((tp, HEAD), lambda j: (0, j)), pl.BlockSpec((tp, HEAD), lambda j: (0, j))],
        out_shape=[jax.ShapeDtypeStruct((tp, D_LRU), F32), jax.ShapeDtypeStruct((tp, D_LRU), F32)],
        scratch_shapes=[pltpu.VMEM((tp + LRU_HALO, HEAD), F32), pltpu.VMEM((tp, HEAD), F32),
                        pltpu.VMEM((tp, HEAD), F32)],
        compiler_params=_cparams(1),
    )(p, p, cw, cb, wa, ba, wx, bx, lam)


def _lru_bwd(p, hs, dyd, cw, cb, wa, ba, wx, bx, lam):
    tp = p.shape[0]
    nb = D_LRU // HEAD
    ng = tp // SUBLANE

    def body(x_ref, gt_ref, hs_ref, dy_ref, cw_ref, cb_ref, wa_ref, ba_ref, wx_ref, bx_ref, lam_ref,
             dx_ref, dgt_ref, dwa_ref, dwx_ref, dv_ref, xsc, asc, bsc, gsc, dusc):
        xsc[0:LRU_HALO, :] = jnp.zeros((LRU_HALO, HEAD), F32)
        xsc[LRU_HALO:, :] = x_ref[...]
        x_all = xsc[...]
        u, r, i, a, mult, sp = _lru_gates(x_all, tp, cw_ref, cb_ref, wa_ref, ba_ref, wx_ref, bx_ref, lam_ref)
        rows = lax.broadcasted_iota(jnp.int32, (tp, HEAD), 0)
        hs = hs_ref[...]
        dy = dy_ref[...]
        gelu, dgelu = _gelu_parts(gt_ref[...])
        dgt_ref[...] = (dy * hs * dgelu).astype(BF16)
        bb = dy * gelu
        aa = jnp.where(rows == tp - 1, 0.0, pltpu.roll(a, tp - 1, 0))
        sub = rows % SUBLANE
        for k in (1, 2, 4):
            m = sub < SUBLANE - k
            bb = jnp.where(m, aa * pltpu.roll(bb, tp - k, 0) + bb, bb)
            aa = jnp.where(m, aa * pltpu.roll(aa, tp - k, 0), aa)
        asc[...] = aa
        bsc[...] = bb

        def grp(gi, carry):
            g = ng - 1 - gi
            r0 = pl.multiple_of(g * SUBLANE, SUBLANE)
            gg = bsc[pl.ds(r0, SUBLANE), :] + asc[pl.ds(r0, SUBLANE), :] * carry
            gsc[pl.ds(r0, SUBLANE), :] = gg
            return jnp.broadcast_to(gg[0:1, :], (SUBLANE, HEAD))

        lax.fori_loop(0, ng, grp, jnp.zeros((SUBLANE, HEAD), F32))
        g = gsc[...]
        first = rows == 0
        hprev = jnp.where(first, 0.0, pltpu.roll(hs, 1, 0))
        iu = i * u
        d_iu = g * jnp.where(first, 1.0, mult)
        dmult_term = jnp.where(first, 0.0, g * iu * (-(a * a) / mult))
        dla = g * hprev * a + dmult_term
        dr = dla * (-LRU_C) * sp
        dv_ref[0, 7:8, :] = _colsum(dla * (LRU_C * r) * _sigmoid(-lam_ref[...]))
        dpr = dr * r * (1.0 - r)
        dpi = d_iu * u * i * (1.0 - i)
        dv_ref[0, 5:6, :] = _colsum(dpr)
        dv_ref[0, 6:7, :] = _colsum(dpi)
        u16 = u.astype(BF16)
        dpr16 = dpr.astype(BF16)
        dpi16 = dpi.astype(BF16)
        dwa_ref[0] = _dot_tn(u16, dpr16)
        dwx_ref[0] = _dot_tn(u16, dpi16)
        du = d_iu * i + _dot_nt(dpr16, wa_ref[0]) + _dot_nt(dpi16, wx_ref[0])
        dv_ref[0, 4:5, :] = _colsum(du)
        for j in range(LRU_CONV):
            sh = LRU_CONV - 1 - j
            xs = x_all if sh == 0 else pltpu.roll(x_all, sh, 0)
            dv_ref[0, j:j + 1, :] = _colsum(du * xs[LRU_HALO:, :])
        dusc[0:tp, :] = du
        dusc[tp:, :] = jnp.zeros((LRU_HALO, HEAD), F32)
        dx_ref[...] = _conv_taps_t(dusc[...], cw_ref, tp, LRU_HALO, LRU_CONV).astype(BF16)

    col, vec, mat = _lru_specs(tp, nb)
    ocol = lambda: pl.BlockSpec((tp, HEAD), lambda j: (0, j))
    return pl.pallas_call(
        body, name="lru_bwd",
        grid=(nb,),
        in_specs=[col(2), col(3), ocol(), ocol(), pl.BlockSpec((LRU_CONV, HEAD), lambda j: (0, j)), vec(), mat(),
                  vec(), mat(), vec(), vec()],
        out_specs=[ocol(), ocol(), mat(), mat(), pl.BlockSpec((1, SUBLANE, HEAD), lambda j: (j, 0, 0))],
        out_shape=[jax.ShapeDtypeStruct((tp, D_LRU), BF16), jax.ShapeDtypeStruct((tp, D_LRU), BF16),
                   jax.ShapeDtypeStruct((nb, HEAD, HEAD), F32), jax.ShapeDtypeStruct((nb, HEAD, HEAD), F32),
                   jax.ShapeDtypeStruct((nb, SUBLANE, HEAD), F32)],
        scratch_shapes=[pltpu.VMEM((tp + LRU_HALO, HEAD), F32), pltpu.VMEM((tp, HEAD), F32),
                        pltpu.VMEM((tp, HEAD), F32), pltpu.VMEM((tp, HEAD), F32),
                        pltpu.VMEM((tp + LRU_HALO, HEAD), F32)],
        compiler_params=_cparams(1),
    )(p, p, hs, dyd, cw, cb, wa, ba, wx, bx, lam)


def _mesh_pos():
    return lax.axis_index("x"), lax.axis_index("y"), lax.axis_index("c")


def _other_chips(x, y):
    return [(1 - x, y), (x, 1 - y), (1 - x, 1 - y)]


ANY = pl.BlockSpec(memory_space=pl.ANY)


def _allgather_shards(arrs, split):
    n = len(arrs)

    def body(*refs):
        ins, outs = refs[:n], refs[n:2 * n]
        send1, recv1, send2, recv2, send3, recv3 = refs[2 * n:]
        x, y, c = _mesh_pos()
        chip = 2 * x + y
        sibling = (x, y, 1 - c)
        others = _other_chips(x, y)

        def rows(k, cc):
            half = arrs[k].shape[0] // 2
            return pl.ds(cc * half, half)

        def remote(src, dst, ssem, rsem, dev):
            return pltpu.make_async_remote_copy(src_ref=src, dst_ref=dst, send_sem=ssem, recv_sem=rsem,
                                                device_id=dev, device_id_type=MESH)

        started = [remote(ins[k], outs[k].at[chip], send3.at[k], recv3.at[k], sibling) for k in range(n)]
        for cp in started:
            cp.start()
        for k in range(n):
            for j, (ox, oy) in enumerate(others):
                if split[k]:
                    src, dst = ins[k].at[rows(k, c)], outs[k].at[chip, rows(k, c)]
                else:
                    src, dst = ins[k], outs[k].at[chip]
                cp = remote(src, dst, send1.at[3 * k + j], recv1.at[3 * k + j], (ox, oy, c))
                cp.start()
                started.append(cp)
        for j, (ox, oy) in enumerate(others):
            ochip = 2 * ox + oy
            for k in range(n):
                if split[k]:
                    blk = outs[k].at[ochip, rows(k, c)]
                    remote(blk, blk, send1.at[3 * k + j], recv1.at[3 * k + j], sibling).wait_recv()
                    cp = remote(blk, blk, send2.at[3 * k + j], recv2.at[3 * k + j], sibling)
                    cp.start()
                    started.append(cp)
                else:
                    blk = outs[k].at[ochip]
                    remote(blk, blk, send1.at[3 * k + j], recv1.at[3 * k + j], sibling).wait_recv()
        for j, (ox, oy) in enumerate(others):
            ochip = 2 * ox + oy
            for k in range(n):
                if split[k]:
                    blk = outs[k].at[ochip, rows(k, 1 - c)]
                    remote(blk, blk, send2.at[3 * k + j], recv2.at[3 * k + j], sibling).wait_recv()
        for k in range(n):
            blk = outs[k].at[chip]
            remote(blk, blk, send3.at[k], recv3.at[k], sibling).wait_recv()
        for cp in started:
            cp.wait_send()

    return pl.pallas_call(
        body, name="allgather_shards",
        in_specs=[ANY] * n, out_specs=[ANY] * n,
        out_shape=[jax.ShapeDtypeStruct((N_SHARD,) + a.shape, a.dtype) for a in arrs],
        scratch_shapes=[pltpu.SemaphoreType.DMA((3 * n,)), pltpu.SemaphoreType.DMA((3 * n,)),
                        pltpu.SemaphoreType.DMA((3 * n,)), pltpu.SemaphoreType.DMA((3 * n,)),
                        pltpu.SemaphoreType.DMA((n,)), pltpu.SemaphoreType.DMA((n,))],
    )(*arrs)


HBM = pl.BlockSpec(memory_space=pltpu.HBM)
SEM = pl.BlockSpec(memory_space=pltpu.SEMAPHORE)
DATAFLOW = pltpu.SideEffectType.DATAFLOW_SIDE_EFFECTING
N_PEERS = 4


def _in_hbm(a):
    return pltpu.with_memory_space_constraint(a, pltpu.HBM)


def _gather_peers(x, y, c):
    return [((ox, oy, c), 2 * ox + oy) for ox, oy in _other_chips(x, y)] + [((x, y, 1 - c), 2 * x + y)]


def _gather_refs(src, land, slot, c, split):
    if not split:
        return src, land.at[slot]
    half = src.shape[0] // 2
    return src.at[pl.ds(c * half, half)], land.at[slot, pl.ds(c * half, half)]


def _gather_start(arrs, split):
    n = len(arrs)

    def body(*refs):
        ins, lands = refs[:n], refs[n:2 * n]
        ssem, rsem = refs[2 * n:2 * n + 2]
        token = refs[-1]
        x, y, c = _mesh_pos()
        chip = 2 * x + y
        for k in range(n):
            for j, (dev, _) in enumerate(_gather_peers(x, y, c)):
                src, dst = _gather_refs(ins[k], lands[k], chip, c, split[k] and j < N_PEERS - 1)
                pltpu.make_async_remote_copy(
                    src_ref=src, dst_ref=dst, send_sem=ssem.at[N_PEERS * k + j],
                    recv_sem=rsem.at[N_PEERS * k + j], device_id=dev, device_id_type=MESH).start()
        token[...] = jnp.zeros_like(token)

    lands = [_in_hbm(lax.empty((N_SHARD,) + a.shape, a.dtype)) for a in arrs]
    out = pl.pallas_call(
        body, name="gather_start",
        in_specs=[HBM] * (2 * n),
        out_specs=[SEM, SEM] + [HBM] * (2 * n) + [pl.BlockSpec(memory_space=pltpu.VMEM)],
        out_shape=[pltpu.SemaphoreType.DMA((N_PEERS * n,)), pltpu.SemaphoreType.DMA((N_PEERS * n,))]
        + [pltpu.HBM(a.shape, a.dtype) for a in arrs]
        + [pltpu.HBM((N_SHARD,) + a.shape, a.dtype) for a in arrs]
        + [jax.ShapeDtypeStruct((SUBLANE, LANE), F32)],
        input_output_aliases={k: 2 + k for k in range(2 * n)},
        compiler_params=pltpu.CompilerParams(has_side_effects=DATAFLOW),
    )(*[_in_hbm(a) for a in arrs], *lands)
    return out[0], out[1], list(out[2:2 + n]), list(out[2 + n:2 + 2 * n]), out[-1]


def _gather_wait(ssem, rsem, srcs, lands, ks, after, split=False):
    n = len(ks)

    def body(*refs):
        ins, lnd = refs[:n], refs[n:2 * n]
        ssem_ref, rsem_ref = refs[2 * n:2 * n + 2]
        x, y, c = _mesh_pos()
        for i, k in enumerate(ks):
            for j, (dev, pchip) in enumerate(_gather_peers(x, y, c)):
                src, dst = _gather_refs(ins[i], lnd[i], pchip, c, split and j < N_PEERS - 1)
                cp = pltpu.make_async_remote_copy(
                    src_ref=src, dst_ref=dst, send_sem=ssem_ref.at[N_PEERS * k + j],
                    recv_sem=rsem_ref.at[N_PEERS * k + j], device_id=dev, device_id_type=MESH)
                cp.wait_send()
                cp.wait_recv()

    out = pl.pallas_call(
        body, name="gather_wait",
        in_specs=[HBM] * (2 * n) + [SEM, SEM] + [ANY] * len(after),
        out_specs=[HBM] * (2 * n),
        out_shape=[pltpu.HBM(a.shape, a.dtype) for a in srcs] + [pltpu.HBM(a.shape, a.dtype) for a in lands],
        input_output_aliases={k: k for k in range(2 * n)},
        compiler_params=pltpu.CompilerParams(has_side_effects=DATAFLOW),
    )(*srcs, *lands, ssem, rsem, *after)
    return list(out[n:])


def _pair_forward(lands):
    n = len(lands)

    def body(*refs):
        outs = refs[n:2 * n]
        ssem, rsem = refs[2 * n:]
        x, y, c = _mesh_pos()
        sibling = (x, y, 1 - c)
        cps = []
        for k in range(n):
            half = lands[k].shape[1] // 2
            for j, (ox, oy) in enumerate(_other_chips(x, y)):
                mine = outs[k].at[2 * ox + oy, pl.ds(c * half, half)]
                cp = pltpu.make_async_remote_copy(src_ref=mine, dst_ref=mine, send_sem=ssem.at[3 * k + j],
                                                  recv_sem=rsem.at[3 * k + j], device_id=sibling, device_id_type=MESH)
                cp.start()
                cps.append(cp)
        for k in range(n):
            half = lands[k].shape[1] // 2
            for j, (ox, oy) in enumerate(_other_chips(x, y)):
                theirs = outs[k].at[2 * ox + oy, pl.ds((1 - c) * half, half)]
                pltpu.make_async_remote_copy(src_ref=theirs, dst_ref=theirs, send_sem=ssem.at[3 * k + j],
                                             recv_sem=rsem.at[3 * k + j], device_id=sibling,
                                             device_id_type=MESH).wait_recv()
        for cp in cps:
            cp.wait_send()

    return pl.pallas_call(
        body, name="pair_forward",
        in_specs=[ANY] * n, out_specs=[ANY] * n,
        out_shape=[jax.ShapeDtypeStruct(a.shape, a.dtype) for a in lands],
        scratch_shapes=[pltpu.SemaphoreType.DMA((3 * n,)), pltpu.SemaphoreType.DMA((3 * n,))],
        input_output_aliases={k: k for k in range(n)},
    )(*lands)


N_SOURCES = 7


def _reduce_peers(x, y, c):
    peers = []
    for ox, oy in _other_chips(x, y):
        for rel in range(2):
            peers.append(((ox, oy, c + rel - 2 * c * rel), 2 * ox + oy))
    peers.append(((x, y, 1 - c), 2 * x + y))
    return peers


def _reduce_start(arrs, slots):
    n = len(arrs)

    def body(*refs):
        ins, lands = refs[:n], refs[n:2 * n]
        ssem, rsem = refs[2 * n:2 * n + 2]
        token = refs[-1]
        x, y, c = _mesh_pos()
        me = 2 * (2 * x + y) + c
        for k in range(n):
            half = arrs[k].shape[1] // 2
            for p, (dev, ochip) in enumerate(_reduce_peers(x, y, c)):
                pltpu.make_async_remote_copy(
                    src_ref=ins[k].at[ochip, pl.ds(dev[2] * half, half)], dst_ref=lands[k].at[me],
                    send_sem=ssem.at[N_SOURCES * k + p], recv_sem=rsem.at[N_SOURCES * k + p],
                    device_id=dev, device_id_type=MESH).start()
        token[...] = jnp.zeros_like(token)

    out = pl.pallas_call(
        body, name="reduce_start",
        in_specs=[HBM] * (2 * n),
        out_specs=[SEM, SEM] + [HBM] * (2 * n) + [pl.BlockSpec(memory_space=pltpu.VMEM)],
        out_shape=[pltpu.SemaphoreType.DMA((N_SOURCES * n,)), pltpu.SemaphoreType.DMA((N_SOURCES * n,))]
        + [pltpu.HBM(a.shape, a.dtype) for a in arrs] + [pltpu.HBM(a.shape, a.dtype) for a in slots]
        + [jax.ShapeDtypeStruct((SUBLANE, LANE), F32)],
        input_output_aliases={k: 2 + k for k in range(2 * n)},
        compiler_params=pltpu.CompilerParams(has_side_effects=DATAFLOW),
    )(*[_in_hbm(a) for a in arrs], *[_in_hbm(a) for a in slots])
    return out[0], out[1], list(out[2:2 + n]), list(out[2 + n:2 + 2 * n]), out[-1]


def _reduce_wait(ssem, rsem, arrs, slots, after):
    n = len(arrs)

    def body(*refs):
        ins, lnd = refs[:n], refs[n:2 * n]
        ssem_ref, rsem_ref = refs[2 * n:2 * n + 2]
        x, y, c = _mesh_pos()
        for k in range(n):
            half = arrs[k].shape[1] // 2
            for p, (dev, ochip) in enumerate(_reduce_peers(x, y, c)):
                cp = pltpu.make_async_remote_copy(
                    src_ref=ins[k].at[ochip, pl.ds(dev[2] * half, half)], dst_ref=lnd[k].at[2 * ochip + dev[2]],
                    send_sem=ssem_ref.at[N_SOURCES * k + p], recv_sem=rsem_ref.at[N_SOURCES * k + p],
                    device_id=dev, device_id_type=MESH)
                cp.wait_send()
                cp.wait_recv()

    out = pl.pallas_call(
        body, name="reduce_wait",
        in_specs=[HBM] * (2 * n) + [SEM, SEM] + [ANY] * len(after),
        out_specs=[HBM] * (2 * n),
        out_shape=[pltpu.HBM(a.shape, a.dtype) for a in arrs] + [pltpu.HBM(a.shape, a.dtype) for a in slots],
        input_output_aliases={k: k for k in range(2 * n)},
        compiler_params=pltpu.CompilerParams(has_side_effects=DATAFLOW),
    )(*arrs, *slots, ssem, rsem, *after)
    return list(out[n:])


def _own_part(arrs, chip, core, me):
    n = len(arrs)
    nb = GRAD_ROW_BLOCKS

    def body(chip_ref, core_ref, me_ref, *refs):
        for k in range(n):
            refs[n + k][...] = refs[k][...]

    def blk(a):
        return (1, a.shape[1] // 2 // nb, a.shape[2])

    grid_spec = pltpu.PrefetchScalarGridSpec(
        num_scalar_prefetch=3, grid=(nb,),
        in_specs=[pl.BlockSpec(blk(a), lambda i, ch, co, me: (ch[0], co[0] * nb + i, 0)) for a in arrs],
        out_specs=[pl.BlockSpec(blk(a), lambda i, ch, co, me: (me[0], i, 0)) for a in arrs])
    return pl.pallas_call(
        body, name="own_part", grid_spec=grid_spec,
        out_shape=[jax.ShapeDtypeStruct((N_DEV, a.shape[1] // 2, a.shape[2]), a.dtype) for a in arrs],
        compiler_params=_cparams(1),
    )(chip, core, me, *arrs)


def _sum_devices(arrs, core):
    n = len(arrs)
    nb = GRAD_ROW_BLOCKS

    def body(c_ref, *refs):
        for k in range(n):
            r = refs[k]
            acc = r[0].astype(F32)
            for dev in range(1, N_DEV):
                acc = acc + r[dev].astype(F32)
            refs[n + k][...] = acc

    grid_spec = pltpu.PrefetchScalarGridSpec(
        num_scalar_prefetch=1, grid=(nb,),
        in_specs=[pl.BlockSpec((N_DEV, a.shape[1] // nb, a.shape[2]), lambda i, c: (0, i, 0)) for a in arrs],
        out_specs=[pl.BlockSpec((a.shape[1] // nb, a.shape[2]), lambda i, c: (c[0] * nb + i, 0)) for a in arrs])
    return pl.pallas_call(
        body, name="sum_devices", grid_spec=grid_spec,
        out_shape=[jax.ShapeDtypeStruct((2 * a.shape[1], a.shape[2]), F32) for a in arrs],
        compiler_params=_cparams(1),
    )(core, *arrs)


def _small_own(v, me):
    m = v.shape[0]

    def body(me_ref, v_ref, o_ref):
        o_ref[0] = v_ref[...]

    grid_spec = pltpu.PrefetchScalarGridSpec(
        num_scalar_prefetch=1, grid=(1,),
        in_specs=[pl.BlockSpec((m, LANE), lambda i, me: (0, 0))],
        out_specs=pl.BlockSpec((1, m, LANE), lambda i, me: (me[0], 0, 0)))
    return pl.pallas_call(
        body, name="small_own", grid_spec=grid_spec,
        out_shape=jax.ShapeDtypeStruct((N_DEV, m, LANE), v.dtype),
        compiler_params=_cparams(1),
    )(me, v)


def _small_start(v, slots):
    def body(v_ref, land, ssem, rsem, v_thru, land_thru, token):
        del v_thru, land_thru
        x, y, c = _mesh_pos()
        me = 2 * (2 * x + y) + c
        for p, (dev, _) in enumerate(_reduce_peers(x, y, c)):
            pltpu.make_async_remote_copy(src_ref=v_ref, dst_ref=land.at[me], send_sem=ssem.at[p],
                                         recv_sem=rsem.at[p], device_id=dev, device_id_type=MESH).start()
        token[...] = jnp.zeros_like(token)

    out = pl.pallas_call(
        body, name="small_start",
        in_specs=[HBM, HBM],
        out_specs=[SEM, SEM, HBM, HBM, pl.BlockSpec(memory_space=pltpu.VMEM)],
        out_shape=[pltpu.SemaphoreType.DMA((N_SOURCES,)), pltpu.SemaphoreType.DMA((N_SOURCES,)),
                   pltpu.HBM(v.shape, v.dtype), pltpu.HBM(slots.shape, slots.dtype),
                   jax.ShapeDtypeStruct((SUBLANE, LANE), F32)],
        input_output_aliases={0: 2, 1: 3},
        compiler_params=pltpu.CompilerParams(has_side_effects=DATAFLOW),
    )(_in_hbm(v), _in_hbm(slots))
    return out


def _small_wait(ssem, rsem, v, slots, after):
    def body(*refs):
        v_ref, land, ssem_ref, rsem_ref = refs[:4]
        x, y, c = _mesh_pos()
        for p, (dev, ochip) in enumerate(_reduce_peers(x, y, c)):
            cp = pltpu.make_async_remote_copy(src_ref=v_ref, dst_ref=land.at[2 * ochip + dev[2]],
                                              send_sem=ssem_ref.at[p], recv_sem=rsem_ref.at[p],
                                              device_id=dev, device_id_type=MESH)
            cp.wait_send()
            cp.wait_recv()

    out = pl.pallas_call(
        body, name="small_wait",
        in_specs=[HBM, HBM, SEM, SEM] + [ANY] * len(after),
        out_specs=[HBM, HBM],
        out_shape=[pltpu.HBM(v.shape, v.dtype), pltpu.HBM(slots.shape, slots.dtype)],
        input_output_aliases={0: 0, 1: 1},
        compiler_params=pltpu.CompilerParams(has_side_effects=DATAFLOW),
    )(v, slots, ssem, rsem, *after)
    return out[1]


def _scatter_start(arrs, slots):
    n = len(arrs)

    def body(*refs):
        ins, lands = refs[:n], refs[n:2 * n]
        ssem, rsem = refs[2 * n:2 * n + 2]
        token = refs[-1]
        x, y, c = _mesh_pos()
        chip = 2 * x + y
        for k in range(n):
            for j, (ox, oy) in enumerate(_other_chips(x, y)):
                pltpu.make_async_remote_copy(
                    src_ref=ins[k].at[2 * ox + oy], dst_ref=lands[k].at[chip], send_sem=ssem.at[3 * k + j],
                    recv_sem=rsem.at[3 * k + j], device_id=(ox, oy, c), device_id_type=MESH).start()
        token[...] = jnp.zeros_like(token)

    out = pl.pallas_call(
        body, name="scatter_start",
        in_specs=[HBM] * (2 * n),
        out_specs=[SEM, SEM] + [HBM] * (2 * n) + [pl.BlockSpec(memory_space=pltpu.VMEM)],
        out_shape=[pltpu.SemaphoreType.DMA((3 * n,)), pltpu.SemaphoreType.DMA((3 * n,))]
        + [pltpu.HBM(a.shape, a.dtype) for a in arrs] + [pltpu.HBM(a.shape, a.dtype) for a in slots]
        + [jax.ShapeDtypeStruct((SUBLANE, LANE), F32)],
        input_output_aliases={k: 2 + k for k in range(2 * n)},
        compiler_params=pltpu.CompilerParams(has_side_effects=DATAFLOW),
    )(*[_in_hbm(a) for a in arrs], *[_in_hbm(a) for a in slots])
    return out[0], out[1], list(out[2:2 + n]), list(out[2 + n:2 + 2 * n]), out[-1]


def _scatter_wait(ssem, rsem, arrs, slots, after):
    n = len(arrs)

    def body(*refs):
        ins, lnd = refs[:n], refs[n:2 * n]
        ssem_ref, rsem_ref = refs[2 * n:2 * n + 2]
        x, y, c = _mesh_pos()
        for k in range(n):
            for j, (ox, oy) in enumerate(_other_chips(x, y)):
                ochip = 2 * ox + oy
                cp = pltpu.make_async_remote_copy(
                    src_ref=ins[k].at[ochip], dst_ref=lnd[k].at[ochip], send_sem=ssem_ref.at[3 * k + j],
                    recv_sem=rsem_ref.at[3 * k + j], device_id=(ox, oy, c), device_id_type=MESH)
                cp.wait_send()
                cp.wait_recv()

    out = pl.pallas_call(
        body, name="scatter_wait",
        in_specs=[HBM] * (2 * n) + [SEM, SEM] + [ANY] * len(after),
        out_specs=[HBM] * (2 * n),
        out_shape=[pltpu.HBM(a.shape, a.dtype) for a in arrs] + [pltpu.HBM(a.shape, a.dtype) for a in slots],
        input_output_aliases={k: k for k in range(2 * n)},
        compiler_params=pltpu.CompilerParams(has_side_effects=DATAFLOW),
    )(*arrs, *slots, ssem, rsem, *after)
    return list(out[n:])


def _pair_exchange_halves(arrs):
    n = len(arrs)

    def body(*refs):
        ins, outs = refs[:n], refs[n:2 * n]
        ssem, rsem = refs[2 * n:]
        x, y, c = _mesh_pos()
        cps = []
        for k in range(n):
            half = arrs[k].shape[1] // 2
            cp = pltpu.make_async_remote_copy(
                src_ref=ins[k].at[:, pl.ds((1 - c) * half, half)], dst_ref=outs[k],
                send_sem=ssem.at[k], recv_sem=rsem.at[k], device_id=(x, y, 1 - c), device_id_type=MESH)
            cp.start()
            cps.append(cp)
        for cp in cps:
            cp.wait()

    return pl.pallas_call(
        body, name="pair_exchange_halves",
        in_specs=[ANY] * n, out_specs=[ANY] * n,
        out_shape=[jax.ShapeDtypeStruct((a.shape[0], a.shape[1] // 2, a.shape[2]), a.dtype) for a in arrs],
        scratch_shapes=[pltpu.SemaphoreType.DMA((n,)), pltpu.SemaphoreType.DMA((n,))],
    )(*arrs)


GRAD_ROW_BLOCKS = 2


def _pair_add(arrs, recvd, core):
    n = len(arrs)
    nb = GRAD_ROW_BLOCKS

    def body(c_ref, *refs):
        for k in range(n):
            refs[2 * n + k][...] = (refs[k][...].astype(F32) + refs[n + k][...].astype(F32)).astype(BF16)

    def blk(a):
        return (1, a.shape[1] // 2 // nb, a.shape[2])

    grid_spec = pltpu.PrefetchScalarGridSpec(
        num_scalar_prefetch=1, grid=(N_SHARD, nb),
        in_specs=[pl.BlockSpec(blk(a), lambda s, i, c: (s, c[0] * nb + i, 0)) for a in arrs]
        + [pl.BlockSpec(blk(a), lambda s, i, c: (s, i, 0)) for a in arrs],
        out_specs=[pl.BlockSpec(blk(a), lambda s, i, c: (s, i, 0)) for a in arrs])
    return pl.pallas_call(
        body, name="pair_add", grid_spec=grid_spec,
        out_shape=[jax.ShapeDtypeStruct(r.shape, BF16) for r in recvd],
        compiler_params=_cparams(2),
    )(core, *arrs, *recvd)


def _own_slot(arrs, chip):
    n = len(arrs)
    nb = GRAD_ROW_BLOCKS

    def body(c_ref, *refs):
        for k in range(n):
            refs[n + k][...] = refs[k][...]

    def blk(a):
        return (1, a.shape[1] // nb, a.shape[2])

    grid_spec = pltpu.PrefetchScalarGridSpec(
        num_scalar_prefetch=1, grid=(nb,),
        in_specs=[pl.BlockSpec(blk(a), lambda i, c: (c[0], i, 0)) for a in arrs],
        out_specs=[pl.BlockSpec(blk(a), lambda i, c: (c[0], i, 0)) for a in arrs])
    return pl.pallas_call(
        body, name="own_slot", grid_spec=grid_spec,
        out_shape=[jax.ShapeDtypeStruct(a.shape, a.dtype) for a in arrs],
        compiler_params=_cparams(1),
    )(chip, *arrs)


def _scatter_to_owners(arrs, slots):
    n = len(arrs)

    def body(*refs):
        ins, outs = refs[:n], refs[2 * n:3 * n]
        ssem, rsem = refs[3 * n:]
        x, y, c = _mesh_pos()
        chip = 2 * x + y
        others = _other_chips(x, y)
        cps = []
        for k in range(n):
            for j, (ox, oy) in enumerate(others):
                cp = pltpu.make_async_remote_copy(
                    src_ref=ins[k].at[2 * ox + oy], dst_ref=outs[k].at[chip],
                    send_sem=ssem.at[3 * k + j], recv_sem=rsem.at[3 * k + j],
                    device_id=(ox, oy, c), device_id_type=MESH)
                cp.start()
                cps.append(cp)
        for k in range(n):
            for j, (ox, oy) in enumerate(others):
                blk = outs[k].at[2 * ox + oy]
                pltpu.make_async_remote_copy(
                    src_ref=blk, dst_ref=blk, send_sem=ssem.at[3 * k + j], recv_sem=rsem.at[3 * k + j],
                    device_id=(ox, oy, c), device_id_type=MESH).wait_recv()
        for cp in cps:
            cp.wait_send()

    return pl.pallas_call(
        body, name="scatter_to_owners",
        in_specs=[ANY] * (2 * n), out_specs=[ANY] * n,
        out_shape=[jax.ShapeDtypeStruct(a.shape, a.dtype) for a in arrs],
        scratch_shapes=[pltpu.SemaphoreType.DMA((3 * n,)), pltpu.SemaphoreType.DMA((3 * n,))],
        input_output_aliases={n + k: k for k in range(n)},
    )(*arrs, *slots)


def _sum_chips(arrs, core):
    n = len(arrs)
    nb = GRAD_ROW_BLOCKS

    def body(c_ref, *refs):
        for k in range(n):
            r = refs[k]
            refs[n + k][...] = ((r[0].astype(F32) + r[1].astype(F32)) + r[2].astype(F32)) + r[3].astype(F32)

    grid_spec = pltpu.PrefetchScalarGridSpec(
        num_scalar_prefetch=1, grid=(nb,),
        in_specs=[pl.BlockSpec((N_SHARD, a.shape[1] // nb, a.shape[2]), lambda i, c: (0, i, 0)) for a in arrs],
        out_specs=[pl.BlockSpec((a.shape[1] // nb, a.shape[2]), lambda i, c: (c[0] * nb + i, 0)) for a in arrs])
    return pl.pallas_call(
        body, name="sum_chips", grid_spec=grid_spec,
        out_shape=[jax.ShapeDtypeStruct((2 * a.shape[1], a.shape[2]), F32) for a in arrs],
        compiler_params=_cparams(1),
    )(core, *arrs)


def _pair_allgather_halves(arrs):
    n = len(arrs)

    def body(*refs):
        outs = refs[n:2 * n]
        ssem, rsem = refs[2 * n:]
        x, y, c = _mesh_pos()
        cps = []
        for k in range(n):
            h = arrs[k].shape[0] // 2
            mine = outs[k].at[pl.ds(c * h, h)]
            cp = pltpu.make_async_remote_copy(src_ref=mine, dst_ref=mine, send_sem=ssem.at[k],
                                              recv_sem=rsem.at[k], device_id=(x, y, 1 - c), device_id_type=MESH)
            cp.start()
            cps.append(cp)
        for k, cp in enumerate(cps):
            h = arrs[k].shape[0] // 2
            theirs = outs[k].at[pl.ds((1 - c) * h, h)]
            pltpu.make_async_remote_copy(src_ref=theirs, dst_ref=theirs, send_sem=ssem.at[k], recv_sem=rsem.at[k],
                                         device_id=(x, y, 1 - c), device_id_type=MESH).wait_recv()
            cp.wait_send()

    return pl.pallas_call(
        body, name="pair_allgather_halves",
        in_specs=[ANY] * n, out_specs=[ANY] * n,
        out_shape=[jax.ShapeDtypeStruct(a.shape, a.dtype) for a in arrs],
        scratch_shapes=[pltpu.SemaphoreType.DMA((n,)), pltpu.SemaphoreType.DMA((n,))],
        input_output_aliases={k: k for k in range(n)},
    )(*arrs)


N_DEV = 8


def _allgather_all(v):
    m_per, n = v.shape

    def body(x_ref, out_ref, send_sems, recv_sems, local_sem):
        x, y, c = _mesh_pos()
        me, sibling = (x, y, c), (x, y, 1 - c)
        chips = _other_chips(x, y)

        def rows(px, py, pc):
            return out_ref.at[pl.ds((4 * px + 2 * py + pc) * m_per, m_per), :]

        def copy(k, block, to, src=None):
            return pltpu.make_async_remote_copy(
                src_ref=rows(*block) if src is None else src, dst_ref=rows(*block),
                send_sem=send_sems.at[k], recv_sem=recv_sems.at[k], device_id=to, device_id_type=MESH)

        mine = pltpu.make_async_copy(x_ref, rows(*me), local_sem)
        mine.start()
        first = [copy(0, me, sibling, src=x_ref)]
        first += [copy(1 + j, me, (*chip, c), src=x_ref) for j, chip in enumerate(chips)]
        for cp in first:
            cp.start()
        passed = [copy(4 + j, (*chip, c), sibling) for j, chip in enumerate(chips)]
        for j, chip in enumerate(chips):
            copy(1 + j, (*chip, c), me).wait_recv()
            passed[j].start()
        copy(0, sibling, me).wait_recv()
        for j, chip in enumerate(chips):
            copy(4 + j, (*chip, 1 - c), me).wait_recv()
        for cp in first + passed:
            cp.wait_send()
        mine.wait()

    return pl.pallas_call(
        body, name="allgather_all",
        out_shape=jax.ShapeDtypeStruct((N_DEV * m_per, n), v.dtype),
        in_specs=[pl.BlockSpec(memory_space=pltpu.VMEM)],
        out_specs=pl.BlockSpec(memory_space=pltpu.VMEM),
        scratch_shapes=[pltpu.SemaphoreType.DMA((7,)), pltpu.SemaphoreType.DMA((7,)), pltpu.SemaphoreType.DMA],
        compiler_params=pltpu.CompilerParams(vmem_limit_bytes=VMEM_LIMIT_MB * 1024 * 1024),
    )(v)


def _adamw_math(w, g, m, v):
    m2 = ADAM_B1 * m + (1.0 - ADAM_B1) * g
    v2 = ADAM_B2 * v + (1.0 - ADAM_B2) * (g * g)
    m_hat = m2 / (1.0 - ADAM_B1 ** ADAM_STEP)
    v_hat = v2 / (1.0 - ADAM_B2 ** ADAM_STEP)
    delta = -ADAM_LR * (m_hat / (jnp.sqrt(v_hat) + ADAM_EPS) + ADAM_WD * w)
    return delta, m2, v2


def _adamw(w, m, v, gs, nblk):
    nl, r, n = w.shape
    assert nl == len(gs) and nl in (1, 2)
    br = r // nblk

    def body(w_ref, m_ref, v_ref, *rest):
        g_refs, (go_ref, d_ref, mo_ref, vo_ref) = rest[:nl], rest[nl:]
        g = g_refs[0][...]
        if nl == 2:
            g = jnp.where(pl.program_id(0) == 0, g, g_refs[1][...])
        delta, m2, v2 = _adamw_math(w_ref[0], g, m_ref[0], v_ref[0])
        go_ref[0] = g
        d_ref[0] = delta
        mo_ref[0] = m2
        vo_ref[0] = v2

    spec = pl.BlockSpec((1, br, n), lambda l, i: (l, i, 0))
    g_specs = [pl.BlockSpec((br, n), lambda l, i: (i, 0))] if nl == 1 else [
        pl.BlockSpec((br, n), lambda l, i: (jnp.where(l == 0, i, nblk - 1), 0)),
        pl.BlockSpec((br, n), lambda l, i: (jnp.where(l == 1, i, 0), 0))]
    return pl.pallas_call(
        body, name="adamw", grid=(nl, nblk),
        in_specs=[spec, spec, spec] + g_specs,
        out_specs=[spec] * 4,
        out_shape=[jax.ShapeDtypeStruct((nl, r, n), F32)] * 4,
        compiler_params=_cparams(2),
    )(w, m, v, *gs)


def _small_reduce_adamw(parts, w, m, v, rep_rows, sh_rows):
    mrows = rep_rows + N_SHARD * sh_rows

    def body(p_ref, w_ref, m_ref, v_ref, go_ref, d_ref, mo_ref, vo_ref):
        x, y, _ = _mesh_pos()
        mine = rep_rows + (2 * x + y) * sh_rows
        g_rep = p_ref[0:rep_rows, :]
        g_sh = p_ref[pl.ds(pl.multiple_of(mine, SUBLANE), sh_rows), :]
        for k in range(1, N_DEV):
            g_rep = g_rep + p_ref[k * mrows:k * mrows + rep_rows, :]
            g_sh = g_sh + p_ref[pl.ds(pl.multiple_of(k * mrows + mine, SUBLANE), sh_rows), :]
        g = jnp.concatenate([g_rep, g_sh], axis=0)
        delta, m2, v2 = _adamw_math(w_ref[...], g, m_ref[...], v_ref[...])
        go_ref[...] = g
        d_ref[...] = delta
        mo_ref[...] = m2
        vo_ref[...] = v2

    return pl.pallas_call(
        body, name="small_reduce_adamw",
        out_shape=[jax.ShapeDtypeStruct((rep_rows + sh_rows, 128), F32)] * 4,
        compiler_params=pltpu.CompilerParams(vmem_limit_bytes=VMEM_LIMIT_MB * 1024 * 1024),
    )(parts, w, m, v)


LANE = 128
REP_SPEC = (("ffn1_norm", 16), ("mix_norm", 16), ("ffn2_norm", 16), ("final_norm", 8), ("pool_w", 128),
            ("pool_scale", 8), ("hgrn_lb_logits", 16), ("hgrn_gnorm", 8), ("lru_wa", 256), ("lru_wx", 256))
SH_SPEC = (("meta_tokens", 32), ("conv_w", 32), ("lru_conv_w", 8), ("conv_b", 8), ("conv_ln_g", 8),
           ("conv_ln_b", 8), ("lru_conv_b", 8), ("lru_ba", 8), ("lru_bx", 8), ("lru_lambda", 8))
REP_ROWS = sum(r for _, r in REP_SPEC)
SH_ROWS = sum(r for _, r in SH_SPEC)


def _pack_rows(vals, spec):
    parts = []
    for name, rows in spec:
        flat = vals[name].astype(F32).reshape(-1, LANE)
        if flat.shape[0] < rows:
            flat = jnp.concatenate([flat, jnp.zeros((rows - flat.shape[0], LANE), F32)], axis=0)
        parts.append(flat)
    return jnp.concatenate(parts, axis=0)


def _unpack_rows(packed, spec, shapes):
    out = {}
    off = 0
    for name, rows in spec:
        shp = shapes[name]
        n = int(np.prod(shp)) // LANE
        out[name] = packed[off:off + n].reshape(shp)
        off += rows
    return out


def _block_diag(blocks):
    n, b, _ = blocks.shape
    return sum(jnp.pad(blocks[g], ((g * b, (n - 1 - g) * b), (g * b, (n - 1 - g) * b))) for g in range(n))


def _diag_blocks(mat, n):
    b = mat.shape[0] // n
    return jnp.stack([mat[g * b:(g + 1) * b, g * b:(g + 1) * b] for g in range(n)])


BIG = ("ffn1_wg", "ffn1_wu", "ffn2_wg", "ffn2_wu", "ffn1_wd", "ffn2_wd", "w_in_even", "w_out_even",
       "w_in_odd", "w_out_odd")
ADAM_BLOCKS = {"ffn1_wg": 8, "ffn1_wu": 8, "ffn2_wg": 8, "ffn2_wu": 8, "ffn1_wd": 4, "ffn2_wd": 4,
               "w_in_even": 4, "w_out_even": 2, "w_in_odd": 4, "w_out_odd": 2}
WEIGHT_NAMES = ('meta_tokens', 'ffn1_norm', 'ffn1_wg', 'ffn1_wu', 'ffn1_wd', 'mix_norm', 'ffn2_norm', 'ffn2_wg',
                'ffn2_wu', 'ffn2_wd', 'w_in_even', 'pool_w', 'pool_scale', 'hgrn_lb_logits', 'hgrn_gnorm',
                'w_out_even', 'w_in_odd', 'conv_w', 'conv_b', 'conv_ln_g', 'conv_ln_b', 'lru_conv_w',
                'lru_conv_b', 'lru_wa', 'lru_ba', 'lru_wx', 'lru_bx', 'lru_lambda', 'w_out_odd', 'final_norm')


def _rows2d(a):
    return a.reshape(-1, a.shape[-1])


def _local_step_v2(x, tgt, w, gathered, small_full):
    s_len, d = x.shape
    t_real = s_len + N_META
    tp = -(-t_real // ROW_ALIGN) * ROW_ALIGN
    tm = _tile(tp, 832, ROW_ALIGN)
    tm_small = _tile(tp, 416, 16)
    tr = _tile(tp, 416, SUBLANE)
    f1 = ("ffn1_norm", "ffn1_wg", "ffn1_wu", "ffn1_wd")
    f2 = ("ffn2_norm", "ffn2_wg", "ffn2_wu", "ffn2_wd")

    meta_full = small_full["meta_tokens"]
    h0 = jnp.concatenate([meta_full, x, jnp.zeros((tp - t_real, d), F32)], axis=0)
    tgt_pad = jnp.concatenate([jnp.zeros((N_META, d), F32), tgt, jnp.zeros((tp - t_real, d), F32)], axis=0)

    w_in_even = jnp.transpose(gathered["w_in_even"], (1, 0, 2)).reshape(d, D_IN_EVEN)
    w_out_even = gathered["w_out_even"].reshape(d, d)
    w_out_odd = gathered["w_out_odd"].reshape(d, d)
    even_piece = [(w_in_even, (d, D_IN_EVEN), (0, 0))]
    odd_pieces = [(gathered["w_in_odd"], (1, d, D_IN_ODD // N_SHARD), (k, 0, 0)) for k in range(N_SHARD)]
    pool_wbd = _block_diag(w["pool_w"][0]).astype(BF16)
    pool_scale = w["pool_scale"]
    wa_bd = _block_diag2(w["lru_wa"][0]).astype(BF16)
    wx_bd = _block_diag2(w["lru_wx"][0]).astype(BF16)
    mst, msk, n_lev = _hgrn_consts(HG_CHUNK)
    conv_w = small_full["conv_w"]
    sf = small_full

    def gain(name, layer):
        return w[name][layer:layer + 1]

    def ffn(h, names, layer):
        return _ffn_fwd(h, gain(names[0], layer), gathered[names[1]], gathered[names[2]], gathered[names[3]],
                        layer, tm)

    h1, a1, b1, n1 = ffn(h0, f1, 0)
    p0, nm0 = _proj_fwd(h1, gain("mix_norm", 0), 0, even_piece, tm_small)
    ya = _pool_fwd(p0, pool_wbd, pool_scale, tr)
    yb, states = _hgrn_fwd(p0, w["hgrn_lb_logits"], w["hgrn_gnorm"], mst, msk, n_lev, tm)
    h2 = _out_fwd(h1, ya, yb, w_out_even, tm)
    h3, a2, b2, n2 = ffn(h2, f2, 0)
    h4, a3, b3, n3 = ffn(h3, f1, 1)
    p1, nm1 = _proj_fwd(h4, gain("mix_norm", 1), 1, odd_pieces, tm_small)
    yc = _convmod_fwd(p1, conv_w, sf["conv_b"], sf["conv_ln_g"], sf["conv_ln_b"], tr)
    lru_args = (sf["lru_conv_w"], sf["lru_conv_b"], wa_bd, sf["lru_ba"], wx_bd, sf["lru_bx"], sf["lru_lambda"])
    yd, hs = _lru_fwd(p1, *lru_args)
    h5 = _out_fwd(h4, yc, yd, w_out_odd, tm)
    h6, a4, b4, n4 = ffn(h5, f2, 1)
    loss, dh6, dg_final = _loss_bwd(h6, w["final_norm"].reshape(1, d), tgt_pad, t_real, tm)

    def ffn_bwd(dho, h, n, a, b, names, layer, acc):
        dh, da, db, dg = _ffn_bwd_act(dho, h, gain(names[0], layer), a, b, gathered[names[1]], gathered[names[2]],
                                      gathered[names[3]], layer, tm_small)
        acc = _ffn_bwd_w(dho, n, a, b, da, db, acc[0], acc[1], acc[2], layer, tm)
        return dh, dg, acc

    none3 = (None, None, None)
    dh5, dg_f2_l1, g_f2 = ffn_bwd(dh6, h5, n4, a4, b4, f2, 1, none3)
    dyc, dyd, dw_out_odd = _out_bwd(dh5, yc, yd, w_out_odd, tm)
    dca, dcb, dconv_w, dconv_vec = _convmod_bwd(p1, dyc, conv_w, sf["conv_b"], sf["conv_ln_g"], sf["conv_ln_b"], tr)
    dlx, dlg, dwa_bd, dwx_bd, dlru_vec = _lru_bwd(p1, hs, dyd, *lru_args)
    dp1 = [dca, dcb, dlx, dlg]
    dh4, dg_mix_l1 = _proj_bwd_act(dh5, h4, gain("mix_norm", 1), 1, dp1, odd_pieces, tm_small)
    dw_in_odd = jnp.stack(_proj_bwd_w(nm1, dp1, tm))
    dh3, dg_f1_l1, g_f1 = ffn_bwd(dh4, h3, n3, a3, b3, f1, 1, none3)
    dh2, dg_f2_l0, g_f2 = ffn_bwd(dh3, h2, n2, a2, b2, f2, 0, g_f2)
    dya, dyb, dw_out_even = _out_bwd(dh2, ya, yb, w_out_even, tm)
    dpool, dpool_wbd, dpool_scale = _pool_bwd(p0, dya, pool_wbd, pool_scale, tr)
    dq, dz, dv, dgate, dlb_logits, dgn_heads = _hgrn_bwd(p0, dyb, states, w["hgrn_lb_logits"], w["hgrn_gnorm"],
                                                         mst, msk, n_lev, tm)
    dp0 = [jnp.concatenate([dpool, dq, dz, dv, dgate], axis=1)]
    dh1, dg_mix_l0 = _proj_bwd_act(dh2, h1, gain("mix_norm", 0), 0, dp0, even_piece, tm_small)
    (dw_in_even,) = _proj_bwd_w(nm0, dp0, tm_small)
    dh0, dg_f1_l0, g_f1 = ffn_bwd(dh1, h0, n1, a1, b1, f1, 0, g_f1)

    grad_x = dh0[N_META:t_real]
    big = {
        "ffn1_wg": g_f1[0], "ffn1_wu": g_f1[1], "ffn1_wd": g_f1[2],
        "ffn2_wg": g_f2[0], "ffn2_wu": g_f2[1], "ffn2_wd": g_f2[2],
        "w_in_even": jnp.transpose(dw_in_even.reshape(d, N_SHARD, D_IN_EVEN // N_SHARD), (1, 0, 2)),
        "w_out_even": dw_out_even.reshape(N_SHARD, d // N_SHARD, d),
        "w_in_odd": dw_in_odd,
        "w_out_odd": dw_out_odd.reshape(N_SHARD, d // N_SHARD, d),
    }
    rep = {
        "ffn1_norm": jnp.concatenate([dg_f1_l0, dg_f1_l1], axis=0),
        "mix_norm": jnp.concatenate([dg_mix_l0, dg_mix_l1], axis=0),
        "ffn2_norm": jnp.concatenate([dg_f2_l0, dg_f2_l1], axis=0),
        "final_norm": dg_final,
        "pool_w": _diag_blocks(dpool_wbd, len(POOL_WINDOWS)),
        "pool_scale": dpool_scale,
        "hgrn_lb_logits": dlb_logits,
        "hgrn_gnorm": jnp.sum(dgn_heads, axis=0),
        "lru_wa": _diag_blocks2(dwa_bd),
        "lru_wx": _diag_blocks2(dwx_bd),
    }
    dmeta = jnp.transpose(dh0[:N_META].reshape(N_META, N_SHARD, 2, LANE), (1, 0, 2, 3)).reshape(N_SHARD, 32, LANE)
    packs = [_pack_rows(rep, REP_SPEC)]
    for s in range(N_SHARD):
        sh = {
            "meta_tokens": dmeta[s], "conv_w": dconv_w[s], "lru_conv_w": dlru_vec[s, 0:4],
            "conv_b": dconv_vec[s, 0:1], "conv_ln_g": dconv_vec[s, 1:2], "conv_ln_b": dconv_vec[s, 2:3],
            "lru_conv_b": dlru_vec[s, 4:5], "lru_ba": dlru_vec[s, 5:6], "lru_bx": dlru_vec[s, 6:7],
            "lru_lambda": dlru_vec[s, 7:8],
        }
        packs.append(_pack_rows(sh, SH_SPEC))
    return loss, grad_x, big, jnp.concatenate(packs, axis=0)


def _block_diag2(heads):
    nb = heads.shape[0] // 2
    return jnp.stack([_block_diag(heads[2 * j:2 * j + 2]) for j in range(nb)])


def _diag_blocks2(mats):
    return jnp.concatenate([_diag_blocks(mats[j], 2) for j in range(mats.shape[0])], axis=0)


def _kernel_v2(x, meta_tokens, ffn1_norm, ffn1_wg, ffn1_wu, ffn1_wd, mix_norm, ffn2_norm, ffn2_wg, ffn2_wu, ffn2_wd, w_in_even, pool_w, pool_scale, hgrn_lb_logits, hgrn_gnorm, w_out_even, w_in_odd, conv_w, conv_b, conv_ln_g, conv_ln_b, lru_conv_w, lru_conv_b, lru_wa, lru_ba, lru_wx, lru_bx, lru_lambda, w_out_odd, final_norm, loss_target, m_meta_tokens, m_ffn1_norm, m_ffn1_wg, m_ffn1_wu, m_ffn1_wd, m_mix_norm, m_ffn2_norm, m_ffn2_wg, m_ffn2_wu, m_ffn2_wd, m_w_in_even, m_pool_w, m_pool_scale, m_hgrn_lb_logits, m_hgrn_gnorm, m_w_out_even, m_w_in_odd, m_conv_w, m_conv_b, m_conv_ln_g, m_conv_ln_b, m_lru_conv_w, m_lru_conv_b, m_lru_wa, m_lru_ba, m_lru_wx, m_lru_bx, m_lru_lambda, m_w_out_odd, m_final_norm, v_meta_tokens, v_ffn1_norm, v_ffn1_wg, v_ffn1_wu, v_ffn1_wd, v_mix_norm, v_ffn2_norm, v_ffn2_wg, v_ffn2_wu, v_ffn2_wd, v_w_in_even, v_pool_w, v_pool_scale, v_hgrn_lb_logits, v_hgrn_gnorm, v_w_out_even, v_w_in_odd, v_conv_w, v_conv_b, v_conv_ln_g, v_conv_ln_b, v_lru_conv_w, v_lru_conv_b, v_lru_wa, v_lru_ba, v_lru_wx, v_lru_bx, v_lru_lambda, v_w_out_odd, v_final_norm):
    args = locals()
    w = {n: args[n] for n in WEIGHT_NAMES}
    m = {n: args["m_" + n] for n in WEIGHT_NAMES}
    v = {n: args["v_" + n] for n in WEIGHT_NAMES}
    shapes = {n: w[n].shape for n in WEIGHT_NAMES}

    big_in = [_rows2d(w[n]).astype(BF16) for n in BIG]
    small_sh = _pack_rows(w, SH_SPEC)
    gath = _allgather_shards(big_in + [small_sh], [True] * len(BIG) + [False])
    gathered = dict(zip(BIG, gath[:len(BIG)]))
    sm = gath[len(BIG)]
    sh_shapes = {n: (N_SHARD,) + tuple(shapes[n]) for n, _ in SH_SPEC}
    per_shard = [_unpack_rows(sm[s], SH_SPEC, shapes) for s in range(N_SHARD)]
    small_full = {}
    for n, _ in SH_SPEC:
        stacked = [per_shard[s][n] for s in range(N_SHARD)]
        small_full[n] = jnp.concatenate([p.reshape(-1, p.shape[-1]) for p in stacked], axis=-1)
    small_full["conv_w"] = jnp.concatenate(
        [small_full["conv_w"], jnp.zeros((CONV_HALO - CONV_WIDTH, D_CONV), F32)], axis=0)

    loss, grad_x, big, small_part = _local_step(x[0], loss_target[0], w, gathered, small_full)
    loss = lax.psum(loss[0, 0], ("x", "y", "c"))

    core = lax.axis_index("c").astype(jnp.int32).reshape(1)
    parts = [big[n] for n in BIG]
    recvd = _pair_exchange_halves(parts)
    pair = _pair_add(parts, recvd, core)
    chip = (2 * lax.axis_index("x") + lax.axis_index("y")).astype(jnp.int32).reshape(1)
    slots = _scatter_to_owners(pair, _own_slot(pair, chip))
    halves = _sum_chips(slots, core)
    full = _pair_allgather_halves(halves)
    out_g, out_d, out_m, out_v = {}, {}, {}, {}
    for n, g in zip(BIG, full):
        res = _adamw(_rows2d(w[n]), _rows2d(m[n]), _rows2d(v[n]), g, 0, ADAM_BLOCKS[n])
        out_g[n], out_d[n], out_m[n], out_v[n] = [r.reshape(shapes[n]) for r in res]

    gathered_small = _allgather_all(small_part)

    def pack_small(src):
        return jnp.concatenate([_pack_rows(src, REP_SPEC), _pack_rows(src, SH_SPEC)], axis=0)

    res = _small_reduce_adamw(gathered_small, pack_small(w), pack_small(m), pack_small(v), REP_ROWS, SH_ROWS)
    for dst, packed in zip((out_g, out_d, out_m, out_v), res):
        dst.update(_unpack_rows(packed[:REP_ROWS], REP_SPEC, shapes))
        dst.update(_unpack_rows(packed[REP_ROWS:], SH_SPEC, shapes))

    return (loss, grad_x[None], *[out_g[n] for n in WEIGHT_NAMES], *[out_d[n] for n in WEIGHT_NAMES],
            *[out_m[n] for n in WEIGHT_NAMES], *[out_v[n] for n in WEIGHT_NAMES])


GATHER_GROUPS = (
    (("small", 0),),
    (("ffn1_wg", 0), ("ffn1_wu", 0), ("ffn1_wd", 0)),
    (("w_in_even", 0), ("w_out_even", 0)),
    (("ffn2_wg", 0), ("ffn2_wu", 0), ("ffn2_wd", 0)),
    (("ffn1_wg", 1), ("ffn1_wu", 1), ("ffn1_wd", 1)),
    (("w_in_odd", 0), ("w_out_odd", 0)),
    (("ffn2_wg", 1), ("ffn2_wu", 1), ("ffn2_wd", 1)),
)
ADAM_ROW_BLOCKS = {"ffn1_wg": 2, "ffn1_wu": 2, "ffn2_wg": 2, "ffn2_wu": 2, "ffn1_wd": 2, "ffn2_wd": 2,
                   "w_in_even": 4, "w_out_even": 2, "w_in_odd": 4, "w_out_odd": 2}
TRANSPOSED = ("ffn1_wg", "ffn1_wu", "ffn2_wg", "ffn2_wu", "w_in_even")
SCATTER_DEPTH = 2


def _unpack_small(sm, shapes):
    per_shard = [_unpack_rows(sm[s], SH_SPEC, shapes) for s in range(N_SHARD)]
    full = {}
    for n, _ in SH_SPEC:
        full[n] = jnp.concatenate([per_shard[s][n].reshape(-1, shapes[n][-1]) for s in range(N_SHARD)], axis=-1)
    full["conv_w"] = jnp.concatenate([full["conv_w"], jnp.zeros((CONV_HALO - CONV_WIDTH, D_CONV), F32)], axis=0)
    return full


def _local_step(x, tgt, w, shapes, fetch, emit, emit_small):
    s_len, d = x.shape
    t_real = s_len + N_META
    tp = -(-t_real // ROW_ALIGN) * ROW_ALIGN
    tm = _tile(tp, 832, ROW_ALIGN)
    tm_small = _tile(tp, 416, 16)
    tr = _tile(tp, 416, SUBLANE)

    def gain(name, layer):
        return w[name][layer:layer + 1]

    pool_wbd = _block_diag(w["pool_w"][0]).astype(BF16)
    pool_scale = w["pool_scale"]
    wa_bd = _block_diag2(w["lru_wa"][0]).astype(BF16)
    wx_bd = _block_diag2(w["lru_wx"][0]).astype(BF16)
    mst, msk, n_lev = _hgrn_consts(HG_CHUNK)

    (sm,) = fetch(0, None)
    sf = _unpack_small(sm, shapes)
    h0 = jnp.concatenate([sf["meta_tokens"], x, jnp.zeros((tp - t_real, d), F32)], axis=0)
    tgt_pad = jnp.concatenate([jnp.zeros((N_META, d), F32), tgt, jnp.zeros((tp - t_real, d), F32)], axis=0)
    f1l0 = fetch(1, h0)
    h1, *s1 = _ffn_fwd(h0, gain("ffn1_norm", 0), *f1l0, 0, tm)
    w_in_even4, w_out_even4 = fetch(2, h1)
    w_out_even = w_out_even4.reshape(d, d)
    even_piece = [(w_in_even4.reshape(D_IN_EVEN, d), (D_IN_EVEN, d), (0, 0))]
    p0, nm0 = _proj_fwd(h1, gain("mix_norm", 0), 0, even_piece, tm_small, wt=True)
    ya = _pool_fwd(p0, pool_wbd, pool_scale, tr)
    yb, states = _hgrn_fwd(p0, w["hgrn_lb_logits"], w["hgrn_gnorm"], mst, msk, n_lev, tm)
    h2 = _out_fwd(h1, ya, yb, w_out_even, tm)
    f2l0 = fetch(3, h2)
    h3, *s2 = _ffn_fwd(h2, gain("ffn2_norm", 0), *f2l0, 0, tm)
    f1l1 = fetch(4, h3)
    h4, *s3 = _ffn_fwd(h3, gain("ffn1_norm", 1), *f1l1, 0, tm)
    w_in_odd4, w_out_odd4 = fetch(5, h4)
    w_out_odd = w_out_odd4.reshape(d, d)
    odd_pieces = [(w_in_odd4, (1, d, D_IN_ODD // N_SHARD), (k, 0, 0)) for k in range(N_SHARD)]
    p1, nm1 = _proj_fwd(h4, gain("mix_norm", 1), 1, odd_pieces, tm_small)
    yc = _convmod_fwd(p1, sf["conv_w"], sf["conv_b"], sf["conv_ln_g"], sf["conv_ln_b"], tr)
    lru_args = (sf["lru_conv_w"], sf["lru_conv_b"], wa_bd, sf["lru_ba"], wx_bd, sf["lru_bx"], sf["lru_lambda"])
    yd, hs = _lru_fwd(p1, *lru_args)
    h5 = _out_fwd(h4, yc, yd, w_out_odd, tm)
    f2l1 = fetch(6, h5)
    h6, *s4 = _ffn_fwd(h5, gain("ffn2_norm", 1), *f2l1, 0, tm)
    loss, dh6, dg_final = _loss_bwd(h6, w["final_norm"].reshape(1, d), tgt_pad, t_real, tm)

    def ffn_bwd(dho, h, saved, norm, wts, after=()):
        ga, gb, sa, n = saved
        dh, da, db, dg, dy = _ffn_bwd_act(dho, h, norm, ga, gb, *wts, 0, tm, after)
        return dh, dg, _ffn_bwd_w(dy, n, sa, da, db, tm)

    dh5, dg_f2_l1, g = ffn_bwd(dh6, h5, s4, gain("ffn2_norm", 1), f2l1)
    sent = emit((("ffn2_wg", 1), ("ffn2_wu", 1), ("ffn2_wd", 1)), g)
    dyc, dyd, dw_out_odd = _out_bwd(dh5, yc, yd, w_out_odd, tm, tuple(sent))
    dca, dcb, dconv_w, dconv_vec = _convmod_bwd(p1, dyc, sf["conv_w"], sf["conv_b"], sf["conv_ln_g"],
                                                sf["conv_ln_b"], tr)
    dlx, dlg, dwa_bd, dwx_bd, dlru_vec = _lru_bwd(p1, hs, dyd, *lru_args)
    dp1 = [dca, dcb, dlx, dlg]
    dh4, dg_mix_l1 = _proj_bwd_act(dh5, h4, gain("mix_norm", 1), 1, dp1, odd_pieces, tm_small)
    dw_in_odd = jnp.stack(_proj_bwd_w(nm1, dp1, tm))
    dh3, dg_f1_l1, g = ffn_bwd(dh4, h3, s3, gain("ffn1_norm", 1), f1l1)
    sent = emit((("w_out_odd", 0), ("w_in_odd", 0), ("ffn1_wg", 1), ("ffn1_wu", 1), ("ffn1_wd", 1)),
                [dw_out_odd.reshape(N_SHARD, d // N_SHARD, d), dw_in_odd] + list(g))
    dh2, dg_f2_l0, g_f2l0 = ffn_bwd(dh3, h2, s2, gain("ffn2_norm", 0), f2l0, tuple(sent))
    dya, dyb, dw_out_even = _out_bwd(dh2, ya, yb, w_out_even, tm)
    dpool, dpool_wbd, dpool_scale = _pool_bwd(p0, dya, pool_wbd, pool_scale, tr)
    dq, dz, dv, dgate, dlb_logits, dgn_heads = _hgrn_bwd(p0, dyb, states, w["hgrn_lb_logits"], w["hgrn_gnorm"],
                                                         mst, msk, n_lev, tm)
    dp0 = [jnp.concatenate([dpool, dq, dz, dv, dgate], axis=1)]
    dh1, dg_mix_l0 = _proj_bwd_act(dh2, h1, gain("mix_norm", 0), 0, dp0, even_piece, tm_small, wt=True)
    (dw_in_even_t,) = _proj_bwd_w(nm0, dp0, tm_small, wt=True)
    sent = emit((("ffn2_wg", 0), ("ffn2_wu", 0), ("ffn2_wd", 0), ("w_out_even", 0), ("w_in_even", 0)),
                list(g_f2l0) + [dw_out_even.reshape(N_SHARD, d // N_SHARD, d),
                                dw_in_even_t.reshape(N_SHARD, D_IN_EVEN // N_SHARD, d)])
    ga, gb, sa, n1 = s1
    dh0, da, db, dg_f1_l0, dy = _ffn_bwd_act(dh1, h0, gain("ffn1_norm", 0), ga, gb, *f1l0, 0, tm, tuple(sent))

    grad_x = dh0[N_META:t_real]
    rep = {
        "ffn1_norm": jnp.concatenate([dg_f1_l0, dg_f1_l1], axis=0),
        "mix_norm": jnp.concatenate([dg_mix_l0, dg_mix_l1], axis=0),
        "ffn2_norm": jnp.concatenate([dg_f2_l0, dg_f2_l1], axis=0),
        "final_norm": dg_final,
        "pool_w": _diag_blocks(dpool_wbd, len(POOL_WINDOWS)),
        "pool_scale": dpool_scale,
        "hgrn_lb_logits": dlb_logits,
        "hgrn_gnorm": jnp.sum(dgn_heads, axis=0),
        "lru_wa": _diag_blocks2(dwa_bd),
        "lru_wx": _diag_blocks2(dwx_bd),
    }
    dmeta = jnp.transpose(dh0[:N_META].reshape(N_META, N_SHARD, 2, LANE), (1, 0, 2, 3)).reshape(N_SHARD, 32, LANE)
    packs = [_pack_rows(rep, REP_SPEC)]
    for s in range(N_SHARD):
        sh = {
            "meta_tokens": dmeta[s], "conv_w": dconv_w[s], "lru_conv_w": dlru_vec[s, 0:4],
            "conv_b": dconv_vec[s, 0:1], "conv_ln_g": dconv_vec[s, 1:2], "conv_ln_b": dconv_vec[s, 2:3],
            "lru_conv_b": dlru_vec[s, 4:5], "lru_ba": dlru_vec[s, 5:6], "lru_bx": dlru_vec[s, 6:7],
            "lru_lambda": dlru_vec[s, 7:8],
        }
        packs.append(_pack_rows(sh, SH_SPEC))
    sent = emit_small(jnp.concatenate(packs, axis=0))
    emit((("ffn1_wg", 0), ("ffn1_wu", 0), ("ffn1_wd", 0)), _ffn_bwd_w(dy, n1, sa, da, db, tm, tuple(sent)))
    return loss, grad_x


def kernel(x, meta_tokens, ffn1_norm, ffn1_wg, ffn1_wu, ffn1_wd, mix_norm, ffn2_norm, ffn2_wg, ffn2_wu, ffn2_wd, w_in_even, pool_w, pool_scale, hgrn_lb_logits, hgrn_gnorm, w_out_even, w_in_odd, conv_w, conv_b, conv_ln_g, conv_ln_b, lru_conv_w, lru_conv_b, lru_wa, lru_ba, lru_wx, lru_bx, lru_lambda, w_out_odd, final_norm, loss_target, m_meta_tokens, m_ffn1_norm, m_ffn1_wg, m_ffn1_wu, m_ffn1_wd, m_mix_norm, m_ffn2_norm, m_ffn2_wg, m_ffn2_wu, m_ffn2_wd, m_w_in_even, m_pool_w, m_pool_scale, m_hgrn_lb_logits, m_hgrn_gnorm, m_w_out_even, m_w_in_odd, m_conv_w, m_conv_b, m_conv_ln_g, m_conv_ln_b, m_lru_conv_w, m_lru_conv_b, m_lru_wa, m_lru_ba, m_lru_wx, m_lru_bx, m_lru_lambda, m_w_out_odd, m_final_norm, v_meta_tokens, v_ffn1_norm, v_ffn1_wg, v_ffn1_wu, v_ffn1_wd, v_mix_norm, v_ffn2_norm, v_ffn2_wg, v_ffn2_wu, v_ffn2_wd, v_w_in_even, v_pool_w, v_pool_scale, v_hgrn_lb_logits, v_hgrn_gnorm, v_w_out_even, v_w_in_odd, v_conv_w, v_conv_b, v_conv_ln_g, v_conv_ln_b, v_lru_conv_w, v_lru_conv_b, v_lru_wa, v_lru_ba, v_lru_wx, v_lru_bx, v_lru_lambda, v_w_out_odd, v_final_norm):
    args = locals()
    w = {n: args[n] for n in WEIGHT_NAMES}
    m = {n: args["m_" + n] for n in WEIGHT_NAMES}
    v = {n: args["v_" + n] for n in WEIGHT_NAMES}
    shapes = {n: w[n].shape for n in WEIGHT_NAMES}
    core = lax.axis_index("c").astype(jnp.int32).reshape(1)
    chip = (2 * lax.axis_index("x") + lax.axis_index("y")).astype(jnp.int32).reshape(1)
    me = 2 * chip + core

    def view(a, n):
        return jnp.swapaxes(a, 1, 2) if n in TRANSPOSED else a

    wv, mv, vv = [{n: view(src[n], n) for n in BIG} for src in (w, m, v)]

    def shard(key):
        n, l = key
        return _pack_rows(w, SH_SPEC) if n == "small" else wv[n][l].astype(BF16)

    started = {}
    for groups in (GATHER_GROUPS[:2], GATHER_GROUPS[2:]):
        gkeys = [key for grp in groups for key in grp]
        ssem, rsem, srcs, lands, token = _gather_start([shard(key) for key in gkeys],
                                                       [key in GATHER_GROUPS[1] for key in gkeys])
        for k, key in enumerate(gkeys):
            started[key] = (ssem, rsem, srcs[k], lands[k], k, token)

    def pack_small(src):
        return jnp.concatenate([_pack_rows(src, REP_SPEC), _pack_rows(src, SH_SPEC)], axis=0)

    small_packs = [pack_small(src) for src in (w, m, v)]

    def fetch(group, after):
        st = [started[key] for key in GATHER_GROUPS[group]]
        deps = (st[0][5],) if after is None else (after,)
        if group == 1:
            deps += (started[GATHER_GROUPS[2][0]][5],) + tuple(small_packs)
        split = group == 1
        got = _gather_wait(st[0][0], st[0][1], [s[2] for s in st], [s[3] for s in st], [s[4] for s in st], deps,
                           split)
        return _pair_forward(got) if split else got

    in_flight, reduced = [], {}

    def collect(entry, after):
        gkeys, gs_sem, gr_sem, grads_thru, slots_thru, _ = entry
        slots = _reduce_wait(gs_sem, gr_sem, grads_thru, slots_thru, after)
        full = _pair_allgather_halves(_sum_devices(slots, core))
        reduced.update(zip(gkeys, full))
        return full[0]

    def emit(gkeys, grads):
        grads = list(grads)
        in_flight.append((gkeys,) + tuple(_reduce_start(grads, _own_part(grads, chip, core, me))))
        token = in_flight[-1][-1]
        if len(in_flight) > SCATTER_DEPTH:
            return token, collect(in_flight[-1 - SCATTER_DEPTH], (token,))
        return (token,)

    small_flight = []

    def emit_small(part):
        small_flight.append(_small_start(part, _small_own(part, me)))
        return (small_flight[0][4],)

    loss, grad_x = _local_step(x[0], loss_target[0], w, shapes, fetch, emit, emit_small)
    loss = lax.psum(loss[0, 0], ("x", "y", "c"))

    s_ssem, s_rsem, s_part, s_slots, _ = small_flight[0]
    small_all = _small_wait(s_ssem, s_rsem, s_part, s_slots, (in_flight[-1][-1],))
    small_res = _small_reduce_adamw(small_all.reshape(-1, LANE), *small_packs, REP_ROWS, SH_ROWS)
    out_g, out_d, out_m, out_v = {}, {}, {}, {}
    deps = (small_res[0],)
    for entry in in_flight[-SCATTER_DEPTH:]:
        collect(entry, deps)
        for n in BIG:
            layers = range(shapes[n][0])
            if n not in out_g and all((n, l) in reduced for l in layers):
                res = _adamw(wv[n], mv[n], vv[n], [reduced[(n, l)] for l in layers], ADAM_ROW_BLOCKS[n])
                out_g[n], out_d[n], out_m[n], out_v[n] = [view(r, n) for r in res]
                deps += (res[1],)

    for dst, packed in zip((out_g, out_d, out_m, out_v), small_res):
        dst.update(_unpack_rows(packed[:REP_ROWS], REP_SPEC, shapes))
        dst.update(_unpack_rows(packed[REP_ROWS:], SH_SPEC, shapes))

    return (loss, grad_x[None], *[out_g[n] for n in WEIGHT_NAMES], *[out_d[n] for n in WEIGHT_NAMES],
            *[out_m[n] for n in WEIGHT_NAMES], *[out_v[n] for n in WEIGHT_NAMES])
```

```python
import functools

import numpy as np
import jax
import jax.numpy as jnp
from jax import lax
from jax.experimental import pallas as pl
from jax.experimental.pallas import tpu as pltpu

F32 = jnp.float32
BF16 = jnp.bfloat16
MESH = pl.DeviceIdType.MESH

EPS = 1e-6
N_META = 16
D_MODEL = 1024
D_FF = 2816
N_SHARD = 4
FF_SHARD = D_FF // N_SHARD
D_POOL = 256
POOL_GROUP = 64
POOL_WINDOWS = (2, 4, 8, 16)
D_HGRN = 768
HG_HEADS = 6
HEAD = 128
HG_CHUNK = 64
HG_HEADS_PER_STEP = 6
HG_PBLOCK = 256
D_IN_EVEN = D_POOL + 4 * D_HGRN
D_CONV = 512
CONV_WIDTH = 31
CONV_HALO = 32
D_LRU = 512
LRU_CONV = 4
LRU_HALO = 8
LRU_C = 8.0
D_IN_ODD = 2 * D_CONV + 2 * D_LRU
SUBLANE = 8
ROW_ALIGN = 64

ADAM_LR = 0.001
ADAM_B1 = 0.9
ADAM_B2 = 0.999
ADAM_EPS = 1e-08
ADAM_WD = 0.01
ADAM_STEP = 10

VMEM_LIMIT_MB = 56


def _cparams(n_grid_axes=0, vmem_mb=VMEM_LIMIT_MB):
    sem = ("arbitrary",) * n_grid_axes if n_grid_axes else None
    return pltpu.CompilerParams(dimension_semantics=sem, vmem_limit_bytes=vmem_mb * 1024 * 1024)


def _tile(n, target, mult):
    best = None
    for t in range(mult, min(n, target) + 1, mult):
        if n % t == 0:
            best = t
    assert best is not None, (n, target, mult)
    return best


def _dot(a, b):
    return jnp.dot(a, b, preferred_element_type=F32)


def _dot_nt(a, b):
    return lax.dot_general(a, b, (((1,), (1,)), ((), ())), preferred_element_type=F32)


def _dot_tn(a, b):
    return lax.dot_general(a, b, (((0,), (0,)), ((), ())), preferred_element_type=F32)


def _sigmoid(x):
    return 1.0 / (1.0 + jnp.exp(-x))


def _colsum(x):
    return jnp.sum(x, axis=0, keepdims=True)


def _rms_stats(h):
    rstd = lax.rsqrt(jnp.mean(h * h, axis=-1, keepdims=True) + EPS)
    return rstd, h * rstd


def _rms_bwd(dn, g, rstd, xhat):
    dng = dn * g
    dh = rstd * (dng - xhat * jnp.mean(dng * xhat, axis=-1, keepdims=True))
    return dh, _colsum(dn * xhat)


def _ffn_fwd(h, norm, wg4, wu4, wd4, layer, tm):
    tp, d = h.shape
    nt = tp // tm

    def body(h_ref, g_ref, wg_ref, wu_ref, wd_ref, ho_ref, ga_ref, gb_ref, sa_ref, n_ref, n_sc, acc):
        s = pl.program_id(1)

        @pl.when(s == 0)
        def _():
            hh = h_ref[...]
            rstd, xhat = _rms_stats(hh)
            n = (xhat * g_ref[...]).astype(BF16)
            n_sc[...] = n
            n_ref[...] = n
            acc[...] = jnp.zeros_like(acc)

        n = n_sc[...]
        a = _dot_nt(n, wg_ref[0])
        b = _dot_nt(n, wu_ref[0])
        sig = _sigmoid(a)
        sil = a * sig
        ga_ref[0] = (sig * (1.0 + a * (1.0 - sig)) * b).astype(BF16)
        gb_ref[0] = sil.astype(BF16)
        sg = (sil * b).astype(BF16)
        sa_ref[0] = sg
        acc[...] += _dot(sg, wd_ref[0])

        @pl.when(s == N_SHARD - 1)
        def _():
            ho_ref[...] = h_ref[...] + 0.5 * acc[...]

    return pl.pallas_call(
        body, name="ffn_fwd",
        grid=(nt, N_SHARD),
        in_specs=[
            pl.BlockSpec((tm, d), lambda i, s: (i, 0)),
            pl.BlockSpec((1, d), lambda i, s: (0, 0)),
            pl.BlockSpec((1, FF_SHARD, d), lambda i, s: (s, layer, 0)),
            pl.BlockSpec((1, FF_SHARD, d), lambda i, s: (s, layer, 0)),
            pl.BlockSpec((1, FF_SHARD, d), lambda i, s: (s, layer, 0)),
        ],
        out_specs=[
            pl.BlockSpec((tm, d), lambda i, s: (i, 0)),
            pl.BlockSpec((1, tm, FF_SHARD), lambda i, s: (s, i, 0)),
            pl.BlockSpec((1, tm, FF_SHARD), lambda i, s: (s, i, 0)),
            pl.BlockSpec((1, tm, FF_SHARD), lambda i, s: (s, i, 0)),
            pl.BlockSpec((tm, d), lambda i, s: (i, 0)),
        ],
        out_shape=[
            jax.ShapeDtypeStruct((tp, d), F32),
            jax.ShapeDtypeStruct((N_SHARD, tp, FF_SHARD), BF16),
            jax.ShapeDtypeStruct((N_SHARD, tp, FF_SHARD), BF16),
            jax.ShapeDtypeStruct((N_SHARD, tp, FF_SHARD), BF16),
            jax.ShapeDtypeStruct((tp, d), BF16),
        ],
        scratch_shapes=[pltpu.VMEM((tm, d), BF16), pltpu.VMEM((tm, d), F32)],
        compiler_params=_cparams(2),
    )(h, norm, wg4, wu4, wd4)


def _ffn_bwd_act(dho, h, norm, ga4, gb4, wg4, wu4, wd4, layer, tm, after=()):
    tp, d = h.shape
    nt = tp // tm

    def body(dho_ref, h_ref, g_ref, ga_ref, gb_ref, wg_ref, wu_ref, wd_ref, *rest):
        dh_ref, da_ref, db_ref, dg_ref, dy_ref, dn_sc = rest[len(after):]
        i = pl.program_id(0)
        s = pl.program_id(1)

        @pl.when(s == 0)
        def _():
            dy_ref[...] = (0.5 * dho_ref[...]).astype(BF16)
            dn_sc[...] = jnp.zeros_like(dn_sc)

        @pl.when((s == 0) & (i == 0))
        def _():
            dg_ref[...] = jnp.zeros_like(dg_ref)

        ds = _dot_nt(dy_ref[...], wd_ref[0])
        da = (ds * ga_ref[0].astype(F32)).astype(BF16)
        db = (ds * gb_ref[0].astype(F32)).astype(BF16)
        da_ref[0] = da
        db_ref[0] = db
        dn_sc[...] += _dot(da, wg_ref[0]) + _dot(db, wu_ref[0])

        @pl.when(s == N_SHARD - 1)
        def _():
            rstd, xhat = _rms_stats(h_ref[...])
            dh, dg = _rms_bwd(dn_sc[...], g_ref[...], rstd, xhat)
            dh_ref[...] = dho_ref[...] + dh
            dg_ref[...] += dg

    return pl.pallas_call(
        body, name="ffn_bwd_act",
        grid=(nt, N_SHARD),
        in_specs=[
            pl.BlockSpec((tm, d), lambda i, s: (i, 0)),
            pl.BlockSpec((tm, d), lambda i, s: (i, 0)),
            pl.BlockSpec((1, d), lambda i, s: (0, 0)),
            pl.BlockSpec((1, tm, FF_SHARD), lambda i, s: (s, i, 0)),
            pl.BlockSpec((1, tm, FF_SHARD), lambda i, s: (s, i, 0)),
            pl.BlockSpec((1, FF_SHARD, d), lambda i, s: (s, layer, 0)),
            pl.BlockSpec((1, FF_SHARD, d), lambda i, s: (s, layer, 0)),
            pl.BlockSpec((1, FF_SHARD, d), lambda i, s: (s, layer, 0)),
        ] + [pl.BlockSpec(memory_space=pl.ANY)] * len(after),
        out_specs=[
            pl.BlockSpec((tm, d), lambda i, s: (i, 0)),
            pl.BlockSpec((1, tm, FF_SHARD), lambda i, s: (s, i, 0)),
            pl.BlockSpec((1, tm, FF_SHARD), lambda i, s: (s, i, 0)),
            pl.BlockSpec((1, d), lambda i, s: (0, 0)),
            pl.BlockSpec((tm, d), lambda i, s: (i, 0)),
        ],
        out_shape=[
            jax.ShapeDtypeStruct((tp, d), F32),
            jax.ShapeDtypeStruct((N_SHARD, tp, FF_SHARD), BF16),
            jax.ShapeDtypeStruct((N_SHARD, tp, FF_SHARD), BF16),
            jax.ShapeDtypeStruct((1, d), F32),
            jax.ShapeDtypeStruct((tp, d), BF16),
        ],
        scratch_shapes=[pltpu.VMEM((tm, d), F32)],
        compiler_params=_cparams(2),
    )(dho, h, norm, ga4, gb4, wg4, wu4, wd4, *after)


def _ffn_bwd_w(dy, n, sa4, da4, db4, tm, after=()):
    tp, d = n.shape
    nt = tp // tm

    def body(dy_ref, n_ref, sa_ref, da_ref, db_ref, *rest):
        og_ref, ou_ref, od_ref, accg, accu, accd = rest[len(after):]
        i = pl.program_id(1)

        @pl.when(i == 0)
        def _():
            accg[...] = jnp.zeros_like(accg)
            accu[...] = jnp.zeros_like(accu)
            accd[...] = jnp.zeros_like(accd)

        nn = n_ref[...]
        accg[...] += _dot_tn(da_ref[0], nn)
        accu[...] += _dot_tn(db_ref[0], nn)
        accd[...] += _dot_tn(sa_ref[0], dy_ref[...])

        @pl.when(i == nt - 1)
        def _():
            og_ref[0] = accg[...].astype(BF16)
            ou_ref[0] = accu[...].astype(BF16)
            od_ref[0] = accd[...].astype(BF16)

    in_specs = [
        pl.BlockSpec((tm, d), lambda s, i: (i, 0)),
        pl.BlockSpec((tm, d), lambda s, i: (i, 0)),
        pl.BlockSpec((1, tm, FF_SHARD), lambda s, i: (s, i, 0)),
        pl.BlockSpec((1, tm, FF_SHARD), lambda s, i: (s, i, 0)),
        pl.BlockSpec((1, tm, FF_SHARD), lambda s, i: (s, i, 0)),
    ]
    return pl.pallas_call(
        body, name="ffn_bwd_w",
        grid=(N_SHARD, nt),
        in_specs=in_specs + [pl.BlockSpec(memory_space=pl.ANY)] * len(after),
        out_specs=[pl.BlockSpec((1, FF_SHARD, d), lambda s, i: (s, 0, 0))] * 3,
        out_shape=[jax.ShapeDtypeStruct((N_SHARD, FF_SHARD, d), BF16)] * 3,
        scratch_shapes=[pltpu.VMEM((FF_SHARD, d), F32)] * 3,
        compiler_params=_cparams(2),
    )(dy, n, sa4, da4, db4, *after)


def _proj_fwd(h, norm, layer, w_pieces, tm, wt=False):
    tp, d = h.shape
    widths = [bs[-2] if wt else bs[-1] for _, bs, _ in w_pieces]
    ntot = sum(widths)
    npc = len(w_pieces)

    def body(*refs):
        h_ref, g_ref = refs[:2]
        w_refs = refs[2:2 + npc]
        p_ref, n_ref = refs[2 + npc:]
        rstd, xhat = _rms_stats(h_ref[...])
        n = (xhat * g_ref[...]).astype(BF16)
        n_ref[...] = n
        off = 0
        for k in range(npc):
            w = w_refs[k][...]
            w = w.reshape(w.shape[-2], w.shape[-1])
            p_ref[:, off:off + widths[k]] = _dot_nt(n, w) if wt else _dot(n, w)
            off += widths[k]

    in_specs = [pl.BlockSpec((tm, d), lambda i: (i, 0)), pl.BlockSpec((1, d), lambda i: (0, 0))]
    for _, bs, idx in w_pieces:
        in_specs.append(pl.BlockSpec(bs, functools.partial(lambda i, idx: idx, idx=idx)))
    return pl.pallas_call(
        body, name="proj_fwd",
        grid=(tp // tm,),
        in_specs=in_specs,
        out_specs=[pl.BlockSpec((tm, ntot), lambda i: (i, 0)), pl.BlockSpec((tm, d), lambda i: (i, 0))],
        out_shape=[jax.ShapeDtypeStruct((tp, ntot), F32), jax.ShapeDtypeStruct((tp, d), BF16)],
        compiler_params=_cparams(1),
    )(h, norm, *[w for w, _, _ in w_pieces])


def _proj_bwd_act(dres, h, norm, layer, dp_pieces, w_pieces, tm, wt=False):
    tp, d = h.shape
    npc = len(dp_pieces)
    nw = len(w_pieces)
    assert nw == npc or (nw == 1 and wt)

    def body(*refs):
        dres_ref, h_ref, g_ref = refs[:3]
        dp_refs = refs[3:3 + npc]
        w_refs = refs[3 + npc:3 + npc + nw]
        dh_ref, dg_ref = refs[3 + npc + nw:]
        i = pl.program_id(0)

        @pl.when(i == 0)
        def _():
            dg_ref[...] = jnp.zeros_like(dg_ref)

        dn = None
        off = 0
        for k in range(npc):
            if nw == npc:
                w = w_refs[k][...]
                w = w.reshape(w.shape[-2], w.shape[-1])
            else:
                w = w_refs[0][off:off + dp_pieces[k].shape[1], :]
                off += dp_pieces[k].shape[1]
            t = _dot(dp_refs[k][...], w) if wt else _dot_nt(dp_refs[k][...], w)
            dn = t if dn is None else dn + t
        rstd, xhat = _rms_stats(h_ref[...])
        dh, dg = _rms_bwd(dn, g_ref[...], rstd, xhat)
        dh_ref[...] = dres_ref[...] + dh
        dg_ref[...] += dg

    in_specs = [pl.BlockSpec((tm, d), lambda i: (i, 0)), pl.BlockSpec((tm, d), lambda i: (i, 0)),
                pl.BlockSpec((1, d), lambda i: (0, 0))]
    for dp in dp_pieces:
        in_specs.append(pl.BlockSpec((tm, dp.shape[1]), lambda i: (i, 0)))
    for _, bs, idx in w_pieces:
        in_specs.append(pl.BlockSpec(bs, functools.partial(lambda i, idx: idx, idx=idx)))
    return pl.pallas_call(
        body, name="proj_bwd_act",
        grid=(tp // tm,),
        in_specs=in_specs,
        out_specs=[pl.BlockSpec((tm, d), lambda i: (i, 0)), pl.BlockSpec((1, d), lambda i: (0, 0))],
        out_shape=[jax.ShapeDtypeStruct((tp, d), F32), jax.ShapeDtypeStruct((1, d), F32)],
        compiler_params=_cparams(1),
    )(dres, h, norm, *dp_pieces, *[w for w, _, _ in w_pieces])


def _proj_bwd_w(n, dp_pieces, tm, wt=False):
    tp, d = n.shape
    npc = len(dp_pieces)
    widths = [dp.shape[1] for dp in dp_pieces]
    oshape = (lambda w: (w, d)) if wt else (lambda w: (d, w))

    def body(*refs):
        n_ref = refs[0]
        dp_refs = refs[1:1 + npc]
        o_refs = refs[1 + npc:1 + 2 * npc]
        accs = refs[1 + 2 * npc:]
        i = pl.program_id(0)

        @pl.when(i == 0)
        def _():
            for acc in accs:
                acc[...] = jnp.zeros_like(acc)

        nn = n_ref[...]
        for k in range(npc):
            accs[k][...] += _dot_tn(dp_refs[k][...], nn) if wt else _dot_tn(nn, dp_refs[k][...])

        @pl.when(i == pl.num_programs(0) - 1)
        def _():
            for k in range(npc):
                o_refs[k][...] = accs[k][...].astype(BF16)

    return pl.pallas_call(
        body, name="proj_bwd_w",
        grid=(tp // tm,),
        in_specs=[pl.BlockSpec((tm, d), lambda i: (i, 0))]
        + [pl.BlockSpec((tm, w), lambda i: (i, 0)) for w in widths],
        out_specs=[pl.BlockSpec(oshape(w), lambda i: (0, 0)) for w in widths],
        out_shape=[jax.ShapeDtypeStruct(oshape(w), BF16) for w in widths],
        scratch_shapes=[pltpu.VMEM(oshape(w), F32) for w in widths],
        compiler_params=_cparams(1),
    )(n, *dp_pieces)


def _out_fwd(h, ya, yb, w, tm):
    tp, d = h.shape
    na, nb = ya.shape[1], yb.shape[1]

    def body(h_ref, ya_ref, yb_ref, w_ref, o_ref):
        y = _dot(ya_ref[...].astype(BF16), w_ref[0:na, :]) + _dot(yb_ref[...].astype(BF16), w_ref[na:, :])
        o_ref[...] = h_ref[...] + y

    return pl.pallas_call(
        body, name="out_fwd",
        grid=(tp // tm,),
        in_specs=[pl.BlockSpec((tm, d), lambda i: (i, 0)), pl.BlockSpec((tm, na), lambda i: (i, 0)),
                  pl.BlockSpec((tm, nb), lambda i: (i, 0)), pl.BlockSpec((d, d), lambda i: (0, 0))],
        out_specs=pl.BlockSpec((tm, d), lambda i: (i, 0)),
        out_shape=jax.ShapeDtypeStruct((tp, d), F32),
        compiler_params=_cparams(1),
    )(h, ya, yb, w)


def _out_bwd(dy, ya, yb, w, tm, after=()):
    tp, d = dy.shape
    na, nb = ya.shape[1], yb.shape[1]

    def body(dy_ref, ya_ref, yb_ref, w_ref, *rest):
        da_ref, db_ref, dw_ref, acc = rest[len(after):]
        i = pl.program_id(0)

        @pl.when(i == 0)
        def _():
            acc[...] = jnp.zeros_like(acc)

        dyb16 = dy_ref[...].astype(BF16)
        da_ref[...] = _dot_nt(dyb16, w_ref[0:na, :])
        db_ref[...] = _dot_nt(dyb16, w_ref[na:, :])
        acc[0:na, :] += _dot_tn(ya_ref[...].astype(BF16), dyb16)
        acc[na:, :] += _dot_tn(yb_ref[...].astype(BF16), dyb16)

        @pl.when(i == pl.num_programs(0) - 1)
        def _():
            dw_ref[...] = acc[...].astype(BF16)

    return pl.pallas_call(
        body, name="out_bwd",
        grid=(tp // tm,),
        in_specs=[pl.BlockSpec((tm, d), lambda i: (i, 0)), pl.BlockSpec((tm, na), lambda i: (i, 0)),
                  pl.BlockSpec((tm, nb), lambda i: (i, 0)), pl.BlockSpec((d, d), lambda i: (0, 0))]
        + [pl.BlockSpec(memory_space=pl.ANY)] * len(after),
        out_specs=[pl.BlockSpec((tm, na), lambda i: (i, 0)), pl.BlockSpec((tm, nb), lambda i: (i, 0)),
                   pl.BlockSpec((d, d), lambda i: (0, 0))],
        out_shape=[jax.ShapeDtypeStruct((tp, na), F32), jax.ShapeDtypeStruct((tp, nb), F32),
                   jax.ShapeDtypeStruct((d, d), BF16)],
        scratch_shapes=[pltpu.VMEM((d, d), F32)],
        compiler_params=_cparams(1),
    )(dy, ya, yb, w, *after)


def _loss_bwd(h, gfin, tgt, t_real, tm):
    tp, d = h.shape

    def body(h_ref, g_ref, t_ref, loss_ref, dh_ref, dg_ref):
        i = pl.program_id(0)

        @pl.when(i == 0)
        def _():
            loss_ref[...] = jnp.zeros_like(loss_ref)
            dg_ref[...] = jnp.zeros_like(dg_ref)

        rows = i * tm + lax.broadcasted_iota(jnp.int32, (tm, 1), 0)
        valid = (rows >= N_META) & (rows < t_real)
        rstd, xhat = _rms_stats(h_ref[...])
        g = g_ref[...]
        err = jnp.where(valid, xhat * g - t_ref[...], 0.0)
        e2 = jnp.sum(err * err, axis=1, keepdims=True)
        loss_ref[...] += (0.5 / d) * jnp.sum(e2, axis=0, keepdims=True)
        dy = err * (1.0 / d)
        dh, dg = _rms_bwd(dy, g, rstd, xhat)
        dh_ref[...] = dh
        dg_ref[...] += dg

    return pl.pallas_call(
        body, name="loss_bwd",
        grid=(tp // tm,),
        in_specs=[pl.BlockSpec((tm, d), lambda i: (i, 0)), pl.BlockSpec((1, d), lambda i: (0, 0)),
                  pl.BlockSpec((tm, d), lambda i: (i, 0))],
        out_specs=[pl.BlockSpec((1, 1), lambda i: (0, 0)), pl.BlockSpec((tm, d), lambda i: (i, 0)),
                   pl.BlockSpec((1, d), lambda i: (0, 0))],
        out_shape=[jax.ShapeDtypeStruct((1, 1), F32), jax.ShapeDtypeStruct((tp, d), F32),
                   jax.ShapeDtypeStruct((1, d), F32)],
        compiler_params=_cparams(1),
    )(h, gfin, tgt)


POOL_HALO = 16


def _pool_lane_consts(n_rows):
    lane = lax.broadcasted_iota(jnp.int32, (n_rows, D_POOL), 1)
    grp = lane // POOL_GROUP
    win = jnp.where(grp == 0, 2.0, jnp.where(grp == 1, 4.0, jnp.where(grp == 2, 8.0, 16.0)))
    return grp, win


def _pool_select(grp, s2, s4, s8, s16):
    return jnp.where(grp == 0, s2, jnp.where(grp == 1, s4, jnp.where(grp == 2, s8, s16)))


def _pool_mixed(x, row0, tr):
    n = tr + POOL_HALO
    s2 = x + pltpu.roll(x, 1, 0)
    s4 = s2 + pltpu.roll(s2, 2, 0)
    s8 = s4 + pltpu.roll(s4, 4, 0)
    s16 = s8 + pltpu.roll(s8, 8, 0)
    grp, win = _pool_lane_consts(n)
    rows = row0 - POOL_HALO + lax.broadcasted_iota(jnp.int32, (n, D_POOL), 0)
    cnt = jnp.minimum((rows + 1).astype(F32), win)
    pooled = _pool_select(grp, s2, s4, s8, s16) / jnp.maximum(cnt, 1.0)
    return (pooled - x)[POOL_HALO:, :]


def _pool_fwd(p, wbd, scale, tr):
    tp = p.shape[0]
    nt = tp // tr

    def body(p_ref, w_ref, s_ref, y_ref, usc):
        usc[0:POOL_HALO, :] = jnp.zeros((POOL_HALO, D_POOL), F32)
        usc[POOL_HALO:, :] = p_ref[...]

        def tile(r, carry):
            r0 = pl.multiple_of(r * tr, SUBLANE)
            x = usc[pl.ds(r0, tr + POOL_HALO), :]
            mixed = _pool_mixed(x, r0, tr)
            y_ref[pl.ds(r0, tr), :] = _dot(mixed.astype(BF16), w_ref[...]) * s_ref[...]
            return carry

        lax.fori_loop(0, nt, tile, 0)

    return pl.pallas_call(
        body, name="pool_fwd",
        grid=(1,),
        in_specs=[pl.BlockSpec((tp, D_POOL), lambda i: (0, 0)), pl.BlockSpec((D_POOL, D_POOL), lambda i: (0, 0)),
                  pl.BlockSpec((1, D_POOL), lambda i: (0, 0))],
        out_specs=pl.BlockSpec((tp, D_POOL), lambda i: (0, 0)),
        out_shape=jax.ShapeDtypeStruct((tp, D_POOL), F32),
        scratch_shapes=[pltpu.VMEM((tp + POOL_HALO, D_POOL), F32)],
        compiler_params=_cparams(1),
    )(p, wbd, scale)


def _pool_bwd(p, dya, wbd, scale, tr):
    tp = p.shape[0]
    nt = tp // tr

    def body(p_ref, dy_ref, w_ref, s_ref, du_ref, dw_ref, ds_ref, usc, gsc):
        usc[0:POOL_HALO, :] = jnp.zeros((POOL_HALO, D_POOL), F32)
        usc[POOL_HALO:, :] = p_ref[...]
        gsc[tp:, :] = jnp.zeros((POOL_HALO, D_POOL), F32)
        dw_ref[...] = jnp.zeros_like(dw_ref)
        ds_ref[...] = jnp.zeros_like(ds_ref)
        grp, win = _pool_lane_consts(tr)

        def tile1(r, carry):
            r0 = pl.multiple_of(r * tr, SUBLANE)
            x = usc[pl.ds(r0, tr + POOL_HALO), :]
            mixed = _pool_mixed(x, r0, tr).astype(BF16)
            dy = dy_ref[pl.ds(r0, tr), :]
            dys = (dy * s_ref[...]).astype(BF16)
            ypre = _dot(mixed, w_ref[...])
            ds_ref[...] += _colsum(dy * ypre)
            dw_ref[...] += _dot_tn(mixed, dys)
            dmx = _dot_nt(dys, w_ref[...])
            rows = r0 + lax.broadcasted_iota(jnp.int32, (tr, D_POOL), 0)
            cnt = jnp.minimum((rows + 1).astype(F32), win)
            gsc[pl.ds(r0, tr), :] = dmx / cnt
            return carry

        lax.fori_loop(0, nt, tile1, 0)
        n = tr + POOL_HALO
        grp2, win2 = _pool_lane_consts(n)

        def tile2(r, carry):
            r0 = pl.multiple_of(r * tr, SUBLANE)
            g = gsc[pl.ds(r0, n), :]
            s2 = g + pltpu.roll(g, n - 1, 0)
            s4 = s2 + pltpu.roll(s2, n - 2, 0)
            s8 = s4 + pltpu.roll(s4, n - 4, 0)
            s16 = s8 + pltpu.roll(s8, n - 8, 0)
            pooled_t = _pool_select(grp2, s2, s4, s8, s16)
            rows = r0 + lax.broadcasted_iota(jnp.int32, (n, D_POOL), 0)
            cnt = jnp.minimum((rows + 1).astype(F32), win2)
            du = pooled_t - g * cnt
            du_ref[pl.ds(r0, tr), :] = du[0:tr, :].astype(BF16)
            return carry

        lax.fori_loop(0, nt, tile2, 0)

    return pl.pallas_call(
        body, name="pool_bwd",
        grid=(1,),
        in_specs=[pl.BlockSpec((tp, D_POOL), lambda i: (0, 0)), pl.BlockSpec((tp, D_POOL), lambda i: (0, 0)),
                  pl.BlockSpec((D_POOL, D_POOL), lambda i: (0, 0)), pl.BlockSpec((1, D_POOL), lambda i: (0, 0))],
        out_specs=[pl.BlockSpec((tp, D_POOL), lambda i: (0, 0)), pl.BlockSpec((D_POOL, D_POOL), lambda i: (0, 0)),
                   pl.BlockSpec((1, D_POOL), lambda i: (0, 0))],
        out_shape=[jax.ShapeDtypeStruct((tp, D_POOL), BF16), jax.ShapeDtypeStruct((D_POOL, D_POOL), F32),
                   jax.ShapeDtypeStruct((1, D_POOL), F32)],
        scratch_shapes=[pltpu.VMEM((tp + POOL_HALO, D_POOL), F32), pltpu.VMEM((tp + POOL_HALO, D_POOL), F32)],
        compiler_params=_cparams(1),
    )(p, dya, wbd, scale)


def _hgrn_levels(ch):
    levels = []
    w = ch // 2
    while w >= 1:
        levels.append(w)
        w //= 2
    return levels


def _hgrn_consts(ch):
    t = np.arange(ch)
    tril = t[None, :] <= t[:, None]
    masks = []
    for w in _hgrn_levels(ch):
        blk = t // (2 * w)
        upper = t % (2 * w) >= w
        masks.append(upper[:, None] & (~upper)[None, :] & (blk[:, None] == blk[None, :]))
    masks.append(tril)
    msk = np.stack(masks).astype(np.float32)
    return jnp.asarray(tril.astype(np.float32), BF16), jnp.asarray(msk, F32), len(masks) - 1


def _split3(x):
    hi = x.astype(BF16)
    r1 = x - hi.astype(F32)
    mid = r1.astype(BF16)
    lo = (r1 - mid.astype(F32)).astype(BF16)
    return hi, mid, lo


def _hgrn_exponents(tril, logf):
    ch = logf.shape[0]
    hi, mid, lo = _split3(logf)
    x = _dot(tril, jnp.concatenate([hi, mid, lo], axis=1))
    b = x[:, 0:HEAD] + x[:, HEAD:2 * HEAD] + x[:, 2 * HEAD:3 * HEAD]
    rows = lax.broadcasted_iota(jnp.int32, (ch, HEAD), 0)
    fx = jnp.broadcast_to(b[ch - 1:ch, :], (ch, HEAD)) - b
    lev = []
    for w in _hgrn_levels(ch):
        pos = rows % (2 * w)
        upper = pos >= w
        if w >= SUBLANE:
            parts = [jnp.broadcast_to(b[k * 2 * w + w - 1:k * 2 * w + w, :], (2 * w, HEAD))
                     for k in range(ch // (2 * w))]
            bmid = parts[0] if len(parts) == 1 else jnp.concatenate(parts, axis=0)
            dx = jnp.where(upper, b - bmid, 0.0)
            ex = jnp.where(upper, 0.0, bmid - b)
        else:
            dx = logf
            ex = jnp.zeros_like(logf)
            for i in range(1, w):
                dx = dx + jnp.where(pos >= w + i, pltpu.roll(logf, i, 0), 0.0)
                ex = ex + jnp.where(pos <= w - 1 - i, pltpu.roll(logf, ch - i, 0), 0.0)
            dx = jnp.where(upper, dx, 0.0)
        lev.append((dx, ex))
    return b, fx, lev


def _hgrn_exponents_bwd(tril, d_b, d_fx, d_blast, lev_grads):
    ch = d_b.shape[0]
    rows = lax.broadcasted_iota(jnp.int32, (ch, HEAD), 0)
    db = d_b - d_fx
    dlf = jnp.zeros_like(d_b)
    for w, (ddx, dex) in zip(_hgrn_levels(ch), lev_grads):
        pos = rows % (2 * w)
        upper = pos >= w
        gu = jnp.where(upper, ddx, 0.0)
        if w >= SUBLANE:
            gl = jnp.where(upper, 0.0, dex)
            db = db + gu - gl
            diff = gl - gu
            for k in range(ch // (2 * w)):
                s = _colsum(diff[k * 2 * w:(k + 1) * 2 * w, :])
                db = db + jnp.where(rows == k * 2 * w + w - 1, s, 0.0)
        else:
            dlf = dlf + gu
            for i in range(1, w):
                dlf = dlf + pltpu.roll(jnp.where(pos >= w + i, gu, 0.0), ch - i, 0)
                dlf = dlf + pltpu.roll(jnp.where(pos <= w - 1 - i, dex, 0.0), i, 0)
    db = db + jnp.where(rows == ch - 1, _colsum(d_fx) + d_blast, 0.0)
    hi = db.astype(BF16)
    lo = (db - hi.astype(F32)).astype(BF16)
    d2 = _dot_tn(tril, jnp.concatenate([hi, lo], axis=1))
    return d2[:, 0:HEAD] + d2[:, HEAD:2 * HEAD] + dlf


def _lockstep(gens):
    results = [None] * len(gens)
    live = list(range(len(gens)))
    while live:
        for i in list(live):
            try:
                next(gens[i])
            except StopIteration as stop:
                results[i] = stop.value
                live.remove(i)
    return results


def _hgrn_gates(q_raw, z, lb):
    sz = _sigmoid(z)
    f = lb + (1.0 - lb) * sz
    q = q_raw * _sigmoid(q_raw)
    k = (1.0 - lb) * (1.0 - sz)
    return q, k, f, sz


def _hgrn_intra(q, k, lev, msk_ref, n_lev, ch):
    eye = (lax.broadcasted_iota(jnp.int32, (ch, ch), 0) == lax.broadcasted_iota(jnp.int32, (ch, ch), 1))
    a = jnp.where(eye, jnp.sum(q * k, axis=1, keepdims=True), 0.0)
    ops = []
    for lv in range(n_lev):
        eq = jnp.exp(lev[lv][0])
        ek = jnp.exp(lev[lv][1])
        qd = q * eq
        kd = k * ek
        a = a + msk_ref[lv] * _dot_nt(qd.astype(BF16), kd.astype(BF16))
        ops.append((eq, ek, qd, kd))
        yield
    return a, ops


def _hgrn_fwd(p, lb_logits, gnorm, mst, msk, n_lev, tm):
    tp = p.shape[0]
    ch = HG_CHUNK
    nct = tm // ch
    nt = tp // tm
    nr = mst.shape[0]
    base = D_POOL // HEAD

    hp = HG_HEADS_PER_STEP
    wide = hp * HEAD
    npr = wide // HG_PBLOCK

    def body(*refs):
        p_refs = refs[:4 * npr]
        lg_ref, gn_ref, mst_ref, msk_ref, y_ref, ss_ref, st_sc = refs[4 * npr:]

        @pl.when(pl.program_id(1) == 0)
        def _():
            st_sc[...] = jnp.zeros_like(st_sc)

        lb_all = _sigmoid(lg_ref[0:1, :] - lg_ref[1:2, :])

        def raw(seg, hh, r0):
            per = HG_PBLOCK // HEAD
            return p_refs[seg * npr + hh // per][pl.ds(r0, ch), (hh % per) * HEAD:(hh % per + 1) * HEAD]

        def one_head(hh, c, r0):
            ls = slice(hh * HEAD, (hh + 1) * HEAD)
            q_raw, z, v, g_raw, st = raw(0, hh, r0), raw(1, hh, r0), raw(2, hh, r0), raw(3, hh, r0), st_sc[hh]
            q, k, f, _ = _hgrn_gates(q_raw, z, lb_all[:, ls])
            yield
            b, fx, lev = _hgrn_exponents(mst_ref[...], jnp.log(f))
            yield
            qe = q * jnp.exp(b)
            a, _ = yield from _hgrn_intra(q, k, lev, msk_ref, n_lev, ch)
            v16 = v.astype(BF16)
            o = _dot_nt(qe.astype(BF16), st.astype(BF16)) + _dot(a.astype(BF16), v16)
            kl = k * jnp.exp(fx)
            st_new = st * jnp.exp(b[ch - 1:ch, :]) + _dot_tn(v16, kl.astype(BF16))
            yield
            rstd = lax.rsqrt(jnp.mean(o * o, axis=-1, keepdims=True) + EPS)
            return st, st_new, o * rstd * gn_ref[...] * (g_raw * _sigmoid(g_raw))

        def chunk(c, carry):
            r0 = pl.multiple_of(c * ch, ch)
            results = _lockstep([one_head(hh, c, r0) for hh in range(hp)])
            for hh, (st, st_new, y) in enumerate(results):
                ss_ref[hh, c] = st
                st_sc[hh] = st_new
                y_ref[pl.ds(r0, ch), hh * HEAD:(hh + 1) * HEAD] = y
            return carry

        lax.fori_loop(0, nct, chunk, 0)

    def pspec(seg, part):
        return pl.BlockSpec((tm, HG_PBLOCK),
                            lambda h, i: (i, (base + seg * HG_HEADS) * HEAD // HG_PBLOCK + h * npr + part))

    return pl.pallas_call(
        body, name="hgrn_fwd",
        grid=(HG_HEADS // hp, nt),
        in_specs=[pspec(seg, part) for seg in range(4) for part in range(npr)]
        + [pl.BlockSpec((2, wide), lambda h, i: (0, h)),
           pl.BlockSpec((1, HEAD), lambda h, i: (0, 0)),
           pl.BlockSpec((nr, ch), lambda h, i: (0, 0)),
           pl.BlockSpec((n_lev + 1, ch, ch), lambda h, i: (0, 0, 0))],
        out_specs=[pl.BlockSpec((tm, wide), lambda h, i: (i, h)),
                   pl.BlockSpec((hp, nct, HEAD, HEAD), lambda h, i: (h, i, 0, 0))],
        out_shape=[jax.ShapeDtypeStruct((tp, D_HGRN), F32),
                   jax.ShapeDtypeStruct((HG_HEADS, tp // ch, HEAD, HEAD), F32)],
        scratch_shapes=[pltpu.VMEM((hp, HEAD, HEAD), F32)],
        compiler_params=_cparams(2),
    )(*([p] * (4 * npr)), lb_logits, gnorm, mst, msk)


def _hgrn_bwd(p, dyb, states, lb_logits, gnorm, mst, msk, n_lev, tm):
    tp = p.shape[0]
    ch = HG_CHUNK
    nct = tm // ch
    nt = tp // tm
    nr = mst.shape[0]
    base = D_POOL // HEAD

    hp = HG_HEADS_PER_STEP
    wide = hp * HEAD
    npr = wide // HG_PBLOCK

    def body(*refs):
        p_refs = refs[:4 * npr]
        (dy_ref, ss_ref, lg_ref, gn_ref, mst_ref, msk_ref,
         dq_ref, dz_ref, dv_ref, dg_ref, dlg_ref, dgn_ref, dst_sc, dlb_sc) = refs[4 * npr:]
        ti = pl.program_id(1)

        def raw(seg, hh, r0):
            per = HG_PBLOCK // HEAD
            return p_refs[seg * npr + hh // per][pl.ds(r0, ch), (hh % per) * HEAD:(hh % per + 1) * HEAD]

        @pl.when(ti == 0)
        def _():
            dst_sc[...] = jnp.zeros_like(dst_sc)
            dlb_sc[...] = jnp.zeros_like(dlb_sc)
            dgn_ref[...] = jnp.zeros_like(dgn_ref)

        lb_all = _sigmoid(lg_ref[0:1, :] - lg_ref[1:2, :])
        gn = gn_ref[...]

        def load_head(hh, c, r0):
            ls = slice(hh * HEAD, (hh + 1) * HEAD)
            return (raw(0, hh, r0), raw(1, hh, r0), raw(2, hh, r0), raw(3, hh, r0),
                    dy_ref[pl.ds(r0, ch), ls], ss_ref[hh, c], dst_sc[hh])

        def store_head(hh, r0, res):
            ls = slice(hh * HEAD, (hh + 1) * HEAD)
            dq_raw, dz, dv, dg_raw, dgn, dst_new, dlb = res
            dq_ref[pl.ds(r0, ch), ls] = dq_raw
            dz_ref[pl.ds(r0, ch), ls] = dz
            dv_ref[pl.ds(r0, ch), ls] = dv
            dg_ref[pl.ds(r0, ch), ls] = dg_raw
            dgn_ref[hh] += dgn
            dst_sc[hh] = dst_new
            dlb_sc[:, ls] += dlb

        def one_head(hh, loaded):
            ls = slice(hh * HEAD, (hh + 1) * HEAD)
            lb = lb_all[:, ls]
            q_raw, z, v, g_raw, dy, st, dst = loaded
            q, k, f, sz = _hgrn_gates(q_raw, z, lb)
            yield
            b, fx, lev = _hgrn_exponents(mst_ref[...], jnp.log(f))
            yield
            eb = jnp.exp(b)
            ef = jnp.exp(fx)
            elast = jnp.exp(b[ch - 1:ch, :])
            qe = q * eb
            kl = k * ef
            a, ops = yield from _hgrn_intra(q, k, lev, msk_ref, n_lev, ch)
            v16 = v.astype(BF16)
            st16 = st.astype(BF16)
            qe16 = qe.astype(BF16)
            kl16 = kl.astype(BF16)
            a16 = a.astype(BF16)
            o = _dot_nt(qe16, st16) + _dot(a16, v16)
            yield
            sg = _sigmoid(g_raw)
            rstd = lax.rsqrt(jnp.mean(o * o, axis=-1, keepdims=True) + EPS)
            oh = o * rstd
            dg_out = (dy * oh * gn * (sg * (1.0 + g_raw * (1.0 - sg)))).astype(BF16)
            don = dy * (g_raw * sg)
            dgn = _colsum(don * oh)
            doh = don * gn
            do = rstd * (doh - oh * jnp.mean(doh * oh, axis=-1, keepdims=True))
            do16 = do.astype(BF16)
            dst16 = dst.astype(BF16)
            yield
            dv = _dot_tn(a16, do16) + _dot_nt(kl16, dst16)
            da = msk_ref[n_lev] * _dot_nt(do16, v16)
            dqe = _dot(do16, st16)
            dkl = _dot(v16, dst16)
            dst_new = dst * elast + _dot_tn(do16, qe16)
            yield
            db_last = _colsum(dst * st) * elast
            dad = jnp.sum(do * v, axis=1, keepdims=True)
            dq = dad * k + dqe * eb
            dk = dad * q + dkl * ef
            lev_grads = []
            for lv in range(n_lev):
                eq, ek, qd, kd = ops[lv]
                gl = (msk_ref[lv] * da).astype(BF16)
                dqd = _dot(gl, kd.astype(BF16))
                dkd = _dot_tn(gl, qd.astype(BF16))
                dq = dq + dqd * eq
                dk = dk + dkd * ek
                lev_grads.append((dqd * qd, dkd * kd))
                yield
            dlogf = _hgrn_exponents_bwd(mst_ref[...], dqe * qe, dkl * kl, db_last, lev_grads)
            yield
            sq = _sigmoid(q_raw)
            dq_out = (dq * (sq * (1.0 + q_raw * (1.0 - sq)))).astype(BF16)
            dfk = dlogf / f - dk
            dz_out = (dfk * (1.0 - lb) * sz * (1.0 - sz)).astype(BF16)
            return dq_out, dz_out, dv.astype(BF16), dg_out, dgn, dst_new, _colsum(dfk * (1.0 - sz))

        def chunk(cc, carry):
            c = nct - 1 - cc
            r0 = pl.multiple_of(c * ch, ch)
            loaded = [load_head(hh, c, r0) for hh in range(HG_HEADS_PER_STEP)]
            results = _lockstep([one_head(hh, loaded[hh]) for hh in range(HG_HEADS_PER_STEP)])
            for hh in range(HG_HEADS_PER_STEP):
                store_head(hh, r0, results[hh])
            return carry

        lax.fori_loop(0, nct, chunk, 0, unroll=1)

        @pl.when(ti == nt - 1)
        def _():
            dl0 = dlb_sc[...] * lb_all * (1.0 - lb_all)
            dlg_ref[0:1, :] = dl0
            dlg_ref[1:2, :] = -dl0

    def pspec(seg, part):
        return pl.BlockSpec((tm, HG_PBLOCK), lambda h, i: (
            nt - 1 - i, (base + seg * HG_HEADS) * HEAD // HG_PBLOCK + h * npr + part))

    ospec = pl.BlockSpec((tm, wide), lambda h, i: (nt - 1 - i, h))
    return pl.pallas_call(
        body, name="hgrn_bwd",
        grid=(HG_HEADS // hp, nt),
        in_specs=[pspec(seg, part) for seg in range(4) for part in range(npr)]
        + [ospec, pl.BlockSpec((hp, nct, HEAD, HEAD), lambda h, i: (h, nt - 1 - i, 0, 0)),
           pl.BlockSpec((2, wide), lambda h, i: (0, h)),
           pl.BlockSpec((1, HEAD), lambda h, i: (0, 0)),
           pl.BlockSpec((nr, ch), lambda h, i: (0, 0)),
           pl.BlockSpec((n_lev + 1, ch, ch), lambda h, i: (0, 0, 0))],
        out_specs=[ospec, ospec, ospec, ospec,
                   pl.BlockSpec((2, wide), lambda h, i: (0, h)),
                   pl.BlockSpec((hp, 1, HEAD), lambda h, i: (h, 0, 0))],
        out_shape=[jax.ShapeDtypeStruct((tp, D_HGRN), BF16)] * 4
        + [jax.ShapeDtypeStruct((2, D_HGRN), F32), jax.ShapeDtypeStruct((HG_HEADS, 1, HEAD), F32)],
        scratch_shapes=[pltpu.VMEM((hp, HEAD, HEAD), F32), pltpu.VMEM((1, wide), F32)],
        compiler_params=_cparams(2),
    )(*([p] * (4 * npr)), dyb, states, lb_logits, gnorm, mst, msk)


def _tap_views(x, tr, halo, width):
    subs = {0: x}
    views = []
    for j in range(width):
        tiles, rem = divmod(width - 1 - j, SUBLANE)
        if rem not in subs:
            subs[rem] = pltpu.roll(x, rem, 0)
        start = halo - tiles * SUBLANE
        views.append(subs[rem][start:start + tr, :])
    return views


def _tap_views_t(y, tr, halo, width):
    n = tr + halo
    subs = {0: y}
    views = []
    for j in range(width):
        tiles, rem = divmod(width - 1 - j, SUBLANE)
        if rem not in subs:
            subs[rem] = pltpu.roll(y, n - rem, 0)
        views.append(subs[rem][tiles * SUBLANE:tiles * SUBLANE + tr, :])
    return views


def _weighted_sum(views, w_ref):
    acc = None
    for j, view in enumerate(views):
        term = view * w_ref[j:j + 1, :]
        acc = term if acc is None else acc + term
    return acc


def _conv_taps(x, w_ref, tr, halo, width):
    return _weighted_sum(_tap_views(x, tr, halo, width), w_ref)


def _conv_taps_t(y, w_ref, tr, halo, width):
    return _weighted_sum(_tap_views_t(y, tr, halo, width), w_ref)


def _ln_stats(cv):
    mu = jnp.mean(cv, axis=-1, keepdims=True)
    xc = cv - mu
    rstd = lax.rsqrt(jnp.mean(xc * xc, axis=-1, keepdims=True) + EPS)
    return rstd, xc * rstd


def _convmod_fwd(p, w, bias, ln_g, ln_b, tr):
    tp = p.shape[0]
    nt = tp // tr
    nb = D_CONV // HEAD

    def body(a_ref, b_ref, w_ref, bi_ref, g_ref, be_ref, y_ref, usc):
        usc[0:CONV_HALO, :] = jnp.zeros((CONV_HALO, HEAD), F32)
        usc[CONV_HALO:, :] = a_ref[...] * _sigmoid(b_ref[...])

        def tile(r, carry):
            r0 = pl.multiple_of(r * tr, SUBLANE)
            x = usc[pl.ds(r0, tr + CONV_HALO), :]
            cv = _conv_taps(x, w_ref, tr, CONV_HALO, CONV_WIDTH) + bi_ref[...]
            _, xh = _ln_stats(cv)
            un = xh * g_ref[...] + be_ref[...]
            y_ref[pl.ds(r0, tr), :] = un * _sigmoid(un)
            return carry

        lax.fori_loop(0, nt, tile, 0)

    vec = lambda: pl.BlockSpec((1, HEAD), lambda j: (0, j))
    return pl.pallas_call(
        body, name="convmod_fwd",
        grid=(nb,),
        in_specs=[pl.BlockSpec((tp, HEAD), lambda j: (0, j)), pl.BlockSpec((tp, HEAD), lambda j: (0, nb + j)),
                  pl.BlockSpec((CONV_HALO, HEAD), lambda j: (0, j)), vec(), vec(), vec()],
        out_specs=pl.BlockSpec((tp, HEAD), lambda j: (0, j)),
        out_shape=jax.ShapeDtypeStruct((tp, D_CONV), F32),
        scratch_shapes=[pltpu.VMEM((tp + CONV_HALO, HEAD), F32)],
        compiler_params=_cparams(1),
    )(p, p, w, bias, ln_g, ln_b)


def _convmod_bwd(p, dyc, w, bias, ln_g, ln_b, tr):
    tp = p.shape[0]
    nt = tp // tr
    nb = D_CONV // HEAD

    def body(a_ref, b_ref, dy_ref, w_ref, bi_ref, g_ref, be_ref, da_ref, db_ref, dw_ref, dv_ref, usc, dsc):
        usc[0:CONV_HALO, :] = jnp.zeros((CONV_HALO, HEAD), F32)
        usc[CONV_HALO:, :] = a_ref[...] * _sigmoid(b_ref[...])
        dsc[tp:, :] = jnp.zeros((CONV_HALO, HEAD), F32)
        dw_ref[...] = jnp.zeros_like(dw_ref)
        dv_ref[...] = jnp.zeros_like(dv_ref)

        def tile1(r, carry):
            r0 = pl.multiple_of(r * tr, SUBLANE)
            x = usc[pl.ds(r0, tr + CONV_HALO), :]
            views = _tap_views(x, tr, CONV_HALO, CONV_WIDTH)
            cv = _weighted_sum(views, w_ref) + bi_ref[...]
            rstd, xh = _ln_stats(cv)
            un = xh * g_ref[...] + be_ref[...]
            sg = _sigmoid(un)
            dun = dy_ref[pl.ds(r0, tr), :] * (sg * (1.0 + un * (1.0 - sg)))
            dv_ref[0, 1:2, :] += _colsum(dun * xh)
            dv_ref[0, 2:3, :] += _colsum(dun)
            dxh = dun * g_ref[...]
            dcv = rstd * (dxh - jnp.mean(dxh, axis=-1, keepdims=True)
                          - xh * jnp.mean(dxh * xh, axis=-1, keepdims=True))
            dv_ref[0, 0:1, :] += _colsum(dcv)
            for j in range(CONV_WIDTH):
                dw_ref[0, j:j + 1, :] += _colsum(dcv * views[j])
            dsc[pl.ds(r0, tr), :] = dcv
            return carry

        lax.fori_loop(0, nt, tile1, 0)

        def tile2(r, carry):
            r0 = pl.multiple_of(r * tr, SUBLANE)
            y = dsc[pl.ds(r0, tr + CONV_HALO), :]
            du = _conv_taps_t(y, w_ref, tr, CONV_HALO, CONV_WIDTH)
            a = a_ref[pl.ds(r0, tr), :]
            sb = _sigmoid(b_ref[pl.ds(r0, tr), :])
            da_ref[pl.ds(r0, tr), :] = (du * sb).astype(BF16)
            db_ref[pl.ds(r0, tr), :] = (du * a * sb * (1.0 - sb)).astype(BF16)
            return carry

        lax.fori_loop(0, nt, tile2, 0)

    vec = lambda: pl.BlockSpec((1, HEAD), lambda j: (0, j))
    col = lambda: pl.BlockSpec((tp, HEAD), lambda j: (0, j))
    return pl.pallas_call(
        body, name="convmod_bwd",
        grid=(nb,),
        in_specs=[col(), pl.BlockSpec((tp, HEAD), lambda j: (0, nb + j)), col(),
                  pl.BlockSpec((CONV_HALO, HEAD), lambda j: (0, j)), vec(), vec(), vec()],
        out_specs=[col(), col(), pl.BlockSpec((1, CONV_HALO, HEAD), lambda j: (j, 0, 0)),
                   pl.BlockSpec((1, SUBLANE, HEAD), lambda j: (j, 0, 0))],
        out_shape=[jax.ShapeDtypeStruct((tp, D_CONV), BF16), jax.ShapeDtypeStruct((tp, D_CONV), BF16),
                   jax.ShapeDtypeStruct((nb, CONV_HALO, HEAD), F32), jax.ShapeDtypeStruct((nb, SUBLANE, HEAD), F32)],
        scratch_shapes=[pltpu.VMEM((tp + CONV_HALO, HEAD), F32), pltpu.VMEM((tp + CONV_HALO, HEAD), F32)],
        compiler_params=_cparams(1),
    )(p, p, dyc, w, bias, ln_g, ln_b)


def _log1p_small(y):
    return jnp.where(y < 1e-4, y * (1.0 - 0.5 * y), jnp.log(1.0 + y))


def _softplus(x):
    return jnp.maximum(x, 0.0) + _log1p_small(jnp.exp(-jnp.abs(x)))


def _expm1(x):
    return jnp.where(jnp.abs(x) < 1e-2, x * (1.0 + 0.5 * x * (1.0 + x * (1.0 / 3.0))), jnp.exp(x) - 1.0)


def _gelu_parts(x):
    c = 0.7978845608028654
    inner = c * (x + 0.044715 * x * x * x)
    th = jnp.tanh(inner)
    gelu = 0.5 * x * (1.0 + th)
    dgelu = 0.5 * (1.0 + th) + 0.5 * x * (1.0 - th * th) * c * (1.0 + 3.0 * 0.044715 * x * x)
    return gelu, dgelu


def _lru_gates(x_all, tp, cw_ref, cb_ref, wa_ref, ba_ref, wx_ref, bx_ref, lam_ref):
    u = _conv_taps(x_all, cw_ref, tp, LRU_HALO, LRU_CONV) + cb_ref[...]
    u16 = u.astype(BF16)
    r = _sigmoid(_dot(u16, wa_ref[0]) + ba_ref[...])
    i = _sigmoid(_dot(u16, wx_ref[0]) + bx_ref[...])
    sp = _softplus(-lam_ref[...])
    la = -LRU_C * r * sp
    a = jnp.exp(la)
    mult = jnp.sqrt(-_expm1(2.0 * la))
    return u, r, i, a, mult, sp


def _lru_specs(tp, nb):
    col = lambda k: pl.BlockSpec((tp, HEAD), functools.partial(lambda j, k: (0, k * nb + j), k=k))
    vec = lambda: pl.BlockSpec((1, HEAD), lambda j: (0, j))
    mat = lambda: pl.BlockSpec((1, HEAD, HEAD), lambda j: (j, 0, 0))
    return col, vec, mat


def _lru_fwd(p, cw, cb, wa, ba, wx, bx, lam):
    tp = p.shape[0]
    nb = D_LRU // HEAD
    ng = tp // SUBLANE

    def body(x_ref, gt_ref, cw_ref, cb_ref, wa_ref, ba_ref, wx_ref, bx_ref, lam_ref, y_ref, hs_ref,
             xsc, asc, bsc):
        xsc[0:LRU_HALO, :] = jnp.zeros((LRU_HALO, HEAD), F32)
        xsc[LRU_HALO:, :] = x_ref[...]
        u, r, i, a, mult, _ = _lru_gates(xsc[...], tp, cw_ref, cb_ref, wa_ref, ba_ref, wx_ref, bx_ref, lam_ref)
        rows = lax.broadcasted_iota(jnp.int32, (tp, HEAD), 0)
        b = jnp.where(rows == 0, 1.0, mult) * (i * u)
        sub = rows % SUBLANE
        for k in (1, 2, 4):
            m = sub >= k
            b = jnp.where(m, a * pltpu.roll(b, k, 0) + b, b)
            a = jnp.where(m, a * pltpu.roll(a, k, 0), a)
        asc[...] = a
        bsc[...] = b

        def grp(g, carry):
            r0 = pl.multiple_of(g * SUBLANE, SUBLANE)
            h = bsc[pl.ds(r0, SUBLANE), :] + asc[pl.ds(r0, SUBLANE), :] * carry
            hs_ref[pl.ds(r0, SUBLANE), :] = h
            return jnp.broadcast_to(h[SUBLANE - 1:SUBLANE, :], (SUBLANE, HEAD))

        lax.fori_loop(0, ng, grp, jnp.zeros((SUBLANE, HEAD), F32))
        gelu, _ = _gelu_parts(gt_ref[...])
        y_ref[...] = gelu * hs_ref[...]

    col, vec, mat = _lru_specs(tp, nb)
    return pl.pallas_call(
        body, name="lru_fwd",
        grid=(nb,),
        in_specs=[col(2), col(3), pl.BlockSpec((LRU_CONV, HEAD), lambda j: (0, j)), vec(), mat(), vec(), mat(),
                  vec(), vec()],
        out_specs=[pl.BlockSpec((tp, HEAD), lambda j: (0, j)), pl.BlockSpec((tp, HEAD), lambda j: (0, j))],
        out_shape=[jax.ShapeDtypeStruct((tp, D_LRU), F32), jax.ShapeDtypeStruct((tp, D_LRU), F32)],
        scratch_shapes=[pltpu.VMEM((tp + LRU_HALO, HEAD), F32), pltpu.VMEM((tp, HEAD), F32),
                        pltpu.VMEM((tp, HEAD), F32)],
        compiler_params=_cparams(1),
    )(p, p, cw, cb, wa, ba, wx, bx, lam)


def _lru_bwd(p, hs, dyd, cw, cb, wa, ba, wx, bx, lam):
    tp = p.shape[0]
    nb = D_LRU // HEAD
    ng = tp // SUBLANE

    def body(x_ref, gt_ref, hs_ref, dy_ref, cw_ref, cb_ref, wa_ref, ba_ref, wx_ref, bx_ref, lam_ref,
             dx_ref, dgt_ref, dwa_ref, dwx_ref, dv_ref, xsc, asc, bsc, gsc, dusc):
        xsc[0:LRU_HALO, :] = jnp.zeros((LRU_HALO, HEAD), F32)
        xsc[LRU_HALO:, :] = x_ref[...]
        x_all = xsc[...]
        u, r, i, a, mult, sp = _lru_gates(x_all, tp, cw_ref, cb_ref, wa_ref, ba_ref, wx_ref, bx_ref, lam_ref)
        rows = lax.broadcasted_iota(jnp.int32, (tp, HEAD), 0)
        hs = hs_ref[...]
        dy = dy_ref[...]
        gelu, dgelu = _gelu_parts(gt_ref[...])
        dgt_ref[...] = (dy * hs * dgelu).astype(BF16)
        bb = dy * gelu
        aa = jnp.where(rows == tp - 1, 0.0, pltpu.roll(a, tp - 1, 0))
        sub = rows % SUBLANE
        for k in (1, 2, 4):
            m = sub < SUBLANE - k
            bb = jnp.where(m, aa * pltpu.roll(bb, tp - k, 0) + bb, bb)
            aa = jnp.where(m, aa * pltpu.roll(aa, tp - k, 0), aa)
        asc[...] = aa
        bsc[...] = bb

        def grp(gi, carry):
            g = ng - 1 - gi
            r0 = pl.multiple_of(g * SUBLANE, SUBLANE)
            gg = bsc[pl.ds(r0, SUBLANE), :] + asc[pl.ds(r0, SUBLANE), :] * carry
            gsc[pl.ds(r0, SUBLANE), :] = gg
            return jnp.broadcast_to(gg[0:1, :], (SUBLANE, HEAD))

        lax.fori_loop(0, ng, grp, jnp.zeros((SUBLANE, HEAD), F32))
        g = gsc[...]
        first = rows == 0
        hprev = jnp.where(first, 0.0, pltpu.roll(hs, 1, 0))
        iu = i * u
        d_iu = g * jnp.where(first, 1.0, mult)
        dmult_term = jnp.where(first, 0.0, g * iu * (-(a * a) / mult))
        dla = g * hprev * a + dmult_term
        dr = dla * (-LRU_C) * sp
        dv_ref[0, 7:8, :] = _colsum(dla * (LRU_C * r) * _sigmoid(-lam_ref[...]))
        dpr = dr * r * (1.0 - r)
        dpi = d_iu * u * i * (1.0 - i)
        dv_ref[0, 5:6, :] = _colsum(dpr)
        dv_ref[0, 6:7, :] = _colsum(dpi)
        u16 = u.astype(BF16)
        dpr16 = dpr.astype(BF16)
        dpi16 = dpi.astype(BF16)
        dwa_ref[0] = _dot_tn(u16, dpr16)
        dwx_ref[0] = _dot_tn(u16, dpi16)
        du = d_iu * i + _dot_nt(dpr16, wa_ref[0]) + _dot_nt(dpi16, wx_ref[0])
        dv_ref[0, 4:5, :] = _colsum(du)
        for j in range(LRU_CONV):
            sh = LRU_CONV - 1 - j
            xs = x_all if sh == 0 else pltpu.roll(x_all, sh, 0)
            dv_ref[0, j:j + 1, :] = _colsum(du * xs[LRU_HALO:, :])
        dusc[0:tp, :] = du
        dusc[tp:, :] = jnp.zeros((LRU_HALO, HEAD), F32)
        dx_ref[...] = _conv_taps_t(dusc[...], cw_ref, tp, LRU_HALO, LRU_CONV).astype(BF16)

    col, vec, mat = _lru_specs(tp, nb)
    ocol = lambda: pl.BlockSpec((tp, HEAD), lambda j: (0, j))
    return pl.pallas_call(
        body, name="lru_bwd",
        grid=(nb,),
        in_specs=[col(2), col(3), ocol(), ocol(), pl.BlockSpec((LRU_CONV, HEAD), lambda j: (0, j)), vec(), mat(),
                  vec(), mat(), vec(), vec()],
        out_specs=[ocol(), ocol(), mat(), mat(), pl.BlockSpec((1, SUBLANE, HEAD), lambda j: (j, 0, 0))],
        out_shape=[jax.ShapeDtypeStruct((tp, D_LRU), BF16), jax.ShapeDtypeStruct((tp, D_LRU), BF16),
                   jax.ShapeDtypeStruct((nb, HEAD, HEAD), F32), jax.ShapeDtypeStruct((nb, HEAD, HEAD), F32),
                   jax.ShapeDtypeStruct((nb, SUBLANE, HEAD), F32)],
        scratch_shapes=[pltpu.VMEM((tp + LRU_HALO, HEAD), F32), pltpu.VMEM((tp, HEAD), F32),
                        pltpu.VMEM((tp, HEAD), F32), pltpu.VMEM((tp, HEAD), F32),
                        pltpu.VMEM((tp + LRU_HALO, HEAD), F32)],
        compiler_params=_cparams(1),
    )(p, p, hs, dyd, cw, cb, wa, ba, wx, bx, lam)


def _mesh_pos():
    return lax.axis_index("x"), lax.axis_index("y"), lax.axis_index("c")


def _other_chips(x, y):
    return [(1 - x, y), (x, 1 - y), (1 - x, 1 - y)]


ANY = pl.BlockSpec(memory_space=pl.ANY)


def _allgather_shards(arrs, split):
    n = len(arrs)

    def body(*refs):
        ins, outs = refs[:n], refs[n:2 * n]
        send1, recv1, send2, recv2, send3, recv3 = refs[2 * n:]
        x, y, c = _mesh_pos()
        chip = 2 * x + y
        sibling = (x, y, 1 - c)
        others = _other_chips(x, y)

        def rows(k, cc):
            half = arrs[k].shape[0] // 2
            return pl.ds(cc * half, half)

        def remote(src, dst, ssem, rsem, dev):
            return pltpu.make_async_remote_copy(src_ref=src, dst_ref=dst, send_sem=ssem, recv_sem=rsem,
                                                device_id=dev, device_id_type=MESH)

        started = [remote(ins[k], outs[k].at[chip], send3.at[k], recv3.at[k], sibling) for k in range(n)]
        for cp in started:
            cp.start()
        for k in range(n):
            for j, (ox, oy) in enumerate(others):
                if split[k]:
                    src, dst = ins[k].at[rows(k, c)], outs[k].at[chip, rows(k, c)]
                else:
                    src, dst = ins[k], outs[k].at[chip]
                cp = remote(src, dst, send1.at[3 * k + j], recv1.at[3 * k + j], (ox, oy, c))
                cp.start()
                started.append(cp)
        for j, (ox, oy) in enumerate(others):
            ochip = 2 * ox + oy
            for k in range(n):
                if split[k]:
                    blk = outs[k].at[ochip, rows(k, c)]
                    remote(blk, blk, send1.at[3 * k + j], recv1.at[3 * k + j], sibling).wait_recv()
                    cp = remote(blk, blk, send2.at[3 * k + j], recv2.at[3 * k + j], sibling)
                    cp.start()
                    started.append(cp)
                else:
                    blk = outs[k].at[ochip]
                    remote(blk, blk, send1.at[3 * k + j], recv1.at[3 * k + j], sibling).wait_recv()
        for j, (ox, oy) in enumerate(others):
            ochip = 2 * ox + oy
            for k in range(n):
                if split[k]:
                    blk = outs[k].at[ochip, rows(k, 1 - c)]
                    remote(blk, blk, send2.at[3 * k + j], recv2.at[3 * k + j], sibling).wait_recv()
        for k in range(n):
            blk = outs[k].at[chip]
            remote(blk, blk, send3.at[k], recv3.at[k], sibling).wait_recv()
        for cp in started:
            cp.wait_send()

    return pl.pallas_call(
        body, name="allgather_shards",
        in_specs=[ANY] * n, out_specs=[ANY] * n,
        out_shape=[jax.ShapeDtypeStruct((N_SHARD,) + a.shape, a.dtype) for a in arrs],
        scratch_shapes=[pltpu.SemaphoreType.DMA((3 * n,)), pltpu.SemaphoreType.DMA((3 * n,)),
                        pltpu.SemaphoreType.DMA((3 * n,)), pltpu.SemaphoreType.DMA((3 * n,)),
                        pltpu.SemaphoreType.DMA((n,)), pltpu.SemaphoreType.DMA((n,))],
    )(*arrs)


HBM = pl.BlockSpec(memory_space=pltpu.HBM)
SEM = pl.BlockSpec(memory_space=pltpu.SEMAPHORE)
DATAFLOW = pltpu.SideEffectType.DATAFLOW_SIDE_EFFECTING
N_PEERS = 4


def _in_hbm(a):
    return pltpu.with_memory_space_constraint(a, pltpu.HBM)


def _gather_peers(x, y, c):
    return [((ox, oy, c), 2 * ox + oy) for ox, oy in _other_chips(x, y)] + [((x, y, 1 - c), 2 * x + y)]


def _gather_refs(src, land, slot, c, split):
    if not split:
        return src, land.at[slot]
    half = src.shape[0] // 2
    return src.at[pl.ds(c * half, half)], land.at[slot, pl.ds(c * half, half)]


def _gather_start(arrs, split):
    n = len(arrs)

    def body(*refs):
        ins, lands = refs[:n], refs[n:2 * n]
        ssem, rsem = refs[2 * n:2 * n + 2]
        token = refs[-1]
        x, y, c = _mesh_pos()
        chip = 2 * x + y
        for k in range(n):
            for j, (dev, _) in enumerate(_gather_peers(x, y, c)):
                src, dst = _gather_refs(ins[k], lands[k], chip, c, split[k] and j < N_PEERS - 1)
                pltpu.make_async_remote_copy(
                    src_ref=src, dst_ref=dst, send_sem=ssem.at[N_PEERS * k + j],
                    recv_sem=rsem.at[N_PEERS * k + j], device_id=dev, device_id_type=MESH).start()
        token[...] = jnp.zeros_like(token)

    lands = [_in_hbm(lax.empty((N_SHARD,) + a.shape, a.dtype)) for a in arrs]
    out = pl.pallas_call(
        body, name="gather_start",
        in_specs=[HBM] * (2 * n),
        out_specs=[SEM, SEM] + [HBM] * (2 * n) + [pl.BlockSpec(memory_space=pltpu.VMEM)],
        out_shape=[pltpu.SemaphoreType.DMA((N_PEERS * n,)), pltpu.SemaphoreType.DMA((N_PEERS * n,))]
        + [pltpu.HBM(a.shape, a.dtype) for a in arrs]
        + [pltpu.HBM((N_SHARD,) + a.shape, a.dtype) for a in arrs]
        + [jax.ShapeDtypeStruct((SUBLANE, LANE), F32)],
        input_output_aliases={k: 2 + k for k in range(2 * n)},
        compiler_params=pltpu.CompilerParams(has_side_effects=DATAFLOW),
    )(*[_in_hbm(a) for a in arrs], *lands)
    return out[0], out[1], list(out[2:2 + n]), list(out[2 + n:2 + 2 * n]), out[-1]


def _gather_wait(ssem, rsem, srcs, lands, ks, after, split=False):
    n = len(ks)

    def body(*refs):
        ins, lnd = refs[:n], refs[n:2 * n]
        ssem_ref, rsem_ref = refs[2 * n:2 * n + 2]
        x, y, c = _mesh_pos()
        for i, k in enumerate(ks):
            for j, (dev, pchip) in enumerate(_gather_peers(x, y, c)):
                src, dst = _gather_refs(ins[i], lnd[i], pchip, c, split and j < N_PEERS - 1)
                cp = pltpu.make_async_remote_copy(
                    src_ref=src, dst_ref=dst, send_sem=ssem_ref.at[N_PEERS * k + j],
                    recv_sem=rsem_ref.at[N_PEERS * k + j], device_id=dev, device_id_type=MESH)
                cp.wait_send()
                cp.wait_recv()

    out = pl.pallas_call(
        body, name="gather_wait",
        in_specs=[HBM] * (2 * n) + [SEM, SEM] + [ANY] * len(after),
        out_specs=[HBM] * (2 * n),
        out_shape=[pltpu.HBM(a.shape, a.dtype) for a in srcs] + [pltpu.HBM(a.shape, a.dtype) for a in lands],
        input_output_aliases={k: k for k in range(2 * n)},
        compiler_params=pltpu.CompilerParams(has_side_effects=DATAFLOW),
    )(*srcs, *lands, ssem, rsem, *after)
    return list(out[n:])


def _pair_forward(lands):
    n = len(lands)

    def body(*refs):
        outs = refs[n:2 * n]
        ssem, rsem = refs[2 * n:]
        x, y, c = _mesh_pos()
        sibling = (x, y, 1 - c)
        cps = []
        for k in range(n):
            half = lands[k].shape[1] // 2
            for j, (ox, oy) in enumerate(_other_chips(x, y)):
                mine = outs[k].at[2 * ox + oy, pl.ds(c * half, half)]
                cp = pltpu.make_async_remote_copy(src_ref=mine, dst_ref=mine, send_sem=ssem.at[3 * k + j],
                                                  recv_sem=rsem.at[3 * k + j], device_id=sibling, device_id_type=MESH)
                cp.start()
                cps.append(cp)
        for k in range(n):
            half = lands[k].shape[1] // 2
            for j, (ox, oy) in enumerate(_other_chips(x, y)):
                theirs = outs[k].at[2 * ox + oy, pl.ds((1 - c) * half, half)]
                pltpu.make_async_remote_copy(src_ref=theirs, dst_ref=theirs, send_sem=ssem.at[3 * k + j],
                                             recv_sem=rsem.at[3 * k + j], device_id=sibling,
                                             device_id_type=MESH).wait_recv()
        for cp in cps:
            cp.wait_send()

    return pl.pallas_call(
        body, name="pair_forward",
        in_specs=[ANY] * n, out_specs=[ANY] * n,
        out_shape=[jax.ShapeDtypeStruct(a.shape, a.dtype) for a in lands],
        scratch_shapes=[pltpu.SemaphoreType.DMA((3 * n,)), pltpu.SemaphoreType.DMA((3 * n,))],
        input_output_aliases={k: k for k in range(n)},
    )(*lands)


N_SOURCES = 7


def _reduce_peers(x, y, c):
    peers = []
    for ox, oy in _other_chips(x, y):
        for rel in range(2):
            peers.append(((ox, oy, c + rel - 2 * c * rel), 2 * ox + oy))
    peers.append(((x, y, 1 - c), 2 * x + y))
    return peers


def _reduce_start(arrs, slots):
    n = len(arrs)

    def body(*refs):
        ins, lands = refs[:n], refs[n:2 * n]
        ssem, rsem = refs[2 * n:2 * n + 2]
        token = refs[-1]
        x, y, c = _mesh_pos()
        me = 2 * (2 * x + y) + c
        for k in range(n):
            half = arrs[k].shape[1] // 2
            for p, (dev, ochip) in enumerate(_reduce_peers(x, y, c)):
                pltpu.make_async_remote_copy(
                    src_ref=ins[k].at[ochip, pl.ds(dev[2] * half, half)], dst_ref=lands[k].at[me],
                    send_sem=ssem.at[N_SOURCES * k + p], recv_sem=rsem.at[N_SOURCES * k + p],
                    device_id=dev, device_id_type=MESH).start()
        token[...] = jnp.zeros_like(token)

    out = pl.pallas_call(
        body, name="reduce_start",
        in_specs=[HBM] * (2 * n),
        out_specs=[SEM, SEM] + [HBM] * (2 * n) + [pl.BlockSpec(memory_space=pltpu.VMEM)],
        out_shape=[pltpu.SemaphoreType.DMA((N_SOURCES * n,)), pltpu.SemaphoreType.DMA((N_SOURCES * n,))]
        + [pltpu.HBM(a.shape, a.dtype) for a in arrs] + [pltpu.HBM(a.shape, a.dtype) for a in slots]
        + [jax.ShapeDtypeStruct((SUBLANE, LANE), F32)],
        input_output_aliases={k: 2 + k for k in range(2 * n)},
        compiler_params=pltpu.CompilerParams(has_side_effects=DATAFLOW),
    )(*[_in_hbm(a) for a in arrs], *[_in_hbm(a) for a in slots])
    return out[0], out[1], list(out[2:2 + n]), list(out[2 + n:2 + 2 * n]), out[-1]


def _reduce_wait(ssem, rsem, arrs, slots, after):
    n = len(arrs)

    def body(*refs):
        ins, lnd = refs[:n], refs[n:2 * n]
        ssem_ref, rsem_ref = refs[2 * n:2 * n + 2]
        x, y, c = _mesh_pos()
        for k in range(n):
            half = arrs[k].shape[1] // 2
            for p, (dev, ochip) in enumerate(_reduce_peers(x, y, c)):
                cp = pltpu.make_async_remote_copy(
                    src_ref=ins[k].at[ochip, pl.ds(dev[2] * half, half)], dst_ref=lnd[k].at[2 * ochip + dev[2]],
                    send_sem=ssem_ref.at[N_SOURCES * k + p], recv_sem=rsem_ref.at[N_SOURCES * k + p],
                    device_id=dev, device_id_type=MESH)
                cp.wait_send()
                cp.wait_recv()

    out = pl.pallas_call(
        body, name="reduce_wait",
        in_specs=[HBM] * (2 * n) + [SEM, SEM] + [ANY] * len(after),
        out_specs=[HBM] * (2 * n),
        out_shape=[pltpu.HBM(a.shape, a.dtype) for a in arrs] + [pltpu.HBM(a.shape, a.dtype) for a in slots],
        input_output_aliases={k: k for k in range(2 * n)},
        compiler_params=pltpu.CompilerParams(has_side_effects=DATAFLOW),
    )(*arrs, *slots, ssem, rsem, *after)
    return list(out[n:])


def _own_part(arrs, chip, core, me):
    n = len(arrs)
    nb = GRAD_ROW_BLOCKS

    def body(chip_ref, core_ref, me_ref, *refs):
        for k in range(n):
            refs[n + k][...] = refs[k][...]

    def blk(a):
        return (1, a.shape[1] // 2 // nb, a.shape[2])

    grid_spec = pltpu.PrefetchScalarGridSpec(
        num_scalar_prefetch=3, grid=(nb,),
        in_specs=[pl.BlockSpec(blk(a), lambda i, ch, co, me: (ch[0], co[0] * nb + i, 0)) for a in arrs],
        out_specs=[pl.BlockSpec(blk(a), lambda i, ch, co, me: (me[0], i, 0)) for a in arrs])
    return pl.pallas_call(
        body, name="own_part", grid_spec=grid_spec,
        out_shape=[jax.ShapeDtypeStruct((N_DEV, a.shape[1] // 2, a.shape[2]), a.dtype) for a in arrs],
        compiler_params=_cparams(1),
    )(chip, core, me, *arrs)


def _sum_devices(arrs, core):
    n = len(arrs)
    nb = GRAD_ROW_BLOCKS

    def body(c_ref, *refs):
        for k in range(n):
            r = refs[k]
            acc = r[0].astype(F32)
            for dev in range(1, N_DEV):
                acc = acc + r[dev].astype(F32)
            refs[n + k][...] = acc

    grid_spec = pltpu.PrefetchScalarGridSpec(
        num_scalar_prefetch=1, grid=(nb,),
        in_specs=[pl.BlockSpec((N_DEV, a.shape[1] // nb, a.shape[2]), lambda i, c: (0, i, 0)) for a in arrs],
        out_specs=[pl.BlockSpec((a.shape[1] // nb, a.shape[2]), lambda i, c: (c[0] * nb + i, 0)) for a in arrs])
    return pl.pallas_call(
        body, name="sum_devices", grid_spec=grid_spec,
        out_shape=[jax.ShapeDtypeStruct((2 * a.shape[1], a.shape[2]), F32) for a in arrs],
        compiler_params=_cparams(1),
    )(core, *arrs)


def _small_own(v, me):
    m = v.shape[0]

    def body(me_ref, v_ref, o_ref):
        o_ref[0] = v_ref[...]

    grid_spec = pltpu.PrefetchScalarGridSpec(
        num_scalar_prefetch=1, grid=(1,),
        in_specs=[pl.BlockSpec((m, LANE), lambda i, me: (0, 0))],
        out_specs=pl.BlockSpec((1, m, LANE), lambda i, me: (me[0], 0, 0)))
    return pl.pallas_call(
        body, name="small_own", grid_spec=grid_spec,
        out_shape=jax.ShapeDtypeStruct((N_DEV, m, LANE), v.dtype),
        compiler_params=_cparams(1),
    )(me, v)


def _small_start(v, slots):
    def body(v_ref, land, ssem, rsem, v_thru, land_thru, token):
        del v_thru, land_thru
        x, y, c = _mesh_pos()
        me = 2 * (2 * x + y) + c
        for p, (dev, _) in enumerate(_reduce_peers(x, y, c)):
            pltpu.make_async_remote_copy(src_ref=v_ref, dst_ref=land.at[me], send_sem=ssem.at[p],
                                         recv_sem=rsem.at[p], device_id=dev, device_id_type=MESH).start()
        token[...] = jnp.zeros_like(token)

    out = pl.pallas_call(
        body, name="small_start",
        in_specs=[HBM, HBM],
        out_specs=[SEM, SEM, HBM, HBM, pl.BlockSpec(memory_space=pltpu.VMEM)],
        out_shape=[pltpu.SemaphoreType.DMA((N_SOURCES,)), pltpu.SemaphoreType.DMA((N_SOURCES,)),
                   pltpu.HBM(v.shape, v.dtype), pltpu.HBM(slots.shape, slots.dtype),
                   jax.ShapeDtypeStruct((SUBLANE, LANE), F32)],
        input_output_aliases={0: 2, 1: 3},
        compiler_params=pltpu.CompilerParams(has_side_effects=DATAFLOW),
    )(_in_hbm(v), _in_hbm(slots))
    return out


def _small_wait(ssem, rsem, v, slots, after):
    def body(*refs):
        v_ref, land, ssem_ref, rsem_ref = refs[:4]
        x, y, c = _mesh_pos()
        for p, (dev, ochip) in enumerate(_reduce_peers(x, y, c)):
            cp = pltpu.make_async_remote_copy(src_ref=v_ref, dst_ref=land.at[2 * ochip + dev[2]],
                                              send_sem=ssem_ref.at[p], recv_sem=rsem_ref.at[p],
                                              device_id=dev, device_id_type=MESH)
            cp.wait_send()
            cp.wait_recv()

    out = pl.pallas_call(
        body, name="small_wait",
        in_specs=[HBM, HBM, SEM, SEM] + [ANY] * len(after),
        out_specs=[HBM, HBM],
        out_shape=[pltpu.HBM(v.shape, v.dtype), pltpu.HBM(slots.shape, slots.dtype)],
        input_output_aliases={0: 0, 1: 1},
        compiler_params=pltpu.CompilerParams(has_side_effects=DATAFLOW),
    )(v, slots, ssem, rsem, *after)
    return out[1]


def _scatter_start(arrs, slots):
    n = len(arrs)

    def body(*refs):
        ins, lands = refs[:n], refs[n:2 * n]
        ssem, rsem = refs[2 * n:2 * n + 2]
        token = refs[-1]
        x, y, c = _mesh_pos()
        chip = 2 * x + y
        for k in range(n):
            for j, (ox, oy) in enumerate(_other_chips(x, y)):
                pltpu.make_async_remote_copy(
                    src_ref=ins[k].at[2 * ox + oy], dst_ref=lands[k].at[chip], send_sem=ssem.at[3 * k + j],
                    recv_sem=rsem.at[3 * k + j], device_id=(ox, oy, c), device_id_type=MESH).start()
        token[...] = jnp.zeros_like(token)

    out = pl.pallas_call(
        body, name="scatter_start",
        in_specs=[HBM] * (2 * n),
        out_specs=[SEM, SEM] + [HBM] * (2 * n) + [pl.BlockSpec(memory_space=pltpu.VMEM)],
        out_shape=[pltpu.SemaphoreType.DMA((3 * n,)), pltpu.SemaphoreType.DMA((3 * n,))]
        + [pltpu.HBM(a.shape, a.dtype) for a in arrs] + [pltpu.HBM(a.shape, a.dtype) for a in slots]
        + [jax.ShapeDtypeStruct((SUBLANE, LANE), F32)],
        input_output_aliases={k: 2 + k for k in range(2 * n)},
        compiler_params=pltpu.CompilerParams(has_side_effects=DATAFLOW),
    )(*[_in_hbm(a) for a in arrs], *[_in_hbm(a) for a in slots])
    return out[0], out[1], list(out[2:2 + n]), list(out[2 + n:2 + 2 * n]), out[-1]


def _scatter_wait(ssem, rsem, arrs, slots, after):
    n = len(arrs)

    def body(*refs):
        ins, lnd = refs[:n], refs[n:2 * n]
        ssem_ref, rsem_ref = refs[2 * n:2 * n + 2]
        x, y, c = _mesh_pos()
        for k in range(n):
            for j, (ox, oy) in enumerate(_other_chips(x, y)):
                ochip = 2 * ox + oy
                cp = pltpu.make_async_remote_copy(
                    src_ref=ins[k].at[ochip], dst_ref=lnd[k].at[ochip], send_sem=ssem_ref.at[3 * k + j],
                    recv_sem=rsem_ref.at[3 * k + j], device_id=(ox, oy, c), device_id_type=MESH)
                cp.wait_send()
                cp.wait_recv()

    out = pl.pallas_call(
        body, name="scatter_wait",
        in_specs=[HBM] * (2 * n) + [SEM, SEM] + [ANY] * len(after),
        out_specs=[HBM] * (2 * n),
        out_shape=[pltpu.HBM(a.shape, a.dtype) for a in arrs] + [pltpu.HBM(a.shape, a.dtype) for a in slots],
        input_output_aliases={k: k for k in range(2 * n)},
        compiler_params=pltpu.CompilerParams(has_side_effects=DATAFLOW),
    )(*arrs, *slots, ssem, rsem, *after)
    return list(out[n:])


def _pair_exchange_halves(arrs):
    n = len(arrs)

    def body(*refs):
        ins, outs = refs[:n], refs[n:2 * n]
        ssem, rsem = refs[2 * n:]
        x, y, c = _mesh_pos()
        cps = []
        for k in range(n):
            half = arrs[k].shape[1] // 2
            cp = pltpu.make_async_remote_copy(
                src_ref=ins[k].at[:, pl.ds((1 - c) * half, half)], dst_ref=outs[k],
                send_sem=ssem.at[k], recv_sem=rsem.at[k], device_id=(x, y, 1 - c), device_id_type=MESH)
            cp.start()
            cps.append(cp)
        for cp in cps:
            cp.wait()

    return pl.pallas_call(
        body, name="pair_exchange_halves",
        in_specs=[ANY] * n, out_specs=[ANY] * n,
        out_shape=[jax.ShapeDtypeStruct((a.shape[0], a.shape[1] // 2, a.shape[2]), a.dtype) for a in arrs],
        scratch_shapes=[pltpu.SemaphoreType.DMA((n,)), pltpu.SemaphoreType.DMA((n,))],
    )(*arrs)


GRAD_ROW_BLOCKS = 2


def _pair_add(arrs, recvd, core):
    n = len(arrs)
    nb = GRAD_ROW_BLOCKS

    def body(c_ref, *refs):
        for k in range(n):
            refs[2 * n + k][...] = (refs[k][...].astype(F32) + refs[n + k][...].astype(F32)).astype(BF16)

    def blk(a):
        return (1, a.shape[1] // 2 // nb, a.shape[2])

    grid_spec = pltpu.PrefetchScalarGridSpec(
        num_scalar_prefetch=1, grid=(N_SHARD, nb),
        in_specs=[pl.BlockSpec(blk(a), lambda s, i, c: (s, c[0] * nb + i, 0)) for a in arrs]
        + [pl.BlockSpec(blk(a), lambda s, i, c: (s, i, 0)) for a in arrs],
        out_specs=[pl.BlockSpec(blk(a), lambda s, i, c: (s, i, 0)) for a in arrs])
    return pl.pallas_call(
        body, name="pair_add", grid_spec=grid_spec,
        out_shape=[jax.ShapeDtypeStruct(r.shape, BF16) for r in recvd],
        compiler_params=_cparams(2),
    )(core, *arrs, *recvd)


def _own_slot(arrs, chip):
    n = len(arrs)
    nb = GRAD_ROW_BLOCKS

    def body(c_ref, *refs):
        for k in range(n):
            refs[n + k][...] = refs[k][...]

    def blk(a):
        return (1, a.shape[1] // nb, a.shape[2])

    grid_spec = pltpu.PrefetchScalarGridSpec(
        num_scalar_prefetch=1, grid=(nb,),
        in_specs=[pl.BlockSpec(blk(a), lambda i, c: (c[0], i, 0)) for a in arrs],
        out_specs=[pl.BlockSpec(blk(a), lambda i, c: (c[0], i, 0)) for a in arrs])
    return pl.pallas_call(
        body, name="own_slot", grid_spec=grid_spec,
        out_shape=[jax.ShapeDtypeStruct(a.shape, a.dtype) for a in arrs],
        compiler_params=_cparams(1),
    )(chip, *arrs)


def _scatter_to_owners(arrs, slots):
    n = len(arrs)

    def body(*refs):
        ins, outs = refs[:n], refs[2 * n:3 * n]
        ssem, rsem = refs[3 * n:]
        x, y, c = _mesh_pos()
        chip = 2 * x + y
        others = _other_chips(x, y)
        cps = []
        for k in range(n):
            for j, (ox, oy) in enumerate(others):
                cp = pltpu.make_async_remote_copy(
                    src_ref=ins[k].at[2 * ox + oy], dst_ref=outs[k].at[chip],
                    send_sem=ssem.at[3 * k + j], recv_sem=rsem.at[3 * k + j],
                    device_id=(ox, oy, c), device_id_type=MESH)
                cp.start()
                cps.append(cp)
        for k in range(n):
            for j, (ox, oy) in enumerate(others):
                blk = outs[k].at[2 * ox + oy]
                pltpu.make_async_remote_copy(
                    src_ref=blk, dst_ref=blk, send_sem=ssem.at[3 * k + j], recv_sem=rsem.at[3 * k + j],
                    device_id=(ox, oy, c), device_id_type=MESH).wait_recv()
        for cp in cps:
            cp.wait_send()

    return pl.pallas_call(
        body, name="scatter_to_owners",
        in_specs=[ANY] * (2 * n), out_specs=[ANY] * n,
        out_shape=[jax.ShapeDtypeStruct(a.shape, a.dtype) for a in arrs],
        scratch_shapes=[pltpu.SemaphoreType.DMA((3 * n,)), pltpu.SemaphoreType.DMA((3 * n,))],
        input_output_aliases={n + k: k for k in range(n)},
    )(*arrs, *slots)


def _sum_chips(arrs, core):
    n = len(arrs)
    nb = GRAD_ROW_BLOCKS

    def body(c_ref, *refs):
        for k in range(n):
            r = refs[k]
            refs[n + k][...] = ((r[0].astype(F32) + r[1].astype(F32)) + r[2].astype(F32)) + r[3].astype(F32)

    grid_spec = pltpu.PrefetchScalarGridSpec(
        num_scalar_prefetch=1, grid=(nb,),
        in_specs=[pl.BlockSpec((N_SHARD, a.shape[1] // nb, a.shape[2]), lambda i, c: (0, i, 0)) for a in arrs],
        out_specs=[pl.BlockSpec((a.shape[1] // nb, a.shape[2]), lambda i, c: (c[0] * nb + i, 0)) for a in arrs])
    return pl.pallas_call(
        body, name="sum_chips", grid_spec=grid_spec,
        out_shape=[jax.ShapeDtypeStruct((2 * a.shape[1], a.shape[2]), F32) for a in arrs],
        compiler_params=_cparams(1),
    )(core, *arrs)


def _pair_allgather_halves(arrs):
    n = len(arrs)

    def body(*refs):
        outs = refs[n:2 * n]
        ssem, rsem = refs[2 * n:]
        x, y, c = _mesh_pos()
        cps = []
        for k in range(n):
            h = arrs[k].shape[0] // 2
            mine = outs[k].at[pl.ds(c * h, h)]
            cp = pltpu.make_async_remote_copy(src_ref=mine, dst_ref=mine, send_sem=ssem.at[k],
                                              recv_sem=rsem.at[k], device_id=(x, y, 1 - c), device_id_type=MESH)
            cp.start()
            cps.append(cp)
        for k, cp in enumerate(cps):
            h = arrs[k].shape[0] // 2
            theirs = outs[k].at[pl.ds((1 - c) * h, h)]
            pltpu.make_async_remote_copy(src_ref=theirs, dst_ref=theirs, send_sem=ssem.at[k], recv_sem=rsem.at[k],
                                         device_id=(x, y, 1 - c), device_id_type=MESH).wait_recv()
            cp.wait_send()

    return pl.pallas_call(
        body, name="pair_allgather_halves",
        in_specs=[ANY] * n, out_specs=[ANY] * n,
        out_shape=[jax.ShapeDtypeStruct(a.shape, a.dtype) for a in arrs],
        scratch_shapes=[pltpu.SemaphoreType.DMA((n,)), pltpu.SemaphoreType.DMA((n,))],
        input_output_aliases={k: k for k in range(n)},
    )(*arrs)


N_DEV = 8


def _allgather_all(v):
    m_per, n = v.shape

    def body(x_ref, out_ref, send_sems, recv_sems, local_sem):
        x, y, c = _mesh_pos()
        me, sibling = (x, y, c), (x, y, 1 - c)
        chips = _other_chips(x, y)

        def rows(px, py, pc):
            return out_ref.at[pl.ds((4 * px + 2 * py + pc) * m_per, m_per), :]

        def copy(k, block, to, src=None):
            return pltpu.make_async_remote_copy(
                src_ref=rows(*block) if src is None else src, dst_ref=rows(*block),
                send_sem=send_sems.at[k], recv_sem=recv_sems.at[k], device_id=to, device_id_type=MESH)

        mine = pltpu.make_async_copy(x_ref, rows(*me), local_sem)
        mine.start()
        first = [copy(0, me, sibling, src=x_ref)]
        first += [copy(1 + j, me, (*chip, c), src=x_ref) for j, chip in enumerate(chips)]
        for cp in first:
            cp.start()
        passed = [copy(4 + j, (*chip, c), sibling) for j, chip in enumerate(chips)]
        for j, chip in enumerate(chips):
            copy(1 + j, (*chip, c), me).wait_recv()
            passed[j].start()
        copy(0, sibling, me).wait_recv()
        for j, chip in enumerate(chips):
            copy(4 + j, (*chip, 1 - c), me).wait_recv()
        for cp in first + passed:
            cp.wait_send()
        mine.wait()

    return pl.pallas_call(
        body, name="allgather_all",
        out_shape=jax.ShapeDtypeStruct((N_DEV * m_per, n), v.dtype),
        in_specs=[pl.BlockSpec(memory_space=pltpu.VMEM)],
        out_specs=pl.BlockSpec(memory_space=pltpu.VMEM),
        scratch_shapes=[pltpu.SemaphoreType.DMA((7,)), pltpu.SemaphoreType.DMA((7,)), pltpu.SemaphoreType.DMA],
        compiler_params=pltpu.CompilerParams(vmem_limit_bytes=VMEM_LIMIT_MB * 1024 * 1024),
    )(v)


def _adamw_math(w, g, m, v):
    m2 = ADAM_B1 * m + (1.0 - ADAM_B1) * g
    v2 = ADAM_B2 * v + (1.0 - ADAM_B2) * (g * g)
    m_hat = m2 / (1.0 - ADAM_B1 ** ADAM_STEP)
    v_hat = v2 / (1.0 - ADAM_B2 ** ADAM_STEP)
    delta = -ADAM_LR * (m_hat / (jnp.sqrt(v_hat) + ADAM_EPS) + ADAM_WD * w)
    return delta, m2, v2


def _adamw(w, m, v, gs, nblk):
    nl, r, n = w.shape
    assert nl == len(gs) and nl in (1, 2)
    br = r // nblk

    def body(w_ref, m_ref, v_ref, *rest):
        g_refs, (go_ref, d_ref, mo_ref, vo_ref) = rest[:nl], rest[nl:]
        g = g_refs[0][...]
        if nl == 2:
            g = jnp.where(pl.program_id(0) == 0, g, g_refs[1][...])
        delta, m2, v2 = _adamw_math(w_ref[0], g, m_ref[0], v_ref[0])
        go_ref[0] = g
        d_ref[0] = delta
        mo_ref[0] = m2
        vo_ref[0] = v2

    spec = pl.BlockSpec((1, br, n), lambda l, i: (l, i, 0))
    g_specs = [pl.BlockSpec((br, n), lambda l, i: (i, 0))] if nl == 1 else [
        pl.BlockSpec((br, n), lambda l, i: (jnp.where(l == 0, i, nblk - 1), 0)),
        pl.BlockSpec((br, n), lambda l, i: (jnp.where(l == 1, i, 0), 0))]
    return pl.pallas_call(
        body, name="adamw", grid=(nl, nblk),
        in_specs=[spec, spec, spec] + g_specs,
        out_specs=[spec] * 4,
        out_shape=[jax.ShapeDtypeStruct((nl, r, n), F32)] * 4,
        compiler_params=_cparams(2),
    )(w, m, v, *gs)


def _small_reduce_adamw(parts, w, m, v, rep_rows, sh_rows):
    mrows = rep_rows + N_SHARD * sh_rows + LOSS_ROWS

    def body(p_ref, w_ref, m_ref, v_ref, go_ref, d_ref, mo_ref, vo_ref, loss_ref):
        x, y, _ = _mesh_pos()
        mine = rep_rows + (2 * x + y) * sh_rows
        g_rep = p_ref[0:rep_rows, :]
        g_sh = p_ref[pl.ds(pl.multiple_of(mine, SUBLANE), sh_rows), :]
        loss = p_ref[mrows - LOSS_ROWS:mrows, :]
        for k in range(1, N_DEV):
            g_rep = g_rep + p_ref[k * mrows:k * mrows + rep_rows, :]
            g_sh = g_sh + p_ref[pl.ds(pl.multiple_of(k * mrows + mine, SUBLANE), sh_rows), :]
            loss = loss + p_ref[(k + 1) * mrows - LOSS_ROWS:(k + 1) * mrows, :]
        g = jnp.concatenate([g_rep, g_sh], axis=0)
        delta, m2, v2 = _adamw_math(w_ref[...], g, m_ref[...], v_ref[...])
        go_ref[...] = g
        d_ref[...] = delta
        mo_ref[...] = m2
        vo_ref[...] = v2
        loss_ref[...] = loss

    return pl.pallas_call(
        body, name="small_reduce_adamw",
        out_shape=[jax.ShapeDtypeStruct((rep_rows + sh_rows, 128), F32)] * 4
        + [jax.ShapeDtypeStruct((LOSS_ROWS, 128), F32)],
        compiler_params=pltpu.CompilerParams(vmem_limit_bytes=VMEM_LIMIT_MB * 1024 * 1024),
    )(parts, w, m, v)


LANE = 128
REP_SPEC = (("ffn1_norm", 16), ("mix_norm", 16), ("ffn2_norm", 16), ("final_norm", 8), ("pool_w", 128),
            ("pool_scale", 8), ("hgrn_lb_logits", 16), ("hgrn_gnorm", 8), ("lru_wa", 256), ("lru_wx", 256))
SH_SPEC = (("meta_tokens", 32), ("conv_w", 32), ("lru_conv_w", 8), ("conv_b", 8), ("conv_ln_g", 8),
           ("conv_ln_b", 8), ("lru_conv_b", 8), ("lru_ba", 8), ("lru_bx", 8), ("lru_lambda", 8))
REP_ROWS = sum(r for _, r in REP_SPEC)
SH_ROWS = sum(r for _, r in SH_SPEC)


def _pack_rows(vals, spec):
    parts = []
    for name, rows in spec:
        flat = vals[name].astype(F32).reshape(-1, LANE)
        if flat.shape[0] < rows:
            flat = jnp.concatenate([flat, jnp.zeros((rows - flat.shape[0], LANE), F32)], axis=0)
        parts.append(flat)
    return jnp.concatenate(parts, axis=0)


def _unpack_rows(packed, spec, shapes):
    out = {}
    off = 0
    for name, rows in spec:
        shp = shapes[name]
        n = int(np.prod(shp)) // LANE
        out[name] = packed[off:off + n].reshape(shp)
        off += rows
    return out


def _block_diag(blocks):
    n, b, _ = blocks.shape
    return sum(jnp.pad(blocks[g], ((g * b, (n - 1 - g) * b), (g * b, (n - 1 - g) * b))) for g in range(n))


def _diag_blocks(mat, n):
    b = mat.shape[0] // n
    return jnp.stack([mat[g * b:(g + 1) * b, g * b:(g + 1) * b] for g in range(n)])


BIG = ("ffn1_wg", "ffn1_wu", "ffn2_wg", "ffn2_wu", "ffn1_wd", "ffn2_wd", "w_in_even", "w_out_even",
       "w_in_odd", "w_out_odd")
ADAM_BLOCKS = {"ffn1_wg": 8, "ffn1_wu": 8, "ffn2_wg": 8, "ffn2_wu": 8, "ffn1_wd": 4, "ffn2_wd": 4,
               "w_in_even": 4, "w_out_even": 2, "w_in_odd": 4, "w_out_odd": 2}
WEIGHT_NAMES = ('meta_tokens', 'ffn1_norm', 'ffn1_wg', 'ffn1_wu', 'ffn1_wd', 'mix_norm', 'ffn2_norm', 'ffn2_wg',
                'ffn2_wu', 'ffn2_wd', 'w_in_even', 'pool_w', 'pool_scale', 'hgrn_lb_logits', 'hgrn_gnorm',
                'w_out_even', 'w_in_odd', 'conv_w', 'conv_b', 'conv_ln_g', 'conv_ln_b', 'lru_conv_w',
                'lru_conv_b', 'lru_wa', 'lru_ba', 'lru_wx', 'lru_bx', 'lru_lambda', 'w_out_odd', 'final_norm')


def _rows2d(a):
    return a.reshape(-1, a.shape[-1])


def _local_step_v2(x, tgt, w, gathered, small_full):
    s_len, d = x.shape
    t_real = s_len + N_META
    tp = -(-t_real // ROW_ALIGN) * ROW_ALIGN
    tm = _tile(tp, 832, ROW_ALIGN)
    tm_small = _tile(tp, 416, 16)
    tr = _tile(tp, 416, SUBLANE)
    f1 = ("ffn1_norm", "ffn1_wg", "ffn1_wu", "ffn1_wd")
    f2 = ("ffn2_norm", "ffn2_wg", "ffn2_wu", "ffn2_wd")

    meta_full = small_full["meta_tokens"]
    h0 = jnp.concatenate([meta_full, x, jnp.zeros((tp - t_real, d), F32)], axis=0)
    tgt_pad = jnp.concatenate([jnp.zeros((N_META, d), F32), tgt, jnp.zeros((tp - t_real, d), F32)], axis=0)

    w_in_even = jnp.transpose(gathered["w_in_even"], (1, 0, 2)).reshape(d, D_IN_EVEN)
    w_out_even = gathered["w_out_even"].reshape(d, d)
    w_out_odd = gathered["w_out_odd"].reshape(d, d)
    even_piece = [(w_in_even, (d, D_IN_EVEN), (0, 0))]
    odd_pieces = [(gathered["w_in_odd"], (1, d, D_IN_ODD // N_SHARD), (k, 0, 0)) for k in range(N_SHARD)]
    pool_wbd = _block_diag(w["pool_w"][0]).astype(BF16)
    pool_scale = w["pool_scale"]
    wa_bd = _block_diag2(w["lru_wa"][0]).astype(BF16)
    wx_bd = _block_diag2(w["lru_wx"][0]).astype(BF16)
    mst, msk, n_lev = _hgrn_consts(HG_CHUNK)
    conv_w = small_full["conv_w"]
    sf = small_full

    def gain(name, layer):
        return w[name][layer:layer + 1]

    def ffn(h, names, layer):
        return _ffn_fwd(h, gain(names[0], layer), gathered[names[1]], gathered[names[2]], gathered[names[3]],
                        layer, tm)

    h1, a1, b1, n1 = ffn(h0, f1, 0)
    p0, nm0 = _proj_fwd(h1, gain("mix_norm", 0), 0, even_piece, tm_small)
    ya = _pool_fwd(p0, pool_wbd, pool_scale, tr)
    yb, states = _hgrn_fwd(p0, w["hgrn_lb_logits"], w["hgrn_gnorm"], mst, msk, n_lev, tm)
    h2 = _out_fwd(h1, ya, yb, w_out_even, tm)
    h3, a2, b2, n2 = ffn(h2, f2, 0)
    h4, a3, b3, n3 = ffn(h3, f1, 1)
    p1, nm1 = _proj_fwd(h4, gain("mix_norm", 1), 1, odd_pieces, tm_small)
    yc = _convmod_fwd(p1, conv_w, sf["conv_b"], sf["conv_ln_g"], sf["conv_ln_b"], tr)
    lru_args = (sf["lru_conv_w"], sf["lru_conv_b"], wa_bd, sf["lru_ba"], wx_bd, sf["lru_bx"], sf["lru_lambda"])
    yd, hs = _lru_fwd(p1, *lru_args)
    h5 = _out_fwd(h4, yc, yd, w_out_odd, tm)
    h6, a4, b4, n4 = ffn(h5, f2, 1)
    loss, dh6, dg_final = _loss_bwd(h6, w["final_norm"].reshape(1, d), tgt_pad, t_real, tm)

    def ffn_bwd(dho, h, n, a, b, names, layer, acc):
        dh, da, db, dg = _ffn_bwd_act(dho, h, gain(names[0], layer), a, b, gathered[names[1]], gathered[names[2]],
                                      gathered[names[3]], layer, tm_small)
        acc = _ffn_bwd_w(dho, n, a, b, da, db, acc[0], acc[1], acc[2], layer, tm)
        return dh, dg, acc

    none3 = (None, None, None)
    dh5, dg_f2_l1, g_f2 = ffn_bwd(dh6, h5, n4, a4, b4, f2, 1, none3)
    dyc, dyd, dw_out_odd = _out_bwd(dh5, yc, yd, w_out_odd, tm)
    dca, dcb, dconv_w, dconv_vec = _convmod_bwd(p1, dyc, conv_w, sf["conv_b"], sf["conv_ln_g"], sf["conv_ln_b"], tr)
    dlx, dlg, dwa_bd, dwx_bd, dlru_vec = _lru_bwd(p1, hs, dyd, *lru_args)
    dp1 = [dca, dcb, dlx, dlg]
    dh4, dg_mix_l1 = _proj_bwd_act(dh5, h4, gain("mix_norm", 1), 1, dp1, odd_pieces, tm_small)
    dw_in_odd = jnp.stack(_proj_bwd_w(nm1, dp1, tm))
    dh3, dg_f1_l1, g_f1 = ffn_bwd(dh4, h3, n3, a3, b3, f1, 1, none3)
    dh2, dg_f2_l0, g_f2 = ffn_bwd(dh3, h2, n2, a2, b2, f2, 0, g_f2)
    dya, dyb, dw_out_even = _out_bwd(dh2, ya, yb, w_out_even, tm)
    dpool, dpool_wbd, dpool_scale = _pool_bwd(p0, dya, pool_wbd, pool_scale, tr)
    dq, dz, dv, dgate, dlb_logits, dgn_heads = _hgrn_bwd(p0, dyb, states, w["hgrn_lb_logits"], w["hgrn_gnorm"],
                                                         mst, msk, n_lev, tm)
    dp0 = [jnp.concatenate([dpool, dq, dz, dv, dgate], axis=1)]
    dh1, dg_mix_l0 = _proj_bwd_act(dh2, h1, gain("mix_norm", 0), 0, dp0, even_piece, tm_small)
    (dw_in_even,) = _proj_bwd_w(nm0, dp0, tm_small)
    dh0, dg_f1_l0, g_f1 = ffn_bwd(dh1, h0, n1, a1, b1, f1, 0, g_f1)

    grad_x = dh0[N_META:t_real]
    big = {
        "ffn1_wg": g_f1[0], "ffn1_wu": g_f1[1], "ffn1_wd": g_f1[2],
        "ffn2_wg": g_f2[0], "ffn2_wu": g_f2[1], "ffn2_wd": g_f2[2],
        "w_in_even": jnp.transpose(dw_in_even.reshape(d, N_SHARD, D_IN_EVEN // N_SHARD), (1, 0, 2)),
        "w_out_even": dw_out_even.reshape(N_SHARD, d // N_SHARD, d),
        "w_in_odd": dw_in_odd,
        "w_out_odd": dw_out_odd.reshape(N_SHARD, d // N_SHARD, d),
    }
    rep = {
        "ffn1_norm": jnp.concatenate([dg_f1_l0, dg_f1_l1], axis=0),
        "mix_norm": jnp.concatenate([dg_mix_l0, dg_mix_l1], axis=0),
        "ffn2_norm": jnp.concatenate([dg_f2_l0, dg_f2_l1], axis=0),
        "final_norm": dg_final,
        "pool_w": _diag_blocks(dpool_wbd, len(POOL_WINDOWS)),
        "pool_scale": dpool_scale,
        "hgrn_lb_logits": dlb_logits,
        "hgrn_gnorm": jnp.sum(dgn_heads, axis=0),
        "lru_wa": _diag_blocks2(dwa_bd),
        "lru_wx": _diag_blocks2(dwx_bd),
    }
    dmeta = jnp.transpose(dh0[:N_META].reshape(N_META, N_SHARD, 2, LANE), (1, 0, 2, 3)).reshape(N_SHARD, 32, LANE)
    packs = [_pack_rows(rep, REP_SPEC)]
    for s in range(N_SHARD):
        sh = {
            "meta_tokens": dmeta[s], "conv_w": dconv_w[s], "lru_conv_w": dlru_vec[s, 0:4],
            "conv_b": dconv_vec[s, 0:1], "conv_ln_g": dconv_vec[s, 1:2], "conv_ln_b": dconv_vec[s, 2:3],
            "lru_conv_b": dlru_vec[s, 4:5], "lru_ba": dlru_vec[s, 5:6], "lru_bx": dlru_vec[s, 6:7],
            "lru_lambda": dlru_vec[s, 7:8],
        }
        packs.append(_pack_rows(sh, SH_SPEC))
    return loss, grad_x, big, jnp.concatenate(packs, axis=0)


def _block_diag2(heads):
    nb = heads.shape[0] // 2
    return jnp.stack([_block_diag(heads[2 * j:2 * j + 2]) for j in range(nb)])


def _diag_blocks2(mats):
    return jnp.concatenate([_diag_blocks(mats[j], 2) for j in range(mats.shape[0])], axis=0)


def _kernel_v2(x, meta_tokens, ffn1_norm, ffn1_wg, ffn1_wu, ffn1_wd, mix_norm, ffn2_norm, ffn2_wg, ffn2_wu, ffn2_wd, w_in_even, pool_w, pool_scale, hgrn_lb_logits, hgrn_gnorm, w_out_even, w_in_odd, conv_w, conv_b, conv_ln_g, conv_ln_b, lru_conv_w, lru_conv_b, lru_wa, lru_ba, lru_wx, lru_bx, lru_lambda, w_out_odd, final_norm, loss_target, m_meta_tokens, m_ffn1_norm, m_ffn1_wg, m_ffn1_wu, m_ffn1_wd, m_mix_norm, m_ffn2_norm, m_ffn2_wg, m_ffn2_wu, m_ffn2_wd, m_w_in_even, m_pool_w, m_pool_scale, m_hgrn_lb_logits, m_hgrn_gnorm, m_w_out_even, m_w_in_odd, m_conv_w, m_conv_b, m_conv_ln_g, m_conv_ln_b, m_lru_conv_w, m_lru_conv_b, m_lru_wa, m_lru_ba, m_lru_wx, m_lru_bx, m_lru_lambda, m_w_out_odd, m_final_norm, v_meta_tokens, v_ffn1_norm, v_ffn1_wg, v_ffn1_wu, v_ffn1_wd, v_mix_norm, v_ffn2_norm, v_ffn2_wg, v_ffn2_wu, v_ffn2_wd, v_w_in_even, v_pool_w, v_pool_scale, v_hgrn_lb_logits, v_hgrn_gnorm, v_w_out_even, v_w_in_odd, v_conv_w, v_conv_b, v_conv_ln_g, v_conv_ln_b, v_lru_conv_w, v_lru_conv_b, v_lru_wa, v_lru_ba, v_lru_wx, v_lru_bx, v_lru_lambda, v_w_out_odd, v_final_norm):
    args = locals()
    w = {n: args[n] for n in WEIGHT_NAMES}
    m = {n: args["m_" + n] for n in WEIGHT_NAMES}
    v = {n: args["v_" + n] for n in WEIGHT_NAMES}
    shapes = {n: w[n].shape for n in WEIGHT_NAMES}

    big_in = [_rows2d(w[n]).astype(BF16) for n in BIG]
    small_sh = _pack_rows(w, SH_SPEC)
    gath = _allgather_shards(big_in + [small_sh], [True] * len(BIG) + [False])
    gathered = dict(zip(BIG, gath[:len(BIG)]))
    sm = gath[len(BIG)]
    sh_shapes = {n: (N_SHARD,) + tuple(shapes[n]) for n, _ in SH_SPEC}
    per_shard = [_unpack_rows(sm[s], SH_SPEC, shapes) for s in range(N_SHARD)]
    small_full = {}
    for n, _ in SH_SPEC:
        stacked = [per_shard[s][n] for s in range(N_SHARD)]
        small_full[n] = jnp.concatenate([p.reshape(-1, p.shape[-1]) for p in stacked], axis=-1)
    small_full["conv_w"] = jnp.concatenate(
        [small_full["conv_w"], jnp.zeros((CONV_HALO - CONV_WIDTH, D_CONV), F32)], axis=0)

    loss, grad_x, big, small_part = _local_step(x[0], loss_target[0], w, gathered, small_full)
    loss = lax.psum(loss[0, 0], ("x", "y", "c"))

    core = lax.axis_index("c").astype(jnp.int32).reshape(1)
    parts = [big[n] for n in BIG]
    recvd = _pair_exchange_halves(parts)
    pair = _pair_add(parts, recvd, core)
    chip = (2 * lax.axis_index("x") + lax.axis_index("y")).astype(jnp.int32).reshape(1)
    slots = _scatter_to_owners(pair, _own_slot(pair, chip))
    halves = _sum_chips(slots, core)
    full = _pair_allgather_halves(halves)
    out_g, out_d, out_m, out_v = {}, {}, {}, {}
    for n, g in zip(BIG, full):
        res = _adamw(_rows2d(w[n]), _rows2d(m[n]), _rows2d(v[n]), g, 0, ADAM_BLOCKS[n])
        out_g[n], out_d[n], out_m[n], out_v[n] = [r.reshape(shapes[n]) for r in res]

    gathered_small = _allgather_all(small_part)

    def pack_small(src):
        return jnp.concatenate([_pack_rows(src, REP_SPEC), _pack_rows(src, SH_SPEC)], axis=0)

    res = _small_reduce_adamw(gathered_small, pack_small(w), pack_small(m), pack_small(v), REP_ROWS, SH_ROWS)
    for dst, packed in zip((out_g, out_d, out_m, out_v), res):
        dst.update(_unpack_rows(packed[:REP_ROWS], REP_SPEC, shapes))
        dst.update(_unpack_rows(packed[REP_ROWS:], SH_SPEC, shapes))

    return (loss, grad_x[None], *[out_g[n] for n in WEIGHT_NAMES], *[out_d[n] for n in WEIGHT_NAMES],
            *[out_m[n] for n in WEIGHT_NAMES], *[out_v[n] for n in WEIGHT_NAMES])


GATHER_GROUPS = (
    (("small", 0),),
    (("ffn1_wg", 0), ("ffn1_wu", 0), ("ffn1_wd", 0)),
    (("w_in_even", 0), ("w_out_even", 0)),
    (("ffn2_wg", 0), ("ffn2_wu", 0), ("ffn2_wd", 0)),
    (("ffn1_wg", 1), ("ffn1_wu", 1), ("ffn1_wd", 1)),
    (("w_in_odd", 0), ("w_out_odd", 0)),
    (("ffn2_wg", 1), ("ffn2_wu", 1), ("ffn2_wd", 1)),
)
ADAM_ROW_BLOCKS = {"ffn1_wg": 2, "ffn1_wu": 2, "ffn2_wg": 2, "ffn2_wu": 2, "ffn1_wd": 2, "ffn2_wd": 2,
                   "w_in_even": 4, "w_out_even": 2, "w_in_odd": 4, "w_out_odd": 2}
TRANSPOSED = ("ffn1_wg", "ffn1_wu", "ffn2_wg", "ffn2_wu", "w_in_even")
SCATTER_DEPTH = 2
LOSS_ROWS = 8


def _unpack_small(sm, shapes):
    per_shard = [_unpack_rows(sm[s], SH_SPEC, shapes) for s in range(N_SHARD)]
    full = {}
    for n, _ in SH_SPEC:
        full[n] = jnp.concatenate([per_shard[s][n].reshape(-1, shapes[n][-1]) for s in range(N_SHARD)], axis=-1)
    full["conv_w"] = jnp.concatenate([full["conv_w"], jnp.zeros((CONV_HALO - CONV_WIDTH, D_CONV), F32)], axis=0)
    return full


def _local_step(x, tgt, w, shapes, fetch, emit, emit_small):
    s_len, d = x.shape
    t_real = s_len + N_META
    tp = -(-t_real // ROW_ALIGN) * ROW_ALIGN
    tm = _tile(tp, 832, ROW_ALIGN)
    tm_small = _tile(tp, 832, 16)
    tr = _tile(tp, 416, SUBLANE)

    def gain(name, layer):
        return w[name][layer:layer + 1]

    pool_wbd = _block_diag(w["pool_w"][0]).astype(BF16)
    pool_scale = w["pool_scale"]
    wa_bd = _block_diag2(w["lru_wa"][0]).astype(BF16)
    wx_bd = _block_diag2(w["lru_wx"][0]).astype(BF16)
    mst, msk, n_lev = _hgrn_consts(HG_CHUNK)

    (sm,) = fetch(0, None)
    sf = _unpack_small(sm, shapes)
    h0 = jnp.concatenate([sf["meta_tokens"], x, jnp.zeros((tp - t_real, d), F32)], axis=0)
    tgt_pad = jnp.concatenate([jnp.zeros((N_META, d), F32), tgt, jnp.zeros((tp - t_real, d), F32)], axis=0)
    f1l0 = fetch(1, h0)
    h1, *s1 = _ffn_fwd(h0, gain("ffn1_norm", 0), *f1l0, 0, tm)
    w_in_even4, w_out_even4 = fetch(2, h1)
    w_out_even = w_out_even4.reshape(d, d)
    even_piece = [(w_in_even4.reshape(D_IN_EVEN, d), (D_IN_EVEN, d), (0, 0))]
    p0, nm0 = _proj_fwd(h1, gain("mix_norm", 0), 0, even_piece, tm_small, wt=True)
    ya = _pool_fwd(p0, pool_wbd, pool_scale, tr)
    yb, states = _hgrn_fwd(p0, w["hgrn_lb_logits"], w["hgrn_gnorm"], mst, msk, n_lev, tm)
    h2 = _out_fwd(h1, ya, yb, w_out_even, tm)
    f2l0 = fetch(3, h2)
    h3, *s2 = _ffn_fwd(h2, gain("ffn2_norm", 0), *f2l0, 0, tm)
    f1l1 = fetch(4, h3)
    h4, *s3 = _ffn_fwd(h3, gain("ffn1_norm", 1), *f1l1, 0, tm)
    w_in_odd4, w_out_odd4 = fetch(5, h4)
    w_out_odd = w_out_odd4.reshape(d, d)
    odd_pieces = [(w_in_odd4, (1, d, D_IN_ODD // N_SHARD), (k, 0, 0)) for k in range(N_SHARD)]
    p1, nm1 = _proj_fwd(h4, gain("mix_norm", 1), 1, odd_pieces, tm_small)
    yc = _convmod_fwd(p1, sf["conv_w"], sf["conv_b"], sf["conv_ln_g"], sf["conv_ln_b"], tr)
    lru_args = (sf["lru_conv_w"], sf["lru_conv_b"], wa_bd, sf["lru_ba"], wx_bd, sf["lru_bx"], sf["lru_lambda"])
    yd, hs = _lru_fwd(p1, *lru_args)
    h5 = _out_fwd(h4, yc, yd, w_out_odd, tm)
    f2l1 = fetch(6, h5)
    h6, *s4 = _ffn_fwd(h5, gain("ffn2_norm", 1), *f2l1, 0, tm)
    loss, dh6, dg_final = _loss_bwd(h6, w["final_norm"].reshape(1, d), tgt_pad, t_real, tm)

    def ffn_bwd(dho, h, saved, norm, wts, after=()):
        ga, gb, sa, n = saved
        dh, da, db, dg, dy = _ffn_bwd_act(dho, h, norm, ga, gb, *wts, 0, tm, after)
        return dh, dg, _ffn_bwd_w(dy, n, sa, da, db, tm)

    dh5, dg_f2_l1, g = ffn_bwd(dh6, h5, s4, gain("ffn2_norm", 1), f2l1)
    sent = emit((("ffn2_wg", 1), ("ffn2_wu", 1), ("ffn2_wd", 1)), g)
    dyc, dyd, dw_out_odd = _out_bwd(dh5, yc, yd, w_out_odd, tm, tuple(sent))
    dca, dcb, dconv_w, dconv_vec = _convmod_bwd(p1, dyc, sf["conv_w"], sf["conv_b"], sf["conv_ln_g"],
                                                sf["conv_ln_b"], tr)
    dlx, dlg, dwa_bd, dwx_bd, dlru_vec = _lru_bwd(p1, hs, dyd, *lru_args)
    dp1 = [dca, dcb, dlx, dlg]
    dh4, dg_mix_l1 = _proj_bwd_act(dh5, h4, gain("mix_norm", 1), 1, dp1, odd_pieces, tm_small)
    dw_in_odd = jnp.stack(_proj_bwd_w(nm1, dp1, tm))
    dh3, dg_f1_l1, g = ffn_bwd(dh4, h3, s3, gain("ffn1_norm", 1), f1l1)
    sent = emit((("w_out_odd", 0), ("w_in_odd", 0), ("ffn1_wg", 1), ("ffn1_wu", 1), ("ffn1_wd", 1)),
                [dw_out_odd.reshape(N_SHARD, d // N_SHARD, d), dw_in_odd] + list(g))
    dh2, dg_f2_l0, g_f2l0 = ffn_bwd(dh3, h2, s2, gain("ffn2_norm", 0), f2l0, tuple(sent))
    dya, dyb, dw_out_even = _out_bwd(dh2, ya, yb, w_out_even, tm)
    dpool, dpool_wbd, dpool_scale = _pool_bwd(p0, dya, pool_wbd, pool_scale, tr)
    dq, dz, dv, dgate, dlb_logits, dgn_heads = _hgrn_bwd(p0, dyb, states, w["hgrn_lb_logits"], w["hgrn_gnorm"],
                                                         mst, msk, n_lev, tm)
    dp0 = [dpool, dq, dz, dv, dgate]
    dh1, dg_mix_l0 = _proj_bwd_act(dh2, h1, gain("mix_norm", 0), 0, dp0, even_piece, tm_small, wt=True)
    dw_in_even_t = jnp.concatenate(_proj_bwd_w(nm0, dp0, tm_small, wt=True), axis=0)
    sent = emit((("ffn2_wg", 0), ("ffn2_wu", 0), ("ffn2_wd", 0), ("w_out_even", 0), ("w_in_even", 0)),
                list(g_f2l0) + [dw_out_even.reshape(N_SHARD, d // N_SHARD, d),
                                dw_in_even_t.reshape(N_SHARD, D_IN_EVEN // N_SHARD, d)])
    ga, gb, sa, n1 = s1
    dh0, da, db, dg_f1_l0, dy = _ffn_bwd_act(dh1, h0, gain("ffn1_norm", 0), ga, gb, *f1l0, 0, tm, tuple(sent))

    grad_x = dh0[N_META:t_real]
    rep = {
        "ffn1_norm": jnp.concatenate([dg_f1_l0, dg_f1_l1], axis=0),
        "mix_norm": jnp.concatenate([dg_mix_l0, dg_mix_l1], axis=0),
        "ffn2_norm": jnp.concatenate([dg_f2_l0, dg_f2_l1], axis=0),
        "final_norm": dg_final,
        "pool_w": _diag_blocks(dpool_wbd, len(POOL_WINDOWS)),
        "pool_scale": dpool_scale,
        "hgrn_lb_logits": dlb_logits,
        "hgrn_gnorm": jnp.sum(dgn_heads, axis=0),
        "lru_wa": _diag_blocks2(dwa_bd),
        "lru_wx": _diag_blocks2(dwx_bd),
    }
    dmeta = jnp.transpose(dh0[:N_META].reshape(N_META, N_SHARD, 2, LANE), (1, 0, 2, 3)).reshape(N_SHARD, 32, LANE)
    packs = [_pack_rows(rep, REP_SPEC)]
    for s in range(N_SHARD):
        sh = {
            "meta_tokens": dmeta[s], "conv_w": dconv_w[s], "lru_conv_w": dlru_vec[s, 0:4],
            "conv_b": dconv_vec[s, 0:1], "conv_ln_g": dconv_vec[s, 1:2], "conv_ln_b": dconv_vec[s, 2:3],
            "lru_conv_b": dlru_vec[s, 4:5], "lru_ba": dlru_vec[s, 5:6], "lru_bx": dlru_vec[s, 6:7],
            "lru_lambda": dlru_vec[s, 7:8],
        }
        packs.append(_pack_rows(sh, SH_SPEC))
    packs.append(jnp.pad(loss, ((0, LOSS_ROWS - 1), (0, LANE - 1))))
    sent = emit_small(jnp.concatenate(packs, axis=0))
    emit((("ffn1_wg", 0), ("ffn1_wu", 0), ("ffn1_wd", 0)), _ffn_bwd_w(dy, n1, sa, da, db, tm, tuple(sent)))
    return grad_x


def kernel(x, meta_tokens, ffn1_norm, ffn1_wg, ffn1_wu, ffn1_wd, mix_norm, ffn2_norm, ffn2_wg, ffn2_wu, ffn2_wd, w_in_even, pool_w, pool_scale, hgrn_lb_logits, hgrn_gnorm, w_out_even, w_in_odd, conv_w, conv_b, conv_ln_g, conv_ln_b, lru_conv_w, lru_conv_b, lru_wa, lru_ba, lru_wx, lru_bx, lru_lambda, w_out_odd, final_norm, loss_target, m_meta_tokens, m_ffn1_norm, m_ffn1_wg, m_ffn1_wu, m_ffn1_wd, m_mix_norm, m_ffn2_norm, m_ffn2_wg, m_ffn2_wu, m_ffn2_wd, m_w_in_even, m_pool_w, m_pool_scale, m_hgrn_lb_logits, m_hgrn_gnorm, m_w_out_even, m_w_in_odd, m_conv_w, m_conv_b, m_conv_ln_g, m_conv_ln_b, m_lru_conv_w, m_lru_conv_b, m_lru_wa, m_lru_ba, m_lru_wx, m_lru_bx, m_lru_lambda, m_w_out_odd, m_final_norm, v_meta_tokens, v_ffn1_norm, v_ffn1_wg, v_ffn1_wu, v_ffn1_wd, v_mix_norm, v_ffn2_norm, v_ffn2_wg, v_ffn2_wu, v_ffn2_wd, v_w_in_even, v_pool_w, v_pool_scale, v_hgrn_lb_logits, v_hgrn_gnorm, v_w_out_even, v_w_in_odd, v_conv_w, v_conv_b, v_conv_ln_g, v_conv_ln_b, v_lru_conv_w, v_lru_conv_b, v_lru_wa, v_lru_ba, v_lru_wx, v_lru_bx, v_lru_lambda, v_w_out_odd, v_final_norm):
    args = locals()
    w = {n: args[n] for n in WEIGHT_NAMES}
    m = {n: args["m_" + n] for n in WEIGHT_NAMES}
    v = {n: args["v_" + n] for n in WEIGHT_NAMES}
    shapes = {n: w[n].shape for n in WEIGHT_NAMES}
    core = lax.axis_index("c").astype(jnp.int32).reshape(1)
    chip = (2 * lax.axis_index("x") + lax.axis_index("y")).astype(jnp.int32).reshape(1)
    me = 2 * chip + core

    def view(a, n):
        return jnp.swapaxes(a, 1, 2) if n in TRANSPOSED else a

    wv, mv, vv = [{n: view(src[n], n) for n in BIG} for src in (w, m, v)]

    def shard(key):
        n, l = key
        return _pack_rows(w, SH_SPEC) if n == "small" else wv[n][l].astype(BF16)

    started = {}
    for groups in (GATHER_GROUPS[:2], GATHER_GROUPS[2:]):
        gkeys = [key for grp in groups for key in grp]
        ssem, rsem, srcs, lands, token = _gather_start([shard(key) for key in gkeys],
                                                       [key in GATHER_GROUPS[1] for key in gkeys])
        for k, key in enumerate(gkeys):
            started[key] = (ssem, rsem, srcs[k], lands[k], k, token)

    def pack_small(src):
        return jnp.concatenate([_pack_rows(src, REP_SPEC), _pack_rows(src, SH_SPEC)], axis=0)

    small_packs = [pack_small(src) for src in (w, m, v)]

    def fetch(group, after):
        st = [started[key] for key in GATHER_GROUPS[group]]
        deps = (st[0][5],) if after is None else (after,)
        if group == 1:
            deps += (started[GATHER_GROUPS[2][0]][5],) + tuple(small_packs)
        split = group == 1
        got = _gather_wait(st[0][0], st[0][1], [s[2] for s in st], [s[3] for s in st], [s[4] for s in st], deps,
                           split)
        return _pair_forward(got) if split else got

    in_flight, reduced = [], {}

    def collect(entry, after):
        gkeys, gs_sem, gr_sem, grads_thru, slots_thru, _ = entry
        slots = _reduce_wait(gs_sem, gr_sem, grads_thru, slots_thru, after)
        full = _pair_allgather_halves(_sum_devices(slots, core))
        reduced.update(zip(gkeys, full))
        return full[0]

    def emit(gkeys, grads):
        grads = list(grads)
        in_flight.append((gkeys,) + tuple(_reduce_start(grads, _own_part(grads, chip, core, me))))
        token = in_flight[-1][-1]
        if len(in_flight) > SCATTER_DEPTH:
            return token, collect(in_flight[-1 - SCATTER_DEPTH], (token,))
        return (token,)

    small_flight = []

    def emit_small(part):
        small_flight.append(_small_start(part, _small_own(part, me)))
        return (small_flight[0][4],)

    grad_x = _local_step(x[0], loss_target[0], w, shapes, fetch, emit, emit_small)

    s_ssem, s_rsem, s_part, s_slots, _ = small_flight[0]
    small_all = _small_wait(s_ssem, s_rsem, s_part, s_slots, (in_flight[-1][-1],))
    small_res = _small_reduce_adamw(small_all.reshape(-1, LANE), *small_packs, REP_ROWS, SH_ROWS)
    out_g, out_d, out_m, out_v = {}, {}, {}, {}
    deps = (small_res[0],)
    for entry in in_flight[-SCATTER_DEPTH:]:
        collect(entry, deps)
        for n in BIG:
            layers = range(shapes[n][0])
            if n not in out_g and all((n, l) in reduced for l in layers):
                res = _adamw(wv[n], mv[n], vv[n], [reduced[(n, l)] for l in layers], ADAM_ROW_BLOCKS[n])
                out_g[n], out_d[n], out_m[n], out_v[n] = [view(r, n) for r in res]
                deps += (res[1],)

    loss = small_res[4][0, 0]
    for dst, packed in zip((out_g, out_d, out_m, out_v), small_res[:4]):
        dst.update(_unpack_rows(packed[:REP_ROWS], REP_SPEC, shapes))
        dst.update(_unpack_rows(packed[REP_ROWS:], SH_SPEC, shapes))

    return (loss, grad_x[None], *[out_g[n] for n in WEIGHT_NAMES], *[out_d[n] for n in WEIGHT_NAMES],
            *[out_m[n] for n in WEIGHT_NAMES], *[out_v[n] for n in WEIGHT_NAMES])
```

```python
import functools

import numpy as np
import jax
import jax.numpy as jnp
from jax import lax
from jax.experimental import pallas as pl
from jax.experimental.pallas import tpu as pltpu

F32 = jnp.float32
BF16 = jnp.bfloat16
MESH = pl.DeviceIdType.MESH

EPS = 1e-6
N_META = 16
D_MODEL = 1024
D_FF = 2816
N_SHARD = 4
FF_SHARD = D_FF // N_SHARD
D_POOL = 256
POOL_GROUP = 64
POOL_WINDOWS = (2, 4, 8, 16)
D_HGRN = 768
HG_HEADS = 6
HEAD = 128
HG_CHUNK = 64
HG_HEADS_PER_STEP = 6
HG_PBLOCK = 256
D_IN_EVEN = D_POOL + 4 * D_HGRN
D_CONV = 512
CONV_WIDTH = 31
CONV_HALO = 32
D_LRU = 512
LRU_CONV = 4
LRU_HALO = 8
LRU_C = 8.0
D_IN_ODD = 2 * D_CONV + 2 * D_LRU
SUBLANE = 8
ROW_ALIGN = 64

ADAM_LR = 0.001
ADAM_B1 = 0.9
ADAM_B2 = 0.999
ADAM_EPS = 1e-08
ADAM_WD = 0.01
ADAM_STEP = 10

VMEM_LIMIT_MB = 56


def _cparams(n_grid_axes=0, vmem_mb=VMEM_LIMIT_MB):
    sem = ("arbitrary",) * n_grid_axes if n_grid_axes else None
    return pltpu.CompilerParams(dimension_semantics=sem, vmem_limit_bytes=vmem_mb * 1024 * 1024)


def _tile(n, target, mult):
    best = None
    for t in range(mult, min(n, target) + 1, mult):
        if n % t == 0:
            best = t
    assert best is not None, (n, target, mult)
    return best


def _dot(a, b):
    return jnp.dot(a, b, preferred_element_type=F32)


def _dot_nt(a, b):
    return lax.dot_general(a, b, (((1,), (1,)), ((), ())), preferred_element_type=F32)


def _dot_tn(a, b):
    return lax.dot_general(a, b, (((0,), (0,)), ((), ())), preferred_element_type=F32)


def _sigmoid(x):
    return 1.0 / (1.0 + jnp.exp(-x))


def _colsum(x):
    return jnp.sum(x, axis=0, keepdims=True)


def _rms_stats(h):
    rstd = lax.rsqrt(jnp.mean(h * h, axis=-1, keepdims=True) + EPS)
    return rstd, h * rstd


def _rms_bwd(dn, g, rstd, xhat):
    dng = dn * g
    dh = rstd * (dng - xhat * jnp.mean(dng * xhat, axis=-1, keepdims=True))
    return dh, _colsum(dn * xhat)


def _ffn_fwd(h, norm, wg4, wu4, wd4, layer, tm):
    tp, d = h.shape
    nt = tp // tm

    def body(h_ref, g_ref, wg_ref, wu_ref, wd_ref, ho_ref, ga_ref, gb_ref, sa_ref, n_ref, n_sc, acc):
        s = pl.program_id(1)

        @pl.when(s == 0)
        def _():
            hh = h_ref[...]
            rstd, xhat = _rms_stats(hh)
            n = (xhat * g_ref[...]).astype(BF16)
            n_sc[...] = n
            n_ref[...] = n
            acc[...] = jnp.zeros_like(acc)

        n = n_sc[...]
        a = _dot_nt(n, wg_ref[0])
        b = _dot_nt(n, wu_ref[0])
        sig = _sigmoid(a)
        sil = a * sig
        ga_ref[0] = (sig * (1.0 + a * (1.0 - sig)) * b).astype(BF16)
        gb_ref[0] = sil.astype(BF16)
        sg = (sil * b).astype(BF16)
        sa_ref[0] = sg
        acc[...] += _dot(sg, wd_ref[0])

        @pl.when(s == N_SHARD - 1)
        def _():
            ho_ref[...] = h_ref[...] + 0.5 * acc[...]

    return pl.pallas_call(
        body, name="ffn_fwd",
        grid=(nt, N_SHARD),
        in_specs=[
            pl.BlockSpec((tm, d), lambda i, s: (i, 0)),
            pl.BlockSpec((1, d), lambda i, s: (0, 0)),
            pl.BlockSpec((1, FF_SHARD, d), lambda i, s: (s, layer, 0)),
            pl.BlockSpec((1, FF_SHARD, d), lambda i, s: (s, layer, 0)),
            pl.BlockSpec((1, FF_SHARD, d), lambda i, s: (s, layer, 0)),
        ],
        out_specs=[
            pl.BlockSpec((tm, d), lambda i, s: (i, 0)),
            pl.BlockSpec((1, tm, FF_SHARD), lambda i, s: (s, i, 0)),
            pl.BlockSpec((1, tm, FF_SHARD), lambda i, s: (s, i, 0)),
            pl.BlockSpec((1, tm, FF_SHARD), lambda i, s: (s, i, 0)),
            pl.BlockSpec((tm, d), lambda i, s: (i, 0)),
        ],
        out_shape=[
            jax.ShapeDtypeStruct((tp, d), F32),
            jax.ShapeDtypeStruct((N_SHARD, tp, FF_SHARD), BF16),
            jax.ShapeDtypeStruct((N_SHARD, tp, FF_SHARD), BF16),
            jax.ShapeDtypeStruct((N_SHARD, tp, FF_SHARD), BF16),
            jax.ShapeDtypeStruct((tp, d), BF16),
        ],
        scratch_shapes=[pltpu.VMEM((tm, d), BF16), pltpu.VMEM((tm, d), F32)],
        compiler_params=_cparams(2),
    )(h, norm, wg4, wu4, wd4)


def _ffn_bwd_act(dho, h, norm, ga4, gb4, wg4, wu4, wd4, layer, tm, after=()):
    tp, d = h.shape
    nt = tp // tm

    def body(dho_ref, h_ref, g_ref, ga_ref, gb_ref, wg_ref, wu_ref, wd_ref, *rest):
        dh_ref, da_ref, db_ref, dg_ref, dy_ref, dn_sc = rest[len(after):]
        i = pl.program_id(0)
        s = pl.program_id(1)

        @pl.when(s == 0)
        def _():
            dy_ref[...] = (0.5 * dho_ref[...]).astype(BF16)
            dn_sc[...] = jnp.zeros_like(dn_sc)

        @pl.when((s == 0) & (i == 0))
        def _():
            dg_ref[...] = jnp.zeros_like(dg_ref)

        ds = _dot_nt(dy_ref[...], wd_ref[0])
        da = (ds * ga_ref[0].astype(F32)).astype(BF16)
        db = (ds * gb_ref[0].astype(F32)).astype(BF16)
        da_ref[0] = da
        db_ref[0] = db
        dn_sc[...] += _dot(da, wg_ref[0]) + _dot(db, wu_ref[0])

        @pl.when(s == N_SHARD - 1)
        def _():
            rstd, xhat = _rms_stats(h_ref[...])
            dh, dg = _rms_bwd(dn_sc[...], g_ref[...], rstd, xhat)
            dh_ref[...] = dho_ref[...] + dh
            dg_ref[...] += dg

    return pl.pallas_call(
        body, name="ffn_bwd_act",
        grid=(nt, N_SHARD),
        in_specs=[
            pl.BlockSpec((tm, d), lambda i, s: (i, 0)),
            pl.BlockSpec((tm, d), lambda i, s: (i, 0)),
            pl.BlockSpec((1, d), lambda i, s: (0, 0)),
            pl.BlockSpec((1, tm, FF_SHARD), lambda i, s: (s, i, 0)),
            pl.BlockSpec((1, tm, FF_SHARD), lambda i, s: (s, i, 0)),
            pl.BlockSpec((1, FF_SHARD, d), lambda i, s: (s, layer, 0)),
            pl.BlockSpec((1, FF_SHARD, d), lambda i, s: (s, layer, 0)),
            pl.BlockSpec((1, FF_SHARD, d), lambda i, s: (s, layer, 0)),
        ] + [pl.BlockSpec(memory_space=pl.ANY)] * len(after),
        out_specs=[
            pl.BlockSpec((tm, d), lambda i, s: (i, 0)),
            pl.BlockSpec((1, tm, FF_SHARD), lambda i, s: (s, i, 0)),
            pl.BlockSpec((1, tm, FF_SHARD), lambda i, s: (s, i, 0)),
            pl.BlockSpec((1, d), lambda i, s: (0, 0)),
            pl.BlockSpec((tm, d), lambda i, s: (i, 0)),
        ],
        out_shape=[
            jax.ShapeDtypeStruct((tp, d), F32),
            jax.ShapeDtypeStruct((N_SHARD, tp, FF_SHARD), BF16),
            jax.ShapeDtypeStruct((N_SHARD, tp, FF_SHARD), BF16),
            jax.ShapeDtypeStruct((1, d), F32),
            jax.ShapeDtypeStruct((tp, d), BF16),
        ],
        scratch_shapes=[pltpu.VMEM((tm, d), F32)],
        compiler_params=_cparams(2),
    )(dho, h, norm, ga4, gb4, wg4, wu4, wd4, *after)


def _ffn_bwd_w(pairs, tm, after=()):
    npair = len(pairs)
    tp, d = pairs[0][1].shape
    nt = tp // tm
    rhs_list = []
    for _, rhs, _ in pairs:
        if all(rhs is not r for r in rhs_list):
            rhs_list.append(rhs)
    rhs_of = [[rhs is r for r in rhs_list].index(True) for _, rhs, _ in pairs]
    nrhs = len(rhs_list)

    def body(*refs):
        rhs_refs = refs[:nrhs]
        lhs_refs = refs[nrhs:nrhs + npair]
        rest = refs[nrhs + npair + len(after):]
        out_refs, accs = rest[:npair], rest[npair:]
        i = pl.program_id(1)

        @pl.when(i == 0)
        def _():
            for acc in accs:
                acc[...] = jnp.zeros_like(acc)

        for k, (_, _, scale) in enumerate(pairs):
            rhs = rhs_refs[rhs_of[k]][...]
            if scale is not None:
                rhs = (scale * rhs).astype(BF16)
            accs[k][...] += _dot_tn(lhs_refs[k][0], rhs)

        @pl.when(i == nt - 1)
        def _():
            for k in range(npair):
                out_refs[k][0] = accs[k][...].astype(BF16)

    return pl.pallas_call(
        body, name="ffn_bwd_w",
        grid=(N_SHARD, nt),
        in_specs=[pl.BlockSpec((tm, d), lambda s, i: (i, 0))] * nrhs
        + [pl.BlockSpec((1, tm, FF_SHARD), lambda s, i: (s, i, 0))] * npair
        + [pl.BlockSpec(memory_space=pl.ANY)] * len(after),
        out_specs=[pl.BlockSpec((1, FF_SHARD, d), lambda s, i: (s, 0, 0))] * npair,
        out_shape=[jax.ShapeDtypeStruct((N_SHARD, FF_SHARD, d), BF16)] * npair,
        scratch_shapes=[pltpu.VMEM((FF_SHARD, d), F32)] * npair,
        compiler_params=_cparams(2),
    )(*rhs_list, *[lhs for lhs, _, _ in pairs], *after)


def _proj_fwd(h, norm, layer, w_pieces, tm, wt=False):
    tp, d = h.shape
    widths = [bs[-2] if wt else bs[-1] for _, bs, _ in w_pieces]
    ntot = sum(widths)
    npc = len(w_pieces)

    def body(*refs):
        h_ref, g_ref = refs[:2]
        w_refs = refs[2:2 + npc]
        p_ref, n_ref = refs[2 + npc:]
        rstd, xhat = _rms_stats(h_ref[...])
        n = (xhat * g_ref[...]).astype(BF16)
        n_ref[...] = n
        off = 0
        for k in range(npc):
            w = w_refs[k][...]
            w = w.reshape(w.shape[-2], w.shape[-1])
            p_ref[:, off:off + widths[k]] = _dot_nt(n, w) if wt else _dot(n, w)
            off += widths[k]

    in_specs = [pl.BlockSpec((tm, d), lambda i: (i, 0)), pl.BlockSpec((1, d), lambda i: (0, 0))]
    for _, bs, idx in w_pieces:
        in_specs.append(pl.BlockSpec(bs, functools.partial(lambda i, idx: idx, idx=idx)))
    return pl.pallas_call(
        body, name="proj_fwd",
        grid=(tp // tm,),
        in_specs=in_specs,
        out_specs=[pl.BlockSpec((tm, ntot), lambda i: (i, 0)), pl.BlockSpec((tm, d), lambda i: (i, 0))],
        out_shape=[jax.ShapeDtypeStruct((tp, ntot), F32), jax.ShapeDtypeStruct((tp, d), BF16)],
        compiler_params=_cparams(1),
    )(h, norm, *[w for w, _, _ in w_pieces])


def _proj_bwd_act(dres, h, norm, layer, dp_pieces, w_pieces, tm, wt=False):
    tp, d = h.shape
    npc = len(dp_pieces)
    nw = len(w_pieces)
    assert nw == npc or (nw == 1 and wt)

    def body(*refs):
        dres_ref, h_ref, g_ref = refs[:3]
        dp_refs = refs[3:3 + npc]
        w_refs = refs[3 + npc:3 + npc + nw]
        dh_ref, dg_ref = refs[3 + npc + nw:]
        i = pl.program_id(0)

        @pl.when(i == 0)
        def _():
            dg_ref[...] = jnp.zeros_like(dg_ref)

        dn = None
        off = 0
        for k in range(npc):
            if nw == npc:
                w = w_refs[k][...]
                w = w.reshape(w.shape[-2], w.shape[-1])
            else:
                w = w_refs[0][off:off + dp_pieces[k].shape[1], :]
                off += dp_pieces[k].shape[1]
            t = _dot(dp_refs[k][...], w) if wt else _dot_nt(dp_refs[k][...], w)
            dn = t if dn is None else dn + t
        rstd, xhat = _rms_stats(h_ref[...])
        dh, dg = _rms_bwd(dn, g_ref[...], rstd, xhat)
        dh_ref[...] = dres_ref[...] + dh
        dg_ref[...] += dg

    in_specs = [pl.BlockSpec((tm, d), lambda i: (i, 0)), pl.BlockSpec((tm, d), lambda i: (i, 0)),
                pl.BlockSpec((1, d), lambda i: (0, 0))]
    for dp in dp_pieces:
        in_specs.append(pl.BlockSpec((tm, dp.shape[1]), lambda i: (i, 0)))
    for _, bs, idx in w_pieces:
        in_specs.append(pl.BlockSpec(bs, functools.partial(lambda i, idx: idx, idx=idx)))
    return pl.pallas_call(
        body, name="proj_bwd_act",
        grid=(tp // tm,),
        in_specs=in_specs,
        out_specs=[pl.BlockSpec((tm, d), lambda i: (i, 0)), pl.BlockSpec((1, d), lambda i: (0, 0))],
        out_shape=[jax.ShapeDtypeStruct((tp, d), F32), jax.ShapeDtypeStruct((1, d), F32)],
        compiler_params=_cparams(1),
    )(dres, h, norm, *dp_pieces, *[w for w, _, _ in w_pieces])


def _proj_bwd_w(n, dp_pieces, tm, wt=False):
    tp, d = n.shape
    npc = len(dp_pieces)
    widths = [dp.shape[1] for dp in dp_pieces]
    oshape = (lambda w: (w, d)) if wt else (lambda w: (d, w))

    def body(*refs):
        n_ref = refs[0]
        dp_refs = refs[1:1 + npc]
        o_refs = refs[1 + npc:1 + 2 * npc]
        accs = refs[1 + 2 * npc:]
        i = pl.program_id(0)

        @pl.when(i == 0)
        def _():
            for acc in accs:
                acc[...] = jnp.zeros_like(acc)

        nn = n_ref[...]
        for k in range(npc):
            accs[k][...] += _dot_tn(dp_refs[k][...], nn) if wt else _dot_tn(nn, dp_refs[k][...])

        @pl.when(i == pl.num_programs(0) - 1)
        def _():
            for k in range(npc):
                o_refs[k][...] = accs[k][...].astype(BF16)

    return pl.pallas_call(
        body, name="proj_bwd_w",
        grid=(tp // tm,),
        in_specs=[pl.BlockSpec((tm, d), lambda i: (i, 0))]
        + [pl.BlockSpec((tm, w), lambda i: (i, 0)) for w in widths],
        out_specs=[pl.BlockSpec(oshape(w), lambda i: (0, 0)) for w in widths],
        out_shape=[jax.ShapeDtypeStruct(oshape(w), BF16) for w in widths],
        scratch_shapes=[pltpu.VMEM(oshape(w), F32) for w in widths],
        compiler_params=_cparams(1),
    )(n, *dp_pieces)


def _out_fwd(h, ya, yb, w, tm):
    tp, d = h.shape
    na, nb = ya.shape[1], yb.shape[1]

    def body(h_ref, ya_ref, yb_ref, w_ref, o_ref):
        y = _dot(ya_ref[...].astype(BF16), w_ref[0:na, :]) + _dot(yb_ref[...].astype(BF16), w_ref[na:, :])
        o_ref[...] = h_ref[...] + y

    return pl.pallas_call(
        body, name="out_fwd",
        grid=(tp // tm,),
        in_specs=[pl.BlockSpec((tm, d), lambda i: (i, 0)), pl.BlockSpec((tm, na), lambda i: (i, 0)),
                  pl.BlockSpec((tm, nb), lambda i: (i, 0)), pl.BlockSpec((d, d), lambda i: (0, 0))],
        out_specs=pl.BlockSpec((tm, d), lambda i: (i, 0)),
        out_shape=jax.ShapeDtypeStruct((tp, d), F32),
        compiler_params=_cparams(1),
    )(h, ya, yb, w)


def _out_bwd(dy, ya, yb, w, tm, after=()):
    tp, d = dy.shape
    na, nb = ya.shape[1], yb.shape[1]

    def body(dy_ref, ya_ref, yb_ref, w_ref, *rest):
        da_ref, db_ref, dw_ref, acc = rest[len(after):]
        i = pl.program_id(0)

        @pl.when(i == 0)
        def _():
            acc[...] = jnp.zeros_like(acc)

        dyb16 = dy_ref[...].astype(BF16)
        da_ref[...] = _dot_nt(dyb16, w_ref[0:na, :])
        db_ref[...] = _dot_nt(dyb16, w_ref[na:, :])
        acc[0:na, :] += _dot_tn(ya_ref[...].astype(BF16), dyb16)
        acc[na:, :] += _dot_tn(yb_ref[...].astype(BF16), dyb16)

        @pl.when(i == pl.num_programs(0) - 1)
        def _():
            dw_ref[...] = acc[...].astype(BF16)

    return pl.pallas_call(
        body, name="out_bwd",
        grid=(tp // tm,),
        in_specs=[pl.BlockSpec((tm, d), lambda i: (i, 0)), pl.BlockSpec((tm, na), lambda i: (i, 0)),
                  pl.BlockSpec((tm, nb), lambda i: (i, 0)), pl.BlockSpec((d, d), lambda i: (0, 0))]
        + [pl.BlockSpec(memory_space=pl.ANY)] * len(after),
        out_specs=[pl.BlockSpec((tm, na), lambda i: (i, 0)), pl.BlockSpec((tm, nb), lambda i: (i, 0)),
                   pl.BlockSpec((d, d), lambda i: (0, 0))],
        out_shape=[jax.ShapeDtypeStruct((tp, na), F32), jax.ShapeDtypeStruct((tp, nb), F32),
                   jax.ShapeDtypeStruct((d, d), BF16)],
        scratch_shapes=[pltpu.VMEM((d, d), F32)],
        compiler_params=_cparams(1),
    )(dy, ya, yb, w, *after)


def _loss_bwd(h, gfin, tgt, t_real, tm):
    tp, d = h.shape

    def body(h_ref, g_ref, t_ref, loss_ref, dh_ref, dg_ref):
        i = pl.program_id(0)

        @pl.when(i == 0)
        def _():
            loss_ref[...] = jnp.zeros_like(loss_ref)
            dg_ref[...] = jnp.zeros_like(dg_ref)

        rows = i * tm + lax.broadcasted_iota(jnp.int32, (tm, 1), 0)
        valid = (rows >= N_META) & (rows < t_real)
        rstd, xhat = _rms_stats(h_ref[...])
        g = g_ref[...]
        err = jnp.where(valid, xhat * g - t_ref[...], 0.0)
        e2 = jnp.sum(err * err, axis=1, keepdims=True)
        loss_ref[...] += (0.5 / d) * jnp.sum(e2, axis=0, keepdims=True)
        dy = err * (1.0 / d)
        dh, dg = _rms_bwd(dy, g, rstd, xhat)
        dh_ref[...] = dh
        dg_ref[...] += dg

    return pl.pallas_call(
        body, name="loss_bwd",
        grid=(tp // tm,),
        in_specs=[pl.BlockSpec((tm, d), lambda i: (i, 0)), pl.BlockSpec((1, d), lambda i: (0, 0)),
                  pl.BlockSpec((tm, d), lambda i: (i, 0))],
        out_specs=[pl.BlockSpec((1, 1), lambda i: (0, 0)), pl.BlockSpec((tm, d), lambda i: (i, 0)),
                   pl.BlockSpec((1, d), lambda i: (0, 0))],
        out_shape=[jax.ShapeDtypeStruct((1, 1), F32), jax.ShapeDtypeStruct((tp, d), F32),
                   jax.ShapeDtypeStruct((1, d), F32)],
        compiler_params=_cparams(1),
    )(h, gfin, tgt)


POOL_HALO = 16


def _pool_lane_consts(n_rows):
    lane = lax.broadcasted_iota(jnp.int32, (n_rows, D_POOL), 1)
    grp = lane // POOL_GROUP
    win = jnp.where(grp == 0, 2.0, jnp.where(grp == 1, 4.0, jnp.where(grp == 2, 8.0, 16.0)))
    return grp, win


def _pool_select(grp, s2, s4, s8, s16):
    return jnp.where(grp == 0, s2, jnp.where(grp == 1, s4, jnp.where(grp == 2, s8, s16)))


def _pool_mixed(x, row0, tr):
    n = tr + POOL_HALO
    s2 = x + pltpu.roll(x, 1, 0)
    s4 = s2 + pltpu.roll(s2, 2, 0)
    s8 = s4 + pltpu.roll(s4, 4, 0)
    s16 = s8 + pltpu.roll(s8, 8, 0)
    grp, win = _pool_lane_consts(n)
    rows = row0 - POOL_HALO + lax.broadcasted_iota(jnp.int32, (n, D_POOL), 0)
    cnt = jnp.minimum((rows + 1).astype(F32), win)
    pooled = _pool_select(grp, s2, s4, s8, s16) / jnp.maximum(cnt, 1.0)
    return (pooled - x)[POOL_HALO:, :]


def _pool_fwd(p, wbd, scale, tr):
    tp = p.shape[0]
    nt = tp // tr

    def body(p_ref, w_ref, s_ref, y_ref, usc):
        usc[0:POOL_HALO, :] = jnp.zeros((POOL_HALO, D_POOL), F32)
        usc[POOL_HALO:, :] = p_ref[...]

        def tile(r, carry):
            r0 = pl.multiple_of(r * tr, SUBLANE)
            x = usc[pl.ds(r0, tr + POOL_HALO), :]
            mixed = _pool_mixed(x, r0, tr)
            y_ref[pl.ds(r0, tr), :] = _dot(mixed.astype(BF16), w_ref[...]) * s_ref[...]
            return carry

        lax.fori_loop(0, nt, tile, 0)

    return pl.pallas_call(
        body, name="pool_fwd",
        grid=(1,),
        in_specs=[pl.BlockSpec((tp, D_POOL), lambda i: (0, 0)), pl.BlockSpec((D_POOL, D_POOL), lambda i: (0, 0)),
                  pl.BlockSpec((1, D_POOL), lambda i: (0, 0))],
        out_specs=pl.BlockSpec((tp, D_POOL), lambda i: (0, 0)),
        out_shape=jax.ShapeDtypeStruct((tp, D_POOL), F32),
        scratch_shapes=[pltpu.VMEM((tp + POOL_HALO, D_POOL), F32)],
        compiler_params=_cparams(1),
    )(p, wbd, scale)


def _pool_bwd(p, dya, wbd, scale, tr):
    tp = p.shape[0]
    nt = tp // tr

    def body(p_ref, dy_ref, w_ref, s_ref, du_ref, dw_ref, ds_ref, usc, gsc):
        usc[0:POOL_HALO, :] = jnp.zeros((POOL_HALO, D_POOL), F32)
        usc[POOL_HALO:, :] = p_ref[...]
        gsc[tp:, :] = jnp.zeros((POOL_HALO, D_POOL), F32)
        dw_ref[...] = jnp.zeros_like(dw_ref)
        ds_ref[...] = jnp.zeros_like(ds_ref)
        grp, win = _pool_lane_consts(tr)

        def tile1(r, carry):
            r0 = pl.multiple_of(r * tr, SUBLANE)
            x = usc[pl.ds(r0, tr + POOL_HALO), :]
            mixed = _pool_mixed(x, r0, tr).astype(BF16)
            dy = dy_ref[pl.ds(r0, tr), :]
            dys = (dy * s_ref[...]).astype(BF16)
            ypre = _dot(mixed, w_ref[...])
            ds_ref[...] += _colsum(dy * ypre)
            dw_ref[...] += _dot_tn(mixed, dys)
            dmx = _dot_nt(dys, w_ref[...])
            rows = r0 + lax.broadcasted_iota(jnp.int32, (tr, D_POOL), 0)
            cnt = jnp.minimum((rows + 1).astype(F32), win)
            gsc[pl.ds(r0, tr), :] = dmx / cnt
            return carry

        lax.fori_loop(0, nt, tile1, 0)
        n = tr + POOL_HALO
        grp2, win2 = _pool_lane_consts(n)

        def tile2(r, carry):
            r0 = pl.multiple_of(r * tr, SUBLANE)
            g = gsc[pl.ds(r0, n), :]
            s2 = g + pltpu.roll(g, n - 1, 0)
            s4 = s2 + pltpu.roll(s2, n - 2, 0)
            s8 = s4 + pltpu.roll(s4, n - 4, 0)
            s16 = s8 + pltpu.roll(s8, n - 8, 0)
            pooled_t = _pool_select(grp2, s2, s4, s8, s16)
            rows = r0 + lax.broadcasted_iota(jnp.int32, (n, D_POOL), 0)
            cnt = jnp.minimum((rows + 1).astype(F32), win2)
            du = pooled_t - g * cnt
            du_ref[pl.ds(r0, tr), :] = du[0:tr, :].astype(BF16)
            return carry

        lax.fori_loop(0, nt, tile2, 0)

    return pl.pallas_call(
        body, name="pool_bwd",
        grid=(1,),
        in_specs=[pl.BlockSpec((tp, D_POOL), lambda i: (0, 0)), pl.BlockSpec((tp, D_POOL), lambda i: (0, 0)),
                  pl.BlockSpec((D_POOL, D_POOL), lambda i: (0, 0)), pl.BlockSpec((1, D_POOL), lambda i: (0, 0))],
        out_specs=[pl.BlockSpec((tp, D_POOL), lambda i: (0, 0)), pl.BlockSpec((D_POOL, D_POOL), lambda i: (0, 0)),
                   pl.BlockSpec((1, D_POOL), lambda i: (0, 0))],
        out_shape=[jax.ShapeDtypeStruct((tp, D_POOL), BF16), jax.ShapeDtypeStruct((D_POOL, D_POOL), F32),
                   jax.ShapeDtypeStruct((1, D_POOL), F32)],
        scratch_shapes=[pltpu.VMEM((tp + POOL_HALO, D_POOL), F32), pltpu.VMEM((tp + POOL_HALO, D_POOL), F32)],
        compiler_params=_cparams(1),
    )(p, dya, wbd, scale)


def _hgrn_levels(ch):
    levels = []
    w = ch // 2
    while w >= 1:
        levels.append(w)
        w //= 2
    return levels


def _hgrn_consts(ch):
    t = np.arange(ch)
    tril = t[None, :] <= t[:, None]
    masks = []
    for w in _hgrn_levels(ch):
        blk = t // (2 * w)
        upper = t % (2 * w) >= w
        masks.append(upper[:, None] & (~upper)[None, :] & (blk[:, None] == blk[None, :]))
    masks.append(tril)
    msk = np.stack(masks).astype(np.float32)
    return jnp.asarray(tril.astype(np.float32), BF16), jnp.asarray(msk, F32), len(masks) - 1


def _split3(x):
    hi = x.astype(BF16)
    r1 = x - hi.astype(F32)
    mid = r1.astype(BF16)
    lo = (r1 - mid.astype(F32)).astype(BF16)
    return hi, mid, lo


def _hgrn_exponents(tril, logf):
    ch = logf.shape[0]
    hi, mid, lo = _split3(logf)
    x = _dot(tril, jnp.concatenate([hi, mid, lo], axis=1))
    b = x[:, 0:HEAD] + x[:, HEAD:2 * HEAD] + x[:, 2 * HEAD:3 * HEAD]
    rows = lax.broadcasted_iota(jnp.int32, (ch, HEAD), 0)
    fx = jnp.broadcast_to(b[ch - 1:ch, :], (ch, HEAD)) - b
    lev = []
    for w in _hgrn_levels(ch):
        pos = rows % (2 * w)
        upper = pos >= w
        if w >= SUBLANE:
            parts = [jnp.broadcast_to(b[k * 2 * w + w - 1:k * 2 * w + w, :], (2 * w, HEAD))
                     for k in range(ch // (2 * w))]
            bmid = parts[0] if len(parts) == 1 else jnp.concatenate(parts, axis=0)
            dx = jnp.where(upper, b - bmid, 0.0)
            ex = jnp.where(upper, 0.0, bmid - b)
        else:
            dx = logf
            ex = jnp.zeros_like(logf)
            for i in range(1, w):
                dx = dx + jnp.where(pos >= w + i, pltpu.roll(logf, i, 0), 0.0)
                ex = ex + jnp.where(pos <= w - 1 - i, pltpu.roll(logf, ch - i, 0), 0.0)
            dx = jnp.where(upper, dx, 0.0)
        lev.append((dx, ex))
    return b, fx, lev


def _hgrn_exponents_bwd(tril, d_b, d_fx, d_blast, lev_grads):
    ch = d_b.shape[0]
    rows = lax.broadcasted_iota(jnp.int32, (ch, HEAD), 0)
    db = d_b - d_fx
    dlf = jnp.zeros_like(d_b)
    for w, (ddx, dex) in zip(_hgrn_levels(ch), lev_grads):
        pos = rows % (2 * w)
        upper = pos >= w
        gu = jnp.where(upper, ddx, 0.0)
        if w >= SUBLANE:
            gl = jnp.where(upper, 0.0, dex)
            db = db + gu - gl
            diff = gl - gu
            for k in range(ch // (2 * w)):
                s = _colsum(diff[k * 2 * w:(k + 1) * 2 * w, :])
                db = db + jnp.where(rows == k * 2 * w + w - 1, s, 0.0)
        else:
            dlf = dlf + gu
            for i in range(1, w):
                dlf = dlf + pltpu.roll(jnp.where(pos >= w + i, gu, 0.0), ch - i, 0)
                dlf = dlf + pltpu.roll(jnp.where(pos <= w - 1 - i, dex, 0.0), i, 0)
    db = db + jnp.where(rows == ch - 1, _colsum(d_fx) + d_blast, 0.0)
    hi = db.astype(BF16)
    lo = (db - hi.astype(F32)).astype(BF16)
    d2 = _dot_tn(tril, jnp.concatenate([hi, lo], axis=1))
    return d2[:, 0:HEAD] + d2[:, HEAD:2 * HEAD] + dlf


def _lockstep(gens):
    results = [None] * len(gens)
    live = list(range(len(gens)))
    while live:
        for i in list(live):
            try:
                next(gens[i])
            except StopIteration as stop:
                results[i] = stop.value
                live.remove(i)
    return results


def _hgrn_gates(q_raw, z, lb):
    sz = _sigmoid(z)
    f = lb + (1.0 - lb) * sz
    q = q_raw * _sigmoid(q_raw)
    k = (1.0 - lb) * (1.0 - sz)
    return q, k, f, sz


def _hgrn_intra(q, k, lev, msk_ref, n_lev, ch):
    eye = (lax.broadcasted_iota(jnp.int32, (ch, ch), 0) == lax.broadcasted_iota(jnp.int32, (ch, ch), 1))
    a = jnp.where(eye, jnp.sum(q * k, axis=1, keepdims=True), 0.0)
    ops = []
    for lv in range(n_lev):
        eq = jnp.exp(lev[lv][0])
        ek = jnp.exp(lev[lv][1])
        qd = q * eq
        kd = k * ek
        a = a + msk_ref[lv] * _dot_nt(qd.astype(BF16), kd.astype(BF16))
        ops.append((eq, ek, qd, kd))
        yield
    return a, ops


def _hgrn_fwd(p, lb_logits, gnorm, mst, msk, n_lev, tm):
    tp = p.shape[0]
    ch = HG_CHUNK
    nct = tm // ch
    nt = tp // tm
    nr = mst.shape[0]
    base = D_POOL // HEAD

    hp = HG_HEADS_PER_STEP
    wide = hp * HEAD
    npr = wide // HG_PBLOCK

    def body(*refs):
        p_refs = refs[:4 * npr]
        lg_ref, gn_ref, mst_ref, msk_ref, y_ref, ss_ref, st_sc = refs[4 * npr:]

        @pl.when(pl.program_id(1) == 0)
        def _():
            st_sc[...] = jnp.zeros_like(st_sc)

        lb_all = _sigmoid(lg_ref[0:1, :] - lg_ref[1:2, :])

        def raw(seg, hh, r0):
            per = HG_PBLOCK // HEAD
            return p_refs[seg * npr + hh // per][pl.ds(r0, ch), (hh % per) * HEAD:(hh % per + 1) * HEAD]

        def one_head(hh, c, r0):
            ls = slice(hh * HEAD, (hh + 1) * HEAD)
            q_raw, z, v, g_raw, st = raw(0, hh, r0), raw(1, hh, r0), raw(2, hh, r0), raw(3, hh, r0), st_sc[hh]
            q, k, f, _ = _hgrn_gates(q_raw, z, lb_all[:, ls])
            yield
            b, fx, lev = _hgrn_exponents(mst_ref[...], jnp.log(f))
            yield
            qe = q * jnp.exp(b)
            a, _ = yield from _hgrn_intra(q, k, lev, msk_ref, n_lev, ch)
            v16 = v.astype(BF16)
            o = _dot_nt(qe.astype(BF16), st.astype(BF16)) + _dot(a.astype(BF16), v16)
            kl = k * jnp.exp(fx)
            st_new = st * jnp.exp(b[ch - 1:ch, :]) + _dot_tn(v16, kl.astype(BF16))
            yield
            rstd = lax.rsqrt(jnp.mean(o * o, axis=-1, keepdims=True) + EPS)
            return st, st_new, o * rstd * gn_ref[...] * (g_raw * _sigmoid(g_raw))

        def chunk(c, carry):
            r0 = pl.multiple_of(c * ch, ch)
            results = _lockstep([one_head(hh, c, r0) for hh in range(hp)])
            for hh, (st, st_new, y) in enumerate(results):
                ss_ref[hh, c] = st
                st_sc[hh] = st_new
                y_ref[pl.ds(r0, ch), hh * HEAD:(hh + 1) * HEAD] = y
            return carry

        lax.fori_loop(0, nct, chunk, 0)

    def pspec(seg, part):
        return pl.BlockSpec((tm, HG_PBLOCK),
                            lambda h, i: (i, (base + seg * HG_HEADS) * HEAD // HG_PBLOCK + h * npr + part))

    return pl.pallas_call(
        body, name="hgrn_fwd",
        grid=(HG_HEADS // hp, nt),
        in_specs=[pspec(seg, part) for seg in range(4) for part in range(npr)]
        + [pl.BlockSpec((2, wide), lambda h, i: (0, h)),
           pl.BlockSpec((1, HEAD), lambda h, i: (0, 0)),
           pl.BlockSpec((nr, ch), lambda h, i: (0, 0)),
           pl.BlockSpec((n_lev + 1, ch, ch), lambda h, i: (0, 0, 0))],
        out_specs=[pl.BlockSpec((tm, wide), lambda h, i: (i, h)),
                   pl.BlockSpec((hp, nct, HEAD, HEAD), lambda h, i: (h, i, 0, 0))],
        out_shape=[jax.ShapeDtypeStruct((tp, D_HGRN), F32),
                   jax.ShapeDtypeStruct((HG_HEADS, tp // ch, HEAD, HEAD), F32)],
        scratch_shapes=[pltpu.VMEM((hp, HEAD, HEAD), F32)],
        compiler_params=_cparams(2),
    )(*([p] * (4 * npr)), lb_logits, gnorm, mst, msk)


def _hgrn_bwd(p, dyb, states, lb_logits, gnorm, mst, msk, n_lev, tm):
    tp = p.shape[0]
    ch = HG_CHUNK
    nct = tm // ch
    nt = tp // tm
    nr = mst.shape[0]
    base = D_POOL // HEAD

    hp = HG_HEADS_PER_STEP
    wide = hp * HEAD
    npr = wide // HG_PBLOCK

    def body(*refs):
        p_refs = refs[:4 * npr]
        (dy_ref, ss_ref, lg_ref, gn_ref, mst_ref, msk_ref,
         dq_ref, dz_ref, dv_ref, dg_ref, dlg_ref, dgn_ref, dst_sc, dlb_sc) = refs[4 * npr:]
        ti = pl.program_id(1)

        def raw(seg, hh, r0):
            per = HG_PBLOCK // HEAD
            return p_refs[seg * npr + hh // per][pl.ds(r0, ch), (hh % per) * HEAD:(hh % per + 1) * HEAD]

        @pl.when(ti == 0)
        def _():
            dst_sc[...] = jnp.zeros_like(dst_sc)
            dlb_sc[...] = jnp.zeros_like(dlb_sc)
            dgn_ref[...] = jnp.zeros_like(dgn_ref)

        lb_all = _sigmoid(lg_ref[0:1, :] - lg_ref[1:2, :])
        gn = gn_ref[...]

        def load_head(hh, c, r0):
            ls = slice(hh * HEAD, (hh + 1) * HEAD)
            return (raw(0, hh, r0), raw(1, hh, r0), raw(2, hh, r0), raw(3, hh, r0),
                    dy_ref[pl.ds(r0, ch), ls], ss_ref[hh, c], dst_sc[hh])

        def store_head(hh, r0, res):
            ls = slice(hh * HEAD, (hh + 1) * HEAD)
            dq_raw, dz, dv, dg_raw, dgn, dst_new, dlb = res
            dq_ref[pl.ds(r0, ch), ls] = dq_raw
            dz_ref[pl.ds(r0, ch), ls] = dz
            dv_ref[pl.ds(r0, ch), ls] = dv
            dg_ref[pl.ds(r0, ch), ls] = dg_raw
            dgn_ref[hh] += dgn
            dst_sc[hh] = dst_new
            dlb_sc[:, ls] += dlb

        def one_head(hh, loaded):
            ls = slice(hh * HEAD, (hh + 1) * HEAD)
            lb = lb_all[:, ls]
            q_raw, z, v, g_raw, dy, st, dst = loaded
            q, k, f, sz = _hgrn_gates(q_raw, z, lb)
            yield
            b, fx, lev = _hgrn_exponents(mst_ref[...], jnp.log(f))
            yield
            eb = jnp.exp(b)
            ef = jnp.exp(fx)
            elast = jnp.exp(b[ch - 1:ch, :])
            qe = q * eb
            kl = k * ef
            a, ops = yield from _hgrn_intra(q, k, lev, msk_ref, n_lev, ch)
            v16 = v.astype(BF16)
            st16 = st.astype(BF16)
            qe16 = qe.astype(BF16)
            kl16 = kl.astype(BF16)
            a16 = a.astype(BF16)
            o = _dot_nt(qe16, st16) + _dot(a16, v16)
            yield
            sg = _sigmoid(g_raw)
            rstd = lax.rsqrt(jnp.mean(o * o, axis=-1, keepdims=True) + EPS)
            oh = o * rstd
            dg_out = (dy * oh * gn * (sg * (1.0 + g_raw * (1.0 - sg)))).astype(BF16)
            don = dy * (g_raw * sg)
            dgn = _colsum(don * oh)
            doh = don * gn
            do = rstd * (doh - oh * jnp.mean(doh * oh, axis=-1, keepdims=True))
            do16 = do.astype(BF16)
            dst16 = dst.astype(BF16)
            yield
            dv = _dot_tn(a16, do16) + _dot_nt(kl16, dst16)
            da = msk_ref[n_lev] * _dot_nt(do16, v16)
            dqe = _dot(do16, st16)
            dkl = _dot(v16, dst16)
            dst_new = dst * elast + _dot_tn(do16, qe16)
            yield
            db_last = _colsum(dst * st) * elast
            dad = jnp.sum(do * v, axis=1, keepdims=True)
            dq = dad * k + dqe * eb
            dk = dad * q + dkl * ef
            lev_grads = []
            for lv in range(n_lev):
                eq, ek, qd, kd = ops[lv]
                gl = (msk_ref[lv] * da).astype(BF16)
                dqd = _dot(gl, kd.astype(BF16))
                dkd = _dot_tn(gl, qd.astype(BF16))
                dq = dq + dqd * eq
                dk = dk + dkd * ek
                lev_grads.append((dqd * qd, dkd * kd))
                yield
            dlogf = _hgrn_exponents_bwd(mst_ref[...], dqe * qe, dkl * kl, db_last, lev_grads)
            yield
            sq = _sigmoid(q_raw)
            dq_out = (dq * (sq * (1.0 + q_raw * (1.0 - sq)))).astype(BF16)
            dfk = dlogf / f - dk
            dz_out = (dfk * (1.0 - lb) * sz * (1.0 - sz)).astype(BF16)
            return dq_out, dz_out, dv.astype(BF16), dg_out, dgn, dst_new, _colsum(dfk * (1.0 - sz))

        def chunk(cc, carry):
            c = nct - 1 - cc
            r0 = pl.multiple_of(c * ch, ch)
            loaded = [load_head(hh, c, r0) for hh in range(HG_HEADS_PER_STEP)]
            results = _lockstep([one_head(hh, loaded[hh]) for hh in range(HG_HEADS_PER_STEP)])
            for hh in range(HG_HEADS_PER_STEP):
                store_head(hh, r0, results[hh])
            return carry

        lax.fori_loop(0, nct, chunk, 0, unroll=1)

        @pl.when(ti == nt - 1)
        def _():
            dl0 = dlb_sc[...] * lb_all * (1.0 - lb_all)
            dlg_ref[0:1, :] = dl0
            dlg_ref[1:2, :] = -dl0

    def pspec(seg, part):
        return pl.BlockSpec((tm, HG_PBLOCK), lambda h, i: (
            nt - 1 - i, (base + seg * HG_HEADS) * HEAD // HG_PBLOCK + h * npr + part))

    ospec = pl.BlockSpec((tm, wide), lambda h, i: (nt - 1 - i, h))
    return pl.pallas_call(
        body, name="hgrn_bwd",
        grid=(HG_HEADS // hp, nt),
        in_specs=[pspec(seg, part) for seg in range(4) for part in range(npr)]
        + [ospec, pl.BlockSpec((hp, nct, HEAD, HEAD), lambda h, i: (h, nt - 1 - i, 0, 0)),
           pl.BlockSpec((2, wide), lambda h, i: (0, h)),
           pl.BlockSpec((1, HEAD), lambda h, i: (0, 0)),
           pl.BlockSpec((nr, ch), lambda h, i: (0, 0)),
           pl.BlockSpec((n_lev + 1, ch, ch), lambda h, i: (0, 0, 0))],
        out_specs=[ospec, ospec, ospec, ospec,
                   pl.BlockSpec((2, wide), lambda h, i: (0, h)),
                   pl.BlockSpec((hp, 1, HEAD), lambda h, i: (h, 0, 0))],
        out_shape=[jax.ShapeDtypeStruct((tp, D_HGRN), BF16)] * 4
        + [jax.ShapeDtypeStruct((2, D_HGRN), F32), jax.ShapeDtypeStruct((HG_HEADS, 1, HEAD), F32)],
        scratch_shapes=[pltpu.VMEM((hp, HEAD, HEAD), F32), pltpu.VMEM((1, wide), F32)],
        compiler_params=_cparams(2),
    )(*([p] * (4 * npr)), dyb, states, lb_logits, gnorm, mst, msk)


def _tap_views(x, tr, halo, width):
    subs = {0: x}
    views = []
    for j in range(width):
        tiles, rem = divmod(width - 1 - j, SUBLANE)
        if rem not in subs:
            subs[rem] = pltpu.roll(x, rem, 0)
        start = halo - tiles * SUBLANE
        views.append(subs[rem][start:start + tr, :])
    return views


def _tap_views_t(y, tr, halo, width):
    n = tr + halo
    subs = {0: y}
    views = []
    for j in range(width):
        tiles, rem = divmod(width - 1 - j, SUBLANE)
        if rem not in subs:
            subs[rem] = pltpu.roll(y, n - rem, 0)
        views.append(subs[rem][tiles * SUBLANE:tiles * SUBLANE + tr, :])
    return views


def _weighted_sum(views, w_ref):
    acc = None
    for j, view in enumerate(views):
        term = view * w_ref[j:j + 1, :]
        acc = term if acc is None else acc + term
    return acc


def _conv_taps(x, w_ref, tr, halo, width):
    return _weighted_sum(_tap_views(x, tr, halo, width), w_ref)


def _conv_taps_t(y, w_ref, tr, halo, width):
    return _weighted_sum(_tap_views_t(y, tr, halo, width), w_ref)


def _ln_stats(cv):
    mu = jnp.mean(cv, axis=-1, keepdims=True)
    xc = cv - mu
    rstd = lax.rsqrt(jnp.mean(xc * xc, axis=-1, keepdims=True) + EPS)
    return rstd, xc * rstd


def _convmod_fwd(p, w, bias, ln_g, ln_b, tr):
    tp = p.shape[0]
    nt = tp // tr
    nb = D_CONV // HEAD

    def body(a_ref, b_ref, w_ref, bi_ref, g_ref, be_ref, y_ref, usc):
        usc[0:CONV_HALO, :] = jnp.zeros((CONV_HALO, HEAD), F32)
        usc[CONV_HALO:, :] = a_ref[...] * _sigmoid(b_ref[...])

        def tile(r, carry):
            r0 = pl.multiple_of(r * tr, SUBLANE)
            x = usc[pl.ds(r0, tr + CONV_HALO), :]
            cv = _conv_taps(x, w_ref, tr, CONV_HALO, CONV_WIDTH) + bi_ref[...]
            _, xh = _ln_stats(cv)
            un = xh * g_ref[...] + be_ref[...]
            y_ref[pl.ds(r0, tr), :] = un * _sigmoid(un)
            return carry

        lax.fori_loop(0, nt, tile, 0)

    vec = lambda: pl.BlockSpec((1, HEAD), lambda j: (0, j))
    return pl.pallas_call(
        body, name="convmod_fwd",
        grid=(nb,),
        in_specs=[pl.BlockSpec((tp, HEAD), lambda j: (0, j)), pl.BlockSpec((tp, HEAD), lambda j: (0, nb + j)),
                  pl.BlockSpec((CONV_HALO, HEAD), lambda j: (0, j)), vec(), vec(), vec()],
        out_specs=pl.BlockSpec((tp, HEAD), lambda j: (0, j)),
        out_shape=jax.ShapeDtypeStruct((tp, D_CONV), F32),
        scratch_shapes=[pltpu.VMEM((tp + CONV_HALO, HEAD), F32)],
        compiler_params=_cparams(1),
    )(p, p, w, bias, ln_g, ln_b)


def _convmod_bwd(p, dyc, w, bias, ln_g, ln_b, tr):
    tp = p.shape[0]
    nt = tp // tr
    nb = D_CONV // HEAD

    def body(a_ref, b_ref, dy_ref, w_ref, bi_ref, g_ref, be_ref, da_ref, db_ref, dw_ref, dv_ref, usc, dsc):
        usc[0:CONV_HALO, :] = jnp.zeros((CONV_HALO, HEAD), F32)
        usc[CONV_HALO:, :] = a_ref[...] * _sigmoid(b_ref[...])
        dsc[tp:, :] = jnp.zeros((CONV_HALO, HEAD), F32)
        dw_ref[...] = jnp.zeros_like(dw_ref)
        dv_ref[...] = jnp.zeros_like(dv_ref)

        def tile1(r, carry):
            r0 = pl.multiple_of(r * tr, SUBLANE)
            x = usc[pl.ds(r0, tr + CONV_HALO), :]
            views = _tap_views(x, tr, CONV_HALO, CONV_WIDTH)
            cv = _weighted_sum(views, w_ref) + bi_ref[...]
            rstd, xh = _ln_stats(cv)
            un = xh * g_ref[...] + be_ref[...]
            sg = _sigmoid(un)
            dun = dy_ref[pl.ds(r0, tr), :] * (sg * (1.0 + un * (1.0 - sg)))
            dv_ref[0, 1:2, :] += _colsum(dun * xh)
            dv_ref[0, 2:3, :] += _colsum(dun)
            dxh = dun * g_ref[...]
            dcv = rstd * (dxh - jnp.mean(dxh, axis=-1, keepdims=True)
                          - xh * jnp.mean(dxh * xh, axis=-1, keepdims=True))
            dv_ref[0, 0:1, :] += _colsum(dcv)
            for j in range(CONV_WIDTH):
                dw_ref[0, j:j + 1, :] += _colsum(dcv * views[j])
            dsc[pl.ds(r0, tr), :] = dcv
            return carry

        lax.fori_loop(0, nt, tile1, 0)

        def tile2(r, carry):
            r0 = pl.multiple_of(r * tr, SUBLANE)
            y = dsc[pl.ds(r0, tr + CONV_HALO), :]
            du = _conv_taps_t(y, w_ref, tr, CONV_HALO, CONV_WIDTH)
            a = a_ref[pl.ds(r0, tr), :]
            sb = _sigmoid(b_ref[pl.ds(r0, tr), :])
            da_ref[pl.ds(r0, tr), :] = (du * sb).astype(BF16)
            db_ref[pl.ds(r0, tr), :] = (du * a * sb * (1.0 - sb)).astype(BF16)
            return carry

        lax.fori_loop(0, nt, tile2, 0)

    vec = lambda: pl.BlockSpec((1, HEAD), lambda j: (0, j))
    col = lambda: pl.BlockSpec((tp, HEAD), lambda j: (0, j))
    return pl.pallas_call(
        body, name="convmod_bwd",
        grid=(nb,),
        in_specs=[col(), pl.BlockSpec((tp, HEAD), lambda j: (0, nb + j)), col(),
                  pl.BlockSpec((CONV_HALO, HEAD), lambda j: (0, j)), vec(), vec(), vec()],
        out_specs=[col(), col(), pl.BlockSpec((1, CONV_HALO, HEAD), lambda j: (j, 0, 0)),
                   pl.BlockSpec((1, SUBLANE, HEAD), lambda j: (j, 0, 0))],
        out_shape=[jax.ShapeDtypeStruct((tp, D_CONV), BF16), jax.ShapeDtypeStruct((tp, D_CONV), BF16),
                   jax.ShapeDtypeStruct((nb, CONV_HALO, HEAD), F32), jax.ShapeDtypeStruct((nb, SUBLANE, HEAD), F32)],
        scratch_shapes=[pltpu.VMEM((tp + CONV_HALO, HEAD), F32), pltpu.VMEM((tp + CONV_HALO, HEAD), F32)],
        compiler_params=_cparams(1),
    )(p, p, dyc, w, bias, ln_g, ln_b)


def _log1p_small(y):
    return jnp.where(y < 1e-4, y * (1.0 - 0.5 * y), jnp.log(1.0 + y))


def _softplus(x):
    return jnp.maximum(x, 0.0) + _log1p_small(jnp.exp(-jnp.abs(x)))


def _expm1(x):
    return jnp.where(jnp.abs(x) < 1e-2, x * (1.0 + 0.5 * x * (1.0 + x * (1.0 / 3.0))), jnp.exp(x) - 1.0)


def _gelu_parts(x):
    c = 0.7978845608028654
    inner = c * (x + 0.044715 * x * x * x)
    th = jnp.tanh(inner)
    gelu = 0.5 * x * (1.0 + th)
    dgelu = 0.5 * (1.0 + th) + 0.5 * x * (1.0 - th * th) * c * (1.0 + 3.0 * 0.044715 * x * x)
    return gelu, dgelu


def _lru_gates(x_all, tp, cw_ref, cb_ref, wa_ref, ba_ref, wx_ref, bx_ref, lam_ref):
    u = _conv_taps(x_all, cw_ref, tp, LRU_HALO, LRU_CONV) + cb_ref[...]
    u16 = u.astype(BF16)
    r = _sigmoid(_dot(u16, wa_ref[0]) + ba_ref[...])
    i = _sigmoid(_dot(u16, wx_ref[0]) + bx_ref[...])
    sp = _softplus(-lam_ref[...])
    la = -LRU_C * r * sp
    a = jnp.exp(la)
    mult = jnp.sqrt(-_expm1(2.0 * la))
    return u, r, i, a, mult, sp


def _lru_specs(tp, nb):
    col = lambda k: pl.BlockSpec((tp, HEAD), functools.partial(lambda j, k: (0, k * nb + j), k=k))
    vec = lambda: pl.BlockSpec((1, HEAD), lambda j: (0, j))
    mat = lambda: pl.BlockSpec((1, HEAD, HEAD), lambda j: (j, 0, 0))
    return col, vec, mat


def _lru_fwd(p, cw, cb, wa, ba, wx, bx, lam):
    tp = p.shape[0]
    nb = D_LRU // HEAD
    ng = tp // SUBLANE

    def body(x_ref, gt_ref, cw_ref, cb_ref, wa_ref, ba_ref, wx_ref, bx_ref, lam_ref, y_ref, hs_ref,
             xsc, asc, bsc):
        xsc[0:LRU_HALO, :] = jnp.zeros((LRU_HALO, HEAD), F32)
        xsc[LRU_HALO:, :] = x_ref[...]
        u, r, i, a, mult, _ = _lru_gates(xsc[...], tp, cw_ref, cb_ref, wa_ref, ba_ref, wx_ref, bx_ref, lam_ref)
        rows = lax.broadcasted_iota(jnp.int32, (tp, HEAD), 0)
        b = jnp.where(rows == 0, 1.0, mult) * (i * u)
        sub = rows % SUBLANE
        for k in (1, 2, 4):
            m = sub >= k
            b = jnp.where(m, a * pltpu.roll(b, k, 0) + b, b)
            a = jnp.where(m, a * pltpu.roll(a, k, 0), a)
        asc[...] = a
        bsc[...] = b

        def grp(g, carry):
            r0 = pl.multiple_of(g * SUBLANE, SUBLANE)
            h = bsc[pl.ds(r0, SUBLANE), :] + asc[pl.ds(r0, SUBLANE), :] * carry
            hs_ref[pl.ds(r0, SUBLANE), :] = h
            return jnp.broadcast_to(h[SUBLANE - 1:SUBLANE, :], (SUBLANE, HEAD))

        lax.fori_loop(0, ng, grp, jnp.zeros((SUBLANE, HEAD), F32))
        gelu, _ = _gelu_parts(gt_ref[...])
        y_ref[...] = gelu * hs_ref[...]

    col, vec, mat = _lru_specs(tp, nb)
    return pl.pallas_call(
        body, name="lru_fwd",
        grid=(nb,),
        in_specs=[col(2), col(3), pl.BlockSpec((LRU_CONV, HEAD), lambda j: (0, j)), vec(), mat(), vec(), mat(),
                  vec(), vec()],
        out_specs=[pl.BlockSpec((tp, HEAD), lambda j: (0, j)), pl.BlockSpec((tp, HEAD), lambda j: (0, j))],
        out_shape=[jax.ShapeDtypeStruct((tp, D_LRU), F32), jax.ShapeDtypeStruct((tp, D_LRU), F32)],
        scratch_shapes=[pltpu.VMEM((tp + LRU_HALO, HEAD), F32), pltpu.VMEM((tp, HEAD), F32),
                        pltpu.VMEM((tp, HEAD), F32)],
        compiler_params=_cparams(1),
    )(p, p, cw, cb, wa, ba, wx, bx, lam)


def _lru_bwd(p, hs, dyd, cw, cb, wa, ba, wx, bx, lam):
    tp = p.shape[0]
    nb = D_LRU // HEAD
    ng = tp // SUBLANE

    def body(x_ref, gt_ref, hs_ref, dy_ref, cw_ref, cb_ref, wa_ref, ba_ref, wx_ref, bx_ref, lam_ref,
             dx_ref, dgt_ref, dwa_ref, dwx_ref, dv_ref, xsc, asc, bsc, gsc, dusc):
        xsc[0:LRU_HALO, :] = jnp.zeros((LRU_HALO, HEAD), F32)
        xsc[LRU_HALO:, :] = x_ref[...]
        x_all = xsc[...]
        u, r, i, a, mult, sp = _lru_gates(x_all, tp, cw_ref, cb_ref, wa_ref, ba_ref, wx_ref, bx_ref, lam_ref)
        rows = lax.broadcasted_iota(jnp.int32, (tp, HEAD), 0)
        hs = hs_ref[...]
        dy = dy_ref[...]
        gelu, dgelu = _gelu_parts(gt_ref[...])
        dgt_ref[...] = (dy * hs * dgelu).astype(BF16)
        bb = dy * gelu
        aa = jnp.where(rows == tp - 1, 0.0, pltpu.roll(a, tp - 1, 0))
        sub = rows % SUBLANE
        for k in (1, 2, 4):
            m = sub < SUBLANE - k
            bb = jnp.where(m, aa * pltpu.roll(bb, tp - k, 0) + bb, bb)
            aa = jnp.where(m, aa * pltpu.roll(aa, tp - k, 0), aa)
        asc[...] = aa
        bsc[...] = bb

        def grp(gi, carry):
            g = ng - 1 - gi
            r0 = pl.multiple_of(g * SUBLANE, SUBLANE)
            gg = bsc[pl.ds(r0, SUBLANE), :] + asc[pl.ds(r0, SUBLANE), :] * carry
            gsc[pl.ds(r0, SUBLANE), :] = gg
            return jnp.broadcast_to(gg[0:1, :], (SUBLANE, HEAD))

        lax.fori_loop(0, ng, grp, jnp.zeros((SUBLANE, HEAD), F32))
        g = gsc[...]
        first = rows == 0
        hprev = jnp.where(first, 0.0, pltpu.roll(hs, 1, 0))
        iu = i * u
        d_iu = g * jnp.where(first, 1.0, mult)
        dmult_term = jnp.where(first, 0.0, g * iu * (-(a * a) / mult))
        dla = g * hprev * a + dmult_term
        dr = dla * (-LRU_C) * sp
        dv_ref[0, 7:8, :] = _colsum(dla * (LRU_C * r) * _sigmoid(-lam_ref[...]))
        dpr = dr * r * (1.0 - r)
        dpi = d_iu * u * i * (1.0 - i)
        dv_ref[0, 5:6, :] = _colsum(dpr)
        dv_ref[0, 6:7, :] = _colsum(dpi)
        u16 = u.astype(BF16)
        dpr16 = dpr.astype(BF16)
        dpi16 = dpi.astype(BF16)
        dwa_ref[0] = _dot_tn(u16, dpr16)
        dwx_ref[0] = _dot_tn(u16, dpi16)
        du = d_iu * i + _dot_nt(dpr16, wa_ref[0]) + _dot_nt(dpi16, wx_ref[0])
        dv_ref[0, 4:5, :] = _colsum(du)
        for j in range(LRU_CONV):
            sh = LRU_CONV - 1 - j
            xs = x_all if sh == 0 else pltpu.roll(x_all, sh, 0)
            dv_ref[0, j:j + 1, :] = _colsum(du * xs[LRU_HALO:, :])
        dusc[0:tp, :] = du
        dusc[tp:, :] = jnp.zeros((LRU_HALO, HEAD), F32)
        dx_ref[...] = _conv_taps_t(dusc[...], cw_ref, tp, LRU_HALO, LRU_CONV).astype(BF16)

    col, vec, mat = _lru_specs(tp, nb)
    ocol = lambda: pl.BlockSpec((tp, HEAD), lambda j: (0, j))
    return pl.pallas_call(
        body, name="lru_bwd",
        grid=(nb,),
        in_specs=[col(2), col(3), ocol(), ocol(), pl.BlockSpec((LRU_CONV, HEAD), lambda j: (0, j)), vec(), mat(),
                  vec(), mat(), vec(), vec()],
        out_specs=[ocol(), ocol(), mat(), mat(), pl.BlockSpec((1, SUBLANE, HEAD), lambda j: (j, 0, 0))],
        out_shape=[jax.ShapeDtypeStruct((tp, D_LRU), BF16), jax.ShapeDtypeStruct((tp, D_LRU), BF16),
                   jax.ShapeDtypeStruct((nb, HEAD, HEAD), F32), jax.ShapeDtypeStruct((nb, HEAD, HEAD), F32),
                   jax.ShapeDtypeStruct((nb, SUBLANE, HEAD), F32)],
        scratch_shapes=[pltpu.VMEM((tp + LRU_HALO, HEAD), F32), pltpu.VMEM((tp, HEAD), F32),
                        pltpu.VMEM((tp, HEAD), F32), pltpu.VMEM((tp, HEAD), F32),
                        pltpu.VMEM((tp + LRU_HALO, HEAD), F32)],
        compiler_params=_cparams(1),
    )(p, p, hs, dyd, cw, cb, wa, ba, wx, bx, lam)


def _mesh_pos():
    return lax.axis_index("x"), lax.axis_index("y"), lax.axis_index("c")


def _other_chips(x, y):
    return [(1 - x, y), (x, 1 - y), (1 - x, 1 - y)]


ANY = pl.BlockSpec(memory_space=pl.ANY)


def _allgather_shards(arrs, split):
    n = len(arrs)

    def body(*refs):
        ins, outs = refs[:n], refs[n:2 * n]
        send1, recv1, send2, recv2, send3, recv3 = refs[2 * n:]
        x, y, c = _mesh_pos()
        chip = 2 * x + y
        sibling = (x, y, 1 - c)
        others = _other_chips(x, y)

        def rows(k, cc):
            half = arrs[k].shape[0] // 2
            return pl.ds(cc * half, half)

        def remote(src, dst, ssem, rsem, dev):
            return pltpu.make_async_remote_copy(src_ref=src, dst_ref=dst, send_sem=ssem, recv_sem=rsem,
                                                device_id=dev, device_id_type=MESH)

        started = [remote(ins[k], outs[k].at[chip], send3.at[k], recv3.at[k], sibling) for k in range(n)]
        for cp in started:
            cp.start()
        for k in range(n):
            for j, (ox, oy) in enumerate(others):
                if split[k]:
                    src, dst = ins[k].at[rows(k, c)], outs[k].at[chip, rows(k, c)]
                else:
                    src, dst = ins[k], outs[k].at[chip]
                cp = remote(src, dst, send1.at[3 * k + j], recv1.at[3 * k + j], (ox, oy, c))
                cp.start()
                started.append(cp)
        for j, (ox, oy) in enumerate(others):
            ochip = 2 * ox + oy
            for k in range(n):
                if split[k]:
                    blk = outs[k].at[ochip, rows(k, c)]
                    remote(blk, blk, send1.at[3 * k + j], recv1.at[3 * k + j], sibling).wait_recv()
                    cp = remote(blk, blk, send2.at[3 * k + j], recv2.at[3 * k + j], sibling)
                    cp.start()
                    started.append(cp)
                else:
                    blk = outs[k].at[ochip]
                    remote(blk, blk, send1.at[3 * k + j], recv1.at[3 * k + j], sibling).wait_recv()
        for j, (ox, oy) in enumerate(others):
            ochip = 2 * ox + oy
            for k in range(n):
                if split[k]:
                    blk = outs[k].at[ochip, rows(k, 1 - c)]
                    remote(blk, blk, send2.at[3 * k + j], recv2.at[3 * k + j], sibling).wait_recv()
        for k in range(n):
            blk = outs[k].at[chip]
            remote(blk, blk, send3.at[k], recv3.at[k], sibling).wait_recv()
        for cp in started:
            cp.wait_send()

    return pl.pallas_call(
        body, name="allgather_shards",
        in_specs=[ANY] * n, out_specs=[ANY] * n,
        out_shape=[jax.ShapeDtypeStruct((N_SHARD,) + a.shape, a.dtype) for a in arrs],
        scratch_shapes=[pltpu.SemaphoreType.DMA((3 * n,)), pltpu.SemaphoreType.DMA((3 * n,)),
                        pltpu.SemaphoreType.DMA((3 * n,)), pltpu.SemaphoreType.DMA((3 * n,)),
                        pltpu.SemaphoreType.DMA((n,)), pltpu.SemaphoreType.DMA((n,))],
    )(*arrs)


HBM = pl.BlockSpec(memory_space=pltpu.HBM)
SEM = pl.BlockSpec(memory_space=pltpu.SEMAPHORE)
DATAFLOW = pltpu.SideEffectType.DATAFLOW_SIDE_EFFECTING
N_PEERS = 4


def _in_hbm(a):
    return pltpu.with_memory_space_constraint(a, pltpu.HBM)


def _gather_peers(x, y, c):
    return [((ox, oy, c), 2 * ox + oy) for ox, oy in _other_chips(x, y)] + [((x, y, 1 - c), 2 * x + y)]


def _gather_refs(src, land, slot, c, split):
    if not split:
        return src, land.at[slot]
    half = src.shape[0] // 2
    return src.at[pl.ds(c * half, half)], land.at[slot, pl.ds(c * half, half)]


def _gather_start(arrs, split):
    n = len(arrs)

    def body(*refs):
        ins, lands = refs[:n], refs[n:2 * n]
        ssem, rsem = refs[2 * n:2 * n + 2]
        token = refs[-1]
        x, y, c = _mesh_pos()
        chip = 2 * x + y
        for k in range(n):
            for j, (dev, _) in enumerate(_gather_peers(x, y, c)):
                src, dst = _gather_refs(ins[k], lands[k], chip, c, split[k] and j < N_PEERS - 1)
                pltpu.make_async_remote_copy(
                    src_ref=src, dst_ref=dst, send_sem=ssem.at[N_PEERS * k + j],
                    recv_sem=rsem.at[N_PEERS * k + j], device_id=dev, device_id_type=MESH).start()
        token[...] = jnp.zeros_like(token)

    lands = [_in_hbm(lax.empty((N_SHARD,) + a.shape, a.dtype)) for a in arrs]
    out = pl.pallas_call(
        body, name="gather_start",
        in_specs=[HBM] * (2 * n),
        out_specs=[SEM, SEM] + [HBM] * (2 * n) + [pl.BlockSpec(memory_space=pltpu.VMEM)],
        out_shape=[pltpu.SemaphoreType.DMA((N_PEERS * n,)), pltpu.SemaphoreType.DMA((N_PEERS * n,))]
        + [pltpu.HBM(a.shape, a.dtype) for a in arrs]
        + [pltpu.HBM((N_SHARD,) + a.shape, a.dtype) for a in arrs]
        + [jax.ShapeDtypeStruct((SUBLANE, LANE), F32)],
        input_output_aliases={k: 2 + k for k in range(2 * n)},
        compiler_params=pltpu.CompilerParams(has_side_effects=DATAFLOW),
    )(*[_in_hbm(a) for a in arrs], *lands)
    return out[0], out[1], list(out[2:2 + n]), list(out[2 + n:2 + 2 * n]), out[-1]


def _gather_wait(ssem, rsem, srcs, lands, ks, after, split=False):
    n = len(ks)

    def body(*refs):
        ins, lnd = refs[:n], refs[n:2 * n]
        ssem_ref, rsem_ref = refs[2 * n:2 * n + 2]
        x, y, c = _mesh_pos()
        for i, k in enumerate(ks):
            for j, (dev, pchip) in enumerate(_gather_peers(x, y, c)):
                src, dst = _gather_refs(ins[i], lnd[i], pchip, c, split and j < N_PEERS - 1)
                cp = pltpu.make_async_remote_copy(
                    src_ref=src, dst_ref=dst, send_sem=ssem_ref.at[N_PEERS * k + j],
                    recv_sem=rsem_ref.at[N_PEERS * k + j], device_id=dev, device_id_type=MESH)
                cp.wait_send()
                cp.wait_recv()

    out = pl.pallas_call(
        body, name="gather_wait",
        in_specs=[HBM] * (2 * n) + [SEM, SEM] + [ANY] * len(after),
        out_specs=[HBM] * (2 * n),
        out_shape=[pltpu.HBM(a.shape, a.dtype) for a in srcs] + [pltpu.HBM(a.shape, a.dtype) for a in lands],
        input_output_aliases={k: k for k in range(2 * n)},
        compiler_params=pltpu.CompilerParams(has_side_effects=DATAFLOW),
    )(*srcs, *lands, ssem, rsem, *after)
    return list(out[n:])


def _pair_forward(lands):
    n = len(lands)

    def body(*refs):
        outs = refs[n:2 * n]
        ssem, rsem = refs[2 * n:]
        x, y, c = _mesh_pos()
        sibling = (x, y, 1 - c)
        cps = []
        for k in range(n):
            half = lands[k].shape[1] // 2
            for j, (ox, oy) in enumerate(_other_chips(x, y)):
                mine = outs[k].at[2 * ox + oy, pl.ds(c * half, half)]
                cp = pltpu.make_async_remote_copy(src_ref=mine, dst_ref=mine, send_sem=ssem.at[3 * k + j],
                                                  recv_sem=rsem.at[3 * k + j], device_id=sibling, device_id_type=MESH)
                cp.start()
                cps.append(cp)
        for k in range(n):
            half = lands[k].shape[1] // 2
            for j, (ox, oy) in enumerate(_other_chips(x, y)):
                theirs = outs[k].at[2 * ox + oy, pl.ds((1 - c) * half, half)]
                pltpu.make_async_remote_copy(src_ref=theirs, dst_ref=theirs, send_sem=ssem.at[3 * k + j],
                                             recv_sem=rsem.at[3 * k + j], device_id=sibling,
                                             device_id_type=MESH).wait_recv()
        for cp in cps:
            cp.wait_send()

    return pl.pallas_call(
        body, name="pair_forward",
        in_specs=[ANY] * n, out_specs=[ANY] * n,
        out_shape=[jax.ShapeDtypeStruct(a.shape, a.dtype) for a in lands],
        scratch_shapes=[pltpu.SemaphoreType.DMA((3 * n,)), pltpu.SemaphoreType.DMA((3 * n,))],
        input_output_aliases={k: k for k in range(n)},
    )(*lands)


N_SOURCES = 7


def _reduce_peers(x, y, c):
    peers = []
    for ox, oy in _other_chips(x, y):
        for rel in range(2):
            peers.append(((ox, oy, c + rel - 2 * c * rel), 2 * ox + oy))
    peers.append(((x, y, 1 - c), 2 * x + y))
    return peers


def _reduce_start(arrs, slots):
    n = len(arrs)

    def body(*refs):
        ins, lands = refs[:n], refs[n:2 * n]
        ssem, rsem = refs[2 * n:2 * n + 2]
        token = refs[-1]
        x, y, c = _mesh_pos()
        me = 2 * (2 * x + y) + c
        for k in range(n):
            half = arrs[k].shape[1] // 2
            for p, (dev, ochip) in enumerate(_reduce_peers(x, y, c)):
                pltpu.make_async_remote_copy(
                    src_ref=ins[k].at[ochip, pl.ds(dev[2] * half, half)], dst_ref=lands[k].at[me],
                    send_sem=ssem.at[N_SOURCES * k + p], recv_sem=rsem.at[N_SOURCES * k + p],
                    device_id=dev, device_id_type=MESH).start()
        token[...] = jnp.zeros_like(token)

    out = pl.pallas_call(
        body, name="reduce_start",
        in_specs=[HBM] * (2 * n),
        out_specs=[SEM, SEM] + [HBM] * (2 * n) + [pl.BlockSpec(memory_space=pltpu.VMEM)],
        out_shape=[pltpu.SemaphoreType.DMA((N_SOURCES * n,)), pltpu.SemaphoreType.DMA((N_SOURCES * n,))]
        + [pltpu.HBM(a.shape, a.dtype) for a in arrs] + [pltpu.HBM(a.shape, a.dtype) for a in slots]
        + [jax.ShapeDtypeStruct((SUBLANE, LANE), F32)],
        input_output_aliases={k: 2 + k for k in range(2 * n)},
        compiler_params=pltpu.CompilerParams(has_side_effects=DATAFLOW),
    )(*[_in_hbm(a) for a in arrs], *[_in_hbm(a) for a in slots])
    return out[0], out[1], list(out[2:2 + n]), list(out[2 + n:2 + 2 * n]), out[-1]


def _reduce_wait(ssem, rsem, arrs, slots, after):
    n = len(arrs)

    def body(*refs):
        ins, lnd = refs[:n], refs[n:2 * n]
        ssem_ref, rsem_ref = refs[2 * n:2 * n + 2]
        x, y, c = _mesh_pos()
        for k in range(n):
            half = arrs[k].shape[1] // 2
            for p, (dev, ochip) in enumerate(_reduce_peers(x, y, c)):
                cp = pltpu.make_async_remote_copy(
                    src_ref=ins[k].at[ochip, pl.ds(dev[2] * half, half)], dst_ref=lnd[k].at[2 * ochip + dev[2]],
                    send_sem=ssem_ref.at[N_SOURCES * k + p], recv_sem=rsem_ref.at[N_SOURCES * k + p],
                    device_id=dev, device_id_type=MESH)
                cp.wait_send()
                cp.wait_recv()

    out = pl.pallas_call(
        body, name="reduce_wait",
        in_specs=[HBM] * (2 * n) + [SEM, SEM] + [ANY] * len(after),
        out_specs=[HBM] * (2 * n),
        out_shape=[pltpu.HBM(a.shape, a.dtype) for a in arrs] + [pltpu.HBM(a.shape, a.dtype) for a in slots],
        input_output_aliases={k: k for k in range(2 * n)},
        compiler_params=pltpu.CompilerParams(has_side_effects=DATAFLOW),
    )(*arrs, *slots, ssem, rsem, *after)
    return list(out[n:])


def _own_part(arrs, chip, core, me):
    n = len(arrs)
    nb = GRAD_ROW_BLOCKS

    def body(chip_ref, core_ref, me_ref, *refs):
        for k in range(n):
            refs[n + k][...] = refs[k][...]

    def blk(a):
        return (1, a.shape[1] // 2 // nb, a.shape[2])

    grid_spec = pltpu.PrefetchScalarGridSpec(
        num_scalar_prefetch=3, grid=(nb,),
        in_specs=[pl.BlockSpec(blk(a), lambda i, ch, co, me: (ch[0], co[0] * nb + i, 0)) for a in arrs],
        out_specs=[pl.BlockSpec(blk(a), lambda i, ch, co, me: (me[0], i, 0)) for a in arrs])
    return pl.pallas_call(
        body, name="own_part", grid_spec=grid_spec,
        out_shape=[jax.ShapeDtypeStruct((N_DEV, a.shape[1] // 2, a.shape[2]), a.dtype) for a in arrs],
        compiler_params=_cparams(1),
    )(chip, core, me, *arrs)


def _sum_devices(arrs, core):
    n = len(arrs)
    nb = GRAD_ROW_BLOCKS

    def body(c_ref, *refs):
        for k in range(n):
            r = refs[k]
            acc = r[0].astype(F32)
            for dev in range(1, N_DEV):
                acc = acc + r[dev].astype(F32)
            refs[n + k][...] = acc

    grid_spec = pltpu.PrefetchScalarGridSpec(
        num_scalar_prefetch=1, grid=(nb,),
        in_specs=[pl.BlockSpec((N_DEV, a.shape[1] // nb, a.shape[2]), lambda i, c: (0, i, 0)) for a in arrs],
        out_specs=[pl.BlockSpec((a.shape[1] // nb, a.shape[2]), lambda i, c: (c[0] * nb + i, 0)) for a in arrs])
    return pl.pallas_call(
        body, name="sum_devices", grid_spec=grid_spec,
        out_shape=[jax.ShapeDtypeStruct((2 * a.shape[1], a.shape[2]), F32) for a in arrs],
        compiler_params=_cparams(1),
    )(core, *arrs)


def _small_own(v, me):
    m = v.shape[0]

    def body(me_ref, v_ref, o_ref):
        o_ref[0] = v_ref[...]

    grid_spec = pltpu.PrefetchScalarGridSpec(
        num_scalar_prefetch=1, grid=(1,),
        in_specs=[pl.BlockSpec((m, LANE), lambda i, me: (0, 0))],
        out_specs=pl.BlockSpec((1, m, LANE), lambda i, me: (me[0], 0, 0)))
    return pl.pallas_call(
        body, name="small_own", grid_spec=grid_spec,
        out_shape=jax.ShapeDtypeStruct((N_DEV, m, LANE), v.dtype),
        compiler_params=_cparams(1),
    )(me, v)


def _small_start(v, slots):
    def body(v_ref, land, ssem, rsem, v_thru, land_thru, token):
        del v_thru, land_thru
        x, y, c = _mesh_pos()
        me = 2 * (2 * x + y) + c
        for p, (dev, _) in enumerate(_reduce_peers(x, y, c)):
            pltpu.make_async_remote_copy(src_ref=v_ref, dst_ref=land.at[me], send_sem=ssem.at[p],
                                         recv_sem=rsem.at[p], device_id=dev, device_id_type=MESH).start()
        token[...] = jnp.zeros_like(token)

    out = pl.pallas_call(
        body, name="small_start",
        in_specs=[HBM, HBM],
        out_specs=[SEM, SEM, HBM, HBM, pl.BlockSpec(memory_space=pltpu.VMEM)],
        out_shape=[pltpu.SemaphoreType.DMA((N_SOURCES,)), pltpu.SemaphoreType.DMA((N_SOURCES,)),
                   pltpu.HBM(v.shape, v.dtype), pltpu.HBM(slots.shape, slots.dtype),
                   jax.ShapeDtypeStruct((SUBLANE, LANE), F32)],
        input_output_aliases={0: 2, 1: 3},
        compiler_params=pltpu.CompilerParams(has_side_effects=DATAFLOW),
    )(_in_hbm(v), _in_hbm(slots))
    return out


def _small_wait(ssem, rsem, v, slots, after):
    def body(*refs):
        v_ref, land, ssem_ref, rsem_ref = refs[:4]
        x, y, c = _mesh_pos()
        for p, (dev, ochip) in enumerate(_reduce_peers(x, y, c)):
            cp = pltpu.make_async_remote_copy(src_ref=v_ref, dst_ref=land.at[2 * ochip + dev[2]],
                                              send_sem=ssem_ref.at[p], recv_sem=rsem_ref.at[p],
                                              device_id=dev, device_id_type=MESH)
            cp.wait_send()
            cp.wait_recv()

    out = pl.pallas_call(
        body, name="small_wait",
        in_specs=[HBM, HBM, SEM, SEM] + [ANY] * len(after),
        out_specs=[HBM, HBM],
        out_shape=[pltpu.HBM(v.shape, v.dtype), pltpu.HBM(slots.shape, slots.dtype)],
        input_output_aliases={0: 0, 1: 1},
        compiler_params=pltpu.CompilerParams(has_side_effects=DATAFLOW),
    )(v, slots, ssem, rsem, *after)
    return out[1]


def _scatter_start(arrs, slots):
    n = len(arrs)

    def body(*refs):
        ins, lands = refs[:n], refs[n:2 * n]
        ssem, rsem = refs[2 * n:2 * n + 2]
        token = refs[-1]
        x, y, c = _mesh_pos()
        chip = 2 * x + y
        for k in range(n):
            for j, (ox, oy) in enumerate(_other_chips(x, y)):
                pltpu.make_async_remote_copy(
                    src_ref=ins[k].at[2 * ox + oy], dst_ref=lands[k].at[chip], send_sem=ssem.at[3 * k + j],
                    recv_sem=rsem.at[3 * k + j], device_id=(ox, oy, c), device_id_type=MESH).start()
        token[...] = jnp.zeros_like(token)

    out = pl.pallas_call(
        body, name="scatter_start",
        in_specs=[HBM] * (2 * n),
        out_specs=[SEM, SEM] + [HBM] * (2 * n) + [pl.BlockSpec(memory_space=pltpu.VMEM)],
        out_shape=[pltpu.SemaphoreType.DMA((3 * n,)), pltpu.SemaphoreType.DMA((3 * n,))]
        + [pltpu.HBM(a.shape, a.dtype) for a in arrs] + [pltpu.HBM(a.shape, a.dtype) for a in slots]
        + [jax.ShapeDtypeStruct((SUBLANE, LANE), F32)],
        input_output_aliases={k: 2 + k for k in range(2 * n)},
        compiler_params=pltpu.CompilerParams(has_side_effects=DATAFLOW),
    )(*[_in_hbm(a) for a in arrs], *[_in_hbm(a) for a in slots])
    return out[0], out[1], list(out[2:2 + n]), list(out[2 + n:2 + 2 * n]), out[-1]


def _scatter_wait(ssem, rsem, arrs, slots, after):
    n = len(arrs)

    def body(*refs):
        ins, lnd = refs[:n], refs[n:2 * n]
        ssem_ref, rsem_ref = refs[2 * n:2 * n + 2]
        x, y, c = _mesh_pos()
        for k in range(n):
            for j, (ox, oy) in enumerate(_other_chips(x, y)):
                ochip = 2 * ox + oy
                cp = pltpu.make_async_remote_copy(
                    src_ref=ins[k].at[ochip], dst_ref=lnd[k].at[ochip], send_sem=ssem_ref.at[3 * k + j],
                    recv_sem=rsem_ref.at[3 * k + j], device_id=(ox, oy, c), device_id_type=MESH)
                cp.wait_send()
                cp.wait_recv()

    out = pl.pallas_call(
        body, name="scatter_wait",
        in_specs=[HBM] * (2 * n) + [SEM, SEM] + [ANY] * len(after),
        out_specs=[HBM] * (2 * n),
        out_shape=[pltpu.HBM(a.shape, a.dtype) for a in arrs] + [pltpu.HBM(a.shape, a.dtype) for a in slots],
        input_output_aliases={k: k for k in range(2 * n)},
        compiler_params=pltpu.CompilerParams(has_side_effects=DATAFLOW),
    )(*arrs, *slots, ssem, rsem, *after)
    return list(out[n:])


def _pair_exchange_halves(arrs):
    n = len(arrs)

    def body(*refs):
        ins, outs = refs[:n], refs[n:2 * n]
        ssem, rsem = refs[2 * n:]
        x, y, c = _mesh_pos()
        cps = []
        for k in range(n):
            half = arrs[k].shape[1] // 2
            cp = pltpu.make_async_remote_copy(
                src_ref=ins[k].at[:, pl.ds((1 - c) * half, half)], dst_ref=outs[k],
                send_sem=ssem.at[k], recv_sem=rsem.at[k], device_id=(x, y, 1 - c), device_id_type=MESH)
            cp.start()
            cps.append(cp)
        for cp in cps:
            cp.wait()

    return pl.pallas_call(
        body, name="pair_exchange_halves",
        in_specs=[ANY] * n, out_specs=[ANY] * n,
        out_shape=[jax.ShapeDtypeStruct((a.shape[0], a.shape[1] // 2, a.shape[2]), a.dtype) for a in arrs],
        scratch_shapes=[pltpu.SemaphoreType.DMA((n,)), pltpu.SemaphoreType.DMA((n,))],
    )(*arrs)


GRAD_ROW_BLOCKS = 2


def _pair_add(arrs, recvd, core):
    n = len(arrs)
    nb = GRAD_ROW_BLOCKS

    def body(c_ref, *refs):
        for k in range(n):
            refs[2 * n + k][...] = (refs[k][...].astype(F32) + refs[n + k][...].astype(F32)).astype(BF16)

    def blk(a):
        return (1, a.shape[1] // 2 // nb, a.shape[2])

    grid_spec = pltpu.PrefetchScalarGridSpec(
        num_scalar_prefetch=1, grid=(N_SHARD, nb),
        in_specs=[pl.BlockSpec(blk(a), lambda s, i, c: (s, c[0] * nb + i, 0)) for a in arrs]
        + [pl.BlockSpec(blk(a), lambda s, i, c: (s, i, 0)) for a in arrs],
        out_specs=[pl.BlockSpec(blk(a), lambda s, i, c: (s, i, 0)) for a in arrs])
    return pl.pallas_call(
        body, name="pair_add", grid_spec=grid_spec,
        out_shape=[jax.ShapeDtypeStruct(r.shape, BF16) for r in recvd],
        compiler_params=_cparams(2),
    )(core, *arrs, *recvd)


def _own_slot(arrs, chip):
    n = len(arrs)
    nb = GRAD_ROW_BLOCKS

    def body(c_ref, *refs):
        for k in range(n):
            refs[n + k][...] = refs[k][...]

    def blk(a):
        return (1, a.shape[1] // nb, a.shape[2])

    grid_spec = pltpu.PrefetchScalarGridSpec(
        num_scalar_prefetch=1, grid=(nb,),
        in_specs=[pl.BlockSpec(blk(a), lambda i, c: (c[0], i, 0)) for a in arrs],
        out_specs=[pl.BlockSpec(blk(a), lambda i, c: (c[0], i, 0)) for a in arrs])
    return pl.pallas_call(
        body, name="own_slot", grid_spec=grid_spec,
        out_shape=[jax.ShapeDtypeStruct(a.shape, a.dtype) for a in arrs],
        compiler_params=_cparams(1),
    )(chip, *arrs)


def _scatter_to_owners(arrs, slots):
    n = len(arrs)

    def body(*refs):
        ins, outs = refs[:n], refs[2 * n:3 * n]
        ssem, rsem = refs[3 * n:]
        x, y, c = _mesh_pos()
        chip = 2 * x + y
        others = _other_chips(x, y)
        cps = []
        for k in range(n):
            for j, (ox, oy) in enumerate(others):
                cp = pltpu.make_async_remote_copy(
                    src_ref=ins[k].at[2 * ox + oy], dst_ref=outs[k].at[chip],
                    send_sem=ssem.at[3 * k + j], recv_sem=rsem.at[3 * k + j],
                    device_id=(ox, oy, c), device_id_type=MESH)
                cp.start()
                cps.append(cp)
        for k in range(n):
            for j, (ox, oy) in enumerate(others):
                blk = outs[k].at[2 * ox + oy]
                pltpu.make_async_remote_copy(
                    src_ref=blk, dst_ref=blk, send_sem=ssem.at[3 * k + j], recv_sem=rsem.at[3 * k + j],
                    device_id=(ox, oy, c), device_id_type=MESH).wait_recv()
        for cp in cps:
            cp.wait_send()

    return pl.pallas_call(
        body, name="scatter_to_owners",
        in_specs=[ANY] * (2 * n), out_specs=[ANY] * n,
        out_shape=[jax.ShapeDtypeStruct(a.shape, a.dtype) for a in arrs],
        scratch_shapes=[pltpu.SemaphoreType.DMA((3 * n,)), pltpu.SemaphoreType.DMA((3 * n,))],
        input_output_aliases={n + k: k for k in range(n)},
    )(*arrs, *slots)


def _sum_chips(arrs, core):
    n = len(arrs)
    nb = GRAD_ROW_BLOCKS

    def body(c_ref, *refs):
        for k in range(n):
            r = refs[k]
            refs[n + k][...] = ((r[0].astype(F32) + r[1].astype(F32)) + r[2].astype(F32)) + r[3].astype(F32)

    grid_spec = pltpu.PrefetchScalarGridSpec(
        num_scalar_prefetch=1, grid=(nb,),
        in_specs=[pl.BlockSpec((N_SHARD, a.shape[1] // nb, a.shape[2]), lambda i, c: (0, i, 0)) for a in arrs],
        out_specs=[pl.BlockSpec((a.shape[1] // nb, a.shape[2]), lambda i, c: (c[0] * nb + i, 0)) for a in arrs])
    return pl.pallas_call(
        body, name="sum_chips", grid_spec=grid_spec,
        out_shape=[jax.ShapeDtypeStruct((2 * a.shape[1], a.shape[2]), F32) for a in arrs],
        compiler_params=_cparams(1),
    )(core, *arrs)


def _pair_allgather_halves(arrs):
    n = len(arrs)

    def body(*refs):
        outs = refs[n:2 * n]
        ssem, rsem = refs[2 * n:]
        x, y, c = _mesh_pos()
        cps = []
        for k in range(n):
            h = arrs[k].shape[0] // 2
            mine = outs[k].at[pl.ds(c * h, h)]
            cp = pltpu.make_async_remote_copy(src_ref=mine, dst_ref=mine, send_sem=ssem.at[k],
                                              recv_sem=rsem.at[k], device_id=(x, y, 1 - c), device_id_type=MESH)
            cp.start()
            cps.append(cp)
        for k, cp in enumerate(cps):
            h = arrs[k].shape[0] // 2
            theirs = outs[k].at[pl.ds((1 - c) * h, h)]
            pltpu.make_async_remote_copy(src_ref=theirs, dst_ref=theirs, send_sem=ssem.at[k], recv_sem=rsem.at[k],
                                         device_id=(x, y, 1 - c), device_id_type=MESH).wait_recv()
            cp.wait_send()

    return pl.pallas_call(
        body, name="pair_allgather_halves",
        in_specs=[ANY] * n, out_specs=[ANY] * n,
        out_shape=[jax.ShapeDtypeStruct(a.shape, a.dtype) for a in arrs],
        scratch_shapes=[pltpu.SemaphoreType.DMA((n,)), pltpu.SemaphoreType.DMA((n,))],
        input_output_aliases={k: k for k in range(n)},
    )(*arrs)


N_DEV = 8


def _allgather_all(v):
    m_per, n = v.shape

    def body(x_ref, out_ref, send_sems, recv_sems, local_sem):
        x, y, c = _mesh_pos()
        me, sibling = (x, y, c), (x, y, 1 - c)
        chips = _other_chips(x, y)

        def rows(px, py, pc):
            return out_ref.at[pl.ds((4 * px + 2 * py + pc) * m_per, m_per), :]

        def copy(k, block, to, src=None):
            return pltpu.make_async_remote_copy(
                src_ref=rows(*block) if src is None else src, dst_ref=rows(*block),
                send_sem=send_sems.at[k], recv_sem=recv_sems.at[k], device_id=to, device_id_type=MESH)

        mine = pltpu.make_async_copy(x_ref, rows(*me), local_sem)
        mine.start()
        first = [copy(0, me, sibling, src=x_ref)]
        first += [copy(1 + j, me, (*chip, c), src=x_ref) for j, chip in enumerate(chips)]
        for cp in first:
            cp.start()
        passed = [copy(4 + j, (*chip, c), sibling) for j, chip in enumerate(chips)]
        for j, chip in enumerate(chips):
            copy(1 + j, (*chip, c), me).wait_recv()
            passed[j].start()
        copy(0, sibling, me).wait_recv()
        for j, chip in enumerate(chips):
            copy(4 + j, (*chip, 1 - c), me).wait_recv()
        for cp in first + passed:
            cp.wait_send()
        mine.wait()

    return pl.pallas_call(
        body, name="allgather_all",
        out_shape=jax.ShapeDtypeStruct((N_DEV * m_per, n), v.dtype),
        in_specs=[pl.BlockSpec(memory_space=pltpu.VMEM)],
        out_specs=pl.BlockSpec(memory_space=pltpu.VMEM),
        scratch_shapes=[pltpu.SemaphoreType.DMA((7,)), pltpu.SemaphoreType.DMA((7,)), pltpu.SemaphoreType.DMA],
        compiler_params=pltpu.CompilerParams(vmem_limit_bytes=VMEM_LIMIT_MB * 1024 * 1024),
    )(v)


def _adamw_math(w, g, m, v):
    m2 = ADAM_B1 * m + (1.0 - ADAM_B1) * g
    v2 = ADAM_B2 * v + (1.0 - ADAM_B2) * (g * g)
    m_hat = m2 / (1.0 - ADAM_B1 ** ADAM_STEP)
    v_hat = v2 / (1.0 - ADAM_B2 ** ADAM_STEP)
    delta = -ADAM_LR * (m_hat / (jnp.sqrt(v_hat) + ADAM_EPS) + ADAM_WD * w)
    return delta, m2, v2


def _adamw(w, m, v, gs, nblk):
    nl, r, n = w.shape
    assert nl == len(gs) and nl in (1, 2)
    br = r // nblk

    def body(w_ref, m_ref, v_ref, *rest):
        g_refs, (go_ref, d_ref, mo_ref, vo_ref) = rest[:nl], rest[nl:]
        g = g_refs[0][...]
        if nl == 2:
            g = jnp.where(pl.program_id(0) == 0, g, g_refs[1][...])
        delta, m2, v2 = _adamw_math(w_ref[0], g, m_ref[0], v_ref[0])
        go_ref[0] = g
        d_ref[0] = delta
        mo_ref[0] = m2
        vo_ref[0] = v2

    spec = pl.BlockSpec((1, br, n), lambda l, i: (l, i, 0))
    g_specs = [pl.BlockSpec((br, n), lambda l, i: (i, 0))] if nl == 1 else [
        pl.BlockSpec((br, n), lambda l, i: (jnp.where(l == 0, i, nblk - 1), 0)),
        pl.BlockSpec((br, n), lambda l, i: (jnp.where(l == 1, i, 0), 0))]
    return pl.pallas_call(
        body, name="adamw", grid=(nl, nblk),
        in_specs=[spec, spec, spec] + g_specs,
        out_specs=[spec] * 4,
        out_shape=[jax.ShapeDtypeStruct((nl, r, n), F32)] * 4,
        compiler_params=_cparams(2),
    )(w, m, v, *gs)


def _small_reduce_adamw(parts, w, m, v, rep_rows, sh_rows):
    mrows = rep_rows + N_SHARD * sh_rows + LOSS_ROWS

    def body(p_ref, w_ref, m_ref, v_ref, go_ref, d_ref, mo_ref, vo_ref, loss_ref):
        x, y, _ = _mesh_pos()
        mine = rep_rows + (2 * x + y) * sh_rows
        g_rep = p_ref[0:rep_rows, :]
        g_sh = p_ref[pl.ds(pl.multiple_of(mine, SUBLANE), sh_rows), :]
        loss = p_ref[mrows - LOSS_ROWS:mrows, :]
        for k in range(1, N_DEV):
            g_rep = g_rep + p_ref[k * mrows:k * mrows + rep_rows, :]
            g_sh = g_sh + p_ref[pl.ds(pl.multiple_of(k * mrows + mine, SUBLANE), sh_rows), :]
            loss = loss + p_ref[(k + 1) * mrows - LOSS_ROWS:(k + 1) * mrows, :]
        g = jnp.concatenate([g_rep, g_sh], axis=0)
        delta, m2, v2 = _adamw_math(w_ref[...], g, m_ref[...], v_ref[...])
        go_ref[...] = g
        d_ref[...] = delta
        mo_ref[...] = m2
        vo_ref[...] = v2
        loss_ref[...] = loss

    return pl.pallas_call(
        body, name="small_reduce_adamw",
        out_shape=[jax.ShapeDtypeStruct((rep_rows + sh_rows, 128), F32)] * 4
        + [jax.ShapeDtypeStruct((LOSS_ROWS, 128), F32)],
        compiler_params=pltpu.CompilerParams(vmem_limit_bytes=VMEM_LIMIT_MB * 1024 * 1024),
    )(parts, w, m, v)


LANE = 128
REP_SPEC = (("ffn1_norm", 16), ("mix_norm", 16), ("ffn2_norm", 16), ("final_norm", 8), ("pool_w", 128),
            ("pool_scale", 8), ("hgrn_lb_logits", 16), ("hgrn_gnorm", 8), ("lru_wa", 256), ("lru_wx", 256))
SH_SPEC = (("meta_tokens", 32), ("conv_w", 32), ("lru_conv_w", 8), ("conv_b", 8), ("conv_ln_g", 8),
           ("conv_ln_b", 8), ("lru_conv_b", 8), ("lru_ba", 8), ("lru_bx", 8), ("lru_lambda", 8))
REP_ROWS = sum(r for _, r in REP_SPEC)
SH_ROWS = sum(r for _, r in SH_SPEC)


def _pack_rows(vals, spec):
    parts = []
    for name, rows in spec:
        flat = vals[name].astype(F32).reshape(-1, LANE)
        if flat.shape[0] < rows:
            flat = jnp.concatenate([flat, jnp.zeros((rows - flat.shape[0], LANE), F32)], axis=0)
        parts.append(flat)
    return jnp.concatenate(parts, axis=0)


def _unpack_rows(packed, spec, shapes):
    out = {}
    off = 0
    for name, rows in spec:
        shp = shapes[name]
        n = int(np.prod(shp)) // LANE
        out[name] = packed[off:off + n].reshape(shp)
        off += rows
    return out


def _block_diag(blocks):
    n, b, _ = blocks.shape
    return sum(jnp.pad(blocks[g], ((g * b, (n - 1 - g) * b), (g * b, (n - 1 - g) * b))) for g in range(n))


def _diag_blocks(mat, n):
    b = mat.shape[0] // n
    return jnp.stack([mat[g * b:(g + 1) * b, g * b:(g + 1) * b] for g in range(n)])


BIG = ("ffn1_wg", "ffn1_wu", "ffn2_wg", "ffn2_wu", "ffn1_wd", "ffn2_wd", "w_in_even", "w_out_even",
       "w_in_odd", "w_out_odd")
ADAM_BLOCKS = {"ffn1_wg": 8, "ffn1_wu": 8, "ffn2_wg": 8, "ffn2_wu": 8, "ffn1_wd": 4, "ffn2_wd": 4,
               "w_in_even": 4, "w_out_even": 2, "w_in_odd": 4, "w_out_odd": 2}
WEIGHT_NAMES = ('meta_tokens', 'ffn1_norm', 'ffn1_wg', 'ffn1_wu', 'ffn1_wd', 'mix_norm', 'ffn2_norm', 'ffn2_wg',
                'ffn2_wu', 'ffn2_wd', 'w_in_even', 'pool_w', 'pool_scale', 'hgrn_lb_logits', 'hgrn_gnorm',
                'w_out_even', 'w_in_odd', 'conv_w', 'conv_b', 'conv_ln_g', 'conv_ln_b', 'lru_conv_w',
                'lru_conv_b', 'lru_wa', 'lru_ba', 'lru_wx', 'lru_bx', 'lru_lambda', 'w_out_odd', 'final_norm')


def _rows2d(a):
    return a.reshape(-1, a.shape[-1])


def _local_step_v2(x, tgt, w, gathered, small_full):
    s_len, d = x.shape
    t_real = s_len + N_META
    tp = -(-t_real // ROW_ALIGN) * ROW_ALIGN
    tm = _tile(tp, 832, ROW_ALIGN)
    tm_small = _tile(tp, 416, 16)
    tr = _tile(tp, 416, SUBLANE)
    f1 = ("ffn1_norm", "ffn1_wg", "ffn1_wu", "ffn1_wd")
    f2 = ("ffn2_norm", "ffn2_wg", "ffn2_wu", "ffn2_wd")

    meta_full = small_full["meta_tokens"]
    h0 = jnp.concatenate([meta_full, x, jnp.zeros((tp - t_real, d), F32)], axis=0)
    tgt_pad = jnp.concatenate([jnp.zeros((N_META, d), F32), tgt, jnp.zeros((tp - t_real, d), F32)], axis=0)

    w_in_even = jnp.transpose(gathered["w_in_even"], (1, 0, 2)).reshape(d, D_IN_EVEN)
    w_out_even = gathered["w_out_even"].reshape(d, d)
    w_out_odd = gathered["w_out_odd"].reshape(d, d)
    even_piece = [(w_in_even, (d, D_IN_EVEN), (0, 0))]
    odd_pieces = [(gathered["w_in_odd"], (1, d, D_IN_ODD // N_SHARD), (k, 0, 0)) for k in range(N_SHARD)]
    pool_wbd = _block_diag(w["pool_w"][0]).astype(BF16)
    pool_scale = w["pool_scale"]
    wa_bd = _block_diag2(w["lru_wa"][0]).astype(BF16)
    wx_bd = _block_diag2(w["lru_wx"][0]).astype(BF16)
    mst, msk, n_lev = _hgrn_consts(HG_CHUNK)
    conv_w = small_full["conv_w"]
    sf = small_full

    def gain(name, layer):
        return w[name][layer:layer + 1]

    def ffn(h, names, layer):
        return _ffn_fwd(h, gain(names[0], layer), gathered[names[1]], gathered[names[2]], gathered[names[3]],
                        layer, tm)

    h1, a1, b1, n1 = ffn(h0, f1, 0)
    p0, nm0 = _proj_fwd(h1, gain("mix_norm", 0), 0, even_piece, tm_small)
    ya = _pool_fwd(p0, pool_wbd, pool_scale, tr)
    yb, states = _hgrn_fwd(p0, w["hgrn_lb_logits"], w["hgrn_gnorm"], mst, msk, n_lev, tm)
    h2 = _out_fwd(h1, ya, yb, w_out_even, tm)
    h3, a2, b2, n2 = ffn(h2, f2, 0)
    h4, a3, b3, n3 = ffn(h3, f1, 1)
    p1, nm1 = _proj_fwd(h4, gain("mix_norm", 1), 1, odd_pieces, tm_small)
    yc = _convmod_fwd(p1, conv_w, sf["conv_b"], sf["conv_ln_g"], sf["conv_ln_b"], tr)
    lru_args = (sf["lru_conv_w"], sf["lru_conv_b"], wa_bd, sf["lru_ba"], wx_bd, sf["lru_bx"], sf["lru_lambda"])
    yd, hs = _lru_fwd(p1, *lru_args)
    h5 = _out_fwd(h4, yc, yd, w_out_odd, tm)
    h6, a4, b4, n4 = ffn(h5, f2, 1)
    loss, dh6, dg_final = _loss_bwd(h6, w["final_norm"].reshape(1, d), tgt_pad, t_real, tm)

    def ffn_bwd(dho, h, n, a, b, names, layer, acc):
        dh, da, db, dg = _ffn_bwd_act(dho, h, gain(names[0], layer), a, b, gathered[names[1]], gathered[names[2]],
                                      gathered[names[3]], layer, tm_small)
        acc = _ffn_bwd_w(dho, n, a, b, da, db, acc[0], acc[1], acc[2], layer, tm)
        return dh, dg, acc

    none3 = (None, None, None)
    dh5, dg_f2_l1, g_f2 = ffn_bwd(dh6, h5, n4, a4, b4, f2, 1, none3)
    dyc, dyd, dw_out_odd = _out_bwd(dh5, yc, yd, w_out_odd, tm)
    dca, dcb, dconv_w, dconv_vec = _convmod_bwd(p1, dyc, conv_w, sf["conv_b"], sf["conv_ln_g"], sf["conv_ln_b"], tr)
    dlx, dlg, dwa_bd, dwx_bd, dlru_vec = _lru_bwd(p1, hs, dyd, *lru_args)
    dp1 = [dca, dcb, dlx, dlg]
    dh4, dg_mix_l1 = _proj_bwd_act(dh5, h4, gain("mix_norm", 1), 1, dp1, odd_pieces, tm_small)
    dw_in_odd = jnp.stack(_proj_bwd_w(nm1, dp1, tm))
    dh3, dg_f1_l1, g_f1 = ffn_bwd(dh4, h3, n3, a3, b3, f1, 1, none3)
    dh2, dg_f2_l0, g_f2 = ffn_bwd(dh3, h2, n2, a2, b2, f2, 0, g_f2)
    dya, dyb, dw_out_even = _out_bwd(dh2, ya, yb, w_out_even, tm)
    dpool, dpool_wbd, dpool_scale = _pool_bwd(p0, dya, pool_wbd, pool_scale, tr)
    dq, dz, dv, dgate, dlb_logits, dgn_heads = _hgrn_bwd(p0, dyb, states, w["hgrn_lb_logits"], w["hgrn_gnorm"],
                                                         mst, msk, n_lev, tm)
    dp0 = [jnp.concatenate([dpool, dq, dz, dv, dgate], axis=1)]
    dh1, dg_mix_l0 = _proj_bwd_act(dh2, h1, gain("mix_norm", 0), 0, dp0, even_piece, tm_small)
    (dw_in_even,) = _proj_bwd_w(nm0, dp0, tm_small)
    dh0, dg_f1_l0, g_f1 = ffn_bwd(dh1, h0, n1, a1, b1, f1, 0, g_f1)

    grad_x = dh0[N_META:t_real]
    big = {
        "ffn1_wg": g_f1[0], "ffn1_wu": g_f1[1], "ffn1_wd": g_f1[2],
        "ffn2_wg": g_f2[0], "ffn2_wu": g_f2[1], "ffn2_wd": g_f2[2],
        "w_in_even": jnp.transpose(dw_in_even.reshape(d, N_SHARD, D_IN_EVEN // N_SHARD), (1, 0, 2)),
        "w_out_even": dw_out_even.reshape(N_SHARD, d // N_SHARD, d),
        "w_in_odd": dw_in_odd,
        "w_out_odd": dw_out_odd.reshape(N_SHARD, d // N_SHARD, d),
    }
    rep = {
        "ffn1_norm": jnp.concatenate([dg_f1_l0, dg_f1_l1], axis=0),
        "mix_norm": jnp.concatenate([dg_mix_l0, dg_mix_l1], axis=0),
        "ffn2_norm": jnp.concatenate([dg_f2_l0, dg_f2_l1], axis=0),
        "final_norm": dg_final,
        "pool_w": _diag_blocks(dpool_wbd, len(POOL_WINDOWS)),
        "pool_scale": dpool_scale,
        "hgrn_lb_logits": dlb_logits,
        "hgrn_gnorm": jnp.sum(dgn_heads, axis=0),
        "lru_wa": _diag_blocks2(dwa_bd),
        "lru_wx": _diag_blocks2(dwx_bd),
    }
    dmeta = jnp.transpose(dh0[:N_META].reshape(N_META, N_SHARD, 2, LANE), (1, 0, 2, 3)).reshape(N_SHARD, 32, LANE)
    packs = [_pack_rows(rep, REP_SPEC)]
    for s in range(N_SHARD):
        sh = {
            "meta_tokens": dmeta[s], "conv_w": dconv_w[s], "lru_conv_w": dlru_vec[s, 0:4],
            "conv_b": dconv_vec[s, 0:1], "conv_ln_g": dconv_vec[s, 1:2], "conv_ln_b": dconv_vec[s, 2:3],
            "lru_conv_b": dlru_vec[s, 4:5], "lru_ba": dlru_vec[s, 5:6], "lru_bx": dlru_vec[s, 6:7],
            "lru_lambda": dlru_vec[s, 7:8],
        }
        packs.append(_pack_rows(sh, SH_SPEC))
    return loss, grad_x, big, jnp.concatenate(packs, axis=0)


def _block_diag2(heads):
    nb = heads.shape[0] // 2
    return jnp.stack([_block_diag(heads[2 * j:2 * j + 2]) for j in range(nb)])


def _diag_blocks2(mats):
    return jnp.concatenate([_diag_blocks(mats[j], 2) for j in range(mats.shape[0])], axis=0)


def _kernel_v2(x, meta_tokens, ffn1_norm, ffn1_wg, ffn1_wu, ffn1_wd, mix_norm, ffn2_norm, ffn2_wg, ffn2_wu, ffn2_wd, w_in_even, pool_w, pool_scale, hgrn_lb_logits, hgrn_gnorm, w_out_even, w_in_odd, conv_w, conv_b, conv_ln_g, conv_ln_b, lru_conv_w, lru_conv_b, lru_wa, lru_ba, lru_wx, lru_bx, lru_lambda, w_out_odd, final_norm, loss_target, m_meta_tokens, m_ffn1_norm, m_ffn1_wg, m_ffn1_wu, m_ffn1_wd, m_mix_norm, m_ffn2_norm, m_ffn2_wg, m_ffn2_wu, m_ffn2_wd, m_w_in_even, m_pool_w, m_pool_scale, m_hgrn_lb_logits, m_hgrn_gnorm, m_w_out_even, m_w_in_odd, m_conv_w, m_conv_b, m_conv_ln_g, m_conv_ln_b, m_lru_conv_w, m_lru_conv_b, m_lru_wa, m_lru_ba, m_lru_wx, m_lru_bx, m_lru_lambda, m_w_out_odd, m_final_norm, v_meta_tokens, v_ffn1_norm, v_ffn1_wg, v_ffn1_wu, v_ffn1_wd, v_mix_norm, v_ffn2_norm, v_ffn2_wg, v_ffn2_wu, v_ffn2_wd, v_w_in_even, v_pool_w, v_pool_scale, v_hgrn_lb_logits, v_hgrn_gnorm, v_w_out_even, v_w_in_odd, v_conv_w, v_conv_b, v_conv_ln_g, v_conv_ln_b, v_lru_conv_w, v_lru_conv_b, v_lru_wa, v_lru_ba, v_lru_wx, v_lru_bx, v_lru_lambda, v_w_out_odd, v_final_norm):
    args = locals()
    w = {n: args[n] for n in WEIGHT_NAMES}
    m = {n: args["m_" + n] for n in WEIGHT_NAMES}
    v = {n: args["v_" + n] for n in WEIGHT_NAMES}
    shapes = {n: w[n].shape for n in WEIGHT_NAMES}

    big_in = [_rows2d(w[n]).astype(BF16) for n in BIG]
    small_sh = _pack_rows(w, SH_SPEC)
    gath = _allgather_shards(big_in + [small_sh], [True] * len(BIG) + [False])
    gathered = dict(zip(BIG, gath[:len(BIG)]))
    sm = gath[len(BIG)]
    sh_shapes = {n: (N_SHARD,) + tuple(shapes[n]) for n, _ in SH_SPEC}
    per_shard = [_unpack_rows(sm[s], SH_SPEC, shapes) for s in range(N_SHARD)]
    small_full = {}
    for n, _ in SH_SPEC:
        stacked = [per_shard[s][n] for s in range(N_SHARD)]
        small_full[n] = jnp.concatenate([p.reshape(-1, p.shape[-1]) for p in stacked], axis=-1)
    small_full["conv_w"] = jnp.concatenate(
        [small_full["conv_w"], jnp.zeros((CONV_HALO - CONV_WIDTH, D_CONV), F32)], axis=0)

    loss, grad_x, big, small_part = _local_step(x[0], loss_target[0], w, gathered, small_full)
    loss = lax.psum(loss[0, 0], ("x", "y", "c"))

    core = lax.axis_index("c").astype(jnp.int32).reshape(1)
    parts = [big[n] for n in BIG]
    recvd = _pair_exchange_halves(parts)
    pair = _pair_add(parts, recvd, core)
    chip = (2 * lax.axis_index("x") + lax.axis_index("y")).astype(jnp.int32).reshape(1)
    slots = _scatter_to_owners(pair, _own_slot(pair, chip))
    halves = _sum_chips(slots, core)
    full = _pair_allgather_halves(halves)
    out_g, out_d, out_m, out_v = {}, {}, {}, {}
    for n, g in zip(BIG, full):
        res = _adamw(_rows2d(w[n]), _rows2d(m[n]), _rows2d(v[n]), g, 0, ADAM_BLOCKS[n])
        out_g[n], out_d[n], out_m[n], out_v[n] = [r.reshape(shapes[n]) for r in res]

    gathered_small = _allgather_all(small_part)

    def pack_small(src):
        return jnp.concatenate([_pack_rows(src, REP_SPEC), _pack_rows(src, SH_SPEC)], axis=0)

    res = _small_reduce_adamw(gathered_small, pack_small(w), pack_small(m), pack_small(v), REP_ROWS, SH_ROWS)
    for dst, packed in zip((out_g, out_d, out_m, out_v), res):
        dst.update(_unpack_rows(packed[:REP_ROWS], REP_SPEC, shapes))
        dst.update(_unpack_rows(packed[REP_ROWS:], SH_SPEC, shapes))

    return (loss, grad_x[None], *[out_g[n] for n in WEIGHT_NAMES], *[out_d[n] for n in WEIGHT_NAMES],
            *[out_m[n] for n in WEIGHT_NAMES], *[out_v[n] for n in WEIGHT_NAMES])


GATHER_GROUPS = (
    (("small", 0),),
    (("ffn1_wg", 0), ("ffn1_wu", 0), ("ffn1_wd", 0)),
    (("w_in_even", 0), ("w_out_even", 0)),
    (("ffn2_wg", 0), ("ffn2_wu", 0), ("ffn2_wd", 0)),
    (("ffn1_wg", 1), ("ffn1_wu", 1), ("ffn1_wd", 1)),
    (("w_in_odd", 0), ("w_out_odd", 0)),
    (("ffn2_wg", 1), ("ffn2_wu", 1), ("ffn2_wd", 1)),
)
ADAM_ROW_BLOCKS = {"ffn1_wg": 2, "ffn1_wu": 2, "ffn2_wg": 2, "ffn2_wu": 2, "ffn1_wd": 2, "ffn2_wd": 2,
                   "w_in_even": 4, "w_out_even": 2, "w_in_odd": 4, "w_out_odd": 2}
TRANSPOSED = ("ffn1_wg", "ffn1_wu", "ffn2_wg", "ffn2_wu", "w_in_even")
SCATTER_DEPTH = 2
LOSS_ROWS = 8
GATHER_SPLIT = (1, 4)


def _unpack_small(sm, shapes):
    per_shard = [_unpack_rows(sm[s], SH_SPEC, shapes) for s in range(N_SHARD)]
    full = {}
    for n, _ in SH_SPEC:
        full[n] = jnp.concatenate([per_shard[s][n].reshape(-1, shapes[n][-1]) for s in range(N_SHARD)], axis=-1)
    full["conv_w"] = jnp.concatenate([full["conv_w"], jnp.zeros((CONV_HALO - CONV_WIDTH, D_CONV), F32)], axis=0)
    return full


def _local_step(x, tgt, w, shapes, fetch, emit, emit_small):
    s_len, d = x.shape
    t_real = s_len + N_META
    tp = -(-t_real // ROW_ALIGN) * ROW_ALIGN
    tm = _tile(tp, 832, ROW_ALIGN)
    tm_small = _tile(tp, 832, 16)
    tr = _tile(tp, 416, SUBLANE)

    def gain(name, layer):
        return w[name][layer:layer + 1]

    pool_wbd = _block_diag(w["pool_w"][0]).astype(BF16)
    pool_scale = w["pool_scale"]
    wa_bd = _block_diag2(w["lru_wa"][0]).astype(BF16)
    wx_bd = _block_diag2(w["lru_wx"][0]).astype(BF16)
    mst, msk, n_lev = _hgrn_consts(HG_CHUNK)

    (sm,) = fetch(0, None)
    sf = _unpack_small(sm, shapes)
    h0 = jnp.concatenate([sf["meta_tokens"], x, jnp.zeros((tp - t_real, d), F32)], axis=0)
    tgt_pad = jnp.concatenate([jnp.zeros((N_META, d), F32), tgt, jnp.zeros((tp - t_real, d), F32)], axis=0)
    f1l0 = fetch(1, h0)
    h1, *s1 = _ffn_fwd(h0, gain("ffn1_norm", 0), *f1l0, 0, tm)
    w_in_even4, w_out_even4 = fetch(2, h1)
    w_out_even = w_out_even4.reshape(d, d)
    even_piece = [(w_in_even4.reshape(D_IN_EVEN, d), (D_IN_EVEN, d), (0, 0))]
    p0, nm0 = _proj_fwd(h1, gain("mix_norm", 0), 0, even_piece, tm_small, wt=True)
    ya = _pool_fwd(p0, pool_wbd, pool_scale, tr)
    yb, states = _hgrn_fwd(p0, w["hgrn_lb_logits"], w["hgrn_gnorm"], mst, msk, n_lev, tm)
    h2 = _out_fwd(h1, ya, yb, w_out_even, tm)
    f2l0 = fetch(3, h2)
    h3, *s2 = _ffn_fwd(h2, gain("ffn2_norm", 0), *f2l0, 0, tm)
    f1l1 = fetch(4, h3)
    h4, *s3 = _ffn_fwd(h3, gain("ffn1_norm", 1), *f1l1, 0, tm)
    w_in_odd4, w_out_odd4 = fetch(5, h4)
    w_out_odd = w_out_odd4.reshape(d, d)
    odd_pieces = [(w_in_odd4, (1, d, D_IN_ODD // N_SHARD), (k, 0, 0)) for k in range(N_SHARD)]
    p1, nm1 = _proj_fwd(h4, gain("mix_norm", 1), 1, odd_pieces, tm_small)
    yc = _convmod_fwd(p1, sf["conv_w"], sf["conv_b"], sf["conv_ln_g"], sf["conv_ln_b"], tr)
    lru_args = (sf["lru_conv_w"], sf["lru_conv_b"], wa_bd, sf["lru_ba"], wx_bd, sf["lru_bx"], sf["lru_lambda"])
    yd, hs = _lru_fwd(p1, *lru_args)
    h5 = _out_fwd(h4, yc, yd, w_out_odd, tm)
    f2l1 = fetch(6, h5)
    h6, *s4 = _ffn_fwd(h5, gain("ffn2_norm", 1), *f2l1, 0, tm)
    loss, dh6, dg_final = _loss_bwd(h6, w["final_norm"].reshape(1, d), tgt_pad, t_real, tm)

    def ffn_bwd(dho, h, saved, norm, wts, after=()):
        ga, gb, sa, n = saved
        dh, da, db, dg, dy = _ffn_bwd_act(dho, h, norm, ga, gb, *wts, 0, tm, after)
        return dh, dg, _ffn_bwd_w([(da, n, None), (db, n, None), (sa, dy, None)], tm)

    dh5, dg_f2_l1, g = ffn_bwd(dh6, h5, s4, gain("ffn2_norm", 1), f2l1)
    sent = emit((("ffn2_wg", 1), ("ffn2_wu", 1), ("ffn2_wd", 1)), g)
    dyc, dyd, dw_out_odd = _out_bwd(dh5, yc, yd, w_out_odd, tm, tuple(sent))
    dca, dcb, dconv_w, dconv_vec = _convmod_bwd(p1, dyc, sf["conv_w"], sf["conv_b"], sf["conv_ln_g"],
                                                sf["conv_ln_b"], tr)
    dlx, dlg, dwa_bd, dwx_bd, dlru_vec = _lru_bwd(p1, hs, dyd, *lru_args)
    dp1 = [dca, dcb, dlx, dlg]
    dh4, dg_mix_l1 = _proj_bwd_act(dh5, h4, gain("mix_norm", 1), 1, dp1, odd_pieces, tm_small)
    dw_in_odd = jnp.stack(_proj_bwd_w(nm1, dp1, tm))
    dh3, dg_f1_l1, g = ffn_bwd(dh4, h3, s3, gain("ffn1_norm", 1), f1l1)
    sent = emit((("w_out_odd", 0), ("w_in_odd", 0), ("ffn1_wg", 1), ("ffn1_wu", 1), ("ffn1_wd", 1)),
                [dw_out_odd.reshape(N_SHARD, d // N_SHARD, d), dw_in_odd] + list(g))
    dh2, dg_f2_l0, g_f2l0 = ffn_bwd(dh3, h2, s2, gain("ffn2_norm", 0), f2l0, tuple(sent))
    dya, dyb, dw_out_even = _out_bwd(dh2, ya, yb, w_out_even, tm)
    dpool, dpool_wbd, dpool_scale = _pool_bwd(p0, dya, pool_wbd, pool_scale, tr)
    dq, dz, dv, dgate, dlb_logits, dgn_heads = _hgrn_bwd(p0, dyb, states, w["hgrn_lb_logits"], w["hgrn_gnorm"],
                                                         mst, msk, n_lev, tm)
    dp0 = [dpool, dq, dz, dv, dgate]
    dh1, dg_mix_l0 = _proj_bwd_act(dh2, h1, gain("mix_norm", 0), 0, dp0, even_piece, tm_small, wt=True)
    dw_in_even_t = jnp.concatenate(_proj_bwd_w(nm0, dp0, tm_small, wt=True), axis=0)
    ga, gb, sa, n1 = s1
    (dwd_f1l0,) = _ffn_bwd_w([(sa, dh1, 0.5)], tm)
    sent = emit((("ffn2_wg", 0), ("ffn2_wu", 0), ("ffn2_wd", 0), ("w_out_even", 0), ("w_in_even", 0),
                 ("ffn1_wd", 0)),
                list(g_f2l0) + [dw_out_even.reshape(N_SHARD, d // N_SHARD, d),
                                dw_in_even_t.reshape(N_SHARD, D_IN_EVEN // N_SHARD, d), dwd_f1l0])
    dh0, da, db, dg_f1_l0, _ = _ffn_bwd_act(dh1, h0, gain("ffn1_norm", 0), ga, gb, *f1l0, 0, tm, tuple(sent))

    grad_x = dh0[N_META:t_real]
    rep = {
        "ffn1_norm": jnp.concatenate([dg_f1_l0, dg_f1_l1], axis=0),
        "mix_norm": jnp.concatenate([dg_mix_l0, dg_mix_l1], axis=0),
        "ffn2_norm": jnp.concatenate([dg_f2_l0, dg_f2_l1], axis=0),
        "final_norm": dg_final,
        "pool_w": _diag_blocks(dpool_wbd, len(POOL_WINDOWS)),
        "pool_scale": dpool_scale,
        "hgrn_lb_logits": dlb_logits,
        "hgrn_gnorm": jnp.sum(dgn_heads, axis=0),
        "lru_wa": _diag_blocks2(dwa_bd),
        "lru_wx": _diag_blocks2(dwx_bd),
    }
    dmeta = jnp.transpose(dh0[:N_META].reshape(N_META, N_SHARD, 2, LANE), (1, 0, 2, 3)).reshape(N_SHARD, 32, LANE)
    packs = [_pack_rows(rep, REP_SPEC)]
    for s in range(N_SHARD):
        sh = {
            "meta_tokens": dmeta[s], "conv_w": dconv_w[s], "lru_conv_w": dlru_vec[s, 0:4],
            "conv_b": dconv_vec[s, 0:1], "conv_ln_g": dconv_vec[s, 1:2], "conv_ln_b": dconv_vec[s, 2:3],
            "lru_conv_b": dlru_vec[s, 4:5], "lru_ba": dlru_vec[s, 5:6], "lru_bx": dlru_vec[s, 6:7],
            "lru_lambda": dlru_vec[s, 7:8],
        }
        packs.append(_pack_rows(sh, SH_SPEC))
    packs.append(jnp.pad(loss, ((0, LOSS_ROWS - 1), (0, LANE - 1))))
    sent = emit_small(jnp.concatenate(packs, axis=0))
    emit((("ffn1_wg", 0), ("ffn1_wu", 0)), _ffn_bwd_w([(da, n1, None), (db, n1, None)], tm, tuple(sent)))
    return grad_x


def kernel(x, meta_tokens, ffn1_norm, ffn1_wg, ffn1_wu, ffn1_wd, mix_norm, ffn2_norm, ffn2_wg, ffn2_wu, ffn2_wd, w_in_even, pool_w, pool_scale, hgrn_lb_logits, hgrn_gnorm, w_out_even, w_in_odd, conv_w, conv_b, conv_ln_g, conv_ln_b, lru_conv_w, lru_conv_b, lru_wa, lru_ba, lru_wx, lru_bx, lru_lambda, w_out_odd, final_norm, loss_target, m_meta_tokens, m_ffn1_norm, m_ffn1_wg, m_ffn1_wu, m_ffn1_wd, m_mix_norm, m_ffn2_norm, m_ffn2_wg, m_ffn2_wu, m_ffn2_wd, m_w_in_even, m_pool_w, m_pool_scale, m_hgrn_lb_logits, m_hgrn_gnorm, m_w_out_even, m_w_in_odd, m_conv_w, m_conv_b, m_conv_ln_g, m_conv_ln_b, m_lru_conv_w, m_lru_conv_b, m_lru_wa, m_lru_ba, m_lru_wx, m_lru_bx, m_lru_lambda, m_w_out_odd, m_final_norm, v_meta_tokens, v_ffn1_norm, v_ffn1_wg, v_ffn1_wu, v_ffn1_wd, v_mix_norm, v_ffn2_norm, v_ffn2_wg, v_ffn2_wu, v_ffn2_wd, v_w_in_even, v_pool_w, v_pool_scale, v_hgrn_lb_logits, v_hgrn_gnorm, v_w_out_even, v_w_in_odd, v_conv_w, v_conv_b, v_conv_ln_g, v_conv_ln_b, v_lru_conv_w, v_lru_conv_b, v_lru_wa, v_lru_ba, v_lru_wx, v_lru_bx, v_lru_lambda, v_w_out_odd, v_final_norm):
    args = locals()
    w = {n: args[n] for n in WEIGHT_NAMES}
    m = {n: args["m_" + n] for n in WEIGHT_NAMES}
    v = {n: args["v_" + n] for n in WEIGHT_NAMES}
    shapes = {n: w[n].shape for n in WEIGHT_NAMES}
    core = lax.axis_index("c").astype(jnp.int32).reshape(1)
    chip = (2 * lax.axis_index("x") + lax.axis_index("y")).astype(jnp.int32).reshape(1)
    me = 2 * chip + core

    def view(a, n):
        return jnp.swapaxes(a, 1, 2) if n in TRANSPOSED else a

    wv, mv, vv = [{n: view(src[n], n) for n in BIG} for src in (w, m, v)]

    def shard(key):
        n, l = key
        return _pack_rows(w, SH_SPEC) if n == "small" else wv[n][l].astype(BF16)

    started = {}
    for groups in (GATHER_GROUPS[:2], GATHER_GROUPS[2:]):
        gkeys = [key for grp in groups for key in grp]
        ssem, rsem, srcs, lands, token = _gather_start([shard(key) for key in gkeys],
                                                       [any(key in GATHER_GROUPS[g] for g in GATHER_SPLIT)
                                                        for key in gkeys])
        for k, key in enumerate(gkeys):
            started[key] = (ssem, rsem, srcs[k], lands[k], k, token)

    def pack_small(src):
        return jnp.concatenate([_pack_rows(src, REP_SPEC), _pack_rows(src, SH_SPEC)], axis=0)

    small_packs = [pack_small(src) for src in (w, m, v)]

    def fetch(group, after):
        st = [started[key] for key in GATHER_GROUPS[group]]
        deps = (st[0][5],) if after is None else (after,)
        if group == 1:
            deps += (started[GATHER_GROUPS[2][0]][5],) + tuple(small_packs)
        split = group in GATHER_SPLIT
        got = _gather_wait(st[0][0], st[0][1], [s[2] for s in st], [s[3] for s in st], [s[4] for s in st], deps,
                           split)
        return _pair_forward(got) if split else got

    in_flight, reduced = [], {}

    def collect(entry, after):
        gkeys, gs_sem, gr_sem, grads_thru, slots_thru, _ = entry
        slots = _reduce_wait(gs_sem, gr_sem, grads_thru, slots_thru, after)
        full = _pair_allgather_halves(_sum_devices(slots, core))
        reduced.update(zip(gkeys, full))
        return full[0]

    def emit(gkeys, grads):
        grads = list(grads)
        in_flight.append((gkeys,) + tuple(_reduce_start(grads, _own_part(grads, chip, core, me))))
        token = in_flight[-1][-1]
        if len(in_flight) > SCATTER_DEPTH:
            return token, collect(in_flight[-1 - SCATTER_DEPTH], (token,))
        return (token,)

    small_flight = []

    def emit_small(part):
        small_flight.append(_small_start(part, _small_own(part, me)))
        return (small_flight[0][4],)

    grad_x = _local_step(x[0], loss_target[0], w, shapes, fetch, emit, emit_small)

    s_ssem, s_rsem, s_part, s_slots, _ = small_flight[0]
    small_all = _small_wait(s_ssem, s_rsem, s_part, s_slots, (in_flight[-1][-1],))
    small_res = _small_reduce_adamw(small_all.reshape(-1, LANE), *small_packs, REP_ROWS, SH_ROWS)
    out_g, out_d, out_m, out_v = {}, {}, {}, {}
    deps = (small_res[0],)
    for entry in in_flight[-SCATTER_DEPTH:]:
        collect(entry, deps)
        for n in BIG:
            layers = range(shapes[n][0])
            if n not in out_g and all((n, l) in reduced for l in layers):
                res = _adamw(wv[n], mv[n], vv[n], [reduced[(n, l)] for l in layers], ADAM_ROW_BLOCKS[n])
                out_g[n], out_d[n], out_m[n], out_v[n] = [view(r, n) for r in res]
                deps += (res[1],)

    loss = small_res[4][0, 0]
    for dst, packed in zip((out_g, out_d, out_m, out_v), small_res[:4]):
        dst.update(_unpack_rows(packed[:REP_ROWS], REP_SPEC, shapes))
        dst.update(_unpack_rows(packed[REP_ROWS:], SH_SPEC, shapes))

    return (loss, grad_x[None], *[out_g[n] for n in WEIGHT_NAMES], *[out_d[n] for n in WEIGHT_NAMES],
            *[out_m[n] for n in WEIGHT_NAMES], *[out_v[n] for n in WEIGHT_NAMES])
```

```python
import functools

import numpy as np
import jax
import jax.numpy as jnp
from jax import lax
from jax.experimental import pallas as pl
from jax.experimental.pallas import tpu as pltpu

F32 = jnp.float32
BF16 = jnp.bfloat16
MESH = pl.DeviceIdType.MESH

EPS = 1e-6
N_META = 16
D_MODEL = 1024
D_FF = 2816
N_SHARD = 4
FF_SHARD = D_FF // N_SHARD
D_POOL = 256
POOL_GROUP = 64
POOL_WINDOWS = (2, 4, 8, 16)
D_HGRN = 768
HG_HEADS = 6
HEAD = 128
HG_CHUNK = 64
HG_HEADS_PER_STEP = 6
HG_PBLOCK = 256
D_IN_EVEN = D_POOL + 4 * D_HGRN
D_CONV = 512
CONV_WIDTH = 31
CONV_HALO = 32
D_LRU = 512
LRU_CONV = 4
LRU_HALO = 8
LRU_C = 8.0
D_IN_ODD = 2 * D_CONV + 2 * D_LRU
SUBLANE = 8
ROW_ALIGN = 64

ADAM_LR = 0.001
ADAM_B1 = 0.9
ADAM_B2 = 0.999
ADAM_EPS = 1e-08
ADAM_WD = 0.01
ADAM_STEP = 10

VMEM_LIMIT_MB = 56


def _cparams(n_grid_axes=0, vmem_mb=VMEM_LIMIT_MB):
    sem = ("arbitrary",) * n_grid_axes if n_grid_axes else None
    return pltpu.CompilerParams(dimension_semantics=sem, vmem_limit_bytes=vmem_mb * 1024 * 1024)


def _tile(n, target, mult):
    best = None
    for t in range(mult, min(n, target) + 1, mult):
        if n % t == 0:
            best = t
    assert best is not None, (n, target, mult)
    return best


def _dot(a, b):
    return jnp.dot(a, b, preferred_element_type=F32)


def _dot_nt(a, b):
    return lax.dot_general(a, b, (((1,), (1,)), ((), ())), preferred_element_type=F32)


def _dot_tn(a, b):
    return lax.dot_general(a, b, (((0,), (0,)), ((), ())), preferred_element_type=F32)


def _sigmoid(x):
    return 1.0 / (1.0 + jnp.exp(-x))


def _colsum(x):
    return jnp.sum(x, axis=0, keepdims=True)


def _rms_stats(h):
    rstd = lax.rsqrt(jnp.mean(h * h, axis=-1, keepdims=True) + EPS)
    return rstd, h * rstd


def _rms_bwd(dn, g, rstd, xhat):
    dng = dn * g
    dh = rstd * (dng - xhat * jnp.mean(dng * xhat, axis=-1, keepdims=True))
    return dh, _colsum(dn * xhat)


def _ffn_fwd(h, norm, wg4, wu4, wd4, layer, tm):
    tp, d = h.shape
    nt = tp // tm

    def body(h_ref, g_ref, wg_ref, wu_ref, wd_ref, ho_ref, ga_ref, gb_ref, sa_ref, n_ref, n_sc, acc):
        s = pl.program_id(1)

        @pl.when(s == 0)
        def _():
            hh = h_ref[...]
            rstd, xhat = _rms_stats(hh)
            n = (xhat * g_ref[...]).astype(BF16)
            n_sc[...] = n
            n_ref[...] = n
            acc[...] = jnp.zeros_like(acc)

        n = n_sc[...]
        a = _dot_nt(n, wg_ref[0])
        b = _dot_nt(n, wu_ref[0])
        sig = _sigmoid(a)
        sil = a * sig
        ga_ref[0] = (sig * (1.0 + a * (1.0 - sig)) * b).astype(BF16)
        gb_ref[0] = sil.astype(BF16)
        sg = (sil * b).astype(BF16)
        sa_ref[0] = sg
        acc[...] += _dot(sg, wd_ref[0])

        @pl.when(s == N_SHARD - 1)
        def _():
            ho_ref[...] = h_ref[...] + 0.5 * acc[...]

    return pl.pallas_call(
        body, name="ffn_fwd",
        grid=(nt, N_SHARD),
        in_specs=[
            pl.BlockSpec((tm, d), lambda i, s: (i, 0)),
            pl.BlockSpec((1, d), lambda i, s: (0, 0)),
            pl.BlockSpec((1, FF_SHARD, d), lambda i, s: (s, layer, 0)),
            pl.BlockSpec((1, FF_SHARD, d), lambda i, s: (s, layer, 0)),
            pl.BlockSpec((1, FF_SHARD, d), lambda i, s: (s, layer, 0)),
        ],
        out_specs=[
            pl.BlockSpec((tm, d), lambda i, s: (i, 0)),
            pl.BlockSpec((1, tm, FF_SHARD), lambda i, s: (s, i, 0)),
            pl.BlockSpec((1, tm, FF_SHARD), lambda i, s: (s, i, 0)),
            pl.BlockSpec((1, tm, FF_SHARD), lambda i, s: (s, i, 0)),
            pl.BlockSpec((tm, d), lambda i, s: (i, 0)),
        ],
        out_shape=[
            jax.ShapeDtypeStruct((tp, d), F32),
            jax.ShapeDtypeStruct((N_SHARD, tp, FF_SHARD), BF16),
            jax.ShapeDtypeStruct((N_SHARD, tp, FF_SHARD), BF16),
            jax.ShapeDtypeStruct((N_SHARD, tp, FF_SHARD), BF16),
            jax.ShapeDtypeStruct((tp, d), BF16),
        ],
        scratch_shapes=[pltpu.VMEM((tm, d), BF16), pltpu.VMEM((tm, d), F32)],
        compiler_params=_cparams(2),
    )(h, norm, wg4, wu4, wd4)


def _ffn_bwd_act(dho, h, norm, ga4, gb4, wg4, wu4, wd4, layer, tm, after=()):
    tp, d = h.shape
    nt = tp // tm

    def body(dho_ref, h_ref, g_ref, ga_ref, gb_ref, wg_ref, wu_ref, wd_ref, *rest):
        dh_ref, da_ref, db_ref, dg_ref, dy_ref, dn_sc = rest[len(after):]
        i = pl.program_id(0)
        s = pl.program_id(1)

        @pl.when(s == 0)
        def _():
            dy_ref[...] = (0.5 * dho_ref[...]).astype(BF16)
            dn_sc[...] = jnp.zeros_like(dn_sc)

        @pl.when((s == 0) & (i == 0))
        def _():
            dg_ref[...] = jnp.zeros_like(dg_ref)

        ds = _dot_nt(dy_ref[...], wd_ref[0])
        da = (ds * ga_ref[0].astype(F32)).astype(BF16)
        db = (ds * gb_ref[0].astype(F32)).astype(BF16)
        da_ref[0] = da
        db_ref[0] = db
        dn_sc[...] += _dot(da, wg_ref[0]) + _dot(db, wu_ref[0])

        @pl.when(s == N_SHARD - 1)
        def _():
            rstd, xhat = _rms_stats(h_ref[...])
            dh, dg = _rms_bwd(dn_sc[...], g_ref[...], rstd, xhat)
            dh_ref[...] = dho_ref[...] + dh
            dg_ref[...] += dg

    return pl.pallas_call(
        body, name="ffn_bwd_act",
        grid=(nt, N_SHARD),
        in_specs=[
            pl.BlockSpec((tm, d), lambda i, s: (i, 0)),
            pl.BlockSpec((tm, d), lambda i, s: (i, 0)),
            pl.BlockSpec((1, d), lambda i, s: (0, 0)),
            pl.BlockSpec((1, tm, FF_SHARD), lambda i, s: (s, i, 0)),
            pl.BlockSpec((1, tm, FF_SHARD), lambda i, s: (s, i, 0)),
            pl.BlockSpec((1, FF_SHARD, d), lambda i, s: (s, layer, 0)),
            pl.BlockSpec((1, FF_SHARD, d), lambda i, s: (s, layer, 0)),
            pl.BlockSpec((1, FF_SHARD, d), lambda i, s: (s, layer, 0)),
        ] + [pl.BlockSpec(memory_space=pl.ANY)] * len(after),
        out_specs=[
            pl.BlockSpec((tm, d), lambda i, s: (i, 0)),
            pl.BlockSpec((1, tm, FF_SHARD), lambda i, s: (s, i, 0)),
            pl.BlockSpec((1, tm, FF_SHARD), lambda i, s: (s, i, 0)),
            pl.BlockSpec((1, d), lambda i, s: (0, 0)),
            pl.BlockSpec((tm, d), lambda i, s: (i, 0)),
        ],
        out_shape=[
            jax.ShapeDtypeStruct((tp, d), F32),
            jax.ShapeDtypeStruct((N_SHARD, tp, FF_SHARD), BF16),
            jax.ShapeDtypeStruct((N_SHARD, tp, FF_SHARD), BF16),
            jax.ShapeDtypeStruct((1, d), F32),
            jax.ShapeDtypeStruct((tp, d), BF16),
        ],
        scratch_shapes=[pltpu.VMEM((tm, d), F32)],
        compiler_params=_cparams(2),
    )(dho, h, norm, ga4, gb4, wg4, wu4, wd4, *after)


def _ffn_bwd_w(pairs, tm, after=()):
    npair = len(pairs)
    tp, d = pairs[0][1].shape
    nt = tp // tm
    rhs_list = []
    for _, rhs, _ in pairs:
        if all(rhs is not r for r in rhs_list):
            rhs_list.append(rhs)
    rhs_of = [[rhs is r for r in rhs_list].index(True) for _, rhs, _ in pairs]
    nrhs = len(rhs_list)

    def body(*refs):
        rhs_refs = refs[:nrhs]
        lhs_refs = refs[nrhs:nrhs + npair]
        rest = refs[nrhs + npair + len(after):]
        out_refs, accs = rest[:npair], rest[npair:]
        i = pl.program_id(1)

        @pl.when(i == 0)
        def _():
            for acc in accs:
                acc[...] = jnp.zeros_like(acc)

        for k, (_, _, scale) in enumerate(pairs):
            rhs = rhs_refs[rhs_of[k]][...]
            if scale is not None:
                rhs = (scale * rhs).astype(BF16)
            accs[k][...] += _dot_tn(lhs_refs[k][0], rhs)

        @pl.when(i == nt - 1)
        def _():
            for k in range(npair):
                out_refs[k][0] = accs[k][...].astype(BF16)

    return pl.pallas_call(
        body, name="ffn_bwd_w",
        grid=(N_SHARD, nt),
        in_specs=[pl.BlockSpec((tm, d), lambda s, i: (i, 0))] * nrhs
        + [pl.BlockSpec((1, tm, FF_SHARD), lambda s, i: (s, i, 0))] * npair
        + [pl.BlockSpec(memory_space=pl.ANY)] * len(after),
        out_specs=[pl.BlockSpec((1, FF_SHARD, d), lambda s, i: (s, 0, 0))] * npair,
        out_shape=[jax.ShapeDtypeStruct((N_SHARD, FF_SHARD, d), BF16)] * npair,
        scratch_shapes=[pltpu.VMEM((FF_SHARD, d), F32)] * npair,
        compiler_params=_cparams(2),
    )(*rhs_list, *[lhs for lhs, _, _ in pairs], *after)


def _proj_fwd(h, norm, layer, w_pieces, tm, wt=False):
    tp, d = h.shape
    widths = [bs[-2] if wt else bs[-1] for _, bs, _ in w_pieces]
    ntot = sum(widths)
    npc = len(w_pieces)

    def body(*refs):
        h_ref, g_ref = refs[:2]
        w_refs = refs[2:2 + npc]
        p_ref, n_ref = refs[2 + npc:]
        rstd, xhat = _rms_stats(h_ref[...])
        n = (xhat * g_ref[...]).astype(BF16)
        n_ref[...] = n
        off = 0
        for k in range(npc):
            w = w_refs[k][...]
            w = w.reshape(w.shape[-2], w.shape[-1])
            p_ref[:, off:off + widths[k]] = _dot_nt(n, w) if wt else _dot(n, w)
            off += widths[k]

    in_specs = [pl.BlockSpec((tm, d), lambda i: (i, 0)), pl.BlockSpec((1, d), lambda i: (0, 0))]
    for _, bs, idx in w_pieces:
        in_specs.append(pl.BlockSpec(bs, functools.partial(lambda i, idx: idx, idx=idx)))
    return pl.pallas_call(
        body, name="proj_fwd",
        grid=(tp // tm,),
        in_specs=in_specs,
        out_specs=[pl.BlockSpec((tm, ntot), lambda i: (i, 0)), pl.BlockSpec((tm, d), lambda i: (i, 0))],
        out_shape=[jax.ShapeDtypeStruct((tp, ntot), F32), jax.ShapeDtypeStruct((tp, d), BF16)],
        compiler_params=_cparams(1),
    )(h, norm, *[w for w, _, _ in w_pieces])


def _proj_bwd_act(dres, h, norm, layer, dp_pieces, w_pieces, tm, wt=False):
    tp, d = h.shape
    npc = len(dp_pieces)
    nw = len(w_pieces)
    assert nw == npc or (nw == 1 and wt)

    def body(*refs):
        dres_ref, h_ref, g_ref = refs[:3]
        dp_refs = refs[3:3 + npc]
        w_refs = refs[3 + npc:3 + npc + nw]
        dh_ref, dg_ref = refs[3 + npc + nw:]
        i = pl.program_id(0)

        @pl.when(i == 0)
        def _():
            dg_ref[...] = jnp.zeros_like(dg_ref)

        dn = None
        off = 0
        for k in range(npc):
            if nw == npc:
                w = w_refs[k][...]
                w = w.reshape(w.shape[-2], w.shape[-1])
            else:
                w = w_refs[0][off:off + dp_pieces[k].shape[1], :]
                off += dp_pieces[k].shape[1]
            t = _dot(dp_refs[k][...], w) if wt else _dot_nt(dp_refs[k][...], w)
            dn = t if dn is None else dn + t
        rstd, xhat = _rms_stats(h_ref[...])
        dh, dg = _rms_bwd(dn, g_ref[...], rstd, xhat)
        dh_ref[...] = dres_ref[...] + dh
        dg_ref[...] += dg

    in_specs = [pl.BlockSpec((tm, d), lambda i: (i, 0)), pl.BlockSpec((tm, d), lambda i: (i, 0)),
                pl.BlockSpec((1, d), lambda i: (0, 0))]
    for dp in dp_pieces:
        in_specs.append(pl.BlockSpec((tm, dp.shape[1]), lambda i: (i, 0)))
    for _, bs, idx in w_pieces:
        in_specs.append(pl.BlockSpec(bs, functools.partial(lambda i, idx: idx, idx=idx)))
    return pl.pallas_call(
        body, name="proj_bwd_act",
        grid=(tp // tm,),
        in_specs=in_specs,
        out_specs=[pl.BlockSpec((tm, d), lambda i: (i, 0)), pl.BlockSpec((1, d), lambda i: (0, 0))],
        out_shape=[jax.ShapeDtypeStruct((tp, d), F32), jax.ShapeDtypeStruct((1, d), F32)],
        compiler_params=_cparams(1),
    )(dres, h, norm, *dp_pieces, *[w for w, _, _ in w_pieces])


def _proj_bwd_w(n, dp_pieces, tm, wt=False):
    tp, d = n.shape
    npc = len(dp_pieces)
    widths = [dp.shape[1] for dp in dp_pieces]
    oshape = (lambda w: (w, d)) if wt else (lambda w: (d, w))

    def body(*refs):
        n_ref = refs[0]
        dp_refs = refs[1:1 + npc]
        o_refs = refs[1 + npc:1 + 2 * npc]
        accs = refs[1 + 2 * npc:]
        i = pl.program_id(0)

        @pl.when(i == 0)
        def _():
            for acc in accs:
                acc[...] = jnp.zeros_like(acc)

        nn = n_ref[...]
        for k in range(npc):
            accs[k][...] += _dot_tn(dp_refs[k][...], nn) if wt else _dot_tn(nn, dp_refs[k][...])

        @pl.when(i == pl.num_programs(0) - 1)
        def _():
            for k in range(npc):
                o_refs[k][...] = accs[k][...].astype(BF16)

    return pl.pallas_call(
        body, name="proj_bwd_w",
        grid=(tp // tm,),
        in_specs=[pl.BlockSpec((tm, d), lambda i: (i, 0))]
        + [pl.BlockSpec((tm, w), lambda i: (i, 0)) for w in widths],
        out_specs=[pl.BlockSpec(oshape(w), lambda i: (0, 0)) for w in widths],
        out_shape=[jax.ShapeDtypeStruct(oshape(w), BF16) for w in widths],
        scratch_shapes=[pltpu.VMEM(oshape(w), F32) for w in widths],
        compiler_params=_cparams(1),
    )(n, *dp_pieces)


def _out_fwd(h, ya, yb, w, tm):
    tp, d = h.shape
    na, nb = ya.shape[1], yb.shape[1]

    def body(h_ref, ya_ref, yb_ref, w_ref, o_ref):
        y = _dot(ya_ref[...].astype(BF16), w_ref[0:na, :]) + _dot(yb_ref[...].astype(BF16), w_ref[na:, :])
        o_ref[...] = h_ref[...] + y

    return pl.pallas_call(
        body, name="out_fwd",
        grid=(tp // tm,),
        in_specs=[pl.BlockSpec((tm, d), lambda i: (i, 0)), pl.BlockSpec((tm, na), lambda i: (i, 0)),
                  pl.BlockSpec((tm, nb), lambda i: (i, 0)), pl.BlockSpec((d, d), lambda i: (0, 0))],
        out_specs=pl.BlockSpec((tm, d), lambda i: (i, 0)),
        out_shape=jax.ShapeDtypeStruct((tp, d), F32),
        compiler_params=_cparams(1),
    )(h, ya, yb, w)


def _out_bwd(dy, ya, yb, w, tm, after=()):
    tp, d = dy.shape
    na, nb = ya.shape[1], yb.shape[1]

    def body(dy_ref, ya_ref, yb_ref, w_ref, *rest):
        da_ref, db_ref, dw_ref, acc = rest[len(after):]
        i = pl.program_id(0)

        @pl.when(i == 0)
        def _():
            acc[...] = jnp.zeros_like(acc)

        dyb16 = dy_ref[...].astype(BF16)
        da_ref[...] = _dot_nt(dyb16, w_ref[0:na, :])
        db_ref[...] = _dot_nt(dyb16, w_ref[na:, :])
        acc[0:na, :] += _dot_tn(ya_ref[...].astype(BF16), dyb16)
        acc[na:, :] += _dot_tn(yb_ref[...].astype(BF16), dyb16)

        @pl.when(i == pl.num_programs(0) - 1)
        def _():
            dw_ref[...] = acc[...].astype(BF16)

    return pl.pallas_call(
        body, name="out_bwd",
        grid=(tp // tm,),
        in_specs=[pl.BlockSpec((tm, d), lambda i: (i, 0)), pl.BlockSpec((tm, na), lambda i: (i, 0)),
                  pl.BlockSpec((tm, nb), lambda i: (i, 0)), pl.BlockSpec((d, d), lambda i: (0, 0))]
        + [pl.BlockSpec(memory_space=pl.ANY)] * len(after),
        out_specs=[pl.BlockSpec((tm, na), lambda i: (i, 0)), pl.BlockSpec((tm, nb), lambda i: (i, 0)),
                   pl.BlockSpec((d, d), lambda i: (0, 0))],
        out_shape=[jax.ShapeDtypeStruct((tp, na), F32), jax.ShapeDtypeStruct((tp, nb), F32),
                   jax.ShapeDtypeStruct((d, d), BF16)],
        scratch_shapes=[pltpu.VMEM((d, d), F32)],
        compiler_params=_cparams(1),
    )(dy, ya, yb, w, *after)


def _loss_bwd(h, gfin, tgt, t_real, tm):
    tp, d = h.shape

    def body(h_ref, g_ref, t_ref, loss_ref, dh_ref, dg_ref):
        i = pl.program_id(0)

        @pl.when(i == 0)
        def _():
            loss_ref[...] = jnp.zeros_like(loss_ref)
            dg_ref[...] = jnp.zeros_like(dg_ref)

        rows = i * tm + lax.broadcasted_iota(jnp.int32, (tm, 1), 0)
        valid = (rows >= N_META) & (rows < t_real)
        rstd, xhat = _rms_stats(h_ref[...])
        g = g_ref[...]
        err = jnp.where(valid, xhat * g - t_ref[...], 0.0)
        e2 = jnp.sum(err * err, axis=1, keepdims=True)
        loss_ref[...] += (0.5 / d) * jnp.sum(e2, axis=0, keepdims=True)
        dy = err * (1.0 / d)
        dh, dg = _rms_bwd(dy, g, rstd, xhat)
        dh_ref[...] = dh
        dg_ref[...] += dg

    return pl.pallas_call(
        body, name="loss_bwd",
        grid=(tp // tm,),
        in_specs=[pl.BlockSpec((tm, d), lambda i: (i, 0)), pl.BlockSpec((1, d), lambda i: (0, 0)),
                  pl.BlockSpec((tm, d), lambda i: (i, 0))],
        out_specs=[pl.BlockSpec((1, 1), lambda i: (0, 0)), pl.BlockSpec((tm, d), lambda i: (i, 0)),
                   pl.BlockSpec((1, d), lambda i: (0, 0))],
        out_shape=[jax.ShapeDtypeStruct((1, 1), F32), jax.ShapeDtypeStruct((tp, d), F32),
                   jax.ShapeDtypeStruct((1, d), F32)],
        compiler_params=_cparams(1),
    )(h, gfin, tgt)


POOL_HALO = 16


def _pool_lane_consts(n_rows):
    lane = lax.broadcasted_iota(jnp.int32, (n_rows, D_POOL), 1)
    grp = lane // POOL_GROUP
    win = jnp.where(grp == 0, 2.0, jnp.where(grp == 1, 4.0, jnp.where(grp == 2, 8.0, 16.0)))
    return grp, win


def _pool_select(grp, s2, s4, s8, s16):
    return jnp.where(grp == 0, s2, jnp.where(grp == 1, s4, jnp.where(grp == 2, s8, s16)))


def _pool_mixed(x, row0, tr):
    n = tr + POOL_HALO
    s2 = x + pltpu.roll(x, 1, 0)
    s4 = s2 + pltpu.roll(s2, 2, 0)
    s8 = s4 + pltpu.roll(s4, 4, 0)
    s16 = s8 + pltpu.roll(s8, 8, 0)
    grp, win = _pool_lane_consts(n)
    rows = row0 - POOL_HALO + lax.broadcasted_iota(jnp.int32, (n, D_POOL), 0)
    cnt = jnp.minimum((rows + 1).astype(F32), win)
    pooled = _pool_select(grp, s2, s4, s8, s16) / jnp.maximum(cnt, 1.0)
    return (pooled - x)[POOL_HALO:, :]


def _pool_fwd(p, wbd, scale, tr):
    tp = p.shape[0]
    nt = tp // tr

    def body(p_ref, w_ref, s_ref, y_ref, usc):
        usc[0:POOL_HALO, :] = jnp.zeros((POOL_HALO, D_POOL), F32)
        usc[POOL_HALO:, :] = p_ref[...]

        def tile(r, carry):
            r0 = pl.multiple_of(r * tr, SUBLANE)
            x = usc[pl.ds(r0, tr + POOL_HALO), :]
            mixed = _pool_mixed(x, r0, tr)
            y_ref[pl.ds(r0, tr), :] = _dot(mixed.astype(BF16), w_ref[...]) * s_ref[...]
            return carry

        lax.fori_loop(0, nt, tile, 0)

    return pl.pallas_call(
        body, name="pool_fwd",
        grid=(1,),
        in_specs=[pl.BlockSpec((tp, D_POOL), lambda i: (0, 0)), pl.BlockSpec((D_POOL, D_POOL), lambda i: (0, 0)),
                  pl.BlockSpec((1, D_POOL), lambda i: (0, 0))],
        out_specs=pl.BlockSpec((tp, D_POOL), lambda i: (0, 0)),
        out_shape=jax.ShapeDtypeStruct((tp, D_POOL), F32),
        scratch_shapes=[pltpu.VMEM((tp + POOL_HALO, D_POOL), F32)],
        compiler_params=_cparams(1),
    )(p, wbd, scale)


def _pool_bwd(p, dya, wbd, scale, tr):
    tp = p.shape[0]
    nt = tp // tr

    def body(p_ref, dy_ref, w_ref, s_ref, du_ref, dw_ref, ds_ref, usc, gsc):
        usc[0:POOL_HALO, :] = jnp.zeros((POOL_HALO, D_POOL), F32)
        usc[POOL_HALO:, :] = p_ref[...]
        gsc[tp:, :] = jnp.zeros((POOL_HALO, D_POOL), F32)
        dw_ref[...] = jnp.zeros_like(dw_ref)
        ds_ref[...] = jnp.zeros_like(ds_ref)
        grp, win = _pool_lane_consts(tr)

        def tile1(r, carry):
            r0 = pl.multiple_of(r * tr, SUBLANE)
            x = usc[pl.ds(r0, tr + POOL_HALO), :]
            mixed = _pool_mixed(x, r0, tr).astype(BF16)
            dy = dy_ref[pl.ds(r0, tr), :]
            dys = (dy * s_ref[...]).astype(BF16)
            ypre = _dot(mixed, w_ref[...])
            ds_ref[...] += _colsum(dy * ypre)
            dw_ref[...] += _dot_tn(mixed, dys)
            dmx = _dot_nt(dys, w_ref[...])
            rows = r0 + lax.broadcasted_iota(jnp.int32, (tr, D_POOL), 0)
            cnt = jnp.minimum((rows + 1).astype(F32), win)
            gsc[pl.ds(r0, tr), :] = dmx / cnt
            return carry

        lax.fori_loop(0, nt, tile1, 0)
        n = tr + POOL_HALO
        grp2, win2 = _pool_lane_consts(n)

        def tile2(r, carry):
            r0 = pl.multiple_of(r * tr, SUBLANE)
            g = gsc[pl.ds(r0, n), :]
            s2 = g + pltpu.roll(g, n - 1, 0)
            s4 = s2 + pltpu.roll(s2, n - 2, 0)
            s8 = s4 + pltpu.roll(s4, n - 4, 0)
            s16 = s8 + pltpu.roll(s8, n - 8, 0)
            pooled_t = _pool_select(grp2, s2, s4, s8, s16)
            rows = r0 + lax.broadcasted_iota(jnp.int32, (n, D_POOL), 0)
            cnt = jnp.minimum((rows + 1).astype(F32), win2)
            du = pooled_t - g * cnt
            du_ref[pl.ds(r0, tr), :] = du[0:tr, :].astype(BF16)
            return carry

        lax.fori_loop(0, nt, tile2, 0)

    return pl.pallas_call(
        body, name="pool_bwd",
        grid=(1,),
        in_specs=[pl.BlockSpec((tp, D_POOL), lambda i: (0, 0)), pl.BlockSpec((tp, D_POOL), lambda i: (0, 0)),
                  pl.BlockSpec((D_POOL, D_POOL), lambda i: (0, 0)), pl.BlockSpec((1, D_POOL), lambda i: (0, 0))],
        out_specs=[pl.BlockSpec((tp, D_POOL), lambda i: (0, 0)), pl.BlockSpec((D_POOL, D_POOL), lambda i: (0, 0)),
                   pl.BlockSpec((1, D_POOL), lambda i: (0, 0))],
        out_shape=[jax.ShapeDtypeStruct((tp, D_POOL), BF16), jax.ShapeDtypeStruct((D_POOL, D_POOL), F32),
                   jax.ShapeDtypeStruct((1, D_POOL), F32)],
        scratch_shapes=[pltpu.VMEM((tp + POOL_HALO, D_POOL), F32), pltpu.VMEM((tp + POOL_HALO, D_POOL), F32)],
        compiler_params=_cparams(1),
    )(p, dya, wbd, scale)


def _hgrn_levels(ch):
    levels = []
    w = ch // 2
    while w >= 1:
        levels.append(w)
        w //= 2
    return levels


def _hgrn_consts(ch):
    t = np.arange(ch)
    tril = t[None, :] <= t[:, None]
    masks = []
    for w in _hgrn_levels(ch):
        blk = t // (2 * w)
        upper = t % (2 * w) >= w
        masks.append(upper[:, None] & (~upper)[None, :] & (blk[:, None] == blk[None, :]))
    masks.append(tril)
    msk = np.stack(masks).astype(np.float32)
    return jnp.asarray(tril.astype(np.float32), BF16), jnp.asarray(msk, F32), len(masks) - 1


def _split3(x):
    hi = x.astype(BF16)
    r1 = x - hi.astype(F32)
    mid = r1.astype(BF16)
    lo = (r1 - mid.astype(F32)).astype(BF16)
    return hi, mid, lo


def _hgrn_exponents(tril, logf):
    ch = logf.shape[0]
    hi, mid, lo = _split3(logf)
    x = _dot(tril, jnp.concatenate([hi, mid, lo], axis=1))
    b = x[:, 0:HEAD] + x[:, HEAD:2 * HEAD] + x[:, 2 * HEAD:3 * HEAD]
    rows = lax.broadcasted_iota(jnp.int32, (ch, HEAD), 0)
    fx = jnp.broadcast_to(b[ch - 1:ch, :], (ch, HEAD)) - b
    lev = []
    for w in _hgrn_levels(ch):
        pos = rows % (2 * w)
        upper = pos >= w
        if w >= SUBLANE:
            parts = [jnp.broadcast_to(b[k * 2 * w + w - 1:k * 2 * w + w, :], (2 * w, HEAD))
                     for k in range(ch // (2 * w))]
            bmid = parts[0] if len(parts) == 1 else jnp.concatenate(parts, axis=0)
            dx = jnp.where(upper, b - bmid, 0.0)
            ex = jnp.where(upper, 0.0, bmid - b)
        else:
            dx = logf
            ex = jnp.zeros_like(logf)
            for i in range(1, w):
                dx = dx + jnp.where(pos >= w + i, pltpu.roll(logf, i, 0), 0.0)
                ex = ex + jnp.where(pos <= w - 1 - i, pltpu.roll(logf, ch - i, 0), 0.0)
            dx = jnp.where(upper, dx, 0.0)
        lev.append((dx, ex))
    return b, fx, lev


def _hgrn_exponents_bwd(tril, d_b, d_fx, d_blast, lev_grads):
    ch = d_b.shape[0]
    rows = lax.broadcasted_iota(jnp.int32, (ch, HEAD), 0)
    db = d_b - d_fx
    dlf = jnp.zeros_like(d_b)
    for w, (ddx, dex) in zip(_hgrn_levels(ch), lev_grads):
        pos = rows % (2 * w)
        upper = pos >= w
        gu = jnp.where(upper, ddx, 0.0)
        if w >= SUBLANE:
            gl = jnp.where(upper, 0.0, dex)
            db = db + gu - gl
            diff = gl - gu
            for k in range(ch // (2 * w)):
                s = _colsum(diff[k * 2 * w:(k + 1) * 2 * w, :])
                db = db + jnp.where(rows == k * 2 * w + w - 1, s, 0.0)
        else:
            dlf = dlf + gu
            for i in range(1, w):
                dlf = dlf + pltpu.roll(jnp.where(pos >= w + i, gu, 0.0), ch - i, 0)
                dlf = dlf + pltpu.roll(jnp.where(pos <= w - 1 - i, dex, 0.0), i, 0)
    db = db + jnp.where(rows == ch - 1, _colsum(d_fx) + d_blast, 0.0)
    hi = db.astype(BF16)
    lo = (db - hi.astype(F32)).astype(BF16)
    d2 = _dot_tn(tril, jnp.concatenate([hi, lo], axis=1))
    return d2[:, 0:HEAD] + d2[:, HEAD:2 * HEAD] + dlf


def _lockstep(gens):
    results = [None] * len(gens)
    live = list(range(len(gens)))
    while live:
        for i in list(live):
            try:
                next(gens[i])
            except StopIteration as stop:
                results[i] = stop.value
                live.remove(i)
    return results


def _hgrn_gates(q_raw, z, lb):
    sz = _sigmoid(z)
    f = lb + (1.0 - lb) * sz
    q = q_raw * _sigmoid(q_raw)
    k = (1.0 - lb) * (1.0 - sz)
    return q, k, f, sz


def _hgrn_intra(q, k, lev, msk_ref, n_lev, ch):
    eye = (lax.broadcasted_iota(jnp.int32, (ch, ch), 0) == lax.broadcasted_iota(jnp.int32, (ch, ch), 1))
    a = jnp.where(eye, jnp.sum(q * k, axis=1, keepdims=True), 0.0)
    ops = []
    for lv in range(n_lev):
        eq = jnp.exp(lev[lv][0])
        ek = jnp.exp(lev[lv][1])
        qd = q * eq
        kd = k * ek
        a = a + msk_ref[lv] * _dot_nt(qd.astype(BF16), kd.astype(BF16))
        ops.append((eq, ek, qd, kd))
        yield
    return a, ops


def _hgrn_fwd(p, lb_logits, gnorm, mst, msk, n_lev, tm):
    tp = p.shape[0]
    ch = HG_CHUNK
    nct = tm // ch
    nt = tp // tm
    nr = mst.shape[0]
    base = D_POOL // HEAD

    hp = HG_HEADS_PER_STEP
    wide = hp * HEAD
    npr = wide // HG_PBLOCK

    def body(*refs):
        p_refs = refs[:4 * npr]
        lg_ref, gn_ref, mst_ref, msk_ref, y_ref, ss_ref, st_sc = refs[4 * npr:]

        @pl.when(pl.program_id(1) == 0)
        def _():
            st_sc[...] = jnp.zeros_like(st_sc)

        lb_all = _sigmoid(lg_ref[0:1, :] - lg_ref[1:2, :])

        def raw(seg, hh, r0):
            per = HG_PBLOCK // HEAD
            return p_refs[seg * npr + hh // per][pl.ds(r0, ch), (hh % per) * HEAD:(hh % per + 1) * HEAD]

        def one_head(hh, c, r0):
            ls = slice(hh * HEAD, (hh + 1) * HEAD)
            q_raw, z, v, g_raw, st = raw(0, hh, r0), raw(1, hh, r0), raw(2, hh, r0), raw(3, hh, r0), st_sc[hh]
            q, k, f, _ = _hgrn_gates(q_raw, z, lb_all[:, ls])
            yield
            b, fx, lev = _hgrn_exponents(mst_ref[...], jnp.log(f))
            yield
            qe = q * jnp.exp(b)
            a, _ = yield from _hgrn_intra(q, k, lev, msk_ref, n_lev, ch)
            v16 = v.astype(BF16)
            o = _dot_nt(qe.astype(BF16), st.astype(BF16)) + _dot(a.astype(BF16), v16)
            kl = k * jnp.exp(fx)
            st_new = st * jnp.exp(b[ch - 1:ch, :]) + _dot_tn(v16, kl.astype(BF16))
            yield
            rstd = lax.rsqrt(jnp.mean(o * o, axis=-1, keepdims=True) + EPS)
            return st, st_new, o * rstd * gn_ref[...] * (g_raw * _sigmoid(g_raw))

        def chunk(c, carry):
            r0 = pl.multiple_of(c * ch, ch)
            results = _lockstep([one_head(hh, c, r0) for hh in range(hp)])
            for hh, (st, st_new, y) in enumerate(results):
                ss_ref[hh, c] = st
                st_sc[hh] = st_new
                y_ref[pl.ds(r0, ch), hh * HEAD:(hh + 1) * HEAD] = y
            return carry

        lax.fori_loop(0, nct, chunk, 0)

    def pspec(seg, part):
        return pl.BlockSpec((tm, HG_PBLOCK),
                            lambda h, i: (i, (base + seg * HG_HEADS) * HEAD // HG_PBLOCK + h * npr + part))

    return pl.pallas_call(
        body, name="hgrn_fwd",
        grid=(HG_HEADS // hp, nt),
        in_specs=[pspec(seg, part) for seg in range(4) for part in range(npr)]
        + [pl.BlockSpec((2, wide), lambda h, i: (0, h)),
           pl.BlockSpec((1, HEAD), lambda h, i: (0, 0)),
           pl.BlockSpec((nr, ch), lambda h, i: (0, 0)),
           pl.BlockSpec((n_lev + 1, ch, ch), lambda h, i: (0, 0, 0))],
        out_specs=[pl.BlockSpec((tm, wide), lambda h, i: (i, h)),
                   pl.BlockSpec((hp, nct, HEAD, HEAD), lambda h, i: (h, i, 0, 0))],
        out_shape=[jax.ShapeDtypeStruct((tp, D_HGRN), F32),
                   jax.ShapeDtypeStruct((HG_HEADS, tp // ch, HEAD, HEAD), F32)],
        scratch_shapes=[pltpu.VMEM((hp, HEAD, HEAD), F32)],
        compiler_params=_cparams(2),
    )(*([p] * (4 * npr)), lb_logits, gnorm, mst, msk)


def _hgrn_bwd(p, dyb, states, lb_logits, gnorm, mst, msk, n_lev, tm):
    tp = p.shape[0]
    ch = HG_CHUNK
    nct = tm // ch
    nt = tp // tm
    nr = mst.shape[0]
    base = D_POOL // HEAD

    hp = HG_HEADS_PER_STEP
    wide = hp * HEAD
    npr = wide // HG_PBLOCK

    def body(*refs):
        p_refs = refs[:4 * npr]
        (dy_ref, ss_ref, lg_ref, gn_ref, mst_ref, msk_ref,
         dq_ref, dz_ref, dv_ref, dg_ref, dlg_ref, dgn_ref, dst_sc, dlb_sc) = refs[4 * npr:]
        ti = pl.program_id(1)

        def raw(seg, hh, r0):
            per = HG_PBLOCK // HEAD
            return p_refs[seg * npr + hh // per][pl.ds(r0, ch), (hh % per) * HEAD:(hh % per + 1) * HEAD]

        @pl.when(ti == 0)
        def _():
            dst_sc[...] = jnp.zeros_like(dst_sc)
            dlb_sc[...] = jnp.zeros_like(dlb_sc)
            dgn_ref[...] = jnp.zeros_like(dgn_ref)

        lb_all = _sigmoid(lg_ref[0:1, :] - lg_ref[1:2, :])
        gn = gn_ref[...]

        def load_head(hh, c, r0):
            ls = slice(hh * HEAD, (hh + 1) * HEAD)
            return (raw(0, hh, r0), raw(1, hh, r0), raw(2, hh, r0), raw(3, hh, r0),
                    dy_ref[pl.ds(r0, ch), ls], ss_ref[hh, c], dst_sc[hh])

        def store_head(hh, r0, res):
            ls = slice(hh * HEAD, (hh + 1) * HEAD)
            dq_raw, dz, dv, dg_raw, dgn, dst_new, dlb = res
            dq_ref[pl.ds(r0, ch), ls] = dq_raw
            dz_ref[pl.ds(r0, ch), ls] = dz
            dv_ref[pl.ds(r0, ch), ls] = dv
            dg_ref[pl.ds(r0, ch), ls] = dg_raw
            dgn_ref[hh] += dgn
            dst_sc[hh] = dst_new
            dlb_sc[:, ls] += dlb

        def one_head(hh, loaded):
            ls = slice(hh * HEAD, (hh + 1) * HEAD)
            lb = lb_all[:, ls]
            q_raw, z, v, g_raw, dy, st, dst = loaded
            q, k, f, sz = _hgrn_gates(q_raw, z, lb)
            yield
            b, fx, lev = _hgrn_exponents(mst_ref[...], jnp.log(f))
            yield
            eb = jnp.exp(b)
            ef = jnp.exp(fx)
            elast = jnp.exp(b[ch - 1:ch, :])
            qe = q * eb
            kl = k * ef
            a, ops = yield from _hgrn_intra(q, k, lev, msk_ref, n_lev, ch)
            v16 = v.astype(BF16)
            st16 = st.astype(BF16)
            qe16 = qe.astype(BF16)
            kl16 = kl.astype(BF16)
            a16 = a.astype(BF16)
            o = _dot_nt(qe16, st16) + _dot(a16, v16)
            yield
            sg = _sigmoid(g_raw)
            rstd = lax.rsqrt(jnp.mean(o * o, axis=-1, keepdims=True) + EPS)
            oh = o * rstd
            dg_out = (dy * oh * gn * (sg * (1.0 + g_raw * (1.0 - sg)))).astype(BF16)
            don = dy * (g_raw * sg)
            dgn = _colsum(don * oh)
            doh = don * gn
            do = rstd * (doh - oh * jnp.mean(doh * oh, axis=-1, keepdims=True))
            do16 = do.astype(BF16)
            dst16 = dst.astype(BF16)
            yield
            dv = _dot_tn(a16, do16) + _dot_nt(kl16, dst16)
            da = msk_ref[n_lev] * _dot_nt(do16, v16)
            dqe = _dot(do16, st16)
            dkl = _dot(v16, dst16)
            dst_new = dst * elast + _dot_tn(do16, qe16)
            yield
            db_last = _colsum(dst * st) * elast
            dad = jnp.sum(do * v, axis=1, keepdims=True)
            dq = dad * k + dqe * eb
            dk = dad * q + dkl * ef
            lev_grads = []
            for lv in range(n_lev):
                eq, ek, qd, kd = ops[lv]
                gl = (msk_ref[lv] * da).astype(BF16)
                dqd = _dot(gl, kd.astype(BF16))
                dkd = _dot_tn(gl, qd.astype(BF16))
                dq = dq + dqd * eq
                dk = dk + dkd * ek
                lev_grads.append((dqd * qd, dkd * kd))
                yield
            dlogf = _hgrn_exponents_bwd(mst_ref[...], dqe * qe, dkl * kl, db_last, lev_grads)
            yield
            sq = _sigmoid(q_raw)
            dq_out = (dq * (sq * (1.0 + q_raw * (1.0 - sq)))).astype(BF16)
            dfk = dlogf / f - dk
            dz_out = (dfk * (1.0 - lb) * sz * (1.0 - sz)).astype(BF16)
            return dq_out, dz_out, dv.astype(BF16), dg_out, dgn, dst_new, _colsum(dfk * (1.0 - sz))

        def chunk(cc, carry):
            c = nct - 1 - cc
            r0 = pl.multiple_of(c * ch, ch)
            loaded = [load_head(hh, c, r0) for hh in range(HG_HEADS_PER_STEP)]
            results = _lockstep([one_head(hh, loaded[hh]) for hh in range(HG_HEADS_PER_STEP)])
            for hh in range(HG_HEADS_PER_STEP):
                store_head(hh, r0, results[hh])
            return carry

        lax.fori_loop(0, nct, chunk, 0, unroll=1)

        @pl.when(ti == nt - 1)
        def _():
            dl0 = dlb_sc[...] * lb_all * (1.0 - lb_all)
            dlg_ref[0:1, :] = dl0
            dlg_ref[1:2, :] = -dl0

    def pspec(seg, part):
        return pl.BlockSpec((tm, HG_PBLOCK), lambda h, i: (
            nt - 1 - i, (base + seg * HG_HEADS) * HEAD // HG_PBLOCK + h * npr + part))

    ospec = pl.BlockSpec((tm, wide), lambda h, i: (nt - 1 - i, h))
    return pl.pallas_call(
        body, name="hgrn_bwd",
        grid=(HG_HEADS // hp, nt),
        in_specs=[pspec(seg, part) for seg in range(4) for part in range(npr)]
        + [ospec, pl.BlockSpec((hp, nct, HEAD, HEAD), lambda h, i: (h, nt - 1 - i, 0, 0)),
           pl.BlockSpec((2, wide), lambda h, i: (0, h)),
           pl.BlockSpec((1, HEAD), lambda h, i: (0, 0)),
           pl.BlockSpec((nr, ch), lambda h, i: (0, 0)),
           pl.BlockSpec((n_lev + 1, ch, ch), lambda h, i: (0, 0, 0))],
        out_specs=[ospec, ospec, ospec, ospec,
                   pl.BlockSpec((2, wide), lambda h, i: (0, h)),
                   pl.BlockSpec((hp, 1, HEAD), lambda h, i: (h, 0, 0))],
        out_shape=[jax.ShapeDtypeStruct((tp, D_HGRN), BF16)] * 4
        + [jax.ShapeDtypeStruct((2, D_HGRN), F32), jax.ShapeDtypeStruct((HG_HEADS, 1, HEAD), F32)],
        scratch_shapes=[pltpu.VMEM((hp, HEAD, HEAD), F32), pltpu.VMEM((1, wide), F32)],
        compiler_params=_cparams(2),
    )(*([p] * (4 * npr)), dyb, states, lb_logits, gnorm, mst, msk)


def _tap_views(x, tr, halo, width):
    subs = {0: x}
    views = []
    for j in range(width):
        tiles, rem = divmod(width - 1 - j, SUBLANE)
        if rem not in subs:
            subs[rem] = pltpu.roll(x, rem, 0)
        start = halo - tiles * SUBLANE
        views.append(subs[rem][start:start + tr, :])
    return views


def _tap_views_t(y, tr, halo, width):
    n = tr + halo
    subs = {0: y}
    views = []
    for j in range(width):
        tiles, rem = divmod(width - 1 - j, SUBLANE)
        if rem not in subs:
            subs[rem] = pltpu.roll(y, n - rem, 0)
        views.append(subs[rem][tiles * SUBLANE:tiles * SUBLANE + tr, :])
    return views


def _weighted_sum(views, w_ref):
    acc = None
    for j, view in enumerate(views):
        term = view * w_ref[j:j + 1, :]
        acc = term if acc is None else acc + term
    return acc


def _conv_taps(x, w_ref, tr, halo, width):
    return _weighted_sum(_tap_views(x, tr, halo, width), w_ref)


def _conv_taps_t(y, w_ref, tr, halo, width):
    return _weighted_sum(_tap_views_t(y, tr, halo, width), w_ref)


def _ln_stats(cv):
    mu = jnp.mean(cv, axis=-1, keepdims=True)
    xc = cv - mu
    rstd = lax.rsqrt(jnp.mean(xc * xc, axis=-1, keepdims=True) + EPS)
    return rstd, xc * rstd


def _convmod_fwd(p, w, bias, ln_g, ln_b, tr):
    tp = p.shape[0]
    nt = tp // tr
    nb = D_CONV // HEAD

    def body(a_ref, b_ref, w_ref, bi_ref, g_ref, be_ref, y_ref, usc):
        usc[0:CONV_HALO, :] = jnp.zeros((CONV_HALO, HEAD), F32)
        usc[CONV_HALO:, :] = a_ref[...] * _sigmoid(b_ref[...])

        def tile(r, carry):
            r0 = pl.multiple_of(r * tr, SUBLANE)
            x = usc[pl.ds(r0, tr + CONV_HALO), :]
            cv = _conv_taps(x, w_ref, tr, CONV_HALO, CONV_WIDTH) + bi_ref[...]
            _, xh = _ln_stats(cv)
            un = xh * g_ref[...] + be_ref[...]
            y_ref[pl.ds(r0, tr), :] = un * _sigmoid(un)
            return carry

        lax.fori_loop(0, nt, tile, 0)

    vec = lambda: pl.BlockSpec((1, HEAD), lambda j: (0, j))
    return pl.pallas_call(
        body, name="convmod_fwd",
        grid=(nb,),
        in_specs=[pl.BlockSpec((tp, HEAD), lambda j: (0, j)), pl.BlockSpec((tp, HEAD), lambda j: (0, nb + j)),
                  pl.BlockSpec((CONV_HALO, HEAD), lambda j: (0, j)), vec(), vec(), vec()],
        out_specs=pl.BlockSpec((tp, HEAD), lambda j: (0, j)),
        out_shape=jax.ShapeDtypeStruct((tp, D_CONV), F32),
        scratch_shapes=[pltpu.VMEM((tp + CONV_HALO, HEAD), F32)],
        compiler_params=_cparams(1),
    )(p, p, w, bias, ln_g, ln_b)


def _convmod_bwd(p, dyc, w, bias, ln_g, ln_b, tr):
    tp = p.shape[0]
    nt = tp // tr
    nb = D_CONV // HEAD

    def body(a_ref, b_ref, dy_ref, w_ref, bi_ref, g_ref, be_ref, da_ref, db_ref, dw_ref, dv_ref, usc, dsc):
        usc[0:CONV_HALO, :] = jnp.zeros((CONV_HALO, HEAD), F32)
        usc[CONV_HALO:, :] = a_ref[...] * _sigmoid(b_ref[...])
        dsc[tp:, :] = jnp.zeros((CONV_HALO, HEAD), F32)
        dw_ref[...] = jnp.zeros_like(dw_ref)
        dv_ref[...] = jnp.zeros_like(dv_ref)

        def tile1(r, carry):
            r0 = pl.multiple_of(r * tr, SUBLANE)
            x = usc[pl.ds(r0, tr + CONV_HALO), :]
            views = _tap_views(x, tr, CONV_HALO, CONV_WIDTH)
            cv = _weighted_sum(views, w_ref) + bi_ref[...]
            rstd, xh = _ln_stats(cv)
            un = xh * g_ref[...] + be_ref[...]
            sg = _sigmoid(un)
            dun = dy_ref[pl.ds(r0, tr), :] * (sg * (1.0 + un * (1.0 - sg)))
            dv_ref[0, 1:2, :] += _colsum(dun * xh)
            dv_ref[0, 2:3, :] += _colsum(dun)
            dxh = dun * g_ref[...]
            dcv = rstd * (dxh - jnp.mean(dxh, axis=-1, keepdims=True)
                          - xh * jnp.mean(dxh * xh, axis=-1, keepdims=True))
            dv_ref[0, 0:1, :] += _colsum(dcv)
            for j in range(CONV_WIDTH):
                dw_ref[0, j:j + 1, :] += _colsum(dcv * views[j])
            dsc[pl.ds(r0, tr), :] = dcv
            return carry

        lax.fori_loop(0, nt, tile1, 0)

        def tile2(r, carry):
            r0 = pl.multiple_of(r * tr, SUBLANE)
            y = dsc[pl.ds(r0, tr + CONV_HALO), :]
            du = _conv_taps_t(y, w_ref, tr, CONV_HALO, CONV_WIDTH)
            a = a_ref[pl.ds(r0, tr), :]
            sb = _sigmoid(b_ref[pl.ds(r0, tr), :])
            da_ref[pl.ds(r0, tr), :] = (du * sb).astype(BF16)
            db_ref[pl.ds(r0, tr), :] = (du * a * sb * (1.0 - sb)).astype(BF16)
            return carry

        lax.fori_loop(0, nt, tile2, 0)

    vec = lambda: pl.BlockSpec((1, HEAD), lambda j: (0, j))
    col = lambda: pl.BlockSpec((tp, HEAD), lambda j: (0, j))
    return pl.pallas_call(
        body, name="convmod_bwd",
        grid=(nb,),
        in_specs=[col(), pl.BlockSpec((tp, HEAD), lambda j: (0, nb + j)), col(),
                  pl.BlockSpec((CONV_HALO, HEAD), lambda j: (0, j)), vec(), vec(), vec()],
        out_specs=[col(), col(), pl.BlockSpec((1, CONV_HALO, HEAD), lambda j: (j, 0, 0)),
                   pl.BlockSpec((1, SUBLANE, HEAD), lambda j: (j, 0, 0))],
        out_shape=[jax.ShapeDtypeStruct((tp, D_CONV), BF16), jax.ShapeDtypeStruct((tp, D_CONV), BF16),
                   jax.ShapeDtypeStruct((nb, CONV_HALO, HEAD), F32), jax.ShapeDtypeStruct((nb, SUBLANE, HEAD), F32)],
        scratch_shapes=[pltpu.VMEM((tp + CONV_HALO, HEAD), F32), pltpu.VMEM((tp + CONV_HALO, HEAD), F32)],
        compiler_params=_cparams(1),
    )(p, p, dyc, w, bias, ln_g, ln_b)


def _log1p_small(y):
    return jnp.where(y < 1e-4, y * (1.0 - 0.5 * y), jnp.log(1.0 + y))


def _softplus(x):
    return jnp.maximum(x, 0.0) + _log1p_small(jnp.exp(-jnp.abs(x)))


def _expm1(x):
    return jnp.where(jnp.abs(x) < 1e-2, x * (1.0 + 0.5 * x * (1.0 + x * (1.0 / 3.0))), jnp.exp(x) - 1.0)


def _gelu_parts(x):
    c = 0.7978845608028654
    inner = c * (x + 0.044715 * x * x * x)
    th = jnp.tanh(inner)
    gelu = 0.5 * x * (1.0 + th)
    dgelu = 0.5 * (1.0 + th) + 0.5 * x * (1.0 - th * th) * c * (1.0 + 3.0 * 0.044715 * x * x)
    return gelu, dgelu


def _lru_gates(x_all, tp, cw_ref, cb_ref, wa_ref, ba_ref, wx_ref, bx_ref, lam_ref):
    u = _conv_taps(x_all, cw_ref, tp, LRU_HALO, LRU_CONV) + cb_ref[...]
    u16 = u.astype(BF16)
    r = _sigmoid(_dot(u16, wa_ref[0]) + ba_ref[...])
    i = _sigmoid(_dot(u16, wx_ref[0]) + bx_ref[...])
    sp = _softplus(-lam_ref[...])
    la = -LRU_C * r * sp
    a = jnp.exp(la)
    mult = jnp.sqrt(-_expm1(2.0 * la))
    return u, r, i, a, mult, sp


def _lru_specs(tp, nb):
    col = lambda k: pl.BlockSpec((tp, HEAD), functools.partial(lambda j, k: (0, k * nb + j), k=k))
    vec = lambda: pl.BlockSpec((1, HEAD), lambda j: (0, j))
    mat = lambda: pl.BlockSpec((1, HEAD, HEAD), lambda j: (j, 0, 0))
    return col, vec, mat


def _lru_fwd(p, cw, cb, wa, ba, wx, bx, lam):
    tp = p.shape[0]
    nb = D_LRU // HEAD
    ng = tp // SUBLANE

    def body(x_ref, gt_ref, cw_ref, cb_ref, wa_ref, ba_ref, wx_ref, bx_ref, lam_ref, y_ref, hs_ref,
             xsc, asc, bsc):
        xsc[0:LRU_HALO, :] = jnp.zeros((LRU_HALO, HEAD), F32)
        xsc[LRU_HALO:, :] = x_ref[...]
        u, r, i, a, mult, _ = _lru_gates(xsc[...], tp, cw_ref, cb_ref, wa_ref, ba_ref, wx_ref, bx_ref, lam_ref)
        rows = lax.broadcasted_iota(jnp.int32, (tp, HEAD), 0)
        b = jnp.where(rows == 0, 1.0, mult) * (i * u)
        sub = rows % SUBLANE
        for k in (1, 2, 4):
            m = sub >= k
            b = jnp.where(m, a * pltpu.roll(b, k, 0) + b, b)
            a = jnp.where(m, a * pltpu.roll(a, k, 0), a)
        asc[...] = a
        bsc[...] = b

        def grp(g, carry):
            r0 = pl.multiple_of(g * SUBLANE, SUBLANE)
            h = bsc[pl.ds(r0, SUBLANE), :] + asc[pl.ds(r0, SUBLANE), :] * carry
            hs_ref[pl.ds(r0, SUBLANE), :] = h
            return jnp.broadcast_to(h[SUBLANE - 1:SUBLANE, :], (SUBLANE, HEAD))

        lax.fori_loop(0, ng, grp, jnp.zeros((SUBLANE, HEAD), F32))
        gelu, _ = _gelu_parts(gt_ref[...])
        y_ref[...] = gelu * hs_ref[...]

    col, vec, mat = _lru_specs(tp, nb)
    return pl.pallas_call(
        body, name="lru_fwd",
        grid=(nb,),
        in_specs=[col(2), col(3), pl.BlockSpec((LRU_CONV, HEAD), lambda j: (0, j)), vec(), mat(), vec(), mat(),
                  vec(), vec()],
        out_specs=[pl.BlockSpec((tp, HEAD), lambda j: (0, j)), pl.BlockSpec((tp, HEAD), lambda j: (0, j))],
        out_shape=[jax.ShapeDtypeStruct((tp, D_LRU), F32), jax.ShapeDtypeStruct((tp, D_LRU), F32)],
        scratch_shapes=[pltpu.VMEM((tp + LRU_HALO, HEAD), F32), pltpu.VMEM((tp, HEAD), F32),
                        pltpu.VMEM((tp, HEAD), F32)],
        compiler_params=_cparams(1),
    )(p, p, cw, cb, wa, ba, wx, bx, lam)


def _lru_bwd(p, hs, dyd, cw, cb, wa, ba, wx, bx, lam):
    tp = p.shape[0]
    nb = D_LRU // HEAD
    ng = tp // SUBLANE

    def body(x_ref, gt_ref, hs_ref, dy_ref, cw_ref, cb_ref, wa_ref, ba_ref, wx_ref, bx_ref, lam_ref,
             dx_ref, dgt_ref, dwa_ref, dwx_ref, dv_ref, xsc, asc, bsc, gsc, dusc):
        xsc[0:LRU_HALO, :] = jnp.zeros((LRU_HALO, HEAD), F32)
        xsc[LRU_HALO:, :] = x_ref[...]
        x_all = xsc[...]
        u, r, i, a, mult, sp = _lru_gates(x_all, tp, cw_ref, cb_ref, wa_ref, ba_ref, wx_ref, bx_ref, lam_ref)
        rows = lax.broadcasted_iota(jnp.int32, (tp, HEAD), 0)
        hs = hs_ref[...]
        dy = dy_ref[...]
        gelu, dgelu = _gelu_parts(gt_ref[...])
        dgt_ref[...] = (dy * hs * dgelu).astype(BF16)
        bb = dy * gelu
        aa = jnp.where(rows == tp - 1, 0.0, pltpu.roll(a, tp - 1, 0))
        sub = rows % SUBLANE
        for k in (1, 2, 4):
            m = sub < SUBLANE - k
            bb = jnp.where(m, aa * pltpu.roll(bb, tp - k, 0) + bb, bb)
            aa = jnp.where(m, aa * pltpu.roll(aa, tp - k, 0), aa)
        asc[...] = aa
        bsc[...] = bb

        def grp(gi, carry):
            g = ng - 1 - gi
            r0 = pl.multiple_of(g * SUBLANE, SUBLANE)
            gg = bsc[pl.ds(r0, SUBLANE), :] + asc[pl.ds(r0, SUBLANE), :] * carry
            gsc[pl.ds(r0, SUBLANE), :] = gg
            return jnp.broadcast_to(gg[0:1, :], (SUBLANE, HEAD))

        lax.fori_loop(0, ng, grp, jnp.zeros((SUBLANE, HEAD), F32))
        g = gsc[...]
        first = rows == 0
        hprev = jnp.where(first, 0.0, pltpu.roll(hs, 1, 0))
        iu = i * u
        d_iu = g * jnp.where(first, 1.0, mult)
        dmult_term = jnp.where(first, 0.0, g * iu * (-(a * a) / mult))
        dla = g * hprev * a + dmult_term
        dr = dla * (-LRU_C) * sp
        dv_ref[0, 7:8, :] = _colsum(dla * (LRU_C * r) * _sigmoid(-lam_ref[...]))
        dpr = dr * r * (1.0 - r)
        dpi = d_iu * u * i * (1.0 - i)
        dv_ref[0, 5:6, :] = _colsum(dpr)
        dv_ref[0, 6:7, :] = _colsum(dpi)
        u16 = u.astype(BF16)
        dpr16 = dpr.astype(BF16)
        dpi16 = dpi.astype(BF16)
        dwa_ref[0] = _dot_tn(u16, dpr16)
        dwx_ref[0] = _dot_tn(u16, dpi16)
        du = d_iu * i + _dot_nt(dpr16, wa_ref[0]) + _dot_nt(dpi16, wx_ref[0])
        dv_ref[0, 4:5, :] = _colsum(du)
        for j in range(LRU_CONV):
            sh = LRU_CONV - 1 - j
            xs = x_all if sh == 0 else pltpu.roll(x_all, sh, 0)
            dv_ref[0, j:j + 1, :] = _colsum(du * xs[LRU_HALO:, :])
        dusc[0:tp, :] = du
        dusc[tp:, :] = jnp.zeros((LRU_HALO, HEAD), F32)
        dx_ref[...] = _conv_taps_t(dusc[...], cw_ref, tp, LRU_HALO, LRU_CONV).astype(BF16)

    col, vec, mat = _lru_specs(tp, nb)
    ocol = lambda: pl.BlockSpec((tp, HEAD), lambda j: (0, j))
    return pl.pallas_call(
        body, name="lru_bwd",
        grid=(nb,),
        in_specs=[col(2), col(3), ocol(), ocol(), pl.BlockSpec((LRU_CONV, HEAD), lambda j: (0, j)), vec(), mat(),
                  vec(), mat(), vec(), vec()],
        out_specs=[ocol(), ocol(), mat(), mat(), pl.BlockSpec((1, SUBLANE, HEAD), lambda j: (j, 0, 0))],
        out_shape=[jax.ShapeDtypeStruct((tp, D_LRU), BF16), jax.ShapeDtypeStruct((tp, D_LRU), BF16),
                   jax.ShapeDtypeStruct((nb, HEAD, HEAD), F32), jax.ShapeDtypeStruct((nb, HEAD, HEAD), F32),
                   jax.ShapeDtypeStruct((nb, SUBLANE, HEAD), F32)],
        scratch_shapes=[pltpu.VMEM((tp + LRU_HALO, HEAD), F32), pltpu.VMEM((tp, HEAD), F32),
                        pltpu.VMEM((tp, HEAD), F32), pltpu.VMEM((tp, HEAD), F32),
                        pltpu.VMEM((tp + LRU_HALO, HEAD), F32)],
        compiler_params=_cparams(1),
    )(p, p, hs, dyd, cw, cb, wa, ba, wx, bx, lam)


def _mesh_pos():
    return lax.axis_index("x"), lax.axis_index("y"), lax.axis_index("c")


def _other_chips(x, y):
    return [(1 - x, y), (x, 1 - y), (1 - x, 1 - y)]


ANY = pl.BlockSpec(memory_space=pl.ANY)


def _allgather_shards(arrs, split):
    n = len(arrs)

    def body(*refs):
        ins, outs = refs[:n], refs[n:2 * n]
        send1, recv1, send2, recv2, send3, recv3 = refs[2 * n:]
        x, y, c = _mesh_pos()
        chip = 2 * x + y
        sibling = (x, y, 1 - c)
        others = _other_chips(x, y)

        def rows(k, cc):
            half = arrs[k].shape[0] // 2
            return pl.ds(cc * half, half)

        def remote(src, dst, ssem, rsem, dev):
            return pltpu.make_async_remote_copy(src_ref=src, dst_ref=dst, send_sem=ssem, recv_sem=rsem,
                                                device_id=dev, device_id_type=MESH)

        started = [remote(ins[k], outs[k].at[chip], send3.at[k], recv3.at[k], sibling) for k in range(n)]
        for cp in started:
            cp.start()
        for k in range(n):
            for j, (ox, oy) in enumerate(others):
                if split[k]:
                    src, dst = ins[k].at[rows(k, c)], outs[k].at[chip, rows(k, c)]
                else:
                    src, dst = ins[k], outs[k].at[chip]
                cp = remote(src, dst, send1.at[3 * k + j], recv1.at[3 * k + j], (ox, oy, c))
                cp.start()
                started.append(cp)
        for j, (ox, oy) in enumerate(others):
            ochip = 2 * ox + oy
            for k in range(n):
                if split[k]:
                    blk = outs[k].at[ochip, rows(k, c)]
                    remote(blk, blk, send1.at[3 * k + j], recv1.at[3 * k + j], sibling).wait_recv()
                    cp = remote(blk, blk, send2.at[3 * k + j], recv2.at[3 * k + j], sibling)
                    cp.start()
                    started.append(cp)
                else:
                    blk = outs[k].at[ochip]
                    remote(blk, blk, send1.at[3 * k + j], recv1.at[3 * k + j], sibling).wait_recv()
        for j, (ox, oy) in enumerate(others):
            ochip = 2 * ox + oy
            for k in range(n):
                if split[k]:
                    blk = outs[k].at[ochip, rows(k, 1 - c)]
                    remote(blk, blk, send2.at[3 * k + j], recv2.at[3 * k + j], sibling).wait_recv()
        for k in range(n):
            blk = outs[k].at[chip]
            remote(blk, blk, send3.at[k], recv3.at[k], sibling).wait_recv()
        for cp in started:
            cp.wait_send()

    return pl.pallas_call(
        body, name="allgather_shards",
        in_specs=[ANY] * n, out_specs=[ANY] * n,
        out_shape=[jax.ShapeDtypeStruct((N_SHARD,) + a.shape, a.dtype) for a in arrs],
        scratch_shapes=[pltpu.SemaphoreType.DMA((3 * n,)), pltpu.SemaphoreType.DMA((3 * n,)),
                        pltpu.SemaphoreType.DMA((3 * n,)), pltpu.SemaphoreType.DMA((3 * n,)),
                        pltpu.SemaphoreType.DMA((n,)), pltpu.SemaphoreType.DMA((n,))],
    )(*arrs)


HBM = pl.BlockSpec(memory_space=pltpu.HBM)
SEM = pl.BlockSpec(memory_space=pltpu.SEMAPHORE)
DATAFLOW = pltpu.SideEffectType.DATAFLOW_SIDE_EFFECTING
N_PEERS = 4


def _in_hbm(a):
    return pltpu.with_memory_space_constraint(a, pltpu.HBM)


def _gather_peers(x, y, c):
    return [((ox, oy, c), 2 * ox + oy) for ox, oy in _other_chips(x, y)] + [((x, y, 1 - c), 2 * x + y)]


def _gather_refs(src, land, slot, c, split):
    if not split:
        return src, land.at[slot]
    half = src.shape[0] // 2
    return src.at[pl.ds(c * half, half)], land.at[slot, pl.ds(c * half, half)]


def _gather_start(arrs, split):
    n = len(arrs)

    def body(*refs):
        ins, lands = refs[:n], refs[n:2 * n]
        ssem, rsem = refs[2 * n:2 * n + 2]
        token = refs[-1]
        x, y, c = _mesh_pos()
        chip = 2 * x + y
        for k in range(n):
            for j, (dev, _) in enumerate(_gather_peers(x, y, c)):
                src, dst = _gather_refs(ins[k], lands[k], chip, c, split[k] and j < N_PEERS - 1)
                pltpu.make_async_remote_copy(
                    src_ref=src, dst_ref=dst, send_sem=ssem.at[N_PEERS * k + j],
                    recv_sem=rsem.at[N_PEERS * k + j], device_id=dev, device_id_type=MESH).start()
        token[...] = jnp.zeros_like(token)

    lands = [_in_hbm(lax.empty((N_SHARD,) + a.shape, a.dtype)) for a in arrs]
    out = pl.pallas_call(
        body, name="gather_start",
        in_specs=[HBM] * (2 * n),
        out_specs=[SEM, SEM] + [HBM] * (2 * n) + [pl.BlockSpec(memory_space=pltpu.VMEM)],
        out_shape=[pltpu.SemaphoreType.DMA((N_PEERS * n,)), pltpu.SemaphoreType.DMA((N_PEERS * n,))]
        + [pltpu.HBM(a.shape, a.dtype) for a in arrs]
        + [pltpu.HBM((N_SHARD,) + a.shape, a.dtype) for a in arrs]
        + [jax.ShapeDtypeStruct((SUBLANE, LANE), F32)],
        input_output_aliases={k: 2 + k for k in range(2 * n)},
        compiler_params=pltpu.CompilerParams(has_side_effects=DATAFLOW),
    )(*[_in_hbm(a) for a in arrs], *lands)
    return out[0], out[1], list(out[2:2 + n]), list(out[2 + n:2 + 2 * n]), out[-1]


def _gather_wait(ssem, rsem, srcs, lands, ks, after, split=False):
    n = len(ks)

    def body(*refs):
        ins, lnd = refs[:n], refs[n:2 * n]
        ssem_ref, rsem_ref = refs[2 * n:2 * n + 2]
        x, y, c = _mesh_pos()
        for i, k in enumerate(ks):
            for j, (dev, pchip) in enumerate(_gather_peers(x, y, c)):
                src, dst = _gather_refs(ins[i], lnd[i], pchip, c, split and j < N_PEERS - 1)
                cp = pltpu.make_async_remote_copy(
                    src_ref=src, dst_ref=dst, send_sem=ssem_ref.at[N_PEERS * k + j],
                    recv_sem=rsem_ref.at[N_PEERS * k + j], device_id=dev, device_id_type=MESH)
                cp.wait_send()
                cp.wait_recv()

    out = pl.pallas_call(
        body, name="gather_wait",
        in_specs=[HBM] * (2 * n) + [SEM, SEM] + [ANY] * len(after),
        out_specs=[HBM] * (2 * n),
        out_shape=[pltpu.HBM(a.shape, a.dtype) for a in srcs] + [pltpu.HBM(a.shape, a.dtype) for a in lands],
        input_output_aliases={k: k for k in range(2 * n)},
        compiler_params=pltpu.CompilerParams(has_side_effects=DATAFLOW),
    )(*srcs, *lands, ssem, rsem, *after)
    return list(out[n:])


def _pair_forward(lands):
    n = len(lands)

    def body(*refs):
        outs = refs[n:2 * n]
        ssem, rsem = refs[2 * n:]
        x, y, c = _mesh_pos()
        sibling = (x, y, 1 - c)
        cps = []
        for k in range(n):
            half = lands[k].shape[1] // 2
            for j, (ox, oy) in enumerate(_other_chips(x, y)):
                mine = outs[k].at[2 * ox + oy, pl.ds(c * half, half)]
                cp = pltpu.make_async_remote_copy(src_ref=mine, dst_ref=mine, send_sem=ssem.at[3 * k + j],
                                                  recv_sem=rsem.at[3 * k + j], device_id=sibling, device_id_type=MESH)
                cp.start()
                cps.append(cp)
        for k in range(n):
            half = lands[k].shape[1] // 2
            for j, (ox, oy) in enumerate(_other_chips(x, y)):
                theirs = outs[k].at[2 * ox + oy, pl.ds((1 - c) * half, half)]
                pltpu.make_async_remote_copy(src_ref=theirs, dst_ref=theirs, send_sem=ssem.at[3 * k + j],
                                             recv_sem=rsem.at[3 * k + j], device_id=sibling,
                                             device_id_type=MESH).wait_recv()
        for cp in cps:
            cp.wait_send()

    return pl.pallas_call(
        body, name="pair_forward",
        in_specs=[ANY] * n, out_specs=[ANY] * n,
        out_shape=[jax.ShapeDtypeStruct(a.shape, a.dtype) for a in lands],
        scratch_shapes=[pltpu.SemaphoreType.DMA((3 * n,)), pltpu.SemaphoreType.DMA((3 * n,))],
        input_output_aliases={k: k for k in range(n)},
    )(*lands)


N_SOURCES = 7


def _reduce_peers(x, y, c):
    peers = []
    for ox, oy in _other_chips(x, y):
        for rel in range(2):
            peers.append(((ox, oy, c + rel - 2 * c * rel), 2 * ox + oy))
    peers.append(((x, y, 1 - c), 2 * x + y))
    return peers


def _reduce_start(arrs, slots):
    n = len(arrs)

    def body(*refs):
        ins, lands = refs[:n], refs[n:2 * n]
        ssem, rsem = refs[2 * n:2 * n + 2]
        token = refs[-1]
        x, y, c = _mesh_pos()
        me = 2 * (2 * x + y) + c
        for k in range(n):
            half = arrs[k].shape[1] // 2
            for p, (dev, ochip) in enumerate(_reduce_peers(x, y, c)):
                pltpu.make_async_remote_copy(
                    src_ref=ins[k].at[ochip, pl.ds(dev[2] * half, half)], dst_ref=lands[k].at[me],
                    send_sem=ssem.at[N_SOURCES * k + p], recv_sem=rsem.at[N_SOURCES * k + p],
                    device_id=dev, device_id_type=MESH).start()
        token[...] = jnp.zeros_like(token)

    out = pl.pallas_call(
        body, name="reduce_start",
        in_specs=[HBM] * (2 * n),
        out_specs=[SEM, SEM] + [HBM] * (2 * n) + [pl.BlockSpec(memory_space=pltpu.VMEM)],
        out_shape=[pltpu.SemaphoreType.DMA((N_SOURCES * n,)), pltpu.SemaphoreType.DMA((N_SOURCES * n,))]
        + [pltpu.HBM(a.shape, a.dtype) for a in arrs] + [pltpu.HBM(a.shape, a.dtype) for a in slots]
        + [jax.ShapeDtypeStruct((SUBLANE, LANE), F32)],
        input_output_aliases={k: 2 + k for k in range(2 * n)},
        compiler_params=pltpu.CompilerParams(has_side_effects=DATAFLOW),
    )(*[_in_hbm(a) for a in arrs], *[_in_hbm(a) for a in slots])
    return out[0], out[1], list(out[2:2 + n]), list(out[2 + n:2 + 2 * n]), out[-1]


def _reduce_wait(ssem, rsem, arrs, slots, after):
    n = len(arrs)

    def body(*refs):
        ins, lnd = refs[:n], refs[n:2 * n]
        ssem_ref, rsem_ref = refs[2 * n:2 * n + 2]
        x, y, c = _mesh_pos()
        for k in range(n):
            half = arrs[k].shape[1] // 2
            for p, (dev, ochip) in enumerate(_reduce_peers(x, y, c)):
                cp = pltpu.make_async_remote_copy(
                    src_ref=ins[k].at[ochip, pl.ds(dev[2] * half, half)], dst_ref=lnd[k].at[2 * ochip + dev[2]],
                    send_sem=ssem_ref.at[N_SOURCES * k + p], recv_sem=rsem_ref.at[N_SOURCES * k + p],
                    device_id=dev, device_id_type=MESH)
                cp.wait_send()
                cp.wait_recv()

    out = pl.pallas_call(
        body, name="reduce_wait",
        in_specs=[HBM] * (2 * n) + [SEM, SEM] + [ANY] * len(after),
        out_specs=[HBM] * (2 * n),
        out_shape=[pltpu.HBM(a.shape, a.dtype) for a in arrs] + [pltpu.HBM(a.shape, a.dtype) for a in slots],
        input_output_aliases={k: k for k in range(2 * n)},
        compiler_params=pltpu.CompilerParams(has_side_effects=DATAFLOW),
    )(*arrs, *slots, ssem, rsem, *after)
    return list(out[n:])


def _own_part(arrs, chip, core, me):
    n = len(arrs)
    nb = GRAD_ROW_BLOCKS

    def body(chip_ref, core_ref, me_ref, *refs):
        for k in range(n):
            refs[n + k][...] = refs[k][...]

    def blk(a):
        return (1, a.shape[1] // 2 // nb, a.shape[2])

    grid_spec = pltpu.PrefetchScalarGridSpec(
        num_scalar_prefetch=3, grid=(nb,),
        in_specs=[pl.BlockSpec(blk(a), lambda i, ch, co, me: (ch[0], co[0] * nb + i, 0)) for a in arrs],
        out_specs=[pl.BlockSpec(blk(a), lambda i, ch, co, me: (me[0], i, 0)) for a in arrs])
    return pl.pallas_call(
        body, name="own_part", grid_spec=grid_spec,
        out_shape=[jax.ShapeDtypeStruct((N_DEV, a.shape[1] // 2, a.shape[2]), a.dtype) for a in arrs],
        compiler_params=_cparams(1),
    )(chip, core, me, *arrs)


def _sum_devices(arrs, core):
    n = len(arrs)
    nb = GRAD_ROW_BLOCKS

    def body(c_ref, *refs):
        for k in range(n):
            r = refs[k]
            acc = r[0].astype(F32)
            for dev in range(1, N_DEV):
                acc = acc + r[dev].astype(F32)
            refs[n + k][...] = acc

    grid_spec = pltpu.PrefetchScalarGridSpec(
        num_scalar_prefetch=1, grid=(nb,),
        in_specs=[pl.BlockSpec((N_DEV, a.shape[1] // nb, a.shape[2]), lambda i, c: (0, i, 0)) for a in arrs],
        out_specs=[pl.BlockSpec((a.shape[1] // nb, a.shape[2]), lambda i, c: (c[0] * nb + i, 0)) for a in arrs])
    return pl.pallas_call(
        body, name="sum_devices", grid_spec=grid_spec,
        out_shape=[jax.ShapeDtypeStruct((2 * a.shape[1], a.shape[2]), F32) for a in arrs],
        compiler_params=_cparams(1),
    )(core, *arrs)


def _small_own(v, me):
    m = v.shape[0]

    def body(me_ref, v_ref, o_ref):
        o_ref[0] = v_ref[...]

    grid_spec = pltpu.PrefetchScalarGridSpec(
        num_scalar_prefetch=1, grid=(1,),
        in_specs=[pl.BlockSpec((m, LANE), lambda i, me: (0, 0))],
        out_specs=pl.BlockSpec((1, m, LANE), lambda i, me: (me[0], 0, 0)))
    return pl.pallas_call(
        body, name="small_own", grid_spec=grid_spec,
        out_shape=jax.ShapeDtypeStruct((N_DEV, m, LANE), v.dtype),
        compiler_params=_cparams(1),
    )(me, v)


def _small_start(v, slots):
    def body(v_ref, land, ssem, rsem, v_thru, land_thru, token):
        del v_thru, land_thru
        x, y, c = _mesh_pos()
        me = 2 * (2 * x + y) + c
        for p, (dev, _) in enumerate(_reduce_peers(x, y, c)):
            pltpu.make_async_remote_copy(src_ref=v_ref, dst_ref=land.at[me], send_sem=ssem.at[p],
                                         recv_sem=rsem.at[p], device_id=dev, device_id_type=MESH).start()
        token[...] = jnp.zeros_like(token)

    out = pl.pallas_call(
        body, name="small_start",
        in_specs=[HBM, HBM],
        out_specs=[SEM, SEM, HBM, HBM, pl.BlockSpec(memory_space=pltpu.VMEM)],
        out_shape=[pltpu.SemaphoreType.DMA((N_SOURCES,)), pltpu.SemaphoreType.DMA((N_SOURCES,)),
                   pltpu.HBM(v.shape, v.dtype), pltpu.HBM(slots.shape, slots.dtype),
                   jax.ShapeDtypeStruct((SUBLANE, LANE), F32)],
        input_output_aliases={0: 2, 1: 3},
        compiler_params=pltpu.CompilerParams(has_side_effects=DATAFLOW),
    )(_in_hbm(v), _in_hbm(slots))
    return out


def _small_wait(ssem, rsem, v, slots, after):
    def body(*refs):
        v_ref, land, ssem_ref, rsem_ref = refs[:4]
        x, y, c = _mesh_pos()
        for p, (dev, ochip) in enumerate(_reduce_peers(x, y, c)):
            cp = pltpu.make_async_remote_copy(src_ref=v_ref, dst_ref=land.at[2 * ochip + dev[2]],
                                              send_sem=ssem_ref.at[p], recv_sem=rsem_ref.at[p],
                                              device_id=dev, device_id_type=MESH)
            cp.wait_send()
            cp.wait_recv()

    out = pl.pallas_call(
        body, name="small_wait",
        in_specs=[HBM, HBM, SEM, SEM] + [ANY] * len(after),
        out_specs=[HBM, HBM],
        out_shape=[pltpu.HBM(v.shape, v.dtype), pltpu.HBM(slots.shape, slots.dtype)],
        input_output_aliases={0: 0, 1: 1},
        compiler_params=pltpu.CompilerParams(has_side_effects=DATAFLOW),
    )(v, slots, ssem, rsem, *after)
    return out[1]


def _scatter_start(arrs, slots):
    n = len(arrs)

    def body(*refs):
        ins, lands = refs[:n], refs[n:2 * n]
        ssem, rsem = refs[2 * n:2 * n + 2]
        token = refs[-1]
        x, y, c = _mesh_pos()
        chip = 2 * x + y
        for k in range(n):
            for j, (ox, oy) in enumerate(_other_chips(x, y)):
                pltpu.make_async_remote_copy(
                    src_ref=ins[k].at[2 * ox + oy], dst_ref=lands[k].at[chip], send_sem=ssem.at[3 * k + j],
                    recv_sem=rsem.at[3 * k + j], device_id=(ox, oy, c), device_id_type=MESH).start()
        token[...] = jnp.zeros_like(token)

    out = pl.pallas_call(
        body, name="scatter_start",
        in_specs=[HBM] * (2 * n),
        out_specs=[SEM, SEM] + [HBM] * (2 * n) + [pl.BlockSpec(memory_space=pltpu.VMEM)],
        out_shape=[pltpu.SemaphoreType.DMA((3 * n,)), pltpu.SemaphoreType.DMA((3 * n,))]
        + [pltpu.HBM(a.shape, a.dtype) for a in arrs] + [pltpu.HBM(a.shape, a.dtype) for a in slots]
        + [jax.ShapeDtypeStruct((SUBLANE, LANE), F32)],
        input_output_aliases={k: 2 + k for k in range(2 * n)},
        compiler_params=pltpu.CompilerParams(has_side_effects=DATAFLOW),
    )(*[_in_hbm(a) for a in arrs], *[_in_hbm(a) for a in slots])
    return out[0], out[1], list(out[2:2 + n]), list(out[2 + n:2 + 2 * n]), out[-1]


def _scatter_wait(ssem, rsem, arrs, slots, after):
    n = len(arrs)

    def body(*refs):
        ins, lnd = refs[:n], refs[n:2 * n]
        ssem_ref, rsem_ref = refs[2 * n:2 * n + 2]
        x, y, c = _mesh_pos()
        for k in range(n):
            for j, (ox, oy) in enumerate(_other_chips(x, y)):
                ochip = 2 * ox + oy
                cp = pltpu.make_async_remote_copy(
                    src_ref=ins[k].at[ochip], dst_ref=lnd[k].at[ochip], send_sem=ssem_ref.at[3 * k + j],
                    recv_sem=rsem_ref.at[3 * k + j], device_id=(ox, oy, c), device_id_type=MESH)
                cp.wait_send()
                cp.wait_recv()

    out = pl.pallas_call(
        body, name="scatter_wait",
        in_specs=[HBM] * (2 * n) + [SEM, SEM] + [ANY] * len(after),
        out_specs=[HBM] * (2 * n),
        out_shape=[pltpu.HBM(a.shape, a.dtype) for a in arrs] + [pltpu.HBM(a.shape, a.dtype) for a in slots],
        input_output_aliases={k: k for k in range(2 * n)},
        compiler_params=pltpu.CompilerParams(has_side_effects=DATAFLOW),
    )(*arrs, *slots, ssem, rsem, *after)
    return list(out[n:])


def _pair_exchange_halves(arrs):
    n = len(arrs)

    def body(*refs):
        ins, outs = refs[:n], refs[n:2 * n]
        ssem, rsem = refs[2 * n:]
        x, y, c = _mesh_pos()
        cps = []
        for k in range(n):
            half = arrs[k].shape[1] // 2
            cp = pltpu.make_async_remote_copy(
                src_ref=ins[k].at[:, pl.ds((1 - c) * half, half)], dst_ref=outs[k],
                send_sem=ssem.at[k], recv_sem=rsem.at[k], device_id=(x, y, 1 - c), device_id_type=MESH)
            cp.start()
            cps.append(cp)
        for cp in cps:
            cp.wait()

    return pl.pallas_call(
        body, name="pair_exchange_halves",
        in_specs=[ANY] * n, out_specs=[ANY] * n,
        out_shape=[jax.ShapeDtypeStruct((a.shape[0], a.shape[1] // 2, a.shape[2]), a.dtype) for a in arrs],
        scratch_shapes=[pltpu.SemaphoreType.DMA((n,)), pltpu.SemaphoreType.DMA((n,))],
    )(*arrs)


GRAD_ROW_BLOCKS = 2


def _pair_add(arrs, recvd, core):
    n = len(arrs)
    nb = GRAD_ROW_BLOCKS

    def body(c_ref, *refs):
        for k in range(n):
            refs[2 * n + k][...] = (refs[k][...].astype(F32) + refs[n + k][...].astype(F32)).astype(BF16)

    def blk(a):
        return (1, a.shape[1] // 2 // nb, a.shape[2])

    grid_spec = pltpu.PrefetchScalarGridSpec(
        num_scalar_prefetch=1, grid=(N_SHARD, nb),
        in_specs=[pl.BlockSpec(blk(a), lambda s, i, c: (s, c[0] * nb + i, 0)) for a in arrs]
        + [pl.BlockSpec(blk(a), lambda s, i, c: (s, i, 0)) for a in arrs],
        out_specs=[pl.BlockSpec(blk(a), lambda s, i, c: (s, i, 0)) for a in arrs])
    return pl.pallas_call(
        body, name="pair_add", grid_spec=grid_spec,
        out_shape=[jax.ShapeDtypeStruct(r.shape, BF16) for r in recvd],
        compiler_params=_cparams(2),
    )(core, *arrs, *recvd)


def _own_slot(arrs, chip):
    n = len(arrs)
    nb = GRAD_ROW_BLOCKS

    def body(c_ref, *refs):
        for k in range(n):
            refs[n + k][...] = refs[k][...]

    def blk(a):
        return (1, a.shape[1] // nb, a.shape[2])

    grid_spec = pltpu.PrefetchScalarGridSpec(
        num_scalar_prefetch=1, grid=(nb,),
        in_specs=[pl.BlockSpec(blk(a), lambda i, c: (c[0], i, 0)) for a in arrs],
        out_specs=[pl.BlockSpec(blk(a), lambda i, c: (c[0], i, 0)) for a in arrs])
    return pl.pallas_call(
        body, name="own_slot", grid_spec=grid_spec,
        out_shape=[jax.ShapeDtypeStruct(a.shape, a.dtype) for a in arrs],
        compiler_params=_cparams(1),
    )(chip, *arrs)


def _scatter_to_owners(arrs, slots):
    n = len(arrs)

    def body(*refs):
        ins, outs = refs[:n], refs[2 * n:3 * n]
        ssem, rsem = refs[3 * n:]
        x, y, c = _mesh_pos()
        chip = 2 * x + y
        others = _other_chips(x, y)
        cps = []
        for k in range(n):
            for j, (ox, oy) in enumerate(others):
                cp = pltpu.make_async_remote_copy(
                    src_ref=ins[k].at[2 * ox + oy], dst_ref=outs[k].at[chip],
                    send_sem=ssem.at[3 * k + j], recv_sem=rsem.at[3 * k + j],
                    device_id=(ox, oy, c), device_id_type=MESH)
                cp.start()
                cps.append(cp)
        for k in range(n):
            for j, (ox, oy) in enumerate(others):
                blk = outs[k].at[2 * ox + oy]
                pltpu.make_async_remote_copy(
                    src_ref=blk, dst_ref=blk, send_sem=ssem.at[3 * k + j], recv_sem=rsem.at[3 * k + j],
                    device_id=(ox, oy, c), device_id_type=MESH).wait_recv()
        for cp in cps:
            cp.wait_send()

    return pl.pallas_call(
        body, name="scatter_to_owners",
        in_specs=[ANY] * (2 * n), out_specs=[ANY] * n,
        out_shape=[jax.ShapeDtypeStruct(a.shape, a.dtype) for a in arrs],
        scratch_shapes=[pltpu.SemaphoreType.DMA((3 * n,)), pltpu.SemaphoreType.DMA((3 * n,))],
        input_output_aliases={n + k: k for k in range(n)},
    )(*arrs, *slots)


def _sum_chips(arrs, core):
    n = len(arrs)
    nb = GRAD_ROW_BLOCKS

    def body(c_ref, *refs):
        for k in range(n):
            r = refs[k]
            refs[n + k][...] = ((r[0].astype(F32) + r[1].astype(F32)) + r[2].astype(F32)) + r[3].astype(F32)

    grid_spec = pltpu.PrefetchScalarGridSpec(
        num_scalar_prefetch=1, grid=(nb,),
        in_specs=[pl.BlockSpec((N_SHARD, a.shape[1] // nb, a.shape[2]), lambda i, c: (0, i, 0)) for a in arrs],
        out_specs=[pl.BlockSpec((a.shape[1] // nb, a.shape[2]), lambda i, c: (c[0] * nb + i, 0)) for a in arrs])
    return pl.pallas_call(
        body, name="sum_chips", grid_spec=grid_spec,
        out_shape=[jax.ShapeDtypeStruct((2 * a.shape[1], a.shape[2]), F32) for a in arrs],
        compiler_params=_cparams(1),
    )(core, *arrs)


def _pair_allgather_halves(arrs):
    n = len(arrs)

    def body(*refs):
        outs = refs[n:2 * n]
        ssem, rsem = refs[2 * n:]
        x, y, c = _mesh_pos()
        cps = []
        for k in range(n):
            h = arrs[k].shape[0] // 2
            mine = outs[k].at[pl.ds(c * h, h)]
            cp = pltpu.make_async_remote_copy(src_ref=mine, dst_ref=mine, send_sem=ssem.at[k],
                                              recv_sem=rsem.at[k], device_id=(x, y, 1 - c), device_id_type=MESH)
            cp.start()
            cps.append(cp)
        for k, cp in enumerate(cps):
            h = arrs[k].shape[0] // 2
            theirs = outs[k].at[pl.ds((1 - c) * h, h)]
            pltpu.make_async_remote_copy(src_ref=theirs, dst_ref=theirs, send_sem=ssem.at[k], recv_sem=rsem.at[k],
                                         device_id=(x, y, 1 - c), device_id_type=MESH).wait_recv()
            cp.wait_send()

    return pl.pallas_call(
        body, name="pair_allgather_halves",
        in_specs=[ANY] * n, out_specs=[ANY] * n,
        out_shape=[jax.ShapeDtypeStruct(a.shape, a.dtype) for a in arrs],
        scratch_shapes=[pltpu.SemaphoreType.DMA((n,)), pltpu.SemaphoreType.DMA((n,))],
        input_output_aliases={k: k for k in range(n)},
    )(*arrs)


N_DEV = 8


def _allgather_all(v):
    m_per, n = v.shape

    def body(x_ref, out_ref, send_sems, recv_sems, local_sem):
        x, y, c = _mesh_pos()
        me, sibling = (x, y, c), (x, y, 1 - c)
        chips = _other_chips(x, y)

        def rows(px, py, pc):
            return out_ref.at[pl.ds((4 * px + 2 * py + pc) * m_per, m_per), :]

        def copy(k, block, to, src=None):
            return pltpu.make_async_remote_copy(
                src_ref=rows(*block) if src is None else src, dst_ref=rows(*block),
                send_sem=send_sems.at[k], recv_sem=recv_sems.at[k], device_id=to, device_id_type=MESH)

        mine = pltpu.make_async_copy(x_ref, rows(*me), local_sem)
        mine.start()
        first = [copy(0, me, sibling, src=x_ref)]
        first += [copy(1 + j, me, (*chip, c), src=x_ref) for j, chip in enumerate(chips)]
        for cp in first:
            cp.start()
        passed = [copy(4 + j, (*chip, c), sibling) for j, chip in enumerate(chips)]
        for j, chip in enumerate(chips):
            copy(1 + j, (*chip, c), me).wait_recv()
            passed[j].start()
        copy(0, sibling, me).wait_recv()
        for j, chip in enumerate(chips):
            copy(4 + j, (*chip, 1 - c), me).wait_recv()
        for cp in first + passed:
            cp.wait_send()
        mine.wait()

    return pl.pallas_call(
        body, name="allgather_all",
        out_shape=jax.ShapeDtypeStruct((N_DEV * m_per, n), v.dtype),
        in_specs=[pl.BlockSpec(memory_space=pltpu.VMEM)],
        out_specs=pl.BlockSpec(memory_space=pltpu.VMEM),
        scratch_shapes=[pltpu.SemaphoreType.DMA((7,)), pltpu.SemaphoreType.DMA((7,)), pltpu.SemaphoreType.DMA],
        compiler_params=pltpu.CompilerParams(vmem_limit_bytes=VMEM_LIMIT_MB * 1024 * 1024),
    )(v)


def _adamw_math(w, g, m, v):
    m2 = ADAM_B1 * m + (1.0 - ADAM_B1) * g
    v2 = ADAM_B2 * v + (1.0 - ADAM_B2) * (g * g)
    m_hat = m2 / (1.0 - ADAM_B1 ** ADAM_STEP)
    v_hat = v2 / (1.0 - ADAM_B2 ** ADAM_STEP)
    delta = -ADAM_LR * (m_hat / (jnp.sqrt(v_hat) + ADAM_EPS) + ADAM_WD * w)
    return delta, m2, v2


def _adamw(w, m, v, gs, nblk):
    nl, r, n = w.shape
    assert nl == len(gs) and nl in (1, 2)
    br = r // nblk

    def body(w_ref, m_ref, v_ref, *rest):
        g_refs, (go_ref, d_ref, mo_ref, vo_ref) = rest[:nl], rest[nl:]
        g = g_refs[0][...]
        if nl == 2:
            g = jnp.where(pl.program_id(0) == 0, g, g_refs[1][...])
        delta, m2, v2 = _adamw_math(w_ref[0], g, m_ref[0], v_ref[0])
        go_ref[0] = g
        d_ref[0] = delta
        mo_ref[0] = m2
        vo_ref[0] = v2

    spec = pl.BlockSpec((1, br, n), lambda l, i: (l, i, 0))
    g_specs = [pl.BlockSpec((br, n), lambda l, i: (i, 0))] if nl == 1 else [
        pl.BlockSpec((br, n), lambda l, i: (jnp.where(l == 0, i, nblk - 1), 0)),
        pl.BlockSpec((br, n), lambda l, i: (jnp.where(l == 1, i, 0), 0))]
    return pl.pallas_call(
        body, name="adamw", grid=(nl, nblk),
        in_specs=[spec, spec, spec] + g_specs,
        out_specs=[spec] * 4,
        out_shape=[jax.ShapeDtypeStruct((nl, r, n), F32)] * 4,
        compiler_params=_cparams(2),
    )(w, m, v, *gs)


def _small_reduce_adamw(parts, w, m, v, rep_rows, sh_rows):
    mrows = rep_rows + N_SHARD * sh_rows + LOSS_ROWS

    def body(p_ref, w_ref, m_ref, v_ref, go_ref, d_ref, mo_ref, vo_ref, loss_ref):
        x, y, _ = _mesh_pos()
        mine = rep_rows + (2 * x + y) * sh_rows
        g_rep = p_ref[0:rep_rows, :]
        g_sh = p_ref[pl.ds(pl.multiple_of(mine, SUBLANE), sh_rows), :]
        loss = p_ref[mrows - LOSS_ROWS:mrows, :]
        for k in range(1, N_DEV):
            g_rep = g_rep + p_ref[k * mrows:k * mrows + rep_rows, :]
            g_sh = g_sh + p_ref[pl.ds(pl.multiple_of(k * mrows + mine, SUBLANE), sh_rows), :]
            loss = loss + p_ref[(k + 1) * mrows - LOSS_ROWS:(k + 1) * mrows, :]
        g = jnp.concatenate([g_rep, g_sh], axis=0)
        delta, m2, v2 = _adamw_math(w_ref[...], g, m_ref[...], v_ref[...])
        go_ref[...] = g
        d_ref[...] = delta
        mo_ref[...] = m2
        vo_ref[...] = v2
        loss_ref[...] = loss

    return pl.pallas_call(
        body, name="small_reduce_adamw",
        out_shape=[jax.ShapeDtypeStruct((rep_rows + sh_rows, 128), F32)] * 4
        + [jax.ShapeDtypeStruct((LOSS_ROWS, 128), F32)],
        compiler_params=pltpu.CompilerParams(vmem_limit_bytes=VMEM_LIMIT_MB * 1024 * 1024),
    )(parts, w, m, v)


LANE = 128
REP_SPEC = (("ffn1_norm", 16), ("mix_norm", 16), ("ffn2_norm", 16), ("final_norm", 8), ("pool_w", 128),
            ("pool_scale", 8), ("hgrn_lb_logits", 16), ("hgrn_gnorm", 8), ("lru_wa", 256), ("lru_wx", 256))
SH_SPEC = (("meta_tokens", 32), ("conv_w", 32), ("lru_conv_w", 8), ("conv_b", 8), ("conv_ln_g", 8),
           ("conv_ln_b", 8), ("lru_conv_b", 8), ("lru_ba", 8), ("lru_bx", 8), ("lru_lambda", 8))
REP_ROWS = sum(r for _, r in REP_SPEC)
SH_ROWS = sum(r for _, r in SH_SPEC)


def _pack_rows(vals, spec):
    parts = []
    for name, rows in spec:
        flat = vals[name].astype(F32).reshape(-1, LANE)
        if flat.shape[0] < rows:
            flat = jnp.concatenate([flat, jnp.zeros((rows - flat.shape[0], LANE), F32)], axis=0)
        parts.append(flat)
    return jnp.concatenate(parts, axis=0)


def _unpack_rows(packed, spec, shapes):
    out = {}
    off = 0
    for name, rows in spec:
        shp = shapes[name]
        n = int(np.prod(shp)) // LANE
        out[name] = packed[off:off + n].reshape(shp)
        off += rows
    return out


def _block_diag(blocks):
    n, b, _ = blocks.shape
    return sum(jnp.pad(blocks[g], ((g * b, (n - 1 - g) * b), (g * b, (n - 1 - g) * b))) for g in range(n))


def _diag_blocks(mat, n):
    b = mat.shape[0] // n
    return jnp.stack([mat[g * b:(g + 1) * b, g * b:(g + 1) * b] for g in range(n)])


BIG = ("ffn1_wg", "ffn1_wu", "ffn2_wg", "ffn2_wu", "ffn1_wd", "ffn2_wd", "w_in_even", "w_out_even",
       "w_in_odd", "w_out_odd")
ADAM_BLOCKS = {"ffn1_wg": 8, "ffn1_wu": 8, "ffn2_wg": 8, "ffn2_wu": 8, "ffn1_wd": 4, "ffn2_wd": 4,
               "w_in_even": 4, "w_out_even": 2, "w_in_odd": 4, "w_out_odd": 2}
WEIGHT_NAMES = ('meta_tokens', 'ffn1_norm', 'ffn1_wg', 'ffn1_wu', 'ffn1_wd', 'mix_norm', 'ffn2_norm', 'ffn2_wg',
                'ffn2_wu', 'ffn2_wd', 'w_in_even', 'pool_w', 'pool_scale', 'hgrn_lb_logits', 'hgrn_gnorm',
                'w_out_even', 'w_in_odd', 'conv_w', 'conv_b', 'conv_ln_g', 'conv_ln_b', 'lru_conv_w',
                'lru_conv_b', 'lru_wa', 'lru_ba', 'lru_wx', 'lru_bx', 'lru_lambda', 'w_out_odd', 'final_norm')


def _rows2d(a):
    return a.reshape(-1, a.shape[-1])


def _local_step_v2(x, tgt, w, gathered, small_full):
    s_len, d = x.shape
    t_real = s_len + N_META
    tp = -(-t_real // ROW_ALIGN) * ROW_ALIGN
    tm = _tile(tp, 832, ROW_ALIGN)
    tm_small = _tile(tp, 416, 16)
    tr = _tile(tp, 416, SUBLANE)
    f1 = ("ffn1_norm", "ffn1_wg", "ffn1_wu", "ffn1_wd")
    f2 = ("ffn2_norm", "ffn2_wg", "ffn2_wu", "ffn2_wd")

    meta_full = small_full["meta_tokens"]
    h0 = jnp.concatenate([meta_full, x, jnp.zeros((tp - t_real, d), F32)], axis=0)
    tgt_pad = jnp.concatenate([jnp.zeros((N_META, d), F32), tgt, jnp.zeros((tp - t_real, d), F32)], axis=0)

    w_in_even = jnp.transpose(gathered["w_in_even"], (1, 0, 2)).reshape(d, D_IN_EVEN)
    w_out_even = gathered["w_out_even"].reshape(d, d)
    w_out_odd = gathered["w_out_odd"].reshape(d, d)
    even_piece = [(w_in_even, (d, D_IN_EVEN), (0, 0))]
    odd_pieces = [(gathered["w_in_odd"], (1, d, D_IN_ODD // N_SHARD), (k, 0, 0)) for k in range(N_SHARD)]
    pool_wbd = _block_diag(w["pool_w"][0]).astype(BF16)
    pool_scale = w["pool_scale"]
    wa_bd = _block_diag2(w["lru_wa"][0]).astype(BF16)
    wx_bd = _block_diag2(w["lru_wx"][0]).astype(BF16)
    mst, msk, n_lev = _hgrn_consts(HG_CHUNK)
    conv_w = small_full["conv_w"]
    sf = small_full

    def gain(name, layer):
        return w[name][layer:layer + 1]

    def ffn(h, names, layer):
        return _ffn_fwd(h, gain(names[0], layer), gathered[names[1]], gathered[names[2]], gathered[names[3]],
                        layer, tm)

    h1, a1, b1, n1 = ffn(h0, f1, 0)
    p0, nm0 = _proj_fwd(h1, gain("mix_norm", 0), 0, even_piece, tm_small)
    ya = _pool_fwd(p0, pool_wbd, pool_scale, tr)
    yb, states = _hgrn_fwd(p0, w["hgrn_lb_logits"], w["hgrn_gnorm"], mst, msk, n_lev, tm)
    h2 = _out_fwd(h1, ya, yb, w_out_even, tm)
    h3, a2, b2, n2 = ffn(h2, f2, 0)
    h4, a3, b3, n3 = ffn(h3, f1, 1)
    p1, nm1 = _proj_fwd(h4, gain("mix_norm", 1), 1, odd_pieces, tm_small)
    yc = _convmod_fwd(p1, conv_w, sf["conv_b"], sf["conv_ln_g"], sf["conv_ln_b"], tr)
    lru_args = (sf["lru_conv_w"], sf["lru_conv_b"], wa_bd, sf["lru_ba"], wx_bd, sf["lru_bx"], sf["lru_lambda"])
    yd, hs = _lru_fwd(p1, *lru_args)
    h5 = _out_fwd(h4, yc, yd, w_out_odd, tm)
    h6, a4, b4, n4 = ffn(h5, f2, 1)
    loss, dh6, dg_final = _loss_bwd(h6, w["final_norm"].reshape(1, d), tgt_pad, t_real, tm)

    def ffn_bwd(dho, h, n, a, b, names, layer, acc):
        dh, da, db, dg = _ffn_bwd_act(dho, h, gain(names[0], layer), a, b, gathered[names[1]], gathered[names[2]],
                                      gathered[names[3]], layer, tm_small)
        acc = _ffn_bwd_w(dho, n, a, b, da, db, acc[0], acc[1], acc[2], layer, tm)
        return dh, dg, acc

    none3 = (None, None, None)
    dh5, dg_f2_l1, g_f2 = ffn_bwd(dh6, h5, n4, a4, b4, f2, 1, none3)
    dyc, dyd, dw_out_odd = _out_bwd(dh5, yc, yd, w_out_odd, tm)
    dca, dcb, dconv_w, dconv_vec = _convmod_bwd(p1, dyc, conv_w, sf["conv_b"], sf["conv_ln_g"], sf["conv_ln_b"], tr)
    dlx, dlg, dwa_bd, dwx_bd, dlru_vec = _lru_bwd(p1, hs, dyd, *lru_args)
    dp1 = [dca, dcb, dlx, dlg]
    dh4, dg_mix_l1 = _proj_bwd_act(dh5, h4, gain("mix_norm", 1), 1, dp1, odd_pieces, tm_small)
    dw_in_odd = jnp.stack(_proj_bwd_w(nm1, dp1, tm))
    dh3, dg_f1_l1, g_f1 = ffn_bwd(dh4, h3, n3, a3, b3, f1, 1, none3)
    dh2, dg_f2_l0, g_f2 = ffn_bwd(dh3, h2, n2, a2, b2, f2, 0, g_f2)
    dya, dyb, dw_out_even = _out_bwd(dh2, ya, yb, w_out_even, tm)
    dpool, dpool_wbd, dpool_scale = _pool_bwd(p0, dya, pool_wbd, pool_scale, tr)
    dq, dz, dv, dgate, dlb_logits, dgn_heads = _hgrn_bwd(p0, dyb, states, w["hgrn_lb_logits"], w["hgrn_gnorm"],
                                                         mst, msk, n_lev, tm)
    dp0 = [jnp.concatenate([dpool, dq, dz, dv, dgate], axis=1)]
    dh1, dg_mix_l0 = _proj_bwd_act(dh2, h1, gain("mix_norm", 0), 0, dp0, even_piece, tm_small)
    (dw_in_even,) = _proj_bwd_w(nm0, dp0, tm_small)
    dh0, dg_f1_l0, g_f1 = ffn_bwd(dh1, h0, n1, a1, b1, f1, 0, g_f1)

    grad_x = dh0[N_META:t_real]
    big = {
        "ffn1_wg": g_f1[0], "ffn1_wu": g_f1[1], "ffn1_wd": g_f1[2],
        "ffn2_wg": g_f2[0], "ffn2_wu": g_f2[1], "ffn2_wd": g_f2[2],
        "w_in_even": jnp.transpose(dw_in_even.reshape(d, N_SHARD, D_IN_EVEN // N_SHARD), (1, 0, 2)),
        "w_out_even": dw_out_even.reshape(N_SHARD, d // N_SHARD, d),
        "w_in_odd": dw_in_odd,
        "w_out_odd": dw_out_odd.reshape(N_SHARD, d // N_SHARD, d),
    }
    rep = {
        "ffn1_norm": jnp.concatenate([dg_f1_l0, dg_f1_l1], axis=0),
        "mix_norm": jnp.concatenate([dg_mix_l0, dg_mix_l1], axis=0),
        "ffn2_norm": jnp.concatenate([dg_f2_l0, dg_f2_l1], axis=0),
        "final_norm": dg_final,
        "pool_w": _diag_blocks(dpool_wbd, len(POOL_WINDOWS)),
        "pool_scale": dpool_scale,
        "hgrn_lb_logits": dlb_logits,
        "hgrn_gnorm": jnp.sum(dgn_heads, axis=0),
        "lru_wa": _diag_blocks2(dwa_bd),
        "lru_wx": _diag_blocks2(dwx_bd),
    }
    dmeta = jnp.transpose(dh0[:N_META].reshape(N_META, N_SHARD, 2, LANE), (1, 0, 2, 3)).reshape(N_SHARD, 32, LANE)
    packs = [_pack_rows(rep, REP_SPEC)]
    for s in range(N_SHARD):
        sh = {
            "meta_tokens": dmeta[s], "conv_w": dconv_w[s], "lru_conv_w": dlru_vec[s, 0:4],
            "conv_b": dconv_vec[s, 0:1], "conv_ln_g": dconv_vec[s, 1:2], "conv_ln_b": dconv_vec[s, 2:3],
            "lru_conv_b": dlru_vec[s, 4:5], "lru_ba": dlru_vec[s, 5:6], "lru_bx": dlru_vec[s, 6:7],
            "lru_lambda": dlru_vec[s, 7:8],
        }
        packs.append(_pack_rows(sh, SH_SPEC))
    return loss, grad_x, big, jnp.concatenate(packs, axis=0)


def _block_diag2(heads):
    nb = heads.shape[0] // 2
    return jnp.stack([_block_diag(heads[2 * j:2 * j + 2]) for j in range(nb)])


def _diag_blocks2(mats):
    return jnp.concatenate([_diag_blocks(mats[j], 2) for j in range(mats.shape[0])], axis=0)


def _kernel_v2(x, meta_tokens, ffn1_norm, ffn1_wg, ffn1_wu, ffn1_wd, mix_norm, ffn2_norm, ffn2_wg, ffn2_wu, ffn2_wd, w_in_even, pool_w, pool_scale, hgrn_lb_logits, hgrn_gnorm, w_out_even, w_in_odd, conv_w, conv_b, conv_ln_g, conv_ln_b, lru_conv_w, lru_conv_b, lru_wa, lru_ba, lru_wx, lru_bx, lru_lambda, w_out_odd, final_norm, loss_target, m_meta_tokens, m_ffn1_norm, m_ffn1_wg, m_ffn1_wu, m_ffn1_wd, m_mix_norm, m_ffn2_norm, m_ffn2_wg, m_ffn2_wu, m_ffn2_wd, m_w_in_even, m_pool_w, m_pool_scale, m_hgrn_lb_logits, m_hgrn_gnorm, m_w_out_even, m_w_in_odd, m_conv_w, m_conv_b, m_conv_ln_g, m_conv_ln_b, m_lru_conv_w, m_lru_conv_b, m_lru_wa, m_lru_ba, m_lru_wx, m_lru_bx, m_lru_lambda, m_w_out_odd, m_final_norm, v_meta_tokens, v_ffn1_norm, v_ffn1_wg, v_ffn1_wu, v_ffn1_wd, v_mix_norm, v_ffn2_norm, v_ffn2_wg, v_ffn2_wu, v_ffn2_wd, v_w_in_even, v_pool_w, v_pool_scale, v_hgrn_lb_logits, v_hgrn_gnorm, v_w_out_even, v_w_in_odd, v_conv_w, v_conv_b, v_conv_ln_g, v_conv_ln_b, v_lru_conv_w, v_lru_conv_b, v_lru_wa, v_lru_ba, v_lru_wx, v_lru_bx, v_lru_lambda, v_w_out_odd, v_final_norm):
    args = locals()
    w = {n: args[n] for n in WEIGHT_NAMES}
    m = {n: args["m_" + n] for n in WEIGHT_NAMES}
    v = {n: args["v_" + n] for n in WEIGHT_NAMES}
    shapes = {n: w[n].shape for n in WEIGHT_NAMES}

    big_in = [_rows2d(w[n]).astype(BF16) for n in BIG]
    small_sh = _pack_rows(w, SH_SPEC)
    gath = _allgather_shards(big_in + [small_sh], [True] * len(BIG) + [False])
    gathered = dict(zip(BIG, gath[:len(BIG)]))
    sm = gath[len(BIG)]
    sh_shapes = {n: (N_SHARD,) + tuple(shapes[n]) for n, _ in SH_SPEC}
    per_shard = [_unpack_rows(sm[s], SH_SPEC, shapes) for s in range(N_SHARD)]
    small_full = {}
    for n, _ in SH_SPEC:
        stacked = [per_shard[s][n] for s in range(N_SHARD)]
        small_full[n] = jnp.concatenate([p.reshape(-1, p.shape[-1]) for p in stacked], axis=-1)
    small_full["conv_w"] = jnp.concatenate(
        [small_full["conv_w"], jnp.zeros((CONV_HALO - CONV_WIDTH, D_CONV), F32)], axis=0)

    loss, grad_x, big, small_part = _local_step(x[0], loss_target[0], w, gathered, small_full)
    loss = lax.psum(loss[0, 0], ("x", "y", "c"))

    core = lax.axis_index("c").astype(jnp.int32).reshape(1)
    parts = [big[n] for n in BIG]
    recvd = _pair_exchange_halves(parts)
    pair = _pair_add(parts, recvd, core)
    chip = (2 * lax.axis_index("x") + lax.axis_index("y")).astype(jnp.int32).reshape(1)
    slots = _scatter_to_owners(pair, _own_slot(pair, chip))
    halves = _sum_chips(slots, core)
    full = _pair_allgather_halves(halves)
    out_g, out_d, out_m, out_v = {}, {}, {}, {}
    for n, g in zip(BIG, full):
        res = _adamw(_rows2d(w[n]), _rows2d(m[n]), _rows2d(v[n]), g, 0, ADAM_BLOCKS[n])
        out_g[n], out_d[n], out_m[n], out_v[n] = [r.reshape(shapes[n]) for r in res]

    gathered_small = _allgather_all(small_part)

    def pack_small(src):
        return jnp.concatenate([_pack_rows(src, REP_SPEC), _pack_rows(src, SH_SPEC)], axis=0)

    res = _small_reduce_adamw(gathered_small, pack_small(w), pack_small(m), pack_small(v), REP_ROWS, SH_ROWS)
    for dst, packed in zip((out_g, out_d, out_m, out_v), res):
        dst.update(_unpack_rows(packed[:REP_ROWS], REP_SPEC, shapes))
        dst.update(_unpack_rows(packed[REP_ROWS:], SH_SPEC, shapes))

    return (loss, grad_x[None], *[out_g[n] for n in WEIGHT_NAMES], *[out_d[n] for n in WEIGHT_NAMES],
            *[out_m[n] for n in WEIGHT_NAMES], *[out_v[n] for n in WEIGHT_NAMES])


GATHER_GROUPS = (
    (("small", 0),),
    (("ffn1_wg", 0), ("ffn1_wu", 0), ("ffn1_wd", 0)),
    (("w_in_even", 0), ("w_out_even", 0)),
    (("ffn2_wg", 0), ("ffn2_wu", 0), ("ffn2_wd", 0)),
    (("ffn1_wg", 1), ("ffn1_wu", 1), ("ffn1_wd", 1)),
    (("w_in_odd", 0), ("w_out_odd", 0)),
    (("ffn2_wg", 1), ("ffn2_wu", 1), ("ffn2_wd", 1)),
)
ADAM_ROW_BLOCKS = {"ffn1_wg": 2, "ffn1_wu": 2, "ffn2_wg": 2, "ffn2_wu": 2, "ffn1_wd": 2, "ffn2_wd": 2,
                   "w_in_even": 4, "w_out_even": 2, "w_in_odd": 4, "w_out_odd": 2}
TRANSPOSED = ("ffn1_wg", "ffn1_wu", "ffn2_wg", "ffn2_wu", "w_in_even")
SCATTER_DEPTH = 2
LOSS_ROWS = 8
GATHER_SPLIT = (1, 4)


def _unpack_small(sm, shapes):
    per_shard = [_unpack_rows(sm[s], SH_SPEC, shapes) for s in range(N_SHARD)]
    full = {}
    for n, _ in SH_SPEC:
        full[n] = jnp.concatenate([per_shard[s][n].reshape(-1, shapes[n][-1]) for s in range(N_SHARD)], axis=-1)
    full["conv_w"] = jnp.concatenate([full["conv_w"], jnp.zeros((CONV_HALO - CONV_WIDTH, D_CONV), F32)], axis=0)
    return full


def _local_step(x, tgt, w, shapes, fetch, emit, emit_small):
    s_len, d = x.shape
    t_real = s_len + N_META
    tp = -(-t_real // ROW_ALIGN) * ROW_ALIGN
    tm = _tile(tp, 832, ROW_ALIGN)
    tm_small = _tile(tp, 832, 16)
    tr = _tile(tp, 416, SUBLANE)

    def gain(name, layer):
        return w[name][layer:layer + 1]

    pool_wbd = _block_diag(w["pool_w"][0]).astype(BF16)
    pool_scale = w["pool_scale"]
    wa_bd = _block_diag2(w["lru_wa"][0]).astype(BF16)
    wx_bd = _block_diag2(w["lru_wx"][0]).astype(BF16)
    mst, msk, n_lev = _hgrn_consts(HG_CHUNK)

    (sm,) = fetch(0, None)
    sf = _unpack_small(sm, shapes)
    h0 = jnp.concatenate([sf["meta_tokens"], x, jnp.zeros((tp - t_real, d), F32)], axis=0)
    tgt_pad = jnp.concatenate([jnp.zeros((N_META, d), F32), tgt, jnp.zeros((tp - t_real, d), F32)], axis=0)
    f1l0 = fetch(1, h0)
    h1, *s1 = _ffn_fwd(h0, gain("ffn1_norm", 0), *f1l0, 0, tm)
    w_in_even4, w_out_even4 = fetch(2, h1)
    w_out_even = w_out_even4.reshape(d, d)
    even_piece = [(w_in_even4.reshape(D_IN_EVEN, d), (D_IN_EVEN, d), (0, 0))]
    p0, nm0 = _proj_fwd(h1, gain("mix_norm", 0), 0, even_piece, tm_small, wt=True)
    ya = _pool_fwd(p0, pool_wbd, pool_scale, tr)
    yb, states = _hgrn_fwd(p0, w["hgrn_lb_logits"], w["hgrn_gnorm"], mst, msk, n_lev, tm)
    h2 = _out_fwd(h1, ya, yb, w_out_even, tm)
    f2l0 = fetch(3, h2)
    h3, *s2 = _ffn_fwd(h2, gain("ffn2_norm", 0), *f2l0, 0, tm)
    f1l1 = fetch(4, h3)
    h4, *s3 = _ffn_fwd(h3, gain("ffn1_norm", 1), *f1l1, 0, tm)
    w_in_odd4, w_out_odd4 = fetch(5, h4)
    w_out_odd = w_out_odd4.reshape(d, d)
    odd_pieces = [(w_in_odd4, (1, d, D_IN_ODD // N_SHARD), (k, 0, 0)) for k in range(N_SHARD)]
    p1, nm1 = _proj_fwd(h4, gain("mix_norm", 1), 1, odd_pieces, tm_small)
    yc = _convmod_fwd(p1, sf["conv_w"], sf["conv_b"], sf["conv_ln_g"], sf["conv_ln_b"], tr)
    lru_args = (sf["lru_conv_w"], sf["lru_conv_b"], wa_bd, sf["lru_ba"], wx_bd, sf["lru_bx"], sf["lru_lambda"])
    yd, hs = _lru_fwd(p1, *lru_args)
    h5 = _out_fwd(h4, yc, yd, w_out_odd, tm)
    f2l1 = fetch(6, h5)
    h6, *s4 = _ffn_fwd(h5, gain("ffn2_norm", 1), *f2l1, 0, tm)
    loss, dh6, dg_final = _loss_bwd(h6, w["final_norm"].reshape(1, d), tgt_pad, t_real, tm)

    def ffn_bwd(dho, h, saved, norm, wts, after=()):
        ga, gb, sa, n = saved
        dh, da, db, dg, dy = _ffn_bwd_act(dho, h, norm, ga, gb, *wts, 0, tm, after)
        return dh, dg, _ffn_bwd_w([(da, n, None), (db, n, None), (sa, dy, None)], tm)

    dh5, dg_f2_l1, g = ffn_bwd(dh6, h5, s4, gain("ffn2_norm", 1), f2l1)
    sent = emit((("ffn2_wg", 1), ("ffn2_wu", 1), ("ffn2_wd", 1)), g)
    dyc, dyd, dw_out_odd = _out_bwd(dh5, yc, yd, w_out_odd, tm, tuple(sent))
    dca, dcb, dconv_w, dconv_vec = _convmod_bwd(p1, dyc, sf["conv_w"], sf["conv_b"], sf["conv_ln_g"],
                                                sf["conv_ln_b"], tr)
    dlx, dlg, dwa_bd, dwx_bd, dlru_vec = _lru_bwd(p1, hs, dyd, *lru_args)
    dp1 = [dca, dcb, dlx, dlg]
    dh4, dg_mix_l1 = _proj_bwd_act(dh5, h4, gain("mix_norm", 1), 1, dp1, odd_pieces, tm_small)
    dw_in_odd = jnp.stack(_proj_bwd_w(nm1, dp1, tm))
    dh3, dg_f1_l1, g = ffn_bwd(dh4, h3, s3, gain("ffn1_norm", 1), f1l1)
    sent = emit((("w_out_odd", 0), ("w_in_odd", 0), ("ffn1_wg", 1), ("ffn1_wu", 1), ("ffn1_wd", 1)),
                [dw_out_odd.reshape(N_SHARD, d // N_SHARD, d), dw_in_odd] + list(g))
    dh2, dg_f2_l0, g_f2l0 = ffn_bwd(dh3, h2, s2, gain("ffn2_norm", 0), f2l0, tuple(sent))
    dya, dyb, dw_out_even = _out_bwd(dh2, ya, yb, w_out_even, tm)
    dpool, dpool_wbd, dpool_scale = _pool_bwd(p0, dya, pool_wbd, pool_scale, tr)
    dq, dz, dv, dgate, dlb_logits, dgn_heads = _hgrn_bwd(p0, dyb, states, w["hgrn_lb_logits"], w["hgrn_gnorm"],
                                                         mst, msk, n_lev, tm)
    dp0 = [dpool, dq, dz, dv, dgate]
    dh1, dg_mix_l0 = _proj_bwd_act(dh2, h1, gain("mix_norm", 0), 0, dp0, even_piece, tm_small, wt=True)
    dw_in_even_t = jnp.concatenate(_proj_bwd_w(nm0, dp0, tm_small, wt=True), axis=0)
    ga, gb, sa, n1 = s1
    (dwd_f1l0,) = _ffn_bwd_w([(sa, dh1, 0.5)], tm)
    sent = emit((("ffn2_wg", 0), ("ffn2_wu", 0), ("ffn2_wd", 0), ("w_out_even", 0), ("w_in_even", 0),
                 ("ffn1_wd", 0)),
                list(g_f2l0) + [dw_out_even.reshape(N_SHARD, d // N_SHARD, d),
                                dw_in_even_t.reshape(N_SHARD, D_IN_EVEN // N_SHARD, d), dwd_f1l0])
    dh0, da, db, dg_f1_l0, _ = _ffn_bwd_act(dh1, h0, gain("ffn1_norm", 0), ga, gb, *f1l0, 0, tm, tuple(sent))

    grad_x = dh0[N_META:t_real]
    rep = {
        "ffn1_norm": jnp.concatenate([dg_f1_l0, dg_f1_l1], axis=0),
        "mix_norm": jnp.concatenate([dg_mix_l0, dg_mix_l1], axis=0),
        "ffn2_norm": jnp.concatenate([dg_f2_l0, dg_f2_l1], axis=0),
        "final_norm": dg_final,
        "pool_w": _diag_blocks(dpool_wbd, len(POOL_WINDOWS)),
        "pool_scale": dpool_scale,
        "hgrn_lb_logits": dlb_logits,
        "hgrn_gnorm": jnp.sum(dgn_heads, axis=0),
        "lru_wa": _diag_blocks2(dwa_bd),
        "lru_wx": _diag_blocks2(dwx_bd),
    }
    dmeta = jnp.transpose(dh0[:N_META].reshape(N_META, N_SHARD, 2, LANE), (1, 0, 2, 3)).reshape(N_SHARD, 32, LANE)
    packs = [_pack_rows(rep, REP_SPEC)]
    for s in range(N_SHARD):
        sh = {
            "meta_tokens": dmeta[s], "conv_w": dconv_w[s], "lru_conv_w": dlru_vec[s, 0:4],
            "conv_b": dconv_vec[s, 0:1], "conv_ln_g": dconv_vec[s, 1:2], "conv_ln_b": dconv_vec[s, 2:3],
            "lru_conv_b": dlru_vec[s, 4:5], "lru_ba": dlru_vec[s, 5:6], "lru_bx": dlru_vec[s, 6:7],
            "lru_lambda": dlru_vec[s, 7:8],
        }
        packs.append(_pack_rows(sh, SH_SPEC))
    packs.append(jnp.pad(loss, ((0, LOSS_ROWS - 1), (0, LANE - 1))))
    sent = emit_small(jnp.concatenate(packs, axis=0))
    emit((("ffn1_wg", 0), ("ffn1_wu", 0)), _ffn_bwd_w([(da, n1, None), (db, n1, None)], tm, tuple(sent)))
    return grad_x


def kernel(x, meta_tokens, ffn1_norm, ffn1_wg, ffn1_wu, ffn1_wd, mix_norm, ffn2_norm, ffn2_wg, ffn2_wu, ffn2_wd, w_in_even, pool_w, pool_scale, hgrn_lb_logits, hgrn_gnorm, w_out_even, w_in_odd, conv_w, conv_b, conv_ln_g, conv_ln_b, lru_conv_w, lru_conv_b, lru_wa, lru_ba, lru_wx, lru_bx, lru_lambda, w_out_odd, final_norm, loss_target, m_meta_tokens, m_ffn1_norm, m_ffn1_wg, m_ffn1_wu, m_ffn1_wd, m_mix_norm, m_ffn2_norm, m_ffn2_wg, m_ffn2_wu, m_ffn2_wd, m_w_in_even, m_pool_w, m_pool_scale, m_hgrn_lb_logits, m_hgrn_gnorm, m_w_out_even, m_w_in_odd, m_conv_w, m_conv_b, m_conv_ln_g, m_conv_ln_b, m_lru_conv_w, m_lru_conv_b, m_lru_wa, m_lru_ba, m_lru_wx, m_lru_bx, m_lru_lambda, m_w_out_odd, m_final_norm, v_meta_tokens, v_ffn1_norm, v_ffn1_wg, v_ffn1_wu, v_ffn1_wd, v_mix_norm, v_ffn2_norm, v_ffn2_wg, v_ffn2_wu, v_ffn2_wd, v_w_in_even, v_pool_w, v_pool_scale, v_hgrn_lb_logits, v_hgrn_gnorm, v_w_out_even, v_w_in_odd, v_conv_w, v_conv_b, v_conv_ln_g, v_conv_ln_b, v_lru_conv_w, v_lru_conv_b, v_lru_wa, v_lru_ba, v_lru_wx, v_lru_bx, v_lru_lambda, v_w_out_odd, v_final_norm):
    args = locals()
    w = {n: args[n] for n in WEIGHT_NAMES}
    m = {n: args["m_" + n] for n in WEIGHT_NAMES}
    v = {n: args["v_" + n] for n in WEIGHT_NAMES}
    shapes = {n: w[n].shape for n in WEIGHT_NAMES}
    core = lax.axis_index("c").astype(jnp.int32).reshape(1)
    chip = (2 * lax.axis_index("x") + lax.axis_index("y")).astype(jnp.int32).reshape(1)
    me = 2 * chip + core

    def view(a, n):
        return jnp.swapaxes(a, 1, 2) if n in TRANSPOSED else a

    wv, mv, vv = [{n: view(src[n], n) for n in BIG} for src in (w, m, v)]

    def shard(key):
        n, l = key
        return _pack_rows(w, SH_SPEC) if n == "small" else wv[n][l].astype(BF16)

    started = {}
    for groups in (GATHER_GROUPS[:2], GATHER_GROUPS[2:]):
        gkeys = [key for grp in groups for key in grp]
        ssem, rsem, srcs, lands, token = _gather_start([shard(key) for key in gkeys],
                                                       [any(key in GATHER_GROUPS[g] for g in GATHER_SPLIT)
                                                        for key in gkeys])
        for k, key in enumerate(gkeys):
            started[key] = (ssem, rsem, srcs[k], lands[k], k, token)

    def pack_small(src):
        return jnp.concatenate([_pack_rows(src, REP_SPEC), _pack_rows(src, SH_SPEC)], axis=0)

    small_packs = [pack_small(src) for src in (w, m, v)]

    def fetch(group, after):
        st = [started[key] for key in GATHER_GROUPS[group]]
        deps = (st[0][5],) if after is None else (after,)
        if group == 1:
            deps += (started[GATHER_GROUPS[2][0]][5],) + tuple(small_packs)
        split = group in GATHER_SPLIT
        got = _gather_wait(st[0][0], st[0][1], [s[2] for s in st], [s[3] for s in st], [s[4] for s in st], deps,
                           split)
        return _pair_forward(got) if split else got

    in_flight, reduced = [], {}

    def collect(entry, after):
        gkeys, gs_sem, gr_sem, grads_thru, slots_thru, _ = entry
        slots = _reduce_wait(gs_sem, gr_sem, grads_thru, slots_thru, after)
        full = _pair_allgather_halves(_sum_devices(slots, core))
        reduced.update(zip(gkeys, full))
        return full[0]

    def emit(gkeys, grads):
        grads = list(grads)
        in_flight.append((gkeys,) + tuple(_reduce_start(grads, _own_part(grads, chip, core, me))))
        token = in_flight[-1][-1]
        if len(in_flight) > SCATTER_DEPTH:
            return token, collect(in_flight[-1 - SCATTER_DEPTH], (token,))
        return (token,)

    small_flight = []

    def emit_small(part):
        small_flight.append(_small_start(part, _small_own(part, me)))
        return (small_flight[0][4],)

    grad_x = _local_step(x[0], loss_target[0], w, shapes, fetch, emit, emit_small)

    out_g, out_d, out_m, out_v = {}, {}, {}, {}
    deps = (in_flight[-1][-1],)

    def adamw_ready():
        done = ()
        for n in BIG:
            layers = range(shapes[n][0])
            if n not in out_g and all((n, l) in reduced for l in layers):
                res = _adamw(wv[n], mv[n], vv[n], [reduced[(n, l)] for l in layers], ADAM_ROW_BLOCKS[n])
                out_g[n], out_d[n], out_m[n], out_v[n] = [view(r, n) for r in res]
                done += (res[1],)
        return done

    for entry in in_flight[-SCATTER_DEPTH:-1]:
        collect(entry, deps)
        deps += adamw_ready()
    s_ssem, s_rsem, s_part, s_slots, _ = small_flight[0]
    small_all = _small_wait(s_ssem, s_rsem, s_part, s_slots, deps)
    small_res = _small_reduce_adamw(small_all.reshape(-1, LANE), *small_packs, REP_ROWS, SH_ROWS)
    collect(in_flight[-1], deps + (small_res[0],))
    adamw_ready()

    loss = small_res[4][0, 0]
    for dst, packed in zip((out_g, out_d, out_m, out_v), small_res[:4]):
        dst.update(_unpack_rows(packed[:REP_ROWS], REP_SPEC, shapes))
        dst.update(_unpack_rows(packed[REP_ROWS:], SH_SPEC, shapes))

    return (loss, grad_x[None], *[out_g[n] for n in WEIGHT_NAMES], *[out_d[n] for n in WEIGHT_NAMES],
            *[out_m[n] for n in WEIGHT_NAMES], *[out_v[n] for n in WEIGHT_NAMES])
```

```python
import functools

import numpy as np
import jax
import jax.numpy as jnp
from jax import lax
from jax.experimental import pallas as pl
from jax.experimental.pallas import tpu as pltpu

F32 = jnp.float32
BF16 = jnp.bfloat16
MESH = pl.DeviceIdType.MESH

EPS = 1e-6
N_META = 16
D_MODEL = 1024
D_FF = 2816
N_SHARD = 4
FF_SHARD = D_FF // N_SHARD
D_POOL = 256
POOL_GROUP = 64
POOL_WINDOWS = (2, 4, 8, 16)
D_HGRN = 768
HG_HEADS = 6
HEAD = 128
HG_CHUNK = 64
HG_HEADS_PER_STEP = 6
HG_PBLOCK = 256
D_IN_EVEN = D_POOL + 4 * D_HGRN
D_CONV = 512
CONV_WIDTH = 31
CONV_HALO = 32
D_LRU = 512
LRU_CONV = 4
LRU_HALO = 8
LRU_C = 8.0
D_IN_ODD = 2 * D_CONV + 2 * D_LRU
SUBLANE = 8
ROW_ALIGN = 64

ADAM_LR = 0.001
ADAM_B1 = 0.9
ADAM_B2 = 0.999
ADAM_EPS = 1e-08
ADAM_WD = 0.01
ADAM_STEP = 10

VMEM_LIMIT_MB = 56


def _cparams(n_grid_axes=0, vmem_mb=VMEM_LIMIT_MB):
    sem = ("arbitrary",) * n_grid_axes if n_grid_axes else None
    return pltpu.CompilerParams(dimension_semantics=sem, vmem_limit_bytes=vmem_mb * 1024 * 1024)


def _tile(n, target, mult):
    best = None
    for t in range(mult, min(n, target) + 1, mult):
        if n % t == 0:
            best = t
    assert best is not None, (n, target, mult)
    return best


def _dot(a, b):
    return jnp.dot(a, b, preferred_element_type=F32)


def _dot_nt(a, b):
    return lax.dot_general(a, b, (((1,), (1,)), ((), ())), preferred_element_type=F32)


def _dot_tn(a, b):
    return lax.dot_general(a, b, (((0,), (0,)), ((), ())), preferred_element_type=F32)


def _sigmoid(x):
    return 1.0 / (1.0 + jnp.exp(-x))


def _colsum(x):
    return jnp.sum(x, axis=0, keepdims=True)


def _rms_stats(h):
    rstd = lax.rsqrt(jnp.mean(h * h, axis=-1, keepdims=True) + EPS)
    return rstd, h * rstd


def _rms_bwd(dn, g, rstd, xhat):
    dng = dn * g
    dh = rstd * (dng - xhat * jnp.mean(dng * xhat, axis=-1, keepdims=True))
    return dh, _colsum(dn * xhat)


def _ffn_fwd(h, norm, wg4, wu4, wd4, layer, tm):
    tp, d = h.shape
    nt = tp // tm

    def body(h_ref, g_ref, wg_ref, wu_ref, wd_ref, ho_ref, ga_ref, gb_ref, sa_ref, n_ref, n_sc, acc):
        s = pl.program_id(1)

        @pl.when(s == 0)
        def _():
            hh = h_ref[...]
            rstd, xhat = _rms_stats(hh)
            n = (xhat * g_ref[...]).astype(BF16)
            n_sc[...] = n
            n_ref[...] = n
            acc[...] = jnp.zeros_like(acc)

        n = n_sc[...]
        a = _dot_nt(n, wg_ref[0])
        b = _dot_nt(n, wu_ref[0])
        sig = _sigmoid(a)
        sil = a * sig
        ga_ref[0] = (sig * (1.0 + a * (1.0 - sig)) * b).astype(BF16)
        gb_ref[0] = sil.astype(BF16)
        sg = (sil * b).astype(BF16)
        sa_ref[0] = sg
        acc[...] += _dot(sg, wd_ref[0])

        @pl.when(s == N_SHARD - 1)
        def _():
            ho_ref[...] = h_ref[...] + 0.5 * acc[...]

    return pl.pallas_call(
        body, name="ffn_fwd",
        grid=(nt, N_SHARD),
        in_specs=[
            pl.BlockSpec((tm, d), lambda i, s: (i, 0)),
            pl.BlockSpec((1, d), lambda i, s: (0, 0)),
            pl.BlockSpec((1, FF_SHARD, d), lambda i, s: (s, layer, 0)),
            pl.BlockSpec((1, FF_SHARD, d), lambda i, s: (s, layer, 0)),
            pl.BlockSpec((1, FF_SHARD, d), lambda i, s: (s, layer, 0)),
        ],
        out_specs=[
            pl.BlockSpec((tm, d), lambda i, s: (i, 0)),
            pl.BlockSpec((1, tm, FF_SHARD), lambda i, s: (s, i, 0)),
            pl.BlockSpec((1, tm, FF_SHARD), lambda i, s: (s, i, 0)),
            pl.BlockSpec((1, tm, FF_SHARD), lambda i, s: (s, i, 0)),
            pl.BlockSpec((tm, d), lambda i, s: (i, 0)),
        ],
        out_shape=[
            jax.ShapeDtypeStruct((tp, d), F32),
            jax.ShapeDtypeStruct((N_SHARD, tp, FF_SHARD), BF16),
            jax.ShapeDtypeStruct((N_SHARD, tp, FF_SHARD), BF16),
            jax.ShapeDtypeStruct((N_SHARD, tp, FF_SHARD), BF16),
            jax.ShapeDtypeStruct((tp, d), BF16),
        ],
        scratch_shapes=[pltpu.VMEM((tm, d), BF16), pltpu.VMEM((tm, d), F32)],
        compiler_params=_cparams(2),
    )(h, norm, wg4, wu4, wd4)


def _ffn_bwd_act(dho, h, norm, ga4, gb4, wg4, wu4, wd4, layer, tm, after=()):
    tp, d = h.shape
    nt = tp // tm

    def body(dho_ref, h_ref, g_ref, ga_ref, gb_ref, wg_ref, wu_ref, wd_ref, *rest):
        dh_ref, da_ref, db_ref, dg_ref, dy_ref, dn_sc = rest[len(after):]
        i = pl.program_id(0)
        s = pl.program_id(1)

        @pl.when(s == 0)
        def _():
            dy_ref[...] = (0.5 * dho_ref[...]).astype(BF16)
            dn_sc[...] = jnp.zeros_like(dn_sc)

        @pl.when((s == 0) & (i == 0))
        def _():
            dg_ref[...] = jnp.zeros_like(dg_ref)

        ds = _dot_nt(dy_ref[...], wd_ref[0])
        da = (ds * ga_ref[0].astype(F32)).astype(BF16)
        db = (ds * gb_ref[0].astype(F32)).astype(BF16)
        da_ref[0] = da
        db_ref[0] = db
        dn_sc[...] += _dot(da, wg_ref[0]) + _dot(db, wu_ref[0])

        @pl.when(s == N_SHARD - 1)
        def _():
            rstd, xhat = _rms_stats(h_ref[...])
            dh, dg = _rms_bwd(dn_sc[...], g_ref[...], rstd, xhat)
            dh_ref[...] = dho_ref[...] + dh
            dg_ref[...] += dg

    return pl.pallas_call(
        body, name="ffn_bwd_act",
        grid=(nt, N_SHARD),
        in_specs=[
            pl.BlockSpec((tm, d), lambda i, s: (i, 0)),
            pl.BlockSpec((tm, d), lambda i, s: (i, 0)),
            pl.BlockSpec((1, d), lambda i, s: (0, 0)),
            pl.BlockSpec((1, tm, FF_SHARD), lambda i, s: (s, i, 0)),
            pl.BlockSpec((1, tm, FF_SHARD), lambda i, s: (s, i, 0)),
            pl.BlockSpec((1, FF_SHARD, d), lambda i, s: (s, layer, 0)),
            pl.BlockSpec((1, FF_SHARD, d), lambda i, s: (s, layer, 0)),
            pl.BlockSpec((1, FF_SHARD, d), lambda i, s: (s, layer, 0)),
        ] + [pl.BlockSpec(memory_space=pl.ANY)] * len(after),
        out_specs=[
            pl.BlockSpec((tm, d), lambda i, s: (i, 0)),
            pl.BlockSpec((1, tm, FF_SHARD), lambda i, s: (s, i, 0)),
            pl.BlockSpec((1, tm, FF_SHARD), lambda i, s: (s, i, 0)),
            pl.BlockSpec((1, d), lambda i, s: (0, 0)),
            pl.BlockSpec((tm, d), lambda i, s: (i, 0)),
        ],
        out_shape=[
            jax.ShapeDtypeStruct((tp, d), F32),
            jax.ShapeDtypeStruct((N_SHARD, tp, FF_SHARD), BF16),
            jax.ShapeDtypeStruct((N_SHARD, tp, FF_SHARD), BF16),
            jax.ShapeDtypeStruct((1, d), F32),
            jax.ShapeDtypeStruct((tp, d), BF16),
        ],
        scratch_shapes=[pltpu.VMEM((tm, d), F32)],
        compiler_params=_cparams(2),
    )(dho, h, norm, ga4, gb4, wg4, wu4, wd4, *after)


def _ffn_bwd_w(pairs, tm, after=()):
    npair = len(pairs)
    tp, d = pairs[0][1].shape
    nt = tp // tm
    rhs_list = []
    for _, rhs, _ in pairs:
        if all(rhs is not r for r in rhs_list):
            rhs_list.append(rhs)
    rhs_of = [[rhs is r for r in rhs_list].index(True) for _, rhs, _ in pairs]
    nrhs = len(rhs_list)

    def body(*refs):
        rhs_refs = refs[:nrhs]
        lhs_refs = refs[nrhs:nrhs + npair]
        rest = refs[nrhs + npair + len(after):]
        out_refs, accs = rest[:npair], rest[npair:]
        i = pl.program_id(1)

        @pl.when(i == 0)
        def _():
            for acc in accs:
                acc[...] = jnp.zeros_like(acc)

        for k, (_, _, scale) in enumerate(pairs):
            rhs = rhs_refs[rhs_of[k]][...]
            if scale is not None:
                rhs = (scale * rhs).astype(BF16)
            accs[k][...] += _dot_tn(lhs_refs[k][0], rhs)

        @pl.when(i == nt - 1)
        def _():
            for k in range(npair):
                out_refs[k][0] = accs[k][...].astype(BF16)

    return pl.pallas_call(
        body, name="ffn_bwd_w",
        grid=(N_SHARD, nt),
        in_specs=[pl.BlockSpec((tm, d), lambda s, i: (i, 0))] * nrhs
        + [pl.BlockSpec((1, tm, FF_SHARD), lambda s, i: (s, i, 0))] * npair
        + [pl.BlockSpec(memory_space=pl.ANY)] * len(after),
        out_specs=[pl.BlockSpec((1, FF_SHARD, d), lambda s, i: (s, 0, 0))] * npair,
        out_shape=[jax.ShapeDtypeStruct((N_SHARD, FF_SHARD, d), BF16)] * npair,
        scratch_shapes=[pltpu.VMEM((FF_SHARD, d), F32)] * npair,
        compiler_params=_cparams(2),
    )(*rhs_list, *[lhs for lhs, _, _ in pairs], *after)


def _proj_fwd(h, norm, layer, w_pieces, tm, wt=False):
    tp, d = h.shape
    widths = [bs[-2] if wt else bs[-1] for _, bs, _ in w_pieces]
    ntot = sum(widths)
    npc = len(w_pieces)

    def body(*refs):
        h_ref, g_ref = refs[:2]
        w_refs = refs[2:2 + npc]
        p_ref, n_ref = refs[2 + npc:]
        rstd, xhat = _rms_stats(h_ref[...])
        n = (xhat * g_ref[...]).astype(BF16)
        n_ref[...] = n
        off = 0
        for k in range(npc):
            w = w_refs[k][...]
            w = w.reshape(w.shape[-2], w.shape[-1])
            p_ref[:, off:off + widths[k]] = _dot_nt(n, w) if wt else _dot(n, w)
            off += widths[k]

    in_specs = [pl.BlockSpec((tm, d), lambda i: (i, 0)), pl.BlockSpec((1, d), lambda i: (0, 0))]
    for _, bs, idx in w_pieces:
        in_specs.append(pl.BlockSpec(bs, functools.partial(lambda i, idx: idx, idx=idx)))
    return pl.pallas_call(
        body, name="proj_fwd",
        grid=(tp // tm,),
        in_specs=in_specs,
        out_specs=[pl.BlockSpec((tm, ntot), lambda i: (i, 0)), pl.BlockSpec((tm, d), lambda i: (i, 0))],
        out_shape=[jax.ShapeDtypeStruct((tp, ntot), F32), jax.ShapeDtypeStruct((tp, d), BF16)],
        compiler_params=_cparams(1),
    )(h, norm, *[w for w, _, _ in w_pieces])


def _proj_bwd_act(dres, h, norm, layer, dp_pieces, w_pieces, tm, wt=False):
    tp, d = h.shape
    npc = len(dp_pieces)
    nw = len(w_pieces)
    assert nw == npc or (nw == 1 and wt)

    def body(*refs):
        dres_ref, h_ref, g_ref = refs[:3]
        dp_refs = refs[3:3 + npc]
        w_refs = refs[3 + npc:3 + npc + nw]
        dh_ref, dg_ref = refs[3 + npc + nw:]
        i = pl.program_id(0)

        @pl.when(i == 0)
        def _():
            dg_ref[...] = jnp.zeros_like(dg_ref)

        dn = None
        off = 0
        for k in range(npc):
            if nw == npc:
                w = w_refs[k][...]
                w = w.reshape(w.shape[-2], w.shape[-1])
            else:
                w = w_refs[0][off:off + dp_pieces[k].shape[1], :]
                off += dp_pieces[k].shape[1]
            t = _dot(dp_refs[k][...], w) if wt else _dot_nt(dp_refs[k][...], w)
            dn = t if dn is None else dn + t
        rstd, xhat = _rms_stats(h_ref[...])
        dh, dg = _rms_bwd(dn, g_ref[...], rstd, xhat)
        dh_ref[...] = dres_ref[...] + dh
        dg_ref[...] += dg

    in_specs = [pl.BlockSpec((tm, d), lambda i: (i, 0)), pl.BlockSpec((tm, d), lambda i: (i, 0)),
                pl.BlockSpec((1, d), lambda i: (0, 0))]
    for dp in dp_pieces:
        in_specs.append(pl.BlockSpec((tm, dp.shape[1]), lambda i: (i, 0)))
    for _, bs, idx in w_pieces:
        in_specs.append(pl.BlockSpec(bs, functools.partial(lambda i, idx: idx, idx=idx)))
    return pl.pallas_call(
        body, name="proj_bwd_act",
        grid=(tp // tm,),
        in_specs=in_specs,
        out_specs=[pl.BlockSpec((tm, d), lambda i: (i, 0)), pl.BlockSpec((1, d), lambda i: (0, 0))],
        out_shape=[jax.ShapeDtypeStruct((tp, d), F32), jax.ShapeDtypeStruct((1, d), F32)],
        compiler_params=_cparams(1),
    )(dres, h, norm, *dp_pieces, *[w for w, _, _ in w_pieces])


def _proj_bwd_w(n, dp_pieces, tm, wt=False):
    tp, d = n.shape
    npc = len(dp_pieces)
    widths = [dp.shape[1] for dp in dp_pieces]
    oshape = (lambda w: (w, d)) if wt else (lambda w: (d, w))

    def body(*refs):
        n_ref = refs[0]
        dp_refs = refs[1:1 + npc]
        o_refs = refs[1 + npc:1 + 2 * npc]
        accs = refs[1 + 2 * npc:]
        i = pl.program_id(0)

        @pl.when(i == 0)
        def _():
            for acc in accs:
                acc[...] = jnp.zeros_like(acc)

        nn = n_ref[...]
        for k in range(npc):
            accs[k][...] += _dot_tn(dp_refs[k][...], nn) if wt else _dot_tn(nn, dp_refs[k][...])

        @pl.when(i == pl.num_programs(0) - 1)
        def _():
            for k in range(npc):
                o_refs[k][...] = accs[k][...].astype(BF16)

    return pl.pallas_call(
        body, name="proj_bwd_w",
        grid=(tp // tm,),
        in_specs=[pl.BlockSpec((tm, d), lambda i: (i, 0))]
        + [pl.BlockSpec((tm, w), lambda i: (i, 0)) for w in widths],
        out_specs=[pl.BlockSpec(oshape(w), lambda i: (0, 0)) for w in widths],
        out_shape=[jax.ShapeDtypeStruct(oshape(w), BF16) for w in widths],
        scratch_shapes=[pltpu.VMEM(oshape(w), F32) for w in widths],
        compiler_params=_cparams(1),
    )(n, *dp_pieces)


def _out_fwd(h, ya, yb, w, tm):
    tp, d = h.shape
    na, nb = ya.shape[1], yb.shape[1]

    def body(h_ref, ya_ref, yb_ref, w_ref, o_ref):
        y = _dot(ya_ref[...].astype(BF16), w_ref[0:na, :]) + _dot(yb_ref[...].astype(BF16), w_ref[na:, :])
        o_ref[...] = h_ref[...] + y

    return pl.pallas_call(
        body, name="out_fwd",
        grid=(tp // tm,),
        in_specs=[pl.BlockSpec((tm, d), lambda i: (i, 0)), pl.BlockSpec((tm, na), lambda i: (i, 0)),
                  pl.BlockSpec((tm, nb), lambda i: (i, 0)), pl.BlockSpec((d, d), lambda i: (0, 0))],
        out_specs=pl.BlockSpec((tm, d), lambda i: (i, 0)),
        out_shape=jax.ShapeDtypeStruct((tp, d), F32),
        compiler_params=_cparams(1),
    )(h, ya, yb, w)


def _out_bwd(dy, ya, yb, w, tm, after=()):
    tp, d = dy.shape
    na, nb = ya.shape[1], yb.shape[1]

    def body(dy_ref, ya_ref, yb_ref, w_ref, *rest):
        da_ref, db_ref, dw_ref, acc = rest[len(after):]
        i = pl.program_id(0)

        @pl.when(i == 0)
        def _():
            acc[...] = jnp.zeros_like(acc)

        dyb16 = dy_ref[...].astype(BF16)
        da_ref[...] = _dot_nt(dyb16, w_ref[0:na, :])
        db_ref[...] = _dot_nt(dyb16, w_ref[na:, :])
        acc[0:na, :] += _dot_tn(ya_ref[...].astype(BF16), dyb16)
        acc[na:, :] += _dot_tn(yb_ref[...].astype(BF16), dyb16)

        @pl.when(i == pl.num_programs(0) - 1)
        def _():
            dw_ref[...] = acc[...].astype(BF16)

    return pl.pallas_call(
        body, name="out_bwd",
        grid=(tp // tm,),
        in_specs=[pl.BlockSpec((tm, d), lambda i: (i, 0)), pl.BlockSpec((tm, na), lambda i: (i, 0)),
                  pl.BlockSpec((tm, nb), lambda i: (i, 0)), pl.BlockSpec((d, d), lambda i: (0, 0))]
        + [pl.BlockSpec(memory_space=pl.ANY)] * len(after),
        out_specs=[pl.BlockSpec((tm, na), lambda i: (i, 0)), pl.BlockSpec((tm, nb), lambda i: (i, 0)),
                   pl.BlockSpec((d, d), lambda i: (0, 0))],
        out_shape=[jax.ShapeDtypeStruct((tp, na), F32), jax.ShapeDtypeStruct((tp, nb), F32),
                   jax.ShapeDtypeStruct((d, d), BF16)],
        scratch_shapes=[pltpu.VMEM((d, d), F32)],
        compiler_params=_cparams(1),
    )(dy, ya, yb, w, *after)


def _loss_bwd(h, gfin, tgt, t_real, tm):
    tp, d = h.shape

    def body(h_ref, g_ref, t_ref, loss_ref, dh_ref, dg_ref):
        i = pl.program_id(0)

        @pl.when(i == 0)
        def _():
            loss_ref[...] = jnp.zeros_like(loss_ref)
            dg_ref[...] = jnp.zeros_like(dg_ref)

        rows = i * tm + lax.broadcasted_iota(jnp.int32, (tm, 1), 0)
        valid = (rows >= N_META) & (rows < t_real)
        rstd, xhat = _rms_stats(h_ref[...])
        g = g_ref[...]
        err = jnp.where(valid, xhat * g - t_ref[...], 0.0)
        e2 = jnp.sum(err * err, axis=1, keepdims=True)
        loss_ref[...] += (0.5 / d) * jnp.sum(e2, axis=0, keepdims=True)
        dy = err * (1.0 / d)
        dh, dg = _rms_bwd(dy, g, rstd, xhat)
        dh_ref[...] = dh
        dg_ref[...] += dg

    return pl.pallas_call(
        body, name="loss_bwd",
        grid=(tp // tm,),
        in_specs=[pl.BlockSpec((tm, d), lambda i: (i, 0)), pl.BlockSpec((1, d), lambda i: (0, 0)),
                  pl.BlockSpec((tm, d), lambda i: (i, 0))],
        out_specs=[pl.BlockSpec((1, 1), lambda i: (0, 0)), pl.BlockSpec((tm, d), lambda i: (i, 0)),
                   pl.BlockSpec((1, d), lambda i: (0, 0))],
        out_shape=[jax.ShapeDtypeStruct((1, 1), F32), jax.ShapeDtypeStruct((tp, d), F32),
                   jax.ShapeDtypeStruct((1, d), F32)],
        compiler_params=_cparams(1),
    )(h, gfin, tgt)


POOL_HALO = 16


def _pool_lane_consts(n_rows):
    lane = lax.broadcasted_iota(jnp.int32, (n_rows, D_POOL), 1)
    grp = lane // POOL_GROUP
    win = jnp.where(grp == 0, 2.0, jnp.where(grp == 1, 4.0, jnp.where(grp == 2, 8.0, 16.0)))
    return grp, win


def _pool_select(grp, s2, s4, s8, s16):
    return jnp.where(grp == 0, s2, jnp.where(grp == 1, s4, jnp.where(grp == 2, s8, s16)))


def _pool_mixed(x, row0, tr):
    n = tr + POOL_HALO
    s2 = x + pltpu.roll(x, 1, 0)
    s4 = s2 + pltpu.roll(s2, 2, 0)
    s8 = s4 + pltpu.roll(s4, 4, 0)
    s16 = s8 + pltpu.roll(s8, 8, 0)
    grp, win = _pool_lane_consts(n)
    rows = row0 - POOL_HALO + lax.broadcasted_iota(jnp.int32, (n, D_POOL), 0)
    cnt = jnp.minimum((rows + 1).astype(F32), win)
    pooled = _pool_select(grp, s2, s4, s8, s16) / jnp.maximum(cnt, 1.0)
    return (pooled - x)[POOL_HALO:, :]


def _pool_fwd(p, wbd, scale, tr):
    tp = p.shape[0]
    nt = tp // tr

    def body(p_ref, w_ref, s_ref, y_ref, usc):
        usc[0:POOL_HALO, :] = jnp.zeros((POOL_HALO, D_POOL), F32)
        usc[POOL_HALO:, :] = p_ref[...]

        def tile(r, carry):
            r0 = pl.multiple_of(r * tr, SUBLANE)
            x = usc[pl.ds(r0, tr + POOL_HALO), :]
            mixed = _pool_mixed(x, r0, tr)
            y_ref[pl.ds(r0, tr), :] = _dot(mixed.astype(BF16), w_ref[...]) * s_ref[...]
            return carry

        lax.fori_loop(0, nt, tile, 0)

    return pl.pallas_call(
        body, name="pool_fwd",
        grid=(1,),
        in_specs=[pl.BlockSpec((tp, D_POOL), lambda i: (0, 0)), pl.BlockSpec((D_POOL, D_POOL), lambda i: (0, 0)),
                  pl.BlockSpec((1, D_POOL), lambda i: (0, 0))],
        out_specs=pl.BlockSpec((tp, D_POOL), lambda i: (0, 0)),
        out_shape=jax.ShapeDtypeStruct((tp, D_POOL), F32),
        scratch_shapes=[pltpu.VMEM((tp + POOL_HALO, D_POOL), F32)],
        compiler_params=_cparams(1),
    )(p, wbd, scale)


def _pool_bwd(p, dya, wbd, scale, tr):
    tp = p.shape[0]
    nt = tp // tr

    def body(p_ref, dy_ref, w_ref, s_ref, du_ref, dw_ref, ds_ref, usc, gsc):
        usc[0:POOL_HALO, :] = jnp.zeros((POOL_HALO, D_POOL), F32)
        usc[POOL_HALO:, :] = p_ref[...]
        gsc[tp:, :] = jnp.zeros((POOL_HALO, D_POOL), F32)
        dw_ref[...] = jnp.zeros_like(dw_ref)
        ds_ref[...] = jnp.zeros_like(ds_ref)
        grp, win = _pool_lane_consts(tr)

        def tile1(r, carry):
            r0 = pl.multiple_of(r * tr, SUBLANE)
            x = usc[pl.ds(r0, tr + POOL_HALO), :]
            mixed = _pool_mixed(x, r0, tr).astype(BF16)
            dy = dy_ref[pl.ds(r0, tr), :]
            dys = (dy * s_ref[...]).astype(BF16)
            ypre = _dot(mixed, w_ref[...])
            ds_ref[...] += _colsum(dy * ypre)
            dw_ref[...] += _dot_tn(mixed, dys)
            dmx = _dot_nt(dys, w_ref[...])
            rows = r0 + lax.broadcasted_iota(jnp.int32, (tr, D_POOL), 0)
            cnt = jnp.minimum((rows + 1).astype(F32), win)
            gsc[pl.ds(r0, tr), :] = dmx / cnt
            return carry

        lax.fori_loop(0, nt, tile1, 0)
        n = tr + POOL_HALO
        grp2, win2 = _pool_lane_consts(n)

        def tile2(r, carry):
            r0 = pl.multiple_of(r * tr, SUBLANE)
            g = gsc[pl.ds(r0, n), :]
            s2 = g + pltpu.roll(g, n - 1, 0)
            s4 = s2 + pltpu.roll(s2, n - 2, 0)
            s8 = s4 + pltpu.roll(s4, n - 4, 0)
            s16 = s8 + pltpu.roll(s8, n - 8, 0)
            pooled_t = _pool_select(grp2, s2, s4, s8, s16)
            rows = r0 + lax.broadcasted_iota(jnp.int32, (n, D_POOL), 0)
            cnt = jnp.minimum((rows + 1).astype(F32), win2)
            du = pooled_t - g * cnt
            du_ref[pl.ds(r0, tr), :] = du[0:tr, :].astype(BF16)
            return carry

        lax.fori_loop(0, nt, tile2, 0)

    return pl.pallas_call(
        body, name="pool_bwd",
        grid=(1,),
        in_specs=[pl.BlockSpec((tp, D_POOL), lambda i: (0, 0)), pl.BlockSpec((tp, D_POOL), lambda i: (0, 0)),
                  pl.BlockSpec((D_POOL, D_POOL), lambda i: (0, 0)), pl.BlockSpec((1, D_POOL), lambda i: (0, 0))],
        out_specs=[pl.BlockSpec((tp, D_POOL), lambda i: (0, 0)), pl.BlockSpec((D_POOL, D_POOL), lambda i: (0, 0)),
                   pl.BlockSpec((1, D_POOL), lambda i: (0, 0))],
        out_shape=[jax.ShapeDtypeStruct((tp, D_POOL), BF16), jax.ShapeDtypeStruct((D_POOL, D_POOL), F32),
                   jax.ShapeDtypeStruct((1, D_POOL), F32)],
        scratch_shapes=[pltpu.VMEM((tp + POOL_HALO, D_POOL), F32), pltpu.VMEM((tp + POOL_HALO, D_POOL), F32)],
        compiler_params=_cparams(1),
    )(p, dya, wbd, scale)


def _hgrn_levels(ch):
    levels = []
    w = ch // 2
    while w >= 1:
        levels.append(w)
        w //= 2
    return levels


def _hgrn_consts(ch):
    t = np.arange(ch)
    tril = t[None, :] <= t[:, None]
    masks = []
    for w in _hgrn_levels(ch):
        blk = t // (2 * w)
        upper = t % (2 * w) >= w
        masks.append(upper[:, None] & (~upper)[None, :] & (blk[:, None] == blk[None, :]))
    masks.append(tril)
    msk = np.stack(masks).astype(np.float32)
    return jnp.asarray(tril.astype(np.float32), BF16), jnp.asarray(msk, F32), len(masks) - 1


def _split3(x):
    hi = x.astype(BF16)
    r1 = x - hi.astype(F32)
    mid = r1.astype(BF16)
    lo = (r1 - mid.astype(F32)).astype(BF16)
    return hi, mid, lo


def _hgrn_exponents(tril, logf):
    ch = logf.shape[0]
    hi, mid, lo = _split3(logf)
    x = _dot(tril, jnp.concatenate([hi, mid, lo], axis=1))
    b = x[:, 0:HEAD] + x[:, HEAD:2 * HEAD] + x[:, 2 * HEAD:3 * HEAD]
    rows = lax.broadcasted_iota(jnp.int32, (ch, HEAD), 0)
    fx = jnp.broadcast_to(b[ch - 1:ch, :], (ch, HEAD)) - b
    lev = []
    for w in _hgrn_levels(ch):
        pos = rows % (2 * w)
        upper = pos >= w
        if w >= SUBLANE:
            parts = [jnp.broadcast_to(b[k * 2 * w + w - 1:k * 2 * w + w, :], (2 * w, HEAD))
                     for k in range(ch // (2 * w))]
            bmid = parts[0] if len(parts) == 1 else jnp.concatenate(parts, axis=0)
            dx = jnp.where(upper, b - bmid, 0.0)
            ex = jnp.where(upper, 0.0, bmid - b)
        else:
            dx = logf
            ex = jnp.zeros_like(logf)
            for i in range(1, w):
                dx = dx + jnp.where(pos >= w + i, pltpu.roll(logf, i, 0), 0.0)
                ex = ex + jnp.where(pos <= w - 1 - i, pltpu.roll(logf, ch - i, 0), 0.0)
            dx = jnp.where(upper, dx, 0.0)
        lev.append((dx, ex))
    return b, fx, lev


def _hgrn_exponents_bwd(tril, d_b, d_fx, d_blast, lev_grads):
    ch = d_b.shape[0]
    rows = lax.broadcasted_iota(jnp.int32, (ch, HEAD), 0)
    db = d_b - d_fx
    dlf = jnp.zeros_like(d_b)
    for w, (ddx, dex) in zip(_hgrn_levels(ch), lev_grads):
        pos = rows % (2 * w)
        upper = pos >= w
        gu = jnp.where(upper, ddx, 0.0)
        if w >= SUBLANE:
            gl = jnp.where(upper, 0.0, dex)
            db = db + gu - gl
            diff = gl - gu
            for k in range(ch // (2 * w)):
                s = _colsum(diff[k * 2 * w:(k + 1) * 2 * w, :])
                db = db + jnp.where(rows == k * 2 * w + w - 1, s, 0.0)
        else:
            dlf = dlf + gu
            for i in range(1, w):
                dlf = dlf + pltpu.roll(jnp.where(pos >= w + i, gu, 0.0), ch - i, 0)
                dlf = dlf + pltpu.roll(jnp.where(pos <= w - 1 - i, dex, 0.0), i, 0)
    db = db + jnp.where(rows == ch - 1, _colsum(d_fx) + d_blast, 0.0)
    hi = db.astype(BF16)
    lo = (db - hi.astype(F32)).astype(BF16)
    d2 = _dot_tn(tril, jnp.concatenate([hi, lo], axis=1))
    return d2[:, 0:HEAD] + d2[:, HEAD:2 * HEAD] + dlf


def _lockstep(gens):
    results = [None] * len(gens)
    live = list(range(len(gens)))
    while live:
        for i in list(live):
            try:
                next(gens[i])
            except StopIteration as stop:
                results[i] = stop.value
                live.remove(i)
    return results


def _hgrn_gates(q_raw, z, lb):
    sz = _sigmoid(z)
    f = lb + (1.0 - lb) * sz
    q = q_raw * _sigmoid(q_raw)
    k = (1.0 - lb) * (1.0 - sz)
    return q, k, f, sz


def _hgrn_intra(q, k, lev, msk_ref, n_lev, ch):
    eye = (lax.broadcasted_iota(jnp.int32, (ch, ch), 0) == lax.broadcasted_iota(jnp.int32, (ch, ch), 1))
    a = jnp.where(eye, jnp.sum(q * k, axis=1, keepdims=True), 0.0)
    ops = []
    for lv in range(n_lev):
        eq = jnp.exp(lev[lv][0])
        ek = jnp.exp(lev[lv][1])
        qd = q * eq
        kd = k * ek
        a = a + msk_ref[lv] * _dot_nt(qd.astype(BF16), kd.astype(BF16))
        ops.append((eq, ek, qd, kd))
        yield
    return a, ops


def _hgrn_fwd(p, lb_logits, gnorm, mst, msk, n_lev, tm):
    tp = p.shape[0]
    ch = HG_CHUNK
    nct = tm // ch
    nt = tp // tm
    nr = mst.shape[0]
    base = D_POOL // HEAD

    hp = HG_HEADS_PER_STEP
    wide = hp * HEAD
    npr = wide // HG_PBLOCK

    def body(*refs):
        p_refs = refs[:4 * npr]
        lg_ref, gn_ref, mst_ref, msk_ref, y_ref, ss_ref, st_sc = refs[4 * npr:]

        @pl.when(pl.program_id(1) == 0)
        def _():
            st_sc[...] = jnp.zeros_like(st_sc)

        lb_all = _sigmoid(lg_ref[0:1, :] - lg_ref[1:2, :])

        def raw(seg, hh, r0):
            per = HG_PBLOCK // HEAD
            return p_refs[seg * npr + hh // per][pl.ds(r0, ch), (hh % per) * HEAD:(hh % per + 1) * HEAD]

        def one_head(hh, c, r0):
            ls = slice(hh * HEAD, (hh + 1) * HEAD)
            q_raw, z, v, g_raw, st = raw(0, hh, r0), raw(1, hh, r0), raw(2, hh, r0), raw(3, hh, r0), st_sc[hh]
            q, k, f, _ = _hgrn_gates(q_raw, z, lb_all[:, ls])
            yield
            b, fx, lev = _hgrn_exponents(mst_ref[...], jnp.log(f))
            yield
            qe = q * jnp.exp(b)
            a, _ = yield from _hgrn_intra(q, k, lev, msk_ref, n_lev, ch)
            v16 = v.astype(BF16)
            o = _dot_nt(qe.astype(BF16), st.astype(BF16)) + _dot(a.astype(BF16), v16)
            kl = k * jnp.exp(fx)
            st_new = st * jnp.exp(b[ch - 1:ch, :]) + _dot_tn(v16, kl.astype(BF16))
            yield
            rstd = lax.rsqrt(jnp.mean(o * o, axis=-1, keepdims=True) + EPS)
            return st, st_new, o * rstd * gn_ref[...] * (g_raw * _sigmoid(g_raw))

        def chunk(c, carry):
            r0 = pl.multiple_of(c * ch, ch)
            results = _lockstep([one_head(hh, c, r0) for hh in range(hp)])
            for hh, (st, st_new, y) in enumerate(results):
                ss_ref[hh, c] = st
                st_sc[hh] = st_new
                y_ref[pl.ds(r0, ch), hh * HEAD:(hh + 1) * HEAD] = y
            return carry

        lax.fori_loop(0, nct, chunk, 0)

    def pspec(seg, part):
        return pl.BlockSpec((tm, HG_PBLOCK),
                            lambda h, i: (i, (base + seg * HG_HEADS) * HEAD // HG_PBLOCK + h * npr + part))

    return pl.pallas_call(
        body, name="hgrn_fwd",
        grid=(HG_HEADS // hp, nt),
        in_specs=[pspec(seg, part) for seg in range(4) for part in range(npr)]
        + [pl.BlockSpec((2, wide), lambda h, i: (0, h)),
           pl.BlockSpec((1, HEAD), lambda h, i: (0, 0)),
           pl.BlockSpec((nr, ch), lambda h, i: (0, 0)),
           pl.BlockSpec((n_lev + 1, ch, ch), lambda h, i: (0, 0, 0))],
        out_specs=[pl.BlockSpec((tm, wide), lambda h, i: (i, h)),
                   pl.BlockSpec((hp, nct, HEAD, HEAD), lambda h, i: (h, i, 0, 0))],
        out_shape=[jax.ShapeDtypeStruct((tp, D_HGRN), F32),
                   jax.ShapeDtypeStruct((HG_HEADS, tp // ch, HEAD, HEAD), F32)],
        scratch_shapes=[pltpu.VMEM((hp, HEAD, HEAD), F32)],
        compiler_params=_cparams(2),
    )(*([p] * (4 * npr)), lb_logits, gnorm, mst, msk)


def _hgrn_bwd(p, dyb, states, lb_logits, gnorm, mst, msk, n_lev, tm):
    tp = p.shape[0]
    ch = HG_CHUNK
    nct = tm // ch
    nt = tp // tm
    nr = mst.shape[0]
    base = D_POOL // HEAD

    hp = HG_HEADS_PER_STEP
    wide = hp * HEAD
    npr = wide // HG_PBLOCK

    def body(*refs):
        p_refs = refs[:4 * npr]
        (dy_ref, ss_ref, lg_ref, gn_ref, mst_ref, msk_ref,
         dq_ref, dz_ref, dv_ref, dg_ref, dlg_ref, dgn_ref, dst_sc, dlb_sc) = refs[4 * npr:]
        ti = pl.program_id(1)

        def raw(seg, hh, r0):
            per = HG_PBLOCK // HEAD
            return p_refs[seg * npr + hh // per][pl.ds(r0, ch), (hh % per) * HEAD:(hh % per + 1) * HEAD]

        @pl.when(ti == 0)
        def _():
            dst_sc[...] = jnp.zeros_like(dst_sc)
            dlb_sc[...] = jnp.zeros_like(dlb_sc)
            dgn_ref[...] = jnp.zeros_like(dgn_ref)

        lb_all = _sigmoid(lg_ref[0:1, :] - lg_ref[1:2, :])
        gn = gn_ref[...]

        def load_head(hh, c, r0):
            ls = slice(hh * HEAD, (hh + 1) * HEAD)
            return (raw(0, hh, r0), raw(1, hh, r0), raw(2, hh, r0), raw(3, hh, r0),
                    dy_ref[pl.ds(r0, ch), ls], ss_ref[hh, c], dst_sc[hh])

        def store_head(hh, r0, res):
            ls = slice(hh * HEAD, (hh + 1) * HEAD)
            dq_raw, dz, dv, dg_raw, dgn, dst_new, dlb = res
            dq_ref[pl.ds(r0, ch), ls] = dq_raw
            dz_ref[pl.ds(r0, ch), ls] = dz
            dv_ref[pl.ds(r0, ch), ls] = dv
            dg_ref[pl.ds(r0, ch), ls] = dg_raw
            dgn_ref[hh] += dgn
            dst_sc[hh] = dst_new
            dlb_sc[:, ls] += dlb

        def one_head(hh, loaded):
            ls = slice(hh * HEAD, (hh + 1) * HEAD)
            lb = lb_all[:, ls]
            q_raw, z, v, g_raw, dy, st, dst = loaded
            q, k, f, sz = _hgrn_gates(q_raw, z, lb)
            yield
            b, fx, lev = _hgrn_exponents(mst_ref[...], jnp.log(f))
            yield
            eb = jnp.exp(b)
            ef = jnp.exp(fx)
            elast = jnp.exp(b[ch - 1:ch, :])
            qe = q * eb
            kl = k * ef
            a, ops = yield from _hgrn_intra(q, k, lev, msk_ref, n_lev, ch)
            v16 = v.astype(BF16)
            st16 = st.astype(BF16)
            qe16 = qe.astype(BF16)
            kl16 = kl.astype(BF16)
            a16 = a.astype(BF16)
            o = _dot_nt(qe16, st16) + _dot(a16, v16)
            yield
            sg = _sigmoid(g_raw)
            rstd = lax.rsqrt(jnp.mean(o * o, axis=-1, keepdims=True) + EPS)
            oh = o * rstd
            dg_out = (dy * oh * gn * (sg * (1.0 + g_raw * (1.0 - sg)))).astype(BF16)
            don = dy * (g_raw * sg)
            dgn = _colsum(don * oh)
            doh = don * gn
            do = rstd * (doh - oh * jnp.mean(doh * oh, axis=-1, keepdims=True))
            do16 = do.astype(BF16)
            dst16 = dst.astype(BF16)
            yield
            dv = _dot_tn(a16, do16) + _dot_nt(kl16, dst16)
            da = msk_ref[n_lev] * _dot_nt(do16, v16)
            dqe = _dot(do16, st16)
            dkl = _dot(v16, dst16)
            dst_new = dst * elast + _dot_tn(do16, qe16)
            yield
            db_last = _colsum(dst * st) * elast
            dad = jnp.sum(do * v, axis=1, keepdims=True)
            dq = dad * k + dqe * eb
            dk = dad * q + dkl * ef
            lev_grads = []
            for lv in range(n_lev):
                eq, ek, qd, kd = ops[lv]
                gl = (msk_ref[lv] * da).astype(BF16)
                dqd = _dot(gl, kd.astype(BF16))
                dkd = _dot_tn(gl, qd.astype(BF16))
                dq = dq + dqd * eq
                dk = dk + dkd * ek
                lev_grads.append((dqd * qd, dkd * kd))
                yield
            dlogf = _hgrn_exponents_bwd(mst_ref[...], dqe * qe, dkl * kl, db_last, lev_grads)
            yield
            sq = _sigmoid(q_raw)
            dq_out = (dq * (sq * (1.0 + q_raw * (1.0 - sq)))).astype(BF16)
            dfk = dlogf / f - dk
            dz_out = (dfk * (1.0 - lb) * sz * (1.0 - sz)).astype(BF16)
            return dq_out, dz_out, dv.astype(BF16), dg_out, dgn, dst_new, _colsum(dfk * (1.0 - sz))

        def chunk(cc, carry):
            c = nct - 1 - cc
            r0 = pl.multiple_of(c * ch, ch)
            loaded = [load_head(hh, c, r0) for hh in range(HG_HEADS_PER_STEP)]
            results = _lockstep([one_head(hh, loaded[hh]) for hh in range(HG_HEADS_PER_STEP)])
            for hh in range(HG_HEADS_PER_STEP):
                store_head(hh, r0, results[hh])
            return carry

        lax.fori_loop(0, nct, chunk, 0, unroll=1)

        @pl.when(ti == nt - 1)
        def _():
            dl0 = dlb_sc[...] * lb_all * (1.0 - lb_all)
            dlg_ref[0:1, :] = dl0
            dlg_ref[1:2, :] = -dl0

    def pspec(seg, part):
        return pl.BlockSpec((tm, HG_PBLOCK), lambda h, i: (
            nt - 1 - i, (base + seg * HG_HEADS) * HEAD // HG_PBLOCK + h * npr + part))

    ospec = pl.BlockSpec((tm, wide), lambda h, i: (nt - 1 - i, h))
    return pl.pallas_call(
        body, name="hgrn_bwd",
        grid=(HG_HEADS // hp, nt),
        in_specs=[pspec(seg, part) for seg in range(4) for part in range(npr)]
        + [ospec, pl.BlockSpec((hp, nct, HEAD, HEAD), lambda h, i: (h, nt - 1 - i, 0, 0)),
           pl.BlockSpec((2, wide), lambda h, i: (0, h)),
           pl.BlockSpec((1, HEAD), lambda h, i: (0, 0)),
           pl.BlockSpec((nr, ch), lambda h, i: (0, 0)),
           pl.BlockSpec((n_lev + 1, ch, ch), lambda h, i: (0, 0, 0))],
        out_specs=[ospec, ospec, ospec, ospec,
                   pl.BlockSpec((2, wide), lambda h, i: (0, h)),
                   pl.BlockSpec((hp, 1, HEAD), lambda h, i: (h, 0, 0))],
        out_shape=[jax.ShapeDtypeStruct((tp, D_HGRN), BF16)] * 4
        + [jax.ShapeDtypeStruct((2, D_HGRN), F32), jax.ShapeDtypeStruct((HG_HEADS, 1, HEAD), F32)],
        scratch_shapes=[pltpu.VMEM((hp, HEAD, HEAD), F32), pltpu.VMEM((1, wide), F32)],
        compiler_params=_cparams(2),
    )(*([p] * (4 * npr)), dyb, states, lb_logits, gnorm, mst, msk)


def _tap_views(x, tr, halo, width):
    subs = {0: x}
    views = []
    for j in range(width):
        tiles, rem = divmod(width - 1 - j, SUBLANE)
        if rem not in subs:
            subs[rem] = pltpu.roll(x, rem, 0)
        start = halo - tiles * SUBLANE
        views.append(subs[rem][start:start + tr, :])
    return views


def _tap_views_t(y, tr, halo, width):
    n = tr + halo
    subs = {0: y}
    views = []
    for j in range(width):
        tiles, rem = divmod(width - 1 - j, SUBLANE)
        if rem not in subs:
            subs[rem] = pltpu.roll(y, n - rem, 0)
        views.append(subs[rem][tiles * SUBLANE:tiles * SUBLANE + tr, :])
    return views


def _weighted_sum(views, w_ref):
    acc = None
    for j, view in enumerate(views):
        term = view * w_ref[j:j + 1, :]
        acc = term if acc is None else acc + term
    return acc


def _conv_taps(x, w_ref, tr, halo, width):
    return _weighted_sum(_tap_views(x, tr, halo, width), w_ref)


def _conv_taps_t(y, w_ref, tr, halo, width):
    return _weighted_sum(_tap_views_t(y, tr, halo, width), w_ref)


def _ln_stats(cv):
    mu = jnp.mean(cv, axis=-1, keepdims=True)
    xc = cv - mu
    rstd = lax.rsqrt(jnp.mean(xc * xc, axis=-1, keepdims=True) + EPS)
    return rstd, xc * rstd


def _convmod_fwd(p, w, bias, ln_g, ln_b, tr):
    tp = p.shape[0]
    nt = tp // tr
    nb = D_CONV // HEAD

    def body(a_ref, b_ref, w_ref, bi_ref, g_ref, be_ref, y_ref, usc):
        usc[0:CONV_HALO, :] = jnp.zeros((CONV_HALO, HEAD), F32)
        usc[CONV_HALO:, :] = a_ref[...] * _sigmoid(b_ref[...])

        def tile(r, carry):
            r0 = pl.multiple_of(r * tr, SUBLANE)
            x = usc[pl.ds(r0, tr + CONV_HALO), :]
            cv = _conv_taps(x, w_ref, tr, CONV_HALO, CONV_WIDTH) + bi_ref[...]
            _, xh = _ln_stats(cv)
            un = xh * g_ref[...] + be_ref[...]
            y_ref[pl.ds(r0, tr), :] = un * _sigmoid(un)
            return carry

        lax.fori_loop(0, nt, tile, 0)

    vec = lambda: pl.BlockSpec((1, HEAD), lambda j: (0, j))
    return pl.pallas_call(
        body, name="convmod_fwd",
        grid=(nb,),
        in_specs=[pl.BlockSpec((tp, HEAD), lambda j: (0, j)), pl.BlockSpec((tp, HEAD), lambda j: (0, nb + j)),
                  pl.BlockSpec((CONV_HALO, HEAD), lambda j: (0, j)), vec(), vec(), vec()],
        out_specs=pl.BlockSpec((tp, HEAD), lambda j: (0, j)),
        out_shape=jax.ShapeDtypeStruct((tp, D_CONV), F32),
        scratch_shapes=[pltpu.VMEM((tp + CONV_HALO, HEAD), F32)],
        compiler_params=_cparams(1),
    )(p, p, w, bias, ln_g, ln_b)


def _convmod_bwd(p, dyc, w, bias, ln_g, ln_b, tr):
    tp = p.shape[0]
    nt = tp // tr
    nb = D_CONV // HEAD

    def body(a_ref, b_ref, dy_ref, w_ref, bi_ref, g_ref, be_ref, da_ref, db_ref, dw_ref, dv_ref, usc, dsc):
        usc[0:CONV_HALO, :] = jnp.zeros((CONV_HALO, HEAD), F32)
        usc[CONV_HALO:, :] = a_ref[...] * _sigmoid(b_ref[...])
        dsc[tp:, :] = jnp.zeros((CONV_HALO, HEAD), F32)
        dw_ref[...] = jnp.zeros_like(dw_ref)
        dv_ref[...] = jnp.zeros_like(dv_ref)

        def tile1(r, carry):
            r0 = pl.multiple_of(r * tr, SUBLANE)
            x = usc[pl.ds(r0, tr + CONV_HALO), :]
            views = _tap_views(x, tr, CONV_HALO, CONV_WIDTH)
            cv = _weighted_sum(views, w_ref) + bi_ref[...]
            rstd, xh = _ln_stats(cv)
            un = xh * g_ref[...] + be_ref[...]
            sg = _sigmoid(un)
            dun = dy_ref[pl.ds(r0, tr), :] * (sg * (1.0 + un * (1.0 - sg)))
            dv_ref[0, 1:2, :] += _colsum(dun * xh)
            dv_ref[0, 2:3, :] += _colsum(dun)
            dxh = dun * g_ref[...]
            dcv = rstd * (dxh - jnp.mean(dxh, axis=-1, keepdims=True)
                          - xh * jnp.mean(dxh * xh, axis=-1, keepdims=True))
            dv_ref[0, 0:1, :] += _colsum(dcv)
            for j in range(CONV_WIDTH):
                dw_ref[0, j:j + 1, :] += _colsum(dcv * views[j])
            dsc[pl.ds(r0, tr), :] = dcv
            return carry

        lax.fori_loop(0, nt, tile1, 0)

        def tile2(r, carry):
            r0 = pl.multiple_of(r * tr, SUBLANE)
            y = dsc[pl.ds(r0, tr + CONV_HALO), :]
            du = _conv_taps_t(y, w_ref, tr, CONV_HALO, CONV_WIDTH)
            a = a_ref[pl.ds(r0, tr), :]
            sb = _sigmoid(b_ref[pl.ds(r0, tr), :])
            da_ref[pl.ds(r0, tr), :] = (du * sb).astype(BF16)
            db_ref[pl.ds(r0, tr), :] = (du * a * sb * (1.0 - sb)).astype(BF16)
            return carry

        lax.fori_loop(0, nt, tile2, 0)

    vec = lambda: pl.BlockSpec((1, HEAD), lambda j: (0, j))
    col = lambda: pl.BlockSpec((tp, HEAD), lambda j: (0, j))
    return pl.pallas_call(
        body, name="convmod_bwd",
        grid=(nb,),
        in_specs=[col(), pl.BlockSpec((tp, HEAD), lambda j: (0, nb + j)), col(),
                  pl.BlockSpec((CONV_HALO, HEAD), lambda j: (0, j)), vec(), vec(), vec()],
        out_specs=[col(), col(), pl.BlockSpec((1, CONV_HALO, HEAD), lambda j: (j, 0, 0)),
                   pl.BlockSpec((1, SUBLANE, HEAD), lambda j: (j, 0, 0))],
        out_shape=[jax.ShapeDtypeStruct((tp, D_CONV), BF16), jax.ShapeDtypeStruct((tp, D_CONV), BF16),
                   jax.ShapeDtypeStruct((nb, CONV_HALO, HEAD), F32), jax.ShapeDtypeStruct((nb, SUBLANE, HEAD), F32)],
        scratch_shapes=[pltpu.VMEM((tp + CONV_HALO, HEAD), F32), pltpu.VMEM((tp + CONV_HALO, HEAD), F32)],
        compiler_params=_cparams(1),
    )(p, p, dyc, w, bias, ln_g, ln_b)


def _log1p_small(y):
    return jnp.where(y < 1e-4, y * (1.0 - 0.5 * y), jnp.log(1.0 + y))


def _softplus(x):
    return jnp.maximum(x, 0.0) + _log1p_small(jnp.exp(-jnp.abs(x)))


def _expm1(x):
    return jnp.where(jnp.abs(x) < 1e-2, x * (1.0 + 0.5 * x * (1.0 + x * (1.0 / 3.0))), jnp.exp(x) - 1.0)


def _gelu_parts(x):
    c = 0.7978845608028654
    inner = c * (x + 0.044715 * x * x * x)
    th = jnp.tanh(inner)
    gelu = 0.5 * x * (1.0 + th)
    dgelu = 0.5 * (1.0 + th) + 0.5 * x * (1.0 - th * th) * c * (1.0 + 3.0 * 0.044715 * x * x)
    return gelu, dgelu


def _lru_gates(x_all, tp, cw_ref, cb_ref, wa_ref, ba_ref, wx_ref, bx_ref, lam_ref):
    u = _conv_taps(x_all, cw_ref, tp, LRU_HALO, LRU_CONV) + cb_ref[...]
    u16 = u.astype(BF16)
    r = _sigmoid(_dot(u16, wa_ref[0]) + ba_ref[...])
    i = _sigmoid(_dot(u16, wx_ref[0]) + bx_ref[...])
    sp = _softplus(-lam_ref[...])
    la = -LRU_C * r * sp
    a = jnp.exp(la)
    mult = jnp.sqrt(-_expm1(2.0 * la))
    return u, r, i, a, mult, sp


def _lru_specs(tp, nb):
    col = lambda k: pl.BlockSpec((tp, HEAD), functools.partial(lambda j, k: (0, k * nb + j), k=k))
    vec = lambda: pl.BlockSpec((1, HEAD), lambda j: (0, j))
    mat = lambda: pl.BlockSpec((1, HEAD, HEAD), lambda j: (j, 0, 0))
    return col, vec, mat


def _lru_fwd(p, cw, cb, wa, ba, wx, bx, lam):
    tp = p.shape[0]
    nb = D_LRU // HEAD
    ng = tp // SUBLANE

    def body(x_ref, gt_ref, cw_ref, cb_ref, wa_ref, ba_ref, wx_ref, bx_ref, lam_ref, y_ref, hs_ref,
             xsc, asc, bsc):
        xsc[0:LRU_HALO, :] = jnp.zeros((LRU_HALO, HEAD), F32)
        xsc[LRU_HALO:, :] = x_ref[...]
        u, r, i, a, mult, _ = _lru_gates(xsc[...], tp, cw_ref, cb_ref, wa_ref, ba_ref, wx_ref, bx_ref, lam_ref)
        rows = lax.broadcasted_iota(jnp.int32, (tp, HEAD), 0)
        b = jnp.where(rows == 0, 1.0, mult) * (i * u)
        sub = rows % SUBLANE
        for k in (1, 2, 4):
            m = sub >= k
            b = jnp.where(m, a * pltpu.roll(b, k, 0) + b, b)
            a = jnp.where(m, a * pltpu.roll(a, k, 0), a)
        asc[...] = a
        bsc[...] = b

        def grp(g, carry):
            r0 = pl.multiple_of(g * SUBLANE, SUBLANE)
            h = bsc[pl.ds(r0, SUBLANE), :] + asc[pl.ds(r0, SUBLANE), :] * carry
            hs_ref[pl.ds(r0, SUBLANE), :] = h
            return jnp.broadcast_to(h[SUBLANE - 1:SUBLANE, :], (SUBLANE, HEAD))

        lax.fori_loop(0, ng, grp, jnp.zeros((SUBLANE, HEAD), F32))
        gelu, _ = _gelu_parts(gt_ref[...])
        y_ref[...] = gelu * hs_ref[...]

    col, vec, mat = _lru_specs(tp, nb)
    return pl.pallas_call(
        body, name="lru_fwd",
        grid=(nb,),
        in_specs=[col(2), col(3), pl.BlockSpec((LRU_CONV, HEAD), lambda j: (0, j)), vec(), mat(), vec(), mat(),
                  vec(), vec()],
        out_specs=[pl.BlockSpec((tp, HEAD), lambda j: (0, j)), pl.BlockSpec((tp, HEAD), lambda j: (0, j))],
        out_shape=[jax.ShapeDtypeStruct((tp, D_LRU), F32), jax.ShapeDtypeStruct((tp, D_LRU), F32)],
        scratch_shapes=[pltpu.VMEM((tp + LRU_HALO, HEAD), F32), pltpu.VMEM((tp, HEAD), F32),
                        pltpu.VMEM((tp, HEAD), F32)],
        compiler_params=_cparams(1),
    )(p, p, cw, cb, wa, ba, wx, bx, lam)


def _lru_bwd(p, hs, dyd, cw, cb, wa, ba, wx, bx, lam):
    tp = p.shape[0]
    nb = D_LRU // HEAD
    ng = tp // SUBLANE

    def body(x_ref, gt_ref, hs_ref, dy_ref, cw_ref, cb_ref, wa_ref, ba_ref, wx_ref, bx_ref, lam_ref,
             dx_ref, dgt_ref, dwa_ref, dwx_ref, dv_ref, xsc, asc, bsc, gsc, dusc):
        xsc[0:LRU_HALO, :] = jnp.zeros((LRU_HALO, HEAD), F32)
        xsc[LRU_HALO:, :] = x_ref[...]
        x_all = xsc[...]
        u, r, i, a, mult, sp = _lru_gates(x_all, tp, cw_ref, cb_ref, wa_ref, ba_ref, wx_ref, bx_ref, lam_ref)
        rows = lax.broadcasted_iota(jnp.int32, (tp, HEAD), 0)
        hs = hs_ref[...]
        dy = dy_ref[...]
        gelu, dgelu = _gelu_parts(gt_ref[...])
        dgt_ref[...] = (dy * hs * dgelu).astype(BF16)
        bb = dy * gelu
        aa = jnp.where(rows == tp - 1, 0.0, pltpu.roll(a, tp - 1, 0))
        sub = rows % SUBLANE
        for k in (1, 2, 4):
            m = sub < SUBLANE - k
            bb = jnp.where(m, aa * pltpu.roll(bb, tp - k, 0) + bb, bb)
            aa = jnp.where(m, aa * pltpu.roll(aa, tp - k, 0), aa)
        asc[...] = aa
        bsc[...] = bb

        def grp(gi, carry):
            g = ng - 1 - gi
            r0 = pl.multiple_of(g * SUBLANE, SUBLANE)
            gg = bsc[pl.ds(r0, SUBLANE), :] + asc[pl.ds(r0, SUBLANE), :] * carry
            gsc[pl.ds(r0, SUBLANE), :] = gg
            return jnp.broadcast_to(gg[0:1, :], (SUBLANE, HEAD))

        lax.fori_loop(0, ng, grp, jnp.zeros((SUBLANE, HEAD), F32))
        g = gsc[...]
        first = rows == 0
        hprev = jnp.where(first, 0.0, pltpu.roll(hs, 1, 0))
        iu = i * u
        d_iu = g * jnp.where(first, 1.0, mult)
        dmult_term = jnp.where(first, 0.0, g * iu * (-(a * a) / mult))
        dla = g * hprev * a + dmult_term
        dr = dla * (-LRU_C) * sp
        dv_ref[0, 7:8, :] = _colsum(dla * (LRU_C * r) * _sigmoid(-lam_ref[...]))
        dpr = dr * r * (1.0 - r)
        dpi = d_iu * u * i * (1.0 - i)
        dv_ref[0, 5:6, :] = _colsum(dpr)
        dv_ref[0, 6:7, :] = _colsum(dpi)
        u16 = u.astype(BF16)
        dpr16 = dpr.astype(BF16)
        dpi16 = dpi.astype(BF16)
        dwa_ref[0] = _dot_tn(u16, dpr16)
        dwx_ref[0] = _dot_tn(u16, dpi16)
        du = d_iu * i + _dot_nt(dpr16, wa_ref[0]) + _dot_nt(dpi16, wx_ref[0])
        dv_ref[0, 4:5, :] = _colsum(du)
        for j in range(LRU_CONV):
            sh = LRU_CONV - 1 - j
            xs = x_all if sh == 0 else pltpu.roll(x_all, sh, 0)
            dv_ref[0, j:j + 1, :] = _colsum(du * xs[LRU_HALO:, :])
        dusc[0:tp, :] = du
        dusc[tp:, :] = jnp.zeros((LRU_HALO, HEAD), F32)
        dx_ref[...] = _conv_taps_t(dusc[...], cw_ref, tp, LRU_HALO, LRU_CONV).astype(BF16)

    col, vec, mat = _lru_specs(tp, nb)
    ocol = lambda: pl.BlockSpec((tp, HEAD), lambda j: (0, j))
    return pl.pallas_call(
        body, name="lru_bwd",
        grid=(nb,),
        in_specs=[col(2), col(3), ocol(), ocol(), pl.BlockSpec((LRU_CONV, HEAD), lambda j: (0, j)), vec(), mat(),
                  vec(), mat(), vec(), vec()],
        out_specs=[ocol(), ocol(), mat(), mat(), pl.BlockSpec((1, SUBLANE, HEAD), lambda j: (j, 0, 0))],
        out_shape=[jax.ShapeDtypeStruct((tp, D_LRU), BF16), jax.ShapeDtypeStruct((tp, D_LRU), BF16),
                   jax.ShapeDtypeStruct((nb, HEAD, HEAD), F32), jax.ShapeDtypeStruct((nb, HEAD, HEAD), F32),
                   jax.ShapeDtypeStruct((nb, SUBLANE, HEAD), F32)],
        scratch_shapes=[pltpu.VMEM((tp + LRU_HALO, HEAD), F32), pltpu.VMEM((tp, HEAD), F32),
                        pltpu.VMEM((tp, HEAD), F32), pltpu.VMEM((tp, HEAD), F32),
                        pltpu.VMEM((tp + LRU_HALO, HEAD), F32)],
        compiler_params=_cparams(1),
    )(p, p, hs, dyd, cw, cb, wa, ba, wx, bx, lam)


def _mesh_pos():
    return lax.axis_index("x"), lax.axis_index("y"), lax.axis_index("c")


def _other_chips(x, y):
    return [(1 - x, y), (x, 1 - y), (1 - x, 1 - y)]


ANY = pl.BlockSpec(memory_space=pl.ANY)


def _allgather_shards(arrs, split):
    n = len(arrs)

    def body(*refs):
        ins, outs = refs[:n], refs[n:2 * n]
        send1, recv1, send2, recv2, send3, recv3 = refs[2 * n:]
        x, y, c = _mesh_pos()
        chip = 2 * x + y
        sibling = (x, y, 1 - c)
        others = _other_chips(x, y)

        def rows(k, cc):
            half = arrs[k].shape[0] // 2
            return pl.ds(cc * half, half)

        def remote(src, dst, ssem, rsem, dev):
            return pltpu.make_async_remote_copy(src_ref=src, dst_ref=dst, send_sem=ssem, recv_sem=rsem,
                                                device_id=dev, device_id_type=MESH)

        started = [remote(ins[k], outs[k].at[chip], send3.at[k], recv3.at[k], sibling) for k in range(n)]
        for cp in started:
            cp.start()
        for k in range(n):
            for j, (ox, oy) in enumerate(others):
                if split[k]:
                    src, dst = ins[k].at[rows(k, c)], outs[k].at[chip, rows(k, c)]
                else:
                    src, dst = ins[k], outs[k].at[chip]
                cp = remote(src, dst, send1.at[3 * k + j], recv1.at[3 * k + j], (ox, oy, c))
                cp.start()
                started.append(cp)
        for j, (ox, oy) in enumerate(others):
            ochip = 2 * ox + oy
            for k in range(n):
                if split[k]:
                    blk = outs[k].at[ochip, rows(k, c)]
                    remote(blk, blk, send1.at[3 * k + j], recv1.at[3 * k + j], sibling).wait_recv()
                    cp = remote(blk, blk, send2.at[3 * k + j], recv2.at[3 * k + j], sibling)
                    cp.start()
                    started.append(cp)
                else:
                    blk = outs[k].at[ochip]
                    remote(blk, blk, send1.at[3 * k + j], recv1.at[3 * k + j], sibling).wait_recv()
        for j, (ox, oy) in enumerate(others):
            ochip = 2 * ox + oy
            for k in range(n):
                if split[k]:
                    blk = outs[k].at[ochip, rows(k, 1 - c)]
                    remote(blk, blk, send2.at[3 * k + j], recv2.at[3 * k + j], sibling).wait_recv()
        for k in range(n):
            blk = outs[k].at[chip]
            remote(blk, blk, send3.at[k], recv3.at[k], sibling).wait_recv()
        for cp in started:
            cp.wait_send()

    return pl.pallas_call(
        body, name="allgather_shards",
        in_specs=[ANY] * n, out_specs=[ANY] * n,
        out_shape=[jax.ShapeDtypeStruct((N_SHARD,) + a.shape, a.dtype) for a in arrs],
        scratch_shapes=[pltpu.SemaphoreType.DMA((3 * n,)), pltpu.SemaphoreType.DMA((3 * n,)),
                        pltpu.SemaphoreType.DMA((3 * n,)), pltpu.SemaphoreType.DMA((3 * n,)),
                        pltpu.SemaphoreType.DMA((n,)), pltpu.SemaphoreType.DMA((n,))],
    )(*arrs)


HBM = pl.BlockSpec(memory_space=pltpu.HBM)
SEM = pl.BlockSpec(memory_space=pltpu.SEMAPHORE)
DATAFLOW = pltpu.SideEffectType.DATAFLOW_SIDE_EFFECTING
N_PEERS = 4


def _in_hbm(a):
    return pltpu.with_memory_space_constraint(a, pltpu.HBM)


def _gather_peers(x, y, c):
    return [((ox, oy, c), 2 * ox + oy) for ox, oy in _other_chips(x, y)] + [((x, y, 1 - c), 2 * x + y)]


def _gather_refs(src, land, slot, c, split):
    if not split:
        return src, land.at[slot]
    half = src.shape[0] // 2
    return src.at[pl.ds(c * half, half)], land.at[slot, pl.ds(c * half, half)]


def _gather_start(arrs, split):
    n = len(arrs)

    def body(*refs):
        ins, lands = refs[:n], refs[n:2 * n]
        ssem, rsem = refs[2 * n:2 * n + 2]
        token = refs[-1]
        x, y, c = _mesh_pos()
        chip = 2 * x + y
        for k in range(n):
            for j, (dev, _) in enumerate(_gather_peers(x, y, c)):
                src, dst = _gather_refs(ins[k], lands[k], chip, c, split[k] and j < N_PEERS - 1)
                pltpu.make_async_remote_copy(
                    src_ref=src, dst_ref=dst, send_sem=ssem.at[N_PEERS * k + j],
                    recv_sem=rsem.at[N_PEERS * k + j], device_id=dev, device_id_type=MESH).start()
        token[...] = jnp.zeros_like(token)

    lands = [_in_hbm(lax.empty((N_SHARD,) + a.shape, a.dtype)) for a in arrs]
    out = pl.pallas_call(
        body, name="gather_start",
        in_specs=[HBM] * (2 * n),
        out_specs=[SEM, SEM] + [HBM] * (2 * n) + [pl.BlockSpec(memory_space=pltpu.VMEM)],
        out_shape=[pltpu.SemaphoreType.DMA((N_PEERS * n,)), pltpu.SemaphoreType.DMA((N_PEERS * n,))]
        + [pltpu.HBM(a.shape, a.dtype) for a in arrs]
        + [pltpu.HBM((N_SHARD,) + a.shape, a.dtype) for a in arrs]
        + [jax.ShapeDtypeStruct((SUBLANE, LANE), F32)],
        input_output_aliases={k: 2 + k for k in range(2 * n)},
        compiler_params=pltpu.CompilerParams(has_side_effects=DATAFLOW),
    )(*[_in_hbm(a) for a in arrs], *lands)
    return out[0], out[1], list(out[2:2 + n]), list(out[2 + n:2 + 2 * n]), out[-1]


def _gather_wait(ssem, rsem, srcs, lands, ks, after, split=False):
    n = len(ks)

    def body(*refs):
        ins, lnd = refs[:n], refs[n:2 * n]
        ssem_ref, rsem_ref = refs[2 * n:2 * n + 2]
        x, y, c = _mesh_pos()
        for i, k in enumerate(ks):
            for j, (dev, pchip) in enumerate(_gather_peers(x, y, c)):
                src, dst = _gather_refs(ins[i], lnd[i], pchip, c, split and j < N_PEERS - 1)
                cp = pltpu.make_async_remote_copy(
                    src_ref=src, dst_ref=dst, send_sem=ssem_ref.at[N_PEERS * k + j],
                    recv_sem=rsem_ref.at[N_PEERS * k + j], device_id=dev, device_id_type=MESH)
                cp.wait_send()
                cp.wait_recv()

    out = pl.pallas_call(
        body, name="gather_wait",
        in_specs=[HBM] * (2 * n) + [SEM, SEM] + [ANY] * len(after),
        out_specs=[HBM] * (2 * n),
        out_shape=[pltpu.HBM(a.shape, a.dtype) for a in srcs] + [pltpu.HBM(a.shape, a.dtype) for a in lands],
        input_output_aliases={k: k for k in range(2 * n)},
        compiler_params=pltpu.CompilerParams(has_side_effects=DATAFLOW),
    )(*srcs, *lands, ssem, rsem, *after)
    return list(out[n:])


def _pair_forward(lands):
    n = len(lands)

    def body(*refs):
        outs = refs[n:2 * n]
        ssem, rsem = refs[2 * n:]
        x, y, c = _mesh_pos()
        sibling = (x, y, 1 - c)
        cps = []
        for k in range(n):
            half = lands[k].shape[1] // 2
            for j, (ox, oy) in enumerate(_other_chips(x, y)):
                mine = outs[k].at[2 * ox + oy, pl.ds(c * half, half)]
                cp = pltpu.make_async_remote_copy(src_ref=mine, dst_ref=mine, send_sem=ssem.at[3 * k + j],
                                                  recv_sem=rsem.at[3 * k + j], device_id=sibling, device_id_type=MESH)
                cp.start()
                cps.append(cp)
        for k in range(n):
            half = lands[k].shape[1] // 2
            for j, (ox, oy) in enumerate(_other_chips(x, y)):
                theirs = outs[k].at[2 * ox + oy, pl.ds((1 - c) * half, half)]
                pltpu.make_async_remote_copy(src_ref=theirs, dst_ref=theirs, send_sem=ssem.at[3 * k + j],
                                             recv_sem=rsem.at[3 * k + j], device_id=sibling,
                                             device_id_type=MESH).wait_recv()
        for cp in cps:
            cp.wait_send()

    return pl.pallas_call(
        body, name="pair_forward",
        in_specs=[ANY] * n, out_specs=[ANY] * n,
        out_shape=[jax.ShapeDtypeStruct(a.shape, a.dtype) for a in lands],
        scratch_shapes=[pltpu.SemaphoreType.DMA((3 * n,)), pltpu.SemaphoreType.DMA((3 * n,))],
        input_output_aliases={k: k for k in range(n)},
    )(*lands)


N_SOURCES = 7


def _reduce_peers(x, y, c):
    peers = []
    for ox, oy in _other_chips(x, y):
        for rel in range(2):
            peers.append(((ox, oy, c + rel - 2 * c * rel), 2 * ox + oy))
    peers.append(((x, y, 1 - c), 2 * x + y))
    return peers


def _reduce_start(arrs, slots):
    n = len(arrs)

    def body(*refs):
        ins, lands = refs[:n], refs[n:2 * n]
        ssem, rsem = refs[2 * n:2 * n + 2]
        token = refs[-1]
        x, y, c = _mesh_pos()
        me = 2 * (2 * x + y) + c
        for k in range(n):
            half = arrs[k].shape[1] // 2
            for p, (dev, ochip) in enumerate(_reduce_peers(x, y, c)):
                pltpu.make_async_remote_copy(
                    src_ref=ins[k].at[ochip, pl.ds(dev[2] * half, half)], dst_ref=lands[k].at[me],
                    send_sem=ssem.at[N_SOURCES * k + p], recv_sem=rsem.at[N_SOURCES * k + p],
                    device_id=dev, device_id_type=MESH).start()
        token[...] = jnp.zeros_like(token)

    out = pl.pallas_call(
        body, name="reduce_start",
        in_specs=[HBM] * (2 * n),
        out_specs=[SEM, SEM] + [HBM] * (2 * n) + [pl.BlockSpec(memory_space=pltpu.VMEM)],
        out_shape=[pltpu.SemaphoreType.DMA((N_SOURCES * n,)), pltpu.SemaphoreType.DMA((N_SOURCES * n,))]
        + [pltpu.HBM(a.shape, a.dtype) for a in arrs] + [pltpu.HBM(a.shape, a.dtype) for a in slots]
        + [jax.ShapeDtypeStruct((SUBLANE, LANE), F32)],
        input_output_aliases={k: 2 + k for k in range(2 * n)},
        compiler_params=pltpu.CompilerParams(has_side_effects=DATAFLOW),
    )(*[_in_hbm(a) for a in arrs], *[_in_hbm(a) for a in slots])
    return out[0], out[1], list(out[2:2 + n]), list(out[2 + n:2 + 2 * n]), out[-1]


def _reduce_wait(ssem, rsem, arrs, slots, after):
    n = len(arrs)

    def body(*refs):
        ins, lnd = refs[:n], refs[n:2 * n]
        ssem_ref, rsem_ref = refs[2 * n:2 * n + 2]
        x, y, c = _mesh_pos()
        for k in range(n):
            half = arrs[k].shape[1] // 2
            for p, (dev, ochip) in enumerate(_reduce_peers(x, y, c)):
                cp = pltpu.make_async_remote_copy(
                    src_ref=ins[k].at[ochip, pl.ds(dev[2] * half, half)], dst_ref=lnd[k].at[2 * ochip + dev[2]],
                    send_sem=ssem_ref.at[N_SOURCES * k + p], recv_sem=rsem_ref.at[N_SOURCES * k + p],
                    device_id=dev, device_id_type=MESH)
                cp.wait_send()
                cp.wait_recv()

    out = pl.pallas_call(
        body, name="reduce_wait",
        in_specs=[HBM] * (2 * n) + [SEM, SEM] + [ANY] * len(after),
        out_specs=[HBM] * (2 * n),
        out_shape=[pltpu.HBM(a.shape, a.dtype) for a in arrs] + [pltpu.HBM(a.shape, a.dtype) for a in slots],
        input_output_aliases={k: k for k in range(2 * n)},
        compiler_params=pltpu.CompilerParams(has_side_effects=DATAFLOW),
    )(*arrs, *slots, ssem, rsem, *after)
    return list(out[n:])


def _own_part(arrs, chip, core, me):
    n = len(arrs)
    nb = GRAD_ROW_BLOCKS

    def body(chip_ref, core_ref, me_ref, *refs):
        for k in range(n):
            refs[n + k][...] = refs[k][...]

    def blk(a):
        return (1, a.shape[1] // 2 // nb, a.shape[2])

    grid_spec = pltpu.PrefetchScalarGridSpec(
        num_scalar_prefetch=3, grid=(nb,),
        in_specs=[pl.BlockSpec(blk(a), lambda i, ch, co, me: (ch[0], co[0] * nb + i, 0)) for a in arrs],
        out_specs=[pl.BlockSpec(blk(a), lambda i, ch, co, me: (me[0], i, 0)) for a in arrs])
    return pl.pallas_call(
        body, name="own_part", grid_spec=grid_spec,
        out_shape=[jax.ShapeDtypeStruct((N_DEV, a.shape[1] // 2, a.shape[2]), a.dtype) for a in arrs],
        compiler_params=_cparams(1),
    )(chip, core, me, *arrs)


def _sum_devices(arrs, core):
    n = len(arrs)
    nb = GRAD_ROW_BLOCKS

    def body(c_ref, *refs):
        for k in range(n):
            r = refs[k]
            acc = r[0].astype(F32)
            for dev in range(1, N_DEV):
                acc = acc + r[dev].astype(F32)
            refs[n + k][...] = acc

    grid_spec = pltpu.PrefetchScalarGridSpec(
        num_scalar_prefetch=1, grid=(nb,),
        in_specs=[pl.BlockSpec((N_DEV, a.shape[1] // nb, a.shape[2]), lambda i, c: (0, i, 0)) for a in arrs],
        out_specs=[pl.BlockSpec((a.shape[1] // nb, a.shape[2]), lambda i, c: (c[0] * nb + i, 0)) for a in arrs])
    return pl.pallas_call(
        body, name="sum_devices", grid_spec=grid_spec,
        out_shape=[jax.ShapeDtypeStruct((2 * a.shape[1], a.shape[2]), F32) for a in arrs],
        compiler_params=_cparams(1),
    )(core, *arrs)


def _small_own(v, me):
    m = v.shape[0]

    def body(me_ref, v_ref, o_ref):
        o_ref[0] = v_ref[...]

    grid_spec = pltpu.PrefetchScalarGridSpec(
        num_scalar_prefetch=1, grid=(1,),
        in_specs=[pl.BlockSpec((m, LANE), lambda i, me: (0, 0))],
        out_specs=pl.BlockSpec((1, m, LANE), lambda i, me: (me[0], 0, 0)))
    return pl.pallas_call(
        body, name="small_own", grid_spec=grid_spec,
        out_shape=jax.ShapeDtypeStruct((N_DEV, m, LANE), v.dtype),
        compiler_params=_cparams(1),
    )(me, v)


def _small_start(v, slots):
    def body(v_ref, land, ssem, rsem, v_thru, land_thru, token):
        del v_thru, land_thru
        x, y, c = _mesh_pos()
        me = 2 * (2 * x + y) + c
        for p, (dev, _) in enumerate(_reduce_peers(x, y, c)):
            pltpu.make_async_remote_copy(src_ref=v_ref, dst_ref=land.at[me], send_sem=ssem.at[p],
                                         recv_sem=rsem.at[p], device_id=dev, device_id_type=MESH).start()
        token[...] = jnp.zeros_like(token)

    out = pl.pallas_call(
        body, name="small_start",
        in_specs=[HBM, HBM],
        out_specs=[SEM, SEM, HBM, HBM, pl.BlockSpec(memory_space=pltpu.VMEM)],
        out_shape=[pltpu.SemaphoreType.DMA((N_SOURCES,)), pltpu.SemaphoreType.DMA((N_SOURCES,)),
                   pltpu.HBM(v.shape, v.dtype), pltpu.HBM(slots.shape, slots.dtype),
                   jax.ShapeDtypeStruct((SUBLANE, LANE), F32)],
        input_output_aliases={0: 2, 1: 3},
        compiler_params=pltpu.CompilerParams(has_side_effects=DATAFLOW),
    )(_in_hbm(v), _in_hbm(slots))
    return out


def _small_wait(ssem, rsem, v, slots, after):
    def body(*refs):
        v_ref, land, ssem_ref, rsem_ref = refs[:4]
        x, y, c = _mesh_pos()
        for p, (dev, ochip) in enumerate(_reduce_peers(x, y, c)):
            cp = pltpu.make_async_remote_copy(src_ref=v_ref, dst_ref=land.at[2 * ochip + dev[2]],
                                              send_sem=ssem_ref.at[p], recv_sem=rsem_ref.at[p],
                                              device_id=dev, device_id_type=MESH)
            cp.wait_send()
            cp.wait_recv()

    out = pl.pallas_call(
        body, name="small_wait",
        in_specs=[HBM, HBM, SEM, SEM] + [ANY] * len(after),
        out_specs=[HBM, HBM],
        out_shape=[pltpu.HBM(v.shape, v.dtype), pltpu.HBM(slots.shape, slots.dtype)],
        input_output_aliases={0: 0, 1: 1},
        compiler_params=pltpu.CompilerParams(has_side_effects=DATAFLOW),
    )(v, slots, ssem, rsem, *after)
    return out[1]


def _scatter_start(arrs, slots):
    n = len(arrs)

    def body(*refs):
        ins, lands = refs[:n], refs[n:2 * n]
        ssem, rsem = refs[2 * n:2 * n + 2]
        token = refs[-1]
        x, y, c = _mesh_pos()
        chip = 2 * x + y
        for k in range(n):
            for j, (ox, oy) in enumerate(_other_chips(x, y)):
                pltpu.make_async_remote_copy(
                    src_ref=ins[k].at[2 * ox + oy], dst_ref=lands[k].at[chip], send_sem=ssem.at[3 * k + j],
                    recv_sem=rsem.at[3 * k + j], device_id=(ox, oy, c), device_id_type=MESH).start()
        token[...] = jnp.zeros_like(token)

    out = pl.pallas_call(
        body, name="scatter_start",
        in_specs=[HBM] * (2 * n),
        out_specs=[SEM, SEM] + [HBM] * (2 * n) + [pl.BlockSpec(memory_space=pltpu.VMEM)],
        out_shape=[pltpu.SemaphoreType.DMA((3 * n,)), pltpu.SemaphoreType.DMA((3 * n,))]
        + [pltpu.HBM(a.shape, a.dtype) for a in arrs] + [pltpu.HBM(a.shape, a.dtype) for a in slots]
        + [jax.ShapeDtypeStruct((SUBLANE, LANE), F32)],
        input_output_aliases={k: 2 + k for k in range(2 * n)},
        compiler_params=pltpu.CompilerParams(has_side_effects=DATAFLOW),
    )(*[_in_hbm(a) for a in arrs], *[_in_hbm(a) for a in slots])
    return out[0], out[1], list(out[2:2 + n]), list(out[2 + n:2 + 2 * n]), out[-1]


def _scatter_wait(ssem, rsem, arrs, slots, after):
    n = len(arrs)

    def body(*refs):
        ins, lnd = refs[:n], refs[n:2 * n]
        ssem_ref, rsem_ref = refs[2 * n:2 * n + 2]
        x, y, c = _mesh_pos()
        for k in range(n):
            for j, (ox, oy) in enumerate(_other_chips(x, y)):
                ochip = 2 * ox + oy
                cp = pltpu.make_async_remote_copy(
                    src_ref=ins[k].at[ochip], dst_ref=lnd[k].at[ochip], send_sem=ssem_ref.at[3 * k + j],
                    recv_sem=rsem_ref.at[3 * k + j], device_id=(ox, oy, c), device_id_type=MESH)
                cp.wait_send()
                cp.wait_recv()

    out = pl.pallas_call(
        body, name="scatter_wait",
        in_specs=[HBM] * (2 * n) + [SEM, SEM] + [ANY] * len(after),
        out_specs=[HBM] * (2 * n),
        out_shape=[pltpu.HBM(a.shape, a.dtype) for a in arrs] + [pltpu.HBM(a.shape, a.dtype) for a in slots],
        input_output_aliases={k: k for k in range(2 * n)},
        compiler_params=pltpu.CompilerParams(has_side_effects=DATAFLOW),
    )(*arrs, *slots, ssem, rsem, *after)
    return list(out[n:])


def _pair_exchange_halves(arrs):
    n = len(arrs)

    def body(*refs):
        ins, outs = refs[:n], refs[n:2 * n]
        ssem, rsem = refs[2 * n:]
        x, y, c = _mesh_pos()
        cps = []
        for k in range(n):
            half = arrs[k].shape[1] // 2
            cp = pltpu.make_async_remote_copy(
                src_ref=ins[k].at[:, pl.ds((1 - c) * half, half)], dst_ref=outs[k],
                send_sem=ssem.at[k], recv_sem=rsem.at[k], device_id=(x, y, 1 - c), device_id_type=MESH)
            cp.start()
            cps.append(cp)
        for cp in cps:
            cp.wait()

    return pl.pallas_call(
        body, name="pair_exchange_halves",
        in_specs=[ANY] * n, out_specs=[ANY] * n,
        out_shape=[jax.ShapeDtypeStruct((a.shape[0], a.shape[1] // 2, a.shape[2]), a.dtype) for a in arrs],
        scratch_shapes=[pltpu.SemaphoreType.DMA((n,)), pltpu.SemaphoreType.DMA((n,))],
    )(*arrs)


GRAD_ROW_BLOCKS = 2


def _pair_add(arrs, recvd, core):
    n = len(arrs)
    nb = GRAD_ROW_BLOCKS

    def body(c_ref, *refs):
        for k in range(n):
            refs[2 * n + k][...] = (refs[k][...].astype(F32) + refs[n + k][...].astype(F32)).astype(BF16)

    def blk(a):
        return (1, a.shape[1] // 2 // nb, a.shape[2])

    grid_spec = pltpu.PrefetchScalarGridSpec(
        num_scalar_prefetch=1, grid=(N_SHARD, nb),
        in_specs=[pl.BlockSpec(blk(a), lambda s, i, c: (s, c[0] * nb + i, 0)) for a in arrs]
        + [pl.BlockSpec(blk(a), lambda s, i, c: (s, i, 0)) for a in arrs],
        out_specs=[pl.BlockSpec(blk(a), lambda s, i, c: (s, i, 0)) for a in arrs])
    return pl.pallas_call(
        body, name="pair_add", grid_spec=grid_spec,
        out_shape=[jax.ShapeDtypeStruct(r.shape, BF16) for r in recvd],
        compiler_params=_cparams(2),
    )(core, *arrs, *recvd)


def _own_slot(arrs, chip):
    n = len(arrs)
    nb = GRAD_ROW_BLOCKS

    def body(c_ref, *refs):
        for k in range(n):
            refs[n + k][...] = refs[k][...]

    def blk(a):
        return (1, a.shape[1] // nb, a.shape[2])

    grid_spec = pltpu.PrefetchScalarGridSpec(
        num_scalar_prefetch=1, grid=(nb,),
        in_specs=[pl.BlockSpec(blk(a), lambda i, c: (c[0], i, 0)) for a in arrs],
        out_specs=[pl.BlockSpec(blk(a), lambda i, c: (c[0], i, 0)) for a in arrs])
    return pl.pallas_call(
        body, name="own_slot", grid_spec=grid_spec,
        out_shape=[jax.ShapeDtypeStruct(a.shape, a.dtype) for a in arrs],
        compiler_params=_cparams(1),
    )(chip, *arrs)


def _scatter_to_owners(arrs, slots):
    n = len(arrs)

    def body(*refs):
        ins, outs = refs[:n], refs[2 * n:3 * n]
        ssem, rsem = refs[3 * n:]
        x, y, c = _mesh_pos()
        chip = 2 * x + y
        others = _other_chips(x, y)
        cps = []
        for k in range(n):
            for j, (ox, oy) in enumerate(others):
                cp = pltpu.make_async_remote_copy(
                    src_ref=ins[k].at[2 * ox + oy], dst_ref=outs[k].at[chip],
                    send_sem=ssem.at[3 * k + j], recv_sem=rsem.at[3 * k + j],
                    device_id=(ox, oy, c), device_id_type=MESH)
                cp.start()
                cps.append(cp)
        for k in range(n):
            for j, (ox, oy) in enumerate(others):
                blk = outs[k].at[2 * ox + oy]
                pltpu.make_async_remote_copy(
                    src_ref=blk, dst_ref=blk, send_sem=ssem.at[3 * k + j], recv_sem=rsem.at[3 * k + j],
                    device_id=(ox, oy, c), device_id_type=MESH).wait_recv()
        for cp in cps:
            cp.wait_send()

    return pl.pallas_call(
        body, name="scatter_to_owners",
        in_specs=[ANY] * (2 * n), out_specs=[ANY] * n,
        out_shape=[jax.ShapeDtypeStruct(a.shape, a.dtype) for a in arrs],
        scratch_shapes=[pltpu.SemaphoreType.DMA((3 * n,)), pltpu.SemaphoreType.DMA((3 * n,))],
        input_output_aliases={n + k: k for k in range(n)},
    )(*arrs, *slots)


def _sum_chips(arrs, core):
    n = len(arrs)
    nb = GRAD_ROW_BLOCKS

    def body(c_ref, *refs):
        for k in range(n):
            r = refs[k]
            refs[n + k][...] = ((r[0].astype(F32) + r[1].astype(F32)) + r[2].astype(F32)) + r[3].astype(F32)

    grid_spec = pltpu.PrefetchScalarGridSpec(
        num_scalar_prefetch=1, grid=(nb,),
        in_specs=[pl.BlockSpec((N_SHARD, a.shape[1] // nb, a.shape[2]), lambda i, c: (0, i, 0)) for a in arrs],
        out_specs=[pl.BlockSpec((a.shape[1] // nb, a.shape[2]), lambda i, c: (c[0] * nb + i, 0)) for a in arrs])
    return pl.pallas_call(
        body, name="sum_chips", grid_spec=grid_spec,
        out_shape=[jax.ShapeDtypeStruct((2 * a.shape[1], a.shape[2]), F32) for a in arrs],
        compiler_params=_cparams(1),
    )(core, *arrs)


def _pair_allgather_halves(arrs):
    n = len(arrs)

    def body(*refs):
        outs = refs[n:2 * n]
        ssem, rsem = refs[2 * n:]
        x, y, c = _mesh_pos()
        cps = []
        for k in range(n):
            h = arrs[k].shape[0] // 2
            mine = outs[k].at[pl.ds(c * h, h)]
            cp = pltpu.make_async_remote_copy(src_ref=mine, dst_ref=mine, send_sem=ssem.at[k],
                                              recv_sem=rsem.at[k], device_id=(x, y, 1 - c), device_id_type=MESH)
            cp.start()
            cps.append(cp)
        for k, cp in enumerate(cps):
            h = arrs[k].shape[0] // 2
            theirs = outs[k].at[pl.ds((1 - c) * h, h)]
            pltpu.make_async_remote_copy(src_ref=theirs, dst_ref=theirs, send_sem=ssem.at[k], recv_sem=rsem.at[k],
                                         device_id=(x, y, 1 - c), device_id_type=MESH).wait_recv()
            cp.wait_send()

    return pl.pallas_call(
        body, name="pair_allgather_halves",
        in_specs=[ANY] * n, out_specs=[ANY] * n,
        out_shape=[jax.ShapeDtypeStruct(a.shape, a.dtype) for a in arrs],
        scratch_shapes=[pltpu.SemaphoreType.DMA((n,)), pltpu.SemaphoreType.DMA((n,))],
        input_output_aliases={k: k for k in range(n)},
    )(*arrs)


N_DEV = 8


def _allgather_all(v):
    m_per, n = v.shape

    def body(x_ref, out_ref, send_sems, recv_sems, local_sem):
        x, y, c = _mesh_pos()
        me, sibling = (x, y, c), (x, y, 1 - c)
        chips = _other_chips(x, y)

        def rows(px, py, pc):
            return out_ref.at[pl.ds((4 * px + 2 * py + pc) * m_per, m_per), :]

        def copy(k, block, to, src=None):
            return pltpu.make_async_remote_copy(
                src_ref=rows(*block) if src is None else src, dst_ref=rows(*block),
                send_sem=send_sems.at[k], recv_sem=recv_sems.at[k], device_id=to, device_id_type=MESH)

        mine = pltpu.make_async_copy(x_ref, rows(*me), local_sem)
        mine.start()
        first = [copy(0, me, sibling, src=x_ref)]
        first += [copy(1 + j, me, (*chip, c), src=x_ref) for j, chip in enumerate(chips)]
        for cp in first:
            cp.start()
        passed = [copy(4 + j, (*chip, c), sibling) for j, chip in enumerate(chips)]
        for j, chip in enumerate(chips):
            copy(1 + j, (*chip, c), me).wait_recv()
            passed[j].start()
        copy(0, sibling, me).wait_recv()
        for j, chip in enumerate(chips):
            copy(4 + j, (*chip, 1 - c), me).wait_recv()
        for cp in first + passed:
            cp.wait_send()
        mine.wait()

    return pl.pallas_call(
        body, name="allgather_all",
        out_shape=jax.ShapeDtypeStruct((N_DEV * m_per, n), v.dtype),
        in_specs=[pl.BlockSpec(memory_space=pltpu.VMEM)],
        out_specs=pl.BlockSpec(memory_space=pltpu.VMEM),
        scratch_shapes=[pltpu.SemaphoreType.DMA((7,)), pltpu.SemaphoreType.DMA((7,)), pltpu.SemaphoreType.DMA],
        compiler_params=pltpu.CompilerParams(vmem_limit_bytes=VMEM_LIMIT_MB * 1024 * 1024),
    )(v)


def _adamw_math(w, g, m, v):
    m2 = ADAM_B1 * m + (1.0 - ADAM_B1) * g
    v2 = ADAM_B2 * v + (1.0 - ADAM_B2) * (g * g)
    m_hat = m2 / (1.0 - ADAM_B1 ** ADAM_STEP)
    v_hat = v2 / (1.0 - ADAM_B2 ** ADAM_STEP)
    delta = -ADAM_LR * (m_hat / (jnp.sqrt(v_hat) + ADAM_EPS) + ADAM_WD * w)
    return delta, m2, v2


def _adamw(w, m, v, gs, nblk):
    nl, r, n = w.shape
    assert nl == len(gs) and nl in (1, 2)
    br = r // nblk

    def body(w_ref, m_ref, v_ref, *rest):
        g_refs, (go_ref, d_ref, mo_ref, vo_ref) = rest[:nl], rest[nl:]
        g = g_refs[0][...]
        if nl == 2:
            g = jnp.where(pl.program_id(0) == 0, g, g_refs[1][...])
        delta, m2, v2 = _adamw_math(w_ref[0], g, m_ref[0], v_ref[0])
        go_ref[0] = g
        d_ref[0] = delta
        mo_ref[0] = m2
        vo_ref[0] = v2

    spec = pl.BlockSpec((1, br, n), lambda l, i: (l, i, 0))
    g_specs = [pl.BlockSpec((br, n), lambda l, i: (i, 0))] if nl == 1 else [
        pl.BlockSpec((br, n), lambda l, i: (jnp.where(l == 0, i, nblk - 1), 0)),
        pl.BlockSpec((br, n), lambda l, i: (jnp.where(l == 1, i, 0), 0))]
    return pl.pallas_call(
        body, name="adamw", grid=(nl, nblk),
        in_specs=[spec, spec, spec] + g_specs,
        out_specs=[spec] * 4,
        out_shape=[jax.ShapeDtypeStruct((nl, r, n), F32)] * 4,
        compiler_params=_cparams(2),
    )(w, m, v, *gs)


def _small_reduce_adamw(parts, w, m, v, rep_rows, sh_rows):
    mrows = rep_rows + N_SHARD * sh_rows + LOSS_ROWS

    def body(p_ref, w_ref, m_ref, v_ref, go_ref, d_ref, mo_ref, vo_ref, loss_ref):
        x, y, _ = _mesh_pos()
        mine = rep_rows + (2 * x + y) * sh_rows
        g_rep = p_ref[0:rep_rows, :]
        g_sh = p_ref[pl.ds(pl.multiple_of(mine, SUBLANE), sh_rows), :]
        loss = p_ref[mrows - LOSS_ROWS:mrows, :]
        for k in range(1, N_DEV):
            g_rep = g_rep + p_ref[k * mrows:k * mrows + rep_rows, :]
            g_sh = g_sh + p_ref[pl.ds(pl.multiple_of(k * mrows + mine, SUBLANE), sh_rows), :]
            loss = loss + p_ref[(k + 1) * mrows - LOSS_ROWS:(k + 1) * mrows, :]
        g = jnp.concatenate([g_rep, g_sh], axis=0)
        delta, m2, v2 = _adamw_math(w_ref[...], g, m_ref[...], v_ref[...])
        go_ref[...] = g
        d_ref[...] = delta
        mo_ref[...] = m2
        vo_ref[...] = v2
        loss_ref[...] = loss

    return pl.pallas_call(
        body, name="small_reduce_adamw",
        out_shape=[jax.ShapeDtypeStruct((rep_rows + sh_rows, 128), F32)] * 4
        + [jax.ShapeDtypeStruct((LOSS_ROWS, 128), F32)],
        compiler_params=pltpu.CompilerParams(vmem_limit_bytes=VMEM_LIMIT_MB * 1024 * 1024),
    )(parts, w, m, v)


LANE = 128
REP_SPEC = (("ffn1_norm", 16), ("mix_norm", 16), ("ffn2_norm", 16), ("final_norm", 8), ("pool_w", 128),
            ("pool_scale", 8), ("hgrn_lb_logits", 16), ("hgrn_gnorm", 8), ("lru_wa", 256), ("lru_wx", 256))
SH_SPEC = (("meta_tokens", 32), ("conv_w", 32), ("lru_conv_w", 8), ("conv_b", 8), ("conv_ln_g", 8),
           ("conv_ln_b", 8), ("lru_conv_b", 8), ("lru_ba", 8), ("lru_bx", 8), ("lru_lambda", 8))
REP_ROWS = sum(r for _, r in REP_SPEC)
SH_ROWS = sum(r for _, r in SH_SPEC)


def _pack_rows(vals, spec):
    parts = []
    for name, rows in spec:
        flat = vals[name].astype(F32).reshape(-1, LANE)
        if flat.shape[0] < rows:
            flat = jnp.concatenate([flat, jnp.zeros((rows - flat.shape[0], LANE), F32)], axis=0)
        parts.append(flat)
    return jnp.concatenate(parts, axis=0)


def _unpack_rows(packed, spec, shapes):
    out = {}
    off = 0
    for name, rows in spec:
        shp = shapes[name]
        n = int(np.prod(shp)) // LANE
        out[name] = packed[off:off + n].reshape(shp)
        off += rows
    return out


def _block_diag(blocks):
    n, b, _ = blocks.shape
    return sum(jnp.pad(blocks[g], ((g * b, (n - 1 - g) * b), (g * b, (n - 1 - g) * b))) for g in range(n))


def _diag_blocks(mat, n):
    b = mat.shape[0] // n
    return jnp.stack([mat[g * b:(g + 1) * b, g * b:(g + 1) * b] for g in range(n)])


BIG = ("ffn1_wg", "ffn1_wu", "ffn2_wg", "ffn2_wu", "ffn1_wd", "ffn2_wd", "w_in_even", "w_out_even",
       "w_in_odd", "w_out_odd")
ADAM_BLOCKS = {"ffn1_wg": 8, "ffn1_wu": 8, "ffn2_wg": 8, "ffn2_wu": 8, "ffn1_wd": 4, "ffn2_wd": 4,
               "w_in_even": 4, "w_out_even": 2, "w_in_odd": 4, "w_out_odd": 2}
WEIGHT_NAMES = ('meta_tokens', 'ffn1_norm', 'ffn1_wg', 'ffn1_wu', 'ffn1_wd', 'mix_norm', 'ffn2_norm', 'ffn2_wg',
                'ffn2_wu', 'ffn2_wd', 'w_in_even', 'pool_w', 'pool_scale', 'hgrn_lb_logits', 'hgrn_gnorm',
                'w_out_even', 'w_in_odd', 'conv_w', 'conv_b', 'conv_ln_g', 'conv_ln_b', 'lru_conv_w',
                'lru_conv_b', 'lru_wa', 'lru_ba', 'lru_wx', 'lru_bx', 'lru_lambda', 'w_out_odd', 'final_norm')


def _rows2d(a):
    return a.reshape(-1, a.shape[-1])


def _local_step_v2(x, tgt, w, gathered, small_full):
    s_len, d = x.shape
    t_real = s_len + N_META
    tp = -(-t_real // ROW_ALIGN) * ROW_ALIGN
    tm = _tile(tp, 832, ROW_ALIGN)
    tm_small = _tile(tp, 416, 16)
    tr = _tile(tp, 416, SUBLANE)
    f1 = ("ffn1_norm", "ffn1_wg", "ffn1_wu", "ffn1_wd")
    f2 = ("ffn2_norm", "ffn2_wg", "ffn2_wu", "ffn2_wd")

    meta_full = small_full["meta_tokens"]
    h0 = jnp.concatenate([meta_full, x, jnp.zeros((tp - t_real, d), F32)], axis=0)
    tgt_pad = jnp.concatenate([jnp.zeros((N_META, d), F32), tgt, jnp.zeros((tp - t_real, d), F32)], axis=0)

    w_in_even = jnp.transpose(gathered["w_in_even"], (1, 0, 2)).reshape(d, D_IN_EVEN)
    w_out_even = gathered["w_out_even"].reshape(d, d)
    w_out_odd = gathered["w_out_odd"].reshape(d, d)
    even_piece = [(w_in_even, (d, D_IN_EVEN), (0, 0))]
    odd_pieces = [(gathered["w_in_odd"], (1, d, D_IN_ODD // N_SHARD), (k, 0, 0)) for k in range(N_SHARD)]
    pool_wbd = _block_diag(w["pool_w"][0]).astype(BF16)
    pool_scale = w["pool_scale"]
    wa_bd = _block_diag2(w["lru_wa"][0]).astype(BF16)
    wx_bd = _block_diag2(w["lru_wx"][0]).astype(BF16)
    mst, msk, n_lev = _hgrn_consts(HG_CHUNK)
    conv_w = small_full["conv_w"]
    sf = small_full

    def gain(name, layer):
        return w[name][layer:layer + 1]

    def ffn(h, names, layer):
        return _ffn_fwd(h, gain(names[0], layer), gathered[names[1]], gathered[names[2]], gathered[names[3]],
                        layer, tm)

    h1, a1, b1, n1 = ffn(h0, f1, 0)
    p0, nm0 = _proj_fwd(h1, gain("mix_norm", 0), 0, even_piece, tm_small)
    ya = _pool_fwd(p0, pool_wbd, pool_scale, tr)
    yb, states = _hgrn_fwd(p0, w["hgrn_lb_logits"], w["hgrn_gnorm"], mst, msk, n_lev, tm)
    h2 = _out_fwd(h1, ya, yb, w_out_even, tm)
    h3, a2, b2, n2 = ffn(h2, f2, 0)
    h4, a3, b3, n3 = ffn(h3, f1, 1)
    p1, nm1 = _proj_fwd(h4, gain("mix_norm", 1), 1, odd_pieces, tm_small)
    yc = _convmod_fwd(p1, conv_w, sf["conv_b"], sf["conv_ln_g"], sf["conv_ln_b"], tr)
    lru_args = (sf["lru_conv_w"], sf["lru_conv_b"], wa_bd, sf["lru_ba"], wx_bd, sf["lru_bx"], sf["lru_lambda"])
    yd, hs = _lru_fwd(p1, *lru_args)
    h5 = _out_fwd(h4, yc, yd, w_out_odd, tm)
    h6, a4, b4, n4 = ffn(h5, f2, 1)
    loss, dh6, dg_final = _loss_bwd(h6, w["final_norm"].reshape(1, d), tgt_pad, t_real, tm)

    def ffn_bwd(dho, h, n, a, b, names, layer, acc):
        dh, da, db, dg = _ffn_bwd_act(dho, h, gain(names[0], layer), a, b, gathered[names[1]], gathered[names[2]],
                                      gathered[names[3]], layer, tm_small)
        acc = _ffn_bwd_w(dho, n, a, b, da, db, acc[0], acc[1], acc[2], layer, tm)
        return dh, dg, acc

    none3 = (None, None, None)
    dh5, dg_f2_l1, g_f2 = ffn_bwd(dh6, h5, n4, a4, b4, f2, 1, none3)
    dyc, dyd, dw_out_odd = _out_bwd(dh5, yc, yd, w_out_odd, tm)
    dca, dcb, dconv_w, dconv_vec = _convmod_bwd(p1, dyc, conv_w, sf["conv_b"], sf["conv_ln_g"], sf["conv_ln_b"], tr)
    dlx, dlg, dwa_bd, dwx_bd, dlru_vec = _lru_bwd(p1, hs, dyd, *lru_args)
    dp1 = [dca, dcb, dlx, dlg]
    dh4, dg_mix_l1 = _proj_bwd_act(dh5, h4, gain("mix_norm", 1), 1, dp1, odd_pieces, tm_small)
    dw_in_odd = jnp.stack(_proj_bwd_w(nm1, dp1, tm))
    dh3, dg_f1_l1, g_f1 = ffn_bwd(dh4, h3, n3, a3, b3, f1, 1, none3)
    dh2, dg_f2_l0, g_f2 = ffn_bwd(dh3, h2, n2, a2, b2, f2, 0, g_f2)
    dya, dyb, dw_out_even = _out_bwd(dh2, ya, yb, w_out_even, tm)
    dpool, dpool_wbd, dpool_scale = _pool_bwd(p0, dya, pool_wbd, pool_scale, tr)
    dq, dz, dv, dgate, dlb_logits, dgn_heads = _hgrn_bwd(p0, dyb, states, w["hgrn_lb_logits"], w["hgrn_gnorm"],
                                                         mst, msk, n_lev, tm)
    dp0 = [jnp.concatenate([dpool, dq, dz, dv, dgate], axis=1)]
    dh1, dg_mix_l0 = _proj_bwd_act(dh2, h1, gain("mix_norm", 0), 0, dp0, even_piece, tm_small)
    (dw_in_even,) = _proj_bwd_w(nm0, dp0, tm_small)
    dh0, dg_f1_l0, g_f1 = ffn_bwd(dh1, h0, n1, a1, b1, f1, 0, g_f1)

    grad_x = dh0[N_META:t_real]
    big = {
        "ffn1_wg": g_f1[0], "ffn1_wu": g_f1[1], "ffn1_wd": g_f1[2],
        "ffn2_wg": g_f2[0], "ffn2_wu": g_f2[1], "ffn2_wd": g_f2[2],
        "w_in_even": jnp.transpose(dw_in_even.reshape(d, N_SHARD, D_IN_EVEN // N_SHARD), (1, 0, 2)),
        "w_out_even": dw_out_even.reshape(N_SHARD, d // N_SHARD, d),
        "w_in_odd": dw_in_odd,
        "w_out_odd": dw_out_odd.reshape(N_SHARD, d // N_SHARD, d),
    }
    rep = {
        "ffn1_norm": jnp.concatenate([dg_f1_l0, dg_f1_l1], axis=0),
        "mix_norm": jnp.concatenate([dg_mix_l0, dg_mix_l1], axis=0),
        "ffn2_norm": jnp.concatenate([dg_f2_l0, dg_f2_l1], axis=0),
        "final_norm": dg_final,
        "pool_w": _diag_blocks(dpool_wbd, len(POOL_WINDOWS)),
        "pool_scale": dpool_scale,
        "hgrn_lb_logits": dlb_logits,
        "hgrn_gnorm": jnp.sum(dgn_heads, axis=0),
        "lru_wa": _diag_blocks2(dwa_bd),
        "lru_wx": _diag_blocks2(dwx_bd),
    }
    dmeta = jnp.transpose(dh0[:N_META].reshape(N_META, N_SHARD, 2, LANE), (1, 0, 2, 3)).reshape(N_SHARD, 32, LANE)
    packs = [_pack_rows(rep, REP_SPEC)]
    for s in range(N_SHARD):
        sh = {
            "meta_tokens": dmeta[s], "conv_w": dconv_w[s], "lru_conv_w": dlru_vec[s, 0:4],
            "conv_b": dconv_vec[s, 0:1], "conv_ln_g": dconv_vec[s, 1:2], "conv_ln_b": dconv_vec[s, 2:3],
            "lru_conv_b": dlru_vec[s, 4:5], "lru_ba": dlru_vec[s, 5:6], "lru_bx": dlru_vec[s, 6:7],
            "lru_lambda": dlru_vec[s, 7:8],
        }
        packs.append(_pack_rows(sh, SH_SPEC))
    return loss, grad_x, big, jnp.concatenate(packs, axis=0)


def _block_diag2(heads):
    nb = heads.shape[0] // 2
    return jnp.stack([_block_diag(heads[2 * j:2 * j + 2]) for j in range(nb)])


def _diag_blocks2(mats):
    return jnp.concatenate([_diag_blocks(mats[j], 2) for j in range(mats.shape[0])], axis=0)


def _kernel_v2(x, meta_tokens, ffn1_norm, ffn1_wg, ffn1_wu, ffn1_wd, mix_norm, ffn2_norm, ffn2_wg, ffn2_wu, ffn2_wd, w_in_even, pool_w, pool_scale, hgrn_lb_logits, hgrn_gnorm, w_out_even, w_in_odd, conv_w, conv_b, conv_ln_g, conv_ln_b, lru_conv_w, lru_conv_b, lru_wa, lru_ba, lru_wx, lru_bx, lru_lambda, w_out_odd, final_norm, loss_target, m_meta_tokens, m_ffn1_norm, m_ffn1_wg, m_ffn1_wu, m_ffn1_wd, m_mix_norm, m_ffn2_norm, m_ffn2_wg, m_ffn2_wu, m_ffn2_wd, m_w_in_even, m_pool_w, m_pool_scale, m_hgrn_lb_logits, m_hgrn_gnorm, m_w_out_even, m_w_in_odd, m_conv_w, m_conv_b, m_conv_ln_g, m_conv_ln_b, m_lru_conv_w, m_lru_conv_b, m_lru_wa, m_lru_ba, m_lru_wx, m_lru_bx, m_lru_lambda, m_w_out_odd, m_final_norm, v_meta_tokens, v_ffn1_norm, v_ffn1_wg, v_ffn1_wu, v_ffn1_wd, v_mix_norm, v_ffn2_norm, v_ffn2_wg, v_ffn2_wu, v_ffn2_wd, v_w_in_even, v_pool_w, v_pool_scale, v_hgrn_lb_logits, v_hgrn_gnorm, v_w_out_even, v_w_in_odd, v_conv_w, v_conv_b, v_conv_ln_g, v_conv_ln_b, v_lru_conv_w, v_lru_conv_b, v_lru_wa, v_lru_ba, v_lru_wx, v_lru_bx, v_lru_lambda, v_w_out_odd, v_final_norm):
    args = locals()
    w = {n: args[n] for n in WEIGHT_NAMES}
    m = {n: args["m_" + n] for n in WEIGHT_NAMES}
    v = {n: args["v_" + n] for n in WEIGHT_NAMES}
    shapes = {n: w[n].shape for n in WEIGHT_NAMES}

    big_in = [_rows2d(w[n]).astype(BF16) for n in BIG]
    small_sh = _pack_rows(w, SH_SPEC)
    gath = _allgather_shards(big_in + [small_sh], [True] * len(BIG) + [False])
    gathered = dict(zip(BIG, gath[:len(BIG)]))
    sm = gath[len(BIG)]
    sh_shapes = {n: (N_SHARD,) + tuple(shapes[n]) for n, _ in SH_SPEC}
    per_shard = [_unpack_rows(sm[s], SH_SPEC, shapes) for s in range(N_SHARD)]
    small_full = {}
    for n, _ in SH_SPEC:
        stacked = [per_shard[s][n] for s in range(N_SHARD)]
        small_full[n] = jnp.concatenate([p.reshape(-1, p.shape[-1]) for p in stacked], axis=-1)
    small_full["conv_w"] = jnp.concatenate(
        [small_full["conv_w"], jnp.zeros((CONV_HALO - CONV_WIDTH, D_CONV), F32)], axis=0)

    loss, grad_x, big, small_part = _local_step(x[0], loss_target[0], w, gathered, small_full)
    loss = lax.psum(loss[0, 0], ("x", "y", "c"))

    core = lax.axis_index("c").astype(jnp.int32).reshape(1)
    parts = [big[n] for n in BIG]
    recvd = _pair_exchange_halves(parts)
    pair = _pair_add(parts, recvd, core)
    chip = (2 * lax.axis_index("x") + lax.axis_index("y")).astype(jnp.int32).reshape(1)
    slots = _scatter_to_owners(pair, _own_slot(pair, chip))
    halves = _sum_chips(slots, core)
    full = _pair_allgather_halves(halves)
    out_g, out_d, out_m, out_v = {}, {}, {}, {}
    for n, g in zip(BIG, full):
        res = _adamw(_rows2d(w[n]), _rows2d(m[n]), _rows2d(v[n]), g, 0, ADAM_BLOCKS[n])
        out_g[n], out_d[n], out_m[n], out_v[n] = [r.reshape(shapes[n]) for r in res]

    gathered_small = _allgather_all(small_part)

    def pack_small(src):
        return jnp.concatenate([_pack_rows(src, REP_SPEC), _pack_rows(src, SH_SPEC)], axis=0)

    res = _small_reduce_adamw(gathered_small, pack_small(w), pack_small(m), pack_small(v), REP_ROWS, SH_ROWS)
    for dst, packed in zip((out_g, out_d, out_m, out_v), res):
        dst.update(_unpack_rows(packed[:REP_ROWS], REP_SPEC, shapes))
        dst.update(_unpack_rows(packed[REP_ROWS:], SH_SPEC, shapes))

    return (loss, grad_x[None], *[out_g[n] for n in WEIGHT_NAMES], *[out_d[n] for n in WEIGHT_NAMES],
            *[out_m[n] for n in WEIGHT_NAMES], *[out_v[n] for n in WEIGHT_NAMES])


GATHER_GROUPS = (
    (("small", 0),),
    (("ffn1_wg", 0), ("ffn1_wu", 0), ("ffn1_wd", 0)),
    (("w_in_even", 0), ("w_out_even", 0)),
    (("ffn2_wg", 0), ("ffn2_wu", 0), ("ffn2_wd", 0)),
    (("ffn1_wg", 1), ("ffn1_wu", 1), ("ffn1_wd", 1)),
    (("w_in_odd", 0), ("w_out_odd", 0)),
    (("ffn2_wg", 1), ("ffn2_wu", 1), ("ffn2_wd", 1)),
)
ADAM_ROW_BLOCKS = {"ffn1_wg": 8, "ffn1_wu": 8, "ffn2_wg": 8, "ffn2_wu": 8, "ffn1_wd": 8, "ffn2_wd": 8,
                   "w_in_even": 8, "w_out_even": 4, "w_in_odd": 8, "w_out_odd": 4}
TRANSPOSED = ("ffn1_wg", "ffn1_wu", "ffn2_wg", "ffn2_wu", "w_in_even")
SCATTER_DEPTH = 2
LOSS_ROWS = 8
GATHER_SPLIT = (1, 4)


def _unpack_small(sm, shapes):
    per_shard = [_unpack_rows(sm[s], SH_SPEC, shapes) for s in range(N_SHARD)]
    full = {}
    for n, _ in SH_SPEC:
        full[n] = jnp.concatenate([per_shard[s][n].reshape(-1, shapes[n][-1]) for s in range(N_SHARD)], axis=-1)
    full["conv_w"] = jnp.concatenate([full["conv_w"], jnp.zeros((CONV_HALO - CONV_WIDTH, D_CONV), F32)], axis=0)
    return full


def _local_step(x, tgt, w, shapes, fetch, emit, emit_small):
    s_len, d = x.shape
    t_real = s_len + N_META
    tp = -(-t_real // ROW_ALIGN) * ROW_ALIGN
    tm = _tile(tp, 832, ROW_ALIGN)
    tm_small = _tile(tp, 832, 16)
    tr = _tile(tp, 416, SUBLANE)

    def gain(name, layer):
        return w[name][layer:layer + 1]

    pool_wbd = _block_diag(w["pool_w"][0]).astype(BF16)
    pool_scale = w["pool_scale"]
    wa_bd = _block_diag2(w["lru_wa"][0]).astype(BF16)
    wx_bd = _block_diag2(w["lru_wx"][0]).astype(BF16)
    mst, msk, n_lev = _hgrn_consts(HG_CHUNK)

    (sm,) = fetch(0, None)
    sf = _unpack_small(sm, shapes)
    h0 = jnp.concatenate([sf["meta_tokens"], x, jnp.zeros((tp - t_real, d), F32)], axis=0)
    tgt_pad = jnp.concatenate([jnp.zeros((N_META, d), F32), tgt, jnp.zeros((tp - t_real, d), F32)], axis=0)
    f1l0 = fetch(1, h0)
    h1, *s1 = _ffn_fwd(h0, gain("ffn1_norm", 0), *f1l0, 0, tm)
    w_in_even4, w_out_even4 = fetch(2, h1)
    w_out_even = w_out_even4.reshape(d, d)
    even_piece = [(w_in_even4.reshape(D_IN_EVEN, d), (D_IN_EVEN, d), (0, 0))]
    p0, nm0 = _proj_fwd(h1, gain("mix_norm", 0), 0, even_piece, tm_small, wt=True)
    ya = _pool_fwd(p0, pool_wbd, pool_scale, tr)
    yb, states = _hgrn_fwd(p0, w["hgrn_lb_logits"], w["hgrn_gnorm"], mst, msk, n_lev, tm)
    h2 = _out_fwd(h1, ya, yb, w_out_even, tm)
    f2l0 = fetch(3, h2)
    h3, *s2 = _ffn_fwd(h2, gain("ffn2_norm", 0), *f2l0, 0, tm)
    f1l1 = fetch(4, h3)
    h4, *s3 = _ffn_fwd(h3, gain("ffn1_norm", 1), *f1l1, 0, tm)
    w_in_odd4, w_out_odd4 = fetch(5, h4)
    w_out_odd = w_out_odd4.reshape(d, d)
    odd_pieces = [(w_in_odd4, (1, d, D_IN_ODD // N_SHARD), (k, 0, 0)) for k in range(N_SHARD)]
    p1, nm1 = _proj_fwd(h4, gain("mix_norm", 1), 1, odd_pieces, tm_small)
    yc = _convmod_fwd(p1, sf["conv_w"], sf["conv_b"], sf["conv_ln_g"], sf["conv_ln_b"], tr)
    lru_args = (sf["lru_conv_w"], sf["lru_conv_b"], wa_bd, sf["lru_ba"], wx_bd, sf["lru_bx"], sf["lru_lambda"])
    yd, hs = _lru_fwd(p1, *lru_args)
    h5 = _out_fwd(h4, yc, yd, w_out_odd, tm)
    f2l1 = fetch(6, h5)
    h6, *s4 = _ffn_fwd(h5, gain("ffn2_norm", 1), *f2l1, 0, tm)
    loss, dh6, dg_final = _loss_bwd(h6, w["final_norm"].reshape(1, d), tgt_pad, t_real, tm)

    def ffn_bwd(dho, h, saved, norm, wts, after=()):
        ga, gb, sa, n = saved
        dh, da, db, dg, dy = _ffn_bwd_act(dho, h, norm, ga, gb, *wts, 0, tm, after)
        return dh, dg, _ffn_bwd_w([(da, n, None), (db, n, None), (sa, dy, None)], tm)

    dh5, dg_f2_l1, g = ffn_bwd(dh6, h5, s4, gain("ffn2_norm", 1), f2l1)
    sent = emit((("ffn2_wg", 1), ("ffn2_wu", 1), ("ffn2_wd", 1)), g)
    dyc, dyd, dw_out_odd = _out_bwd(dh5, yc, yd, w_out_odd, tm, tuple(sent))
    dca, dcb, dconv_w, dconv_vec = _convmod_bwd(p1, dyc, sf["conv_w"], sf["conv_b"], sf["conv_ln_g"],
                                                sf["conv_ln_b"], tr)
    dlx, dlg, dwa_bd, dwx_bd, dlru_vec = _lru_bwd(p1, hs, dyd, *lru_args)
    dp1 = [dca, dcb, dlx, dlg]
    dh4, dg_mix_l1 = _proj_bwd_act(dh5, h4, gain("mix_norm", 1), 1, dp1, odd_pieces, tm_small)
    dw_in_odd = jnp.stack(_proj_bwd_w(nm1, dp1, tm))
    dh3, dg_f1_l1, g = ffn_bwd(dh4, h3, s3, gain("ffn1_norm", 1), f1l1)
    sent = emit((("w_out_odd", 0), ("w_in_odd", 0), ("ffn1_wg", 1), ("ffn1_wu", 1), ("ffn1_wd", 1)),
                [dw_out_odd.reshape(N_SHARD, d // N_SHARD, d), dw_in_odd] + list(g))
    dh2, dg_f2_l0, g_f2l0 = ffn_bwd(dh3, h2, s2, gain("ffn2_norm", 0), f2l0, tuple(sent))
    dya, dyb, dw_out_even = _out_bwd(dh2, ya, yb, w_out_even, tm)
    dpool, dpool_wbd, dpool_scale = _pool_bwd(p0, dya, pool_wbd, pool_scale, tr)
    dq, dz, dv, dgate, dlb_logits, dgn_heads = _hgrn_bwd(p0, dyb, states, w["hgrn_lb_logits"], w["hgrn_gnorm"],
                                                         mst, msk, n_lev, tm)
    dp0 = [dpool, dq, dz, dv, dgate]
    dh1, dg_mix_l0 = _proj_bwd_act(dh2, h1, gain("mix_norm", 0), 0, dp0, even_piece, tm_small, wt=True)
    dw_in_even_t = jnp.concatenate(_proj_bwd_w(nm0, dp0, tm_small, wt=True), axis=0)
    ga, gb, sa, n1 = s1
    (dwd_f1l0,) = _ffn_bwd_w([(sa, dh1, 0.5)], tm)
    sent = emit((("ffn2_wg", 0), ("ffn2_wu", 0), ("ffn2_wd", 0), ("w_out_even", 0), ("w_in_even", 0),
                 ("ffn1_wd", 0)),
                list(g_f2l0) + [dw_out_even.reshape(N_SHARD, d // N_SHARD, d),
                                dw_in_even_t.reshape(N_SHARD, D_IN_EVEN // N_SHARD, d), dwd_f1l0])
    dh0, da, db, dg_f1_l0, _ = _ffn_bwd_act(dh1, h0, gain("ffn1_norm", 0), ga, gb, *f1l0, 0, tm, tuple(sent))

    grad_x = dh0[N_META:t_real]
    rep = {
        "ffn1_norm": jnp.concatenate([dg_f1_l0, dg_f1_l1], axis=0),
        "mix_norm": jnp.concatenate([dg_mix_l0, dg_mix_l1], axis=0),
        "ffn2_norm": jnp.concatenate([dg_f2_l0, dg_f2_l1], axis=0),
        "final_norm": dg_final,
        "pool_w": _diag_blocks(dpool_wbd, len(POOL_WINDOWS)),
        "pool_scale": dpool_scale,
        "hgrn_lb_logits": dlb_logits,
        "hgrn_gnorm": jnp.sum(dgn_heads, axis=0),
        "lru_wa": _diag_blocks2(dwa_bd),
        "lru_wx": _diag_blocks2(dwx_bd),
    }
    dmeta = jnp.transpose(dh0[:N_META].reshape(N_META, N_SHARD, 2, LANE), (1, 0, 2, 3)).reshape(N_SHARD, 32, LANE)
    packs = [_pack_rows(rep, REP_SPEC)]
    for s in range(N_SHARD):
        sh = {
            "meta_tokens": dmeta[s], "conv_w": dconv_w[s], "lru_conv_w": dlru_vec[s, 0:4],
            "conv_b": dconv_vec[s, 0:1], "conv_ln_g": dconv_vec[s, 1:2], "conv_ln_b": dconv_vec[s, 2:3],
            "lru_conv_b": dlru_vec[s, 4:5], "lru_ba": dlru_vec[s, 5:6], "lru_bx": dlru_vec[s, 6:7],
            "lru_lambda": dlru_vec[s, 7:8],
        }
        packs.append(_pack_rows(sh, SH_SPEC))
    packs.append(jnp.pad(loss, ((0, LOSS_ROWS - 1), (0, LANE - 1))))
    sent = emit_small(jnp.concatenate(packs, axis=0))
    emit((("ffn1_wg", 0), ("ffn1_wu", 0)), _ffn_bwd_w([(da, n1, None), (db, n1, None)], tm, tuple(sent)))
    return grad_x


def kernel(x, meta_tokens, ffn1_norm, ffn1_wg, ffn1_wu, ffn1_wd, mix_norm, ffn2_norm, ffn2_wg, ffn2_wu, ffn2_wd, w_in_even, pool_w, pool_scale, hgrn_lb_logits, hgrn_gnorm, w_out_even, w_in_odd, conv_w, conv_b, conv_ln_g, conv_ln_b, lru_conv_w, lru_conv_b, lru_wa, lru_ba, lru_wx, lru_bx, lru_lambda, w_out_odd, final_norm, loss_target, m_meta_tokens, m_ffn1_norm, m_ffn1_wg, m_ffn1_wu, m_ffn1_wd, m_mix_norm, m_ffn2_norm, m_ffn2_wg, m_ffn2_wu, m_ffn2_wd, m_w_in_even, m_pool_w, m_pool_scale, m_hgrn_lb_logits, m_hgrn_gnorm, m_w_out_even, m_w_in_odd, m_conv_w, m_conv_b, m_conv_ln_g, m_conv_ln_b, m_lru_conv_w, m_lru_conv_b, m_lru_wa, m_lru_ba, m_lru_wx, m_lru_bx, m_lru_lambda, m_w_out_odd, m_final_norm, v_meta_tokens, v_ffn1_norm, v_ffn1_wg, v_ffn1_wu, v_ffn1_wd, v_mix_norm, v_ffn2_norm, v_ffn2_wg, v_ffn2_wu, v_ffn2_wd, v_w_in_even, v_pool_w, v_pool_scale, v_hgrn_lb_logits, v_hgrn_gnorm, v_w_out_even, v_w_in_odd, v_conv_w, v_conv_b, v_conv_ln_g, v_conv_ln_b, v_lru_conv_w, v_lru_conv_b, v_lru_wa, v_lru_ba, v_lru_wx, v_lru_bx, v_lru_lambda, v_w_out_odd, v_final_norm):
    args = locals()
    w = {n: args[n] for n in WEIGHT_NAMES}
    m = {n: args["m_" + n] for n in WEIGHT_NAMES}
    v = {n: args["v_" + n] for n in WEIGHT_NAMES}
    shapes = {n: w[n].shape for n in WEIGHT_NAMES}
    core = lax.axis_index("c").astype(jnp.int32).reshape(1)
    chip = (2 * lax.axis_index("x") + lax.axis_index("y")).astype(jnp.int32).reshape(1)
    me = 2 * chip + core

    def view(a, n):
        return jnp.swapaxes(a, 1, 2) if n in TRANSPOSED else a

    wv, mv, vv = [{n: view(src[n], n) for n in BIG} for src in (w, m, v)]

    def shard(key):
        n, l = key
        return _pack_rows(w, SH_SPEC) if n == "small" else wv[n][l].astype(BF16)

    started = {}
    for groups in (GATHER_GROUPS[:2], GATHER_GROUPS[2:]):
        gkeys = [key for grp in groups for key in grp]
        ssem, rsem, srcs, lands, token = _gather_start([shard(key) for key in gkeys],
                                                       [any(key in GATHER_GROUPS[g] for g in GATHER_SPLIT)
                                                        for key in gkeys])
        for k, key in enumerate(gkeys):
            started[key] = (ssem, rsem, srcs[k], lands[k], k, token)

    def pack_small(src):
        return jnp.concatenate([_pack_rows(src, REP_SPEC), _pack_rows(src, SH_SPEC)], axis=0)

    small_packs = [pack_small(src) for src in (w, m, v)]

    def fetch(group, after):
        st = [started[key] for key in GATHER_GROUPS[group]]
        deps = (st[0][5],) if after is None else (after,)
        if group == 1:
            deps += (started[GATHER_GROUPS[2][0]][5],) + tuple(small_packs)
        split = group in GATHER_SPLIT
        got = _gather_wait(st[0][0], st[0][1], [s[2] for s in st], [s[3] for s in st], [s[4] for s in st], deps,
                           split)
        return _pair_forward(got) if split else got

    in_flight, reduced = [], {}

    def collect(entry, after):
        gkeys, gs_sem, gr_sem, grads_thru, slots_thru, _ = entry
        slots = _reduce_wait(gs_sem, gr_sem, grads_thru, slots_thru, after)
        full = _pair_allgather_halves(_sum_devices(slots, core))
        reduced.update(zip(gkeys, full))
        return full[0]

    def emit(gkeys, grads):
        grads = list(grads)
        in_flight.append((gkeys,) + tuple(_reduce_start(grads, _own_part(grads, chip, core, me))))
        token = in_flight[-1][-1]
        if len(in_flight) > SCATTER_DEPTH:
            return token, collect(in_flight[-1 - SCATTER_DEPTH], (token,))
        return (token,)

    small_flight = []

    def emit_small(part):
        small_flight.append(_small_start(part, _small_own(part, me)))
        return (small_flight[0][4],)

    grad_x = _local_step(x[0], loss_target[0], w, shapes, fetch, emit, emit_small)

    out_g, out_d, out_m, out_v = {}, {}, {}, {}
    deps = (in_flight[-1][-1],)

    def adamw_ready():
        done = ()
        for n in BIG:
            layers = range(shapes[n][0])
            if n not in out_g and all((n, l) in reduced for l in layers):
                res = _adamw(wv[n], mv[n], vv[n], [reduced[(n, l)] for l in layers], ADAM_ROW_BLOCKS[n])
                out_g[n], out_d[n], out_m[n], out_v[n] = [view(r, n) for r in res]
                done += (res[1],)
        return done

    for entry in in_flight[-SCATTER_DEPTH:-1]:
        collect(entry, deps)
        deps += adamw_ready()
    s_ssem, s_rsem, s_part, s_slots, _ = small_flight[0]
    small_all = _small_wait(s_ssem, s_rsem, s_part, s_slots, deps)
    small_res = _small_reduce_adamw(small_all.reshape(-1, LANE), *small_packs, REP_ROWS, SH_ROWS)
    collect(in_flight[-1], deps + (small_res[0],))
    adamw_ready()

    loss = small_res[4][0, 0]
    for dst, packed in zip((out_g, out_d, out_m, out_v), small_res[:4]):
        dst.update(_unpack_rows(packed[:REP_ROWS], REP_SPEC, shapes))
        dst.update(_unpack_rows(packed[REP_ROWS:], SH_SPEC, shapes))

    return (loss, grad_x[None], *[out_g[n] for n in WEIGHT_NAMES], *[out_d[n] for n in WEIGHT_NAMES],
            *[out_m[n] for n in WEIGHT_NAMES], *[out_v[n] for n in WEIGHT_NAMES])
```

```python
import functools

import numpy as np
import jax
import jax.numpy as jnp
from jax import lax
from jax.experimental import pallas as pl
from jax.experimental.pallas import tpu as pltpu

F32 = jnp.float32
BF16 = jnp.bfloat16
MESH = pl.DeviceIdType.MESH

EPS = 1e-6
N_META = 16
D_FF = 2816
N_SHARD = 4
FF_SHARD = D_FF // N_SHARD
D_POOL = 256
POOL_GROUP = 64
POOL_WINDOWS = (2, 4, 8, 16)
D_HGRN = 768
HG_HEADS = 6
HEAD = 128
HG_CHUNK = 64
HG_HEADS_PER_STEP = 6
HG_PBLOCK = 256
D_IN_EVEN = D_POOL + 4 * D_HGRN
D_CONV = 512
CONV_WIDTH = 31
CONV_HALO = 32
D_LRU = 512
LRU_CONV = 4
LRU_HALO = 8
LRU_C = 8.0
D_IN_ODD = 2 * D_CONV + 2 * D_LRU
SUBLANE = 8
ROW_ALIGN = 64

ADAM_LR = 0.001
ADAM_B1 = 0.9
ADAM_B2 = 0.999
ADAM_EPS = 1e-08
ADAM_WD = 0.01
ADAM_STEP = 10

VMEM_LIMIT_MB = 56


def _cparams(n_grid_axes=0, vmem_mb=VMEM_LIMIT_MB):
    sem = ("arbitrary",) * n_grid_axes if n_grid_axes else None
    return pltpu.CompilerParams(dimension_semantics=sem, vmem_limit_bytes=vmem_mb * 1024 * 1024)


def _tile(n, target, mult):
    best = None
    for t in range(mult, min(n, target) + 1, mult):
        if n % t == 0:
            best = t
    assert best is not None, (n, target, mult)
    return best


def _dot(a, b):
    return jnp.dot(a, b, preferred_element_type=F32)


def _dot_nt(a, b):
    return lax.dot_general(a, b, (((1,), (1,)), ((), ())), preferred_element_type=F32)


def _dot_tn(a, b):
    return lax.dot_general(a, b, (((0,), (0,)), ((), ())), preferred_element_type=F32)


def _sigmoid(x):
    return 1.0 / (1.0 + jnp.exp(-x))


def _colsum(x):
    return jnp.sum(x, axis=0, keepdims=True)


def _rms_stats(h):
    rstd = lax.rsqrt(jnp.mean(h * h, axis=-1, keepdims=True) + EPS)
    return rstd, h * rstd


def _rms_bwd(dn, g, rstd, xhat):
    dng = dn * g
    dh = rstd * (dng - xhat * jnp.mean(dng * xhat, axis=-1, keepdims=True))
    return dh, _colsum(dn * xhat)


def _ffn_fwd(h, norm, wg4, wu4, wd4, tm):
    tp, d = h.shape
    nt = tp // tm

    def body(h_ref, g_ref, wg_ref, wu_ref, wd_ref, ho_ref, ga_ref, gb_ref, sa_ref, n_ref, n_sc, acc):
        s = pl.program_id(1)

        @pl.when(s == 0)
        def _():
            hh = h_ref[...]
            rstd, xhat = _rms_stats(hh)
            n = (xhat * g_ref[...]).astype(BF16)
            n_sc[...] = n
            n_ref[...] = n
            acc[...] = jnp.zeros_like(acc)

        n = n_sc[...]
        a = _dot_nt(n, wg_ref[0])
        b = _dot_nt(n, wu_ref[0])
        sig = _sigmoid(a)
        sil = a * sig
        ga_ref[0] = (sig * (1.0 + a * (1.0 - sig)) * b).astype(BF16)
        gb_ref[0] = sil.astype(BF16)
        sg = (sil * b).astype(BF16)
        sa_ref[0] = sg
        acc[...] += _dot(sg, wd_ref[0])

        @pl.when(s == N_SHARD - 1)
        def _():
            ho_ref[...] = h_ref[...] + 0.5 * acc[...]

    return pl.pallas_call(
        body, name="ffn_fwd",
        grid=(nt, N_SHARD),
        in_specs=[
            pl.BlockSpec((tm, d), lambda i, s: (i, 0)),
            pl.BlockSpec((1, d), lambda i, s: (0, 0)),
            pl.BlockSpec((1, FF_SHARD, d), lambda i, s: (s, 0, 0)),
            pl.BlockSpec((1, FF_SHARD, d), lambda i, s: (s, 0, 0)),
            pl.BlockSpec((1, FF_SHARD, d), lambda i, s: (s, 0, 0)),
        ],
        out_specs=[
            pl.BlockSpec((tm, d), lambda i, s: (i, 0)),
            pl.BlockSpec((1, tm, FF_SHARD), lambda i, s: (s, i, 0)),
            pl.BlockSpec((1, tm, FF_SHARD), lambda i, s: (s, i, 0)),
            pl.BlockSpec((1, tm, FF_SHARD), lambda i, s: (s, i, 0)),
            pl.BlockSpec((tm, d), lambda i, s: (i, 0)),
        ],
        out_shape=[
            jax.ShapeDtypeStruct((tp, d), F32),
            jax.ShapeDtypeStruct((N_SHARD, tp, FF_SHARD), BF16),
            jax.ShapeDtypeStruct((N_SHARD, tp, FF_SHARD), BF16),
            jax.ShapeDtypeStruct((N_SHARD, tp, FF_SHARD), BF16),
            jax.ShapeDtypeStruct((tp, d), BF16),
        ],
        scratch_shapes=[pltpu.VMEM((tm, d), BF16), pltpu.VMEM((tm, d), F32)],
        compiler_params=_cparams(2),
    )(h, norm, wg4, wu4, wd4)


def _ffn_bwd_act(dho, h, norm, ga4, gb4, wg4, wu4, wd4, tm, after=()):
    tp, d = h.shape
    nt = tp // tm

    def body(dho_ref, h_ref, g_ref, ga_ref, gb_ref, wg_ref, wu_ref, wd_ref, *rest):
        dh_ref, da_ref, db_ref, dg_ref, dy_ref, dn_sc = rest[len(after):]
        i = pl.program_id(0)
        s = pl.program_id(1)

        @pl.when(s == 0)
        def _():
            dy_ref[...] = (0.5 * dho_ref[...]).astype(BF16)
            dn_sc[...] = jnp.zeros_like(dn_sc)

        @pl.when((s == 0) & (i == 0))
        def _():
            dg_ref[...] = jnp.zeros_like(dg_ref)

        ds = _dot_nt(dy_ref[...], wd_ref[0])
        da = (ds * ga_ref[0].astype(F32)).astype(BF16)
        db = (ds * gb_ref[0].astype(F32)).astype(BF16)
        da_ref[0] = da
        db_ref[0] = db
        dn_sc[...] += _dot(da, wg_ref[0]) + _dot(db, wu_ref[0])

        @pl.when(s == N_SHARD - 1)
        def _():
            rstd, xhat = _rms_stats(h_ref[...])
            dh, dg = _rms_bwd(dn_sc[...], g_ref[...], rstd, xhat)
            dh_ref[...] = dho_ref[...] + dh
            dg_ref[...] += dg

    return pl.pallas_call(
        body, name="ffn_bwd_act",
        grid=(nt, N_SHARD),
        in_specs=[
            pl.BlockSpec((tm, d), lambda i, s: (i, 0)),
            pl.BlockSpec((tm, d), lambda i, s: (i, 0)),
            pl.BlockSpec((1, d), lambda i, s: (0, 0)),
            pl.BlockSpec((1, tm, FF_SHARD), lambda i, s: (s, i, 0)),
            pl.BlockSpec((1, tm, FF_SHARD), lambda i, s: (s, i, 0)),
            pl.BlockSpec((1, FF_SHARD, d), lambda i, s: (s, 0, 0)),
            pl.BlockSpec((1, FF_SHARD, d), lambda i, s: (s, 0, 0)),
            pl.BlockSpec((1, FF_SHARD, d), lambda i, s: (s, 0, 0)),
        ] + [pl.BlockSpec(memory_space=pl.ANY)] * len(after),
        out_specs=[
            pl.BlockSpec((tm, d), lambda i, s: (i, 0)),
            pl.BlockSpec((1, tm, FF_SHARD), lambda i, s: (s, i, 0)),
            pl.BlockSpec((1, tm, FF_SHARD), lambda i, s: (s, i, 0)),
            pl.BlockSpec((1, d), lambda i, s: (0, 0)),
            pl.BlockSpec((tm, d), lambda i, s: (i, 0)),
        ],
        out_shape=[
            jax.ShapeDtypeStruct((tp, d), F32),
            jax.ShapeDtypeStruct((N_SHARD, tp, FF_SHARD), BF16),
            jax.ShapeDtypeStruct((N_SHARD, tp, FF_SHARD), BF16),
            jax.ShapeDtypeStruct((1, d), F32),
            jax.ShapeDtypeStruct((tp, d), BF16),
        ],
        scratch_shapes=[pltpu.VMEM((tm, d), F32)],
        compiler_params=_cparams(2),
    )(dho, h, norm, ga4, gb4, wg4, wu4, wd4, *after)


def _ffn_bwd_w(pairs, tm, after=()):
    npair = len(pairs)
    tp, d = pairs[0][1].shape
    nt = tp // tm
    rhs_list = []
    for _, rhs, _ in pairs:
        if all(rhs is not r for r in rhs_list):
            rhs_list.append(rhs)
    rhs_of = [[rhs is r for r in rhs_list].index(True) for _, rhs, _ in pairs]
    nrhs = len(rhs_list)

    def body(*refs):
        rhs_refs = refs[:nrhs]
        lhs_refs = refs[nrhs:nrhs + npair]
        rest = refs[nrhs + npair + len(after):]
        out_refs, accs = rest[:npair], rest[npair:]
        i = pl.program_id(1)

        @pl.when(i == 0)
        def _():
            for acc in accs:
                acc[...] = jnp.zeros_like(acc)

        for k, (_, _, scale) in enumerate(pairs):
            rhs = rhs_refs[rhs_of[k]][...]
            if scale is not None:
                rhs = (scale * rhs).astype(BF16)
            accs[k][...] += _dot_tn(lhs_refs[k][0], rhs)

        @pl.when(i == nt - 1)
        def _():
            for k in range(npair):
                out_refs[k][0] = accs[k][...].astype(BF16)

    return pl.pallas_call(
        body, name="ffn_bwd_w",
        grid=(N_SHARD, nt),
        in_specs=[pl.BlockSpec((tm, d), lambda s, i: (i, 0))] * nrhs
        + [pl.BlockSpec((1, tm, FF_SHARD), lambda s, i: (s, i, 0))] * npair
        + [pl.BlockSpec(memory_space=pl.ANY)] * len(after),
        out_specs=[pl.BlockSpec((1, FF_SHARD, d), lambda s, i: (s, 0, 0))] * npair,
        out_shape=[jax.ShapeDtypeStruct((N_SHARD, FF_SHARD, d), BF16)] * npair,
        scratch_shapes=[pltpu.VMEM((FF_SHARD, d), F32)] * npair,
        compiler_params=_cparams(2),
    )(*rhs_list, *[lhs for lhs, _, _ in pairs], *after)


def _proj_fwd(h, norm, w_pieces, tm, wt=False):
    tp, d = h.shape
    widths = [bs[-2] if wt else bs[-1] for _, bs, _ in w_pieces]
    ntot = sum(widths)
    npc = len(w_pieces)

    def body(*refs):
        h_ref, g_ref = refs[:2]
        w_refs = refs[2:2 + npc]
        p_ref, n_ref = refs[2 + npc:]
        rstd, xhat = _rms_stats(h_ref[...])
        n = (xhat * g_ref[...]).astype(BF16)
        n_ref[...] = n
        off = 0
        for k in range(npc):
            w = w_refs[k][...]
            w = w.reshape(w.shape[-2], w.shape[-1])
            p_ref[:, off:off + widths[k]] = _dot_nt(n, w) if wt else _dot(n, w)
            off += widths[k]

    in_specs = [pl.BlockSpec((tm, d), lambda i: (i, 0)), pl.BlockSpec((1, d), lambda i: (0, 0))]
    for _, bs, idx in w_pieces:
        in_specs.append(pl.BlockSpec(bs, functools.partial(lambda i, idx: idx, idx=idx)))
    return pl.pallas_call(
        body, name="proj_fwd",
        grid=(tp // tm,),
        in_specs=in_specs,
        out_specs=[pl.BlockSpec((tm, ntot), lambda i: (i, 0)), pl.BlockSpec((tm, d), lambda i: (i, 0))],
        out_shape=[jax.ShapeDtypeStruct((tp, ntot), F32), jax.ShapeDtypeStruct((tp, d), BF16)],
        compiler_params=_cparams(1),
    )(h, norm, *[w for w, _, _ in w_pieces])


def _proj_bwd_act(dres, h, norm, dp_pieces, w_pieces, tm, wt=False):
    tp, d = h.shape
    npc = len(dp_pieces)
    nw = len(w_pieces)
    assert nw == npc or (nw == 1 and wt)

    def body(*refs):
        dres_ref, h_ref, g_ref = refs[:3]
        dp_refs = refs[3:3 + npc]
        w_refs = refs[3 + npc:3 + npc + nw]
        dh_ref, dg_ref = refs[3 + npc + nw:]
        i = pl.program_id(0)

        @pl.when(i == 0)
        def _():
            dg_ref[...] = jnp.zeros_like(dg_ref)

        dn = None
        off = 0
        for k in range(npc):
            if nw == npc:
                w = w_refs[k][...]
                w = w.reshape(w.shape[-2], w.shape[-1])
            else:
                w = w_refs[0][off:off + dp_pieces[k].shape[1], :]
                off += dp_pieces[k].shape[1]
            t = _dot(dp_refs[k][...], w) if wt else _dot_nt(dp_refs[k][...], w)
            dn = t if dn is None else dn + t
        rstd, xhat = _rms_stats(h_ref[...])
        dh, dg = _rms_bwd(dn, g_ref[...], rstd, xhat)
        dh_ref[...] = dres_ref[...] + dh
        dg_ref[...] += dg

    in_specs = [pl.BlockSpec((tm, d), lambda i: (i, 0)), pl.BlockSpec((tm, d), lambda i: (i, 0)),
                pl.BlockSpec((1, d), lambda i: (0, 0))]
    for dp in dp_pieces:
        in_specs.append(pl.BlockSpec((tm, dp.shape[1]), lambda i: (i, 0)))
    for _, bs, idx in w_pieces:
        in_specs.append(pl.BlockSpec(bs, functools.partial(lambda i, idx: idx, idx=idx)))
    return pl.pallas_call(
        body, name="proj_bwd_act",
        grid=(tp // tm,),
        in_specs=in_specs,
        out_specs=[pl.BlockSpec((tm, d), lambda i: (i, 0)), pl.BlockSpec((1, d), lambda i: (0, 0))],
        out_shape=[jax.ShapeDtypeStruct((tp, d), F32), jax.ShapeDtypeStruct((1, d), F32)],
        compiler_params=_cparams(1),
    )(dres, h, norm, *dp_pieces, *[w for w, _, _ in w_pieces])


def _proj_bwd_w(n, dp_pieces, tm, wt=False):
    tp, d = n.shape
    npc = len(dp_pieces)
    widths = [dp.shape[1] for dp in dp_pieces]
    oshape = (lambda w: (w, d)) if wt else (lambda w: (d, w))

    def body(*refs):
        n_ref = refs[0]
        dp_refs = refs[1:1 + npc]
        o_refs = refs[1 + npc:1 + 2 * npc]
        accs = refs[1 + 2 * npc:]
        i = pl.program_id(0)

        @pl.when(i == 0)
        def _():
            for acc in accs:
                acc[...] = jnp.zeros_like(acc)

        nn = n_ref[...]
        for k in range(npc):
            accs[k][...] += _dot_tn(dp_refs[k][...], nn) if wt else _dot_tn(nn, dp_refs[k][...])

        @pl.when(i == pl.num_programs(0) - 1)
        def _():
            for k in range(npc):
                o_refs[k][...] = accs[k][...].astype(BF16)

    return pl.pallas_call(
        body, name="proj_bwd_w",
        grid=(tp // tm,),
        in_specs=[pl.BlockSpec((tm, d), lambda i: (i, 0))]
        + [pl.BlockSpec((tm, w), lambda i: (i, 0)) for w in widths],
        out_specs=[pl.BlockSpec(oshape(w), lambda i: (0, 0)) for w in widths],
        out_shape=[jax.ShapeDtypeStruct(oshape(w), BF16) for w in widths],
        scratch_shapes=[pltpu.VMEM(oshape(w), F32) for w in widths],
        compiler_params=_cparams(1),
    )(n, *dp_pieces)


def _out_fwd(h, ya, yb, w, tm):
    tp, d = h.shape
    na, nb = ya.shape[1], yb.shape[1]

    def body(h_ref, ya_ref, yb_ref, w_ref, o_ref):
        y = _dot(ya_ref[...].astype(BF16), w_ref[0:na, :]) + _dot(yb_ref[...].astype(BF16), w_ref[na:, :])
        o_ref[...] = h_ref[...] + y

    return pl.pallas_call(
        body, name="out_fwd",
        grid=(tp // tm,),
        in_specs=[pl.BlockSpec((tm, d), lambda i: (i, 0)), pl.BlockSpec((tm, na), lambda i: (i, 0)),
                  pl.BlockSpec((tm, nb), lambda i: (i, 0)), pl.BlockSpec((d, d), lambda i: (0, 0))],
        out_specs=pl.BlockSpec((tm, d), lambda i: (i, 0)),
        out_shape=jax.ShapeDtypeStruct((tp, d), F32),
        compiler_params=_cparams(1),
    )(h, ya, yb, w)


def _out_bwd(dy, ya, yb, w, tm, after=()):
    tp, d = dy.shape
    na, nb = ya.shape[1], yb.shape[1]

    def body(dy_ref, ya_ref, yb_ref, w_ref, *rest):
        da_ref, db_ref, dw_ref, acc = rest[len(after):]
        i = pl.program_id(0)

        @pl.when(i == 0)
        def _():
            acc[...] = jnp.zeros_like(acc)

        dyb16 = dy_ref[...].astype(BF16)
        da_ref[...] = _dot_nt(dyb16, w_ref[0:na, :])
        db_ref[...] = _dot_nt(dyb16, w_ref[na:, :])
        acc[0:na, :] += _dot_tn(ya_ref[...].astype(BF16), dyb16)
        acc[na:, :] += _dot_tn(yb_ref[...].astype(BF16), dyb16)

        @pl.when(i == pl.num_programs(0) - 1)
        def _():
            dw_ref[...] = acc[...].astype(BF16)

    return pl.pallas_call(
        body, name="out_bwd",
        grid=(tp // tm,),
        in_specs=[pl.BlockSpec((tm, d), lambda i: (i, 0)), pl.BlockSpec((tm, na), lambda i: (i, 0)),
                  pl.BlockSpec((tm, nb), lambda i: (i, 0)), pl.BlockSpec((d, d), lambda i: (0, 0))]
        + [pl.BlockSpec(memory_space=pl.ANY)] * len(after),
        out_specs=[pl.BlockSpec((tm, na), lambda i: (i, 0)), pl.BlockSpec((tm, nb), lambda i: (i, 0)),
                   pl.BlockSpec((d, d), lambda i: (0, 0))],
        out_shape=[jax.ShapeDtypeStruct((tp, na), F32), jax.ShapeDtypeStruct((tp, nb), F32),
                   jax.ShapeDtypeStruct((d, d), BF16)],
        scratch_shapes=[pltpu.VMEM((d, d), F32)],
        compiler_params=_cparams(1),
    )(dy, ya, yb, w, *after)


def _loss_bwd(h, gfin, tgt, t_real, tm):
    tp, d = h.shape

    def body(h_ref, g_ref, t_ref, loss_ref, dh_ref, dg_ref):
        i = pl.program_id(0)

        @pl.when(i == 0)
        def _():
            loss_ref[...] = jnp.zeros_like(loss_ref)
            dg_ref[...] = jnp.zeros_like(dg_ref)

        rows = i * tm + lax.broadcasted_iota(jnp.int32, (tm, 1), 0)
        valid = (rows >= N_META) & (rows < t_real)
        rstd, xhat = _rms_stats(h_ref[...])
        g = g_ref[...]
        err = jnp.where(valid, xhat * g - t_ref[...], 0.0)
        e2 = jnp.sum(err * err, axis=1, keepdims=True)
        loss_ref[...] += (0.5 / d) * jnp.sum(e2, axis=0, keepdims=True)
        dy = err * (1.0 / d)
        dh, dg = _rms_bwd(dy, g, rstd, xhat)
        dh_ref[...] = dh
        dg_ref[...] += dg

    return pl.pallas_call(
        body, name="loss_bwd",
        grid=(tp // tm,),
        in_specs=[pl.BlockSpec((tm, d), lambda i: (i, 0)), pl.BlockSpec((1, d), lambda i: (0, 0)),
                  pl.BlockSpec((tm, d), lambda i: (i, 0))],
        out_specs=[pl.BlockSpec((1, 1), lambda i: (0, 0)), pl.BlockSpec((tm, d), lambda i: (i, 0)),
                   pl.BlockSpec((1, d), lambda i: (0, 0))],
        out_shape=[jax.ShapeDtypeStruct((1, 1), F32), jax.ShapeDtypeStruct((tp, d), F32),
                   jax.ShapeDtypeStruct((1, d), F32)],
        compiler_params=_cparams(1),
    )(h, gfin, tgt)


POOL_HALO = 16


def _pool_lane_consts(n_rows):
    lane = lax.broadcasted_iota(jnp.int32, (n_rows, D_POOL), 1)
    grp = lane // POOL_GROUP
    win = jnp.where(grp == 0, 2.0, jnp.where(grp == 1, 4.0, jnp.where(grp == 2, 8.0, 16.0)))
    return grp, win


def _pool_select(grp, s2, s4, s8, s16):
    return jnp.where(grp == 0, s2, jnp.where(grp == 1, s4, jnp.where(grp == 2, s8, s16)))


def _pool_mixed(x, row0, tr):
    n = tr + POOL_HALO
    s2 = x + pltpu.roll(x, 1, 0)
    s4 = s2 + pltpu.roll(s2, 2, 0)
    s8 = s4 + pltpu.roll(s4, 4, 0)
    s16 = s8 + pltpu.roll(s8, 8, 0)
    grp, win = _pool_lane_consts(n)
    rows = row0 - POOL_HALO + lax.broadcasted_iota(jnp.int32, (n, D_POOL), 0)
    cnt = jnp.minimum((rows + 1).astype(F32), win)
    pooled = _pool_select(grp, s2, s4, s8, s16) / jnp.maximum(cnt, 1.0)
    return (pooled - x)[POOL_HALO:, :]


def _pool_fwd(p, wbd, scale, tr):
    tp = p.shape[0]
    nt = tp // tr

    def body(p_ref, w_ref, s_ref, y_ref, usc):
        usc[0:POOL_HALO, :] = jnp.zeros((POOL_HALO, D_POOL), F32)
        usc[POOL_HALO:, :] = p_ref[...]

        def tile(r, carry):
            r0 = pl.multiple_of(r * tr, SUBLANE)
            x = usc[pl.ds(r0, tr + POOL_HALO), :]
            mixed = _pool_mixed(x, r0, tr)
            y_ref[pl.ds(r0, tr), :] = _dot(mixed.astype(BF16), w_ref[...]) * s_ref[...]
            return carry

        lax.fori_loop(0, nt, tile, 0)

    return pl.pallas_call(
        body, name="pool_fwd",
        grid=(1,),
        in_specs=[pl.BlockSpec((tp, D_POOL), lambda i: (0, 0)), pl.BlockSpec((D_POOL, D_POOL), lambda i: (0, 0)),
                  pl.BlockSpec((1, D_POOL), lambda i: (0, 0))],
        out_specs=pl.BlockSpec((tp, D_POOL), lambda i: (0, 0)),
        out_shape=jax.ShapeDtypeStruct((tp, D_POOL), F32),
        scratch_shapes=[pltpu.VMEM((tp + POOL_HALO, D_POOL), F32)],
        compiler_params=_cparams(1),
    )(p, wbd, scale)


def _pool_bwd(p, dya, wbd, scale, tr):
    tp = p.shape[0]
    nt = tp // tr

    def body(p_ref, dy_ref, w_ref, s_ref, du_ref, dw_ref, ds_ref, usc, gsc):
        usc[0:POOL_HALO, :] = jnp.zeros((POOL_HALO, D_POOL), F32)
        usc[POOL_HALO:, :] = p_ref[...]
        gsc[tp:, :] = jnp.zeros((POOL_HALO, D_POOL), F32)
        dw_ref[...] = jnp.zeros_like(dw_ref)
        ds_ref[...] = jnp.zeros_like(ds_ref)
        grp, win = _pool_lane_consts(tr)

        def tile1(r, carry):
            r0 = pl.multiple_of(r * tr, SUBLANE)
            x = usc[pl.ds(r0, tr + POOL_HALO), :]
            mixed = _pool_mixed(x, r0, tr).astype(BF16)
            dy = dy_ref[pl.ds(r0, tr), :]
            dys = (dy * s_ref[...]).astype(BF16)
            ypre = _dot(mixed, w_ref[...])
            ds_ref[...] += _colsum(dy * ypre)
            dw_ref[...] += _dot_tn(mixed, dys)
            dmx = _dot_nt(dys, w_ref[...])
            rows = r0 + lax.broadcasted_iota(jnp.int32, (tr, D_POOL), 0)
            cnt = jnp.minimum((rows + 1).astype(F32), win)
            gsc[pl.ds(r0, tr), :] = dmx / cnt
            return carry

        lax.fori_loop(0, nt, tile1, 0)
        n = tr + POOL_HALO
        grp2, win2 = _pool_lane_consts(n)

        def tile2(r, carry):
            r0 = pl.multiple_of(r * tr, SUBLANE)
            g = gsc[pl.ds(r0, n), :]
            s2 = g + pltpu.roll(g, n - 1, 0)
            s4 = s2 + pltpu.roll(s2, n - 2, 0)
            s8 = s4 + pltpu.roll(s4, n - 4, 0)
            s16 = s8 + pltpu.roll(s8, n - 8, 0)
            pooled_t = _pool_select(grp2, s2, s4, s8, s16)
            rows = r0 + lax.broadcasted_iota(jnp.int32, (n, D_POOL), 0)
            cnt = jnp.minimum((rows + 1).astype(F32), win2)
            du = pooled_t - g * cnt
            du_ref[pl.ds(r0, tr), :] = du[0:tr, :].astype(BF16)
            return carry

        lax.fori_loop(0, nt, tile2, 0)

    return pl.pallas_call(
        body, name="pool_bwd",
        grid=(1,),
        in_specs=[pl.BlockSpec((tp, D_POOL), lambda i: (0, 0)), pl.BlockSpec((tp, D_POOL), lambda i: (0, 0)),
                  pl.BlockSpec((D_POOL, D_POOL), lambda i: (0, 0)), pl.BlockSpec((1, D_POOL), lambda i: (0, 0))],
        out_specs=[pl.BlockSpec((tp, D_POOL), lambda i: (0, 0)), pl.BlockSpec((D_POOL, D_POOL), lambda i: (0, 0)),
                   pl.BlockSpec((1, D_POOL), lambda i: (0, 0))],
        out_shape=[jax.ShapeDtypeStruct((tp, D_POOL), BF16), jax.ShapeDtypeStruct((D_POOL, D_POOL), F32),
                   jax.ShapeDtypeStruct((1, D_POOL), F32)],
        scratch_shapes=[pltpu.VMEM((tp + POOL_HALO, D_POOL), F32), pltpu.VMEM((tp + POOL_HALO, D_POOL), F32)],
        compiler_params=_cparams(1),
    )(p, dya, wbd, scale)


def _hgrn_levels(ch):
    levels = []
    w = ch // 2
    while w >= 1:
        levels.append(w)
        w //= 2
    return levels


def _hgrn_consts(ch):
    t = np.arange(ch)
    tril = t[None, :] <= t[:, None]
    masks = []
    for w in _hgrn_levels(ch):
        blk = t // (2 * w)
        upper = t % (2 * w) >= w
        masks.append(upper[:, None] & (~upper)[None, :] & (blk[:, None] == blk[None, :]))
    masks.append(tril)
    msk = np.stack(masks).astype(np.float32)
    return jnp.asarray(tril.astype(np.float32), BF16), jnp.asarray(msk, F32), len(masks) - 1


def _split3(x):
    hi = x.astype(BF16)
    r1 = x - hi.astype(F32)
    mid = r1.astype(BF16)
    lo = (r1 - mid.astype(F32)).astype(BF16)
    return hi, mid, lo


def _hgrn_exponents(tril, logf):
    ch = logf.shape[0]
    hi, mid, lo = _split3(logf)
    x = _dot(tril, jnp.concatenate([hi, mid, lo], axis=1))
    b = x[:, 0:HEAD] + x[:, HEAD:2 * HEAD] + x[:, 2 * HEAD:3 * HEAD]
    rows = lax.broadcasted_iota(jnp.int32, (ch, HEAD), 0)
    fx = jnp.broadcast_to(b[ch - 1:ch, :], (ch, HEAD)) - b
    lev = []
    for w in _hgrn_levels(ch):
        pos = rows % (2 * w)
        upper = pos >= w
        if w >= SUBLANE:
            parts = [jnp.broadcast_to(b[k * 2 * w + w - 1:k * 2 * w + w, :], (2 * w, HEAD))
                     for k in range(ch // (2 * w))]
            bmid = parts[0] if len(parts) == 1 else jnp.concatenate(parts, axis=0)
            dx = jnp.where(upper, b - bmid, 0.0)
            ex = jnp.where(upper, 0.0, bmid - b)
        else:
            dx = logf
            ex = jnp.zeros_like(logf)
            for i in range(1, w):
                dx = dx + jnp.where(pos >= w + i, pltpu.roll(logf, i, 0), 0.0)
                ex = ex + jnp.where(pos <= w - 1 - i, pltpu.roll(logf, ch - i, 0), 0.0)
            dx = jnp.where(upper, dx, 0.0)
        lev.append((dx, ex))
    return b, fx, lev


def _hgrn_exponents_bwd(tril, d_b, d_fx, d_blast, lev_grads):
    ch = d_b.shape[0]
    rows = lax.broadcasted_iota(jnp.int32, (ch, HEAD), 0)
    db = d_b - d_fx
    dlf = jnp.zeros_like(d_b)
    for w, (ddx, dex) in zip(_hgrn_levels(ch), lev_grads):
        pos = rows % (2 * w)
        upper = pos >= w
        gu = jnp.where(upper, ddx, 0.0)
        if w >= SUBLANE:
            gl = jnp.where(upper, 0.0, dex)
            db = db + gu - gl
            diff = gl - gu
            for k in range(ch // (2 * w)):
                s = _colsum(diff[k * 2 * w:(k + 1) * 2 * w, :])
                db = db + jnp.where(rows == k * 2 * w + w - 1, s, 0.0)
        else:
            dlf = dlf + gu
            for i in range(1, w):
                dlf = dlf + pltpu.roll(jnp.where(pos >= w + i, gu, 0.0), ch - i, 0)
                dlf = dlf + pltpu.roll(jnp.where(pos <= w - 1 - i, dex, 0.0), i, 0)
    db = db + jnp.where(rows == ch - 1, _colsum(d_fx) + d_blast, 0.0)
    hi = db.astype(BF16)
    lo = (db - hi.astype(F32)).astype(BF16)
    d2 = _dot_tn(tril, jnp.concatenate([hi, lo], axis=1))
    return d2[:, 0:HEAD] + d2[:, HEAD:2 * HEAD] + dlf


def _lockstep(gens):
    results = [None] * len(gens)
    live = list(range(len(gens)))
    while live:
        for i in list(live):
            try:
                next(gens[i])
            except StopIteration as stop:
                results[i] = stop.value
                live.remove(i)
    return results


def _hgrn_gates(q_raw, z, lb):
    sz = _sigmoid(z)
    f = lb + (1.0 - lb) * sz
    q = q_raw * _sigmoid(q_raw)
    k = (1.0 - lb) * (1.0 - sz)
    return q, k, f, sz


def _hgrn_intra(q, k, lev, msk_ref, n_lev, ch):
    eye = (lax.broadcasted_iota(jnp.int32, (ch, ch), 0) == lax.broadcasted_iota(jnp.int32, (ch, ch), 1))
    a = jnp.where(eye, jnp.sum(q * k, axis=1, keepdims=True), 0.0)
    ops = []
    for lv in range(n_lev):
        eq = jnp.exp(lev[lv][0])
        ek = jnp.exp(lev[lv][1])
        qd = q * eq
        kd = k * ek
        a = a + msk_ref[lv] * _dot_nt(qd.astype(BF16), kd.astype(BF16))
        ops.append((eq, ek, qd, kd))
        yield
    return a, ops


def _hgrn_fwd(p, lb_logits, gnorm, mst, msk, n_lev, tm):
    tp = p.shape[0]
    ch = HG_CHUNK
    nct = tm // ch
    nt = tp // tm
    nr = mst.shape[0]
    base = D_POOL // HEAD

    hp = HG_HEADS_PER_STEP
    wide = hp * HEAD
    npr = wide // HG_PBLOCK

    def body(*refs):
        p_refs = refs[:4 * npr]
        lg_ref, gn_ref, mst_ref, msk_ref, y_ref, ss_ref, st_sc = refs[4 * npr:]

        @pl.when(pl.program_id(1) == 0)
        def _():
            st_sc[...] = jnp.zeros_like(st_sc)

        lb_all = _sigmoid(lg_ref[0:1, :] - lg_ref[1:2, :])

        def raw(seg, hh, r0):
            per = HG_PBLOCK // HEAD
            return p_refs[seg * npr + hh // per][pl.ds(r0, ch), (hh % per) * HEAD:(hh % per + 1) * HEAD]

        def one_head(hh, c, r0):
            ls = slice(hh * HEAD, (hh + 1) * HEAD)
            q_raw, z, v, g_raw, st = raw(0, hh, r0), raw(1, hh, r0), raw(2, hh, r0), raw(3, hh, r0), st_sc[hh]
            q, k, f, _ = _hgrn_gates(q_raw, z, lb_all[:, ls])
            yield
            b, fx, lev = _hgrn_exponents(mst_ref[...], jnp.log(f))
            yield
            qe = q * jnp.exp(b)
            a, _ = yield from _hgrn_intra(q, k, lev, msk_ref, n_lev, ch)
            v16 = v.astype(BF16)
            o = _dot_nt(qe.astype(BF16), st.astype(BF16)) + _dot(a.astype(BF16), v16)
            kl = k * jnp.exp(fx)
            st_new = st * jnp.exp(b[ch - 1:ch, :]) + _dot_tn(v16, kl.astype(BF16))
            yield
            rstd = lax.rsqrt(jnp.mean(o * o, axis=-1, keepdims=True) + EPS)
            return st, st_new, o * rstd * gn_ref[...] * (g_raw * _sigmoid(g_raw))

        def chunk(c, carry):
            r0 = pl.multiple_of(c * ch, ch)
            results = _lockstep([one_head(hh, c, r0) for hh in range(hp)])
            for hh, (st, st_new, y) in enumerate(results):
                ss_ref[hh, c] = st
                st_sc[hh] = st_new
                y_ref[pl.ds(r0, ch), hh * HEAD:(hh + 1) * HEAD] = y
            return carry

        lax.fori_loop(0, nct, chunk, 0)

    def pspec(seg, part):
        return pl.BlockSpec((tm, HG_PBLOCK),
                            lambda h, i: (i, (base + seg * HG_HEADS) * HEAD // HG_PBLOCK + h * npr + part))

    return pl.pallas_call(
        body, name="hgrn_fwd",
        grid=(HG_HEADS // hp, nt),
        in_specs=[pspec(seg, part) for seg in range(4) for part in range(npr)]
        + [pl.BlockSpec((2, wide), lambda h, i: (0, h)),
           pl.BlockSpec((1, HEAD), lambda h, i: (0, 0)),
           pl.BlockSpec((nr, ch), lambda h, i: (0, 0)),
           pl.BlockSpec((n_lev + 1, ch, ch), lambda h, i: (0, 0, 0))],
        out_specs=[pl.BlockSpec((tm, wide), lambda h, i: (i, h)),
                   pl.BlockSpec((hp, nct, HEAD, HEAD), lambda h, i: (h, i, 0, 0))],
        out_shape=[jax.ShapeDtypeStruct((tp, D_HGRN), F32),
                   jax.ShapeDtypeStruct((HG_HEADS, tp // ch, HEAD, HEAD), F32)],
        scratch_shapes=[pltpu.VMEM((hp, HEAD, HEAD), F32)],
        compiler_params=_cparams(2),
    )(*([p] * (4 * npr)), lb_logits, gnorm, mst, msk)


def _hgrn_bwd(p, dyb, states, lb_logits, gnorm, mst, msk, n_lev, tm):
    tp = p.shape[0]
    ch = HG_CHUNK
    nct = tm // ch
    nt = tp // tm
    nr = mst.shape[0]
    base = D_POOL // HEAD

    hp = HG_HEADS_PER_STEP
    wide = hp * HEAD
    npr = wide // HG_PBLOCK

    def body(*refs):
        p_refs = refs[:4 * npr]
        (dy_ref, ss_ref, lg_ref, gn_ref, mst_ref, msk_ref,
         dq_ref, dz_ref, dv_ref, dg_ref, dlg_ref, dgn_ref, dst_sc, dlb_sc) = refs[4 * npr:]
        ti = pl.program_id(1)

        def raw(seg, hh, r0):
            per = HG_PBLOCK // HEAD
            return p_refs[seg * npr + hh // per][pl.ds(r0, ch), (hh % per) * HEAD:(hh % per + 1) * HEAD]

        @pl.when(ti == 0)
        def _():
            dst_sc[...] = jnp.zeros_like(dst_sc)
            dlb_sc[...] = jnp.zeros_like(dlb_sc)
            dgn_ref[...] = jnp.zeros_like(dgn_ref)

        lb_all = _sigmoid(lg_ref[0:1, :] - lg_ref[1:2, :])
        gn = gn_ref[...]

        def load_head(hh, c, r0):
            ls = slice(hh * HEAD, (hh + 1) * HEAD)
            return (raw(0, hh, r0), raw(1, hh, r0), raw(2, hh, r0), raw(3, hh, r0),
                    dy_ref[pl.ds(r0, ch), ls], ss_ref[hh, c], dst_sc[hh])

        def store_head(hh, r0, res):
            ls = slice(hh * HEAD, (hh + 1) * HEAD)
            dq_raw, dz, dv, dg_raw, dgn, dst_new, dlb = res
            dq_ref[pl.ds(r0, ch), ls] = dq_raw
            dz_ref[pl.ds(r0, ch), ls] = dz
            dv_ref[pl.ds(r0, ch), ls] = dv
            dg_ref[pl.ds(r0, ch), ls] = dg_raw
            dgn_ref[hh] += dgn
            dst_sc[hh] = dst_new
            dlb_sc[:, ls] += dlb

        def one_head(hh, loaded):
            ls = slice(hh * HEAD, (hh + 1) * HEAD)
            lb = lb_all[:, ls]
            q_raw, z, v, g_raw, dy, st, dst = loaded
            q, k, f, sz = _hgrn_gates(q_raw, z, lb)
            yield
            b, fx, lev = _hgrn_exponents(mst_ref[...], jnp.log(f))
            yield
            eb = jnp.exp(b)
            ef = jnp.exp(fx)
            elast = jnp.exp(b[ch - 1:ch, :])
            qe = q * eb
            kl = k * ef
            a, ops = yield from _hgrn_intra(q, k, lev, msk_ref, n_lev, ch)
            v16 = v.astype(BF16)
            st16 = st.astype(BF16)
            qe16 = qe.astype(BF16)
            kl16 = kl.astype(BF16)
            a16 = a.astype(BF16)
            o = _dot_nt(qe16, st16) + _dot(a16, v16)
            yield
            sg = _sigmoid(g_raw)
            rstd = lax.rsqrt(jnp.mean(o * o, axis=-1, keepdims=True) + EPS)
            oh = o * rstd
            dg_out = (dy * oh * gn * (sg * (1.0 + g_raw * (1.0 - sg)))).astype(BF16)
            don = dy * (g_raw * sg)
            dgn = _colsum(don * oh)
            doh = don * gn
            do = rstd * (doh - oh * jnp.mean(doh * oh, axis=-1, keepdims=True))
            do16 = do.astype(BF16)
            dst16 = dst.astype(BF16)
            yield
            dv = _dot_tn(a16, do16) + _dot_nt(kl16, dst16)
            da = msk_ref[n_lev] * _dot_nt(do16, v16)
            dqe = _dot(do16, st16)
            dkl = _dot(v16, dst16)
            dst_new = dst * elast + _dot_tn(do16, qe16)
            yield
            db_last = _colsum(dst * st) * elast
            dad = jnp.sum(do * v, axis=1, keepdims=True)
            dq = dad * k + dqe * eb
            dk = dad * q + dkl * ef
            lev_grads = []
            for lv in range(n_lev):
                eq, ek, qd, kd = ops[lv]
                gl = (msk_ref[lv] * da).astype(BF16)
                dqd = _dot(gl, kd.astype(BF16))
                dkd = _dot_tn(gl, qd.astype(BF16))
                dq = dq + dqd * eq
                dk = dk + dkd * ek
                lev_grads.append((dqd * qd, dkd * kd))
                yield
            dlogf = _hgrn_exponents_bwd(mst_ref[...], dqe * qe, dkl * kl, db_last, lev_grads)
            yield
            sq = _sigmoid(q_raw)
            dq_out = (dq * (sq * (1.0 + q_raw * (1.0 - sq)))).astype(BF16)
            dfk = dlogf / f - dk
            dz_out = (dfk * (1.0 - lb) * sz * (1.0 - sz)).astype(BF16)
            return dq_out, dz_out, dv.astype(BF16), dg_out, dgn, dst_new, _colsum(dfk * (1.0 - sz))

        def chunk(cc, carry):
            c = nct - 1 - cc
            r0 = pl.multiple_of(c * ch, ch)
            loaded = [load_head(hh, c, r0) for hh in range(HG_HEADS_PER_STEP)]
            results = _lockstep([one_head(hh, loaded[hh]) for hh in range(HG_HEADS_PER_STEP)])
            for hh in range(HG_HEADS_PER_STEP):
                store_head(hh, r0, results[hh])
            return carry

        lax.fori_loop(0, nct, chunk, 0, unroll=1)

        @pl.when(ti == nt - 1)
        def _():
            dl0 = dlb_sc[...] * lb_all * (1.0 - lb_all)
            dlg_ref[0:1, :] = dl0
            dlg_ref[1:2, :] = -dl0

    def pspec(seg, part):
        return pl.BlockSpec((tm, HG_PBLOCK), lambda h, i: (
            nt - 1 - i, (base + seg * HG_HEADS) * HEAD // HG_PBLOCK + h * npr + part))

    ospec = pl.BlockSpec((tm, wide), lambda h, i: (nt - 1 - i, h))
    return pl.pallas_call(
        body, name="hgrn_bwd",
        grid=(HG_HEADS // hp, nt),
        in_specs=[pspec(seg, part) for seg in range(4) for part in range(npr)]
        + [ospec, pl.BlockSpec((hp, nct, HEAD, HEAD), lambda h, i: (h, nt - 1 - i, 0, 0)),
           pl.BlockSpec((2, wide), lambda h, i: (0, h)),
           pl.BlockSpec((1, HEAD), lambda h, i: (0, 0)),
           pl.BlockSpec((nr, ch), lambda h, i: (0, 0)),
           pl.BlockSpec((n_lev + 1, ch, ch), lambda h, i: (0, 0, 0))],
        out_specs=[ospec, ospec, ospec, ospec,
                   pl.BlockSpec((2, wide), lambda h, i: (0, h)),
                   pl.BlockSpec((hp, 1, HEAD), lambda h, i: (h, 0, 0))],
        out_shape=[jax.ShapeDtypeStruct((tp, D_HGRN), BF16)] * 4
        + [jax.ShapeDtypeStruct((2, D_HGRN), F32), jax.ShapeDtypeStruct((HG_HEADS, 1, HEAD), F32)],
        scratch_shapes=[pltpu.VMEM((hp, HEAD, HEAD), F32), pltpu.VMEM((1, wide), F32)],
        compiler_params=_cparams(2),
    )(*([p] * (4 * npr)), dyb, states, lb_logits, gnorm, mst, msk)


def _tap_views(x, tr, halo, width):
    subs = {0: x}
    views = []
    for j in range(width):
        tiles, rem = divmod(width - 1 - j, SUBLANE)
        if rem not in subs:
            subs[rem] = pltpu.roll(x, rem, 0)
        start = halo - tiles * SUBLANE
        views.append(subs[rem][start:start + tr, :])
    return views


def _tap_views_t(y, tr, halo, width):
    n = tr + halo
    subs = {0: y}
    views = []
    for j in range(width):
        tiles, rem = divmod(width - 1 - j, SUBLANE)
        if rem not in subs:
            subs[rem] = pltpu.roll(y, n - rem, 0)
        views.append(subs[rem][tiles * SUBLANE:tiles * SUBLANE + tr, :])
    return views


def _weighted_sum(views, w_ref):
    acc = None
    for j, view in enumerate(views):
        term = view * w_ref[j:j + 1, :]
        acc = term if acc is None else acc + term
    return acc


def _conv_taps(x, w_ref, tr, halo, width):
    return _weighted_sum(_tap_views(x, tr, halo, width), w_ref)


def _conv_taps_t(y, w_ref, tr, halo, width):
    return _weighted_sum(_tap_views_t(y, tr, halo, width), w_ref)


def _ln_stats(cv):
    mu = jnp.mean(cv, axis=-1, keepdims=True)
    xc = cv - mu
    rstd = lax.rsqrt(jnp.mean(xc * xc, axis=-1, keepdims=True) + EPS)
    return rstd, xc * rstd


def _convmod_fwd(p, w, bias, ln_g, ln_b, tr):
    tp = p.shape[0]
    nt = tp // tr
    nb = D_CONV // HEAD

    def body(a_ref, b_ref, w_ref, bi_ref, g_ref, be_ref, y_ref, cv_ref, usc):
        usc[0:CONV_HALO, :] = jnp.zeros((CONV_HALO, HEAD), F32)
        usc[CONV_HALO:, :] = a_ref[...] * _sigmoid(b_ref[...])

        def tile(r, carry):
            r0 = pl.multiple_of(r * tr, SUBLANE)
            x = usc[pl.ds(r0, tr + CONV_HALO), :]
            cv = _conv_taps(x, w_ref, tr, CONV_HALO, CONV_WIDTH) + bi_ref[...]
            cv_ref[pl.ds(r0, tr), :] = cv
            _, xh = _ln_stats(cv)
            un = xh * g_ref[...] + be_ref[...]
            y_ref[pl.ds(r0, tr), :] = un * _sigmoid(un)
            return carry

        lax.fori_loop(0, nt, tile, 0)

    vec = lambda: pl.BlockSpec((1, HEAD), lambda j: (0, j))
    return pl.pallas_call(
        body, name="convmod_fwd",
        grid=(nb,),
        in_specs=[pl.BlockSpec((tp, HEAD), lambda j: (0, j)), pl.BlockSpec((tp, HEAD), lambda j: (0, nb + j)),
                  pl.BlockSpec((CONV_HALO, HEAD), lambda j: (0, j)), vec(), vec(), vec()],
        out_specs=[pl.BlockSpec((tp, HEAD), lambda j: (0, j))] * 2,
        out_shape=[jax.ShapeDtypeStruct((tp, D_CONV), F32)] * 2,
        scratch_shapes=[pltpu.VMEM((tp + CONV_HALO, HEAD), F32)],
        compiler_params=_cparams(1),
    )(p, p, w, bias, ln_g, ln_b)


def _convmod_bwd(p, cv_saved, dyc, w, ln_g, ln_b, tr):
    tp = p.shape[0]
    nt = tp // tr
    nb = D_CONV // HEAD

    def body(a_ref, b_ref, cv_ref, dy_ref, w_ref, g_ref, be_ref, da_ref, db_ref, dw_ref, dv_ref, usc, dsc):
        usc[0:CONV_HALO, :] = jnp.zeros((CONV_HALO, HEAD), F32)
        usc[CONV_HALO:, :] = a_ref[...] * _sigmoid(b_ref[...])
        dsc[tp:, :] = jnp.zeros((CONV_HALO, HEAD), F32)
        dw_ref[...] = jnp.zeros_like(dw_ref)
        dv_ref[...] = jnp.zeros_like(dv_ref)

        def tile1(r, carry):
            r0 = pl.multiple_of(r * tr, SUBLANE)
            x = usc[pl.ds(r0, tr + CONV_HALO), :]
            views = _tap_views(x, tr, CONV_HALO, CONV_WIDTH)
            rstd, xh = _ln_stats(cv_ref[pl.ds(r0, tr), :])
            un = xh * g_ref[...] + be_ref[...]
            sg = _sigmoid(un)
            dun = dy_ref[pl.ds(r0, tr), :] * (sg * (1.0 + un * (1.0 - sg)))
            dv_ref[0, 1:2, :] += _colsum(dun * xh)
            dv_ref[0, 2:3, :] += _colsum(dun)
            dxh = dun * g_ref[...]
            dcv = rstd * (dxh - jnp.mean(dxh, axis=-1, keepdims=True)
                          - xh * jnp.mean(dxh * xh, axis=-1, keepdims=True))
            dv_ref[0, 0:1, :] += _colsum(dcv)
            for j in range(CONV_WIDTH):
                dw_ref[0, j:j + 1, :] += _colsum(dcv * views[j])
            dsc[pl.ds(r0, tr), :] = dcv
            return carry

        lax.fori_loop(0, nt, tile1, 0)

        def tile2(r, carry):
            r0 = pl.multiple_of(r * tr, SUBLANE)
            y = dsc[pl.ds(r0, tr + CONV_HALO), :]
            du = _conv_taps_t(y, w_ref, tr, CONV_HALO, CONV_WIDTH)
            a = a_ref[pl.ds(r0, tr), :]
            sb = _sigmoid(b_ref[pl.ds(r0, tr), :])
            da_ref[pl.ds(r0, tr), :] = (du * sb).astype(BF16)
            db_ref[pl.ds(r0, tr), :] = (du * a * sb * (1.0 - sb)).astype(BF16)
            return carry

        lax.fori_loop(0, nt, tile2, 0)

    vec = lambda: pl.BlockSpec((1, HEAD), lambda j: (0, j))
    col = lambda: pl.BlockSpec((tp, HEAD), lambda j: (0, j))
    return pl.pallas_call(
        body, name="convmod_bwd",
        grid=(nb,),
        in_specs=[col(), pl.BlockSpec((tp, HEAD), lambda j: (0, nb + j)), col(), col(),
                  pl.BlockSpec((CONV_HALO, HEAD), lambda j: (0, j)), vec(), vec()],
        out_specs=[col(), col(), pl.BlockSpec((1, CONV_HALO, HEAD), lambda j: (j, 0, 0)),
                   pl.BlockSpec((1, SUBLANE, HEAD), lambda j: (j, 0, 0))],
        out_shape=[jax.ShapeDtypeStruct((tp, D_CONV), BF16), jax.ShapeDtypeStruct((tp, D_CONV), BF16),
                   jax.ShapeDtypeStruct((nb, CONV_HALO, HEAD), F32), jax.ShapeDtypeStruct((nb, SUBLANE, HEAD), F32)],
        scratch_shapes=[pltpu.VMEM((tp + CONV_HALO, HEAD), F32), pltpu.VMEM((tp + CONV_HALO, HEAD), F32)],
        compiler_params=_cparams(1),
    )(p, p, cv_saved, dyc, w, ln_g, ln_b)


def _log1p_small(y):
    return jnp.where(y < 1e-4, y * (1.0 - 0.5 * y), jnp.log(1.0 + y))


def _softplus(x):
    return jnp.maximum(x, 0.0) + _log1p_small(jnp.exp(-jnp.abs(x)))


def _expm1(x):
    return jnp.where(jnp.abs(x) < 1e-2, x * (1.0 + 0.5 * x * (1.0 + x * (1.0 / 3.0))), jnp.exp(x) - 1.0)


def _gelu_parts(x):
    c = 0.7978845608028654
    inner = c * (x + 0.044715 * x * x * x)
    th = jnp.tanh(inner)
    gelu = 0.5 * x * (1.0 + th)
    dgelu = 0.5 * (1.0 + th) + 0.5 * x * (1.0 - th * th) * c * (1.0 + 3.0 * 0.044715 * x * x)
    return gelu, dgelu


def _lru_gates(x_all, tp, cw_ref, cb_ref, wa_ref, ba_ref, wx_ref, bx_ref, lam_ref):
    u = _conv_taps(x_all, cw_ref, tp, LRU_HALO, LRU_CONV) + cb_ref[...]
    u16 = u.astype(BF16)
    r = _sigmoid(_dot(u16, wa_ref[0]) + ba_ref[...])
    i = _sigmoid(_dot(u16, wx_ref[0]) + bx_ref[...])
    sp = _softplus(-lam_ref[...])
    la = -LRU_C * r * sp
    a = jnp.exp(la)
    mult = jnp.sqrt(-_expm1(2.0 * la))
    return u, r, i, a, mult, sp


def _lru_specs(tp, nb):
    col = lambda k: pl.BlockSpec((tp, HEAD), functools.partial(lambda j, k: (0, k * nb + j), k=k))
    vec = lambda: pl.BlockSpec((1, HEAD), lambda j: (0, j))
    mat = lambda: pl.BlockSpec((1, HEAD, HEAD), lambda j: (j, 0, 0))
    return col, vec, mat


def _lru_fwd(p, cw, cb, wa, ba, wx, bx, lam):
    tp = p.shape[0]
    nb = D_LRU // HEAD
    ng = tp // SUBLANE

    def body(x_ref, gt_ref, cw_ref, cb_ref, wa_ref, ba_ref, wx_ref, bx_ref, lam_ref, y_ref, hs_ref,
             xsc, asc, bsc):
        xsc[0:LRU_HALO, :] = jnp.zeros((LRU_HALO, HEAD), F32)
        xsc[LRU_HALO:, :] = x_ref[...]
        u, r, i, a, mult, _ = _lru_gates(xsc[...], tp, cw_ref, cb_ref, wa_ref, ba_ref, wx_ref, bx_ref, lam_ref)
        rows = lax.broadcasted_iota(jnp.int32, (tp, HEAD), 0)
        b = jnp.where(rows == 0, 1.0, mult) * (i * u)
        sub = rows % SUBLANE
        for k in (1, 2, 4):
            m = sub >= k
            b = jnp.where(m, a * pltpu.roll(b, k, 0) + b, b)
            a = jnp.where(m, a * pltpu.roll(a, k, 0), a)
        asc[...] = a
        bsc[...] = b

        def grp(g, carry):
            r0 = pl.multiple_of(g * SUBLANE, SUBLANE)
            h = bsc[pl.ds(r0, SUBLANE), :] + asc[pl.ds(r0, SUBLANE), :] * carry
            hs_ref[pl.ds(r0, SUBLANE), :] = h
            return jnp.broadcast_to(h[SUBLANE - 1:SUBLANE, :], (SUBLANE, HEAD))

        lax.fori_loop(0, ng, grp, jnp.zeros((SUBLANE, HEAD), F32))
        gelu, _ = _gelu_parts(gt_ref[...])
        y_ref[...] = gelu * hs_ref[...]

    col, vec, mat = _lru_specs(tp, nb)
    return pl.pallas_call(
        body, name="lru_fwd",
        grid=(nb,),
        in_specs=[col(2), col(3), pl.BlockSpec((LRU_CONV, HEAD), lambda j: (0, j)), vec(), mat(), vec(), mat(),
                  vec(), vec()],
        out_specs=[pl.BlockSpec((tp, HEAD), lambda j: (0, j)), pl.BlockSpec((tp, HEAD), lambda j: (0, j))],
        out_shape=[jax.ShapeDtypeStruct((tp, D_LRU), F32), jax.ShapeDtypeStruct((tp, D_LRU), F32)],
        scratch_shapes=[pltpu.VMEM((tp + LRU_HALO, HEAD), F32), pltpu.VMEM((tp, HEAD), F32),
                        pltpu.VMEM((tp, HEAD), F32)],
        compiler_params=_cparams(1),
    )(p, p, cw, cb, wa, ba, wx, bx, lam)


def _lru_bwd(p, hs, dyd, cw, cb, wa, ba, wx, bx, lam):
    tp = p.shape[0]
    nb = D_LRU // HEAD
    ng = tp // SUBLANE

    def body(x_ref, gt_ref, hs_ref, dy_ref, cw_ref, cb_ref, wa_ref, ba_ref, wx_ref, bx_ref, lam_ref,
             dx_ref, dgt_ref, dwa_ref, dwx_ref, dv_ref, xsc, asc, bsc, gsc, dusc):
        xsc[0:LRU_HALO, :] = jnp.zeros((LRU_HALO, HEAD), F32)
        xsc[LRU_HALO:, :] = x_ref[...]
        x_all = xsc[...]
        u, r, i, a, mult, sp = _lru_gates(x_all, tp, cw_ref, cb_ref, wa_ref, ba_ref, wx_ref, bx_ref, lam_ref)
        rows = lax.broadcasted_iota(jnp.int32, (tp, HEAD), 0)
        hs = hs_ref[...]
        dy = dy_ref[...]
        gelu, dgelu = _gelu_parts(gt_ref[...])
        dgt_ref[...] = (dy * hs * dgelu).astype(BF16)
        bb = dy * gelu
        aa = jnp.where(rows == tp - 1, 0.0, pltpu.roll(a, tp - 1, 0))
        sub = rows % SUBLANE
        for k in (1, 2, 4):
            m = sub < SUBLANE - k
            bb = jnp.where(m, aa * pltpu.roll(bb, tp - k, 0) + bb, bb)
            aa = jnp.where(m, aa * pltpu.roll(aa, tp - k, 0), aa)
        asc[...] = aa
        bsc[...] = bb

        def grp(gi, carry):
            g = ng - 1 - gi
            r0 = pl.multiple_of(g * SUBLANE, SUBLANE)
            gg = bsc[pl.ds(r0, SUBLANE), :] + asc[pl.ds(r0, SUBLANE), :] * carry
            gsc[pl.ds(r0, SUBLANE), :] = gg
            return jnp.broadcast_to(gg[0:1, :], (SUBLANE, HEAD))

        lax.fori_loop(0, ng, grp, jnp.zeros((SUBLANE, HEAD), F32))
        g = gsc[...]
        first = rows == 0
        hprev = jnp.where(first, 0.0, pltpu.roll(hs, 1, 0))
        iu = i * u
        d_iu = g * jnp.where(first, 1.0, mult)
        dmult_term = jnp.where(first, 0.0, g * iu * (-(a * a) / mult))
        dla = g * hprev * a + dmult_term
        dr = dla * (-LRU_C) * sp
        dv_ref[0, 7:8, :] = _colsum(dla * (LRU_C * r) * _sigmoid(-lam_ref[...]))
        dpr = dr * r * (1.0 - r)
        dpi = d_iu * u * i * (1.0 - i)
        dv_ref[0, 5:6, :] = _colsum(dpr)
        dv_ref[0, 6:7, :] = _colsum(dpi)
        u16 = u.astype(BF16)
        dpr16 = dpr.astype(BF16)
        dpi16 = dpi.astype(BF16)
        dwa_ref[0] = _dot_tn(u16, dpr16)
        dwx_ref[0] = _dot_tn(u16, dpi16)
        du = d_iu * i + _dot_nt(dpr16, wa_ref[0]) + _dot_nt(dpi16, wx_ref[0])
        dv_ref[0, 4:5, :] = _colsum(du)
        for j in range(LRU_CONV):
            sh = LRU_CONV - 1 - j
            xs = x_all if sh == 0 else pltpu.roll(x_all, sh, 0)
            dv_ref[0, j:j + 1, :] = _colsum(du * xs[LRU_HALO:, :])
        dusc[0:tp, :] = du
        dusc[tp:, :] = jnp.zeros((LRU_HALO, HEAD), F32)
        dx_ref[...] = _conv_taps_t(dusc[...], cw_ref, tp, LRU_HALO, LRU_CONV).astype(BF16)

    col, vec, mat = _lru_specs(tp, nb)
    ocol = lambda: pl.BlockSpec((tp, HEAD), lambda j: (0, j))
    return pl.pallas_call(
        body, name="lru_bwd",
        grid=(nb,),
        in_specs=[col(2), col(3), ocol(), ocol(), pl.BlockSpec((LRU_CONV, HEAD), lambda j: (0, j)), vec(), mat(),
                  vec(), mat(), vec(), vec()],
        out_specs=[ocol(), ocol(), mat(), mat(), pl.BlockSpec((1, SUBLANE, HEAD), lambda j: (j, 0, 0))],
        out_shape=[jax.ShapeDtypeStruct((tp, D_LRU), BF16), jax.ShapeDtypeStruct((tp, D_LRU), BF16),
                   jax.ShapeDtypeStruct((nb, HEAD, HEAD), F32), jax.ShapeDtypeStruct((nb, HEAD, HEAD), F32),
                   jax.ShapeDtypeStruct((nb, SUBLANE, HEAD), F32)],
        scratch_shapes=[pltpu.VMEM((tp + LRU_HALO, HEAD), F32), pltpu.VMEM((tp, HEAD), F32),
                        pltpu.VMEM((tp, HEAD), F32), pltpu.VMEM((tp, HEAD), F32),
                        pltpu.VMEM((tp + LRU_HALO, HEAD), F32)],
        compiler_params=_cparams(1),
    )(p, p, hs, dyd, cw, cb, wa, ba, wx, bx, lam)


def _mesh_pos():
    return lax.axis_index("x"), lax.axis_index("y"), lax.axis_index("c")


def _other_chips(x, y):
    return [(1 - x, y), (x, 1 - y), (1 - x, 1 - y)]


ANY = pl.BlockSpec(memory_space=pl.ANY)


HBM = pl.BlockSpec(memory_space=pltpu.HBM)
SEM = pl.BlockSpec(memory_space=pltpu.SEMAPHORE)
DATAFLOW = pltpu.SideEffectType.DATAFLOW_SIDE_EFFECTING
N_PEERS = 4


def _in_hbm(a):
    return pltpu.with_memory_space_constraint(a, pltpu.HBM)


def _gather_peers(x, y, c):
    return [((ox, oy, c), 2 * ox + oy) for ox, oy in _other_chips(x, y)] + [((x, y, 1 - c), 2 * x + y)]


def _gather_refs(src, land, slot, c, split):
    if not split:
        return src, land.at[slot]
    half = src.shape[0] // 2
    return src.at[pl.ds(c * half, half)], land.at[slot, pl.ds(c * half, half)]


def _gather_start(arrs, split):
    n = len(arrs)

    def body(*refs):
        ins, lands = refs[:n], refs[n:2 * n]
        ssem, rsem = refs[2 * n:2 * n + 2]
        token = refs[-1]
        x, y, c = _mesh_pos()
        chip = 2 * x + y
        for k in range(n):
            for j, (dev, _) in enumerate(_gather_peers(x, y, c)):
                src, dst = _gather_refs(ins[k], lands[k], chip, c, split[k] and j < N_PEERS - 1)
                pltpu.make_async_remote_copy(
                    src_ref=src, dst_ref=dst, send_sem=ssem.at[N_PEERS * k + j],
                    recv_sem=rsem.at[N_PEERS * k + j], device_id=dev, device_id_type=MESH).start()
        token[...] = jnp.zeros_like(token)

    lands = [_in_hbm(lax.empty((N_SHARD,) + a.shape, a.dtype)) for a in arrs]
    out = pl.pallas_call(
        body, name="gather_start",
        in_specs=[HBM] * (2 * n),
        out_specs=[SEM, SEM] + [HBM] * (2 * n) + [pl.BlockSpec(memory_space=pltpu.VMEM)],
        out_shape=[pltpu.SemaphoreType.DMA((N_PEERS * n,)), pltpu.SemaphoreType.DMA((N_PEERS * n,))]
        + [pltpu.HBM(a.shape, a.dtype) for a in arrs]
        + [pltpu.HBM((N_SHARD,) + a.shape, a.dtype) for a in arrs]
        + [jax.ShapeDtypeStruct((SUBLANE, LANE), F32)],
        input_output_aliases={k: 2 + k for k in range(2 * n)},
        compiler_params=pltpu.CompilerParams(has_side_effects=DATAFLOW),
    )(*[_in_hbm(a) for a in arrs], *lands)
    return out[0], out[1], list(out[2:2 + n]), list(out[2 + n:2 + 2 * n]), out[-1]


def _gather_wait(ssem, rsem, srcs, lands, ks, after, split=False):
    n = len(ks)

    def body(*refs):
        ins, lnd = refs[:n], refs[n:2 * n]
        ssem_ref, rsem_ref = refs[2 * n:2 * n + 2]
        x, y, c = _mesh_pos()
        for i, k in enumerate(ks):
            for j, (dev, pchip) in enumerate(_gather_peers(x, y, c)):
                src, dst = _gather_refs(ins[i], lnd[i], pchip, c, split and j < N_PEERS - 1)
                cp = pltpu.make_async_remote_copy(
                    src_ref=src, dst_ref=dst, send_sem=ssem_ref.at[N_PEERS * k + j],
                    recv_sem=rsem_ref.at[N_PEERS * k + j], device_id=dev, device_id_type=MESH)
                cp.wait_send()
                cp.wait_recv()

    out = pl.pallas_call(
        body, name="gather_wait",
        in_specs=[HBM] * (2 * n) + [SEM, SEM] + [ANY] * len(after),
        out_specs=[HBM] * (2 * n),
        out_shape=[pltpu.HBM(a.shape, a.dtype) for a in srcs] + [pltpu.HBM(a.shape, a.dtype) for a in lands],
        input_output_aliases={k: k for k in range(2 * n)},
        compiler_params=pltpu.CompilerParams(has_side_effects=DATAFLOW),
    )(*srcs, *lands, ssem, rsem, *after)
    return list(out[n:])


def _pair_forward(lands):
    n = len(lands)

    def body(*refs):
        outs = refs[n:2 * n]
        ssem, rsem = refs[2 * n:]
        x, y, c = _mesh_pos()
        sibling = (x, y, 1 - c)
        cps = []
        for k in range(n):
            half = lands[k].shape[1] // 2
            for j, (ox, oy) in enumerate(_other_chips(x, y)):
                mine = outs[k].at[2 * ox + oy, pl.ds(c * half, half)]
                cp = pltpu.make_async_remote_copy(src_ref=mine, dst_ref=mine, send_sem=ssem.at[3 * k + j],
                                                  recv_sem=rsem.at[3 * k + j], device_id=sibling, device_id_type=MESH)
                cp.start()
                cps.append(cp)
        for k in range(n):
            half = lands[k].shape[1] // 2
            for j, (ox, oy) in enumerate(_other_chips(x, y)):
                theirs = outs[k].at[2 * ox + oy, pl.ds((1 - c) * half, half)]
                pltpu.make_async_remote_copy(src_ref=theirs, dst_ref=theirs, send_sem=ssem.at[3 * k + j],
                                             recv_sem=rsem.at[3 * k + j], device_id=sibling,
                                             device_id_type=MESH).wait_recv()
        for cp in cps:
            cp.wait_send()

    return pl.pallas_call(
        body, name="pair_forward",
        in_specs=[ANY] * n, out_specs=[ANY] * n,
        out_shape=[jax.ShapeDtypeStruct(a.shape, a.dtype) for a in lands],
        scratch_shapes=[pltpu.SemaphoreType.DMA((3 * n,)), pltpu.SemaphoreType.DMA((3 * n,))],
        input_output_aliases={k: k for k in range(n)},
    )(*lands)


N_SOURCES = 7


def _reduce_peers(x, y, c):
    peers = []
    for ox, oy in _other_chips(x, y):
        for rel in range(2):
            peers.append(((ox, oy, c + rel - 2 * c * rel), 2 * ox + oy))
    peers.append(((x, y, 1 - c), 2 * x + y))
    return peers


def _reduce_start(arrs, slots):
    n = len(arrs)

    def body(*refs):
        ins, lands = refs[:n], refs[n:2 * n]
        ssem, rsem = refs[2 * n:2 * n + 2]
        token = refs[-1]
        x, y, c = _mesh_pos()
        me = 2 * (2 * x + y) + c
        for k in range(n):
            half = arrs[k].shape[1] // 2
            for p, (dev, ochip) in enumerate(_reduce_peers(x, y, c)):
                pltpu.make_async_remote_copy(
                    src_ref=ins[k].at[ochip, pl.ds(dev[2] * half, half)], dst_ref=lands[k].at[me],
                    send_sem=ssem.at[N_SOURCES * k + p], recv_sem=rsem.at[N_SOURCES * k + p],
                    device_id=dev, device_id_type=MESH).start()
        token[...] = jnp.zeros_like(token)

    out = pl.pallas_call(
        body, name="reduce_start",
        in_specs=[HBM] * (2 * n),
        out_specs=[SEM, SEM] + [HBM] * (2 * n) + [pl.BlockSpec(memory_space=pltpu.VMEM)],
        out_shape=[pltpu.SemaphoreType.DMA((N_SOURCES * n,)), pltpu.SemaphoreType.DMA((N_SOURCES * n,))]
        + [pltpu.HBM(a.shape, a.dtype) for a in arrs] + [pltpu.HBM(a.shape, a.dtype) for a in slots]
        + [jax.ShapeDtypeStruct((SUBLANE, LANE), F32)],
        input_output_aliases={k: 2 + k for k in range(2 * n)},
        compiler_params=pltpu.CompilerParams(has_side_effects=DATAFLOW),
    )(*[_in_hbm(a) for a in arrs], *[_in_hbm(a) for a in slots])
    return out[0], out[1], list(out[2:2 + n]), list(out[2 + n:2 + 2 * n]), out[-1]


def _reduce_wait(ssem, rsem, arrs, slots, after):
    n = len(arrs)

    def body(*refs):
        ins, lnd = refs[:n], refs[n:2 * n]
        ssem_ref, rsem_ref = refs[2 * n:2 * n + 2]
        x, y, c = _mesh_pos()
        for k in range(n):
            half = arrs[k].shape[1] // 2
            for p, (dev, ochip) in enumerate(_reduce_peers(x, y, c)):
                cp = pltpu.make_async_remote_copy(
                    src_ref=ins[k].at[ochip, pl.ds(dev[2] * half, half)], dst_ref=lnd[k].at[2 * ochip + dev[2]],
                    send_sem=ssem_ref.at[N_SOURCES * k + p], recv_sem=rsem_ref.at[N_SOURCES * k + p],
                    device_id=dev, device_id_type=MESH)
                cp.wait_send()
                cp.wait_recv()

    out = pl.pallas_call(
        body, name="reduce_wait",
        in_specs=[HBM] * (2 * n) + [SEM, SEM] + [ANY] * len(after),
        out_specs=[HBM] * (2 * n),
        out_shape=[pltpu.HBM(a.shape, a.dtype) for a in arrs] + [pltpu.HBM(a.shape, a.dtype) for a in slots],
        input_output_aliases={k: k for k in range(2 * n)},
        compiler_params=pltpu.CompilerParams(has_side_effects=DATAFLOW),
    )(*arrs, *slots, ssem, rsem, *after)
    return list(out[n:])


def _own_part(arrs, chip, core, me):
    n = len(arrs)
    nb = GRAD_ROW_BLOCKS

    def body(chip_ref, core_ref, me_ref, *refs):
        for k in range(n):
            refs[n + k][...] = refs[k][...]

    def blk(a):
        return (1, a.shape[1] // 2 // nb, a.shape[2])

    grid_spec = pltpu.PrefetchScalarGridSpec(
        num_scalar_prefetch=3, grid=(nb,),
        in_specs=[pl.BlockSpec(blk(a), lambda i, ch, co, me: (ch[0], co[0] * nb + i, 0)) for a in arrs],
        out_specs=[pl.BlockSpec(blk(a), lambda i, ch, co, me: (me[0], i, 0)) for a in arrs])
    return pl.pallas_call(
        body, name="own_part", grid_spec=grid_spec,
        out_shape=[jax.ShapeDtypeStruct((N_DEV, a.shape[1] // 2, a.shape[2]), a.dtype) for a in arrs],
        compiler_params=_cparams(1),
    )(chip, core, me, *arrs)


def _sum_devices(arrs, core):
    n = len(arrs)
    nb = GRAD_ROW_BLOCKS

    def body(c_ref, *refs):
        for k in range(n):
            r = refs[k]
            acc = r[0].astype(F32)
            for dev in range(1, N_DEV):
                acc = acc + r[dev].astype(F32)
            refs[n + k][...] = acc

    grid_spec = pltpu.PrefetchScalarGridSpec(
        num_scalar_prefetch=1, grid=(nb,),
        in_specs=[pl.BlockSpec((N_DEV, a.shape[1] // nb, a.shape[2]), lambda i, c: (0, i, 0)) for a in arrs],
        out_specs=[pl.BlockSpec((a.shape[1] // nb, a.shape[2]), lambda i, c: (c[0] * nb + i, 0)) for a in arrs])
    return pl.pallas_call(
        body, name="sum_devices", grid_spec=grid_spec,
        out_shape=[jax.ShapeDtypeStruct((2 * a.shape[1], a.shape[2]), F32) for a in arrs],
        compiler_params=_cparams(1),
    )(core, *arrs)


def _small_own(v, me):
    m = v.shape[0]

    def body(me_ref, v_ref, o_ref):
        o_ref[0] = v_ref[...]

    grid_spec = pltpu.PrefetchScalarGridSpec(
        num_scalar_prefetch=1, grid=(1,),
        in_specs=[pl.BlockSpec((m, LANE), lambda i, me: (0, 0))],
        out_specs=pl.BlockSpec((1, m, LANE), lambda i, me: (me[0], 0, 0)))
    return pl.pallas_call(
        body, name="small_own", grid_spec=grid_spec,
        out_shape=jax.ShapeDtypeStruct((N_DEV, m, LANE), v.dtype),
        compiler_params=_cparams(1),
    )(me, v)


def _small_start(v, slots):
    def body(v_ref, land, ssem, rsem, v_thru, land_thru, token):
        del v_thru, land_thru
        x, y, c = _mesh_pos()
        me = 2 * (2 * x + y) + c
        for p, (dev, _) in enumerate(_reduce_peers(x, y, c)):
            pltpu.make_async_remote_copy(src_ref=v_ref, dst_ref=land.at[me], send_sem=ssem.at[p],
                                         recv_sem=rsem.at[p], device_id=dev, device_id_type=MESH).start()
        token[...] = jnp.zeros_like(token)

    out = pl.pallas_call(
        body, name="small_start",
        in_specs=[HBM, HBM],
        out_specs=[SEM, SEM, HBM, HBM, pl.BlockSpec(memory_space=pltpu.VMEM)],
        out_shape=[pltpu.SemaphoreType.DMA((N_SOURCES,)), pltpu.SemaphoreType.DMA((N_SOURCES,)),
                   pltpu.HBM(v.shape, v.dtype), pltpu.HBM(slots.shape, slots.dtype),
                   jax.ShapeDtypeStruct((SUBLANE, LANE), F32)],
        input_output_aliases={0: 2, 1: 3},
        compiler_params=pltpu.CompilerParams(has_side_effects=DATAFLOW),
    )(_in_hbm(v), _in_hbm(slots))
    return out


def _small_wait(ssem, rsem, v, slots, after):
    def body(*refs):
        v_ref, land, ssem_ref, rsem_ref = refs[:4]
        x, y, c = _mesh_pos()
        for p, (dev, ochip) in enumerate(_reduce_peers(x, y, c)):
            cp = pltpu.make_async_remote_copy(src_ref=v_ref, dst_ref=land.at[2 * ochip + dev[2]],
                                              send_sem=ssem_ref.at[p], recv_sem=rsem_ref.at[p],
                                              device_id=dev, device_id_type=MESH)
            cp.wait_send()
            cp.wait_recv()

    out = pl.pallas_call(
        body, name="small_wait",
        in_specs=[HBM, HBM, SEM, SEM] + [ANY] * len(after),
        out_specs=[HBM, HBM],
        out_shape=[pltpu.HBM(v.shape, v.dtype), pltpu.HBM(slots.shape, slots.dtype)],
        input_output_aliases={0: 0, 1: 1},
        compiler_params=pltpu.CompilerParams(has_side_effects=DATAFLOW),
    )(v, slots, ssem, rsem, *after)
    return out[1]


GRAD_ROW_BLOCKS = 2


def _pair_allgather_halves(arrs):
    n = len(arrs)

    def body(*refs):
        outs = refs[n:2 * n]
        ssem, rsem = refs[2 * n:]
        x, y, c = _mesh_pos()
        cps = []
        for k in range(n):
            h = arrs[k].shape[0] // 2
            mine = outs[k].at[pl.ds(c * h, h)]
            cp = pltpu.make_async_remote_copy(src_ref=mine, dst_ref=mine, send_sem=ssem.at[k],
                                              recv_sem=rsem.at[k], device_id=(x, y, 1 - c), device_id_type=MESH)
            cp.start()
            cps.append(cp)
        for k, cp in enumerate(cps):
            h = arrs[k].shape[0] // 2
            theirs = outs[k].at[pl.ds((1 - c) * h, h)]
            pltpu.make_async_remote_copy(src_ref=theirs, dst_ref=theirs, send_sem=ssem.at[k], recv_sem=rsem.at[k],
                                         device_id=(x, y, 1 - c), device_id_type=MESH).wait_recv()
            cp.wait_send()

    return pl.pallas_call(
        body, name="pair_allgather_halves",
        in_specs=[ANY] * n, out_specs=[ANY] * n,
        out_shape=[jax.ShapeDtypeStruct(a.shape, a.dtype) for a in arrs],
        scratch_shapes=[pltpu.SemaphoreType.DMA((n,)), pltpu.SemaphoreType.DMA((n,))],
        input_output_aliases={k: k for k in range(n)},
    )(*arrs)


N_DEV = 8


def _adamw_math(w, g, m, v):
    m2 = ADAM_B1 * m + (1.0 - ADAM_B1) * g
    v2 = ADAM_B2 * v + (1.0 - ADAM_B2) * (g * g)
    m_hat = m2 / (1.0 - ADAM_B1 ** ADAM_STEP)
    v_hat = v2 / (1.0 - ADAM_B2 ** ADAM_STEP)
    delta = -ADAM_LR * (m_hat / (jnp.sqrt(v_hat) + ADAM_EPS) + ADAM_WD * w)
    return delta, m2, v2


def _adamw(w, m, v, gs, nblk):
    nl, r, n = w.shape
    assert nl == len(gs) and nl in (1, 2)
    br = r // nblk

    def body(w_ref, m_ref, v_ref, *rest):
        g_refs, (go_ref, d_ref, mo_ref, vo_ref) = rest[:nl], rest[nl:]
        g = g_refs[0][...]
        if nl == 2:
            g = jnp.where(pl.program_id(0) == 0, g, g_refs[1][...])
        delta, m2, v2 = _adamw_math(w_ref[0], g, m_ref[0], v_ref[0])
        go_ref[0] = g
        d_ref[0] = delta
        mo_ref[0] = m2
        vo_ref[0] = v2

    spec = pl.BlockSpec((1, br, n), lambda l, i: (l, i, 0))
    g_specs = [pl.BlockSpec((br, n), lambda l, i: (i, 0))] if nl == 1 else [
        pl.BlockSpec((br, n), lambda l, i: (jnp.where(l == 0, i, nblk - 1), 0)),
        pl.BlockSpec((br, n), lambda l, i: (jnp.where(l == 1, i, 0), 0))]
    return pl.pallas_call(
        body, name="adamw", grid=(nl, nblk),
        in_specs=[spec, spec, spec] + g_specs,
        out_specs=[spec] * 4,
        out_shape=[jax.ShapeDtypeStruct((nl, r, n), F32)] * 4,
        compiler_params=_cparams(2),
    )(w, m, v, *gs)


def _small_reduce_adamw(parts, w, m, v, rep_rows, sh_rows):
    mrows = rep_rows + N_SHARD * sh_rows + LOSS_ROWS

    def body(p_ref, w_ref, m_ref, v_ref, go_ref, d_ref, mo_ref, vo_ref, loss_ref):
        x, y, _ = _mesh_pos()
        mine = rep_rows + (2 * x + y) * sh_rows
        g_rep = p_ref[0:rep_rows, :]
        g_sh = p_ref[pl.ds(pl.multiple_of(mine, SUBLANE), sh_rows), :]
        loss = p_ref[mrows - LOSS_ROWS:mrows, :]
        for k in range(1, N_DEV):
            g_rep = g_rep + p_ref[k * mrows:k * mrows + rep_rows, :]
            g_sh = g_sh + p_ref[pl.ds(pl.multiple_of(k * mrows + mine, SUBLANE), sh_rows), :]
            loss = loss + p_ref[(k + 1) * mrows - LOSS_ROWS:(k + 1) * mrows, :]
        g = jnp.concatenate([g_rep, g_sh], axis=0)
        delta, m2, v2 = _adamw_math(w_ref[...], g, m_ref[...], v_ref[...])
        go_ref[...] = g
        d_ref[...] = delta
        mo_ref[...] = m2
        vo_ref[...] = v2
        loss_ref[...] = loss

    return pl.pallas_call(
        body, name="small_reduce_adamw",
        out_shape=[jax.ShapeDtypeStruct((rep_rows + sh_rows, 128), F32)] * 4
        + [jax.ShapeDtypeStruct((LOSS_ROWS, 128), F32)],
        compiler_params=pltpu.CompilerParams(vmem_limit_bytes=VMEM_LIMIT_MB * 1024 * 1024),
    )(parts, w, m, v)


LANE = 128
REP_SPEC = (("ffn1_norm", 16), ("mix_norm", 16), ("ffn2_norm", 16), ("final_norm", 8), ("pool_w", 128),
            ("pool_scale", 8), ("hgrn_lb_logits", 16), ("hgrn_gnorm", 8), ("lru_wa", 256), ("lru_wx", 256))
SH_SPEC = (("meta_tokens", 32), ("conv_w", 32), ("lru_conv_w", 8), ("conv_b", 8), ("conv_ln_g", 8),
           ("conv_ln_b", 8), ("lru_conv_b", 8), ("lru_ba", 8), ("lru_bx", 8), ("lru_lambda", 8))
REP_ROWS = sum(r for _, r in REP_SPEC)
SH_ROWS = sum(r for _, r in SH_SPEC)


def _pack_rows(vals, spec):
    parts = []
    for name, rows in spec:
        flat = vals[name].astype(F32).reshape(-1, LANE)
        if flat.shape[0] < rows:
            flat = jnp.concatenate([flat, jnp.zeros((rows - flat.shape[0], LANE), F32)], axis=0)
        parts.append(flat)
    return jnp.concatenate(parts, axis=0)


def _unpack_rows(packed, spec, shapes):
    out = {}
    off = 0
    for name, rows in spec:
        shp = shapes[name]
        n = int(np.prod(shp)) // LANE
        out[name] = packed[off:off + n].reshape(shp)
        off += rows
    return out


def _block_diag(blocks):
    n, b, _ = blocks.shape
    return sum(jnp.pad(blocks[g], ((g * b, (n - 1 - g) * b), (g * b, (n - 1 - g) * b))) for g in range(n))


def _diag_blocks(mat, n):
    b = mat.shape[0] // n
    return jnp.stack([mat[g * b:(g + 1) * b, g * b:(g + 1) * b] for g in range(n)])


BIG = ("ffn1_wg", "ffn1_wu", "ffn2_wg", "ffn2_wu", "ffn1_wd", "ffn2_wd", "w_in_even", "w_out_even",
       "w_in_odd", "w_out_odd")
WEIGHT_NAMES = ('meta_tokens', 'ffn1_norm', 'ffn1_wg', 'ffn1_wu', 'ffn1_wd', 'mix_norm', 'ffn2_norm', 'ffn2_wg',
                'ffn2_wu', 'ffn2_wd', 'w_in_even', 'pool_w', 'pool_scale', 'hgrn_lb_logits', 'hgrn_gnorm',
                'w_out_even', 'w_in_odd', 'conv_w', 'conv_b', 'conv_ln_g', 'conv_ln_b', 'lru_conv_w',
                'lru_conv_b', 'lru_wa', 'lru_ba', 'lru_wx', 'lru_bx', 'lru_lambda', 'w_out_odd', 'final_norm')


def _block_diag2(heads):
    nb = heads.shape[0] // 2
    return jnp.stack([_block_diag(heads[2 * j:2 * j + 2]) for j in range(nb)])


def _diag_blocks2(mats):
    return jnp.concatenate([_diag_blocks(mats[j], 2) for j in range(mats.shape[0])], axis=0)


GATHER_GROUPS = (
    (("small", 0),),
    (("ffn1_wg", 0), ("ffn1_wu", 0), ("ffn1_wd", 0)),
    (("w_in_even", 0), ("w_out_even", 0)),
    (("ffn2_wg", 0), ("ffn2_wu", 0), ("ffn2_wd", 0)),
    (("ffn1_wg", 1), ("ffn1_wu", 1), ("ffn1_wd", 1)),
    (("w_in_odd", 0), ("w_out_odd", 0)),
    (("ffn2_wg", 1), ("ffn2_wu", 1), ("ffn2_wd", 1)),
)
ADAM_ROW_BLOCKS = {"ffn1_wg": 2, "ffn1_wu": 2, "ffn2_wg": 2, "ffn2_wu": 2, "ffn1_wd": 2, "ffn2_wd": 2,
                   "w_in_even": 4, "w_out_even": 2, "w_in_odd": 4, "w_out_odd": 2}
TRANSPOSED = ("ffn1_wg", "ffn1_wu", "ffn2_wg", "ffn2_wu", "w_in_even")
SCATTER_DEPTH = 2
LOSS_ROWS = 8
GATHER_SPLIT = (1, 4)


def _unpack_small(sm, shapes):
    per_shard = [_unpack_rows(sm[s], SH_SPEC, shapes) for s in range(N_SHARD)]
    full = {}
    for n, _ in SH_SPEC:
        full[n] = jnp.concatenate([per_shard[s][n].reshape(-1, shapes[n][-1]) for s in range(N_SHARD)], axis=-1)
    full["conv_w"] = jnp.concatenate([full["conv_w"], jnp.zeros((CONV_HALO - CONV_WIDTH, D_CONV), F32)], axis=0)
    return full


def _local_step(x, tgt, w, shapes, fetch, emit, emit_small):
    s_len, d = x.shape
    t_real = s_len + N_META
    tp = -(-t_real // ROW_ALIGN) * ROW_ALIGN
    tm = _tile(tp, 832, ROW_ALIGN)
    tm_small = _tile(tp, 832, 16)
    tr = _tile(tp, 416, SUBLANE)

    def gain(name, layer):
        return w[name][layer:layer + 1]

    pool_wbd = _block_diag(w["pool_w"][0]).astype(BF16)
    pool_scale = w["pool_scale"]
    wa_bd = _block_diag2(w["lru_wa"][0]).astype(BF16)
    wx_bd = _block_diag2(w["lru_wx"][0]).astype(BF16)
    mst, msk, n_lev = _hgrn_consts(HG_CHUNK)

    (sm,) = fetch(0, None)
    sf = _unpack_small(sm, shapes)
    h0 = jnp.concatenate([sf["meta_tokens"], x, jnp.zeros((tp - t_real, d), F32)], axis=0)
    tgt_pad = jnp.concatenate([jnp.zeros((N_META, d), F32), tgt, jnp.zeros((tp - t_real, d), F32)], axis=0)
    f1l0 = fetch(1, h0)
    h1, *s1 = _ffn_fwd(h0, gain("ffn1_norm", 0), *f1l0, tm)
    w_in_even4, w_out_even4 = fetch(2, h1)
    w_out_even = w_out_even4.reshape(d, d)
    even_piece = [(w_in_even4.reshape(D_IN_EVEN, d), (D_IN_EVEN, d), (0, 0))]
    p0, nm0 = _proj_fwd(h1, gain("mix_norm", 0), even_piece, tm_small, wt=True)
    ya = _pool_fwd(p0, pool_wbd, pool_scale, tr)
    yb, states = _hgrn_fwd(p0, w["hgrn_lb_logits"], w["hgrn_gnorm"], mst, msk, n_lev, tm)
    h2 = _out_fwd(h1, ya, yb, w_out_even, tm)
    f2l0 = fetch(3, h2)
    h3, *s2 = _ffn_fwd(h2, gain("ffn2_norm", 0), *f2l0, tm)
    f1l1 = fetch(4, h3)
    h4, *s3 = _ffn_fwd(h3, gain("ffn1_norm", 1), *f1l1, tm)
    w_in_odd4, w_out_odd4 = fetch(5, h4)
    w_out_odd = w_out_odd4.reshape(d, d)
    odd_pieces = [(w_in_odd4, (1, d, D_IN_ODD // N_SHARD), (k, 0, 0)) for k in range(N_SHARD)]
    p1, nm1 = _proj_fwd(h4, gain("mix_norm", 1), odd_pieces, tm_small)
    yc, conv_out = _convmod_fwd(p1, sf["conv_w"], sf["conv_b"], sf["conv_ln_g"], sf["conv_ln_b"], tr)
    lru_args = (sf["lru_conv_w"], sf["lru_conv_b"], wa_bd, sf["lru_ba"], wx_bd, sf["lru_bx"], sf["lru_lambda"])
    yd, hs = _lru_fwd(p1, *lru_args)
    h5 = _out_fwd(h4, yc, yd, w_out_odd, tm)
    f2l1 = fetch(6, h5)
    h6, *s4 = _ffn_fwd(h5, gain("ffn2_norm", 1), *f2l1, tm)
    loss, dh6, dg_final = _loss_bwd(h6, w["final_norm"].reshape(1, d), tgt_pad, t_real, tm)

    def ffn_bwd(dho, h, saved, norm, wts, after=()):
        ga, gb, sa, n = saved
        dh, da, db, dg, dy = _ffn_bwd_act(dho, h, norm, ga, gb, *wts, tm, after)
        return dh, dg, _ffn_bwd_w([(da, n, None), (db, n, None), (sa, dy, None)], tm)

    dh5, dg_f2_l1, g = ffn_bwd(dh6, h5, s4, gain("ffn2_norm", 1), f2l1)
    sent = emit((("ffn2_wg", 1), ("ffn2_wu", 1), ("ffn2_wd", 1)), g)
    dyc, dyd, dw_out_odd = _out_bwd(dh5, yc, yd, w_out_odd, tm, tuple(sent))
    dca, dcb, dconv_w, dconv_vec = _convmod_bwd(p1, conv_out, dyc, sf["conv_w"], sf["conv_ln_g"],
                                                sf["conv_ln_b"], tr)
    dlx, dlg, dwa_bd, dwx_bd, dlru_vec = _lru_bwd(p1, hs, dyd, *lru_args)
    dp1 = [dca, dcb, dlx, dlg]
    dh4, dg_mix_l1 = _proj_bwd_act(dh5, h4, gain("mix_norm", 1), dp1, odd_pieces, tm_small)
    dw_in_odd = jnp.stack(_proj_bwd_w(nm1, dp1, tm))
    dh3, dg_f1_l1, g = ffn_bwd(dh4, h3, s3, gain("ffn1_norm", 1), f1l1)
    sent = emit((("w_out_odd", 0), ("w_in_odd", 0), ("ffn1_wg", 1), ("ffn1_wu", 1), ("ffn1_wd", 1)),
                [dw_out_odd.reshape(N_SHARD, d // N_SHARD, d), dw_in_odd] + list(g))
    dh2, dg_f2_l0, g_f2l0 = ffn_bwd(dh3, h2, s2, gain("ffn2_norm", 0), f2l0, tuple(sent))
    dya, dyb, dw_out_even = _out_bwd(dh2, ya, yb, w_out_even, tm)
    dpool, dpool_wbd, dpool_scale = _pool_bwd(p0, dya, pool_wbd, pool_scale, tr)
    dq, dz, dv, dgate, dlb_logits, dgn_heads = _hgrn_bwd(p0, dyb, states, w["hgrn_lb_logits"], w["hgrn_gnorm"],
                                                         mst, msk, n_lev, tm)
    dp0 = [dpool, dq, dz, dv, dgate]
    dh1, dg_mix_l0 = _proj_bwd_act(dh2, h1, gain("mix_norm", 0), dp0, even_piece, tm_small, wt=True)
    dw_in_even_t = jnp.concatenate(_proj_bwd_w(nm0, dp0, tm_small, wt=True), axis=0)
    ga, gb, sa, n1 = s1
    (dwd_f1l0,) = _ffn_bwd_w([(sa, dh1, 0.5)], tm)
    sent = emit((("ffn2_wg", 0), ("ffn2_wu", 0), ("ffn2_wd", 0), ("w_out_even", 0), ("w_in_even", 0),
                 ("ffn1_wd", 0)),
                list(g_f2l0) + [dw_out_even.reshape(N_SHARD, d // N_SHARD, d),
                                dw_in_even_t.reshape(N_SHARD, D_IN_EVEN // N_SHARD, d), dwd_f1l0])
    dh0, da, db, dg_f1_l0, _ = _ffn_bwd_act(dh1, h0, gain("ffn1_norm", 0), ga, gb, *f1l0, tm, tuple(sent))

    grad_x = dh0[N_META:t_real]
    rep = {
        "ffn1_norm": jnp.concatenate([dg_f1_l0, dg_f1_l1], axis=0),
        "mix_norm": jnp.concatenate([dg_mix_l0, dg_mix_l1], axis=0),
        "ffn2_norm": jnp.concatenate([dg_f2_l0, dg_f2_l1], axis=0),
        "final_norm": dg_final,
        "pool_w": _diag_blocks(dpool_wbd, len(POOL_WINDOWS)),
        "pool_scale": dpool_scale,
        "hgrn_lb_logits": dlb_logits,
        "hgrn_gnorm": jnp.sum(dgn_heads, axis=0),
        "lru_wa": _diag_blocks2(dwa_bd),
        "lru_wx": _diag_blocks2(dwx_bd),
    }
    dmeta = jnp.transpose(dh0[:N_META].reshape(N_META, N_SHARD, 2, LANE), (1, 0, 2, 3)).reshape(N_SHARD, 32, LANE)
    packs = [_pack_rows(rep, REP_SPEC)]
    for s in range(N_SHARD):
        sh = {
            "meta_tokens": dmeta[s], "conv_w": dconv_w[s], "lru_conv_w": dlru_vec[s, 0:4],
            "conv_b": dconv_vec[s, 0:1], "conv_ln_g": dconv_vec[s, 1:2], "conv_ln_b": dconv_vec[s, 2:3],
            "lru_conv_b": dlru_vec[s, 4:5], "lru_ba": dlru_vec[s, 5:6], "lru_bx": dlru_vec[s, 6:7],
            "lru_lambda": dlru_vec[s, 7:8],
        }
        packs.append(_pack_rows(sh, SH_SPEC))
    packs.append(jnp.pad(loss, ((0, LOSS_ROWS - 1), (0, LANE - 1))))
    sent = emit_small(jnp.concatenate(packs, axis=0))
    emit((("ffn1_wg", 0), ("ffn1_wu", 0)), _ffn_bwd_w([(da, n1, None), (db, n1, None)], tm, tuple(sent)))
    return grad_x


def kernel(x, meta_tokens, ffn1_norm, ffn1_wg, ffn1_wu, ffn1_wd, mix_norm, ffn2_norm, ffn2_wg, ffn2_wu, ffn2_wd, w_in_even, pool_w, pool_scale, hgrn_lb_logits, hgrn_gnorm, w_out_even, w_in_odd, conv_w, conv_b, conv_ln_g, conv_ln_b, lru_conv_w, lru_conv_b, lru_wa, lru_ba, lru_wx, lru_bx, lru_lambda, w_out_odd, final_norm, loss_target, m_meta_tokens, m_ffn1_norm, m_ffn1_wg, m_ffn1_wu, m_ffn1_wd, m_mix_norm, m_ffn2_norm, m_ffn2_wg, m_ffn2_wu, m_ffn2_wd, m_w_in_even, m_pool_w, m_pool_scale, m_hgrn_lb_logits, m_hgrn_gnorm, m_w_out_even, m_w_in_odd, m_conv_w, m_conv_b, m_conv_ln_g, m_conv_ln_b, m_lru_conv_w, m_lru_conv_b, m_lru_wa, m_lru_ba, m_lru_wx, m_lru_bx, m_lru_lambda, m_w_out_odd, m_final_norm, v_meta_tokens, v_ffn1_norm, v_ffn1_wg, v_ffn1_wu, v_ffn1_wd, v_mix_norm, v_ffn2_norm, v_ffn2_wg, v_ffn2_wu, v_ffn2_wd, v_w_in_even, v_pool_w, v_pool_scale, v_hgrn_lb_logits, v_hgrn_gnorm, v_w_out_even, v_w_in_odd, v_conv_w, v_conv_b, v_conv_ln_g, v_conv_ln_b, v_lru_conv_w, v_lru_conv_b, v_lru_wa, v_lru_ba, v_lru_wx, v_lru_bx, v_lru_lambda, v_w_out_odd, v_final_norm):
    args = locals()
    w = {n: args[n] for n in WEIGHT_NAMES}
    m = {n: args["m_" + n] for n in WEIGHT_NAMES}
    v = {n: args["v_" + n] for n in WEIGHT_NAMES}
    shapes = {n: w[n].shape for n in WEIGHT_NAMES}
    core = lax.axis_index("c").astype(jnp.int32).reshape(1)
    chip = (2 * lax.axis_index("x") + lax.axis_index("y")).astype(jnp.int32).reshape(1)
    me = 2 * chip + core

    def view(a, n):
        return jnp.swapaxes(a, 1, 2) if n in TRANSPOSED else a

    wv, mv, vv = [{n: view(src[n], n) for n in BIG} for src in (w, m, v)]

    def shard(key):
        n, l = key
        return _pack_rows(w, SH_SPEC) if n == "small" else wv[n][l].astype(BF16)

    started = {}
    for groups in (GATHER_GROUPS[:2], GATHER_GROUPS[2:]):
        gkeys = [key for grp in groups for key in grp]
        ssem, rsem, srcs, lands, token = _gather_start([shard(key) for key in gkeys],
                                                       [any(key in GATHER_GROUPS[g] for g in GATHER_SPLIT)
                                                        for key in gkeys])
        for k, key in enumerate(gkeys):
            started[key] = (ssem, rsem, srcs[k], lands[k], k, token)

    def pack_small(src):
        return jnp.concatenate([_pack_rows(src, REP_SPEC), _pack_rows(src, SH_SPEC)], axis=0)

    small_packs = [pack_small(src) for src in (w, m, v)]

    def fetch(group, after):
        st = [started[key] for key in GATHER_GROUPS[group]]
        deps = (st[0][5],) if after is None else (after,)
        if group == 1:
            deps += (started[GATHER_GROUPS[2][0]][5],) + tuple(small_packs)
        split = group in GATHER_SPLIT
        got = _gather_wait(st[0][0], st[0][1], [s[2] for s in st], [s[3] for s in st], [s[4] for s in st], deps,
                           split)
        return _pair_forward(got) if split else got

    in_flight, reduced = [], {}

    def collect(entry, after):
        gkeys, gs_sem, gr_sem, grads_thru, slots_thru, _ = entry
        slots = _reduce_wait(gs_sem, gr_sem, grads_thru, slots_thru, after)
        full = _pair_allgather_halves(_sum_devices(slots, core))
        reduced.update(zip(gkeys, full))
        return full[0]

    def emit(gkeys, grads):
        grads = list(grads)
        in_flight.append((gkeys,) + tuple(_reduce_start(grads, _own_part(grads, chip, core, me))))
        token = in_flight[-1][-1]
        if len(in_flight) > SCATTER_DEPTH:
            return token, collect(in_flight[-1 - SCATTER_DEPTH], (token,))
        return (token,)

    small_flight = []

    def emit_small(part):
        small_flight.append(_small_start(part, _small_own(part, me)))
        return (small_flight[0][4],)

    grad_x = _local_step(x[0], loss_target[0], w, shapes, fetch, emit, emit_small)

    out_g, out_d, out_m, out_v = {}, {}, {}, {}
    deps = (in_flight[-1][-1],)

    def adamw_ready():
        done = ()
        for n in BIG:
            layers = range(shapes[n][0])
            if n not in out_g and all((n, l) in reduced for l in layers):
                res = _adamw(wv[n], mv[n], vv[n], [reduced[(n, l)] for l in layers], ADAM_ROW_BLOCKS[n])
                out_g[n], out_d[n], out_m[n], out_v[n] = [view(r, n) for r in res]
                done += (res[1],)
        return done

    deps += adamw_ready()
    for entry in in_flight[-SCATTER_DEPTH:-1]:
        collect(entry, deps)
        deps += adamw_ready()
    s_ssem, s_rsem, s_part, s_slots, _ = small_flight[0]
    small_all = _small_wait(s_ssem, s_rsem, s_part, s_slots, deps)
    small_res = _small_reduce_adamw(small_all.reshape(-1, LANE), *small_packs, REP_ROWS, SH_ROWS)
    collect(in_flight[-1], deps + (small_res[0],))
    adamw_ready()

    loss = small_res[4][0, 0]
    for dst, packed in zip((out_g, out_d, out_m, out_v), small_res[:4]):
        dst.update(_unpack_rows(packed[:REP_ROWS], REP_SPEC, shapes))
        dst.update(_unpack_rows(packed[REP_ROWS:], SH_SPEC, shapes))

    return (loss, grad_x[None], *[out_g[n] for n in WEIGHT_NAMES], *[out_d[n] for n in WEIGHT_NAMES],
            *[out_m[n] for n in WEIGHT_NAMES], *[out_v[n] for n in WEIGHT_NAMES])
```

```python
import functools

import numpy as np
import jax
import jax.numpy as jnp
from jax import lax
from jax.experimental import pallas as pl
from jax.experimental.pallas import tpu as pltpu

F32 = jnp.float32
BF16 = jnp.bfloat16
MESH = pl.DeviceIdType.MESH

EPS = 1e-6
N_META = 16
D_FF = 2816
N_SHARD = 4
FF_SHARD = D_FF // N_SHARD
D_POOL = 256
POOL_GROUP = 64
POOL_WINDOWS = (2, 4, 8, 16)
D_HGRN = 768
HG_HEADS = 6
HEAD = 128
HG_CHUNK = 64
HG_HEADS_PER_STEP = 6
HG_PBLOCK = 256
D_IN_EVEN = D_POOL + 4 * D_HGRN
D_CONV = 512
CONV_WIDTH = 31
CONV_HALO = 32
D_LRU = 512
LRU_CONV = 4
LRU_HALO = 8
LRU_C = 8.0
D_IN_ODD = 2 * D_CONV + 2 * D_LRU
SUBLANE = 8
ROW_ALIGN = 64

ADAM_LR = 0.001
ADAM_B1 = 0.9
ADAM_B2 = 0.999
ADAM_EPS = 1e-08
ADAM_WD = 0.01
ADAM_STEP = 10

VMEM_LIMIT_MB = 56


def _cparams(n_grid_axes=0, vmem_mb=VMEM_LIMIT_MB):
    sem = ("arbitrary",) * n_grid_axes if n_grid_axes else None
    return pltpu.CompilerParams(dimension_semantics=sem, vmem_limit_bytes=vmem_mb * 1024 * 1024)


def _tile(n, target, mult):
    best = None
    for t in range(mult, min(n, target) + 1, mult):
        if n % t == 0:
            best = t
    assert best is not None, (n, target, mult)
    return best


def _dot(a, b):
    return jnp.dot(a, b, preferred_element_type=F32)


def _dot_nt(a, b):
    return lax.dot_general(a, b, (((1,), (1,)), ((), ())), preferred_element_type=F32)


def _dot_tn(a, b):
    return lax.dot_general(a, b, (((0,), (0,)), ((), ())), preferred_element_type=F32)


def _sigmoid(x):
    return 1.0 / (1.0 + jnp.exp(-x))


def _colsum(x):
    return jnp.sum(x, axis=0, keepdims=True)


def _rms_stats(h):
    rstd = lax.rsqrt(jnp.mean(h * h, axis=-1, keepdims=True) + EPS)
    return rstd, h * rstd


def _rms_bwd(dn, g, rstd, xhat):
    dng = dn * g
    dh = rstd * (dng - xhat * jnp.mean(dng * xhat, axis=-1, keepdims=True))
    return dh, _colsum(dn * xhat)


def _ffn_fwd(h, norm, wg4, wu4, wd4, tm):
    tp, d = h.shape
    nt = tp // tm

    def body(h_ref, g_ref, wg_ref, wu_ref, wd_ref, ho_ref, ga_ref, gb_ref, sa_ref, n_ref, n_sc, acc):
        s = pl.program_id(1)

        @pl.when(s == 0)
        def _():
            hh = h_ref[...]
            rstd, xhat = _rms_stats(hh)
            n = (xhat * g_ref[...]).astype(BF16)
            n_sc[...] = n
            n_ref[...] = n
            acc[...] = jnp.zeros_like(acc)

        n = n_sc[...]
        a = _dot_nt(n, wg_ref[0])
        b = _dot_nt(n, wu_ref[0])
        sig = _sigmoid(a)
        sil = a * sig
        ga_ref[0] = (sig * (1.0 + a * (1.0 - sig)) * b).astype(BF16)
        gb_ref[0] = sil.astype(BF16)
        sg = (sil * b).astype(BF16)
        sa_ref[0] = sg
        acc[...] += _dot(sg, wd_ref[0])

        @pl.when(s == N_SHARD - 1)
        def _():
            ho_ref[...] = h_ref[...] + 0.5 * acc[...]

    return pl.pallas_call(
        body, name="ffn_fwd",
        grid=(nt, N_SHARD),
        in_specs=[
            pl.BlockSpec((tm, d), lambda i, s: (i, 0)),
            pl.BlockSpec((1, d), lambda i, s: (0, 0)),
            pl.BlockSpec((1, FF_SHARD, d), lambda i, s: (s, 0, 0)),
            pl.BlockSpec((1, FF_SHARD, d), lambda i, s: (s, 0, 0)),
            pl.BlockSpec((1, FF_SHARD, d), lambda i, s: (s, 0, 0)),
        ],
        out_specs=[
            pl.BlockSpec((tm, d), lambda i, s: (i, 0)),
            pl.BlockSpec((1, tm, FF_SHARD), lambda i, s: (s, i, 0)),
            pl.BlockSpec((1, tm, FF_SHARD), lambda i, s: (s, i, 0)),
            pl.BlockSpec((1, tm, FF_SHARD), lambda i, s: (s, i, 0)),
            pl.BlockSpec((tm, d), lambda i, s: (i, 0)),
        ],
        out_shape=[
            jax.ShapeDtypeStruct((tp, d), F32),
            jax.ShapeDtypeStruct((N_SHARD, tp, FF_SHARD), BF16),
            jax.ShapeDtypeStruct((N_SHARD, tp, FF_SHARD), BF16),
            jax.ShapeDtypeStruct((N_SHARD, tp, FF_SHARD), BF16),
            jax.ShapeDtypeStruct((tp, d), BF16),
        ],
        scratch_shapes=[pltpu.VMEM((tm, d), BF16), pltpu.VMEM((tm, d), F32)],
        compiler_params=_cparams(2),
    )(h, norm, wg4, wu4, wd4)


def _ffn_bwd_act(dho, h, norm, ga4, gb4, wg4, wu4, wd4, tm, after=()):
    tp, d = h.shape
    nt = tp // tm

    def body(dho_ref, h_ref, g_ref, ga_ref, gb_ref, wg_ref, wu_ref, wd_ref, *rest):
        dh_ref, da_ref, db_ref, dg_ref, dy_ref, dn_sc = rest[len(after):]
        i = pl.program_id(0)
        s = pl.program_id(1)

        @pl.when(s == 0)
        def _():
            dy_ref[...] = (0.5 * dho_ref[...]).astype(BF16)
            dn_sc[...] = jnp.zeros_like(dn_sc)

        @pl.when((s == 0) & (i == 0))
        def _():
            dg_ref[...] = jnp.zeros_like(dg_ref)

        ds = _dot_nt(dy_ref[...], wd_ref[0])
        da = (ds * ga_ref[0].astype(F32)).astype(BF16)
        db = (ds * gb_ref[0].astype(F32)).astype(BF16)
        da_ref[0] = da
        db_ref[0] = db
        dn_sc[...] += _dot(da, wg_ref[0]) + _dot(db, wu_ref[0])

        @pl.when(s == N_SHARD - 1)
        def _():
            rstd, xhat = _rms_stats(h_ref[...])
            dh, dg = _rms_bwd(dn_sc[...], g_ref[...], rstd, xhat)
            dh_ref[...] = dho_ref[...] + dh
            dg_ref[...] += dg

    return pl.pallas_call(
        body, name="ffn_bwd_act",
        grid=(nt, N_SHARD),
        in_specs=[
            pl.BlockSpec((tm, d), lambda i, s: (i, 0)),
            pl.BlockSpec((tm, d), lambda i, s: (i, 0)),
            pl.BlockSpec((1, d), lambda i, s: (0, 0)),
            pl.BlockSpec((1, tm, FF_SHARD), lambda i, s: (s, i, 0)),
            pl.BlockSpec((1, tm, FF_SHARD), lambda i, s: (s, i, 0)),
            pl.BlockSpec((1, FF_SHARD, d), lambda i, s: (s, 0, 0)),
            pl.BlockSpec((1, FF_SHARD, d), lambda i, s: (s, 0, 0)),
            pl.BlockSpec((1, FF_SHARD, d), lambda i, s: (s, 0, 0)),
        ] + [pl.BlockSpec(memory_space=pl.ANY)] * len(after),
        out_specs=[
            pl.BlockSpec((tm, d), lambda i, s: (i, 0)),
            pl.BlockSpec((1, tm, FF_SHARD), lambda i, s: (s, i, 0)),
            pl.BlockSpec((1, tm, FF_SHARD), lambda i, s: (s, i, 0)),
            pl.BlockSpec((1, d), lambda i, s: (0, 0)),
            pl.BlockSpec((tm, d), lambda i, s: (i, 0)),
        ],
        out_shape=[
            jax.ShapeDtypeStruct((tp, d), F32),
            jax.ShapeDtypeStruct((N_SHARD, tp, FF_SHARD), BF16),
            jax.ShapeDtypeStruct((N_SHARD, tp, FF_SHARD), BF16),
            jax.ShapeDtypeStruct((1, d), F32),
            jax.ShapeDtypeStruct((tp, d), BF16),
        ],
        scratch_shapes=[pltpu.VMEM((tm, d), F32)],
        compiler_params=_cparams(2),
    )(dho, h, norm, ga4, gb4, wg4, wu4, wd4, *after)


def _ffn_bwd_w(pairs, tm, after=()):
    npair = len(pairs)
    tp, d = pairs[0][1].shape
    nt = tp // tm
    rhs_list = []
    for _, rhs, _ in pairs:
        if all(rhs is not r for r in rhs_list):
            rhs_list.append(rhs)
    rhs_of = [[rhs is r for r in rhs_list].index(True) for _, rhs, _ in pairs]
    nrhs = len(rhs_list)

    def body(*refs):
        rhs_refs = refs[:nrhs]
        lhs_refs = refs[nrhs:nrhs + npair]
        rest = refs[nrhs + npair + len(after):]
        out_refs, accs = rest[:npair], rest[npair:]
        i = pl.program_id(1)

        @pl.when(i == 0)
        def _():
            for acc in accs:
                acc[...] = jnp.zeros_like(acc)

        for k, (_, _, scale) in enumerate(pairs):
            rhs = rhs_refs[rhs_of[k]][...]
            if scale is not None:
                rhs = (scale * rhs).astype(BF16)
            accs[k][...] += _dot_tn(lhs_refs[k][0], rhs)

        @pl.when(i == nt - 1)
        def _():
            for k in range(npair):
                out_refs[k][0] = accs[k][...].astype(BF16)

    return pl.pallas_call(
        body, name="ffn_bwd_w",
        grid=(N_SHARD, nt),
        in_specs=[pl.BlockSpec((tm, d), lambda s, i: (i, 0))] * nrhs
        + [pl.BlockSpec((1, tm, FF_SHARD), lambda s, i: (s, i, 0))] * npair
        + [pl.BlockSpec(memory_space=pl.ANY)] * len(after),
        out_specs=[pl.BlockSpec((1, FF_SHARD, d), lambda s, i: (s, 0, 0))] * npair,
        out_shape=[jax.ShapeDtypeStruct((N_SHARD, FF_SHARD, d), BF16)] * npair,
        scratch_shapes=[pltpu.VMEM((FF_SHARD, d), F32)] * npair,
        compiler_params=_cparams(2),
    )(*rhs_list, *[lhs for lhs, _, _ in pairs], *after)


def _proj_fwd(h, norm, w_pieces, tm, wt=False):
    tp, d = h.shape
    widths = [bs[-2] if wt else bs[-1] for _, bs, _ in w_pieces]
    ntot = sum(widths)
    npc = len(w_pieces)

    def body(*refs):
        h_ref, g_ref = refs[:2]
        w_refs = refs[2:2 + npc]
        p_ref, n_ref = refs[2 + npc:]
        rstd, xhat = _rms_stats(h_ref[...])
        n = (xhat * g_ref[...]).astype(BF16)
        n_ref[...] = n
        off = 0
        for k in range(npc):
            w = w_refs[k][...]
            w = w.reshape(w.shape[-2], w.shape[-1])
            p_ref[:, off:off + widths[k]] = _dot_nt(n, w) if wt else _dot(n, w)
            off += widths[k]

    in_specs = [pl.BlockSpec((tm, d), lambda i: (i, 0)), pl.BlockSpec((1, d), lambda i: (0, 0))]
    for _, bs, idx in w_pieces:
        in_specs.append(pl.BlockSpec(bs, functools.partial(lambda i, idx: idx, idx=idx)))
    return pl.pallas_call(
        body, name="proj_fwd",
        grid=(tp // tm,),
        in_specs=in_specs,
        out_specs=[pl.BlockSpec((tm, ntot), lambda i: (i, 0)), pl.BlockSpec((tm, d), lambda i: (i, 0))],
        out_shape=[jax.ShapeDtypeStruct((tp, ntot), F32), jax.ShapeDtypeStruct((tp, d), BF16)],
        compiler_params=_cparams(1),
    )(h, norm, *[w for w, _, _ in w_pieces])


def _proj_bwd_act(dres, h, norm, dp_pieces, w_pieces, tm, wt=False):
    tp, d = h.shape
    npc = len(dp_pieces)
    nw = len(w_pieces)
    assert nw == npc or (nw == 1 and wt)

    def body(*refs):
        dres_ref, h_ref, g_ref = refs[:3]
        dp_refs = refs[3:3 + npc]
        w_refs = refs[3 + npc:3 + npc + nw]
        dh_ref, dg_ref = refs[3 + npc + nw:]
        i = pl.program_id(0)

        @pl.when(i == 0)
        def _():
            dg_ref[...] = jnp.zeros_like(dg_ref)

        dn = None
        off = 0
        for k in range(npc):
            if nw == npc:
                w = w_refs[k][...]
                w = w.reshape(w.shape[-2], w.shape[-1])
            else:
                w = w_refs[0][off:off + dp_pieces[k].shape[1], :]
                off += dp_pieces[k].shape[1]
            t = _dot(dp_refs[k][...], w) if wt else _dot_nt(dp_refs[k][...], w)
            dn = t if dn is None else dn + t
        rstd, xhat = _rms_stats(h_ref[...])
        dh, dg = _rms_bwd(dn, g_ref[...], rstd, xhat)
        dh_ref[...] = dres_ref[...] + dh
        dg_ref[...] += dg

    in_specs = [pl.BlockSpec((tm, d), lambda i: (i, 0)), pl.BlockSpec((tm, d), lambda i: (i, 0)),
                pl.BlockSpec((1, d), lambda i: (0, 0))]
    for dp in dp_pieces:
        in_specs.append(pl.BlockSpec((tm, dp.shape[1]), lambda i: (i, 0)))
    for _, bs, idx in w_pieces:
        in_specs.append(pl.BlockSpec(bs, functools.partial(lambda i, idx: idx, idx=idx)))
    return pl.pallas_call(
        body, name="proj_bwd_act",
        grid=(tp // tm,),
        in_specs=in_specs,
        out_specs=[pl.BlockSpec((tm, d), lambda i: (i, 0)), pl.BlockSpec((1, d), lambda i: (0, 0))],
        out_shape=[jax.ShapeDtypeStruct((tp, d), F32), jax.ShapeDtypeStruct((1, d), F32)],
        compiler_params=_cparams(1),
    )(dres, h, norm, *dp_pieces, *[w for w, _, _ in w_pieces])


def _proj_bwd_w(n, dp_pieces, tm, wt=False):
    tp, d = n.shape
    npc = len(dp_pieces)
    widths = [dp.shape[1] for dp in dp_pieces]
    oshape = (lambda w: (w, d)) if wt else (lambda w: (d, w))

    def body(*refs):
        n_ref = refs[0]
        dp_refs = refs[1:1 + npc]
        o_refs = refs[1 + npc:1 + 2 * npc]
        accs = refs[1 + 2 * npc:]
        i = pl.program_id(0)

        @pl.when(i == 0)
        def _():
            for acc in accs:
                acc[...] = jnp.zeros_like(acc)

        nn = n_ref[...]
        for k in range(npc):
            accs[k][...] += _dot_tn(dp_refs[k][...], nn) if wt else _dot_tn(nn, dp_refs[k][...])

        @pl.when(i == pl.num_programs(0) - 1)
        def _():
            for k in range(npc):
                o_refs[k][...] = accs[k][...].astype(BF16)

    return pl.pallas_call(
        body, name="proj_bwd_w",
        grid=(tp // tm,),
        in_specs=[pl.BlockSpec((tm, d), lambda i: (i, 0))]
        + [pl.BlockSpec((tm, w), lambda i: (i, 0)) for w in widths],
        out_specs=[pl.BlockSpec(oshape(w), lambda i: (0, 0)) for w in widths],
        out_shape=[jax.ShapeDtypeStruct(oshape(w), BF16) for w in widths],
        scratch_shapes=[pltpu.VMEM(oshape(w), F32) for w in widths],
        compiler_params=_cparams(1),
    )(n, *dp_pieces)


def _out_fwd(h, ya, yb, w, tm):
    tp, d = h.shape
    na, nb = ya.shape[1], yb.shape[1]

    def body(h_ref, ya_ref, yb_ref, w_ref, o_ref):
        y = _dot(ya_ref[...].astype(BF16), w_ref[0:na, :]) + _dot(yb_ref[...].astype(BF16), w_ref[na:, :])
        o_ref[...] = h_ref[...] + y

    return pl.pallas_call(
        body, name="out_fwd",
        grid=(tp // tm,),
        in_specs=[pl.BlockSpec((tm, d), lambda i: (i, 0)), pl.BlockSpec((tm, na), lambda i: (i, 0)),
                  pl.BlockSpec((tm, nb), lambda i: (i, 0)), pl.BlockSpec((d, d), lambda i: (0, 0))],
        out_specs=pl.BlockSpec((tm, d), lambda i: (i, 0)),
        out_shape=jax.ShapeDtypeStruct((tp, d), F32),
        compiler_params=_cparams(1),
    )(h, ya, yb, w)


def _out_bwd(dy, ya, yb, w, tm, after=()):
    tp, d = dy.shape
    na, nb = ya.shape[1], yb.shape[1]

    def body(dy_ref, ya_ref, yb_ref, w_ref, *rest):
        da_ref, db_ref, dw_ref, acc = rest[len(after):]
        i = pl.program_id(0)

        @pl.when(i == 0)
        def _():
            acc[...] = jnp.zeros_like(acc)

        dyb16 = dy_ref[...].astype(BF16)
        da_ref[...] = _dot_nt(dyb16, w_ref[0:na, :])
        db_ref[...] = _dot_nt(dyb16, w_ref[na:, :])
        acc[0:na, :] += _dot_tn(ya_ref[...].astype(BF16), dyb16)
        acc[na:, :] += _dot_tn(yb_ref[...].astype(BF16), dyb16)

        @pl.when(i == pl.num_programs(0) - 1)
        def _():
            dw_ref[...] = acc[...].astype(BF16)

    return pl.pallas_call(
        body, name="out_bwd",
        grid=(tp // tm,),
        in_specs=[pl.BlockSpec((tm, d), lambda i: (i, 0)), pl.BlockSpec((tm, na), lambda i: (i, 0)),
                  pl.BlockSpec((tm, nb), lambda i: (i, 0)), pl.BlockSpec((d, d), lambda i: (0, 0))]
        + [pl.BlockSpec(memory_space=pl.ANY)] * len(after),
        out_specs=[pl.BlockSpec((tm, na), lambda i: (i, 0)), pl.BlockSpec((tm, nb), lambda i: (i, 0)),
                   pl.BlockSpec((d, d), lambda i: (0, 0))],
        out_shape=[jax.ShapeDtypeStruct((tp, na), F32), jax.ShapeDtypeStruct((tp, nb), F32),
                   jax.ShapeDtypeStruct((d, d), BF16)],
        scratch_shapes=[pltpu.VMEM((d, d), F32)],
        compiler_params=_cparams(1),
    )(dy, ya, yb, w, *after)


def _loss_bwd(h, gfin, tgt, t_real, tm):
    tp, d = h.shape

    def body(h_ref, g_ref, t_ref, loss_ref, dh_ref, dg_ref):
        i = pl.program_id(0)

        @pl.when(i == 0)
        def _():
            loss_ref[...] = jnp.zeros_like(loss_ref)
            dg_ref[...] = jnp.zeros_like(dg_ref)

        rows = i * tm + lax.broadcasted_iota(jnp.int32, (tm, 1), 0)
        valid = (rows >= N_META) & (rows < t_real)
        rstd, xhat = _rms_stats(h_ref[...])
        g = g_ref[...]
        err = jnp.where(valid, xhat * g - t_ref[...], 0.0)
        e2 = jnp.sum(err * err, axis=1, keepdims=True)
        loss_ref[...] += (0.5 / d) * jnp.sum(e2, axis=0, keepdims=True)
        dy = err * (1.0 / d)
        dh, dg = _rms_bwd(dy, g, rstd, xhat)
        dh_ref[...] = dh
        dg_ref[...] += dg

    return pl.pallas_call(
        body, name="loss_bwd",
        grid=(tp // tm,),
        in_specs=[pl.BlockSpec((tm, d), lambda i: (i, 0)), pl.BlockSpec((1, d), lambda i: (0, 0)),
                  pl.BlockSpec((tm, d), lambda i: (i, 0))],
        out_specs=[pl.BlockSpec((1, 1), lambda i: (0, 0)), pl.BlockSpec((tm, d), lambda i: (i, 0)),
                   pl.BlockSpec((1, d), lambda i: (0, 0))],
        out_shape=[jax.ShapeDtypeStruct((1, 1), F32), jax.ShapeDtypeStruct((tp, d), F32),
                   jax.ShapeDtypeStruct((1, d), F32)],
        compiler_params=_cparams(1),
    )(h, gfin, tgt)


POOL_HALO = 16


def _pool_lane_consts(n_rows):
    lane = lax.broadcasted_iota(jnp.int32, (n_rows, D_POOL), 1)
    grp = lane // POOL_GROUP
    win = jnp.where(grp == 0, 2.0, jnp.where(grp == 1, 4.0, jnp.where(grp == 2, 8.0, 16.0)))
    return grp, win


def _pool_select(grp, s2, s4, s8, s16):
    return jnp.where(grp == 0, s2, jnp.where(grp == 1, s4, jnp.where(grp == 2, s8, s16)))


def _pool_mixed(x, row0, tr):
    n = tr + POOL_HALO
    s2 = x + pltpu.roll(x, 1, 0)
    s4 = s2 + pltpu.roll(s2, 2, 0)
    s8 = s4 + pltpu.roll(s4, 4, 0)
    s16 = s8 + pltpu.roll(s8, 8, 0)
    grp, win = _pool_lane_consts(n)
    rows = row0 - POOL_HALO + lax.broadcasted_iota(jnp.int32, (n, D_POOL), 0)
    cnt = jnp.minimum((rows + 1).astype(F32), win)
    pooled = _pool_select(grp, s2, s4, s8, s16) / jnp.maximum(cnt, 1.0)
    return (pooled - x)[POOL_HALO:, :]


def _pool_fwd(p, wbd, scale, tr):
    tp = p.shape[0]
    nt = tp // tr

    def body(p_ref, w_ref, s_ref, y_ref, usc):
        usc[0:POOL_HALO, :] = jnp.zeros((POOL_HALO, D_POOL), F32)
        usc[POOL_HALO:, :] = p_ref[...]

        def tile(r, carry):
            r0 = pl.multiple_of(r * tr, SUBLANE)
            x = usc[pl.ds(r0, tr + POOL_HALO), :]
            mixed = _pool_mixed(x, r0, tr)
            y_ref[pl.ds(r0, tr), :] = (_dot(mixed.astype(BF16), w_ref[...]) * s_ref[...]).astype(BF16)
            return carry

        lax.fori_loop(0, nt, tile, 0)

    return pl.pallas_call(
        body, name="pool_fwd",
        grid=(1,),
        in_specs=[pl.BlockSpec((tp, D_POOL), lambda i: (0, 0)), pl.BlockSpec((D_POOL, D_POOL), lambda i: (0, 0)),
                  pl.BlockSpec((1, D_POOL), lambda i: (0, 0))],
        out_specs=pl.BlockSpec((tp, D_POOL), lambda i: (0, 0)),
        out_shape=jax.ShapeDtypeStruct((tp, D_POOL), BF16),
        scratch_shapes=[pltpu.VMEM((tp + POOL_HALO, D_POOL), F32)],
        compiler_params=_cparams(1),
    )(p, wbd, scale)


def _pool_bwd(p, dya, wbd, scale, tr):
    tp = p.shape[0]
    nt = tp // tr

    def body(p_ref, dy_ref, w_ref, s_ref, du_ref, dw_ref, ds_ref, usc, gsc):
        usc[0:POOL_HALO, :] = jnp.zeros((POOL_HALO, D_POOL), F32)
        usc[POOL_HALO:, :] = p_ref[...]
        gsc[tp:, :] = jnp.zeros((POOL_HALO, D_POOL), F32)
        dw_ref[...] = jnp.zeros_like(dw_ref)
        ds_ref[...] = jnp.zeros_like(ds_ref)
        grp, win = _pool_lane_consts(tr)

        def tile1(r, carry):
            r0 = pl.multiple_of(r * tr, SUBLANE)
            x = usc[pl.ds(r0, tr + POOL_HALO), :]
            mixed = _pool_mixed(x, r0, tr).astype(BF16)
            dy = dy_ref[pl.ds(r0, tr), :]
            dys = (dy * s_ref[...]).astype(BF16)
            ypre = _dot(mixed, w_ref[...])
            ds_ref[...] += _colsum(dy * ypre)
            dw_ref[...] += _dot_tn(mixed, dys)
            dmx = _dot_nt(dys, w_ref[...])
            rows = r0 + lax.broadcasted_iota(jnp.int32, (tr, D_POOL), 0)
            cnt = jnp.minimum((rows + 1).astype(F32), win)
            gsc[pl.ds(r0, tr), :] = dmx / cnt
            return carry

        lax.fori_loop(0, nt, tile1, 0)
        n = tr + POOL_HALO
        grp2, win2 = _pool_lane_consts(n)

        def tile2(r, carry):
            r0 = pl.multiple_of(r * tr, SUBLANE)
            g = gsc[pl.ds(r0, n), :]
            s2 = g + pltpu.roll(g, n - 1, 0)
            s4 = s2 + pltpu.roll(s2, n - 2, 0)
            s8 = s4 + pltpu.roll(s4, n - 4, 0)
            s16 = s8 + pltpu.roll(s8, n - 8, 0)
            pooled_t = _pool_select(grp2, s2, s4, s8, s16)
            rows = r0 + lax.broadcasted_iota(jnp.int32, (n, D_POOL), 0)
            cnt = jnp.minimum((rows + 1).astype(F32), win2)
            du = pooled_t - g * cnt
            du_ref[pl.ds(r0, tr), :] = du[0:tr, :].astype(BF16)
            return carry

        lax.fori_loop(0, nt, tile2, 0)

    return pl.pallas_call(
        body, name="pool_bwd",
        grid=(1,),
        in_specs=[pl.BlockSpec((tp, D_POOL), lambda i: (0, 0)), pl.BlockSpec((tp, D_POOL), lambda i: (0, 0)),
                  pl.BlockSpec((D_POOL, D_POOL), lambda i: (0, 0)), pl.BlockSpec((1, D_POOL), lambda i: (0, 0))],
        out_specs=[pl.BlockSpec((tp, D_POOL), lambda i: (0, 0)), pl.BlockSpec((D_POOL, D_POOL), lambda i: (0, 0)),
                   pl.BlockSpec((1, D_POOL), lambda i: (0, 0))],
        out_shape=[jax.ShapeDtypeStruct((tp, D_POOL), BF16), jax.ShapeDtypeStruct((D_POOL, D_POOL), F32),
                   jax.ShapeDtypeStruct((1, D_POOL), F32)],
        scratch_shapes=[pltpu.VMEM((tp + POOL_HALO, D_POOL), F32), pltpu.VMEM((tp + POOL_HALO, D_POOL), F32)],
        compiler_params=_cparams(1),
    )(p, dya, wbd, scale)


def _hgrn_levels(ch):
    levels = []
    w = ch // 2
    while w >= 1:
        levels.append(w)
        w //= 2
    return levels


def _hgrn_consts(ch):
    t = np.arange(ch)
    tril = t[None, :] <= t[:, None]
    masks = []
    for w in _hgrn_levels(ch):
        blk = t // (2 * w)
        upper = t % (2 * w) >= w
        masks.append(upper[:, None] & (~upper)[None, :] & (blk[:, None] == blk[None, :]))
    masks.append(tril)
    msk = np.stack(masks).astype(np.float32)
    return jnp.asarray(tril.astype(np.float32), BF16), jnp.asarray(msk, F32), len(masks) - 1


def _split3(x):
    hi = x.astype(BF16)
    r1 = x - hi.astype(F32)
    mid = r1.astype(BF16)
    lo = (r1 - mid.astype(F32)).astype(BF16)
    return hi, mid, lo


def _hgrn_exponents(tril, logf):
    ch = logf.shape[0]
    hi, mid, lo = _split3(logf)
    x = _dot(tril, jnp.concatenate([hi, mid, lo], axis=1))
    b = x[:, 0:HEAD] + x[:, HEAD:2 * HEAD] + x[:, 2 * HEAD:3 * HEAD]
    rows = lax.broadcasted_iota(jnp.int32, (ch, HEAD), 0)
    fx = jnp.broadcast_to(b[ch - 1:ch, :], (ch, HEAD)) - b
    lev = []
    for w in _hgrn_levels(ch):
        pos = rows % (2 * w)
        upper = pos >= w
        if w >= SUBLANE:
            parts = [jnp.broadcast_to(b[k * 2 * w + w - 1:k * 2 * w + w, :], (2 * w, HEAD))
                     for k in range(ch // (2 * w))]
            bmid = parts[0] if len(parts) == 1 else jnp.concatenate(parts, axis=0)
            dx = jnp.where(upper, b - bmid, 0.0)
            ex = jnp.where(upper, 0.0, bmid - b)
        else:
            dx = logf
            ex = jnp.zeros_like(logf)
            for i in range(1, w):
                dx = dx + jnp.where(pos >= w + i, pltpu.roll(logf, i, 0), 0.0)
                ex = ex + jnp.where(pos <= w - 1 - i, pltpu.roll(logf, ch - i, 0), 0.0)
            dx = jnp.where(upper, dx, 0.0)
        lev.append((dx, ex))
    return b, fx, lev


def _hgrn_exponents_bwd(tril, d_b, d_fx, d_blast, lev_grads):
    ch = d_b.shape[0]
    rows = lax.broadcasted_iota(jnp.int32, (ch, HEAD), 0)
    db = d_b - d_fx
    dlf = jnp.zeros_like(d_b)
    for w, (ddx, dex) in zip(_hgrn_levels(ch), lev_grads):
        pos = rows % (2 * w)
        upper = pos >= w
        gu = jnp.where(upper, ddx, 0.0)
        if w >= SUBLANE:
            gl = jnp.where(upper, 0.0, dex)
            db = db + gu - gl
            diff = gl - gu
            for k in range(ch // (2 * w)):
                s = _colsum(diff[k * 2 * w:(k + 1) * 2 * w, :])
                db = db + jnp.where(rows == k * 2 * w + w - 1, s, 0.0)
        else:
            dlf = dlf + gu
            for i in range(1, w):
                dlf = dlf + pltpu.roll(jnp.where(pos >= w + i, gu, 0.0), ch - i, 0)
                dlf = dlf + pltpu.roll(jnp.where(pos <= w - 1 - i, dex, 0.0), i, 0)
    db = db + jnp.where(rows == ch - 1, _colsum(d_fx) + d_blast, 0.0)
    hi = db.astype(BF16)
    lo = (db - hi.astype(F32)).astype(BF16)
    d2 = _dot_tn(tril, jnp.concatenate([hi, lo], axis=1))
    return d2[:, 0:HEAD] + d2[:, HEAD:2 * HEAD] + dlf


def _lockstep(gens):
    results = [None] * len(gens)
    live = list(range(len(gens)))
    while live:
        for i in list(live):
            try:
                next(gens[i])
            except StopIteration as stop:
                results[i] = stop.value
                live.remove(i)
    return results


def _hgrn_gates(q_raw, z, lb):
    sz = _sigmoid(z)
    f = lb + (1.0 - lb) * sz
    q = q_raw * _sigmoid(q_raw)
    k = (1.0 - lb) * (1.0 - sz)
    return q, k, f, sz


def _hgrn_intra(q, k, lev, msk_ref, n_lev, ch):
    eye = (lax.broadcasted_iota(jnp.int32, (ch, ch), 0) == lax.broadcasted_iota(jnp.int32, (ch, ch), 1))
    a = jnp.where(eye, jnp.sum(q * k, axis=1, keepdims=True), 0.0)
    ops = []
    for lv in range(n_lev):
        eq = jnp.exp(lev[lv][0])
        ek = jnp.exp(lev[lv][1])
        qd = q * eq
        kd = k * ek
        a = a + msk_ref[lv] * _dot_nt(qd.astype(BF16), kd.astype(BF16))
        ops.append((eq, ek, qd, kd))
        yield
    return a, ops


def _hgrn_fwd(p, lb_logits, gnorm, mst, msk, n_lev, tm):
    tp = p.shape[0]
    ch = HG_CHUNK
    nct = tm // ch
    nt = tp // tm
    nr = mst.shape[0]
    base = D_POOL // HEAD

    hp = HG_HEADS_PER_STEP
    wide = hp * HEAD
    npr = wide // HG_PBLOCK

    def body(*refs):
        p_refs = refs[:4 * npr]
        lg_ref, gn_ref, mst_ref, msk_ref, y_ref, ss_ref, st_sc = refs[4 * npr:]

        @pl.when(pl.program_id(1) == 0)
        def _():
            st_sc[...] = jnp.zeros_like(st_sc)

        lb_all = _sigmoid(lg_ref[0:1, :] - lg_ref[1:2, :])

        def raw(seg, hh, r0):
            per = HG_PBLOCK // HEAD
            return p_refs[seg * npr + hh // per][pl.ds(r0, ch), (hh % per) * HEAD:(hh % per + 1) * HEAD]

        def one_head(hh, c, r0):
            ls = slice(hh * HEAD, (hh + 1) * HEAD)
            q_raw, z, v, g_raw, st = raw(0, hh, r0), raw(1, hh, r0), raw(2, hh, r0), raw(3, hh, r0), st_sc[hh]
            q, k, f, _ = _hgrn_gates(q_raw, z, lb_all[:, ls])
            yield
            b, fx, lev = _hgrn_exponents(mst_ref[...], jnp.log(f))
            yield
            qe = q * jnp.exp(b)
            a, _ = yield from _hgrn_intra(q, k, lev, msk_ref, n_lev, ch)
            v16 = v.astype(BF16)
            o = _dot_nt(qe.astype(BF16), st.astype(BF16)) + _dot(a.astype(BF16), v16)
            kl = k * jnp.exp(fx)
            st_new = st * jnp.exp(b[ch - 1:ch, :]) + _dot_tn(v16, kl.astype(BF16))
            yield
            rstd = lax.rsqrt(jnp.mean(o * o, axis=-1, keepdims=True) + EPS)
            return st, st_new, o * rstd * gn_ref[...] * (g_raw * _sigmoid(g_raw))

        def chunk(c, carry):
            r0 = pl.multiple_of(c * ch, ch)
            results = _lockstep([one_head(hh, c, r0) for hh in range(hp)])
            for hh, (st, st_new, y) in enumerate(results):
                ss_ref[hh, c] = st
                st_sc[hh] = st_new
                y_ref[pl.ds(r0, ch), hh * HEAD:(hh + 1) * HEAD] = y.astype(BF16)
            return carry

        lax.fori_loop(0, nct, chunk, 0)

    def pspec(seg, part):
        return pl.BlockSpec((tm, HG_PBLOCK),
                            lambda h, i: (i, (base + seg * HG_HEADS) * HEAD // HG_PBLOCK + h * npr + part))

    return pl.pallas_call(
        body, name="hgrn_fwd",
        grid=(HG_HEADS // hp, nt),
        in_specs=[pspec(seg, part) for seg in range(4) for part in range(npr)]
        + [pl.BlockSpec((2, wide), lambda h, i: (0, h)),
           pl.BlockSpec((1, HEAD), lambda h, i: (0, 0)),
           pl.BlockSpec((nr, ch), lambda h, i: (0, 0)),
           pl.BlockSpec((n_lev + 1, ch, ch), lambda h, i: (0, 0, 0))],
        out_specs=[pl.BlockSpec((tm, wide), lambda h, i: (i, h)),
                   pl.BlockSpec((hp, nct, HEAD, HEAD), lambda h, i: (h, i, 0, 0))],
        out_shape=[jax.ShapeDtypeStruct((tp, D_HGRN), BF16),
                   jax.ShapeDtypeStruct((HG_HEADS, tp // ch, HEAD, HEAD), F32)],
        scratch_shapes=[pltpu.VMEM((hp, HEAD, HEAD), F32)],
        compiler_params=_cparams(2),
    )(*([p] * (4 * npr)), lb_logits, gnorm, mst, msk)


def _hgrn_bwd(p, dyb, states, lb_logits, gnorm, mst, msk, n_lev, tm):
    tp = p.shape[0]
    ch = HG_CHUNK
    nct = tm // ch
    nt = tp // tm
    nr = mst.shape[0]
    base = D_POOL // HEAD

    hp = HG_HEADS_PER_STEP
    wide = hp * HEAD
    npr = wide // HG_PBLOCK

    def body(*refs):
        p_refs = refs[:4 * npr]
        (dy_ref, ss_ref, lg_ref, gn_ref, mst_ref, msk_ref,
         dq_ref, dz_ref, dv_ref, dg_ref, dlg_ref, dgn_ref, dst_sc, dlb_sc) = refs[4 * npr:]
        ti = pl.program_id(1)

        def raw(seg, hh, r0):
            per = HG_PBLOCK // HEAD
            return p_refs[seg * npr + hh // per][pl.ds(r0, ch), (hh % per) * HEAD:(hh % per + 1) * HEAD]

        @pl.when(ti == 0)
        def _():
            dst_sc[...] = jnp.zeros_like(dst_sc)
            dlb_sc[...] = jnp.zeros_like(dlb_sc)
            dgn_ref[...] = jnp.zeros_like(dgn_ref)

        lb_all = _sigmoid(lg_ref[0:1, :] - lg_ref[1:2, :])
        gn = gn_ref[...]

        def load_head(hh, c, r0):
            ls = slice(hh * HEAD, (hh + 1) * HEAD)
            return (raw(0, hh, r0), raw(1, hh, r0), raw(2, hh, r0), raw(3, hh, r0),
                    dy_ref[pl.ds(r0, ch), ls], ss_ref[hh, c], dst_sc[hh])

        def store_head(hh, r0, res):
            ls = slice(hh * HEAD, (hh + 1) * HEAD)
            dq_raw, dz, dv, dg_raw, dgn, dst_new, dlb = res
            dq_ref[pl.ds(r0, ch), ls] = dq_raw
            dz_ref[pl.ds(r0, ch), ls] = dz
            dv_ref[pl.ds(r0, ch), ls] = dv
            dg_ref[pl.ds(r0, ch), ls] = dg_raw
            dgn_ref[hh] += dgn
            dst_sc[hh] = dst_new
            dlb_sc[:, ls] += dlb

        def one_head(hh, loaded):
            ls = slice(hh * HEAD, (hh + 1) * HEAD)
            lb = lb_all[:, ls]
            q_raw, z, v, g_raw, dy, st, dst = loaded
            q, k, f, sz = _hgrn_gates(q_raw, z, lb)
            yield
            b, fx, lev = _hgrn_exponents(mst_ref[...], jnp.log(f))
            yield
            eb = jnp.exp(b)
            ef = jnp.exp(fx)
            elast = jnp.exp(b[ch - 1:ch, :])
            qe = q * eb
            kl = k * ef
            a, ops = yield from _hgrn_intra(q, k, lev, msk_ref, n_lev, ch)
            v16 = v.astype(BF16)
            st16 = st.astype(BF16)
            qe16 = qe.astype(BF16)
            kl16 = kl.astype(BF16)
            a16 = a.astype(BF16)
            o = _dot_nt(qe16, st16) + _dot(a16, v16)
            yield
            sg = _sigmoid(g_raw)
            rstd = lax.rsqrt(jnp.mean(o * o, axis=-1, keepdims=True) + EPS)
            oh = o * rstd
            dg_out = (dy * oh * gn * (sg * (1.0 + g_raw * (1.0 - sg)))).astype(BF16)
            don = dy * (g_raw * sg)
            dgn = _colsum(don * oh)
            doh = don * gn
            do = rstd * (doh - oh * jnp.mean(doh * oh, axis=-1, keepdims=True))
            do16 = do.astype(BF16)
            dst16 = dst.astype(BF16)
            yield
            dv = _dot_tn(a16, do16) + _dot_nt(kl16, dst16)
            da = msk_ref[n_lev] * _dot_nt(do16, v16)
            dqe = _dot(do16, st16)
            dkl = _dot(v16, dst16)
            dst_new = dst * elast + _dot_tn(do16, qe16)
            yield
            db_last = _colsum(dst * st) * elast
            dad = jnp.sum(do * v, axis=1, keepdims=True)
            dq = dad * k + dqe * eb
            dk = dad * q + dkl * ef
            lev_grads = []
            for lv in range(n_lev):
                eq, ek, qd, kd = ops[lv]
                gl = (msk_ref[lv] * da).astype(BF16)
                dqd = _dot(gl, kd.astype(BF16))
                dkd = _dot_tn(gl, qd.astype(BF16))
                dq = dq + dqd * eq
                dk = dk + dkd * ek
                lev_grads.append((dqd * qd, dkd * kd))
                yield
            dlogf = _hgrn_exponents_bwd(mst_ref[...], dqe * qe, dkl * kl, db_last, lev_grads)
            yield
            sq = _sigmoid(q_raw)
            dq_out = (dq * (sq * (1.0 + q_raw * (1.0 - sq)))).astype(BF16)
            dfk = dlogf / f - dk
            dz_out = (dfk * (1.0 - lb) * sz * (1.0 - sz)).astype(BF16)
            return dq_out, dz_out, dv.astype(BF16), dg_out, dgn, dst_new, _colsum(dfk * (1.0 - sz))

        def chunk(cc, carry):
            c = nct - 1 - cc
            r0 = pl.multiple_of(c * ch, ch)
            loaded = [load_head(hh, c, r0) for hh in range(HG_HEADS_PER_STEP)]
            results = _lockstep([one_head(hh, loaded[hh]) for hh in range(HG_HEADS_PER_STEP)])
            for hh in range(HG_HEADS_PER_STEP):
                store_head(hh, r0, results[hh])
            return carry

        lax.fori_loop(0, nct, chunk, 0, unroll=1)

        @pl.when(ti == nt - 1)
        def _():
            dl0 = dlb_sc[...] * lb_all * (1.0 - lb_all)
            dlg_ref[0:1, :] = dl0
            dlg_ref[1:2, :] = -dl0

    def pspec(seg, part):
        return pl.BlockSpec((tm, HG_PBLOCK), lambda h, i: (
            nt - 1 - i, (base + seg * HG_HEADS) * HEAD // HG_PBLOCK + h * npr + part))

    ospec = pl.BlockSpec((tm, wide), lambda h, i: (nt - 1 - i, h))
    return pl.pallas_call(
        body, name="hgrn_bwd",
        grid=(HG_HEADS // hp, nt),
        in_specs=[pspec(seg, part) for seg in range(4) for part in range(npr)]
        + [ospec, pl.BlockSpec((hp, nct, HEAD, HEAD), lambda h, i: (h, nt - 1 - i, 0, 0)),
           pl.BlockSpec((2, wide), lambda h, i: (0, h)),
           pl.BlockSpec((1, HEAD), lambda h, i: (0, 0)),
           pl.BlockSpec((nr, ch), lambda h, i: (0, 0)),
           pl.BlockSpec((n_lev + 1, ch, ch), lambda h, i: (0, 0, 0))],
        out_specs=[ospec, ospec, ospec, ospec,
                   pl.BlockSpec((2, wide), lambda h, i: (0, h)),
                   pl.BlockSpec((hp, 1, HEAD), lambda h, i: (h, 0, 0))],
        out_shape=[jax.ShapeDtypeStruct((tp, D_HGRN), BF16)] * 4
        + [jax.ShapeDtypeStruct((2, D_HGRN), F32), jax.ShapeDtypeStruct((HG_HEADS, 1, HEAD), F32)],
        scratch_shapes=[pltpu.VMEM((hp, HEAD, HEAD), F32), pltpu.VMEM((1, wide), F32)],
        compiler_params=_cparams(2),
    )(*([p] * (4 * npr)), dyb, states, lb_logits, gnorm, mst, msk)


def _tap_views(x, tr, halo, width):
    subs = {0: x}
    views = []
    for j in range(width):
        tiles, rem = divmod(width - 1 - j, SUBLANE)
        if rem not in subs:
            subs[rem] = pltpu.roll(x, rem, 0)
        start = halo - tiles * SUBLANE
        views.append(subs[rem][start:start + tr, :])
    return views


def _tap_views_t(y, tr, halo, width):
    n = tr + halo
    subs = {0: y}
    views = []
    for j in range(width):
        tiles, rem = divmod(width - 1 - j, SUBLANE)
        if rem not in subs:
            subs[rem] = pltpu.roll(y, n - rem, 0)
        views.append(subs[rem][tiles * SUBLANE:tiles * SUBLANE + tr, :])
    return views


def _weighted_sum(views, w_ref):
    acc = None
    for j, view in enumerate(views):
        term = view * w_ref[j:j + 1, :]
        acc = term if acc is None else acc + term
    return acc


def _conv_taps(x, w_ref, tr, halo, width):
    return _weighted_sum(_tap_views(x, tr, halo, width), w_ref)


def _conv_taps_t(y, w_ref, tr, halo, width):
    return _weighted_sum(_tap_views_t(y, tr, halo, width), w_ref)


def _ln_stats(cv):
    mu = jnp.mean(cv, axis=-1, keepdims=True)
    xc = cv - mu
    rstd = lax.rsqrt(jnp.mean(xc * xc, axis=-1, keepdims=True) + EPS)
    return rstd, xc * rstd


def _convmod_fwd(p, w, bias, ln_g, ln_b, tr):
    tp = p.shape[0]
    nt = tp // tr
    nb = D_CONV // HEAD

    def body(a_ref, b_ref, w_ref, bi_ref, g_ref, be_ref, y_ref, cv_ref, usc):
        usc[0:CONV_HALO, :] = jnp.zeros((CONV_HALO, HEAD), F32)
        usc[CONV_HALO:, :] = a_ref[...] * _sigmoid(b_ref[...])

        def tile(r, carry):
            r0 = pl.multiple_of(r * tr, SUBLANE)
            x = usc[pl.ds(r0, tr + CONV_HALO), :]
            cv = _conv_taps(x, w_ref, tr, CONV_HALO, CONV_WIDTH) + bi_ref[...]
            cv_ref[pl.ds(r0, tr), :] = cv
            _, xh = _ln_stats(cv)
            un = xh * g_ref[...] + be_ref[...]
            y_ref[pl.ds(r0, tr), :] = (un * _sigmoid(un)).astype(BF16)
            return carry

        lax.fori_loop(0, nt, tile, 0)

    vec = lambda: pl.BlockSpec((1, HEAD), lambda j: (0, j))
    return pl.pallas_call(
        body, name="convmod_fwd",
        grid=(nb,),
        in_specs=[pl.BlockSpec((tp, HEAD), lambda j: (0, j)), pl.BlockSpec((tp, HEAD), lambda j: (0, nb + j)),
                  pl.BlockSpec((CONV_HALO, HEAD), lambda j: (0, j)), vec(), vec(), vec()],
        out_specs=[pl.BlockSpec((tp, HEAD), lambda j: (0, j))] * 2,
        out_shape=[jax.ShapeDtypeStruct((tp, D_CONV), BF16), jax.ShapeDtypeStruct((tp, D_CONV), F32)],
        scratch_shapes=[pltpu.VMEM((tp + CONV_HALO, HEAD), F32)],
        compiler_params=_cparams(1),
    )(p, p, w, bias, ln_g, ln_b)


def _convmod_bwd(p, cv_saved, dyc, w, ln_g, ln_b, tr):
    tp = p.shape[0]
    nt = tp // tr
    nb = D_CONV // HEAD

    def body(a_ref, b_ref, cv_ref, dy_ref, w_ref, g_ref, be_ref, da_ref, db_ref, dw_ref, dv_ref, usc, dsc):
        usc[0:CONV_HALO, :] = jnp.zeros((CONV_HALO, HEAD), F32)
        usc[CONV_HALO:, :] = a_ref[...] * _sigmoid(b_ref[...])
        dsc[tp:, :] = jnp.zeros((CONV_HALO, HEAD), F32)
        dw_ref[...] = jnp.zeros_like(dw_ref)
        dv_ref[...] = jnp.zeros_like(dv_ref)

        def tile1(r, carry):
            r0 = pl.multiple_of(r * tr, SUBLANE)
            x = usc[pl.ds(r0, tr + CONV_HALO), :]
            views = _tap_views(x, tr, CONV_HALO, CONV_WIDTH)
            rstd, xh = _ln_stats(cv_ref[pl.ds(r0, tr), :])
            un = xh * g_ref[...] + be_ref[...]
            sg = _sigmoid(un)
            dun = dy_ref[pl.ds(r0, tr), :] * (sg * (1.0 + un * (1.0 - sg)))
            dv_ref[0, 1:2, :] += _colsum(dun * xh)
            dv_ref[0, 2:3, :] += _colsum(dun)
            dxh = dun * g_ref[...]
            dcv = rstd * (dxh - jnp.mean(dxh, axis=-1, keepdims=True)
                          - xh * jnp.mean(dxh * xh, axis=-1, keepdims=True))
            dv_ref[0, 0:1, :] += _colsum(dcv)
            for j in range(CONV_WIDTH):
                dw_ref[0, j:j + 1, :] += _colsum(dcv * views[j])
            dsc[pl.ds(r0, tr), :] = dcv
            return carry

        lax.fori_loop(0, nt, tile1, 0)

        def tile2(r, carry):
            r0 = pl.multiple_of(r * tr, SUBLANE)
            y = dsc[pl.ds(r0, tr + CONV_HALO), :]
            du = _conv_taps_t(y, w_ref, tr, CONV_HALO, CONV_WIDTH)
            a = a_ref[pl.ds(r0, tr), :]
            sb = _sigmoid(b_ref[pl.ds(r0, tr), :])
            da_ref[pl.ds(r0, tr), :] = (du * sb).astype(BF16)
            db_ref[pl.ds(r0, tr), :] = (du * a * sb * (1.0 - sb)).astype(BF16)
            return carry

        lax.fori_loop(0, nt, tile2, 0)

    vec = lambda: pl.BlockSpec((1, HEAD), lambda j: (0, j))
    col = lambda: pl.BlockSpec((tp, HEAD), lambda j: (0, j))
    return pl.pallas_call(
        body, name="convmod_bwd",
        grid=(nb,),
        in_specs=[col(), pl.BlockSpec((tp, HEAD), lambda j: (0, nb + j)), col(), col(),
                  pl.BlockSpec((CONV_HALO, HEAD), lambda j: (0, j)), vec(), vec()],
        out_specs=[col(), col(), pl.BlockSpec((1, CONV_HALO, HEAD), lambda j: (j, 0, 0)),
                   pl.BlockSpec((1, SUBLANE, HEAD), lambda j: (j, 0, 0))],
        out_shape=[jax.ShapeDtypeStruct((tp, D_CONV), BF16), jax.ShapeDtypeStruct((tp, D_CONV), BF16),
                   jax.ShapeDtypeStruct((nb, CONV_HALO, HEAD), F32), jax.ShapeDtypeStruct((nb, SUBLANE, HEAD), F32)],
        scratch_shapes=[pltpu.VMEM((tp + CONV_HALO, HEAD), F32), pltpu.VMEM((tp + CONV_HALO, HEAD), F32)],
        compiler_params=_cparams(1),
    )(p, p, cv_saved, dyc, w, ln_g, ln_b)


def _log1p_small(y):
    return jnp.where(y < 1e-4, y * (1.0 - 0.5 * y), jnp.log(1.0 + y))


def _softplus(x):
    return jnp.maximum(x, 0.0) + _log1p_small(jnp.exp(-jnp.abs(x)))


def _expm1(x):
    return jnp.where(jnp.abs(x) < 1e-2, x * (1.0 + 0.5 * x * (1.0 + x * (1.0 / 3.0))), jnp.exp(x) - 1.0)


def _gelu_parts(x):
    c = 0.7978845608028654
    inner = c * (x + 0.044715 * x * x * x)
    th = jnp.tanh(inner)
    gelu = 0.5 * x * (1.0 + th)
    dgelu = 0.5 * (1.0 + th) + 0.5 * x * (1.0 - th * th) * c * (1.0 + 3.0 * 0.044715 * x * x)
    return gelu, dgelu


def _lru_gates(x_all, tp, cw_ref, cb_ref, wa_ref, ba_ref, wx_ref, bx_ref, lam_ref):
    u = _conv_taps(x_all, cw_ref, tp, LRU_HALO, LRU_CONV) + cb_ref[...]
    u16 = u.astype(BF16)
    r = _sigmoid(_dot(u16, wa_ref[0]) + ba_ref[...])
    i = _sigmoid(_dot(u16, wx_ref[0]) + bx_ref[...])
    sp = _softplus(-lam_ref[...])
    la = -LRU_C * r * sp
    a = jnp.exp(la)
    mult = jnp.sqrt(-_expm1(2.0 * la))
    return u, r, i, a, mult, sp


def _lru_specs(tp, nb):
    col = lambda k: pl.BlockSpec((tp, HEAD), functools.partial(lambda j, k: (0, k * nb + j), k=k))
    vec = lambda: pl.BlockSpec((1, HEAD), lambda j: (0, j))
    mat = lambda: pl.BlockSpec((1, HEAD, HEAD), lambda j: (j, 0, 0))
    return col, vec, mat


def _lru_fwd(p, cw, cb, wa, ba, wx, bx, lam):
    tp = p.shape[0]
    nb = D_LRU // HEAD
    ng = tp // SUBLANE

    def body(x_ref, gt_ref, cw_ref, cb_ref, wa_ref, ba_ref, wx_ref, bx_ref, lam_ref, y_ref, hs_ref,
             xsc, asc, bsc):
        xsc[0:LRU_HALO, :] = jnp.zeros((LRU_HALO, HEAD), F32)
        xsc[LRU_HALO:, :] = x_ref[...]
        u, r, i, a, mult, _ = _lru_gates(xsc[...], tp, cw_ref, cb_ref, wa_ref, ba_ref, wx_ref, bx_ref, lam_ref)
        rows = lax.broadcasted_iota(jnp.int32, (tp, HEAD), 0)
        b = jnp.where(rows == 0, 1.0, mult) * (i * u)
        sub = rows % SUBLANE
        for k in (1, 2, 4):
            m = sub >= k
            b = jnp.where(m, a * pltpu.roll(b, k, 0) + b, b)
            a = jnp.where(m, a * pltpu.roll(a, k, 0), a)
        asc[...] = a
        bsc[...] = b

        def grp(g, carry):
            r0 = pl.multiple_of(g * SUBLANE, SUBLANE)
            h = bsc[pl.ds(r0, SUBLANE), :] + asc[pl.ds(r0, SUBLANE), :] * carry
            hs_ref[pl.ds(r0, SUBLANE), :] = h
            return jnp.broadcast_to(h[SUBLANE - 1:SUBLANE, :], (SUBLANE, HEAD))

        lax.fori_loop(0, ng, grp, jnp.zeros((SUBLANE, HEAD), F32))
        gelu, _ = _gelu_parts(gt_ref[...])
        y_ref[...] = (gelu * hs_ref[...]).astype(BF16)

    col, vec, mat = _lru_specs(tp, nb)
    return pl.pallas_call(
        body, name="lru_fwd",
        grid=(nb,),
        in_specs=[col(2), col(3), pl.BlockSpec((LRU_CONV, HEAD), lambda j: (0, j)), vec(), mat(), vec(), mat(),
                  vec(), vec()],
        out_specs=[pl.BlockSpec((tp, HEAD), lambda j: (0, j)), pl.BlockSpec((tp, HEAD), lambda j: (0, j))],
        out_shape=[jax.ShapeDtypeStruct((tp, D_LRU), BF16), jax.ShapeDtypeStruct((tp, D_LRU), F32)],
        scratch_shapes=[pltpu.VMEM((tp + LRU_HALO, HEAD), F32), pltpu.VMEM((tp, HEAD), F32),
                        pltpu.VMEM((tp, HEAD), F32)],
        compiler_params=_cparams(1),
    )(p, p, cw, cb, wa, ba, wx, bx, lam)


def _lru_bwd(p, hs, dyd, cw, cb, wa, ba, wx, bx, lam):
    tp = p.shape[0]
    nb = D_LRU // HEAD
    ng = tp // SUBLANE

    def body(x_ref, gt_ref, hs_ref, dy_ref, cw_ref, cb_ref, wa_ref, ba_ref, wx_ref, bx_ref, lam_ref,
             dx_ref, dgt_ref, dwa_ref, dwx_ref, dv_ref, xsc, asc, bsc, gsc, dusc):
        xsc[0:LRU_HALO, :] = jnp.zeros((LRU_HALO, HEAD), F32)
        xsc[LRU_HALO:, :] = x_ref[...]
        x_all = xsc[...]
        u, r, i, a, mult, sp = _lru_gates(x_all, tp, cw_ref, cb_ref, wa_ref, ba_ref, wx_ref, bx_ref, lam_ref)
        rows = lax.broadcasted_iota(jnp.int32, (tp, HEAD), 0)
        hs = hs_ref[...]
        dy = dy_ref[...]
        gelu, dgelu = _gelu_parts(gt_ref[...])
        dgt_ref[...] = (dy * hs * dgelu).astype(BF16)
        bb = dy * gelu
        aa = jnp.where(rows == tp - 1, 0.0, pltpu.roll(a, tp - 1, 0))
        sub = rows % SUBLANE
        for k in (1, 2, 4):
            m = sub < SUBLANE - k
            bb = jnp.where(m, aa * pltpu.roll(bb, tp - k, 0) + bb, bb)
            aa = jnp.where(m, aa * pltpu.roll(aa, tp - k, 0), aa)
        asc[...] = aa
        bsc[...] = bb

        def grp(gi, carry):
            g = ng - 1 - gi
            r0 = pl.multiple_of(g * SUBLANE, SUBLANE)
            gg = bsc[pl.ds(r0, SUBLANE), :] + asc[pl.ds(r0, SUBLANE), :] * carry
            gsc[pl.ds(r0, SUBLANE), :] = gg
            return jnp.broadcast_to(gg[0:1, :], (SUBLANE, HEAD))

        lax.fori_loop(0, ng, grp, jnp.zeros((SUBLANE, HEAD), F32))
        g = gsc[...]
        first = rows == 0
        hprev = jnp.where(first, 0.0, pltpu.roll(hs, 1, 0))
        iu = i * u
        d_iu = g * jnp.where(first, 1.0, mult)
        dmult_term = jnp.where(first, 0.0, g * iu * (-(a * a) / mult))
        dla = g * hprev * a + dmult_term
        dr = dla * (-LRU_C) * sp
        dv_ref[0, 7:8, :] = _colsum(dla * (LRU_C * r) * _sigmoid(-lam_ref[...]))
        dpr = dr * r * (1.0 - r)
        dpi = d_iu * u * i * (1.0 - i)
        dv_ref[0, 5:6, :] = _colsum(dpr)
        dv_ref[0, 6:7, :] = _colsum(dpi)
        u16 = u.astype(BF16)
        dpr16 = dpr.astype(BF16)
        dpi16 = dpi.astype(BF16)
        dwa_ref[0] = _dot_tn(u16, dpr16)
        dwx_ref[0] = _dot_tn(u16, dpi16)
        du = d_iu * i + _dot_nt(dpr16, wa_ref[0]) + _dot_nt(dpi16, wx_ref[0])
        dv_ref[0, 4:5, :] = _colsum(du)
        for j in range(LRU_CONV):
            sh = LRU_CONV - 1 - j
            xs = x_all if sh == 0 else pltpu.roll(x_all, sh, 0)
            dv_ref[0, j:j + 1, :] = _colsum(du * xs[LRU_HALO:, :])
        dusc[0:tp, :] = du
        dusc[tp:, :] = jnp.zeros((LRU_HALO, HEAD), F32)
        dx_ref[...] = _conv_taps_t(dusc[...], cw_ref, tp, LRU_HALO, LRU_CONV).astype(BF16)

    col, vec, mat = _lru_specs(tp, nb)
    ocol = lambda: pl.BlockSpec((tp, HEAD), lambda j: (0, j))
    return pl.pallas_call(
        body, name="lru_bwd",
        grid=(nb,),
        in_specs=[col(2), col(3), ocol(), ocol(), pl.BlockSpec((LRU_CONV, HEAD), lambda j: (0, j)), vec(), mat(),
                  vec(), mat(), vec(), vec()],
        out_specs=[ocol(), ocol(), mat(), mat(), pl.BlockSpec((1, SUBLANE, HEAD), lambda j: (j, 0, 0))],
        out_shape=[jax.ShapeDtypeStruct((tp, D_LRU), BF16), jax.ShapeDtypeStruct((tp, D_LRU), BF16),
                   jax.ShapeDtypeStruct((nb, HEAD, HEAD), F32), jax.ShapeDtypeStruct((nb, HEAD, HEAD), F32),
                   jax.ShapeDtypeStruct((nb, SUBLANE, HEAD), F32)],
        scratch_shapes=[pltpu.VMEM((tp + LRU_HALO, HEAD), F32), pltpu.VMEM((tp, HEAD), F32),
                        pltpu.VMEM((tp, HEAD), F32), pltpu.VMEM((tp, HEAD), F32),
                        pltpu.VMEM((tp + LRU_HALO, HEAD), F32)],
        compiler_params=_cparams(1),
    )(p, p, hs, dyd, cw, cb, wa, ba, wx, bx, lam)


def _mesh_pos():
    return lax.axis_index("x"), lax.axis_index("y"), lax.axis_index("c")


def _other_chips(x, y):
    return [(1 - x, y), (x, 1 - y), (1 - x, 1 - y)]


ANY = pl.BlockSpec(memory_space=pl.ANY)


HBM = pl.BlockSpec(memory_space=pltpu.HBM)
SEM = pl.BlockSpec(memory_space=pltpu.SEMAPHORE)
DATAFLOW = pltpu.SideEffectType.DATAFLOW_SIDE_EFFECTING
N_PEERS = 4


def _in_hbm(a):
    return pltpu.with_memory_space_constraint(a, pltpu.HBM)


def _gather_peers(x, y, c):
    return [((ox, oy, c), 2 * ox + oy) for ox, oy in _other_chips(x, y)] + [((x, y, 1 - c), 2 * x + y)]


def _gather_refs(src, land, slot, c, split):
    if not split:
        return src, land.at[slot]
    half = src.shape[0] // 2
    return src.at[pl.ds(c * half, half)], land.at[slot, pl.ds(c * half, half)]


def _gather_start(arrs, split):
    n = len(arrs)

    def body(*refs):
        ins, lands = refs[:n], refs[n:2 * n]
        ssem, rsem = refs[2 * n:2 * n + 2]
        token = refs[-1]
        x, y, c = _mesh_pos()
        chip = 2 * x + y
        for k in range(n):
            for j, (dev, _) in enumerate(_gather_peers(x, y, c)):
                src, dst = _gather_refs(ins[k], lands[k], chip, c, split[k] and j < N_PEERS - 1)
                pltpu.make_async_remote_copy(
                    src_ref=src, dst_ref=dst, send_sem=ssem.at[N_PEERS * k + j],
                    recv_sem=rsem.at[N_PEERS * k + j], device_id=dev, device_id_type=MESH).start()
        token[...] = jnp.zeros_like(token)

    lands = [_in_hbm(lax.empty((N_SHARD,) + a.shape, a.dtype)) for a in arrs]
    out = pl.pallas_call(
        body, name="gather_start",
        in_specs=[HBM] * (2 * n),
        out_specs=[SEM, SEM] + [HBM] * (2 * n) + [pl.BlockSpec(memory_space=pltpu.VMEM)],
        out_shape=[pltpu.SemaphoreType.DMA((N_PEERS * n,)), pltpu.SemaphoreType.DMA((N_PEERS * n,))]
        + [pltpu.HBM(a.shape, a.dtype) for a in arrs]
        + [pltpu.HBM((N_SHARD,) + a.shape, a.dtype) for a in arrs]
        + [jax.ShapeDtypeStruct((SUBLANE, LANE), F32)],
        input_output_aliases={k: 2 + k for k in range(2 * n)},
        compiler_params=pltpu.CompilerParams(has_side_effects=DATAFLOW),
    )(*[_in_hbm(a) for a in arrs], *lands)
    return out[0], out[1], list(out[2:2 + n]), list(out[2 + n:2 + 2 * n]), out[-1]


def _gather_wait(ssem, rsem, srcs, lands, ks, after, split=False):
    n = len(ks)

    def body(*refs):
        ins, lnd = refs[:n], refs[n:2 * n]
        ssem_ref, rsem_ref = refs[2 * n:2 * n + 2]
        x, y, c = _mesh_pos()
        for i, k in enumerate(ks):
            for j, (dev, pchip) in enumerate(_gather_peers(x, y, c)):
                src, dst = _gather_refs(ins[i], lnd[i], pchip, c, split and j < N_PEERS - 1)
                cp = pltpu.make_async_remote_copy(
                    src_ref=src, dst_ref=dst, send_sem=ssem_ref.at[N_PEERS * k + j],
                    recv_sem=rsem_ref.at[N_PEERS * k + j], device_id=dev, device_id_type=MESH)
                cp.wait_send()
                cp.wait_recv()

    out = pl.pallas_call(
        body, name="gather_wait",
        in_specs=[HBM] * (2 * n) + [SEM, SEM] + [ANY] * len(after),
        out_specs=[HBM] * (2 * n),
        out_shape=[pltpu.HBM(a.shape, a.dtype) for a in srcs] + [pltpu.HBM(a.shape, a.dtype) for a in lands],
        input_output_aliases={k: k for k in range(2 * n)},
        compiler_params=pltpu.CompilerParams(has_side_effects=DATAFLOW),
    )(*srcs, *lands, ssem, rsem, *after)
    return list(out[n:])


def _pair_forward(lands):
    n = len(lands)

    def body(*refs):
        outs = refs[n:2 * n]
        ssem, rsem = refs[2 * n:]
        x, y, c = _mesh_pos()
        sibling = (x, y, 1 - c)
        cps = []
        for k in range(n):
            half = lands[k].shape[1] // 2
            for j, (ox, oy) in enumerate(_other_chips(x, y)):
                mine = outs[k].at[2 * ox + oy, pl.ds(c * half, half)]
                cp = pltpu.make_async_remote_copy(src_ref=mine, dst_ref=mine, send_sem=ssem.at[3 * k + j],
                                                  recv_sem=rsem.at[3 * k + j], device_id=sibling, device_id_type=MESH)
                cp.start()
                cps.append(cp)
        for k in range(n):
            half = lands[k].shape[1] // 2
            for j, (ox, oy) in enumerate(_other_chips(x, y)):
                theirs = outs[k].at[2 * ox + oy, pl.ds((1 - c) * half, half)]
                pltpu.make_async_remote_copy(src_ref=theirs, dst_ref=theirs, send_sem=ssem.at[3 * k + j],
                                             recv_sem=rsem.at[3 * k + j], device_id=sibling,
                                             device_id_type=MESH).wait_recv()
        for cp in cps:
            cp.wait_send()

    return pl.pallas_call(
        body, name="pair_forward",
        in_specs=[ANY] * n, out_specs=[ANY] * n,
        out_shape=[jax.ShapeDtypeStruct(a.shape, a.dtype) for a in lands],
        scratch_shapes=[pltpu.SemaphoreType.DMA((3 * n,)), pltpu.SemaphoreType.DMA((3 * n,))],
        input_output_aliases={k: k for k in range(n)},
    )(*lands)


N_SOURCES = 7


def _reduce_peers(x, y, c):
    peers = []
    for ox, oy in _other_chips(x, y):
        for rel in range(2):
            peers.append(((ox, oy, c + rel - 2 * c * rel), 2 * ox + oy))
    peers.append(((x, y, 1 - c), 2 * x + y))
    return peers


def _reduce_start(arrs, slots):
    n = len(arrs)

    def body(*refs):
        ins, lands = refs[:n], refs[n:2 * n]
        ssem, rsem = refs[2 * n:2 * n + 2]
        token = refs[-1]
        x, y, c = _mesh_pos()
        me = 2 * (2 * x + y) + c
        for k in range(n):
            half = arrs[k].shape[1] // 2
            for p, (dev, ochip) in enumerate(_reduce_peers(x, y, c)):
                pltpu.make_async_remote_copy(
                    src_ref=ins[k].at[ochip, pl.ds(dev[2] * half, half)], dst_ref=lands[k].at[me],
                    send_sem=ssem.at[N_SOURCES * k + p], recv_sem=rsem.at[N_SOURCES * k + p],
                    device_id=dev, device_id_type=MESH).start()
        token[...] = jnp.zeros_like(token)

    out = pl.pallas_call(
        body, name="reduce_start",
        in_specs=[HBM] * (2 * n),
        out_specs=[SEM, SEM] + [HBM] * (2 * n) + [pl.BlockSpec(memory_space=pltpu.VMEM)],
        out_shape=[pltpu.SemaphoreType.DMA((N_SOURCES * n,)), pltpu.SemaphoreType.DMA((N_SOURCES * n,))]
        + [pltpu.HBM(a.shape, a.dtype) for a in arrs] + [pltpu.HBM(a.shape, a.dtype) for a in slots]
        + [jax.ShapeDtypeStruct((SUBLANE, LANE), F32)],
        input_output_aliases={k: 2 + k for k in range(2 * n)},
        compiler_params=pltpu.CompilerParams(has_side_effects=DATAFLOW),
    )(*[_in_hbm(a) for a in arrs], *[_in_hbm(a) for a in slots])
    return out[0], out[1], list(out[2:2 + n]), list(out[2 + n:2 + 2 * n]), out[-1]


def _reduce_wait(ssem, rsem, arrs, slots, after):
    n = len(arrs)

    def body(*refs):
        ins, lnd = refs[:n], refs[n:2 * n]
        ssem_ref, rsem_ref = refs[2 * n:2 * n + 2]
        x, y, c = _mesh_pos()
        for k in range(n):
            half = arrs[k].shape[1] // 2
            for p, (dev, ochip) in enumerate(_reduce_peers(x, y, c)):
                cp = pltpu.make_async_remote_copy(
                    src_ref=ins[k].at[ochip, pl.ds(dev[2] * half, half)], dst_ref=lnd[k].at[2 * ochip + dev[2]],
                    send_sem=ssem_ref.at[N_SOURCES * k + p], recv_sem=rsem_ref.at[N_SOURCES * k + p],
                    device_id=dev, device_id_type=MESH)
                cp.wait_send()
                cp.wait_recv()

    out = pl.pallas_call(
        body, name="reduce_wait",
        in_specs=[HBM] * (2 * n) + [SEM, SEM] + [ANY] * len(after),
        out_specs=[HBM] * (2 * n),
        out_shape=[pltpu.HBM(a.shape, a.dtype) for a in arrs] + [pltpu.HBM(a.shape, a.dtype) for a in slots],
        input_output_aliases={k: k for k in range(2 * n)},
        compiler_params=pltpu.CompilerParams(has_side_effects=DATAFLOW),
    )(*arrs, *slots, ssem, rsem, *after)
    return list(out[n:])


def _own_part(arrs, chip, core, me):
    n = len(arrs)
    nb = GRAD_ROW_BLOCKS

    def body(chip_ref, core_ref, me_ref, *refs):
        for k in range(n):
            refs[n + k][...] = refs[k][...]

    def blk(a):
        return (1, a.shape[1] // 2 // nb, a.shape[2])

    grid_spec = pltpu.PrefetchScalarGridSpec(
        num_scalar_prefetch=3, grid=(nb,),
        in_specs=[pl.BlockSpec(blk(a), lambda i, ch, co, me: (ch[0], co[0] * nb + i, 0)) for a in arrs],
        out_specs=[pl.BlockSpec(blk(a), lambda i, ch, co, me: (me[0], i, 0)) for a in arrs])
    return pl.pallas_call(
        body, name="own_part", grid_spec=grid_spec,
        out_shape=[jax.ShapeDtypeStruct((N_DEV, a.shape[1] // 2, a.shape[2]), a.dtype) for a in arrs],
        compiler_params=_cparams(1),
    )(chip, core, me, *arrs)


def _sum_devices(arrs, core):
    n = len(arrs)
    nb = GRAD_ROW_BLOCKS

    def body(c_ref, *refs):
        for k in range(n):
            r = refs[k]
            acc = r[0].astype(F32)
            for dev in range(1, N_DEV):
                acc = acc + r[dev].astype(F32)
            refs[n + k][...] = acc

    grid_spec = pltpu.PrefetchScalarGridSpec(
        num_scalar_prefetch=1, grid=(nb,),
        in_specs=[pl.BlockSpec((N_DEV, a.shape[1] // nb, a.shape[2]), lambda i, c: (0, i, 0)) for a in arrs],
        out_specs=[pl.BlockSpec((a.shape[1] // nb, a.shape[2]), lambda i, c: (c[0] * nb + i, 0)) for a in arrs])
    return pl.pallas_call(
        body, name="sum_devices", grid_spec=grid_spec,
        out_shape=[jax.ShapeDtypeStruct((2 * a.shape[1], a.shape[2]), F32) for a in arrs],
        compiler_params=_cparams(1),
    )(core, *arrs)


def _small_own(v, me):
    m = v.shape[0]

    def body(me_ref, v_ref, o_ref):
        o_ref[0] = v_ref[...]

    grid_spec = pltpu.PrefetchScalarGridSpec(
        num_scalar_prefetch=1, grid=(1,),
        in_specs=[pl.BlockSpec((m, LANE), lambda i, me: (0, 0))],
        out_specs=pl.BlockSpec((1, m, LANE), lambda i, me: (me[0], 0, 0)))
    return pl.pallas_call(
        body, name="small_own", grid_spec=grid_spec,
        out_shape=jax.ShapeDtypeStruct((N_DEV, m, LANE), v.dtype),
        compiler_params=_cparams(1),
    )(me, v)


def _small_start(v, slots):
    def body(v_ref, land, ssem, rsem, v_thru, land_thru, token):
        del v_thru, land_thru
        x, y, c = _mesh_pos()
        me = 2 * (2 * x + y) + c
        for p, (dev, _) in enumerate(_reduce_peers(x, y, c)):
            pltpu.make_async_remote_copy(src_ref=v_ref, dst_ref=land.at[me], send_sem=ssem.at[p],
                                         recv_sem=rsem.at[p], device_id=dev, device_id_type=MESH).start()
        token[...] = jnp.zeros_like(token)

    out = pl.pallas_call(
        body, name="small_start",
        in_specs=[HBM, HBM],
        out_specs=[SEM, SEM, HBM, HBM, pl.BlockSpec(memory_space=pltpu.VMEM)],
        out_shape=[pltpu.SemaphoreType.DMA((N_SOURCES,)), pltpu.SemaphoreType.DMA((N_SOURCES,)),
                   pltpu.HBM(v.shape, v.dtype), pltpu.HBM(slots.shape, slots.dtype),
                   jax.ShapeDtypeStruct((SUBLANE, LANE), F32)],
        input_output_aliases={0: 2, 1: 3},
        compiler_params=pltpu.CompilerParams(has_side_effects=DATAFLOW),
    )(_in_hbm(v), _in_hbm(slots))
    return out


def _small_wait(ssem, rsem, v, slots, after):
    def body(*refs):
        v_ref, land, ssem_ref, rsem_ref = refs[:4]
        x, y, c = _mesh_pos()
        for p, (dev, ochip) in enumerate(_reduce_peers(x, y, c)):
            cp = pltpu.make_async_remote_copy(src_ref=v_ref, dst_ref=land.at[2 * ochip + dev[2]],
                                              send_sem=ssem_ref.at[p], recv_sem=rsem_ref.at[p],
                                              device_id=dev, device_id_type=MESH)
            cp.wait_send()
            cp.wait_recv()

    out = pl.pallas_call(
        body, name="small_wait",
        in_specs=[HBM, HBM, SEM, SEM] + [ANY] * len(after),
        out_specs=[HBM, HBM],
        out_shape=[pltpu.HBM(v.shape, v.dtype), pltpu.HBM(slots.shape, slots.dtype)],
        input_output_aliases={0: 0, 1: 1},
        compiler_params=pltpu.CompilerParams(has_side_effects=DATAFLOW),
    )(v, slots, ssem, rsem, *after)
    return out[1]


GRAD_ROW_BLOCKS = 2


def _pair_allgather_halves(arrs):
    n = len(arrs)

    def body(*refs):
        outs = refs[n:2 * n]
        ssem, rsem = refs[2 * n:]
        x, y, c = _mesh_pos()
        cps = []
        for k in range(n):
            h = arrs[k].shape[0] // 2
            mine = outs[k].at[pl.ds(c * h, h)]
            cp = pltpu.make_async_remote_copy(src_ref=mine, dst_ref=mine, send_sem=ssem.at[k],
                                              recv_sem=rsem.at[k], device_id=(x, y, 1 - c), device_id_type=MESH)
            cp.start()
            cps.append(cp)
        for k, cp in enumerate(cps):
            h = arrs[k].shape[0] // 2
            theirs = outs[k].at[pl.ds((1 - c) * h, h)]
            pltpu.make_async_remote_copy(src_ref=theirs, dst_ref=theirs, send_sem=ssem.at[k], recv_sem=rsem.at[k],
                                         device_id=(x, y, 1 - c), device_id_type=MESH).wait_recv()
            cp.wait_send()

    return pl.pallas_call(
        body, name="pair_allgather_halves",
        in_specs=[ANY] * n, out_specs=[ANY] * n,
        out_shape=[jax.ShapeDtypeStruct(a.shape, a.dtype) for a in arrs],
        scratch_shapes=[pltpu.SemaphoreType.DMA((n,)), pltpu.SemaphoreType.DMA((n,))],
        input_output_aliases={k: k for k in range(n)},
    )(*arrs)


N_DEV = 8


def _adamw_math(w, g, m, v):
    m2 = ADAM_B1 * m + (1.0 - ADAM_B1) * g
    v2 = ADAM_B2 * v + (1.0 - ADAM_B2) * (g * g)
    m_hat = m2 / (1.0 - ADAM_B1 ** ADAM_STEP)
    v_hat = v2 / (1.0 - ADAM_B2 ** ADAM_STEP)
    delta = -ADAM_LR * (m_hat / (jnp.sqrt(v_hat) + ADAM_EPS) + ADAM_WD * w)
    return delta, m2, v2


def _adamw(w, m, v, gs, nblk):
    nl, r, n = w.shape
    assert nl == len(gs) and nl in (1, 2)
    br = r // nblk

    def body(w_ref, m_ref, v_ref, *rest):
        g_refs, (go_ref, d_ref, mo_ref, vo_ref) = rest[:nl], rest[nl:]
        g = g_refs[0][...]
        if nl == 2:
            g = jnp.where(pl.program_id(0) == 0, g, g_refs[1][...])
        delta, m2, v2 = _adamw_math(w_ref[0], g, m_ref[0], v_ref[0])
        go_ref[0] = g
        d_ref[0] = delta
        mo_ref[0] = m2
        vo_ref[0] = v2

    spec = pl.BlockSpec((1, br, n), lambda l, i: (l, i, 0))
    g_specs = [pl.BlockSpec((br, n), lambda l, i: (i, 0))] if nl == 1 else [
        pl.BlockSpec((br, n), lambda l, i: (jnp.where(l == 0, i, nblk - 1), 0)),
        pl.BlockSpec((br, n), lambda l, i: (jnp.where(l == 1, i, 0), 0))]
    return pl.pallas_call(
        body, name="adamw", grid=(nl, nblk),
        in_specs=[spec, spec, spec] + g_specs,
        out_specs=[spec] * 4,
        out_shape=[jax.ShapeDtypeStruct((nl, r, n), F32)] * 4,
        compiler_params=_cparams(2),
    )(w, m, v, *gs)


def _small_reduce_adamw(parts, w, m, v, rep_rows, sh_rows):
    mrows = rep_rows + N_SHARD * sh_rows + LOSS_ROWS

    def body(p_ref, w_ref, m_ref, v_ref, go_ref, d_ref, mo_ref, vo_ref, loss_ref):
        x, y, _ = _mesh_pos()
        mine = rep_rows + (2 * x + y) * sh_rows
        g_rep = p_ref[0:rep_rows, :]
        g_sh = p_ref[pl.ds(pl.multiple_of(mine, SUBLANE), sh_rows), :]
        loss = p_ref[mrows - LOSS_ROWS:mrows, :]
        for k in range(1, N_DEV):
            g_rep = g_rep + p_ref[k * mrows:k * mrows + rep_rows, :]
            g_sh = g_sh + p_ref[pl.ds(pl.multiple_of(k * mrows + mine, SUBLANE), sh_rows), :]
            loss = loss + p_ref[(k + 1) * mrows - LOSS_ROWS:(k + 1) * mrows, :]
        g = jnp.concatenate([g_rep, g_sh], axis=0)
        delta, m2, v2 = _adamw_math(w_ref[...], g, m_ref[...], v_ref[...])
        go_ref[...] = g
        d_ref[...] = delta
        mo_ref[...] = m2
        vo_ref[...] = v2
        loss_ref[...] = loss

    return pl.pallas_call(
        body, name="small_reduce_adamw",
        out_shape=[jax.ShapeDtypeStruct((rep_rows + sh_rows, 128), F32)] * 4
        + [jax.ShapeDtypeStruct((LOSS_ROWS, 128), F32)],
        compiler_params=pltpu.CompilerParams(vmem_limit_bytes=VMEM_LIMIT_MB * 1024 * 1024),
    )(parts, w, m, v)


LANE = 128
REP_SPEC = (("ffn1_norm", 16), ("mix_norm", 16), ("ffn2_norm", 16), ("final_norm", 8), ("pool_w", 128),
            ("pool_scale", 8), ("hgrn_lb_logits", 16), ("hgrn_gnorm", 8), ("lru_wa", 256), ("lru_wx", 256))
SH_SPEC = (("meta_tokens", 32), ("conv_w", 32), ("lru_conv_w", 8), ("conv_b", 8), ("conv_ln_g", 8),
           ("conv_ln_b", 8), ("lru_conv_b", 8), ("lru_ba", 8), ("lru_bx", 8), ("lru_lambda", 8))
REP_ROWS = sum(r for _, r in REP_SPEC)
SH_ROWS = sum(r for _, r in SH_SPEC)


def _pack_rows(vals, spec):
    parts = []
    for name, rows in spec:
        flat = vals[name].astype(F32).reshape(-1, LANE)
        if flat.shape[0] < rows:
            flat = jnp.concatenate([flat, jnp.zeros((rows - flat.shape[0], LANE), F32)], axis=0)
        parts.append(flat)
    return jnp.concatenate(parts, axis=0)


def _unpack_rows(packed, spec, shapes):
    out = {}
    off = 0
    for name, rows in spec:
        shp = shapes[name]
        n = int(np.prod(shp)) // LANE
        out[name] = packed[off:off + n].reshape(shp)
        off += rows
    return out


def _block_diag(blocks):
    n, b, _ = blocks.shape
    return sum(jnp.pad(blocks[g], ((g * b, (n - 1 - g) * b), (g * b, (n - 1 - g) * b))) for g in range(n))


def _diag_blocks(mat, n):
    b = mat.shape[0] // n
    return jnp.stack([mat[g * b:(g + 1) * b, g * b:(g + 1) * b] for g in range(n)])


BIG = ("ffn1_wg", "ffn1_wu", "ffn2_wg", "ffn2_wu", "ffn1_wd", "ffn2_wd", "w_in_even", "w_out_even",
       "w_in_odd", "w_out_odd")
WEIGHT_NAMES = ('meta_tokens', 'ffn1_norm', 'ffn1_wg', 'ffn1_wu', 'ffn1_wd', 'mix_norm', 'ffn2_norm', 'ffn2_wg',
                'ffn2_wu', 'ffn2_wd', 'w_in_even', 'pool_w', 'pool_scale', 'hgrn_lb_logits', 'hgrn_gnorm',
                'w_out_even', 'w_in_odd', 'conv_w', 'conv_b', 'conv_ln_g', 'conv_ln_b', 'lru_conv_w',
                'lru_conv_b', 'lru_wa', 'lru_ba', 'lru_wx', 'lru_bx', 'lru_lambda', 'w_out_odd', 'final_norm')


def _block_diag2(heads):
    nb = heads.shape[0] // 2
    return jnp.stack([_block_diag(heads[2 * j:2 * j + 2]) for j in range(nb)])


def _diag_blocks2(mats):
    return jnp.concatenate([_diag_blocks(mats[j], 2) for j in range(mats.shape[0])], axis=0)


GATHER_GROUPS = (
    (("small", 0),),
    (("ffn1_wg", 0), ("ffn1_wu", 0), ("ffn1_wd", 0)),
    (("w_in_even", 0), ("w_out_even", 0)),
    (("ffn2_wg", 0), ("ffn2_wu", 0), ("ffn2_wd", 0)),
    (("ffn1_wg", 1), ("ffn1_wu", 1), ("ffn1_wd", 1)),
    (("w_in_odd", 0), ("w_out_odd", 0)),
    (("ffn2_wg", 1), ("ffn2_wu", 1), ("ffn2_wd", 1)),
)
ADAM_ROW_BLOCKS = {"ffn1_wg": 2, "ffn1_wu": 2, "ffn2_wg": 2, "ffn2_wu": 2, "ffn1_wd": 2, "ffn2_wd": 2,
                   "w_in_even": 4, "w_out_even": 2, "w_in_odd": 4, "w_out_odd": 2}
TRANSPOSED = ("ffn1_wg", "ffn1_wu", "ffn2_wg", "ffn2_wu", "w_in_even")
SCATTER_DEPTH = 2
LOSS_ROWS = 8
GATHER_SPLIT = (1, 4)


def _unpack_small(sm, shapes):
    per_shard = [_unpack_rows(sm[s], SH_SPEC, shapes) for s in range(N_SHARD)]
    full = {}
    for n, _ in SH_SPEC:
        full[n] = jnp.concatenate([per_shard[s][n].reshape(-1, shapes[n][-1]) for s in range(N_SHARD)], axis=-1)
    full["conv_w"] = jnp.concatenate([full["conv_w"], jnp.zeros((CONV_HALO - CONV_WIDTH, D_CONV), F32)], axis=0)
    return full


def _local_step(x, tgt, w, shapes, fetch, emit, emit_small):
    s_len, d = x.shape
    t_real = s_len + N_META
    tp = -(-t_real // ROW_ALIGN) * ROW_ALIGN
    tm = _tile(tp, 832, ROW_ALIGN)
    tm_small = _tile(tp, 832, 16)
    tr = _tile(tp, 416, SUBLANE)
    tm_wgrad = _tile(tp, 1040, 16)

    def gain(name, layer):
        return w[name][layer:layer + 1]

    pool_wbd = _block_diag(w["pool_w"][0]).astype(BF16)
    pool_scale = w["pool_scale"]
    wa_bd = _block_diag2(w["lru_wa"][0]).astype(BF16)
    wx_bd = _block_diag2(w["lru_wx"][0]).astype(BF16)
    mst, msk, n_lev = _hgrn_consts(HG_CHUNK)

    (sm,) = fetch(0, None)
    sf = _unpack_small(sm, shapes)
    h0 = jnp.concatenate([sf["meta_tokens"], x, jnp.zeros((tp - t_real, d), F32)], axis=0)
    tgt_pad = jnp.concatenate([jnp.zeros((N_META, d), F32), tgt, jnp.zeros((tp - t_real, d), F32)], axis=0)
    f1l0 = fetch(1, h0)
    h1, *s1 = _ffn_fwd(h0, gain("ffn1_norm", 0), *f1l0, tm)
    w_in_even4, w_out_even4 = fetch(2, h1)
    w_out_even = w_out_even4.reshape(d, d)
    even_piece = [(w_in_even4.reshape(D_IN_EVEN, d), (D_IN_EVEN, d), (0, 0))]
    p0, nm0 = _proj_fwd(h1, gain("mix_norm", 0), even_piece, tm_small, wt=True)
    ya = _pool_fwd(p0, pool_wbd, pool_scale, tr)
    yb, states = _hgrn_fwd(p0, w["hgrn_lb_logits"], w["hgrn_gnorm"], mst, msk, n_lev, tm)
    h2 = _out_fwd(h1, ya, yb, w_out_even, tm)
    f2l0 = fetch(3, h2)
    h3, *s2 = _ffn_fwd(h2, gain("ffn2_norm", 0), *f2l0, tm)
    f1l1 = fetch(4, h3)
    h4, *s3 = _ffn_fwd(h3, gain("ffn1_norm", 1), *f1l1, tm)
    w_in_odd4, w_out_odd4 = fetch(5, h4)
    w_out_odd = w_out_odd4.reshape(d, d)
    odd_pieces = [(w_in_odd4, (1, d, D_IN_ODD // N_SHARD), (k, 0, 0)) for k in range(N_SHARD)]
    p1, nm1 = _proj_fwd(h4, gain("mix_norm", 1), odd_pieces, tm_small)
    yc, conv_out = _convmod_fwd(p1, sf["conv_w"], sf["conv_b"], sf["conv_ln_g"], sf["conv_ln_b"], tr)
    lru_args = (sf["lru_conv_w"], sf["lru_conv_b"], wa_bd, sf["lru_ba"], wx_bd, sf["lru_bx"], sf["lru_lambda"])
    yd, hs = _lru_fwd(p1, *lru_args)
    h5 = _out_fwd(h4, yc, yd, w_out_odd, tm)
    f2l1 = fetch(6, h5)
    h6, *s4 = _ffn_fwd(h5, gain("ffn2_norm", 1), *f2l1, tm)
    loss, dh6, dg_final = _loss_bwd(h6, w["final_norm"].reshape(1, d), tgt_pad, t_real, tm)

    def ffn_bwd(dho, h, saved, norm, wts, after=()):
        ga, gb, sa, n = saved
        dh, da, db, dg, dy = _ffn_bwd_act(dho, h, norm, ga, gb, *wts, tm, after)
        return dh, dg, _ffn_bwd_w([(da, n, None), (db, n, None), (sa, dy, None)], tm_wgrad)

    dh5, dg_f2_l1, g = ffn_bwd(dh6, h5, s4, gain("ffn2_norm", 1), f2l1)
    sent = emit((("ffn2_wg", 1), ("ffn2_wu", 1), ("ffn2_wd", 1)), g)
    dyc, dyd, dw_out_odd = _out_bwd(dh5, yc, yd, w_out_odd, tm, tuple(sent))
    dca, dcb, dconv_w, dconv_vec = _convmod_bwd(p1, conv_out, dyc, sf["conv_w"], sf["conv_ln_g"],
                                                sf["conv_ln_b"], tr)
    dlx, dlg, dwa_bd, dwx_bd, dlru_vec = _lru_bwd(p1, hs, dyd, *lru_args)
    dp1 = [dca, dcb, dlx, dlg]
    dh4, dg_mix_l1 = _proj_bwd_act(dh5, h4, gain("mix_norm", 1), dp1, odd_pieces, tm_small)
    dw_in_odd = jnp.stack(_proj_bwd_w(nm1, dp1, tm))
    dh3, dg_f1_l1, g = ffn_bwd(dh4, h3, s3, gain("ffn1_norm", 1), f1l1)
    sent = emit((("w_out_odd", 0), ("w_in_odd", 0), ("ffn1_wg", 1), ("ffn1_wu", 1), ("ffn1_wd", 1)),
                [dw_out_odd.reshape(N_SHARD, d // N_SHARD, d), dw_in_odd] + list(g))
    dh2, dg_f2_l0, g_f2l0 = ffn_bwd(dh3, h2, s2, gain("ffn2_norm", 0), f2l0, tuple(sent))
    dya, dyb, dw_out_even = _out_bwd(dh2, ya, yb, w_out_even, tm)
    dpool, dpool_wbd, dpool_scale = _pool_bwd(p0, dya, pool_wbd, pool_scale, tr)
    dq, dz, dv, dgate, dlb_logits, dgn_heads = _hgrn_bwd(p0, dyb, states, w["hgrn_lb_logits"], w["hgrn_gnorm"],
                                                         mst, msk, n_lev, tm)
    dp0 = [dpool, dq, dz, dv, dgate]
    dh1, dg_mix_l0 = _proj_bwd_act(dh2, h1, gain("mix_norm", 0), dp0, even_piece, tm_small, wt=True)
    dw_in_even_t = jnp.concatenate(_proj_bwd_w(nm0, dp0, tm_small, wt=True), axis=0)
    ga, gb, sa, n1 = s1
    (dwd_f1l0,) = _ffn_bwd_w([(sa, dh1, 0.5)], tm_wgrad)
    sent = emit((("ffn2_wg", 0), ("ffn2_wu", 0), ("ffn2_wd", 0), ("w_out_even", 0), ("w_in_even", 0),
                 ("ffn1_wd", 0)),
                list(g_f2l0) + [dw_out_even.reshape(N_SHARD, d // N_SHARD, d),
                                dw_in_even_t.reshape(N_SHARD, D_IN_EVEN // N_SHARD, d), dwd_f1l0])
    dh0, da, db, dg_f1_l0, _ = _ffn_bwd_act(dh1, h0, gain("ffn1_norm", 0), ga, gb, *f1l0, tm, tuple(sent))

    grad_x = dh0[N_META:t_real]
    rep = {
        "ffn1_norm": jnp.concatenate([dg_f1_l0, dg_f1_l1], axis=0),
        "mix_norm": jnp.concatenate([dg_mix_l0, dg_mix_l1], axis=0),
        "ffn2_norm": jnp.concatenate([dg_f2_l0, dg_f2_l1], axis=0),
        "final_norm": dg_final,
        "pool_w": _diag_blocks(dpool_wbd, len(POOL_WINDOWS)),
        "pool_scale": dpool_scale,
        "hgrn_lb_logits": dlb_logits,
        "hgrn_gnorm": jnp.sum(dgn_heads, axis=0),
        "lru_wa": _diag_blocks2(dwa_bd),
        "lru_wx": _diag_blocks2(dwx_bd),
    }
    dmeta = jnp.transpose(dh0[:N_META].reshape(N_META, N_SHARD, 2, LANE), (1, 0, 2, 3)).reshape(N_SHARD, 32, LANE)
    packs = [_pack_rows(rep, REP_SPEC)]
    for s in range(N_SHARD):
        sh = {
            "meta_tokens": dmeta[s], "conv_w": dconv_w[s], "lru_conv_w": dlru_vec[s, 0:4],
            "conv_b": dconv_vec[s, 0:1], "conv_ln_g": dconv_vec[s, 1:2], "conv_ln_b": dconv_vec[s, 2:3],
            "lru_conv_b": dlru_vec[s, 4:5], "lru_ba": dlru_vec[s, 5:6], "lru_bx": dlru_vec[s, 6:7],
            "lru_lambda": dlru_vec[s, 7:8],
        }
        packs.append(_pack_rows(sh, SH_SPEC))
    packs.append(jnp.pad(loss, ((0, LOSS_ROWS - 1), (0, LANE - 1))))
    sent = emit_small(jnp.concatenate(packs, axis=0))
    emit((("ffn1_wg", 0), ("ffn1_wu", 0)), _ffn_bwd_w([(da, n1, None), (db, n1, None)], tm_wgrad, tuple(sent)))
    return grad_x


def kernel(x, meta_tokens, ffn1_norm, ffn1_wg, ffn1_wu, ffn1_wd, mix_norm, ffn2_norm, ffn2_wg, ffn2_wu, ffn2_wd, w_in_even, pool_w, pool_scale, hgrn_lb_logits, hgrn_gnorm, w_out_even, w_in_odd, conv_w, conv_b, conv_ln_g, conv_ln_b, lru_conv_w, lru_conv_b, lru_wa, lru_ba, lru_wx, lru_bx, lru_lambda, w_out_odd, final_norm, loss_target, m_meta_tokens, m_ffn1_norm, m_ffn1_wg, m_ffn1_wu, m_ffn1_wd, m_mix_norm, m_ffn2_norm, m_ffn2_wg, m_ffn2_wu, m_ffn2_wd, m_w_in_even, m_pool_w, m_pool_scale, m_hgrn_lb_logits, m_hgrn_gnorm, m_w_out_even, m_w_in_odd, m_conv_w, m_conv_b, m_conv_ln_g, m_conv_ln_b, m_lru_conv_w, m_lru_conv_b, m_lru_wa, m_lru_ba, m_lru_wx, m_lru_bx, m_lru_lambda, m_w_out_odd, m_final_norm, v_meta_tokens, v_ffn1_norm, v_ffn1_wg, v_ffn1_wu, v_ffn1_wd, v_mix_norm, v_ffn2_norm, v_ffn2_wg, v_ffn2_wu, v_ffn2_wd, v_w_in_even, v_pool_w, v_pool_scale, v_hgrn_lb_logits, v_hgrn_gnorm, v_w_out_even, v_w_in_odd, v_conv_w, v_conv_b, v_conv_ln_g, v_conv_ln_b, v_lru_conv_w, v_lru_conv_b, v_lru_wa, v_lru_ba, v_lru_wx, v_lru_bx, v_lru_lambda, v_w_out_odd, v_final_norm):
    args = locals()
    w = {n: args[n] for n in WEIGHT_NAMES}
    m = {n: args["m_" + n] for n in WEIGHT_NAMES}
    v = {n: args["v_" + n] for n in WEIGHT_NAMES}
    shapes = {n: w[n].shape for n in WEIGHT_NAMES}
    core = lax.axis_index("c").astype(jnp.int32).reshape(1)
    chip = (2 * lax.axis_index("x") + lax.axis_index("y")).astype(jnp.int32).reshape(1)
    me = 2 * chip + core

    def view(a, n):
        return jnp.swapaxes(a, 1, 2) if n in TRANSPOSED else a

    wv, mv, vv = [{n: view(src[n], n) for n in BIG} for src in (w, m, v)]

    def shard(key):
        n, l = key
        return _pack_rows(w, SH_SPEC) if n == "small" else wv[n][l].astype(BF16)

    started = {}
    for groups in (GATHER_GROUPS[:2], GATHER_GROUPS[2:]):
        gkeys = [key for grp in groups for key in grp]
        ssem, rsem, srcs, lands, token = _gather_start([shard(key) for key in gkeys],
                                                       [any(key in GATHER_GROUPS[g] for g in GATHER_SPLIT)
                                                        for key in gkeys])
        for k, key in enumerate(gkeys):
            started[key] = (ssem, rsem, srcs[k], lands[k], k, token)

    def pack_small(src):
        return jnp.concatenate([_pack_rows(src, REP_SPEC), _pack_rows(src, SH_SPEC)], axis=0)

    small_packs = [pack_small(src) for src in (w, m, v)]

    def fetch(group, after):
        st = [started[key] for key in GATHER_GROUPS[group]]
        deps = (st[0][5],) if after is None else (after,)
        if group == 1:
            deps += (started[GATHER_GROUPS[2][0]][5],) + tuple(small_packs)
        split = group in GATHER_SPLIT
        got = _gather_wait(st[0][0], st[0][1], [s[2] for s in st], [s[3] for s in st], [s[4] for s in st], deps,
                           split)
        return _pair_forward(got) if split else got

    in_flight, reduced = [], {}

    def collect(entry, after):
        gkeys, gs_sem, gr_sem, grads_thru, slots_thru, _ = entry
        slots = _reduce_wait(gs_sem, gr_sem, grads_thru, slots_thru, after)
        full = _pair_allgather_halves(_sum_devices(slots, core))
        reduced.update(zip(gkeys, full))
        return full[0]

    def emit(gkeys, grads):
        grads = list(grads)
        in_flight.append((gkeys,) + tuple(_reduce_start(grads, _own_part(grads, chip, core, me))))
        token = in_flight[-1][-1]
        if len(in_flight) > SCATTER_DEPTH:
            return token, collect(in_flight[-1 - SCATTER_DEPTH], (token,))
        return (token,)

    small_flight = []

    def emit_small(part):
        small_flight.append(_small_start(part, _small_own(part, me)))
        return (small_flight[0][4],)

    grad_x = _local_step(x[0], loss_target[0], w, shapes, fetch, emit, emit_small)

    out_g, out_d, out_m, out_v = {}, {}, {}, {}
    deps = (in_flight[-1][-1],)

    def adamw_ready():
        done = ()
        for n in BIG:
            layers = range(shapes[n][0])
            if n not in out_g and all((n, l) in reduced for l in layers):
                res = _adamw(wv[n], mv[n], vv[n], [reduced[(n, l)] for l in layers], ADAM_ROW_BLOCKS[n])
                out_g[n], out_d[n], out_m[n], out_v[n] = [view(r, n) for r in res]
                done += (res[1],)
        return done

    deps += adamw_ready()
    for entry in in_flight[-SCATTER_DEPTH:-1]:
        collect(entry, deps)
        deps += adamw_ready()
    s_ssem, s_rsem, s_part, s_slots, _ = small_flight[0]
    small_all = _small_wait(s_ssem, s_rsem, s_part, s_slots, deps)
    small_res = _small_reduce_adamw(small_all.reshape(-1, LANE), *small_packs, REP_ROWS, SH_ROWS)
    collect(in_flight[-1], deps + (small_res[0],))
    adamw_ready()

    loss = small_res[4][0, 0]
    for dst, packed in zip((out_g, out_d, out_m, out_v), small_res[:4]):
        dst.update(_unpack_rows(packed[:REP_ROWS], REP_SPEC, shapes))
        dst.update(_unpack_rows(packed[REP_ROWS:], SH_SPEC, shapes))

    return (loss, grad_x[None], *[out_g[n] for n in WEIGHT_NAMES], *[out_d[n] for n in WEIGHT_NAMES],
            *[out_m[n] for n in WEIGHT_NAMES], *[out_v[n] for n in WEIGHT_NAMES])
```

```python
import functools

import numpy as np
import jax
import jax.numpy as jnp
from jax import lax
from jax.experimental import pallas as pl
from jax.experimental.pallas import tpu as pltpu

F32 = jnp.float32
BF16 = jnp.bfloat16
MESH = pl.DeviceIdType.MESH

EPS = 1e-6
N_META = 16
D_FF = 2816
N_SHARD = 4
FF_SHARD = D_FF // N_SHARD
D_POOL = 256
POOL_GROUP = 64
POOL_WINDOWS = (2, 4, 8, 16)
D_HGRN = 768
HG_HEADS = 6
HEAD = 128
HG_CHUNK = 64
HG_HEADS_PER_STEP = 6
HG_PBLOCK = 256
D_IN_EVEN = D_POOL + 4 * D_HGRN
D_CONV = 512
CONV_WIDTH = 31
CONV_HALO = 32
D_LRU = 512
LRU_CONV = 4
LRU_HALO = 8
LRU_C = 8.0
D_IN_ODD = 2 * D_CONV + 2 * D_LRU
SUBLANE = 8
ROW_ALIGN = 64

ADAM_LR = 0.001
ADAM_B1 = 0.9
ADAM_B2 = 0.999
ADAM_EPS = 1e-08
ADAM_WD = 0.01
ADAM_STEP = 10

VMEM_LIMIT_MB = 56


def _cparams(n_grid_axes=0, vmem_mb=VMEM_LIMIT_MB):
    sem = ("arbitrary",) * n_grid_axes if n_grid_axes else None
    return pltpu.CompilerParams(dimension_semantics=sem, vmem_limit_bytes=vmem_mb * 1024 * 1024)


def _tile(n, target, mult):
    best = None
    for t in range(mult, min(n, target) + 1, mult):
        if n % t == 0:
            best = t
    assert best is not None, (n, target, mult)
    return best


def _dot(a, b):
    return jnp.dot(a, b, preferred_element_type=F32)


def _dot_nt(a, b):
    return lax.dot_general(a, b, (((1,), (1,)), ((), ())), preferred_element_type=F32)


def _dot_tn(a, b):
    return lax.dot_general(a, b, (((0,), (0,)), ((), ())), preferred_element_type=F32)


def _sigmoid(x):
    return 1.0 / (1.0 + jnp.exp(-x))


def _colsum(x):
    return jnp.sum(x, axis=0, keepdims=True)


def _rms_stats(h):
    rstd = lax.rsqrt(jnp.mean(h * h, axis=-1, keepdims=True) + EPS)
    return rstd, h * rstd


def _rms_bwd(dn, g, rstd, xhat):
    dng = dn * g
    dh = rstd * (dng - xhat * jnp.mean(dng * xhat, axis=-1, keepdims=True))
    return dh, _colsum(dn * xhat)


def _ffn_fwd(h, norm, wg4, wu4, wd4, tm):
    tp, d = h.shape
    nt = tp // tm

    def body(h_ref, g_ref, wg_ref, wu_ref, wd_ref, ho_ref, ga_ref, gb_ref, sa_ref, n_ref, n_sc, acc):
        s = pl.program_id(1)

        @pl.when(s == 0)
        def _():
            hh = h_ref[...]
            rstd, xhat = _rms_stats(hh)
            n = (xhat * g_ref[...]).astype(BF16)
            n_sc[...] = n
            n_ref[...] = n
            acc[...] = jnp.zeros_like(acc)

        n = n_sc[...]
        a = _dot_nt(n, wg_ref[0])
        b = _dot_nt(n, wu_ref[0])
        sig = _sigmoid(a)
        sil = a * sig
        ga_ref[0] = (sig * (1.0 + a * (1.0 - sig)) * b).astype(BF16)
        gb_ref[0] = sil.astype(BF16)
        sg = (sil * b).astype(BF16)
        sa_ref[0] = sg
        acc[...] += _dot(sg, wd_ref[0])

        @pl.when(s == N_SHARD - 1)
        def _():
            ho_ref[...] = h_ref[...] + 0.5 * acc[...]

    return pl.pallas_call(
        body, name="ffn_fwd",
        grid=(nt, N_SHARD),
        in_specs=[
            pl.BlockSpec((tm, d), lambda i, s: (i, 0)),
            pl.BlockSpec((1, d), lambda i, s: (0, 0)),
            pl.BlockSpec((1, FF_SHARD, d), lambda i, s: (s, 0, 0)),
            pl.BlockSpec((1, FF_SHARD, d), lambda i, s: (s, 0, 0)),
            pl.BlockSpec((1, FF_SHARD, d), lambda i, s: (s, 0, 0)),
        ],
        out_specs=[
            pl.BlockSpec((tm, d), lambda i, s: (i, 0)),
            pl.BlockSpec((1, tm, FF_SHARD), lambda i, s: (s, i, 0)),
            pl.BlockSpec((1, tm, FF_SHARD), lambda i, s: (s, i, 0)),
            pl.BlockSpec((1, tm, FF_SHARD), lambda i, s: (s, i, 0)),
            pl.BlockSpec((tm, d), lambda i, s: (i, 0)),
        ],
        out_shape=[
            jax.ShapeDtypeStruct((tp, d), F32),
            jax.ShapeDtypeStruct((N_SHARD, tp, FF_SHARD), BF16),
            jax.ShapeDtypeStruct((N_SHARD, tp, FF_SHARD), BF16),
            jax.ShapeDtypeStruct((N_SHARD, tp, FF_SHARD), BF16),
            jax.ShapeDtypeStruct((tp, d), BF16),
        ],
        scratch_shapes=[pltpu.VMEM((tm, d), BF16), pltpu.VMEM((tm, d), F32)],
        compiler_params=_cparams(2),
    )(h, norm, wg4, wu4, wd4)


def _ffn_bwd_act(dho, h, norm, ga4, gb4, wg4, wu4, wd4, tm, after=()):
    tp, d = h.shape
    nt = tp // tm

    def body(dho_ref, h_ref, g_ref, ga_ref, gb_ref, wg_ref, wu_ref, wd_ref, *rest):
        dh_ref, da_ref, db_ref, dg_ref, dy_ref, dn_sc = rest[len(after):]
        i = pl.program_id(0)
        s = pl.program_id(1)

        @pl.when(s == 0)
        def _():
            dy_ref[...] = (0.5 * dho_ref[...]).astype(BF16)
            dn_sc[...] = jnp.zeros_like(dn_sc)

        @pl.when((s == 0) & (i == 0))
        def _():
            dg_ref[...] = jnp.zeros_like(dg_ref)

        ds = _dot_nt(dy_ref[...], wd_ref[0])
        da = (ds * ga_ref[0].astype(F32)).astype(BF16)
        db = (ds * gb_ref[0].astype(F32)).astype(BF16)
        da_ref[0] = da
        db_ref[0] = db
        dn_sc[...] += _dot(da, wg_ref[0]) + _dot(db, wu_ref[0])

        @pl.when(s == N_SHARD - 1)
        def _():
            rstd, xhat = _rms_stats(h_ref[...])
            dh, dg = _rms_bwd(dn_sc[...], g_ref[...], rstd, xhat)
            dh_ref[...] = dho_ref[...] + dh
            dg_ref[...] += dg

    return pl.pallas_call(
        body, name="ffn_bwd_act",
        grid=(nt, N_SHARD),
        in_specs=[
            pl.BlockSpec((tm, d), lambda i, s: (i, 0)),
            pl.BlockSpec((tm, d), lambda i, s: (i, 0)),
            pl.BlockSpec((1, d), lambda i, s: (0, 0)),
            pl.BlockSpec((1, tm, FF_SHARD), lambda i, s: (s, i, 0)),
            pl.BlockSpec((1, tm, FF_SHARD), lambda i, s: (s, i, 0)),
            pl.BlockSpec((1, FF_SHARD, d), lambda i, s: (s, 0, 0)),
            pl.BlockSpec((1, FF_SHARD, d), lambda i, s: (s, 0, 0)),
            pl.BlockSpec((1, FF_SHARD, d), lambda i, s: (s, 0, 0)),
        ] + [pl.BlockSpec(memory_space=pl.ANY)] * len(after),
        out_specs=[
            pl.BlockSpec((tm, d), lambda i, s: (i, 0)),
            pl.BlockSpec((1, tm, FF_SHARD), lambda i, s: (s, i, 0)),
            pl.BlockSpec((1, tm, FF_SHARD), lambda i, s: (s, i, 0)),
            pl.BlockSpec((1, d), lambda i, s: (0, 0)),
            pl.BlockSpec((tm, d), lambda i, s: (i, 0)),
        ],
        out_shape=[
            jax.ShapeDtypeStruct((tp, d), F32),
            jax.ShapeDtypeStruct((N_SHARD, tp, FF_SHARD), BF16),
            jax.ShapeDtypeStruct((N_SHARD, tp, FF_SHARD), BF16),
            jax.ShapeDtypeStruct((1, d), F32),
            jax.ShapeDtypeStruct((tp, d), BF16),
        ],
        scratch_shapes=[pltpu.VMEM((tm, d), F32)],
        compiler_params=_cparams(2),
    )(dho, h, norm, ga4, gb4, wg4, wu4, wd4, *after)


def _ffn_bwd_w(pairs, tm, after=()):
    npair = len(pairs)
    tp, d = pairs[0][1].shape
    nt = tp // tm
    rhs_list = []
    for _, rhs, _ in pairs:
        if all(rhs is not r for r in rhs_list):
            rhs_list.append(rhs)
    rhs_of = [[rhs is r for r in rhs_list].index(True) for _, rhs, _ in pairs]
    nrhs = len(rhs_list)

    def body(*refs):
        rhs_refs = refs[:nrhs]
        lhs_refs = refs[nrhs:nrhs + npair]
        rest = refs[nrhs + npair + len(after):]
        out_refs, accs = rest[:npair], rest[npair:]
        i = pl.program_id(1)

        @pl.when(i == 0)
        def _():
            for acc in accs:
                acc[...] = jnp.zeros_like(acc)

        for k, (_, _, scale) in enumerate(pairs):
            rhs = rhs_refs[rhs_of[k]][...]
            if scale is not None:
                rhs = (scale * rhs).astype(BF16)
            accs[k][...] += _dot_tn(lhs_refs[k][0], rhs)

        @pl.when(i == nt - 1)
        def _():
            for k in range(npair):
                out_refs[k][0] = accs[k][...].astype(BF16)

    return pl.pallas_call(
        body, name="ffn_bwd_w",
        grid=(N_SHARD, nt),
        in_specs=[pl.BlockSpec((tm, d), lambda s, i: (i, 0))] * nrhs
        + [pl.BlockSpec((1, tm, FF_SHARD), lambda s, i: (s, i, 0))] * npair
        + [pl.BlockSpec(memory_space=pl.ANY)] * len(after),
        out_specs=[pl.BlockSpec((1, FF_SHARD, d), lambda s, i: (s, 0, 0))] * npair,
        out_shape=[jax.ShapeDtypeStruct((N_SHARD, FF_SHARD, d), BF16)] * npair,
        scratch_shapes=[pltpu.VMEM((FF_SHARD, d), F32)] * npair,
        compiler_params=_cparams(2),
    )(*rhs_list, *[lhs for lhs, _, _ in pairs], *after)


def _proj_fwd(h, norm, w_pieces, tm, wt=False):
    tp, d = h.shape
    widths = [bs[-2] if wt else bs[-1] for _, bs, _ in w_pieces]
    ntot = sum(widths)
    npc = len(w_pieces)

    def body(*refs):
        h_ref, g_ref = refs[:2]
        w_refs = refs[2:2 + npc]
        p_ref, n_ref = refs[2 + npc:]
        rstd, xhat = _rms_stats(h_ref[...])
        n = (xhat * g_ref[...]).astype(BF16)
        n_ref[...] = n
        off = 0
        for k in range(npc):
            w = w_refs[k][...]
            w = w.reshape(w.shape[-2], w.shape[-1])
            p_ref[:, off:off + widths[k]] = _dot_nt(n, w) if wt else _dot(n, w)
            off += widths[k]

    in_specs = [pl.BlockSpec((tm, d), lambda i: (i, 0)), pl.BlockSpec((1, d), lambda i: (0, 0))]
    for _, bs, idx in w_pieces:
        in_specs.append(pl.BlockSpec(bs, functools.partial(lambda i, idx: idx, idx=idx)))
    return pl.pallas_call(
        body, name="proj_fwd",
        grid=(tp // tm,),
        in_specs=in_specs,
        out_specs=[pl.BlockSpec((tm, ntot), lambda i: (i, 0)), pl.BlockSpec((tm, d), lambda i: (i, 0))],
        out_shape=[jax.ShapeDtypeStruct((tp, ntot), F32), jax.ShapeDtypeStruct((tp, d), BF16)],
        compiler_params=_cparams(1),
    )(h, norm, *[w for w, _, _ in w_pieces])


def _proj_bwd_act(dres, h, norm, dp_pieces, w_pieces, tm, wt=False):
    tp, d = h.shape
    npc = len(dp_pieces)
    nw = len(w_pieces)
    assert nw == npc or (nw == 1 and wt)

    def body(*refs):
        dres_ref, h_ref, g_ref = refs[:3]
        dp_refs = refs[3:3 + npc]
        w_refs = refs[3 + npc:3 + npc + nw]
        dh_ref, dg_ref = refs[3 + npc + nw:]
        i = pl.program_id(0)

        @pl.when(i == 0)
        def _():
            dg_ref[...] = jnp.zeros_like(dg_ref)

        dn = None
        off = 0
        for k in range(npc):
            if nw == npc:
                w = w_refs[k][...]
                w = w.reshape(w.shape[-2], w.shape[-1])
            else:
                w = w_refs[0][off:off + dp_pieces[k].shape[1], :]
                off += dp_pieces[k].shape[1]
            t = _dot(dp_refs[k][...], w) if wt else _dot_nt(dp_refs[k][...], w)
            dn = t if dn is None else dn + t
        rstd, xhat = _rms_stats(h_ref[...])
        dh, dg = _rms_bwd(dn, g_ref[...], rstd, xhat)
        dh_ref[...] = dres_ref[...] + dh
        dg_ref[...] += dg

    in_specs = [pl.BlockSpec((tm, d), lambda i: (i, 0)), pl.BlockSpec((tm, d), lambda i: (i, 0)),
                pl.BlockSpec((1, d), lambda i: (0, 0))]
    for dp in dp_pieces:
        in_specs.append(pl.BlockSpec((tm, dp.shape[1]), lambda i: (i, 0)))
    for _, bs, idx in w_pieces:
        in_specs.append(pl.BlockSpec(bs, functools.partial(lambda i, idx: idx, idx=idx)))
    return pl.pallas_call(
        body, name="proj_bwd_act",
        grid=(tp // tm,),
        in_specs=in_specs,
        out_specs=[pl.BlockSpec((tm, d), lambda i: (i, 0)), pl.BlockSpec((1, d), lambda i: (0, 0))],
        out_shape=[jax.ShapeDtypeStruct((tp, d), F32), jax.ShapeDtypeStruct((1, d), F32)],
        compiler_params=_cparams(1),
    )(dres, h, norm, *dp_pieces, *[w for w, _, _ in w_pieces])


def _proj_bwd_w(n, dp_pieces, tm, wt=False):
    tp, d = n.shape
    npc = len(dp_pieces)
    widths = [dp.shape[1] for dp in dp_pieces]
    oshape = (lambda w: (w, d)) if wt else (lambda w: (d, w))

    def body(*refs):
        n_ref = refs[0]
        dp_refs = refs[1:1 + npc]
        o_refs = refs[1 + npc:1 + 2 * npc]
        accs = refs[1 + 2 * npc:]
        i = pl.program_id(0)

        @pl.when(i == 0)
        def _():
            for acc in accs:
                acc[...] = jnp.zeros_like(acc)

        nn = n_ref[...]
        for k in range(npc):
            accs[k][...] += _dot_tn(dp_refs[k][...], nn) if wt else _dot_tn(nn, dp_refs[k][...])

        @pl.when(i == pl.num_programs(0) - 1)
        def _():
            for k in range(npc):
                o_refs[k][...] = accs[k][...].astype(BF16)

    return pl.pallas_call(
        body, name="proj_bwd_w",
        grid=(tp // tm,),
        in_specs=[pl.BlockSpec((tm, d), lambda i: (i, 0))]
        + [pl.BlockSpec((tm, w), lambda i: (i, 0)) for w in widths],
        out_specs=[pl.BlockSpec(oshape(w), lambda i: (0, 0)) for w in widths],
        out_shape=[jax.ShapeDtypeStruct(oshape(w), BF16) for w in widths],
        scratch_shapes=[pltpu.VMEM(oshape(w), F32) for w in widths],
        compiler_params=_cparams(1),
    )(n, *dp_pieces)


def _out_fwd(h, ya, yb, w, tm):
    tp, d = h.shape
    na, nb = ya.shape[1], yb.shape[1]

    def body(h_ref, ya_ref, yb_ref, w_ref, o_ref):
        y = _dot(ya_ref[...].astype(BF16), w_ref[0:na, :]) + _dot(yb_ref[...].astype(BF16), w_ref[na:, :])
        o_ref[...] = h_ref[...] + y

    return pl.pallas_call(
        body, name="out_fwd",
        grid=(tp // tm,),
        in_specs=[pl.BlockSpec((tm, d), lambda i: (i, 0)), pl.BlockSpec((tm, na), lambda i: (i, 0)),
                  pl.BlockSpec((tm, nb), lambda i: (i, 0)), pl.BlockSpec((d, d), lambda i: (0, 0))],
        out_specs=pl.BlockSpec((tm, d), lambda i: (i, 0)),
        out_shape=jax.ShapeDtypeStruct((tp, d), F32),
        compiler_params=_cparams(1),
    )(h, ya, yb, w)


def _out_bwd(dy, ya, yb, w, tm, after=()):
    tp, d = dy.shape
    na, nb = ya.shape[1], yb.shape[1]

    def body(dy_ref, ya_ref, yb_ref, w_ref, *rest):
        da_ref, db_ref, dw_ref, acc = rest[len(after):]
        i = pl.program_id(0)

        @pl.when(i == 0)
        def _():
            acc[...] = jnp.zeros_like(acc)

        dyb16 = dy_ref[...].astype(BF16)
        da_ref[...] = _dot_nt(dyb16, w_ref[0:na, :])
        db_ref[...] = _dot_nt(dyb16, w_ref[na:, :])
        acc[0:na, :] += _dot_tn(ya_ref[...].astype(BF16), dyb16)
        acc[na:, :] += _dot_tn(yb_ref[...].astype(BF16), dyb16)

        @pl.when(i == pl.num_programs(0) - 1)
        def _():
            dw_ref[...] = acc[...].astype(BF16)

    return pl.pallas_call(
        body, name="out_bwd",
        grid=(tp // tm,),
        in_specs=[pl.BlockSpec((tm, d), lambda i: (i, 0)), pl.BlockSpec((tm, na), lambda i: (i, 0)),
                  pl.BlockSpec((tm, nb), lambda i: (i, 0)), pl.BlockSpec((d, d), lambda i: (0, 0))]
        + [pl.BlockSpec(memory_space=pl.ANY)] * len(after),
        out_specs=[pl.BlockSpec((tm, na), lambda i: (i, 0)), pl.BlockSpec((tm, nb), lambda i: (i, 0)),
                   pl.BlockSpec((d, d), lambda i: (0, 0))],
        out_shape=[jax.ShapeDtypeStruct((tp, na), F32), jax.ShapeDtypeStruct((tp, nb), F32),
                   jax.ShapeDtypeStruct((d, d), BF16)],
        scratch_shapes=[pltpu.VMEM((d, d), F32)],
        compiler_params=_cparams(1),
    )(dy, ya, yb, w, *after)


def _loss_bwd(h, gfin, tgt, t_real, tm):
    tp, d = h.shape

    def body(h_ref, g_ref, t_ref, loss_ref, dh_ref, dg_ref):
        i = pl.program_id(0)

        @pl.when(i == 0)
        def _():
            loss_ref[...] = jnp.zeros_like(loss_ref)
            dg_ref[...] = jnp.zeros_like(dg_ref)

        rows = i * tm + lax.broadcasted_iota(jnp.int32, (tm, 1), 0)
        valid = (rows >= N_META) & (rows < t_real)
        rstd, xhat = _rms_stats(h_ref[...])
        g = g_ref[...]
        err = jnp.where(valid, xhat * g - t_ref[...], 0.0)
        e2 = jnp.sum(err * err, axis=1, keepdims=True)
        loss_ref[...] += (0.5 / d) * jnp.sum(e2, axis=0, keepdims=True)
        dy = err * (1.0 / d)
        dh, dg = _rms_bwd(dy, g, rstd, xhat)
        dh_ref[...] = dh
        dg_ref[...] += dg

    return pl.pallas_call(
        body, name="loss_bwd",
        grid=(tp // tm,),
        in_specs=[pl.BlockSpec((tm, d), lambda i: (i, 0)), pl.BlockSpec((1, d), lambda i: (0, 0)),
                  pl.BlockSpec((tm, d), lambda i: (i, 0))],
        out_specs=[pl.BlockSpec((1, 1), lambda i: (0, 0)), pl.BlockSpec((tm, d), lambda i: (i, 0)),
                   pl.BlockSpec((1, d), lambda i: (0, 0))],
        out_shape=[jax.ShapeDtypeStruct((1, 1), F32), jax.ShapeDtypeStruct((tp, d), F32),
                   jax.ShapeDtypeStruct((1, d), F32)],
        compiler_params=_cparams(1),
    )(h, gfin, tgt)


POOL_HALO = 16


def _pool_lane_consts(n_rows):
    lane = lax.broadcasted_iota(jnp.int32, (n_rows, D_POOL), 1)
    grp = lane // POOL_GROUP
    win = jnp.where(grp == 0, 2.0, jnp.where(grp == 1, 4.0, jnp.where(grp == 2, 8.0, 16.0)))
    return grp, win


def _pool_select(grp, s2, s4, s8, s16):
    return jnp.where(grp == 0, s2, jnp.where(grp == 1, s4, jnp.where(grp == 2, s8, s16)))


def _pool_mixed(x, row0, tr):
    n = tr + POOL_HALO
    s2 = x + pltpu.roll(x, 1, 0)
    s4 = s2 + pltpu.roll(s2, 2, 0)
    s8 = s4 + pltpu.roll(s4, 4, 0)
    s16 = s8 + pltpu.roll(s8, 8, 0)
    grp, win = _pool_lane_consts(n)
    rows = row0 - POOL_HALO + lax.broadcasted_iota(jnp.int32, (n, D_POOL), 0)
    cnt = jnp.minimum((rows + 1).astype(F32), win)
    pooled = _pool_select(grp, s2, s4, s8, s16) / jnp.maximum(cnt, 1.0)
    return (pooled - x)[POOL_HALO:, :]


def _pool_fwd(p, wbd, scale, tr):
    tp = p.shape[0]
    nt = tp // tr

    def body(p_ref, w_ref, s_ref, y_ref, usc):
        usc[0:POOL_HALO, :] = jnp.zeros((POOL_HALO, D_POOL), F32)
        usc[POOL_HALO:, :] = p_ref[...]

        def tile(r, carry):
            r0 = pl.multiple_of(r * tr, SUBLANE)
            x = usc[pl.ds(r0, tr + POOL_HALO), :]
            mixed = _pool_mixed(x, r0, tr)
            y_ref[pl.ds(r0, tr), :] = (_dot(mixed.astype(BF16), w_ref[...]) * s_ref[...]).astype(BF16)
            return carry

        lax.fori_loop(0, nt, tile, 0)

    return pl.pallas_call(
        body, name="pool_fwd",
        grid=(1,),
        in_specs=[pl.BlockSpec((tp, D_POOL), lambda i: (0, 0)), pl.BlockSpec((D_POOL, D_POOL), lambda i: (0, 0)),
                  pl.BlockSpec((1, D_POOL), lambda i: (0, 0))],
        out_specs=pl.BlockSpec((tp, D_POOL), lambda i: (0, 0)),
        out_shape=jax.ShapeDtypeStruct((tp, D_POOL), BF16),
        scratch_shapes=[pltpu.VMEM((tp + POOL_HALO, D_POOL), F32)],
        compiler_params=_cparams(1),
    )(p, wbd, scale)


def _pool_bwd(p, dya, wbd, scale, tr):
    tp = p.shape[0]
    nt = tp // tr

    def body(p_ref, dy_ref, w_ref, s_ref, du_ref, dw_ref, ds_ref, usc, gsc):
        usc[0:POOL_HALO, :] = jnp.zeros((POOL_HALO, D_POOL), F32)
        usc[POOL_HALO:, :] = p_ref[...]
        gsc[tp:, :] = jnp.zeros((POOL_HALO, D_POOL), F32)
        dw_ref[...] = jnp.zeros_like(dw_ref)
        ds_ref[...] = jnp.zeros_like(ds_ref)
        grp, win = _pool_lane_consts(tr)

        def tile1(r, carry):
            r0 = pl.multiple_of(r * tr, SUBLANE)
            x = usc[pl.ds(r0, tr + POOL_HALO), :]
            mixed = _pool_mixed(x, r0, tr).astype(BF16)
            dy = dy_ref[pl.ds(r0, tr), :]
            dys = (dy * s_ref[...]).astype(BF16)
            ypre = _dot(mixed, w_ref[...])
            ds_ref[...] += _colsum(dy * ypre)
            dw_ref[...] += _dot_tn(mixed, dys)
            dmx = _dot_nt(dys, w_ref[...])
            rows = r0 + lax.broadcasted_iota(jnp.int32, (tr, D_POOL), 0)
            cnt = jnp.minimum((rows + 1).astype(F32), win)
            gsc[pl.ds(r0, tr), :] = dmx / cnt
            return carry

        lax.fori_loop(0, nt, tile1, 0)
        n = tr + POOL_HALO
        grp2, win2 = _pool_lane_consts(n)

        def tile2(r, carry):
            r0 = pl.multiple_of(r * tr, SUBLANE)
            g = gsc[pl.ds(r0, n), :]
            s2 = g + pltpu.roll(g, n - 1, 0)
            s4 = s2 + pltpu.roll(s2, n - 2, 0)
            s8 = s4 + pltpu.roll(s4, n - 4, 0)
            s16 = s8 + pltpu.roll(s8, n - 8, 0)
            pooled_t = _pool_select(grp2, s2, s4, s8, s16)
            rows = r0 + lax.broadcasted_iota(jnp.int32, (n, D_POOL), 0)
            cnt = jnp.minimum((rows + 1).astype(F32), win2)
            du = pooled_t - g * cnt
            du_ref[pl.ds(r0, tr), :] = du[0:tr, :].astype(BF16)
            return carry

        lax.fori_loop(0, nt, tile2, 0)

    return pl.pallas_call(
        body, name="pool_bwd",
        grid=(1,),
        in_specs=[pl.BlockSpec((tp, D_POOL), lambda i: (0, 0)), pl.BlockSpec((tp, D_POOL), lambda i: (0, 0)),
                  pl.BlockSpec((D_POOL, D_POOL), lambda i: (0, 0)), pl.BlockSpec((1, D_POOL), lambda i: (0, 0))],
        out_specs=[pl.BlockSpec((tp, D_POOL), lambda i: (0, 0)), pl.BlockSpec((D_POOL, D_POOL), lambda i: (0, 0)),
                   pl.BlockSpec((1, D_POOL), lambda i: (0, 0))],
        out_shape=[jax.ShapeDtypeStruct((tp, D_POOL), BF16), jax.ShapeDtypeStruct((D_POOL, D_POOL), F32),
                   jax.ShapeDtypeStruct((1, D_POOL), F32)],
        scratch_shapes=[pltpu.VMEM((tp + POOL_HALO, D_POOL), F32), pltpu.VMEM((tp + POOL_HALO, D_POOL), F32)],
        compiler_params=_cparams(1),
    )(p, dya, wbd, scale)


def _hgrn_levels(ch):
    levels = []
    w = ch // 2
    while w >= 1:
        levels.append(w)
        w //= 2
    return levels


def _hgrn_consts(ch):
    t = np.arange(ch)
    tril = t[None, :] <= t[:, None]
    masks = []
    for w in _hgrn_levels(ch):
        blk = t // (2 * w)
        upper = t % (2 * w) >= w
        masks.append(upper[:, None] & (~upper)[None, :] & (blk[:, None] == blk[None, :]))
    masks.append(tril)
    msk = np.stack(masks).astype(np.float32)
    return jnp.asarray(tril.astype(np.float32), BF16), jnp.asarray(msk, F32), len(masks) - 1


def _split3(x):
    hi = x.astype(BF16)
    r1 = x - hi.astype(F32)
    mid = r1.astype(BF16)
    lo = (r1 - mid.astype(F32)).astype(BF16)
    return hi, mid, lo


def _hgrn_exponents(tril, logf):
    ch = logf.shape[0]
    hi, mid, lo = _split3(logf)
    x = _dot(tril, jnp.concatenate([hi, mid, lo], axis=1))
    b = x[:, 0:HEAD] + x[:, HEAD:2 * HEAD] + x[:, 2 * HEAD:3 * HEAD]
    rows = lax.broadcasted_iota(jnp.int32, (ch, HEAD), 0)
    fx = jnp.broadcast_to(b[ch - 1:ch, :], (ch, HEAD)) - b
    lev = []
    for w in _hgrn_levels(ch):
        pos = rows % (2 * w)
        upper = pos >= w
        if w >= SUBLANE:
            parts = [jnp.broadcast_to(b[k * 2 * w + w - 1:k * 2 * w + w, :], (2 * w, HEAD))
                     for k in range(ch // (2 * w))]
            bmid = parts[0] if len(parts) == 1 else jnp.concatenate(parts, axis=0)
            dx = jnp.where(upper, b - bmid, 0.0)
            ex = jnp.where(upper, 0.0, bmid - b)
        else:
            dx = logf
            ex = jnp.zeros_like(logf)
            for i in range(1, w):
                dx = dx + jnp.where(pos >= w + i, pltpu.roll(logf, i, 0), 0.0)
                ex = ex + jnp.where(pos <= w - 1 - i, pltpu.roll(logf, ch - i, 0), 0.0)
            dx = jnp.where(upper, dx, 0.0)
        lev.append((dx, ex))
    return b, fx, lev


def _hgrn_exponents_bwd(tril, d_b, d_fx, d_blast, lev_grads):
    ch = d_b.shape[0]
    rows = lax.broadcasted_iota(jnp.int32, (ch, HEAD), 0)
    db = d_b - d_fx
    dlf = jnp.zeros_like(d_b)
    for w, (ddx, dex) in zip(_hgrn_levels(ch), lev_grads):
        pos = rows % (2 * w)
        upper = pos >= w
        gu = jnp.where(upper, ddx, 0.0)
        if w >= SUBLANE:
            gl = jnp.where(upper, 0.0, dex)
            db = db + gu - gl
            diff = gl - gu
            for k in range(ch // (2 * w)):
                s = _colsum(diff[k * 2 * w:(k + 1) * 2 * w, :])
                db = db + jnp.where(rows == k * 2 * w + w - 1, s, 0.0)
        else:
            dlf = dlf + gu
            for i in range(1, w):
                dlf = dlf + pltpu.roll(jnp.where(pos >= w + i, gu, 0.0), ch - i, 0)
                dlf = dlf + pltpu.roll(jnp.where(pos <= w - 1 - i, dex, 0.0), i, 0)
    db = db + jnp.where(rows == ch - 1, _colsum(d_fx) + d_blast, 0.0)
    hi = db.astype(BF16)
    lo = (db - hi.astype(F32)).astype(BF16)
    d2 = _dot_tn(tril, jnp.concatenate([hi, lo], axis=1))
    return d2[:, 0:HEAD] + d2[:, HEAD:2 * HEAD] + dlf


def _lockstep(gens):
    results = [None] * len(gens)
    live = list(range(len(gens)))
    while live:
        for i in list(live):
            try:
                next(gens[i])
            except StopIteration as stop:
                results[i] = stop.value
                live.remove(i)
    return results


def _hgrn_gates(q_raw, z, lb):
    sz = _sigmoid(z)
    f = lb + (1.0 - lb) * sz
    q = q_raw * _sigmoid(q_raw)
    k = (1.0 - lb) * (1.0 - sz)
    return q, k, f, sz


def _hgrn_intra(q, k, lev, msk_ref, n_lev, ch):
    eye = (lax.broadcasted_iota(jnp.int32, (ch, ch), 0) == lax.broadcasted_iota(jnp.int32, (ch, ch), 1))
    a = jnp.where(eye, jnp.sum(q * k, axis=1, keepdims=True), 0.0)
    ops = []
    for lv in range(n_lev):
        eq = jnp.exp(lev[lv][0])
        ek = jnp.exp(lev[lv][1])
        qd = q * eq
        kd = k * ek
        a = a + msk_ref[lv] * _dot_nt(qd.astype(BF16), kd.astype(BF16))
        ops.append((eq, ek, qd, kd))
        yield
    return a, ops


def _hgrn_fwd(p, lb_logits, gnorm, mst, msk, n_lev, tm):
    tp = p.shape[0]
    ch = HG_CHUNK
    nct = tm // ch
    nt = tp // tm
    nr = mst.shape[0]
    base = D_POOL // HEAD

    hp = HG_HEADS_PER_STEP
    wide = hp * HEAD
    npr = wide // HG_PBLOCK

    def body(*refs):
        p_refs = refs[:4 * npr]
        lg_ref, gn_ref, mst_ref, msk_ref, y_ref, ss_ref, st_sc = refs[4 * npr:]

        @pl.when(pl.program_id(1) == 0)
        def _():
            st_sc[...] = jnp.zeros_like(st_sc)

        lb_all = _sigmoid(lg_ref[0:1, :] - lg_ref[1:2, :])

        def raw(seg, hh, r0):
            per = HG_PBLOCK // HEAD
            return p_refs[seg * npr + hh // per][pl.ds(r0, ch), (hh % per) * HEAD:(hh % per + 1) * HEAD]

        def one_head(hh, c, r0):
            ls = slice(hh * HEAD, (hh + 1) * HEAD)
            q_raw, z, v, g_raw, st = raw(0, hh, r0), raw(1, hh, r0), raw(2, hh, r0), raw(3, hh, r0), st_sc[hh]
            q, k, f, _ = _hgrn_gates(q_raw, z, lb_all[:, ls])
            yield
            b, fx, lev = _hgrn_exponents(mst_ref[...], jnp.log(f))
            yield
            qe = q * jnp.exp(b)
            a, _ = yield from _hgrn_intra(q, k, lev, msk_ref, n_lev, ch)
            v16 = v.astype(BF16)
            o = _dot_nt(qe.astype(BF16), st.astype(BF16)) + _dot(a.astype(BF16), v16)
            kl = k * jnp.exp(fx)
            st_new = st * jnp.exp(b[ch - 1:ch, :]) + _dot_tn(v16, kl.astype(BF16))
            yield
            rstd = lax.rsqrt(jnp.mean(o * o, axis=-1, keepdims=True) + EPS)
            return st, st_new, o * rstd * gn_ref[...] * (g_raw * _sigmoid(g_raw))

        def chunk(c, carry):
            r0 = pl.multiple_of(c * ch, ch)
            results = _lockstep([one_head(hh, c, r0) for hh in range(hp)])
            for hh, (st, st_new, y) in enumerate(results):
                ss_ref[hh, c] = st
                st_sc[hh] = st_new
                y_ref[pl.ds(r0, ch), hh * HEAD:(hh + 1) * HEAD] = y.astype(BF16)
            return carry

        lax.fori_loop(0, nct, chunk, 0)

    def pspec(seg, part):
        return pl.BlockSpec((tm, HG_PBLOCK),
                            lambda h, i: (i, (base + seg * HG_HEADS) * HEAD // HG_PBLOCK + h * npr + part))

    return pl.pallas_call(
        body, name="hgrn_fwd",
        grid=(HG_HEADS // hp, nt),
        in_specs=[pspec(seg, part) for seg in range(4) for part in range(npr)]
        + [pl.BlockSpec((2, wide), lambda h, i: (0, h)),
           pl.BlockSpec((1, HEAD), lambda h, i: (0, 0)),
           pl.BlockSpec((nr, ch), lambda h, i: (0, 0)),
           pl.BlockSpec((n_lev + 1, ch, ch), lambda h, i: (0, 0, 0))],
        out_specs=[pl.BlockSpec((tm, wide), lambda h, i: (i, h)),
                   pl.BlockSpec((hp, nct, HEAD, HEAD), lambda h, i: (h, i, 0, 0))],
        out_shape=[jax.ShapeDtypeStruct((tp, D_HGRN), BF16),
                   jax.ShapeDtypeStruct((HG_HEADS, tp // ch, HEAD, HEAD), F32)],
        scratch_shapes=[pltpu.VMEM((hp, HEAD, HEAD), F32)],
        compiler_params=_cparams(2),
    )(*([p] * (4 * npr)), lb_logits, gnorm, mst, msk)


def _hgrn_bwd(p, dyb, states, lb_logits, gnorm, mst, msk, n_lev, tm):
    tp = p.shape[0]
    ch = HG_CHUNK
    nct = tm // ch
    nt = tp // tm
    nr = mst.shape[0]
    base = D_POOL // HEAD

    hp = HG_HEADS_PER_STEP
    wide = hp * HEAD
    npr = wide // HG_PBLOCK

    def body(*refs):
        p_refs = refs[:4 * npr]
        (dy_ref, ss_ref, lg_ref, gn_ref, mst_ref, msk_ref,
         dq_ref, dz_ref, dv_ref, dg_ref, dlg_ref, dgn_ref, dst_sc, dlb_sc) = refs[4 * npr:]
        ti = pl.program_id(1)

        def raw(seg, hh, r0):
            per = HG_PBLOCK // HEAD
            return p_refs[seg * npr + hh // per][pl.ds(r0, ch), (hh % per) * HEAD:(hh % per + 1) * HEAD]

        @pl.when(ti == 0)
        def _():
            dst_sc[...] = jnp.zeros_like(dst_sc)
            dlb_sc[...] = jnp.zeros_like(dlb_sc)
            dgn_ref[...] = jnp.zeros_like(dgn_ref)

        lb_all = _sigmoid(lg_ref[0:1, :] - lg_ref[1:2, :])
        gn = gn_ref[...]

        def load_head(hh, c, r0):
            ls = slice(hh * HEAD, (hh + 1) * HEAD)
            return (raw(0, hh, r0), raw(1, hh, r0), raw(2, hh, r0), raw(3, hh, r0),
                    dy_ref[pl.ds(r0, ch), ls], ss_ref[hh, c], dst_sc[hh])

        def store_head(hh, r0, res):
            ls = slice(hh * HEAD, (hh + 1) * HEAD)
            dq_raw, dz, dv, dg_raw, dgn, dst_new, dlb = res
            dq_ref[pl.ds(r0, ch), ls] = dq_raw
            dz_ref[pl.ds(r0, ch), ls] = dz
            dv_ref[pl.ds(r0, ch), ls] = dv
            dg_ref[pl.ds(r0, ch), ls] = dg_raw
            dgn_ref[hh] += dgn
            dst_sc[hh] = dst_new
            dlb_sc[:, ls] += dlb

        def one_head(hh, loaded):
            ls = slice(hh * HEAD, (hh + 1) * HEAD)
            lb = lb_all[:, ls]
            q_raw, z, v, g_raw, dy, st, dst = loaded
            q, k, f, sz = _hgrn_gates(q_raw, z, lb)
            yield
            b, fx, lev = _hgrn_exponents(mst_ref[...], jnp.log(f))
            yield
            eb = jnp.exp(b)
            ef = jnp.exp(fx)
            elast = jnp.exp(b[ch - 1:ch, :])
            qe = q * eb
            kl = k * ef
            a, ops = yield from _hgrn_intra(q, k, lev, msk_ref, n_lev, ch)
            v16 = v.astype(BF16)
            st16 = st.astype(BF16)
            qe16 = qe.astype(BF16)
            kl16 = kl.astype(BF16)
            a16 = a.astype(BF16)
            o = _dot_nt(qe16, st16) + _dot(a16, v16)
            yield
            sg = _sigmoid(g_raw)
            rstd = lax.rsqrt(jnp.mean(o * o, axis=-1, keepdims=True) + EPS)
            oh = o * rstd
            dg_out = (dy * oh * gn * (sg * (1.0 + g_raw * (1.0 - sg)))).astype(BF16)
            don = dy * (g_raw * sg)
            dgn = _colsum(don * oh)
            doh = don * gn
            do = rstd * (doh - oh * jnp.mean(doh * oh, axis=-1, keepdims=True))
            do16 = do.astype(BF16)
            dst16 = dst.astype(BF16)
            yield
            dv = _dot_tn(a16, do16) + _dot_nt(kl16, dst16)
            da = msk_ref[n_lev] * _dot_nt(do16, v16)
            dqe = _dot(do16, st16)
            dkl = _dot(v16, dst16)
            dst_new = dst * elast + _dot_tn(do16, qe16)
            yield
            db_last = _colsum(dst * st) * elast
            dad = jnp.sum(do * v, axis=1, keepdims=True)
            dq = dad * k + dqe * eb
            dk = dad * q + dkl * ef
            lev_grads = []
            for lv in range(n_lev):
                eq, ek, qd, kd = ops[lv]
                gl = (msk_ref[lv] * da).astype(BF16)
                dqd = _dot(gl, kd.astype(BF16))
                dkd = _dot_tn(gl, qd.astype(BF16))
                dq = dq + dqd * eq
                dk = dk + dkd * ek
                lev_grads.append((dqd * qd, dkd * kd))
                yield
            dlogf = _hgrn_exponents_bwd(mst_ref[...], dqe * qe, dkl * kl, db_last, lev_grads)
            yield
            sq = _sigmoid(q_raw)
            dq_out = (dq * (sq * (1.0 + q_raw * (1.0 - sq)))).astype(BF16)
            dfk = dlogf / f - dk
            dz_out = (dfk * (1.0 - lb) * sz * (1.0 - sz)).astype(BF16)
            return dq_out, dz_out, dv.astype(BF16), dg_out, dgn, dst_new, _colsum(dfk * (1.0 - sz))

        def chunk(cc, carry):
            c = nct - 1 - cc
            r0 = pl.multiple_of(c * ch, ch)
            loaded = [load_head(hh, c, r0) for hh in range(HG_HEADS_PER_STEP)]
            results = _lockstep([one_head(hh, loaded[hh]) for hh in range(HG_HEADS_PER_STEP)])
            for hh in range(HG_HEADS_PER_STEP):
                store_head(hh, r0, results[hh])
            return carry

        lax.fori_loop(0, nct, chunk, 0, unroll=1)

        @pl.when(ti == nt - 1)
        def _():
            dl0 = dlb_sc[...] * lb_all * (1.0 - lb_all)
            dlg_ref[0:1, :] = dl0
            dlg_ref[1:2, :] = -dl0

    def pspec(seg, part):
        return pl.BlockSpec((tm, HG_PBLOCK), lambda h, i: (
            nt - 1 - i, (base + seg * HG_HEADS) * HEAD // HG_PBLOCK + h * npr + part))

    ospec = pl.BlockSpec((tm, wide), lambda h, i: (nt - 1 - i, h))
    return pl.pallas_call(
        body, name="hgrn_bwd",
        grid=(HG_HEADS // hp, nt),
        in_specs=[pspec(seg, part) for seg in range(4) for part in range(npr)]
        + [ospec, pl.BlockSpec((hp, nct, HEAD, HEAD), lambda h, i: (h, nt - 1 - i, 0, 0)),
           pl.BlockSpec((2, wide), lambda h, i: (0, h)),
           pl.BlockSpec((1, HEAD), lambda h, i: (0, 0)),
           pl.BlockSpec((nr, ch), lambda h, i: (0, 0)),
           pl.BlockSpec((n_lev + 1, ch, ch), lambda h, i: (0, 0, 0))],
        out_specs=[ospec, ospec, ospec, ospec,
                   pl.BlockSpec((2, wide), lambda h, i: (0, h)),
                   pl.BlockSpec((hp, 1, HEAD), lambda h, i: (h, 0, 0))],
        out_shape=[jax.ShapeDtypeStruct((tp, D_HGRN), BF16)] * 4
        + [jax.ShapeDtypeStruct((2, D_HGRN), F32), jax.ShapeDtypeStruct((HG_HEADS, 1, HEAD), F32)],
        scratch_shapes=[pltpu.VMEM((hp, HEAD, HEAD), F32), pltpu.VMEM((1, wide), F32)],
        compiler_params=_cparams(2),
    )(*([p] * (4 * npr)), dyb, states, lb_logits, gnorm, mst, msk)


def _tap_views(x, tr, halo, width):
    subs = {0: x}
    views = []
    for j in range(width):
        tiles, rem = divmod(width - 1 - j, SUBLANE)
        if rem not in subs:
            subs[rem] = pltpu.roll(x, rem, 0)
        start = halo - tiles * SUBLANE
        views.append(subs[rem][start:start + tr, :])
    return views


def _tap_views_t(y, tr, halo, width):
    n = tr + halo
    subs = {0: y}
    views = []
    for j in range(width):
        tiles, rem = divmod(width - 1 - j, SUBLANE)
        if rem not in subs:
            subs[rem] = pltpu.roll(y, n - rem, 0)
        views.append(subs[rem][tiles * SUBLANE:tiles * SUBLANE + tr, :])
    return views


def _weighted_sum(views, w_ref):
    acc = None
    for j, view in enumerate(views):
        term = view * w_ref[j:j + 1, :]
        acc = term if acc is None else acc + term
    return acc


def _conv_taps(x, w_ref, tr, halo, width):
    return _weighted_sum(_tap_views(x, tr, halo, width), w_ref)


def _conv_taps_t(y, w_ref, tr, halo, width):
    return _weighted_sum(_tap_views_t(y, tr, halo, width), w_ref)


def _ln_stats(cv):
    mu = jnp.mean(cv, axis=-1, keepdims=True)
    xc = cv - mu
    rstd = lax.rsqrt(jnp.mean(xc * xc, axis=-1, keepdims=True) + EPS)
    return rstd, xc * rstd


def _convmod_fwd(p, w, bias, ln_g, ln_b, tr):
    tp = p.shape[0]
    nt = tp // tr
    nb = D_CONV // HEAD

    def body(a_ref, b_ref, w_ref, bi_ref, g_ref, be_ref, y_ref, cv_ref, usc):
        usc[0:CONV_HALO, :] = jnp.zeros((CONV_HALO, HEAD), F32)
        usc[CONV_HALO:, :] = a_ref[...] * _sigmoid(b_ref[...])

        def tile(r, carry):
            r0 = pl.multiple_of(r * tr, SUBLANE)
            x = usc[pl.ds(r0, tr + CONV_HALO), :]
            cv = _conv_taps(x, w_ref, tr, CONV_HALO, CONV_WIDTH) + bi_ref[...]
            cv_ref[pl.ds(r0, tr), :] = cv
            _, xh = _ln_stats(cv)
            un = xh * g_ref[...] + be_ref[...]
            y_ref[pl.ds(r0, tr), :] = (un * _sigmoid(un)).astype(BF16)
            return carry

        lax.fori_loop(0, nt, tile, 0)

    vec = lambda: pl.BlockSpec((1, HEAD), lambda j: (0, j))
    return pl.pallas_call(
        body, name="convmod_fwd",
        grid=(nb,),
        in_specs=[pl.BlockSpec((tp, HEAD), lambda j: (0, j)), pl.BlockSpec((tp, HEAD), lambda j: (0, nb + j)),
                  pl.BlockSpec((CONV_HALO, HEAD), lambda j: (0, j)), vec(), vec(), vec()],
        out_specs=[pl.BlockSpec((tp, HEAD), lambda j: (0, j))] * 2,
        out_shape=[jax.ShapeDtypeStruct((tp, D_CONV), BF16), jax.ShapeDtypeStruct((tp, D_CONV), F32)],
        scratch_shapes=[pltpu.VMEM((tp + CONV_HALO, HEAD), F32)],
        compiler_params=_cparams(1),
    )(p, p, w, bias, ln_g, ln_b)


def _convmod_bwd(p, cv_saved, dyc, w, ln_g, ln_b, tr):
    tp = p.shape[0]
    nt = tp // tr
    nb = D_CONV // HEAD

    def body(a_ref, b_ref, cv_ref, dy_ref, w_ref, g_ref, be_ref, da_ref, db_ref, dw_ref, dv_ref, usc, dsc):
        usc[0:CONV_HALO, :] = jnp.zeros((CONV_HALO, HEAD), F32)
        usc[CONV_HALO:, :] = a_ref[...] * _sigmoid(b_ref[...])
        dsc[tp:, :] = jnp.zeros((CONV_HALO, HEAD), F32)
        dw_ref[...] = jnp.zeros_like(dw_ref)
        dv_ref[...] = jnp.zeros_like(dv_ref)

        def tile1(r, carry):
            r0 = pl.multiple_of(r * tr, SUBLANE)
            x = usc[pl.ds(r0, tr + CONV_HALO), :]
            views = _tap_views(x, tr, CONV_HALO, CONV_WIDTH)
            rstd, xh = _ln_stats(cv_ref[pl.ds(r0, tr), :])
            un = xh * g_ref[...] + be_ref[...]
            sg = _sigmoid(un)
            dun = dy_ref[pl.ds(r0, tr), :] * (sg * (1.0 + un * (1.0 - sg)))
            dv_ref[0, 1:2, :] += _colsum(dun * xh)
            dv_ref[0, 2:3, :] += _colsum(dun)
            dxh = dun * g_ref[...]
            dcv = rstd * (dxh - jnp.mean(dxh, axis=-1, keepdims=True)
                          - xh * jnp.mean(dxh * xh, axis=-1, keepdims=True))
            dv_ref[0, 0:1, :] += _colsum(dcv)
            for j in range(CONV_WIDTH):
                dw_ref[0, j:j + 1, :] += _colsum(dcv * views[j])
            dsc[pl.ds(r0, tr), :] = dcv
            return carry

        lax.fori_loop(0, nt, tile1, 0)

        def tile2(r, carry):
            r0 = pl.multiple_of(r * tr, SUBLANE)
            y = dsc[pl.ds(r0, tr + CONV_HALO), :]
            du = _conv_taps_t(y, w_ref, tr, CONV_HALO, CONV_WIDTH)
            a = a_ref[pl.ds(r0, tr), :]
            sb = _sigmoid(b_ref[pl.ds(r0, tr), :])
            da_ref[pl.ds(r0, tr), :] = (du * sb).astype(BF16)
            db_ref[pl.ds(r0, tr), :] = (du * a * sb * (1.0 - sb)).astype(BF16)
            return carry

        lax.fori_loop(0, nt, tile2, 0)

    vec = lambda: pl.BlockSpec((1, HEAD), lambda j: (0, j))
    col = lambda: pl.BlockSpec((tp, HEAD), lambda j: (0, j))
    return pl.pallas_call(
        body, name="convmod_bwd",
        grid=(nb,),
        in_specs=[col(), pl.BlockSpec((tp, HEAD), lambda j: (0, nb + j)), col(), col(),
                  pl.BlockSpec((CONV_HALO, HEAD), lambda j: (0, j)), vec(), vec()],
        out_specs=[col(), col(), pl.BlockSpec((1, CONV_HALO, HEAD), lambda j: (j, 0, 0)),
                   pl.BlockSpec((1, SUBLANE, HEAD), lambda j: (j, 0, 0))],
        out_shape=[jax.ShapeDtypeStruct((tp, D_CONV), BF16), jax.ShapeDtypeStruct((tp, D_CONV), BF16),
                   jax.ShapeDtypeStruct((nb, CONV_HALO, HEAD), F32), jax.ShapeDtypeStruct((nb, SUBLANE, HEAD), F32)],
        scratch_shapes=[pltpu.VMEM((tp + CONV_HALO, HEAD), F32), pltpu.VMEM((tp + CONV_HALO, HEAD), F32)],
        compiler_params=_cparams(1),
    )(p, p, cv_saved, dyc, w, ln_g, ln_b)


def _log1p_small(y):
    return jnp.where(y < 1e-4, y * (1.0 - 0.5 * y), jnp.log(1.0 + y))


def _softplus(x):
    return jnp.maximum(x, 0.0) + _log1p_small(jnp.exp(-jnp.abs(x)))


def _expm1(x):
    return jnp.where(jnp.abs(x) < 1e-2, x * (1.0 + 0.5 * x * (1.0 + x * (1.0 / 3.0))), jnp.exp(x) - 1.0)


def _gelu_parts(x):
    c = 0.7978845608028654
    inner = c * (x + 0.044715 * x * x * x)
    th = jnp.tanh(inner)
    gelu = 0.5 * x * (1.0 + th)
    dgelu = 0.5 * (1.0 + th) + 0.5 * x * (1.0 - th * th) * c * (1.0 + 3.0 * 0.044715 * x * x)
    return gelu, dgelu


def _lru_gates(x_all, tp, cw_ref, cb_ref, wa_ref, ba_ref, wx_ref, bx_ref, lam_ref):
    u = _conv_taps(x_all, cw_ref, tp, LRU_HALO, LRU_CONV) + cb_ref[...]
    u16 = u.astype(BF16)
    r = _sigmoid(_dot(u16, wa_ref[0]) + ba_ref[...])
    i = _sigmoid(_dot(u16, wx_ref[0]) + bx_ref[...])
    sp = _softplus(-lam_ref[...])
    la = -LRU_C * r * sp
    a = jnp.exp(la)
    mult = jnp.sqrt(-_expm1(2.0 * la))
    return u, r, i, a, mult, sp


def _lru_specs(tp, nb):
    col = lambda k: pl.BlockSpec((tp, HEAD), functools.partial(lambda j, k: (0, k * nb + j), k=k))
    vec = lambda: pl.BlockSpec((1, HEAD), lambda j: (0, j))
    mat = lambda: pl.BlockSpec((1, HEAD, HEAD), lambda j: (j, 0, 0))
    return col, vec, mat


def _lru_fwd(p, cw, cb, wa, ba, wx, bx, lam):
    tp = p.shape[0]
    nb = D_LRU // HEAD
    ng = tp // SUBLANE

    def body(x_ref, gt_ref, cw_ref, cb_ref, wa_ref, ba_ref, wx_ref, bx_ref, lam_ref, y_ref, hs_ref,
             xsc, asc, bsc):
        xsc[0:LRU_HALO, :] = jnp.zeros((LRU_HALO, HEAD), F32)
        xsc[LRU_HALO:, :] = x_ref[...]
        u, r, i, a, mult, _ = _lru_gates(xsc[...], tp, cw_ref, cb_ref, wa_ref, ba_ref, wx_ref, bx_ref, lam_ref)
        rows = lax.broadcasted_iota(jnp.int32, (tp, HEAD), 0)
        b = jnp.where(rows == 0, 1.0, mult) * (i * u)
        sub = rows % SUBLANE
        for k in (1, 2, 4):
            m = sub >= k
            b = jnp.where(m, a * pltpu.roll(b, k, 0) + b, b)
            a = jnp.where(m, a * pltpu.roll(a, k, 0), a)
        asc[...] = a
        bsc[...] = b

        def grp(g, carry):
            r0 = pl.multiple_of(g * SUBLANE, SUBLANE)
            h = bsc[pl.ds(r0, SUBLANE), :] + asc[pl.ds(r0, SUBLANE), :] * carry
            hs_ref[pl.ds(r0, SUBLANE), :] = h
            return jnp.broadcast_to(h[SUBLANE - 1:SUBLANE, :], (SUBLANE, HEAD))

        lax.fori_loop(0, ng, grp, jnp.zeros((SUBLANE, HEAD), F32))
        gelu, _ = _gelu_parts(gt_ref[...])
        y_ref[...] = (gelu * hs_ref[...]).astype(BF16)

    col, vec, mat = _lru_specs(tp, nb)
    return pl.pallas_call(
        body, name="lru_fwd",
        grid=(nb,),
        in_specs=[col(2), col(3), pl.BlockSpec((LRU_CONV, HEAD), lambda j: (0, j)), vec(), mat(), vec(), mat(),
                  vec(), vec()],
        out_specs=[pl.BlockSpec((tp, HEAD), lambda j: (0, j)), pl.BlockSpec((tp, HEAD), lambda j: (0, j))],
        out_shape=[jax.ShapeDtypeStruct((tp, D_LRU), BF16), jax.ShapeDtypeStruct((tp, D_LRU), F32)],
        scratch_shapes=[pltpu.VMEM((tp + LRU_HALO, HEAD), F32), pltpu.VMEM((tp, HEAD), F32),
                        pltpu.VMEM((tp, HEAD), F32)],
        compiler_params=_cparams(1),
    )(p, p, cw, cb, wa, ba, wx, bx, lam)


def _lru_bwd(p, hs, dyd, cw, cb, wa, ba, wx, bx, lam):
    tp = p.shape[0]
    nb = D_LRU // HEAD
    ng = tp // SUBLANE

    def body(x_ref, gt_ref, hs_ref, dy_ref, cw_ref, cb_ref, wa_ref, ba_ref, wx_ref, bx_ref, lam_ref,
             dx_ref, dgt_ref, dwa_ref, dwx_ref, dv_ref, xsc, asc, bsc, gsc, dusc):
        xsc[0:LRU_HALO, :] = jnp.zeros((LRU_HALO, HEAD), F32)
        xsc[LRU_HALO:, :] = x_ref[...]
        x_all = xsc[...]
        u, r, i, a, mult, sp = _lru_gates(x_all, tp, cw_ref, cb_ref, wa_ref, ba_ref, wx_ref, bx_ref, lam_ref)
        rows = lax.broadcasted_iota(jnp.int32, (tp, HEAD), 0)
        hs = hs_ref[...]
        dy = dy_ref[...]
        gelu, dgelu = _gelu_parts(gt_ref[...])
        dgt_ref[...] = (dy * hs * dgelu).astype(BF16)
        bb = dy * gelu
        aa = jnp.where(rows == tp - 1, 0.0, pltpu.roll(a, tp - 1, 0))
        sub = rows % SUBLANE
        for k in (1, 2, 4):
            m = sub < SUBLANE - k
            bb = jnp.where(m, aa * pltpu.roll(bb, tp - k, 0) + bb, bb)
            aa = jnp.where(m, aa * pltpu.roll(aa, tp - k, 0), aa)
        asc[...] = aa
        bsc[...] = bb

        def grp(gi, carry):
            g = ng - 1 - gi
            r0 = pl.multiple_of(g * SUBLANE, SUBLANE)
            gg = bsc[pl.ds(r0, SUBLANE), :] + asc[pl.ds(r0, SUBLANE), :] * carry
            gsc[pl.ds(r0, SUBLANE), :] = gg
            return jnp.broadcast_to(gg[0:1, :], (SUBLANE, HEAD))

        lax.fori_loop(0, ng, grp, jnp.zeros((SUBLANE, HEAD), F32))
        g = gsc[...]
        first = rows == 0
        hprev = jnp.where(first, 0.0, pltpu.roll(hs, 1, 0))
        iu = i * u
        d_iu = g * jnp.where(first, 1.0, mult)
        dmult_term = jnp.where(first, 0.0, g * iu * (-(a * a) / mult))
        dla = g * hprev * a + dmult_term
        dr = dla * (-LRU_C) * sp
        dv_ref[0, 7:8, :] = _colsum(dla * (LRU_C * r) * _sigmoid(-lam_ref[...]))
        dpr = dr * r * (1.0 - r)
        dpi = d_iu * u * i * (1.0 - i)
        dv_ref[0, 5:6, :] = _colsum(dpr)
        dv_ref[0, 6:7, :] = _colsum(dpi)
        u16 = u.astype(BF16)
        dpr16 = dpr.astype(BF16)
        dpi16 = dpi.astype(BF16)
        dwa_ref[0] = _dot_tn(u16, dpr16)
        dwx_ref[0] = _dot_tn(u16, dpi16)
        du = d_iu * i + _dot_nt(dpr16, wa_ref[0]) + _dot_nt(dpi16, wx_ref[0])
        dv_ref[0, 4:5, :] = _colsum(du)
        for j in range(LRU_CONV):
            sh = LRU_CONV - 1 - j
            xs = x_all if sh == 0 else pltpu.roll(x_all, sh, 0)
            dv_ref[0, j:j + 1, :] = _colsum(du * xs[LRU_HALO:, :])
        dusc[0:tp, :] = du
        dusc[tp:, :] = jnp.zeros((LRU_HALO, HEAD), F32)
        dx_ref[...] = _conv_taps_t(dusc[...], cw_ref, tp, LRU_HALO, LRU_CONV).astype(BF16)

    col, vec, mat = _lru_specs(tp, nb)
    ocol = lambda: pl.BlockSpec((tp, HEAD), lambda j: (0, j))
    return pl.pallas_call(
        body, name="lru_bwd",
        grid=(nb,),
        in_specs=[col(2), col(3), ocol(), ocol(), pl.BlockSpec((LRU_CONV, HEAD), lambda j: (0, j)), vec(), mat(),
                  vec(), mat(), vec(), vec()],
        out_specs=[ocol(), ocol(), mat(), mat(), pl.BlockSpec((1, SUBLANE, HEAD), lambda j: (j, 0, 0))],
        out_shape=[jax.ShapeDtypeStruct((tp, D_LRU), BF16), jax.ShapeDtypeStruct((tp, D_LRU), BF16),
                   jax.ShapeDtypeStruct((nb, HEAD, HEAD), F32), jax.ShapeDtypeStruct((nb, HEAD, HEAD), F32),
                   jax.ShapeDtypeStruct((nb, SUBLANE, HEAD), F32)],
        scratch_shapes=[pltpu.VMEM((tp + LRU_HALO, HEAD), F32), pltpu.VMEM((tp, HEAD), F32),
                        pltpu.VMEM((tp, HEAD), F32), pltpu.VMEM((tp, HEAD), F32),
                        pltpu.VMEM((tp + LRU_HALO, HEAD), F32)],
        compiler_params=_cparams(1),
    )(p, p, hs, dyd, cw, cb, wa, ba, wx, bx, lam)


def _mesh_pos():
    return lax.axis_index("x"), lax.axis_index("y"), lax.axis_index("c")


def _other_chips(x, y):
    return [(1 - x, y), (x, 1 - y), (1 - x, 1 - y)]


ANY = pl.BlockSpec(memory_space=pl.ANY)


HBM = pl.BlockSpec(memory_space=pltpu.HBM)
SEM = pl.BlockSpec(memory_space=pltpu.SEMAPHORE)
DATAFLOW = pltpu.SideEffectType.DATAFLOW_SIDE_EFFECTING
N_PEERS = 4


def _in_hbm(a):
    return pltpu.with_memory_space_constraint(a, pltpu.HBM)


def _gather_peers(x, y, c):
    return [((ox, oy, c), 2 * ox + oy) for ox, oy in _other_chips(x, y)] + [((x, y, 1 - c), 2 * x + y)]


def _gather_refs(src, land, slot, c, split):
    if not split:
        return src, land.at[slot]
    half = src.shape[0] // 2
    return src.at[pl.ds(c * half, half)], land.at[slot, pl.ds(c * half, half)]


def _gather_start(arrs, split):
    n = len(arrs)

    def body(*refs):
        ins, lands = refs[:n], refs[n:2 * n]
        ssem, rsem = refs[2 * n:2 * n + 2]
        token = refs[-1]
        x, y, c = _mesh_pos()
        chip = 2 * x + y
        for k in range(n):
            for j, (dev, _) in enumerate(_gather_peers(x, y, c)):
                src, dst = _gather_refs(ins[k], lands[k], chip, c, split[k] and j < N_PEERS - 1)
                pltpu.make_async_remote_copy(
                    src_ref=src, dst_ref=dst, send_sem=ssem.at[N_PEERS * k + j],
                    recv_sem=rsem.at[N_PEERS * k + j], device_id=dev, device_id_type=MESH).start()
        token[...] = jnp.zeros_like(token)

    lands = [_in_hbm(lax.empty((N_SHARD,) + a.shape, a.dtype)) for a in arrs]
    out = pl.pallas_call(
        body, name="gather_start",
        in_specs=[HBM] * (2 * n),
        out_specs=[SEM, SEM] + [HBM] * (2 * n) + [pl.BlockSpec(memory_space=pltpu.VMEM)],
        out_shape=[pltpu.SemaphoreType.DMA((N_PEERS * n,)), pltpu.SemaphoreType.DMA((N_PEERS * n,))]
        + [pltpu.HBM(a.shape, a.dtype) for a in arrs]
        + [pltpu.HBM((N_SHARD,) + a.shape, a.dtype) for a in arrs]
        + [jax.ShapeDtypeStruct((SUBLANE, LANE), F32)],
        input_output_aliases={k: 2 + k for k in range(2 * n)},
        compiler_params=pltpu.CompilerParams(has_side_effects=DATAFLOW),
    )(*[_in_hbm(a) for a in arrs], *lands)
    return out[0], out[1], list(out[2:2 + n]), list(out[2 + n:2 + 2 * n]), out[-1]


def _gather_wait(ssem, rsem, srcs, lands, ks, after, split=False):
    n = len(ks)

    def body(*refs):
        ins, lnd = refs[:n], refs[n:2 * n]
        ssem_ref, rsem_ref = refs[2 * n:2 * n + 2]
        x, y, c = _mesh_pos()
        for i, k in enumerate(ks):
            for j, (dev, pchip) in enumerate(_gather_peers(x, y, c)):
                src, dst = _gather_refs(ins[i], lnd[i], pchip, c, split and j < N_PEERS - 1)
                cp = pltpu.make_async_remote_copy(
                    src_ref=src, dst_ref=dst, send_sem=ssem_ref.at[N_PEERS * k + j],
                    recv_sem=rsem_ref.at[N_PEERS * k + j], device_id=dev, device_id_type=MESH)
                cp.wait_send()
                cp.wait_recv()

    out = pl.pallas_call(
        body, name="gather_wait",
        in_specs=[HBM] * (2 * n) + [SEM, SEM] + [ANY] * len(after),
        out_specs=[HBM] * (2 * n),
        out_shape=[pltpu.HBM(a.shape, a.dtype) for a in srcs] + [pltpu.HBM(a.shape, a.dtype) for a in lands],
        input_output_aliases={k: k for k in range(2 * n)},
        compiler_params=pltpu.CompilerParams(has_side_effects=DATAFLOW),
    )(*srcs, *lands, ssem, rsem, *after)
    return list(out[n:])


def _pair_forward(lands):
    n = len(lands)

    def body(*refs):
        outs = refs[n:2 * n]
        ssem, rsem = refs[2 * n:]
        x, y, c = _mesh_pos()
        sibling = (x, y, 1 - c)
        cps = []
        for k in range(n):
            half = lands[k].shape[1] // 2
            for j, (ox, oy) in enumerate(_other_chips(x, y)):
                mine = outs[k].at[2 * ox + oy, pl.ds(c * half, half)]
                cp = pltpu.make_async_remote_copy(src_ref=mine, dst_ref=mine, send_sem=ssem.at[3 * k + j],
                                                  recv_sem=rsem.at[3 * k + j], device_id=sibling, device_id_type=MESH)
                cp.start()
                cps.append(cp)
        for k in range(n):
            half = lands[k].shape[1] // 2
            for j, (ox, oy) in enumerate(_other_chips(x, y)):
                theirs = outs[k].at[2 * ox + oy, pl.ds((1 - c) * half, half)]
                pltpu.make_async_remote_copy(src_ref=theirs, dst_ref=theirs, send_sem=ssem.at[3 * k + j],
                                             recv_sem=rsem.at[3 * k + j], device_id=sibling,
                                             device_id_type=MESH).wait_recv()
        for cp in cps:
            cp.wait_send()

    return pl.pallas_call(
        body, name="pair_forward",
        in_specs=[ANY] * n, out_specs=[ANY] * n,
        out_shape=[jax.ShapeDtypeStruct(a.shape, a.dtype) for a in lands],
        scratch_shapes=[pltpu.SemaphoreType.DMA((3 * n,)), pltpu.SemaphoreType.DMA((3 * n,))],
        input_output_aliases={k: k for k in range(n)},
    )(*lands)


N_SOURCES = 7


def _reduce_peers(x, y, c):
    peers = []
    for ox, oy in _other_chips(x, y):
        for rel in range(2):
            peers.append(((ox, oy, c + rel - 2 * c * rel), 2 * ox + oy))
    peers.append(((x, y, 1 - c), 2 * x + y))
    return peers


def _reduce_start(arrs, slots):
    n = len(arrs)

    def body(*refs):
        ins, lands = refs[:n], refs[n:2 * n]
        ssem, rsem = refs[2 * n:2 * n + 2]
        token = refs[-1]
        x, y, c = _mesh_pos()
        me = 2 * (2 * x + y) + c
        for k in range(n):
            half = arrs[k].shape[1] // 2
            for p, (dev, ochip) in enumerate(_reduce_peers(x, y, c)):
                pltpu.make_async_remote_copy(
                    src_ref=ins[k].at[ochip, pl.ds(dev[2] * half, half)], dst_ref=lands[k].at[me],
                    send_sem=ssem.at[N_SOURCES * k + p], recv_sem=rsem.at[N_SOURCES * k + p],
                    device_id=dev, device_id_type=MESH).start()
        token[...] = jnp.zeros_like(token)

    out = pl.pallas_call(
        body, name="reduce_start",
        in_specs=[HBM] * (2 * n),
        out_specs=[SEM, SEM] + [HBM] * (2 * n) + [pl.BlockSpec(memory_space=pltpu.VMEM)],
        out_shape=[pltpu.SemaphoreType.DMA((N_SOURCES * n,)), pltpu.SemaphoreType.DMA((N_SOURCES * n,))]
        + [pltpu.HBM(a.shape, a.dtype) for a in arrs] + [pltpu.HBM(a.shape, a.dtype) for a in slots]
        + [jax.ShapeDtypeStruct((SUBLANE, LANE), F32)],
        input_output_aliases={k: 2 + k for k in range(2 * n)},
        compiler_params=pltpu.CompilerParams(has_side_effects=DATAFLOW),
    )(*[_in_hbm(a) for a in arrs], *[_in_hbm(a) for a in slots])
    return out[0], out[1], list(out[2:2 + n]), list(out[2 + n:2 + 2 * n]), out[-1]


def _reduce_wait(ssem, rsem, arrs, slots, after):
    n = len(arrs)

    def body(*refs):
        ins, lnd = refs[:n], refs[n:2 * n]
        ssem_ref, rsem_ref = refs[2 * n:2 * n + 2]
        x, y, c = _mesh_pos()
        for k in range(n):
            half = arrs[k].shape[1] // 2
            for p, (dev, ochip) in enumerate(_reduce_peers(x, y, c)):
                cp = pltpu.make_async_remote_copy(
                    src_ref=ins[k].at[ochip, pl.ds(dev[2] * half, half)], dst_ref=lnd[k].at[2 * ochip + dev[2]],
                    send_sem=ssem_ref.at[N_SOURCES * k + p], recv_sem=rsem_ref.at[N_SOURCES * k + p],
                    device_id=dev, device_id_type=MESH)
                cp.wait_send()
                cp.wait_recv()

    out = pl.pallas_call(
        body, name="reduce_wait",
        in_specs=[HBM] * (2 * n) + [SEM, SEM] + [ANY] * len(after),
        out_specs=[HBM] * (2 * n),
        out_shape=[pltpu.HBM(a.shape, a.dtype) for a in arrs] + [pltpu.HBM(a.shape, a.dtype) for a in slots],
        input_output_aliases={k: k for k in range(2 * n)},
        compiler_params=pltpu.CompilerParams(has_side_effects=DATAFLOW),
    )(*arrs, *slots, ssem, rsem, *after)
    return list(out[n:])


def _own_part(arrs, chip, core, me):
    n = len(arrs)
    nb = GRAD_ROW_BLOCKS

    def body(chip_ref, core_ref, me_ref, *refs):
        for k in range(n):
            refs[n + k][...] = refs[k][...]

    def blk(a):
        return (1, a.shape[1] // 2 // nb, a.shape[2])

    grid_spec = pltpu.PrefetchScalarGridSpec(
        num_scalar_prefetch=3, grid=(nb,),
        in_specs=[pl.BlockSpec(blk(a), lambda i, ch, co, me: (ch[0], co[0] * nb + i, 0)) for a in arrs],
        out_specs=[pl.BlockSpec(blk(a), lambda i, ch, co, me: (me[0], i, 0)) for a in arrs])
    return pl.pallas_call(
        body, name="own_part", grid_spec=grid_spec,
        out_shape=[jax.ShapeDtypeStruct((N_DEV, a.shape[1] // 2, a.shape[2]), a.dtype) for a in arrs],
        compiler_params=_cparams(1),
    )(chip, core, me, *arrs)


def _sum_devices(arrs, core):
    n = len(arrs)
    nb = GRAD_ROW_BLOCKS

    def body(c_ref, *refs):
        for k in range(n):
            r = refs[k]
            acc = r[0].astype(F32)
            for dev in range(1, N_DEV):
                acc = acc + r[dev].astype(F32)
            refs[n + k][...] = acc

    grid_spec = pltpu.PrefetchScalarGridSpec(
        num_scalar_prefetch=1, grid=(nb,),
        in_specs=[pl.BlockSpec((N_DEV, a.shape[1] // nb, a.shape[2]), lambda i, c: (0, i, 0)) for a in arrs],
        out_specs=[pl.BlockSpec((a.shape[1] // nb, a.shape[2]), lambda i, c: (c[0] * nb + i, 0)) for a in arrs])
    return pl.pallas_call(
        body, name="sum_devices", grid_spec=grid_spec,
        out_shape=[jax.ShapeDtypeStruct((2 * a.shape[1], a.shape[2]), F32) for a in arrs],
        compiler_params=_cparams(1),
    )(core, *arrs)


def _small_own(v, me):
    m = v.shape[0]

    def body(me_ref, v_ref, o_ref):
        o_ref[0] = v_ref[...]

    grid_spec = pltpu.PrefetchScalarGridSpec(
        num_scalar_prefetch=1, grid=(1,),
        in_specs=[pl.BlockSpec((m, LANE), lambda i, me: (0, 0))],
        out_specs=pl.BlockSpec((1, m, LANE), lambda i, me: (me[0], 0, 0)))
    return pl.pallas_call(
        body, name="small_own", grid_spec=grid_spec,
        out_shape=jax.ShapeDtypeStruct((N_DEV, m, LANE), v.dtype),
        compiler_params=_cparams(1),
    )(me, v)


def _small_start(v, slots):
    def body(v_ref, land, ssem, rsem, v_thru, land_thru, token):
        del v_thru, land_thru
        x, y, c = _mesh_pos()
        me = 2 * (2 * x + y) + c
        for p, (dev, _) in enumerate(_reduce_peers(x, y, c)):
            pltpu.make_async_remote_copy(src_ref=v_ref, dst_ref=land.at[me], send_sem=ssem.at[p],
                                         recv_sem=rsem.at[p], device_id=dev, device_id_type=MESH).start()
        token[...] = jnp.zeros_like(token)

    out = pl.pallas_call(
        body, name="small_start",
        in_specs=[HBM, HBM],
        out_specs=[SEM, SEM, HBM, HBM, pl.BlockSpec(memory_space=pltpu.VMEM)],
        out_shape=[pltpu.SemaphoreType.DMA((N_SOURCES,)), pltpu.SemaphoreType.DMA((N_SOURCES,)),
                   pltpu.HBM(v.shape, v.dtype), pltpu.HBM(slots.shape, slots.dtype),
                   jax.ShapeDtypeStruct((SUBLANE, LANE), F32)],
        input_output_aliases={0: 2, 1: 3},
        compiler_params=pltpu.CompilerParams(has_side_effects=DATAFLOW),
    )(_in_hbm(v), _in_hbm(slots))
    return out


def _small_wait(ssem, rsem, v, slots, after):
    def body(*refs):
        v_ref, land, ssem_ref, rsem_ref = refs[:4]
        x, y, c = _mesh_pos()
        for p, (dev, ochip) in enumerate(_reduce_peers(x, y, c)):
            cp = pltpu.make_async_remote_copy(src_ref=v_ref, dst_ref=land.at[2 * ochip + dev[2]],
                                              send_sem=ssem_ref.at[p], recv_sem=rsem_ref.at[p],
                                              device_id=dev, device_id_type=MESH)
            cp.wait_send()
            cp.wait_recv()

    out = pl.pallas_call(
        body, name="small_wait",
        in_specs=[HBM, HBM, SEM, SEM] + [ANY] * len(after),
        out_specs=[HBM, HBM],
        out_shape=[pltpu.HBM(v.shape, v.dtype), pltpu.HBM(slots.shape, slots.dtype)],
        input_output_aliases={0: 0, 1: 1},
        compiler_params=pltpu.CompilerParams(has_side_effects=DATAFLOW),
    )(v, slots, ssem, rsem, *after)
    return out[1]


GRAD_ROW_BLOCKS = 2


def _pair_allgather_halves(arrs):
    n = len(arrs)

    def body(*refs):
        outs = refs[n:2 * n]
        ssem, rsem = refs[2 * n:]
        x, y, c = _mesh_pos()
        cps = []
        for k in range(n):
            h = arrs[k].shape[0] // 2
            mine = outs[k].at[pl.ds(c * h, h)]
            cp = pltpu.make_async_remote_copy(src_ref=mine, dst_ref=mine, send_sem=ssem.at[k],
                                              recv_sem=rsem.at[k], device_id=(x, y, 1 - c), device_id_type=MESH)
            cp.start()
            cps.append(cp)
        for k, cp in enumerate(cps):
            h = arrs[k].shape[0] // 2
            theirs = outs[k].at[pl.ds((1 - c) * h, h)]
            pltpu.make_async_remote_copy(src_ref=theirs, dst_ref=theirs, send_sem=ssem.at[k], recv_sem=rsem.at[k],
                                         device_id=(x, y, 1 - c), device_id_type=MESH).wait_recv()
            cp.wait_send()

    return pl.pallas_call(
        body, name="pair_allgather_halves",
        in_specs=[ANY] * n, out_specs=[ANY] * n,
        out_shape=[jax.ShapeDtypeStruct(a.shape, a.dtype) for a in arrs],
        scratch_shapes=[pltpu.SemaphoreType.DMA((n,)), pltpu.SemaphoreType.DMA((n,))],
        input_output_aliases={k: k for k in range(n)},
    )(*arrs)


N_DEV = 8


def _adamw_math(w, g, m, v):
    m2 = ADAM_B1 * m + (1.0 - ADAM_B1) * g
    v2 = ADAM_B2 * v + (1.0 - ADAM_B2) * (g * g)
    m_hat = m2 / (1.0 - ADAM_B1 ** ADAM_STEP)
    v_hat = v2 / (1.0 - ADAM_B2 ** ADAM_STEP)
    delta = -ADAM_LR * (m_hat / (jnp.sqrt(v_hat) + ADAM_EPS) + ADAM_WD * w)
    return delta, m2, v2


def _adamw(w, m, v, gs, nblk):
    nl, r, n = w.shape
    assert nl == len(gs) and nl in (1, 2)
    br = r // nblk

    def body(w_ref, m_ref, v_ref, *rest):
        g_refs, (go_ref, d_ref, mo_ref, vo_ref) = rest[:nl], rest[nl:]
        g = g_refs[0][...]
        if nl == 2:
            g = jnp.where(pl.program_id(0) == 0, g, g_refs[1][...])
        delta, m2, v2 = _adamw_math(w_ref[0], g, m_ref[0], v_ref[0])
        go_ref[0] = g
        d_ref[0] = delta
        mo_ref[0] = m2
        vo_ref[0] = v2

    spec = pl.BlockSpec((1, br, n), lambda l, i: (l, i, 0))
    g_specs = [pl.BlockSpec((br, n), lambda l, i: (i, 0))] if nl == 1 else [
        pl.BlockSpec((br, n), lambda l, i: (jnp.where(l == 0, i, nblk - 1), 0)),
        pl.BlockSpec((br, n), lambda l, i: (jnp.where(l == 1, i, 0), 0))]
    return pl.pallas_call(
        body, name="adamw", grid=(nl, nblk),
        in_specs=[spec, spec, spec] + g_specs,
        out_specs=[spec] * 4,
        out_shape=[jax.ShapeDtypeStruct((nl, r, n), F32)] * 4,
        compiler_params=_cparams(2),
    )(w, m, v, *gs)


def _small_reduce_adamw(parts, w, m, v, rep_rows, sh_rows):
    mrows = rep_rows + N_SHARD * sh_rows + LOSS_ROWS

    def body(p_ref, w_ref, m_ref, v_ref, go_ref, d_ref, mo_ref, vo_ref, loss_ref):
        x, y, _ = _mesh_pos()
        mine = rep_rows + (2 * x + y) * sh_rows
        g_rep = p_ref[0:rep_rows, :]
        g_sh = p_ref[pl.ds(pl.multiple_of(mine, SUBLANE), sh_rows), :]
        loss = p_ref[mrows - LOSS_ROWS:mrows, :]
        for k in range(1, N_DEV):
            g_rep = g_rep + p_ref[k * mrows:k * mrows + rep_rows, :]
            g_sh = g_sh + p_ref[pl.ds(pl.multiple_of(k * mrows + mine, SUBLANE), sh_rows), :]
            loss = loss + p_ref[(k + 1) * mrows - LOSS_ROWS:(k + 1) * mrows, :]
        g = jnp.concatenate([g_rep, g_sh], axis=0)
        delta, m2, v2 = _adamw_math(w_ref[...], g, m_ref[...], v_ref[...])
        go_ref[...] = g
        d_ref[...] = delta
        mo_ref[...] = m2
        vo_ref[...] = v2
        loss_ref[...] = loss

    return pl.pallas_call(
        body, name="small_reduce_adamw",
        out_shape=[jax.ShapeDtypeStruct((rep_rows + sh_rows, 128), F32)] * 4
        + [jax.ShapeDtypeStruct((LOSS_ROWS, 128), F32)],
        compiler_params=pltpu.CompilerParams(vmem_limit_bytes=VMEM_LIMIT_MB * 1024 * 1024),
    )(parts, w, m, v)


LANE = 128
REP_SPEC = (("ffn1_norm", 16), ("mix_norm", 16), ("ffn2_norm", 16), ("final_norm", 8), ("pool_w", 128),
            ("pool_scale", 8), ("hgrn_lb_logits", 16), ("hgrn_gnorm", 8), ("lru_wa", 256), ("lru_wx", 256))
SH_SPEC = (("meta_tokens", 32), ("conv_w", 32), ("lru_conv_w", 8), ("conv_b", 8), ("conv_ln_g", 8),
           ("conv_ln_b", 8), ("lru_conv_b", 8), ("lru_ba", 8), ("lru_bx", 8), ("lru_lambda", 8))
REP_ROWS = sum(r for _, r in REP_SPEC)
SH_ROWS = sum(r for _, r in SH_SPEC)


def _pack_rows(vals, spec):
    parts = []
    for name, rows in spec:
        flat = vals[name].astype(F32).reshape(-1, LANE)
        if flat.shape[0] < rows:
            flat = jnp.concatenate([flat, jnp.zeros((rows - flat.shape[0], LANE), F32)], axis=0)
        parts.append(flat)
    return jnp.concatenate(parts, axis=0)


def _unpack_rows(packed, spec, shapes):
    out = {}
    off = 0
    for name, rows in spec:
        shp = shapes[name]
        n = int(np.prod(shp)) // LANE
        out[name] = packed[off:off + n].reshape(shp)
        off += rows
    return out


def _block_diag(blocks):
    n, b, _ = blocks.shape
    return sum(jnp.pad(blocks[g], ((g * b, (n - 1 - g) * b), (g * b, (n - 1 - g) * b))) for g in range(n))


def _diag_blocks(mat, n):
    b = mat.shape[0] // n
    return jnp.stack([mat[g * b:(g + 1) * b, g * b:(g + 1) * b] for g in range(n)])


BIG = ("ffn1_wg", "ffn1_wu", "ffn2_wg", "ffn2_wu", "ffn1_wd", "ffn2_wd", "w_in_even", "w_out_even",
       "w_in_odd", "w_out_odd")
WEIGHT_NAMES = ('meta_tokens', 'ffn1_norm', 'ffn1_wg', 'ffn1_wu', 'ffn1_wd', 'mix_norm', 'ffn2_norm', 'ffn2_wg',
                'ffn2_wu', 'ffn2_wd', 'w_in_even', 'pool_w', 'pool_scale', 'hgrn_lb_logits', 'hgrn_gnorm',
                'w_out_even', 'w_in_odd', 'conv_w', 'conv_b', 'conv_ln_g', 'conv_ln_b', 'lru_conv_w',
                'lru_conv_b', 'lru_wa', 'lru_ba', 'lru_wx', 'lru_bx', 'lru_lambda', 'w_out_odd', 'final_norm')


def _block_diag2(heads):
    nb = heads.shape[0] // 2
    return jnp.stack([_block_diag(heads[2 * j:2 * j + 2]) for j in range(nb)])


def _diag_blocks2(mats):
    return jnp.concatenate([_diag_blocks(mats[j], 2) for j in range(mats.shape[0])], axis=0)


GATHER_GROUPS = (
    (("small", 0),),
    (("ffn1_wg", 0), ("ffn1_wu", 0), ("ffn1_wd", 0)),
    (("w_in_even", 0), ("w_out_even", 0)),
    (("ffn2_wg", 0), ("ffn2_wu", 0), ("ffn2_wd", 0)),
    (("ffn1_wg", 1), ("ffn1_wu", 1), ("ffn1_wd", 1)),
    (("w_in_odd", 0), ("w_out_odd", 0)),
    (("ffn2_wg", 1), ("ffn2_wu", 1), ("ffn2_wd", 1)),
)
ADAM_ROW_BLOCKS = {"ffn1_wg": 2, "ffn1_wu": 2, "ffn2_wg": 2, "ffn2_wu": 2, "ffn1_wd": 2, "ffn2_wd": 2,
                   "w_in_even": 4, "w_out_even": 2, "w_in_odd": 4, "w_out_odd": 2}
TRANSPOSED = ("ffn1_wg", "ffn1_wu", "ffn2_wg", "ffn2_wu", "w_in_even")
SCATTER_DEPTH = 2
LOSS_ROWS = 8
GATHER_SPLIT = (1, 4)


def _unpack_small(sm, shapes):
    per_shard = [_unpack_rows(sm[s], SH_SPEC, shapes) for s in range(N_SHARD)]
    full = {}
    for n, _ in SH_SPEC:
        full[n] = jnp.concatenate([per_shard[s][n].reshape(-1, shapes[n][-1]) for s in range(N_SHARD)], axis=-1)
    full["conv_w"] = jnp.concatenate([full["conv_w"], jnp.zeros((CONV_HALO - CONV_WIDTH, D_CONV), F32)], axis=0)
    return full


def _local_step(x, tgt, w, shapes, fetch, emit, emit_small):
    s_len, d = x.shape
    t_real = s_len + N_META
    tp = -(-t_real // ROW_ALIGN) * ROW_ALIGN
    tm = _tile(tp, 832, ROW_ALIGN)
    tm_small = _tile(tp, 832, 16)
    tr = _tile(tp, 416, SUBLANE)
    tm_wgrad = _tile(tp, 1040, 16)

    def gain(name, layer):
        return w[name][layer:layer + 1]

    pool_wbd = _block_diag(w["pool_w"][0]).astype(BF16)
    pool_scale = w["pool_scale"]
    wa_bd = _block_diag2(w["lru_wa"][0]).astype(BF16)
    wx_bd = _block_diag2(w["lru_wx"][0]).astype(BF16)
    mst, msk, n_lev = _hgrn_consts(HG_CHUNK)

    (sm,) = fetch(0, None)
    sf = _unpack_small(sm, shapes)
    h0 = jnp.concatenate([sf["meta_tokens"], x, jnp.zeros((tp - t_real, d), F32)], axis=0)
    tgt_pad = jnp.concatenate([jnp.zeros((N_META, d), F32), tgt, jnp.zeros((tp - t_real, d), F32)], axis=0)
    f1l0 = fetch(1, h0)
    h1, *s1 = _ffn_fwd(h0, gain("ffn1_norm", 0), *f1l0, tm)
    w_in_even4, w_out_even4 = fetch(2, h1)
    w_out_even = w_out_even4.reshape(d, d)
    even_piece = [(w_in_even4.reshape(D_IN_EVEN, d), (D_IN_EVEN, d), (0, 0))]
    p0, nm0 = _proj_fwd(h1, gain("mix_norm", 0), even_piece, tm_small, wt=True)
    ya = _pool_fwd(p0, pool_wbd, pool_scale, tr)
    yb, states = _hgrn_fwd(p0, w["hgrn_lb_logits"], w["hgrn_gnorm"], mst, msk, n_lev, tm)
    h2 = _out_fwd(h1, ya, yb, w_out_even, tm)
    f2l0 = fetch(3, h2)
    h3, *s2 = _ffn_fwd(h2, gain("ffn2_norm", 0), *f2l0, tm)
    f1l1 = fetch(4, h3)
    h4, *s3 = _ffn_fwd(h3, gain("ffn1_norm", 1), *f1l1, tm)
    w_in_odd4, w_out_odd4 = fetch(5, h4)
    w_out_odd = w_out_odd4.reshape(d, d)
    odd_pieces = [(w_in_odd4, (1, d, D_IN_ODD // N_SHARD), (k, 0, 0)) for k in range(N_SHARD)]
    p1, nm1 = _proj_fwd(h4, gain("mix_norm", 1), odd_pieces, tm_small)
    yc, conv_out = _convmod_fwd(p1, sf["conv_w"], sf["conv_b"], sf["conv_ln_g"], sf["conv_ln_b"], tr)
    lru_args = (sf["lru_conv_w"], sf["lru_conv_b"], wa_bd, sf["lru_ba"], wx_bd, sf["lru_bx"], sf["lru_lambda"])
    yd, hs = _lru_fwd(p1, *lru_args)
    h5 = _out_fwd(h4, yc, yd, w_out_odd, tm)
    f2l1 = fetch(6, h5)
    h6, *s4 = _ffn_fwd(h5, gain("ffn2_norm", 1), *f2l1, tm)
    loss, dh6, dg_final = _loss_bwd(h6, w["final_norm"].reshape(1, d), tgt_pad, t_real, tm)

    def ffn_bwd(dho, h, saved, norm, wts, after=()):
        ga, gb, sa, n = saved
        dh, da, db, dg, dy = _ffn_bwd_act(dho, h, norm, ga, gb, *wts, tm, after)
        return dh, dg, _ffn_bwd_w([(da, n, None), (db, n, None), (sa, dy, None)], tm_wgrad)

    dh5, dg_f2_l1, g = ffn_bwd(dh6, h5, s4, gain("ffn2_norm", 1), f2l1)
    sent = emit((("ffn2_wg", 1), ("ffn2_wu", 1), ("ffn2_wd", 1)), g)
    dyc, dyd, dw_out_odd = _out_bwd(dh5, yc, yd, w_out_odd, tm, tuple(sent))
    dca, dcb, dconv_w, dconv_vec = _convmod_bwd(p1, conv_out, dyc, sf["conv_w"], sf["conv_ln_g"],
                                                sf["conv_ln_b"], tr)
    dlx, dlg, dwa_bd, dwx_bd, dlru_vec = _lru_bwd(p1, hs, dyd, *lru_args)
    dp1 = [dca, dcb, dlx, dlg]
    dh4, dg_mix_l1 = _proj_bwd_act(dh5, h4, gain("mix_norm", 1), dp1, odd_pieces, tm_small)
    dw_in_odd = jnp.stack(_proj_bwd_w(nm1, dp1, tm))
    dh3, dg_f1_l1, g = ffn_bwd(dh4, h3, s3, gain("ffn1_norm", 1), f1l1)
    sent = emit((("w_out_odd", 0), ("w_in_odd", 0), ("ffn1_wg", 1), ("ffn1_wu", 1), ("ffn1_wd", 1)),
                [dw_out_odd.reshape(N_SHARD, d // N_SHARD, d), dw_in_odd] + list(g))
    dh2, dg_f2_l0, g = ffn_bwd(dh3, h2, s2, gain("ffn2_norm", 0), f2l0, tuple(sent))
    sent = emit((("ffn2_wg", 0), ("ffn2_wu", 0), ("ffn2_wd", 0)), g)
    dya, dyb, dw_out_even = _out_bwd(dh2, ya, yb, w_out_even, tm, tuple(sent))
    dpool, dpool_wbd, dpool_scale = _pool_bwd(p0, dya, pool_wbd, pool_scale, tr)
    dq, dz, dv, dgate, dlb_logits, dgn_heads = _hgrn_bwd(p0, dyb, states, w["hgrn_lb_logits"], w["hgrn_gnorm"],
                                                         mst, msk, n_lev, tm)
    dp0 = [dpool, dq, dz, dv, dgate]
    dh1, dg_mix_l0 = _proj_bwd_act(dh2, h1, gain("mix_norm", 0), dp0, even_piece, tm_small, wt=True)
    dw_in_even_t = jnp.concatenate(_proj_bwd_w(nm0, dp0, tm_small, wt=True), axis=0)
    ga, gb, sa, n1 = s1
    (dwd_f1l0,) = _ffn_bwd_w([(sa, dh1, 0.5)], tm_wgrad)
    sent = emit((("w_out_even", 0), ("w_in_even", 0), ("ffn1_wd", 0)),
                [dw_out_even.reshape(N_SHARD, d // N_SHARD, d),
                 dw_in_even_t.reshape(N_SHARD, D_IN_EVEN // N_SHARD, d), dwd_f1l0])
    dh0, da, db, dg_f1_l0, _ = _ffn_bwd_act(dh1, h0, gain("ffn1_norm", 0), ga, gb, *f1l0, tm, tuple(sent))

    grad_x = dh0[N_META:t_real]
    rep = {
        "ffn1_norm": jnp.concatenate([dg_f1_l0, dg_f1_l1], axis=0),
        "mix_norm": jnp.concatenate([dg_mix_l0, dg_mix_l1], axis=0),
        "ffn2_norm": jnp.concatenate([dg_f2_l0, dg_f2_l1], axis=0),
        "final_norm": dg_final,
        "pool_w": _diag_blocks(dpool_wbd, len(POOL_WINDOWS)),
        "pool_scale": dpool_scale,
        "hgrn_lb_logits": dlb_logits,
        "hgrn_gnorm": jnp.sum(dgn_heads, axis=0),
        "lru_wa": _diag_blocks2(dwa_bd),
        "lru_wx": _diag_blocks2(dwx_bd),
    }
    dmeta = jnp.transpose(dh0[:N_META].reshape(N_META, N_SHARD, 2, LANE), (1, 0, 2, 3)).reshape(N_SHARD, 32, LANE)
    packs = [_pack_rows(rep, REP_SPEC)]
    for s in range(N_SHARD):
        sh = {
            "meta_tokens": dmeta[s], "conv_w": dconv_w[s], "lru_conv_w": dlru_vec[s, 0:4],
            "conv_b": dconv_vec[s, 0:1], "conv_ln_g": dconv_vec[s, 1:2], "conv_ln_b": dconv_vec[s, 2:3],
            "lru_conv_b": dlru_vec[s, 4:5], "lru_ba": dlru_vec[s, 5:6], "lru_bx": dlru_vec[s, 6:7],
            "lru_lambda": dlru_vec[s, 7:8],
        }
        packs.append(_pack_rows(sh, SH_SPEC))
    packs.append(jnp.pad(loss, ((0, LOSS_ROWS - 1), (0, LANE - 1))))
    sent = emit_small(jnp.concatenate(packs, axis=0))
    emit((("ffn1_wg", 0), ("ffn1_wu", 0)), _ffn_bwd_w([(da, n1, None), (db, n1, None)], tm_wgrad, tuple(sent)))
    return grad_x


def kernel(x, meta_tokens, ffn1_norm, ffn1_wg, ffn1_wu, ffn1_wd, mix_norm, ffn2_norm, ffn2_wg, ffn2_wu, ffn2_wd, w_in_even, pool_w, pool_scale, hgrn_lb_logits, hgrn_gnorm, w_out_even, w_in_odd, conv_w, conv_b, conv_ln_g, conv_ln_b, lru_conv_w, lru_conv_b, lru_wa, lru_ba, lru_wx, lru_bx, lru_lambda, w_out_odd, final_norm, loss_target, m_meta_tokens, m_ffn1_norm, m_ffn1_wg, m_ffn1_wu, m_ffn1_wd, m_mix_norm, m_ffn2_norm, m_ffn2_wg, m_ffn2_wu, m_ffn2_wd, m_w_in_even, m_pool_w, m_pool_scale, m_hgrn_lb_logits, m_hgrn_gnorm, m_w_out_even, m_w_in_odd, m_conv_w, m_conv_b, m_conv_ln_g, m_conv_ln_b, m_lru_conv_w, m_lru_conv_b, m_lru_wa, m_lru_ba, m_lru_wx, m_lru_bx, m_lru_lambda, m_w_out_odd, m_final_norm, v_meta_tokens, v_ffn1_norm, v_ffn1_wg, v_ffn1_wu, v_ffn1_wd, v_mix_norm, v_ffn2_norm, v_ffn2_wg, v_ffn2_wu, v_ffn2_wd, v_w_in_even, v_pool_w, v_pool_scale, v_hgrn_lb_logits, v_hgrn_gnorm, v_w_out_even, v_w_in_odd, v_conv_w, v_conv_b, v_conv_ln_g, v_conv_ln_b, v_lru_conv_w, v_lru_conv_b, v_lru_wa, v_lru_ba, v_lru_wx, v_lru_bx, v_lru_lambda, v_w_out_odd, v_final_norm):
    args = locals()
    w = {n: args[n] for n in WEIGHT_NAMES}
    m = {n: args["m_" + n] for n in WEIGHT_NAMES}
    v = {n: args["v_" + n] for n in WEIGHT_NAMES}
    shapes = {n: w[n].shape for n in WEIGHT_NAMES}
    core = lax.axis_index("c").astype(jnp.int32).reshape(1)
    chip = (2 * lax.axis_index("x") + lax.axis_index("y")).astype(jnp.int32).reshape(1)
    me = 2 * chip + core

    def view(a, n):
        return jnp.swapaxes(a, 1, 2) if n in TRANSPOSED else a

    wv, mv, vv = [{n: view(src[n], n) for n in BIG} for src in (w, m, v)]

    def shard(key):
        n, l = key
        return _pack_rows(w, SH_SPEC) if n == "small" else wv[n][l].astype(BF16)

    started = {}
    for groups in (GATHER_GROUPS[:2], GATHER_GROUPS[2:]):
        gkeys = [key for grp in groups for key in grp]
        ssem, rsem, srcs, lands, token = _gather_start([shard(key) for key in gkeys],
                                                       [any(key in GATHER_GROUPS[g] for g in GATHER_SPLIT)
                                                        for key in gkeys])
        for k, key in enumerate(gkeys):
            started[key] = (ssem, rsem, srcs[k], lands[k], k, token)

    def pack_small(src):
        return jnp.concatenate([_pack_rows(src, REP_SPEC), _pack_rows(src, SH_SPEC)], axis=0)

    small_packs = [pack_small(src) for src in (w, m, v)]

    def fetch(group, after):
        st = [started[key] for key in GATHER_GROUPS[group]]
        deps = (st[0][5],) if after is None else (after,)
        if group == 1:
            deps += (started[GATHER_GROUPS[2][0]][5],) + tuple(small_packs)
        split = group in GATHER_SPLIT
        got = _gather_wait(st[0][0], st[0][1], [s[2] for s in st], [s[3] for s in st], [s[4] for s in st], deps,
                           split)
        return _pair_forward(got) if split else got

    in_flight, reduced = [], {}

    def collect(entry, after):
        gkeys, gs_sem, gr_sem, grads_thru, slots_thru, _ = entry
        slots = _reduce_wait(gs_sem, gr_sem, grads_thru, slots_thru, after)
        full = _pair_allgather_halves(_sum_devices(slots, core))
        reduced.update(zip(gkeys, full))
        return full[0]

    def emit(gkeys, grads):
        grads = list(grads)
        in_flight.append((gkeys,) + tuple(_reduce_start(grads, _own_part(grads, chip, core, me))))
        token = in_flight[-1][-1]
        if len(in_flight) > SCATTER_DEPTH:
            return token, collect(in_flight[-1 - SCATTER_DEPTH], (token,))
        return (token,)

    small_flight = []

    def emit_small(part):
        small_flight.append(_small_start(part, _small_own(part, me)))
        return (small_flight[0][4],)

    grad_x = _local_step(x[0], loss_target[0], w, shapes, fetch, emit, emit_small)

    out_g, out_d, out_m, out_v = {}, {}, {}, {}
    deps = (in_flight[-1][-1],)

    def adamw_ready():
        done = ()
        for n in BIG:
            layers = range(shapes[n][0])
            if n not in out_g and all((n, l) in reduced for l in layers):
                res = _adamw(wv[n], mv[n], vv[n], [reduced[(n, l)] for l in layers], ADAM_ROW_BLOCKS[n])
                out_g[n], out_d[n], out_m[n], out_v[n] = [view(r, n) for r in res]
                done += (res[1],)
        return done

    deps += adamw_ready()
    for entry in in_flight[-SCATTER_DEPTH:-1]:
        collect(entry, deps)
        deps += adamw_ready()
    s_ssem, s_rsem, s_part, s_slots, _ = small_flight[0]
    small_all = _small_wait(s_ssem, s_rsem, s_part, s_slots, deps)
    small_res = _small_reduce_adamw(small_all.reshape(-1, LANE), *small_packs, REP_ROWS, SH_ROWS)
    collect(in_flight[-1], deps + (small_res[0],))
    adamw_ready()

    loss = small_res[4][0, 0]
    for dst, packed in zip((out_g, out_d, out_m, out_v), small_res[:4]):
        dst.update(_unpack_rows(packed[:REP_ROWS], REP_SPEC, shapes))
        dst.update(_unpack_rows(packed[REP_ROWS:], SH_SPEC, shapes))

    return (loss, grad_x[None], *[out_g[n] for n in WEIGHT_NAMES], *[out_d[n] for n in WEIGHT_NAMES],
            *[out_m[n] for n in WEIGHT_NAMES], *[out_v[n] for n in WEIGHT_NAMES])
```

```python
import functools

import numpy as np
import jax
import jax.numpy as jnp
from jax import lax
from jax.experimental import pallas as pl
from jax.experimental.pallas import tpu as pltpu

F32 = jnp.float32
BF16 = jnp.bfloat16
MESH = pl.DeviceIdType.MESH

EPS = 1e-6
N_META = 16
D_FF = 2816
N_SHARD = 4
FF_SHARD = D_FF // N_SHARD
D_POOL = 256
POOL_GROUP = 64
POOL_WINDOWS = (2, 4, 8, 16)
D_HGRN = 768
HG_HEADS = 6
HEAD = 128
HG_CHUNK = 64
HG_HEADS_PER_STEP = 6
FFN_ROW_BLOCKS = 2
HG_PBLOCK = 256
D_IN_EVEN = D_POOL + 4 * D_HGRN
D_CONV = 512
CONV_WIDTH = 31
CONV_HALO = 32
D_LRU = 512
LRU_CONV = 4
LRU_HALO = 8
LRU_C = 8.0
D_IN_ODD = 2 * D_CONV + 2 * D_LRU
SUBLANE = 8
MXU_DEPTH = 256
ROW_ALIGN = 64

ADAM_LR = 0.001
ADAM_B1 = 0.9
ADAM_B2 = 0.999
ADAM_EPS = 1e-08
ADAM_WD = 0.01
ADAM_STEP = 10

VMEM_LIMIT_MB = 56


def _cparams(n_grid_axes=0, vmem_mb=VMEM_LIMIT_MB):
    sem = ("arbitrary",) * n_grid_axes if n_grid_axes else None
    return pltpu.CompilerParams(dimension_semantics=sem, vmem_limit_bytes=vmem_mb * 1024 * 1024)


def _tile(n, target, mult):
    best = None
    for t in range(mult, min(n, target) + 1, mult):
        if n % t == 0:
            best = t
    assert best is not None, (n, target, mult)
    return best


def _dot(a, b):
    return jnp.dot(a, b, preferred_element_type=F32)


def _dot_nt(a, b):
    return lax.dot_general(a, b, (((1,), (1,)), ((), ())), preferred_element_type=F32)


def _dot_tn(a, b):
    return lax.dot_general(a, b, (((0,), (0,)), ((), ())), preferred_element_type=F32)


def _sigmoid(x):
    return 1.0 / (1.0 + jnp.exp(-x))


def _colsum(x):
    return jnp.sum(x, axis=0, keepdims=True)


def _rms_stats(h):
    rstd = lax.rsqrt(jnp.mean(h * h, axis=-1, keepdims=True) + EPS)
    return rstd, h * rstd


def _rms_bwd(dn, g, rstd, xhat):
    dng = dn * g
    dh = rstd * (dng - xhat * jnp.mean(dng * xhat, axis=-1, keepdims=True))
    return dh, _colsum(dn * xhat)


def _ffn_fwd(h, norm, wg4, wu4, wd4, tm):
    tp, d = h.shape
    nt = tp // tm

    def body(h_ref, g_ref, wg_ref, wu_ref, wd_ref, ho_ref, ga_ref, gb_ref, sa_ref, n_ref, n_sc, acc):
        s = pl.program_id(1)

        @pl.when(s == 0)
        def _():
            hh = h_ref[...]
            rstd, xhat = _rms_stats(hh)
            n = (xhat * g_ref[...]).astype(BF16)
            n_sc[...] = n
            n_ref[...] = n
            acc[...] = jnp.zeros_like(acc)

        n = n_sc[...]
        a = _dot_nt(n, wg_ref[0])
        b = _dot_nt(n, wu_ref[0])
        sig = _sigmoid(a)
        sil = a * sig
        ga_ref[0] = (sig * (1.0 + a * (1.0 - sig)) * b).astype(BF16)
        gb_ref[0] = sil.astype(BF16)
        sg = (sil * b).astype(BF16)
        sa_ref[0] = sg
        acc[...] += _dot(sg, wd_ref[0])

        @pl.when(s == N_SHARD - 1)
        def _():
            ho_ref[...] = h_ref[...] + 0.5 * acc[...]

    return pl.pallas_call(
        body, name="ffn_fwd",
        grid=(nt, N_SHARD),
        in_specs=[
            pl.BlockSpec((tm, d), lambda i, s: (i, 0)),
            pl.BlockSpec((1, d), lambda i, s: (0, 0)),
            pl.BlockSpec((1, FF_SHARD, d), lambda i, s: (s, 0, 0)),
            pl.BlockSpec((1, FF_SHARD, d), lambda i, s: (s, 0, 0)),
            pl.BlockSpec((1, FF_SHARD, d), lambda i, s: (s, 0, 0)),
        ],
        out_specs=[
            pl.BlockSpec((tm, d), lambda i, s: (i, 0)),
            pl.BlockSpec((1, tm, FF_SHARD), lambda i, s: (s, i, 0)),
            pl.BlockSpec((1, tm, FF_SHARD), lambda i, s: (s, i, 0)),
            pl.BlockSpec((1, tm, FF_SHARD), lambda i, s: (s, i, 0)),
            pl.BlockSpec((tm, d), lambda i, s: (i, 0)),
        ],
        out_shape=[
            jax.ShapeDtypeStruct((tp, d), F32),
            jax.ShapeDtypeStruct((N_SHARD, tp, FF_SHARD), BF16),
            jax.ShapeDtypeStruct((N_SHARD, tp, FF_SHARD), BF16),
            jax.ShapeDtypeStruct((N_SHARD, tp, FF_SHARD), BF16),
            jax.ShapeDtypeStruct((tp, d), BF16),
        ],
        scratch_shapes=[pltpu.VMEM((tm, d), BF16), pltpu.VMEM((tm, d), F32)],
        compiler_params=_cparams(2),
    )(h, norm, wg4, wu4, wd4)


def _ffn_bwd_act(dho, h, norm, ga4, gb4, wg4, wu4, wd4, tm, after=()):
    tp, d = h.shape
    nt = tp // tm

    def body(dho_ref, h_ref, g_ref, ga_ref, gb_ref, wg_ref, wu_ref, wd_ref, *rest):
        dh_ref, da_ref, db_ref, dg_ref, dy_ref, dn_sc = rest[len(after):]
        i = pl.program_id(0)
        s = pl.program_id(1)

        @pl.when(s == 0)
        def _():
            dy_ref[...] = (0.5 * dho_ref[...]).astype(BF16)
            dn_sc[...] = jnp.zeros_like(dn_sc)

        @pl.when((s == 0) & (i == 0))
        def _():
            dg_ref[...] = jnp.zeros_like(dg_ref)

        hb = tm // FFN_ROW_BLOCKS
        rows = [pl.ds(k * hb, hb) for k in range(FFN_ROW_BLOCKS)]
        ds = _dot_nt(dy_ref[rows[0], :], wd_ref[0])
        for k in range(FFN_ROW_BLOCKS):
            ds_next = _dot_nt(dy_ref[rows[k + 1], :], wd_ref[0]) if k + 1 < FFN_ROW_BLOCKS else None
            da = (ds * ga_ref[0, rows[k], :].astype(F32)).astype(BF16)
            db = (ds * gb_ref[0, rows[k], :].astype(F32)).astype(BF16)
            da_ref[0, rows[k], :] = da
            db_ref[0, rows[k], :] = db
            dn_sc[rows[k], :] += _dot(da, wg_ref[0]) + _dot(db, wu_ref[0])
            ds = ds_next

        @pl.when(s == N_SHARD - 1)
        def _():
            rstd, xhat = _rms_stats(h_ref[...])
            dh, dg = _rms_bwd(dn_sc[...], g_ref[...], rstd, xhat)
            dh_ref[...] = dho_ref[...] + dh
            dg_ref[...] += dg

    return pl.pallas_call(
        body, name="ffn_bwd_act",
        grid=(nt, N_SHARD),
        in_specs=[
            pl.BlockSpec((tm, d), lambda i, s: (i, 0)),
            pl.BlockSpec((tm, d), lambda i, s: (i, 0)),
            pl.BlockSpec((1, d), lambda i, s: (0, 0)),
            pl.BlockSpec((1, tm, FF_SHARD), lambda i, s: (s, i, 0)),
            pl.BlockSpec((1, tm, FF_SHARD), lambda i, s: (s, i, 0)),
            pl.BlockSpec((1, FF_SHARD, d), lambda i, s: (s, 0, 0)),
            pl.BlockSpec((1, FF_SHARD, d), lambda i, s: (s, 0, 0)),
            pl.BlockSpec((1, FF_SHARD, d), lambda i, s: (s, 0, 0)),
        ] + [pl.BlockSpec(memory_space=pl.ANY)] * len(after),
        out_specs=[
            pl.BlockSpec((tm, d), lambda i, s: (i, 0)),
            pl.BlockSpec((1, tm, FF_SHARD), lambda i, s: (s, i, 0)),
            pl.BlockSpec((1, tm, FF_SHARD), lambda i, s: (s, i, 0)),
            pl.BlockSpec((1, d), lambda i, s: (0, 0)),
            pl.BlockSpec((tm, d), lambda i, s: (i, 0)),
        ],
        out_shape=[
            jax.ShapeDtypeStruct((tp, d), F32),
            jax.ShapeDtypeStruct((N_SHARD, tp, FF_SHARD), BF16),
            jax.ShapeDtypeStruct((N_SHARD, tp, FF_SHARD), BF16),
            jax.ShapeDtypeStruct((1, d), F32),
            jax.ShapeDtypeStruct((tp, d), BF16),
        ],
        scratch_shapes=[pltpu.VMEM((tm, d), F32)],
        compiler_params=_cparams(2),
    )(dho, h, norm, ga4, gb4, wg4, wu4, wd4, *after)


def _wgrad_tiles(tp, target):
    tm = min(target, tp) // MXU_DEPTH * MXU_DEPTH
    nt = tp // tm
    tail = tp - nt * tm
    assert tail == 0 or (nt * tm) % tail == 0 and tail % 16 == 0, (tp, tm, tail)
    return tm, nt, tail


def _ffn_bwd_w(pairs, tiles, after=()):
    npair = len(pairs)
    tp, d = pairs[0][1].shape
    tm, nt, tail = tiles
    rhs_list = []
    for _, rhs, _ in pairs:
        if all(rhs is not r for r in rhs_list):
            rhs_list.append(rhs)
    rhs_of = [[rhs is r for r in rhs_list].index(True) for _, rhs, _ in pairs]
    nrhs = len(rhs_list)
    nin = nrhs + npair
    ntail = nin if tail else 0

    def body(*refs):
        rhs_refs = refs[:nrhs]
        lhs_refs = refs[nrhs:nin]
        rhs_tails = refs[nin:nin + ntail][:nrhs]
        lhs_tails = refs[nin:nin + ntail][nrhs:]
        rest = refs[nin + ntail + len(after):]
        out_refs, accs = rest[:npair], rest[npair:]
        i = pl.program_id(1)

        @pl.when(i == 0)
        def _():
            for acc in accs:
                acc[...] = jnp.zeros_like(acc)

        def accumulate(lhs, rhs):
            for k, (_, _, scale) in enumerate(pairs):
                r = rhs[rhs_of[k]][...]
                if scale is not None:
                    r = (scale * r).astype(BF16)
                accs[k][...] += _dot_tn(lhs[k][0], r)

        accumulate(lhs_refs, rhs_refs)

        @pl.when(i == nt - 1)
        def _():
            if tail:
                accumulate(lhs_tails, rhs_tails)
            for k in range(npair):
                out_refs[k][0] = accs[k][...].astype(BF16)

    tail_specs, tail_args = [], []
    if tail:
        tb = nt * tm // tail
        tail_specs = ([pl.BlockSpec((tail, d), lambda s, i: (tb, 0))] * nrhs
                      + [pl.BlockSpec((1, tail, FF_SHARD), lambda s, i: (s, tb, 0))] * npair)
        tail_args = [*rhs_list, *[lhs for lhs, _, _ in pairs]]
    return pl.pallas_call(
        body, name="ffn_bwd_w",
        grid=(N_SHARD, nt),
        in_specs=[pl.BlockSpec((tm, d), lambda s, i: (i, 0))] * nrhs
        + [pl.BlockSpec((1, tm, FF_SHARD), lambda s, i: (s, i, 0))] * npair
        + tail_specs
        + [pl.BlockSpec(memory_space=pl.ANY)] * len(after),
        out_specs=[pl.BlockSpec((1, FF_SHARD, d), lambda s, i: (s, 0, 0))] * npair,
        out_shape=[jax.ShapeDtypeStruct((N_SHARD, FF_SHARD, d), BF16)] * npair,
        scratch_shapes=[pltpu.VMEM((FF_SHARD, d), F32)] * npair,
        compiler_params=_cparams(2),
    )(*rhs_list, *[lhs for lhs, _, _ in pairs], *tail_args, *after)


def _proj_fwd(h, norm, w_pieces, tm, wt=False):
    tp, d = h.shape
    widths = [bs[-2] if wt else bs[-1] for _, bs, _ in w_pieces]
    ntot = sum(widths)
    npc = len(w_pieces)

    def body(*refs):
        h_ref, g_ref = refs[:2]
        w_refs = refs[2:2 + npc]
        p_ref, n_ref = refs[2 + npc:]
        rstd, xhat = _rms_stats(h_ref[...])
        n = (xhat * g_ref[...]).astype(BF16)
        n_ref[...] = n
        off = 0
        for k in range(npc):
            w = w_refs[k][...]
            w = w.reshape(w.shape[-2], w.shape[-1])
            p_ref[:, off:off + widths[k]] = _dot_nt(n, w) if wt else _dot(n, w)
            off += widths[k]

    in_specs = [pl.BlockSpec((tm, d), lambda i: (i, 0)), pl.BlockSpec((1, d), lambda i: (0, 0))]
    for _, bs, idx in w_pieces:
        in_specs.append(pl.BlockSpec(bs, functools.partial(lambda i, idx: idx, idx=idx)))
    return pl.pallas_call(
        body, name="proj_fwd",
        grid=(tp // tm,),
        in_specs=in_specs,
        out_specs=[pl.BlockSpec((tm, ntot), lambda i: (i, 0)), pl.BlockSpec((tm, d), lambda i: (i, 0))],
        out_shape=[jax.ShapeDtypeStruct((tp, ntot), F32), jax.ShapeDtypeStruct((tp, d), BF16)],
        compiler_params=_cparams(1),
    )(h, norm, *[w for w, _, _ in w_pieces])


def _proj_bwd_act(dres, h, norm, dp_pieces, w_pieces, tm, wt=False):
    tp, d = h.shape
    npc = len(dp_pieces)
    nw = len(w_pieces)
    assert nw == npc or (nw == 1 and wt)

    def body(*refs):
        dres_ref, h_ref, g_ref = refs[:3]
        dp_refs = refs[3:3 + npc]
        w_refs = refs[3 + npc:3 + npc + nw]
        dh_ref, dg_ref = refs[3 + npc + nw:]
        i = pl.program_id(0)

        @pl.when(i == 0)
        def _():
            dg_ref[...] = jnp.zeros_like(dg_ref)

        dn = None
        off = 0
        for k in range(npc):
            if nw == npc:
                w = w_refs[k][...]
                w = w.reshape(w.shape[-2], w.shape[-1])
            else:
                w = w_refs[0][off:off + dp_pieces[k].shape[1], :]
                off += dp_pieces[k].shape[1]
            t = _dot(dp_refs[k][...], w) if wt else _dot_nt(dp_refs[k][...], w)
            dn = t if dn is None else dn + t
        rstd, xhat = _rms_stats(h_ref[...])
        dh, dg = _rms_bwd(dn, g_ref[...], rstd, xhat)
        dh_ref[...] = dres_ref[...] + dh
        dg_ref[...] += dg

    in_specs = [pl.BlockSpec((tm, d), lambda i: (i, 0)), pl.BlockSpec((tm, d), lambda i: (i, 0)),
                pl.BlockSpec((1, d), lambda i: (0, 0))]
    for dp in dp_pieces:
        in_specs.append(pl.BlockSpec((tm, dp.shape[1]), lambda i: (i, 0)))
    for _, bs, idx in w_pieces:
        in_specs.append(pl.BlockSpec(bs, functools.partial(lambda i, idx: idx, idx=idx)))
    return pl.pallas_call(
        body, name="proj_bwd_act",
        grid=(tp // tm,),
        in_specs=in_specs,
        out_specs=[pl.BlockSpec((tm, d), lambda i: (i, 0)), pl.BlockSpec((1, d), lambda i: (0, 0))],
        out_shape=[jax.ShapeDtypeStruct((tp, d), F32), jax.ShapeDtypeStruct((1, d), F32)],
        compiler_params=_cparams(1),
    )(dres, h, norm, *dp_pieces, *[w for w, _, _ in w_pieces])


def _proj_bwd_w(n, dp_pieces, tm, wt=False):
    tp, d = n.shape
    npc = len(dp_pieces)
    widths = [dp.shape[1] for dp in dp_pieces]
    oshape = (lambda w: (w, d)) if wt else (lambda w: (d, w))

    def body(*refs):
        n_ref = refs[0]
        dp_refs = refs[1:1 + npc]
        o_refs = refs[1 + npc:1 + 2 * npc]
        accs = refs[1 + 2 * npc:]
        i = pl.program_id(0)

        @pl.when(i == 0)
        def _():
            for acc in accs:
                acc[...] = jnp.zeros_like(acc)

        nn = n_ref[...]
        for k in range(npc):
            accs[k][...] += _dot_tn(dp_refs[k][...], nn) if wt else _dot_tn(nn, dp_refs[k][...])

        @pl.when(i == pl.num_programs(0) - 1)
        def _():
            for k in range(npc):
                o_refs[k][...] = accs[k][...].astype(BF16)

    return pl.pallas_call(
        body, name="proj_bwd_w",
        grid=(tp // tm,),
        in_specs=[pl.BlockSpec((tm, d), lambda i: (i, 0))]
        + [pl.BlockSpec((tm, w), lambda i: (i, 0)) for w in widths],
        out_specs=[pl.BlockSpec(oshape(w), lambda i: (0, 0)) for w in widths],
        out_shape=[jax.ShapeDtypeStruct(oshape(w), BF16) for w in widths],
        scratch_shapes=[pltpu.VMEM(oshape(w), F32) for w in widths],
        compiler_params=_cparams(1),
    )(n, *dp_pieces)


def _out_fwd(h, ya, yb, w, tm):
    tp, d = h.shape
    na, nb = ya.shape[1], yb.shape[1]

    def body(h_ref, ya_ref, yb_ref, w_ref, o_ref):
        y = _dot(ya_ref[...].astype(BF16), w_ref[0:na, :]) + _dot(yb_ref[...].astype(BF16), w_ref[na:, :])
        o_ref[...] = h_ref[...] + y

    return pl.pallas_call(
        body, name="out_fwd",
        grid=(tp // tm,),
        in_specs=[pl.BlockSpec((tm, d), lambda i: (i, 0)), pl.BlockSpec((tm, na), lambda i: (i, 0)),
                  pl.BlockSpec((tm, nb), lambda i: (i, 0)), pl.BlockSpec((d, d), lambda i: (0, 0))],
        out_specs=pl.BlockSpec((tm, d), lambda i: (i, 0)),
        out_shape=jax.ShapeDtypeStruct((tp, d), F32),
        compiler_params=_cparams(1),
    )(h, ya, yb, w)


def _out_bwd(dy, ya, yb, w, tm, after=()):
    tp, d = dy.shape
    na, nb = ya.shape[1], yb.shape[1]

    def body(dy_ref, ya_ref, yb_ref, w_ref, *rest):
        da_ref, db_ref, dw_ref, acc = rest[len(after):]
        i = pl.program_id(0)

        @pl.when(i == 0)
        def _():
            acc[...] = jnp.zeros_like(acc)

        dyb16 = dy_ref[...].astype(BF16)
        da_ref[...] = _dot_nt(dyb16, w_ref[0:na, :])
        db_ref[...] = _dot_nt(dyb16, w_ref[na:, :])
        acc[0:na, :] += _dot_tn(ya_ref[...].astype(BF16), dyb16)
        acc[na:, :] += _dot_tn(yb_ref[...].astype(BF16), dyb16)

        @pl.when(i == pl.num_programs(0) - 1)
        def _():
            dw_ref[...] = acc[...].astype(BF16)

    return pl.pallas_call(
        body, name="out_bwd",
        grid=(tp // tm,),
        in_specs=[pl.BlockSpec((tm, d), lambda i: (i, 0)), pl.BlockSpec((tm, na), lambda i: (i, 0)),
                  pl.BlockSpec((tm, nb), lambda i: (i, 0)), pl.BlockSpec((d, d), lambda i: (0, 0))]
        + [pl.BlockSpec(memory_space=pl.ANY)] * len(after),
        out_specs=[pl.BlockSpec((tm, na), lambda i: (i, 0)), pl.BlockSpec((tm, nb), lambda i: (i, 0)),
                   pl.BlockSpec((d, d), lambda i: (0, 0))],
        out_shape=[jax.ShapeDtypeStruct((tp, na), F32), jax.ShapeDtypeStruct((tp, nb), F32),
                   jax.ShapeDtypeStruct((d, d), BF16)],
        scratch_shapes=[pltpu.VMEM((d, d), F32)],
        compiler_params=_cparams(1),
    )(dy, ya, yb, w, *after)


def _loss_bwd(h, gfin, tgt, t_real, tm):
    tp, d = h.shape

    def body(h_ref, g_ref, t_ref, loss_ref, dh_ref, dg_ref):
        i = pl.program_id(0)

        @pl.when(i == 0)
        def _():
            loss_ref[...] = jnp.zeros_like(loss_ref)
            dg_ref[...] = jnp.zeros_like(dg_ref)

        rows = i * tm + lax.broadcasted_iota(jnp.int32, (tm, 1), 0)
        valid = (rows >= N_META) & (rows < t_real)
        rstd, xhat = _rms_stats(h_ref[...])
        g = g_ref[...]
        err = jnp.where(valid, xhat * g - t_ref[...], 0.0)
        e2 = jnp.sum(err * err, axis=1, keepdims=True)
        loss_ref[...] += (0.5 / d) * jnp.sum(e2, axis=0, keepdims=True)
        dy = err * (1.0 / d)
        dh, dg = _rms_bwd(dy, g, rstd, xhat)
        dh_ref[...] = dh
        dg_ref[...] += dg

    return pl.pallas_call(
        body, name="loss_bwd",
        grid=(tp // tm,),
        in_specs=[pl.BlockSpec((tm, d), lambda i: (i, 0)), pl.BlockSpec((1, d), lambda i: (0, 0)),
                  pl.BlockSpec((tm, d), lambda i: (i, 0))],
        out_specs=[pl.BlockSpec((1, 1), lambda i: (0, 0)), pl.BlockSpec((tm, d), lambda i: (i, 0)),
                   pl.BlockSpec((1, d), lambda i: (0, 0))],
        out_shape=[jax.ShapeDtypeStruct((1, 1), F32), jax.ShapeDtypeStruct((tp, d), F32),
                   jax.ShapeDtypeStruct((1, d), F32)],
        compiler_params=_cparams(1),
    )(h, gfin, tgt)


POOL_HALO = 16


def _pool_lane_consts(n_rows):
    lane = lax.broadcasted_iota(jnp.int32, (n_rows, D_POOL), 1)
    grp = lane // POOL_GROUP
    win = jnp.where(grp == 0, 2.0, jnp.where(grp == 1, 4.0, jnp.where(grp == 2, 8.0, 16.0)))
    return grp, win


def _pool_select(grp, s2, s4, s8, s16):
    return jnp.where(grp == 0, s2, jnp.where(grp == 1, s4, jnp.where(grp == 2, s8, s16)))


def _pool_mixed(x, row0, tr):
    n = tr + POOL_HALO
    s2 = x + pltpu.roll(x, 1, 0)
    s4 = s2 + pltpu.roll(s2, 2, 0)
    s8 = s4 + pltpu.roll(s4, 4, 0)
    s16 = s8 + pltpu.roll(s8, 8, 0)
    grp, win = _pool_lane_consts(n)
    rows = row0 - POOL_HALO + lax.broadcasted_iota(jnp.int32, (n, D_POOL), 0)
    cnt = jnp.minimum((rows + 1).astype(F32), win)
    pooled = _pool_select(grp, s2, s4, s8, s16) / jnp.maximum(cnt, 1.0)
    return (pooled - x)[POOL_HALO:, :]


def _pool_fwd(p, wbd, scale, tr):
    tp = p.shape[0]
    nt = tp // tr

    def body(p_ref, w_ref, s_ref, y_ref, usc):
        usc[0:POOL_HALO, :] = jnp.zeros((POOL_HALO, D_POOL), F32)
        usc[POOL_HALO:, :] = p_ref[...]

        def tile(r, carry):
            r0 = pl.multiple_of(r * tr, SUBLANE)
            x = usc[pl.ds(r0, tr + POOL_HALO), :]
            mixed = _pool_mixed(x, r0, tr)
            y_ref[pl.ds(r0, tr), :] = (_dot(mixed.astype(BF16), w_ref[...]) * s_ref[...]).astype(BF16)
            return carry

        lax.fori_loop(0, nt, tile, 0)

    return pl.pallas_call(
        body, name="pool_fwd",
        grid=(1,),
        in_specs=[pl.BlockSpec((tp, D_POOL), lambda i: (0, 0)), pl.BlockSpec((D_POOL, D_POOL), lambda i: (0, 0)),
                  pl.BlockSpec((1, D_POOL), lambda i: (0, 0))],
        out_specs=pl.BlockSpec((tp, D_POOL), lambda i: (0, 0)),
        out_shape=jax.ShapeDtypeStruct((tp, D_POOL), BF16),
        scratch_shapes=[pltpu.VMEM((tp + POOL_HALO, D_POOL), F32)],
        compiler_params=_cparams(1),
    )(p, wbd, scale)


def _pool_bwd(p, dya, wbd, scale, tr):
    tp = p.shape[0]
    nt = tp // tr

    def body(p_ref, dy_ref, w_ref, s_ref, du_ref, dw_ref, ds_ref, usc, gsc):
        usc[0:POOL_HALO, :] = jnp.zeros((POOL_HALO, D_POOL), F32)
        usc[POOL_HALO:, :] = p_ref[...]
        gsc[tp:, :] = jnp.zeros((POOL_HALO, D_POOL), F32)
        dw_ref[...] = jnp.zeros_like(dw_ref)
        ds_ref[...] = jnp.zeros_like(ds_ref)
        grp, win = _pool_lane_consts(tr)

        def tile1(r, carry):
            r0 = pl.multiple_of(r * tr, SUBLANE)
            x = usc[pl.ds(r0, tr + POOL_HALO), :]
            mixed = _pool_mixed(x, r0, tr).astype(BF16)
            dy = dy_ref[pl.ds(r0, tr), :]
            dys = (dy * s_ref[...]).astype(BF16)
            ypre = _dot(mixed, w_ref[...])
            ds_ref[...] += _colsum(dy * ypre)
            dw_ref[...] += _dot_tn(mixed, dys)
            dmx = _dot_nt(dys, w_ref[...])
            rows = r0 + lax.broadcasted_iota(jnp.int32, (tr, D_POOL), 0)
            cnt = jnp.minimum((rows + 1).astype(F32), win)
            gsc[pl.ds(r0, tr), :] = dmx / cnt
            return carry

        lax.fori_loop(0, nt, tile1, 0)
        n = tr + POOL_HALO
        grp2, win2 = _pool_lane_consts(n)

        def tile2(r, carry):
            r0 = pl.multiple_of(r * tr, SUBLANE)
            g = gsc[pl.ds(r0, n), :]
            s2 = g + pltpu.roll(g, n - 1, 0)
            s4 = s2 + pltpu.roll(s2, n - 2, 0)
            s8 = s4 + pltpu.roll(s4, n - 4, 0)
            s16 = s8 + pltpu.roll(s8, n - 8, 0)
            pooled_t = _pool_select(grp2, s2, s4, s8, s16)
            rows = r0 + lax.broadcasted_iota(jnp.int32, (n, D_POOL), 0)
            cnt = jnp.minimum((rows + 1).astype(F32), win2)
            du = pooled_t - g * cnt
            du_ref[pl.ds(r0, tr), :] = du[0:tr, :].astype(BF16)
            return carry

        lax.fori_loop(0, nt, tile2, 0)

    return pl.pallas_call(
        body, name="pool_bwd",
        grid=(1,),
        in_specs=[pl.BlockSpec((tp, D_POOL), lambda i: (0, 0)), pl.BlockSpec((tp, D_POOL), lambda i: (0, 0)),
                  pl.BlockSpec((D_POOL, D_POOL), lambda i: (0, 0)), pl.BlockSpec((1, D_POOL), lambda i: (0, 0))],
        out_specs=[pl.BlockSpec((tp, D_POOL), lambda i: (0, 0)), pl.BlockSpec((D_POOL, D_POOL), lambda i: (0, 0)),
                   pl.BlockSpec((1, D_POOL), lambda i: (0, 0))],
        out_shape=[jax.ShapeDtypeStruct((tp, D_POOL), BF16), jax.ShapeDtypeStruct((D_POOL, D_POOL), F32),
                   jax.ShapeDtypeStruct((1, D_POOL), F32)],
        scratch_shapes=[pltpu.VMEM((tp + POOL_HALO, D_POOL), F32), pltpu.VMEM((tp + POOL_HALO, D_POOL), F32)],
        compiler_params=_cparams(1),
    )(p, dya, wbd, scale)


def _hgrn_levels(ch):
    levels = []
    w = ch // 2
    while w >= 1:
        levels.append(w)
        w //= 2
    return levels


def _hgrn_consts(ch):
    t = np.arange(ch)
    tril = t[None, :] <= t[:, None]
    masks = []
    for w in _hgrn_levels(ch):
        blk = t // (2 * w)
        upper = t % (2 * w) >= w
        masks.append(upper[:, None] & (~upper)[None, :] & (blk[:, None] == blk[None, :]))
    masks.append(tril)
    msk = np.stack(masks).astype(np.float32)
    return jnp.asarray(tril.astype(np.float32), BF16), jnp.asarray(msk, F32), len(masks) - 1


def _split3(x):
    hi = x.astype(BF16)
    r1 = x - hi.astype(F32)
    mid = r1.astype(BF16)
    lo = (r1 - mid.astype(F32)).astype(BF16)
    return hi, mid, lo


def _hgrn_exponents(tril, logf):
    ch = logf.shape[0]
    hi, mid, lo = _split3(logf)
    x = _dot(tril, jnp.concatenate([hi, mid, lo], axis=1))
    b = x[:, 0:HEAD] + x[:, HEAD:2 * HEAD] + x[:, 2 * HEAD:3 * HEAD]
    rows = lax.broadcasted_iota(jnp.int32, (ch, HEAD), 0)
    fx = jnp.broadcast_to(b[ch - 1:ch, :], (ch, HEAD)) - b
    lev = []
    for w in _hgrn_levels(ch):
        pos = rows % (2 * w)
        upper = pos >= w
        if w >= SUBLANE:
            parts = [jnp.broadcast_to(b[k * 2 * w + w - 1:k * 2 * w + w, :], (2 * w, HEAD))
                     for k in range(ch // (2 * w))]
            bmid = parts[0] if len(parts) == 1 else jnp.concatenate(parts, axis=0)
            dx = jnp.where(upper, b - bmid, 0.0)
            ex = jnp.where(upper, 0.0, bmid - b)
        else:
            dx = logf
            ex = jnp.zeros_like(logf)
            for i in range(1, w):
                dx = dx + jnp.where(pos >= w + i, pltpu.roll(logf, i, 0), 0.0)
                ex = ex + jnp.where(pos <= w - 1 - i, pltpu.roll(logf, ch - i, 0), 0.0)
            dx = jnp.where(upper, dx, 0.0)
        lev.append((dx, ex))
    return b, fx, lev


def _hgrn_exponents_bwd(tril, d_b, d_fx, d_blast, lev_grads):
    ch = d_b.shape[0]
    rows = lax.broadcasted_iota(jnp.int32, (ch, HEAD), 0)
    db = d_b - d_fx
    dlf = jnp.zeros_like(d_b)
    for w, (ddx, dex) in zip(_hgrn_levels(ch), lev_grads):
        pos = rows % (2 * w)
        upper = pos >= w
        gu = jnp.where(upper, ddx, 0.0)
        if w >= SUBLANE:
            gl = jnp.where(upper, 0.0, dex)
            db = db + gu - gl
            diff = gl - gu
            for k in range(ch // (2 * w)):
                s = _colsum(diff[k * 2 * w:(k + 1) * 2 * w, :])
                db = db + jnp.where(rows == k * 2 * w + w - 1, s, 0.0)
        else:
            dlf = dlf + gu
            for i in range(1, w):
                dlf = dlf + pltpu.roll(jnp.where(pos >= w + i, gu, 0.0), ch - i, 0)
                dlf = dlf + pltpu.roll(jnp.where(pos <= w - 1 - i, dex, 0.0), i, 0)
    db = db + jnp.where(rows == ch - 1, _colsum(d_fx) + d_blast, 0.0)
    hi = db.astype(BF16)
    lo = (db - hi.astype(F32)).astype(BF16)
    d2 = _dot_tn(tril, jnp.concatenate([hi, lo], axis=1))
    return d2[:, 0:HEAD] + d2[:, HEAD:2 * HEAD] + dlf


def _lockstep(gens):
    results = [None] * len(gens)
    live = list(range(len(gens)))
    while live:
        for i in list(live):
            try:
                next(gens[i])
            except StopIteration as stop:
                results[i] = stop.value
                live.remove(i)
    return results


def _hgrn_gates(q_raw, z, lb):
    sz = _sigmoid(z)
    f = lb + (1.0 - lb) * sz
    q = q_raw * _sigmoid(q_raw)
    k = (1.0 - lb) * (1.0 - sz)
    return q, k, f, sz


def _hgrn_intra(q, k, lev, msk_ref, n_lev, ch):
    eye = (lax.broadcasted_iota(jnp.int32, (ch, ch), 0) == lax.broadcasted_iota(jnp.int32, (ch, ch), 1))
    a = jnp.where(eye, jnp.sum(q * k, axis=1, keepdims=True), 0.0)
    ops = []
    for lv in range(n_lev):
        eq = jnp.exp(lev[lv][0])
        ek = jnp.exp(lev[lv][1])
        qd = q * eq
        kd = k * ek
        a = a + msk_ref[lv] * _dot_nt(qd.astype(BF16), kd.astype(BF16))
        ops.append((eq, ek, qd, kd))
        yield
    return a, ops


def _hgrn_fwd(p, lb_logits, gnorm, mst, msk, n_lev, tm):
    tp = p.shape[0]
    ch = HG_CHUNK
    nct = tm // ch
    nt = tp // tm
    nr = mst.shape[0]
    base = D_POOL // HEAD

    hp = HG_HEADS_PER_STEP
    wide = hp * HEAD
    npr = wide // HG_PBLOCK

    def body(*refs):
        p_refs = refs[:4 * npr]
        lg_ref, gn_ref, mst_ref, msk_ref, y_ref, ss_ref, st_sc = refs[4 * npr:]

        @pl.when(pl.program_id(1) == 0)
        def _():
            st_sc[...] = jnp.zeros_like(st_sc)

        lb_all = _sigmoid(lg_ref[0:1, :] - lg_ref[1:2, :])

        def raw(seg, hh, r0):
            per = HG_PBLOCK // HEAD
            return p_refs[seg * npr + hh // per][pl.ds(r0, ch), (hh % per) * HEAD:(hh % per + 1) * HEAD]

        def one_head(hh, c, r0):
            ls = slice(hh * HEAD, (hh + 1) * HEAD)
            q_raw, z, v, g_raw, st = raw(0, hh, r0), raw(1, hh, r0), raw(2, hh, r0), raw(3, hh, r0), st_sc[hh]
            q, k, f, _ = _hgrn_gates(q_raw, z, lb_all[:, ls])
            yield
            b, fx, lev = _hgrn_exponents(mst_ref[...], jnp.log(f))
            yield
            qe = q * jnp.exp(b)
            a, _ = yield from _hgrn_intra(q, k, lev, msk_ref, n_lev, ch)
            v16 = v.astype(BF16)
            o = _dot_nt(qe.astype(BF16), st.astype(BF16)) + _dot(a.astype(BF16), v16)
            kl = k * jnp.exp(fx)
            st_new = st * jnp.exp(b[ch - 1:ch, :]) + _dot_tn(v16, kl.astype(BF16))
            yield
            rstd = lax.rsqrt(jnp.mean(o * o, axis=-1, keepdims=True) + EPS)
            return st, st_new, o * rstd * gn_ref[...] * (g_raw * _sigmoid(g_raw))

        def chunk(c, carry):
            r0 = pl.multiple_of(c * ch, ch)
            results = _lockstep([one_head(hh, c, r0) for hh in range(hp)])
            for hh, (st, st_new, y) in enumerate(results):
                ss_ref[hh, c] = st
                st_sc[hh] = st_new
                y_ref[pl.ds(r0, ch), hh * HEAD:(hh + 1) * HEAD] = y.astype(BF16)
            return carry

        lax.fori_loop(0, nct, chunk, 0)

    def pspec(seg, part):
        return pl.BlockSpec((tm, HG_PBLOCK),
                            lambda h, i: (i, (base + seg * HG_HEADS) * HEAD // HG_PBLOCK + h * npr + part))

    return pl.pallas_call(
        body, name="hgrn_fwd",
        grid=(HG_HEADS // hp, nt),
        in_specs=[pspec(seg, part) for seg in range(4) for part in range(npr)]
        + [pl.BlockSpec((2, wide), lambda h, i: (0, h)),
           pl.BlockSpec((1, HEAD), lambda h, i: (0, 0)),
           pl.BlockSpec((nr, ch), lambda h, i: (0, 0)),
           pl.BlockSpec((n_lev + 1, ch, ch), lambda h, i: (0, 0, 0))],
        out_specs=[pl.BlockSpec((tm, wide), lambda h, i: (i, h)),
                   pl.BlockSpec((hp, nct, HEAD, HEAD), lambda h, i: (h, i, 0, 0))],
        out_shape=[jax.ShapeDtypeStruct((tp, D_HGRN), BF16),
                   jax.ShapeDtypeStruct((HG_HEADS, tp // ch, HEAD, HEAD), F32)],
        scratch_shapes=[pltpu.VMEM((hp, HEAD, HEAD), F32)],
        compiler_params=_cparams(2),
    )(*([p] * (4 * npr)), lb_logits, gnorm, mst, msk)


def _hgrn_bwd(p, dyb, states, lb_logits, gnorm, mst, msk, n_lev, tm):
    tp = p.shape[0]
    ch = HG_CHUNK
    nct = tm // ch
    nt = tp // tm
    nr = mst.shape[0]
    base = D_POOL // HEAD

    hp = HG_HEADS_PER_STEP
    wide = hp * HEAD
    npr = wide // HG_PBLOCK

    def body(*refs):
        p_refs = refs[:4 * npr]
        (dy_ref, ss_ref, lg_ref, gn_ref, mst_ref, msk_ref,
         dq_ref, dz_ref, dv_ref, dg_ref, dlg_ref, dgn_ref, dst_sc, dlb_sc) = refs[4 * npr:]
        ti = pl.program_id(1)

        def raw(seg, hh, r0):
            per = HG_PBLOCK // HEAD
            return p_refs[seg * npr + hh // per][pl.ds(r0, ch), (hh % per) * HEAD:(hh % per + 1) * HEAD]

        @pl.when(ti == 0)
        def _():
            dst_sc[...] = jnp.zeros_like(dst_sc)
            dlb_sc[...] = jnp.zeros_like(dlb_sc)
            dgn_ref[...] = jnp.zeros_like(dgn_ref)

        lb_all = _sigmoid(lg_ref[0:1, :] - lg_ref[1:2, :])
        gn = gn_ref[...]

        def load_head(hh, c, r0):
            ls = slice(hh * HEAD, (hh + 1) * HEAD)
            return (raw(0, hh, r0), raw(1, hh, r0), raw(2, hh, r0), raw(3, hh, r0),
                    dy_ref[pl.ds(r0, ch), ls], ss_ref[hh, c], dst_sc[hh])

        def store_head(hh, r0, res):
            ls = slice(hh * HEAD, (hh + 1) * HEAD)
            dq_raw, dz, dv, dg_raw, dgn, dst_new, dlb = res
            dq_ref[pl.ds(r0, ch), ls] = dq_raw
            dz_ref[pl.ds(r0, ch), ls] = dz
            dv_ref[pl.ds(r0, ch), ls] = dv
            dg_ref[pl.ds(r0, ch), ls] = dg_raw
            dgn_ref[hh] += dgn
            dst_sc[hh] = dst_new
            dlb_sc[:, ls] += dlb

        def one_head(hh, loaded):
            ls = slice(hh * HEAD, (hh + 1) * HEAD)
            lb = lb_all[:, ls]
            q_raw, z, v, g_raw, dy, st, dst = loaded
            q, k, f, sz = _hgrn_gates(q_raw, z, lb)
            yield
            b, fx, lev = _hgrn_exponents(mst_ref[...], jnp.log(f))
            yield
            eb = jnp.exp(b)
            ef = jnp.exp(fx)
            elast = jnp.exp(b[ch - 1:ch, :])
            qe = q * eb
            kl = k * ef
            a, ops = yield from _hgrn_intra(q, k, lev, msk_ref, n_lev, ch)
            v16 = v.astype(BF16)
            st16 = st.astype(BF16)
            qe16 = qe.astype(BF16)
            kl16 = kl.astype(BF16)
            a16 = a.astype(BF16)
            o = _dot_nt(qe16, st16) + _dot(a16, v16)
            yield
            sg = _sigmoid(g_raw)
            rstd = lax.rsqrt(jnp.mean(o * o, axis=-1, keepdims=True) + EPS)
            oh = o * rstd
            dg_out = (dy * oh * gn * (sg * (1.0 + g_raw * (1.0 - sg)))).astype(BF16)
            don = dy * (g_raw * sg)
            dgn = _colsum(don * oh)
            doh = don * gn
            do = rstd * (doh - oh * jnp.mean(doh * oh, axis=-1, keepdims=True))
            do16 = do.astype(BF16)
            dst16 = dst.astype(BF16)
            yield
            dv = _dot_tn(a16, do16) + _dot_nt(kl16, dst16)
            da = msk_ref[n_lev] * _dot_nt(do16, v16)
            dqe = _dot(do16, st16)
            dkl = _dot(v16, dst16)
            dst_new = dst * elast + _dot_tn(do16, qe16)
            yield
            db_last = _colsum(dst * st) * elast
            dad = jnp.sum(do * v, axis=1, keepdims=True)
            dq = dad * k + dqe * eb
            dk = dad * q + dkl * ef
            lev_grads = []
            for lv in range(n_lev):
                eq, ek, qd, kd = ops[lv]
                gl = (msk_ref[lv] * da).astype(BF16)
                dqd = _dot(gl, kd.astype(BF16))
                dkd = _dot_tn(gl, qd.astype(BF16))
                dq = dq + dqd * eq
                dk = dk + dkd * ek
                lev_grads.append((dqd * qd, dkd * kd))
                yield
            dlogf = _hgrn_exponents_bwd(mst_ref[...], dqe * qe, dkl * kl, db_last, lev_grads)
            yield
            sq = _sigmoid(q_raw)
            dq_out = (dq * (sq * (1.0 + q_raw * (1.0 - sq)))).astype(BF16)
            dfk = dlogf / f - dk
            dz_out = (dfk * (1.0 - lb) * sz * (1.0 - sz)).astype(BF16)
            return dq_out, dz_out, dv.astype(BF16), dg_out, dgn, dst_new, _colsum(dfk * (1.0 - sz))

        def chunk(cc, carry):
            c = nct - 1 - cc
            r0 = pl.multiple_of(c * ch, ch)
            loaded = [load_head(hh, c, r0) for hh in range(HG_HEADS_PER_STEP)]
            results = _lockstep([one_head(hh, loaded[hh]) for hh in range(HG_HEADS_PER_STEP)])
            for hh in range(HG_HEADS_PER_STEP):
                store_head(hh, r0, results[hh])
            return carry

        lax.fori_loop(0, nct, chunk, 0, unroll=1)

        @pl.when(ti == nt - 1)
        def _():
            dl0 = dlb_sc[...] * lb_all * (1.0 - lb_all)
            dlg_ref[0:1, :] = dl0
            dlg_ref[1:2, :] = -dl0

    def pspec(seg, part):
        return pl.BlockSpec((tm, HG_PBLOCK), lambda h, i: (
            nt - 1 - i, (base + seg * HG_HEADS) * HEAD // HG_PBLOCK + h * npr + part))

    ospec = pl.BlockSpec((tm, wide), lambda h, i: (nt - 1 - i, h))
    return pl.pallas_call(
        body, name="hgrn_bwd",
        grid=(HG_HEADS // hp, nt),
        in_specs=[pspec(seg, part) for seg in range(4) for part in range(npr)]
        + [ospec, pl.BlockSpec((hp, nct, HEAD, HEAD), lambda h, i: (h, nt - 1 - i, 0, 0)),
           pl.BlockSpec((2, wide), lambda h, i: (0, h)),
           pl.BlockSpec((1, HEAD), lambda h, i: (0, 0)),
           pl.BlockSpec((nr, ch), lambda h, i: (0, 0)),
           pl.BlockSpec((n_lev + 1, ch, ch), lambda h, i: (0, 0, 0))],
        out_specs=[ospec, ospec, ospec, ospec,
                   pl.BlockSpec((2, wide), lambda h, i: (0, h)),
                   pl.BlockSpec((hp, 1, HEAD), lambda h, i: (h, 0, 0))],
        out_shape=[jax.ShapeDtypeStruct((tp, D_HGRN), BF16)] * 4
        + [jax.ShapeDtypeStruct((2, D_HGRN), F32), jax.ShapeDtypeStruct((HG_HEADS, 1, HEAD), F32)],
        scratch_shapes=[pltpu.VMEM((hp, HEAD, HEAD), F32), pltpu.VMEM((1, wide), F32)],
        compiler_params=_cparams(2),
    )(*([p] * (4 * npr)), dyb, states, lb_logits, gnorm, mst, msk)


def _tap_views(x, tr, halo, width):
    subs = {0: x}
    views = []
    for j in range(width):
        tiles, rem = divmod(width - 1 - j, SUBLANE)
        if rem not in subs:
            subs[rem] = pltpu.roll(x, rem, 0)
        start = halo - tiles * SUBLANE
        views.append(subs[rem][start:start + tr, :])
    return views


def _tap_views_t(y, tr, halo, width):
    n = tr + halo
    subs = {0: y}
    views = []
    for j in range(width):
        tiles, rem = divmod(width - 1 - j, SUBLANE)
        if rem not in subs:
            subs[rem] = pltpu.roll(y, n - rem, 0)
        views.append(subs[rem][tiles * SUBLANE:tiles * SUBLANE + tr, :])
    return views


def _weighted_sum(views, w_ref):
    acc = None
    for j, view in enumerate(views):
        term = view * w_ref[j:j + 1, :]
        acc = term if acc is None else acc + term
    return acc


def _conv_taps(x, w_ref, tr, halo, width):
    return _weighted_sum(_tap_views(x, tr, halo, width), w_ref)


def _conv_taps_t(y, w_ref, tr, halo, width):
    return _weighted_sum(_tap_views_t(y, tr, halo, width), w_ref)


def _ln_stats(cv):
    mu = jnp.mean(cv, axis=-1, keepdims=True)
    xc = cv - mu
    rstd = lax.rsqrt(jnp.mean(xc * xc, axis=-1, keepdims=True) + EPS)
    return rstd, xc * rstd


def _convmod_fwd(p, w, bias, ln_g, ln_b, tr):
    tp = p.shape[0]
    nt = tp // tr
    nb = D_CONV // HEAD

    def body(a_ref, b_ref, w_ref, bi_ref, g_ref, be_ref, y_ref, cv_ref, usc):
        usc[0:CONV_HALO, :] = jnp.zeros((CONV_HALO, HEAD), F32)
        usc[CONV_HALO:, :] = a_ref[...] * _sigmoid(b_ref[...])

        def tile(r, carry):
            r0 = pl.multiple_of(r * tr, SUBLANE)
            x = usc[pl.ds(r0, tr + CONV_HALO), :]
            cv = _conv_taps(x, w_ref, tr, CONV_HALO, CONV_WIDTH) + bi_ref[...]
            cv_ref[pl.ds(r0, tr), :] = cv
            _, xh = _ln_stats(cv)
            un = xh * g_ref[...] + be_ref[...]
            y_ref[pl.ds(r0, tr), :] = (un * _sigmoid(un)).astype(BF16)
            return carry

        lax.fori_loop(0, nt, tile, 0)

    vec = lambda: pl.BlockSpec((1, HEAD), lambda j: (0, j))
    return pl.pallas_call(
        body, name="convmod_fwd",
        grid=(nb,),
        in_specs=[pl.BlockSpec((tp, HEAD), lambda j: (0, j)), pl.BlockSpec((tp, HEAD), lambda j: (0, nb + j)),
                  pl.BlockSpec((CONV_HALO, HEAD), lambda j: (0, j)), vec(), vec(), vec()],
        out_specs=[pl.BlockSpec((tp, HEAD), lambda j: (0, j))] * 2,
        out_shape=[jax.ShapeDtypeStruct((tp, D_CONV), BF16), jax.ShapeDtypeStruct((tp, D_CONV), F32)],
        scratch_shapes=[pltpu.VMEM((tp + CONV_HALO, HEAD), F32)],
        compiler_params=_cparams(1),
    )(p, p, w, bias, ln_g, ln_b)


def _convmod_bwd(p, cv_saved, dyc, w, ln_g, ln_b, tr):
    tp = p.shape[0]
    nt = tp // tr
    nb = D_CONV // HEAD

    def body(a_ref, b_ref, cv_ref, dy_ref, w_ref, g_ref, be_ref, da_ref, db_ref, dw_ref, dv_ref, usc, dsc):
        usc[0:CONV_HALO, :] = jnp.zeros((CONV_HALO, HEAD), F32)
        usc[CONV_HALO:, :] = a_ref[...] * _sigmoid(b_ref[...])
        dsc[tp:, :] = jnp.zeros((CONV_HALO, HEAD), F32)
        dw_ref[...] = jnp.zeros_like(dw_ref)
        dv_ref[...] = jnp.zeros_like(dv_ref)

        def tile1(r, carry):
            r0 = pl.multiple_of(r * tr, SUBLANE)
            x = usc[pl.ds(r0, tr + CONV_HALO), :]
            views = _tap_views(x, tr, CONV_HALO, CONV_WIDTH)
            rstd, xh = _ln_stats(cv_ref[pl.ds(r0, tr), :])
            un = xh * g_ref[...] + be_ref[...]
            sg = _sigmoid(un)
            dun = dy_ref[pl.ds(r0, tr), :] * (sg * (1.0 + un * (1.0 - sg)))
            dv_ref[0, 1:2, :] += _colsum(dun * xh)
            dv_ref[0, 2:3, :] += _colsum(dun)
            dxh = dun * g_ref[...]
            dcv = rstd * (dxh - jnp.mean(dxh, axis=-1, keepdims=True)
                          - xh * jnp.mean(dxh * xh, axis=-1, keepdims=True))
            dv_ref[0, 0:1, :] += _colsum(dcv)
            for j in range(CONV_WIDTH):
                dw_ref[0, j:j + 1, :] += _colsum(dcv * views[j])
            dsc[pl.ds(r0, tr), :] = dcv
            return carry

        lax.fori_loop(0, nt, tile1, 0)

        def tile2(r, carry):
            r0 = pl.multiple_of(r * tr, SUBLANE)
            y = dsc[pl.ds(r0, tr + CONV_HALO), :]
            du = _conv_taps_t(y, w_ref, tr, CONV_HALO, CONV_WIDTH)
            a = a_ref[pl.ds(r0, tr), :]
            sb = _sigmoid(b_ref[pl.ds(r0, tr), :])
            da_ref[pl.ds(r0, tr), :] = (du * sb).astype(BF16)
            db_ref[pl.ds(r0, tr), :] = (du * a * sb * (1.0 - sb)).astype(BF16)
            return carry

        lax.fori_loop(0, nt, tile2, 0)

    vec = lambda: pl.BlockSpec((1, HEAD), lambda j: (0, j))
    col = lambda: pl.BlockSpec((tp, HEAD), lambda j: (0, j))
    return pl.pallas_call(
        body, name="convmod_bwd",
        grid=(nb,),
        in_specs=[col(), pl.BlockSpec((tp, HEAD), lambda j: (0, nb + j)), col(), col(),
                  pl.BlockSpec((CONV_HALO, HEAD), lambda j: (0, j)), vec(), vec()],
        out_specs=[col(), col(), pl.BlockSpec((1, CONV_HALO, HEAD), lambda j: (j, 0, 0)),
                   pl.BlockSpec((1, SUBLANE, HEAD), lambda j: (j, 0, 0))],
        out_shape=[jax.ShapeDtypeStruct((tp, D_CONV), BF16), jax.ShapeDtypeStruct((tp, D_CONV), BF16),
                   jax.ShapeDtypeStruct((nb, CONV_HALO, HEAD), F32), jax.ShapeDtypeStruct((nb, SUBLANE, HEAD), F32)],
        scratch_shapes=[pltpu.VMEM((tp + CONV_HALO, HEAD), F32), pltpu.VMEM((tp + CONV_HALO, HEAD), F32)],
        compiler_params=_cparams(1),
    )(p, p, cv_saved, dyc, w, ln_g, ln_b)


def _log1p_small(y):
    return jnp.where(y < 1e-4, y * (1.0 - 0.5 * y), jnp.log(1.0 + y))


def _softplus(x):
    return jnp.maximum(x, 0.0) + _log1p_small(jnp.exp(-jnp.abs(x)))


def _expm1(x):
    return jnp.where(jnp.abs(x) < 1e-2, x * (1.0 + 0.5 * x * (1.0 + x * (1.0 / 3.0))), jnp.exp(x) - 1.0)


def _gelu_parts(x):
    c = 0.7978845608028654
    inner = c * (x + 0.044715 * x * x * x)
    th = jnp.tanh(inner)
    gelu = 0.5 * x * (1.0 + th)
    dgelu = 0.5 * (1.0 + th) + 0.5 * x * (1.0 - th * th) * c * (1.0 + 3.0 * 0.044715 * x * x)
    return gelu, dgelu


def _lru_gates(x_all, tp, cw_ref, cb_ref, wa_ref, ba_ref, wx_ref, bx_ref, lam_ref):
    u = _conv_taps(x_all, cw_ref, tp, LRU_HALO, LRU_CONV) + cb_ref[...]
    u16 = u.astype(BF16)
    r = _sigmoid(_dot(u16, wa_ref[0]) + ba_ref[...])
    i = _sigmoid(_dot(u16, wx_ref[0]) + bx_ref[...])
    sp = _softplus(-lam_ref[...])
    la = -LRU_C * r * sp
    a = jnp.exp(la)
    mult = jnp.sqrt(-_expm1(2.0 * la))
    return u, r, i, a, mult, sp


def _lru_specs(tp, nb):
    col = lambda k: pl.BlockSpec((tp, HEAD), functools.partial(lambda j, k: (0, k * nb + j), k=k))
    vec = lambda: pl.BlockSpec((1, HEAD), lambda j: (0, j))
    mat = lambda: pl.BlockSpec((1, HEAD, HEAD), lambda j: (j, 0, 0))
    return col, vec, mat


def _lru_fwd(p, cw, cb, wa, ba, wx, bx, lam):
    tp = p.shape[0]
    nb = D_LRU // HEAD
    ng = tp // SUBLANE

    def body(x_ref, gt_ref, cw_ref, cb_ref, wa_ref, ba_ref, wx_ref, bx_ref, lam_ref, y_ref, hs_ref,
             xsc, asc, bsc):
        xsc[0:LRU_HALO, :] = jnp.zeros((LRU_HALO, HEAD), F32)
        xsc[LRU_HALO:, :] = x_ref[...]
        u, r, i, a, mult, _ = _lru_gates(xsc[...], tp, cw_ref, cb_ref, wa_ref, ba_ref, wx_ref, bx_ref, lam_ref)
        rows = lax.broadcasted_iota(jnp.int32, (tp, HEAD), 0)
        b = jnp.where(rows == 0, 1.0, mult) * (i * u)
        sub = rows % SUBLANE
        for k in (1, 2, 4):
            m = sub >= k
            b = jnp.where(m, a * pltpu.roll(b, k, 0) + b, b)
            a = jnp.where(m, a * pltpu.roll(a, k, 0), a)
        asc[...] = a
        bsc[...] = b

        def grp(g, carry):
            r0 = pl.multiple_of(g * SUBLANE, SUBLANE)
            h = bsc[pl.ds(r0, SUBLANE), :] + asc[pl.ds(r0, SUBLANE), :] * carry
            hs_ref[pl.ds(r0, SUBLANE), :] = h
            return jnp.broadcast_to(h[SUBLANE - 1:SUBLANE, :], (SUBLANE, HEAD))

        lax.fori_loop(0, ng, grp, jnp.zeros((SUBLANE, HEAD), F32))
        gelu, _ = _gelu_parts(gt_ref[...])
        y_ref[...] = (gelu * hs_ref[...]).astype(BF16)

    col, vec, mat = _lru_specs(tp, nb)
    return pl.pallas_call(
        body, name="lru_fwd",
        grid=(nb,),
        in_specs=[col(2), col(3), pl.BlockSpec((LRU_CONV, HEAD), lambda j: (0, j)), vec(), mat(), vec(), mat(),
                  vec(), vec()],
        out_specs=[pl.BlockSpec((tp, HEAD), lambda j: (0, j)), pl.BlockSpec((tp, HEAD), lambda j: (0, j))],
        out_shape=[jax.ShapeDtypeStruct((tp, D_LRU), BF16), jax.ShapeDtypeStruct((tp, D_LRU), F32)],
        scratch_shapes=[pltpu.VMEM((tp + LRU_HALO, HEAD), F32), pltpu.VMEM((tp, HEAD), F32),
                        pltpu.VMEM((tp, HEAD), F32)],
        compiler_params=_cparams(1),
    )(p, p, cw, cb, wa, ba, wx, bx, lam)


def _lru_bwd(p, hs, dyd, cw, cb, wa, ba, wx, bx, lam):
    tp = p.shape[0]
    nb = D_LRU // HEAD
    ng = tp // SUBLANE

    def body(x_ref, gt_ref, hs_ref, dy_ref, cw_ref, cb_ref, wa_ref, ba_ref, wx_ref, bx_ref, lam_ref,
             dx_ref, dgt_ref, dwa_ref, dwx_ref, dv_ref, xsc, asc, bsc, gsc, dusc):
        xsc[0:LRU_HALO, :] = jnp.zeros((LRU_HALO, HEAD), F32)
        xsc[LRU_HALO:, :] = x_ref[...]
        x_all = xsc[...]
        u, r, i, a, mult, sp = _lru_gates(x_all, tp, cw_ref, cb_ref, wa_ref, ba_ref, wx_ref, bx_ref, lam_ref)
        rows = lax.broadcasted_iota(jnp.int32, (tp, HEAD), 0)
        hs = hs_ref[...]
        dy = dy_ref[...]
        gelu, dgelu = _gelu_parts(gt_ref[...])
        dgt_ref[...] = (dy * hs * dgelu).astype(BF16)
        bb = dy * gelu
        aa = jnp.where(rows == tp - 1, 0.0, pltpu.roll(a, tp - 1, 0))
        sub = rows % SUBLANE
        for k in (1, 2, 4):
            m = sub < SUBLANE - k
            bb = jnp.where(m, aa * pltpu.roll(bb, tp - k, 0) + bb, bb)
            aa = jnp.where(m, aa * pltpu.roll(aa, tp - k, 0), aa)
        asc[...] = aa
        bsc[...] = bb

        def grp(gi, carry):
            g = ng - 1 - gi
            r0 = pl.multiple_of(g * SUBLANE, SUBLANE)
            gg = bsc[pl.ds(r0, SUBLANE), :] + asc[pl.ds(r0, SUBLANE), :] * carry
            gsc[pl.ds(r0, SUBLANE), :] = gg
            return jnp.broadcast_to(gg[0:1, :], (SUBLANE, HEAD))

        lax.fori_loop(0, ng, grp, jnp.zeros((SUBLANE, HEAD), F32))
        g = gsc[...]
        first = rows == 0
        hprev = jnp.where(first, 0.0, pltpu.roll(hs, 1, 0))
        iu = i * u
        d_iu = g * jnp.where(first, 1.0, mult)
        dmult_term = jnp.where(first, 0.0, g * iu * (-(a * a) / mult))
        dla = g * hprev * a + dmult_term
        dr = dla * (-LRU_C) * sp
        dv_ref[0, 7:8, :] = _colsum(dla * (LRU_C * r) * _sigmoid(-lam_ref[...]))
        dpr = dr * r * (1.0 - r)
        dpi = d_iu * u * i * (1.0 - i)
        dv_ref[0, 5:6, :] = _colsum(dpr)
        dv_ref[0, 6:7, :] = _colsum(dpi)
        u16 = u.astype(BF16)
        dpr16 = dpr.astype(BF16)
        dpi16 = dpi.astype(BF16)
        dwa_ref[0] = _dot_tn(u16, dpr16)
        dwx_ref[0] = _dot_tn(u16, dpi16)
        du = d_iu * i + _dot_nt(dpr16, wa_ref[0]) + _dot_nt(dpi16, wx_ref[0])
        dv_ref[0, 4:5, :] = _colsum(du)
        for j in range(LRU_CONV):
            sh = LRU_CONV - 1 - j
            xs = x_all if sh == 0 else pltpu.roll(x_all, sh, 0)
            dv_ref[0, j:j + 1, :] = _colsum(du * xs[LRU_HALO:, :])
        dusc[0:tp, :] = du
        dusc[tp:, :] = jnp.zeros((LRU_HALO, HEAD), F32)
        dx_ref[...] = _conv_taps_t(dusc[...], cw_ref, tp, LRU_HALO, LRU_CONV).astype(BF16)

    col, vec, mat = _lru_specs(tp, nb)
    ocol = lambda: pl.BlockSpec((tp, HEAD), lambda j: (0, j))
    return pl.pallas_call(
        body, name="lru_bwd",
        grid=(nb,),
        in_specs=[col(2), col(3), ocol(), ocol(), pl.BlockSpec((LRU_CONV, HEAD), lambda j: (0, j)), vec(), mat(),
                  vec(), mat(), vec(), vec()],
        out_specs=[ocol(), ocol(), mat(), mat(), pl.BlockSpec((1, SUBLANE, HEAD), lambda j: (j, 0, 0))],
        out_shape=[jax.ShapeDtypeStruct((tp, D_LRU), BF16), jax.ShapeDtypeStruct((tp, D_LRU), BF16),
                   jax.ShapeDtypeStruct((nb, HEAD, HEAD), F32), jax.ShapeDtypeStruct((nb, HEAD, HEAD), F32),
                   jax.ShapeDtypeStruct((nb, SUBLANE, HEAD), F32)],
        scratch_shapes=[pltpu.VMEM((tp + LRU_HALO, HEAD), F32), pltpu.VMEM((tp, HEAD), F32),
                        pltpu.VMEM((tp, HEAD), F32), pltpu.VMEM((tp, HEAD), F32),
                        pltpu.VMEM((tp + LRU_HALO, HEAD), F32)],
        compiler_params=_cparams(1),
    )(p, p, hs, dyd, cw, cb, wa, ba, wx, bx, lam)


def _mesh_pos():
    return lax.axis_index("x"), lax.axis_index("y"), lax.axis_index("c")


def _other_chips(x, y):
    return [(1 - x, y), (x, 1 - y), (1 - x, 1 - y)]


ANY = pl.BlockSpec(memory_space=pl.ANY)


HBM = pl.BlockSpec(memory_space=pltpu.HBM)
SEM = pl.BlockSpec(memory_space=pltpu.SEMAPHORE)
DATAFLOW = pltpu.SideEffectType.DATAFLOW_SIDE_EFFECTING
N_PEERS = 4


def _in_hbm(a):
    return pltpu.with_memory_space_constraint(a, pltpu.HBM)


def _gather_peers(x, y, c):
    return [((ox, oy, c), 2 * ox + oy) for ox, oy in _other_chips(x, y)] + [((x, y, 1 - c), 2 * x + y)]


def _gather_refs(src, land, slot, c, split):
    if not split:
        return src, land.at[slot]
    half = src.shape[0] // 2
    return src.at[pl.ds(c * half, half)], land.at[slot, pl.ds(c * half, half)]


def _gather_start(arrs, split):
    n = len(arrs)

    def body(*refs):
        ins, lands = refs[:n], refs[n:2 * n]
        ssem, rsem = refs[2 * n:2 * n + 2]
        token = refs[-1]
        x, y, c = _mesh_pos()
        chip = 2 * x + y
        for k in range(n):
            for j, (dev, _) in enumerate(_gather_peers(x, y, c)):
                src, dst = _gather_refs(ins[k], lands[k], chip, c, split[k] and j < N_PEERS - 1)
                pltpu.make_async_remote_copy(
                    src_ref=src, dst_ref=dst, send_sem=ssem.at[N_PEERS * k + j],
                    recv_sem=rsem.at[N_PEERS * k + j], device_id=dev, device_id_type=MESH).start()
        token[...] = jnp.zeros_like(token)

    lands = [_in_hbm(lax.empty((N_SHARD,) + a.shape, a.dtype)) for a in arrs]
    out = pl.pallas_call(
        body, name="gather_start",
        in_specs=[HBM] * (2 * n),
        out_specs=[SEM, SEM] + [HBM] * (2 * n) + [pl.BlockSpec(memory_space=pltpu.VMEM)],
        out_shape=[pltpu.SemaphoreType.DMA((N_PEERS * n,)), pltpu.SemaphoreType.DMA((N_PEERS * n,))]
        + [pltpu.HBM(a.shape, a.dtype) for a in arrs]
        + [pltpu.HBM((N_SHARD,) + a.shape, a.dtype) for a in arrs]
        + [jax.ShapeDtypeStruct((SUBLANE, LANE), F32)],
        input_output_aliases={k: 2 + k for k in range(2 * n)},
        compiler_params=pltpu.CompilerParams(has_side_effects=DATAFLOW),
    )(*[_in_hbm(a) for a in arrs], *lands)
    return out[0], out[1], list(out[2:2 + n]), list(out[2 + n:2 + 2 * n]), out[-1]


def _gather_wait(ssem, rsem, srcs, lands, ks, after, split=False):
    n = len(ks)

    def body(*refs):
        ins, lnd = refs[:n], refs[n:2 * n]
        ssem_ref, rsem_ref = refs[2 * n:2 * n + 2]
        x, y, c = _mesh_pos()
        for i, k in enumerate(ks):
            for j, (dev, pchip) in enumerate(_gather_peers(x, y, c)):
                src, dst = _gather_refs(ins[i], lnd[i], pchip, c, split and j < N_PEERS - 1)
                cp = pltpu.make_async_remote_copy(
                    src_ref=src, dst_ref=dst, send_sem=ssem_ref.at[N_PEERS * k + j],
                    recv_sem=rsem_ref.at[N_PEERS * k + j], device_id=dev, device_id_type=MESH)
                cp.wait_send()
                cp.wait_recv()

    out = pl.pallas_call(
        body, name="gather_wait",
        in_specs=[HBM] * (2 * n) + [SEM, SEM] + [ANY] * len(after),
        out_specs=[HBM] * (2 * n),
        out_shape=[pltpu.HBM(a.shape, a.dtype) for a in srcs] + [pltpu.HBM(a.shape, a.dtype) for a in lands],
        input_output_aliases={k: k for k in range(2 * n)},
        compiler_params=pltpu.CompilerParams(has_side_effects=DATAFLOW),
    )(*srcs, *lands, ssem, rsem, *after)
    return list(out[n:])


def _pair_forward(lands):
    n = len(lands)

    def body(*refs):
        outs = refs[n:2 * n]
        ssem, rsem = refs[2 * n:]
        x, y, c = _mesh_pos()
        sibling = (x, y, 1 - c)
        cps = []
        for k in range(n):
            half = lands[k].shape[1] // 2
            for j, (ox, oy) in enumerate(_other_chips(x, y)):
                mine = outs[k].at[2 * ox + oy, pl.ds(c * half, half)]
                cp = pltpu.make_async_remote_copy(src_ref=mine, dst_ref=mine, send_sem=ssem.at[3 * k + j],
                                                  recv_sem=rsem.at[3 * k + j], device_id=sibling, device_id_type=MESH)
                cp.start()
                cps.append(cp)
        for k in range(n):
            half = lands[k].shape[1] // 2
            for j, (ox, oy) in enumerate(_other_chips(x, y)):
                theirs = outs[k].at[2 * ox + oy, pl.ds((1 - c) * half, half)]
                pltpu.make_async_remote_copy(src_ref=theirs, dst_ref=theirs, send_sem=ssem.at[3 * k + j],
                                             recv_sem=rsem.at[3 * k + j], device_id=sibling,
                                             device_id_type=MESH).wait_recv()
        for cp in cps:
            cp.wait_send()

    return pl.pallas_call(
        body, name="pair_forward",
        in_specs=[ANY] * n, out_specs=[ANY] * n,
        out_shape=[jax.ShapeDtypeStruct(a.shape, a.dtype) for a in lands],
        scratch_shapes=[pltpu.SemaphoreType.DMA((3 * n,)), pltpu.SemaphoreType.DMA((3 * n,))],
        input_output_aliases={k: k for k in range(n)},
    )(*lands)


N_SOURCES = 7


def _reduce_peers(x, y, c):
    peers = []
    for ox, oy in _other_chips(x, y):
        for rel in range(2):
            peers.append(((ox, oy, c + rel - 2 * c * rel), 2 * ox + oy))
    peers.append(((x, y, 1 - c), 2 * x + y))
    return peers


def _reduce_start(arrs, slots):
    n = len(arrs)

    def body(*refs):
        ins, lands = refs[:n], refs[n:2 * n]
        ssem, rsem = refs[2 * n:2 * n + 2]
        token = refs[-1]
        x, y, c = _mesh_pos()
        me = 2 * (2 * x + y) + c
        for k in range(n):
            half = arrs[k].shape[1] // 2
            for p, (dev, ochip) in enumerate(_reduce_peers(x, y, c)):
                pltpu.make_async_remote_copy(
                    src_ref=ins[k].at[ochip, pl.ds(dev[2] * half, half)], dst_ref=lands[k].at[me],
                    send_sem=ssem.at[N_SOURCES * k + p], recv_sem=rsem.at[N_SOURCES * k + p],
                    device_id=dev, device_id_type=MESH).start()
        token[...] = jnp.zeros_like(token)

    out = pl.pallas_call(
        body, name="reduce_start",
        in_specs=[HBM] * (2 * n),
        out_specs=[SEM, SEM] + [HBM] * (2 * n) + [pl.BlockSpec(memory_space=pltpu.VMEM)],
        out_shape=[pltpu.SemaphoreType.DMA((N_SOURCES * n,)), pltpu.SemaphoreType.DMA((N_SOURCES * n,))]
        + [pltpu.HBM(a.shape, a.dtype) for a in arrs] + [pltpu.HBM(a.shape, a.dtype) for a in slots]
        + [jax.ShapeDtypeStruct((SUBLANE, LANE), F32)],
        input_output_aliases={k: 2 + k for k in range(2 * n)},
        compiler_params=pltpu.CompilerParams(has_side_effects=DATAFLOW),
    )(*[_in_hbm(a) for a in arrs], *[_in_hbm(a) for a in slots])
    return out[0], out[1], list(out[2:2 + n]), list(out[2 + n:2 + 2 * n]), out[-1]


def _reduce_wait(ssem, rsem, arrs, slots, after):
    n = len(arrs)

    def body(*refs):
        ins, lnd = refs[:n], refs[n:2 * n]
        ssem_ref, rsem_ref = refs[2 * n:2 * n + 2]
        x, y, c = _mesh_pos()
        for k in range(n):
            half = arrs[k].shape[1] // 2
            for p, (dev, ochip) in enumerate(_reduce_peers(x, y, c)):
                cp = pltpu.make_async_remote_copy(
                    src_ref=ins[k].at[ochip, pl.ds(dev[2] * half, half)], dst_ref=lnd[k].at[2 * ochip + dev[2]],
                    send_sem=ssem_ref.at[N_SOURCES * k + p], recv_sem=rsem_ref.at[N_SOURCES * k + p],
                    device_id=dev, device_id_type=MESH)
                cp.wait_send()
                cp.wait_recv()

    out = pl.pallas_call(
        body, name="reduce_wait",
        in_specs=[HBM] * (2 * n) + [SEM, SEM] + [ANY] * len(after),
        out_specs=[HBM] * (2 * n),
        out_shape=[pltpu.HBM(a.shape, a.dtype) for a in arrs] + [pltpu.HBM(a.shape, a.dtype) for a in slots],
        input_output_aliases={k: k for k in range(2 * n)},
        compiler_params=pltpu.CompilerParams(has_side_effects=DATAFLOW),
    )(*arrs, *slots, ssem, rsem, *after)
    return list(out[n:])


def _own_part(arrs, chip, core, me):
    n = len(arrs)
    nb = GRAD_ROW_BLOCKS

    def body(chip_ref, core_ref, me_ref, *refs):
        for k in range(n):
            refs[n + k][...] = refs[k][...]

    def blk(a):
        return (1, a.shape[1] // 2 // nb, a.shape[2])

    grid_spec = pltpu.PrefetchScalarGridSpec(
        num_scalar_prefetch=3, grid=(nb,),
        in_specs=[pl.BlockSpec(blk(a), lambda i, ch, co, me: (ch[0], co[0] * nb + i, 0)) for a in arrs],
        out_specs=[pl.BlockSpec(blk(a), lambda i, ch, co, me: (me[0], i, 0)) for a in arrs])
    return pl.pallas_call(
        body, name="own_part", grid_spec=grid_spec,
        out_shape=[jax.ShapeDtypeStruct((N_DEV, a.shape[1] // 2, a.shape[2]), a.dtype) for a in arrs],
        compiler_params=_cparams(1),
    )(chip, core, me, *arrs)


def _sum_devices(arrs, core):
    n = len(arrs)
    nb = GRAD_ROW_BLOCKS

    def body(c_ref, *refs):
        for k in range(n):
            r = refs[k]
            acc = r[0].astype(F32)
            for dev in range(1, N_DEV):
                acc = acc + r[dev].astype(F32)
            refs[n + k][...] = acc

    grid_spec = pltpu.PrefetchScalarGridSpec(
        num_scalar_prefetch=1, grid=(nb,),
        in_specs=[pl.BlockSpec((N_DEV, a.shape[1] // nb, a.shape[2]), lambda i, c: (0, i, 0)) for a in arrs],
        out_specs=[pl.BlockSpec((a.shape[1] // nb, a.shape[2]), lambda i, c: (c[0] * nb + i, 0)) for a in arrs])
    return pl.pallas_call(
        body, name="sum_devices", grid_spec=grid_spec,
        out_shape=[jax.ShapeDtypeStruct((2 * a.shape[1], a.shape[2]), F32) for a in arrs],
        compiler_params=_cparams(1),
    )(core, *arrs)


def _small_own(v, me):
    m = v.shape[0]

    def body(me_ref, v_ref, o_ref):
        o_ref[0] = v_ref[...]

    grid_spec = pltpu.PrefetchScalarGridSpec(
        num_scalar_prefetch=1, grid=(1,),
        in_specs=[pl.BlockSpec((m, LANE), lambda i, me: (0, 0))],
        out_specs=pl.BlockSpec((1, m, LANE), lambda i, me: (me[0], 0, 0)))
    return pl.pallas_call(
        body, name="small_own", grid_spec=grid_spec,
        out_shape=jax.ShapeDtypeStruct((N_DEV, m, LANE), v.dtype),
        compiler_params=_cparams(1),
    )(me, v)


def _small_start(v, slots):
    def body(v_ref, land, ssem, rsem, v_thru, land_thru, token):
        del v_thru, land_thru
        x, y, c = _mesh_pos()
        me = 2 * (2 * x + y) + c
        for p, (dev, _) in enumerate(_reduce_peers(x, y, c)):
            pltpu.make_async_remote_copy(src_ref=v_ref, dst_ref=land.at[me], send_sem=ssem.at[p],
                                         recv_sem=rsem.at[p], device_id=dev, device_id_type=MESH).start()
        token[...] = jnp.zeros_like(token)

    out = pl.pallas_call(
        body, name="small_start",
        in_specs=[HBM, HBM],
        out_specs=[SEM, SEM, HBM, HBM, pl.BlockSpec(memory_space=pltpu.VMEM)],
        out_shape=[pltpu.SemaphoreType.DMA((N_SOURCES,)), pltpu.SemaphoreType.DMA((N_SOURCES,)),
                   pltpu.HBM(v.shape, v.dtype), pltpu.HBM(slots.shape, slots.dtype),
                   jax.ShapeDtypeStruct((SUBLANE, LANE), F32)],
        input_output_aliases={0: 2, 1: 3},
        compiler_params=pltpu.CompilerParams(has_side_effects=DATAFLOW),
    )(_in_hbm(v), _in_hbm(slots))
    return out


def _small_wait(ssem, rsem, v, slots, after):
    def body(*refs):
        v_ref, land, ssem_ref, rsem_ref = refs[:4]
        x, y, c = _mesh_pos()
        for p, (dev, ochip) in enumerate(_reduce_peers(x, y, c)):
            cp = pltpu.make_async_remote_copy(src_ref=v_ref, dst_ref=land.at[2 * ochip + dev[2]],
                                              send_sem=ssem_ref.at[p], recv_sem=rsem_ref.at[p],
                                              device_id=dev, device_id_type=MESH)
            cp.wait_send()
            cp.wait_recv()

    out = pl.pallas_call(
        body, name="small_wait",
        in_specs=[HBM, HBM, SEM, SEM] + [ANY] * len(after),
        out_specs=[HBM, HBM],
        out_shape=[pltpu.HBM(v.shape, v.dtype), pltpu.HBM(slots.shape, slots.dtype)],
        input_output_aliases={0: 0, 1: 1},
        compiler_params=pltpu.CompilerParams(has_side_effects=DATAFLOW),
    )(v, slots, ssem, rsem, *after)
    return out[1]


GRAD_ROW_BLOCKS = 2


def _pair_allgather_halves(arrs):
    n = len(arrs)

    def body(*refs):
        outs = refs[n:2 * n]
        ssem, rsem = refs[2 * n:]
        x, y, c = _mesh_pos()
        cps = []
        for k in range(n):
            h = arrs[k].shape[0] // 2
            mine = outs[k].at[pl.ds(c * h, h)]
            cp = pltpu.make_async_remote_copy(src_ref=mine, dst_ref=mine, send_sem=ssem.at[k],
                                              recv_sem=rsem.at[k], device_id=(x, y, 1 - c), device_id_type=MESH)
            cp.start()
            cps.append(cp)
        for k, cp in enumerate(cps):
            h = arrs[k].shape[0] // 2
            theirs = outs[k].at[pl.ds((1 - c) * h, h)]
            pltpu.make_async_remote_copy(src_ref=theirs, dst_ref=theirs, send_sem=ssem.at[k], recv_sem=rsem.at[k],
                                         device_id=(x, y, 1 - c), device_id_type=MESH).wait_recv()
            cp.wait_send()

    return pl.pallas_call(
        body, name="pair_allgather_halves",
        in_specs=[ANY] * n, out_specs=[ANY] * n,
        out_shape=[jax.ShapeDtypeStruct(a.shape, a.dtype) for a in arrs],
        scratch_shapes=[pltpu.SemaphoreType.DMA((n,)), pltpu.SemaphoreType.DMA((n,))],
        input_output_aliases={k: k for k in range(n)},
    )(*arrs)


N_DEV = 8


def _adamw_math(w, g, m, v):
    m2 = ADAM_B1 * m + (1.0 - ADAM_B1) * g
    v2 = ADAM_B2 * v + (1.0 - ADAM_B2) * (g * g)
    m_hat = m2 / (1.0 - ADAM_B1 ** ADAM_STEP)
    v_hat = v2 / (1.0 - ADAM_B2 ** ADAM_STEP)
    delta = -ADAM_LR * (m_hat / (jnp.sqrt(v_hat) + ADAM_EPS) + ADAM_WD * w)
    return delta, m2, v2


def _adamw(w, m, v, gs, nblk):
    nl, r, n = w.shape
    assert nl == len(gs) and nl in (1, 2)
    br = r // nblk

    def body(w_ref, m_ref, v_ref, *rest):
        g_refs, (go_ref, d_ref, mo_ref, vo_ref) = rest[:nl], rest[nl:]
        g = g_refs[0][...]
        if nl == 2:
            g = jnp.where(pl.program_id(0) == 0, g, g_refs[1][...])
        delta, m2, v2 = _adamw_math(w_ref[0], g, m_ref[0], v_ref[0])
        go_ref[0] = g
        d_ref[0] = delta
        mo_ref[0] = m2
        vo_ref[0] = v2

    spec = pl.BlockSpec((1, br, n), lambda l, i: (l, i, 0))
    g_specs = [pl.BlockSpec((br, n), lambda l, i: (i, 0))] if nl == 1 else [
        pl.BlockSpec((br, n), lambda l, i: (jnp.where(l == 0, i, nblk - 1), 0)),
        pl.BlockSpec((br, n), lambda l, i: (jnp.where(l == 1, i, 0), 0))]
    return pl.pallas_call(
        body, name="adamw", grid=(nl, nblk),
        in_specs=[spec, spec, spec] + g_specs,
        out_specs=[spec] * 4,
        out_shape=[jax.ShapeDtypeStruct((nl, r, n), F32)] * 4,
        compiler_params=_cparams(2),
    )(w, m, v, *gs)


def _small_reduce_adamw(parts, w, m, v, rep_rows, sh_rows):
    mrows = rep_rows + N_SHARD * sh_rows + LOSS_ROWS

    def body(p_ref, w_ref, m_ref, v_ref, go_ref, d_ref, mo_ref, vo_ref, loss_ref):
        x, y, _ = _mesh_pos()
        mine = rep_rows + (2 * x + y) * sh_rows
        g_rep = p_ref[0:rep_rows, :]
        g_sh = p_ref[pl.ds(pl.multiple_of(mine, SUBLANE), sh_rows), :]
        loss = p_ref[mrows - LOSS_ROWS:mrows, :]
        for k in range(1, N_DEV):
            g_rep = g_rep + p_ref[k * mrows:k * mrows + rep_rows, :]
            g_sh = g_sh + p_ref[pl.ds(pl.multiple_of(k * mrows + mine, SUBLANE), sh_rows), :]
            loss = loss + p_ref[(k + 1) * mrows - LOSS_ROWS:(k + 1) * mrows, :]
        g = jnp.concatenate([g_rep, g_sh], axis=0)
        delta, m2, v2 = _adamw_math(w_ref[...], g, m_ref[...], v_ref[...])
        go_ref[...] = g
        d_ref[...] = delta
        mo_ref[...] = m2
        vo_ref[...] = v2
        loss_ref[...] = loss

    return pl.pallas_call(
        body, name="small_reduce_adamw",
        out_shape=[jax.ShapeDtypeStruct((rep_rows + sh_rows, 128), F32)] * 4
        + [jax.ShapeDtypeStruct((LOSS_ROWS, 128), F32)],
        compiler_params=pltpu.CompilerParams(vmem_limit_bytes=VMEM_LIMIT_MB * 1024 * 1024),
    )(parts, w, m, v)


LANE = 128
REP_SPEC = (("ffn1_norm", 16), ("mix_norm", 16), ("ffn2_norm", 16), ("final_norm", 8), ("pool_w", 128),
            ("pool_scale", 8), ("hgrn_lb_logits", 16), ("hgrn_gnorm", 8), ("lru_wa", 256), ("lru_wx", 256))
SH_SPEC = (("meta_tokens", 32), ("conv_w", 32), ("lru_conv_w", 8), ("conv_b", 8), ("conv_ln_g", 8),
           ("conv_ln_b", 8), ("lru_conv_b", 8), ("lru_ba", 8), ("lru_bx", 8), ("lru_lambda", 8))
REP_ROWS = sum(r for _, r in REP_SPEC)
SH_ROWS = sum(r for _, r in SH_SPEC)


def _pack_rows(vals, spec):
    parts = []
    for name, rows in spec:
        flat = vals[name].astype(F32).reshape(-1, LANE)
        if flat.shape[0] < rows:
            flat = jnp.concatenate([flat, jnp.zeros((rows - flat.shape[0], LANE), F32)], axis=0)
        parts.append(flat)
    return jnp.concatenate(parts, axis=0)


def _unpack_rows(packed, spec, shapes):
    out = {}
    off = 0
    for name, rows in spec:
        shp = shapes[name]
        n = int(np.prod(shp)) // LANE
        out[name] = packed[off:off + n].reshape(shp)
        off += rows
    return out


def _block_diag(blocks):
    n, b, _ = blocks.shape
    return sum(jnp.pad(blocks[g], ((g * b, (n - 1 - g) * b), (g * b, (n - 1 - g) * b))) for g in range(n))


def _diag_blocks(mat, n):
    b = mat.shape[0] // n
    return jnp.stack([mat[g * b:(g + 1) * b, g * b:(g + 1) * b] for g in range(n)])


BIG = ("ffn1_wg", "ffn1_wu", "ffn2_wg", "ffn2_wu", "ffn1_wd", "ffn2_wd", "w_in_even", "w_out_even",
       "w_in_odd", "w_out_odd")
WEIGHT_NAMES = ('meta_tokens', 'ffn1_norm', 'ffn1_wg', 'ffn1_wu', 'ffn1_wd', 'mix_norm', 'ffn2_norm', 'ffn2_wg',
                'ffn2_wu', 'ffn2_wd', 'w_in_even', 'pool_w', 'pool_scale', 'hgrn_lb_logits', 'hgrn_gnorm',
                'w_out_even', 'w_in_odd', 'conv_w', 'conv_b', 'conv_ln_g', 'conv_ln_b', 'lru_conv_w',
                'lru_conv_b', 'lru_wa', 'lru_ba', 'lru_wx', 'lru_bx', 'lru_lambda', 'w_out_odd', 'final_norm')


def _block_diag2(heads):
    nb = heads.shape[0] // 2
    return jnp.stack([_block_diag(heads[2 * j:2 * j + 2]) for j in range(nb)])


def _diag_blocks2(mats):
    return jnp.concatenate([_diag_blocks(mats[j], 2) for j in range(mats.shape[0])], axis=0)


GATHER_GROUPS = (
    (("small", 0),),
    (("ffn1_wg", 0), ("ffn1_wu", 0), ("ffn1_wd", 0)),
    (("w_in_even", 0), ("w_out_even", 0)),
    (("ffn2_wg", 0), ("ffn2_wu", 0), ("ffn2_wd", 0)),
    (("ffn1_wg", 1), ("ffn1_wu", 1), ("ffn1_wd", 1)),
    (("w_in_odd", 0), ("w_out_odd", 0)),
    (("ffn2_wg", 1), ("ffn2_wu", 1), ("ffn2_wd", 1)),
)
ADAM_ROW_BLOCKS = {"ffn1_wg": 2, "ffn1_wu": 2, "ffn2_wg": 2, "ffn2_wu": 2, "ffn1_wd": 2, "ffn2_wd": 2,
                   "w_in_even": 4, "w_out_even": 2, "w_in_odd": 4, "w_out_odd": 2}
TRANSPOSED = ("ffn1_wg", "ffn1_wu", "ffn2_wg", "ffn2_wu", "w_in_even")
SCATTER_DEPTH = 2
LOSS_ROWS = 8
GATHER_SPLIT = (1, 4)


def _unpack_small(sm, shapes):
    per_shard = [_unpack_rows(sm[s], SH_SPEC, shapes) for s in range(N_SHARD)]
    full = {}
    for n, _ in SH_SPEC:
        full[n] = jnp.concatenate([per_shard[s][n].reshape(-1, shapes[n][-1]) for s in range(N_SHARD)], axis=-1)
    full["conv_w"] = jnp.concatenate([full["conv_w"], jnp.zeros((CONV_HALO - CONV_WIDTH, D_CONV), F32)], axis=0)
    return full


def _local_step(x, tgt, w, shapes, fetch, emit, emit_small):
    s_len, d = x.shape
    t_real = s_len + N_META
    tp = -(-t_real // ROW_ALIGN) * ROW_ALIGN
    tm = _tile(tp, 832, ROW_ALIGN)
    tm_small = _tile(tp, 832, 16)
    tr = _tile(tp, 416, SUBLANE)
    tm_wgrad = _wgrad_tiles(tp, 1024)

    def gain(name, layer):
        return w[name][layer:layer + 1]

    pool_wbd = _block_diag(w["pool_w"][0]).astype(BF16)
    pool_scale = w["pool_scale"]
    wa_bd = _block_diag2(w["lru_wa"][0]).astype(BF16)
    wx_bd = _block_diag2(w["lru_wx"][0]).astype(BF16)
    mst, msk, n_lev = _hgrn_consts(HG_CHUNK)

    (sm,) = fetch(0, None)
    sf = _unpack_small(sm, shapes)
    h0 = jnp.concatenate([sf["meta_tokens"], x, jnp.zeros((tp - t_real, d), F32)], axis=0)
    tgt_pad = jnp.concatenate([jnp.zeros((N_META, d), F32), tgt, jnp.zeros((tp - t_real, d), F32)], axis=0)
    f1l0 = fetch(1, h0)
    h1, *s1 = _ffn_fwd(h0, gain("ffn1_norm", 0), *f1l0, tm)
    w_in_even4, w_out_even4 = fetch(2, h1)
    w_out_even = w_out_even4.reshape(d, d)
    even_piece = [(w_in_even4.reshape(D_IN_EVEN, d), (D_IN_EVEN, d), (0, 0))]
    p0, nm0 = _proj_fwd(h1, gain("mix_norm", 0), even_piece, tm_small, wt=True)
    ya = _pool_fwd(p0, pool_wbd, pool_scale, tr)
    yb, states = _hgrn_fwd(p0, w["hgrn_lb_logits"], w["hgrn_gnorm"], mst, msk, n_lev, tm)
    h2 = _out_fwd(h1, ya, yb, w_out_even, tm)
    f2l0 = fetch(3, h2)
    h3, *s2 = _ffn_fwd(h2, gain("ffn2_norm", 0), *f2l0, tm)
    f1l1 = fetch(4, h3)
    h4, *s3 = _ffn_fwd(h3, gain("ffn1_norm", 1), *f1l1, tm)
    w_in_odd4, w_out_odd4 = fetch(5, h4)
    w_out_odd = w_out_odd4.reshape(d, d)
    odd_pieces = [(w_in_odd4, (1, d, D_IN_ODD // N_SHARD), (k, 0, 0)) for k in range(N_SHARD)]
    p1, nm1 = _proj_fwd(h4, gain("mix_norm", 1), odd_pieces, tm_small)
    yc, conv_out = _convmod_fwd(p1, sf["conv_w"], sf["conv_b"], sf["conv_ln_g"], sf["conv_ln_b"], tr)
    lru_args = (sf["lru_conv_w"], sf["lru_conv_b"], wa_bd, sf["lru_ba"], wx_bd, sf["lru_bx"], sf["lru_lambda"])
    yd, hs = _lru_fwd(p1, *lru_args)
    h5 = _out_fwd(h4, yc, yd, w_out_odd, tm)
    f2l1 = fetch(6, h5)
    h6, *s4 = _ffn_fwd(h5, gain("ffn2_norm", 1), *f2l1, tm)
    loss, dh6, dg_final = _loss_bwd(h6, w["final_norm"].reshape(1, d), tgt_pad, t_real, tm)

    def ffn_bwd(dho, h, saved, norm, wts, after=()):
        ga, gb, sa, n = saved
        dh, da, db, dg, dy = _ffn_bwd_act(dho, h, norm, ga, gb, *wts, tm, after)
        return dh, dg, _ffn_bwd_w([(da, n, None), (db, n, None), (sa, dy, None)], tm_wgrad)

    dh5, dg_f2_l1, g = ffn_bwd(dh6, h5, s4, gain("ffn2_norm", 1), f2l1)
    sent = emit((("ffn2_wg", 1), ("ffn2_wu", 1), ("ffn2_wd", 1)), g)
    dyc, dyd, dw_out_odd = _out_bwd(dh5, yc, yd, w_out_odd, tm, tuple(sent))
    dca, dcb, dconv_w, dconv_vec = _convmod_bwd(p1, conv_out, dyc, sf["conv_w"], sf["conv_ln_g"],
                                                sf["conv_ln_b"], tr)
    dlx, dlg, dwa_bd, dwx_bd, dlru_vec = _lru_bwd(p1, hs, dyd, *lru_args)
    dp1 = [dca, dcb, dlx, dlg]
    dh4, dg_mix_l1 = _proj_bwd_act(dh5, h4, gain("mix_norm", 1), dp1, odd_pieces, tm_small)
    dw_in_odd = jnp.stack(_proj_bwd_w(nm1, dp1, tm))
    dh3, dg_f1_l1, g = ffn_bwd(dh4, h3, s3, gain("ffn1_norm", 1), f1l1)
    sent = emit((("w_out_odd", 0), ("w_in_odd", 0), ("ffn1_wg", 1), ("ffn1_wu", 1), ("ffn1_wd", 1)),
                [dw_out_odd.reshape(N_SHARD, d // N_SHARD, d), dw_in_odd] + list(g))
    dh2, dg_f2_l0, g = ffn_bwd(dh3, h2, s2, gain("ffn2_norm", 0), f2l0, tuple(sent))
    sent = emit((("ffn2_wg", 0), ("ffn2_wu", 0), ("ffn2_wd", 0)), g)
    dya, dyb, dw_out_even = _out_bwd(dh2, ya, yb, w_out_even, tm, tuple(sent))
    dpool, dpool_wbd, dpool_scale = _pool_bwd(p0, dya, pool_wbd, pool_scale, tr)
    dq, dz, dv, dgate, dlb_logits, dgn_heads = _hgrn_bwd(p0, dyb, states, w["hgrn_lb_logits"], w["hgrn_gnorm"],
                                                         mst, msk, n_lev, tm)
    dp0 = [dpool, dq, dz, dv, dgate]
    dh1, dg_mix_l0 = _proj_bwd_act(dh2, h1, gain("mix_norm", 0), dp0, even_piece, tm_small, wt=True)
    dw_in_even_t = jnp.concatenate(_proj_bwd_w(nm0, dp0, tm_small, wt=True), axis=0)
    ga, gb, sa, n1 = s1
    (dwd_f1l0,) = _ffn_bwd_w([(sa, dh1, 0.5)], tm_wgrad)
    sent = emit((("w_out_even", 0), ("w_in_even", 0), ("ffn1_wd", 0)),
                [dw_out_even.reshape(N_SHARD, d // N_SHARD, d),
                 dw_in_even_t.reshape(N_SHARD, D_IN_EVEN // N_SHARD, d), dwd_f1l0])
    dh0, da, db, dg_f1_l0, _ = _ffn_bwd_act(dh1, h0, gain("ffn1_norm", 0), ga, gb, *f1l0, tm, tuple(sent))

    grad_x = dh0[N_META:t_real]
    rep = {
        "ffn1_norm": jnp.concatenate([dg_f1_l0, dg_f1_l1], axis=0),
        "mix_norm": jnp.concatenate([dg_mix_l0, dg_mix_l1], axis=0),
        "ffn2_norm": jnp.concatenate([dg_f2_l0, dg_f2_l1], axis=0),
        "final_norm": dg_final,
        "pool_w": _diag_blocks(dpool_wbd, len(POOL_WINDOWS)),
        "pool_scale": dpool_scale,
        "hgrn_lb_logits": dlb_logits,
        "hgrn_gnorm": jnp.sum(dgn_heads, axis=0),
        "lru_wa": _diag_blocks2(dwa_bd),
        "lru_wx": _diag_blocks2(dwx_bd),
    }
    dmeta = jnp.transpose(dh0[:N_META].reshape(N_META, N_SHARD, 2, LANE), (1, 0, 2, 3)).reshape(N_SHARD, 32, LANE)
    packs = [_pack_rows(rep, REP_SPEC)]
    for s in range(N_SHARD):
        sh = {
            "meta_tokens": dmeta[s], "conv_w": dconv_w[s], "lru_conv_w": dlru_vec[s, 0:4],
            "conv_b": dconv_vec[s, 0:1], "conv_ln_g": dconv_vec[s, 1:2], "conv_ln_b": dconv_vec[s, 2:3],
            "lru_conv_b": dlru_vec[s, 4:5], "lru_ba": dlru_vec[s, 5:6], "lru_bx": dlru_vec[s, 6:7],
            "lru_lambda": dlru_vec[s, 7:8],
        }
        packs.append(_pack_rows(sh, SH_SPEC))
    packs.append(jnp.pad(loss, ((0, LOSS_ROWS - 1), (0, LANE - 1))))
    sent = emit_small(jnp.concatenate(packs, axis=0))
    emit((("ffn1_wg", 0), ("ffn1_wu", 0)), _ffn_bwd_w([(da, n1, None), (db, n1, None)], tm_wgrad, tuple(sent)))
    return grad_x


def kernel(x, meta_tokens, ffn1_norm, ffn1_wg, ffn1_wu, ffn1_wd, mix_norm, ffn2_norm, ffn2_wg, ffn2_wu, ffn2_wd, w_in_even, pool_w, pool_scale, hgrn_lb_logits, hgrn_gnorm, w_out_even, w_in_odd, conv_w, conv_b, conv_ln_g, conv_ln_b, lru_conv_w, lru_conv_b, lru_wa, lru_ba, lru_wx, lru_bx, lru_lambda, w_out_odd, final_norm, loss_target, m_meta_tokens, m_ffn1_norm, m_ffn1_wg, m_ffn1_wu, m_ffn1_wd, m_mix_norm, m_ffn2_norm, m_ffn2_wg, m_ffn2_wu, m_ffn2_wd, m_w_in_even, m_pool_w, m_pool_scale, m_hgrn_lb_logits, m_hgrn_gnorm, m_w_out_even, m_w_in_odd, m_conv_w, m_conv_b, m_conv_ln_g, m_conv_ln_b, m_lru_conv_w, m_lru_conv_b, m_lru_wa, m_lru_ba, m_lru_wx, m_lru_bx, m_lru_lambda, m_w_out_odd, m_final_norm, v_meta_tokens, v_ffn1_norm, v_ffn1_wg, v_ffn1_wu, v_ffn1_wd, v_mix_norm, v_ffn2_norm, v_ffn2_wg, v_ffn2_wu, v_ffn2_wd, v_w_in_even, v_pool_w, v_pool_scale, v_hgrn_lb_logits, v_hgrn_gnorm, v_w_out_even, v_w_in_odd, v_conv_w, v_conv_b, v_conv_ln_g, v_conv_ln_b, v_lru_conv_w, v_lru_conv_b, v_lru_wa, v_lru_ba, v_lru_wx, v_lru_bx, v_lru_lambda, v_w_out_odd, v_final_norm):
    args = locals()
    w = {n: args[n] for n in WEIGHT_NAMES}
    m = {n: args["m_" + n] for n in WEIGHT_NAMES}
    v = {n: args["v_" + n] for n in WEIGHT_NAMES}
    shapes = {n: w[n].shape for n in WEIGHT_NAMES}
    core = lax.axis_index("c").astype(jnp.int32).reshape(1)
    chip = (2 * lax.axis_index("x") + lax.axis_index("y")).astype(jnp.int32).reshape(1)
    me = 2 * chip + core

    def view(a, n):
        return jnp.swapaxes(a, 1, 2) if n in TRANSPOSED else a

    wv, mv, vv = [{n: view(src[n], n) for n in BIG} for src in (w, m, v)]

    def shard(key):
        n, l = key
        return _pack_rows(w, SH_SPEC) if n == "small" else wv[n][l].astype(BF16)

    started = {}
    for groups in (GATHER_GROUPS[:2], GATHER_GROUPS[2:]):
        gkeys = [key for grp in groups for key in grp]
        ssem, rsem, srcs, lands, token = _gather_start([shard(key) for key in gkeys],
                                                       [any(key in GATHER_GROUPS[g] for g in GATHER_SPLIT)
                                                        for key in gkeys])
        for k, key in enumerate(gkeys):
            started[key] = (ssem, rsem, srcs[k], lands[k], k, token)

    def pack_small(src):
        return jnp.concatenate([_pack_rows(src, REP_SPEC), _pack_rows(src, SH_SPEC)], axis=0)

    small_packs = [pack_small(src) for src in (w, m, v)]

    def fetch(group, after):
        st = [started[key] for key in GATHER_GROUPS[group]]
        deps = (st[0][5],) if after is None else (after,)
        if group == 1:
            deps += (started[GATHER_GROUPS[2][0]][5],) + tuple(small_packs)
        split = group in GATHER_SPLIT
        got = _gather_wait(st[0][0], st[0][1], [s[2] for s in st], [s[3] for s in st], [s[4] for s in st], deps,
                           split)
        return _pair_forward(got) if split else got

    in_flight, reduced = [], {}

    def collect(entry, after):
        gkeys, gs_sem, gr_sem, grads_thru, slots_thru, _ = entry
        slots = _reduce_wait(gs_sem, gr_sem, grads_thru, slots_thru, after)
        full = _pair_allgather_halves(_sum_devices(slots, core))
        reduced.update(zip(gkeys, full))
        return full[0]

    def emit(gkeys, grads):
        grads = list(grads)
        in_flight.append((gkeys,) + tuple(_reduce_start(grads, _own_part(grads, chip, core, me))))
        token = in_flight[-1][-1]
        if len(in_flight) > SCATTER_DEPTH:
            return token, collect(in_flight[-1 - SCATTER_DEPTH], (token,))
        return (token,)

    small_flight = []

    def emit_small(part):
        small_flight.append(_small_start(part, _small_own(part, me)))
        return (small_flight[0][4],)

    grad_x = _local_step(x[0], loss_target[0], w, shapes, fetch, emit, emit_small)

    out_g, out_d, out_m, out_v = {}, {}, {}, {}
    deps = (in_flight[-1][-1],)

    def adamw_ready():
        done = ()
        for n in BIG:
            layers = range(shapes[n][0])
            if n not in out_g and all((n, l) in reduced for l in layers):
                res = _adamw(wv[n], mv[n], vv[n], [reduced[(n, l)] for l in layers], ADAM_ROW_BLOCKS[n])
                out_g[n], out_d[n], out_m[n], out_v[n] = [view(r, n) for r in res]
                done += (res[1],)
        return done

    deps += adamw_ready()
    for entry in in_flight[-SCATTER_DEPTH:-1]:
        collect(entry, deps)
        deps += adamw_ready()
    s_ssem, s_rsem, s_part, s_slots, _ = small_flight[0]
    small_all = _small_wait(s_ssem, s_rsem, s_part, s_slots, deps)
    small_res = _small_reduce_adamw(small_all.reshape(-1, LANE), *small_packs, REP_ROWS, SH_ROWS)
    collect(in_flight[-1], deps + (small_res[0],))
    adamw_ready()

    loss = small_res[4][0, 0]
    for dst, packed in zip((out_g, out_d, out_m, out_v), small_res[:4]):
        dst.update(_unpack_rows(packed[:REP_ROWS], REP_SPEC, shapes))
        dst.update(_unpack_rows(packed[REP_ROWS:], SH_SPEC, shapes))

    return (loss, grad_x[None], *[out_g[n] for n in WEIGHT_NAMES], *[out_d[n] for n in WEIGHT_NAMES],
            *[out_m[n] for n in WEIGHT_NAMES], *[out_v[n] for n in WEIGHT_NAMES])
```

```python
import functools

import numpy as np
import jax
import jax.numpy as jnp
from jax import lax
from jax.experimental import pallas as pl
from jax.experimental.pallas import tpu as pltpu

F32 = jnp.float32
BF16 = jnp.bfloat16
MESH = pl.DeviceIdType.MESH

EPS = 1e-6
N_META = 16
D_FF = 2816
N_SHARD = 4
FF_SHARD = D_FF // N_SHARD
D_POOL = 256
POOL_GROUP = 64
POOL_WINDOWS = (2, 4, 8, 16)
D_HGRN = 768
HG_HEADS = 6
HEAD = 128
HG_CHUNK = 64
HG_HEADS_PER_STEP = 6
FFN_ROW_BLOCKS = 2
HG_PBLOCK = 256
D_IN_EVEN = D_POOL + 4 * D_HGRN
D_CONV = 512
CONV_WIDTH = 31
CONV_HALO = 32
D_LRU = 512
LRU_CONV = 4
LRU_HALO = 8
LRU_C = 8.0
D_IN_ODD = 2 * D_CONV + 2 * D_LRU
SUBLANE = 8
MXU_DEPTH = 256
ROW_ALIGN = 64

ADAM_LR = 0.001
ADAM_B1 = 0.9
ADAM_B2 = 0.999
ADAM_EPS = 1e-08
ADAM_WD = 0.01
ADAM_STEP = 10

VMEM_LIMIT_MB = 56


def _cparams(n_grid_axes=0, vmem_mb=VMEM_LIMIT_MB):
    sem = ("arbitrary",) * n_grid_axes if n_grid_axes else None
    return pltpu.CompilerParams(dimension_semantics=sem, vmem_limit_bytes=vmem_mb * 1024 * 1024)


def _tile(n, target, mult):
    best = None
    for t in range(mult, min(n, target) + 1, mult):
        if n % t == 0:
            best = t
    assert best is not None, (n, target, mult)
    return best


def _dot(a, b):
    return jnp.dot(a, b, preferred_element_type=F32)


def _dot_nt(a, b):
    return lax.dot_general(a, b, (((1,), (1,)), ((), ())), preferred_element_type=F32)


def _dot_tn(a, b):
    return lax.dot_general(a, b, (((0,), (0,)), ((), ())), preferred_element_type=F32)


def _sigmoid(x):
    return 1.0 / (1.0 + jnp.exp(-x))


def _colsum(x):
    return jnp.sum(x, axis=0, keepdims=True)


def _rms_stats(h):
    rstd = lax.rsqrt(jnp.mean(h * h, axis=-1, keepdims=True) + EPS)
    return rstd, h * rstd


def _rms_bwd(dn, g, rstd, xhat):
    dng = dn * g
    dh = rstd * (dng - xhat * jnp.mean(dng * xhat, axis=-1, keepdims=True))
    return dh, _colsum(dn * xhat)


def _ffn_fwd(h, norm, wg4, wu4, wd4, tm):
    tp, d = h.shape
    nt = tp // tm

    def body(h_ref, g_ref, wg_ref, wu_ref, wd_ref, ho_ref, ga_ref, gb_ref, sa_ref, n_ref, n_sc, acc):
        s = pl.program_id(1)

        @pl.when(s == 0)
        def _():
            hh = h_ref[...]
            rstd, xhat = _rms_stats(hh)
            n = (xhat * g_ref[...]).astype(BF16)
            n_sc[...] = n
            n_ref[...] = n
            acc[...] = jnp.zeros_like(acc)

        n = n_sc[...]
        a = _dot_nt(n, wg_ref[0])
        b = _dot_nt(n, wu_ref[0])
        sig = _sigmoid(a)
        sil = a * sig
        ga_ref[0] = (sig * (1.0 + a * (1.0 - sig)) * b).astype(BF16)
        gb_ref[0] = sil.astype(BF16)
        sg = (sil * b).astype(BF16)
        sa_ref[0] = sg
        acc[...] += _dot(sg, wd_ref[0])

        @pl.when(s == N_SHARD - 1)
        def _():
            ho_ref[...] = h_ref[...] + 0.5 * acc[...]

    return pl.pallas_call(
        body, name="ffn_fwd",
        grid=(nt, N_SHARD),
        in_specs=[
            pl.BlockSpec((tm, d), lambda i, s: (i, 0)),
            pl.BlockSpec((1, d), lambda i, s: (0, 0)),
            pl.BlockSpec((1, FF_SHARD, d), lambda i, s: (s, 0, 0)),
            pl.BlockSpec((1, FF_SHARD, d), lambda i, s: (s, 0, 0)),
            pl.BlockSpec((1, FF_SHARD, d), lambda i, s: (s, 0, 0)),
        ],
        out_specs=[
            pl.BlockSpec((tm, d), lambda i, s: (i, 0)),
            pl.BlockSpec((1, tm, FF_SHARD), lambda i, s: (s, i, 0)),
            pl.BlockSpec((1, tm, FF_SHARD), lambda i, s: (s, i, 0)),
            pl.BlockSpec((1, tm, FF_SHARD), lambda i, s: (s, i, 0)),
            pl.BlockSpec((tm, d), lambda i, s: (i, 0)),
        ],
        out_shape=[
            jax.ShapeDtypeStruct((tp, d), F32),
            jax.ShapeDtypeStruct((N_SHARD, tp, FF_SHARD), BF16),
            jax.ShapeDtypeStruct((N_SHARD, tp, FF_SHARD), BF16),
            jax.ShapeDtypeStruct((N_SHARD, tp, FF_SHARD), BF16),
            jax.ShapeDtypeStruct((tp, d), BF16),
        ],
        scratch_shapes=[pltpu.VMEM((tm, d), BF16), pltpu.VMEM((tm, d), F32)],
        compiler_params=_cparams(2),
    )(h, norm, wg4, wu4, wd4)


def _ffn_bwd_act(dho, h, norm, ga4, gb4, wg4, wu4, wd4, tm, after=()):
    tp, d = h.shape
    nt = tp // tm

    def body(dho_ref, h_ref, g_ref, ga_ref, gb_ref, wg_ref, wu_ref, wd_ref, *rest):
        dh_ref, da_ref, db_ref, dg_ref, dy_ref, dn_sc = rest[len(after):]
        i = pl.program_id(0)
        s = pl.program_id(1)

        @pl.when(s == 0)
        def _():
            dy_ref[...] = (0.5 * dho_ref[...]).astype(BF16)
            dn_sc[...] = jnp.zeros_like(dn_sc)

        @pl.when((s == 0) & (i == 0))
        def _():
            dg_ref[...] = jnp.zeros_like(dg_ref)

        hb = tm // FFN_ROW_BLOCKS
        rows = [pl.ds(k * hb, hb) for k in range(FFN_ROW_BLOCKS)]

        def shard_step(last):
            ds = _dot_nt(dy_ref[rows[0], :], wd_ref[0])
            dg = None
            for k in range(FFN_ROW_BLOCKS):
                ds_next = _dot_nt(dy_ref[rows[k + 1], :], wd_ref[0]) if k + 1 < FFN_ROW_BLOCKS else None
                da = (ds * ga_ref[0, rows[k], :].astype(F32)).astype(BF16)
                db = (ds * gb_ref[0, rows[k], :].astype(F32)).astype(BF16)
                da_ref[0, rows[k], :] = da
                db_ref[0, rows[k], :] = db
                dn = dn_sc[rows[k], :] + (_dot(da, wg_ref[0]) + _dot(db, wu_ref[0]))
                if last:
                    rstd, xhat = _rms_stats(h_ref[rows[k], :])
                    dh, dgk = _rms_bwd(dn, g_ref[...], rstd, xhat)
                    dh_ref[rows[k], :] = dho_ref[rows[k], :] + dh
                    dg = dgk if dg is None else dg + dgk
                else:
                    dn_sc[rows[k], :] = dn
                ds = ds_next
            if last:
                dg_ref[...] += dg

        pl.when(s < N_SHARD - 1)(functools.partial(shard_step, False))
        pl.when(s == N_SHARD - 1)(functools.partial(shard_step, True))

    return pl.pallas_call(
        body, name="ffn_bwd_act",
        grid=(nt, N_SHARD),
        in_specs=[
            pl.BlockSpec((tm, d), lambda i, s: (i, 0)),
            pl.BlockSpec((tm, d), lambda i, s: (i, 0)),
            pl.BlockSpec((1, d), lambda i, s: (0, 0)),
            pl.BlockSpec((1, tm, FF_SHARD), lambda i, s: (s, i, 0)),
            pl.BlockSpec((1, tm, FF_SHARD), lambda i, s: (s, i, 0)),
            pl.BlockSpec((1, FF_SHARD, d), lambda i, s: (s, 0, 0)),
            pl.BlockSpec((1, FF_SHARD, d), lambda i, s: (s, 0, 0)),
            pl.BlockSpec((1, FF_SHARD, d), lambda i, s: (s, 0, 0)),
        ] + [pl.BlockSpec(memory_space=pl.ANY)] * len(after),
        out_specs=[
            pl.BlockSpec((tm, d), lambda i, s: (i, 0)),
            pl.BlockSpec((1, tm, FF_SHARD), lambda i, s: (s, i, 0)),
            pl.BlockSpec((1, tm, FF_SHARD), lambda i, s: (s, i, 0)),
            pl.BlockSpec((1, d), lambda i, s: (0, 0)),
            pl.BlockSpec((tm, d), lambda i, s: (i, 0)),
        ],
        out_shape=[
            jax.ShapeDtypeStruct((tp, d), F32),
            jax.ShapeDtypeStruct((N_SHARD, tp, FF_SHARD), BF16),
            jax.ShapeDtypeStruct((N_SHARD, tp, FF_SHARD), BF16),
            jax.ShapeDtypeStruct((1, d), F32),
            jax.ShapeDtypeStruct((tp, d), BF16),
        ],
        scratch_shapes=[pltpu.VMEM((tm, d), F32)],
        compiler_params=_cparams(2),
    )(dho, h, norm, ga4, gb4, wg4, wu4, wd4, *after)


def _wgrad_tiles(tp, target):
    tm = min(target, tp) // MXU_DEPTH * MXU_DEPTH
    nt = tp // tm
    tail = tp - nt * tm
    assert tail == 0 or (nt * tm) % tail == 0 and tail % 16 == 0, (tp, tm, tail)
    return tm, nt, tail


def _ffn_bwd_w(pairs, tiles, after=()):
    npair = len(pairs)
    tp, d = pairs[0][1].shape
    tm, nt, tail = tiles
    rhs_list = []
    for _, rhs, _ in pairs:
        if all(rhs is not r for r in rhs_list):
            rhs_list.append(rhs)
    rhs_of = [[rhs is r for r in rhs_list].index(True) for _, rhs, _ in pairs]
    nrhs = len(rhs_list)
    nin = nrhs + npair
    ntail = nin if tail else 0

    def body(*refs):
        rhs_refs = refs[:nrhs]
        lhs_refs = refs[nrhs:nin]
        rhs_tails = refs[nin:nin + ntail][:nrhs]
        lhs_tails = refs[nin:nin + ntail][nrhs:]
        rest = refs[nin + ntail + len(after):]
        out_refs, accs = rest[:npair], rest[npair:]
        i = pl.program_id(1)

        @pl.when(i == 0)
        def _():
            for acc in accs:
                acc[...] = jnp.zeros_like(acc)

        def accumulate(lhs, rhs):
            for k, (_, _, scale) in enumerate(pairs):
                r = rhs[rhs_of[k]][...]
                if scale is not None:
                    r = (scale * r).astype(BF16)
                accs[k][...] += _dot_tn(lhs[k][0], r)

        accumulate(lhs_refs, rhs_refs)

        @pl.when(i == nt - 1)
        def _():
            if tail:
                accumulate(lhs_tails, rhs_tails)
            for k in range(npair):
                out_refs[k][0] = accs[k][...].astype(BF16)

    tail_specs, tail_args = [], []
    if tail:
        tb = nt * tm // tail
        tail_specs = ([pl.BlockSpec((tail, d), lambda s, i: (tb, 0))] * nrhs
                      + [pl.BlockSpec((1, tail, FF_SHARD), lambda s, i: (s, tb, 0))] * npair)
        tail_args = [*rhs_list, *[lhs for lhs, _, _ in pairs]]
    return pl.pallas_call(
        body, name="ffn_bwd_w",
        grid=(N_SHARD, nt),
        in_specs=[pl.BlockSpec((tm, d), lambda s, i: (i, 0))] * nrhs
        + [pl.BlockSpec((1, tm, FF_SHARD), lambda s, i: (s, i, 0))] * npair
        + tail_specs
        + [pl.BlockSpec(memory_space=pl.ANY)] * len(after),
        out_specs=[pl.BlockSpec((1, FF_SHARD, d), lambda s, i: (s, 0, 0))] * npair,
        out_shape=[jax.ShapeDtypeStruct((N_SHARD, FF_SHARD, d), BF16)] * npair,
        scratch_shapes=[pltpu.VMEM((FF_SHARD, d), F32)] * npair,
        compiler_params=_cparams(2),
    )(*rhs_list, *[lhs for lhs, _, _ in pairs], *tail_args, *after)


def _proj_fwd(h, norm, w_pieces, tm, wt=False):
    tp, d = h.shape
    widths = [bs[-2] if wt else bs[-1] for _, bs, _ in w_pieces]
    ntot = sum(widths)
    npc = len(w_pieces)

    def body(*refs):
        h_ref, g_ref = refs[:2]
        w_refs = refs[2:2 + npc]
        p_ref, n_ref = refs[2 + npc:]
        rstd, xhat = _rms_stats(h_ref[...])
        n = (xhat * g_ref[...]).astype(BF16)
        n_ref[...] = n
        off = 0
        for k in range(npc):
            w = w_refs[k][...]
            w = w.reshape(w.shape[-2], w.shape[-1])
            p_ref[:, off:off + widths[k]] = _dot_nt(n, w) if wt else _dot(n, w)
            off += widths[k]

    in_specs = [pl.BlockSpec((tm, d), lambda i: (i, 0)), pl.BlockSpec((1, d), lambda i: (0, 0))]
    for _, bs, idx in w_pieces:
        in_specs.append(pl.BlockSpec(bs, functools.partial(lambda i, idx: idx, idx=idx)))
    return pl.pallas_call(
        body, name="proj_fwd",
        grid=(tp // tm,),
        in_specs=in_specs,
        out_specs=[pl.BlockSpec((tm, ntot), lambda i: (i, 0)), pl.BlockSpec((tm, d), lambda i: (i, 0))],
        out_shape=[jax.ShapeDtypeStruct((tp, ntot), F32), jax.ShapeDtypeStruct((tp, d), BF16)],
        compiler_params=_cparams(1),
    )(h, norm, *[w for w, _, _ in w_pieces])


def _proj_bwd_act(dres, h, norm, dp_pieces, w_pieces, tm, wt=False):
    tp, d = h.shape
    npc = len(dp_pieces)
    nw = len(w_pieces)
    assert nw == npc or (nw == 1 and wt)

    def body(*refs):
        dres_ref, h_ref, g_ref = refs[:3]
        dp_refs = refs[3:3 + npc]
        w_refs = refs[3 + npc:3 + npc + nw]
        dh_ref, dg_ref = refs[3 + npc + nw:]
        i = pl.program_id(0)

        @pl.when(i == 0)
        def _():
            dg_ref[...] = jnp.zeros_like(dg_ref)

        def matmuls(rows):
            dn = None
            off = 0
            for k in range(npc):
                if nw == npc:
                    w = w_refs[k][...]
                    w = w.reshape(w.shape[-2], w.shape[-1])
                else:
                    w = w_refs[0][off:off + dp_pieces[k].shape[1], :]
                    off += dp_pieces[k].shape[1]
                t = _dot(dp_refs[k][rows, :], w) if wt else _dot_nt(dp_refs[k][rows, :], w)
                dn = t if dn is None else dn + t
            return dn

        hb = tm // FFN_ROW_BLOCKS
        blocks = [pl.ds(k * hb, hb) for k in range(FFN_ROW_BLOCKS)]
        dn = matmuls(blocks[0])
        dg = None
        for k, rows in enumerate(blocks):
            dn_next = matmuls(blocks[k + 1]) if k + 1 < FFN_ROW_BLOCKS else None
            rstd, xhat = _rms_stats(h_ref[rows, :])
            dh, dgk = _rms_bwd(dn, g_ref[...], rstd, xhat)
            dh_ref[rows, :] = dres_ref[rows, :] + dh
            dg = dgk if dg is None else dg + dgk
            dn = dn_next
        dg_ref[...] += dg

    in_specs = [pl.BlockSpec((tm, d), lambda i: (i, 0)), pl.BlockSpec((tm, d), lambda i: (i, 0)),
                pl.BlockSpec((1, d), lambda i: (0, 0))]
    for dp in dp_pieces:
        in_specs.append(pl.BlockSpec((tm, dp.shape[1]), lambda i: (i, 0)))
    for _, bs, idx in w_pieces:
        in_specs.append(pl.BlockSpec(bs, functools.partial(lambda i, idx: idx, idx=idx)))
    return pl.pallas_call(
        body, name="proj_bwd_act",
        grid=(tp // tm,),
        in_specs=in_specs,
        out_specs=[pl.BlockSpec((tm, d), lambda i: (i, 0)), pl.BlockSpec((1, d), lambda i: (0, 0))],
        out_shape=[jax.ShapeDtypeStruct((tp, d), F32), jax.ShapeDtypeStruct((1, d), F32)],
        compiler_params=_cparams(1),
    )(dres, h, norm, *dp_pieces, *[w for w, _, _ in w_pieces])


def _proj_bwd_w(n, dp_pieces, tiles, wt=False):
    tp, d = n.shape
    tm, nt, tail = tiles
    npc = len(dp_pieces)
    widths = [dp.shape[1] for dp in dp_pieces]
    oshape = (lambda w: (w, d)) if wt else (lambda w: (d, w))
    nin = 1 + npc
    ntail = nin if tail else 0

    def body(*refs):
        o_refs = refs[nin + ntail:nin + ntail + npc]
        accs = refs[nin + ntail + npc:]
        i = pl.program_id(0)

        @pl.when(i == 0)
        def _():
            for acc in accs:
                acc[...] = jnp.zeros_like(acc)

        def accumulate(n_ref, dp_refs):
            nn = n_ref[...]
            for k in range(npc):
                accs[k][...] += _dot_tn(dp_refs[k][...], nn) if wt else _dot_tn(nn, dp_refs[k][...])

        accumulate(refs[0], refs[1:nin])

        @pl.when(i == nt - 1)
        def _():
            if tail:
                accumulate(refs[nin], refs[nin + 1:nin + ntail])
            for k in range(npc):
                o_refs[k][...] = accs[k][...].astype(BF16)

    tail_specs = []
    if tail:
        tb = nt * tm // tail
        tail_specs = [pl.BlockSpec((tail, d), lambda i: (tb, 0))] + [pl.BlockSpec((tail, w), lambda i: (tb, 0))
                                                                    for w in widths]
    return pl.pallas_call(
        body, name="proj_bwd_w",
        grid=(nt,),
        in_specs=[pl.BlockSpec((tm, d), lambda i: (i, 0))]
        + [pl.BlockSpec((tm, w), lambda i: (i, 0)) for w in widths] + tail_specs,
        out_specs=[pl.BlockSpec(oshape(w), lambda i: (0, 0)) for w in widths],
        out_shape=[jax.ShapeDtypeStruct(oshape(w), BF16) for w in widths],
        scratch_shapes=[pltpu.VMEM(oshape(w), F32) for w in widths],
        compiler_params=_cparams(1),
    )(n, *dp_pieces, *((n, *dp_pieces) if tail else ()))


def _out_fwd(h, ya, yb, w, tm):
    tp, d = h.shape
    na, nb = ya.shape[1], yb.shape[1]

    def body(h_ref, ya_ref, yb_ref, w_ref, o_ref):
        y = _dot(ya_ref[...].astype(BF16), w_ref[0:na, :]) + _dot(yb_ref[...].astype(BF16), w_ref[na:, :])
        o_ref[...] = h_ref[...] + y

    return pl.pallas_call(
        body, name="out_fwd",
        grid=(tp // tm,),
        in_specs=[pl.BlockSpec((tm, d), lambda i: (i, 0)), pl.BlockSpec((tm, na), lambda i: (i, 0)),
                  pl.BlockSpec((tm, nb), lambda i: (i, 0)), pl.BlockSpec((d, d), lambda i: (0, 0))],
        out_specs=pl.BlockSpec((tm, d), lambda i: (i, 0)),
        out_shape=jax.ShapeDtypeStruct((tp, d), F32),
        compiler_params=_cparams(1),
    )(h, ya, yb, w)


def _out_bwd(dy, ya, yb, w, tm, after=()):
    tp, d = dy.shape
    na, nb = ya.shape[1], yb.shape[1]

    def body(dy_ref, ya_ref, yb_ref, w_ref, *rest):
        da_ref, db_ref, dw_ref, acc = rest[len(after):]
        i = pl.program_id(0)

        @pl.when(i == 0)
        def _():
            acc[...] = jnp.zeros_like(acc)

        dyb16 = dy_ref[...].astype(BF16)
        da_ref[...] = _dot_nt(dyb16, w_ref[0:na, :])
        db_ref[...] = _dot_nt(dyb16, w_ref[na:, :])
        acc[0:na, :] += _dot_tn(ya_ref[...].astype(BF16), dyb16)
        acc[na:, :] += _dot_tn(yb_ref[...].astype(BF16), dyb16)

        @pl.when(i == pl.num_programs(0) - 1)
        def _():
            dw_ref[...] = acc[...].astype(BF16)

    return pl.pallas_call(
        body, name="out_bwd",
        grid=(tp // tm,),
        in_specs=[pl.BlockSpec((tm, d), lambda i: (i, 0)), pl.BlockSpec((tm, na), lambda i: (i, 0)),
                  pl.BlockSpec((tm, nb), lambda i: (i, 0)), pl.BlockSpec((d, d), lambda i: (0, 0))]
        + [pl.BlockSpec(memory_space=pl.ANY)] * len(after),
        out_specs=[pl.BlockSpec((tm, na), lambda i: (i, 0)), pl.BlockSpec((tm, nb), lambda i: (i, 0)),
                   pl.BlockSpec((d, d), lambda i: (0, 0))],
        out_shape=[jax.ShapeDtypeStruct((tp, na), F32), jax.ShapeDtypeStruct((tp, nb), F32),
                   jax.ShapeDtypeStruct((d, d), BF16)],
        scratch_shapes=[pltpu.VMEM((d, d), F32)],
        compiler_params=_cparams(1),
    )(dy, ya, yb, w, *after)


def _loss_bwd(h, gfin, tgt, t_real, tm):
    tp, d = h.shape

    def body(h_ref, g_ref, t_ref, loss_ref, dh_ref, dg_ref):
        i = pl.program_id(0)

        @pl.when(i == 0)
        def _():
            loss_ref[...] = jnp.zeros_like(loss_ref)
            dg_ref[...] = jnp.zeros_like(dg_ref)

        rows = i * tm + lax.broadcasted_iota(jnp.int32, (tm, 1), 0)
        valid = (rows >= N_META) & (rows < t_real)
        rstd, xhat = _rms_stats(h_ref[...])
        g = g_ref[...]
        err = jnp.where(valid, xhat * g - t_ref[...], 0.0)
        e2 = jnp.sum(err * err, axis=1, keepdims=True)
        loss_ref[...] += (0.5 / d) * jnp.sum(e2, axis=0, keepdims=True)
        dy = err * (1.0 / d)
        dh, dg = _rms_bwd(dy, g, rstd, xhat)
        dh_ref[...] = dh
        dg_ref[...] += dg

    return pl.pallas_call(
        body, name="loss_bwd",
        grid=(tp // tm,),
        in_specs=[pl.BlockSpec((tm, d), lambda i: (i, 0)), pl.BlockSpec((1, d), lambda i: (0, 0)),
                  pl.BlockSpec((tm, d), lambda i: (i, 0))],
        out_specs=[pl.BlockSpec((1, 1), lambda i: (0, 0)), pl.BlockSpec((tm, d), lambda i: (i, 0)),
                   pl.BlockSpec((1, d), lambda i: (0, 0))],
        out_shape=[jax.ShapeDtypeStruct((1, 1), F32), jax.ShapeDtypeStruct((tp, d), F32),
                   jax.ShapeDtypeStruct((1, d), F32)],
        compiler_params=_cparams(1),
    )(h, gfin, tgt)


POOL_HALO = 16


def _pool_lane_consts(n_rows):
    lane = lax.broadcasted_iota(jnp.int32, (n_rows, D_POOL), 1)
    grp = lane // POOL_GROUP
    win = jnp.where(grp == 0, 2.0, jnp.where(grp == 1, 4.0, jnp.where(grp == 2, 8.0, 16.0)))
    return grp, win


def _pool_select(grp, s2, s4, s8, s16):
    return jnp.where(grp == 0, s2, jnp.where(grp == 1, s4, jnp.where(grp == 2, s8, s16)))


def _pool_mixed(x, row0, tr):
    n = tr + POOL_HALO
    s2 = x + pltpu.roll(x, 1, 0)
    s4 = s2 + pltpu.roll(s2, 2, 0)
    s8 = s4 + pltpu.roll(s4, 4, 0)
    s16 = s8 + pltpu.roll(s8, 8, 0)
    grp, win = _pool_lane_consts(n)
    rows = row0 - POOL_HALO + lax.broadcasted_iota(jnp.int32, (n, D_POOL), 0)
    cnt = jnp.minimum((rows + 1).astype(F32), win)
    pooled = _pool_select(grp, s2, s4, s8, s16) / jnp.maximum(cnt, 1.0)
    return (pooled - x)[POOL_HALO:, :]


def _pool_fwd(p, wbd, scale, tr):
    tp = p.shape[0]
    nt = tp // tr

    def body(p_ref, w_ref, s_ref, y_ref, usc):
        usc[0:POOL_HALO, :] = jnp.zeros((POOL_HALO, D_POOL), F32)
        usc[POOL_HALO:, :] = p_ref[...]

        def tile(r, carry):
            r0 = pl.multiple_of(r * tr, SUBLANE)
            x = usc[pl.ds(r0, tr + POOL_HALO), :]
            mixed = _pool_mixed(x, r0, tr)
            y_ref[pl.ds(r0, tr), :] = (_dot(mixed.astype(BF16), w_ref[...]) * s_ref[...]).astype(BF16)
            return carry

        lax.fori_loop(0, nt, tile, 0)

    return pl.pallas_call(
        body, name="pool_fwd",
        grid=(1,),
        in_specs=[pl.BlockSpec((tp, D_POOL), lambda i: (0, 0)), pl.BlockSpec((D_POOL, D_POOL), lambda i: (0, 0)),
                  pl.BlockSpec((1, D_POOL), lambda i: (0, 0))],
        out_specs=pl.BlockSpec((tp, D_POOL), lambda i: (0, 0)),
        out_shape=jax.ShapeDtypeStruct((tp, D_POOL), BF16),
        scratch_shapes=[pltpu.VMEM((tp + POOL_HALO, D_POOL), F32)],
        compiler_params=_cparams(1),
    )(p, wbd, scale)


def _pool_bwd(p, dya, wbd, scale, tr):
    tp = p.shape[0]
    nt = tp // tr

    def body(p_ref, dy_ref, w_ref, s_ref, du_ref, dw_ref, ds_ref, usc, gsc):
        usc[0:POOL_HALO, :] = jnp.zeros((POOL_HALO, D_POOL), F32)
        usc[POOL_HALO:, :] = p_ref[...]
        gsc[tp:, :] = jnp.zeros((POOL_HALO, D_POOL), F32)
        dw_ref[...] = jnp.zeros_like(dw_ref)
        ds_ref[...] = jnp.zeros_like(ds_ref)
        grp, win = _pool_lane_consts(tr)

        def tile1(r, carry):
            r0 = pl.multiple_of(r * tr, SUBLANE)
            x = usc[pl.ds(r0, tr + POOL_HALO), :]
            mixed = _pool_mixed(x, r0, tr).astype(BF16)
            dy = dy_ref[pl.ds(r0, tr), :]
            dys = (dy * s_ref[...]).astype(BF16)
            ypre = _dot(mixed, w_ref[...])
            ds_ref[...] += _colsum(dy * ypre)
            dw_ref[...] += _dot_tn(mixed, dys)
            dmx = _dot_nt(dys, w_ref[...])
            rows = r0 + lax.broadcasted_iota(jnp.int32, (tr, D_POOL), 0)
            cnt = jnp.minimum((rows + 1).astype(F32), win)
            gsc[pl.ds(r0, tr), :] = dmx / cnt
            return carry

        lax.fori_loop(0, nt, tile1, 0)
        n = tr + POOL_HALO
        grp2, win2 = _pool_lane_consts(n)

        def tile2(r, carry):
            r0 = pl.multiple_of(r * tr, SUBLANE)
            g = gsc[pl.ds(r0, n), :]
            s2 = g + pltpu.roll(g, n - 1, 0)
            s4 = s2 + pltpu.roll(s2, n - 2, 0)
            s8 = s4 + pltpu.roll(s4, n - 4, 0)
            s16 = s8 + pltpu.roll(s8, n - 8, 0)
            pooled_t = _pool_select(grp2, s2, s4, s8, s16)
            rows = r0 + lax.broadcasted_iota(jnp.int32, (n, D_POOL), 0)
            cnt = jnp.minimum((rows + 1).astype(F32), win2)
            du = pooled_t - g * cnt
            du_ref[pl.ds(r0, tr), :] = du[0:tr, :].astype(BF16)
            return carry

        lax.fori_loop(0, nt, tile2, 0)

    return pl.pallas_call(
        body, name="pool_bwd",
        grid=(1,),
        in_specs=[pl.BlockSpec((tp, D_POOL), lambda i: (0, 0)), pl.BlockSpec((tp, D_POOL), lambda i: (0, 0)),
                  pl.BlockSpec((D_POOL, D_POOL), lambda i: (0, 0)), pl.BlockSpec((1, D_POOL), lambda i: (0, 0))],
        out_specs=[pl.BlockSpec((tp, D_POOL), lambda i: (0, 0)), pl.BlockSpec((D_POOL, D_POOL), lambda i: (0, 0)),
                   pl.BlockSpec((1, D_POOL), lambda i: (0, 0))],
        out_shape=[jax.ShapeDtypeStruct((tp, D_POOL), BF16), jax.ShapeDtypeStruct((D_POOL, D_POOL), F32),
                   jax.ShapeDtypeStruct((1, D_POOL), F32)],
        scratch_shapes=[pltpu.VMEM((tp + POOL_HALO, D_POOL), F32), pltpu.VMEM((tp + POOL_HALO, D_POOL), F32)],
        compiler_params=_cparams(1),
    )(p, dya, wbd, scale)


def _hgrn_levels(ch):
    levels = []
    w = ch // 2
    while w >= 1:
        levels.append(w)
        w //= 2
    return levels


def _hgrn_consts(ch):
    t = np.arange(ch)
    tril = t[None, :] <= t[:, None]
    masks = []
    for w in _hgrn_levels(ch):
        blk = t // (2 * w)
        upper = t % (2 * w) >= w
        masks.append(upper[:, None] & (~upper)[None, :] & (blk[:, None] == blk[None, :]))
    masks.append(tril)
    msk = np.stack(masks).astype(np.float32)
    return jnp.asarray(tril.astype(np.float32), BF16), jnp.asarray(msk, F32), len(masks) - 1


def _split3(x):
    hi = x.astype(BF16)
    r1 = x - hi.astype(F32)
    mid = r1.astype(BF16)
    lo = (r1 - mid.astype(F32)).astype(BF16)
    return hi, mid, lo


def _hgrn_exponents(tril, logf):
    ch = logf.shape[0]
    hi, mid, lo = _split3(logf)
    x = _dot(tril, jnp.concatenate([hi, mid, lo], axis=1))
    b = x[:, 0:HEAD] + x[:, HEAD:2 * HEAD] + x[:, 2 * HEAD:3 * HEAD]
    rows = lax.broadcasted_iota(jnp.int32, (ch, HEAD), 0)
    fx = jnp.broadcast_to(b[ch - 1:ch, :], (ch, HEAD)) - b
    lev = []
    for w in _hgrn_levels(ch):
        pos = rows % (2 * w)
        upper = pos >= w
        if w >= SUBLANE:
            parts = [jnp.broadcast_to(b[k * 2 * w + w - 1:k * 2 * w + w, :], (2 * w, HEAD))
                     for k in range(ch // (2 * w))]
            bmid = parts[0] if len(parts) == 1 else jnp.concatenate(parts, axis=0)
            dx = jnp.where(upper, b - bmid, 0.0)
            ex = jnp.where(upper, 0.0, bmid - b)
        else:
            dx = logf
            ex = jnp.zeros_like(logf)
            for i in range(1, w):
                dx = dx + jnp.where(pos >= w + i, pltpu.roll(logf, i, 0), 0.0)
                ex = ex + jnp.where(pos <= w - 1 - i, pltpu.roll(logf, ch - i, 0), 0.0)
            dx = jnp.where(upper, dx, 0.0)
        lev.append((dx, ex))
    return b, fx, lev


def _hgrn_exponents_bwd(tril, d_b, d_fx, d_blast, lev_grads):
    ch = d_b.shape[0]
    rows = lax.broadcasted_iota(jnp.int32, (ch, HEAD), 0)
    db = d_b - d_fx
    dlf = jnp.zeros_like(d_b)
    for w, (ddx, dex) in zip(_hgrn_levels(ch), lev_grads):
        pos = rows % (2 * w)
        upper = pos >= w
        gu = jnp.where(upper, ddx, 0.0)
        if w >= SUBLANE:
            gl = jnp.where(upper, 0.0, dex)
            db = db + gu - gl
            diff = gl - gu
            for k in range(ch // (2 * w)):
                s = _colsum(diff[k * 2 * w:(k + 1) * 2 * w, :])
                db = db + jnp.where(rows == k * 2 * w + w - 1, s, 0.0)
        else:
            dlf = dlf + gu
            for i in range(1, w):
                dlf = dlf + pltpu.roll(jnp.where(pos >= w + i, gu, 0.0), ch - i, 0)
                dlf = dlf + pltpu.roll(jnp.where(pos <= w - 1 - i, dex, 0.0), i, 0)
    db = db + jnp.where(rows == ch - 1, _colsum(d_fx) + d_blast, 0.0)
    hi = db.astype(BF16)
    lo = (db - hi.astype(F32)).astype(BF16)
    d2 = _dot_tn(tril, jnp.concatenate([hi, lo], axis=1))
    return d2[:, 0:HEAD] + d2[:, HEAD:2 * HEAD] + dlf


def _lockstep(gens):
    results = [None] * len(gens)
    live = list(range(len(gens)))
    while live:
        for i in list(live):
            try:
                next(gens[i])
            except StopIteration as stop:
                results[i] = stop.value
                live.remove(i)
    return results


def _hgrn_gates(q_raw, z, lb):
    sz = _sigmoid(z)
    f = lb + (1.0 - lb) * sz
    q = q_raw * _sigmoid(q_raw)
    k = (1.0 - lb) * (1.0 - sz)
    return q, k, f, sz


def _hgrn_intra(q, k, lev, msk_ref, n_lev, ch):
    eye = (lax.broadcasted_iota(jnp.int32, (ch, ch), 0) == lax.broadcasted_iota(jnp.int32, (ch, ch), 1))
    a = jnp.where(eye, jnp.sum(q * k, axis=1, keepdims=True), 0.0)
    ops = []
    for lv in range(n_lev):
        eq = jnp.exp(lev[lv][0])
        ek = jnp.exp(lev[lv][1])
        qd = q * eq
        kd = k * ek
        a = a + msk_ref[lv] * _dot_nt(qd.astype(BF16), kd.astype(BF16))
        ops.append((eq, ek, qd, kd))
        yield
    return a, ops


def _hgrn_fwd(p, lb_logits, gnorm, mst, msk, n_lev, tm):
    tp = p.shape[0]
    ch = HG_CHUNK
    nct = tm // ch
    nt = tp // tm
    nr = mst.shape[0]
    base = D_POOL // HEAD

    hp = HG_HEADS_PER_STEP
    wide = hp * HEAD
    npr = wide // HG_PBLOCK

    def body(*refs):
        p_refs = refs[:4 * npr]
        lg_ref, gn_ref, mst_ref, msk_ref, y_ref, ss_ref, st_sc = refs[4 * npr:]

        @pl.when(pl.program_id(1) == 0)
        def _():
            st_sc[...] = jnp.zeros_like(st_sc)

        lb_all = _sigmoid(lg_ref[0:1, :] - lg_ref[1:2, :])

        def raw(seg, hh, r0):
            per = HG_PBLOCK // HEAD
            return p_refs[seg * npr + hh // per][pl.ds(r0, ch), (hh % per) * HEAD:(hh % per + 1) * HEAD]

        def one_head(hh, c, r0):
            ls = slice(hh * HEAD, (hh + 1) * HEAD)
            q_raw, z, v, g_raw, st = raw(0, hh, r0), raw(1, hh, r0), raw(2, hh, r0), raw(3, hh, r0), st_sc[hh]
            q, k, f, _ = _hgrn_gates(q_raw, z, lb_all[:, ls])
            yield
            b, fx, lev = _hgrn_exponents(mst_ref[...], jnp.log(f))
            yield
            qe = q * jnp.exp(b)
            a, _ = yield from _hgrn_intra(q, k, lev, msk_ref, n_lev, ch)
            v16 = v.astype(BF16)
            o = _dot_nt(qe.astype(BF16), st.astype(BF16)) + _dot(a.astype(BF16), v16)
            kl = k * jnp.exp(fx)
            st_new = st * jnp.exp(b[ch - 1:ch, :]) + _dot_tn(v16, kl.astype(BF16))
            yield
            rstd = lax.rsqrt(jnp.mean(o * o, axis=-1, keepdims=True) + EPS)
            return st, st_new, o * rstd * gn_ref[...] * (g_raw * _sigmoid(g_raw))

        def chunk(c, carry):
            r0 = pl.multiple_of(c * ch, ch)
            results = _lockstep([one_head(hh, c, r0) for hh in range(hp)])
            for hh, (st, st_new, y) in enumerate(results):
                ss_ref[hh, c] = st
                st_sc[hh] = st_new
                y_ref[pl.ds(r0, ch), hh * HEAD:(hh + 1) * HEAD] = y.astype(BF16)
            return carry

        lax.fori_loop(0, nct, chunk, 0)

    def pspec(seg, part):
        return pl.BlockSpec((tm, HG_PBLOCK),
                            lambda h, i: (i, (base + seg * HG_HEADS) * HEAD // HG_PBLOCK + h * npr + part))

    return pl.pallas_call(
        body, name="hgrn_fwd",
        grid=(HG_HEADS // hp, nt),
        in_specs=[pspec(seg, part) for seg in range(4) for part in range(npr)]
        + [pl.BlockSpec((2, wide), lambda h, i: (0, h)),
           pl.BlockSpec((1, HEAD), lambda h, i: (0, 0)),
           pl.BlockSpec((nr, ch), lambda h, i: (0, 0)),
           pl.BlockSpec((n_lev + 1, ch, ch), lambda h, i: (0, 0, 0))],
        out_specs=[pl.BlockSpec((tm, wide), lambda h, i: (i, h)),
                   pl.BlockSpec((hp, nct, HEAD, HEAD), lambda h, i: (h, i, 0, 0))],
        out_shape=[jax.ShapeDtypeStruct((tp, D_HGRN), BF16),
                   jax.ShapeDtypeStruct((HG_HEADS, tp // ch, HEAD, HEAD), F32)],
        scratch_shapes=[pltpu.VMEM((hp, HEAD, HEAD), F32)],
        compiler_params=_cparams(2),
    )(*([p] * (4 * npr)), lb_logits, gnorm, mst, msk)


def _hgrn_bwd(p, dyb, states, lb_logits, gnorm, mst, msk, n_lev, tm):
    tp = p.shape[0]
    ch = HG_CHUNK
    nct = tm // ch
    nt = tp // tm
    nr = mst.shape[0]
    base = D_POOL // HEAD

    hp = HG_HEADS_PER_STEP
    wide = hp * HEAD
    npr = wide // HG_PBLOCK

    def body(*refs):
        p_refs = refs[:4 * npr]
        (dy_ref, ss_ref, lg_ref, gn_ref, mst_ref, msk_ref,
         dq_ref, dz_ref, dv_ref, dg_ref, dlg_ref, dgn_ref, dst_sc, dlb_sc) = refs[4 * npr:]
        ti = pl.program_id(1)

        def raw(seg, hh, r0):
            per = HG_PBLOCK // HEAD
            return p_refs[seg * npr + hh // per][pl.ds(r0, ch), (hh % per) * HEAD:(hh % per + 1) * HEAD]

        @pl.when(ti == 0)
        def _():
            dst_sc[...] = jnp.zeros_like(dst_sc)
            dlb_sc[...] = jnp.zeros_like(dlb_sc)
            dgn_ref[...] = jnp.zeros_like(dgn_ref)

        lb_all = _sigmoid(lg_ref[0:1, :] - lg_ref[1:2, :])
        gn = gn_ref[...]

        def load_head(hh, c, r0):
            ls = slice(hh * HEAD, (hh + 1) * HEAD)
            return (raw(0, hh, r0), raw(1, hh, r0), raw(2, hh, r0), raw(3, hh, r0),
                    dy_ref[pl.ds(r0, ch), ls], ss_ref[hh, c], dst_sc[hh])

        def store_head(hh, r0, res):
            ls = slice(hh * HEAD, (hh + 1) * HEAD)
            dq_raw, dz, dv, dg_raw, dgn, dst_new, dlb = res
            dq_ref[pl.ds(r0, ch), ls] = dq_raw
            dz_ref[pl.ds(r0, ch), ls] = dz
            dv_ref[pl.ds(r0, ch), ls] = dv
            dg_ref[pl.ds(r0, ch), ls] = dg_raw
            dgn_ref[hh] += dgn
            dst_sc[hh] = dst_new
            dlb_sc[:, ls] += dlb

        def one_head(hh, loaded):
            ls = slice(hh * HEAD, (hh + 1) * HEAD)
            lb = lb_all[:, ls]
            q_raw, z, v, g_raw, dy, st, dst = loaded
            q, k, f, sz = _hgrn_gates(q_raw, z, lb)
            yield
            b, fx, lev = _hgrn_exponents(mst_ref[...], jnp.log(f))
            yield
            eb = jnp.exp(b)
            ef = jnp.exp(fx)
            elast = jnp.exp(b[ch - 1:ch, :])
            qe = q * eb
            kl = k * ef
            a, ops = yield from _hgrn_intra(q, k, lev, msk_ref, n_lev, ch)
            v16 = v.astype(BF16)
            st16 = st.astype(BF16)
            qe16 = qe.astype(BF16)
            kl16 = kl.astype(BF16)
            a16 = a.astype(BF16)
            o = _dot_nt(qe16, st16) + _dot(a16, v16)
            yield
            sg = _sigmoid(g_raw)
            rstd = lax.rsqrt(jnp.mean(o * o, axis=-1, keepdims=True) + EPS)
            oh = o * rstd
            dg_out = (dy * oh * gn * (sg * (1.0 + g_raw * (1.0 - sg)))).astype(BF16)
            don = dy * (g_raw * sg)
            dgn = _colsum(don * oh)
            doh = don * gn
            do = rstd * (doh - oh * jnp.mean(doh * oh, axis=-1, keepdims=True))
            do16 = do.astype(BF16)
            dst16 = dst.astype(BF16)
            yield
            dv = _dot_tn(a16, do16) + _dot_nt(kl16, dst16)
            da = msk_ref[n_lev] * _dot_nt(do16, v16)
            dqe = _dot(do16, st16)
            dkl = _dot(v16, dst16)
            dst_new = dst * elast + _dot_tn(do16, qe16)
            yield
            db_last = _colsum(dst * st) * elast
            dad = jnp.sum(do * v, axis=1, keepdims=True)
            dq = dad * k + dqe * eb
            dk = dad * q + dkl * ef
            lev_grads = []
            for lv in range(n_lev):
                eq, ek, qd, kd = ops[lv]
                gl = (msk_ref[lv] * da).astype(BF16)
                dqd = _dot(gl, kd.astype(BF16))
                dkd = _dot_tn(gl, qd.astype(BF16))
                dq = dq + dqd * eq
                dk = dk + dkd * ek
                lev_grads.append((dqd * qd, dkd * kd))
                yield
            dlogf = _hgrn_exponents_bwd(mst_ref[...], dqe * qe, dkl * kl, db_last, lev_grads)
            yield
            sq = _sigmoid(q_raw)
            dq_out = (dq * (sq * (1.0 + q_raw * (1.0 - sq)))).astype(BF16)
            dfk = dlogf / f - dk
            dz_out = (dfk * (1.0 - lb) * sz * (1.0 - sz)).astype(BF16)
            return dq_out, dz_out, dv.astype(BF16), dg_out, dgn, dst_new, _colsum(dfk * (1.0 - sz))

        def chunk(cc, carry):
            c = nct - 1 - cc
            r0 = pl.multiple_of(c * ch, ch)
            loaded = [load_head(hh, c, r0) for hh in range(HG_HEADS_PER_STEP)]
            results = _lockstep([one_head(hh, loaded[hh]) for hh in range(HG_HEADS_PER_STEP)])
            for hh in range(HG_HEADS_PER_STEP):
                store_head(hh, r0, results[hh])
            return carry

        lax.fori_loop(0, nct, chunk, 0, unroll=1)

        @pl.when(ti == nt - 1)
        def _():
            dl0 = dlb_sc[...] * lb_all * (1.0 - lb_all)
            dlg_ref[0:1, :] = dl0
            dlg_ref[1:2, :] = -dl0

    def pspec(seg, part):
        return pl.BlockSpec((tm, HG_PBLOCK), lambda h, i: (
            nt - 1 - i, (base + seg * HG_HEADS) * HEAD // HG_PBLOCK + h * npr + part))

    ospec = pl.BlockSpec((tm, wide), lambda h, i: (nt - 1 - i, h))
    return pl.pallas_call(
        body, name="hgrn_bwd",
        grid=(HG_HEADS // hp, nt),
        in_specs=[pspec(seg, part) for seg in range(4) for part in range(npr)]
        + [ospec, pl.BlockSpec((hp, nct, HEAD, HEAD), lambda h, i: (h, nt - 1 - i, 0, 0)),
           pl.BlockSpec((2, wide), lambda h, i: (0, h)),
           pl.BlockSpec((1, HEAD), lambda h, i: (0, 0)),
           pl.BlockSpec((nr, ch), lambda h, i: (0, 0)),
           pl.BlockSpec((n_lev + 1, ch, ch), lambda h, i: (0, 0, 0))],
        out_specs=[ospec, ospec, ospec, ospec,
                   pl.BlockSpec((2, wide), lambda h, i: (0, h)),
                   pl.BlockSpec((hp, 1, HEAD), lambda h, i: (h, 0, 0))],
        out_shape=[jax.ShapeDtypeStruct((tp, D_HGRN), BF16)] * 4
        + [jax.ShapeDtypeStruct((2, D_HGRN), F32), jax.ShapeDtypeStruct((HG_HEADS, 1, HEAD), F32)],
        scratch_shapes=[pltpu.VMEM((hp, HEAD, HEAD), F32), pltpu.VMEM((1, wide), F32)],
        compiler_params=_cparams(2),
    )(*([p] * (4 * npr)), dyb, states, lb_logits, gnorm, mst, msk)


def _tap_views(x, tr, halo, width):
    subs = {0: x}
    views = []
    for j in range(width):
        tiles, rem = divmod(width - 1 - j, SUBLANE)
        if rem not in subs:
            subs[rem] = pltpu.roll(x, rem, 0)
        start = halo - tiles * SUBLANE
        views.append(subs[rem][start:start + tr, :])
    return views


def _tap_views_t(y, tr, halo, width):
    n = tr + halo
    subs = {0: y}
    views = []
    for j in range(width):
        tiles, rem = divmod(width - 1 - j, SUBLANE)
        if rem not in subs:
            subs[rem] = pltpu.roll(y, n - rem, 0)
        views.append(subs[rem][tiles * SUBLANE:tiles * SUBLANE + tr, :])
    return views


def _weighted_sum(views, w_ref):
    acc = None
    for j, view in enumerate(views):
        term = view * w_ref[j:j + 1, :]
        acc = term if acc is None else acc + term
    return acc


def _conv_taps(x, w_ref, tr, halo, width):
    return _weighted_sum(_tap_views(x, tr, halo, width), w_ref)


def _conv_taps_t(y, w_ref, tr, halo, width):
    return _weighted_sum(_tap_views_t(y, tr, halo, width), w_ref)


def _ln_stats(cv):
    mu = jnp.mean(cv, axis=-1, keepdims=True)
    xc = cv - mu
    rstd = lax.rsqrt(jnp.mean(xc * xc, axis=-1, keepdims=True) + EPS)
    return rstd, xc * rstd


def _convmod_fwd(p, w, bias, ln_g, ln_b, tr):
    tp = p.shape[0]
    nt = tp // tr
    nb = D_CONV // HEAD

    def body(a_ref, b_ref, w_ref, bi_ref, g_ref, be_ref, y_ref, cv_ref, usc):
        usc[0:CONV_HALO, :] = jnp.zeros((CONV_HALO, HEAD), F32)
        usc[CONV_HALO:, :] = a_ref[...] * _sigmoid(b_ref[...])

        def tile(r, carry):
            r0 = pl.multiple_of(r * tr, SUBLANE)
            x = usc[pl.ds(r0, tr + CONV_HALO), :]
            cv = _conv_taps(x, w_ref, tr, CONV_HALO, CONV_WIDTH) + bi_ref[...]
            cv_ref[pl.ds(r0, tr), :] = cv
            _, xh = _ln_stats(cv)
            un = xh * g_ref[...] + be_ref[...]
            y_ref[pl.ds(r0, tr), :] = (un * _sigmoid(un)).astype(BF16)
            return carry

        lax.fori_loop(0, nt, tile, 0)

    vec = lambda: pl.BlockSpec((1, HEAD), lambda j: (0, j))
    return pl.pallas_call(
        body, name="convmod_fwd",
        grid=(nb,),
        in_specs=[pl.BlockSpec((tp, HEAD), lambda j: (0, j)), pl.BlockSpec((tp, HEAD), lambda j: (0, nb + j)),
                  pl.BlockSpec((CONV_HALO, HEAD), lambda j: (0, j)), vec(), vec(), vec()],
        out_specs=[pl.BlockSpec((tp, HEAD), lambda j: (0, j))] * 2,
        out_shape=[jax.ShapeDtypeStruct((tp, D_CONV), BF16), jax.ShapeDtypeStruct((tp, D_CONV), F32)],
        scratch_shapes=[pltpu.VMEM((tp + CONV_HALO, HEAD), F32)],
        compiler_params=_cparams(1),
    )(p, p, w, bias, ln_g, ln_b)


def _convmod_bwd(p, cv_saved, dyc, w, ln_g, ln_b, tr):
    tp = p.shape[0]
    nt = tp // tr
    nb = D_CONV // HEAD

    def body(a_ref, b_ref, cv_ref, dy_ref, w_ref, g_ref, be_ref, da_ref, db_ref, dw_ref, dv_ref, usc, dsc):
        usc[0:CONV_HALO, :] = jnp.zeros((CONV_HALO, HEAD), F32)
        usc[CONV_HALO:, :] = a_ref[...] * _sigmoid(b_ref[...])
        dsc[tp:, :] = jnp.zeros((CONV_HALO, HEAD), F32)
        dw_ref[...] = jnp.zeros_like(dw_ref)
        dv_ref[...] = jnp.zeros_like(dv_ref)

        def tile1(r, carry):
            r0 = pl.multiple_of(r * tr, SUBLANE)
            x = usc[pl.ds(r0, tr + CONV_HALO), :]
            views = _tap_views(x, tr, CONV_HALO, CONV_WIDTH)
            rstd, xh = _ln_stats(cv_ref[pl.ds(r0, tr), :])
            un = xh * g_ref[...] + be_ref[...]
            sg = _sigmoid(un)
            dun = dy_ref[pl.ds(r0, tr), :] * (sg * (1.0 + un * (1.0 - sg)))
            dv_ref[0, 1:2, :] += _colsum(dun * xh)
            dv_ref[0, 2:3, :] += _colsum(dun)
            dxh = dun * g_ref[...]
            dcv = rstd * (dxh - jnp.mean(dxh, axis=-1, keepdims=True)
                          - xh * jnp.mean(dxh * xh, axis=-1, keepdims=True))
            dv_ref[0, 0:1, :] += _colsum(dcv)
            for j in range(CONV_WIDTH):
                dw_ref[0, j:j + 1, :] += _colsum(dcv * views[j])
            dsc[pl.ds(r0, tr), :] = dcv
            return carry

        lax.fori_loop(0, nt, tile1, 0)

        def tile2(r, carry):
            r0 = pl.multiple_of(r * tr, SUBLANE)
            y = dsc[pl.ds(r0, tr + CONV_HALO), :]
            du = _conv_taps_t(y, w_ref, tr, CONV_HALO, CONV_WIDTH)
            a = a_ref[pl.ds(r0, tr), :]
            sb = _sigmoid(b_ref[pl.ds(r0, tr), :])
            da_ref[pl.ds(r0, tr), :] = (du * sb).astype(BF16)
            db_ref[pl.ds(r0, tr), :] = (du * a * sb * (1.0 - sb)).astype(BF16)
            return carry

        lax.fori_loop(0, nt, tile2, 0)

    vec = lambda: pl.BlockSpec((1, HEAD), lambda j: (0, j))
    col = lambda: pl.BlockSpec((tp, HEAD), lambda j: (0, j))
    return pl.pallas_call(
        body, name="convmod_bwd",
        grid=(nb,),
        in_specs=[col(), pl.BlockSpec((tp, HEAD), lambda j: (0, nb + j)), col(), col(),
                  pl.BlockSpec((CONV_HALO, HEAD), lambda j: (0, j)), vec(), vec()],
        out_specs=[col(), col(), pl.BlockSpec((1, CONV_HALO, HEAD), lambda j: (j, 0, 0)),
                   pl.BlockSpec((1, SUBLANE, HEAD), lambda j: (j, 0, 0))],
        out_shape=[jax.ShapeDtypeStruct((tp, D_CONV), BF16), jax.ShapeDtypeStruct((tp, D_CONV), BF16),
                   jax.ShapeDtypeStruct((nb, CONV_HALO, HEAD), F32), jax.ShapeDtypeStruct((nb, SUBLANE, HEAD), F32)],
        scratch_shapes=[pltpu.VMEM((tp + CONV_HALO, HEAD), F32), pltpu.VMEM((tp + CONV_HALO, HEAD), F32)],
        compiler_params=_cparams(1),
    )(p, p, cv_saved, dyc, w, ln_g, ln_b)


def _log1p_small(y):
    return jnp.where(y < 1e-4, y * (1.0 - 0.5 * y), jnp.log(1.0 + y))


def _softplus(x):
    return jnp.maximum(x, 0.0) + _log1p_small(jnp.exp(-jnp.abs(x)))


def _expm1(x):
    return jnp.where(jnp.abs(x) < 1e-2, x * (1.0 + 0.5 * x * (1.0 + x * (1.0 / 3.0))), jnp.exp(x) - 1.0)


def _gelu_parts(x):
    c = 0.7978845608028654
    inner = c * (x + 0.044715 * x * x * x)
    th = jnp.tanh(inner)
    gelu = 0.5 * x * (1.0 + th)
    dgelu = 0.5 * (1.0 + th) + 0.5 * x * (1.0 - th * th) * c * (1.0 + 3.0 * 0.044715 * x * x)
    return gelu, dgelu


def _lru_gates(x_all, tp, cw_ref, cb_ref, wa_ref, ba_ref, wx_ref, bx_ref, lam_ref):
    u = _conv_taps(x_all, cw_ref, tp, LRU_HALO, LRU_CONV) + cb_ref[...]
    u16 = u.astype(BF16)
    r = _sigmoid(_dot(u16, wa_ref[0]) + ba_ref[...])
    i = _sigmoid(_dot(u16, wx_ref[0]) + bx_ref[...])
    sp = _softplus(-lam_ref[...])
    la = -LRU_C * r * sp
    a = jnp.exp(la)
    mult = jnp.sqrt(-_expm1(2.0 * la))
    return u, r, i, a, mult, sp


def _lru_specs(tp, nb):
    col = lambda k: pl.BlockSpec((tp, HEAD), functools.partial(lambda j, k: (0, k * nb + j), k=k))
    vec = lambda: pl.BlockSpec((1, HEAD), lambda j: (0, j))
    mat = lambda: pl.BlockSpec((1, HEAD, HEAD), lambda j: (j, 0, 0))
    return col, vec, mat


def _lru_fwd(p, cw, cb, wa, ba, wx, bx, lam):
    tp = p.shape[0]
    nb = D_LRU // HEAD
    ng = tp // SUBLANE

    def body(x_ref, gt_ref, cw_ref, cb_ref, wa_ref, ba_ref, wx_ref, bx_ref, lam_ref, y_ref, hs_ref,
             xsc, asc, bsc):
        xsc[0:LRU_HALO, :] = jnp.zeros((LRU_HALO, HEAD), F32)
        xsc[LRU_HALO:, :] = x_ref[...]
        u, r, i, a, mult, _ = _lru_gates(xsc[...], tp, cw_ref, cb_ref, wa_ref, ba_ref, wx_ref, bx_ref, lam_ref)
        rows = lax.broadcasted_iota(jnp.int32, (tp, HEAD), 0)
        b = jnp.where(rows == 0, 1.0, mult) * (i * u)
        sub = rows % SUBLANE
        for k in (1, 2, 4):
            m = sub >= k
            b = jnp.where(m, a * pltpu.roll(b, k, 0) + b, b)
            a = jnp.where(m, a * pltpu.roll(a, k, 0), a)
        asc[...] = a
        bsc[...] = b

        def grp(g, carry):
            r0 = pl.multiple_of(g * SUBLANE, SUBLANE)
            h = bsc[pl.ds(r0, SUBLANE), :] + asc[pl.ds(r0, SUBLANE), :] * carry
            hs_ref[pl.ds(r0, SUBLANE), :] = h
            return jnp.broadcast_to(h[SUBLANE - 1:SUBLANE, :], (SUBLANE, HEAD))

        lax.fori_loop(0, ng, grp, jnp.zeros((SUBLANE, HEAD), F32))
        gelu, _ = _gelu_parts(gt_ref[...])
        y_ref[...] = (gelu * hs_ref[...]).astype(BF16)

    col, vec, mat = _lru_specs(tp, nb)
    return pl.pallas_call(
        body, name="lru_fwd",
        grid=(nb,),
        in_specs=[col(2), col(3), pl.BlockSpec((LRU_CONV, HEAD), lambda j: (0, j)), vec(), mat(), vec(), mat(),
                  vec(), vec()],
        out_specs=[pl.BlockSpec((tp, HEAD), lambda j: (0, j)), pl.BlockSpec((tp, HEAD), lambda j: (0, j))],
        out_shape=[jax.ShapeDtypeStruct((tp, D_LRU), BF16), jax.ShapeDtypeStruct((tp, D_LRU), F32)],
        scratch_shapes=[pltpu.VMEM((tp + LRU_HALO, HEAD), F32), pltpu.VMEM((tp, HEAD), F32),
                        pltpu.VMEM((tp, HEAD), F32)],
        compiler_params=_cparams(1),
    )(p, p, cw, cb, wa, ba, wx, bx, lam)


def _lru_bwd(p, hs, dyd, cw, cb, wa, ba, wx, bx, lam):
    tp = p.shape[0]
    nb = D_LRU // HEAD
    ng = tp // SUBLANE

    def body(x_ref, gt_ref, hs_ref, dy_ref, cw_ref, cb_ref, wa_ref, ba_ref, wx_ref, bx_ref, lam_ref,
             dx_ref, dgt_ref, dwa_ref, dwx_ref, dv_ref, xsc, asc, bsc, gsc, dusc):
        xsc[0:LRU_HALO, :] = jnp.zeros((LRU_HALO, HEAD), F32)
        xsc[LRU_HALO:, :] = x_ref[...]
        x_all = xsc[...]
        u, r, i, a, mult, sp = _lru_gates(x_all, tp, cw_ref, cb_ref, wa_ref, ba_ref, wx_ref, bx_ref, lam_ref)
        rows = lax.broadcasted_iota(jnp.int32, (tp, HEAD), 0)
        hs = hs_ref[...]
        dy = dy_ref[...]
        gelu, dgelu = _gelu_parts(gt_ref[...])
        dgt_ref[...] = (dy * hs * dgelu).astype(BF16)
        bb = dy * gelu
        aa = jnp.where(rows == tp - 1, 0.0, pltpu.roll(a, tp - 1, 0))
        sub = rows % SUBLANE
        for k in (1, 2, 4):
            m = sub < SUBLANE - k
            bb = jnp.where(m, aa * pltpu.roll(bb, tp - k, 0) + bb, bb)
            aa = jnp.where(m, aa * pltpu.roll(aa, tp - k, 0), aa)
        asc[...] = aa
        bsc[...] = bb

        def grp(gi, carry):
            g = ng - 1 - gi
            r0 = pl.multiple_of(g * SUBLANE, SUBLANE)
            gg = bsc[pl.ds(r0, SUBLANE), :] + asc[pl.ds(r0, SUBLANE), :] * carry
            gsc[pl.ds(r0, SUBLANE), :] = gg
            return jnp.broadcast_to(gg[0:1, :], (SUBLANE, HEAD))

        lax.fori_loop(0, ng, grp, jnp.zeros((SUBLANE, HEAD), F32))
        g = gsc[...]
        first = rows == 0
        hprev = jnp.where(first, 0.0, pltpu.roll(hs, 1, 0))
        iu = i * u
        d_iu = g * jnp.where(first, 1.0, mult)
        dmult_term = jnp.where(first, 0.0, g * iu * (-(a * a) / mult))
        dla = g * hprev * a + dmult_term
        dr = dla * (-LRU_C) * sp
        dv_ref[0, 7:8, :] = _colsum(dla * (LRU_C * r) * _sigmoid(-lam_ref[...]))
        dpr = dr * r * (1.0 - r)
        dpi = d_iu * u * i * (1.0 - i)
        dv_ref[0, 5:6, :] = _colsum(dpr)
        dv_ref[0, 6:7, :] = _colsum(dpi)
        u16 = u.astype(BF16)
        dpr16 = dpr.astype(BF16)
        dpi16 = dpi.astype(BF16)
        dwa_ref[0] = _dot_tn(u16, dpr16)
        dwx_ref[0] = _dot_tn(u16, dpi16)
        du = d_iu * i + _dot_nt(dpr16, wa_ref[0]) + _dot_nt(dpi16, wx_ref[0])
        dv_ref[0, 4:5, :] = _colsum(du)
        for j in range(LRU_CONV):
            sh = LRU_CONV - 1 - j
            xs = x_all if sh == 0 else pltpu.roll(x_all, sh, 0)
            dv_ref[0, j:j + 1, :] = _colsum(du * xs[LRU_HALO:, :])
        dusc[0:tp, :] = du
        dusc[tp:, :] = jnp.zeros((LRU_HALO, HEAD), F32)
        dx_ref[...] = _conv_taps_t(dusc[...], cw_ref, tp, LRU_HALO, LRU_CONV).astype(BF16)

    col, vec, mat = _lru_specs(tp, nb)
    ocol = lambda: pl.BlockSpec((tp, HEAD), lambda j: (0, j))
    return pl.pallas_call(
        body, name="lru_bwd",
        grid=(nb,),
        in_specs=[col(2), col(3), ocol(), ocol(), pl.BlockSpec((LRU_CONV, HEAD), lambda j: (0, j)), vec(), mat(),
                  vec(), mat(), vec(), vec()],
        out_specs=[ocol(), ocol(), mat(), mat(), pl.BlockSpec((1, SUBLANE, HEAD), lambda j: (j, 0, 0))],
        out_shape=[jax.ShapeDtypeStruct((tp, D_LRU), BF16), jax.ShapeDtypeStruct((tp, D_LRU), BF16),
                   jax.ShapeDtypeStruct((nb, HEAD, HEAD), F32), jax.ShapeDtypeStruct((nb, HEAD, HEAD), F32),
                   jax.ShapeDtypeStruct((nb, SUBLANE, HEAD), F32)],
        scratch_shapes=[pltpu.VMEM((tp + LRU_HALO, HEAD), F32), pltpu.VMEM((tp, HEAD), F32),
                        pltpu.VMEM((tp, HEAD), F32), pltpu.VMEM((tp, HEAD), F32),
                        pltpu.VMEM((tp + LRU_HALO, HEAD), F32)],
        compiler_params=_cparams(1),
    )(p, p, hs, dyd, cw, cb, wa, ba, wx, bx, lam)


def _mesh_pos():
    return lax.axis_index("x"), lax.axis_index("y"), lax.axis_index("c")


def _other_chips(x, y):
    return [(1 - x, y), (x, 1 - y), (1 - x, 1 - y)]


ANY = pl.BlockSpec(memory_space=pl.ANY)


HBM = pl.BlockSpec(memory_space=pltpu.HBM)
SEM = pl.BlockSpec(memory_space=pltpu.SEMAPHORE)
DATAFLOW = pltpu.SideEffectType.DATAFLOW_SIDE_EFFECTING
N_PEERS = 4


def _in_hbm(a):
    return pltpu.with_memory_space_constraint(a, pltpu.HBM)


def _gather_peers(x, y, c):
    return [((ox, oy, c), 2 * ox + oy) for ox, oy in _other_chips(x, y)] + [((x, y, 1 - c), 2 * x + y)]


def _gather_refs(src, land, slot, c, split):
    if not split:
        return src, land.at[slot]
    half = src.shape[0] // 2
    return src.at[pl.ds(c * half, half)], land.at[slot, pl.ds(c * half, half)]


def _gather_start(arrs, split):
    n = len(arrs)

    def body(*refs):
        ins, lands = refs[:n], refs[n:2 * n]
        ssem, rsem = refs[2 * n:2 * n + 2]
        token = refs[-1]
        x, y, c = _mesh_pos()
        chip = 2 * x + y
        for k in range(n):
            for j, (dev, _) in enumerate(_gather_peers(x, y, c)):
                src, dst = _gather_refs(ins[k], lands[k], chip, c, split[k] and j < N_PEERS - 1)
                pltpu.make_async_remote_copy(
                    src_ref=src, dst_ref=dst, send_sem=ssem.at[N_PEERS * k + j],
                    recv_sem=rsem.at[N_PEERS * k + j], device_id=dev, device_id_type=MESH).start()
        token[...] = jnp.zeros_like(token)

    lands = [_in_hbm(lax.empty((N_SHARD,) + a.shape, a.dtype)) for a in arrs]
    out = pl.pallas_call(
        body, name="gather_start",
        in_specs=[HBM] * (2 * n),
        out_specs=[SEM, SEM] + [HBM] * (2 * n) + [pl.BlockSpec(memory_space=pltpu.VMEM)],
        out_shape=[pltpu.SemaphoreType.DMA((N_PEERS * n,)), pltpu.SemaphoreType.DMA((N_PEERS * n,))]
        + [pltpu.HBM(a.shape, a.dtype) for a in arrs]
        + [pltpu.HBM((N_SHARD,) + a.shape, a.dtype) for a in arrs]
        + [jax.ShapeDtypeStruct((SUBLANE, LANE), F32)],
        input_output_aliases={k: 2 + k for k in range(2 * n)},
        compiler_params=pltpu.CompilerParams(has_side_effects=DATAFLOW),
    )(*[_in_hbm(a) for a in arrs], *lands)
    return out[0], out[1], list(out[2:2 + n]), list(out[2 + n:2 + 2 * n]), out[-1]


def _gather_wait(ssem, rsem, srcs, lands, ks, after, split=False):
    n = len(ks)

    def body(*refs):
        ins, lnd = refs[:n], refs[n:2 * n]
        ssem_ref, rsem_ref = refs[2 * n:2 * n + 2]
        x, y, c = _mesh_pos()
        for i, k in enumerate(ks):
            for j, (dev, pchip) in enumerate(_gather_peers(x, y, c)):
                src, dst = _gather_refs(ins[i], lnd[i], pchip, c, split and j < N_PEERS - 1)
                cp = pltpu.make_async_remote_copy(
                    src_ref=src, dst_ref=dst, send_sem=ssem_ref.at[N_PEERS * k + j],
                    recv_sem=rsem_ref.at[N_PEERS * k + j], device_id=dev, device_id_type=MESH)
                cp.wait_send()
                cp.wait_recv()

    out = pl.pallas_call(
        body, name="gather_wait",
        in_specs=[HBM] * (2 * n) + [SEM, SEM] + [ANY] * len(after),
        out_specs=[HBM] * (2 * n),
        out_shape=[pltpu.HBM(a.shape, a.dtype) for a in srcs] + [pltpu.HBM(a.shape, a.dtype) for a in lands],
        input_output_aliases={k: k for k in range(2 * n)},
        compiler_params=pltpu.CompilerParams(has_side_effects=DATAFLOW),
    )(*srcs, *lands, ssem, rsem, *after)
    return list(out[n:])


def _pair_forward(lands):
    n = len(lands)

    def body(*refs):
        outs = refs[n:2 * n]
        ssem, rsem = refs[2 * n:]
        x, y, c = _mesh_pos()
        sibling = (x, y, 1 - c)
        cps = []
        for k in range(n):
            half = lands[k].shape[1] // 2
            for j, (ox, oy) in enumerate(_other_chips(x, y)):
                mine = outs[k].at[2 * ox + oy, pl.ds(c * half, half)]
                cp = pltpu.make_async_remote_copy(src_ref=mine, dst_ref=mine, send_sem=ssem.at[3 * k + j],
                                                  recv_sem=rsem.at[3 * k + j], device_id=sibling, device_id_type=MESH)
                cp.start()
                cps.append(cp)
        for k in range(n):
            half = lands[k].shape[1] // 2
            for j, (ox, oy) in enumerate(_other_chips(x, y)):
                theirs = outs[k].at[2 * ox + oy, pl.ds((1 - c) * half, half)]
                pltpu.make_async_remote_copy(src_ref=theirs, dst_ref=theirs, send_sem=ssem.at[3 * k + j],
                                             recv_sem=rsem.at[3 * k + j], device_id=sibling,
                                             device_id_type=MESH).wait_recv()
        for cp in cps:
            cp.wait_send()

    return pl.pallas_call(
        body, name="pair_forward",
        in_specs=[ANY] * n, out_specs=[ANY] * n,
        out_shape=[jax.ShapeDtypeStruct(a.shape, a.dtype) for a in lands],
        scratch_shapes=[pltpu.SemaphoreType.DMA((3 * n,)), pltpu.SemaphoreType.DMA((3 * n,))],
        input_output_aliases={k: k for k in range(n)},
    )(*lands)


N_SOURCES = 7


def _reduce_peers(x, y, c):
    peers = []
    for ox, oy in _other_chips(x, y):
        for rel in range(2):
            peers.append(((ox, oy, c + rel - 2 * c * rel), 2 * ox + oy))
    peers.append(((x, y, 1 - c), 2 * x + y))
    return peers


def _reduce_start(arrs, slots):
    n = len(arrs)

    def body(*refs):
        ins, lands = refs[:n], refs[n:2 * n]
        ssem, rsem = refs[2 * n:2 * n + 2]
        token = refs[-1]
        x, y, c = _mesh_pos()
        me = 2 * (2 * x + y) + c
        for k in range(n):
            half = arrs[k].shape[1] // 2
            for p, (dev, ochip) in enumerate(_reduce_peers(x, y, c)):
                pltpu.make_async_remote_copy(
                    src_ref=ins[k].at[ochip, pl.ds(dev[2] * half, half)], dst_ref=lands[k].at[me],
                    send_sem=ssem.at[N_SOURCES * k + p], recv_sem=rsem.at[N_SOURCES * k + p],
                    device_id=dev, device_id_type=MESH).start()
        token[...] = jnp.zeros_like(token)

    out = pl.pallas_call(
        body, name="reduce_start",
        in_specs=[HBM] * (2 * n),
        out_specs=[SEM, SEM] + [HBM] * (2 * n) + [pl.BlockSpec(memory_space=pltpu.VMEM)],
        out_shape=[pltpu.SemaphoreType.DMA((N_SOURCES * n,)), pltpu.SemaphoreType.DMA((N_SOURCES * n,))]
        + [pltpu.HBM(a.shape, a.dtype) for a in arrs] + [pltpu.HBM(a.shape, a.dtype) for a in slots]
        + [jax.ShapeDtypeStruct((SUBLANE, LANE), F32)],
        input_output_aliases={k: 2 + k for k in range(2 * n)},
        compiler_params=pltpu.CompilerParams(has_side_effects=DATAFLOW),
    )(*[_in_hbm(a) for a in arrs], *[_in_hbm(a) for a in slots])
    return out[0], out[1], list(out[2:2 + n]), list(out[2 + n:2 + 2 * n]), out[-1]


def _reduce_wait(ssem, rsem, arrs, slots, after):
    n = len(arrs)

    def body(*refs):
        ins, lnd = refs[:n], refs[n:2 * n]
        ssem_ref, rsem_ref = refs[2 * n:2 * n + 2]
        x, y, c = _mesh_pos()
        for k in range(n):
            half = arrs[k].shape[1] // 2
            for p, (dev, ochip) in enumerate(_reduce_peers(x, y, c)):
                cp = pltpu.make_async_remote_copy(
                    src_ref=ins[k].at[ochip, pl.ds(dev[2] * half, half)], dst_ref=lnd[k].at[2 * ochip + dev[2]],
                    send_sem=ssem_ref.at[N_SOURCES * k + p], recv_sem=rsem_ref.at[N_SOURCES * k + p],
                    device_id=dev, device_id_type=MESH)
                cp.wait_send()
                cp.wait_recv()

    out = pl.pallas_call(
        body, name="reduce_wait",
        in_specs=[HBM] * (2 * n) + [SEM, SEM] + [ANY] * len(after),
        out_specs=[HBM] * (2 * n),
        out_shape=[pltpu.HBM(a.shape, a.dtype) for a in arrs] + [pltpu.HBM(a.shape, a.dtype) for a in slots],
        input_output_aliases={k: k for k in range(2 * n)},
        compiler_params=pltpu.CompilerParams(has_side_effects=DATAFLOW),
    )(*arrs, *slots, ssem, rsem, *after)
    return list(out[n:])


def _own_part(arrs, chip, core, me):
    n = len(arrs)
    nb = GRAD_ROW_BLOCKS

    def body(chip_ref, core_ref, me_ref, *refs):
        for k in range(n):
            refs[n + k][...] = refs[k][...]

    def blk(a):
        return (1, a.shape[1] // 2 // nb, a.shape[2])

    grid_spec = pltpu.PrefetchScalarGridSpec(
        num_scalar_prefetch=3, grid=(nb,),
        in_specs=[pl.BlockSpec(blk(a), lambda i, ch, co, me: (ch[0], co[0] * nb + i, 0)) for a in arrs],
        out_specs=[pl.BlockSpec(blk(a), lambda i, ch, co, me: (me[0], i, 0)) for a in arrs])
    return pl.pallas_call(
        body, name="own_part", grid_spec=grid_spec,
        out_shape=[jax.ShapeDtypeStruct((N_DEV, a.shape[1] // 2, a.shape[2]), a.dtype) for a in arrs],
        compiler_params=_cparams(1),
    )(chip, core, me, *arrs)


def _sum_devices(arrs, core):
    n = len(arrs)
    nb = GRAD_ROW_BLOCKS

    def body(c_ref, *refs):
        for k in range(n):
            r = refs[k]
            acc = r[0].astype(F32)
            for dev in range(1, N_DEV):
                acc = acc + r[dev].astype(F32)
            refs[n + k][...] = acc

    grid_spec = pltpu.PrefetchScalarGridSpec(
        num_scalar_prefetch=1, grid=(nb,),
        in_specs=[pl.BlockSpec((N_DEV, a.shape[1] // nb, a.shape[2]), lambda i, c: (0, i, 0)) for a in arrs],
        out_specs=[pl.BlockSpec((a.shape[1] // nb, a.shape[2]), lambda i, c: (c[0] * nb + i, 0)) for a in arrs])
    return pl.pallas_call(
        body, name="sum_devices", grid_spec=grid_spec,
        out_shape=[jax.ShapeDtypeStruct((2 * a.shape[1], a.shape[2]), F32) for a in arrs],
        compiler_params=_cparams(1),
    )(core, *arrs)


def _small_own(v, me):
    m = v.shape[0]

    def body(me_ref, v_ref, o_ref):
        o_ref[0] = v_ref[...]

    grid_spec = pltpu.PrefetchScalarGridSpec(
        num_scalar_prefetch=1, grid=(1,),
        in_specs=[pl.BlockSpec((m, LANE), lambda i, me: (0, 0))],
        out_specs=pl.BlockSpec((1, m, LANE), lambda i, me: (me[0], 0, 0)))
    return pl.pallas_call(
        body, name="small_own", grid_spec=grid_spec,
        out_shape=jax.ShapeDtypeStruct((N_DEV, m, LANE), v.dtype),
        compiler_params=_cparams(1),
    )(me, v)


def _small_start(v, slots):
    def body(v_ref, land, ssem, rsem, v_thru, land_thru, token):
        del v_thru, land_thru
        x, y, c = _mesh_pos()
        me = 2 * (2 * x + y) + c
        for p, (dev, _) in enumerate(_reduce_peers(x, y, c)):
            pltpu.make_async_remote_copy(src_ref=v_ref, dst_ref=land.at[me], send_sem=ssem.at[p],
                                         recv_sem=rsem.at[p], device_id=dev, device_id_type=MESH).start()
        token[...] = jnp.zeros_like(token)

    out = pl.pallas_call(
        body, name="small_start",
        in_specs=[HBM, HBM],
        out_specs=[SEM, SEM, HBM, HBM, pl.BlockSpec(memory_space=pltpu.VMEM)],
        out_shape=[pltpu.SemaphoreType.DMA((N_SOURCES,)), pltpu.SemaphoreType.DMA((N_SOURCES,)),
                   pltpu.HBM(v.shape, v.dtype), pltpu.HBM(slots.shape, slots.dtype),
                   jax.ShapeDtypeStruct((SUBLANE, LANE), F32)],
        input_output_aliases={0: 2, 1: 3},
        compiler_params=pltpu.CompilerParams(has_side_effects=DATAFLOW),
    )(_in_hbm(v), _in_hbm(slots))
    return out


def _small_wait(ssem, rsem, v, slots, after):
    def body(*refs):
        v_ref, land, ssem_ref, rsem_ref = refs[:4]
        x, y, c = _mesh_pos()
        for p, (dev, ochip) in enumerate(_reduce_peers(x, y, c)):
            cp = pltpu.make_async_remote_copy(src_ref=v_ref, dst_ref=land.at[2 * ochip + dev[2]],
                                              send_sem=ssem_ref.at[p], recv_sem=rsem_ref.at[p],
                                              device_id=dev, device_id_type=MESH)
            cp.wait_send()
            cp.wait_recv()

    out = pl.pallas_call(
        body, name="small_wait",
        in_specs=[HBM, HBM, SEM, SEM] + [ANY] * len(after),
        out_specs=[HBM, HBM],
        out_shape=[pltpu.HBM(v.shape, v.dtype), pltpu.HBM(slots.shape, slots.dtype)],
        input_output_aliases={0: 0, 1: 1},
        compiler_params=pltpu.CompilerParams(has_side_effects=DATAFLOW),
    )(v, slots, ssem, rsem, *after)
    return out[1]


GRAD_ROW_BLOCKS = 2


def _pair_allgather_halves(arrs):
    n = len(arrs)

    def body(*refs):
        outs = refs[n:2 * n]
        ssem, rsem = refs[2 * n:]
        x, y, c = _mesh_pos()
        cps = []
        for k in range(n):
            h = arrs[k].shape[0] // 2
            mine = outs[k].at[pl.ds(c * h, h)]
            cp = pltpu.make_async_remote_copy(src_ref=mine, dst_ref=mine, send_sem=ssem.at[k],
                                              recv_sem=rsem.at[k], device_id=(x, y, 1 - c), device_id_type=MESH)
            cp.start()
            cps.append(cp)
        for k, cp in enumerate(cps):
            h = arrs[k].shape[0] // 2
            theirs = outs[k].at[pl.ds((1 - c) * h, h)]
            pltpu.make_async_remote_copy(src_ref=theirs, dst_ref=theirs, send_sem=ssem.at[k], recv_sem=rsem.at[k],
                                         device_id=(x, y, 1 - c), device_id_type=MESH).wait_recv()
            cp.wait_send()

    return pl.pallas_call(
        body, name="pair_allgather_halves",
        in_specs=[ANY] * n, out_specs=[ANY] * n,
        out_shape=[jax.ShapeDtypeStruct(a.shape, a.dtype) for a in arrs],
        scratch_shapes=[pltpu.SemaphoreType.DMA((n,)), pltpu.SemaphoreType.DMA((n,))],
        input_output_aliases={k: k for k in range(n)},
    )(*arrs)


N_DEV = 8


def _adamw_math(w, g, m, v):
    m2 = ADAM_B1 * m + (1.0 - ADAM_B1) * g
    v2 = ADAM_B2 * v + (1.0 - ADAM_B2) * (g * g)
    m_hat = m2 / (1.0 - ADAM_B1 ** ADAM_STEP)
    v_hat = v2 / (1.0 - ADAM_B2 ** ADAM_STEP)
    delta = -ADAM_LR * (m_hat / (jnp.sqrt(v_hat) + ADAM_EPS) + ADAM_WD * w)
    return delta, m2, v2


def _adamw(w, m, v, gs, nblk):
    nl, r, n = w.shape
    assert nl == len(gs) and nl in (1, 2)
    br = r // nblk

    def body(w_ref, m_ref, v_ref, *rest):
        g_refs, (go_ref, d_ref, mo_ref, vo_ref) = rest[:nl], rest[nl:]
        g = g_refs[0][...]
        if nl == 2:
            g = jnp.where(pl.program_id(0) == 0, g, g_refs[1][...])
        delta, m2, v2 = _adamw_math(w_ref[0], g, m_ref[0], v_ref[0])
        go_ref[0] = g
        d_ref[0] = delta
        mo_ref[0] = m2
        vo_ref[0] = v2

    spec = pl.BlockSpec((1, br, n), lambda l, i: (l, i, 0))
    g_specs = [pl.BlockSpec((br, n), lambda l, i: (i, 0))] if nl == 1 else [
        pl.BlockSpec((br, n), lambda l, i: (jnp.where(l == 0, i, nblk - 1), 0)),
        pl.BlockSpec((br, n), lambda l, i: (jnp.where(l == 1, i, 0), 0))]
    return pl.pallas_call(
        body, name="adamw", grid=(nl, nblk),
        in_specs=[spec, spec, spec] + g_specs,
        out_specs=[spec] * 4,
        out_shape=[jax.ShapeDtypeStruct((nl, r, n), F32)] * 4,
        compiler_params=_cparams(2),
    )(w, m, v, *gs)


def _small_reduce_adamw(parts, w, m, v, rep_rows, sh_rows):
    mrows = rep_rows + N_SHARD * sh_rows + LOSS_ROWS

    def body(p_ref, w_ref, m_ref, v_ref, go_ref, d_ref, mo_ref, vo_ref, loss_ref):
        x, y, _ = _mesh_pos()
        mine = rep_rows + (2 * x + y) * sh_rows
        g_rep = p_ref[0:rep_rows, :]
        g_sh = p_ref[pl.ds(pl.multiple_of(mine, SUBLANE), sh_rows), :]
        loss = p_ref[mrows - LOSS_ROWS:mrows, :]
        for k in range(1, N_DEV):
            g_rep = g_rep + p_ref[k * mrows:k * mrows + rep_rows, :]
            g_sh = g_sh + p_ref[pl.ds(pl.multiple_of(k * mrows + mine, SUBLANE), sh_rows), :]
            loss = loss + p_ref[(k + 1) * mrows - LOSS_ROWS:(k + 1) * mrows, :]
        g = jnp.concatenate([g_rep, g_sh], axis=0)
        delta, m2, v2 = _adamw_math(w_ref[...], g, m_ref[...], v_ref[...])
        go_ref[...] = g
        d_ref[...] = delta
        mo_ref[...] = m2
        vo_ref[...] = v2
        loss_ref[...] = loss

    return pl.pallas_call(
        body, name="small_reduce_adamw",
        out_shape=[jax.ShapeDtypeStruct((rep_rows + sh_rows, 128), F32)] * 4
        + [jax.ShapeDtypeStruct((LOSS_ROWS, 128), F32)],
        compiler_params=pltpu.CompilerParams(vmem_limit_bytes=VMEM_LIMIT_MB * 1024 * 1024),
    )(parts, w, m, v)


LANE = 128
REP_SPEC = (("ffn1_norm", 16), ("mix_norm", 16), ("ffn2_norm", 16), ("final_norm", 8), ("pool_w", 128),
            ("pool_scale", 8), ("hgrn_lb_logits", 16), ("hgrn_gnorm", 8), ("lru_wa", 256), ("lru_wx", 256))
SH_SPEC = (("meta_tokens", 32), ("conv_w", 32), ("lru_conv_w", 8), ("conv_b", 8), ("conv_ln_g", 8),
           ("conv_ln_b", 8), ("lru_conv_b", 8), ("lru_ba", 8), ("lru_bx", 8), ("lru_lambda", 8))
REP_ROWS = sum(r for _, r in REP_SPEC)
SH_ROWS = sum(r for _, r in SH_SPEC)


def _pack_rows(vals, spec):
    parts = []
    for name, rows in spec:
        flat = vals[name].astype(F32).reshape(-1, LANE)
        if flat.shape[0] < rows:
            flat = jnp.concatenate([flat, jnp.zeros((rows - flat.shape[0], LANE), F32)], axis=0)
        parts.append(flat)
    return jnp.concatenate(parts, axis=0)


def _unpack_rows(packed, spec, shapes):
    out = {}
    off = 0
    for name, rows in spec:
        shp = shapes[name]
        n = int(np.prod(shp)) // LANE
        out[name] = packed[off:off + n].reshape(shp)
        off += rows
    return out


def _block_diag(blocks):
    n, b, _ = blocks.shape
    return sum(jnp.pad(blocks[g], ((g * b, (n - 1 - g) * b), (g * b, (n - 1 - g) * b))) for g in range(n))


def _diag_blocks(mat, n):
    b = mat.shape[0] // n
    return jnp.stack([mat[g * b:(g + 1) * b, g * b:(g + 1) * b] for g in range(n)])


BIG = ("ffn1_wg", "ffn1_wu", "ffn2_wg", "ffn2_wu", "ffn1_wd", "ffn2_wd", "w_in_even", "w_out_even",
       "w_in_odd", "w_out_odd")
WEIGHT_NAMES = ('meta_tokens', 'ffn1_norm', 'ffn1_wg', 'ffn1_wu', 'ffn1_wd', 'mix_norm', 'ffn2_norm', 'ffn2_wg',
                'ffn2_wu', 'ffn2_wd', 'w_in_even', 'pool_w', 'pool_scale', 'hgrn_lb_logits', 'hgrn_gnorm',
                'w_out_even', 'w_in_odd', 'conv_w', 'conv_b', 'conv_ln_g', 'conv_ln_b', 'lru_conv_w',
                'lru_conv_b', 'lru_wa', 'lru_ba', 'lru_wx', 'lru_bx', 'lru_lambda', 'w_out_odd', 'final_norm')


def _block_diag2(heads):
    nb = heads.shape[0] // 2
    return jnp.stack([_block_diag(heads[2 * j:2 * j + 2]) for j in range(nb)])


def _diag_blocks2(mats):
    return jnp.concatenate([_diag_blocks(mats[j], 2) for j in range(mats.shape[0])], axis=0)


GATHER_GROUPS = (
    (("small", 0),),
    (("ffn1_wg", 0), ("ffn1_wu", 0), ("ffn1_wd", 0)),
    (("w_in_even", 0), ("w_out_even", 0)),
    (("ffn2_wg", 0), ("ffn2_wu", 0), ("ffn2_wd", 0)),
    (("ffn1_wg", 1), ("ffn1_wu", 1), ("ffn1_wd", 1)),
    (("w_in_odd", 0), ("w_out_odd", 0)),
    (("ffn2_wg", 1), ("ffn2_wu", 1), ("ffn2_wd", 1)),
)
ADAM_ROW_BLOCKS = {"ffn1_wg": 2, "ffn1_wu": 2, "ffn2_wg": 2, "ffn2_wu": 2, "ffn1_wd": 2, "ffn2_wd": 2,
                   "w_in_even": 4, "w_out_even": 2, "w_in_odd": 4, "w_out_odd": 2}
TRANSPOSED = ("ffn1_wg", "ffn1_wu", "ffn2_wg", "ffn2_wu", "w_in_even")
SCATTER_DEPTH = 2
LOSS_ROWS = 8
GATHER_SPLIT = (1, 4)


def _unpack_small(sm, shapes):
    per_shard = [_unpack_rows(sm[s], SH_SPEC, shapes) for s in range(N_SHARD)]
    full = {}
    for n, _ in SH_SPEC:
        full[n] = jnp.concatenate([per_shard[s][n].reshape(-1, shapes[n][-1]) for s in range(N_SHARD)], axis=-1)
    full["conv_w"] = jnp.concatenate([full["conv_w"], jnp.zeros((CONV_HALO - CONV_WIDTH, D_CONV), F32)], axis=0)
    return full


def _local_step(x, tgt, w, shapes, fetch, emit, emit_small):
    s_len, d = x.shape
    t_real = s_len + N_META
    tp = -(-t_real // ROW_ALIGN) * ROW_ALIGN
    tm = _tile(tp, 832, ROW_ALIGN)
    tm_small = _tile(tp, 832, 16)
    tr = _tile(tp, 416, SUBLANE)
    tm_wgrad = _wgrad_tiles(tp, 1024)

    def gain(name, layer):
        return w[name][layer:layer + 1]

    pool_wbd = _block_diag(w["pool_w"][0]).astype(BF16)
    pool_scale = w["pool_scale"]
    wa_bd = _block_diag2(w["lru_wa"][0]).astype(BF16)
    wx_bd = _block_diag2(w["lru_wx"][0]).astype(BF16)
    mst, msk, n_lev = _hgrn_consts(HG_CHUNK)

    (sm,) = fetch(0, None)
    sf = _unpack_small(sm, shapes)
    h0 = jnp.concatenate([sf["meta_tokens"], x, jnp.zeros((tp - t_real, d), F32)], axis=0)
    tgt_pad = jnp.concatenate([jnp.zeros((N_META, d), F32), tgt, jnp.zeros((tp - t_real, d), F32)], axis=0)
    f1l0 = fetch(1, h0)
    h1, *s1 = _ffn_fwd(h0, gain("ffn1_norm", 0), *f1l0, tm)
    w_in_even4, w_out_even4 = fetch(2, h1)
    w_out_even = w_out_even4.reshape(d, d)
    even_piece = [(w_in_even4.reshape(D_IN_EVEN, d), (D_IN_EVEN, d), (0, 0))]
    p0, nm0 = _proj_fwd(h1, gain("mix_norm", 0), even_piece, tm_small, wt=True)
    ya = _pool_fwd(p0, pool_wbd, pool_scale, tr)
    yb, states = _hgrn_fwd(p0, w["hgrn_lb_logits"], w["hgrn_gnorm"], mst, msk, n_lev, tm)
    h2 = _out_fwd(h1, ya, yb, w_out_even, tm)
    f2l0 = fetch(3, h2)
    h3, *s2 = _ffn_fwd(h2, gain("ffn2_norm", 0), *f2l0, tm)
    f1l1 = fetch(4, h3)
    h4, *s3 = _ffn_fwd(h3, gain("ffn1_norm", 1), *f1l1, tm)
    w_in_odd4, w_out_odd4 = fetch(5, h4)
    w_out_odd = w_out_odd4.reshape(d, d)
    odd_pieces = [(w_in_odd4, (1, d, D_IN_ODD // N_SHARD), (k, 0, 0)) for k in range(N_SHARD)]
    p1, nm1 = _proj_fwd(h4, gain("mix_norm", 1), odd_pieces, tm_small)
    yc, conv_out = _convmod_fwd(p1, sf["conv_w"], sf["conv_b"], sf["conv_ln_g"], sf["conv_ln_b"], tr)
    lru_args = (sf["lru_conv_w"], sf["lru_conv_b"], wa_bd, sf["lru_ba"], wx_bd, sf["lru_bx"], sf["lru_lambda"])
    yd, hs = _lru_fwd(p1, *lru_args)
    h5 = _out_fwd(h4, yc, yd, w_out_odd, tm)
    f2l1 = fetch(6, h5)
    h6, *s4 = _ffn_fwd(h5, gain("ffn2_norm", 1), *f2l1, tm)
    loss, dh6, dg_final = _loss_bwd(h6, w["final_norm"].reshape(1, d), tgt_pad, t_real, tm)

    def ffn_bwd(dho, h, saved, norm, wts, after=()):
        ga, gb, sa, n = saved
        dh, da, db, dg, dy = _ffn_bwd_act(dho, h, norm, ga, gb, *wts, tm, after)
        return dh, dg, _ffn_bwd_w([(da, n, None), (db, n, None), (sa, dy, None)], tm_wgrad)

    dh5, dg_f2_l1, g = ffn_bwd(dh6, h5, s4, gain("ffn2_norm", 1), f2l1)
    sent = emit((("ffn2_wg", 1), ("ffn2_wu", 1), ("ffn2_wd", 1)), g)
    dyc, dyd, dw_out_odd = _out_bwd(dh5, yc, yd, w_out_odd, tm, tuple(sent))
    dca, dcb, dconv_w, dconv_vec = _convmod_bwd(p1, conv_out, dyc, sf["conv_w"], sf["conv_ln_g"],
                                                sf["conv_ln_b"], tr)
    dlx, dlg, dwa_bd, dwx_bd, dlru_vec = _lru_bwd(p1, hs, dyd, *lru_args)
    dp1 = [dca, dcb, dlx, dlg]
    dh4, dg_mix_l1 = _proj_bwd_act(dh5, h4, gain("mix_norm", 1), dp1, odd_pieces, tm_small)
    dw_in_odd = jnp.stack(_proj_bwd_w(nm1, dp1, tm_wgrad))
    dh3, dg_f1_l1, g = ffn_bwd(dh4, h3, s3, gain("ffn1_norm", 1), f1l1)
    sent = emit((("w_out_odd", 0), ("w_in_odd", 0), ("ffn1_wg", 1), ("ffn1_wu", 1), ("ffn1_wd", 1)),
                [dw_out_odd.reshape(N_SHARD, d // N_SHARD, d), dw_in_odd] + list(g))
    dh2, dg_f2_l0, g = ffn_bwd(dh3, h2, s2, gain("ffn2_norm", 0), f2l0, tuple(sent))
    sent = emit((("ffn2_wg", 0), ("ffn2_wu", 0), ("ffn2_wd", 0)), g)
    dya, dyb, dw_out_even = _out_bwd(dh2, ya, yb, w_out_even, tm, tuple(sent))
    dpool, dpool_wbd, dpool_scale = _pool_bwd(p0, dya, pool_wbd, pool_scale, tr)
    dq, dz, dv, dgate, dlb_logits, dgn_heads = _hgrn_bwd(p0, dyb, states, w["hgrn_lb_logits"], w["hgrn_gnorm"],
                                                         mst, msk, n_lev, tm)
    dp0 = [dpool, dq, dz, dv, dgate]
    dh1, dg_mix_l0 = _proj_bwd_act(dh2, h1, gain("mix_norm", 0), dp0, even_piece, tm_small, wt=True)
    dw_in_even_t = jnp.concatenate(_proj_bwd_w(nm0, dp0, tm_wgrad, wt=True), axis=0)
    ga, gb, sa, n1 = s1
    (dwd_f1l0,) = _ffn_bwd_w([(sa, dh1, 0.5)], tm_wgrad)
    sent = emit((("w_out_even", 0), ("w_in_even", 0), ("ffn1_wd", 0)),
                [dw_out_even.reshape(N_SHARD, d // N_SHARD, d),
                 dw_in_even_t.reshape(N_SHARD, D_IN_EVEN // N_SHARD, d), dwd_f1l0])
    dh0, da, db, dg_f1_l0, _ = _ffn_bwd_act(dh1, h0, gain("ffn1_norm", 0), ga, gb, *f1l0, tm, tuple(sent))

    grad_x = dh0[N_META:t_real]
    rep = {
        "ffn1_norm": jnp.concatenate([dg_f1_l0, dg_f1_l1], axis=0),
        "mix_norm": jnp.concatenate([dg_mix_l0, dg_mix_l1], axis=0),
        "ffn2_norm": jnp.concatenate([dg_f2_l0, dg_f2_l1], axis=0),
        "final_norm": dg_final,
        "pool_w": _diag_blocks(dpool_wbd, len(POOL_WINDOWS)),
        "pool_scale": dpool_scale,
        "hgrn_lb_logits": dlb_logits,
        "hgrn_gnorm": jnp.sum(dgn_heads, axis=0),
        "lru_wa": _diag_blocks2(dwa_bd),
        "lru_wx": _diag_blocks2(dwx_bd),
    }
    dmeta = jnp.transpose(dh0[:N_META].reshape(N_META, N_SHARD, 2, LANE), (1, 0, 2, 3)).reshape(N_SHARD, 32, LANE)
    packs = [_pack_rows(rep, REP_SPEC)]
    for s in range(N_SHARD):
        sh = {
            "meta_tokens": dmeta[s], "conv_w": dconv_w[s], "lru_conv_w": dlru_vec[s, 0:4],
            "conv_b": dconv_vec[s, 0:1], "conv_ln_g": dconv_vec[s, 1:2], "conv_ln_b": dconv_vec[s, 2:3],
            "lru_conv_b": dlru_vec[s, 4:5], "lru_ba": dlru_vec[s, 5:6], "lru_bx": dlru_vec[s, 6:7],
            "lru_lambda": dlru_vec[s, 7:8],
        }
        packs.append(_pack_rows(sh, SH_SPEC))
    packs.append(jnp.pad(loss, ((0, LOSS_ROWS - 1), (0, LANE - 1))))
    sent = emit_small(jnp.concatenate(packs, axis=0))
    emit((("ffn1_wg", 0), ("ffn1_wu", 0)), _ffn_bwd_w([(da, n1, None), (db, n1, None)], tm_wgrad, tuple(sent)))
    return grad_x


def kernel(x, meta_tokens, ffn1_norm, ffn1_wg, ffn1_wu, ffn1_wd, mix_norm, ffn2_norm, ffn2_wg, ffn2_wu, ffn2_wd, w_in_even, pool_w, pool_scale, hgrn_lb_logits, hgrn_gnorm, w_out_even, w_in_odd, conv_w, conv_b, conv_ln_g, conv_ln_b, lru_conv_w, lru_conv_b, lru_wa, lru_ba, lru_wx, lru_bx, lru_lambda, w_out_odd, final_norm, loss_target, m_meta_tokens, m_ffn1_norm, m_ffn1_wg, m_ffn1_wu, m_ffn1_wd, m_mix_norm, m_ffn2_norm, m_ffn2_wg, m_ffn2_wu, m_ffn2_wd, m_w_in_even, m_pool_w, m_pool_scale, m_hgrn_lb_logits, m_hgrn_gnorm, m_w_out_even, m_w_in_odd, m_conv_w, m_conv_b, m_conv_ln_g, m_conv_ln_b, m_lru_conv_w, m_lru_conv_b, m_lru_wa, m_lru_ba, m_lru_wx, m_lru_bx, m_lru_lambda, m_w_out_odd, m_final_norm, v_meta_tokens, v_ffn1_norm, v_ffn1_wg, v_ffn1_wu, v_ffn1_wd, v_mix_norm, v_ffn2_norm, v_ffn2_wg, v_ffn2_wu, v_ffn2_wd, v_w_in_even, v_pool_w, v_pool_scale, v_hgrn_lb_logits, v_hgrn_gnorm, v_w_out_even, v_w_in_odd, v_conv_w, v_conv_b, v_conv_ln_g, v_conv_ln_b, v_lru_conv_w, v_lru_conv_b, v_lru_wa, v_lru_ba, v_lru_wx, v_lru_bx, v_lru_lambda, v_w_out_odd, v_final_norm):
    args = locals()
    w = {n: args[n] for n in WEIGHT_NAMES}
    m = {n: args["m_" + n] for n in WEIGHT_NAMES}
    v = {n: args["v_" + n] for n in WEIGHT_NAMES}
    shapes = {n: w[n].shape for n in WEIGHT_NAMES}
    core = lax.axis_index("c").astype(jnp.int32).reshape(1)
    chip = (2 * lax.axis_index("x") + lax.axis_index("y")).astype(jnp.int32).reshape(1)
    me = 2 * chip + core

    def view(a, n):
        return jnp.swapaxes(a, 1, 2) if n in TRANSPOSED else a

    wv, mv, vv = [{n: view(src[n], n) for n in BIG} for src in (w, m, v)]

    def shard(key):
        n, l = key
        return _pack_rows(w, SH_SPEC) if n == "small" else wv[n][l].astype(BF16)

    started = {}
    for groups in (GATHER_GROUPS[:2], GATHER_GROUPS[2:]):
        gkeys = [key for grp in groups for key in grp]
        ssem, rsem, srcs, lands, token = _gather_start([shard(key) for key in gkeys],
                                                       [any(key in GATHER_GROUPS[g] for g in GATHER_SPLIT)
                                                        for key in gkeys])
        for k, key in enumerate(gkeys):
            started[key] = (ssem, rsem, srcs[k], lands[k], k, token)

    def pack_small(src):
        return jnp.concatenate([_pack_rows(src, REP_SPEC), _pack_rows(src, SH_SPEC)], axis=0)

    small_packs = [pack_small(src) for src in (w, m, v)]

    def fetch(group, after):
        st = [started[key] for key in GATHER_GROUPS[group]]
        deps = (st[0][5],) if after is None else (after,)
        if group == 1:
            deps += (started[GATHER_GROUPS[2][0]][5],) + tuple(small_packs)
        split = group in GATHER_SPLIT
        got = _gather_wait(st[0][0], st[0][1], [s[2] for s in st], [s[3] for s in st], [s[4] for s in st], deps,
                           split)
        return _pair_forward(got) if split else got

    in_flight, reduced = [], {}

    def collect(entry, after):
        gkeys, gs_sem, gr_sem, grads_thru, slots_thru, _ = entry
        slots = _reduce_wait(gs_sem, gr_sem, grads_thru, slots_thru, after)
        full = _pair_allgather_halves(_sum_devices(slots, core))
        reduced.update(zip(gkeys, full))
        return full[0]

    def emit(gkeys, grads):
        grads = list(grads)
        in_flight.append((gkeys,) + tuple(_reduce_start(grads, _own_part(grads, chip, core, me))))
        token = in_flight[-1][-1]
        if len(in_flight) > SCATTER_DEPTH:
            return token, collect(in_flight[-1 - SCATTER_DEPTH], (token,))
        return (token,)

    small_flight = []

    def emit_small(part):
        small_flight.append(_small_start(part, _small_own(part, me)))
        return (small_flight[0][4],)

    grad_x = _local_step(x[0], loss_target[0], w, shapes, fetch, emit, emit_small)

    out_g, out_d, out_m, out_v = {}, {}, {}, {}
    deps = (in_flight[-1][-1],)

    def adamw_ready():
        done = ()
        for n in BIG:
            layers = range(shapes[n][0])
            if n not in out_g and all((n, l) in reduced for l in layers):
                res = _adamw(wv[n], mv[n], vv[n], [reduced[(n, l)] for l in layers], ADAM_ROW_BLOCKS[n])
                out_g[n], out_d[n], out_m[n], out_v[n] = [view(r, n) for r in res]
                done += (res[1],)
        return done

    deps += adamw_ready()
    for entry in in_flight[-SCATTER_DEPTH:-1]:
        collect(entry, deps)
        deps += adamw_ready()
    s_ssem, s_rsem, s_part, s_slots, _ = small_flight[0]
    small_all = _small_wait(s_ssem, s_rsem, s_part, s_slots, deps)
    small_res = _small_reduce_adamw(small_all.reshape(-1, LANE), *small_packs, REP_ROWS, SH_ROWS)
    collect(in_flight[-1], deps + (small_res[0],))
    adamw_ready()

    loss = small_res[4][0, 0]
    for dst, packed in zip((out_g, out_d, out_m, out_v), small_res[:4]):
        dst.update(_unpack_rows(packed[:REP_ROWS], REP_SPEC, shapes))
        dst.update(_unpack_rows(packed[REP_ROWS:], SH_SPEC, shapes))

    return (loss, grad_x[None], *[out_g[n] for n in WEIGHT_NAMES], *[out_d[n] for n in WEIGHT_NAMES],
            *[out_m[n] for n in WEIGHT_NAMES], *[out_v[n] for n in WEIGHT_NAMES])
```

```python
import functools

import numpy as np
import jax
import jax.numpy as jnp
from jax import lax
from jax.experimental import pallas as pl
from jax.experimental.pallas import tpu as pltpu

F32 = jnp.float32
BF16 = jnp.bfloat16
MESH = pl.DeviceIdType.MESH

EPS = 1e-6
N_META = 16
D_FF = 2816
N_SHARD = 4
FF_SHARD = D_FF // N_SHARD
D_POOL = 256
POOL_GROUP = 64
POOL_WINDOWS = (2, 4, 8, 16)
D_HGRN = 768
HG_HEADS = 6
HEAD = 128
HG_CHUNK = 64
HG_HEADS_PER_STEP = 6
FFN_ROW_BLOCKS = 2
HG_PBLOCK = 256
D_IN_EVEN = D_POOL + 4 * D_HGRN
D_CONV = 512
CONV_WIDTH = 31
CONV_HALO = 32
D_LRU = 512
LRU_CONV = 4
LRU_HALO = 8
LRU_C = 8.0
D_IN_ODD = 2 * D_CONV + 2 * D_LRU
SUBLANE = 8
MXU_DEPTH = 256
ROW_ALIGN = 64

ADAM_LR = 0.001
ADAM_B1 = 0.9
ADAM_B2 = 0.999
ADAM_EPS = 1e-08
ADAM_WD = 0.01
ADAM_STEP = 10

VMEM_LIMIT_MB = 56


def _cparams(n_grid_axes=0, vmem_mb=VMEM_LIMIT_MB):
    sem = ("arbitrary",) * n_grid_axes if n_grid_axes else None
    return pltpu.CompilerParams(dimension_semantics=sem, vmem_limit_bytes=vmem_mb * 1024 * 1024)


def _tile(n, target, mult):
    best = None
    for t in range(mult, min(n, target) + 1, mult):
        if n % t == 0:
            best = t
    assert best is not None, (n, target, mult)
    return best


def _dot(a, b):
    return jnp.dot(a, b, preferred_element_type=F32)


def _dot_nt(a, b):
    return lax.dot_general(a, b, (((1,), (1,)), ((), ())), preferred_element_type=F32)


def _dot_tn(a, b):
    return lax.dot_general(a, b, (((0,), (0,)), ((), ())), preferred_element_type=F32)


def _sigmoid(x):
    return 1.0 / (1.0 + jnp.exp(-x))


def _colsum(x):
    return jnp.sum(x, axis=0, keepdims=True)


def _rms_stats(h):
    rstd = lax.rsqrt(jnp.mean(h * h, axis=-1, keepdims=True) + EPS)
    return rstd, h * rstd


def _rms_bwd(dn, g, rstd, xhat):
    dng = dn * g
    dh = rstd * (dng - xhat * jnp.mean(dng * xhat, axis=-1, keepdims=True))
    return dh, _colsum(dn * xhat)


def _ffn_fwd(h, norm, wg4, wu4, wd4, tm):
    tp, d = h.shape
    nt = tp // tm

    def body(h_ref, g_ref, wg_ref, wu_ref, wd_ref, ho_ref, ga_ref, gb_ref, sa_ref, n_ref, n_sc, acc):
        s = pl.program_id(1)

        @pl.when(s == 0)
        def _():
            hh = h_ref[...]
            rstd, xhat = _rms_stats(hh)
            n = (xhat * g_ref[...]).astype(BF16)
            n_sc[...] = n
            n_ref[...] = n
            acc[...] = jnp.zeros_like(acc)

        n = n_sc[...]
        a = _dot_nt(n, wg_ref[0])
        b = _dot_nt(n, wu_ref[0])
        sig = _sigmoid(a)
        sil = a * sig
        ga_ref[0] = (sig * (1.0 + a * (1.0 - sig)) * b).astype(BF16)
        gb_ref[0] = sil.astype(BF16)
        sg = (sil * b).astype(BF16)
        sa_ref[0] = sg
        acc[...] += _dot(sg, wd_ref[0])

        @pl.when(s == N_SHARD - 1)
        def _():
            ho_ref[...] = h_ref[...] + 0.5 * acc[...]

    return pl.pallas_call(
        body, name="ffn_fwd",
        grid=(nt, N_SHARD),
        in_specs=[
            pl.BlockSpec((tm, d), lambda i, s: (i, 0)),
            pl.BlockSpec((1, d), lambda i, s: (0, 0)),
            pl.BlockSpec((1, FF_SHARD, d), lambda i, s: (s, 0, 0)),
            pl.BlockSpec((1, FF_SHARD, d), lambda i, s: (s, 0, 0)),
            pl.BlockSpec((1, FF_SHARD, d), lambda i, s: (s, 0, 0)),
        ],
        out_specs=[
            pl.BlockSpec((tm, d), lambda i, s: (i, 0)),
            pl.BlockSpec((1, tm, FF_SHARD), lambda i, s: (s, i, 0)),
            pl.BlockSpec((1, tm, FF_SHARD), lambda i, s: (s, i, 0)),
            pl.BlockSpec((1, tm, FF_SHARD), lambda i, s: (s, i, 0)),
            pl.BlockSpec((tm, d), lambda i, s: (i, 0)),
        ],
        out_shape=[
            jax.ShapeDtypeStruct((tp, d), F32),
            jax.ShapeDtypeStruct((N_SHARD, tp, FF_SHARD), BF16),
            jax.ShapeDtypeStruct((N_SHARD, tp, FF_SHARD), BF16),
            jax.ShapeDtypeStruct((N_SHARD, tp, FF_SHARD), BF16),
            jax.ShapeDtypeStruct((tp, d), BF16),
        ],
        scratch_shapes=[pltpu.VMEM((tm, d), BF16), pltpu.VMEM((tm, d), F32)],
        compiler_params=_cparams(2),
    )(h, norm, wg4, wu4, wd4)


def _ffn_bwd_act(dho, h, norm, ga4, gb4, wg4, wu4, wd4, tm, after=()):
    tp, d = h.shape
    nt = tp // tm

    def body(dho_ref, h_ref, g_ref, ga_ref, gb_ref, wg_ref, wu_ref, wd_ref, *rest):
        dh_ref, da_ref, db_ref, dg_ref, dy_ref, dn_sc = rest[len(after):]
        i = pl.program_id(0)
        s = pl.program_id(1)

        @pl.when(s == 0)
        def _():
            dy_ref[...] = (0.5 * dho_ref[...]).astype(BF16)
            dn_sc[...] = jnp.zeros_like(dn_sc)

        @pl.when((s == 0) & (i == 0))
        def _():
            dg_ref[...] = jnp.zeros_like(dg_ref)

        hb = tm // FFN_ROW_BLOCKS
        rows = [pl.ds(k * hb, hb) for k in range(FFN_ROW_BLOCKS)]

        def shard_step(last):
            ds = _dot_nt(dy_ref[rows[0], :], wd_ref[0])
            dg = None
            for k in range(FFN_ROW_BLOCKS):
                ds_next = _dot_nt(dy_ref[rows[k + 1], :], wd_ref[0]) if k + 1 < FFN_ROW_BLOCKS else None
                da = (ds * ga_ref[0, rows[k], :].astype(F32)).astype(BF16)
                db = (ds * gb_ref[0, rows[k], :].astype(F32)).astype(BF16)
                da_ref[0, rows[k], :] = da
                db_ref[0, rows[k], :] = db
                dn = dn_sc[rows[k], :] + (_dot(da, wg_ref[0]) + _dot(db, wu_ref[0]))
                if last:
                    rstd, xhat = _rms_stats(h_ref[rows[k], :])
                    dh, dgk = _rms_bwd(dn, g_ref[...], rstd, xhat)
                    dh_ref[rows[k], :] = dho_ref[rows[k], :] + dh
                    dg = dgk if dg is None else dg + dgk
                else:
                    dn_sc[rows[k], :] = dn
                ds = ds_next
            if last:
                dg_ref[...] += dg

        pl.when(s < N_SHARD - 1)(functools.partial(shard_step, False))
        pl.when(s == N_SHARD - 1)(functools.partial(shard_step, True))

    return pl.pallas_call(
        body, name="ffn_bwd_act",
        grid=(nt, N_SHARD),
        in_specs=[
            pl.BlockSpec((tm, d), lambda i, s: (i, 0)),
            pl.BlockSpec((tm, d), lambda i, s: (i, 0)),
            pl.BlockSpec((1, d), lambda i, s: (0, 0)),
            pl.BlockSpec((1, tm, FF_SHARD), lambda i, s: (s, i, 0)),
            pl.BlockSpec((1, tm, FF_SHARD), lambda i, s: (s, i, 0)),
            pl.BlockSpec((1, FF_SHARD, d), lambda i, s: (s, 0, 0)),
            pl.BlockSpec((1, FF_SHARD, d), lambda i, s: (s, 0, 0)),
            pl.BlockSpec((1, FF_SHARD, d), lambda i, s: (s, 0, 0)),
        ] + [pl.BlockSpec(memory_space=pl.ANY)] * len(after),
        out_specs=[
            pl.BlockSpec((tm, d), lambda i, s: (i, 0)),
            pl.BlockSpec((1, tm, FF_SHARD), lambda i, s: (s, i, 0)),
            pl.BlockSpec((1, tm, FF_SHARD), lambda i, s: (s, i, 0)),
            pl.BlockSpec((1, d), lambda i, s: (0, 0)),
            pl.BlockSpec((tm, d), lambda i, s: (i, 0)),
        ],
        out_shape=[
            jax.ShapeDtypeStruct((tp, d), F32),
            jax.ShapeDtypeStruct((N_SHARD, tp, FF_SHARD), BF16),
            jax.ShapeDtypeStruct((N_SHARD, tp, FF_SHARD), BF16),
            jax.ShapeDtypeStruct((1, d), F32),
            jax.ShapeDtypeStruct((tp, d), BF16),
        ],
        scratch_shapes=[pltpu.VMEM((tm, d), F32)],
        compiler_params=_cparams(2),
    )(dho, h, norm, ga4, gb4, wg4, wu4, wd4, *after)


def _wgrad_tiles(tp, target):
    tm = min(target, tp) // MXU_DEPTH * MXU_DEPTH
    nt = tp // tm
    tail = tp - nt * tm
    assert tail == 0 or (nt * tm) % tail == 0 and tail % 16 == 0, (tp, tm, tail)
    return tm, nt, tail


def _ffn_bwd_w(pairs, tiles, after=()):
    npair = len(pairs)
    tp, d = pairs[0][1].shape
    tm, nt, tail = tiles
    rhs_list = []
    for _, rhs, _ in pairs:
        if all(rhs is not r for r in rhs_list):
            rhs_list.append(rhs)
    rhs_of = [[rhs is r for r in rhs_list].index(True) for _, rhs, _ in pairs]
    nrhs = len(rhs_list)
    nin = nrhs + npair
    ntail = nin if tail else 0

    def body(*refs):
        rhs_refs = refs[:nrhs]
        lhs_refs = refs[nrhs:nin]
        rhs_tails = refs[nin:nin + ntail][:nrhs]
        lhs_tails = refs[nin:nin + ntail][nrhs:]
        rest = refs[nin + ntail + len(after):]
        out_refs, accs = rest[:npair], rest[npair:]
        i = pl.program_id(1)

        @pl.when(i == 0)
        def _():
            for acc in accs:
                acc[...] = jnp.zeros_like(acc)

        def accumulate(lhs, rhs):
            for k, (_, _, scale) in enumerate(pairs):
                r = rhs[rhs_of[k]][...]
                if scale is not None:
                    r = (scale * r).astype(BF16)
                accs[k][...] += _dot_tn(lhs[k][0], r)

        accumulate(lhs_refs, rhs_refs)

        @pl.when(i == nt - 1)
        def _():
            if tail:
                accumulate(lhs_tails, rhs_tails)
            for k in range(npair):
                out_refs[k][0] = accs[k][...].astype(BF16)

    tail_specs, tail_args = [], []
    if tail:
        tb = nt * tm // tail
        tail_specs = ([pl.BlockSpec((tail, d), lambda s, i: (tb, 0))] * nrhs
                      + [pl.BlockSpec((1, tail, FF_SHARD), lambda s, i: (s, tb, 0))] * npair)
        tail_args = [*rhs_list, *[lhs for lhs, _, _ in pairs]]
    return pl.pallas_call(
        body, name="ffn_bwd_w",
        grid=(N_SHARD, nt),
        in_specs=[pl.BlockSpec((tm, d), lambda s, i: (i, 0))] * nrhs
        + [pl.BlockSpec((1, tm, FF_SHARD), lambda s, i: (s, i, 0))] * npair
        + tail_specs
        + [pl.BlockSpec(memory_space=pl.ANY)] * len(after),
        out_specs=[pl.BlockSpec((1, FF_SHARD, d), lambda s, i: (s, 0, 0))] * npair,
        out_shape=[jax.ShapeDtypeStruct((N_SHARD, FF_SHARD, d), BF16)] * npair,
        scratch_shapes=[pltpu.VMEM((FF_SHARD, d), F32)] * npair,
        compiler_params=_cparams(2),
    )(*rhs_list, *[lhs for lhs, _, _ in pairs], *tail_args, *after)


def _proj_fwd(h, norm, w_pieces, tm, wt=False):
    tp, d = h.shape
    widths = [bs[-2] if wt else bs[-1] for _, bs, _ in w_pieces]
    ntot = sum(widths)
    npc = len(w_pieces)

    def body(*refs):
        h_ref, g_ref = refs[:2]
        w_refs = refs[2:2 + npc]
        p_ref, n_ref = refs[2 + npc:]
        rstd, xhat = _rms_stats(h_ref[...])
        n = (xhat * g_ref[...]).astype(BF16)
        n_ref[...] = n
        off = 0
        for k in range(npc):
            w = w_refs[k][...]
            w = w.reshape(w.shape[-2], w.shape[-1])
            p_ref[:, off:off + widths[k]] = _dot_nt(n, w) if wt else _dot(n, w)
            off += widths[k]

    in_specs = [pl.BlockSpec((tm, d), lambda i: (i, 0)), pl.BlockSpec((1, d), lambda i: (0, 0))]
    for _, bs, idx in w_pieces:
        in_specs.append(pl.BlockSpec(bs, functools.partial(lambda i, idx: idx, idx=idx)))
    return pl.pallas_call(
        body, name="proj_fwd",
        grid=(tp // tm,),
        in_specs=in_specs,
        out_specs=[pl.BlockSpec((tm, ntot), lambda i: (i, 0)), pl.BlockSpec((tm, d), lambda i: (i, 0))],
        out_shape=[jax.ShapeDtypeStruct((tp, ntot), F32), jax.ShapeDtypeStruct((tp, d), BF16)],
        compiler_params=_cparams(1),
    )(h, norm, *[w for w, _, _ in w_pieces])


def _proj_bwd_act(dres, h, norm, dp_pieces, w_pieces, tm, wt=False):
    tp, d = h.shape
    npc = len(dp_pieces)
    nw = len(w_pieces)
    assert nw == npc or (nw == 1 and wt)

    def body(*refs):
        dres_ref, h_ref, g_ref = refs[:3]
        dp_refs = refs[3:3 + npc]
        w_refs = refs[3 + npc:3 + npc + nw]
        dh_ref, dg_ref = refs[3 + npc + nw:]
        i = pl.program_id(0)

        @pl.when(i == 0)
        def _():
            dg_ref[...] = jnp.zeros_like(dg_ref)

        def matmuls(rows):
            dn = None
            off = 0
            for k in range(npc):
                if nw == npc:
                    w = w_refs[k][...]
                    w = w.reshape(w.shape[-2], w.shape[-1])
                else:
                    w = w_refs[0][off:off + dp_pieces[k].shape[1], :]
                    off += dp_pieces[k].shape[1]
                t = _dot(dp_refs[k][rows, :], w) if wt else _dot_nt(dp_refs[k][rows, :], w)
                dn = t if dn is None else dn + t
            return dn

        hb = tm // FFN_ROW_BLOCKS
        blocks = [pl.ds(k * hb, hb) for k in range(FFN_ROW_BLOCKS)]
        dn = matmuls(blocks[0])
        dg = None
        for k, rows in enumerate(blocks):
            dn_next = matmuls(blocks[k + 1]) if k + 1 < FFN_ROW_BLOCKS else None
            rstd, xhat = _rms_stats(h_ref[rows, :])
            dh, dgk = _rms_bwd(dn, g_ref[...], rstd, xhat)
            dh_ref[rows, :] = dres_ref[rows, :] + dh
            dg = dgk if dg is None else dg + dgk
            dn = dn_next
        dg_ref[...] += dg

    in_specs = [pl.BlockSpec((tm, d), lambda i: (i, 0)), pl.BlockSpec((tm, d), lambda i: (i, 0)),
                pl.BlockSpec((1, d), lambda i: (0, 0))]
    for dp in dp_pieces:
        in_specs.append(pl.BlockSpec((tm, dp.shape[1]), lambda i: (i, 0)))
    for _, bs, idx in w_pieces:
        in_specs.append(pl.BlockSpec(bs, functools.partial(lambda i, idx: idx, idx=idx)))
    return pl.pallas_call(
        body, name="proj_bwd_act",
        grid=(tp // tm,),
        in_specs=in_specs,
        out_specs=[pl.BlockSpec((tm, d), lambda i: (i, 0)), pl.BlockSpec((1, d), lambda i: (0, 0))],
        out_shape=[jax.ShapeDtypeStruct((tp, d), F32), jax.ShapeDtypeStruct((1, d), F32)],
        compiler_params=_cparams(1),
    )(dres, h, norm, *dp_pieces, *[w for w, _, _ in w_pieces])


def _proj_bwd_w(n, dp_pieces, tiles, wt=False):
    tp, d = n.shape
    tm, nt, tail = tiles
    npc = len(dp_pieces)
    widths = [dp.shape[1] for dp in dp_pieces]
    oshape = (lambda w: (w, d)) if wt else (lambda w: (d, w))
    nin = 1 + npc
    ntail = nin if tail else 0

    def body(*refs):
        o_refs = refs[nin + ntail:nin + ntail + npc]
        accs = refs[nin + ntail + npc:]
        i = pl.program_id(0)

        @pl.when(i == 0)
        def _():
            for acc in accs:
                acc[...] = jnp.zeros_like(acc)

        def accumulate(n_ref, dp_refs):
            nn = n_ref[...]
            for k in range(npc):
                accs[k][...] += _dot_tn(dp_refs[k][...], nn) if wt else _dot_tn(nn, dp_refs[k][...])

        accumulate(refs[0], refs[1:nin])

        @pl.when(i == nt - 1)
        def _():
            if tail:
                accumulate(refs[nin], refs[nin + 1:nin + ntail])
            for k in range(npc):
                o_refs[k][...] = accs[k][...].astype(BF16)

    tail_specs = []
    if tail:
        tb = nt * tm // tail
        tail_specs = [pl.BlockSpec((tail, d), lambda i: (tb, 0))] + [pl.BlockSpec((tail, w), lambda i: (tb, 0))
                                                                    for w in widths]
    return pl.pallas_call(
        body, name="proj_bwd_w",
        grid=(nt,),
        in_specs=[pl.BlockSpec((tm, d), lambda i: (i, 0))]
        + [pl.BlockSpec((tm, w), lambda i: (i, 0)) for w in widths] + tail_specs,
        out_specs=[pl.BlockSpec(oshape(w), lambda i: (0, 0)) for w in widths],
        out_shape=[jax.ShapeDtypeStruct(oshape(w), BF16) for w in widths],
        scratch_shapes=[pltpu.VMEM(oshape(w), F32) for w in widths],
        compiler_params=_cparams(1),
    )(n, *dp_pieces, *((n, *dp_pieces) if tail else ()))


def _out_fwd(h, ya, yb, w, tm):
    tp, d = h.shape
    na, nb = ya.shape[1], yb.shape[1]

    def body(h_ref, ya_ref, yb_ref, w_ref, o_ref):
        y = _dot(ya_ref[...].astype(BF16), w_ref[0:na, :]) + _dot(yb_ref[...].astype(BF16), w_ref[na:, :])
        o_ref[...] = h_ref[...] + y

    return pl.pallas_call(
        body, name="out_fwd",
        grid=(tp // tm,),
        in_specs=[pl.BlockSpec((tm, d), lambda i: (i, 0)), pl.BlockSpec((tm, na), lambda i: (i, 0)),
                  pl.BlockSpec((tm, nb), lambda i: (i, 0)), pl.BlockSpec((d, d), lambda i: (0, 0))],
        out_specs=pl.BlockSpec((tm, d), lambda i: (i, 0)),
        out_shape=jax.ShapeDtypeStruct((tp, d), F32),
        compiler_params=_cparams(1),
    )(h, ya, yb, w)


def _out_bwd(dy, ya, yb, w, tm, after=()):
    tp, d = dy.shape
    na, nb = ya.shape[1], yb.shape[1]

    def body(dy_ref, ya_ref, yb_ref, w_ref, *rest):
        da_ref, db_ref, dw_ref, acc = rest[len(after):]
        i = pl.program_id(0)

        @pl.when(i == 0)
        def _():
            acc[...] = jnp.zeros_like(acc)

        dyb16 = dy_ref[...].astype(BF16)
        da_ref[...] = _dot_nt(dyb16, w_ref[0:na, :])
        db_ref[...] = _dot_nt(dyb16, w_ref[na:, :])
        acc[0:na, :] += _dot_tn(ya_ref[...].astype(BF16), dyb16)
        acc[na:, :] += _dot_tn(yb_ref[...].astype(BF16), dyb16)

        @pl.when(i == pl.num_programs(0) - 1)
        def _():
            dw_ref[...] = acc[...].astype(BF16)

    return pl.pallas_call(
        body, name="out_bwd",
        grid=(tp // tm,),
        in_specs=[pl.BlockSpec((tm, d), lambda i: (i, 0)), pl.BlockSpec((tm, na), lambda i: (i, 0)),
                  pl.BlockSpec((tm, nb), lambda i: (i, 0)), pl.BlockSpec((d, d), lambda i: (0, 0))]
        + [pl.BlockSpec(memory_space=pl.ANY)] * len(after),
        out_specs=[pl.BlockSpec((tm, na), lambda i: (i, 0)), pl.BlockSpec((tm, nb), lambda i: (i, 0)),
                   pl.BlockSpec((d, d), lambda i: (0, 0))],
        out_shape=[jax.ShapeDtypeStruct((tp, na), F32), jax.ShapeDtypeStruct((tp, nb), F32),
                   jax.ShapeDtypeStruct((d, d), BF16)],
        scratch_shapes=[pltpu.VMEM((d, d), F32)],
        compiler_params=_cparams(1),
    )(dy, ya, yb, w, *after)


def _loss_bwd(h, gfin, tgt, t_real, tm):
    tp, d = h.shape

    def body(h_ref, g_ref, t_ref, loss_ref, dh_ref, dg_ref):
        i = pl.program_id(0)

        @pl.when(i == 0)
        def _():
            loss_ref[...] = jnp.zeros_like(loss_ref)
            dg_ref[...] = jnp.zeros_like(dg_ref)

        rows = i * tm + lax.broadcasted_iota(jnp.int32, (tm, 1), 0)
        valid = (rows >= N_META) & (rows < t_real)
        rstd, xhat = _rms_stats(h_ref[...])
        g = g_ref[...]
        err = jnp.where(valid, xhat * g - t_ref[...], 0.0)
        e2 = jnp.sum(err * err, axis=1, keepdims=True)
        loss_ref[...] += (0.5 / d) * jnp.sum(e2, axis=0, keepdims=True)
        dy = err * (1.0 / d)
        dh, dg = _rms_bwd(dy, g, rstd, xhat)
        dh_ref[...] = dh
        dg_ref[...] += dg

    return pl.pallas_call(
        body, name="loss_bwd",
        grid=(tp // tm,),
        in_specs=[pl.BlockSpec((tm, d), lambda i: (i, 0)), pl.BlockSpec((1, d), lambda i: (0, 0)),
                  pl.BlockSpec((tm, d), lambda i: (i, 0))],
        out_specs=[pl.BlockSpec((1, 1), lambda i: (0, 0)), pl.BlockSpec((tm, d), lambda i: (i, 0)),
                   pl.BlockSpec((1, d), lambda i: (0, 0))],
        out_shape=[jax.ShapeDtypeStruct((1, 1), F32), jax.ShapeDtypeStruct((tp, d), F32),
                   jax.ShapeDtypeStruct((1, d), F32)],
        compiler_params=_cparams(1),
    )(h, gfin, tgt)


POOL_HALO = 16


def _pool_lane_consts(n_rows):
    lane = lax.broadcasted_iota(jnp.int32, (n_rows, D_POOL), 1)
    grp = lane // POOL_GROUP
    win = jnp.where(grp == 0, 2.0, jnp.where(grp == 1, 4.0, jnp.where(grp == 2, 8.0, 16.0)))
    return grp, win


def _pool_select(grp, s2, s4, s8, s16):
    return jnp.where(grp == 0, s2, jnp.where(grp == 1, s4, jnp.where(grp == 2, s8, s16)))


def _pool_mixed(x, row0, tr):
    n = tr + POOL_HALO
    s2 = x + pltpu.roll(x, 1, 0)
    s4 = s2 + pltpu.roll(s2, 2, 0)
    s8 = s4 + pltpu.roll(s4, 4, 0)
    s16 = s8 + pltpu.roll(s8, 8, 0)
    grp, win = _pool_lane_consts(n)
    rows = row0 - POOL_HALO + lax.broadcasted_iota(jnp.int32, (n, D_POOL), 0)
    cnt = jnp.minimum((rows + 1).astype(F32), win)
    pooled = _pool_select(grp, s2, s4, s8, s16) / jnp.maximum(cnt, 1.0)
    return (pooled - x)[POOL_HALO:, :]


def _pool_fwd(p, wbd, scale, tr):
    tp = p.shape[0]
    nt = tp // tr

    def body(p_ref, w_ref, s_ref, y_ref, usc):
        usc[0:POOL_HALO, :] = jnp.zeros((POOL_HALO, D_POOL), F32)
        usc[POOL_HALO:, :] = p_ref[...]

        def tile(r, carry):
            r0 = pl.multiple_of(r * tr, SUBLANE)
            x = usc[pl.ds(r0, tr + POOL_HALO), :]
            mixed = _pool_mixed(x, r0, tr)
            y_ref[pl.ds(r0, tr), :] = (_dot(mixed.astype(BF16), w_ref[...]) * s_ref[...]).astype(BF16)
            return carry

        lax.fori_loop(0, nt, tile, 0)

    return pl.pallas_call(
        body, name="pool_fwd",
        grid=(1,),
        in_specs=[pl.BlockSpec((tp, D_POOL), lambda i: (0, 0)), pl.BlockSpec((D_POOL, D_POOL), lambda i: (0, 0)),
                  pl.BlockSpec((1, D_POOL), lambda i: (0, 0))],
        out_specs=pl.BlockSpec((tp, D_POOL), lambda i: (0, 0)),
        out_shape=jax.ShapeDtypeStruct((tp, D_POOL), BF16),
        scratch_shapes=[pltpu.VMEM((tp + POOL_HALO, D_POOL), F32)],
        compiler_params=_cparams(1),
    )(p, wbd, scale)


def _pool_bwd(p, dya, wbd, scale, tr):
    tp = p.shape[0]
    nt = tp // tr

    def body(p_ref, dy_ref, w_ref, s_ref, du_ref, dw_ref, ds_ref, usc, gsc):
        usc[0:POOL_HALO, :] = jnp.zeros((POOL_HALO, D_POOL), F32)
        usc[POOL_HALO:, :] = p_ref[...]
        gsc[tp:, :] = jnp.zeros((POOL_HALO, D_POOL), F32)
        dw_ref[...] = jnp.zeros_like(dw_ref)
        ds_ref[...] = jnp.zeros_like(ds_ref)
        grp, win = _pool_lane_consts(tr)

        def tile1(r, carry):
            r0 = pl.multiple_of(r * tr, SUBLANE)
            x = usc[pl.ds(r0, tr + POOL_HALO), :]
            mixed = _pool_mixed(x, r0, tr).astype(BF16)
            dy = dy_ref[pl.ds(r0, tr), :]
            dys = (dy * s_ref[...]).astype(BF16)
            ypre = _dot(mixed, w_ref[...])
            ds_ref[...] += _colsum(dy * ypre)
            dw_ref[...] += _dot_tn(mixed, dys)
            dmx = _dot_nt(dys, w_ref[...])
            rows = r0 + lax.broadcasted_iota(jnp.int32, (tr, D_POOL), 0)
            cnt = jnp.minimum((rows + 1).astype(F32), win)
            gsc[pl.ds(r0, tr), :] = dmx / cnt
            return carry

        lax.fori_loop(0, nt, tile1, 0)
        n = tr + POOL_HALO
        grp2, win2 = _pool_lane_consts(n)

        def tile2(r, carry):
            r0 = pl.multiple_of(r * tr, SUBLANE)
            g = gsc[pl.ds(r0, n), :]
            s2 = g + pltpu.roll(g, n - 1, 0)
            s4 = s2 + pltpu.roll(s2, n - 2, 0)
            s8 = s4 + pltpu.roll(s4, n - 4, 0)
            s16 = s8 + pltpu.roll(s8, n - 8, 0)
            pooled_t = _pool_select(grp2, s2, s4, s8, s16)
            rows = r0 + lax.broadcasted_iota(jnp.int32, (n, D_POOL), 0)
            cnt = jnp.minimum((rows + 1).astype(F32), win2)
            du = pooled_t - g * cnt
            du_ref[pl.ds(r0, tr), :] = du[0:tr, :].astype(BF16)
            return carry

        lax.fori_loop(0, nt, tile2, 0)

    return pl.pallas_call(
        body, name="pool_bwd",
        grid=(1,),
        in_specs=[pl.BlockSpec((tp, D_POOL), lambda i: (0, 0)), pl.BlockSpec((tp, D_POOL), lambda i: (0, 0)),
                  pl.BlockSpec((D_POOL, D_POOL), lambda i: (0, 0)), pl.BlockSpec((1, D_POOL), lambda i: (0, 0))],
        out_specs=[pl.BlockSpec((tp, D_POOL), lambda i: (0, 0)), pl.BlockSpec((D_POOL, D_POOL), lambda i: (0, 0)),
                   pl.BlockSpec((1, D_POOL), lambda i: (0, 0))],
        out_shape=[jax.ShapeDtypeStruct((tp, D_POOL), BF16), jax.ShapeDtypeStruct((D_POOL, D_POOL), F32),
                   jax.ShapeDtypeStruct((1, D_POOL), F32)],
        scratch_shapes=[pltpu.VMEM((tp + POOL_HALO, D_POOL), F32), pltpu.VMEM((tp + POOL_HALO, D_POOL), F32)],
        compiler_params=_cparams(1),
    )(p, dya, wbd, scale)


def _hgrn_levels(ch):
    levels = []
    w = ch // 2
    while w >= 1:
        levels.append(w)
        w //= 2
    return levels


def _hgrn_consts(ch):
    t = np.arange(ch)
    tril = t[None, :] <= t[:, None]
    masks = []
    for w in _hgrn_levels(ch):
        blk = t // (2 * w)
        upper = t % (2 * w) >= w
        masks.append(upper[:, None] & (~upper)[None, :] & (blk[:, None] == blk[None, :]))
    masks.append(tril)
    msk = np.stack(masks).astype(np.float32)
    return jnp.asarray(tril.astype(np.float32), BF16), jnp.asarray(msk, F32), len(masks) - 1


def _split3(x):
    hi = x.astype(BF16)
    r1 = x - hi.astype(F32)
    mid = r1.astype(BF16)
    lo = (r1 - mid.astype(F32)).astype(BF16)
    return hi, mid, lo


def _hgrn_exponents(tril, logf):
    ch = logf.shape[0]
    hi, mid, lo = _split3(logf)
    x = _dot(tril, jnp.concatenate([hi, mid, lo], axis=1))
    b = x[:, 0:HEAD] + x[:, HEAD:2 * HEAD] + x[:, 2 * HEAD:3 * HEAD]
    rows = lax.broadcasted_iota(jnp.int32, (ch, HEAD), 0)
    fx = jnp.broadcast_to(b[ch - 1:ch, :], (ch, HEAD)) - b
    lev = []
    for w in _hgrn_levels(ch):
        pos = rows % (2 * w)
        upper = pos >= w
        if w >= SUBLANE:
            parts = [jnp.broadcast_to(b[k * 2 * w + w - 1:k * 2 * w + w, :], (2 * w, HEAD))
                     for k in range(ch // (2 * w))]
            bmid = parts[0] if len(parts) == 1 else jnp.concatenate(parts, axis=0)
            dx = jnp.where(upper, b - bmid, 0.0)
            ex = jnp.where(upper, 0.0, bmid - b)
        else:
            dx = logf
            ex = jnp.zeros_like(logf)
            for i in range(1, w):
                dx = dx + jnp.where(pos >= w + i, pltpu.roll(logf, i, 0), 0.0)
                ex = ex + jnp.where(pos <= w - 1 - i, pltpu.roll(logf, ch - i, 0), 0.0)
            dx = jnp.where(upper, dx, 0.0)
        lev.append((dx, ex))
    return b, fx, lev


def _hgrn_exponents_bwd(tril, d_b, d_fx, d_blast, lev_grads):
    ch = d_b.shape[0]
    rows = lax.broadcasted_iota(jnp.int32, (ch, HEAD), 0)
    db = d_b - d_fx
    dlf = jnp.zeros_like(d_b)
    for w, (ddx, dex) in zip(_hgrn_levels(ch), lev_grads):
        pos = rows % (2 * w)
        upper = pos >= w
        gu = jnp.where(upper, ddx, 0.0)
        if w >= SUBLANE:
            gl = jnp.where(upper, 0.0, dex)
            db = db + gu - gl
            diff = gl - gu
            for k in range(ch // (2 * w)):
                s = _colsum(diff[k * 2 * w:(k + 1) * 2 * w, :])
                db = db + jnp.where(rows == k * 2 * w + w - 1, s, 0.0)
        else:
            dlf = dlf + gu
            for i in range(1, w):
                dlf = dlf + pltpu.roll(jnp.where(pos >= w + i, gu, 0.0), ch - i, 0)
                dlf = dlf + pltpu.roll(jnp.where(pos <= w - 1 - i, dex, 0.0), i, 0)
    db = db + jnp.where(rows == ch - 1, _colsum(d_fx) + d_blast, 0.0)
    hi = db.astype(BF16)
    lo = (db - hi.astype(F32)).astype(BF16)
    d2 = _dot_tn(tril, jnp.concatenate([hi, lo], axis=1))
    return d2[:, 0:HEAD] + d2[:, HEAD:2 * HEAD] + dlf


def _lockstep(gens):
    results = [None] * len(gens)
    live = list(range(len(gens)))
    while live:
        for i in list(live):
            try:
                next(gens[i])
            except StopIteration as stop:
                results[i] = stop.value
                live.remove(i)
    return results


def _hgrn_gates(q_raw, z, lb):
    sz = _sigmoid(z)
    f = lb + (1.0 - lb) * sz
    q = q_raw * _sigmoid(q_raw)
    k = (1.0 - lb) * (1.0 - sz)
    return q, k, f, sz


def _hgrn_intra(q, k, lev, msk_ref, n_lev, ch):
    eye = (lax.broadcasted_iota(jnp.int32, (ch, ch), 0) == lax.broadcasted_iota(jnp.int32, (ch, ch), 1))
    a = jnp.where(eye, jnp.sum(q * k, axis=1, keepdims=True), 0.0)
    ops = []
    for lv in range(n_lev):
        eq = jnp.exp(lev[lv][0])
        ek = jnp.exp(lev[lv][1])
        qd = q * eq
        kd = k * ek
        a = a + msk_ref[lv] * _dot_nt(qd.astype(BF16), kd.astype(BF16))
        ops.append((eq, ek, qd, kd))
        yield
    return a, ops


def _hgrn_fwd(p, lb_logits, gnorm, mst, msk, n_lev, tm):
    tp = p.shape[0]
    ch = HG_CHUNK
    nct = tm // ch
    nt = tp // tm
    nr = mst.shape[0]
    base = D_POOL // HEAD

    hp = HG_HEADS_PER_STEP
    wide = hp * HEAD
    npr = wide // HG_PBLOCK

    def body(*refs):
        p_refs = refs[:4 * npr]
        lg_ref, gn_ref, mst_ref, msk_ref, y_ref, ss_ref, st_sc = refs[4 * npr:]

        @pl.when(pl.program_id(1) == 0)
        def _():
            st_sc[...] = jnp.zeros_like(st_sc)

        lb_all = _sigmoid(lg_ref[0:1, :] - lg_ref[1:2, :])

        def raw(seg, hh, r0):
            per = HG_PBLOCK // HEAD
            return p_refs[seg * npr + hh // per][pl.ds(r0, ch), (hh % per) * HEAD:(hh % per + 1) * HEAD]

        def one_head(hh, c, r0):
            ls = slice(hh * HEAD, (hh + 1) * HEAD)
            q_raw, z, v, g_raw, st = raw(0, hh, r0), raw(1, hh, r0), raw(2, hh, r0), raw(3, hh, r0), st_sc[hh]
            q, k, f, _ = _hgrn_gates(q_raw, z, lb_all[:, ls])
            yield
            b, fx, lev = _hgrn_exponents(mst_ref[...], jnp.log(f))
            yield
            qe = q * jnp.exp(b)
            a, _ = yield from _hgrn_intra(q, k, lev, msk_ref, n_lev, ch)
            v16 = v.astype(BF16)
            o = _dot_nt(qe.astype(BF16), st.astype(BF16)) + _dot(a.astype(BF16), v16)
            kl = k * jnp.exp(fx)
            st_new = st * jnp.exp(b[ch - 1:ch, :]) + _dot_tn(v16, kl.astype(BF16))
            yield
            rstd = lax.rsqrt(jnp.mean(o * o, axis=-1, keepdims=True) + EPS)
            return st, st_new, o * rstd * gn_ref[...] * (g_raw * _sigmoid(g_raw))

        def chunk(c, carry):
            r0 = pl.multiple_of(c * ch, ch)
            results = _lockstep([one_head(hh, c, r0) for hh in range(hp)])
            for hh, (st, st_new, y) in enumerate(results):
                ss_ref[hh, c] = st
                st_sc[hh] = st_new
                y_ref[pl.ds(r0, ch), hh * HEAD:(hh + 1) * HEAD] = y.astype(BF16)
            return carry

        lax.fori_loop(0, nct, chunk, 0)

    def pspec(seg, part):
        return pl.BlockSpec((tm, HG_PBLOCK),
                            lambda h, i: (i, (base + seg * HG_HEADS) * HEAD // HG_PBLOCK + h * npr + part))

    return pl.pallas_call(
        body, name="hgrn_fwd",
        grid=(HG_HEADS // hp, nt),
        in_specs=[pspec(seg, part) for seg in range(4) for part in range(npr)]
        + [pl.BlockSpec((2, wide), lambda h, i: (0, h)),
           pl.BlockSpec((1, HEAD), lambda h, i: (0, 0)),
           pl.BlockSpec((nr, ch), lambda h, i: (0, 0)),
           pl.BlockSpec((n_lev + 1, ch, ch), lambda h, i: (0, 0, 0))],
        out_specs=[pl.BlockSpec((tm, wide), lambda h, i: (i, h)),
                   pl.BlockSpec((hp, nct, HEAD, HEAD), lambda h, i: (h, i, 0, 0))],
        out_shape=[jax.ShapeDtypeStruct((tp, D_HGRN), BF16),
                   jax.ShapeDtypeStruct((HG_HEADS, tp // ch, HEAD, HEAD), F32)],
        scratch_shapes=[pltpu.VMEM((hp, HEAD, HEAD), F32)],
        compiler_params=_cparams(2),
    )(*([p] * (4 * npr)), lb_logits, gnorm, mst, msk)


def _hgrn_bwd(p, dyb, states, lb_logits, gnorm, mst, msk, n_lev, tm):
    tp = p.shape[0]
    ch = HG_CHUNK
    nct = tm // ch
    nt = tp // tm
    nr = mst.shape[0]
    base = D_POOL // HEAD

    hp = HG_HEADS_PER_STEP
    wide = hp * HEAD
    npr = wide // HG_PBLOCK

    def body(*refs):
        p_refs = refs[:4 * npr]
        (dy_ref, ss_ref, lg_ref, gn_ref, mst_ref, msk_ref,
         dq_ref, dz_ref, dv_ref, dg_ref, dlg_ref, dgn_ref, dst_sc, dlb_sc) = refs[4 * npr:]
        ti = pl.program_id(1)

        def raw(seg, hh, r0):
            per = HG_PBLOCK // HEAD
            return p_refs[seg * npr + hh // per][pl.ds(r0, ch), (hh % per) * HEAD:(hh % per + 1) * HEAD]

        @pl.when(ti == 0)
        def _():
            dst_sc[...] = jnp.zeros_like(dst_sc)
            dlb_sc[...] = jnp.zeros_like(dlb_sc)
            dgn_ref[...] = jnp.zeros_like(dgn_ref)

        lb_all = _sigmoid(lg_ref[0:1, :] - lg_ref[1:2, :])
        gn = gn_ref[...]

        def load_head(hh, c, r0):
            ls = slice(hh * HEAD, (hh + 1) * HEAD)
            return (raw(0, hh, r0), raw(1, hh, r0), raw(2, hh, r0), raw(3, hh, r0),
                    dy_ref[pl.ds(r0, ch), ls], ss_ref[hh, c], dst_sc[hh])

        def store_head(hh, r0, res):
            ls = slice(hh * HEAD, (hh + 1) * HEAD)
            dq_raw, dz, dv, dg_raw, dgn, dst_new, dlb = res
            dq_ref[pl.ds(r0, ch), ls] = dq_raw
            dz_ref[pl.ds(r0, ch), ls] = dz
            dv_ref[pl.ds(r0, ch), ls] = dv
            dg_ref[pl.ds(r0, ch), ls] = dg_raw
            dgn_ref[hh] += dgn
            dst_sc[hh] = dst_new
            dlb_sc[:, ls] += dlb

        def one_head(hh, loaded):
            ls = slice(hh * HEAD, (hh + 1) * HEAD)
            lb = lb_all[:, ls]
            q_raw, z, v, g_raw, dy, st, dst = loaded
            q, k, f, sz = _hgrn_gates(q_raw, z, lb)
            yield
            b, fx, lev = _hgrn_exponents(mst_ref[...], jnp.log(f))
            yield
            eb = jnp.exp(b)
            ef = jnp.exp(fx)
            elast = jnp.exp(b[ch - 1:ch, :])
            qe = q * eb
            kl = k * ef
            a, ops = yield from _hgrn_intra(q, k, lev, msk_ref, n_lev, ch)
            v16 = v.astype(BF16)
            st16 = st.astype(BF16)
            qe16 = qe.astype(BF16)
            kl16 = kl.astype(BF16)
            a16 = a.astype(BF16)
            o = _dot_nt(qe16, st16) + _dot(a16, v16)
            yield
            sg = _sigmoid(g_raw)
            rstd = lax.rsqrt(jnp.mean(o * o, axis=-1, keepdims=True) + EPS)
            oh = o * rstd
            dg_out = (dy * oh * gn * (sg * (1.0 + g_raw * (1.0 - sg)))).astype(BF16)
            don = dy * (g_raw * sg)
            dgn = _colsum(don * oh)
            doh = don * gn
            do = rstd * (doh - oh * jnp.mean(doh * oh, axis=-1, keepdims=True))
            do16 = do.astype(BF16)
            dst16 = dst.astype(BF16)
            yield
            dv = _dot_tn(a16, do16) + _dot_nt(kl16, dst16)
            da = msk_ref[n_lev] * _dot_nt(do16, v16)
            dqe = _dot(do16, st16)
            dkl = _dot(v16, dst16)
            dst_new = dst * elast + _dot_tn(do16, qe16)
            yield
            db_last = _colsum(dst * st) * elast
            dad = jnp.sum(do * v, axis=1, keepdims=True)
            dq = dad * k + dqe * eb
            dk = dad * q + dkl * ef
            lev_grads = []
            for lv in range(n_lev):
                eq, ek, qd, kd = ops[lv]
                gl = (msk_ref[lv] * da).astype(BF16)
                dqd = _dot(gl, kd.astype(BF16))
                dkd = _dot_tn(gl, qd.astype(BF16))
                dq = dq + dqd * eq
                dk = dk + dkd * ek
                lev_grads.append((dqd * qd, dkd * kd))
                yield
            dlogf = _hgrn_exponents_bwd(mst_ref[...], dqe * qe, dkl * kl, db_last, lev_grads)
            yield
            sq = _sigmoid(q_raw)
            dq_out = (dq * (sq * (1.0 + q_raw * (1.0 - sq)))).astype(BF16)
            dfk = dlogf / f - dk
            dz_out = (dfk * (1.0 - lb) * sz * (1.0 - sz)).astype(BF16)
            return dq_out, dz_out, dv.astype(BF16), dg_out, dgn, dst_new, _colsum(dfk * (1.0 - sz))

        def chunk(cc, carry):
            c = nct - 1 - cc
            r0 = pl.multiple_of(c * ch, ch)
            loaded = [load_head(hh, c, r0) for hh in range(HG_HEADS_PER_STEP)]
            results = _lockstep([one_head(hh, loaded[hh]) for hh in range(HG_HEADS_PER_STEP)])
            for hh in range(HG_HEADS_PER_STEP):
                store_head(hh, r0, results[hh])
            return carry

        lax.fori_loop(0, nct, chunk, 0, unroll=1)

        @pl.when(ti == nt - 1)
        def _():
            dl0 = dlb_sc[...] * lb_all * (1.0 - lb_all)
            dlg_ref[0:1, :] = dl0
            dlg_ref[1:2, :] = -dl0

    def pspec(seg, part):
        return pl.BlockSpec((tm, HG_PBLOCK), lambda h, i: (
            nt - 1 - i, (base + seg * HG_HEADS) * HEAD // HG_PBLOCK + h * npr + part))

    ospec = pl.BlockSpec((tm, wide), lambda h, i: (nt - 1 - i, h))
    return pl.pallas_call(
        body, name="hgrn_bwd",
        grid=(HG_HEADS // hp, nt),
        in_specs=[pspec(seg, part) for seg in range(4) for part in range(npr)]
        + [ospec, pl.BlockSpec((hp, nct, HEAD, HEAD), lambda h, i: (h, nt - 1 - i, 0, 0)),
           pl.BlockSpec((2, wide), lambda h, i: (0, h)),
           pl.BlockSpec((1, HEAD), lambda h, i: (0, 0)),
           pl.BlockSpec((nr, ch), lambda h, i: (0, 0)),
           pl.BlockSpec((n_lev + 1, ch, ch), lambda h, i: (0, 0, 0))],
        out_specs=[ospec, ospec, ospec, ospec,
                   pl.BlockSpec((2, wide), lambda h, i: (0, h)),
                   pl.BlockSpec((hp, 1, HEAD), lambda h, i: (h, 0, 0))],
        out_shape=[jax.ShapeDtypeStruct((tp, D_HGRN), BF16)] * 4
        + [jax.ShapeDtypeStruct((2, D_HGRN), F32), jax.ShapeDtypeStruct((HG_HEADS, 1, HEAD), F32)],
        scratch_shapes=[pltpu.VMEM((hp, HEAD, HEAD), F32), pltpu.VMEM((1, wide), F32)],
        compiler_params=_cparams(2),
    )(*([p] * (4 * npr)), dyb, states, lb_logits, gnorm, mst, msk)


def _tap_views(x, tr, halo, width):
    subs = {0: x}
    views = []
    for j in range(width):
        tiles, rem = divmod(width - 1 - j, SUBLANE)
        if rem not in subs:
            subs[rem] = pltpu.roll(x, rem, 0)
        start = halo - tiles * SUBLANE
        views.append(subs[rem][start:start + tr, :])
    return views


def _tap_views_t(y, tr, halo, width):
    n = tr + halo
    subs = {0: y}
    views = []
    for j in range(width):
        tiles, rem = divmod(width - 1 - j, SUBLANE)
        if rem not in subs:
            subs[rem] = pltpu.roll(y, n - rem, 0)
        views.append(subs[rem][tiles * SUBLANE:tiles * SUBLANE + tr, :])
    return views


def _weighted_sum(views, w_ref):
    acc = None
    for j, view in enumerate(views):
        term = view * w_ref[j:j + 1, :]
        acc = term if acc is None else acc + term
    return acc


def _conv_taps(x, w_ref, tr, halo, width):
    return _weighted_sum(_tap_views(x, tr, halo, width), w_ref)


def _conv_taps_t(y, w_ref, tr, halo, width):
    return _weighted_sum(_tap_views_t(y, tr, halo, width), w_ref)


def _ln_stats(cv):
    mu = jnp.mean(cv, axis=-1, keepdims=True)
    xc = cv - mu
    rstd = lax.rsqrt(jnp.mean(xc * xc, axis=-1, keepdims=True) + EPS)
    return rstd, xc * rstd


def _convmod_fwd(p, w, bias, ln_g, ln_b, tr):
    tp = p.shape[0]
    nt = tp // tr
    nb = D_CONV // HEAD

    def body(a_ref, b_ref, w_ref, bi_ref, g_ref, be_ref, y_ref, cv_ref, usc):
        usc[0:CONV_HALO, :] = jnp.zeros((CONV_HALO, HEAD), F32)
        usc[CONV_HALO:, :] = a_ref[...] * _sigmoid(b_ref[...])

        def tile(r, carry):
            r0 = pl.multiple_of(r * tr, SUBLANE)
            x = usc[pl.ds(r0, tr + CONV_HALO), :]
            cv = _conv_taps(x, w_ref, tr, CONV_HALO, CONV_WIDTH) + bi_ref[...]
            cv_ref[pl.ds(r0, tr), :] = cv
            _, xh = _ln_stats(cv)
            un = xh * g_ref[...] + be_ref[...]
            y_ref[pl.ds(r0, tr), :] = (un * _sigmoid(un)).astype(BF16)
            return carry

        lax.fori_loop(0, nt, tile, 0)

    vec = lambda: pl.BlockSpec((1, HEAD), lambda j: (0, j))
    return pl.pallas_call(
        body, name="convmod_fwd",
        grid=(nb,),
        in_specs=[pl.BlockSpec((tp, HEAD), lambda j: (0, j)), pl.BlockSpec((tp, HEAD), lambda j: (0, nb + j)),
                  pl.BlockSpec((CONV_HALO, HEAD), lambda j: (0, j)), vec(), vec(), vec()],
        out_specs=[pl.BlockSpec((tp, HEAD), lambda j: (0, j))] * 2,
        out_shape=[jax.ShapeDtypeStruct((tp, D_CONV), BF16), jax.ShapeDtypeStruct((tp, D_CONV), F32)],
        scratch_shapes=[pltpu.VMEM((tp + CONV_HALO, HEAD), F32)],
        compiler_params=_cparams(1),
    )(p, p, w, bias, ln_g, ln_b)


def _convmod_bwd(p, cv_saved, dyc, w, ln_g, ln_b, tr):
    tp = p.shape[0]
    nt = tp // tr
    nb = D_CONV // HEAD

    def body(a_ref, b_ref, cv_ref, dy_ref, w_ref, g_ref, be_ref, da_ref, db_ref, dw_ref, dv_ref, usc, dsc):
        usc[0:CONV_HALO, :] = jnp.zeros((CONV_HALO, HEAD), F32)
        usc[CONV_HALO:, :] = a_ref[...] * _sigmoid(b_ref[...])
        dsc[tp:, :] = jnp.zeros((CONV_HALO, HEAD), F32)
        dw_ref[...] = jnp.zeros_like(dw_ref)
        dv_ref[...] = jnp.zeros_like(dv_ref)

        def tile1(r, carry):
            r0 = pl.multiple_of(r * tr, SUBLANE)
            x = usc[pl.ds(r0, tr + CONV_HALO), :]
            views = _tap_views(x, tr, CONV_HALO, CONV_WIDTH)
            rstd, xh = _ln_stats(cv_ref[pl.ds(r0, tr), :])
            un = xh * g_ref[...] + be_ref[...]
            sg = _sigmoid(un)
            dun = dy_ref[pl.ds(r0, tr), :] * (sg * (1.0 + un * (1.0 - sg)))
            dv_ref[0, 1:2, :] += _colsum(dun * xh)
            dv_ref[0, 2:3, :] += _colsum(dun)
            dxh = dun * g_ref[...]
            dcv = rstd * (dxh - jnp.mean(dxh, axis=-1, keepdims=True)
                          - xh * jnp.mean(dxh * xh, axis=-1, keepdims=True))
            dv_ref[0, 0:1, :] += _colsum(dcv)
            for j in range(CONV_WIDTH):
                dw_ref[0, j:j + 1, :] += _colsum(dcv * views[j])
            dsc[pl.ds(r0, tr), :] = dcv
            return carry

        lax.fori_loop(0, nt, tile1, 0)

        def tile2(r, carry):
            r0 = pl.multiple_of(r * tr, SUBLANE)
            y = dsc[pl.ds(r0, tr + CONV_HALO), :]
            du = _conv_taps_t(y, w_ref, tr, CONV_HALO, CONV_WIDTH)
            a = a_ref[pl.ds(r0, tr), :]
            sb = _sigmoid(b_ref[pl.ds(r0, tr), :])
            da_ref[pl.ds(r0, tr), :] = (du * sb).astype(BF16)
            db_ref[pl.ds(r0, tr), :] = (du * a * sb * (1.0 - sb)).astype(BF16)
            return carry

        lax.fori_loop(0, nt, tile2, 0)

    vec = lambda: pl.BlockSpec((1, HEAD), lambda j: (0, j))
    col = lambda: pl.BlockSpec((tp, HEAD), lambda j: (0, j))
    return pl.pallas_call(
        body, name="convmod_bwd",
        grid=(nb,),
        in_specs=[col(), pl.BlockSpec((tp, HEAD), lambda j: (0, nb + j)), col(), col(),
                  pl.BlockSpec((CONV_HALO, HEAD), lambda j: (0, j)), vec(), vec()],
        out_specs=[col(), col(), pl.BlockSpec((1, CONV_HALO, HEAD), lambda j: (j, 0, 0)),
                   pl.BlockSpec((1, SUBLANE, HEAD), lambda j: (j, 0, 0))],
        out_shape=[jax.ShapeDtypeStruct((tp, D_CONV), BF16), jax.ShapeDtypeStruct((tp, D_CONV), BF16),
                   jax.ShapeDtypeStruct((nb, CONV_HALO, HEAD), F32), jax.ShapeDtypeStruct((nb, SUBLANE, HEAD), F32)],
        scratch_shapes=[pltpu.VMEM((tp + CONV_HALO, HEAD), F32), pltpu.VMEM((tp + CONV_HALO, HEAD), F32)],
        compiler_params=_cparams(1),
    )(p, p, cv_saved, dyc, w, ln_g, ln_b)


def _log1p_small(y):
    return jnp.where(y < 1e-4, y * (1.0 - 0.5 * y), jnp.log(1.0 + y))


def _softplus(x):
    return jnp.maximum(x, 0.0) + _log1p_small(jnp.exp(-jnp.abs(x)))


def _expm1(x):
    return jnp.where(jnp.abs(x) < 1e-2, x * (1.0 + 0.5 * x * (1.0 + x * (1.0 / 3.0))), jnp.exp(x) - 1.0)


def _gelu_parts(x):
    c = 0.7978845608028654
    inner = c * (x + 0.044715 * x * x * x)
    th = jnp.tanh(inner)
    gelu = 0.5 * x * (1.0 + th)
    dgelu = 0.5 * (1.0 + th) + 0.5 * x * (1.0 - th * th) * c * (1.0 + 3.0 * 0.044715 * x * x)
    return gelu, dgelu


def _lru_gates(x_all, tp, cw_ref, cb_ref, wa_ref, ba_ref, wx_ref, bx_ref, lam_ref):
    u = _conv_taps(x_all, cw_ref, tp, LRU_HALO, LRU_CONV) + cb_ref[...]
    u16 = u.astype(BF16)
    r = _sigmoid(_dot(u16, wa_ref[0]) + ba_ref[...])
    i = _sigmoid(_dot(u16, wx_ref[0]) + bx_ref[...])
    sp = _softplus(-lam_ref[...])
    la = -LRU_C * r * sp
    a = jnp.exp(la)
    mult = jnp.sqrt(-_expm1(2.0 * la))
    return u, r, i, a, mult, sp


def _lru_specs(tp, nb):
    col = lambda k: pl.BlockSpec((tp, HEAD), functools.partial(lambda j, k: (0, k * nb + j), k=k))
    vec = lambda: pl.BlockSpec((1, HEAD), lambda j: (0, j))
    mat = lambda: pl.BlockSpec((1, HEAD, HEAD), lambda j: (j, 0, 0))
    return col, vec, mat


def _lru_fwd(p, cw, cb, wa, ba, wx, bx, lam):
    tp = p.shape[0]
    nb = D_LRU // HEAD
    ng = tp // SUBLANE

    def body(x_ref, gt_ref, cw_ref, cb_ref, wa_ref, ba_ref, wx_ref, bx_ref, lam_ref, y_ref, hs_ref,
             xsc, asc, bsc):
        xsc[0:LRU_HALO, :] = jnp.zeros((LRU_HALO, HEAD), F32)
        xsc[LRU_HALO:, :] = x_ref[...]
        u, r, i, a, mult, _ = _lru_gates(xsc[...], tp, cw_ref, cb_ref, wa_ref, ba_ref, wx_ref, bx_ref, lam_ref)
        rows = lax.broadcasted_iota(jnp.int32, (tp, HEAD), 0)
        b = jnp.where(rows == 0, 1.0, mult) * (i * u)
        sub = rows % SUBLANE
        for k in (1, 2, 4):
            m = sub >= k
            b = jnp.where(m, a * pltpu.roll(b, k, 0) + b, b)
            a = jnp.where(m, a * pltpu.roll(a, k, 0), a)
        asc[...] = a
        bsc[...] = b

        def grp(g, carry):
            r0 = pl.multiple_of(g * SUBLANE, SUBLANE)
            h = bsc[pl.ds(r0, SUBLANE), :] + asc[pl.ds(r0, SUBLANE), :] * carry
            hs_ref[pl.ds(r0, SUBLANE), :] = h
            return jnp.broadcast_to(h[SUBLANE - 1:SUBLANE, :], (SUBLANE, HEAD))

        lax.fori_loop(0, ng, grp, jnp.zeros((SUBLANE, HEAD), F32))
        gelu, _ = _gelu_parts(gt_ref[...])
        y_ref[...] = (gelu * hs_ref[...]).astype(BF16)

    col, vec, mat = _lru_specs(tp, nb)
    return pl.pallas_call(
        body, name="lru_fwd",
        grid=(nb,),
        in_specs=[col(2), col(3), pl.BlockSpec((LRU_CONV, HEAD), lambda j: (0, j)), vec(), mat(), vec(), mat(),
                  vec(), vec()],
        out_specs=[pl.BlockSpec((tp, HEAD), lambda j: (0, j)), pl.BlockSpec((tp, HEAD), lambda j: (0, j))],
        out_shape=[jax.ShapeDtypeStruct((tp, D_LRU), BF16), jax.ShapeDtypeStruct((tp, D_LRU), F32)],
        scratch_shapes=[pltpu.VMEM((tp + LRU_HALO, HEAD), F32), pltpu.VMEM((tp, HEAD), F32),
                        pltpu.VMEM((tp, HEAD), F32)],
        compiler_params=_cparams(1),
    )(p, p, cw, cb, wa, ba, wx, bx, lam)


def _lru_bwd(p, hs, dyd, cw, cb, wa, ba, wx, bx, lam):
    tp = p.shape[0]
    nb = D_LRU // HEAD
    ng = tp // SUBLANE

    def body(x_ref, gt_ref, hs_ref, dy_ref, cw_ref, cb_ref, wa_ref, ba_ref, wx_ref, bx_ref, lam_ref,
             dx_ref, dgt_ref, dwa_ref, dwx_ref, dv_ref, xsc, asc, bsc, gsc, dusc):
        xsc[0:LRU_HALO, :] = jnp.zeros((LRU_HALO, HEAD), F32)
        xsc[LRU_HALO:, :] = x_ref[...]
        x_all = xsc[...]
        u, r, i, a, mult, sp = _lru_gates(x_all, tp, cw_ref, cb_ref, wa_ref, ba_ref, wx_ref, bx_ref, lam_ref)
        rows = lax.broadcasted_iota(jnp.int32, (tp, HEAD), 0)
        hs = hs_ref[...]
        dy = dy_ref[...]
        gelu, dgelu = _gelu_parts(gt_ref[...])
        dgt_ref[...] = (dy * hs * dgelu).astype(BF16)
        bb = dy * gelu
        aa = jnp.where(rows == tp - 1, 0.0, pltpu.roll(a, tp - 1, 0))
        sub = rows % SUBLANE
        for k in (1, 2, 4):
            m = sub < SUBLANE - k
            bb = jnp.where(m, aa * pltpu.roll(bb, tp - k, 0) + bb, bb)
            aa = jnp.where(m, aa * pltpu.roll(aa, tp - k, 0), aa)
        asc[...] = aa
        bsc[...] = bb

        def grp(gi, carry):
            g = ng - 1 - gi
            r0 = pl.multiple_of(g * SUBLANE, SUBLANE)
            gg = bsc[pl.ds(r0, SUBLANE), :] + asc[pl.ds(r0, SUBLANE), :] * carry
            gsc[pl.ds(r0, SUBLANE), :] = gg
            return jnp.broadcast_to(gg[0:1, :], (SUBLANE, HEAD))

        lax.fori_loop(0, ng, grp, jnp.zeros((SUBLANE, HEAD), F32))
        g = gsc[...]
        first = rows == 0
        hprev = jnp.where(first, 0.0, pltpu.roll(hs, 1, 0))
        iu = i * u
        d_iu = g * jnp.where(first, 1.0, mult)
        dmult_term = jnp.where(first, 0.0, g * iu * (-(a * a) / mult))
        dla = g * hprev * a + dmult_term
        dr = dla * (-LRU_C) * sp
        dv_ref[0, 7:8, :] = _colsum(dla * (LRU_C * r) * _sigmoid(-lam_ref[...]))
        dpr = dr * r * (1.0 - r)
        dpi = d_iu * u * i * (1.0 - i)
        dv_ref[0, 5:6, :] = _colsum(dpr)
        dv_ref[0, 6:7, :] = _colsum(dpi)
        u16 = u.astype(BF16)
        dpr16 = dpr.astype(BF16)
        dpi16 = dpi.astype(BF16)
        dwa_ref[0] = _dot_tn(u16, dpr16)
        dwx_ref[0] = _dot_tn(u16, dpi16)
        du = d_iu * i + _dot_nt(dpr16, wa_ref[0]) + _dot_nt(dpi16, wx_ref[0])
        dv_ref[0, 4:5, :] = _colsum(du)
        for j in range(LRU_CONV):
            sh = LRU_CONV - 1 - j
            xs = x_all if sh == 0 else pltpu.roll(x_all, sh, 0)
            dv_ref[0, j:j + 1, :] = _colsum(du * xs[LRU_HALO:, :])
        dusc[0:tp, :] = du
        dusc[tp:, :] = jnp.zeros((LRU_HALO, HEAD), F32)
        dx_ref[...] = _conv_taps_t(dusc[...], cw_ref, tp, LRU_HALO, LRU_CONV).astype(BF16)

    col, vec, mat = _lru_specs(tp, nb)
    ocol = lambda: pl.BlockSpec((tp, HEAD), lambda j: (0, j))
    return pl.pallas_call(
        body, name="lru_bwd",
        grid=(nb,),
        in_specs=[col(2), col(3), ocol(), ocol(), pl.BlockSpec((LRU_CONV, HEAD), lambda j: (0, j)), vec(), mat(),
                  vec(), mat(), vec(), vec()],
        out_specs=[ocol(), ocol(), mat(), mat(), pl.BlockSpec((1, SUBLANE, HEAD), lambda j: (j, 0, 0))],
        out_shape=[jax.ShapeDtypeStruct((tp, D_LRU), BF16), jax.ShapeDtypeStruct((tp, D_LRU), BF16),
                   jax.ShapeDtypeStruct((nb, HEAD, HEAD), F32), jax.ShapeDtypeStruct((nb, HEAD, HEAD), F32),
                   jax.ShapeDtypeStruct((nb, SUBLANE, HEAD), F32)],
        scratch_shapes=[pltpu.VMEM((tp + LRU_HALO, HEAD), F32), pltpu.VMEM((tp, HEAD), F32),
                        pltpu.VMEM((tp, HEAD), F32), pltpu.VMEM((tp, HEAD), F32),
                        pltpu.VMEM((tp + LRU_HALO, HEAD), F32)],
        compiler_params=_cparams(1),
    )(p, p, hs, dyd, cw, cb, wa, ba, wx, bx, lam)


def _mesh_pos():
    return lax.axis_index("x"), lax.axis_index("y"), lax.axis_index("c")


def _other_chips(x, y):
    return [(1 - x, y), (x, 1 - y), (1 - x, 1 - y)]


ANY = pl.BlockSpec(memory_space=pl.ANY)


HBM = pl.BlockSpec(memory_space=pltpu.HBM)
SEM = pl.BlockSpec(memory_space=pltpu.SEMAPHORE)
DATAFLOW = pltpu.SideEffectType.DATAFLOW_SIDE_EFFECTING
N_PEERS = 4


def _in_hbm(a):
    return pltpu.with_memory_space_constraint(a, pltpu.HBM)


def _gather_peers(x, y, c):
    return [((ox, oy, c), 2 * ox + oy) for ox, oy in _other_chips(x, y)] + [((x, y, 1 - c), 2 * x + y)]


def _gather_refs(src, land, slot, c, split):
    if not split:
        return src, land.at[slot]
    half = src.shape[0] // 2
    return src.at[pl.ds(c * half, half)], land.at[slot, pl.ds(c * half, half)]


def _gather_start(arrs, split):
    n = len(arrs)

    def body(*refs):
        ins, lands = refs[:n], refs[n:2 * n]
        ssem, rsem = refs[2 * n:2 * n + 2]
        token = refs[-1]
        x, y, c = _mesh_pos()
        chip = 2 * x + y
        for k in range(n):
            for j, (dev, _) in enumerate(_gather_peers(x, y, c)):
                src, dst = _gather_refs(ins[k], lands[k], chip, c, split[k] and j < N_PEERS - 1)
                pltpu.make_async_remote_copy(
                    src_ref=src, dst_ref=dst, send_sem=ssem.at[N_PEERS * k + j],
                    recv_sem=rsem.at[N_PEERS * k + j], device_id=dev, device_id_type=MESH).start()
        token[...] = jnp.zeros_like(token)

    lands = [_in_hbm(lax.empty((N_SHARD,) + a.shape, a.dtype)) for a in arrs]
    out = pl.pallas_call(
        body, name="gather_start",
        in_specs=[HBM] * (2 * n),
        out_specs=[SEM, SEM] + [HBM] * (2 * n) + [pl.BlockSpec(memory_space=pltpu.VMEM)],
        out_shape=[pltpu.SemaphoreType.DMA((N_PEERS * n,)), pltpu.SemaphoreType.DMA((N_PEERS * n,))]
        + [pltpu.HBM(a.shape, a.dtype) for a in arrs]
        + [pltpu.HBM((N_SHARD,) + a.shape, a.dtype) for a in arrs]
        + [jax.ShapeDtypeStruct((SUBLANE, LANE), F32)],
        input_output_aliases={k: 2 + k for k in range(2 * n)},
        compiler_params=pltpu.CompilerParams(has_side_effects=DATAFLOW),
    )(*[_in_hbm(a) for a in arrs], *lands)
    return out[0], out[1], list(out[2:2 + n]), list(out[2 + n:2 + 2 * n]), out[-1]


def _gather_wait(ssem, rsem, srcs, lands, ks, after, split=False):
    n = len(ks)

    def body(*refs):
        ins, lnd = refs[:n], refs[n:2 * n]
        ssem_ref, rsem_ref = refs[2 * n:2 * n + 2]
        x, y, c = _mesh_pos()
        for i, k in enumerate(ks):
            for j, (dev, pchip) in enumerate(_gather_peers(x, y, c)):
                src, dst = _gather_refs(ins[i], lnd[i], pchip, c, split and j < N_PEERS - 1)
                cp = pltpu.make_async_remote_copy(
                    src_ref=src, dst_ref=dst, send_sem=ssem_ref.at[N_PEERS * k + j],
                    recv_sem=rsem_ref.at[N_PEERS * k + j], device_id=dev, device_id_type=MESH)
                cp.wait_send()
                cp.wait_recv()

    out = pl.pallas_call(
        body, name="gather_wait",
        in_specs=[HBM] * (2 * n) + [SEM, SEM] + [ANY] * len(after),
        out_specs=[HBM] * (2 * n),
        out_shape=[pltpu.HBM(a.shape, a.dtype) for a in srcs] + [pltpu.HBM(a.shape, a.dtype) for a in lands],
        input_output_aliases={k: k for k in range(2 * n)},
        compiler_params=pltpu.CompilerParams(has_side_effects=DATAFLOW),
    )(*srcs, *lands, ssem, rsem, *after)
    return list(out[n:])


def _pair_forward(lands):
    n = len(lands)

    def body(*refs):
        outs = refs[n:2 * n]
        ssem, rsem = refs[2 * n:]
        x, y, c = _mesh_pos()
        sibling = (x, y, 1 - c)
        cps = []
        for k in range(n):
            half = lands[k].shape[1] // 2
            for j, (ox, oy) in enumerate(_other_chips(x, y)):
                mine = outs[k].at[2 * ox + oy, pl.ds(c * half, half)]
                cp = pltpu.make_async_remote_copy(src_ref=mine, dst_ref=mine, send_sem=ssem.at[3 * k + j],
                                                  recv_sem=rsem.at[3 * k + j], device_id=sibling, device_id_type=MESH)
                cp.start()
                cps.append(cp)
        for k in range(n):
            half = lands[k].shape[1] // 2
            for j, (ox, oy) in enumerate(_other_chips(x, y)):
                theirs = outs[k].at[2 * ox + oy, pl.ds((1 - c) * half, half)]
                pltpu.make_async_remote_copy(src_ref=theirs, dst_ref=theirs, send_sem=ssem.at[3 * k + j],
                                             recv_sem=rsem.at[3 * k + j], device_id=sibling,
                                             device_id_type=MESH).wait_recv()
        for cp in cps:
            cp.wait_send()

    return pl.pallas_call(
        body, name="pair_forward",
        in_specs=[ANY] * n, out_specs=[ANY] * n,
        out_shape=[jax.ShapeDtypeStruct(a.shape, a.dtype) for a in lands],
        scratch_shapes=[pltpu.SemaphoreType.DMA((3 * n,)), pltpu.SemaphoreType.DMA((3 * n,))],
        input_output_aliases={k: k for k in range(n)},
    )(*lands)


N_SOURCES = 7


def _reduce_peers(x, y, c):
    peers = []
    for ox, oy in _other_chips(x, y):
        for rel in range(2):
            peers.append(((ox, oy, c + rel - 2 * c * rel), 2 * ox + oy))
    peers.append(((x, y, 1 - c), 2 * x + y))
    return peers


def _reduce_start(arrs, slots):
    n = len(arrs)

    def body(*refs):
        ins, lands = refs[:n], refs[n:2 * n]
        ssem, rsem = refs[2 * n:2 * n + 2]
        token = refs[-1]
        x, y, c = _mesh_pos()
        me = 2 * (2 * x + y) + c
        for k in range(n):
            half = arrs[k].shape[1] // 2
            for p, (dev, ochip) in enumerate(_reduce_peers(x, y, c)):
                pltpu.make_async_remote_copy(
                    src_ref=ins[k].at[ochip, pl.ds(dev[2] * half, half)], dst_ref=lands[k].at[me],
                    send_sem=ssem.at[N_SOURCES * k + p], recv_sem=rsem.at[N_SOURCES * k + p],
                    device_id=dev, device_id_type=MESH).start()
        token[...] = jnp.zeros_like(token)

    out = pl.pallas_call(
        body, name="reduce_start",
        in_specs=[HBM] * (2 * n),
        out_specs=[SEM, SEM] + [HBM] * (2 * n) + [pl.BlockSpec(memory_space=pltpu.VMEM)],
        out_shape=[pltpu.SemaphoreType.DMA((N_SOURCES * n,)), pltpu.SemaphoreType.DMA((N_SOURCES * n,))]
        + [pltpu.HBM(a.shape, a.dtype) for a in arrs] + [pltpu.HBM(a.shape, a.dtype) for a in slots]
        + [jax.ShapeDtypeStruct((SUBLANE, LANE), F32)],
        input_output_aliases={k: 2 + k for k in range(2 * n)},
        compiler_params=pltpu.CompilerParams(has_side_effects=DATAFLOW),
    )(*[_in_hbm(a) for a in arrs], *[_in_hbm(a) for a in slots])
    return out[0], out[1], list(out[2:2 + n]), list(out[2 + n:2 + 2 * n]), out[-1]


def _reduce_wait(ssem, rsem, arrs, slots, after):
    n = len(arrs)

    def body(*refs):
        ins, lnd = refs[:n], refs[n:2 * n]
        ssem_ref, rsem_ref = refs[2 * n:2 * n + 2]
        x, y, c = _mesh_pos()
        for k in range(n):
            half = arrs[k].shape[1] // 2
            for p, (dev, ochip) in enumerate(_reduce_peers(x, y, c)):
                cp = pltpu.make_async_remote_copy(
                    src_ref=ins[k].at[ochip, pl.ds(dev[2] * half, half)], dst_ref=lnd[k].at[2 * ochip + dev[2]],
                    send_sem=ssem_ref.at[N_SOURCES * k + p], recv_sem=rsem_ref.at[N_SOURCES * k + p],
                    device_id=dev, device_id_type=MESH)
                cp.wait_send()
                cp.wait_recv()

    out = pl.pallas_call(
        body, name="reduce_wait",
        in_specs=[HBM] * (2 * n) + [SEM, SEM] + [ANY] * len(after),
        out_specs=[HBM] * (2 * n),
        out_shape=[pltpu.HBM(a.shape, a.dtype) for a in arrs] + [pltpu.HBM(a.shape, a.dtype) for a in slots],
        input_output_aliases={k: k for k in range(2 * n)},
        compiler_params=pltpu.CompilerParams(has_side_effects=DATAFLOW),
    )(*arrs, *slots, ssem, rsem, *after)
    return list(out[n:])


def _own_part(arrs, chip, core, me):
    n = len(arrs)
    nb = GRAD_ROW_BLOCKS

    def body(chip_ref, core_ref, me_ref, *refs):
        for k in range(n):
            refs[n + k][...] = refs[k][...]

    def blk(a):
        return (1, a.shape[1] // 2 // nb, a.shape[2])

    grid_spec = pltpu.PrefetchScalarGridSpec(
        num_scalar_prefetch=3, grid=(nb,),
        in_specs=[pl.BlockSpec(blk(a), lambda i, ch, co, me: (ch[0], co[0] * nb + i, 0)) for a in arrs],
        out_specs=[pl.BlockSpec(blk(a), lambda i, ch, co, me: (me[0], i, 0)) for a in arrs])
    return pl.pallas_call(
        body, name="own_part", grid_spec=grid_spec,
        out_shape=[jax.ShapeDtypeStruct((N_DEV, a.shape[1] // 2, a.shape[2]), a.dtype) for a in arrs],
        compiler_params=_cparams(1),
    )(chip, core, me, *arrs)


def _sum_devices(arrs, core):
    n = len(arrs)
    nb = GRAD_ROW_BLOCKS

    def body(c_ref, *refs):
        for k in range(n):
            r = refs[k]
            acc = r[0].astype(F32)
            for dev in range(1, N_DEV):
                acc = acc + r[dev].astype(F32)
            refs[n + k][...] = acc

    grid_spec = pltpu.PrefetchScalarGridSpec(
        num_scalar_prefetch=1, grid=(nb,),
        in_specs=[pl.BlockSpec((N_DEV, a.shape[1] // nb, a.shape[2]), lambda i, c: (0, i, 0)) for a in arrs],
        out_specs=[pl.BlockSpec((a.shape[1] // nb, a.shape[2]), lambda i, c: (c[0] * nb + i, 0)) for a in arrs])
    return pl.pallas_call(
        body, name="sum_devices", grid_spec=grid_spec,
        out_shape=[jax.ShapeDtypeStruct((2 * a.shape[1], a.shape[2]), F32) for a in arrs],
        compiler_params=_cparams(1),
    )(core, *arrs)


def _small_own(v, me):
    m = v.shape[0]

    def body(me_ref, v_ref, o_ref):
        o_ref[0] = v_ref[...]

    grid_spec = pltpu.PrefetchScalarGridSpec(
        num_scalar_prefetch=1, grid=(1,),
        in_specs=[pl.BlockSpec((m, LANE), lambda i, me: (0, 0))],
        out_specs=pl.BlockSpec((1, m, LANE), lambda i, me: (me[0], 0, 0)))
    return pl.pallas_call(
        body, name="small_own", grid_spec=grid_spec,
        out_shape=jax.ShapeDtypeStruct((N_DEV, m, LANE), v.dtype),
        compiler_params=_cparams(1),
    )(me, v)


def _small_start(v, slots):
    def body(v_ref, land, ssem, rsem, v_thru, land_thru, token):
        del v_thru, land_thru
        x, y, c = _mesh_pos()
        me = 2 * (2 * x + y) + c
        for p, (dev, _) in enumerate(_reduce_peers(x, y, c)):
            pltpu.make_async_remote_copy(src_ref=v_ref, dst_ref=land.at[me], send_sem=ssem.at[p],
                                         recv_sem=rsem.at[p], device_id=dev, device_id_type=MESH).start()
        token[...] = jnp.zeros_like(token)

    out = pl.pallas_call(
        body, name="small_start",
        in_specs=[HBM, HBM],
        out_specs=[SEM, SEM, HBM, HBM, pl.BlockSpec(memory_space=pltpu.VMEM)],
        out_shape=[pltpu.SemaphoreType.DMA((N_SOURCES,)), pltpu.SemaphoreType.DMA((N_SOURCES,)),
                   pltpu.HBM(v.shape, v.dtype), pltpu.HBM(slots.shape, slots.dtype),
                   jax.ShapeDtypeStruct((SUBLANE, LANE), F32)],
        input_output_aliases={0: 2, 1: 3},
        compiler_params=pltpu.CompilerParams(has_side_effects=DATAFLOW),
    )(_in_hbm(v), _in_hbm(slots))
    return out


def _small_wait(ssem, rsem, v, slots, after):
    def body(*refs):
        v_ref, land, ssem_ref, rsem_ref = refs[:4]
        x, y, c = _mesh_pos()
        for p, (dev, ochip) in enumerate(_reduce_peers(x, y, c)):
            cp = pltpu.make_async_remote_copy(src_ref=v_ref, dst_ref=land.at[2 * ochip + dev[2]],
                                              send_sem=ssem_ref.at[p], recv_sem=rsem_ref.at[p],
                                              device_id=dev, device_id_type=MESH)
            cp.wait_send()
            cp.wait_recv()

    out = pl.pallas_call(
        body, name="small_wait",
        in_specs=[HBM, HBM, SEM, SEM] + [ANY] * len(after),
        out_specs=[HBM, HBM],
        out_shape=[pltpu.HBM(v.shape, v.dtype), pltpu.HBM(slots.shape, slots.dtype)],
        input_output_aliases={0: 0, 1: 1},
        compiler_params=pltpu.CompilerParams(has_side_effects=DATAFLOW),
    )(v, slots, ssem, rsem, *after)
    return out[1]


GRAD_ROW_BLOCKS = 2


def _pair_allgather_halves(arrs):
    n = len(arrs)

    def body(*refs):
        outs = refs[n:2 * n]
        ssem, rsem = refs[2 * n:]
        x, y, c = _mesh_pos()
        cps = []
        for k in range(n):
            h = arrs[k].shape[0] // 2
            mine = outs[k].at[pl.ds(c * h, h)]
            cp = pltpu.make_async_remote_copy(src_ref=mine, dst_ref=mine, send_sem=ssem.at[k],
                                              recv_sem=rsem.at[k], device_id=(x, y, 1 - c), device_id_type=MESH)
            cp.start()
            cps.append(cp)
        for k, cp in enumerate(cps):
            h = arrs[k].shape[0] // 2
            theirs = outs[k].at[pl.ds((1 - c) * h, h)]
            pltpu.make_async_remote_copy(src_ref=theirs, dst_ref=theirs, send_sem=ssem.at[k], recv_sem=rsem.at[k],
                                         device_id=(x, y, 1 - c), device_id_type=MESH).wait_recv()
            cp.wait_send()

    return pl.pallas_call(
        body, name="pair_allgather_halves",
        in_specs=[ANY] * n, out_specs=[ANY] * n,
        out_shape=[jax.ShapeDtypeStruct(a.shape, a.dtype) for a in arrs],
        scratch_shapes=[pltpu.SemaphoreType.DMA((n,)), pltpu.SemaphoreType.DMA((n,))],
        input_output_aliases={k: k for k in range(n)},
    )(*arrs)


N_DEV = 8


def _adamw_math(w, g, m, v):
    m2 = ADAM_B1 * m + (1.0 - ADAM_B1) * g
    v2 = ADAM_B2 * v + (1.0 - ADAM_B2) * (g * g)
    m_hat = m2 / (1.0 - ADAM_B1 ** ADAM_STEP)
    v_hat = v2 / (1.0 - ADAM_B2 ** ADAM_STEP)
    delta = -ADAM_LR * (m_hat / (jnp.sqrt(v_hat) + ADAM_EPS) + ADAM_WD * w)
    return delta, m2, v2


def _adamw(w, m, v, gs, nblk):
    nl, r, n = w.shape
    assert nl == len(gs) and nl in (1, 2)
    br = r // nblk

    def body(w_ref, m_ref, v_ref, *rest):
        g_refs, (go_ref, d_ref, mo_ref, vo_ref) = rest[:nl], rest[nl:]
        g = g_refs[0][...]
        if nl == 2:
            g = jnp.where(pl.program_id(0) == 0, g, g_refs[1][...])
        delta, m2, v2 = _adamw_math(w_ref[0], g, m_ref[0], v_ref[0])
        go_ref[0] = g
        d_ref[0] = delta
        mo_ref[0] = m2
        vo_ref[0] = v2

    spec = pl.BlockSpec((1, br, n), lambda l, i: (l, i, 0))
    g_specs = [pl.BlockSpec((br, n), lambda l, i: (i, 0))] if nl == 1 else [
        pl.BlockSpec((br, n), lambda l, i: (jnp.where(l == 0, i, nblk - 1), 0)),
        pl.BlockSpec((br, n), lambda l, i: (jnp.where(l == 1, i, 0), 0))]
    return pl.pallas_call(
        body, name="adamw", grid=(nl, nblk),
        in_specs=[spec, spec, spec] + g_specs,
        out_specs=[spec] * 4,
        out_shape=[jax.ShapeDtypeStruct((nl, r, n), F32)] * 4,
        compiler_params=_cparams(2),
    )(w, m, v, *gs)


def _small_reduce_adamw(parts, w, m, v, rep_rows, sh_rows):
    mrows = rep_rows + N_SHARD * sh_rows + LOSS_ROWS

    def body(p_ref, w_ref, m_ref, v_ref, go_ref, d_ref, mo_ref, vo_ref, loss_ref):
        x, y, _ = _mesh_pos()
        mine = rep_rows + (2 * x + y) * sh_rows
        g_rep = p_ref[0:rep_rows, :]
        g_sh = p_ref[pl.ds(pl.multiple_of(mine, SUBLANE), sh_rows), :]
        loss = p_ref[mrows - LOSS_ROWS:mrows, :]
        for k in range(1, N_DEV):
            g_rep = g_rep + p_ref[k * mrows:k * mrows + rep_rows, :]
            g_sh = g_sh + p_ref[pl.ds(pl.multiple_of(k * mrows + mine, SUBLANE), sh_rows), :]
            loss = loss + p_ref[(k + 1) * mrows - LOSS_ROWS:(k + 1) * mrows, :]
        g = jnp.concatenate([g_rep, g_sh], axis=0)
        delta, m2, v2 = _adamw_math(w_ref[...], g, m_ref[...], v_ref[...])
        go_ref[...] = g
        d_ref[...] = delta
        mo_ref[...] = m2
        vo_ref[...] = v2
        loss_ref[...] = loss

    return pl.pallas_call(
        body, name="small_reduce_adamw",
        out_shape=[jax.ShapeDtypeStruct((rep_rows + sh_rows, 128), F32)] * 4
        + [jax.ShapeDtypeStruct((LOSS_ROWS, 128), F32)],
        compiler_params=pltpu.CompilerParams(vmem_limit_bytes=VMEM_LIMIT_MB * 1024 * 1024),
    )(parts, w, m, v)


LANE = 128
REP_SPEC = (("ffn1_norm", 16), ("mix_norm", 16), ("ffn2_norm", 16), ("final_norm", 8), ("pool_w", 256),
            ("pool_scale", 8), ("hgrn_lb_logits", 16), ("hgrn_gnorm", 8), ("lru_wa", 512), ("lru_wx", 512))
SH_SPEC = (("meta_tokens", 32), ("conv_w", 32), ("lru_conv_w", 8), ("conv_b", 8), ("conv_ln_g", 8),
           ("conv_ln_b", 8), ("lru_conv_b", 8), ("lru_ba", 8), ("lru_bx", 8), ("lru_lambda", 8))
REP_ROWS = sum(r for _, r in REP_SPEC)
SH_ROWS = sum(r for _, r in SH_SPEC)


def _pack_rows(vals, spec):
    parts = []
    for name, rows in spec:
        val = vals[name].astype(F32)
        if val.shape[-1] < LANE:
            flat = jnp.pad(val.reshape(-1, val.shape[-1]), ((0, 0), (0, LANE - val.shape[-1])))
        else:
            flat = val.reshape(-1, LANE)
        if flat.shape[0] < rows:
            flat = jnp.concatenate([flat, jnp.zeros((rows - flat.shape[0], LANE), F32)], axis=0)
        parts.append(flat)
    return jnp.concatenate(parts, axis=0)


def _unpack_rows(packed, spec, shapes):
    out = {}
    off = 0
    for name, rows in spec:
        shp = shapes[name]
        width = min(shp[-1], LANE)
        n = int(np.prod(shp)) // width
        out[name] = packed[off:off + n, :width].reshape(shp)
        off += rows
    return out


def _block_diag(blocks):
    n, b, _ = blocks.shape
    return sum(jnp.pad(blocks[g], ((g * b, (n - 1 - g) * b), (g * b, (n - 1 - g) * b))) for g in range(n))


def _diag_blocks(mat, n):
    b = mat.shape[0] // n
    return jnp.stack([mat[g * b:(g + 1) * b, g * b:(g + 1) * b] for g in range(n)])


BIG = ("ffn1_wg", "ffn1_wu", "ffn2_wg", "ffn2_wu", "ffn1_wd", "ffn2_wd", "w_in_even", "w_out_even",
       "w_in_odd", "w_out_odd")
WEIGHT_NAMES = ('meta_tokens', 'ffn1_norm', 'ffn1_wg', 'ffn1_wu', 'ffn1_wd', 'mix_norm', 'ffn2_norm', 'ffn2_wg',
                'ffn2_wu', 'ffn2_wd', 'w_in_even', 'pool_w', 'pool_scale', 'hgrn_lb_logits', 'hgrn_gnorm',
                'w_out_even', 'w_in_odd', 'conv_w', 'conv_b', 'conv_ln_g', 'conv_ln_b', 'lru_conv_w',
                'lru_conv_b', 'lru_wa', 'lru_ba', 'lru_wx', 'lru_bx', 'lru_lambda', 'w_out_odd', 'final_norm')


def _block_diag2(heads):
    nb = heads.shape[0] // 2
    return jnp.stack([_block_diag(heads[2 * j:2 * j + 2]) for j in range(nb)])


def _diag_blocks2(mats):
    return jnp.concatenate([_diag_blocks(mats[j], 2) for j in range(mats.shape[0])], axis=0)


GATHER_GROUPS = (
    (("small", 0),),
    (("ffn1_wg", 0), ("ffn1_wu", 0), ("ffn1_wd", 0)),
    (("w_in_even", 0), ("w_out_even", 0)),
    (("ffn2_wg", 0), ("ffn2_wu", 0), ("ffn2_wd", 0)),
    (("ffn1_wg", 1), ("ffn1_wu", 1), ("ffn1_wd", 1)),
    (("w_in_odd", 0), ("w_out_odd", 0)),
    (("ffn2_wg", 1), ("ffn2_wu", 1), ("ffn2_wd", 1)),
)
ADAM_ROW_BLOCKS = {"ffn1_wg": 2, "ffn1_wu": 2, "ffn2_wg": 2, "ffn2_wu": 2, "ffn1_wd": 2, "ffn2_wd": 2,
                   "w_in_even": 4, "w_out_even": 2, "w_in_odd": 4, "w_out_odd": 2}
TRANSPOSED = ("ffn1_wg", "ffn1_wu", "ffn2_wg", "ffn2_wu", "w_in_even")
SCATTER_DEPTH = 2
LOSS_ROWS = 8
GATHER_SPLIT = (1, 4)


def _unpack_small(sm, shapes):
    per_shard = [_unpack_rows(sm[s], SH_SPEC, shapes) for s in range(N_SHARD)]
    full = {}
    for n, _ in SH_SPEC:
        full[n] = jnp.concatenate([per_shard[s][n].reshape(-1, shapes[n][-1]) for s in range(N_SHARD)], axis=-1)
    full["conv_w"] = jnp.concatenate([full["conv_w"], jnp.zeros((CONV_HALO - CONV_WIDTH, D_CONV), F32)], axis=0)
    return full


def _local_step(x, tgt, w, shapes, fetch, emit, emit_small):
    s_len, d = x.shape
    t_real = s_len + N_META
    tp = -(-t_real // ROW_ALIGN) * ROW_ALIGN
    tm = _tile(tp, 832, ROW_ALIGN)
    tm_small = _tile(tp, 832, 16)
    tr = _tile(tp, 416, SUBLANE)
    tm_wgrad = _wgrad_tiles(tp, 1024)

    def gain(name, layer):
        return w[name][layer:layer + 1]

    pool_wbd = _block_diag(w["pool_w"][0]).astype(BF16)
    pool_scale = w["pool_scale"]
    wa_bd = _block_diag2(w["lru_wa"][0]).astype(BF16)
    wx_bd = _block_diag2(w["lru_wx"][0]).astype(BF16)
    mst, msk, n_lev = _hgrn_consts(HG_CHUNK)

    (sm,) = fetch(0, None)
    sf = _unpack_small(sm, shapes)
    h0 = jnp.concatenate([sf["meta_tokens"], x, jnp.zeros((tp - t_real, d), F32)], axis=0)
    tgt_pad = jnp.concatenate([jnp.zeros((N_META, d), F32), tgt, jnp.zeros((tp - t_real, d), F32)], axis=0)
    f1l0 = fetch(1, h0)
    h1, *s1 = _ffn_fwd(h0, gain("ffn1_norm", 0), *f1l0, tm)
    w_in_even4, w_out_even4 = fetch(2, h1)
    w_out_even = w_out_even4.reshape(d, d)
    even_piece = [(w_in_even4.reshape(D_IN_EVEN, d), (D_IN_EVEN, d), (0, 0))]
    p0, nm0 = _proj_fwd(h1, gain("mix_norm", 0), even_piece, tm_small, wt=True)
    ya = _pool_fwd(p0, pool_wbd, pool_scale, tr)
    yb, states = _hgrn_fwd(p0, w["hgrn_lb_logits"], w["hgrn_gnorm"], mst, msk, n_lev, tm)
    h2 = _out_fwd(h1, ya, yb, w_out_even, tm)
    f2l0 = fetch(3, h2)
    h3, *s2 = _ffn_fwd(h2, gain("ffn2_norm", 0), *f2l0, tm)
    f1l1 = fetch(4, h3)
    h4, *s3 = _ffn_fwd(h3, gain("ffn1_norm", 1), *f1l1, tm)
    w_in_odd4, w_out_odd4 = fetch(5, h4)
    w_out_odd = w_out_odd4.reshape(d, d)
    odd_pieces = [(w_in_odd4, (1, d, D_IN_ODD // N_SHARD), (k, 0, 0)) for k in range(N_SHARD)]
    p1, nm1 = _proj_fwd(h4, gain("mix_norm", 1), odd_pieces, tm_small)
    yc, conv_out = _convmod_fwd(p1, sf["conv_w"], sf["conv_b"], sf["conv_ln_g"], sf["conv_ln_b"], tr)
    lru_args = (sf["lru_conv_w"], sf["lru_conv_b"], wa_bd, sf["lru_ba"], wx_bd, sf["lru_bx"], sf["lru_lambda"])
    yd, hs = _lru_fwd(p1, *lru_args)
    h5 = _out_fwd(h4, yc, yd, w_out_odd, tm)
    f2l1 = fetch(6, h5)
    h6, *s4 = _ffn_fwd(h5, gain("ffn2_norm", 1), *f2l1, tm)
    loss, dh6, dg_final = _loss_bwd(h6, w["final_norm"].reshape(1, d), tgt_pad, t_real, tm)

    def ffn_bwd(dho, h, saved, norm, wts, after=()):
        ga, gb, sa, n = saved
        dh, da, db, dg, dy = _ffn_bwd_act(dho, h, norm, ga, gb, *wts, tm, after)
        return dh, dg, _ffn_bwd_w([(da, n, None), (db, n, None), (sa, dy, None)], tm_wgrad)

    dh5, dg_f2_l1, g = ffn_bwd(dh6, h5, s4, gain("ffn2_norm", 1), f2l1)
    sent = emit((("ffn2_wg", 1), ("ffn2_wu", 1), ("ffn2_wd", 1)), g)
    dyc, dyd, dw_out_odd = _out_bwd(dh5, yc, yd, w_out_odd, tm, tuple(sent))
    dca, dcb, dconv_w, dconv_vec = _convmod_bwd(p1, conv_out, dyc, sf["conv_w"], sf["conv_ln_g"],
                                                sf["conv_ln_b"], tr)
    dlx, dlg, dwa_bd, dwx_bd, dlru_vec = _lru_bwd(p1, hs, dyd, *lru_args)
    dp1 = [dca, dcb, dlx, dlg]
    dh4, dg_mix_l1 = _proj_bwd_act(dh5, h4, gain("mix_norm", 1), dp1, odd_pieces, tm_small)
    dw_in_odd = jnp.stack(_proj_bwd_w(nm1, dp1, tm_wgrad))
    dh3, dg_f1_l1, g = ffn_bwd(dh4, h3, s3, gain("ffn1_norm", 1), f1l1)
    sent = emit((("w_out_odd", 0), ("w_in_odd", 0), ("ffn1_wg", 1), ("ffn1_wu", 1), ("ffn1_wd", 1)),
                [dw_out_odd.reshape(N_SHARD, d // N_SHARD, d), dw_in_odd] + list(g))
    dh2, dg_f2_l0, g = ffn_bwd(dh3, h2, s2, gain("ffn2_norm", 0), f2l0, tuple(sent))
    sent = emit((("ffn2_wg", 0), ("ffn2_wu", 0), ("ffn2_wd", 0)), g)
    dya, dyb, dw_out_even = _out_bwd(dh2, ya, yb, w_out_even, tm, tuple(sent))
    dpool, dpool_wbd, dpool_scale = _pool_bwd(p0, dya, pool_wbd, pool_scale, tr)
    dq, dz, dv, dgate, dlb_logits, dgn_heads = _hgrn_bwd(p0, dyb, states, w["hgrn_lb_logits"], w["hgrn_gnorm"],
                                                         mst, msk, n_lev, tm)
    dp0 = [dpool, dq, dz, dv, dgate]
    dh1, dg_mix_l0 = _proj_bwd_act(dh2, h1, gain("mix_norm", 0), dp0, even_piece, tm_small, wt=True)
    dw_in_even_t = jnp.concatenate(_proj_bwd_w(nm0, dp0, tm_wgrad, wt=True), axis=0)
    ga, gb, sa, n1 = s1
    (dwd_f1l0,) = _ffn_bwd_w([(sa, dh1, 0.5)], tm_wgrad)
    sent = emit((("w_out_even", 0), ("w_in_even", 0), ("ffn1_wd", 0)),
                [dw_out_even.reshape(N_SHARD, d // N_SHARD, d),
                 dw_in_even_t.reshape(N_SHARD, D_IN_EVEN // N_SHARD, d), dwd_f1l0])
    dh0, da, db, dg_f1_l0, _ = _ffn_bwd_act(dh1, h0, gain("ffn1_norm", 0), ga, gb, *f1l0, tm, tuple(sent))

    grad_x = dh0[N_META:t_real]
    rep = {
        "ffn1_norm": jnp.concatenate([dg_f1_l0, dg_f1_l1], axis=0),
        "mix_norm": jnp.concatenate([dg_mix_l0, dg_mix_l1], axis=0),
        "ffn2_norm": jnp.concatenate([dg_f2_l0, dg_f2_l1], axis=0),
        "final_norm": dg_final,
        "pool_w": _diag_blocks(dpool_wbd, len(POOL_WINDOWS)),
        "pool_scale": dpool_scale,
        "hgrn_lb_logits": dlb_logits,
        "hgrn_gnorm": jnp.sum(dgn_heads, axis=0),
        "lru_wa": _diag_blocks2(dwa_bd),
        "lru_wx": _diag_blocks2(dwx_bd),
    }
    dmeta = jnp.transpose(dh0[:N_META].reshape(N_META, N_SHARD, 2, LANE), (1, 0, 2, 3)).reshape(N_SHARD, 32, LANE)
    packs = [_pack_rows(rep, REP_SPEC)]
    for s in range(N_SHARD):
        sh = {
            "meta_tokens": dmeta[s], "conv_w": dconv_w[s], "lru_conv_w": dlru_vec[s, 0:4],
            "conv_b": dconv_vec[s, 0:1], "conv_ln_g": dconv_vec[s, 1:2], "conv_ln_b": dconv_vec[s, 2:3],
            "lru_conv_b": dlru_vec[s, 4:5], "lru_ba": dlru_vec[s, 5:6], "lru_bx": dlru_vec[s, 6:7],
            "lru_lambda": dlru_vec[s, 7:8],
        }
        packs.append(_pack_rows(sh, SH_SPEC))
    packs.append(jnp.pad(loss, ((0, LOSS_ROWS - 1), (0, LANE - 1))))
    sent = emit_small(jnp.concatenate(packs, axis=0))
    emit((("ffn1_wg", 0), ("ffn1_wu", 0)), _ffn_bwd_w([(da, n1, None), (db, n1, None)], tm_wgrad, tuple(sent)))
    return grad_x


def kernel(x, meta_tokens, ffn1_norm, ffn1_wg, ffn1_wu, ffn1_wd, mix_norm, ffn2_norm, ffn2_wg, ffn2_wu, ffn2_wd, w_in_even, pool_w, pool_scale, hgrn_lb_logits, hgrn_gnorm, w_out_even, w_in_odd, conv_w, conv_b, conv_ln_g, conv_ln_b, lru_conv_w, lru_conv_b, lru_wa, lru_ba, lru_wx, lru_bx, lru_lambda, w_out_odd, final_norm, loss_target, m_meta_tokens, m_ffn1_norm, m_ffn1_wg, m_ffn1_wu, m_ffn1_wd, m_mix_norm, m_ffn2_norm, m_ffn2_wg, m_ffn2_wu, m_ffn2_wd, m_w_in_even, m_pool_w, m_pool_scale, m_hgrn_lb_logits, m_hgrn_gnorm, m_w_out_even, m_w_in_odd, m_conv_w, m_conv_b, m_conv_ln_g, m_conv_ln_b, m_lru_conv_w, m_lru_conv_b, m_lru_wa, m_lru_ba, m_lru_wx, m_lru_bx, m_lru_lambda, m_w_out_odd, m_final_norm, v_meta_tokens, v_ffn1_norm, v_ffn1_wg, v_ffn1_wu, v_ffn1_wd, v_mix_norm, v_ffn2_norm, v_ffn2_wg, v_ffn2_wu, v_ffn2_wd, v_w_in_even, v_pool_w, v_pool_scale, v_hgrn_lb_logits, v_hgrn_gnorm, v_w_out_even, v_w_in_odd, v_conv_w, v_conv_b, v_conv_ln_g, v_conv_ln_b, v_lru_conv_w, v_lru_conv_b, v_lru_wa, v_lru_ba, v_lru_wx, v_lru_bx, v_lru_lambda, v_w_out_odd, v_final_norm):
    args = locals()
    w = {n: args[n] for n in WEIGHT_NAMES}
    m = {n: args["m_" + n] for n in WEIGHT_NAMES}
    v = {n: args["v_" + n] for n in WEIGHT_NAMES}
    shapes = {n: w[n].shape for n in WEIGHT_NAMES}
    core = lax.axis_index("c").astype(jnp.int32).reshape(1)
    chip = (2 * lax.axis_index("x") + lax.axis_index("y")).astype(jnp.int32).reshape(1)
    me = 2 * chip + core

    def view(a, n):
        return jnp.swapaxes(a, 1, 2) if n in TRANSPOSED else a

    wv, mv, vv = [{n: view(src[n], n) for n in BIG} for src in (w, m, v)]

    def shard(key):
        n, l = key
        return _pack_rows(w, SH_SPEC) if n == "small" else wv[n][l].astype(BF16)

    started = {}
    for groups in (GATHER_GROUPS[:2], GATHER_GROUPS[2:]):
        gkeys = [key for grp in groups for key in grp]
        ssem, rsem, srcs, lands, token = _gather_start([shard(key) for key in gkeys],
                                                       [any(key in GATHER_GROUPS[g] for g in GATHER_SPLIT)
                                                        for key in gkeys])
        for k, key in enumerate(gkeys):
            started[key] = (ssem, rsem, srcs[k], lands[k], k, token)

    def pack_small(src):
        return jnp.concatenate([_pack_rows(src, REP_SPEC), _pack_rows(src, SH_SPEC)], axis=0)

    small_packs = [pack_small(src) for src in (w, m, v)]

    def fetch(group, after):
        st = [started[key] for key in GATHER_GROUPS[group]]
        deps = (st[0][5],) if after is None else (after,)
        if group == 1:
            deps += (started[GATHER_GROUPS[2][0]][5],) + tuple(small_packs)
        split = group in GATHER_SPLIT
        got = _gather_wait(st[0][0], st[0][1], [s[2] for s in st], [s[3] for s in st], [s[4] for s in st], deps,
                           split)
        return _pair_forward(got) if split else got

    in_flight, reduced = [], {}

    def collect(entry, after):
        gkeys, gs_sem, gr_sem, grads_thru, slots_thru, _ = entry
        slots = _reduce_wait(gs_sem, gr_sem, grads_thru, slots_thru, after)
        full = _pair_allgather_halves(_sum_devices(slots, core))
        reduced.update(zip(gkeys, full))
        return full[0]

    def emit(gkeys, grads):
        grads = list(grads)
        in_flight.append((gkeys,) + tuple(_reduce_start(grads, _own_part(grads, chip, core, me))))
        token = in_flight[-1][-1]
        if len(in_flight) > SCATTER_DEPTH:
            return token, collect(in_flight[-1 - SCATTER_DEPTH], (token,))
        return (token,)

    small_flight = []

    def emit_small(part):
        small_flight.append(_small_start(part, _small_own(part, me)))
        return (small_flight[0][4],)

    grad_x = _local_step(x[0], loss_target[0], w, shapes, fetch, emit, emit_small)

    out_g, out_d, out_m, out_v = {}, {}, {}, {}
    deps = (in_flight[-1][-1],)

    def adamw_ready():
        done = ()
        for n in BIG:
            layers = range(shapes[n][0])
            if n not in out_g and all((n, l) in reduced for l in layers):
                res = _adamw(wv[n], mv[n], vv[n], [reduced[(n, l)] for l in layers], ADAM_ROW_BLOCKS[n])
                out_g[n], out_d[n], out_m[n], out_v[n] = [view(r, n) for r in res]
                done += (res[1],)
        return done

    deps += adamw_ready()
    for entry in in_flight[-SCATTER_DEPTH:-1]:
        collect(entry, deps)
        deps += adamw_ready()
    s_ssem, s_rsem, s_part, s_slots, _ = small_flight[0]
    small_all = _small_wait(s_ssem, s_rsem, s_part, s_slots, deps)
    small_res = _small_reduce_adamw(small_all.reshape(-1, LANE), *small_packs, REP_ROWS, SH_ROWS)
    collect(in_flight[-1], deps + (small_res[0],))
    adamw_ready()

    loss = small_res[4][0, 0]
    for dst, packed in zip((out_g, out_d, out_m, out_v), small_res[:4]):
        dst.update(_unpack_rows(packed[:REP_ROWS], REP_SPEC, shapes))
        dst.update(_unpack_rows(packed[REP_ROWS:], SH_SPEC, shapes))

    return (loss, grad_x[None], *[out_g[n] for n in WEIGHT_NAMES], *[out_d[n] for n in WEIGHT_NAMES],
            *[out_m[n] for n in WEIGHT_NAMES], *[out_v[n] for n in WEIGHT_NAMES])
```

```python
import functools

import numpy as np
import jax
import jax.numpy as jnp
from jax import lax
from jax.experimental import pallas as pl
from jax.experimental.pallas import tpu as pltpu

F32 = jnp.float32
BF16 = jnp.bfloat16
MESH = pl.DeviceIdType.MESH

EPS = 1e-6
N_META = 16
D_FF = 2816
N_SHARD = 4
FF_SHARD = D_FF // N_SHARD
D_POOL = 256
POOL_GROUP = 64
POOL_WINDOWS = (2, 4, 8, 16)
D_HGRN = 768
HG_HEADS = 6
HEAD = 128
HG_CHUNK = 64
HG_HEADS_PER_STEP = 6
FFN_ROW_BLOCKS = 2
HG_PBLOCK = 256
D_IN_EVEN = D_POOL + 4 * D_HGRN
D_CONV = 512
CONV_WIDTH = 31
CONV_HALO = 32
D_LRU = 512
LRU_CONV = 4
LRU_HALO = 8
LRU_C = 8.0
D_IN_ODD = 2 * D_CONV + 2 * D_LRU
SUBLANE = 8
MXU_DEPTH = 256
ROW_ALIGN = 64

ADAM_LR = 0.001
ADAM_B1 = 0.9
ADAM_B2 = 0.999
ADAM_EPS = 1e-08
ADAM_WD = 0.01
ADAM_STEP = 10

VMEM_LIMIT_MB = 56


def _cparams(n_grid_axes=0, vmem_mb=VMEM_LIMIT_MB):
    sem = ("arbitrary",) * n_grid_axes if n_grid_axes else None
    return pltpu.CompilerParams(dimension_semantics=sem, vmem_limit_bytes=vmem_mb * 1024 * 1024)


def _tile(n, target, mult):
    best = None
    for t in range(mult, min(n, target) + 1, mult):
        if n % t == 0:
            best = t
    assert best is not None, (n, target, mult)
    return best


def _dot(a, b):
    return jnp.dot(a, b, preferred_element_type=F32)


def _dot_nt(a, b):
    return lax.dot_general(a, b, (((1,), (1,)), ((), ())), preferred_element_type=F32)


def _dot_tn(a, b):
    return lax.dot_general(a, b, (((0,), (0,)), ((), ())), preferred_element_type=F32)


def _sigmoid(x):
    return 1.0 / (1.0 + jnp.exp(-x))


def _colsum(x):
    return jnp.sum(x, axis=0, keepdims=True)


def _rms_stats(h):
    rstd = lax.rsqrt(jnp.mean(h * h, axis=-1, keepdims=True) + EPS)
    return rstd, h * rstd


def _rms_bwd(dn, g, rstd, xhat):
    dng = dn * g
    dh = rstd * (dng - xhat * jnp.mean(dng * xhat, axis=-1, keepdims=True))
    return dh, _colsum(dn * xhat)


def _ffn_fwd(h, norm, wg4, wu4, wd4, tm):
    tp, d = h.shape
    nt = tp // tm

    def body(h_ref, g_ref, wg_ref, wu_ref, wd_ref, ho_ref, ga_ref, gb_ref, sa_ref, n_ref, n_sc, acc):
        s = pl.program_id(1)

        @pl.when(s == 0)
        def _():
            hh = h_ref[...]
            rstd, xhat = _rms_stats(hh)
            n = (xhat * g_ref[...]).astype(BF16)
            n_sc[...] = n
            n_ref[...] = n
            acc[...] = jnp.zeros_like(acc)

        n = n_sc[...]
        a = _dot_nt(n, wg_ref[0])
        b = _dot_nt(n, wu_ref[0])
        sig = _sigmoid(a)
        sil = a * sig
        ga_ref[0] = (sig * (1.0 + a * (1.0 - sig)) * b).astype(BF16)
        gb_ref[0] = sil.astype(BF16)
        sg = (sil * b).astype(BF16)
        sa_ref[0] = sg
        acc[...] += _dot(sg, wd_ref[0])

        @pl.when(s == N_SHARD - 1)
        def _():
            ho_ref[...] = h_ref[...] + 0.5 * acc[...]

    return pl.pallas_call(
        body, name="ffn_fwd",
        grid=(nt, N_SHARD),
        in_specs=[
            pl.BlockSpec((tm, d), lambda i, s: (i, 0)),
            pl.BlockSpec((1, d), lambda i, s: (0, 0)),
            pl.BlockSpec((1, FF_SHARD, d), lambda i, s: (s, 0, 0)),
            pl.BlockSpec((1, FF_SHARD, d), lambda i, s: (s, 0, 0)),
            pl.BlockSpec((1, FF_SHARD, d), lambda i, s: (s, 0, 0)),
        ],
        out_specs=[
            pl.BlockSpec((tm, d), lambda i, s: (i, 0)),
            pl.BlockSpec((1, tm, FF_SHARD), lambda i, s: (s, i, 0)),
            pl.BlockSpec((1, tm, FF_SHARD), lambda i, s: (s, i, 0)),
            pl.BlockSpec((1, tm, FF_SHARD), lambda i, s: (s, i, 0)),
            pl.BlockSpec((tm, d), lambda i, s: (i, 0)),
        ],
        out_shape=[
            jax.ShapeDtypeStruct((tp, d), F32),
            jax.ShapeDtypeStruct((N_SHARD, tp, FF_SHARD), BF16),
            jax.ShapeDtypeStruct((N_SHARD, tp, FF_SHARD), BF16),
            jax.ShapeDtypeStruct((N_SHARD, tp, FF_SHARD), BF16),
            jax.ShapeDtypeStruct((tp, d), BF16),
        ],
        scratch_shapes=[pltpu.VMEM((tm, d), BF16), pltpu.VMEM((tm, d), F32)],
        compiler_params=_cparams(2),
    )(h, norm, wg4, wu4, wd4)


def _ffn_bwd_act(dho, h, norm, ga4, gb4, wg4, wu4, wd4, tm, after=()):
    tp, d = h.shape
    nt = tp // tm

    def body(dho_ref, h_ref, g_ref, ga_ref, gb_ref, wg_ref, wu_ref, wd_ref, *rest):
        dh_ref, da_ref, db_ref, dg_ref, dy_ref, dn_sc = rest[len(after):]
        i = pl.program_id(0)
        s = pl.program_id(1)

        @pl.when(s == 0)
        def _():
            dy_ref[...] = (0.5 * dho_ref[...]).astype(BF16)
            dn_sc[...] = jnp.zeros_like(dn_sc)

        @pl.when((s == 0) & (i == 0))
        def _():
            dg_ref[...] = jnp.zeros_like(dg_ref)

        hb = tm // FFN_ROW_BLOCKS
        rows = [pl.ds(k * hb, hb) for k in range(FFN_ROW_BLOCKS)]

        def shard_step(last):
            ds = _dot_nt(dy_ref[rows[0], :], wd_ref[0])
            dg = None
            for k in range(FFN_ROW_BLOCKS):
                ds_next = _dot_nt(dy_ref[rows[k + 1], :], wd_ref[0]) if k + 1 < FFN_ROW_BLOCKS else None
                da = (ds * ga_ref[0, rows[k], :].astype(F32)).astype(BF16)
                db = (ds * gb_ref[0, rows[k], :].astype(F32)).astype(BF16)
                da_ref[0, rows[k], :] = da
                db_ref[0, rows[k], :] = db
                dn = dn_sc[rows[k], :] + (_dot(da, wg_ref[0]) + _dot(db, wu_ref[0]))
                if last:
                    rstd, xhat = _rms_stats(h_ref[rows[k], :])
                    dh, dgk = _rms_bwd(dn, g_ref[...], rstd, xhat)
                    dh_ref[rows[k], :] = dho_ref[rows[k], :] + dh
                    dg = dgk if dg is None else dg + dgk
                else:
                    dn_sc[rows[k], :] = dn
                ds = ds_next
            if last:
                dg_ref[...] += dg

        pl.when(s < N_SHARD - 1)(functools.partial(shard_step, False))
        pl.when(s == N_SHARD - 1)(functools.partial(shard_step, True))

    return pl.pallas_call(
        body, name="ffn_bwd_act",
        grid=(nt, N_SHARD),
        in_specs=[
            pl.BlockSpec((tm, d), lambda i, s: (i, 0)),
            pl.BlockSpec((tm, d), lambda i, s: (i, 0)),
            pl.BlockSpec((1, d), lambda i, s: (0, 0)),
            pl.BlockSpec((1, tm, FF_SHARD), lambda i, s: (s, i, 0)),
            pl.BlockSpec((1, tm, FF_SHARD), lambda i, s: (s, i, 0)),
            pl.BlockSpec((1, FF_SHARD, d), lambda i, s: (s, 0, 0)),
            pl.BlockSpec((1, FF_SHARD, d), lambda i, s: (s, 0, 0)),
            pl.BlockSpec((1, FF_SHARD, d), lambda i, s: (s, 0, 0)),
        ] + [pl.BlockSpec(memory_space=pl.ANY)] * len(after),
        out_specs=[
            pl.BlockSpec((tm, d), lambda i, s: (i, 0)),
            pl.BlockSpec((1, tm, FF_SHARD), lambda i, s: (s, i, 0)),
            pl.BlockSpec((1, tm, FF_SHARD), lambda i, s: (s, i, 0)),
            pl.BlockSpec((1, d), lambda i, s: (0, 0)),
            pl.BlockSpec((tm, d), lambda i, s: (i, 0)),
        ],
        out_shape=[
            jax.ShapeDtypeStruct((tp, d), F32),
            jax.ShapeDtypeStruct((N_SHARD, tp, FF_SHARD), BF16),
            jax.ShapeDtypeStruct((N_SHARD, tp, FF_SHARD), BF16),
            jax.ShapeDtypeStruct((1, d), F32),
            jax.ShapeDtypeStruct((tp, d), BF16),
        ],
        scratch_shapes=[pltpu.VMEM((tm, d), F32)],
        compiler_params=_cparams(2),
    )(dho, h, norm, ga4, gb4, wg4, wu4, wd4, *after)


def _wgrad_tiles(tp, target):
    tm = min(target, tp) // MXU_DEPTH * MXU_DEPTH
    nt = tp // tm
    tail = tp - nt * tm
    assert tail == 0 or (nt * tm) % tail == 0 and tail % 16 == 0, (tp, tm, tail)
    return tm, nt, tail


def _ffn_bwd_w(pairs, tiles, after=()):
    npair = len(pairs)
    tp, d = pairs[0][1].shape
    tm, nt, tail = tiles
    rhs_list = []
    for _, rhs, _ in pairs:
        if all(rhs is not r for r in rhs_list):
            rhs_list.append(rhs)
    rhs_of = [[rhs is r for r in rhs_list].index(True) for _, rhs, _ in pairs]
    nrhs = len(rhs_list)
    nin = nrhs + npair
    ntail = nin if tail else 0

    def body(*refs):
        rhs_refs = refs[:nrhs]
        lhs_refs = refs[nrhs:nin]
        rhs_tails = refs[nin:nin + ntail][:nrhs]
        lhs_tails = refs[nin:nin + ntail][nrhs:]
        rest = refs[nin + ntail + len(after):]
        out_refs, accs = rest[:npair], rest[npair:]
        i = pl.program_id(1)

        @pl.when(i == 0)
        def _():
            for acc in accs:
                acc[...] = jnp.zeros_like(acc)

        def accumulate(lhs, rhs):
            for k, (_, _, scale) in enumerate(pairs):
                r = rhs[rhs_of[k]][...]
                if scale is not None:
                    r = (scale * r).astype(BF16)
                accs[k][...] += _dot_tn(lhs[k][0], r)

        accumulate(lhs_refs, rhs_refs)

        @pl.when(i == nt - 1)
        def _():
            if tail:
                accumulate(lhs_tails, rhs_tails)
            for k in range(npair):
                out_refs[k][0] = accs[k][...].astype(BF16)

    tail_specs, tail_args = [], []
    if tail:
        tb = nt * tm // tail
        tail_specs = ([pl.BlockSpec((tail, d), lambda s, i: (tb, 0))] * nrhs
                      + [pl.BlockSpec((1, tail, FF_SHARD), lambda s, i: (s, tb, 0))] * npair)
        tail_args = [*rhs_list, *[lhs for lhs, _, _ in pairs]]
    return pl.pallas_call(
        body, name="ffn_bwd_w",
        grid=(N_SHARD, nt),
        in_specs=[pl.BlockSpec((tm, d), lambda s, i: (i, 0))] * nrhs
        + [pl.BlockSpec((1, tm, FF_SHARD), lambda s, i: (s, i, 0))] * npair
        + tail_specs
        + [pl.BlockSpec(memory_space=pl.ANY)] * len(after),
        out_specs=[pl.BlockSpec((1, FF_SHARD, d), lambda s, i: (s, 0, 0))] * npair,
        out_shape=[jax.ShapeDtypeStruct((N_SHARD, FF_SHARD, d), BF16)] * npair,
        scratch_shapes=[pltpu.VMEM((FF_SHARD, d), F32)] * npair,
        compiler_params=_cparams(2),
    )(*rhs_list, *[lhs for lhs, _, _ in pairs], *tail_args, *after)


def _proj_fwd(h, norm, w_pieces, tm, wt=False):
    tp, d = h.shape
    widths = [bs[-2] if wt else bs[-1] for _, bs, _ in w_pieces]
    ntot = sum(widths)
    npc = len(w_pieces)

    def body(*refs):
        h_ref, g_ref = refs[:2]
        w_refs = refs[2:2 + npc]
        p_ref, n_ref = refs[2 + npc:]
        rstd, xhat = _rms_stats(h_ref[...])
        n = (xhat * g_ref[...]).astype(BF16)
        n_ref[...] = n
        off = 0
        for k in range(npc):
            w = w_refs[k][...]
            w = w.reshape(w.shape[-2], w.shape[-1])
            p_ref[:, off:off + widths[k]] = _dot_nt(n, w) if wt else _dot(n, w)
            off += widths[k]

    in_specs = [pl.BlockSpec((tm, d), lambda i: (i, 0)), pl.BlockSpec((1, d), lambda i: (0, 0))]
    for _, bs, idx in w_pieces:
        in_specs.append(pl.BlockSpec(bs, functools.partial(lambda i, idx: idx, idx=idx)))
    return pl.pallas_call(
        body, name="proj_fwd",
        grid=(tp // tm,),
        in_specs=in_specs,
        out_specs=[pl.BlockSpec((tm, ntot), lambda i: (i, 0)), pl.BlockSpec((tm, d), lambda i: (i, 0))],
        out_shape=[jax.ShapeDtypeStruct((tp, ntot), F32), jax.ShapeDtypeStruct((tp, d), BF16)],
        compiler_params=_cparams(1),
    )(h, norm, *[w for w, _, _ in w_pieces])


def _proj_bwd_act(dres, h, norm, dp_pieces, w_pieces, tm, wt=False):
    tp, d = h.shape
    npc = len(dp_pieces)
    nw = len(w_pieces)
    assert nw == npc or (nw == 1 and wt)

    def body(*refs):
        dres_ref, h_ref, g_ref = refs[:3]
        dp_refs = refs[3:3 + npc]
        w_refs = refs[3 + npc:3 + npc + nw]
        dh_ref, dg_ref = refs[3 + npc + nw:]
        i = pl.program_id(0)

        @pl.when(i == 0)
        def _():
            dg_ref[...] = jnp.zeros_like(dg_ref)

        def matmuls(rows):
            dn = None
            off = 0
            for k in range(npc):
                if nw == npc:
                    w = w_refs[k][...]
                    w = w.reshape(w.shape[-2], w.shape[-1])
                else:
                    w = w_refs[0][off:off + dp_pieces[k].shape[1], :]
                    off += dp_pieces[k].shape[1]
                t = _dot(dp_refs[k][rows, :], w) if wt else _dot_nt(dp_refs[k][rows, :], w)
                dn = t if dn is None else dn + t
            return dn

        hb = tm // FFN_ROW_BLOCKS
        blocks = [pl.ds(k * hb, hb) for k in range(FFN_ROW_BLOCKS)]
        dn = matmuls(blocks[0])
        dg = None
        for k, rows in enumerate(blocks):
            dn_next = matmuls(blocks[k + 1]) if k + 1 < FFN_ROW_BLOCKS else None
            rstd, xhat = _rms_stats(h_ref[rows, :])
            dh, dgk = _rms_bwd(dn, g_ref[...], rstd, xhat)
            dh_ref[rows, :] = dres_ref[rows, :] + dh
            dg = dgk if dg is None else dg + dgk
            dn = dn_next
        dg_ref[...] += dg

    in_specs = [pl.BlockSpec((tm, d), lambda i: (i, 0)), pl.BlockSpec((tm, d), lambda i: (i, 0)),
                pl.BlockSpec((1, d), lambda i: (0, 0))]
    for dp in dp_pieces:
        in_specs.append(pl.BlockSpec((tm, dp.shape[1]), lambda i: (i, 0)))
    for _, bs, idx in w_pieces:
        in_specs.append(pl.BlockSpec(bs, functools.partial(lambda i, idx: idx, idx=idx)))
    return pl.pallas_call(
        body, name="proj_bwd_act",
        grid=(tp // tm,),
        in_specs=in_specs,
        out_specs=[pl.BlockSpec((tm, d), lambda i: (i, 0)), pl.BlockSpec((1, d), lambda i: (0, 0))],
        out_shape=[jax.ShapeDtypeStruct((tp, d), F32), jax.ShapeDtypeStruct((1, d), F32)],
        compiler_params=_cparams(1),
    )(dres, h, norm, *dp_pieces, *[w for w, _, _ in w_pieces])


def _proj_bwd_w(n, dp_pieces, tiles, wt=False):
    tp, d = n.shape
    tm, nt, tail = tiles
    npc = len(dp_pieces)
    widths = [dp.shape[1] for dp in dp_pieces]
    oshape = (lambda w: (w, d)) if wt else (lambda w: (d, w))
    nin = 1 + npc
    ntail = nin if tail else 0

    def body(*refs):
        o_refs = refs[nin + ntail:nin + ntail + npc]
        accs = refs[nin + ntail + npc:]
        i = pl.program_id(0)

        @pl.when(i == 0)
        def _():
            for acc in accs:
                acc[...] = jnp.zeros_like(acc)

        def accumulate(n_ref, dp_refs):
            nn = n_ref[...]
            for k in range(npc):
                accs[k][...] += _dot_tn(dp_refs[k][...], nn) if wt else _dot_tn(nn, dp_refs[k][...])

        accumulate(refs[0], refs[1:nin])

        @pl.when(i == nt - 1)
        def _():
            if tail:
                accumulate(refs[nin], refs[nin + 1:nin + ntail])
            for k in range(npc):
                o_refs[k][...] = accs[k][...].astype(BF16)

    tail_specs = []
    if tail:
        tb = nt * tm // tail
        tail_specs = [pl.BlockSpec((tail, d), lambda i: (tb, 0))] + [pl.BlockSpec((tail, w), lambda i: (tb, 0))
                                                                    for w in widths]
    return pl.pallas_call(
        body, name="proj_bwd_w",
        grid=(nt,),
        in_specs=[pl.BlockSpec((tm, d), lambda i: (i, 0))]
        + [pl.BlockSpec((tm, w), lambda i: (i, 0)) for w in widths] + tail_specs,
        out_specs=[pl.BlockSpec(oshape(w), lambda i: (0, 0)) for w in widths],
        out_shape=[jax.ShapeDtypeStruct(oshape(w), BF16) for w in widths],
        scratch_shapes=[pltpu.VMEM(oshape(w), F32) for w in widths],
        compiler_params=_cparams(1),
    )(n, *dp_pieces, *((n, *dp_pieces) if tail else ()))


def _out_fwd(h, ya, yb, w, tm):
    tp, d = h.shape
    na, nb = ya.shape[1], yb.shape[1]

    def body(h_ref, ya_ref, yb_ref, w_ref, o_ref):
        y = _dot(ya_ref[...].astype(BF16), w_ref[0:na, :]) + _dot(yb_ref[...].astype(BF16), w_ref[na:, :])
        o_ref[...] = h_ref[...] + y

    return pl.pallas_call(
        body, name="out_fwd",
        grid=(tp // tm,),
        in_specs=[pl.BlockSpec((tm, d), lambda i: (i, 0)), pl.BlockSpec((tm, na), lambda i: (i, 0)),
                  pl.BlockSpec((tm, nb), lambda i: (i, 0)), pl.BlockSpec((d, d), lambda i: (0, 0))],
        out_specs=pl.BlockSpec((tm, d), lambda i: (i, 0)),
        out_shape=jax.ShapeDtypeStruct((tp, d), F32),
        compiler_params=_cparams(1),
    )(h, ya, yb, w)


def _out_bwd(dy, ya, yb, w, tm, after=()):
    tp, d = dy.shape
    na, nb = ya.shape[1], yb.shape[1]

    def body(dy_ref, ya_ref, yb_ref, w_ref, *rest):
        da_ref, db_ref, dw_ref, acc = rest[len(after):]
        i = pl.program_id(0)

        @pl.when(i == 0)
        def _():
            acc[...] = jnp.zeros_like(acc)

        dyb16 = dy_ref[...].astype(BF16)
        da_ref[...] = _dot_nt(dyb16, w_ref[0:na, :])
        db_ref[...] = _dot_nt(dyb16, w_ref[na:, :])
        acc[0:na, :] += _dot_tn(ya_ref[...].astype(BF16), dyb16)
        acc[na:, :] += _dot_tn(yb_ref[...].astype(BF16), dyb16)

        @pl.when(i == pl.num_programs(0) - 1)
        def _():
            dw_ref[...] = acc[...].astype(BF16)

    return pl.pallas_call(
        body, name="out_bwd",
        grid=(tp // tm,),
        in_specs=[pl.BlockSpec((tm, d), lambda i: (i, 0)), pl.BlockSpec((tm, na), lambda i: (i, 0)),
                  pl.BlockSpec((tm, nb), lambda i: (i, 0)), pl.BlockSpec((d, d), lambda i: (0, 0))]
        + [pl.BlockSpec(memory_space=pl.ANY)] * len(after),
        out_specs=[pl.BlockSpec((tm, na), lambda i: (i, 0)), pl.BlockSpec((tm, nb), lambda i: (i, 0)),
                   pl.BlockSpec((d, d), lambda i: (0, 0))],
        out_shape=[jax.ShapeDtypeStruct((tp, na), F32), jax.ShapeDtypeStruct((tp, nb), F32),
                   jax.ShapeDtypeStruct((d, d), BF16)],
        scratch_shapes=[pltpu.VMEM((d, d), F32)],
        compiler_params=_cparams(1),
    )(dy, ya, yb, w, *after)


def _loss_bwd(h, gfin, tgt, t_real, tm):
    tp, d = h.shape

    def body(h_ref, g_ref, t_ref, loss_ref, dh_ref, dg_ref):
        i = pl.program_id(0)

        @pl.when(i == 0)
        def _():
            loss_ref[...] = jnp.zeros_like(loss_ref)
            dg_ref[...] = jnp.zeros_like(dg_ref)

        rows = i * tm + lax.broadcasted_iota(jnp.int32, (tm, 1), 0)
        valid = (rows >= N_META) & (rows < t_real)
        rstd, xhat = _rms_stats(h_ref[...])
        g = g_ref[...]
        err = jnp.where(valid, xhat * g - t_ref[...], 0.0)
        e2 = jnp.sum(err * err, axis=1, keepdims=True)
        loss_ref[...] += (0.5 / d) * jnp.sum(e2, axis=0, keepdims=True)
        dy = err * (1.0 / d)
        dh, dg = _rms_bwd(dy, g, rstd, xhat)
        dh_ref[...] = dh
        dg_ref[...] += dg

    return pl.pallas_call(
        body, name="loss_bwd",
        grid=(tp // tm,),
        in_specs=[pl.BlockSpec((tm, d), lambda i: (i, 0)), pl.BlockSpec((1, d), lambda i: (0, 0)),
                  pl.BlockSpec((tm, d), lambda i: (i, 0))],
        out_specs=[pl.BlockSpec((1, 1), lambda i: (0, 0)), pl.BlockSpec((tm, d), lambda i: (i, 0)),
                   pl.BlockSpec((1, d), lambda i: (0, 0))],
        out_shape=[jax.ShapeDtypeStruct((1, 1), F32), jax.ShapeDtypeStruct((tp, d), F32),
                   jax.ShapeDtypeStruct((1, d), F32)],
        compiler_params=_cparams(1),
    )(h, gfin, tgt)


POOL_HALO = 16


def _pool_lane_consts(n_rows):
    lane = lax.broadcasted_iota(jnp.int32, (n_rows, D_POOL), 1)
    grp = lane // POOL_GROUP
    win = jnp.where(grp == 0, 2.0, jnp.where(grp == 1, 4.0, jnp.where(grp == 2, 8.0, 16.0)))
    return grp, win


def _pool_select(grp, s2, s4, s8, s16):
    return jnp.where(grp == 0, s2, jnp.where(grp == 1, s4, jnp.where(grp == 2, s8, s16)))


def _pool_mixed(x, row0, tr):
    n = tr + POOL_HALO
    s2 = x + pltpu.roll(x, 1, 0)
    s4 = s2 + pltpu.roll(s2, 2, 0)
    s8 = s4 + pltpu.roll(s4, 4, 0)
    s16 = s8 + pltpu.roll(s8, 8, 0)
    grp, win = _pool_lane_consts(n)
    rows = row0 - POOL_HALO + lax.broadcasted_iota(jnp.int32, (n, D_POOL), 0)
    cnt = jnp.minimum((rows + 1).astype(F32), win)
    pooled = _pool_select(grp, s2, s4, s8, s16) / jnp.maximum(cnt, 1.0)
    return (pooled - x)[POOL_HALO:, :]


def _pool_fwd(p, wbd, scale, tr):
    tp = p.shape[0]
    nt = tp // tr

    def body(p_ref, w_ref, s_ref, y_ref, usc):
        usc[0:POOL_HALO, :] = jnp.zeros((POOL_HALO, D_POOL), F32)
        usc[POOL_HALO:, :] = p_ref[...]

        def tile(r, carry):
            r0 = pl.multiple_of(r * tr, SUBLANE)
            x = usc[pl.ds(r0, tr + POOL_HALO), :]
            mixed = _pool_mixed(x, r0, tr)
            y_ref[pl.ds(r0, tr), :] = (_dot(mixed.astype(BF16), w_ref[...]) * s_ref[...]).astype(BF16)
            return carry

        lax.fori_loop(0, nt, tile, 0)

    return pl.pallas_call(
        body, name="pool_fwd",
        grid=(1,),
        in_specs=[pl.BlockSpec((tp, D_POOL), lambda i: (0, 0)), pl.BlockSpec((D_POOL, D_POOL), lambda i: (0, 0)),
                  pl.BlockSpec((1, D_POOL), lambda i: (0, 0))],
        out_specs=pl.BlockSpec((tp, D_POOL), lambda i: (0, 0)),
        out_shape=jax.ShapeDtypeStruct((tp, D_POOL), BF16),
        scratch_shapes=[pltpu.VMEM((tp + POOL_HALO, D_POOL), F32)],
        compiler_params=_cparams(1),
    )(p, wbd, scale)


def _pool_bwd(p, dya, wbd, scale, tr):
    tp = p.shape[0]
    nt = tp // tr

    def body(p_ref, dy_ref, w_ref, s_ref, du_ref, dw_ref, ds_ref, usc, gsc):
        usc[0:POOL_HALO, :] = jnp.zeros((POOL_HALO, D_POOL), F32)
        usc[POOL_HALO:, :] = p_ref[...]
        gsc[tp:, :] = jnp.zeros((POOL_HALO, D_POOL), F32)
        dw_ref[...] = jnp.zeros_like(dw_ref)
        ds_ref[...] = jnp.zeros_like(ds_ref)
        grp, win = _pool_lane_consts(tr)

        def tile1(r, carry):
            r0 = pl.multiple_of(r * tr, SUBLANE)
            x = usc[pl.ds(r0, tr + POOL_HALO), :]
            mixed = _pool_mixed(x, r0, tr).astype(BF16)
            dy = dy_ref[pl.ds(r0, tr), :]
            dys = (dy * s_ref[...]).astype(BF16)
            ypre = _dot(mixed, w_ref[...])
            ds_ref[...] += _colsum(dy * ypre)
            dw_ref[...] += _dot_tn(mixed, dys)
            dmx = _dot_nt(dys, w_ref[...])
            rows = r0 + lax.broadcasted_iota(jnp.int32, (tr, D_POOL), 0)
            cnt = jnp.minimum((rows + 1).astype(F32), win)
            gsc[pl.ds(r0, tr), :] = dmx / cnt
            return carry

        lax.fori_loop(0, nt, tile1, 0)
        n = tr + POOL_HALO
        grp2, win2 = _pool_lane_consts(n)

        def tile2(r, carry):
            r0 = pl.multiple_of(r * tr, SUBLANE)
            g = gsc[pl.ds(r0, n), :]
            s2 = g + pltpu.roll(g, n - 1, 0)
            s4 = s2 + pltpu.roll(s2, n - 2, 0)
            s8 = s4 + pltpu.roll(s4, n - 4, 0)
            s16 = s8 + pltpu.roll(s8, n - 8, 0)
            pooled_t = _pool_select(grp2, s2, s4, s8, s16)
            rows = r0 + lax.broadcasted_iota(jnp.int32, (n, D_POOL), 0)
            cnt = jnp.minimum((rows + 1).astype(F32), win2)
            du = pooled_t - g * cnt
            du_ref[pl.ds(r0, tr), :] = du[0:tr, :].astype(BF16)
            return carry

        lax.fori_loop(0, nt, tile2, 0)

    return pl.pallas_call(
        body, name="pool_bwd",
        grid=(1,),
        in_specs=[pl.BlockSpec((tp, D_POOL), lambda i: (0, 0)), pl.BlockSpec((tp, D_POOL), lambda i: (0, 0)),
                  pl.BlockSpec((D_POOL, D_POOL), lambda i: (0, 0)), pl.BlockSpec((1, D_POOL), lambda i: (0, 0))],
        out_specs=[pl.BlockSpec((tp, D_POOL), lambda i: (0, 0)), pl.BlockSpec((D_POOL, D_POOL), lambda i: (0, 0)),
                   pl.BlockSpec((1, D_POOL), lambda i: (0, 0))],
        out_shape=[jax.ShapeDtypeStruct((tp, D_POOL), BF16), jax.ShapeDtypeStruct((D_POOL, D_POOL), F32),
                   jax.ShapeDtypeStruct((1, D_POOL), F32)],
        scratch_shapes=[pltpu.VMEM((tp + POOL_HALO, D_POOL), F32), pltpu.VMEM((tp + POOL_HALO, D_POOL), F32)],
        compiler_params=_cparams(1),
    )(p, dya, wbd, scale)


def _hgrn_levels(ch):
    levels = []
    w = ch // 2
    while w >= 1:
        levels.append(w)
        w //= 2
    return levels


def _hgrn_consts(ch):
    t = np.arange(ch)
    tril = t[None, :] <= t[:, None]
    masks = []
    for w in _hgrn_levels(ch):
        blk = t // (2 * w)
        upper = t % (2 * w) >= w
        masks.append(upper[:, None] & (~upper)[None, :] & (blk[:, None] == blk[None, :]))
    masks.append(tril)
    msk = np.stack(masks).astype(np.float32)
    return jnp.asarray(tril.astype(np.float32), BF16), jnp.asarray(msk, F32), len(masks) - 1


def _split3(x):
    hi = x.astype(BF16)
    r1 = x - hi.astype(F32)
    mid = r1.astype(BF16)
    lo = (r1 - mid.astype(F32)).astype(BF16)
    return hi, mid, lo


def _hgrn_exponents(tril, logf):
    ch = logf.shape[0]
    hi, mid, lo = _split3(logf)
    x = _dot(tril, jnp.concatenate([hi, mid, lo], axis=1))
    b = x[:, 0:HEAD] + x[:, HEAD:2 * HEAD] + x[:, 2 * HEAD:3 * HEAD]
    rows = lax.broadcasted_iota(jnp.int32, (ch, HEAD), 0)
    fx = jnp.broadcast_to(b[ch - 1:ch, :], (ch, HEAD)) - b
    lev = []
    for w in _hgrn_levels(ch):
        pos = rows % (2 * w)
        upper = pos >= w
        if w >= SUBLANE:
            parts = [jnp.broadcast_to(b[k * 2 * w + w - 1:k * 2 * w + w, :], (2 * w, HEAD))
                     for k in range(ch // (2 * w))]
            bmid = parts[0] if len(parts) == 1 else jnp.concatenate(parts, axis=0)
            dx = jnp.where(upper, b - bmid, 0.0)
            ex = jnp.where(upper, 0.0, bmid - b)
        else:
            dx = logf
            ex = jnp.zeros_like(logf)
            for i in range(1, w):
                dx = dx + jnp.where(pos >= w + i, pltpu.roll(logf, i, 0), 0.0)
                ex = ex + jnp.where(pos <= w - 1 - i, pltpu.roll(logf, ch - i, 0), 0.0)
            dx = jnp.where(upper, dx, 0.0)
        lev.append((dx, ex))
    return b, fx, lev


def _hgrn_exponents_bwd(tril, d_b, d_fx, d_blast, lev_grads):
    ch = d_b.shape[0]
    rows = lax.broadcasted_iota(jnp.int32, (ch, HEAD), 0)
    db = d_b - d_fx
    dlf = jnp.zeros_like(d_b)
    for w, (ddx, dex) in zip(_hgrn_levels(ch), lev_grads):
        pos = rows % (2 * w)
        upper = pos >= w
        gu = jnp.where(upper, ddx, 0.0)
        if w >= SUBLANE:
            gl = jnp.where(upper, 0.0, dex)
            db = db + gu - gl
            diff = gl - gu
            for k in range(ch // (2 * w)):
                s = _colsum(diff[k * 2 * w:(k + 1) * 2 * w, :])
                db = db + jnp.where(rows == k * 2 * w + w - 1, s, 0.0)
        else:
            dlf = dlf + gu
            for i in range(1, w):
                dlf = dlf + pltpu.roll(jnp.where(pos >= w + i, gu, 0.0), ch - i, 0)
                dlf = dlf + pltpu.roll(jnp.where(pos <= w - 1 - i, dex, 0.0), i, 0)
    db = db + jnp.where(rows == ch - 1, _colsum(d_fx) + d_blast, 0.0)
    hi = db.astype(BF16)
    lo = (db - hi.astype(F32)).astype(BF16)
    d2 = _dot_tn(tril, jnp.concatenate([hi, lo], axis=1))
    return d2[:, 0:HEAD] + d2[:, HEAD:2 * HEAD] + dlf


def _lockstep(gens):
    results = [None] * len(gens)
    live = list(range(len(gens)))
    while live:
        for i in list(live):
            try:
                next(gens[i])
            except StopIteration as stop:
                results[i] = stop.value
                live.remove(i)
    return results


def _hgrn_gates(q_raw, z, lb):
    sz = _sigmoid(z)
    f = lb + (1.0 - lb) * sz
    q = q_raw * _sigmoid(q_raw)
    k = (1.0 - lb) * (1.0 - sz)
    return q, k, f, sz


def _hgrn_intra(q, k, lev, msk_ref, n_lev, ch):
    eye = (lax.broadcasted_iota(jnp.int32, (ch, ch), 0) == lax.broadcasted_iota(jnp.int32, (ch, ch), 1))
    a = jnp.where(eye, jnp.sum(q * k, axis=1, keepdims=True), 0.0)
    ops = []
    for lv in range(n_lev):
        eq = jnp.exp(lev[lv][0])
        ek = jnp.exp(lev[lv][1])
        qd = q * eq
        kd = k * ek
        a = a + msk_ref[lv] * _dot_nt(qd.astype(BF16), kd.astype(BF16))
        ops.append((eq, ek, qd, kd))
        yield
    return a, ops


def _hgrn_fwd(p, lb_logits, gnorm, mst, msk, n_lev, tm):
    tp = p.shape[0]
    ch = HG_CHUNK
    nct = tm // ch
    nt = tp // tm
    nr = mst.shape[0]
    base = D_POOL // HEAD

    hp = HG_HEADS_PER_STEP
    wide = hp * HEAD
    npr = wide // HG_PBLOCK

    def body(*refs):
        p_refs = refs[:4 * npr]
        lg_ref, gn_ref, mst_ref, msk_ref, y_ref, ss_ref, st_sc = refs[4 * npr:]

        @pl.when(pl.program_id(1) == 0)
        def _():
            st_sc[...] = jnp.zeros_like(st_sc)

        lb_all = _sigmoid(lg_ref[0:1, :] - lg_ref[1:2, :])

        def raw(seg, hh, r0):
            per = HG_PBLOCK // HEAD
            return p_refs[seg * npr + hh // per][pl.ds(r0, ch), (hh % per) * HEAD:(hh % per + 1) * HEAD]

        def one_head(hh, c, r0):
            ls = slice(hh * HEAD, (hh + 1) * HEAD)
            q_raw, z, v, g_raw, st = raw(0, hh, r0), raw(1, hh, r0), raw(2, hh, r0), raw(3, hh, r0), st_sc[hh]
            q, k, f, _ = _hgrn_gates(q_raw, z, lb_all[:, ls])
            yield
            b, fx, lev = _hgrn_exponents(mst_ref[...], jnp.log(f))
            yield
            qe = q * jnp.exp(b)
            a, _ = yield from _hgrn_intra(q, k, lev, msk_ref, n_lev, ch)
            v16 = v.astype(BF16)
            o = _dot_nt(qe.astype(BF16), st.astype(BF16)) + _dot(a.astype(BF16), v16)
            kl = k * jnp.exp(fx)
            st_new = st * jnp.exp(b[ch - 1:ch, :]) + _dot_tn(v16, kl.astype(BF16))
            yield
            rstd = lax.rsqrt(jnp.mean(o * o, axis=-1, keepdims=True) + EPS)
            return st, st_new, o * rstd * gn_ref[...] * (g_raw * _sigmoid(g_raw))

        def chunk(c, carry):
            r0 = pl.multiple_of(c * ch, ch)
            results = _lockstep([one_head(hh, c, r0) for hh in range(hp)])
            for hh, (st, st_new, y) in enumerate(results):
                ss_ref[hh, c] = st
                st_sc[hh] = st_new
                y_ref[pl.ds(r0, ch), hh * HEAD:(hh + 1) * HEAD] = y.astype(BF16)
            return carry

        lax.fori_loop(0, nct, chunk, 0)

    def pspec(seg, part):
        return pl.BlockSpec((tm, HG_PBLOCK),
                            lambda h, i: (i, (base + seg * HG_HEADS) * HEAD // HG_PBLOCK + h * npr + part))

    return pl.pallas_call(
        body, name="hgrn_fwd",
        grid=(HG_HEADS // hp, nt),
        in_specs=[pspec(seg, part) for seg in range(4) for part in range(npr)]
        + [pl.BlockSpec((2, wide), lambda h, i: (0, h)),
           pl.BlockSpec((1, HEAD), lambda h, i: (0, 0)),
           pl.BlockSpec((nr, ch), lambda h, i: (0, 0)),
           pl.BlockSpec((n_lev + 1, ch, ch), lambda h, i: (0, 0, 0))],
        out_specs=[pl.BlockSpec((tm, wide), lambda h, i: (i, h)),
                   pl.BlockSpec((hp, nct, HEAD, HEAD), lambda h, i: (h, i, 0, 0))],
        out_shape=[jax.ShapeDtypeStruct((tp, D_HGRN), BF16),
                   jax.ShapeDtypeStruct((HG_HEADS, tp // ch, HEAD, HEAD), F32)],
        scratch_shapes=[pltpu.VMEM((hp, HEAD, HEAD), F32)],
        compiler_params=_cparams(2),
    )(*([p] * (4 * npr)), lb_logits, gnorm, mst, msk)


def _hgrn_bwd(p, dyb, states, lb_logits, gnorm, mst, msk, n_lev, tm):
    tp = p.shape[0]
    ch = HG_CHUNK
    nct = tm // ch
    nt = tp // tm
    nr = mst.shape[0]
    base = D_POOL // HEAD

    hp = HG_HEADS_PER_STEP
    wide = hp * HEAD
    npr = wide // HG_PBLOCK

    def body(*refs):
        p_refs = refs[:4 * npr]
        (dy_ref, ss_ref, lg_ref, gn_ref, mst_ref, msk_ref,
         dq_ref, dz_ref, dv_ref, dg_ref, dlg_ref, dgn_ref, dst_sc, dlb_sc) = refs[4 * npr:]
        ti = pl.program_id(1)

        def raw(seg, hh, r0):
            per = HG_PBLOCK // HEAD
            return p_refs[seg * npr + hh // per][pl.ds(r0, ch), (hh % per) * HEAD:(hh % per + 1) * HEAD]

        @pl.when(ti == 0)
        def _():
            dst_sc[...] = jnp.zeros_like(dst_sc)
            dlb_sc[...] = jnp.zeros_like(dlb_sc)
            dgn_ref[...] = jnp.zeros_like(dgn_ref)

        lb_all = _sigmoid(lg_ref[0:1, :] - lg_ref[1:2, :])
        gn = gn_ref[...]

        def load_head(hh, c, r0):
            ls = slice(hh * HEAD, (hh + 1) * HEAD)
            return (raw(0, hh, r0), raw(1, hh, r0), raw(2, hh, r0), raw(3, hh, r0),
                    dy_ref[pl.ds(r0, ch), ls], ss_ref[hh, c], dst_sc[hh])

        def store_head(hh, r0, res):
            ls = slice(hh * HEAD, (hh + 1) * HEAD)
            dq_raw, dz, dv, dg_raw, dgn, dst_new, dlb = res
            dq_ref[pl.ds(r0, ch), ls] = dq_raw
            dz_ref[pl.ds(r0, ch), ls] = dz
            dv_ref[pl.ds(r0, ch), ls] = dv
            dg_ref[pl.ds(r0, ch), ls] = dg_raw
            dgn_ref[hh] += dgn
            dst_sc[hh] = dst_new
            dlb_sc[:, ls] += dlb

        def one_head(hh, loaded):
            ls = slice(hh * HEAD, (hh + 1) * HEAD)
            lb = lb_all[:, ls]
            q_raw, z, v, g_raw, dy, st, dst = loaded
            q, k, f, sz = _hgrn_gates(q_raw, z, lb)
            yield
            b, fx, lev = _hgrn_exponents(mst_ref[...], jnp.log(f))
            yield
            eb = jnp.exp(b)
            ef = jnp.exp(fx)
            elast = jnp.exp(b[ch - 1:ch, :])
            qe = q * eb
            kl = k * ef
            a, ops = yield from _hgrn_intra(q, k, lev, msk_ref, n_lev, ch)
            v16 = v.astype(BF16)
            st16 = st.astype(BF16)
            qe16 = qe.astype(BF16)
            kl16 = kl.astype(BF16)
            a16 = a.astype(BF16)
            o = _dot_nt(qe16, st16) + _dot(a16, v16)
            yield
            sg = _sigmoid(g_raw)
            rstd = lax.rsqrt(jnp.mean(o * o, axis=-1, keepdims=True) + EPS)
            oh = o * rstd
            dg_out = (dy * oh * gn * (sg * (1.0 + g_raw * (1.0 - sg)))).astype(BF16)
            don = dy * (g_raw * sg)
            dgn = _colsum(don * oh)
            doh = don * gn
            do = rstd * (doh - oh * jnp.mean(doh * oh, axis=-1, keepdims=True))
            do16 = do.astype(BF16)
            dst16 = dst.astype(BF16)
            yield
            dv = _dot_tn(a16, do16) + _dot_nt(kl16, dst16)
            da = msk_ref[n_lev] * _dot_nt(do16, v16)
            dqe = _dot(do16, st16)
            dkl = _dot(v16, dst16)
            dst_new = dst * elast + _dot_tn(do16, qe16)
            yield
            db_last = _colsum(dst * st) * elast
            dad = jnp.sum(do * v, axis=1, keepdims=True)
            dq = dad * k + dqe * eb
            dk = dad * q + dkl * ef
            lev_grads = []
            for lv in range(n_lev):
                eq, ek, qd, kd = ops[lv]
                gl = (msk_ref[lv] * da).astype(BF16)
                dqd = _dot(gl, kd.astype(BF16))
                dkd = _dot_tn(gl, qd.astype(BF16))
                dq = dq + dqd * eq
                dk = dk + dkd * ek
                lev_grads.append((dqd * qd, dkd * kd))
                yield
            dlogf = _hgrn_exponents_bwd(mst_ref[...], dqe * qe, dkl * kl, db_last, lev_grads)
            yield
            sq = _sigmoid(q_raw)
            dq_out = (dq * (sq * (1.0 + q_raw * (1.0 - sq)))).astype(BF16)
            dfk = dlogf / f - dk
            dz_out = (dfk * (1.0 - lb) * sz * (1.0 - sz)).astype(BF16)
            return dq_out, dz_out, dv.astype(BF16), dg_out, dgn, dst_new, _colsum(dfk * (1.0 - sz))

        def chunk(cc, carry):
            c = nct - 1 - cc
            r0 = pl.multiple_of(c * ch, ch)
            loaded = [load_head(hh, c, r0) for hh in range(HG_HEADS_PER_STEP)]
            results = _lockstep([one_head(hh, loaded[hh]) for hh in range(HG_HEADS_PER_STEP)])
            for hh in range(HG_HEADS_PER_STEP):
                store_head(hh, r0, results[hh])
            return carry

        lax.fori_loop(0, nct, chunk, 0, unroll=1)

        @pl.when(ti == nt - 1)
        def _():
            dl0 = dlb_sc[...] * lb_all * (1.0 - lb_all)
            dlg_ref[0:1, :] = dl0
            dlg_ref[1:2, :] = -dl0

    def pspec(seg, part):
        return pl.BlockSpec((tm, HG_PBLOCK), lambda h, i: (
            nt - 1 - i, (base + seg * HG_HEADS) * HEAD // HG_PBLOCK + h * npr + part))

    ospec = pl.BlockSpec((tm, wide), lambda h, i: (nt - 1 - i, h))
    return pl.pallas_call(
        body, name="hgrn_bwd",
        grid=(HG_HEADS // hp, nt),
        in_specs=[pspec(seg, part) for seg in range(4) for part in range(npr)]
        + [ospec, pl.BlockSpec((hp, nct, HEAD, HEAD), lambda h, i: (h, nt - 1 - i, 0, 0)),
           pl.BlockSpec((2, wide), lambda h, i: (0, h)),
           pl.BlockSpec((1, HEAD), lambda h, i: (0, 0)),
           pl.BlockSpec((nr, ch), lambda h, i: (0, 0)),
           pl.BlockSpec((n_lev + 1, ch, ch), lambda h, i: (0, 0, 0))],
        out_specs=[ospec, ospec, ospec, ospec,
                   pl.BlockSpec((2, wide), lambda h, i: (0, h)),
                   pl.BlockSpec((hp, 1, HEAD), lambda h, i: (h, 0, 0))],
        out_shape=[jax.ShapeDtypeStruct((tp, D_HGRN), BF16)] * 4
        + [jax.ShapeDtypeStruct((2, D_HGRN), F32), jax.ShapeDtypeStruct((HG_HEADS, 1, HEAD), F32)],
        scratch_shapes=[pltpu.VMEM((hp, HEAD, HEAD), F32), pltpu.VMEM((1, wide), F32)],
        compiler_params=_cparams(2),
    )(*([p] * (4 * npr)), dyb, states, lb_logits, gnorm, mst, msk)


def _tap_views(x, tr, halo, width):
    subs = {0: x}
    views = []
    for j in range(width):
        tiles, rem = divmod(width - 1 - j, SUBLANE)
        if rem not in subs:
            subs[rem] = pltpu.roll(x, rem, 0)
        start = halo - tiles * SUBLANE
        views.append(subs[rem][start:start + tr, :])
    return views


def _tap_views_t(y, tr, halo, width):
    n = tr + halo
    subs = {0: y}
    views = []
    for j in range(width):
        tiles, rem = divmod(width - 1 - j, SUBLANE)
        if rem not in subs:
            subs[rem] = pltpu.roll(y, n - rem, 0)
        views.append(subs[rem][tiles * SUBLANE:tiles * SUBLANE + tr, :])
    return views


def _weighted_sum(views, w_ref):
    acc = None
    for j, view in enumerate(views):
        term = view * w_ref[j:j + 1, :]
        acc = term if acc is None else acc + term
    return acc


def _conv_taps(x, w_ref, tr, halo, width):
    return _weighted_sum(_tap_views(x, tr, halo, width), w_ref)


def _conv_taps_t(y, w_ref, tr, halo, width):
    return _weighted_sum(_tap_views_t(y, tr, halo, width), w_ref)


def _ln_stats(cv):
    mu = jnp.mean(cv, axis=-1, keepdims=True)
    xc = cv - mu
    rstd = lax.rsqrt(jnp.mean(xc * xc, axis=-1, keepdims=True) + EPS)
    return rstd, xc * rstd


def _convmod_fwd(p, w, bias, ln_g, ln_b, tr):
    tp = p.shape[0]
    nt = tp // tr
    nb = D_CONV // HEAD

    def body(a_ref, b_ref, w_ref, bi_ref, g_ref, be_ref, y_ref, cv_ref, usc):
        usc[0:CONV_HALO, :] = jnp.zeros((CONV_HALO, HEAD), F32)
        usc[CONV_HALO:, :] = a_ref[...] * _sigmoid(b_ref[...])

        def tile(r, carry):
            r0 = pl.multiple_of(r * tr, SUBLANE)
            x = usc[pl.ds(r0, tr + CONV_HALO), :]
            cv = _conv_taps(x, w_ref, tr, CONV_HALO, CONV_WIDTH) + bi_ref[...]
            cv_ref[pl.ds(r0, tr), :] = cv
            _, xh = _ln_stats(cv)
            un = xh * g_ref[...] + be_ref[...]
            y_ref[pl.ds(r0, tr), :] = (un * _sigmoid(un)).astype(BF16)
            return carry

        lax.fori_loop(0, nt, tile, 0)

    vec = lambda: pl.BlockSpec((1, HEAD), lambda j: (0, j))
    return pl.pallas_call(
        body, name="convmod_fwd",
        grid=(nb,),
        in_specs=[pl.BlockSpec((tp, HEAD), lambda j: (0, j)), pl.BlockSpec((tp, HEAD), lambda j: (0, nb + j)),
                  pl.BlockSpec((CONV_HALO, HEAD), lambda j: (0, j)), vec(), vec(), vec()],
        out_specs=[pl.BlockSpec((tp, HEAD), lambda j: (0, j))] * 2,
        out_shape=[jax.ShapeDtypeStruct((tp, D_CONV), BF16), jax.ShapeDtypeStruct((tp, D_CONV), F32)],
        scratch_shapes=[pltpu.VMEM((tp + CONV_HALO, HEAD), F32)],
        compiler_params=_cparams(1),
    )(p, p, w, bias, ln_g, ln_b)


def _convmod_bwd(p, cv_saved, dyc, w, ln_g, ln_b, tr):
    tp = p.shape[0]
    nt = tp // tr
    nb = D_CONV // HEAD

    def body(a_ref, b_ref, cv_ref, dy_ref, w_ref, g_ref, be_ref, da_ref, db_ref, dw_ref, dv_ref, usc, dsc):
        usc[0:CONV_HALO, :] = jnp.zeros((CONV_HALO, HEAD), F32)
        usc[CONV_HALO:, :] = a_ref[...] * _sigmoid(b_ref[...])
        dsc[tp:, :] = jnp.zeros((CONV_HALO, HEAD), F32)
        dw_ref[...] = jnp.zeros_like(dw_ref)
        dv_ref[...] = jnp.zeros_like(dv_ref)

        def tile1(r, carry):
            r0 = pl.multiple_of(r * tr, SUBLANE)
            x = usc[pl.ds(r0, tr + CONV_HALO), :]
            views = _tap_views(x, tr, CONV_HALO, CONV_WIDTH)
            rstd, xh = _ln_stats(cv_ref[pl.ds(r0, tr), :])
            un = xh * g_ref[...] + be_ref[...]
            sg = _sigmoid(un)
            dun = dy_ref[pl.ds(r0, tr), :] * (sg * (1.0 + un * (1.0 - sg)))
            dv_ref[0, 1:2, :] += _colsum(dun * xh)
            dv_ref[0, 2:3, :] += _colsum(dun)
            dxh = dun * g_ref[...]
            dcv = rstd * (dxh - jnp.mean(dxh, axis=-1, keepdims=True)
                          - xh * jnp.mean(dxh * xh, axis=-1, keepdims=True))
            dv_ref[0, 0:1, :] += _colsum(dcv)
            for j in range(CONV_WIDTH):
                dw_ref[0, j:j + 1, :] += _colsum(dcv * views[j])
            dsc[pl.ds(r0, tr), :] = dcv
            return carry

        lax.fori_loop(0, nt, tile1, 0)

        def tile2(r, carry):
            r0 = pl.multiple_of(r * tr, SUBLANE)
            y = dsc[pl.ds(r0, tr + CONV_HALO), :]
            du = _conv_taps_t(y, w_ref, tr, CONV_HALO, CONV_WIDTH)
            a = a_ref[pl.ds(r0, tr), :]
            sb = _sigmoid(b_ref[pl.ds(r0, tr), :])
            da_ref[pl.ds(r0, tr), :] = (du * sb).astype(BF16)
            db_ref[pl.ds(r0, tr), :] = (du * a * sb * (1.0 - sb)).astype(BF16)
            return carry

        lax.fori_loop(0, nt, tile2, 0)

    vec = lambda: pl.BlockSpec((1, HEAD), lambda j: (0, j))
    col = lambda: pl.BlockSpec((tp, HEAD), lambda j: (0, j))
    return pl.pallas_call(
        body, name="convmod_bwd",
        grid=(nb,),
        in_specs=[col(), pl.BlockSpec((tp, HEAD), lambda j: (0, nb + j)), col(), col(),
                  pl.BlockSpec((CONV_HALO, HEAD), lambda j: (0, j)), vec(), vec()],
        out_specs=[col(), col(), pl.BlockSpec((1, CONV_HALO, HEAD), lambda j: (j, 0, 0)),
                   pl.BlockSpec((1, SUBLANE, HEAD), lambda j: (j, 0, 0))],
        out_shape=[jax.ShapeDtypeStruct((tp, D_CONV), BF16), jax.ShapeDtypeStruct((tp, D_CONV), BF16),
                   jax.ShapeDtypeStruct((nb, CONV_HALO, HEAD), F32), jax.ShapeDtypeStruct((nb, SUBLANE, HEAD), F32)],
        scratch_shapes=[pltpu.VMEM((tp + CONV_HALO, HEAD), F32), pltpu.VMEM((tp + CONV_HALO, HEAD), F32)],
        compiler_params=_cparams(1),
    )(p, p, cv_saved, dyc, w, ln_g, ln_b)


def _log1p_small(y):
    return jnp.where(y < 1e-4, y * (1.0 - 0.5 * y), jnp.log(1.0 + y))


def _softplus(x):
    return jnp.maximum(x, 0.0) + _log1p_small(jnp.exp(-jnp.abs(x)))


def _expm1(x):
    return jnp.where(jnp.abs(x) < 1e-2, x * (1.0 + 0.5 * x * (1.0 + x * (1.0 / 3.0))), jnp.exp(x) - 1.0)


def _gelu_parts(x):
    c = 0.7978845608028654
    inner = c * (x + 0.044715 * x * x * x)
    th = jnp.tanh(inner)
    gelu = 0.5 * x * (1.0 + th)
    dgelu = 0.5 * (1.0 + th) + 0.5 * x * (1.0 - th * th) * c * (1.0 + 3.0 * 0.044715 * x * x)
    return gelu, dgelu


def _lru_gates(x_all, tp, cw_ref, cb_ref, wa_ref, ba_ref, wx_ref, bx_ref, lam_ref):
    u = _conv_taps(x_all, cw_ref, tp, LRU_HALO, LRU_CONV) + cb_ref[...]
    u16 = u.astype(BF16)
    r = _sigmoid(_dot(u16, wa_ref[0]) + ba_ref[...])
    i = _sigmoid(_dot(u16, wx_ref[0]) + bx_ref[...])
    sp = _softplus(-lam_ref[...])
    la = -LRU_C * r * sp
    a = jnp.exp(la)
    mult = jnp.sqrt(-_expm1(2.0 * la))
    return u, r, i, a, mult, sp


def _lru_specs(tp, nb):
    col = lambda k: pl.BlockSpec((tp, HEAD), functools.partial(lambda j, k: (0, k * nb + j), k=k))
    vec = lambda: pl.BlockSpec((1, HEAD), lambda j: (0, j))
    mat = lambda: pl.BlockSpec((1, HEAD, HEAD), lambda j: (j, 0, 0))
    return col, vec, mat


def _lru_fwd(p, cw, cb, wa, ba, wx, bx, lam):
    tp = p.shape[0]
    nb = D_LRU // HEAD
    ng = tp // SUBLANE

    def body(x_ref, gt_ref, cw_ref, cb_ref, wa_ref, ba_ref, wx_ref, bx_ref, lam_ref, y_ref, hs_ref,
             xsc, asc, bsc):
        xsc[0:LRU_HALO, :] = jnp.zeros((LRU_HALO, HEAD), F32)
        xsc[LRU_HALO:, :] = x_ref[...]
        u, r, i, a, mult, _ = _lru_gates(xsc[...], tp, cw_ref, cb_ref, wa_ref, ba_ref, wx_ref, bx_ref, lam_ref)
        rows = lax.broadcasted_iota(jnp.int32, (tp, HEAD), 0)
        b = jnp.where(rows == 0, 1.0, mult) * (i * u)
        sub = rows % SUBLANE
        for k in (1, 2, 4):
            m = sub >= k
            b = jnp.where(m, a * pltpu.roll(b, k, 0) + b, b)
            a = jnp.where(m, a * pltpu.roll(a, k, 0), a)
        asc[...] = a
        bsc[...] = b

        def grp(g, carry):
            r0 = pl.multiple_of(g * SUBLANE, SUBLANE)
            h = bsc[pl.ds(r0, SUBLANE), :] + asc[pl.ds(r0, SUBLANE), :] * carry
            hs_ref[pl.ds(r0, SUBLANE), :] = h
            return jnp.broadcast_to(h[SUBLANE - 1:SUBLANE, :], (SUBLANE, HEAD))

        lax.fori_loop(0, ng, grp, jnp.zeros((SUBLANE, HEAD), F32))
        gelu, _ = _gelu_parts(gt_ref[...])
        y_ref[...] = (gelu * hs_ref[...]).astype(BF16)

    col, vec, mat = _lru_specs(tp, nb)
    return pl.pallas_call(
        body, name="lru_fwd",
        grid=(nb,),
        in_specs=[col(2), col(3), pl.BlockSpec((LRU_CONV, HEAD), lambda j: (0, j)), vec(), mat(), vec(), mat(),
                  vec(), vec()],
        out_specs=[pl.BlockSpec((tp, HEAD), lambda j: (0, j)), pl.BlockSpec((tp, HEAD), lambda j: (0, j))],
        out_shape=[jax.ShapeDtypeStruct((tp, D_LRU), BF16), jax.ShapeDtypeStruct((tp, D_LRU), F32)],
        scratch_shapes=[pltpu.VMEM((tp + LRU_HALO, HEAD), F32), pltpu.VMEM((tp, HEAD), F32),
                        pltpu.VMEM((tp, HEAD), F32)],
        compiler_params=_cparams(1),
    )(p, p, cw, cb, wa, ba, wx, bx, lam)


def _lru_bwd(p, hs, dyd, cw, cb, wa, ba, wx, bx, lam):
    tp = p.shape[0]
    nb = D_LRU // HEAD
    ng = tp // SUBLANE

    def body(x_ref, gt_ref, hs_ref, dy_ref, cw_ref, cb_ref, wa_ref, ba_ref, wx_ref, bx_ref, lam_ref,
             dx_ref, dgt_ref, dwa_ref, dwx_ref, dv_ref, xsc, asc, bsc, gsc, dusc):
        xsc[0:LRU_HALO, :] = jnp.zeros((LRU_HALO, HEAD), F32)
        xsc[LRU_HALO:, :] = x_ref[...]
        x_all = xsc[...]
        u, r, i, a, mult, sp = _lru_gates(x_all, tp, cw_ref, cb_ref, wa_ref, ba_ref, wx_ref, bx_ref, lam_ref)
        rows = lax.broadcasted_iota(jnp.int32, (tp, HEAD), 0)
        hs = hs_ref[...]
        dy = dy_ref[...]
        gelu, dgelu = _gelu_parts(gt_ref[...])
        dgt_ref[...] = (dy * hs * dgelu).astype(BF16)
        bb = dy * gelu
        aa = jnp.where(rows == tp - 1, 0.0, pltpu.roll(a, tp - 1, 0))
        sub = rows % SUBLANE
        for k in (1, 2, 4):
            m = sub < SUBLANE - k
            bb = jnp.where(m, aa * pltpu.roll(bb, tp - k, 0) + bb, bb)
            aa = jnp.where(m, aa * pltpu.roll(aa, tp - k, 0), aa)
        asc[...] = aa
        bsc[...] = bb

        def grp(gi, carry):
            g = ng - 1 - gi
            r0 = pl.multiple_of(g * SUBLANE, SUBLANE)
            gg = bsc[pl.ds(r0, SUBLANE), :] + asc[pl.ds(r0, SUBLANE), :] * carry
            gsc[pl.ds(r0, SUBLANE), :] = gg
            return jnp.broadcast_to(gg[0:1, :], (SUBLANE, HEAD))

        lax.fori_loop(0, ng, grp, jnp.zeros((SUBLANE, HEAD), F32))
        g = gsc[...]
        first = rows == 0
        hprev = jnp.where(first, 0.0, pltpu.roll(hs, 1, 0))
        iu = i * u
        d_iu = g * jnp.where(first, 1.0, mult)
        dmult_term = jnp.where(first, 0.0, g * iu * (-(a * a) / mult))
        dla = g * hprev * a + dmult_term
        dr = dla * (-LRU_C) * sp
        dv_ref[0, 7:8, :] = _colsum(dla * (LRU_C * r) * _sigmoid(-lam_ref[...]))
        dpr = dr * r * (1.0 - r)
        dpi = d_iu * u * i * (1.0 - i)
        dv_ref[0, 5:6, :] = _colsum(dpr)
        dv_ref[0, 6:7, :] = _colsum(dpi)
        u16 = u.astype(BF16)
        dpr16 = dpr.astype(BF16)
        dpi16 = dpi.astype(BF16)
        dwa_ref[0] = _dot_tn(u16, dpr16)
        dwx_ref[0] = _dot_tn(u16, dpi16)
        du = d_iu * i + _dot_nt(dpr16, wa_ref[0]) + _dot_nt(dpi16, wx_ref[0])
        dv_ref[0, 4:5, :] = _colsum(du)
        for j in range(LRU_CONV):
            sh = LRU_CONV - 1 - j
            xs = x_all if sh == 0 else pltpu.roll(x_all, sh, 0)
            dv_ref[0, j:j + 1, :] = _colsum(du * xs[LRU_HALO:, :])
        dusc[0:tp, :] = du
        dusc[tp:, :] = jnp.zeros((LRU_HALO, HEAD), F32)
        dx_ref[...] = _conv_taps_t(dusc[...], cw_ref, tp, LRU_HALO, LRU_CONV).astype(BF16)

    col, vec, mat = _lru_specs(tp, nb)
    ocol = lambda: pl.BlockSpec((tp, HEAD), lambda j: (0, j))
    return pl.pallas_call(
        body, name="lru_bwd",
        grid=(nb,),
        in_specs=[col(2), col(3), ocol(), ocol(), pl.BlockSpec((LRU_CONV, HEAD), lambda j: (0, j)), vec(), mat(),
                  vec(), mat(), vec(), vec()],
        out_specs=[ocol(), ocol(), mat(), mat(), pl.BlockSpec((1, SUBLANE, HEAD), lambda j: (j, 0, 0))],
        out_shape=[jax.ShapeDtypeStruct((tp, D_LRU), BF16), jax.ShapeDtypeStruct((tp, D_LRU), BF16),
                   jax.ShapeDtypeStruct((nb, HEAD, HEAD), F32), jax.ShapeDtypeStruct((nb, HEAD, HEAD), F32),
                   jax.ShapeDtypeStruct((nb, SUBLANE, HEAD), F32)],
        scratch_shapes=[pltpu.VMEM((tp + LRU_HALO, HEAD), F32), pltpu.VMEM((tp, HEAD), F32),
                        pltpu.VMEM((tp, HEAD), F32), pltpu.VMEM((tp, HEAD), F32),
                        pltpu.VMEM((tp + LRU_HALO, HEAD), F32)],
        compiler_params=_cparams(1),
    )(p, p, hs, dyd, cw, cb, wa, ba, wx, bx, lam)


def _mesh_pos():
    return lax.axis_index("x"), lax.axis_index("y"), lax.axis_index("c")


def _other_chips(x, y):
    return [(1 - x, y), (x, 1 - y), (1 - x, 1 - y)]


ANY = pl.BlockSpec(memory_space=pl.ANY)


HBM = pl.BlockSpec(memory_space=pltpu.HBM)
SEM = pl.BlockSpec(memory_space=pltpu.SEMAPHORE)
DATAFLOW = pltpu.SideEffectType.DATAFLOW_SIDE_EFFECTING
N_PEERS = 4


def _in_hbm(a):
    return pltpu.with_memory_space_constraint(a, pltpu.HBM)


def _gather_peers(x, y, c):
    return [((ox, oy, c), 2 * ox + oy) for ox, oy in _other_chips(x, y)] + [((x, y, 1 - c), 2 * x + y)]


def _gather_refs(src, land, slot, c, split):
    if not split:
        return src, land.at[slot]
    half = src.shape[0] // 2
    return src.at[pl.ds(c * half, half)], land.at[slot, pl.ds(c * half, half)]


def _gather_start(arrs, split):
    n = len(arrs)

    def body(*refs):
        ins, lands = refs[:n], refs[n:2 * n]
        ssem, rsem = refs[2 * n:2 * n + 2]
        token = refs[-1]
        x, y, c = _mesh_pos()
        chip = 2 * x + y
        for k in range(n):
            for j, (dev, _) in enumerate(_gather_peers(x, y, c)):
                src, dst = _gather_refs(ins[k], lands[k], chip, c, split[k] and j < N_PEERS - 1)
                pltpu.make_async_remote_copy(
                    src_ref=src, dst_ref=dst, send_sem=ssem.at[N_PEERS * k + j],
                    recv_sem=rsem.at[N_PEERS * k + j], device_id=dev, device_id_type=MESH).start()
        token[...] = jnp.zeros_like(token)

    lands = [_in_hbm(lax.empty((N_SHARD,) + a.shape, a.dtype)) for a in arrs]
    out = pl.pallas_call(
        body, name="gather_start",
        in_specs=[HBM] * (2 * n),
        out_specs=[SEM, SEM] + [HBM] * (2 * n) + [pl.BlockSpec(memory_space=pltpu.VMEM)],
        out_shape=[pltpu.SemaphoreType.DMA((N_PEERS * n,)), pltpu.SemaphoreType.DMA((N_PEERS * n,))]
        + [pltpu.HBM(a.shape, a.dtype) for a in arrs]
        + [pltpu.HBM((N_SHARD,) + a.shape, a.dtype) for a in arrs]
        + [jax.ShapeDtypeStruct((SUBLANE, LANE), F32)],
        input_output_aliases={k: 2 + k for k in range(2 * n)},
        compiler_params=pltpu.CompilerParams(has_side_effects=DATAFLOW),
    )(*[_in_hbm(a) for a in arrs], *lands)
    return out[0], out[1], list(out[2:2 + n]), list(out[2 + n:2 + 2 * n]), out[-1]


def _gather_wait(ssem, rsem, srcs, lands, ks, after, split=False):
    n = len(ks)

    def body(*refs):
        ins, lnd = refs[:n], refs[n:2 * n]
        ssem_ref, rsem_ref = refs[2 * n:2 * n + 2]
        x, y, c = _mesh_pos()
        for i, k in enumerate(ks):
            for j, (dev, pchip) in enumerate(_gather_peers(x, y, c)):
                src, dst = _gather_refs(ins[i], lnd[i], pchip, c, split and j < N_PEERS - 1)
                cp = pltpu.make_async_remote_copy(
                    src_ref=src, dst_ref=dst, send_sem=ssem_ref.at[N_PEERS * k + j],
                    recv_sem=rsem_ref.at[N_PEERS * k + j], device_id=dev, device_id_type=MESH)
                cp.wait_send()
                cp.wait_recv()

    out = pl.pallas_call(
        body, name="gather_wait",
        in_specs=[HBM] * (2 * n) + [SEM, SEM] + [ANY] * len(after),
        out_specs=[HBM] * (2 * n),
        out_shape=[pltpu.HBM(a.shape, a.dtype) for a in srcs] + [pltpu.HBM(a.shape, a.dtype) for a in lands],
        input_output_aliases={k: k for k in range(2 * n)},
        compiler_params=pltpu.CompilerParams(has_side_effects=DATAFLOW),
    )(*srcs, *lands, ssem, rsem, *after)
    return list(out[n:])


def _pair_forward(lands):
    n = len(lands)

    def body(*refs):
        outs = refs[n:2 * n]
        ssem, rsem = refs[2 * n:]
        x, y, c = _mesh_pos()
        sibling = (x, y, 1 - c)
        cps = []
        for k in range(n):
            half = lands[k].shape[1] // 2
            for j, (ox, oy) in enumerate(_other_chips(x, y)):
                mine = outs[k].at[2 * ox + oy, pl.ds(c * half, half)]
                cp = pltpu.make_async_remote_copy(src_ref=mine, dst_ref=mine, send_sem=ssem.at[3 * k + j],
                                                  recv_sem=rsem.at[3 * k + j], device_id=sibling, device_id_type=MESH)
                cp.start()
                cps.append(cp)
        for k in range(n):
            half = lands[k].shape[1] // 2
            for j, (ox, oy) in enumerate(_other_chips(x, y)):
                theirs = outs[k].at[2 * ox + oy, pl.ds((1 - c) * half, half)]
                pltpu.make_async_remote_copy(src_ref=theirs, dst_ref=theirs, send_sem=ssem.at[3 * k + j],
                                             recv_sem=rsem.at[3 * k + j], device_id=sibling,
                                             device_id_type=MESH).wait_recv()
        for cp in cps:
            cp.wait_send()

    return pl.pallas_call(
        body, name="pair_forward",
        in_specs=[ANY] * n, out_specs=[ANY] * n,
        out_shape=[jax.ShapeDtypeStruct(a.shape, a.dtype) for a in lands],
        scratch_shapes=[pltpu.SemaphoreType.DMA((3 * n,)), pltpu.SemaphoreType.DMA((3 * n,))],
        input_output_aliases={k: k for k in range(n)},
    )(*lands)


N_SOURCES = 7


def _reduce_peers(x, y, c):
    peers = []
    for ox, oy in _other_chips(x, y):
        for rel in range(2):
            peers.append(((ox, oy, c + rel - 2 * c * rel), 2 * ox + oy))
    peers.append(((x, y, 1 - c), 2 * x + y))
    return peers


def _reduce_start(arrs, slots):
    n = len(arrs)

    def body(*refs):
        ins, lands = refs[:n], refs[n:2 * n]
        ssem, rsem = refs[2 * n:2 * n + 2]
        token = refs[-1]
        x, y, c = _mesh_pos()
        me = 2 * (2 * x + y) + c
        for k in range(n):
            half = arrs[k].shape[1] // 2
            for p, (dev, ochip) in enumerate(_reduce_peers(x, y, c)):
                pltpu.make_async_remote_copy(
                    src_ref=ins[k].at[ochip, pl.ds(dev[2] * half, half)], dst_ref=lands[k].at[me],
                    send_sem=ssem.at[N_SOURCES * k + p], recv_sem=rsem.at[N_SOURCES * k + p],
                    device_id=dev, device_id_type=MESH).start()
        token[...] = jnp.zeros_like(token)

    out = pl.pallas_call(
        body, name="reduce_start",
        in_specs=[HBM] * (2 * n),
        out_specs=[SEM, SEM] + [HBM] * (2 * n) + [pl.BlockSpec(memory_space=pltpu.VMEM)],
        out_shape=[pltpu.SemaphoreType.DMA((N_SOURCES * n,)), pltpu.SemaphoreType.DMA((N_SOURCES * n,))]
        + [pltpu.HBM(a.shape, a.dtype) for a in arrs] + [pltpu.HBM(a.shape, a.dtype) for a in slots]
        + [jax.ShapeDtypeStruct((SUBLANE, LANE), F32)],
        input_output_aliases={k: 2 + k for k in range(2 * n)},
        compiler_params=pltpu.CompilerParams(has_side_effects=DATAFLOW),
    )(*[_in_hbm(a) for a in arrs], *[_in_hbm(a) for a in slots])
    return out[0], out[1], list(out[2:2 + n]), list(out[2 + n:2 + 2 * n]), out[-1]


def _reduce_wait(ssem, rsem, arrs, slots, after):
    n = len(arrs)

    def body(*refs):
        ins, lnd = refs[:n], refs[n:2 * n]
        ssem_ref, rsem_ref = refs[2 * n:2 * n + 2]
        x, y, c = _mesh_pos()
        for k in range(n):
            half = arrs[k].shape[1] // 2
            for p, (dev, ochip) in enumerate(_reduce_peers(x, y, c)):
                cp = pltpu.make_async_remote_copy(
                    src_ref=ins[k].at[ochip, pl.ds(dev[2] * half, half)], dst_ref=lnd[k].at[2 * ochip + dev[2]],
                    send_sem=ssem_ref.at[N_SOURCES * k + p], recv_sem=rsem_ref.at[N_SOURCES * k + p],
                    device_id=dev, device_id_type=MESH)
                cp.wait_send()
                cp.wait_recv()

    out = pl.pallas_call(
        body, name="reduce_wait",
        in_specs=[HBM] * (2 * n) + [SEM, SEM] + [ANY] * len(after),
        out_specs=[HBM] * (2 * n),
        out_shape=[pltpu.HBM(a.shape, a.dtype) for a in arrs] + [pltpu.HBM(a.shape, a.dtype) for a in slots],
        input_output_aliases={k: k for k in range(2 * n)},
        compiler_params=pltpu.CompilerParams(has_side_effects=DATAFLOW),
    )(*arrs, *slots, ssem, rsem, *after)
    return list(out[n:])


def _own_part(arrs, chip, core, me):
    n = len(arrs)
    nb = GRAD_ROW_BLOCKS

    def body(chip_ref, core_ref, me_ref, *refs):
        for k in range(n):
            refs[n + k][...] = refs[k][...]

    def blk(a):
        return (1, a.shape[1] // 2 // nb, a.shape[2])

    grid_spec = pltpu.PrefetchScalarGridSpec(
        num_scalar_prefetch=3, grid=(nb,),
        in_specs=[pl.BlockSpec(blk(a), lambda i, ch, co, me: (ch[0], co[0] * nb + i, 0)) for a in arrs],
        out_specs=[pl.BlockSpec(blk(a), lambda i, ch, co, me: (me[0], i, 0)) for a in arrs])
    return pl.pallas_call(
        body, name="own_part", grid_spec=grid_spec,
        out_shape=[jax.ShapeDtypeStruct((N_DEV, a.shape[1] // 2, a.shape[2]), a.dtype) for a in arrs],
        compiler_params=_cparams(1),
    )(chip, core, me, *arrs)


def _sum_exchange(arrs):
    n = len(arrs)
    nb = GRAD_ROW_BLOCKS

    def body(*refs):
        ins, outs, bufs = refs[:n], refs[n:2 * n], refs[2 * n:3 * n]
        lsem, ssem, rsem = refs[3 * n:]
        i = pl.program_id(0)
        x, y, c = _mesh_pos()

        def copies(k, j):
            h = arrs[k].shape[1]
            hb = h // nb
            src = bufs[k].at[j]
            mine = outs[k].at[pl.ds(c * h + j * hb, hb)]
            theirs = outs[k].at[pl.ds((1 - c) * h + j * hb, hb)]
            sibling = dict(send_sem=ssem.at[k, j], recv_sem=rsem.at[k, j], device_id=(x, y, 1 - c),
                           device_id_type=MESH)
            return (pltpu.make_async_copy(src, mine, lsem.at[k, j]),
                    pltpu.make_async_remote_copy(src_ref=src, dst_ref=mine, **sibling),
                    pltpu.make_async_remote_copy(src_ref=src, dst_ref=theirs, **sibling))

        for k in range(n):
            r = ins[k]
            acc = r[0].astype(F32)
            for dev in range(1, N_DEV):
                acc = acc + r[dev].astype(F32)
            bufs[k][i] = acc
            local, send, _ = copies(k, i)
            local.start()
            send.start()

        @pl.when(i == nb - 1)
        def _():
            for k in range(n):
                for j in range(nb):
                    local, send, landing = copies(k, j)
                    local.wait()
                    send.wait_send()
                    landing.wait_recv()

    return pl.pallas_call(
        body, name="sum_exchange", grid=(nb,),
        in_specs=[pl.BlockSpec((N_DEV, a.shape[1] // nb, a.shape[2]), lambda i: (0, i, 0)) for a in arrs],
        out_specs=[ANY] * n,
        out_shape=[jax.ShapeDtypeStruct((2 * a.shape[1], a.shape[2]), F32) for a in arrs],
        scratch_shapes=[pltpu.VMEM((nb, a.shape[1] // nb, a.shape[2]), F32) for a in arrs]
        + [pltpu.SemaphoreType.DMA((n, nb))] * 3,
        compiler_params=_cparams(1),
    )(*arrs)


def _small_own(v, me):
    m = v.shape[0]

    def body(me_ref, v_ref, o_ref):
        o_ref[0] = v_ref[...]

    grid_spec = pltpu.PrefetchScalarGridSpec(
        num_scalar_prefetch=1, grid=(1,),
        in_specs=[pl.BlockSpec((m, LANE), lambda i, me: (0, 0))],
        out_specs=pl.BlockSpec((1, m, LANE), lambda i, me: (me[0], 0, 0)))
    return pl.pallas_call(
        body, name="small_own", grid_spec=grid_spec,
        out_shape=jax.ShapeDtypeStruct((N_DEV, m, LANE), v.dtype),
        compiler_params=_cparams(1),
    )(me, v)


def _small_start(v, slots):
    def body(v_ref, land, ssem, rsem, v_thru, land_thru, token):
        del v_thru, land_thru
        x, y, c = _mesh_pos()
        me = 2 * (2 * x + y) + c
        for p, (dev, _) in enumerate(_reduce_peers(x, y, c)):
            pltpu.make_async_remote_copy(src_ref=v_ref, dst_ref=land.at[me], send_sem=ssem.at[p],
                                         recv_sem=rsem.at[p], device_id=dev, device_id_type=MESH).start()
        token[...] = jnp.zeros_like(token)

    out = pl.pallas_call(
        body, name="small_start",
        in_specs=[HBM, HBM],
        out_specs=[SEM, SEM, HBM, HBM, pl.BlockSpec(memory_space=pltpu.VMEM)],
        out_shape=[pltpu.SemaphoreType.DMA((N_SOURCES,)), pltpu.SemaphoreType.DMA((N_SOURCES,)),
                   pltpu.HBM(v.shape, v.dtype), pltpu.HBM(slots.shape, slots.dtype),
                   jax.ShapeDtypeStruct((SUBLANE, LANE), F32)],
        input_output_aliases={0: 2, 1: 3},
        compiler_params=pltpu.CompilerParams(has_side_effects=DATAFLOW),
    )(_in_hbm(v), _in_hbm(slots))
    return out


def _small_wait(ssem, rsem, v, slots, after):
    def body(*refs):
        v_ref, land, ssem_ref, rsem_ref = refs[:4]
        x, y, c = _mesh_pos()
        for p, (dev, ochip) in enumerate(_reduce_peers(x, y, c)):
            cp = pltpu.make_async_remote_copy(src_ref=v_ref, dst_ref=land.at[2 * ochip + dev[2]],
                                              send_sem=ssem_ref.at[p], recv_sem=rsem_ref.at[p],
                                              device_id=dev, device_id_type=MESH)
            cp.wait_send()
            cp.wait_recv()

    out = pl.pallas_call(
        body, name="small_wait",
        in_specs=[HBM, HBM, SEM, SEM] + [ANY] * len(after),
        out_specs=[HBM, HBM],
        out_shape=[pltpu.HBM(v.shape, v.dtype), pltpu.HBM(slots.shape, slots.dtype)],
        input_output_aliases={0: 0, 1: 1},
        compiler_params=pltpu.CompilerParams(has_side_effects=DATAFLOW),
    )(v, slots, ssem, rsem, *after)
    return out[1]


GRAD_ROW_BLOCKS = 2


N_DEV = 8


def _adamw_math(w, g, m, v):
    m2 = ADAM_B1 * m + (1.0 - ADAM_B1) * g
    v2 = ADAM_B2 * v + (1.0 - ADAM_B2) * (g * g)
    m_hat = m2 / (1.0 - ADAM_B1 ** ADAM_STEP)
    v_hat = v2 / (1.0 - ADAM_B2 ** ADAM_STEP)
    delta = -ADAM_LR * (m_hat / (jnp.sqrt(v_hat) + ADAM_EPS) + ADAM_WD * w)
    return delta, m2, v2


def _adamw(w, m, v, gs, nblk):
    nl, r, n = w.shape
    assert nl == len(gs) and nl in (1, 2)
    br = r // nblk

    def body(w_ref, m_ref, v_ref, *rest):
        g_refs, (go_ref, d_ref, mo_ref, vo_ref) = rest[:nl], rest[nl:]
        g = g_refs[0][...]
        if nl == 2:
            g = jnp.where(pl.program_id(0) == 0, g, g_refs[1][...])
        delta, m2, v2 = _adamw_math(w_ref[0], g, m_ref[0], v_ref[0])
        go_ref[0] = g
        d_ref[0] = delta
        mo_ref[0] = m2
        vo_ref[0] = v2

    spec = pl.BlockSpec((1, br, n), lambda l, i: (l, i, 0))
    g_specs = [pl.BlockSpec((br, n), lambda l, i: (i, 0))] if nl == 1 else [
        pl.BlockSpec((br, n), lambda l, i: (jnp.where(l == 0, i, nblk - 1), 0)),
        pl.BlockSpec((br, n), lambda l, i: (jnp.where(l == 1, i, 0), 0))]
    return pl.pallas_call(
        body, name="adamw", grid=(nl, nblk),
        in_specs=[spec, spec, spec] + g_specs,
        out_specs=[spec] * 4,
        out_shape=[jax.ShapeDtypeStruct((nl, r, n), F32)] * 4,
        compiler_params=_cparams(2),
    )(w, m, v, *gs)


def _small_reduce_adamw(parts, w, m, v, rep_rows, sh_rows):
    mrows = rep_rows + N_SHARD * sh_rows + LOSS_ROWS

    def body(p_ref, w_ref, m_ref, v_ref, go_ref, d_ref, mo_ref, vo_ref, loss_ref):
        x, y, _ = _mesh_pos()
        mine = rep_rows + (2 * x + y) * sh_rows
        g_rep = p_ref[0:rep_rows, :]
        g_sh = p_ref[pl.ds(pl.multiple_of(mine, SUBLANE), sh_rows), :]
        loss = p_ref[mrows - LOSS_ROWS:mrows, :]
        for k in range(1, N_DEV):
            g_rep = g_rep + p_ref[k * mrows:k * mrows + rep_rows, :]
            g_sh = g_sh + p_ref[pl.ds(pl.multiple_of(k * mrows + mine, SUBLANE), sh_rows), :]
            loss = loss + p_ref[(k + 1) * mrows - LOSS_ROWS:(k + 1) * mrows, :]
        g = jnp.concatenate([g_rep, g_sh], axis=0)
        delta, m2, v2 = _adamw_math(w_ref[...], g, m_ref[...], v_ref[...])
        go_ref[...] = g
        d_ref[...] = delta
        mo_ref[...] = m2
        vo_ref[...] = v2
        loss_ref[...] = loss

    return pl.pallas_call(
        body, name="small_reduce_adamw",
        out_shape=[jax.ShapeDtypeStruct((rep_rows + sh_rows, 128), F32)] * 4
        + [jax.ShapeDtypeStruct((LOSS_ROWS, 128), F32)],
        compiler_params=pltpu.CompilerParams(vmem_limit_bytes=VMEM_LIMIT_MB * 1024 * 1024),
    )(parts, w, m, v)


LANE = 128
REP_SPEC = (("ffn1_norm", 16), ("mix_norm", 16), ("ffn2_norm", 16), ("final_norm", 8), ("pool_w", 256),
            ("pool_scale", 8), ("hgrn_lb_logits", 16), ("hgrn_gnorm", 8), ("lru_wa", 512), ("lru_wx", 512))
SH_SPEC = (("meta_tokens", 32), ("conv_w", 32), ("lru_conv_w", 8), ("conv_b", 8), ("conv_ln_g", 8),
           ("conv_ln_b", 8), ("lru_conv_b", 8), ("lru_ba", 8), ("lru_bx", 8), ("lru_lambda", 8))
REP_ROWS = sum(r for _, r in REP_SPEC)
SH_ROWS = sum(r for _, r in SH_SPEC)


def _pack_rows(vals, spec):
    parts = []
    for name, rows in spec:
        val = vals[name].astype(F32)
        if val.shape[-1] < LANE:
            flat = jnp.pad(val.reshape(-1, val.shape[-1]), ((0, 0), (0, LANE - val.shape[-1])))
        else:
            flat = val.reshape(-1, LANE)
        if flat.shape[0] < rows:
            flat = jnp.concatenate([flat, jnp.zeros((rows - flat.shape[0], LANE), F32)], axis=0)
        parts.append(flat)
    return jnp.concatenate(parts, axis=0)


def _unpack_rows(packed, spec, shapes):
    out = {}
    off = 0
    for name, rows in spec:
        shp = shapes[name]
        width = min(shp[-1], LANE)
        n = int(np.prod(shp)) // width
        out[name] = packed[off:off + n, :width].reshape(shp)
        off += rows
    return out


def _block_diag(blocks):
    n, b, _ = blocks.shape
    return sum(jnp.pad(blocks[g], ((g * b, (n - 1 - g) * b), (g * b, (n - 1 - g) * b))) for g in range(n))


def _diag_blocks(mat, n):
    b = mat.shape[0] // n
    return jnp.stack([mat[g * b:(g + 1) * b, g * b:(g + 1) * b] for g in range(n)])


BIG = ("ffn1_wg", "ffn1_wu", "ffn2_wg", "ffn2_wu", "ffn1_wd", "ffn2_wd", "w_in_even", "w_out_even",
       "w_in_odd", "w_out_odd")
WEIGHT_NAMES = ('meta_tokens', 'ffn1_norm', 'ffn1_wg', 'ffn1_wu', 'ffn1_wd', 'mix_norm', 'ffn2_norm', 'ffn2_wg',
                'ffn2_wu', 'ffn2_wd', 'w_in_even', 'pool_w', 'pool_scale', 'hgrn_lb_logits', 'hgrn_gnorm',
                'w_out_even', 'w_in_odd', 'conv_w', 'conv_b', 'conv_ln_g', 'conv_ln_b', 'lru_conv_w',
                'lru_conv_b', 'lru_wa', 'lru_ba', 'lru_wx', 'lru_bx', 'lru_lambda', 'w_out_odd', 'final_norm')


def _block_diag2(heads):
    nb = heads.shape[0] // 2
    return jnp.stack([_block_diag(heads[2 * j:2 * j + 2]) for j in range(nb)])


def _diag_blocks2(mats):
    return jnp.concatenate([_diag_blocks(mats[j], 2) for j in range(mats.shape[0])], axis=0)


GATHER_GROUPS = (
    (("small", 0),),
    (("ffn1_wg", 0), ("ffn1_wu", 0), ("ffn1_wd", 0)),
    (("w_in_even", 0), ("w_out_even", 0)),
    (("ffn2_wg", 0), ("ffn2_wu", 0), ("ffn2_wd", 0)),
    (("ffn1_wg", 1), ("ffn1_wu", 1), ("ffn1_wd", 1)),
    (("w_in_odd", 0), ("w_out_odd", 0)),
    (("ffn2_wg", 1), ("ffn2_wu", 1), ("ffn2_wd", 1)),
)
ADAM_ROW_BLOCKS = {"ffn1_wg": 2, "ffn1_wu": 2, "ffn2_wg": 2, "ffn2_wu": 2, "ffn1_wd": 2, "ffn2_wd": 2,
                   "w_in_even": 4, "w_out_even": 2, "w_in_odd": 4, "w_out_odd": 2}
TRANSPOSED = ("ffn1_wg", "ffn1_wu", "ffn2_wg", "ffn2_wu", "w_in_even")
SCATTER_DEPTH = 2
LOSS_ROWS = 8
GATHER_SPLIT = (1, 4)


def _unpack_small(sm, shapes):
    per_shard = [_unpack_rows(sm[s], SH_SPEC, shapes) for s in range(N_SHARD)]
    full = {}
    for n, _ in SH_SPEC:
        full[n] = jnp.concatenate([per_shard[s][n].reshape(-1, shapes[n][-1]) for s in range(N_SHARD)], axis=-1)
    full["conv_w"] = jnp.concatenate([full["conv_w"], jnp.zeros((CONV_HALO - CONV_WIDTH, D_CONV), F32)], axis=0)
    return full


def _local_step(x, tgt, w, shapes, fetch, emit, emit_small):
    s_len, d = x.shape
    t_real = s_len + N_META
    tp = -(-t_real // ROW_ALIGN) * ROW_ALIGN
    tm = _tile(tp, 832, ROW_ALIGN)
    tm_small = _tile(tp, 832, 16)
    tr = _tile(tp, 416, SUBLANE)
    tm_wgrad = _wgrad_tiles(tp, 1024)

    def gain(name, layer):
        return w[name][layer:layer + 1]

    pool_wbd = _block_diag(w["pool_w"][0]).astype(BF16)
    pool_scale = w["pool_scale"]
    wa_bd = _block_diag2(w["lru_wa"][0]).astype(BF16)
    wx_bd = _block_diag2(w["lru_wx"][0]).astype(BF16)
    mst, msk, n_lev = _hgrn_consts(HG_CHUNK)

    (sm,) = fetch(0, None)
    sf = _unpack_small(sm, shapes)
    h0 = jnp.concatenate([sf["meta_tokens"], x, jnp.zeros((tp - t_real, d), F32)], axis=0)
    tgt_pad = jnp.concatenate([jnp.zeros((N_META, d), F32), tgt, jnp.zeros((tp - t_real, d), F32)], axis=0)
    f1l0 = fetch(1, h0)
    h1, *s1 = _ffn_fwd(h0, gain("ffn1_norm", 0), *f1l0, tm)
    w_in_even4, w_out_even4 = fetch(2, h1)
    w_out_even = w_out_even4.reshape(d, d)
    even_piece = [(w_in_even4.reshape(D_IN_EVEN, d), (D_IN_EVEN, d), (0, 0))]
    p0, nm0 = _proj_fwd(h1, gain("mix_norm", 0), even_piece, tm_small, wt=True)
    ya = _pool_fwd(p0, pool_wbd, pool_scale, tr)
    yb, states = _hgrn_fwd(p0, w["hgrn_lb_logits"], w["hgrn_gnorm"], mst, msk, n_lev, tm)
    h2 = _out_fwd(h1, ya, yb, w_out_even, tm)
    f2l0 = fetch(3, h2)
    h3, *s2 = _ffn_fwd(h2, gain("ffn2_norm", 0), *f2l0, tm)
    f1l1 = fetch(4, h3)
    h4, *s3 = _ffn_fwd(h3, gain("ffn1_norm", 1), *f1l1, tm)
    w_in_odd4, w_out_odd4 = fetch(5, h4)
    w_out_odd = w_out_odd4.reshape(d, d)
    odd_pieces = [(w_in_odd4, (1, d, D_IN_ODD // N_SHARD), (k, 0, 0)) for k in range(N_SHARD)]
    p1, nm1 = _proj_fwd(h4, gain("mix_norm", 1), odd_pieces, tm_small)
    yc, conv_out = _convmod_fwd(p1, sf["conv_w"], sf["conv_b"], sf["conv_ln_g"], sf["conv_ln_b"], tr)
    lru_args = (sf["lru_conv_w"], sf["lru_conv_b"], wa_bd, sf["lru_ba"], wx_bd, sf["lru_bx"], sf["lru_lambda"])
    yd, hs = _lru_fwd(p1, *lru_args)
    h5 = _out_fwd(h4, yc, yd, w_out_odd, tm)
    f2l1 = fetch(6, h5)
    h6, *s4 = _ffn_fwd(h5, gain("ffn2_norm", 1), *f2l1, tm)
    loss, dh6, dg_final = _loss_bwd(h6, w["final_norm"].reshape(1, d), tgt_pad, t_real, tm)

    def ffn_bwd(dho, h, saved, norm, wts, after=()):
        ga, gb, sa, n = saved
        dh, da, db, dg, dy = _ffn_bwd_act(dho, h, norm, ga, gb, *wts, tm, after)
        return dh, dg, _ffn_bwd_w([(da, n, None), (db, n, None), (sa, dy, None)], tm_wgrad)

    dh5, dg_f2_l1, g = ffn_bwd(dh6, h5, s4, gain("ffn2_norm", 1), f2l1)
    sent = emit((("ffn2_wg", 1), ("ffn2_wu", 1), ("ffn2_wd", 1)), g)
    dyc, dyd, dw_out_odd = _out_bwd(dh5, yc, yd, w_out_odd, tm, tuple(sent))
    dca, dcb, dconv_w, dconv_vec = _convmod_bwd(p1, conv_out, dyc, sf["conv_w"], sf["conv_ln_g"],
                                                sf["conv_ln_b"], tr)
    dlx, dlg, dwa_bd, dwx_bd, dlru_vec = _lru_bwd(p1, hs, dyd, *lru_args)
    dp1 = [dca, dcb, dlx, dlg]
    dh4, dg_mix_l1 = _proj_bwd_act(dh5, h4, gain("mix_norm", 1), dp1, odd_pieces, tm_small)
    dw_in_odd = jnp.stack(_proj_bwd_w(nm1, dp1, tm_wgrad))
    dh3, dg_f1_l1, g = ffn_bwd(dh4, h3, s3, gain("ffn1_norm", 1), f1l1)
    sent = emit((("w_out_odd", 0), ("w_in_odd", 0), ("ffn1_wg", 1), ("ffn1_wu", 1), ("ffn1_wd", 1)),
                [dw_out_odd.reshape(N_SHARD, d // N_SHARD, d), dw_in_odd] + list(g))
    dh2, dg_f2_l0, g = ffn_bwd(dh3, h2, s2, gain("ffn2_norm", 0), f2l0, tuple(sent))
    sent = emit((("ffn2_wg", 0), ("ffn2_wu", 0), ("ffn2_wd", 0)), g)
    dya, dyb, dw_out_even = _out_bwd(dh2, ya, yb, w_out_even, tm, tuple(sent))
    dpool, dpool_wbd, dpool_scale = _pool_bwd(p0, dya, pool_wbd, pool_scale, tr)
    dq, dz, dv, dgate, dlb_logits, dgn_heads = _hgrn_bwd(p0, dyb, states, w["hgrn_lb_logits"], w["hgrn_gnorm"],
                                                         mst, msk, n_lev, tm)
    dp0 = [dpool, dq, dz, dv, dgate]
    dh1, dg_mix_l0 = _proj_bwd_act(dh2, h1, gain("mix_norm", 0), dp0, even_piece, tm_small, wt=True)
    dw_in_even_t = jnp.concatenate(_proj_bwd_w(nm0, dp0, tm_wgrad, wt=True), axis=0)
    ga, gb, sa, n1 = s1
    (dwd_f1l0,) = _ffn_bwd_w([(sa, dh1, 0.5)], tm_wgrad)
    sent = emit((("w_out_even", 0), ("w_in_even", 0), ("ffn1_wd", 0)),
                [dw_out_even.reshape(N_SHARD, d // N_SHARD, d),
                 dw_in_even_t.reshape(N_SHARD, D_IN_EVEN // N_SHARD, d), dwd_f1l0])
    dh0, da, db, dg_f1_l0, _ = _ffn_bwd_act(dh1, h0, gain("ffn1_norm", 0), ga, gb, *f1l0, tm, tuple(sent))

    grad_x = dh0[N_META:t_real]
    rep = {
        "ffn1_norm": jnp.concatenate([dg_f1_l0, dg_f1_l1], axis=0),
        "mix_norm": jnp.concatenate([dg_mix_l0, dg_mix_l1], axis=0),
        "ffn2_norm": jnp.concatenate([dg_f2_l0, dg_f2_l1], axis=0),
        "final_norm": dg_final,
        "pool_w": _diag_blocks(dpool_wbd, len(POOL_WINDOWS)),
        "pool_scale": dpool_scale,
        "hgrn_lb_logits": dlb_logits,
        "hgrn_gnorm": jnp.sum(dgn_heads, axis=0),
        "lru_wa": _diag_blocks2(dwa_bd),
        "lru_wx": _diag_blocks2(dwx_bd),
    }
    dmeta = jnp.transpose(dh0[:N_META].reshape(N_META, N_SHARD, 2, LANE), (1, 0, 2, 3)).reshape(N_SHARD, 32, LANE)
    packs = [_pack_rows(rep, REP_SPEC)]
    for s in range(N_SHARD):
        sh = {
            "meta_tokens": dmeta[s], "conv_w": dconv_w[s], "lru_conv_w": dlru_vec[s, 0:4],
            "conv_b": dconv_vec[s, 0:1], "conv_ln_g": dconv_vec[s, 1:2], "conv_ln_b": dconv_vec[s, 2:3],
            "lru_conv_b": dlru_vec[s, 4:5], "lru_ba": dlru_vec[s, 5:6], "lru_bx": dlru_vec[s, 6:7],
            "lru_lambda": dlru_vec[s, 7:8],
        }
        packs.append(_pack_rows(sh, SH_SPEC))
    packs.append(jnp.pad(loss, ((0, LOSS_ROWS - 1), (0, LANE - 1))))
    sent = emit_small(jnp.concatenate(packs, axis=0))
    emit((("ffn1_wg", 0), ("ffn1_wu", 0)), _ffn_bwd_w([(da, n1, None), (db, n1, None)], tm_wgrad, tuple(sent)))
    return grad_x


def kernel(x, meta_tokens, ffn1_norm, ffn1_wg, ffn1_wu, ffn1_wd, mix_norm, ffn2_norm, ffn2_wg, ffn2_wu, ffn2_wd, w_in_even, pool_w, pool_scale, hgrn_lb_logits, hgrn_gnorm, w_out_even, w_in_odd, conv_w, conv_b, conv_ln_g, conv_ln_b, lru_conv_w, lru_conv_b, lru_wa, lru_ba, lru_wx, lru_bx, lru_lambda, w_out_odd, final_norm, loss_target, m_meta_tokens, m_ffn1_norm, m_ffn1_wg, m_ffn1_wu, m_ffn1_wd, m_mix_norm, m_ffn2_norm, m_ffn2_wg, m_ffn2_wu, m_ffn2_wd, m_w_in_even, m_pool_w, m_pool_scale, m_hgrn_lb_logits, m_hgrn_gnorm, m_w_out_even, m_w_in_odd, m_conv_w, m_conv_b, m_conv_ln_g, m_conv_ln_b, m_lru_conv_w, m_lru_conv_b, m_lru_wa, m_lru_ba, m_lru_wx, m_lru_bx, m_lru_lambda, m_w_out_odd, m_final_norm, v_meta_tokens, v_ffn1_norm, v_ffn1_wg, v_ffn1_wu, v_ffn1_wd, v_mix_norm, v_ffn2_norm, v_ffn2_wg, v_ffn2_wu, v_ffn2_wd, v_w_in_even, v_pool_w, v_pool_scale, v_hgrn_lb_logits, v_hgrn_gnorm, v_w_out_even, v_w_in_odd, v_conv_w, v_conv_b, v_conv_ln_g, v_conv_ln_b, v_lru_conv_w, v_lru_conv_b, v_lru_wa, v_lru_ba, v_lru_wx, v_lru_bx, v_lru_lambda, v_w_out_odd, v_final_norm):
    args = locals()
    w = {n: args[n] for n in WEIGHT_NAMES}
    m = {n: args["m_" + n] for n in WEIGHT_NAMES}
    v = {n: args["v_" + n] for n in WEIGHT_NAMES}
    shapes = {n: w[n].shape for n in WEIGHT_NAMES}
    core = lax.axis_index("c").astype(jnp.int32).reshape(1)
    chip = (2 * lax.axis_index("x") + lax.axis_index("y")).astype(jnp.int32).reshape(1)
    me = 2 * chip + core

    def view(a, n):
        return jnp.swapaxes(a, 1, 2) if n in TRANSPOSED else a

    wv, mv, vv = [{n: view(src[n], n) for n in BIG} for src in (w, m, v)]

    def shard(key):
        n, l = key
        return _pack_rows(w, SH_SPEC) if n == "small" else wv[n][l].astype(BF16)

    started = {}
    for groups in (GATHER_GROUPS[:2], GATHER_GROUPS[2:]):
        gkeys = [key for grp in groups for key in grp]
        ssem, rsem, srcs, lands, token = _gather_start([shard(key) for key in gkeys],
                                                       [any(key in GATHER_GROUPS[g] for g in GATHER_SPLIT)
                                                        for key in gkeys])
        for k, key in enumerate(gkeys):
            started[key] = (ssem, rsem, srcs[k], lands[k], k, token)

    def pack_small(src):
        return jnp.concatenate([_pack_rows(src, REP_SPEC), _pack_rows(src, SH_SPEC)], axis=0)

    small_packs = [pack_small(src) for src in (w, m, v)]

    def fetch(group, after):
        st = [started[key] for key in GATHER_GROUPS[group]]
        deps = (st[0][5],) if after is None else (after,)
        if group == 1:
            deps += (started[GATHER_GROUPS[2][0]][5],) + tuple(small_packs)
        split = group in GATHER_SPLIT
        got = _gather_wait(st[0][0], st[0][1], [s[2] for s in st], [s[3] for s in st], [s[4] for s in st], deps,
                           split)
        return _pair_forward(got) if split else got

    in_flight, reduced = [], {}

    def collect(entry, after):
        gkeys, gs_sem, gr_sem, grads_thru, slots_thru, _ = entry
        slots = _reduce_wait(gs_sem, gr_sem, grads_thru, slots_thru, after)
        full = _sum_exchange(slots)
        reduced.update(zip(gkeys, full))
        return full[0]

    def emit(gkeys, grads):
        grads = list(grads)
        in_flight.append((gkeys,) + tuple(_reduce_start(grads, _own_part(grads, chip, core, me))))
        token = in_flight[-1][-1]
        if len(in_flight) > SCATTER_DEPTH:
            return token, collect(in_flight[-1 - SCATTER_DEPTH], (token,))
        return (token,)

    small_flight = []

    def emit_small(part):
        small_flight.append(_small_start(part, _small_own(part, me)))
        return (small_flight[0][4],)

    grad_x = _local_step(x[0], loss_target[0], w, shapes, fetch, emit, emit_small)

    out_g, out_d, out_m, out_v = {}, {}, {}, {}
    deps = (in_flight[-1][-1],)

    def adamw_ready():
        done = ()
        for n in BIG:
            layers = range(shapes[n][0])
            if n not in out_g and all((n, l) in reduced for l in layers):
                res = _adamw(wv[n], mv[n], vv[n], [reduced[(n, l)] for l in layers], ADAM_ROW_BLOCKS[n])
                out_g[n], out_d[n], out_m[n], out_v[n] = [view(r, n) for r in res]
                done += (res[1],)
        return done

    deps += adamw_ready()
    for entry in in_flight[-SCATTER_DEPTH:-1]:
        collect(entry, deps)
        deps += adamw_ready()
    s_ssem, s_rsem, s_part, s_slots, _ = small_flight[0]
    small_all = _small_wait(s_ssem, s_rsem, s_part, s_slots, deps)
    small_res = _small_reduce_adamw(small_all.reshape(-1, LANE), *small_packs, REP_ROWS, SH_ROWS)
    collect(in_flight[-1], deps + (small_res[0],))
    adamw_ready()

    loss = small_res[4][0, 0]
    for dst, packed in zip((out_g, out_d, out_m, out_v), small_res[:4]):
        dst.update(_unpack_rows(packed[:REP_ROWS], REP_SPEC, shapes))
        dst.update(_unpack_rows(packed[REP_ROWS:], SH_SPEC, shapes))

    return (loss, grad_x[None], *[out_g[n] for n in WEIGHT_NAMES], *[out_d[n] for n in WEIGHT_NAMES],
            *[out_m[n] for n in WEIGHT_NAMES], *[out_v[n] for n in WEIGHT_NAMES])
```

```python
import functools

import numpy as np
import jax
import jax.numpy as jnp
from jax import lax
from jax.experimental import pallas as pl
from jax.experimental.pallas import tpu as pltpu

F32 = jnp.float32
BF16 = jnp.bfloat16
MESH = pl.DeviceIdType.MESH

EPS = 1e-6
N_META = 16
D_FF = 2816
N_SHARD = 4
FF_SHARD = D_FF // N_SHARD
D_POOL = 256
POOL_GROUP = 64
POOL_WINDOWS = (2, 4, 8, 16)
D_HGRN = 768
HG_HEADS = 6
HEAD = 128
HG_CHUNK = 64
HG_HEADS_PER_STEP = 6
PIPE_ROW_BLOCKS = 2
HG_PBLOCK = 256
D_IN_EVEN = D_POOL + 4 * D_HGRN
D_CONV = 512
CONV_WIDTH = 31
CONV_HALO = 32
D_LRU = 512
LRU_CONV = 4
LRU_HALO = 8
LRU_C = 8.0
D_IN_ODD = 2 * D_CONV + 2 * D_LRU
SUBLANE = 8
MXU_DEPTH = 256
ROW_ALIGN = 64

ADAM_LR = 0.001
ADAM_B1 = 0.9
ADAM_B2 = 0.999
ADAM_EPS = 1e-08
ADAM_WD = 0.01
ADAM_STEP = 10

VMEM_LIMIT_MB = 56


def _cparams(n_grid_axes=0, vmem_mb=VMEM_LIMIT_MB):
    sem = ("arbitrary",) * n_grid_axes if n_grid_axes else None
    return pltpu.CompilerParams(dimension_semantics=sem, vmem_limit_bytes=vmem_mb * 1024 * 1024)


def _tile(n, target, mult):
    best = None
    for t in range(mult, min(n, target) + 1, mult):
        if n % t == 0:
            best = t
    assert best is not None, (n, target, mult)
    return best


def _dot(a, b):
    return jnp.dot(a, b, preferred_element_type=F32)


def _dot_nt(a, b):
    return lax.dot_general(a, b, (((1,), (1,)), ((), ())), preferred_element_type=F32)


def _dot_tn(a, b):
    return lax.dot_general(a, b, (((0,), (0,)), ((), ())), preferred_element_type=F32)


def _sigmoid(x):
    return 1.0 / (1.0 + jnp.exp(-x))


def _colsum(x):
    return jnp.sum(x, axis=0, keepdims=True)


def _rms_stats(h):
    rstd = lax.rsqrt(jnp.mean(h * h, axis=-1, keepdims=True) + EPS)
    return rstd, h * rstd


def _rms_bwd(dn, g, rstd, xhat):
    dng = dn * g
    dh = rstd * (dng - xhat * jnp.mean(dng * xhat, axis=-1, keepdims=True))
    return dh, _colsum(dn * xhat)


def _ffn_fwd(h, norm, wg4, wu4, wd4, tm):
    tp, d = h.shape
    nt = tp // tm

    def body(h_ref, g_ref, wg_ref, wu_ref, wd_ref, ho_ref, ga_ref, gb_ref, sa_ref, n_ref, n_sc, acc):
        s = pl.program_id(1)

        @pl.when(s == 0)
        def _():
            hh = h_ref[...]
            rstd, xhat = _rms_stats(hh)
            n = (xhat * g_ref[...]).astype(BF16)
            n_sc[...] = n
            n_ref[...] = n
            acc[...] = jnp.zeros_like(acc)

        n = n_sc[...]
        a = _dot_nt(n, wg_ref[s])
        b = _dot_nt(n, wu_ref[s])
        sig = _sigmoid(a)
        sil = a * sig
        ga_ref[0] = (sig * (1.0 + a * (1.0 - sig)) * b).astype(BF16)
        gb_ref[0] = sil.astype(BF16)
        sg = (sil * b).astype(BF16)
        sa_ref[0] = sg
        acc[...] += _dot(sg, wd_ref[s])

        @pl.when(s == N_SHARD - 1)
        def _():
            ho_ref[...] = h_ref[...] + 0.5 * acc[...]

    resident = pl.BlockSpec((N_SHARD, FF_SHARD, d), lambda i, s: (0, 0, 0), pipeline_mode=pl.Buffered(1))
    return pl.pallas_call(
        body, name="ffn_fwd",
        grid=(nt, N_SHARD),
        in_specs=[
            pl.BlockSpec((tm, d), lambda i, s: (i, 0)),
            pl.BlockSpec((1, d), lambda i, s: (0, 0)),
            resident, resident, resident,
        ],
        out_specs=[
            pl.BlockSpec((tm, d), lambda i, s: (i, 0)),
            pl.BlockSpec((1, tm, FF_SHARD), lambda i, s: (s, i, 0)),
            pl.BlockSpec((1, tm, FF_SHARD), lambda i, s: (s, i, 0)),
            pl.BlockSpec((1, tm, FF_SHARD), lambda i, s: (s, i, 0)),
            pl.BlockSpec((tm, d), lambda i, s: (i, 0)),
        ],
        out_shape=[
            jax.ShapeDtypeStruct((tp, d), F32),
            jax.ShapeDtypeStruct((N_SHARD, tp, FF_SHARD), BF16),
            jax.ShapeDtypeStruct((N_SHARD, tp, FF_SHARD), BF16),
            jax.ShapeDtypeStruct((N_SHARD, tp, FF_SHARD), BF16),
            jax.ShapeDtypeStruct((tp, d), BF16),
        ],
        scratch_shapes=[pltpu.VMEM((tm, d), BF16), pltpu.VMEM((tm, d), F32)],
        compiler_params=_cparams(2),
    )(h, norm, wg4, wu4, wd4)


def _ffn_bwd_act(dho, h, norm, ga4, gb4, wg4, wu4, wd4, tm, after=()):
    tp, d = h.shape
    nt = tp // tm

    def body(dho_ref, h_ref, g_ref, ga_ref, gb_ref, wg_ref, wu_ref, wd_ref, *rest):
        dh_ref, da_ref, db_ref, dg_ref, dy_ref, dn_sc = rest[len(after):]
        i = pl.program_id(0)
        s = pl.program_id(1)

        @pl.when(s == 0)
        def _():
            dy_ref[...] = (0.5 * dho_ref[...]).astype(BF16)
            dn_sc[...] = jnp.zeros_like(dn_sc)

        @pl.when((s == 0) & (i == 0))
        def _():
            dg_ref[...] = jnp.zeros_like(dg_ref)

        hb = tm // PIPE_ROW_BLOCKS
        rows = [pl.ds(k * hb, hb) for k in range(PIPE_ROW_BLOCKS)]

        def shard_step(last):
            ds = _dot_nt(dy_ref[rows[0], :], wd_ref[0])
            dg = None
            for k in range(PIPE_ROW_BLOCKS):
                ds_next = _dot_nt(dy_ref[rows[k + 1], :], wd_ref[0]) if k + 1 < PIPE_ROW_BLOCKS else None
                da = (ds * ga_ref[0, rows[k], :].astype(F32)).astype(BF16)
                db = (ds * gb_ref[0, rows[k], :].astype(F32)).astype(BF16)
                da_ref[0, rows[k], :] = da
                db_ref[0, rows[k], :] = db
                dn = dn_sc[rows[k], :] + (_dot(da, wg_ref[0]) + _dot(db, wu_ref[0]))
                if last:
                    rstd, xhat = _rms_stats(h_ref[rows[k], :])
                    dh, dgk = _rms_bwd(dn, g_ref[...], rstd, xhat)
                    dh_ref[rows[k], :] = dho_ref[rows[k], :] + dh
                    dg = dgk if dg is None else dg + dgk
                else:
                    dn_sc[rows[k], :] = dn
                ds = ds_next
            if last:
                dg_ref[...] += dg

        pl.when(s < N_SHARD - 1)(functools.partial(shard_step, False))
        pl.when(s == N_SHARD - 1)(functools.partial(shard_step, True))

    return pl.pallas_call(
        body, name="ffn_bwd_act",
        grid=(nt, N_SHARD),
        in_specs=[
            pl.BlockSpec((tm, d), lambda i, s: (i, 0)),
            pl.BlockSpec((tm, d), lambda i, s: (i, 0)),
            pl.BlockSpec((1, d), lambda i, s: (0, 0)),
            pl.BlockSpec((1, tm, FF_SHARD), lambda i, s: (s, i, 0)),
            pl.BlockSpec((1, tm, FF_SHARD), lambda i, s: (s, i, 0)),
            pl.BlockSpec((1, FF_SHARD, d), lambda i, s: (s, 0, 0)),
            pl.BlockSpec((1, FF_SHARD, d), lambda i, s: (s, 0, 0)),
            pl.BlockSpec((1, FF_SHARD, d), lambda i, s: (s, 0, 0)),
        ] + [pl.BlockSpec(memory_space=pl.ANY)] * len(after),
        out_specs=[
            pl.BlockSpec((tm, d), lambda i, s: (i, 0)),
            pl.BlockSpec((1, tm, FF_SHARD), lambda i, s: (s, i, 0)),
            pl.BlockSpec((1, tm, FF_SHARD), lambda i, s: (s, i, 0)),
            pl.BlockSpec((1, d), lambda i, s: (0, 0)),
            pl.BlockSpec((tm, d), lambda i, s: (i, 0)),
        ],
        out_shape=[
            jax.ShapeDtypeStruct((tp, d), F32),
            jax.ShapeDtypeStruct((N_SHARD, tp, FF_SHARD), BF16),
            jax.ShapeDtypeStruct((N_SHARD, tp, FF_SHARD), BF16),
            jax.ShapeDtypeStruct((1, d), F32),
            jax.ShapeDtypeStruct((tp, d), BF16),
        ],
        scratch_shapes=[pltpu.VMEM((tm, d), F32)],
        compiler_params=_cparams(2),
    )(dho, h, norm, ga4, gb4, wg4, wu4, wd4, *after)


def _wgrad_tiles(tp, target):
    tm = min(target, tp) // MXU_DEPTH * MXU_DEPTH
    nt = tp // tm
    tail = tp - nt * tm
    assert tail == 0 or (nt * tm) % tail == 0 and tail % 16 == 0, (tp, tm, tail)
    return tm, nt, tail


def _ffn_bwd_w(pairs, tiles, after=()):
    npair = len(pairs)
    tp, d = pairs[0][1].shape
    tm, nt, tail = tiles
    rhs_list = []
    for _, rhs, _ in pairs:
        if all(rhs is not r for r in rhs_list):
            rhs_list.append(rhs)
    rhs_of = [[rhs is r for r in rhs_list].index(True) for _, rhs, _ in pairs]
    nrhs = len(rhs_list)
    nin = nrhs + npair
    ntail = nin if tail else 0

    def body(*refs):
        rhs_refs = refs[:nrhs]
        lhs_refs = refs[nrhs:nin]
        rhs_tails = refs[nin:nin + ntail][:nrhs]
        lhs_tails = refs[nin:nin + ntail][nrhs:]
        rest = refs[nin + ntail + len(after):]
        out_refs, accs = rest[:npair], rest[npair:]
        i = pl.program_id(1)

        @pl.when(i == 0)
        def _():
            for acc in accs:
                acc[...] = jnp.zeros_like(acc)

        def accumulate(lhs, rhs):
            for k, (_, _, scale) in enumerate(pairs):
                r = rhs[rhs_of[k]][...]
                if scale is not None:
                    r = (scale * r).astype(BF16)
                accs[k][...] += _dot_tn(lhs[k][0], r)

        accumulate(lhs_refs, rhs_refs)

        @pl.when(i == nt - 1)
        def _():
            if tail:
                accumulate(lhs_tails, rhs_tails)
            for k in range(npair):
                out_refs[k][0] = accs[k][...].astype(BF16)

    tail_specs, tail_args = [], []
    if tail:
        tb = nt * tm // tail
        tail_specs = ([pl.BlockSpec((tail, d), lambda s, i: (tb, 0))] * nrhs
                      + [pl.BlockSpec((1, tail, FF_SHARD), lambda s, i: (s, tb, 0))] * npair)
        tail_args = [*rhs_list, *[lhs for lhs, _, _ in pairs]]
    return pl.pallas_call(
        body, name="ffn_bwd_w",
        grid=(N_SHARD, nt),
        in_specs=[pl.BlockSpec((tm, d), lambda s, i: (i, 0))] * nrhs
        + [pl.BlockSpec((1, tm, FF_SHARD), lambda s, i: (s, i, 0))] * npair
        + tail_specs
        + [pl.BlockSpec(memory_space=pl.ANY)] * len(after),
        out_specs=[pl.BlockSpec((1, FF_SHARD, d), lambda s, i: (s, 0, 0))] * npair,
        out_shape=[jax.ShapeDtypeStruct((N_SHARD, FF_SHARD, d), BF16)] * npair,
        scratch_shapes=[pltpu.VMEM((FF_SHARD, d), F32)] * npair,
        compiler_params=_cparams(2),
    )(*rhs_list, *[lhs for lhs, _, _ in pairs], *tail_args, *after)


def _proj_fwd(h, norm, w_pieces, tm, wt=False):
    tp, d = h.shape
    widths = [bs[-2] if wt else bs[-1] for _, bs, _ in w_pieces]
    ntot = sum(widths)
    npc = len(w_pieces)

    def body(*refs):
        h_ref, g_ref = refs[:2]
        w_refs = refs[2:2 + npc]
        p_ref, n_ref = refs[2 + npc:]
        rstd, xhat = _rms_stats(h_ref[...])
        n = (xhat * g_ref[...]).astype(BF16)
        n_ref[...] = n
        off = 0
        for k in range(npc):
            w = w_refs[k][...]
            w = w.reshape(w.shape[-2], w.shape[-1])
            p_ref[:, off:off + widths[k]] = _dot_nt(n, w) if wt else _dot(n, w)
            off += widths[k]

    in_specs = [pl.BlockSpec((tm, d), lambda i: (i, 0)), pl.BlockSpec((1, d), lambda i: (0, 0))]
    for _, bs, idx in w_pieces:
        in_specs.append(pl.BlockSpec(bs, functools.partial(lambda i, idx: idx, idx=idx)))
    return pl.pallas_call(
        body, name="proj_fwd",
        grid=(tp // tm,),
        in_specs=in_specs,
        out_specs=[pl.BlockSpec((tm, ntot), lambda i: (i, 0)), pl.BlockSpec((tm, d), lambda i: (i, 0))],
        out_shape=[jax.ShapeDtypeStruct((tp, ntot), F32), jax.ShapeDtypeStruct((tp, d), BF16)],
        compiler_params=_cparams(1),
    )(h, norm, *[w for w, _, _ in w_pieces])


def _proj_bwd_act(dres, h, norm, dp_pieces, w_pieces, tm, wt=False):
    tp, d = h.shape
    npc = len(dp_pieces)
    nw = len(w_pieces)
    assert nw == npc or (nw == 1 and wt)

    def body(*refs):
        dres_ref, h_ref, g_ref = refs[:3]
        dp_refs = refs[3:3 + npc]
        w_refs = refs[3 + npc:3 + npc + nw]
        dh_ref, dg_ref = refs[3 + npc + nw:]
        i = pl.program_id(0)

        @pl.when(i == 0)
        def _():
            dg_ref[...] = jnp.zeros_like(dg_ref)

        def matmuls(rows):
            dn = None
            off = 0
            for k in range(npc):
                if nw == npc:
                    w = w_refs[k][...]
                    w = w.reshape(w.shape[-2], w.shape[-1])
                else:
                    w = w_refs[0][off:off + dp_pieces[k].shape[1], :]
                    off += dp_pieces[k].shape[1]
                t = _dot(dp_refs[k][rows, :], w) if wt else _dot_nt(dp_refs[k][rows, :], w)
                dn = t if dn is None else dn + t
            return dn

        hb = tm // PIPE_ROW_BLOCKS
        blocks = [pl.ds(k * hb, hb) for k in range(PIPE_ROW_BLOCKS)]
        dn = matmuls(blocks[0])
        dg = None
        for k, rows in enumerate(blocks):
            dn_next = matmuls(blocks[k + 1]) if k + 1 < PIPE_ROW_BLOCKS else None
            rstd, xhat = _rms_stats(h_ref[rows, :])
            dh, dgk = _rms_bwd(dn, g_ref[...], rstd, xhat)
            dh_ref[rows, :] = dres_ref[rows, :] + dh
            dg = dgk if dg is None else dg + dgk
            dn = dn_next
        dg_ref[...] += dg

    in_specs = [pl.BlockSpec((tm, d), lambda i: (i, 0)), pl.BlockSpec((tm, d), lambda i: (i, 0)),
                pl.BlockSpec((1, d), lambda i: (0, 0))]
    for dp in dp_pieces:
        in_specs.append(pl.BlockSpec((tm, dp.shape[1]), lambda i: (i, 0)))
    for _, bs, idx in w_pieces:
        in_specs.append(pl.BlockSpec(bs, functools.partial(lambda i, idx: idx, idx=idx)))
    return pl.pallas_call(
        body, name="proj_bwd_act",
        grid=(tp // tm,),
        in_specs=in_specs,
        out_specs=[pl.BlockSpec((tm, d), lambda i: (i, 0)), pl.BlockSpec((1, d), lambda i: (0, 0))],
        out_shape=[jax.ShapeDtypeStruct((tp, d), F32), jax.ShapeDtypeStruct((1, d), F32)],
        compiler_params=_cparams(1),
    )(dres, h, norm, *dp_pieces, *[w for w, _, _ in w_pieces])


def _proj_bwd_w(n, dp_pieces, tiles, wt=False):
    tp, d = n.shape
    tm, nt, tail = tiles
    npc = len(dp_pieces)
    widths = [dp.shape[1] for dp in dp_pieces]
    oshape = (lambda w: (w, d)) if wt else (lambda w: (d, w))
    nin = 1 + npc
    ntail = nin if tail else 0

    def body(*refs):
        o_refs = refs[nin + ntail:nin + ntail + npc]
        accs = refs[nin + ntail + npc:]
        i = pl.program_id(0)

        @pl.when(i == 0)
        def _():
            for acc in accs:
                acc[...] = jnp.zeros_like(acc)

        def accumulate(n_ref, dp_refs):
            nn = n_ref[...]
            for k in range(npc):
                accs[k][...] += _dot_tn(dp_refs[k][...], nn) if wt else _dot_tn(nn, dp_refs[k][...])

        accumulate(refs[0], refs[1:nin])

        @pl.when(i == nt - 1)
        def _():
            if tail:
                accumulate(refs[nin], refs[nin + 1:nin + ntail])
            for k in range(npc):
                o_refs[k][...] = accs[k][...].astype(BF16)

    tail_specs = []
    if tail:
        tb = nt * tm // tail
        tail_specs = [pl.BlockSpec((tail, d), lambda i: (tb, 0))] + [pl.BlockSpec((tail, w), lambda i: (tb, 0))
                                                                    for w in widths]
    return pl.pallas_call(
        body, name="proj_bwd_w",
        grid=(nt,),
        in_specs=[pl.BlockSpec((tm, d), lambda i: (i, 0))]
        + [pl.BlockSpec((tm, w), lambda i: (i, 0)) for w in widths] + tail_specs,
        out_specs=[pl.BlockSpec(oshape(w), lambda i: (0, 0)) for w in widths],
        out_shape=[jax.ShapeDtypeStruct(oshape(w), BF16) for w in widths],
        scratch_shapes=[pltpu.VMEM(oshape(w), F32) for w in widths],
        compiler_params=_cparams(1),
    )(n, *dp_pieces, *((n, *dp_pieces) if tail else ()))


def _out_fwd(h, ya, yb, w, tm):
    tp, d = h.shape
    na, nb = ya.shape[1], yb.shape[1]

    def body(h_ref, ya_ref, yb_ref, w_ref, o_ref):
        y = _dot(ya_ref[...].astype(BF16), w_ref[0:na, :]) + _dot(yb_ref[...].astype(BF16), w_ref[na:, :])
        o_ref[...] = h_ref[...] + y

    return pl.pallas_call(
        body, name="out_fwd",
        grid=(tp // tm,),
        in_specs=[pl.BlockSpec((tm, d), lambda i: (i, 0)), pl.BlockSpec((tm, na), lambda i: (i, 0)),
                  pl.BlockSpec((tm, nb), lambda i: (i, 0)), pl.BlockSpec((d, d), lambda i: (0, 0))],
        out_specs=pl.BlockSpec((tm, d), lambda i: (i, 0)),
        out_shape=jax.ShapeDtypeStruct((tp, d), F32),
        compiler_params=_cparams(1),
    )(h, ya, yb, w)


def _out_bwd(dy, ya, yb, w, tm, after=()):
    tp, d = dy.shape
    na, nb = ya.shape[1], yb.shape[1]

    def body(dy_ref, ya_ref, yb_ref, w_ref, *rest):
        da_ref, db_ref, dw_ref, acc = rest[len(after):]
        i = pl.program_id(0)

        @pl.when(i == 0)
        def _():
            acc[...] = jnp.zeros_like(acc)

        dyb16 = dy_ref[...].astype(BF16)
        da_ref[...] = _dot_nt(dyb16, w_ref[0:na, :])
        db_ref[...] = _dot_nt(dyb16, w_ref[na:, :])
        acc[0:na, :] += _dot_tn(ya_ref[...].astype(BF16), dyb16)
        acc[na:, :] += _dot_tn(yb_ref[...].astype(BF16), dyb16)

        @pl.when(i == pl.num_programs(0) - 1)
        def _():
            dw_ref[...] = acc[...].astype(BF16)

    return pl.pallas_call(
        body, name="out_bwd",
        grid=(tp // tm,),
        in_specs=[pl.BlockSpec((tm, d), lambda i: (i, 0)), pl.BlockSpec((tm, na), lambda i: (i, 0)),
                  pl.BlockSpec((tm, nb), lambda i: (i, 0)), pl.BlockSpec((d, d), lambda i: (0, 0))]
        + [pl.BlockSpec(memory_space=pl.ANY)] * len(after),
        out_specs=[pl.BlockSpec((tm, na), lambda i: (i, 0)), pl.BlockSpec((tm, nb), lambda i: (i, 0)),
                   pl.BlockSpec((d, d), lambda i: (0, 0))],
        out_shape=[jax.ShapeDtypeStruct((tp, na), F32), jax.ShapeDtypeStruct((tp, nb), F32),
                   jax.ShapeDtypeStruct((d, d), BF16)],
        scratch_shapes=[pltpu.VMEM((d, d), F32)],
        compiler_params=_cparams(1),
    )(dy, ya, yb, w, *after)


def _loss_bwd(h, gfin, tgt, t_real, tm):
    tp, d = h.shape

    def body(h_ref, g_ref, t_ref, loss_ref, dh_ref, dg_ref):
        i = pl.program_id(0)

        @pl.when(i == 0)
        def _():
            loss_ref[...] = jnp.zeros_like(loss_ref)
            dg_ref[...] = jnp.zeros_like(dg_ref)

        rows = i * tm + lax.broadcasted_iota(jnp.int32, (tm, 1), 0)
        valid = (rows >= N_META) & (rows < t_real)
        rstd, xhat = _rms_stats(h_ref[...])
        g = g_ref[...]
        err = jnp.where(valid, xhat * g - t_ref[...], 0.0)
        e2 = jnp.sum(err * err, axis=1, keepdims=True)
        loss_ref[...] += (0.5 / d) * jnp.sum(e2, axis=0, keepdims=True)
        dy = err * (1.0 / d)
        dh, dg = _rms_bwd(dy, g, rstd, xhat)
        dh_ref[...] = dh
        dg_ref[...] += dg

    return pl.pallas_call(
        body, name="loss_bwd",
        grid=(tp // tm,),
        in_specs=[pl.BlockSpec((tm, d), lambda i: (i, 0)), pl.BlockSpec((1, d), lambda i: (0, 0)),
                  pl.BlockSpec((tm, d), lambda i: (i, 0))],
        out_specs=[pl.BlockSpec((1, 1), lambda i: (0, 0)), pl.BlockSpec((tm, d), lambda i: (i, 0)),
                   pl.BlockSpec((1, d), lambda i: (0, 0))],
        out_shape=[jax.ShapeDtypeStruct((1, 1), F32), jax.ShapeDtypeStruct((tp, d), F32),
                   jax.ShapeDtypeStruct((1, d), F32)],
        compiler_params=_cparams(1),
    )(h, gfin, tgt)


POOL_HALO = 16


def _pool_lane_consts(n_rows):
    lane = lax.broadcasted_iota(jnp.int32, (n_rows, D_POOL), 1)
    grp = lane // POOL_GROUP
    win = jnp.where(grp == 0, 2.0, jnp.where(grp == 1, 4.0, jnp.where(grp == 2, 8.0, 16.0)))
    return grp, win


def _pool_select(grp, s2, s4, s8, s16):
    return jnp.where(grp == 0, s2, jnp.where(grp == 1, s4, jnp.where(grp == 2, s8, s16)))


def _pool_mixed(x, row0, tr):
    n = tr + POOL_HALO
    s2 = x + pltpu.roll(x, 1, 0)
    s4 = s2 + pltpu.roll(s2, 2, 0)
    s8 = s4 + pltpu.roll(s4, 4, 0)
    s16 = s8 + pltpu.roll(s8, 8, 0)
    grp, win = _pool_lane_consts(n)
    rows = row0 - POOL_HALO + lax.broadcasted_iota(jnp.int32, (n, D_POOL), 0)
    cnt = jnp.minimum((rows + 1).astype(F32), win)
    pooled = _pool_select(grp, s2, s4, s8, s16) / jnp.maximum(cnt, 1.0)
    return (pooled - x)[POOL_HALO:, :]


def _pool_fwd(p, wbd, scale, tr):
    tp = p.shape[0]
    nt = tp // tr

    def body(p_ref, w_ref, s_ref, y_ref, usc):
        usc[0:POOL_HALO, :] = jnp.zeros((POOL_HALO, D_POOL), F32)
        usc[POOL_HALO:, :] = p_ref[...]

        def tile(r, carry):
            r0 = pl.multiple_of(r * tr, SUBLANE)
            x = usc[pl.ds(r0, tr + POOL_HALO), :]
            mixed = _pool_mixed(x, r0, tr)
            y_ref[pl.ds(r0, tr), :] = (_dot(mixed.astype(BF16), w_ref[...]) * s_ref[...]).astype(BF16)
            return carry

        lax.fori_loop(0, nt, tile, 0)

    return pl.pallas_call(
        body, name="pool_fwd",
        grid=(1,),
        in_specs=[pl.BlockSpec((tp, D_POOL), lambda i: (0, 0)), pl.BlockSpec((D_POOL, D_POOL), lambda i: (0, 0)),
                  pl.BlockSpec((1, D_POOL), lambda i: (0, 0))],
        out_specs=pl.BlockSpec((tp, D_POOL), lambda i: (0, 0)),
        out_shape=jax.ShapeDtypeStruct((tp, D_POOL), BF16),
        scratch_shapes=[pltpu.VMEM((tp + POOL_HALO, D_POOL), F32)],
        compiler_params=_cparams(1),
    )(p, wbd, scale)


def _pool_bwd(p, dya, wbd, scale, tr):
    tp = p.shape[0]
    nt = tp // tr

    def body(p_ref, dy_ref, w_ref, s_ref, du_ref, dw_ref, ds_ref, usc, gsc):
        usc[0:POOL_HALO, :] = jnp.zeros((POOL_HALO, D_POOL), F32)
        usc[POOL_HALO:, :] = p_ref[...]
        gsc[tp:, :] = jnp.zeros((POOL_HALO, D_POOL), F32)
        dw_ref[...] = jnp.zeros_like(dw_ref)
        ds_ref[...] = jnp.zeros_like(ds_ref)
        grp, win = _pool_lane_consts(tr)

        def tile1(r, carry):
            r0 = pl.multiple_of(r * tr, SUBLANE)
            x = usc[pl.ds(r0, tr + POOL_HALO), :]
            mixed = _pool_mixed(x, r0, tr).astype(BF16)
            dy = dy_ref[pl.ds(r0, tr), :]
            dys = (dy * s_ref[...]).astype(BF16)
            ypre = _dot(mixed, w_ref[...])
            ds_ref[...] += _colsum(dy * ypre)
            dw_ref[...] += _dot_tn(mixed, dys)
            dmx = _dot_nt(dys, w_ref[...])
            rows = r0 + lax.broadcasted_iota(jnp.int32, (tr, D_POOL), 0)
            cnt = jnp.minimum((rows + 1).astype(F32), win)
            gsc[pl.ds(r0, tr), :] = dmx / cnt
            return carry

        lax.fori_loop(0, nt, tile1, 0)
        n = tr + POOL_HALO
        grp2, win2 = _pool_lane_consts(n)

        def tile2(r, carry):
            r0 = pl.multiple_of(r * tr, SUBLANE)
            g = gsc[pl.ds(r0, n), :]
            s2 = g + pltpu.roll(g, n - 1, 0)
            s4 = s2 + pltpu.roll(s2, n - 2, 0)
            s8 = s4 + pltpu.roll(s4, n - 4, 0)
            s16 = s8 + pltpu.roll(s8, n - 8, 0)
            pooled_t = _pool_select(grp2, s2, s4, s8, s16)
            rows = r0 + lax.broadcasted_iota(jnp.int32, (n, D_POOL), 0)
            cnt = jnp.minimum((rows + 1).astype(F32), win2)
            du = pooled_t - g * cnt
            du_ref[pl.ds(r0, tr), :] = du[0:tr, :].astype(BF16)
            return carry

        lax.fori_loop(0, nt, tile2, 0)

    return pl.pallas_call(
        body, name="pool_bwd",
        grid=(1,),
        in_specs=[pl.BlockSpec((tp, D_POOL), lambda i: (0, 0)), pl.BlockSpec((tp, D_POOL), lambda i: (0, 0)),
                  pl.BlockSpec((D_POOL, D_POOL), lambda i: (0, 0)), pl.BlockSpec((1, D_POOL), lambda i: (0, 0))],
        out_specs=[pl.BlockSpec((tp, D_POOL), lambda i: (0, 0)), pl.BlockSpec((D_POOL, D_POOL), lambda i: (0, 0)),
                   pl.BlockSpec((1, D_POOL), lambda i: (0, 0))],
        out_shape=[jax.ShapeDtypeStruct((tp, D_POOL), BF16), jax.ShapeDtypeStruct((D_POOL, D_POOL), F32),
                   jax.ShapeDtypeStruct((1, D_POOL), F32)],
        scratch_shapes=[pltpu.VMEM((tp + POOL_HALO, D_POOL), F32), pltpu.VMEM((tp + POOL_HALO, D_POOL), F32)],
        compiler_params=_cparams(1),
    )(p, dya, wbd, scale)


def _hgrn_levels(ch):
    levels = []
    w = ch // 2
    while w >= 1:
        levels.append(w)
        w //= 2
    return levels


def _hgrn_consts(ch):
    t = np.arange(ch)
    tril = t[None, :] <= t[:, None]
    masks = []
    for w in _hgrn_levels(ch):
        blk = t // (2 * w)
        upper = t % (2 * w) >= w
        masks.append(upper[:, None] & (~upper)[None, :] & (blk[:, None] == blk[None, :]))
    masks.append(tril)
    msk = np.stack(masks).astype(np.float32)
    return jnp.asarray(tril.astype(np.float32), BF16), jnp.asarray(msk, F32), len(masks) - 1


def _split3(x):
    hi = x.astype(BF16)
    r1 = x - hi.astype(F32)
    mid = r1.astype(BF16)
    lo = (r1 - mid.astype(F32)).astype(BF16)
    return hi, mid, lo


def _hgrn_exponents(tril, logf):
    ch = logf.shape[0]
    hi, mid, lo = _split3(logf)
    x = _dot(tril, jnp.concatenate([hi, mid, lo], axis=1))
    b = x[:, 0:HEAD] + x[:, HEAD:2 * HEAD] + x[:, 2 * HEAD:3 * HEAD]
    rows = lax.broadcasted_iota(jnp.int32, (ch, HEAD), 0)
    fx = jnp.broadcast_to(b[ch - 1:ch, :], (ch, HEAD)) - b
    lev = []
    for w in _hgrn_levels(ch):
        pos = rows % (2 * w)
        upper = pos >= w
        if w >= SUBLANE:
            parts = [jnp.broadcast_to(b[k * 2 * w + w - 1:k * 2 * w + w, :], (2 * w, HEAD))
                     for k in range(ch // (2 * w))]
            bmid = parts[0] if len(parts) == 1 else jnp.concatenate(parts, axis=0)
            dx = jnp.where(upper, b - bmid, 0.0)
            ex = jnp.where(upper, 0.0, bmid - b)
        else:
            dx = logf
            ex = jnp.zeros_like(logf)
            for i in range(1, w):
                dx = dx + jnp.where(pos >= w + i, pltpu.roll(logf, i, 0), 0.0)
                ex = ex + jnp.where(pos <= w - 1 - i, pltpu.roll(logf, ch - i, 0), 0.0)
            dx = jnp.where(upper, dx, 0.0)
        lev.append((dx, ex))
    return b, fx, lev


def _hgrn_exponents_bwd(tril, d_b, d_fx, d_blast, lev_grads):
    ch = d_b.shape[0]
    rows = lax.broadcasted_iota(jnp.int32, (ch, HEAD), 0)
    db = d_b - d_fx
    dlf = jnp.zeros_like(d_b)
    for w, (ddx, dex) in zip(_hgrn_levels(ch), lev_grads):
        pos = rows % (2 * w)
        upper = pos >= w
        gu = jnp.where(upper, ddx, 0.0)
        if w >= SUBLANE:
            gl = jnp.where(upper, 0.0, dex)
            db = db + gu - gl
            diff = gl - gu
            for k in range(ch // (2 * w)):
                s = _colsum(diff[k * 2 * w:(k + 1) * 2 * w, :])
                db = db + jnp.where(rows == k * 2 * w + w - 1, s, 0.0)
        else:
            dlf = dlf + gu
            for i in range(1, w):
                dlf = dlf + pltpu.roll(jnp.where(pos >= w + i, gu, 0.0), ch - i, 0)
                dlf = dlf + pltpu.roll(jnp.where(pos <= w - 1 - i, dex, 0.0), i, 0)
    db = db + jnp.where(rows == ch - 1, _colsum(d_fx) + d_blast, 0.0)
    hi = db.astype(BF16)
    lo = (db - hi.astype(F32)).astype(BF16)
    d2 = _dot_tn(tril, jnp.concatenate([hi, lo], axis=1))
    return d2[:, 0:HEAD] + d2[:, HEAD:2 * HEAD] + dlf


def _lockstep(gens):
    results = [None] * len(gens)
    live = list(range(len(gens)))
    while live:
        for i in list(live):
            try:
                next(gens[i])
            except StopIteration as stop:
                results[i] = stop.value
                live.remove(i)
    return results


def _hgrn_gates(q_raw, z, lb):
    sz = _sigmoid(z)
    f = lb + (1.0 - lb) * sz
    q = q_raw * _sigmoid(q_raw)
    k = (1.0 - lb) * (1.0 - sz)
    return q, k, f, sz


def _hgrn_intra(q, k, lev, msk_ref, n_lev, ch):
    eye = (lax.broadcasted_iota(jnp.int32, (ch, ch), 0) == lax.broadcasted_iota(jnp.int32, (ch, ch), 1))
    a = jnp.where(eye, jnp.sum(q * k, axis=1, keepdims=True), 0.0)
    ops = []
    for lv in range(n_lev):
        eq = jnp.exp(lev[lv][0])
        ek = jnp.exp(lev[lv][1])
        qd = q * eq
        kd = k * ek
        a = a + msk_ref[lv] * _dot_nt(qd.astype(BF16), kd.astype(BF16))
        ops.append((eq, ek, qd, kd))
        yield
    return a, ops


def _hgrn_fwd(p, lb_logits, gnorm, mst, msk, n_lev, tm):
    tp = p.shape[0]
    ch = HG_CHUNK
    nct = tm // ch
    nt = tp // tm
    nr = mst.shape[0]
    base = D_POOL // HEAD

    hp = HG_HEADS_PER_STEP
    wide = hp * HEAD
    npr = wide // HG_PBLOCK

    def body(*refs):
        p_refs = refs[:4 * npr]
        lg_ref, gn_ref, mst_ref, msk_ref, y_ref, ss_ref, st_sc = refs[4 * npr:]

        @pl.when(pl.program_id(1) == 0)
        def _():
            st_sc[...] = jnp.zeros_like(st_sc)

        lb_all = _sigmoid(lg_ref[0:1, :] - lg_ref[1:2, :])

        def raw(seg, hh, r0):
            per = HG_PBLOCK // HEAD
            return p_refs[seg * npr + hh // per][pl.ds(r0, ch), (hh % per) * HEAD:(hh % per + 1) * HEAD]

        def one_head(hh, c, r0):
            ls = slice(hh * HEAD, (hh + 1) * HEAD)
            q_raw, z, v, g_raw, st = raw(0, hh, r0), raw(1, hh, r0), raw(2, hh, r0), raw(3, hh, r0), st_sc[hh]
            q, k, f, _ = _hgrn_gates(q_raw, z, lb_all[:, ls])
            yield
            b, fx, lev = _hgrn_exponents(mst_ref[...], jnp.log(f))
            yield
            qe = q * jnp.exp(b)
            a, _ = yield from _hgrn_intra(q, k, lev, msk_ref, n_lev, ch)
            v16 = v.astype(BF16)
            o = _dot_nt(qe.astype(BF16), st.astype(BF16)) + _dot(a.astype(BF16), v16)
            kl = k * jnp.exp(fx)
            st_new = st * jnp.exp(b[ch - 1:ch, :]) + _dot_tn(v16, kl.astype(BF16))
            yield
            rstd = lax.rsqrt(jnp.mean(o * o, axis=-1, keepdims=True) + EPS)
            return st, st_new, o * rstd * gn_ref[...] * (g_raw * _sigmoid(g_raw))

        def chunk(c, carry):
            r0 = pl.multiple_of(c * ch, ch)
            results = _lockstep([one_head(hh, c, r0) for hh in range(hp)])
            for hh, (st, st_new, y) in enumerate(results):
                ss_ref[hh, c] = st
                st_sc[hh] = st_new
                y_ref[pl.ds(r0, ch), hh * HEAD:(hh + 1) * HEAD] = y.astype(BF16)
            return carry

        lax.fori_loop(0, nct, chunk, 0)

    def pspec(seg, part):
        return pl.BlockSpec((tm, HG_PBLOCK),
                            lambda h, i: (i, (base + seg * HG_HEADS) * HEAD // HG_PBLOCK + h * npr + part))

    return pl.pallas_call(
        body, name="hgrn_fwd",
        grid=(HG_HEADS // hp, nt),
        in_specs=[pspec(seg, part) for seg in range(4) for part in range(npr)]
        + [pl.BlockSpec((2, wide), lambda h, i: (0, h)),
           pl.BlockSpec((1, HEAD), lambda h, i: (0, 0)),
           pl.BlockSpec((nr, ch), lambda h, i: (0, 0)),
           pl.BlockSpec((n_lev + 1, ch, ch), lambda h, i: (0, 0, 0))],
        out_specs=[pl.BlockSpec((tm, wide), lambda h, i: (i, h)),
                   pl.BlockSpec((hp, nct, HEAD, HEAD), lambda h, i: (h, i, 0, 0))],
        out_shape=[jax.ShapeDtypeStruct((tp, D_HGRN), BF16),
                   jax.ShapeDtypeStruct((HG_HEADS, tp // ch, HEAD, HEAD), F32)],
        scratch_shapes=[pltpu.VMEM((hp, HEAD, HEAD), F32)],
        compiler_params=_cparams(2),
    )(*([p] * (4 * npr)), lb_logits, gnorm, mst, msk)


def _hgrn_bwd(p, dyb, states, lb_logits, gnorm, mst, msk, n_lev, tm):
    tp = p.shape[0]
    ch = HG_CHUNK
    nct = tm // ch
    nt = tp // tm
    nr = mst.shape[0]
    base = D_POOL // HEAD

    hp = HG_HEADS_PER_STEP
    wide = hp * HEAD
    npr = wide // HG_PBLOCK

    def body(*refs):
        p_refs = refs[:4 * npr]
        (dy_ref, ss_ref, lg_ref, gn_ref, mst_ref, msk_ref,
         dq_ref, dz_ref, dv_ref, dg_ref, dlg_ref, dgn_ref, dst_sc, dlb_sc) = refs[4 * npr:]
        ti = pl.program_id(1)

        def raw(seg, hh, r0):
            per = HG_PBLOCK // HEAD
            return p_refs[seg * npr + hh // per][pl.ds(r0, ch), (hh % per) * HEAD:(hh % per + 1) * HEAD]

        @pl.when(ti == 0)
        def _():
            dst_sc[...] = jnp.zeros_like(dst_sc)
            dlb_sc[...] = jnp.zeros_like(dlb_sc)
            dgn_ref[...] = jnp.zeros_like(dgn_ref)

        lb_all = _sigmoid(lg_ref[0:1, :] - lg_ref[1:2, :])
        gn = gn_ref[...]

        def load_head(hh, c, r0):
            ls = slice(hh * HEAD, (hh + 1) * HEAD)
            return (raw(0, hh, r0), raw(1, hh, r0), raw(2, hh, r0), raw(3, hh, r0),
                    dy_ref[pl.ds(r0, ch), ls], ss_ref[hh, c], dst_sc[hh])

        def store_head(hh, r0, res):
            ls = slice(hh * HEAD, (hh + 1) * HEAD)
            dq_raw, dz, dv, dg_raw, dgn, dst_new, dlb = res
            dq_ref[pl.ds(r0, ch), ls] = dq_raw
            dz_ref[pl.ds(r0, ch), ls] = dz
            dv_ref[pl.ds(r0, ch), ls] = dv
            dg_ref[pl.ds(r0, ch), ls] = dg_raw
            dgn_ref[hh] += dgn
            dst_sc[hh] = dst_new
            dlb_sc[:, ls] += dlb

        def one_head(hh, loaded):
            ls = slice(hh * HEAD, (hh + 1) * HEAD)
            lb = lb_all[:, ls]
            q_raw, z, v, g_raw, dy, st, dst = loaded
            q, k, f, sz = _hgrn_gates(q_raw, z, lb)
            yield
            b, fx, lev = _hgrn_exponents(mst_ref[...], jnp.log(f))
            yield
            eb = jnp.exp(b)
            ef = jnp.exp(fx)
            elast = jnp.exp(b[ch - 1:ch, :])
            qe = q * eb
            kl = k * ef
            a, ops = yield from _hgrn_intra(q, k, lev, msk_ref, n_lev, ch)
            v16 = v.astype(BF16)
            st16 = st.astype(BF16)
            qe16 = qe.astype(BF16)
            kl16 = kl.astype(BF16)
            a16 = a.astype(BF16)
            o = _dot_nt(qe16, st16) + _dot(a16, v16)
            yield
            sg = _sigmoid(g_raw)
            rstd = lax.rsqrt(jnp.mean(o * o, axis=-1, keepdims=True) + EPS)
            oh = o * rstd
            dg_out = (dy * oh * gn * (sg * (1.0 + g_raw * (1.0 - sg)))).astype(BF16)
            don = dy * (g_raw * sg)
            dgn = _colsum(don * oh)
            doh = don * gn
            do = rstd * (doh - oh * jnp.mean(doh * oh, axis=-1, keepdims=True))
            do16 = do.astype(BF16)
            dst16 = dst.astype(BF16)
            yield
            dv = _dot_tn(a16, do16) + _dot_nt(kl16, dst16)
            da = msk_ref[n_lev] * _dot_nt(do16, v16)
            dqe = _dot(do16, st16)
            dkl = _dot(v16, dst16)
            dst_new = dst * elast + _dot_tn(do16, qe16)
            yield
            db_last = _colsum(dst * st) * elast
            dad = jnp.sum(do * v, axis=1, keepdims=True)
            dq = dad * k + dqe * eb
            dk = dad * q + dkl * ef
            lev_grads = []
            for lv in range(n_lev):
                eq, ek, qd, kd = ops[lv]
                gl = (msk_ref[lv] * da).astype(BF16)
                dqd = _dot(gl, kd.astype(BF16))
                dkd = _dot_tn(gl, qd.astype(BF16))
                dq = dq + dqd * eq
                dk = dk + dkd * ek
                lev_grads.append((dqd * qd, dkd * kd))
                yield
            dlogf = _hgrn_exponents_bwd(mst_ref[...], dqe * qe, dkl * kl, db_last, lev_grads)
            yield
            sq = _sigmoid(q_raw)
            dq_out = (dq * (sq * (1.0 + q_raw * (1.0 - sq)))).astype(BF16)
            dfk = dlogf / f - dk
            dz_out = (dfk * (1.0 - lb) * sz * (1.0 - sz)).astype(BF16)
            return dq_out, dz_out, dv.astype(BF16), dg_out, dgn, dst_new, _colsum(dfk * (1.0 - sz))

        def chunk(cc, carry):
            c = nct - 1 - cc
            r0 = pl.multiple_of(c * ch, ch)
            loaded = [load_head(hh, c, r0) for hh in range(HG_HEADS_PER_STEP)]
            results = _lockstep([one_head(hh, loaded[hh]) for hh in range(HG_HEADS_PER_STEP)])
            for hh in range(HG_HEADS_PER_STEP):
                store_head(hh, r0, results[hh])
            return carry

        lax.fori_loop(0, nct, chunk, 0, unroll=1)

        @pl.when(ti == nt - 1)
        def _():
            dl0 = dlb_sc[...] * lb_all * (1.0 - lb_all)
            dlg_ref[0:1, :] = dl0
            dlg_ref[1:2, :] = -dl0

    def pspec(seg, part):
        return pl.BlockSpec((tm, HG_PBLOCK), lambda h, i: (
            nt - 1 - i, (base + seg * HG_HEADS) * HEAD // HG_PBLOCK + h * npr + part))

    ospec = pl.BlockSpec((tm, wide), lambda h, i: (nt - 1 - i, h))
    return pl.pallas_call(
        body, name="hgrn_bwd",
        grid=(HG_HEADS // hp, nt),
        in_specs=[pspec(seg, part) for seg in range(4) for part in range(npr)]
        + [ospec, pl.BlockSpec((hp, nct, HEAD, HEAD), lambda h, i: (h, nt - 1 - i, 0, 0)),
           pl.BlockSpec((2, wide), lambda h, i: (0, h)),
           pl.BlockSpec((1, HEAD), lambda h, i: (0, 0)),
           pl.BlockSpec((nr, ch), lambda h, i: (0, 0)),
           pl.BlockSpec((n_lev + 1, ch, ch), lambda h, i: (0, 0, 0))],
        out_specs=[ospec, ospec, ospec, ospec,
                   pl.BlockSpec((2, wide), lambda h, i: (0, h)),
                   pl.BlockSpec((hp, 1, HEAD), lambda h, i: (h, 0, 0))],
        out_shape=[jax.ShapeDtypeStruct((tp, D_HGRN), BF16)] * 4
        + [jax.ShapeDtypeStruct((2, D_HGRN), F32), jax.ShapeDtypeStruct((HG_HEADS, 1, HEAD), F32)],
        scratch_shapes=[pltpu.VMEM((hp, HEAD, HEAD), F32), pltpu.VMEM((1, wide), F32)],
        compiler_params=_cparams(2),
    )(*([p] * (4 * npr)), dyb, states, lb_logits, gnorm, mst, msk)


def _tap_views(x, tr, halo, width):
    subs = {0: x}
    views = []
    for j in range(width):
        tiles, rem = divmod(width - 1 - j, SUBLANE)
        if rem not in subs:
            subs[rem] = pltpu.roll(x, rem, 0)
        start = halo - tiles * SUBLANE
        views.append(subs[rem][start:start + tr, :])
    return views


def _tap_views_t(y, tr, halo, width):
    n = tr + halo
    subs = {0: y}
    views = []
    for j in range(width):
        tiles, rem = divmod(width - 1 - j, SUBLANE)
        if rem not in subs:
            subs[rem] = pltpu.roll(y, n - rem, 0)
        views.append(subs[rem][tiles * SUBLANE:tiles * SUBLANE + tr, :])
    return views


def _weighted_sum(views, w_ref):
    acc = None
    for j, view in enumerate(views):
        term = view * w_ref[j:j + 1, :]
        acc = term if acc is None else acc + term
    return acc


def _conv_taps(x, w_ref, tr, halo, width):
    return _weighted_sum(_tap_views(x, tr, halo, width), w_ref)


def _conv_taps_t(y, w_ref, tr, halo, width):
    return _weighted_sum(_tap_views_t(y, tr, halo, width), w_ref)


def _ln_stats(cv):
    mu = jnp.mean(cv, axis=-1, keepdims=True)
    xc = cv - mu
    rstd = lax.rsqrt(jnp.mean(xc * xc, axis=-1, keepdims=True) + EPS)
    return rstd, xc * rstd


def _convmod_fwd(p, w, bias, ln_g, ln_b, tr):
    tp = p.shape[0]
    nt = tp // tr
    nb = D_CONV // HEAD

    def body(a_ref, b_ref, w_ref, bi_ref, g_ref, be_ref, y_ref, cv_ref, usc):
        usc[0:CONV_HALO, :] = jnp.zeros((CONV_HALO, HEAD), F32)
        usc[CONV_HALO:, :] = a_ref[...] * _sigmoid(b_ref[...])

        def tile(r, carry):
            r0 = pl.multiple_of(r * tr, SUBLANE)
            x = usc[pl.ds(r0, tr + CONV_HALO), :]
            cv = _conv_taps(x, w_ref, tr, CONV_HALO, CONV_WIDTH) + bi_ref[...]
            cv_ref[pl.ds(r0, tr), :] = cv
            _, xh = _ln_stats(cv)
            un = xh * g_ref[...] + be_ref[...]
            y_ref[pl.ds(r0, tr), :] = (un * _sigmoid(un)).astype(BF16)
            return carry

        lax.fori_loop(0, nt, tile, 0)

    vec = lambda: pl.BlockSpec((1, HEAD), lambda j: (0, j))
    return pl.pallas_call(
        body, name="convmod_fwd",
        grid=(nb,),
        in_specs=[pl.BlockSpec((tp, HEAD), lambda j: (0, j)), pl.BlockSpec((tp, HEAD), lambda j: (0, nb + j)),
                  pl.BlockSpec((CONV_HALO, HEAD), lambda j: (0, j)), vec(), vec(), vec()],
        out_specs=[pl.BlockSpec((tp, HEAD), lambda j: (0, j))] * 2,
        out_shape=[jax.ShapeDtypeStruct((tp, D_CONV), BF16), jax.ShapeDtypeStruct((tp, D_CONV), F32)],
        scratch_shapes=[pltpu.VMEM((tp + CONV_HALO, HEAD), F32)],
        compiler_params=_cparams(1),
    )(p, p, w, bias, ln_g, ln_b)


def _convmod_bwd(p, cv_saved, dyc, w, ln_g, ln_b, tr):
    tp = p.shape[0]
    nt = tp // tr
    nb = D_CONV // HEAD

    def body(a_ref, b_ref, cv_ref, dy_ref, w_ref, g_ref, be_ref, da_ref, db_ref, dw_ref, dv_ref, usc, dsc):
        usc[0:CONV_HALO, :] = jnp.zeros((CONV_HALO, HEAD), F32)
        usc[CONV_HALO:, :] = a_ref[...] * _sigmoid(b_ref[...])
        dsc[tp:, :] = jnp.zeros((CONV_HALO, HEAD), F32)
        dw_ref[...] = jnp.zeros_like(dw_ref)
        dv_ref[...] = jnp.zeros_like(dv_ref)

        def tile1(r, carry):
            r0 = pl.multiple_of(r * tr, SUBLANE)
            x = usc[pl.ds(r0, tr + CONV_HALO), :]
            views = _tap_views(x, tr, CONV_HALO, CONV_WIDTH)
            rstd, xh = _ln_stats(cv_ref[pl.ds(r0, tr), :])
            un = xh * g_ref[...] + be_ref[...]
            sg = _sigmoid(un)
            dun = dy_ref[pl.ds(r0, tr), :] * (sg * (1.0 + un * (1.0 - sg)))
            dv_ref[0, 1:2, :] += _colsum(dun * xh)
            dv_ref[0, 2:3, :] += _colsum(dun)
            dxh = dun * g_ref[...]
            dcv = rstd * (dxh - jnp.mean(dxh, axis=-1, keepdims=True)
                          - xh * jnp.mean(dxh * xh, axis=-1, keepdims=True))
            dv_ref[0, 0:1, :] += _colsum(dcv)
            for j in range(CONV_WIDTH):
                dw_ref[0, j:j + 1, :] += _colsum(dcv * views[j])
            dsc[pl.ds(r0, tr), :] = dcv
            return carry

        lax.fori_loop(0, nt, tile1, 0)

        def tile2(r, carry):
            r0 = pl.multiple_of(r * tr, SUBLANE)
            y = dsc[pl.ds(r0, tr + CONV_HALO), :]
            du = _conv_taps_t(y, w_ref, tr, CONV_HALO, CONV_WIDTH)
            a = a_ref[pl.ds(r0, tr), :]
            sb = _sigmoid(b_ref[pl.ds(r0, tr), :])
            da_ref[pl.ds(r0, tr), :] = (du * sb).astype(BF16)
            db_ref[pl.ds(r0, tr), :] = (du * a * sb * (1.0 - sb)).astype(BF16)
            return carry

        lax.fori_loop(0, nt, tile2, 0)

    vec = lambda: pl.BlockSpec((1, HEAD), lambda j: (0, j))
    col = lambda: pl.BlockSpec((tp, HEAD), lambda j: (0, j))
    return pl.pallas_call(
        body, name="convmod_bwd",
        grid=(nb,),
        in_specs=[col(), pl.BlockSpec((tp, HEAD), lambda j: (0, nb + j)), col(), col(),
                  pl.BlockSpec((CONV_HALO, HEAD), lambda j: (0, j)), vec(), vec()],
        out_specs=[col(), col(), pl.BlockSpec((1, CONV_HALO, HEAD), lambda j: (j, 0, 0)),
                   pl.BlockSpec((1, SUBLANE, HEAD), lambda j: (j, 0, 0))],
        out_shape=[jax.ShapeDtypeStruct((tp, D_CONV), BF16), jax.ShapeDtypeStruct((tp, D_CONV), BF16),
                   jax.ShapeDtypeStruct((nb, CONV_HALO, HEAD), F32), jax.ShapeDtypeStruct((nb, SUBLANE, HEAD), F32)],
        scratch_shapes=[pltpu.VMEM((tp + CONV_HALO, HEAD), F32), pltpu.VMEM((tp + CONV_HALO, HEAD), F32)],
        compiler_params=_cparams(1),
    )(p, p, cv_saved, dyc, w, ln_g, ln_b)


def _log1p_small(y):
    return jnp.where(y < 1e-4, y * (1.0 - 0.5 * y), jnp.log(1.0 + y))


def _softplus(x):
    return jnp.maximum(x, 0.0) + _log1p_small(jnp.exp(-jnp.abs(x)))


def _expm1(x):
    return jnp.where(jnp.abs(x) < 1e-2, x * (1.0 + 0.5 * x * (1.0 + x * (1.0 / 3.0))), jnp.exp(x) - 1.0)


def _gelu_parts(x):
    c = 0.7978845608028654
    inner = c * (x + 0.044715 * x * x * x)
    th = jnp.tanh(inner)
    gelu = 0.5 * x * (1.0 + th)
    dgelu = 0.5 * (1.0 + th) + 0.5 * x * (1.0 - th * th) * c * (1.0 + 3.0 * 0.044715 * x * x)
    return gelu, dgelu


def _lru_gates(x_all, tp, cw_ref, cb_ref, wa_ref, ba_ref, wx_ref, bx_ref, lam_ref):
    u = _conv_taps(x_all, cw_ref, tp, LRU_HALO, LRU_CONV) + cb_ref[...]
    u16 = u.astype(BF16)
    r = _sigmoid(_dot(u16, wa_ref[0]) + ba_ref[...])
    i = _sigmoid(_dot(u16, wx_ref[0]) + bx_ref[...])
    sp = _softplus(-lam_ref[...])
    la = -LRU_C * r * sp
    a = jnp.exp(la)
    mult = jnp.sqrt(-_expm1(2.0 * la))
    return u, r, i, a, mult, sp


def _lru_specs(tp, nb):
    col = lambda k: pl.BlockSpec((tp, HEAD), functools.partial(lambda j, k: (0, k * nb + j), k=k))
    vec = lambda: pl.BlockSpec((1, HEAD), lambda j: (0, j))
    mat = lambda: pl.BlockSpec((1, HEAD, HEAD), lambda j: (j, 0, 0))
    return col, vec, mat


def _lru_fwd(p, cw, cb, wa, ba, wx, bx, lam):
    tp = p.shape[0]
    nb = D_LRU // HEAD
    ng = tp // SUBLANE

    def body(x_ref, gt_ref, cw_ref, cb_ref, wa_ref, ba_ref, wx_ref, bx_ref, lam_ref, y_ref, hs_ref,
             xsc, asc, bsc):
        xsc[0:LRU_HALO, :] = jnp.zeros((LRU_HALO, HEAD), F32)
        xsc[LRU_HALO:, :] = x_ref[...]
        u, r, i, a, mult, _ = _lru_gates(xsc[...], tp, cw_ref, cb_ref, wa_ref, ba_ref, wx_ref, bx_ref, lam_ref)
        rows = lax.broadcasted_iota(jnp.int32, (tp, HEAD), 0)
        b = jnp.where(rows == 0, 1.0, mult) * (i * u)
        sub = rows % SUBLANE
        for k in (1, 2, 4):
            m = sub >= k
            b = jnp.where(m, a * pltpu.roll(b, k, 0) + b, b)
            a = jnp.where(m, a * pltpu.roll(a, k, 0), a)
        asc[...] = a
        bsc[...] = b

        def grp(g, carry):
            r0 = pl.multiple_of(g * SUBLANE, SUBLANE)
            h = bsc[pl.ds(r0, SUBLANE), :] + asc[pl.ds(r0, SUBLANE), :] * carry
            hs_ref[pl.ds(r0, SUBLANE), :] = h
            return jnp.broadcast_to(h[SUBLANE - 1:SUBLANE, :], (SUBLANE, HEAD))

        lax.fori_loop(0, ng, grp, jnp.zeros((SUBLANE, HEAD), F32))
        gelu, _ = _gelu_parts(gt_ref[...])
        y_ref[...] = (gelu * hs_ref[...]).astype(BF16)

    col, vec, mat = _lru_specs(tp, nb)
    return pl.pallas_call(
        body, name="lru_fwd",
        grid=(nb,),
        in_specs=[col(2), col(3), pl.BlockSpec((LRU_CONV, HEAD), lambda j: (0, j)), vec(), mat(), vec(), mat(),
                  vec(), vec()],
        out_specs=[pl.BlockSpec((tp, HEAD), lambda j: (0, j)), pl.BlockSpec((tp, HEAD), lambda j: (0, j))],
        out_shape=[jax.ShapeDtypeStruct((tp, D_LRU), BF16), jax.ShapeDtypeStruct((tp, D_LRU), F32)],
        scratch_shapes=[pltpu.VMEM((tp + LRU_HALO, HEAD), F32), pltpu.VMEM((tp, HEAD), F32),
                        pltpu.VMEM((tp, HEAD), F32)],
        compiler_params=_cparams(1),
    )(p, p, cw, cb, wa, ba, wx, bx, lam)


def _lru_bwd(p, hs, dyd, cw, cb, wa, ba, wx, bx, lam):
    tp = p.shape[0]
    nb = D_LRU // HEAD
    ng = tp // SUBLANE

    def body(x_ref, gt_ref, hs_ref, dy_ref, cw_ref, cb_ref, wa_ref, ba_ref, wx_ref, bx_ref, lam_ref,
             dx_ref, dgt_ref, dwa_ref, dwx_ref, dv_ref, xsc, asc, bsc, gsc, dusc):
        xsc[0:LRU_HALO, :] = jnp.zeros((LRU_HALO, HEAD), F32)
        xsc[LRU_HALO:, :] = x_ref[...]
        x_all = xsc[...]
        u, r, i, a, mult, sp = _lru_gates(x_all, tp, cw_ref, cb_ref, wa_ref, ba_ref, wx_ref, bx_ref, lam_ref)
        rows = lax.broadcasted_iota(jnp.int32, (tp, HEAD), 0)
        hs = hs_ref[...]
        dy = dy_ref[...]
        gelu, dgelu = _gelu_parts(gt_ref[...])
        dgt_ref[...] = (dy * hs * dgelu).astype(BF16)
        bb = dy * gelu
        aa = jnp.where(rows == tp - 1, 0.0, pltpu.roll(a, tp - 1, 0))
        sub = rows % SUBLANE
        for k in (1, 2, 4):
            m = sub < SUBLANE - k
            bb = jnp.where(m, aa * pltpu.roll(bb, tp - k, 0) + bb, bb)
            aa = jnp.where(m, aa * pltpu.roll(aa, tp - k, 0), aa)
        asc[...] = aa
        bsc[...] = bb

        def grp(gi, carry):
            g = ng - 1 - gi
            r0 = pl.multiple_of(g * SUBLANE, SUBLANE)
            gg = bsc[pl.ds(r0, SUBLANE), :] + asc[pl.ds(r0, SUBLANE), :] * carry
            gsc[pl.ds(r0, SUBLANE), :] = gg
            return jnp.broadcast_to(gg[0:1, :], (SUBLANE, HEAD))

        lax.fori_loop(0, ng, grp, jnp.zeros((SUBLANE, HEAD), F32))
        g = gsc[...]
        first = rows == 0
        hprev = jnp.where(first, 0.0, pltpu.roll(hs, 1, 0))
        iu = i * u
        d_iu = g * jnp.where(first, 1.0, mult)
        dmult_term = jnp.where(first, 0.0, g * iu * (-(a * a) / mult))
        dla = g * hprev * a + dmult_term
        dr = dla * (-LRU_C) * sp
        dv_ref[0, 7:8, :] = _colsum(dla * (LRU_C * r) * _sigmoid(-lam_ref[...]))
        dpr = dr * r * (1.0 - r)
        dpi = d_iu * u * i * (1.0 - i)
        dv_ref[0, 5:6, :] = _colsum(dpr)
        dv_ref[0, 6:7, :] = _colsum(dpi)
        u16 = u.astype(BF16)
        dpr16 = dpr.astype(BF16)
        dpi16 = dpi.astype(BF16)
        dwa_ref[0] = _dot_tn(u16, dpr16)
        dwx_ref[0] = _dot_tn(u16, dpi16)
        du = d_iu * i + _dot_nt(dpr16, wa_ref[0]) + _dot_nt(dpi16, wx_ref[0])
        dv_ref[0, 4:5, :] = _colsum(du)
        for j in range(LRU_CONV):
            sh = LRU_CONV - 1 - j
            xs = x_all if sh == 0 else pltpu.roll(x_all, sh, 0)
            dv_ref[0, j:j + 1, :] = _colsum(du * xs[LRU_HALO:, :])
        dusc[0:tp, :] = du
        dusc[tp:, :] = jnp.zeros((LRU_HALO, HEAD), F32)
        dx_ref[...] = _conv_taps_t(dusc[...], cw_ref, tp, LRU_HALO, LRU_CONV).astype(BF16)

    col, vec, mat = _lru_specs(tp, nb)
    ocol = lambda: pl.BlockSpec((tp, HEAD), lambda j: (0, j))
    return pl.pallas_call(
        body, name="lru_bwd",
        grid=(nb,),
        in_specs=[col(2), col(3), ocol(), ocol(), pl.BlockSpec((LRU_CONV, HEAD), lambda j: (0, j)), vec(), mat(),
                  vec(), mat(), vec(), vec()],
        out_specs=[ocol(), ocol(), mat(), mat(), pl.BlockSpec((1, SUBLANE, HEAD), lambda j: (j, 0, 0))],
        out_shape=[jax.ShapeDtypeStruct((tp, D_LRU), BF16), jax.ShapeDtypeStruct((tp, D_LRU), BF16),
                   jax.ShapeDtypeStruct((nb, HEAD, HEAD), F32), jax.ShapeDtypeStruct((nb, HEAD, HEAD), F32),
                   jax.ShapeDtypeStruct((nb, SUBLANE, HEAD), F32)],
        scratch_shapes=[pltpu.VMEM((tp + LRU_HALO, HEAD), F32), pltpu.VMEM((tp, HEAD), F32),
                        pltpu.VMEM((tp, HEAD), F32), pltpu.VMEM((tp, HEAD), F32),
                        pltpu.VMEM((tp + LRU_HALO, HEAD), F32)],
        compiler_params=_cparams(1),
    )(p, p, hs, dyd, cw, cb, wa, ba, wx, bx, lam)


def _mesh_pos():
    return lax.axis_index("x"), lax.axis_index("y"), lax.axis_index("c")


def _other_chips(x, y):
    return [(1 - x, y), (x, 1 - y), (1 - x, 1 - y)]


ANY = pl.BlockSpec(memory_space=pl.ANY)


HBM = pl.BlockSpec(memory_space=pltpu.HBM)
SEM = pl.BlockSpec(memory_space=pltpu.SEMAPHORE)
DATAFLOW = pltpu.SideEffectType.DATAFLOW_SIDE_EFFECTING
N_PEERS = 4


def _in_hbm(a):
    return pltpu.with_memory_space_constraint(a, pltpu.HBM)


def _gather_peers(x, y, c):
    return [((ox, oy, c), 2 * ox + oy) for ox, oy in _other_chips(x, y)] + [((x, y, 1 - c), 2 * x + y)]


def _gather_refs(src, land, slot, c, split):
    if not split:
        return src, land.at[slot]
    half = src.shape[0] // 2
    return src.at[pl.ds(c * half, half)], land.at[slot, pl.ds(c * half, half)]


def _gather_start(arrs, split):
    n = len(arrs)

    def body(*refs):
        ins, lands = refs[:n], refs[n:2 * n]
        ssem, rsem = refs[2 * n:2 * n + 2]
        token = refs[-1]
        x, y, c = _mesh_pos()
        chip = 2 * x + y
        for k in range(n):
            for j, (dev, _) in enumerate(_gather_peers(x, y, c)):
                src, dst = _gather_refs(ins[k], lands[k], chip, c, split[k] and j < N_PEERS - 1)
                pltpu.make_async_remote_copy(
                    src_ref=src, dst_ref=dst, send_sem=ssem.at[N_PEERS * k + j],
                    recv_sem=rsem.at[N_PEERS * k + j], device_id=dev, device_id_type=MESH).start()
        token[...] = jnp.zeros_like(token)

    lands = [_in_hbm(lax.empty((N_SHARD,) + a.shape, a.dtype)) for a in arrs]
    out = pl.pallas_call(
        body, name="gather_start",
        in_specs=[HBM] * (2 * n),
        out_specs=[SEM, SEM] + [HBM] * (2 * n) + [pl.BlockSpec(memory_space=pltpu.VMEM)],
        out_shape=[pltpu.SemaphoreType.DMA((N_PEERS * n,)), pltpu.SemaphoreType.DMA((N_PEERS * n,))]
        + [pltpu.HBM(a.shape, a.dtype) for a in arrs]
        + [pltpu.HBM((N_SHARD,) + a.shape, a.dtype) for a in arrs]
        + [jax.ShapeDtypeStruct((SUBLANE, LANE), F32)],
        input_output_aliases={k: 2 + k for k in range(2 * n)},
        compiler_params=pltpu.CompilerParams(has_side_effects=DATAFLOW),
    )(*[_in_hbm(a) for a in arrs], *lands)
    return out[0], out[1], list(out[2:2 + n]), list(out[2 + n:2 + 2 * n]), out[-1]


def _gather_wait(ssem, rsem, srcs, lands, ks, after, split=False):
    n = len(ks)

    def body(*refs):
        ins, lnd = refs[:n], refs[n:2 * n]
        ssem_ref, rsem_ref = refs[2 * n:2 * n + 2]
        x, y, c = _mesh_pos()
        for i, k in enumerate(ks):
            for j, (dev, pchip) in enumerate(_gather_peers(x, y, c)):
                src, dst = _gather_refs(ins[i], lnd[i], pchip, c, split and j < N_PEERS - 1)
                cp = pltpu.make_async_remote_copy(
                    src_ref=src, dst_ref=dst, send_sem=ssem_ref.at[N_PEERS * k + j],
                    recv_sem=rsem_ref.at[N_PEERS * k + j], device_id=dev, device_id_type=MESH)
                cp.wait_send()
                cp.wait_recv()

    out = pl.pallas_call(
        body, name="gather_wait",
        in_specs=[HBM] * (2 * n) + [SEM, SEM] + [ANY] * len(after),
        out_specs=[HBM] * (2 * n),
        out_shape=[pltpu.HBM(a.shape, a.dtype) for a in srcs] + [pltpu.HBM(a.shape, a.dtype) for a in lands],
        input_output_aliases={k: k for k in range(2 * n)},
        compiler_params=pltpu.CompilerParams(has_side_effects=DATAFLOW),
    )(*srcs, *lands, ssem, rsem, *after)
    return list(out[n:])


def _pair_forward(lands):
    n = len(lands)

    def body(*refs):
        outs = refs[n:2 * n]
        ssem, rsem = refs[2 * n:]
        x, y, c = _mesh_pos()
        sibling = (x, y, 1 - c)
        cps = []
        for k in range(n):
            half = lands[k].shape[1] // 2
            for j, (ox, oy) in enumerate(_other_chips(x, y)):
                mine = outs[k].at[2 * ox + oy, pl.ds(c * half, half)]
                cp = pltpu.make_async_remote_copy(src_ref=mine, dst_ref=mine, send_sem=ssem.at[3 * k + j],
                                                  recv_sem=rsem.at[3 * k + j], device_id=sibling, device_id_type=MESH)
                cp.start()
                cps.append(cp)
        for k in range(n):
            half = lands[k].shape[1] // 2
            for j, (ox, oy) in enumerate(_other_chips(x, y)):
                theirs = outs[k].at[2 * ox + oy, pl.ds((1 - c) * half, half)]
                pltpu.make_async_remote_copy(src_ref=theirs, dst_ref=theirs, send_sem=ssem.at[3 * k + j],
                                             recv_sem=rsem.at[3 * k + j], device_id=sibling,
                                             device_id_type=MESH).wait_recv()
        for cp in cps:
            cp.wait_send()

    return pl.pallas_call(
        body, name="pair_forward",
        in_specs=[ANY] * n, out_specs=[ANY] * n,
        out_shape=[jax.ShapeDtypeStruct(a.shape, a.dtype) for a in lands],
        scratch_shapes=[pltpu.SemaphoreType.DMA((3 * n,)), pltpu.SemaphoreType.DMA((3 * n,))],
        input_output_aliases={k: k for k in range(n)},
    )(*lands)


N_SOURCES = 7


def _reduce_peers(x, y, c):
    peers = []
    for ox, oy in _other_chips(x, y):
        for rel in range(2):
            peers.append(((ox, oy, c + rel - 2 * c * rel), 2 * ox + oy))
    peers.append(((x, y, 1 - c), 2 * x + y))
    return peers


def _reduce_start(arrs, slots):
    n = len(arrs)

    def body(*refs):
        ins, lands = refs[:n], refs[n:2 * n]
        ssem, rsem = refs[2 * n:2 * n + 2]
        token = refs[-1]
        x, y, c = _mesh_pos()
        me = 2 * (2 * x + y) + c
        for k in range(n):
            half = arrs[k].shape[1] // 2
            for p, (dev, ochip) in enumerate(_reduce_peers(x, y, c)):
                pltpu.make_async_remote_copy(
                    src_ref=ins[k].at[ochip, pl.ds(dev[2] * half, half)], dst_ref=lands[k].at[me],
                    send_sem=ssem.at[N_SOURCES * k + p], recv_sem=rsem.at[N_SOURCES * k + p],
                    device_id=dev, device_id_type=MESH).start()
        token[...] = jnp.zeros_like(token)

    out = pl.pallas_call(
        body, name="reduce_start",
        in_specs=[HBM] * (2 * n),
        out_specs=[SEM, SEM] + [HBM] * (2 * n) + [pl.BlockSpec(memory_space=pltpu.VMEM)],
        out_shape=[pltpu.SemaphoreType.DMA((N_SOURCES * n,)), pltpu.SemaphoreType.DMA((N_SOURCES * n,))]
        + [pltpu.HBM(a.shape, a.dtype) for a in arrs] + [pltpu.HBM(a.shape, a.dtype) for a in slots]
        + [jax.ShapeDtypeStruct((SUBLANE, LANE), F32)],
        input_output_aliases={k: 2 + k for k in range(2 * n)},
        compiler_params=pltpu.CompilerParams(has_side_effects=DATAFLOW),
    )(*[_in_hbm(a) for a in arrs], *[_in_hbm(a) for a in slots])
    return out[0], out[1], list(out[2:2 + n]), list(out[2 + n:2 + 2 * n]), out[-1]


def _reduce_wait(ssem, rsem, arrs, slots, after):
    n = len(arrs)

    def body(*refs):
        ins, lnd = refs[:n], refs[n:2 * n]
        ssem_ref, rsem_ref = refs[2 * n:2 * n + 2]
        x, y, c = _mesh_pos()
        for k in range(n):
            half = arrs[k].shape[1] // 2
            for p, (dev, ochip) in enumerate(_reduce_peers(x, y, c)):
                cp = pltpu.make_async_remote_copy(
                    src_ref=ins[k].at[ochip, pl.ds(dev[2] * half, half)], dst_ref=lnd[k].at[2 * ochip + dev[2]],
                    send_sem=ssem_ref.at[N_SOURCES * k + p], recv_sem=rsem_ref.at[N_SOURCES * k + p],
                    device_id=dev, device_id_type=MESH)
                cp.wait_send()
                cp.wait_recv()

    out = pl.pallas_call(
        body, name="reduce_wait",
        in_specs=[HBM] * (2 * n) + [SEM, SEM] + [ANY] * len(after),
        out_specs=[HBM] * (2 * n),
        out_shape=[pltpu.HBM(a.shape, a.dtype) for a in arrs] + [pltpu.HBM(a.shape, a.dtype) for a in slots],
        input_output_aliases={k: k for k in range(2 * n)},
        compiler_params=pltpu.CompilerParams(has_side_effects=DATAFLOW),
    )(*arrs, *slots, ssem, rsem, *after)
    return list(out[n:])


def _own_part(arrs, chip, core, me):
    n = len(arrs)
    nb = GRAD_ROW_BLOCKS

    def body(chip_ref, core_ref, me_ref, *refs):
        for k in range(n):
            refs[n + k][...] = refs[k][...]

    def blk(a):
        return (1, a.shape[1] // 2 // nb, a.shape[2])

    grid_spec = pltpu.PrefetchScalarGridSpec(
        num_scalar_prefetch=3, grid=(nb,),
        in_specs=[pl.BlockSpec(blk(a), lambda i, ch, co, me: (ch[0], co[0] * nb + i, 0)) for a in arrs],
        out_specs=[pl.BlockSpec(blk(a), lambda i, ch, co, me: (me[0], i, 0)) for a in arrs])
    return pl.pallas_call(
        body, name="own_part", grid_spec=grid_spec,
        out_shape=[jax.ShapeDtypeStruct((N_DEV, a.shape[1] // 2, a.shape[2]), a.dtype) for a in arrs],
        compiler_params=_cparams(1),
    )(chip, core, me, *arrs)


def _sum_exchange(arrs):
    n = len(arrs)
    nb = GRAD_ROW_BLOCKS

    def body(*refs):
        ins, outs, bufs = refs[:n], refs[n:2 * n], refs[2 * n:3 * n]
        lsem, ssem, rsem = refs[3 * n:]
        i = pl.program_id(0)
        x, y, c = _mesh_pos()

        def copies(k, j):
            h = arrs[k].shape[1]
            hb = h // nb
            src = bufs[k].at[j]
            mine = outs[k].at[pl.ds(c * h + j * hb, hb)]
            theirs = outs[k].at[pl.ds((1 - c) * h + j * hb, hb)]
            sibling = dict(send_sem=ssem.at[k, j], recv_sem=rsem.at[k, j], device_id=(x, y, 1 - c),
                           device_id_type=MESH)
            return (pltpu.make_async_copy(src, mine, lsem.at[k, j]),
                    pltpu.make_async_remote_copy(src_ref=src, dst_ref=mine, **sibling),
                    pltpu.make_async_remote_copy(src_ref=src, dst_ref=theirs, **sibling))

        for k in range(n):
            r = ins[k]
            acc = r[0].astype(F32)
            for dev in range(1, N_DEV):
                acc = acc + r[dev].astype(F32)
            bufs[k][i] = acc
            local, send, _ = copies(k, i)
            local.start()
            send.start()

        @pl.when(i == nb - 1)
        def _():
            for k in range(n):
                for j in range(nb):
                    local, send, landing = copies(k, j)
                    local.wait()
                    send.wait_send()
                    landing.wait_recv()

    return pl.pallas_call(
        body, name="sum_exchange", grid=(nb,),
        in_specs=[pl.BlockSpec((N_DEV, a.shape[1] // nb, a.shape[2]), lambda i: (0, i, 0)) for a in arrs],
        out_specs=[ANY] * n,
        out_shape=[jax.ShapeDtypeStruct((2 * a.shape[1], a.shape[2]), F32) for a in arrs],
        scratch_shapes=[pltpu.VMEM((nb, a.shape[1] // nb, a.shape[2]), F32) for a in arrs]
        + [pltpu.SemaphoreType.DMA((n, nb))] * 3,
        compiler_params=_cparams(1),
    )(*arrs)


def _small_own(v, me):
    m = v.shape[0]

    def body(me_ref, v_ref, o_ref):
        o_ref[0] = v_ref[...]

    grid_spec = pltpu.PrefetchScalarGridSpec(
        num_scalar_prefetch=1, grid=(1,),
        in_specs=[pl.BlockSpec((m, LANE), lambda i, me: (0, 0))],
        out_specs=pl.BlockSpec((1, m, LANE), lambda i, me: (me[0], 0, 0)))
    return pl.pallas_call(
        body, name="small_own", grid_spec=grid_spec,
        out_shape=jax.ShapeDtypeStruct((N_DEV, m, LANE), v.dtype),
        compiler_params=_cparams(1),
    )(me, v)


def _small_start(v, slots):
    def body(v_ref, land, ssem, rsem, v_thru, land_thru, token):
        del v_thru, land_thru
        x, y, c = _mesh_pos()
        me = 2 * (2 * x + y) + c
        for p, (dev, _) in enumerate(_reduce_peers(x, y, c)):
            pltpu.make_async_remote_copy(src_ref=v_ref, dst_ref=land.at[me], send_sem=ssem.at[p],
                                         recv_sem=rsem.at[p], device_id=dev, device_id_type=MESH).start()
        token[...] = jnp.zeros_like(token)

    out = pl.pallas_call(
        body, name="small_start",
        in_specs=[HBM, HBM],
        out_specs=[SEM, SEM, HBM, HBM, pl.BlockSpec(memory_space=pltpu.VMEM)],
        out_shape=[pltpu.SemaphoreType.DMA((N_SOURCES,)), pltpu.SemaphoreType.DMA((N_SOURCES,)),
                   pltpu.HBM(v.shape, v.dtype), pltpu.HBM(slots.shape, slots.dtype),
                   jax.ShapeDtypeStruct((SUBLANE, LANE), F32)],
        input_output_aliases={0: 2, 1: 3},
        compiler_params=pltpu.CompilerParams(has_side_effects=DATAFLOW),
    )(_in_hbm(v), _in_hbm(slots))
    return out


def _small_wait(ssem, rsem, v, slots, after):
    def body(*refs):
        v_ref, land, ssem_ref, rsem_ref = refs[:4]
        x, y, c = _mesh_pos()
        for p, (dev, ochip) in enumerate(_reduce_peers(x, y, c)):
            cp = pltpu.make_async_remote_copy(src_ref=v_ref, dst_ref=land.at[2 * ochip + dev[2]],
                                              send_sem=ssem_ref.at[p], recv_sem=rsem_ref.at[p],
                                              device_id=dev, device_id_type=MESH)
            cp.wait_send()
            cp.wait_recv()

    out = pl.pallas_call(
        body, name="small_wait",
        in_specs=[HBM, HBM, SEM, SEM] + [ANY] * len(after),
        out_specs=[HBM, HBM],
        out_shape=[pltpu.HBM(v.shape, v.dtype), pltpu.HBM(slots.shape, slots.dtype)],
        input_output_aliases={0: 0, 1: 1},
        compiler_params=pltpu.CompilerParams(has_side_effects=DATAFLOW),
    )(v, slots, ssem, rsem, *after)
    return out[1]


GRAD_ROW_BLOCKS = 2


N_DEV = 8


def _adamw_math(w, g, m, v):
    m2 = ADAM_B1 * m + (1.0 - ADAM_B1) * g
    v2 = ADAM_B2 * v + (1.0 - ADAM_B2) * (g * g)
    m_hat = m2 / (1.0 - ADAM_B1 ** ADAM_STEP)
    v_hat = v2 / (1.0 - ADAM_B2 ** ADAM_STEP)
    delta = -ADAM_LR * (m_hat / (jnp.sqrt(v_hat) + ADAM_EPS) + ADAM_WD * w)
    return delta, m2, v2


def _adamw(w, m, v, gs, nblk):
    nl, r, n = w.shape
    assert nl == len(gs) and nl in (1, 2)
    br = r // nblk

    def body(w_ref, m_ref, v_ref, *rest):
        g_refs, (go_ref, d_ref, mo_ref, vo_ref) = rest[:nl], rest[nl:]
        g = g_refs[0][...]
        if nl == 2:
            g = jnp.where(pl.program_id(0) == 0, g, g_refs[1][...])
        delta, m2, v2 = _adamw_math(w_ref[0], g, m_ref[0], v_ref[0])
        go_ref[0] = g
        d_ref[0] = delta
        mo_ref[0] = m2
        vo_ref[0] = v2

    spec = pl.BlockSpec((1, br, n), lambda l, i: (l, i, 0))
    g_specs = [pl.BlockSpec((br, n), lambda l, i: (i, 0))] if nl == 1 else [
        pl.BlockSpec((br, n), lambda l, i: (jnp.where(l == 0, i, nblk - 1), 0)),
        pl.BlockSpec((br, n), lambda l, i: (jnp.where(l == 1, i, 0), 0))]
    return pl.pallas_call(
        body, name="adamw", grid=(nl, nblk),
        in_specs=[spec, spec, spec] + g_specs,
        out_specs=[spec] * 4,
        out_shape=[jax.ShapeDtypeStruct((nl, r, n), F32)] * 4,
        compiler_params=_cparams(2),
    )(w, m, v, *gs)


def _small_reduce_adamw(parts, w, m, v, rep_rows, sh_rows):
    mrows = rep_rows + N_SHARD * sh_rows + LOSS_ROWS

    def body(p_ref, w_ref, m_ref, v_ref, go_ref, d_ref, mo_ref, vo_ref, loss_ref):
        x, y, _ = _mesh_pos()
        mine = rep_rows + (2 * x + y) * sh_rows
        g_rep = p_ref[0:rep_rows, :]
        g_sh = p_ref[pl.ds(pl.multiple_of(mine, SUBLANE), sh_rows), :]
        loss = p_ref[mrows - LOSS_ROWS:mrows, :]
        for k in range(1, N_DEV):
            g_rep = g_rep + p_ref[k * mrows:k * mrows + rep_rows, :]
            g_sh = g_sh + p_ref[pl.ds(pl.multiple_of(k * mrows + mine, SUBLANE), sh_rows), :]
            loss = loss + p_ref[(k + 1) * mrows - LOSS_ROWS:(k + 1) * mrows, :]
        g = jnp.concatenate([g_rep, g_sh], axis=0)
        delta, m2, v2 = _adamw_math(w_ref[...], g, m_ref[...], v_ref[...])
        go_ref[...] = g
        d_ref[...] = delta
        mo_ref[...] = m2
        vo_ref[...] = v2
        loss_ref[...] = loss

    return pl.pallas_call(
        body, name="small_reduce_adamw",
        out_shape=[jax.ShapeDtypeStruct((rep_rows + sh_rows, 128), F32)] * 4
        + [jax.ShapeDtypeStruct((LOSS_ROWS, 128), F32)],
        compiler_params=pltpu.CompilerParams(vmem_limit_bytes=VMEM_LIMIT_MB * 1024 * 1024),
    )(parts, w, m, v)


LANE = 128
REP_SPEC = (("ffn1_norm", 16), ("mix_norm", 16), ("ffn2_norm", 16), ("final_norm", 8), ("pool_w", 256),
            ("pool_scale", 8), ("hgrn_lb_logits", 16), ("hgrn_gnorm", 8), ("lru_wa", 512), ("lru_wx", 512))
SH_SPEC = (("meta_tokens", 32), ("conv_w", 32), ("lru_conv_w", 8), ("conv_b", 8), ("conv_ln_g", 8),
           ("conv_ln_b", 8), ("lru_conv_b", 8), ("lru_ba", 8), ("lru_bx", 8), ("lru_lambda", 8))
REP_ROWS = sum(r for _, r in REP_SPEC)
SH_ROWS = sum(r for _, r in SH_SPEC)


def _pack_rows(vals, spec):
    parts = []
    for name, rows in spec:
        val = vals[name].astype(F32)
        if val.shape[-1] < LANE:
            flat = jnp.pad(val.reshape(-1, val.shape[-1]), ((0, 0), (0, LANE - val.shape[-1])))
        else:
            flat = val.reshape(-1, LANE)
        if flat.shape[0] < rows:
            flat = jnp.concatenate([flat, jnp.zeros((rows - flat.shape[0], LANE), F32)], axis=0)
        parts.append(flat)
    return jnp.concatenate(parts, axis=0)


def _unpack_rows(packed, spec, shapes):
    out = {}
    off = 0
    for name, rows in spec:
        shp = shapes[name]
        width = min(shp[-1], LANE)
        n = int(np.prod(shp)) // width
        out[name] = packed[off:off + n, :width].reshape(shp)
        off += rows
    return out


def _block_diag(blocks):
    n, b, _ = blocks.shape
    return sum(jnp.pad(blocks[g], ((g * b, (n - 1 - g) * b), (g * b, (n - 1 - g) * b))) for g in range(n))


def _diag_blocks(mat, n):
    b = mat.shape[0] // n
    return jnp.stack([mat[g * b:(g + 1) * b, g * b:(g + 1) * b] for g in range(n)])


BIG = ("ffn1_wg", "ffn1_wu", "ffn2_wg", "ffn2_wu", "ffn1_wd", "ffn2_wd", "w_in_even", "w_out_even",
       "w_in_odd", "w_out_odd")
WEIGHT_NAMES = ('meta_tokens', 'ffn1_norm', 'ffn1_wg', 'ffn1_wu', 'ffn1_wd', 'mix_norm', 'ffn2_norm', 'ffn2_wg',
                'ffn2_wu', 'ffn2_wd', 'w_in_even', 'pool_w', 'pool_scale', 'hgrn_lb_logits', 'hgrn_gnorm',
                'w_out_even', 'w_in_odd', 'conv_w', 'conv_b', 'conv_ln_g', 'conv_ln_b', 'lru_conv_w',
                'lru_conv_b', 'lru_wa', 'lru_ba', 'lru_wx', 'lru_bx', 'lru_lambda', 'w_out_odd', 'final_norm')


def _block_diag2(heads):
    nb = heads.shape[0] // 2
    return jnp.stack([_block_diag(heads[2 * j:2 * j + 2]) for j in range(nb)])


def _diag_blocks2(mats):
    return jnp.concatenate([_diag_blocks(mats[j], 2) for j in range(mats.shape[0])], axis=0)


GATHER_GROUPS = (
    (("small", 0),),
    (("ffn1_wg", 0), ("ffn1_wu", 0), ("ffn1_wd", 0)),
    (("w_in_even", 0), ("w_out_even", 0)),
    (("ffn2_wg", 0), ("ffn2_wu", 0), ("ffn2_wd", 0)),
    (("ffn1_wg", 1), ("ffn1_wu", 1), ("ffn1_wd", 1)),
    (("w_in_odd", 0), ("w_out_odd", 0)),
    (("ffn2_wg", 1), ("ffn2_wu", 1), ("ffn2_wd", 1)),
)
ADAM_ROW_BLOCKS = {"ffn1_wg": 2, "ffn1_wu": 2, "ffn2_wg": 2, "ffn2_wu": 2, "ffn1_wd": 2, "ffn2_wd": 2,
                   "w_in_even": 4, "w_out_even": 2, "w_in_odd": 4, "w_out_odd": 2}
TRANSPOSED = ("ffn1_wg", "ffn1_wu", "ffn2_wg", "ffn2_wu", "w_in_even")
SCATTER_DEPTH = 2
LOSS_ROWS = 8
GATHER_SPLIT = (1, 4)


def _unpack_small(sm, shapes):
    per_shard = [_unpack_rows(sm[s], SH_SPEC, shapes) for s in range(N_SHARD)]
    full = {}
    for n, _ in SH_SPEC:
        full[n] = jnp.concatenate([per_shard[s][n].reshape(-1, shapes[n][-1]) for s in range(N_SHARD)], axis=-1)
    full["conv_w"] = jnp.concatenate([full["conv_w"], jnp.zeros((CONV_HALO - CONV_WIDTH, D_CONV), F32)], axis=0)
    return full


def _local_step(x, tgt, w, shapes, fetch, emit, emit_small):
    s_len, d = x.shape
    t_real = s_len + N_META
    tp = -(-t_real // ROW_ALIGN) * ROW_ALIGN
    tm = _tile(tp, 832, ROW_ALIGN)
    tm_small = _tile(tp, 832, 16)
    tr = _tile(tp, 416, SUBLANE)
    tm_wgrad = _wgrad_tiles(tp, 1024)

    def gain(name, layer):
        return w[name][layer:layer + 1]

    pool_wbd = _block_diag(w["pool_w"][0]).astype(BF16)
    pool_scale = w["pool_scale"]
    wa_bd = _block_diag2(w["lru_wa"][0]).astype(BF16)
    wx_bd = _block_diag2(w["lru_wx"][0]).astype(BF16)
    mst, msk, n_lev = _hgrn_consts(HG_CHUNK)

    (sm,) = fetch(0, None)
    sf = _unpack_small(sm, shapes)
    h0 = jnp.concatenate([sf["meta_tokens"], x, jnp.zeros((tp - t_real, d), F32)], axis=0)
    tgt_pad = jnp.concatenate([jnp.zeros((N_META, d), F32), tgt, jnp.zeros((tp - t_real, d), F32)], axis=0)
    f1l0 = fetch(1, h0)
    h1, *s1 = _ffn_fwd(h0, gain("ffn1_norm", 0), *f1l0, tm)
    w_in_even4, w_out_even4 = fetch(2, h1)
    w_out_even = w_out_even4.reshape(d, d)
    even_piece = [(w_in_even4.reshape(D_IN_EVEN, d), (D_IN_EVEN, d), (0, 0))]
    p0, nm0 = _proj_fwd(h1, gain("mix_norm", 0), even_piece, tm_small, wt=True)
    ya = _pool_fwd(p0, pool_wbd, pool_scale, tr)
    yb, states = _hgrn_fwd(p0, w["hgrn_lb_logits"], w["hgrn_gnorm"], mst, msk, n_lev, tm)
    h2 = _out_fwd(h1, ya, yb, w_out_even, tm)
    f2l0 = fetch(3, h2)
    h3, *s2 = _ffn_fwd(h2, gain("ffn2_norm", 0), *f2l0, tm)
    f1l1 = fetch(4, h3)
    h4, *s3 = _ffn_fwd(h3, gain("ffn1_norm", 1), *f1l1, tm)
    w_in_odd4, w_out_odd4 = fetch(5, h4)
    w_out_odd = w_out_odd4.reshape(d, d)
    odd_pieces = [(w_in_odd4, (1, d, D_IN_ODD // N_SHARD), (k, 0, 0)) for k in range(N_SHARD)]
    p1, nm1 = _proj_fwd(h4, gain("mix_norm", 1), odd_pieces, tm_small)
    yc, conv_out = _convmod_fwd(p1, sf["conv_w"], sf["conv_b"], sf["conv_ln_g"], sf["conv_ln_b"], tr)
    lru_args = (sf["lru_conv_w"], sf["lru_conv_b"], wa_bd, sf["lru_ba"], wx_bd, sf["lru_bx"], sf["lru_lambda"])
    yd, hs = _lru_fwd(p1, *lru_args)
    h5 = _out_fwd(h4, yc, yd, w_out_odd, tm)
    f2l1 = fetch(6, h5)
    h6, *s4 = _ffn_fwd(h5, gain("ffn2_norm", 1), *f2l1, tm)
    loss, dh6, dg_final = _loss_bwd(h6, w["final_norm"].reshape(1, d), tgt_pad, t_real, tm)

    def ffn_bwd(dho, h, saved, norm, wts, after=()):
        ga, gb, sa, n = saved
        dh, da, db, dg, dy = _ffn_bwd_act(dho, h, norm, ga, gb, *wts, tm, after)
        return dh, dg, _ffn_bwd_w([(da, n, None), (db, n, None), (sa, dy, None)], tm_wgrad)

    dh5, dg_f2_l1, g = ffn_bwd(dh6, h5, s4, gain("ffn2_norm", 1), f2l1)
    sent = emit((("ffn2_wg", 1), ("ffn2_wu", 1), ("ffn2_wd", 1)), g)
    dyc, dyd, dw_out_odd = _out_bwd(dh5, yc, yd, w_out_odd, tm, tuple(sent))
    dca, dcb, dconv_w, dconv_vec = _convmod_bwd(p1, conv_out, dyc, sf["conv_w"], sf["conv_ln_g"],
                                                sf["conv_ln_b"], tr)
    dlx, dlg, dwa_bd, dwx_bd, dlru_vec = _lru_bwd(p1, hs, dyd, *lru_args)
    dp1 = [dca, dcb, dlx, dlg]
    dh4, dg_mix_l1 = _proj_bwd_act(dh5, h4, gain("mix_norm", 1), dp1, odd_pieces, tm_small)
    dw_in_odd = jnp.stack(_proj_bwd_w(nm1, dp1, tm_wgrad))
    dh3, dg_f1_l1, g = ffn_bwd(dh4, h3, s3, gain("ffn1_norm", 1), f1l1)
    sent = emit((("w_out_odd", 0), ("w_in_odd", 0), ("ffn1_wg", 1), ("ffn1_wu", 1), ("ffn1_wd", 1)),
                [dw_out_odd.reshape(N_SHARD, d // N_SHARD, d), dw_in_odd] + list(g))
    dh2, dg_f2_l0, g = ffn_bwd(dh3, h2, s2, gain("ffn2_norm", 0), f2l0, tuple(sent))
    sent = emit((("ffn2_wg", 0), ("ffn2_wu", 0), ("ffn2_wd", 0)), g)
    dya, dyb, dw_out_even = _out_bwd(dh2, ya, yb, w_out_even, tm, tuple(sent))
    dpool, dpool_wbd, dpool_scale = _pool_bwd(p0, dya, pool_wbd, pool_scale, tr)
    dq, dz, dv, dgate, dlb_logits, dgn_heads = _hgrn_bwd(p0, dyb, states, w["hgrn_lb_logits"], w["hgrn_gnorm"],
                                                         mst, msk, n_lev, tm)
    dp0 = [dpool, dq, dz, dv, dgate]
    dh1, dg_mix_l0 = _proj_bwd_act(dh2, h1, gain("mix_norm", 0), dp0, even_piece, tm_small, wt=True)
    dw_in_even_t = jnp.concatenate(_proj_bwd_w(nm0, dp0, tm_wgrad, wt=True), axis=0)
    ga, gb, sa, n1 = s1
    (dwd_f1l0,) = _ffn_bwd_w([(sa, dh1, 0.5)], tm_wgrad)
    sent = emit((("w_out_even", 0), ("w_in_even", 0), ("ffn1_wd", 0)),
                [dw_out_even.reshape(N_SHARD, d // N_SHARD, d),
                 dw_in_even_t.reshape(N_SHARD, D_IN_EVEN // N_SHARD, d), dwd_f1l0])
    dh0, da, db, dg_f1_l0, _ = _ffn_bwd_act(dh1, h0, gain("ffn1_norm", 0), ga, gb, *f1l0, tm, tuple(sent))

    grad_x = dh0[N_META:t_real]
    rep = {
        "ffn1_norm": jnp.concatenate([dg_f1_l0, dg_f1_l1], axis=0),
        "mix_norm": jnp.concatenate([dg_mix_l0, dg_mix_l1], axis=0),
        "ffn2_norm": jnp.concatenate([dg_f2_l0, dg_f2_l1], axis=0),
        "final_norm": dg_final,
        "pool_w": _diag_blocks(dpool_wbd, len(POOL_WINDOWS)),
        "pool_scale": dpool_scale,
        "hgrn_lb_logits": dlb_logits,
        "hgrn_gnorm": jnp.sum(dgn_heads, axis=0),
        "lru_wa": _diag_blocks2(dwa_bd),
        "lru_wx": _diag_blocks2(dwx_bd),
    }
    dmeta = jnp.transpose(dh0[:N_META].reshape(N_META, N_SHARD, 2, LANE), (1, 0, 2, 3)).reshape(N_SHARD, 32, LANE)
    packs = [_pack_rows(rep, REP_SPEC)]
    for s in range(N_SHARD):
        sh = {
            "meta_tokens": dmeta[s], "conv_w": dconv_w[s], "lru_conv_w": dlru_vec[s, 0:4],
            "conv_b": dconv_vec[s, 0:1], "conv_ln_g": dconv_vec[s, 1:2], "conv_ln_b": dconv_vec[s, 2:3],
            "lru_conv_b": dlru_vec[s, 4:5], "lru_ba": dlru_vec[s, 5:6], "lru_bx": dlru_vec[s, 6:7],
            "lru_lambda": dlru_vec[s, 7:8],
        }
        packs.append(_pack_rows(sh, SH_SPEC))
    packs.append(jnp.pad(loss, ((0, LOSS_ROWS - 1), (0, LANE - 1))))
    sent = emit_small(jnp.concatenate(packs, axis=0))
    emit((("ffn1_wg", 0), ("ffn1_wu", 0)), _ffn_bwd_w([(da, n1, None), (db, n1, None)], tm_wgrad, tuple(sent)))
    return grad_x


def kernel(x, meta_tokens, ffn1_norm, ffn1_wg, ffn1_wu, ffn1_wd, mix_norm, ffn2_norm, ffn2_wg, ffn2_wu, ffn2_wd, w_in_even, pool_w, pool_scale, hgrn_lb_logits, hgrn_gnorm, w_out_even, w_in_odd, conv_w, conv_b, conv_ln_g, conv_ln_b, lru_conv_w, lru_conv_b, lru_wa, lru_ba, lru_wx, lru_bx, lru_lambda, w_out_odd, final_norm, loss_target, m_meta_tokens, m_ffn1_norm, m_ffn1_wg, m_ffn1_wu, m_ffn1_wd, m_mix_norm, m_ffn2_norm, m_ffn2_wg, m_ffn2_wu, m_ffn2_wd, m_w_in_even, m_pool_w, m_pool_scale, m_hgrn_lb_logits, m_hgrn_gnorm, m_w_out_even, m_w_in_odd, m_conv_w, m_conv_b, m_conv_ln_g, m_conv_ln_b, m_lru_conv_w, m_lru_conv_b, m_lru_wa, m_lru_ba, m_lru_wx, m_lru_bx, m_lru_lambda, m_w_out_odd, m_final_norm, v_meta_tokens, v_ffn1_norm, v_ffn1_wg, v_ffn1_wu, v_ffn1_wd, v_mix_norm, v_ffn2_norm, v_ffn2_wg, v_ffn2_wu, v_ffn2_wd, v_w_in_even, v_pool_w, v_pool_scale, v_hgrn_lb_logits, v_hgrn_gnorm, v_w_out_even, v_w_in_odd, v_conv_w, v_conv_b, v_conv_ln_g, v_conv_ln_b, v_lru_conv_w, v_lru_conv_b, v_lru_wa, v_lru_ba, v_lru_wx, v_lru_bx, v_lru_lambda, v_w_out_odd, v_final_norm):
    args = locals()
    w = {n: args[n] for n in WEIGHT_NAMES}
    m = {n: args["m_" + n] for n in WEIGHT_NAMES}
    v = {n: args["v_" + n] for n in WEIGHT_NAMES}
    shapes = {n: w[n].shape for n in WEIGHT_NAMES}
    core = lax.axis_index("c").astype(jnp.int32).reshape(1)
    chip = (2 * lax.axis_index("x") + lax.axis_index("y")).astype(jnp.int32).reshape(1)
    me = 2 * chip + core

    def view(a, n):
        return jnp.swapaxes(a, 1, 2) if n in TRANSPOSED else a

    wv, mv, vv = [{n: view(src[n], n) for n in BIG} for src in (w, m, v)]

    def shard(key):
        n, l = key
        return _pack_rows(w, SH_SPEC) if n == "small" else wv[n][l].astype(BF16)

    started = {}
    for groups in (GATHER_GROUPS[:2], GATHER_GROUPS[2:]):
        gkeys = [key for grp in groups for key in grp]
        ssem, rsem, srcs, lands, token = _gather_start([shard(key) for key in gkeys],
                                                       [any(key in GATHER_GROUPS[g] for g in GATHER_SPLIT)
                                                        for key in gkeys])
        for k, key in enumerate(gkeys):
            started[key] = (ssem, rsem, srcs[k], lands[k], k, token)

    def pack_small(src):
        return jnp.concatenate([_pack_rows(src, REP_SPEC), _pack_rows(src, SH_SPEC)], axis=0)

    small_packs = [pack_small(src) for src in (w, m, v)]

    def fetch(group, after):
        st = [started[key] for key in GATHER_GROUPS[group]]
        deps = (st[0][5],) if after is None else (after,)
        if group == 1:
            deps += (started[GATHER_GROUPS[2][0]][5],) + tuple(small_packs)
        split = group in GATHER_SPLIT
        got = _gather_wait(st[0][0], st[0][1], [s[2] for s in st], [s[3] for s in st], [s[4] for s in st], deps,
                           split)
        return _pair_forward(got) if split else got

    in_flight, reduced = [], {}

    def collect(entry, after):
        gkeys, gs_sem, gr_sem, grads_thru, slots_thru, _ = entry
        slots = _reduce_wait(gs_sem, gr_sem, grads_thru, slots_thru, after)
        full = _sum_exchange(slots)
        reduced.update(zip(gkeys, full))
        return full[0]

    def emit(gkeys, grads):
        grads = list(grads)
        in_flight.append((gkeys,) + tuple(_reduce_start(grads, _own_part(grads, chip, core, me))))
        token = in_flight[-1][-1]
        if len(in_flight) > SCATTER_DEPTH:
            return token, collect(in_flight[-1 - SCATTER_DEPTH], (token,))
        return (token,)

    small_flight = []

    def emit_small(part):
        small_flight.append(_small_start(part, _small_own(part, me)))
        return (small_flight[0][4],)

    grad_x = _local_step(x[0], loss_target[0], w, shapes, fetch, emit, emit_small)

    out_g, out_d, out_m, out_v = {}, {}, {}, {}
    deps = (in_flight[-1][-1],)

    def adamw_ready():
        done = ()
        for n in BIG:
            layers = range(shapes[n][0])
            if n not in out_g and all((n, l) in reduced for l in layers):
                res = _adamw(wv[n], mv[n], vv[n], [reduced[(n, l)] for l in layers], ADAM_ROW_BLOCKS[n])
                out_g[n], out_d[n], out_m[n], out_v[n] = [view(r, n) for r in res]
                done += (res[1],)
        return done

    deps += adamw_ready()
    for entry in in_flight[-SCATTER_DEPTH:-1]:
        collect(entry, deps)
        deps += adamw_ready()
    s_ssem, s_rsem, s_part, s_slots, _ = small_flight[0]
    small_all = _small_wait(s_ssem, s_rsem, s_part, s_slots, deps)
    small_res = _small_reduce_adamw(small_all.reshape(-1, LANE), *small_packs, REP_ROWS, SH_ROWS)
    collect(in_flight[-1], deps + (small_res[0],))
    adamw_ready()

    loss = small_res[4][0, 0]
    for dst, packed in zip((out_g, out_d, out_m, out_v), small_res[:4]):
        dst.update(_unpack_rows(packed[:REP_ROWS], REP_SPEC, shapes))
        dst.update(_unpack_rows(packed[REP_ROWS:], SH_SPEC, shapes))

    return (loss, grad_x[None], *[out_g[n] for n in WEIGHT_NAMES], *[out_d[n] for n in WEIGHT_NAMES],
            *[out_m[n] for n in WEIGHT_NAMES], *[out_v[n] for n in WEIGHT_NAMES])
```

```python
import functools

import numpy as np
import jax
import jax.numpy as jnp
from jax import lax
from jax.experimental import pallas as pl
from jax.experimental.pallas import tpu as pltpu

F32 = jnp.float32
BF16 = jnp.bfloat16
MESH = pl.DeviceIdType.MESH

EPS = 1e-6
N_META = 16
D_FF = 2816
N_SHARD = 4
FF_SHARD = D_FF // N_SHARD
D_POOL = 256
POOL_GROUP = 64
POOL_WINDOWS = (2, 4, 8, 16)
D_HGRN = 768
HG_HEADS = 6
HEAD = 128
HG_CHUNK = 64
HG_HEADS_PER_STEP = 6
PIPE_ROW_BLOCKS = 2
HG_PBLOCK = 256
D_IN_EVEN = D_POOL + 4 * D_HGRN
D_CONV = 512
CONV_WIDTH = 31
CONV_HALO = 32
D_LRU = 512
LRU_CONV = 4
LRU_HALO = 8
LRU_C = 8.0
D_IN_ODD = 2 * D_CONV + 2 * D_LRU
SUBLANE = 8
MXU_DEPTH = 256
ROW_ALIGN = 64

ADAM_LR = 0.001
ADAM_B1 = 0.9
ADAM_B2 = 0.999
ADAM_EPS = 1e-08
ADAM_WD = 0.01
ADAM_STEP = 10

VMEM_LIMIT_MB = 56


def _cparams(n_grid_axes=0, vmem_mb=VMEM_LIMIT_MB):
    sem = ("arbitrary",) * n_grid_axes if n_grid_axes else None
    return pltpu.CompilerParams(dimension_semantics=sem, vmem_limit_bytes=vmem_mb * 1024 * 1024)


def _tile(n, target, mult):
    best = None
    for t in range(mult, min(n, target) + 1, mult):
        if n % t == 0:
            best = t
    assert best is not None, (n, target, mult)
    return best


def _dot(a, b):
    return jnp.dot(a, b, preferred_element_type=F32)


def _dot_nt(a, b):
    return lax.dot_general(a, b, (((1,), (1,)), ((), ())), preferred_element_type=F32)


def _dot_tn(a, b):
    return lax.dot_general(a, b, (((0,), (0,)), ((), ())), preferred_element_type=F32)


def _sigmoid(x):
    return 1.0 / (1.0 + jnp.exp(-x))


def _colsum(x):
    return jnp.sum(x, axis=0, keepdims=True)


def _rms_stats(h):
    rstd = lax.rsqrt(jnp.mean(h * h, axis=-1, keepdims=True) + EPS)
    return rstd, h * rstd


def _rms_bwd(dn, g, rstd, xhat):
    dng = dn * g
    dh = rstd * (dng - xhat * jnp.mean(dng * xhat, axis=-1, keepdims=True))
    return dh, _colsum(dn * xhat)


def _ffn_fwd(h, norm, wg4, wu4, wd4, tm):
    tp, d = h.shape
    nt = tp // tm

    def body(h_ref, g_ref, wg_ref, wu_ref, wd_ref, ho_ref, ga_ref, gb_ref, sa_ref, n_ref, n_sc, acc):
        s = pl.program_id(1)

        @pl.when(s == 0)
        def _():
            hh = h_ref[...]
            rstd, xhat = _rms_stats(hh)
            n = (xhat * g_ref[...]).astype(BF16)
            n_sc[...] = n
            n_ref[...] = n
            acc[...] = jnp.zeros_like(acc)

        hb = tm // PIPE_ROW_BLOCKS
        rows = [pl.ds(k * hb, hb) for k in range(PIPE_ROW_BLOCKS)]

        def gate_up(k):
            n = n_sc[rows[k], :]
            return _dot_nt(n, wg_ref[0]), _dot_nt(n, wu_ref[0])

        ab = gate_up(0)
        for k in range(PIPE_ROW_BLOCKS):
            ab_next = gate_up(k + 1) if k + 1 < PIPE_ROW_BLOCKS else None
            a, b = ab
            sig = _sigmoid(a)
            sil = a * sig
            ga_ref[0, rows[k], :] = (sig * (1.0 + a * (1.0 - sig)) * b).astype(BF16)
            gb_ref[0, rows[k], :] = sil.astype(BF16)
            sg = (sil * b).astype(BF16)
            sa_ref[0, rows[k], :] = sg
            acc[rows[k], :] += _dot(sg, wd_ref[0])
            ab = ab_next

        @pl.when(s == N_SHARD - 1)
        def _():
            ho_ref[...] = h_ref[...] + 0.5 * acc[...]

    return pl.pallas_call(
        body, name="ffn_fwd",
        grid=(nt, N_SHARD),
        in_specs=[
            pl.BlockSpec((tm, d), lambda i, s: (i, 0)),
            pl.BlockSpec((1, d), lambda i, s: (0, 0)),
            pl.BlockSpec((1, FF_SHARD, d), lambda i, s: (s, 0, 0)),
            pl.BlockSpec((1, FF_SHARD, d), lambda i, s: (s, 0, 0)),
            pl.BlockSpec((1, FF_SHARD, d), lambda i, s: (s, 0, 0)),
        ],
        out_specs=[
            pl.BlockSpec((tm, d), lambda i, s: (i, 0)),
            pl.BlockSpec((1, tm, FF_SHARD), lambda i, s: (s, i, 0)),
            pl.BlockSpec((1, tm, FF_SHARD), lambda i, s: (s, i, 0)),
            pl.BlockSpec((1, tm, FF_SHARD), lambda i, s: (s, i, 0)),
            pl.BlockSpec((tm, d), lambda i, s: (i, 0)),
        ],
        out_shape=[
            jax.ShapeDtypeStruct((tp, d), F32),
            jax.ShapeDtypeStruct((N_SHARD, tp, FF_SHARD), BF16),
            jax.ShapeDtypeStruct((N_SHARD, tp, FF_SHARD), BF16),
            jax.ShapeDtypeStruct((N_SHARD, tp, FF_SHARD), BF16),
            jax.ShapeDtypeStruct((tp, d), BF16),
        ],
        scratch_shapes=[pltpu.VMEM((tm, d), BF16), pltpu.VMEM((tm, d), F32)],
        compiler_params=_cparams(2),
    )(h, norm, wg4, wu4, wd4)


def _ffn_bwd_act(dho, h, norm, ga4, gb4, wg4, wu4, wd4, tm, after=()):
    tp, d = h.shape
    nt = tp // tm

    def body(dho_ref, h_ref, g_ref, ga_ref, gb_ref, wg_ref, wu_ref, wd_ref, *rest):
        dh_ref, da_ref, db_ref, dg_ref, dy_ref, dn_sc = rest[len(after):]
        i = pl.program_id(0)
        s = pl.program_id(1)

        @pl.when(s == 0)
        def _():
            dy_ref[...] = (0.5 * dho_ref[...]).astype(BF16)
            dn_sc[...] = jnp.zeros_like(dn_sc)

        @pl.when((s == 0) & (i == 0))
        def _():
            dg_ref[...] = jnp.zeros_like(dg_ref)

        hb = tm // PIPE_ROW_BLOCKS
        rows = [pl.ds(k * hb, hb) for k in range(PIPE_ROW_BLOCKS)]

        def shard_step(last):
            ds = _dot_nt(dy_ref[rows[0], :], wd_ref[0])
            dg = None
            for k in range(PIPE_ROW_BLOCKS):
                ds_next = _dot_nt(dy_ref[rows[k + 1], :], wd_ref[0]) if k + 1 < PIPE_ROW_BLOCKS else None
                da = (ds * ga_ref[0, rows[k], :].astype(F32)).astype(BF16)
                db = (ds * gb_ref[0, rows[k], :].astype(F32)).astype(BF16)
                da_ref[0, rows[k], :] = da
                db_ref[0, rows[k], :] = db
                dn = dn_sc[rows[k], :] + (_dot(da, wg_ref[0]) + _dot(db, wu_ref[0]))
                if last:
                    rstd, xhat = _rms_stats(h_ref[rows[k], :])
                    dh, dgk = _rms_bwd(dn, g_ref[...], rstd, xhat)
                    dh_ref[rows[k], :] = dho_ref[rows[k], :] + dh
                    dg = dgk if dg is None else dg + dgk
                else:
                    dn_sc[rows[k], :] = dn
                ds = ds_next
            if last:
                dg_ref[...] += dg

        pl.when(s < N_SHARD - 1)(functools.partial(shard_step, False))
        pl.when(s == N_SHARD - 1)(functools.partial(shard_step, True))

    return pl.pallas_call(
        body, name="ffn_bwd_act",
        grid=(nt, N_SHARD),
        in_specs=[
            pl.BlockSpec((tm, d), lambda i, s: (i, 0)),
            pl.BlockSpec((tm, d), lambda i, s: (i, 0)),
            pl.BlockSpec((1, d), lambda i, s: (0, 0)),
            pl.BlockSpec((1, tm, FF_SHARD), lambda i, s: (s, i, 0)),
            pl.BlockSpec((1, tm, FF_SHARD), lambda i, s: (s, i, 0)),
            pl.BlockSpec((1, FF_SHARD, d), lambda i, s: (s, 0, 0)),
            pl.BlockSpec((1, FF_SHARD, d), lambda i, s: (s, 0, 0)),
            pl.BlockSpec((1, FF_SHARD, d), lambda i, s: (s, 0, 0)),
        ] + [pl.BlockSpec(memory_space=pl.ANY)] * len(after),
        out_specs=[
            pl.BlockSpec((tm, d), lambda i, s: (i, 0)),
            pl.BlockSpec((1, tm, FF_SHARD), lambda i, s: (s, i, 0)),
            pl.BlockSpec((1, tm, FF_SHARD), lambda i, s: (s, i, 0)),
            pl.BlockSpec((1, d), lambda i, s: (0, 0)),
            pl.BlockSpec((tm, d), lambda i, s: (i, 0)),
        ],
        out_shape=[
            jax.ShapeDtypeStruct((tp, d), F32),
            jax.ShapeDtypeStruct((N_SHARD, tp, FF_SHARD), BF16),
            jax.ShapeDtypeStruct((N_SHARD, tp, FF_SHARD), BF16),
            jax.ShapeDtypeStruct((1, d), F32),
            jax.ShapeDtypeStruct((tp, d), BF16),
        ],
        scratch_shapes=[pltpu.VMEM((tm, d), F32)],
        compiler_params=_cparams(2),
    )(dho, h, norm, ga4, gb4, wg4, wu4, wd4, *after)


def _wgrad_tiles(tp, target):
    tm = min(target, tp) // MXU_DEPTH * MXU_DEPTH
    nt = tp // tm
    tail = tp - nt * tm
    assert tail == 0 or (nt * tm) % tail == 0 and tail % 16 == 0, (tp, tm, tail)
    return tm, nt, tail


def _ffn_bwd_w(pairs, tiles, after=()):
    npair = len(pairs)
    tp, d = pairs[0][1].shape
    tm, nt, tail = tiles
    rhs_list = []
    for _, rhs, _ in pairs:
        if all(rhs is not r for r in rhs_list):
            rhs_list.append(rhs)
    rhs_of = [[rhs is r for r in rhs_list].index(True) for _, rhs, _ in pairs]
    nrhs = len(rhs_list)
    nin = nrhs + npair
    ntail = nin if tail else 0

    def body(*refs):
        rhs_refs = refs[:nrhs]
        lhs_refs = refs[nrhs:nin]
        rhs_tails = refs[nin:nin + ntail][:nrhs]
        lhs_tails = refs[nin:nin + ntail][nrhs:]
        rest = refs[nin + ntail + len(after):]
        out_refs, accs = rest[:npair], rest[npair:]
        i = pl.program_id(1)

        @pl.when(i == 0)
        def _():
            for acc in accs:
                acc[...] = jnp.zeros_like(acc)

        def accumulate(lhs, rhs):
            for k, (_, _, scale) in enumerate(pairs):
                r = rhs[rhs_of[k]][...]
                if scale is not None:
                    r = (scale * r).astype(BF16)
                accs[k][...] += _dot_tn(lhs[k][0], r)

        accumulate(lhs_refs, rhs_refs)

        @pl.when(i == nt - 1)
        def _():
            if tail:
                accumulate(lhs_tails, rhs_tails)
            for k in range(npair):
                out_refs[k][0] = accs[k][...].astype(BF16)

    tail_specs, tail_args = [], []
    if tail:
        tb = nt * tm // tail
        tail_specs = ([pl.BlockSpec((tail, d), lambda s, i: (tb, 0))] * nrhs
                      + [pl.BlockSpec((1, tail, FF_SHARD), lambda s, i: (s, tb, 0))] * npair)
        tail_args = [*rhs_list, *[lhs for lhs, _, _ in pairs]]
    return pl.pallas_call(
        body, name="ffn_bwd_w",
        grid=(N_SHARD, nt),
        in_specs=[pl.BlockSpec((tm, d), lambda s, i: (i, 0))] * nrhs
        + [pl.BlockSpec((1, tm, FF_SHARD), lambda s, i: (s, i, 0))] * npair
        + tail_specs
        + [pl.BlockSpec(memory_space=pl.ANY)] * len(after),
        out_specs=[pl.BlockSpec((1, FF_SHARD, d), lambda s, i: (s, 0, 0))] * npair,
        out_shape=[jax.ShapeDtypeStruct((N_SHARD, FF_SHARD, d), BF16)] * npair,
        scratch_shapes=[pltpu.VMEM((FF_SHARD, d), F32)] * npair,
        compiler_params=_cparams(2),
    )(*rhs_list, *[lhs for lhs, _, _ in pairs], *tail_args, *after)


def _proj_fwd(h, norm, w_pieces, tm, wt=False):
    tp, d = h.shape
    widths = [bs[-2] if wt else bs[-1] for _, bs, _ in w_pieces]
    ntot = sum(widths)
    npc = len(w_pieces)

    def body(*refs):
        h_ref, g_ref = refs[:2]
        w_refs = refs[2:2 + npc]
        p_ref, n_ref = refs[2 + npc:]
        rstd, xhat = _rms_stats(h_ref[...])
        n = (xhat * g_ref[...]).astype(BF16)
        n_ref[...] = n
        off = 0
        for k in range(npc):
            w = w_refs[k][...]
            w = w.reshape(w.shape[-2], w.shape[-1])
            p_ref[:, off:off + widths[k]] = _dot_nt(n, w) if wt else _dot(n, w)
            off += widths[k]

    in_specs = [pl.BlockSpec((tm, d), lambda i: (i, 0)), pl.BlockSpec((1, d), lambda i: (0, 0))]
    for _, bs, idx in w_pieces:
        in_specs.append(pl.BlockSpec(bs, functools.partial(lambda i, idx: idx, idx=idx)))
    return pl.pallas_call(
        body, name="proj_fwd",
        grid=(tp // tm,),
        in_specs=in_specs,
        out_specs=[pl.BlockSpec((tm, ntot), lambda i: (i, 0)), pl.BlockSpec((tm, d), lambda i: (i, 0))],
        out_shape=[jax.ShapeDtypeStruct((tp, ntot), F32), jax.ShapeDtypeStruct((tp, d), BF16)],
        compiler_params=_cparams(1),
    )(h, norm, *[w for w, _, _ in w_pieces])


def _proj_bwd_act(dres, h, norm, dp_pieces, w_pieces, tm, wt=False):
    tp, d = h.shape
    npc = len(dp_pieces)
    nw = len(w_pieces)
    assert nw == npc or (nw == 1 and wt)

    def body(*refs):
        dres_ref, h_ref, g_ref = refs[:3]
        dp_refs = refs[3:3 + npc]
        w_refs = refs[3 + npc:3 + npc + nw]
        dh_ref, dg_ref = refs[3 + npc + nw:]
        i = pl.program_id(0)

        @pl.when(i == 0)
        def _():
            dg_ref[...] = jnp.zeros_like(dg_ref)

        def matmuls(rows):
            dn = None
            off = 0
            for k in range(npc):
                if nw == npc:
                    w = w_refs[k][...]
                    w = w.reshape(w.shape[-2], w.shape[-1])
                else:
                    w = w_refs[0][off:off + dp_pieces[k].shape[1], :]
                    off += dp_pieces[k].shape[1]
                t = _dot(dp_refs[k][rows, :], w) if wt else _dot_nt(dp_refs[k][rows, :], w)
                dn = t if dn is None else dn + t
            return dn

        hb = tm // PIPE_ROW_BLOCKS
        blocks = [pl.ds(k * hb, hb) for k in range(PIPE_ROW_BLOCKS)]
        dn = matmuls(blocks[0])
        dg = None
        for k, rows in enumerate(blocks):
            dn_next = matmuls(blocks[k + 1]) if k + 1 < PIPE_ROW_BLOCKS else None
            rstd, xhat = _rms_stats(h_ref[rows, :])
            dh, dgk = _rms_bwd(dn, g_ref[...], rstd, xhat)
            dh_ref[rows, :] = dres_ref[rows, :] + dh
            dg = dgk if dg is None else dg + dgk
            dn = dn_next
        dg_ref[...] += dg

    in_specs = [pl.BlockSpec((tm, d), lambda i: (i, 0)), pl.BlockSpec((tm, d), lambda i: (i, 0)),
                pl.BlockSpec((1, d), lambda i: (0, 0))]
    for dp in dp_pieces:
        in_specs.append(pl.BlockSpec((tm, dp.shape[1]), lambda i: (i, 0)))
    for _, bs, idx in w_pieces:
        in_specs.append(pl.BlockSpec(bs, functools.partial(lambda i, idx: idx, idx=idx)))
    return pl.pallas_call(
        body, name="proj_bwd_act",
        grid=(tp // tm,),
        in_specs=in_specs,
        out_specs=[pl.BlockSpec((tm, d), lambda i: (i, 0)), pl.BlockSpec((1, d), lambda i: (0, 0))],
        out_shape=[jax.ShapeDtypeStruct((tp, d), F32), jax.ShapeDtypeStruct((1, d), F32)],
        compiler_params=_cparams(1),
    )(dres, h, norm, *dp_pieces, *[w for w, _, _ in w_pieces])


def _proj_bwd_w(n, dp_pieces, tiles, wt=False):
    tp, d = n.shape
    tm, nt, tail = tiles
    npc = len(dp_pieces)
    widths = [dp.shape[1] for dp in dp_pieces]
    oshape = (lambda w: (w, d)) if wt else (lambda w: (d, w))
    nin = 1 + npc
    ntail = nin if tail else 0

    def body(*refs):
        o_refs = refs[nin + ntail:nin + ntail + npc]
        accs = refs[nin + ntail + npc:]
        i = pl.program_id(0)

        @pl.when(i == 0)
        def _():
            for acc in accs:
                acc[...] = jnp.zeros_like(acc)

        def accumulate(n_ref, dp_refs):
            nn = n_ref[...]
            for k in range(npc):
                accs[k][...] += _dot_tn(dp_refs[k][...], nn) if wt else _dot_tn(nn, dp_refs[k][...])

        accumulate(refs[0], refs[1:nin])

        @pl.when(i == nt - 1)
        def _():
            if tail:
                accumulate(refs[nin], refs[nin + 1:nin + ntail])
            for k in range(npc):
                o_refs[k][...] = accs[k][...].astype(BF16)

    tail_specs = []
    if tail:
        tb = nt * tm // tail
        tail_specs = [pl.BlockSpec((tail, d), lambda i: (tb, 0))] + [pl.BlockSpec((tail, w), lambda i: (tb, 0))
                                                                    for w in widths]
    return pl.pallas_call(
        body, name="proj_bwd_w",
        grid=(nt,),
        in_specs=[pl.BlockSpec((tm, d), lambda i: (i, 0))]
        + [pl.BlockSpec((tm, w), lambda i: (i, 0)) for w in widths] + tail_specs,
        out_specs=[pl.BlockSpec(oshape(w), lambda i: (0, 0)) for w in widths],
        out_shape=[jax.ShapeDtypeStruct(oshape(w), BF16) for w in widths],
        scratch_shapes=[pltpu.VMEM(oshape(w), F32) for w in widths],
        compiler_params=_cparams(1),
    )(n, *dp_pieces, *((n, *dp_pieces) if tail else ()))


def _out_fwd(h, ya, yb, w, tm):
    tp, d = h.shape
    na, nb = ya.shape[1], yb.shape[1]

    def body(h_ref, ya_ref, yb_ref, w_ref, o_ref):
        y = _dot(ya_ref[...].astype(BF16), w_ref[0:na, :]) + _dot(yb_ref[...].astype(BF16), w_ref[na:, :])
        o_ref[...] = h_ref[...] + y

    return pl.pallas_call(
        body, name="out_fwd",
        grid=(tp // tm,),
        in_specs=[pl.BlockSpec((tm, d), lambda i: (i, 0)), pl.BlockSpec((tm, na), lambda i: (i, 0)),
                  pl.BlockSpec((tm, nb), lambda i: (i, 0)), pl.BlockSpec((d, d), lambda i: (0, 0))],
        out_specs=pl.BlockSpec((tm, d), lambda i: (i, 0)),
        out_shape=jax.ShapeDtypeStruct((tp, d), F32),
        compiler_params=_cparams(1),
    )(h, ya, yb, w)


def _out_bwd(dy, ya, yb, w, tm, after=()):
    tp, d = dy.shape
    na, nb = ya.shape[1], yb.shape[1]

    def body(dy_ref, ya_ref, yb_ref, w_ref, *rest):
        da_ref, db_ref, dw_ref, acc = rest[len(after):]
        i = pl.program_id(0)

        @pl.when(i == 0)
        def _():
            acc[...] = jnp.zeros_like(acc)

        dyb16 = dy_ref[...].astype(BF16)
        da_ref[...] = _dot_nt(dyb16, w_ref[0:na, :])
        db_ref[...] = _dot_nt(dyb16, w_ref[na:, :])
        acc[0:na, :] += _dot_tn(ya_ref[...].astype(BF16), dyb16)
        acc[na:, :] += _dot_tn(yb_ref[...].astype(BF16), dyb16)

        @pl.when(i == pl.num_programs(0) - 1)
        def _():
            dw_ref[...] = acc[...].astype(BF16)

    return pl.pallas_call(
        body, name="out_bwd",
        grid=(tp // tm,),
        in_specs=[pl.BlockSpec((tm, d), lambda i: (i, 0)), pl.BlockSpec((tm, na), lambda i: (i, 0)),
                  pl.BlockSpec((tm, nb), lambda i: (i, 0)), pl.BlockSpec((d, d), lambda i: (0, 0))]
        + [pl.BlockSpec(memory_space=pl.ANY)] * len(after),
        out_specs=[pl.BlockSpec((tm, na), lambda i: (i, 0)), pl.BlockSpec((tm, nb), lambda i: (i, 0)),
                   pl.BlockSpec((d, d), lambda i: (0, 0))],
        out_shape=[jax.ShapeDtypeStruct((tp, na), F32), jax.ShapeDtypeStruct((tp, nb), F32),
                   jax.ShapeDtypeStruct((d, d), BF16)],
        scratch_shapes=[pltpu.VMEM((d, d), F32)],
        compiler_params=_cparams(1),
    )(dy, ya, yb, w, *after)


def _loss_bwd(h, gfin, tgt, t_real, tm):
    tp, d = h.shape

    def body(h_ref, g_ref, t_ref, loss_ref, dh_ref, dg_ref):
        i = pl.program_id(0)

        @pl.when(i == 0)
        def _():
            loss_ref[...] = jnp.zeros_like(loss_ref)
            dg_ref[...] = jnp.zeros_like(dg_ref)

        rows = i * tm + lax.broadcasted_iota(jnp.int32, (tm, 1), 0)
        valid = (rows >= N_META) & (rows < t_real)
        rstd, xhat = _rms_stats(h_ref[...])
        g = g_ref[...]
        err = jnp.where(valid, xhat * g - t_ref[...], 0.0)
        e2 = jnp.sum(err * err, axis=1, keepdims=True)
        loss_ref[...] += (0.5 / d) * jnp.sum(e2, axis=0, keepdims=True)
        dy = err * (1.0 / d)
        dh, dg = _rms_bwd(dy, g, rstd, xhat)
        dh_ref[...] = dh
        dg_ref[...] += dg

    return pl.pallas_call(
        body, name="loss_bwd",
        grid=(tp // tm,),
        in_specs=[pl.BlockSpec((tm, d), lambda i: (i, 0)), pl.BlockSpec((1, d), lambda i: (0, 0)),
                  pl.BlockSpec((tm, d), lambda i: (i, 0))],
        out_specs=[pl.BlockSpec((1, 1), lambda i: (0, 0)), pl.BlockSpec((tm, d), lambda i: (i, 0)),
                   pl.BlockSpec((1, d), lambda i: (0, 0))],
        out_shape=[jax.ShapeDtypeStruct((1, 1), F32), jax.ShapeDtypeStruct((tp, d), F32),
                   jax.ShapeDtypeStruct((1, d), F32)],
        compiler_params=_cparams(1),
    )(h, gfin, tgt)


POOL_HALO = 16


def _pool_lane_consts(n_rows):
    lane = lax.broadcasted_iota(jnp.int32, (n_rows, D_POOL), 1)
    grp = lane // POOL_GROUP
    win = jnp.where(grp == 0, 2.0, jnp.where(grp == 1, 4.0, jnp.where(grp == 2, 8.0, 16.0)))
    return grp, win


def _pool_select(grp, s2, s4, s8, s16):
    return jnp.where(grp == 0, s2, jnp.where(grp == 1, s4, jnp.where(grp == 2, s8, s16)))


def _pool_mixed(x, row0, tr):
    n = tr + POOL_HALO
    s2 = x + pltpu.roll(x, 1, 0)
    s4 = s2 + pltpu.roll(s2, 2, 0)
    s8 = s4 + pltpu.roll(s4, 4, 0)
    s16 = s8 + pltpu.roll(s8, 8, 0)
    grp, win = _pool_lane_consts(n)
    rows = row0 - POOL_HALO + lax.broadcasted_iota(jnp.int32, (n, D_POOL), 0)
    cnt = jnp.minimum((rows + 1).astype(F32), win)
    pooled = _pool_select(grp, s2, s4, s8, s16) / jnp.maximum(cnt, 1.0)
    return (pooled - x)[POOL_HALO:, :]


def _pool_fwd(p, wbd, scale, tr):
    tp = p.shape[0]
    nt = tp // tr

    def body(p_ref, w_ref, s_ref, y_ref, usc):
        usc[0:POOL_HALO, :] = jnp.zeros((POOL_HALO, D_POOL), F32)
        usc[POOL_HALO:, :] = p_ref[...]

        def tile(r, carry):
            r0 = pl.multiple_of(r * tr, SUBLANE)
            x = usc[pl.ds(r0, tr + POOL_HALO), :]
            mixed = _pool_mixed(x, r0, tr)
            y_ref[pl.ds(r0, tr), :] = (_dot(mixed.astype(BF16), w_ref[...]) * s_ref[...]).astype(BF16)
            return carry

        lax.fori_loop(0, nt, tile, 0)

    return pl.pallas_call(
        body, name="pool_fwd",
        grid=(1,),
        in_specs=[pl.BlockSpec((tp, D_POOL), lambda i: (0, 0)), pl.BlockSpec((D_POOL, D_POOL), lambda i: (0, 0)),
                  pl.BlockSpec((1, D_POOL), lambda i: (0, 0))],
        out_specs=pl.BlockSpec((tp, D_POOL), lambda i: (0, 0)),
        out_shape=jax.ShapeDtypeStruct((tp, D_POOL), BF16),
        scratch_shapes=[pltpu.VMEM((tp + POOL_HALO, D_POOL), F32)],
        compiler_params=_cparams(1),
    )(p, wbd, scale)


def _pool_bwd(p, dya, wbd, scale, tr):
    tp = p.shape[0]
    nt = tp // tr

    def body(p_ref, dy_ref, w_ref, s_ref, du_ref, dw_ref, ds_ref, usc, gsc):
        usc[0:POOL_HALO, :] = jnp.zeros((POOL_HALO, D_POOL), F32)
        usc[POOL_HALO:, :] = p_ref[...]
        gsc[tp:, :] = jnp.zeros((POOL_HALO, D_POOL), F32)
        dw_ref[...] = jnp.zeros_like(dw_ref)
        ds_ref[...] = jnp.zeros_like(ds_ref)
        grp, win = _pool_lane_consts(tr)

        def tile1(r, carry):
            r0 = pl.multiple_of(r * tr, SUBLANE)
            x = usc[pl.ds(r0, tr + POOL_HALO), :]
            mixed = _pool_mixed(x, r0, tr).astype(BF16)
            dy = dy_ref[pl.ds(r0, tr), :]
            dys = (dy * s_ref[...]).astype(BF16)
            ypre = _dot(mixed, w_ref[...])
            ds_ref[...] += _colsum(dy * ypre)
            dw_ref[...] += _dot_tn(mixed, dys)
            dmx = _dot_nt(dys, w_ref[...])
            rows = r0 + lax.broadcasted_iota(jnp.int32, (tr, D_POOL), 0)
            cnt = jnp.minimum((rows + 1).astype(F32), win)
            gsc[pl.ds(r0, tr), :] = dmx / cnt
            return carry

        lax.fori_loop(0, nt, tile1, 0)
        n = tr + POOL_HALO
        grp2, win2 = _pool_lane_consts(n)

        def tile2(r, carry):
            r0 = pl.multiple_of(r * tr, SUBLANE)
            g = gsc[pl.ds(r0, n), :]
            s2 = g + pltpu.roll(g, n - 1, 0)
            s4 = s2 + pltpu.roll(s2, n - 2, 0)
            s8 = s4 + pltpu.roll(s4, n - 4, 0)
            s16 = s8 + pltpu.roll(s8, n - 8, 0)
            pooled_t = _pool_select(grp2, s2, s4, s8, s16)
            rows = r0 + lax.broadcasted_iota(jnp.int32, (n, D_POOL), 0)
            cnt = jnp.minimum((rows + 1).astype(F32), win2)
            du = pooled_t - g * cnt
            du_ref[pl.ds(r0, tr), :] = du[0:tr, :].astype(BF16)
            return carry

        lax.fori_loop(0, nt, tile2, 0)

    return pl.pallas_call(
        body, name="pool_bwd",
        grid=(1,),
        in_specs=[pl.BlockSpec((tp, D_POOL), lambda i: (0, 0)), pl.BlockSpec((tp, D_POOL), lambda i: (0, 0)),
                  pl.BlockSpec((D_POOL, D_POOL), lambda i: (0, 0)), pl.BlockSpec((1, D_POOL), lambda i: (0, 0))],
        out_specs=[pl.BlockSpec((tp, D_POOL), lambda i: (0, 0)), pl.BlockSpec((D_POOL, D_POOL), lambda i: (0, 0)),
                   pl.BlockSpec((1, D_POOL), lambda i: (0, 0))],
        out_shape=[jax.ShapeDtypeStruct((tp, D_POOL), BF16), jax.ShapeDtypeStruct((D_POOL, D_POOL), F32),
                   jax.ShapeDtypeStruct((1, D_POOL), F32)],
        scratch_shapes=[pltpu.VMEM((tp + POOL_HALO, D_POOL), F32), pltpu.VMEM((tp + POOL_HALO, D_POOL), F32)],
        compiler_params=_cparams(1),
    )(p, dya, wbd, scale)


def _hgrn_levels(ch):
    levels = []
    w = ch // 2
    while w >= 1:
        levels.append(w)
        w //= 2
    return levels


def _hgrn_consts(ch):
    t = np.arange(ch)
    tril = t[None, :] <= t[:, None]
    masks = []
    for w in _hgrn_levels(ch):
        blk = t // (2 * w)
        upper = t % (2 * w) >= w
        masks.append(upper[:, None] & (~upper)[None, :] & (blk[:, None] == blk[None, :]))
    masks.append(tril)
    msk = np.stack(masks).astype(np.float32)
    return jnp.asarray(tril.astype(np.float32), BF16), jnp.asarray(msk, F32), len(masks) - 1


def _split3(x):
    hi = x.astype(BF16)
    r1 = x - hi.astype(F32)
    mid = r1.astype(BF16)
    lo = (r1 - mid.astype(F32)).astype(BF16)
    return hi, mid, lo


def _hgrn_exponents(tril, logf):
    ch = logf.shape[0]
    hi, mid, lo = _split3(logf)
    x = _dot(tril, jnp.concatenate([hi, mid, lo], axis=1))
    b = x[:, 0:HEAD] + x[:, HEAD:2 * HEAD] + x[:, 2 * HEAD:3 * HEAD]
    rows = lax.broadcasted_iota(jnp.int32, (ch, HEAD), 0)
    fx = jnp.broadcast_to(b[ch - 1:ch, :], (ch, HEAD)) - b
    lev = []
    for w in _hgrn_levels(ch):
        pos = rows % (2 * w)
        upper = pos >= w
        if w >= SUBLANE:
            parts = [jnp.broadcast_to(b[k * 2 * w + w - 1:k * 2 * w + w, :], (2 * w, HEAD))
                     for k in range(ch // (2 * w))]
            bmid = parts[0] if len(parts) == 1 else jnp.concatenate(parts, axis=0)
            dx = jnp.where(upper, b - bmid, 0.0)
            ex = jnp.where(upper, 0.0, bmid - b)
        else:
            dx = logf
            ex = jnp.zeros_like(logf)
            for i in range(1, w):
                dx = dx + jnp.where(pos >= w + i, pltpu.roll(logf, i, 0), 0.0)
                ex = ex + jnp.where(pos <= w - 1 - i, pltpu.roll(logf, ch - i, 0), 0.0)
            dx = jnp.where(upper, dx, 0.0)
        lev.append((dx, ex))
    return b, fx, lev


def _hgrn_exponents_bwd(tril, d_b, d_fx, d_blast, lev_grads):
    ch = d_b.shape[0]
    rows = lax.broadcasted_iota(jnp.int32, (ch, HEAD), 0)
    db = d_b - d_fx
    dlf = jnp.zeros_like(d_b)
    for w, (ddx, dex) in zip(_hgrn_levels(ch), lev_grads):
        pos = rows % (2 * w)
        upper = pos >= w
        gu = jnp.where(upper, ddx, 0.0)
        if w >= SUBLANE:
            gl = jnp.where(upper, 0.0, dex)
            db = db + gu - gl
            diff = gl - gu
            for k in range(ch // (2 * w)):
                s = _colsum(diff[k * 2 * w:(k + 1) * 2 * w, :])
                db = db + jnp.where(rows == k * 2 * w + w - 1, s, 0.0)
        else:
            dlf = dlf + gu
            for i in range(1, w):
                dlf = dlf + pltpu.roll(jnp.where(pos >= w + i, gu, 0.0), ch - i, 0)
                dlf = dlf + pltpu.roll(jnp.where(pos <= w - 1 - i, dex, 0.0), i, 0)
    db = db + jnp.where(rows == ch - 1, _colsum(d_fx) + d_blast, 0.0)
    hi = db.astype(BF16)
    lo = (db - hi.astype(F32)).astype(BF16)
    d2 = _dot_tn(tril, jnp.concatenate([hi, lo], axis=1))
    return d2[:, 0:HEAD] + d2[:, HEAD:2 * HEAD] + dlf


def _lockstep(gens):
    results = [None] * len(gens)
    live = list(range(len(gens)))
    while live:
        for i in list(live):
            try:
                next(gens[i])
            except StopIteration as stop:
                results[i] = stop.value
                live.remove(i)
    return results


def _hgrn_gates(q_raw, z, lb):
    sz = _sigmoid(z)
    f = lb + (1.0 - lb) * sz
    q = q_raw * _sigmoid(q_raw)
    k = (1.0 - lb) * (1.0 - sz)
    return q, k, f, sz


def _hgrn_intra(q, k, lev, msk_ref, n_lev, ch):
    eye = (lax.broadcasted_iota(jnp.int32, (ch, ch), 0) == lax.broadcasted_iota(jnp.int32, (ch, ch), 1))
    a = jnp.where(eye, jnp.sum(q * k, axis=1, keepdims=True), 0.0)
    ops = []
    for lv in range(n_lev):
        eq = jnp.exp(lev[lv][0])
        ek = jnp.exp(lev[lv][1])
        qd = q * eq
        kd = k * ek
        a = a + msk_ref[lv] * _dot_nt(qd.astype(BF16), kd.astype(BF16))
        ops.append((eq, ek, qd, kd))
        yield
    return a, ops


def _hgrn_fwd(p, lb_logits, gnorm, mst, msk, n_lev, tm):
    tp = p.shape[0]
    ch = HG_CHUNK
    nct = tm // ch
    nt = tp // tm
    nr = mst.shape[0]
    base = D_POOL // HEAD

    hp = HG_HEADS_PER_STEP
    wide = hp * HEAD
    npr = wide // HG_PBLOCK

    def body(*refs):
        p_refs = refs[:4 * npr]
        lg_ref, gn_ref, mst_ref, msk_ref, y_ref, ss_ref, st_sc = refs[4 * npr:]

        @pl.when(pl.program_id(1) == 0)
        def _():
            st_sc[...] = jnp.zeros_like(st_sc)

        lb_all = _sigmoid(lg_ref[0:1, :] - lg_ref[1:2, :])

        def raw(seg, hh, r0):
            per = HG_PBLOCK // HEAD
            return p_refs[seg * npr + hh // per][pl.ds(r0, ch), (hh % per) * HEAD:(hh % per + 1) * HEAD]

        def one_head(hh, c, r0):
            ls = slice(hh * HEAD, (hh + 1) * HEAD)
            q_raw, z, v, g_raw, st = raw(0, hh, r0), raw(1, hh, r0), raw(2, hh, r0), raw(3, hh, r0), st_sc[hh]
            q, k, f, _ = _hgrn_gates(q_raw, z, lb_all[:, ls])
            yield
            b, fx, lev = _hgrn_exponents(mst_ref[...], jnp.log(f))
            yield
            qe = q * jnp.exp(b)
            a, _ = yield from _hgrn_intra(q, k, lev, msk_ref, n_lev, ch)
            v16 = v.astype(BF16)
            o = _dot_nt(qe.astype(BF16), st.astype(BF16)) + _dot(a.astype(BF16), v16)
            kl = k * jnp.exp(fx)
            st_new = st * jnp.exp(b[ch - 1:ch, :]) + _dot_tn(v16, kl.astype(BF16))
            yield
            rstd = lax.rsqrt(jnp.mean(o * o, axis=-1, keepdims=True) + EPS)
            return st, st_new, o * rstd * gn_ref[...] * (g_raw * _sigmoid(g_raw))

        def chunk(c, carry):
            r0 = pl.multiple_of(c * ch, ch)
            results = _lockstep([one_head(hh, c, r0) for hh in range(hp)])
            for hh, (st, st_new, y) in enumerate(results):
                ss_ref[hh, c] = st
                st_sc[hh] = st_new
                y_ref[pl.ds(r0, ch), hh * HEAD:(hh + 1) * HEAD] = y.astype(BF16)
            return carry

        lax.fori_loop(0, nct, chunk, 0)

    def pspec(seg, part):
        return pl.BlockSpec((tm, HG_PBLOCK),
                            lambda h, i: (i, (base + seg * HG_HEADS) * HEAD // HG_PBLOCK + h * npr + part))

    return pl.pallas_call(
        body, name="hgrn_fwd",
        grid=(HG_HEADS // hp, nt),
        in_specs=[pspec(seg, part) for seg in range(4) for part in range(npr)]
        + [pl.BlockSpec((2, wide), lambda h, i: (0, h)),
           pl.BlockSpec((1, HEAD), lambda h, i: (0, 0)),
           pl.BlockSpec((nr, ch), lambda h, i: (0, 0)),
           pl.BlockSpec((n_lev + 1, ch, ch), lambda h, i: (0, 0, 0))],
        out_specs=[pl.BlockSpec((tm, wide), lambda h, i: (i, h)),
                   pl.BlockSpec((hp, nct, HEAD, HEAD), lambda h, i: (h, i, 0, 0))],
        out_shape=[jax.ShapeDtypeStruct((tp, D_HGRN), BF16),
                   jax.ShapeDtypeStruct((HG_HEADS, tp // ch, HEAD, HEAD), F32)],
        scratch_shapes=[pltpu.VMEM((hp, HEAD, HEAD), F32)],
        compiler_params=_cparams(2),
    )(*([p] * (4 * npr)), lb_logits, gnorm, mst, msk)


def _hgrn_bwd(p, dyb, states, lb_logits, gnorm, mst, msk, n_lev, tm):
    tp = p.shape[0]
    ch = HG_CHUNK
    nct = tm // ch
    nt = tp // tm
    nr = mst.shape[0]
    base = D_POOL // HEAD

    hp = HG_HEADS_PER_STEP
    wide = hp * HEAD
    npr = wide // HG_PBLOCK

    def body(*refs):
        p_refs = refs[:4 * npr]
        (dy_ref, ss_ref, lg_ref, gn_ref, mst_ref, msk_ref,
         dq_ref, dz_ref, dv_ref, dg_ref, dlg_ref, dgn_ref, dst_sc, dlb_sc) = refs[4 * npr:]
        ti = pl.program_id(1)

        def raw(seg, hh, r0):
            per = HG_PBLOCK // HEAD
            return p_refs[seg * npr + hh // per][pl.ds(r0, ch), (hh % per) * HEAD:(hh % per + 1) * HEAD]

        @pl.when(ti == 0)
        def _():
            dst_sc[...] = jnp.zeros_like(dst_sc)
            dlb_sc[...] = jnp.zeros_like(dlb_sc)
            dgn_ref[...] = jnp.zeros_like(dgn_ref)

        lb_all = _sigmoid(lg_ref[0:1, :] - lg_ref[1:2, :])
        gn = gn_ref[...]

        def load_head(hh, c, r0):
            ls = slice(hh * HEAD, (hh + 1) * HEAD)
            return (raw(0, hh, r0), raw(1, hh, r0), raw(2, hh, r0), raw(3, hh, r0),
                    dy_ref[pl.ds(r0, ch), ls], ss_ref[hh, c], dst_sc[hh])

        def store_head(hh, r0, res):
            ls = slice(hh * HEAD, (hh + 1) * HEAD)
            dq_raw, dz, dv, dg_raw, dgn, dst_new, dlb = res
            dq_ref[pl.ds(r0, ch), ls] = dq_raw
            dz_ref[pl.ds(r0, ch), ls] = dz
            dv_ref[pl.ds(r0, ch), ls] = dv
            dg_ref[pl.ds(r0, ch), ls] = dg_raw
            dgn_ref[hh] += dgn
            dst_sc[hh] = dst_new
            dlb_sc[:, ls] += dlb

        def one_head(hh, loaded):
            ls = slice(hh * HEAD, (hh + 1) * HEAD)
            lb = lb_all[:, ls]
            q_raw, z, v, g_raw, dy, st, dst = loaded
            q, k, f, sz = _hgrn_gates(q_raw, z, lb)
            yield
            b, fx, lev = _hgrn_exponents(mst_ref[...], jnp.log(f))
            yield
            eb = jnp.exp(b)
            ef = jnp.exp(fx)
            elast = jnp.exp(b[ch - 1:ch, :])
            qe = q * eb
            kl = k * ef
            a, ops = yield from _hgrn_intra(q, k, lev, msk_ref, n_lev, ch)
            v16 = v.astype(BF16)
            st16 = st.astype(BF16)
            qe16 = qe.astype(BF16)
            kl16 = kl.astype(BF16)
            a16 = a.astype(BF16)
            o = _dot_nt(qe16, st16) + _dot(a16, v16)
            yield
            sg = _sigmoid(g_raw)
            rstd = lax.rsqrt(jnp.mean(o * o, axis=-1, keepdims=True) + EPS)
            oh = o * rstd
            dg_out = (dy * oh * gn * (sg * (1.0 + g_raw * (1.0 - sg)))).astype(BF16)
            don = dy * (g_raw * sg)
            dgn = _colsum(don * oh)
            doh = don * gn
            do = rstd * (doh - oh * jnp.mean(doh * oh, axis=-1, keepdims=True))
            do16 = do.astype(BF16)
            dst16 = dst.astype(BF16)
            yield
            dv = _dot_tn(a16, do16) + _dot_nt(kl16, dst16)
            da = msk_ref[n_lev] * _dot_nt(do16, v16)
            dqe = _dot(do16, st16)
            dkl = _dot(v16, dst16)
            dst_new = dst * elast + _dot_tn(do16, qe16)
            yield
            db_last = _colsum(dst * st) * elast
            dad = jnp.sum(do * v, axis=1, keepdims=True)
            dq = dad * k + dqe * eb
            dk = dad * q + dkl * ef
            lev_grads = []
            for lv in range(n_lev):
                eq, ek, qd, kd = ops[lv]
                gl = (msk_ref[lv] * da).astype(BF16)
                dqd = _dot(gl, kd.astype(BF16))
                dkd = _dot_tn(gl, qd.astype(BF16))
                dq = dq + dqd * eq
                dk = dk + dkd * ek
                lev_grads.append((dqd * qd, dkd * kd))
                yield
            dlogf = _hgrn_exponents_bwd(mst_ref[...], dqe * qe, dkl * kl, db_last, lev_grads)
            yield
            sq = _sigmoid(q_raw)
            dq_out = (dq * (sq * (1.0 + q_raw * (1.0 - sq)))).astype(BF16)
            dfk = dlogf / f - dk
            dz_out = (dfk * (1.0 - lb) * sz * (1.0 - sz)).astype(BF16)
            return dq_out, dz_out, dv.astype(BF16), dg_out, dgn, dst_new, _colsum(dfk * (1.0 - sz))

        def chunk(cc, carry):
            c = nct - 1 - cc
            r0 = pl.multiple_of(c * ch, ch)
            loaded = [load_head(hh, c, r0) for hh in range(HG_HEADS_PER_STEP)]
            results = _lockstep([one_head(hh, loaded[hh]) for hh in range(HG_HEADS_PER_STEP)])
            for hh in range(HG_HEADS_PER_STEP):
                store_head(hh, r0, results[hh])
            return carry

        lax.fori_loop(0, nct, chunk, 0, unroll=1)

        @pl.when(ti == nt - 1)
        def _():
            dl0 = dlb_sc[...] * lb_all * (1.0 - lb_all)
            dlg_ref[0:1, :] = dl0
            dlg_ref[1:2, :] = -dl0

    def pspec(seg, part):
        return pl.BlockSpec((tm, HG_PBLOCK), lambda h, i: (
            nt - 1 - i, (base + seg * HG_HEADS) * HEAD // HG_PBLOCK + h * npr + part))

    ospec = pl.BlockSpec((tm, wide), lambda h, i: (nt - 1 - i, h))
    return pl.pallas_call(
        body, name="hgrn_bwd",
        grid=(HG_HEADS // hp, nt),
        in_specs=[pspec(seg, part) for seg in range(4) for part in range(npr)]
        + [ospec, pl.BlockSpec((hp, nct, HEAD, HEAD), lambda h, i: (h, nt - 1 - i, 0, 0)),
           pl.BlockSpec((2, wide), lambda h, i: (0, h)),
           pl.BlockSpec((1, HEAD), lambda h, i: (0, 0)),
           pl.BlockSpec((nr, ch), lambda h, i: (0, 0)),
           pl.BlockSpec((n_lev + 1, ch, ch), lambda h, i: (0, 0, 0))],
        out_specs=[ospec, ospec, ospec, ospec,
                   pl.BlockSpec((2, wide), lambda h, i: (0, h)),
                   pl.BlockSpec((hp, 1, HEAD), lambda h, i: (h, 0, 0))],
        out_shape=[jax.ShapeDtypeStruct((tp, D_HGRN), BF16)] * 4
        + [jax.ShapeDtypeStruct((2, D_HGRN), F32), jax.ShapeDtypeStruct((HG_HEADS, 1, HEAD), F32)],
        scratch_shapes=[pltpu.VMEM((hp, HEAD, HEAD), F32), pltpu.VMEM((1, wide), F32)],
        compiler_params=_cparams(2),
    )(*([p] * (4 * npr)), dyb, states, lb_logits, gnorm, mst, msk)


def _tap_views(x, tr, halo, width):
    subs = {0: x}
    views = []
    for j in range(width):
        tiles, rem = divmod(width - 1 - j, SUBLANE)
        if rem not in subs:
            subs[rem] = pltpu.roll(x, rem, 0)
        start = halo - tiles * SUBLANE
        views.append(subs[rem][start:start + tr, :])
    return views


def _tap_views_t(y, tr, halo, width):
    n = tr + halo
    subs = {0: y}
    views = []
    for j in range(width):
        tiles, rem = divmod(width - 1 - j, SUBLANE)
        if rem not in subs:
            subs[rem] = pltpu.roll(y, n - rem, 0)
        views.append(subs[rem][tiles * SUBLANE:tiles * SUBLANE + tr, :])
    return views


def _weighted_sum(views, w_ref):
    acc = None
    for j, view in enumerate(views):
        term = view * w_ref[j:j + 1, :]
        acc = term if acc is None else acc + term
    return acc


def _conv_taps(x, w_ref, tr, halo, width):
    return _weighted_sum(_tap_views(x, tr, halo, width), w_ref)


def _conv_taps_t(y, w_ref, tr, halo, width):
    return _weighted_sum(_tap_views_t(y, tr, halo, width), w_ref)


def _ln_stats(cv):
    mu = jnp.mean(cv, axis=-1, keepdims=True)
    xc = cv - mu
    rstd = lax.rsqrt(jnp.mean(xc * xc, axis=-1, keepdims=True) + EPS)
    return rstd, xc * rstd


def _convmod_fwd(p, w, bias, ln_g, ln_b, tr):
    tp = p.shape[0]
    nt = tp // tr
    nb = D_CONV // HEAD

    def body(a_ref, b_ref, w_ref, bi_ref, g_ref, be_ref, y_ref, cv_ref, usc):
        usc[0:CONV_HALO, :] = jnp.zeros((CONV_HALO, HEAD), F32)
        usc[CONV_HALO:, :] = a_ref[...] * _sigmoid(b_ref[...])

        def tile(r, carry):
            r0 = pl.multiple_of(r * tr, SUBLANE)
            x = usc[pl.ds(r0, tr + CONV_HALO), :]
            cv = _conv_taps(x, w_ref, tr, CONV_HALO, CONV_WIDTH) + bi_ref[...]
            cv_ref[pl.ds(r0, tr), :] = cv
            _, xh = _ln_stats(cv)
            un = xh * g_ref[...] + be_ref[...]
            y_ref[pl.ds(r0, tr), :] = (un * _sigmoid(un)).astype(BF16)
            return carry

        lax.fori_loop(0, nt, tile, 0)

    vec = lambda: pl.BlockSpec((1, HEAD), lambda j: (0, j))
    return pl.pallas_call(
        body, name="convmod_fwd",
        grid=(nb,),
        in_specs=[pl.BlockSpec((tp, HEAD), lambda j: (0, j)), pl.BlockSpec((tp, HEAD), lambda j: (0, nb + j)),
                  pl.BlockSpec((CONV_HALO, HEAD), lambda j: (0, j)), vec(), vec(), vec()],
        out_specs=[pl.BlockSpec((tp, HEAD), lambda j: (0, j))] * 2,
        out_shape=[jax.ShapeDtypeStruct((tp, D_CONV), BF16), jax.ShapeDtypeStruct((tp, D_CONV), F32)],
        scratch_shapes=[pltpu.VMEM((tp + CONV_HALO, HEAD), F32)],
        compiler_params=_cparams(1),
    )(p, p, w, bias, ln_g, ln_b)


def _convmod_bwd(p, cv_saved, dyc, w, ln_g, ln_b, tr):
    tp = p.shape[0]
    nt = tp // tr
    nb = D_CONV // HEAD

    def body(a_ref, b_ref, cv_ref, dy_ref, w_ref, g_ref, be_ref, da_ref, db_ref, dw_ref, dv_ref, usc, dsc):
        usc[0:CONV_HALO, :] = jnp.zeros((CONV_HALO, HEAD), F32)
        usc[CONV_HALO:, :] = a_ref[...] * _sigmoid(b_ref[...])
        dsc[tp:, :] = jnp.zeros((CONV_HALO, HEAD), F32)
        dw_ref[...] = jnp.zeros_like(dw_ref)
        dv_ref[...] = jnp.zeros_like(dv_ref)

        def tile1(r, carry):
            r0 = pl.multiple_of(r * tr, SUBLANE)
            x = usc[pl.ds(r0, tr + CONV_HALO), :]
            views = _tap_views(x, tr, CONV_HALO, CONV_WIDTH)
            rstd, xh = _ln_stats(cv_ref[pl.ds(r0, tr), :])
            un = xh * g_ref[...] + be_ref[...]
            sg = _sigmoid(un)
            dun = dy_ref[pl.ds(r0, tr), :] * (sg * (1.0 + un * (1.0 - sg)))
            dv_ref[0, 1:2, :] += _colsum(dun * xh)
            dv_ref[0, 2:3, :] += _colsum(dun)
            dxh = dun * g_ref[...]
            dcv = rstd * (dxh - jnp.mean(dxh, axis=-1, keepdims=True)
                          - xh * jnp.mean(dxh * xh, axis=-1, keepdims=True))
            dv_ref[0, 0:1, :] += _colsum(dcv)
            for j in range(CONV_WIDTH):
                dw_ref[0, j:j + 1, :] += _colsum(dcv * views[j])
            dsc[pl.ds(r0, tr), :] = dcv
            return carry

        lax.fori_loop(0, nt, tile1, 0)

        def tile2(r, carry):
            r0 = pl.multiple_of(r * tr, SUBLANE)
            y = dsc[pl.ds(r0, tr + CONV_HALO), :]
            du = _conv_taps_t(y, w_ref, tr, CONV_HALO, CONV_WIDTH)
            a = a_ref[pl.ds(r0, tr), :]
            sb = _sigmoid(b_ref[pl.ds(r0, tr), :])
            da_ref[pl.ds(r0, tr), :] = (du * sb).astype(BF16)
            db_ref[pl.ds(r0, tr), :] = (du * a * sb * (1.0 - sb)).astype(BF16)
            return carry

        lax.fori_loop(0, nt, tile2, 0)

    vec = lambda: pl.BlockSpec((1, HEAD), lambda j: (0, j))
    col = lambda: pl.BlockSpec((tp, HEAD), lambda j: (0, j))
    return pl.pallas_call(
        body, name="convmod_bwd",
        grid=(nb,),
        in_specs=[col(), pl.BlockSpec((tp, HEAD), lambda j: (0, nb + j)), col(), col(),
                  pl.BlockSpec((CONV_HALO, HEAD), lambda j: (0, j)), vec(), vec()],
        out_specs=[col(), col(), pl.BlockSpec((1, CONV_HALO, HEAD), lambda j: (j, 0, 0)),
                   pl.BlockSpec((1, SUBLANE, HEAD), lambda j: (j, 0, 0))],
        out_shape=[jax.ShapeDtypeStruct((tp, D_CONV), BF16), jax.ShapeDtypeStruct((tp, D_CONV), BF16),
                   jax.ShapeDtypeStruct((nb, CONV_HALO, HEAD), F32), jax.ShapeDtypeStruct((nb, SUBLANE, HEAD), F32)],
        scratch_shapes=[pltpu.VMEM((tp + CONV_HALO, HEAD), F32), pltpu.VMEM((tp + CONV_HALO, HEAD), F32)],
        compiler_params=_cparams(1),
    )(p, p, cv_saved, dyc, w, ln_g, ln_b)


def _log1p_small(y):
    return jnp.where(y < 1e-4, y * (1.0 - 0.5 * y), jnp.log(1.0 + y))


def _softplus(x):
    return jnp.maximum(x, 0.0) + _log1p_small(jnp.exp(-jnp.abs(x)))


def _expm1(x):
    return jnp.where(jnp.abs(x) < 1e-2, x * (1.0 + 0.5 * x * (1.0 + x * (1.0 / 3.0))), jnp.exp(x) - 1.0)


def _gelu_parts(x):
    c = 0.7978845608028654
    inner = c * (x + 0.044715 * x * x * x)
    th = jnp.tanh(inner)
    gelu = 0.5 * x * (1.0 + th)
    dgelu = 0.5 * (1.0 + th) + 0.5 * x * (1.0 - th * th) * c * (1.0 + 3.0 * 0.044715 * x * x)
    return gelu, dgelu


def _lru_gates(x_all, tp, cw_ref, cb_ref, wa_ref, ba_ref, wx_ref, bx_ref, lam_ref):
    u = _conv_taps(x_all, cw_ref, tp, LRU_HALO, LRU_CONV) + cb_ref[...]
    u16 = u.astype(BF16)
    r = _sigmoid(_dot(u16, wa_ref[0]) + ba_ref[...])
    i = _sigmoid(_dot(u16, wx_ref[0]) + bx_ref[...])
    sp = _softplus(-lam_ref[...])
    la = -LRU_C * r * sp
    a = jnp.exp(la)
    mult = jnp.sqrt(-_expm1(2.0 * la))
    return u, r, i, a, mult, sp


def _lru_specs(tp, nb):
    col = lambda k: pl.BlockSpec((tp, HEAD), functools.partial(lambda j, k: (0, k * nb + j), k=k))
    vec = lambda: pl.BlockSpec((1, HEAD), lambda j: (0, j))
    mat = lambda: pl.BlockSpec((1, HEAD, HEAD), lambda j: (j, 0, 0))
    return col, vec, mat


def _lru_fwd(p, cw, cb, wa, ba, wx, bx, lam):
    tp = p.shape[0]
    nb = D_LRU // HEAD
    ng = tp // SUBLANE

    def body(x_ref, gt_ref, cw_ref, cb_ref, wa_ref, ba_ref, wx_ref, bx_ref, lam_ref, y_ref, hs_ref,
             xsc, asc, bsc):
        xsc[0:LRU_HALO, :] = jnp.zeros((LRU_HALO, HEAD), F32)
        xsc[LRU_HALO:, :] = x_ref[...]
        u, r, i, a, mult, _ = _lru_gates(xsc[...], tp, cw_ref, cb_ref, wa_ref, ba_ref, wx_ref, bx_ref, lam_ref)
        rows = lax.broadcasted_iota(jnp.int32, (tp, HEAD), 0)
        b = jnp.where(rows == 0, 1.0, mult) * (i * u)
        sub = rows % SUBLANE
        for k in (1, 2, 4):
            m = sub >= k
            b = jnp.where(m, a * pltpu.roll(b, k, 0) + b, b)
            a = jnp.where(m, a * pltpu.roll(a, k, 0), a)
        asc[...] = a
        bsc[...] = b

        def grp(g, carry):
            r0 = pl.multiple_of(g * SUBLANE, SUBLANE)
            h = bsc[pl.ds(r0, SUBLANE), :] + asc[pl.ds(r0, SUBLANE), :] * carry
            hs_ref[pl.ds(r0, SUBLANE), :] = h
            return jnp.broadcast_to(h[SUBLANE - 1:SUBLANE, :], (SUBLANE, HEAD))

        lax.fori_loop(0, ng, grp, jnp.zeros((SUBLANE, HEAD), F32))
        gelu, _ = _gelu_parts(gt_ref[...])
        y_ref[...] = (gelu * hs_ref[...]).astype(BF16)

    col, vec, mat = _lru_specs(tp, nb)
    return pl.pallas_call(
        body, name="lru_fwd",
        grid=(nb,),
        in_specs=[col(2), col(3), pl.BlockSpec((LRU_CONV, HEAD), lambda j: (0, j)), vec(), mat(), vec(), mat(),
                  vec(), vec()],
        out_specs=[pl.BlockSpec((tp, HEAD), lambda j: (0, j)), pl.BlockSpec((tp, HEAD), lambda j: (0, j))],
        out_shape=[jax.ShapeDtypeStruct((tp, D_LRU), BF16), jax.ShapeDtypeStruct((tp, D_LRU), F32)],
        scratch_shapes=[pltpu.VMEM((tp + LRU_HALO, HEAD), F32), pltpu.VMEM((tp, HEAD), F32),
                        pltpu.VMEM((tp, HEAD), F32)],
        compiler_params=_cparams(1),
    )(p, p, cw, cb, wa, ba, wx, bx, lam)


def _lru_bwd(p, hs, dyd, cw, cb, wa, ba, wx, bx, lam):
    tp = p.shape[0]
    nb = D_LRU // HEAD
    ng = tp // SUBLANE

    def body(x_ref, gt_ref, hs_ref, dy_ref, cw_ref, cb_ref, wa_ref, ba_ref, wx_ref, bx_ref, lam_ref,
             dx_ref, dgt_ref, dwa_ref, dwx_ref, dv_ref, xsc, asc, bsc, gsc, dusc):
        xsc[0:LRU_HALO, :] = jnp.zeros((LRU_HALO, HEAD), F32)
        xsc[LRU_HALO:, :] = x_ref[...]
        x_all = xsc[...]
        u, r, i, a, mult, sp = _lru_gates(x_all, tp, cw_ref, cb_ref, wa_ref, ba_ref, wx_ref, bx_ref, lam_ref)
        rows = lax.broadcasted_iota(jnp.int32, (tp, HEAD), 0)
        hs = hs_ref[...]
        dy = dy_ref[...]
        gelu, dgelu = _gelu_parts(gt_ref[...])
        dgt_ref[...] = (dy * hs * dgelu).astype(BF16)
        bb = dy * gelu
        aa = jnp.where(rows == tp - 1, 0.0, pltpu.roll(a, tp - 1, 0))
        sub = rows % SUBLANE
        for k in (1, 2, 4):
            m = sub < SUBLANE - k
            bb = jnp.where(m, aa * pltpu.roll(bb, tp - k, 0) + bb, bb)
            aa = jnp.where(m, aa * pltpu.roll(aa, tp - k, 0), aa)
        asc[...] = aa
        bsc[...] = bb

        def grp(gi, carry):
            g = ng - 1 - gi
            r0 = pl.multiple_of(g * SUBLANE, SUBLANE)
            gg = bsc[pl.ds(r0, SUBLANE), :] + asc[pl.ds(r0, SUBLANE), :] * carry
            gsc[pl.ds(r0, SUBLANE), :] = gg
            return jnp.broadcast_to(gg[0:1, :], (SUBLANE, HEAD))

        lax.fori_loop(0, ng, grp, jnp.zeros((SUBLANE, HEAD), F32))
        g = gsc[...]
        first = rows == 0
        hprev = jnp.where(first, 0.0, pltpu.roll(hs, 1, 0))
        iu = i * u
        d_iu = g * jnp.where(first, 1.0, mult)
        dmult_term = jnp.where(first, 0.0, g * iu * (-(a * a) / mult))
        dla = g * hprev * a + dmult_term
        dr = dla * (-LRU_C) * sp
        dv_ref[0, 7:8, :] = _colsum(dla * (LRU_C * r) * _sigmoid(-lam_ref[...]))
        dpr = dr * r * (1.0 - r)
        dpi = d_iu * u * i * (1.0 - i)
        dv_ref[0, 5:6, :] = _colsum(dpr)
        dv_ref[0, 6:7, :] = _colsum(dpi)
        u16 = u.astype(BF16)
        dpr16 = dpr.astype(BF16)
        dpi16 = dpi.astype(BF16)
        dwa_ref[0] = _dot_tn(u16, dpr16)
        dwx_ref[0] = _dot_tn(u16, dpi16)
        du = d_iu * i + _dot_nt(dpr16, wa_ref[0]) + _dot_nt(dpi16, wx_ref[0])
        dv_ref[0, 4:5, :] = _colsum(du)
        for j in range(LRU_CONV):
            sh = LRU_CONV - 1 - j
            xs = x_all if sh == 0 else pltpu.roll(x_all, sh, 0)
            dv_ref[0, j:j + 1, :] = _colsum(du * xs[LRU_HALO:, :])
        dusc[0:tp, :] = du
        dusc[tp:, :] = jnp.zeros((LRU_HALO, HEAD), F32)
        dx_ref[...] = _conv_taps_t(dusc[...], cw_ref, tp, LRU_HALO, LRU_CONV).astype(BF16)

    col, vec, mat = _lru_specs(tp, nb)
    ocol = lambda: pl.BlockSpec((tp, HEAD), lambda j: (0, j))
    return pl.pallas_call(
        body, name="lru_bwd",
        grid=(nb,),
        in_specs=[col(2), col(3), ocol(), ocol(), pl.BlockSpec((LRU_CONV, HEAD), lambda j: (0, j)), vec(), mat(),
                  vec(), mat(), vec(), vec()],
        out_specs=[ocol(), ocol(), mat(), mat(), pl.BlockSpec((1, SUBLANE, HEAD), lambda j: (j, 0, 0))],
        out_shape=[jax.ShapeDtypeStruct((tp, D_LRU), BF16), jax.ShapeDtypeStruct((tp, D_LRU), BF16),
                   jax.ShapeDtypeStruct((nb, HEAD, HEAD), F32), jax.ShapeDtypeStruct((nb, HEAD, HEAD), F32),
                   jax.ShapeDtypeStruct((nb, SUBLANE, HEAD), F32)],
        scratch_shapes=[pltpu.VMEM((tp + LRU_HALO, HEAD), F32), pltpu.VMEM((tp, HEAD), F32),
                        pltpu.VMEM((tp, HEAD), F32), pltpu.VMEM((tp, HEAD), F32),
                        pltpu.VMEM((tp + LRU_HALO, HEAD), F32)],
        compiler_params=_cparams(1),
    )(p, p, hs, dyd, cw, cb, wa, ba, wx, bx, lam)


def _mesh_pos():
    return lax.axis_index("x"), lax.axis_index("y"), lax.axis_index("c")


def _other_chips(x, y):
    return [(1 - x, y), (x, 1 - y), (1 - x, 1 - y)]


ANY = pl.BlockSpec(memory_space=pl.ANY)


HBM = pl.BlockSpec(memory_space=pltpu.HBM)
SEM = pl.BlockSpec(memory_space=pltpu.SEMAPHORE)
DATAFLOW = pltpu.SideEffectType.DATAFLOW_SIDE_EFFECTING
N_PEERS = 4


def _in_hbm(a):
    return pltpu.with_memory_space_constraint(a, pltpu.HBM)


def _gather_peers(x, y, c):
    return [((ox, oy, c), 2 * ox + oy) for ox, oy in _other_chips(x, y)] + [((x, y, 1 - c), 2 * x + y)]


def _gather_refs(src, land, slot, c, split):
    if not split:
        return src, land.at[slot]
    half = src.shape[0] // 2
    return src.at[pl.ds(c * half, half)], land.at[slot, pl.ds(c * half, half)]


def _gather_start(arrs, split):
    n = len(arrs)

    def body(*refs):
        ins, lands = refs[:n], refs[n:2 * n]
        ssem, rsem = refs[2 * n:2 * n + 2]
        token = refs[-1]
        x, y, c = _mesh_pos()
        chip = 2 * x + y
        for k in range(n):
            for j, (dev, _) in enumerate(_gather_peers(x, y, c)):
                src, dst = _gather_refs(ins[k], lands[k], chip, c, split[k] and j < N_PEERS - 1)
                pltpu.make_async_remote_copy(
                    src_ref=src, dst_ref=dst, send_sem=ssem.at[N_PEERS * k + j],
                    recv_sem=rsem.at[N_PEERS * k + j], device_id=dev, device_id_type=MESH).start()
        token[...] = jnp.zeros_like(token)

    lands = [_in_hbm(lax.empty((N_SHARD,) + a.shape, a.dtype)) for a in arrs]
    out = pl.pallas_call(
        body, name="gather_start",
        in_specs=[HBM] * (2 * n),
        out_specs=[SEM, SEM] + [HBM] * (2 * n) + [pl.BlockSpec(memory_space=pltpu.VMEM)],
        out_shape=[pltpu.SemaphoreType.DMA((N_PEERS * n,)), pltpu.SemaphoreType.DMA((N_PEERS * n,))]
        + [pltpu.HBM(a.shape, a.dtype) for a in arrs]
        + [pltpu.HBM((N_SHARD,) + a.shape, a.dtype) for a in arrs]
        + [jax.ShapeDtypeStruct((SUBLANE, LANE), F32)],
        input_output_aliases={k: 2 + k for k in range(2 * n)},
        compiler_params=pltpu.CompilerParams(has_side_effects=DATAFLOW),
    )(*[_in_hbm(a) for a in arrs], *lands)
    return out[0], out[1], list(out[2:2 + n]), list(out[2 + n:2 + 2 * n]), out[-1]


def _gather_wait(ssem, rsem, srcs, lands, ks, after, split=False):
    n = len(ks)

    def body(*refs):
        ins, lnd = refs[:n], refs[n:2 * n]
        ssem_ref, rsem_ref = refs[2 * n:2 * n + 2]
        x, y, c = _mesh_pos()
        for i, k in enumerate(ks):
            for j, (dev, pchip) in enumerate(_gather_peers(x, y, c)):
                src, dst = _gather_refs(ins[i], lnd[i], pchip, c, split and j < N_PEERS - 1)
                cp = pltpu.make_async_remote_copy(
                    src_ref=src, dst_ref=dst, send_sem=ssem_ref.at[N_PEERS * k + j],
                    recv_sem=rsem_ref.at[N_PEERS * k + j], device_id=dev, device_id_type=MESH)
                cp.wait_send()
                cp.wait_recv()

    out = pl.pallas_call(
        body, name="gather_wait",
        in_specs=[HBM] * (2 * n) + [SEM, SEM] + [ANY] * len(after),
        out_specs=[HBM] * (2 * n),
        out_shape=[pltpu.HBM(a.shape, a.dtype) for a in srcs] + [pltpu.HBM(a.shape, a.dtype) for a in lands],
        input_output_aliases={k: k for k in range(2 * n)},
        compiler_params=pltpu.CompilerParams(has_side_effects=DATAFLOW),
    )(*srcs, *lands, ssem, rsem, *after)
    return list(out[n:])


def _pair_forward(lands):
    n = len(lands)

    def body(*refs):
        outs = refs[n:2 * n]
        ssem, rsem = refs[2 * n:]
        x, y, c = _mesh_pos()
        sibling = (x, y, 1 - c)
        cps = []
        for k in range(n):
            half = lands[k].shape[1] // 2
            for j, (ox, oy) in enumerate(_other_chips(x, y)):
                mine = outs[k].at[2 * ox + oy, pl.ds(c * half, half)]
                cp = pltpu.make_async_remote_copy(src_ref=mine, dst_ref=mine, send_sem=ssem.at[3 * k + j],
                                                  recv_sem=rsem.at[3 * k + j], device_id=sibling, device_id_type=MESH)
                cp.start()
                cps.append(cp)
        for k in range(n):
            half = lands[k].shape[1] // 2
            for j, (ox, oy) in enumerate(_other_chips(x, y)):
                theirs = outs[k].at[2 * ox + oy, pl.ds((1 - c) * half, half)]
                pltpu.make_async_remote_copy(src_ref=theirs, dst_ref=theirs, send_sem=ssem.at[3 * k + j],
                                             recv_sem=rsem.at[3 * k + j], device_id=sibling,
                                             device_id_type=MESH).wait_recv()
        for cp in cps:
            cp.wait_send()

    return pl.pallas_call(
        body, name="pair_forward",
        in_specs=[ANY] * n, out_specs=[ANY] * n,
        out_shape=[jax.ShapeDtypeStruct(a.shape, a.dtype) for a in lands],
        scratch_shapes=[pltpu.SemaphoreType.DMA((3 * n,)), pltpu.SemaphoreType.DMA((3 * n,))],
        input_output_aliases={k: k for k in range(n)},
    )(*lands)


N_SOURCES = 7


def _reduce_peers(x, y, c):
    peers = []
    for ox, oy in _other_chips(x, y):
        for rel in range(2):
            peers.append(((ox, oy, c + rel - 2 * c * rel), 2 * ox + oy))
    peers.append(((x, y, 1 - c), 2 * x + y))
    return peers


def _reduce_start(arrs, slots):
    n = len(arrs)

    def body(*refs):
        ins, lands = refs[:n], refs[n:2 * n]
        ssem, rsem = refs[2 * n:2 * n + 2]
        token = refs[-1]
        x, y, c = _mesh_pos()
        me = 2 * (2 * x + y) + c
        for k in range(n):
            half = arrs[k].shape[1] // 2
            for p, (dev, ochip) in enumerate(_reduce_peers(x, y, c)):
                pltpu.make_async_remote_copy(
                    src_ref=ins[k].at[ochip, pl.ds(dev[2] * half, half)], dst_ref=lands[k].at[me],
                    send_sem=ssem.at[N_SOURCES * k + p], recv_sem=rsem.at[N_SOURCES * k + p],
                    device_id=dev, device_id_type=MESH).start()
        token[...] = jnp.zeros_like(token)

    out = pl.pallas_call(
        body, name="reduce_start",
        in_specs=[HBM] * (2 * n),
        out_specs=[SEM, SEM] + [HBM] * (2 * n) + [pl.BlockSpec(memory_space=pltpu.VMEM)],
        out_shape=[pltpu.SemaphoreType.DMA((N_SOURCES * n,)), pltpu.SemaphoreType.DMA((N_SOURCES * n,))]
        + [pltpu.HBM(a.shape, a.dtype) for a in arrs] + [pltpu.HBM(a.shape, a.dtype) for a in slots]
        + [jax.ShapeDtypeStruct((SUBLANE, LANE), F32)],
        input_output_aliases={k: 2 + k for k in range(2 * n)},
        compiler_params=pltpu.CompilerParams(has_side_effects=DATAFLOW),
    )(*[_in_hbm(a) for a in arrs], *[_in_hbm(a) for a in slots])
    return out[0], out[1], list(out[2:2 + n]), list(out[2 + n:2 + 2 * n]), out[-1]


def _reduce_wait(ssem, rsem, arrs, slots, after):
    n = len(arrs)

    def body(*refs):
        ins, lnd = refs[:n], refs[n:2 * n]
        ssem_ref, rsem_ref = refs[2 * n:2 * n + 2]
        x, y, c = _mesh_pos()
        for k in range(n):
            half = arrs[k].shape[1] // 2
            for p, (dev, ochip) in enumerate(_reduce_peers(x, y, c)):
                cp = pltpu.make_async_remote_copy(
                    src_ref=ins[k].at[ochip, pl.ds(dev[2] * half, half)], dst_ref=lnd[k].at[2 * ochip + dev[2]],
                    send_sem=ssem_ref.at[N_SOURCES * k + p], recv_sem=rsem_ref.at[N_SOURCES * k + p],
                    device_id=dev, device_id_type=MESH)
                cp.wait_send()
                cp.wait_recv()

    out = pl.pallas_call(
        body, name="reduce_wait",
        in_specs=[HBM] * (2 * n) + [SEM, SEM] + [ANY] * len(after),
        out_specs=[HBM] * (2 * n),
        out_shape=[pltpu.HBM(a.shape, a.dtype) for a in arrs] + [pltpu.HBM(a.shape, a.dtype) for a in slots],
        input_output_aliases={k: k for k in range(2 * n)},
        compiler_params=pltpu.CompilerParams(has_side_effects=DATAFLOW),
    )(*arrs, *slots, ssem, rsem, *after)
    return list(out[n:])


def _own_part(arrs, chip, core, me):
    n = len(arrs)
    nb = GRAD_ROW_BLOCKS

    def body(chip_ref, core_ref, me_ref, *refs):
        for k in range(n):
            refs[n + k][...] = refs[k][...]

    def blk(a):
        return (1, a.shape[1] // 2 // nb, a.shape[2])

    grid_spec = pltpu.PrefetchScalarGridSpec(
        num_scalar_prefetch=3, grid=(nb,),
        in_specs=[pl.BlockSpec(blk(a), lambda i, ch, co, me: (ch[0], co[0] * nb + i, 0)) for a in arrs],
        out_specs=[pl.BlockSpec(blk(a), lambda i, ch, co, me: (me[0], i, 0)) for a in arrs])
    return pl.pallas_call(
        body, name="own_part", grid_spec=grid_spec,
        out_shape=[jax.ShapeDtypeStruct((N_DEV, a.shape[1] // 2, a.shape[2]), a.dtype) for a in arrs],
        compiler_params=_cparams(1),
    )(chip, core, me, *arrs)


def _sum_exchange(arrs):
    n = len(arrs)
    nb = GRAD_ROW_BLOCKS

    def body(*refs):
        ins, outs, bufs = refs[:n], refs[n:2 * n], refs[2 * n:3 * n]
        lsem, ssem, rsem = refs[3 * n:]
        i = pl.program_id(0)
        x, y, c = _mesh_pos()

        def copies(k, j):
            h = arrs[k].shape[1]
            hb = h // nb
            src = bufs[k].at[j]
            mine = outs[k].at[pl.ds(c * h + j * hb, hb)]
            theirs = outs[k].at[pl.ds((1 - c) * h + j * hb, hb)]
            sibling = dict(send_sem=ssem.at[k, j], recv_sem=rsem.at[k, j], device_id=(x, y, 1 - c),
                           device_id_type=MESH)
            return (pltpu.make_async_copy(src, mine, lsem.at[k, j]),
                    pltpu.make_async_remote_copy(src_ref=src, dst_ref=mine, **sibling),
                    pltpu.make_async_remote_copy(src_ref=src, dst_ref=theirs, **sibling))

        for k in range(n):
            r = ins[k]
            acc = r[0].astype(F32)
            for dev in range(1, N_DEV):
                acc = acc + r[dev].astype(F32)
            bufs[k][i] = acc
            local, send, _ = copies(k, i)
            local.start()
            send.start()

        @pl.when(i == nb - 1)
        def _():
            for k in range(n):
                for j in range(nb):
                    local, send, landing = copies(k, j)
                    local.wait()
                    send.wait_send()
                    landing.wait_recv()

    return pl.pallas_call(
        body, name="sum_exchange", grid=(nb,),
        in_specs=[pl.BlockSpec((N_DEV, a.shape[1] // nb, a.shape[2]), lambda i: (0, i, 0)) for a in arrs],
        out_specs=[ANY] * n,
        out_shape=[jax.ShapeDtypeStruct((2 * a.shape[1], a.shape[2]), F32) for a in arrs],
        scratch_shapes=[pltpu.VMEM((nb, a.shape[1] // nb, a.shape[2]), F32) for a in arrs]
        + [pltpu.SemaphoreType.DMA((n, nb))] * 3,
        compiler_params=_cparams(1),
    )(*arrs)


def _small_own(v, me):
    m = v.shape[0]

    def body(me_ref, v_ref, o_ref):
        o_ref[0] = v_ref[...]

    grid_spec = pltpu.PrefetchScalarGridSpec(
        num_scalar_prefetch=1, grid=(1,),
        in_specs=[pl.BlockSpec((m, LANE), lambda i, me: (0, 0))],
        out_specs=pl.BlockSpec((1, m, LANE), lambda i, me: (me[0], 0, 0)))
    return pl.pallas_call(
        body, name="small_own", grid_spec=grid_spec,
        out_shape=jax.ShapeDtypeStruct((N_DEV, m, LANE), v.dtype),
        compiler_params=_cparams(1),
    )(me, v)


def _small_start(v, slots):
    def body(v_ref, land, ssem, rsem, v_thru, land_thru, token):
        del v_thru, land_thru
        x, y, c = _mesh_pos()
        me = 2 * (2 * x + y) + c
        for p, (dev, _) in enumerate(_reduce_peers(x, y, c)):
            pltpu.make_async_remote_copy(src_ref=v_ref, dst_ref=land.at[me], send_sem=ssem.at[p],
                                         recv_sem=rsem.at[p], device_id=dev, device_id_type=MESH).start()
        token[...] = jnp.zeros_like(token)

    out = pl.pallas_call(
        body, name="small_start",
        in_specs=[HBM, HBM],
        out_specs=[SEM, SEM, HBM, HBM, pl.BlockSpec(memory_space=pltpu.VMEM)],
        out_shape=[pltpu.SemaphoreType.DMA((N_SOURCES,)), pltpu.SemaphoreType.DMA((N_SOURCES,)),
                   pltpu.HBM(v.shape, v.dtype), pltpu.HBM(slots.shape, slots.dtype),
                   jax.ShapeDtypeStruct((SUBLANE, LANE), F32)],
        input_output_aliases={0: 2, 1: 3},
        compiler_params=pltpu.CompilerParams(has_side_effects=DATAFLOW),
    )(_in_hbm(v), _in_hbm(slots))
    return out


def _small_wait(ssem, rsem, v, slots, after):
    def body(*refs):
        v_ref, land, ssem_ref, rsem_ref = refs[:4]
        x, y, c = _mesh_pos()
        for p, (dev, ochip) in enumerate(_reduce_peers(x, y, c)):
            cp = pltpu.make_async_remote_copy(src_ref=v_ref, dst_ref=land.at[2 * ochip + dev[2]],
                                              send_sem=ssem_ref.at[p], recv_sem=rsem_ref.at[p],
                                              device_id=dev, device_id_type=MESH)
            cp.wait_send()
            cp.wait_recv()

    out = pl.pallas_call(
        body, name="small_wait",
        in_specs=[HBM, HBM, SEM, SEM] + [ANY] * len(after),
        out_specs=[HBM, HBM],
        out_shape=[pltpu.HBM(v.shape, v.dtype), pltpu.HBM(slots.shape, slots.dtype)],
        input_output_aliases={0: 0, 1: 1},
        compiler_params=pltpu.CompilerParams(has_side_effects=DATAFLOW),
    )(v, slots, ssem, rsem, *after)
    return out[1]


GRAD_ROW_BLOCKS = 2


N_DEV = 8


def _adamw_math(w, g, m, v):
    m2 = ADAM_B1 * m + (1.0 - ADAM_B1) * g
    v2 = ADAM_B2 * v + (1.0 - ADAM_B2) * (g * g)
    m_hat = m2 / (1.0 - ADAM_B1 ** ADAM_STEP)
    v_hat = v2 / (1.0 - ADAM_B2 ** ADAM_STEP)
    delta = -ADAM_LR * (m_hat / (jnp.sqrt(v_hat) + ADAM_EPS) + ADAM_WD * w)
    return delta, m2, v2


def _adamw(w, m, v, gs, nblk):
    nl, r, n = w.shape
    assert nl == len(gs) and nl in (1, 2)
    br = r // nblk

    def body(w_ref, m_ref, v_ref, *rest):
        g_refs, (go_ref, d_ref, mo_ref, vo_ref) = rest[:nl], rest[nl:]
        g = g_refs[0][...]
        if nl == 2:
            g = jnp.where(pl.program_id(0) == 0, g, g_refs[1][...])
        delta, m2, v2 = _adamw_math(w_ref[0], g, m_ref[0], v_ref[0])
        go_ref[0] = g
        d_ref[0] = delta
        mo_ref[0] = m2
        vo_ref[0] = v2

    spec = pl.BlockSpec((1, br, n), lambda l, i: (l, i, 0))
    g_specs = [pl.BlockSpec((br, n), lambda l, i: (i, 0))] if nl == 1 else [
        pl.BlockSpec((br, n), lambda l, i: (jnp.where(l == 0, i, nblk - 1), 0)),
        pl.BlockSpec((br, n), lambda l, i: (jnp.where(l == 1, i, 0), 0))]
    return pl.pallas_call(
        body, name="adamw", grid=(nl, nblk),
        in_specs=[spec, spec, spec] + g_specs,
        out_specs=[spec] * 4,
        out_shape=[jax.ShapeDtypeStruct((nl, r, n), F32)] * 4,
        compiler_params=_cparams(2),
    )(w, m, v, *gs)


def _small_reduce_adamw(parts, w, m, v, rep_rows, sh_rows):
    mrows = rep_rows + N_SHARD * sh_rows + LOSS_ROWS

    def body(p_ref, w_ref, m_ref, v_ref, go_ref, d_ref, mo_ref, vo_ref, loss_ref):
        x, y, _ = _mesh_pos()
        mine = rep_rows + (2 * x + y) * sh_rows
        g_rep = p_ref[0:rep_rows, :]
        g_sh = p_ref[pl.ds(pl.multiple_of(mine, SUBLANE), sh_rows), :]
        loss = p_ref[mrows - LOSS_ROWS:mrows, :]
        for k in range(1, N_DEV):
            g_rep = g_rep + p_ref[k * mrows:k * mrows + rep_rows, :]
            g_sh = g_sh + p_ref[pl.ds(pl.multiple_of(k * mrows + mine, SUBLANE), sh_rows), :]
            loss = loss + p_ref[(k + 1) * mrows - LOSS_ROWS:(k + 1) * mrows, :]
        g = jnp.concatenate([g_rep, g_sh], axis=0)
        delta, m2, v2 = _adamw_math(w_ref[...], g, m_ref[...], v_ref[...])
        go_ref[...] = g
        d_ref[...] = delta
        mo_ref[...] = m2
        vo_ref[...] = v2
        loss_ref[...] = loss

    return pl.pallas_call(
        body, name="small_reduce_adamw",
        out_shape=[jax.ShapeDtypeStruct((rep_rows + sh_rows, 128), F32)] * 4
        + [jax.ShapeDtypeStruct((LOSS_ROWS, 128), F32)],
        compiler_params=pltpu.CompilerParams(vmem_limit_bytes=VMEM_LIMIT_MB * 1024 * 1024),
    )(parts, w, m, v)


LANE = 128
REP_SPEC = (("ffn1_norm", 16), ("mix_norm", 16), ("ffn2_norm", 16), ("final_norm", 8), ("pool_w", 256),
            ("pool_scale", 8), ("hgrn_lb_logits", 16), ("hgrn_gnorm", 8), ("lru_wa", 512), ("lru_wx", 512))
SH_SPEC = (("meta_tokens", 32), ("conv_w", 32), ("lru_conv_w", 8), ("conv_b", 8), ("conv_ln_g", 8),
           ("conv_ln_b", 8), ("lru_conv_b", 8), ("lru_ba", 8), ("lru_bx", 8), ("lru_lambda", 8))
REP_ROWS = sum(r for _, r in REP_SPEC)
SH_ROWS = sum(r for _, r in SH_SPEC)


def _pack_rows(vals, spec):
    parts = []
    for name, rows in spec:
        val = vals[name].astype(F32)
        if val.shape[-1] < LANE:
            flat = jnp.pad(val.reshape(-1, val.shape[-1]), ((0, 0), (0, LANE - val.shape[-1])))
        else:
            flat = val.reshape(-1, LANE)
        if flat.shape[0] < rows:
            flat = jnp.concatenate([flat, jnp.zeros((rows - flat.shape[0], LANE), F32)], axis=0)
        parts.append(flat)
    return jnp.concatenate(parts, axis=0)


def _unpack_rows(packed, spec, shapes):
    out = {}
    off = 0
    for name, rows in spec:
        shp = shapes[name]
        width = min(shp[-1], LANE)
        n = int(np.prod(shp)) // width
        out[name] = packed[off:off + n, :width].reshape(shp)
        off += rows
    return out


def _block_diag(blocks):
    n, b, _ = blocks.shape
    return sum(jnp.pad(blocks[g], ((g * b, (n - 1 - g) * b), (g * b, (n - 1 - g) * b))) for g in range(n))


def _diag_blocks(mat, n):
    b = mat.shape[0] // n
    return jnp.stack([mat[g * b:(g + 1) * b, g * b:(g + 1) * b] for g in range(n)])


BIG = ("ffn1_wg", "ffn1_wu", "ffn2_wg", "ffn2_wu", "ffn1_wd", "ffn2_wd", "w_in_even", "w_out_even",
       "w_in_odd", "w_out_odd")
WEIGHT_NAMES = ('meta_tokens', 'ffn1_norm', 'ffn1_wg', 'ffn1_wu', 'ffn1_wd', 'mix_norm', 'ffn2_norm', 'ffn2_wg',
                'ffn2_wu', 'ffn2_wd', 'w_in_even', 'pool_w', 'pool_scale', 'hgrn_lb_logits', 'hgrn_gnorm',
                'w_out_even', 'w_in_odd', 'conv_w', 'conv_b', 'conv_ln_g', 'conv_ln_b', 'lru_conv_w',
                'lru_conv_b', 'lru_wa', 'lru_ba', 'lru_wx', 'lru_bx', 'lru_lambda', 'w_out_odd', 'final_norm')


def _block_diag2(heads):
    nb = heads.shape[0] // 2
    return jnp.stack([_block_diag(heads[2 * j:2 * j + 2]) for j in range(nb)])


def _diag_blocks2(mats):
    return jnp.concatenate([_diag_blocks(mats[j], 2) for j in range(mats.shape[0])], axis=0)


GATHER_GROUPS = (
    (("small", 0),),
    (("ffn1_wg", 0), ("ffn1_wu", 0), ("ffn1_wd", 0)),
    (("w_in_even", 0), ("w_out_even", 0)),
    (("ffn2_wg", 0), ("ffn2_wu", 0), ("ffn2_wd", 0)),
    (("ffn1_wg", 1), ("ffn1_wu", 1), ("ffn1_wd", 1)),
    (("w_in_odd", 0), ("w_out_odd", 0)),
    (("ffn2_wg", 1), ("ffn2_wu", 1), ("ffn2_wd", 1)),
)
ADAM_ROW_BLOCKS = {"ffn1_wg": 2, "ffn1_wu": 2, "ffn2_wg": 2, "ffn2_wu": 2, "ffn1_wd": 2, "ffn2_wd": 2,
                   "w_in_even": 4, "w_out_even": 2, "w_in_odd": 4, "w_out_odd": 2}
TRANSPOSED = ("ffn1_wg", "ffn1_wu", "ffn2_wg", "ffn2_wu", "w_in_even")
SCATTER_DEPTH = 2
LOSS_ROWS = 8
GATHER_SPLIT = (1, 4)


def _unpack_small(sm, shapes):
    per_shard = [_unpack_rows(sm[s], SH_SPEC, shapes) for s in range(N_SHARD)]
    full = {}
    for n, _ in SH_SPEC:
        full[n] = jnp.concatenate([per_shard[s][n].reshape(-1, shapes[n][-1]) for s in range(N_SHARD)], axis=-1)
    full["conv_w"] = jnp.concatenate([full["conv_w"], jnp.zeros((CONV_HALO - CONV_WIDTH, D_CONV), F32)], axis=0)
    return full


def _local_step(x, tgt, w, shapes, fetch, emit, emit_small):
    s_len, d = x.shape
    t_real = s_len + N_META
    tp = -(-t_real // ROW_ALIGN) * ROW_ALIGN
    tm = _tile(tp, 832, ROW_ALIGN)
    tm_small = _tile(tp, 832, 16)
    tr = _tile(tp, 416, SUBLANE)
    tm_wgrad = _wgrad_tiles(tp, 1024)

    def gain(name, layer):
        return w[name][layer:layer + 1]

    pool_wbd = _block_diag(w["pool_w"][0]).astype(BF16)
    pool_scale = w["pool_scale"]
    wa_bd = _block_diag2(w["lru_wa"][0]).astype(BF16)
    wx_bd = _block_diag2(w["lru_wx"][0]).astype(BF16)
    mst, msk, n_lev = _hgrn_consts(HG_CHUNK)

    (sm,) = fetch(0, None)
    sf = _unpack_small(sm, shapes)
    h0 = jnp.concatenate([sf["meta_tokens"], x, jnp.zeros((tp - t_real, d), F32)], axis=0)
    tgt_pad = jnp.concatenate([jnp.zeros((N_META, d), F32), tgt, jnp.zeros((tp - t_real, d), F32)], axis=0)
    f1l0 = fetch(1, h0)
    h1, *s1 = _ffn_fwd(h0, gain("ffn1_norm", 0), *f1l0, tm)
    w_in_even4, w_out_even4 = fetch(2, h1)
    w_out_even = w_out_even4.reshape(d, d)
    even_piece = [(w_in_even4.reshape(D_IN_EVEN, d), (D_IN_EVEN, d), (0, 0))]
    p0, nm0 = _proj_fwd(h1, gain("mix_norm", 0), even_piece, tm_small, wt=True)
    ya = _pool_fwd(p0, pool_wbd, pool_scale, tr)
    yb, states = _hgrn_fwd(p0, w["hgrn_lb_logits"], w["hgrn_gnorm"], mst, msk, n_lev, tm)
    h2 = _out_fwd(h1, ya, yb, w_out_even, tm)
    f2l0 = fetch(3, h2)
    h3, *s2 = _ffn_fwd(h2, gain("ffn2_norm", 0), *f2l0, tm)
    f1l1 = fetch(4, h3)
    h4, *s3 = _ffn_fwd(h3, gain("ffn1_norm", 1), *f1l1, tm)
    w_in_odd4, w_out_odd4 = fetch(5, h4)
    w_out_odd = w_out_odd4.reshape(d, d)
    odd_pieces = [(w_in_odd4, (1, d, D_IN_ODD // N_SHARD), (k, 0, 0)) for k in range(N_SHARD)]
    p1, nm1 = _proj_fwd(h4, gain("mix_norm", 1), odd_pieces, tm_small)
    yc, conv_out = _convmod_fwd(p1, sf["conv_w"], sf["conv_b"], sf["conv_ln_g"], sf["conv_ln_b"], tr)
    lru_args = (sf["lru_conv_w"], sf["lru_conv_b"], wa_bd, sf["lru_ba"], wx_bd, sf["lru_bx"], sf["lru_lambda"])
    yd, hs = _lru_fwd(p1, *lru_args)
    h5 = _out_fwd(h4, yc, yd, w_out_odd, tm)
    f2l1 = fetch(6, h5)
    h6, *s4 = _ffn_fwd(h5, gain("ffn2_norm", 1), *f2l1, tm)
    loss, dh6, dg_final = _loss_bwd(h6, w["final_norm"].reshape(1, d), tgt_pad, t_real, tm)

    def ffn_bwd(dho, h, saved, norm, wts, after=()):
        ga, gb, sa, n = saved
        dh, da, db, dg, dy = _ffn_bwd_act(dho, h, norm, ga, gb, *wts, tm, after)
        return dh, dg, _ffn_bwd_w([(da, n, None), (db, n, None), (sa, dy, None)], tm_wgrad)

    dh5, dg_f2_l1, g = ffn_bwd(dh6, h5, s4, gain("ffn2_norm", 1), f2l1)
    sent = emit((("ffn2_wg", 1), ("ffn2_wu", 1), ("ffn2_wd", 1)), g)
    dyc, dyd, dw_out_odd = _out_bwd(dh5, yc, yd, w_out_odd, tm, tuple(sent))
    dca, dcb, dconv_w, dconv_vec = _convmod_bwd(p1, conv_out, dyc, sf["conv_w"], sf["conv_ln_g"],
                                                sf["conv_ln_b"], tr)
    dlx, dlg, dwa_bd, dwx_bd, dlru_vec = _lru_bwd(p1, hs, dyd, *lru_args)
    dp1 = [dca, dcb, dlx, dlg]
    dh4, dg_mix_l1 = _proj_bwd_act(dh5, h4, gain("mix_norm", 1), dp1, odd_pieces, tm_small)
    dw_in_odd = jnp.stack(_proj_bwd_w(nm1, dp1, tm_wgrad))
    dh3, dg_f1_l1, g = ffn_bwd(dh4, h3, s3, gain("ffn1_norm", 1), f1l1)
    sent = emit((("w_out_odd", 0), ("w_in_odd", 0), ("ffn1_wg", 1), ("ffn1_wu", 1), ("ffn1_wd", 1)),
                [dw_out_odd.reshape(N_SHARD, d // N_SHARD, d), dw_in_odd] + list(g))
    dh2, dg_f2_l0, g = ffn_bwd(dh3, h2, s2, gain("ffn2_norm", 0), f2l0, tuple(sent))
    sent = emit((("ffn2_wg", 0), ("ffn2_wu", 0), ("ffn2_wd", 0)), g)
    dya, dyb, dw_out_even = _out_bwd(dh2, ya, yb, w_out_even, tm, tuple(sent))
    dpool, dpool_wbd, dpool_scale = _pool_bwd(p0, dya, pool_wbd, pool_scale, tr)
    dq, dz, dv, dgate, dlb_logits, dgn_heads = _hgrn_bwd(p0, dyb, states, w["hgrn_lb_logits"], w["hgrn_gnorm"],
                                                         mst, msk, n_lev, tm)
    dp0 = [dpool, dq, dz, dv, dgate]
    dh1, dg_mix_l0 = _proj_bwd_act(dh2, h1, gain("mix_norm", 0), dp0, even_piece, tm_small, wt=True)
    dw_in_even_t = jnp.concatenate(_proj_bwd_w(nm0, dp0, tm_wgrad, wt=True), axis=0)
    ga, gb, sa, n1 = s1
    (dwd_f1l0,) = _ffn_bwd_w([(sa, dh1, 0.5)], tm_wgrad)
    sent = emit((("w_out_even", 0), ("w_in_even", 0), ("ffn1_wd", 0)),
                [dw_out_even.reshape(N_SHARD, d // N_SHARD, d),
                 dw_in_even_t.reshape(N_SHARD, D_IN_EVEN // N_SHARD, d), dwd_f1l0])
    dh0, da, db, dg_f1_l0, _ = _ffn_bwd_act(dh1, h0, gain("ffn1_norm", 0), ga, gb, *f1l0, tm, tuple(sent))

    grad_x = dh0[N_META:t_real]
    rep = {
        "ffn1_norm": jnp.concatenate([dg_f1_l0, dg_f1_l1], axis=0),
        "mix_norm": jnp.concatenate([dg_mix_l0, dg_mix_l1], axis=0),
        "ffn2_norm": jnp.concatenate([dg_f2_l0, dg_f2_l1], axis=0),
        "final_norm": dg_final,
        "pool_w": _diag_blocks(dpool_wbd, len(POOL_WINDOWS)),
        "pool_scale": dpool_scale,
        "hgrn_lb_logits": dlb_logits,
        "hgrn_gnorm": jnp.sum(dgn_heads, axis=0),
        "lru_wa": _diag_blocks2(dwa_bd),
        "lru_wx": _diag_blocks2(dwx_bd),
    }
    dmeta = jnp.transpose(dh0[:N_META].reshape(N_META, N_SHARD, 2, LANE), (1, 0, 2, 3)).reshape(N_SHARD, 32, LANE)
    packs = [_pack_rows(rep, REP_SPEC)]
    for s in range(N_SHARD):
        sh = {
            "meta_tokens": dmeta[s], "conv_w": dconv_w[s], "lru_conv_w": dlru_vec[s, 0:4],
            "conv_b": dconv_vec[s, 0:1], "conv_ln_g": dconv_vec[s, 1:2], "conv_ln_b": dconv_vec[s, 2:3],
            "lru_conv_b": dlru_vec[s, 4:5], "lru_ba": dlru_vec[s, 5:6], "lru_bx": dlru_vec[s, 6:7],
            "lru_lambda": dlru_vec[s, 7:8],
        }
        packs.append(_pack_rows(sh, SH_SPEC))
    packs.append(jnp.pad(loss, ((0, LOSS_ROWS - 1), (0, LANE - 1))))
    sent = emit_small(jnp.concatenate(packs, axis=0))
    emit((("ffn1_wg", 0), ("ffn1_wu", 0)), _ffn_bwd_w([(da, n1, None), (db, n1, None)], tm_wgrad, tuple(sent)))
    return grad_x


def kernel(x, meta_tokens, ffn1_norm, ffn1_wg, ffn1_wu, ffn1_wd, mix_norm, ffn2_norm, ffn2_wg, ffn2_wu, ffn2_wd, w_in_even, pool_w, pool_scale, hgrn_lb_logits, hgrn_gnorm, w_out_even, w_in_odd, conv_w, conv_b, conv_ln_g, conv_ln_b, lru_conv_w, lru_conv_b, lru_wa, lru_ba, lru_wx, lru_bx, lru_lambda, w_out_odd, final_norm, loss_target, m_meta_tokens, m_ffn1_norm, m_ffn1_wg, m_ffn1_wu, m_ffn1_wd, m_mix_norm, m_ffn2_norm, m_ffn2_wg, m_ffn2_wu, m_ffn2_wd, m_w_in_even, m_pool_w, m_pool_scale, m_hgrn_lb_logits, m_hgrn_gnorm, m_w_out_even, m_w_in_odd, m_conv_w, m_conv_b, m_conv_ln_g, m_conv_ln_b, m_lru_conv_w, m_lru_conv_b, m_lru_wa, m_lru_ba, m_lru_wx, m_lru_bx, m_lru_lambda, m_w_out_odd, m_final_norm, v_meta_tokens, v_ffn1_norm, v_ffn1_wg, v_ffn1_wu, v_ffn1_wd, v_mix_norm, v_ffn2_norm, v_ffn2_wg, v_ffn2_wu, v_ffn2_wd, v_w_in_even, v_pool_w, v_pool_scale, v_hgrn_lb_logits, v_hgrn_gnorm, v_w_out_even, v_w_in_odd, v_conv_w, v_conv_b, v_conv_ln_g, v_conv_ln_b, v_lru_conv_w, v_lru_conv_b, v_lru_wa, v_lru_ba, v_lru_wx, v_lru_bx, v_lru_lambda, v_w_out_odd, v_final_norm):
    args = locals()
    w = {n: args[n] for n in WEIGHT_NAMES}
    m = {n: args["m_" + n] for n in WEIGHT_NAMES}
    v = {n: args["v_" + n] for n in WEIGHT_NAMES}
    shapes = {n: w[n].shape for n in WEIGHT_NAMES}
    core = lax.axis_index("c").astype(jnp.int32).reshape(1)
    chip = (2 * lax.axis_index("x") + lax.axis_index("y")).astype(jnp.int32).reshape(1)
    me = 2 * chip + core

    def view(a, n):
        return jnp.swapaxes(a, 1, 2) if n in TRANSPOSED else a

    wv, mv, vv = [{n: view(src[n], n) for n in BIG} for src in (w, m, v)]

    def shard(key):
        n, l = key
        return _pack_rows(w, SH_SPEC) if n == "small" else wv[n][l].astype(BF16)

    started = {}
    for groups in (GATHER_GROUPS[:2], GATHER_GROUPS[2:]):
        gkeys = [key for grp in groups for key in grp]
        ssem, rsem, srcs, lands, token = _gather_start([shard(key) for key in gkeys],
                                                       [any(key in GATHER_GROUPS[g] for g in GATHER_SPLIT)
                                                        for key in gkeys])
        for k, key in enumerate(gkeys):
            started[key] = (ssem, rsem, srcs[k], lands[k], k, token)

    def pack_small(src):
        return jnp.concatenate([_pack_rows(src, REP_SPEC), _pack_rows(src, SH_SPEC)], axis=0)

    small_packs = [pack_small(src) for src in (w, m, v)]

    def fetch(group, after):
        st = [started[key] for key in GATHER_GROUPS[group]]
        deps = (st[0][5],) if after is None else (after,)
        if group == 1:
            deps += (started[GATHER_GROUPS[2][0]][5],) + tuple(small_packs)
        split = group in GATHER_SPLIT
        got = _gather_wait(st[0][0], st[0][1], [s[2] for s in st], [s[3] for s in st], [s[4] for s in st], deps,
                           split)
        return _pair_forward(got) if split else got

    in_flight, reduced = [], {}

    def collect(entry, after):
        gkeys, gs_sem, gr_sem, grads_thru, slots_thru, _ = entry
        slots = _reduce_wait(gs_sem, gr_sem, grads_thru, slots_thru, after)
        full = _sum_exchange(slots)
        reduced.update(zip(gkeys, full))
        return full[0]

    def emit(gkeys, grads):
        grads = list(grads)
        in_flight.append((gkeys,) + tuple(_reduce_start(grads, _own_part(grads, chip, core, me))))
        token = in_flight[-1][-1]
        if len(in_flight) > SCATTER_DEPTH:
            return token, collect(in_flight[-1 - SCATTER_DEPTH], (token,))
        return (token,)

    small_flight = []

    def emit_small(part):
        small_flight.append(_small_start(part, _small_own(part, me)))
        return (small_flight[0][4],)

    grad_x = _local_step(x[0], loss_target[0], w, shapes, fetch, emit, emit_small)

    out_g, out_d, out_m, out_v = {}, {}, {}, {}
    deps = (in_flight[-1][-1],)

    def adamw_ready():
        done = ()
        for n in BIG:
            layers = range(shapes[n][0])
            if n not in out_g and all((n, l) in reduced for l in layers):
                res = _adamw(wv[n], mv[n], vv[n], [reduced[(n, l)] for l in layers], ADAM_ROW_BLOCKS[n])
                out_g[n], out_d[n], out_m[n], out_v[n] = [view(r, n) for r in res]
                done += (res[1],)
        return done

    deps += adamw_ready()
    for entry in in_flight[-SCATTER_DEPTH:-1]:
        collect(entry, deps)
        deps += adamw_ready()
    s_ssem, s_rsem, s_part, s_slots, _ = small_flight[0]
    small_all = _small_wait(s_ssem, s_rsem, s_part, s_slots, deps)
    small_res = _small_reduce_adamw(small_all.reshape(-1, LANE), *small_packs, REP_ROWS, SH_ROWS)
    collect(in_flight[-1], deps + (small_res[0],))
    adamw_ready()

    loss = small_res[4][0, 0]
    for dst, packed in zip((out_g, out_d, out_m, out_v), small_res[:4]):
        dst.update(_unpack_rows(packed[:REP_ROWS], REP_SPEC, shapes))
        dst.update(_unpack_rows(packed[REP_ROWS:], SH_SPEC, shapes))

    return (loss, grad_x[None], *[out_g[n] for n in WEIGHT_NAMES], *[out_d[n] for n in WEIGHT_NAMES],
            *[out_m[n] for n in WEIGHT_NAMES], *[out_v[n] for n in WEIGHT_NAMES])
```
